```python
import math
import jax, jax.numpy as jnp
from jax import lax
import numpy as np

D_MODEL = 1024
BATCH = 8
SEQ = 2048
DEPTH = 1

CONV_WIDTH = D_MODEL // 2
CONV_K = 3
HEAD_DIM = 64
N_HEADS = (D_MODEL - CONV_WIDTH) // HEAD_DIM
N_KV_HEADS = 2
GQA_GROUP = N_HEADS // N_KV_HEADS
ATTN_WIDTH = N_HEADS * HEAD_DIM
KV_WIDTH = N_KV_HEADS * HEAD_DIM
WINDOW = 128
BLK = 128
NUM_BUCKETS = 32
MAX_DISTANCE = 128
MAX_EXACT = NUM_BUCKETS // 2
D_FF = 2816
FFN_K = 3
EPS = 1e-6
NEG_INF = -1e30
IN_WIDTH = 3 * CONV_WIDTH + ATTN_WIDTH + 2 * KV_WIDTH

kernel_name = "hybrid_shortconv_swa_sink_convffn"


def rms_norm(x, g):
    xf = x.astype(jnp.float32)
    y = xf * lax.rsqrt(jnp.mean(xf * xf, axis=-1, keepdims=True) + EPS)
    return (y * g.astype(jnp.float32)).astype(x.dtype)


def causal_dwconv(x, w):
    K = w.shape[0]
    S = x.shape[1]
    xp = jnp.pad(x, ((0, 0), (K - 1, 0), (0, 0)))
    y = xp[:, 0:S] * w[0]
    for k in range(1, K):
        y = y + xp[:, k:k + S] * w[k]
    return y


def band_offsets():
    q = jnp.arange(BLK, dtype=jnp.int32)[:, None]
    j = jnp.arange(2 * BLK, dtype=jnp.int32)[None, :]
    return q + BLK - j


def t5_band_bias(rel_table, d):
    n = jnp.maximum(d, 0)
    nf = jnp.maximum(n, 1).astype(jnp.float32)
    large = MAX_EXACT + (jnp.log(nf / MAX_EXACT) / math.log(MAX_DISTANCE / MAX_EXACT)
                         * (NUM_BUCKETS - MAX_EXACT)).astype(jnp.int32)
    large = jnp.minimum(large, NUM_BUCKETS - 1)
    bucket = jnp.where(n < MAX_EXACT, n, large)
    bias = rel_table[bucket].astype(jnp.float32)
    return bias.transpose(2, 0, 1).reshape(N_KV_HEADS, GQA_GROUP, BLK, 2 * BLK)


def band_blocks(t):
    Bn, S = t.shape[0], t.shape[1]
    nb = S // BLK
    tb = t.reshape(Bn, nb, BLK, N_KV_HEADS, HEAD_DIM)
    prev = jnp.pad(tb, ((0, 0), (1, 0), (0, 0), (0, 0), (0, 0)))[:, :-1]
    return jnp.concatenate([prev, tb], axis=2)


def sliding_window_attention(q, k, v, sinks, bias, d):
    Bn, S = q.shape[0], q.shape[1]
    nb = S // BLK
    qb = q.reshape(Bn, nb, BLK, N_KV_HEADS, GQA_GROUP, HEAD_DIM)
    kband = band_blocks(k)
    vband = band_blocks(v)
    scale = HEAD_DIM ** -0.5
    logits = jnp.einsum('bnqhgd,bnkhd->bnhgqk', qb, kband).astype(jnp.float32) * scale + bias
    within = (d >= 0) & (d < WINDOW)
    key_pos = (jnp.arange(nb, dtype=jnp.int32)[:, None, None] - 1) * BLK \
        + jnp.arange(2 * BLK, dtype=jnp.int32)[None, None, :]
    valid = within[None] & (key_pos >= 0)
    logits = jnp.where(valid[None, :, None, None], logits, NEG_INF)
    sink = sinks.astype(jnp.float32).reshape(1, 1, N_KV_HEADS, GQA_GROUP, 1, 1)
    m = jnp.maximum(jnp.max(logits, axis=-1, keepdims=True), sink)
    p = jnp.exp(logits - m)
    denom = jnp.sum(p, axis=-1, keepdims=True) + jnp.exp(sink - m)
    probs = (p / denom).astype(v.dtype)
    out = jnp.einsum('bnhgqk,bnkhd->bnqhgd', probs, vband)
    return out.reshape(Bn, S, ATTN_WIDTH)


def _fwd_setup_inputs(seed: int = 0) -> dict:
    key = jax.random.key(seed)
    ks = jax.random.split(key, 17)
    f32 = jnp.float32

    def nrm(k, shape, scale):
        return jax.random.normal(k, shape, f32) * scale

    def gain(k, shape):
        return 1.0 + 0.02 * jax.random.normal(k, shape, f32)

    return {
        "x": jax.random.normal(ks[0], (BATCH, SEQ, D_MODEL), f32),
        "norm_mix_g": gain(ks[1], (DEPTH, D_MODEL)),
        "w_in": nrm(ks[2], (DEPTH, D_MODEL, IN_WIDTH), D_MODEL ** -0.5),
        "conv_w": nrm(ks[3], (DEPTH, CONV_K, CONV_WIDTH), CONV_K ** -0.5),
        "q_norm_g": gain(ks[4], (DEPTH, HEAD_DIM)),
        "k_norm_g": gain(ks[5], (DEPTH, HEAD_DIM)),
        "rel_bias_table": nrm(ks[6], (NUM_BUCKETS, N_HEADS), 0.5),
        "sinks": nrm(ks[7], (DEPTH, N_HEADS), 1.0),
        "out_norm_conv_g": gain(ks[8], (DEPTH, CONV_WIDTH)),
        "out_norm_attn_g": gain(ks[9], (DEPTH, ATTN_WIDTH)),
        "w_out": nrm(ks[10], (DEPTH, CONV_WIDTH + ATTN_WIDTH, D_MODEL), (CONV_WIDTH + ATTN_WIDTH) ** -0.5),
        "norm_ffn_g": gain(ks[11], (DEPTH, D_MODEL)),
        "w_up": nrm(ks[12], (DEPTH, D_MODEL, 2 * D_FF), D_MODEL ** -0.5),
        "ffn_conv_w": nrm(ks[13], (DEPTH, FFN_K, 2 * D_FF), FFN_K ** -0.5),
        "ffn_conv_b": nrm(ks[14], (DEPTH, 2 * D_FF), 0.02),
        "w_down": nrm(ks[15], (DEPTH, D_FF, D_MODEL), D_FF ** -0.5),
    }


def _fwd_reference(x, norm_mix_g, w_in, conv_w, q_norm_g, k_norm_g, rel_bias_table, sinks,
              out_norm_conv_g, out_norm_attn_g, w_out, norm_ffn_g, w_up, ffn_conv_w,
              ffn_conv_b, w_down):
    Bn, S, _ = x.shape
    d = band_offsets()
    bias = t5_band_bias(rel_bias_table, d)
    h = x
    for l in range(DEPTH):
        u = rms_norm(h, norm_mix_g[l])
        proj = jnp.einsum('bsd,de->bse', u, w_in[l])
        c0 = 0
        gate_b = proj[..., c0:c0 + CONV_WIDTH]; c0 += CONV_WIDTH
        gate_c = proj[..., c0:c0 + CONV_WIDTH]; c0 += CONV_WIDTH
        hc = proj[..., c0:c0 + CONV_WIDTH]; c0 += CONV_WIDTH
        q = proj[..., c0:c0 + ATTN_WIDTH]; c0 += ATTN_WIDTH
        k = proj[..., c0:c0 + KV_WIDTH]; c0 += KV_WIDTH
        v = proj[..., c0:c0 + KV_WIDTH]

        y_conv = gate_b * causal_dwconv(gate_c * hc, conv_w[l])

        q = rms_norm(q.reshape(Bn, S, N_HEADS, HEAD_DIM), q_norm_g[l])
        k = rms_norm(k.reshape(Bn, S, N_KV_HEADS, HEAD_DIM), k_norm_g[l])
        v = v.reshape(Bn, S, N_KV_HEADS, HEAD_DIM)
        y_attn = sliding_window_attention(q, k, v, sinks[l], bias, d)

        y = jnp.concatenate([rms_norm(y_conv, out_norm_conv_g[l]),
                             rms_norm(y_attn, out_norm_attn_g[l])], axis=-1)
        h = h + jnp.einsum('bse,ed->bsd', y, w_out[l])

        u = rms_norm(h, norm_ffn_g[l])
        up = jnp.einsum('bsd,df->bsf', u, w_up[l])
        up = causal_dwconv(up, ffn_conv_w[l]) + ffn_conv_b[l]
        g, val = up[..., :D_FF], up[..., D_FF:]
        h = h + jnp.einsum('bsf,fd->bsd', jax.nn.silu(g) * val, w_down[l])
    return h


import jax as _jax
import jax.numpy as _jnp

TWIN_FORMAT = 'train_step'
FWD_PARAMS = ['x', 'norm_mix_g', 'w_in', 'conv_w', 'q_norm_g', 'k_norm_g', 'rel_bias_table', 'sinks', 'out_norm_conv_g', 'out_norm_attn_g', 'w_out', 'norm_ffn_g', 'w_up', 'ffn_conv_w', 'ffn_conv_b', 'w_down']
TWIN_WEIGHTS = ['norm_mix_g', 'w_in', 'conv_w', 'q_norm_g', 'k_norm_g', 'rel_bias_table', 'sinks', 'out_norm_conv_g', 'out_norm_attn_g', 'w_out', 'norm_ffn_g', 'w_up', 'ffn_conv_w', 'ffn_conv_b', 'w_down']
TWIN_DIFF_INPUT = 'x'
TWIN_INPUTS = ['x', 'norm_mix_g', 'w_in', 'conv_w', 'q_norm_g', 'k_norm_g', 'rel_bias_table', 'sinks', 'out_norm_conv_g', 'out_norm_attn_g', 'w_out', 'norm_ffn_g', 'w_up', 'ffn_conv_w', 'ffn_conv_b', 'w_down', 'loss_target', 'm_norm_mix_g', 'm_w_in', 'm_conv_w', 'm_q_norm_g', 'm_k_norm_g', 'm_rel_bias_table', 'm_sinks', 'm_out_norm_conv_g', 'm_out_norm_attn_g', 'm_w_out', 'm_norm_ffn_g', 'm_w_up', 'm_ffn_conv_w', 'm_ffn_conv_b', 'm_w_down', 'v_norm_mix_g', 'v_w_in', 'v_conv_w', 'v_q_norm_g', 'v_k_norm_g', 'v_rel_bias_table', 'v_sinks', 'v_out_norm_conv_g', 'v_out_norm_attn_g', 'v_w_out', 'v_norm_ffn_g', 'v_w_up', 'v_ffn_conv_w', 'v_ffn_conv_b', 'v_w_down']
TWIN_OUTPUTS = ['loss', 'grad_x', 'grad_norm_mix_g', 'grad_w_in', 'grad_conv_w', 'grad_q_norm_g', 'grad_k_norm_g', 'grad_rel_bias_table', 'grad_sinks', 'grad_out_norm_conv_g', 'grad_out_norm_attn_g', 'grad_w_out', 'grad_norm_ffn_g', 'grad_w_up', 'grad_ffn_conv_w', 'grad_ffn_conv_b', 'grad_w_down', 'delta_norm_mix_g', 'delta_w_in', 'delta_conv_w', 'delta_q_norm_g', 'delta_k_norm_g', 'delta_rel_bias_table', 'delta_sinks', 'delta_out_norm_conv_g', 'delta_out_norm_attn_g', 'delta_w_out', 'delta_norm_ffn_g', 'delta_w_up', 'delta_ffn_conv_w', 'delta_ffn_conv_b', 'delta_w_down', 'new_m_norm_mix_g', 'new_m_w_in', 'new_m_conv_w', 'new_m_q_norm_g', 'new_m_k_norm_g', 'new_m_rel_bias_table', 'new_m_sinks', 'new_m_out_norm_conv_g', 'new_m_out_norm_attn_g', 'new_m_w_out', 'new_m_norm_ffn_g', 'new_m_w_up', 'new_m_ffn_conv_w', 'new_m_ffn_conv_b', 'new_m_w_down', 'new_v_norm_mix_g', 'new_v_w_in', 'new_v_conv_w', 'new_v_q_norm_g', 'new_v_k_norm_g', 'new_v_rel_bias_table', 'new_v_sinks', 'new_v_out_norm_conv_g', 'new_v_out_norm_attn_g', 'new_v_w_out', 'new_v_norm_ffn_g', 'new_v_w_up', 'new_v_ffn_conv_w', 'new_v_ffn_conv_b', 'new_v_w_down']
TWIN_LEAF_KINDS = {'loss': 'loss', 'grad_x': 'grad_x', 'grad_norm_mix_g': 'grad_w', 'grad_w_in': 'grad_w', 'grad_conv_w': 'grad_w', 'grad_q_norm_g': 'grad_w', 'grad_k_norm_g': 'grad_w', 'grad_rel_bias_table': 'grad_w', 'grad_sinks': 'grad_w', 'grad_out_norm_conv_g': 'grad_w', 'grad_out_norm_attn_g': 'grad_w', 'grad_w_out': 'grad_w', 'grad_norm_ffn_g': 'grad_w', 'grad_w_up': 'grad_w', 'grad_ffn_conv_w': 'grad_w', 'grad_ffn_conv_b': 'grad_w', 'grad_w_down': 'grad_w', 'delta_norm_mix_g': 'delta_w', 'delta_w_in': 'delta_w', 'delta_conv_w': 'delta_w', 'delta_q_norm_g': 'delta_w', 'delta_k_norm_g': 'delta_w', 'delta_rel_bias_table': 'delta_w', 'delta_sinks': 'delta_w', 'delta_out_norm_conv_g': 'delta_w', 'delta_out_norm_attn_g': 'delta_w', 'delta_w_out': 'delta_w', 'delta_norm_ffn_g': 'delta_w', 'delta_w_up': 'delta_w', 'delta_ffn_conv_w': 'delta_w', 'delta_ffn_conv_b': 'delta_w', 'delta_w_down': 'delta_w', 'new_m_norm_mix_g': 'new_m', 'new_m_w_in': 'new_m', 'new_m_conv_w': 'new_m', 'new_m_q_norm_g': 'new_m', 'new_m_k_norm_g': 'new_m', 'new_m_rel_bias_table': 'new_m', 'new_m_sinks': 'new_m', 'new_m_out_norm_conv_g': 'new_m', 'new_m_out_norm_attn_g': 'new_m', 'new_m_w_out': 'new_m', 'new_m_norm_ffn_g': 'new_m', 'new_m_w_up': 'new_m', 'new_m_ffn_conv_w': 'new_m', 'new_m_ffn_conv_b': 'new_m', 'new_m_w_down': 'new_m', 'new_v_norm_mix_g': 'new_v', 'new_v_w_in': 'new_v', 'new_v_conv_w': 'new_v', 'new_v_q_norm_g': 'new_v', 'new_v_k_norm_g': 'new_v', 'new_v_rel_bias_table': 'new_v', 'new_v_sinks': 'new_v', 'new_v_out_norm_conv_g': 'new_v', 'new_v_out_norm_attn_g': 'new_v', 'new_v_w_out': 'new_v', 'new_v_norm_ffn_g': 'new_v', 'new_v_w_up': 'new_v', 'new_v_ffn_conv_w': 'new_v', 'new_v_ffn_conv_b': 'new_v', 'new_v_w_down': 'new_v'}


def _forward(args):
    return _fwd_reference(*[args[k] for k in FWD_PARAMS])


def _output_shape():
    out = _jax.eval_shape(lambda: _forward(_fwd_setup_inputs(0)))
    return out.shape, out.dtype

N_MICROBATCH = 1
ADAM_LR = 0.001
ADAM_B1 = 0.9
ADAM_B2 = 0.999
ADAM_EPS = 1e-08
ADAM_WD = 0.01
ADAM_STEP = 10
PER_EXAMPLE_BATCH_AXIS = {'x': 0, 'loss_target': 0}
SHARED_INPUTS = []
_WEIGHT_DTYPES = {'norm_mix_g': _jnp.float32, 'w_in': _jnp.float32, 'conv_w': _jnp.float32, 'q_norm_g': _jnp.float32, 'k_norm_g': _jnp.float32, 'rel_bias_table': _jnp.float32, 'sinks': _jnp.float32, 'out_norm_conv_g': _jnp.float32, 'out_norm_attn_g': _jnp.float32, 'w_out': _jnp.float32, 'norm_ffn_g': _jnp.float32, 'w_up': _jnp.float32, 'ffn_conv_w': _jnp.float32, 'ffn_conv_b': _jnp.float32, 'w_down': _jnp.float32}
MOMENT_SCALE = {'norm_mix_g': 6.228202e-01, 'w_in': 4.165603e-01, 'conv_w': 5.126106e-01, 'q_norm_g': 8.002899e-01, 'k_norm_g': 8.157190e-01, 'rel_bias_table': 5.013274e-01, 'sinks': 8.685081e-02, 'out_norm_conv_g': 2.040471e+01, 'out_norm_attn_g': 1.588724e+01, 'w_out': 6.255424e-01, 'norm_ffn_g': 1.351268e+01, 'w_up': 1.923542e-01, 'ffn_conv_w': 1.859803e+00, 'ffn_conv_b': 1.610288e+00, 'w_down': 2.304258e-01}


def _to_microbatches(a, axis):
    t = _jnp.moveaxis(a, axis, 0)
    t = t.reshape((N_MICROBATCH, t.shape[0] // N_MICROBATCH) + t.shape[1:])
    return _jnp.moveaxis(t, 1, axis + 1)


def setup_inputs(seed: int = 0) -> dict:
    inp = _fwd_setup_inputs(seed)
    key = _jax.random.fold_in(_jax.random.key(seed), 7919)
    shape, _ = _output_shape()
    out = dict(inp)
    out["loss_target"] = _jax.random.normal(_jax.random.fold_in(key, 0), shape, _jnp.float32)
    for i, name in enumerate(TWIN_WEIGHTS):
        w = inp[name].astype(_jnp.float32)
        if MOMENT_SCALE is None:
            s = _jnp.sqrt(_jnp.mean(_jnp.square(w)) + 1e-30)
        else:
            s = MOMENT_SCALE[name]
        km, kv = _jax.random.split(_jax.random.fold_in(key, i + 1))
        out[name] = w
        out["m_" + name] = s * _jax.random.normal(km, w.shape, _jnp.float32)
        out["v_" + name] = (s * s) * _jax.random.uniform(kv, w.shape, _jnp.float32, 0.5, 1.5)
    if N_MICROBATCH > 1:
        for name, axis in PER_EXAMPLE_BATCH_AXIS.items():
            out[name] = _to_microbatches(out[name], axis)
    return {'x': out['x'], 'norm_mix_g': out['norm_mix_g'], 'w_in': out['w_in'], 'conv_w': out['conv_w'], 'q_norm_g': out['q_norm_g'], 'k_norm_g': out['k_norm_g'], 'rel_bias_table': out['rel_bias_table'], 'sinks': out['sinks'], 'out_norm_conv_g': out['out_norm_conv_g'], 'out_norm_attn_g': out['out_norm_attn_g'], 'w_out': out['w_out'], 'norm_ffn_g': out['norm_ffn_g'], 'w_up': out['w_up'], 'ffn_conv_w': out['ffn_conv_w'], 'ffn_conv_b': out['ffn_conv_b'], 'w_down': out['w_down'], 'loss_target': out['loss_target'], 'm_norm_mix_g': out['m_norm_mix_g'], 'm_w_in': out['m_w_in'], 'm_conv_w': out['m_conv_w'], 'm_q_norm_g': out['m_q_norm_g'], 'm_k_norm_g': out['m_k_norm_g'], 'm_rel_bias_table': out['m_rel_bias_table'], 'm_sinks': out['m_sinks'], 'm_out_norm_conv_g': out['m_out_norm_conv_g'], 'm_out_norm_attn_g': out['m_out_norm_attn_g'], 'm_w_out': out['m_w_out'], 'm_norm_ffn_g': out['m_norm_ffn_g'], 'm_w_up': out['m_w_up'], 'm_ffn_conv_w': out['m_ffn_conv_w'], 'm_ffn_conv_b': out['m_ffn_conv_b'], 'm_w_down': out['m_w_down'], 'v_norm_mix_g': out['v_norm_mix_g'], 'v_w_in': out['v_w_in'], 'v_conv_w': out['v_conv_w'], 'v_q_norm_g': out['v_q_norm_g'], 'v_k_norm_g': out['v_k_norm_g'], 'v_rel_bias_table': out['v_rel_bias_table'], 'v_sinks': out['v_sinks'], 'v_out_norm_conv_g': out['v_out_norm_conv_g'], 'v_out_norm_attn_g': out['v_out_norm_attn_g'], 'v_w_out': out['v_w_out'], 'v_norm_ffn_g': out['v_norm_ffn_g'], 'v_w_up': out['v_w_up'], 'v_ffn_conv_w': out['v_ffn_conv_w'], 'v_ffn_conv_b': out['v_ffn_conv_b'], 'v_w_down': out['v_w_down']}


def _loss(weights, diff, rest, loss_target):
    with _jax.named_scope("forward"):
        args = {**rest, TWIN_DIFF_INPUT: diff, **{k: w.astype(_WEIGHT_DTYPES[k]) for k, w in weights.items()}}
        y = _forward(args)
    with _jax.named_scope("loss_head"):
        err = _jnp.square(y.astype(_jnp.float32) - loss_target)
        return 0.5 * _jnp.sum(_jnp.mean(err, axis=-1)) if err.ndim else 0.5 * err


def _adamw(w, g, m, v):
    m = ADAM_B1 * m + (1.0 - ADAM_B1) * g
    v = ADAM_B2 * v + (1.0 - ADAM_B2) * _jnp.square(g)
    m_hat = m / (1.0 - ADAM_B1 ** ADAM_STEP)
    v_hat = v / (1.0 - ADAM_B2 ** ADAM_STEP)
    delta = -ADAM_LR * (m_hat / (_jnp.sqrt(v_hat) + ADAM_EPS) + ADAM_WD * w)
    return delta, m, v


def reference(x, norm_mix_g, w_in, conv_w, q_norm_g, k_norm_g, rel_bias_table, sinks, out_norm_conv_g, out_norm_attn_g, w_out, norm_ffn_g, w_up, ffn_conv_w, ffn_conv_b, w_down, loss_target, m_norm_mix_g, m_w_in, m_conv_w, m_q_norm_g, m_k_norm_g, m_rel_bias_table, m_sinks, m_out_norm_conv_g, m_out_norm_attn_g, m_w_out, m_norm_ffn_g, m_w_up, m_ffn_conv_w, m_ffn_conv_b, m_w_down, v_norm_mix_g, v_w_in, v_conv_w, v_q_norm_g, v_k_norm_g, v_rel_bias_table, v_sinks, v_out_norm_conv_g, v_out_norm_attn_g, v_w_out, v_norm_ffn_g, v_w_up, v_ffn_conv_w, v_ffn_conv_b, v_w_down):
    given = dict(x=x, norm_mix_g=norm_mix_g, w_in=w_in, conv_w=conv_w, q_norm_g=q_norm_g, k_norm_g=k_norm_g, rel_bias_table=rel_bias_table, sinks=sinks, out_norm_conv_g=out_norm_conv_g, out_norm_attn_g=out_norm_attn_g, w_out=w_out, norm_ffn_g=norm_ffn_g, w_up=w_up, ffn_conv_w=ffn_conv_w, ffn_conv_b=ffn_conv_b, w_down=w_down, loss_target=loss_target, m_norm_mix_g=m_norm_mix_g, m_w_in=m_w_in, m_conv_w=m_conv_w, m_q_norm_g=m_q_norm_g, m_k_norm_g=m_k_norm_g, m_rel_bias_table=m_rel_bias_table, m_sinks=m_sinks, m_out_norm_conv_g=m_out_norm_conv_g, m_out_norm_attn_g=m_out_norm_attn_g, m_w_out=m_w_out, m_norm_ffn_g=m_norm_ffn_g, m_w_up=m_w_up, m_ffn_conv_w=m_ffn_conv_w, m_ffn_conv_b=m_ffn_conv_b, m_w_down=m_w_down, v_norm_mix_g=v_norm_mix_g, v_w_in=v_w_in, v_conv_w=v_conv_w, v_q_norm_g=v_q_norm_g, v_k_norm_g=v_k_norm_g, v_rel_bias_table=v_rel_bias_table, v_sinks=v_sinks, v_out_norm_conv_g=v_out_norm_conv_g, v_out_norm_attn_g=v_out_norm_attn_g, v_w_out=v_w_out, v_norm_ffn_g=v_norm_ffn_g, v_w_up=v_w_up, v_ffn_conv_w=v_ffn_conv_w, v_ffn_conv_b=v_ffn_conv_b, v_w_down=v_w_down)
    weights = {n: given[n] for n in TWIN_WEIGHTS}
    shared = {n: given[n] for n in SHARED_INPUTS}
    per_example = {n: given[n] for n in ['x']}
    grad_fn = _jax.value_and_grad(_loss, argnums=(0, 1))

    def one_microbatch(ex, loss_target):
        ex = dict(ex)
        diff = ex.pop(TWIN_DIFF_INPUT)
        return grad_fn(weights, diff, {**shared, **ex}, loss_target)

    if N_MICROBATCH == 1:
        loss, (grad_w, grad_x) = one_microbatch(per_example, given["loss_target"])
    else:
        def body(carry, xs):
            loss_sum, grad_sum = carry
            l_k, (gw_k, gx_k) = one_microbatch(xs[0], xs[1])
            with _jax.named_scope("update"):
                return (loss_sum + l_k, _jax.tree.map(_jnp.add, grad_sum, gw_k)), gx_k

        init = (_jnp.zeros((), _jnp.float32), _jax.tree.map(_jnp.zeros_like, weights))
        (loss, grad_w), grad_x = _jax.lax.scan(body, init, (per_example, given["loss_target"]))
    with _jax.named_scope("update"):
        delta_w, new_m, new_v = {}, {}, {}
        for n in TWIN_WEIGHTS:
            delta_w[n], new_m[n], new_v[n] = _adamw(weights[n], grad_w[n], given["m_" + n], given["v_" + n])
    return (loss, grad_x, *[grad_w[n] for n in TWIN_WEIGHTS], *[delta_w[n] for n in TWIN_WEIGHTS],
            *[new_m[n] for n in TWIN_WEIGHTS], *[new_v[n] for n in TWIN_WEIGHTS])
```

```python
import functools
import math

import numpy as np

import jax
import jax.numpy as jnp
from jax import lax
from jax.experimental import pallas as pl
from jax.experimental.pallas import tpu as pltpu

F32 = jnp.float32
BF = jnp.bfloat16
SDS = jax.ShapeDtypeStruct

T = 2048
D = 1024
CW = 512
AW = 512
HD = 64
NH = 8
NKV = 2
GQ = 4
INW = 2304
DFF = 2816
BLK = 128
NB = T // BLK
NBUCKET = 32
EPS = 1e-6
NEG_INF = -1e30
N_CHIPS = 4
N_DEV = 8

ADAM_LR = 0.001
ADAM_B1 = 0.9
ADAM_B2 = 0.999
ADAM_EPS = 1e-08
ADAM_WD = 0.01
ADAM_STEP = 10

MIB = 1024 * 1024
MESH = pl.DeviceIdType.MESH

_pcall = pl.pallas_call


def _params(sem=None, vmem_mib=None):
    kw = {}
    if sem is not None:
        kw["dimension_semantics"] = sem
    if vmem_mib is not None:
        kw["vmem_limit_bytes"] = vmem_mib * MIB
    return pltpu.CompilerParams(**kw)


def _dot(a, b, ca, cb):
    return lax.dot_general(a, b, (((ca,), (cb,)), ((), ())), preferred_element_type=F32)


def _rms_bwd(dy, x, r, g):
    dg = jnp.sum(dy * (x * r), axis=0, keepdims=True)
    dgx = dy * g
    dx = r * dgx - x * (r * r * r) * jnp.mean(x * dgx, axis=-1, keepdims=True)
    return dx, dg


def _inproj(x, g1, w_int):
    tm = 256

    def body(x_ref, g_ref, w_ref, proj_ref, u_ref):
        xf = x_ref[...]
        r = lax.rsqrt(jnp.mean(xf * xf, axis=-1, keepdims=True) + EPS)
        u = (xf * r * g_ref[...]).astype(BF)
        u_ref[...] = u
        proj_ref[...] = _dot(u, w_ref[...], 1, 1)

    return _pcall(
        body, name="inproj", grid=(T // tm,),
        in_specs=[pl.BlockSpec((tm, D), lambda i: (i, 0)), pl.BlockSpec((1, D), lambda i: (0, 0)),
                  pl.BlockSpec((INW, D), lambda i: (0, 0))],
        out_specs=[pl.BlockSpec((tm, INW), lambda i: (i, 0)), pl.BlockSpec((tm, D), lambda i: (i, 0))],
        out_shape=[SDS((T, INW), F32), SDS((T, D), BF)],
        compiler_params=_params(("parallel",), 40),
    )(x, g1, w_int)


def _outproj(y, w_out, x, g2):
    tm = 256

    def body(y_ref, w_ref, x_ref, g_ref, h1_ref, u2_ref):
        h1 = x_ref[...] + _dot(y_ref[...], w_ref[...], 1, 0)
        h1_ref[...] = h1
        r = lax.rsqrt(jnp.mean(h1 * h1, axis=-1, keepdims=True) + EPS)
        u2_ref[...] = (h1 * r * g_ref[...]).astype(BF)

    return _pcall(
        body, name="outproj", grid=(T // tm,),
        in_specs=[pl.BlockSpec((tm, D), lambda i: (i, 0)), pl.BlockSpec((D, D), lambda i: (0, 0)),
                  pl.BlockSpec((tm, D), lambda i: (i, 0)), pl.BlockSpec((1, D), lambda i: (0, 0))],
        out_specs=[pl.BlockSpec((tm, D), lambda i: (i, 0)), pl.BlockSpec((tm, D), lambda i: (i, 0))],
        out_shape=[SDS((T, D), F32), SDS((T, D), BF)],
        compiler_params=_params(("parallel",), 32),
    )(y, w_out, x, g2)


def _ffn_up(u2, w_upt):
    tm, tn = 1024, 512

    def body(u_ref, w_ref, o_ref):
        o_ref[...] = _dot(u_ref[...], w_ref[...], 1, 1)

    return _pcall(
        body, name="ffn_up", grid=(T // tm, 2 * DFF // tn),
        in_specs=[pl.BlockSpec((tm, D), lambda i, j: (i, 0)), pl.BlockSpec((tn, D), lambda i, j: (j, 0))],
        out_specs=pl.BlockSpec((tm, tn), lambda i, j: (i, j)),
        out_shape=SDS((T, 2 * DFF), F32),
        compiler_params=_params(("parallel", "parallel"), 32),
    )(u2, w_upt)


def _ffn_down(a, w_down, h1, tgt):
    tm = 256

    def body(a_ref, w_ref, h1_ref, t_ref, dh_ref, dhb_ref, l_ref):
        @pl.when(pl.program_id(0) == 0)
        def _():
            l_ref[...] = jnp.zeros_like(l_ref)

        h2 = h1_ref[...] + _dot(a_ref[...], w_ref[...], 1, 0)
        e = h2 - t_ref[...]
        dh = e * (1.0 / D)
        dh_ref[...] = dh
        dhb_ref[...] = dh.astype(BF)
        e2 = jnp.sum((e * e).reshape(tm // 8, 8, D), axis=0)
        acc = e2[:, 0:128]
        for k in range(1, D // 128):
            acc = acc + e2[:, k * 128:(k + 1) * 128]
        l_ref[...] += acc

    return _pcall(
        body, name="ffn_down", grid=(T // tm,),
        in_specs=[pl.BlockSpec((tm, DFF), lambda i: (i, 0)), pl.BlockSpec((DFF, D), lambda i: (0, 0)),
                  pl.BlockSpec((tm, D), lambda i: (i, 0)), pl.BlockSpec((tm, D), lambda i: (i, 0))],
        out_specs=[pl.BlockSpec((tm, D), lambda i: (i, 0)), pl.BlockSpec((tm, D), lambda i: (i, 0)),
                   pl.BlockSpec((8, 128), lambda i: (0, 0))],
        out_shape=[SDS((T, D), F32), SDS((T, D), BF), SDS((8, 128), F32)],
        compiler_params=_params(("arbitrary",), 40),
    )(a, w_down, h1, tgt)


def _bucket_table():
    q = np.arange(BLK, dtype=np.int32)[:, None]
    j = np.arange(2 * BLK, dtype=np.int32)[None, :]
    n = np.maximum(q + BLK - j, 0)
    nf = np.maximum(n, 1).astype(np.float32)
    max_exact = NBUCKET // 2
    large = max_exact + (np.log(nf / np.float32(max_exact)) / np.float32(math.log(BLK / max_exact))
                         * np.float32(NBUCKET - max_exact)).astype(np.int32)
    large = np.minimum(large, NBUCKET - 1)
    return np.where(n < max_exact, n, large).astype(np.int32)


def _band_bias(table, bucket):
    def body(tab_ref, bk_ref, o_ref):
        bk = bk_ref[...]
        eq = [bk == b for b in range(NBUCKET)]
        for h in range(NH):
            acc = jnp.zeros((BLK, 2 * BLK), F32)
            for b in range(NBUCKET):
                acc = jnp.where(eq[b], tab_ref[b, h], acc)
            o_ref[h * BLK:(h + 1) * BLK, :] = acc

    return _pcall(
        body, name="band_bias", out_shape=SDS((NH * BLK, 2 * BLK), F32),
        in_specs=[pl.BlockSpec(memory_space=pltpu.SMEM), pl.BlockSpec(memory_space=pltpu.VMEM)],
        out_specs=pl.BlockSpec(memory_space=pltpu.VMEM),
    )(table, bucket)


def _band_bias_bwd(dbias, bucket):
    def body(db_ref, bk_ref, o_ref):
        bk = bk_ref[...]
        for b in range(NBUCKET):
            m = bk == b
            for h in range(NH):
                v = jnp.where(m, db_ref[h * BLK:(h + 1) * BLK, :], 0.0)
                s = jnp.sum(jnp.sum(v, axis=1, keepdims=True), axis=0, keepdims=True)
                o_ref[h:h + 1, b:b + 1] = s

    return _pcall(body, name="band_bias_bwd", out_shape=SDS((NH, NBUCKET), F32))(dbias, bucket)


def _mix_forward(P, zc8, zh8, pkv, first, cw, qg, kg, gco, gao, sink_ref, bias_ref):
    gate_b = P[:, 0:CW]
    gate_c = P[:, CW:2 * CW]
    hc = P[:, 2 * CW:3 * CW]
    z = gate_c * hc
    keep = jnp.where(first, 0.0, 1.0)
    zp = zc8 * zh8 * keep
    p1 = zp[7:8, :]
    p2 = zp[6:7, :]
    row = lax.broadcasted_iota(jnp.int32, (BLK, 1), 0)
    z1 = jnp.where(row == 0, p1, pltpu.roll(z, 1, 0))
    z2 = jnp.where(row == 0, p2, jnp.where(row == 1, p1, pltpu.roll(z, 2, 0)))
    cz = cw[0:1, :] * z2 + cw[1:2, :] * z1 + cw[2:3, :] * z
    y_conv = gate_b * cz

    scale = HD ** -0.5
    qi = lax.broadcasted_iota(jnp.int32, (GQ * BLK, 2 * BLK), 0) & (BLK - 1)
    kj = lax.broadcasted_iota(jnp.int32, (GQ * BLK, 2 * BLK), 1)
    dd = qi + BLK - kj
    first_key = jnp.where(first, BLK, 0)
    valid = (dd >= 0) & (dd < BLK) & (kj >= first_key)

    q0 = 3 * CW
    k0 = q0 + AW
    v0 = k0 + NKV * HD
    heads = []
    outs = []
    for kv in range(NKV):
        kb_raw = jnp.concatenate([pkv[:, kv * HD:(kv + 1) * HD], P[:, k0 + kv * HD:k0 + (kv + 1) * HD]], axis=0)
        rk = lax.rsqrt(jnp.mean(kb_raw * kb_raw, axis=-1, keepdims=True) + EPS)
        kb = (kb_raw * rk * kg).astype(BF)
        vb = jnp.concatenate([pkv[:, NKV * HD + kv * HD:NKV * HD + (kv + 1) * HD],
                              P[:, v0 + kv * HD:v0 + (kv + 1) * HD]], axis=0).astype(BF)
        q_raw, rq, qn = [], [], []
        for g in range(GQ):
            h = kv * GQ + g
            qh = P[:, q0 + h * HD:q0 + (h + 1) * HD]
            r = lax.rsqrt(jnp.mean(qh * qh, axis=-1, keepdims=True) + EPS)
            q_raw.append(qh)
            rq.append(r)
            qn.append(qh * r * qg)
        Q = jnp.concatenate(qn, axis=0).astype(BF)
        S = _dot(Q, kb, 1, 1) * scale + bias_ref[kv * GQ * BLK:(kv + 1) * GQ * BLK, :]
        S = jnp.where(valid, S, NEG_INF)
        sink = jnp.concatenate([jnp.full((BLK, 1), sink_ref[0, kv * GQ + g], F32) for g in range(GQ)], axis=0)
        m = jnp.maximum(jnp.max(S, axis=-1, keepdims=True), sink)
        p = jnp.exp(S - m)
        es = jnp.exp(sink - m)
        denom = jnp.sum(p, axis=-1, keepdims=True) + es
        probs = p / denom
        O = _dot(probs.astype(BF), vb, 1, 0)
        heads.append(dict(kb_raw=kb_raw, rk=rk, kb=kb, vb=vb, q_raw=q_raw, rq=rq, Q=Q, probs=probs,
                          psink=es / denom, O=O))
        outs += [O[g * BLK:(g + 1) * BLK, :] for g in range(GQ)]
    y_attn = jnp.concatenate(outs, axis=1)

    rc = lax.rsqrt(jnp.mean(y_conv * y_conv, axis=-1, keepdims=True) + EPS)
    ra = lax.rsqrt(jnp.mean(y_attn * y_attn, axis=-1, keepdims=True) + EPS)
    y = jnp.concatenate([y_conv * rc * gco, y_attn * ra * gao], axis=1)
    return dict(gate_b=gate_b, gate_c=gate_c, hc=hc, z=z, z1=z1, z2=z2, cz=cz, y_conv=y_conv, y_attn=y_attn,
                rc=rc, ra=ra, heads=heads, y=y, row=row, scale=scale)


def _mix_in_specs(blk):
    return [
        pl.BlockSpec(memory_space=pltpu.SMEM),
        pl.BlockSpec((BLK, INW), lambda s: (blk(s), 0)),
        pl.BlockSpec((8, CW), lambda s: (jnp.maximum(blk(s) * (BLK // 8) - 1, 0), 1)),
        pl.BlockSpec((8, CW), lambda s: (jnp.maximum(blk(s) * (BLK // 8) - 1, 0), 2)),
        pl.BlockSpec((BLK, 2 * NKV * HD), lambda s: (jnp.maximum(blk(s) - 1, 0), (3 * CW + AW) // (2 * NKV * HD))),
    ]


def _mix_param_specs():
    return [
        pl.BlockSpec((8, CW), lambda s: (0, 0)),
        pl.BlockSpec((1, HD), lambda s: (0, 0)),
        pl.BlockSpec((1, HD), lambda s: (0, 0)),
        pl.BlockSpec((1, CW), lambda s: (0, 0)),
        pl.BlockSpec((1, AW), lambda s: (0, 0)),
        pl.BlockSpec((NH * BLK, 2 * BLK), lambda s: (0, 0)),
    ]


def _mix_fwd(proj, sinks, cw8, qg, kg, gco, gao, bias):
    def body(sink_ref, p_ref, zc_ref, zh_ref, pkv_ref, cw_ref, qg_ref, kg_ref, gco_ref, gao_ref, bias_ref, y_ref):
        first = pl.program_id(0) == 0
        f = _mix_forward(p_ref[...], zc_ref[...], zh_ref[...], pkv_ref[...], first, cw_ref[...], qg_ref[...],
                         kg_ref[...], gco_ref[...], gao_ref[...], sink_ref, bias_ref)
        y_ref[...] = f["y"].astype(BF)

    return _pcall(
        body, name="mix_fwd", grid=(NB,),
        in_specs=_mix_in_specs(lambda s: s) + _mix_param_specs(),
        out_specs=pl.BlockSpec((BLK, D), lambda s: (s, 0)),
        out_shape=SDS((T, D), BF),
        compiler_params=_params(("parallel",), 32),
    )(sinks, proj, proj, proj, proj, cw8, qg, kg, gco, gao, bias)


def _mix_bwd(proj, dy, sinks, cw8, qg, kg, gco, gao, bias):
    def blk(s):
        return NB - 1 - s

    def body(sink_ref, p_ref, zc_ref, zh_ref, pkv_ref, dy_ref, cw_ref, qg_ref, kg_ref, gco_ref, gao_ref, bias_ref,
             dproj_ref, dcw_ref, dqg_ref, dkg_ref, dgco_ref, dgao_ref, dsink_ref, dbias_ref,
             ndcz_ref, dkc_ref, dvc_ref):
        s = pl.program_id(0)
        first = s == NB - 1

        @pl.when(s == 0)
        def _():
            for r in (dcw_ref, dqg_ref, dkg_ref, dgco_ref, dgao_ref, dsink_ref, dbias_ref, ndcz_ref, dkc_ref, dvc_ref):
                r[...] = jnp.zeros_like(r)

        cw = cw_ref[...]
        qg_v, kg_v, gco_v, gao_v = qg_ref[...], kg_ref[...], gco_ref[...], gao_ref[...]
        f = _mix_forward(p_ref[...], zc_ref[...], zh_ref[...], pkv_ref[...], first, cw, qg_v, kg_v, gco_v, gao_v,
                         sink_ref, bias_ref)
        dy = dy_ref[...]
        dyc, dgco = _rms_bwd(dy[:, 0:CW], f["y_conv"], f["rc"], gco_v)
        dya, dgao = _rms_bwd(dy[:, CW:CW + AW], f["y_attn"], f["ra"], gao_v)
        dgco_ref[...] += dgco
        dgao_ref[...] += dgao

        row = f["row"]
        dgate_b = dyc * f["cz"]
        dcz = dyc * f["gate_b"]
        dcw_ref[0:1, :] += jnp.sum(dcz * f["z2"], axis=0, keepdims=True)
        dcw_ref[1:2, :] += jnp.sum(dcz * f["z1"], axis=0, keepdims=True)
        dcw_ref[2:3, :] += jnp.sum(dcz * f["z"], axis=0, keepdims=True)
        nxt = ndcz_ref[...]
        n0 = nxt[0:1, :]
        n1 = nxt[1:2, :]
        d1 = jnp.where(row == BLK - 1, n0, pltpu.roll(dcz, BLK - 1, 0))
        d2 = jnp.where(row == BLK - 1, n1, jnp.where(row == BLK - 2, n0, pltpu.roll(dcz, BLK - 2, 0)))
        dz = cw[2:3, :] * dcz + cw[1:2, :] * d1 + cw[0:1, :] * d2
        ndcz_ref[...] = dcz[0:8, :]
        dproj_ref[:, 0:CW] = dgate_b.astype(BF)
        dproj_ref[:, CW:2 * CW] = (dz * f["hc"]).astype(BF)
        dproj_ref[:, 2 * CW:3 * CW] = (dz * f["gate_c"]).astype(BF)

        scale = f["scale"]
        dq_cols, dk_cols, dv_cols = [], [], []
        for kv in range(NKV):
            hd = f["heads"][kv]
            dO = jnp.concatenate([dya[:, (kv * GQ + g) * HD:(kv * GQ + g + 1) * HD] for g in range(GQ)], axis=0)
            delta = jnp.sum(dO * hd["O"], axis=-1, keepdims=True)
            dOb = dO.astype(BF)
            dP = _dot(dOb, hd["vb"], 1, 1)
            dS = hd["probs"] * (dP - delta)
            dsk = hd["psink"] * delta
            for g in range(GQ):
                h = kv * GQ + g
                tot = jnp.sum(dsk[g * BLK:(g + 1) * BLK, :], axis=0, keepdims=True)
                dsink_ref[h:h + 1, :] -= jnp.broadcast_to(tot, (1, 128))
            dbias_ref[kv * GQ * BLK:(kv + 1) * GQ * BLK, :] += dS
            dSs = (dS * scale).astype(BF)
            dQ = _dot(dSs, hd["kb"], 1, 0)
            dKb = _dot(dSs, hd["Q"], 0, 0)
            dVb = _dot(hd["probs"].astype(BF), dOb, 0, 0)
            dkn = dKb[BLK:, :] + dkc_ref[:, kv * HD:(kv + 1) * HD]
            dvn = dVb[BLK:, :] + dvc_ref[:, kv * HD:(kv + 1) * HD]
            dkc_ref[:, kv * HD:(kv + 1) * HD] = dKb[:BLK, :]
            dvc_ref[:, kv * HD:(kv + 1) * HD] = dVb[:BLK, :]
            dk_raw, dkg = _rms_bwd(dkn, hd["kb_raw"][BLK:, :], hd["rk"][BLK:, :], kg_v)
            dkg_ref[...] += dkg
            dk_cols.append(dk_raw)
            dv_cols.append(dvn)
            for g in range(GQ):
                dq_raw, dqg = _rms_bwd(dQ[g * BLK:(g + 1) * BLK, :], hd["q_raw"][g], hd["rq"][g], qg_v)
                dqg_ref[...] += dqg
                dq_cols.append(dq_raw)
        dproj_ref[:, 3 * CW:INW] = jnp.concatenate(dq_cols + dk_cols + dv_cols, axis=1).astype(BF)

    small = lambda r, c: pl.BlockSpec((r, c), lambda s: (0, 0))
    return _pcall(
        body, name="mix_bwd", grid=(NB,),
        in_specs=_mix_in_specs(blk) + [pl.BlockSpec((BLK, D), lambda s: (blk(s), 0))] + _mix_param_specs(),
        out_specs=[pl.BlockSpec((BLK, INW), lambda s: (blk(s), 0)), small(8, CW), small(1, HD), small(1, HD),
                   small(1, CW), small(1, AW), small(NH, 128), small(NH * BLK, 2 * BLK)],
        out_shape=[SDS((T, INW), BF), SDS((8, CW), F32), SDS((1, HD), F32), SDS((1, HD), F32), SDS((1, CW), F32),
                   SDS((1, AW), F32), SDS((NH, 128), F32), SDS((NH * BLK, 2 * BLK), F32)],
        scratch_shapes=[pltpu.VMEM((8, CW), F32), pltpu.VMEM((BLK, NKV * HD), F32), pltpu.VMEM((BLK, NKV * HD), F32)],
        compiler_params=_params(("arbitrary",), 40),
    )(sinks, proj, proj, proj, proj, dy, cw8, qg, kg, gco, gao, bias)


FT = 256
NFT = DFF // FT


def _shift_down(u, row, k):
    return jnp.where(row >= k, pltpu.roll(u, k, 0), 0.0)


def _shift_up(u, row, k):
    return jnp.where(row < T - k, pltpu.roll(u, T - k, 0), 0.0)


def _ffn_act_specs():
    return [
        pl.BlockSpec((T, FT), lambda j: (0, j)), pl.BlockSpec((T, FT), lambda j: (0, NFT + j)),
        pl.BlockSpec((8, FT), lambda j: (0, j)), pl.BlockSpec((8, FT), lambda j: (0, NFT + j)),
        pl.BlockSpec((1, FT), lambda j: (0, j)), pl.BlockSpec((1, FT), lambda j: (0, NFT + j)),
    ]


def _ffn_act(up, fw8, fb):
    def body(ug_ref, uv_ref, wg_ref, wv_ref, bg_ref, bv_ref, a_ref):
        row = lax.broadcasted_iota(jnp.int32, (T, 1), 0)

        def conv(u, w, b):
            return w[0:1, :] * _shift_down(u, row, 2) + w[1:2, :] * _shift_down(u, row, 1) + w[2:3, :] * u + b

        gp = conv(ug_ref[...], wg_ref[...], bg_ref[...])
        vp = conv(uv_ref[...], wv_ref[...], bv_ref[...])
        a_ref[...] = (gp * jax.nn.sigmoid(gp) * vp).astype(BF)

    return _pcall(
        body, name="ffn_act", grid=(NFT,), in_specs=_ffn_act_specs(),
        out_specs=pl.BlockSpec((T, FT), lambda j: (0, j)), out_shape=SDS((T, DFF), BF),
        compiler_params=_params(("parallel",), 48),
    )(up, up, fw8, fw8, fb, fb)


def _ffn_act_bwd(up, da, fw8, fb):
    def body(ug_ref, uv_ref, wg_ref, wv_ref, bg_ref, bv_ref, da_ref,
             dug_ref, duv_ref, dwg_ref, dwv_ref, dbg_ref, dbv_ref):
        row = lax.broadcasted_iota(jnp.int32, (T, 1), 0)
        wg, wv = wg_ref[...], wv_ref[...]
        ug, uv = ug_ref[...], uv_ref[...]
        ug1, ug2 = _shift_down(ug, row, 1), _shift_down(ug, row, 2)
        uv1, uv2 = _shift_down(uv, row, 1), _shift_down(uv, row, 2)
        gp = wg[0:1, :] * ug2 + wg[1:2, :] * ug1 + wg[2:3, :] * ug + bg_ref[...]
        vp = wv[0:1, :] * uv2 + wv[1:2, :] * uv1 + wv[2:3, :] * uv + bv_ref[...]
        sig = jax.nn.sigmoid(gp)
        da = da_ref[...]
        dvp = da * (gp * sig)
        dgp = da * vp * (sig * (1.0 + gp * (1.0 - sig)))

        def finish(dp, u, u1, u2, w, du_ref, dw_ref, db_ref):
            db_ref[...] = jnp.sum(dp, axis=0, keepdims=True)
            dw_ref[...] = jnp.zeros_like(dw_ref)
            dw_ref[0:1, :] = jnp.sum(dp * u2, axis=0, keepdims=True)
            dw_ref[1:2, :] = jnp.sum(dp * u1, axis=0, keepdims=True)
            dw_ref[2:3, :] = jnp.sum(dp * u, axis=0, keepdims=True)
            du = w[2:3, :] * dp + w[1:2, :] * _shift_up(dp, row, 1) + w[0:1, :] * _shift_up(dp, row, 2)
            du_ref[...] = du.astype(BF)

        finish(dgp, ug, ug1, ug2, wg, dug_ref, dwg_ref, dbg_ref)
        finish(dvp, uv, uv1, uv2, wv, duv_ref, dwv_ref, dbv_ref)

    col = lambda r: pl.BlockSpec((r, FT), lambda j: (0, j))
    return _pcall(
        body, name="ffn_act_bwd", grid=(NFT,),
        in_specs=_ffn_act_specs() + [pl.BlockSpec((T, FT), lambda j: (0, j))],
        out_specs=[col(T), col(T), col(8), col(8), col(1), col(1)],
        out_shape=[SDS((T, DFF), BF), SDS((T, DFF), BF), SDS((8, DFF), F32), SDS((8, DFF), F32),
                   SDS((1, DFF), F32), SDS((1, DFF), F32)],
        compiler_params=_params(("parallel",), 56),
    )(up, up, fw8, fw8, fb, fb, da)


def _ffn_down_bwd(dh2b, w_down):
    tm = 256

    def body(d_ref, w_ref, o_ref):
        o_ref[...] = _dot(d_ref[...], w_ref[...], 1, 1)

    return _pcall(
        body, name="ffn_down_bwd", grid=(T // tm,),
        in_specs=[pl.BlockSpec((tm, D), lambda i: (i, 0)), pl.BlockSpec((DFF, D), lambda i: (0, 0))],
        out_specs=pl.BlockSpec((tm, DFF), lambda i: (i, 0)), out_shape=SDS((T, DFF), F32),
        compiler_params=_params(("parallel",), 40),
    )(dh2b, w_down)


def _norm_matmul_bwd(name, a_list, w_t, k_offsets, xin, g, dres, want_bf16):
    tm = 256
    ks = [a.shape[1] for a in a_list]
    n_a = len(a_list)

    def body(*refs):
        a_refs = refs[:n_a]
        w_ref, x_ref, g_ref, r_ref = refs[n_a:n_a + 4]
        outs = refs[n_a + 4:]
        dx_ref, dg_ref = outs[0], outs[-1]

        @pl.when(pl.program_id(0) == 0)
        def _():
            dg_ref[...] = jnp.zeros_like(dg_ref)

        du = _dot(a_refs[0][...], w_ref[k_offsets[0]:k_offsets[0] + ks[0], :], 1, 0)
        for k in range(1, n_a):
            du = du + _dot(a_refs[k][...], w_ref[k_offsets[k]:k_offsets[k] + ks[k], :], 1, 0)
        x = x_ref[...]
        r = lax.rsqrt(jnp.mean(x * x, axis=-1, keepdims=True) + EPS)
        dx, dg = _rms_bwd(du, x, r, g_ref[...])
        dx = r_ref[...] + dx
        dx_ref[...] = dx
        if want_bf16:
            outs[1][...] = dx.astype(BF)
        dg_ref[...] += dg

    tile = lambda c: pl.BlockSpec((tm, c), lambda i: (i, 0))
    out_specs = [tile(D)] + ([tile(D)] if want_bf16 else []) + [pl.BlockSpec((1, D), lambda i: (0, 0))]
    out_shape = [SDS((T, D), F32)] + ([SDS((T, D), BF)] if want_bf16 else []) + [SDS((1, D), F32)]
    return _pcall(
        body, name=name, grid=(T // tm,),
        in_specs=[tile(k) for k in ks] + [pl.BlockSpec(w_t.shape, lambda i: (0, 0)), tile(D),
                                           pl.BlockSpec((1, D), lambda i: (0, 0)), tile(D)],
        out_specs=out_specs, out_shape=out_shape,
        compiler_params=_params(("arbitrary",), 56),
    )(*a_list, w_t, xin, g, dres)


def _out_bwd(dh1b, w_out):
    tm = 256

    def body(d_ref, w_ref, o_ref):
        o_ref[...] = _dot(d_ref[...], w_ref[...], 1, 1)

    return _pcall(
        body, name="out_bwd", grid=(T // tm,),
        in_specs=[pl.BlockSpec((tm, D), lambda i: (i, 0)), pl.BlockSpec((D, D), lambda i: (0, 0))],
        out_specs=pl.BlockSpec((tm, D), lambda i: (i, 0)), out_shape=SDS((T, D), F32),
        compiler_params=_params(("parallel",), 32),
    )(dh1b, w_out)


def _wgrad(name, a_list, b):
    tm = 256
    steps = [a.shape[1] // tm for a in a_list]
    starts = [sum(steps[:k]) for k in range(len(a_list))]
    n_a = len(a_list)

    def body(*refs):
        a_refs, b_ref, o32_ref, obf_ref = refs[:n_a], refs[n_a], refs[n_a + 1], refs[n_a + 2]
        i = pl.program_id(0)
        for k in range(n_a):
            @pl.when((i >= starts[k]) & (i < starts[k] + steps[k]))
            def _(k=k):
                r = _dot(a_refs[k][...], b_ref[...], 0, 0)
                o32_ref[...] = r
                obf_ref[...] = r.astype(BF)

    def a_spec(k):
        return pl.BlockSpec((T, tm), lambda i: (0, jnp.clip(i - starts[k], 0, steps[k] - 1)))

    m_total = tm * sum(steps)
    return _pcall(
        body, name=name, grid=(sum(steps),),
        in_specs=[a_spec(k) for k in range(n_a)] + [pl.BlockSpec((T, D), lambda i: (0, 0))],
        out_specs=[pl.BlockSpec((tm, D), lambda i: (i, 0)), pl.BlockSpec((tm, D), lambda i: (i, 0))],
        out_shape=[SDS((m_total, D), F32), SDS((m_total, D), BF)],
        compiler_params=_params(("parallel",), 40),
    )(*a_list, b)


ANY = pl.BlockSpec(memory_space=pl.ANY)
BIG = (("w_in", INW // N_CHIPS), ("w_out", D // N_CHIPS), ("w_up", 2 * DFF // N_CHIPS), ("w_down", DFF // N_CHIPS))
N_BIG = len(BIG)


def _place():
    x, y, c = lax.axis_index("x"), lax.axis_index("y"), lax.axis_index("c")
    return x, y, c, [(1 - x, y), (x, 1 - y), (1 - x, 1 - y)]


def _rcopy(src, dst, ssem, rsem, dev):
    return pltpu.make_async_remote_copy(src_ref=src, dst_ref=dst, send_sem=ssem, recv_sem=rsem, device_id=dev,
                                        device_id_type=MESH)


def _gather_weights(shards, smalls):
    n_small = len(smalls)

    def body(*refs):
        s_big, s_small = refs[:N_BIG], refs[N_BIG:N_BIG + n_small]
        o_big = refs[N_BIG + n_small:2 * N_BIG + n_small]
        o_small = refs[2 * N_BIG + n_small:2 * (N_BIG + n_small)]
        send_sems, recv_sems, loc_sems = refs[2 * (N_BIG + n_small):]
        x, y, c, chips = _place()
        j = 2 * x + y
        local, sends = [], []
        for w, (_, R) in enumerate(BIG):
            cp = pltpu.make_async_copy(s_big[w], o_big[w].at[pl.ds(pl.multiple_of(j * R, 16), R)], loc_sems.at[w])
            cp.start()
            local.append(cp)
        for w in range(n_small):
            cp = pltpu.make_async_copy(s_small[w], o_small[w].at[j], loc_sems.at[N_BIG + w])
            cp.start()
            local.append(cp)

        def rows(w, chip_index, core):
            R = BIG[w][1]
            return pl.ds(pl.multiple_of(chip_index * R + core * (R // 2), 16), R // 2)

        for w in range(N_BIG):
            h = BIG[w][1] // 2
            for r, (px, py) in enumerate(chips):
                k = w * 6 + r
                cp = _rcopy(s_big[w].at[pl.ds(pl.multiple_of(c * h, 16), h)], o_big[w].at[rows(w, j, c)],
                            send_sems.at[k], recv_sems.at[k], (px, py, c))
                cp.start()
                sends.append(cp)
        for w in range(n_small):
            for r, (px, py) in enumerate(chips):
                k = N_BIG * 6 + w * 3 + r
                cp = _rcopy(s_small[w], o_small[w].at[j], send_sems.at[k], recv_sems.at[k], (px, py, c))
                cp.start()
                sends.append(cp)
        for w in range(N_BIG):
            for r, (px, py) in enumerate(chips):
                pj = 2 * px + py
                got = o_big[w].at[rows(w, pj, c)]
                _rcopy(got, got, send_sems.at[w * 6 + r], recv_sems.at[w * 6 + r], (px, py, c)).wait_recv()
                k = w * 6 + 3 + r
                cp = _rcopy(got, got, send_sems.at[k], recv_sems.at[k], (x, y, 1 - c))
                cp.start()
                sends.append(cp)
        for w in range(N_BIG):
            for r, (px, py) in enumerate(chips):
                pj = 2 * px + py
                k = w * 6 + 3 + r
                got = o_big[w].at[rows(w, pj, 1 - c)]
                _rcopy(got, got, send_sems.at[k], recv_sems.at[k], (x, y, 1 - c)).wait_recv()
        for w in range(n_small):
            for r, (px, py) in enumerate(chips):
                k = N_BIG * 6 + w * 3 + r
                got = o_small[w].at[2 * px + py]
                _rcopy(got, got, send_sems.at[k], recv_sems.at[k], (px, py, c)).wait_recv()
        for cp in sends:
            cp.wait_send()
        for cp in local:
            cp.wait()

    n_sem = N_BIG * 6 + n_small * 3
    out_shape = [SDS((N_CHIPS * s.shape[0], D), BF) for s in shards] + [SDS((N_CHIPS,) + s.shape, F32) for s in smalls]
    return _pcall(
        body, name="gather_weights", out_shape=out_shape,
        in_specs=[ANY] * (N_BIG + n_small), out_specs=[ANY] * (N_BIG + n_small),
        scratch_shapes=[pltpu.SemaphoreType.DMA((n_sem,)), pltpu.SemaphoreType.DMA((n_sem,)),
                        pltpu.SemaphoreType.DMA((N_BIG + n_small,))],
    )(*shards, *smalls)


def _to_sibling(gbf, small):
    def body(*refs):
        g_refs, sm_ref = refs[:N_BIG], refs[N_BIG]
        o_refs, sm_all = refs[N_BIG + 1:2 * N_BIG + 1], refs[2 * N_BIG + 1]
        send_sems, recv_sems, loc_sem = refs[2 * N_BIG + 2:]
        x, y, c, _ = _place()
        me = 4 * x + 2 * y + c
        sib = (x, y, 1 - c)
        loc = pltpu.make_async_copy(sm_ref, sm_all.at[me], loc_sem)
        loc.start()
        sends = []
        for w in range(N_BIG):
            for jj in range(N_CHIPS):
                k = w * N_CHIPS + jj
                cp = _rcopy(g_refs[w].at[2 * jj + (1 - c)], o_refs[w].at[jj], send_sems.at[k], recv_sems.at[k], sib)
                cp.start()
                sends.append(cp)
        peers = []
        for rel in range(1, N_DEV):
            fx, fy, fc = (rel >> 2) & 1, (rel >> 1) & 1, rel & 1
            px, py, pc = x ^ fx, y ^ fy, c ^ fc
            k = N_BIG * N_CHIPS + rel - 1
            cp = _rcopy(sm_ref, sm_all.at[me], send_sems.at[k], recv_sems.at[k], (px, py, pc))
            cp.start()
            sends.append(cp)
            peers.append((k, 4 * px + 2 * py + pc, (px, py, pc)))
        for w in range(N_BIG):
            for jj in range(N_CHIPS):
                k = w * N_CHIPS + jj
                _rcopy(o_refs[w].at[jj], o_refs[w].at[jj], send_sems.at[k], recv_sems.at[k], sib).wait_recv()
        for k, pidx, dev in peers:
            _rcopy(sm_all.at[pidx], sm_all.at[pidx], send_sems.at[k], recv_sems.at[k], dev).wait_recv()
        for cp in sends:
            cp.wait_send()
        loc.wait()

    n_sem = N_BIG * N_CHIPS + N_DEV - 1
    out_shape = [SDS((N_CHIPS,) + g.shape[1:], BF) for g in gbf] + [SDS((N_DEV,) + small.shape, F32)]
    return _pcall(
        body, name="to_sibling", out_shape=out_shape, in_specs=[ANY] * (N_BIG + 1), out_specs=[ANY] * (N_BIG + 1),
        scratch_shapes=[pltpu.SemaphoreType.DMA((n_sem,)), pltpu.SemaphoreType.DMA((n_sem,)), pltpu.SemaphoreType.DMA],
    )(*gbf, small)


def _to_chips(pbf):
    def body(*refs):
        p_refs, o_refs = refs[:N_BIG], refs[N_BIG:2 * N_BIG]
        send_sems, recv_sems = refs[2 * N_BIG:]
        x, y, c, chips = _place()
        sends = []
        for w in range(N_BIG):
            for r, (px, py) in enumerate(chips):
                k = w * 3 + r
                cp = _rcopy(p_refs[w].at[2 * px + py], o_refs[w].at[r], send_sems.at[k], recv_sems.at[k], (px, py, c))
                cp.start()
                sends.append(cp)
        for w in range(N_BIG):
            for r, (px, py) in enumerate(chips):
                k = w * 3 + r
                _rcopy(o_refs[w].at[r], o_refs[w].at[r], send_sems.at[k], recv_sems.at[k], (px, py, c)).wait_recv()
        for cp in sends:
            cp.wait_send()

    return _pcall(
        body, name="to_chips", out_shape=[SDS((3,) + p.shape[1:], BF) for p in pbf],
        in_specs=[ANY] * N_BIG, out_specs=[ANY] * N_BIG,
        scratch_shapes=[pltpu.SemaphoreType.DMA((N_BIG * 3,)), pltpu.SemaphoreType.DMA((N_BIG * 3,))],
    )(*pbf)


def _swap_halves(fin):
    def body(*refs):
        f_refs, o_refs = refs[:N_BIG], refs[N_BIG:2 * N_BIG]
        send_sems, recv_sems, loc_sems = refs[2 * N_BIG:]
        x, y, c, _ = _place()
        sib = (x, y, 1 - c)
        cps = []
        for w in range(N_BIG):
            loc = pltpu.make_async_copy(f_refs[w], o_refs[w].at[c], loc_sems.at[w])
            loc.start()
            cp = _rcopy(f_refs[w], o_refs[w].at[c], send_sems.at[w], recv_sems.at[w], sib)
            cp.start()
            cps.append((loc, cp))
        for w in range(N_BIG):
            got = o_refs[w].at[1 - c]
            _rcopy(got, got, send_sems.at[w], recv_sems.at[w], sib).wait_recv()
        for loc, cp in cps:
            cp.wait_send()
            loc.wait()

    return _pcall(
        body, name="swap_halves", out_shape=[SDS((2,) + f.shape, F32) for f in fin],
        in_specs=[ANY] * N_BIG, out_specs=[ANY] * N_BIG,
        scratch_shapes=[pltpu.SemaphoreType.DMA((N_BIG,)), pltpu.SemaphoreType.DMA((N_BIG,)),
                        pltpu.SemaphoreType.DMA((N_BIG,))],
    )(*fin)


def _chip_sum(name, g32, from_sib, core):
    h = g32.shape[1]
    th = h // 2

    def body(core_ref, chip_ref, g_ref, s_ref, pbf_ref, own_ref):
        p = g_ref[0] + s_ref[0].astype(F32)
        pbf_ref[0] = p.astype(BF)

        @pl.when(pl.program_id(1) == chip_ref[0])
        def _():
            own_ref[...] = p

    chip = 2 * lax.axis_index("x") + lax.axis_index("y")
    grid_spec = pltpu.PrefetchScalarGridSpec(
        num_scalar_prefetch=2, grid=(h // th, N_CHIPS),
        in_specs=[pl.BlockSpec((1, th, D), lambda t, jj, core_ref, chip_ref: (2 * jj + core_ref[0], t, 0)),
                  pl.BlockSpec((1, th, D), lambda t, jj, core_ref, chip_ref: (jj, t, 0))],
        out_specs=[pl.BlockSpec((1, th, D), lambda t, jj, core_ref, chip_ref: (jj, t, 0)),
                   pl.BlockSpec((th, D), lambda t, jj, core_ref, chip_ref: (t, 0))],
    )
    return _pcall(
        body, name=name, grid_spec=grid_spec, out_shape=[SDS((N_CHIPS, h, D), BF), SDS((h, D), F32)],
        compiler_params=_params(("arbitrary", "arbitrary"), 32),
    )(jnp.reshape(core, (1,)).astype(jnp.int32), jnp.reshape(chip, (1,)).astype(jnp.int32), g32, from_sib)


def _final_sum(name, own, from_chips):
    h = own.shape[0]

    def body(o_ref, r_ref, f_ref):
        f_ref[...] = ((o_ref[...] + r_ref[0].astype(F32)) + r_ref[1].astype(F32)) + r_ref[2].astype(F32)

    return _pcall(body, name=name, out_shape=SDS((h, D), F32), compiler_params=_params(None, 32))(own, from_chips)


def _sum_devices(sm_all):
    def body(a_ref, o_ref):
        s = a_ref[0]
        for k in range(1, N_DEV):
            s = s + a_ref[k]
        o_ref[...] = s

    return _pcall(body, name="sum_devices", out_shape=SDS(sm_all.shape[1:], F32))(sm_all)


def _adamw(name, w, g, m, v, tr):
    rows, cols = w.shape

    def body(w_ref, g_ref, m_ref, v_ref, d_ref, nm_ref, nv_ref):
        gv = g_ref[...]
        nm = ADAM_B1 * m_ref[...] + (1.0 - ADAM_B1) * gv
        nv = ADAM_B2 * v_ref[...] + (1.0 - ADAM_B2) * (gv * gv)
        m_hat = nm / (1.0 - ADAM_B1 ** ADAM_STEP)
        v_hat = nv / (1.0 - ADAM_B2 ** ADAM_STEP)
        d_ref[...] = -ADAM_LR * (m_hat / (jnp.sqrt(v_hat) + ADAM_EPS) + ADAM_WD * w_ref[...])
        nm_ref[...] = nm
        nv_ref[...] = nv

    spec = pl.BlockSpec((tr, cols), lambda i: (i, 0))
    return _pcall(
        body, name=name, grid=(rows // tr,), in_specs=[spec] * 4, out_specs=[spec] * 3,
        out_shape=[SDS((rows, cols), F32)] * 3, compiler_params=_params(("parallel",), 32),
    )(w, g, m, v)


def _pad_rows(a, rows):
    return jnp.pad(a, ((0, rows - a.shape[0]), (0, 0)))


def _pack(parts, rows):
    flat = jnp.concatenate([p.reshape(-1) for p in parts])
    return jnp.pad(flat, (0, rows * 128 - flat.shape[0])).reshape(rows, 128)


def _unpack(buf, shapes):
    flat = buf.reshape(-1)
    out, o = [], 0
    for s in shapes:
        n = int(np.prod(s))
        out.append(flat[o:o + n].reshape(s))
        o += n
    return out


def _local_step(x, tgt, g1, w_int, cw, qg, kg, table, sinks, gco, gao, w_out, g2, w_upt, fw, fb, w_down):
    bucket = jnp.asarray(_bucket_table())
    cw8 = _pad_rows(cw, 8)
    fw8 = _pad_rows(fw, 8)
    bias = _band_bias(table, bucket)
    proj, u1 = _inproj(x, g1, w_int)
    y = _mix_fwd(proj, sinks, cw8, qg, kg, gco, gao, bias)
    h1, u2 = _outproj(y, w_out, x, g2)
    up = _ffn_up(u2, w_upt)
    a = _ffn_act(up, fw8, fb)
    dh2, dh2b, sq = _ffn_down(a, w_down, h1, tgt)

    g_down = _wgrad("wgrad_down", [a], dh2b)
    da = _ffn_down_bwd(dh2b, w_down)
    dug, duv, dfwg, dfwv, dfbg, dfbv = _ffn_act_bwd(up, da, fw8, fb)
    g_up = _wgrad("wgrad_up", [dug, duv], u2)
    dh1, dh1b, dg2 = _norm_matmul_bwd("ffn_up_bwd", [dug, duv], w_upt, [0, DFF], h1, g2, dh2, True)
    g_out = _wgrad("wgrad_out", [y], dh1b)
    dy = _out_bwd(dh1b, w_out)
    dproj, dcw8, dqg, dkg, dgco, dgao, dsink, dbias = _mix_bwd(proj, dy, sinks, cw8, qg, kg, gco, gao, bias)
    dtable = _band_bias_bwd(dbias, bucket).T
    g_in = _wgrad("wgrad_in", [dproj], u1)
    dx, dg1 = _norm_matmul_bwd("in_bwd", [dproj], w_int, [0], x, g1, dh1, False)

    dfw = jnp.concatenate([dfwg[0:3], dfwv[0:3]], axis=1)
    dfb = jnp.concatenate([dfbg, dfbv], axis=1)
    small = [dg1, dcw8[0:3], dqg, dkg, dtable, dsink[:, 0].reshape(1, NH), dgco, dgao, dg2, dfw, dfb]
    return sq, dx, [g_in, g_out, g_up, g_down], small


SMALL_FULL = [(1, D), (3, CW), (1, HD), (1, HD), (NBUCKET, NH), (1, NH), (1, CW), (1, AW), (1, D), (3, 2 * DFF), (1, 2 * DFF)]
SMALL_ROWS_FULL = 216
SMALL_LOCAL = [(1, D), (3, CW // N_CHIPS), (1, HD), (1, HD), (NBUCKET, NH), (1, NH), (1, CW), (1, AW), (1, D),
               (3, 2 * DFF // N_CHIPS), (1, 2 * DFF)]
SMALL_ROWS_LOCAL = 112


def kernel(x, norm_mix_g, w_in, conv_w, q_norm_g, k_norm_g, rel_bias_table, sinks, out_norm_conv_g, out_norm_attn_g, w_out, norm_ffn_g, w_up, ffn_conv_w, ffn_conv_b, w_down, loss_target, m_norm_mix_g, m_w_in, m_conv_w, m_q_norm_g, m_k_norm_g, m_rel_bias_table, m_sinks, m_out_norm_conv_g, m_out_norm_attn_g, m_w_out, m_norm_ffn_g, m_w_up, m_ffn_conv_w, m_ffn_conv_b, m_w_down, v_norm_mix_g, v_w_in, v_conv_w, v_q_norm_g, v_k_norm_g, v_rel_bias_table, v_sinks, v_out_norm_conv_g, v_out_norm_attn_g, v_w_out, v_norm_ffn_g, v_w_up, v_ffn_conv_w, v_ffn_conv_b, v_w_down):
    chip = 2 * lax.axis_index("x") + lax.axis_index("y")
    core = lax.axis_index("c")

    shards = [w_in[0].T.astype(BF), w_out[0].astype(BF), w_up[0].T.astype(BF), w_down[0].astype(BF)]
    smalls = [_pad_rows(conv_w[0], 8), _pad_rows(ffn_conv_w[0], 8)]
    w_int, w_out_f, w_upt, w_down_f, cw_all, fw_all = _gather_weights(shards, smalls)
    cw = jnp.transpose(cw_all[:, 0:3, :], (1, 0, 2)).reshape(3, CW)
    fw = jnp.transpose(fw_all[:, 0:3, :], (1, 0, 2)).reshape(3, 2 * DFF)

    sq, dx, big, small = _local_step(
        x[0], loss_target[0], norm_mix_g, w_int, cw, q_norm_g, k_norm_g, rel_bias_table, sinks, out_norm_conv_g,
        out_norm_attn_g, w_out_f, norm_ffn_g, w_upt, fw, ffn_conv_b, w_down_f)
    loss = lax.psum(0.5 * jnp.sum(sq) / D, ("x", "y", "c"))

    halves = [R // 2 for _, R in BIG]
    g32 = [g[0].reshape(N_DEV, h, D) for g, h in zip(big, halves)]
    gbf = [g[1].reshape(N_DEV, h, D) for g, h in zip(big, halves)]
    *from_sib, sm_all = _to_sibling(gbf, _pack(small, SMALL_ROWS_FULL))
    sums = [_chip_sum("chip_sum_" + n, g, s, core) for (n, _), g, s in zip(BIG, g32, from_sib)]
    from_chips = _to_chips([p for p, _ in sums])
    fin = [_final_sum("final_sum_" + n, own, r) for (n, _), (_, own), r in zip(BIG, sums, from_chips)]
    full = [f.reshape(2 * h, D) for f, h in zip(_swap_halves(fin), halves)]
    g_w_in, g_w_out, g_w_up, g_w_down = full[0].T, full[1], full[2].T, full[3]
    sg = _unpack(_sum_devices(sm_all), SMALL_FULL)
    sg[1] = lax.dynamic_slice_in_dim(sg[1], chip * (CW // N_CHIPS), CW // N_CHIPS, axis=1)
    sg[9] = lax.dynamic_slice_in_dim(sg[9], chip * (2 * DFF // N_CHIPS), 2 * DFF // N_CHIPS, axis=1)

    d_in, nm_in, nv_in = _adamw("adamw_w_in", w_in[0], g_w_in, m_w_in[0], v_w_in[0], 256)
    d_out, nm_out, nv_out = _adamw("adamw_w_out", w_out[0], g_w_out, m_w_out[0], v_w_out[0], 256)
    d_up, nm_up, nv_up = _adamw("adamw_w_up", w_up[0], g_w_up, m_w_up[0], v_w_up[0], 256)
    d_down, nm_down, nv_down = _adamw("adamw_w_down", w_down[0], g_w_down, m_w_down[0], v_w_down[0], 352)
    sw = [norm_mix_g, conv_w[0], q_norm_g, k_norm_g, rel_bias_table, sinks, out_norm_conv_g, out_norm_attn_g,
          norm_ffn_g, ffn_conv_w[0], ffn_conv_b]
    smm = [m_norm_mix_g, m_conv_w[0], m_q_norm_g, m_k_norm_g, m_rel_bias_table, m_sinks, m_out_norm_conv_g,
           m_out_norm_attn_g, m_norm_ffn_g, m_ffn_conv_w[0], m_ffn_conv_b]
    smv = [v_norm_mix_g, v_conv_w[0], v_q_norm_g, v_k_norm_g, v_rel_bias_table, v_sinks, v_out_norm_conv_g,
           v_out_norm_attn_g, v_norm_ffn_g, v_ffn_conv_w[0], v_ffn_conv_b]
    packed = [_pack(p, SMALL_ROWS_LOCAL) for p in (sw, sg, smm, smv)]
    sd, snm, snv = [_unpack(b, SMALL_LOCAL) for b in _adamw("adamw_small", *packed, SMALL_ROWS_LOCAL)]

    def order(s, b_in, b_out, b_up, b_down):
        return (s[0], b_in[None], s[1][None], s[2], s[3], s[4], s[5], s[6], s[7], b_out[None], s[8], b_up[None],
                s[9][None], s[10], b_down[None])

    return (loss, dx[None],
            *order(sg, g_w_in, g_w_out, g_w_up, g_w_down),
            *order(sd, d_in, d_out, d_up, d_down),
            *order(snm, nm_in, nm_out, nm_up, nm_down),
            *order(snv, nv_in, nv_out, nv_up, nv_down))
```

```python
import functools
import math

import numpy as np

import jax
import jax.numpy as jnp
from jax import lax
from jax.experimental import pallas as pl
from jax.experimental.pallas import tpu as pltpu

F32 = jnp.float32
BF = jnp.bfloat16
SDS = jax.ShapeDtypeStruct

T = 2048
D = 1024
CW = 512
AW = 512
HD = 64
NH = 8
NKV = 2
GQ = 4
INW = 2304
DFF = 2816
BLK = 128
NB = T // BLK
NBUCKET = 32
EPS = 1e-6
NEG_INF = -1e30
N_CHIPS = 4
N_DEV = 8

ADAM_LR = 0.001
ADAM_B1 = 0.9
ADAM_B2 = 0.999
ADAM_EPS = 1e-08
ADAM_WD = 0.01
ADAM_STEP = 10

MIB = 1024 * 1024
MESH = pl.DeviceIdType.MESH

_pcall = pl.pallas_call


def _params(sem=None, vmem_mib=None):
    kw = {}
    if sem is not None:
        kw["dimension_semantics"] = sem
    if vmem_mib is not None:
        kw["vmem_limit_bytes"] = vmem_mib * MIB
    return pltpu.CompilerParams(**kw)


def _dot(a, b, ca, cb):
    return lax.dot_general(a, b, (((ca,), (cb,)), ((), ())), preferred_element_type=F32)


def _rms_bwd(dy, x, r, g):
    dg = jnp.sum(dy * (x * r), axis=0, keepdims=True)
    dgx = dy * g
    dx = r * dgx - x * (r * r * r) * jnp.mean(x * dgx, axis=-1, keepdims=True)
    return dx, dg


def _inproj(x, g1, w_int):
    tm = 256

    def body(x_ref, g_ref, w_ref, proj_ref, u_ref):
        xf = x_ref[...]
        r = lax.rsqrt(jnp.mean(xf * xf, axis=-1, keepdims=True) + EPS)
        u = (xf * r * g_ref[...]).astype(BF)
        u_ref[...] = u
        proj_ref[...] = _dot(u, w_ref[...], 1, 1)

    return _pcall(
        body, name="inproj", grid=(T // tm,),
        in_specs=[pl.BlockSpec((tm, D), lambda i: (i, 0)), pl.BlockSpec((1, D), lambda i: (0, 0)),
                  pl.BlockSpec((INW, D), lambda i: (0, 0))],
        out_specs=[pl.BlockSpec((tm, INW), lambda i: (i, 0)), pl.BlockSpec((tm, D), lambda i: (i, 0))],
        out_shape=[SDS((T, INW), F32), SDS((T, D), BF)],
        compiler_params=_params(("parallel",), 40),
    )(x, g1, w_int)


def _outproj(y, w_out, x, g2):
    tm = 256

    def body(y_ref, w_ref, x_ref, g_ref, h1_ref, u2_ref):
        h1 = x_ref[...] + _dot(y_ref[...], w_ref[...], 1, 0)
        h1_ref[...] = h1
        r = lax.rsqrt(jnp.mean(h1 * h1, axis=-1, keepdims=True) + EPS)
        u2_ref[...] = (h1 * r * g_ref[...]).astype(BF)

    return _pcall(
        body, name="outproj", grid=(T // tm,),
        in_specs=[pl.BlockSpec((tm, D), lambda i: (i, 0)), pl.BlockSpec((D, D), lambda i: (0, 0)),
                  pl.BlockSpec((tm, D), lambda i: (i, 0)), pl.BlockSpec((1, D), lambda i: (0, 0))],
        out_specs=[pl.BlockSpec((tm, D), lambda i: (i, 0)), pl.BlockSpec((tm, D), lambda i: (i, 0))],
        out_shape=[SDS((T, D), F32), SDS((T, D), BF)],
        compiler_params=_params(("parallel",), 32),
    )(y, w_out, x, g2)


def _ffn_up(u2, w_upt):
    tm, tn = 1024, 512

    def body(u_ref, w_ref, o_ref):
        o_ref[...] = _dot(u_ref[...], w_ref[...], 1, 1)

    return _pcall(
        body, name="ffn_up", grid=(T // tm, 2 * DFF // tn),
        in_specs=[pl.BlockSpec((tm, D), lambda i, j: (i, 0)), pl.BlockSpec((tn, D), lambda i, j: (j, 0))],
        out_specs=pl.BlockSpec((tm, tn), lambda i, j: (i, j)),
        out_shape=SDS((T, 2 * DFF), F32),
        compiler_params=_params(("parallel", "parallel"), 32),
    )(u2, w_upt)


def _ffn_down(a, w_down, h1, tgt):
    tm = 256

    def body(a_ref, w_ref, h1_ref, t_ref, dh_ref, dhb_ref, l_ref):
        @pl.when(pl.program_id(0) == 0)
        def _():
            l_ref[...] = jnp.zeros_like(l_ref)

        h2 = h1_ref[...] + _dot(a_ref[...], w_ref[...], 1, 0)
        e = h2 - t_ref[...]
        dh = e * (1.0 / D)
        dh_ref[...] = dh
        dhb_ref[...] = dh.astype(BF)
        e2 = jnp.sum((e * e).reshape(tm // 8, 8, D), axis=0)
        acc = e2[:, 0:128]
        for k in range(1, D // 128):
            acc = acc + e2[:, k * 128:(k + 1) * 128]
        l_ref[...] += acc

    return _pcall(
        body, name="ffn_down", grid=(T // tm,),
        in_specs=[pl.BlockSpec((tm, DFF), lambda i: (i, 0)), pl.BlockSpec((DFF, D), lambda i: (0, 0)),
                  pl.BlockSpec((tm, D), lambda i: (i, 0)), pl.BlockSpec((tm, D), lambda i: (i, 0))],
        out_specs=[pl.BlockSpec((tm, D), lambda i: (i, 0)), pl.BlockSpec((tm, D), lambda i: (i, 0)),
                   pl.BlockSpec((8, 128), lambda i: (0, 0))],
        out_shape=[SDS((T, D), F32), SDS((T, D), BF), SDS((8, 128), F32)],
        compiler_params=_params(("arbitrary",), 40),
    )(a, w_down, h1, tgt)


def _bucket_table():
    q = np.arange(BLK, dtype=np.int32)[:, None]
    j = np.arange(2 * BLK, dtype=np.int32)[None, :]
    n = np.maximum(q + BLK - j, 0)
    nf = np.maximum(n, 1).astype(np.float32)
    max_exact = NBUCKET // 2
    large = max_exact + (np.log(nf / np.float32(max_exact)) / np.float32(math.log(BLK / max_exact))
                         * np.float32(NBUCKET - max_exact)).astype(np.int32)
    large = np.minimum(large, NBUCKET - 1)
    return np.where(n < max_exact, n, large).astype(np.int32)


def _band_bias(table, bucket):
    def body(tab_ref, bk_ref, o_ref):
        bk = bk_ref[...]
        eq = [bk == b for b in range(NBUCKET)]
        for h in range(NH):
            acc = jnp.zeros((BLK, 2 * BLK), F32)
            for b in range(NBUCKET):
                acc = jnp.where(eq[b], tab_ref[b, h], acc)
            o_ref[h * BLK:(h + 1) * BLK, :] = acc

    return _pcall(
        body, name="band_bias", out_shape=SDS((NH * BLK, 2 * BLK), F32),
        in_specs=[pl.BlockSpec(memory_space=pltpu.SMEM), pl.BlockSpec(memory_space=pltpu.VMEM)],
        out_specs=pl.BlockSpec(memory_space=pltpu.VMEM),
    )(table, bucket)


def _band_bias_bwd(dbias, bucket):
    def body(db_ref, bk_ref, o_ref):
        bk = bk_ref[...]
        for b in range(NBUCKET):
            m = bk == b
            for h in range(NH):
                v = jnp.where(m, db_ref[h * BLK:(h + 1) * BLK, :], 0.0)
                s = jnp.sum(jnp.sum(v, axis=1, keepdims=True), axis=0, keepdims=True)
                o_ref[h:h + 1, b:b + 1] = s

    return _pcall(body, name="band_bias_bwd", out_shape=SDS((NH, NBUCKET), F32))(dbias, bucket)


def _mix_forward(P, zc8, zh8, pkv, first, cw, qg, kg, gco, gao, sink_ref, bias_ref):
    gate_b = P[:, 0:CW]
    gate_c = P[:, CW:2 * CW]
    hc = P[:, 2 * CW:3 * CW]
    z = gate_c * hc
    keep = jnp.where(first, 0.0, 1.0)
    zp = zc8 * zh8 * keep
    p1 = zp[7:8, :]
    p2 = zp[6:7, :]
    row = lax.broadcasted_iota(jnp.int32, (BLK, 1), 0)
    z1 = jnp.where(row == 0, p1, pltpu.roll(z, 1, 0))
    z2 = jnp.where(row == 0, p2, jnp.where(row == 1, p1, pltpu.roll(z, 2, 0)))
    cz = cw[0:1, :] * z2 + cw[1:2, :] * z1 + cw[2:3, :] * z
    y_conv = gate_b * cz

    scale = HD ** -0.5
    qi = lax.broadcasted_iota(jnp.int32, (GQ * BLK, 2 * BLK), 0) & (BLK - 1)
    kj = lax.broadcasted_iota(jnp.int32, (GQ * BLK, 2 * BLK), 1)
    dd = qi + BLK - kj
    first_key = jnp.where(first, BLK, 0)
    valid = (dd >= 0) & (dd < BLK) & (kj >= first_key)

    q0 = 3 * CW
    k0 = q0 + AW
    v0 = k0 + NKV * HD
    heads = []
    outs = []
    for kv in range(NKV):
        kb_raw = jnp.concatenate([pkv[:, kv * HD:(kv + 1) * HD], P[:, k0 + kv * HD:k0 + (kv + 1) * HD]], axis=0)
        rk = lax.rsqrt(jnp.mean(kb_raw * kb_raw, axis=-1, keepdims=True) + EPS)
        kb = (kb_raw * rk * kg).astype(BF)
        vb = jnp.concatenate([pkv[:, NKV * HD + kv * HD:NKV * HD + (kv + 1) * HD],
                              P[:, v0 + kv * HD:v0 + (kv + 1) * HD]], axis=0).astype(BF)
        q_raw, rq, qn = [], [], []
        for g in range(GQ):
            h = kv * GQ + g
            qh = P[:, q0 + h * HD:q0 + (h + 1) * HD]
            r = lax.rsqrt(jnp.mean(qh * qh, axis=-1, keepdims=True) + EPS)
            q_raw.append(qh)
            rq.append(r)
            qn.append(qh * r * qg)
        Q = jnp.concatenate(qn, axis=0).astype(BF)
        S = _dot(Q, kb, 1, 1) * scale + bias_ref[kv * GQ * BLK:(kv + 1) * GQ * BLK, :]
        S = jnp.where(valid, S, NEG_INF)
        sink = jnp.concatenate([jnp.full((BLK, 1), sink_ref[0, kv * GQ + g], F32) for g in range(GQ)], axis=0)
        m = jnp.maximum(jnp.max(S, axis=-1, keepdims=True), sink)
        p = jnp.exp(S - m)
        es = jnp.exp(sink - m)
        denom = jnp.sum(p, axis=-1, keepdims=True) + es
        probs = p / denom
        O = _dot(probs.astype(BF), vb, 1, 0)
        heads.append(dict(kb_raw=kb_raw, rk=rk, kb=kb, vb=vb, q_raw=q_raw, rq=rq, Q=Q, probs=probs,
                          psink=es / denom, O=O))
        outs += [O[g * BLK:(g + 1) * BLK, :] for g in range(GQ)]
    y_attn = jnp.concatenate(outs, axis=1)

    rc = lax.rsqrt(jnp.mean(y_conv * y_conv, axis=-1, keepdims=True) + EPS)
    ra = lax.rsqrt(jnp.mean(y_attn * y_attn, axis=-1, keepdims=True) + EPS)
    y = jnp.concatenate([y_conv * rc * gco, y_attn * ra * gao], axis=1)
    return dict(gate_b=gate_b, gate_c=gate_c, hc=hc, z=z, z1=z1, z2=z2, cz=cz, y_conv=y_conv, y_attn=y_attn,
                rc=rc, ra=ra, heads=heads, y=y, row=row, scale=scale)


def _mix_in_specs(blk):
    return [
        pl.BlockSpec(memory_space=pltpu.SMEM),
        pl.BlockSpec((BLK, INW), lambda s: (blk(s), 0)),
        pl.BlockSpec((8, CW), lambda s: (jnp.maximum(blk(s) * (BLK // 8) - 1, 0), 1)),
        pl.BlockSpec((8, CW), lambda s: (jnp.maximum(blk(s) * (BLK // 8) - 1, 0), 2)),
        pl.BlockSpec((BLK, 2 * NKV * HD), lambda s: (jnp.maximum(blk(s) - 1, 0), (3 * CW + AW) // (2 * NKV * HD))),
    ]


def _mix_param_specs():
    return [
        pl.BlockSpec((8, CW), lambda s: (0, 0)),
        pl.BlockSpec((1, HD), lambda s: (0, 0)),
        pl.BlockSpec((1, HD), lambda s: (0, 0)),
        pl.BlockSpec((1, CW), lambda s: (0, 0)),
        pl.BlockSpec((1, AW), lambda s: (0, 0)),
        pl.BlockSpec((NH * BLK, 2 * BLK), lambda s: (0, 0)),
    ]


def _mix_fwd(proj, sinks, cw8, qg, kg, gco, gao, bias):
    def body(sink_ref, p_ref, zc_ref, zh_ref, pkv_ref, cw_ref, qg_ref, kg_ref, gco_ref, gao_ref, bias_ref, y_ref):
        first = pl.program_id(0) == 0
        f = _mix_forward(p_ref[...], zc_ref[...], zh_ref[...], pkv_ref[...], first, cw_ref[...], qg_ref[...],
                         kg_ref[...], gco_ref[...], gao_ref[...], sink_ref, bias_ref)
        y_ref[...] = f["y"].astype(BF)

    return _pcall(
        body, name="mix_fwd", grid=(NB,),
        in_specs=_mix_in_specs(lambda s: s) + _mix_param_specs(),
        out_specs=pl.BlockSpec((BLK, D), lambda s: (s, 0)),
        out_shape=SDS((T, D), BF),
        compiler_params=_params(("parallel",), 32),
    )(sinks, proj, proj, proj, proj, cw8, qg, kg, gco, gao, bias)


def _mix_bwd(proj, dy, sinks, cw8, qg, kg, gco, gao, bias):
    def blk(s):
        return NB - 1 - s

    def body(sink_ref, p_ref, zc_ref, zh_ref, pkv_ref, dy_ref, cw_ref, qg_ref, kg_ref, gco_ref, gao_ref, bias_ref,
             dproj_ref, dcw_ref, dqg_ref, dkg_ref, dgco_ref, dgao_ref, dsink_ref, dbias_ref,
             ndcz_ref, dkc_ref, dvc_ref):
        s = pl.program_id(0)
        first = s == NB - 1

        @pl.when(s == 0)
        def _():
            for r in (dcw_ref, dqg_ref, dkg_ref, dgco_ref, dgao_ref, dsink_ref, dbias_ref, ndcz_ref, dkc_ref, dvc_ref):
                r[...] = jnp.zeros_like(r)

        cw = cw_ref[...]
        qg_v, kg_v, gco_v, gao_v = qg_ref[...], kg_ref[...], gco_ref[...], gao_ref[...]
        f = _mix_forward(p_ref[...], zc_ref[...], zh_ref[...], pkv_ref[...], first, cw, qg_v, kg_v, gco_v, gao_v,
                         sink_ref, bias_ref)
        dy = dy_ref[...]
        dyc, dgco = _rms_bwd(dy[:, 0:CW], f["y_conv"], f["rc"], gco_v)
        dya, dgao = _rms_bwd(dy[:, CW:CW + AW], f["y_attn"], f["ra"], gao_v)
        dgco_ref[...] += dgco
        dgao_ref[...] += dgao

        row = f["row"]
        dgate_b = dyc * f["cz"]
        dcz = dyc * f["gate_b"]
        dcw_ref[0:1, :] += jnp.sum(dcz * f["z2"], axis=0, keepdims=True)
        dcw_ref[1:2, :] += jnp.sum(dcz * f["z1"], axis=0, keepdims=True)
        dcw_ref[2:3, :] += jnp.sum(dcz * f["z"], axis=0, keepdims=True)
        nxt = ndcz_ref[...]
        n0 = nxt[0:1, :]
        n1 = nxt[1:2, :]
        d1 = jnp.where(row == BLK - 1, n0, pltpu.roll(dcz, BLK - 1, 0))
        d2 = jnp.where(row == BLK - 1, n1, jnp.where(row == BLK - 2, n0, pltpu.roll(dcz, BLK - 2, 0)))
        dz = cw[2:3, :] * dcz + cw[1:2, :] * d1 + cw[0:1, :] * d2
        ndcz_ref[...] = dcz[0:8, :]
        dproj_ref[:, 0:CW] = dgate_b.astype(BF)
        dproj_ref[:, CW:2 * CW] = (dz * f["hc"]).astype(BF)
        dproj_ref[:, 2 * CW:3 * CW] = (dz * f["gate_c"]).astype(BF)

        scale = f["scale"]
        dq_cols, dk_cols, dv_cols = [], [], []
        for kv in range(NKV):
            hd = f["heads"][kv]
            dO = jnp.concatenate([dya[:, (kv * GQ + g) * HD:(kv * GQ + g + 1) * HD] for g in range(GQ)], axis=0)
            delta = jnp.sum(dO * hd["O"], axis=-1, keepdims=True)
            dOb = dO.astype(BF)
            dP = _dot(dOb, hd["vb"], 1, 1)
            dS = hd["probs"] * (dP - delta)
            dsk = hd["psink"] * delta
            for g in range(GQ):
                h = kv * GQ + g
                tot = jnp.sum(dsk[g * BLK:(g + 1) * BLK, :], axis=0, keepdims=True)
                dsink_ref[h:h + 1, :] -= jnp.broadcast_to(tot, (1, 128))
            dbias_ref[kv * GQ * BLK:(kv + 1) * GQ * BLK, :] += dS
            dSs = (dS * scale).astype(BF)
            dQ = _dot(dSs, hd["kb"], 1, 0)
            dKb = _dot(dSs, hd["Q"], 0, 0)
            dVb = _dot(hd["probs"].astype(BF), dOb, 0, 0)
            dkn = dKb[BLK:, :] + dkc_ref[:, kv * HD:(kv + 1) * HD]
            dvn = dVb[BLK:, :] + dvc_ref[:, kv * HD:(kv + 1) * HD]
            dkc_ref[:, kv * HD:(kv + 1) * HD] = dKb[:BLK, :]
            dvc_ref[:, kv * HD:(kv + 1) * HD] = dVb[:BLK, :]
            dk_raw, dkg = _rms_bwd(dkn, hd["kb_raw"][BLK:, :], hd["rk"][BLK:, :], kg_v)
            dkg_ref[...] += dkg
            dk_cols.append(dk_raw)
            dv_cols.append(dvn)
            for g in range(GQ):
                dq_raw, dqg = _rms_bwd(dQ[g * BLK:(g + 1) * BLK, :], hd["q_raw"][g], hd["rq"][g], qg_v)
                dqg_ref[...] += dqg
                dq_cols.append(dq_raw)
        dproj_ref[:, 3 * CW:INW] = jnp.concatenate(dq_cols + dk_cols + dv_cols, axis=1).astype(BF)

    small = lambda r, c: pl.BlockSpec((r, c), lambda s: (0, 0))
    return _pcall(
        body, name="mix_bwd", grid=(NB,),
        in_specs=_mix_in_specs(blk) + [pl.BlockSpec((BLK, D), lambda s: (blk(s), 0))] + _mix_param_specs(),
        out_specs=[pl.BlockSpec((BLK, INW), lambda s: (blk(s), 0)), small(8, CW), small(1, HD), small(1, HD),
                   small(1, CW), small(1, AW), small(NH, 128), small(NH * BLK, 2 * BLK)],
        out_shape=[SDS((T, INW), BF), SDS((8, CW), F32), SDS((1, HD), F32), SDS((1, HD), F32), SDS((1, CW), F32),
                   SDS((1, AW), F32), SDS((NH, 128), F32), SDS((NH * BLK, 2 * BLK), F32)],
        scratch_shapes=[pltpu.VMEM((8, CW), F32), pltpu.VMEM((BLK, NKV * HD), F32), pltpu.VMEM((BLK, NKV * HD), F32)],
        compiler_params=_params(("arbitrary",), 40),
    )(sinks, proj, proj, proj, proj, dy, cw8, qg, kg, gco, gao, bias)


FT = 256
NFT = DFF // FT


def _shift_down(u, row, k):
    return jnp.where(row >= k, pltpu.roll(u, k, 0), 0.0)


def _shift_up(u, row, k):
    return jnp.where(row < T - k, pltpu.roll(u, T - k, 0), 0.0)


def _ffn_act_specs():
    return [
        pl.BlockSpec((T, FT), lambda j: (0, j)), pl.BlockSpec((T, FT), lambda j: (0, NFT + j)),
        pl.BlockSpec((8, FT), lambda j: (0, j)), pl.BlockSpec((8, FT), lambda j: (0, NFT + j)),
        pl.BlockSpec((1, FT), lambda j: (0, j)), pl.BlockSpec((1, FT), lambda j: (0, NFT + j)),
    ]


def _ffn_act(up, fw8, fb):
    def body(ug_ref, uv_ref, wg_ref, wv_ref, bg_ref, bv_ref, a_ref):
        row = lax.broadcasted_iota(jnp.int32, (T, 1), 0)

        def conv(u, w, b):
            return w[0:1, :] * _shift_down(u, row, 2) + w[1:2, :] * _shift_down(u, row, 1) + w[2:3, :] * u + b

        gp = conv(ug_ref[...], wg_ref[...], bg_ref[...])
        vp = conv(uv_ref[...], wv_ref[...], bv_ref[...])
        a_ref[...] = (gp * jax.nn.sigmoid(gp) * vp).astype(BF)

    return _pcall(
        body, name="ffn_act", grid=(NFT,), in_specs=_ffn_act_specs(),
        out_specs=pl.BlockSpec((T, FT), lambda j: (0, j)), out_shape=SDS((T, DFF), BF),
        compiler_params=_params(("parallel",), 48),
    )(up, up, fw8, fw8, fb, fb)


def _ffn_act_bwd(up, da, fw8, fb):
    def body(ug_ref, uv_ref, wg_ref, wv_ref, bg_ref, bv_ref, da_ref,
             dug_ref, duv_ref, dwg_ref, dwv_ref, dbg_ref, dbv_ref):
        row = lax.broadcasted_iota(jnp.int32, (T, 1), 0)
        wg, wv = wg_ref[...], wv_ref[...]
        ug, uv = ug_ref[...], uv_ref[...]
        ug1, ug2 = _shift_down(ug, row, 1), _shift_down(ug, row, 2)
        uv1, uv2 = _shift_down(uv, row, 1), _shift_down(uv, row, 2)
        gp = wg[0:1, :] * ug2 + wg[1:2, :] * ug1 + wg[2:3, :] * ug + bg_ref[...]
        vp = wv[0:1, :] * uv2 + wv[1:2, :] * uv1 + wv[2:3, :] * uv + bv_ref[...]
        sig = jax.nn.sigmoid(gp)
        da = da_ref[...]
        dvp = da * (gp * sig)
        dgp = da * vp * (sig * (1.0 + gp * (1.0 - sig)))

        def finish(dp, u, u1, u2, w, du_ref, dw_ref, db_ref):
            db_ref[...] = jnp.sum(dp, axis=0, keepdims=True)
            dw_ref[...] = jnp.zeros_like(dw_ref)
            dw_ref[0:1, :] = jnp.sum(dp * u2, axis=0, keepdims=True)
            dw_ref[1:2, :] = jnp.sum(dp * u1, axis=0, keepdims=True)
            dw_ref[2:3, :] = jnp.sum(dp * u, axis=0, keepdims=True)
            du = w[2:3, :] * dp + w[1:2, :] * _shift_up(dp, row, 1) + w[0:1, :] * _shift_up(dp, row, 2)
            du_ref[...] = du.astype(BF)

        finish(dgp, ug, ug1, ug2, wg, dug_ref, dwg_ref, dbg_ref)
        finish(dvp, uv, uv1, uv2, wv, duv_ref, dwv_ref, dbv_ref)

    col = lambda r: pl.BlockSpec((r, FT), lambda j: (0, j))
    return _pcall(
        body, name="ffn_act_bwd", grid=(NFT,),
        in_specs=_ffn_act_specs() + [pl.BlockSpec((T, FT), lambda j: (0, j))],
        out_specs=[col(T), col(T), col(8), col(8), col(1), col(1)],
        out_shape=[SDS((T, DFF), BF), SDS((T, DFF), BF), SDS((8, DFF), F32), SDS((8, DFF), F32),
                   SDS((1, DFF), F32), SDS((1, DFF), F32)],
        compiler_params=_params(("parallel",), 56),
    )(up, up, fw8, fw8, fb, fb, da)


def _ffn_down_bwd(dh2b, w_down):
    tm = 256

    def body(d_ref, w_ref, o_ref):
        o_ref[...] = _dot(d_ref[...], w_ref[...], 1, 1)

    return _pcall(
        body, name="ffn_down_bwd", grid=(T // tm,),
        in_specs=[pl.BlockSpec((tm, D), lambda i: (i, 0)), pl.BlockSpec((DFF, D), lambda i: (0, 0))],
        out_specs=pl.BlockSpec((tm, DFF), lambda i: (i, 0)), out_shape=SDS((T, DFF), F32),
        compiler_params=_params(("parallel",), 40),
    )(dh2b, w_down)


def _norm_matmul_bwd(name, a_list, w_t, k_offsets, xin, g, dres, want_bf16):
    tm = 256
    ks = [a.shape[1] for a in a_list]
    n_a = len(a_list)

    def body(*refs):
        a_refs = refs[:n_a]
        w_ref, x_ref, g_ref, r_ref = refs[n_a:n_a + 4]
        outs = refs[n_a + 4:]
        dx_ref, dg_ref = outs[0], outs[-1]

        @pl.when(pl.program_id(0) == 0)
        def _():
            dg_ref[...] = jnp.zeros_like(dg_ref)

        du = _dot(a_refs[0][...], w_ref[k_offsets[0]:k_offsets[0] + ks[0], :], 1, 0)
        for k in range(1, n_a):
            du = du + _dot(a_refs[k][...], w_ref[k_offsets[k]:k_offsets[k] + ks[k], :], 1, 0)
        x = x_ref[...]
        r = lax.rsqrt(jnp.mean(x * x, axis=-1, keepdims=True) + EPS)
        dx, dg = _rms_bwd(du, x, r, g_ref[...])
        dx = r_ref[...] + dx
        dx_ref[...] = dx
        if want_bf16:
            outs[1][...] = dx.astype(BF)
        dg_ref[...] += dg

    tile = lambda c: pl.BlockSpec((tm, c), lambda i: (i, 0))
    out_specs = [tile(D)] + ([tile(D)] if want_bf16 else []) + [pl.BlockSpec((1, D), lambda i: (0, 0))]
    out_shape = [SDS((T, D), F32)] + ([SDS((T, D), BF)] if want_bf16 else []) + [SDS((1, D), F32)]
    return _pcall(
        body, name=name, grid=(T // tm,),
        in_specs=[tile(k) for k in ks] + [pl.BlockSpec(w_t.shape, lambda i: (0, 0)), tile(D),
                                           pl.BlockSpec((1, D), lambda i: (0, 0)), tile(D)],
        out_specs=out_specs, out_shape=out_shape,
        compiler_params=_params(("arbitrary",), 56),
    )(*a_list, w_t, xin, g, dres)


def _out_bwd(dh1b, w_out):
    tm = 256

    def body(d_ref, w_ref, o_ref):
        o_ref[...] = _dot(d_ref[...], w_ref[...], 1, 1)

    return _pcall(
        body, name="out_bwd", grid=(T // tm,),
        in_specs=[pl.BlockSpec((tm, D), lambda i: (i, 0)), pl.BlockSpec((D, D), lambda i: (0, 0))],
        out_specs=pl.BlockSpec((tm, D), lambda i: (i, 0)), out_shape=SDS((T, D), F32),
        compiler_params=_params(("parallel",), 32),
    )(dh1b, w_out)


def _wgrad(name, a_list, b):
    tm = 256
    steps = [a.shape[1] // tm for a in a_list]
    starts = [sum(steps[:k]) for k in range(len(a_list))]
    n_a = len(a_list)

    def body(*refs):
        a_refs, b_ref, o32_ref, obf_ref = refs[:n_a], refs[n_a], refs[n_a + 1], refs[n_a + 2]
        i = pl.program_id(0)
        for k in range(n_a):
            @pl.when((i >= starts[k]) & (i < starts[k] + steps[k]))
            def _(k=k):
                r = _dot(a_refs[k][...], b_ref[...], 0, 0)
                o32_ref[...] = r
                obf_ref[...] = r.astype(BF)

    def a_spec(k):
        return pl.BlockSpec((T, tm), lambda i: (0, jnp.clip(i - starts[k], 0, steps[k] - 1)))

    m_total = tm * sum(steps)
    return _pcall(
        body, name=name, grid=(sum(steps),),
        in_specs=[a_spec(k) for k in range(n_a)] + [pl.BlockSpec((T, D), lambda i: (0, 0))],
        out_specs=[pl.BlockSpec((tm, D), lambda i: (i, 0)), pl.BlockSpec((tm, D), lambda i: (i, 0))],
        out_shape=[SDS((m_total, D), F32), SDS((m_total, D), BF)],
        compiler_params=_params(("parallel",), 40),
    )(*a_list, b)


ANY = pl.BlockSpec(memory_space=pl.ANY)
BIG = (("w_in", INW // N_CHIPS), ("w_out", D // N_CHIPS), ("w_up", 2 * DFF // N_CHIPS), ("w_down", DFF // N_CHIPS))
N_BIG = len(BIG)


def _place():
    x, y, c = lax.axis_index("x"), lax.axis_index("y"), lax.axis_index("c")
    return x, y, c, [(1 - x, y), (x, 1 - y), (1 - x, 1 - y)]


def _rcopy(src, dst, ssem, rsem, dev):
    return pltpu.make_async_remote_copy(src_ref=src, dst_ref=dst, send_sem=ssem, recv_sem=rsem, device_id=dev,
                                        device_id_type=MESH)


def _gather_weights(placed, placed_small):
    n_small = len(placed_small)

    def body(*refs):
        o_big = refs[N_BIG + n_small:2 * N_BIG + n_small]
        o_small = refs[2 * N_BIG + n_small:2 * (N_BIG + n_small)]
        send_sems, recv_sems = refs[2 * (N_BIG + n_small):]
        x, y, c, chips = _place()
        j = 2 * x + y
        sends = []

        def rows(w, chip_index, core):
            R = BIG[w][1]
            return pl.ds(pl.multiple_of(chip_index * R + core * (R // 2), 16), R // 2)

        for w in range(N_BIG):
            mine = o_big[w].at[rows(w, j, c)]
            for r, (px, py) in enumerate(chips):
                k = w * 6 + r
                cp = _rcopy(mine, mine, send_sems.at[k], recv_sems.at[k], (px, py, c))
                cp.start()
                sends.append(cp)
        for w in range(n_small):
            for r, (px, py) in enumerate(chips):
                k = N_BIG * 6 + w * 3 + r
                cp = _rcopy(o_small[w].at[j], o_small[w].at[j], send_sems.at[k], recv_sems.at[k], (px, py, c))
                cp.start()
                sends.append(cp)
        for w in range(N_BIG):
            for r, (px, py) in enumerate(chips):
                pj = 2 * px + py
                got = o_big[w].at[rows(w, pj, c)]
                _rcopy(got, got, send_sems.at[w * 6 + r], recv_sems.at[w * 6 + r], (px, py, c)).wait_recv()
                k = w * 6 + 3 + r
                cp = _rcopy(got, got, send_sems.at[k], recv_sems.at[k], (x, y, 1 - c))
                cp.start()
                sends.append(cp)
        for w in range(N_BIG):
            for r, (px, py) in enumerate(chips):
                pj = 2 * px + py
                k = w * 6 + 3 + r
                got = o_big[w].at[rows(w, pj, 1 - c)]
                _rcopy(got, got, send_sems.at[k], recv_sems.at[k], (x, y, 1 - c)).wait_recv()
        for w in range(n_small):
            for r, (px, py) in enumerate(chips):
                k = N_BIG * 6 + w * 3 + r
                got = o_small[w].at[2 * px + py]
                _rcopy(got, got, send_sems.at[k], recv_sems.at[k], (px, py, c)).wait_recv()
        for cp in sends:
            cp.wait_send()

    n_sem = N_BIG * 6 + n_small * 3
    n_arr = N_BIG + n_small
    return _pcall(
        body, name="gather_weights", out_shape=[SDS(a.shape, a.dtype) for a in (*placed, *placed_small)],
        in_specs=[ANY] * n_arr, out_specs=[ANY] * n_arr, input_output_aliases={k: k for k in range(n_arr)},
        scratch_shapes=[pltpu.SemaphoreType.DMA((n_sem,)), pltpu.SemaphoreType.DMA((n_sem,))],
    )(*placed, *placed_small)


def _to_sibling(gbf, small):
    def body(*refs):
        g_refs, sm_ref = refs[:N_BIG], refs[N_BIG]
        o_refs, sm_all = refs[N_BIG + 1:2 * N_BIG + 1], refs[2 * N_BIG + 1]
        send_sems, recv_sems, loc_sem = refs[2 * N_BIG + 2:]
        x, y, c, _ = _place()
        me = 4 * x + 2 * y + c
        sib = (x, y, 1 - c)
        loc = pltpu.make_async_copy(sm_ref, sm_all.at[me], loc_sem)
        loc.start()
        sends = []
        for w in range(N_BIG):
            for jj in range(N_CHIPS):
                k = w * N_CHIPS + jj
                cp = _rcopy(g_refs[w].at[2 * jj + (1 - c)], o_refs[w].at[jj], send_sems.at[k], recv_sems.at[k], sib)
                cp.start()
                sends.append(cp)
        peers = []
        for rel in range(1, N_DEV):
            fx, fy, fc = (rel >> 2) & 1, (rel >> 1) & 1, rel & 1
            px, py, pc = x ^ fx, y ^ fy, c ^ fc
            k = N_BIG * N_CHIPS + rel - 1
            cp = _rcopy(sm_ref, sm_all.at[me], send_sems.at[k], recv_sems.at[k], (px, py, pc))
            cp.start()
            sends.append(cp)
            peers.append((k, 4 * px + 2 * py + pc, (px, py, pc)))
        for w in range(N_BIG):
            for jj in range(N_CHIPS):
                k = w * N_CHIPS + jj
                _rcopy(o_refs[w].at[jj], o_refs[w].at[jj], send_sems.at[k], recv_sems.at[k], sib).wait_recv()
        for k, pidx, dev in peers:
            _rcopy(sm_all.at[pidx], sm_all.at[pidx], send_sems.at[k], recv_sems.at[k], dev).wait_recv()
        for cp in sends:
            cp.wait_send()
        loc.wait()

    n_sem = N_BIG * N_CHIPS + N_DEV - 1
    out_shape = [SDS((N_CHIPS,) + g.shape[1:], BF) for g in gbf] + [SDS((N_DEV,) + small.shape, F32)]
    return _pcall(
        body, name="to_sibling", out_shape=out_shape, in_specs=[ANY] * (N_BIG + 1), out_specs=[ANY] * (N_BIG + 1),
        scratch_shapes=[pltpu.SemaphoreType.DMA((n_sem,)), pltpu.SemaphoreType.DMA((n_sem,)), pltpu.SemaphoreType.DMA],
    )(*gbf, small)


def _to_chips(pbf):
    def body(*refs):
        p_refs, o_refs = refs[:N_BIG], refs[N_BIG:2 * N_BIG]
        send_sems, recv_sems = refs[2 * N_BIG:]
        x, y, c, chips = _place()
        sends = []
        for w in range(N_BIG):
            for r, (px, py) in enumerate(chips):
                k = w * 3 + r
                cp = _rcopy(p_refs[w].at[2 * px + py], o_refs[w].at[r], send_sems.at[k], recv_sems.at[k], (px, py, c))
                cp.start()
                sends.append(cp)
        for w in range(N_BIG):
            for r, (px, py) in enumerate(chips):
                k = w * 3 + r
                _rcopy(o_refs[w].at[r], o_refs[w].at[r], send_sems.at[k], recv_sems.at[k], (px, py, c)).wait_recv()
        for cp in sends:
            cp.wait_send()

    return _pcall(
        body, name="to_chips", out_shape=[SDS((3,) + p.shape[1:], BF) for p in pbf],
        in_specs=[ANY] * N_BIG, out_specs=[ANY] * N_BIG,
        scratch_shapes=[pltpu.SemaphoreType.DMA((N_BIG * 3,)), pltpu.SemaphoreType.DMA((N_BIG * 3,))],
    )(*pbf)


def _swap_halves(fin):
    def body(*refs):
        o_refs = refs[N_BIG:2 * N_BIG]
        send_sems, recv_sems = refs[2 * N_BIG:]
        x, y, c, _ = _place()
        sib = (x, y, 1 - c)
        cps = []
        for w in range(N_BIG):
            cp = _rcopy(o_refs[w].at[c], o_refs[w].at[c], send_sems.at[w], recv_sems.at[w], sib)
            cp.start()
            cps.append(cp)
        for w in range(N_BIG):
            got = o_refs[w].at[1 - c]
            _rcopy(got, got, send_sems.at[w], recv_sems.at[w], sib).wait_recv()
        for cp in cps:
            cp.wait_send()

    return _pcall(
        body, name="swap_halves", out_shape=[SDS(f.shape, F32) for f in fin],
        in_specs=[ANY] * N_BIG, out_specs=[ANY] * N_BIG, input_output_aliases={k: k for k in range(N_BIG)},
        scratch_shapes=[pltpu.SemaphoreType.DMA((N_BIG,)), pltpu.SemaphoreType.DMA((N_BIG,))],
    )(*fin)


def _chip_sum(name, g32, from_sib, core):
    h = g32.shape[1]
    th = h // 2

    def body(core_ref, chip_ref, g_ref, s_ref, pbf_ref, own_ref):
        p = g_ref[0] + s_ref[0].astype(F32)
        pbf_ref[0] = p.astype(BF)

        @pl.when(pl.program_id(1) == chip_ref[0])
        def _():
            own_ref[...] = p

    chip = 2 * lax.axis_index("x") + lax.axis_index("y")
    grid_spec = pltpu.PrefetchScalarGridSpec(
        num_scalar_prefetch=2, grid=(h // th, N_CHIPS),
        in_specs=[pl.BlockSpec((1, th, D), lambda t, jj, core_ref, chip_ref: (2 * jj + core_ref[0], t, 0)),
                  pl.BlockSpec((1, th, D), lambda t, jj, core_ref, chip_ref: (jj, t, 0))],
        out_specs=[pl.BlockSpec((1, th, D), lambda t, jj, core_ref, chip_ref: (jj, t, 0)),
                   pl.BlockSpec((th, D), lambda t, jj, core_ref, chip_ref: (t, 0))],
    )
    return _pcall(
        body, name=name, grid_spec=grid_spec, out_shape=[SDS((N_CHIPS, h, D), BF), SDS((h, D), F32)],
        compiler_params=_params(("arbitrary", "arbitrary"), 32),
    )(jnp.reshape(core, (1,)).astype(jnp.int32), jnp.reshape(chip, (1,)).astype(jnp.int32), g32, from_sib)


def _final_sum(name, own, from_chips, core):
    h = own.shape[0]

    def body(core_ref, o_ref, r_ref, f_ref):
        f_ref[0] = ((o_ref[...] + r_ref[0].astype(F32)) + r_ref[1].astype(F32)) + r_ref[2].astype(F32)

    grid_spec = pltpu.PrefetchScalarGridSpec(
        num_scalar_prefetch=1, grid=(1,),
        in_specs=[pl.BlockSpec((h, D), lambda i, core_ref: (0, 0)), pl.BlockSpec((3, h, D), lambda i, core_ref: (0, 0, 0))],
        out_specs=pl.BlockSpec((1, h, D), lambda i, core_ref: (core_ref[0], 0, 0)),
    )
    return _pcall(body, name=name, grid_spec=grid_spec, out_shape=SDS((2, h, D), F32),
                  compiler_params=_params(("arbitrary",), 40))(jnp.reshape(core, (1,)).astype(jnp.int32), own, from_chips)


def _sum_devices(sm_all):
    def body(a_ref, o_ref):
        s = a_ref[0]
        for k in range(1, N_DEV):
            s = s + a_ref[k]
        o_ref[...] = s

    return _pcall(body, name="sum_devices", out_shape=SDS(sm_all.shape[1:], F32))(sm_all)


def _adamw(name, w, g, m, v, tr):
    rows, cols = w.shape

    def body(w_ref, g_ref, m_ref, v_ref, d_ref, nm_ref, nv_ref):
        gv = g_ref[...]
        nm = ADAM_B1 * m_ref[...] + (1.0 - ADAM_B1) * gv
        nv = ADAM_B2 * v_ref[...] + (1.0 - ADAM_B2) * (gv * gv)
        m_hat = nm / (1.0 - ADAM_B1 ** ADAM_STEP)
        v_hat = nv / (1.0 - ADAM_B2 ** ADAM_STEP)
        d_ref[...] = -ADAM_LR * (m_hat / (jnp.sqrt(v_hat) + ADAM_EPS) + ADAM_WD * w_ref[...])
        nm_ref[...] = nm
        nv_ref[...] = nv

    spec = pl.BlockSpec((tr, cols), lambda i: (i, 0))
    return _pcall(
        body, name=name, grid=(rows // tr,), in_specs=[spec] * 4, out_specs=[spec] * 3,
        out_shape=[SDS((rows, cols), F32)] * 3, compiler_params=_params(("parallel",), 32),
    )(w, g, m, v)


def _pad_rows(a, rows):
    return jnp.pad(a, ((0, rows - a.shape[0]), (0, 0)))


def _pack(parts, rows):
    flat = jnp.concatenate([p.reshape(-1) for p in parts])
    return jnp.pad(flat, (0, rows * 128 - flat.shape[0])).reshape(rows, 128)


def _unpack(buf, shapes):
    flat = buf.reshape(-1)
    out, o = [], 0
    for s in shapes:
        n = int(np.prod(s))
        out.append(flat[o:o + n].reshape(s))
        o += n
    return out


def _local_step(x, tgt, g1, w_int, cw, qg, kg, table, sinks, gco, gao, w_out, g2, w_upt, fw, fb, w_down):
    bucket = jnp.asarray(_bucket_table())
    cw8 = _pad_rows(cw, 8)
    fw8 = _pad_rows(fw, 8)
    bias = _band_bias(table, bucket)
    proj, u1 = _inproj(x, g1, w_int)
    y = _mix_fwd(proj, sinks, cw8, qg, kg, gco, gao, bias)
    h1, u2 = _outproj(y, w_out, x, g2)
    up = _ffn_up(u2, w_upt)
    a = _ffn_act(up, fw8, fb)
    dh2, dh2b, sq = _ffn_down(a, w_down, h1, tgt)

    g_down = _wgrad("wgrad_down", [a], dh2b)
    da = _ffn_down_bwd(dh2b, w_down)
    dug, duv, dfwg, dfwv, dfbg, dfbv = _ffn_act_bwd(up, da, fw8, fb)
    g_up = _wgrad("wgrad_up", [dug, duv], u2)
    dh1, dh1b, dg2 = _norm_matmul_bwd("ffn_up_bwd", [dug, duv], w_upt, [0, DFF], h1, g2, dh2, True)
    g_out = _wgrad("wgrad_out", [y], dh1b)
    dy = _out_bwd(dh1b, w_out)
    dproj, dcw8, dqg, dkg, dgco, dgao, dsink, dbias = _mix_bwd(proj, dy, sinks, cw8, qg, kg, gco, gao, bias)
    dtable = _band_bias_bwd(dbias, bucket).T
    g_in = _wgrad("wgrad_in", [dproj], u1)
    dx, dg1 = _norm_matmul_bwd("in_bwd", [dproj], w_int, [0], x, g1, dh1, False)

    dfw = jnp.concatenate([dfwg[0:3], dfwv[0:3]], axis=1)
    dfb = jnp.concatenate([dfbg, dfbv], axis=1)
    small = [dg1, dcw8[0:3], dqg, dkg, dtable, dsink[:, 0].reshape(1, NH), dgco, dgao, dg2, dfw, dfb]
    return sq, dx, [g_in, g_out, g_up, g_down], small


SMALL_FULL = [(1, D), (3, CW), (1, HD), (1, HD), (NBUCKET, NH), (1, NH), (1, CW), (1, AW), (1, D), (3, 2 * DFF), (1, 2 * DFF)]
SMALL_ROWS_FULL = 216
SMALL_LOCAL = [(1, D), (3, CW // N_CHIPS), (1, HD), (1, HD), (NBUCKET, NH), (1, NH), (1, CW), (1, AW), (1, D),
               (3, 2 * DFF // N_CHIPS), (1, 2 * DFF)]
SMALL_ROWS_LOCAL = 112


def kernel(x, norm_mix_g, w_in, conv_w, q_norm_g, k_norm_g, rel_bias_table, sinks, out_norm_conv_g, out_norm_attn_g, w_out, norm_ffn_g, w_up, ffn_conv_w, ffn_conv_b, w_down, loss_target, m_norm_mix_g, m_w_in, m_conv_w, m_q_norm_g, m_k_norm_g, m_rel_bias_table, m_sinks, m_out_norm_conv_g, m_out_norm_attn_g, m_w_out, m_norm_ffn_g, m_w_up, m_ffn_conv_w, m_ffn_conv_b, m_w_down, v_norm_mix_g, v_w_in, v_conv_w, v_q_norm_g, v_k_norm_g, v_rel_bias_table, v_sinks, v_out_norm_conv_g, v_out_norm_attn_g, v_w_out, v_norm_ffn_g, v_w_up, v_ffn_conv_w, v_ffn_conv_b, v_w_down):
    chip = 2 * lax.axis_index("x") + lax.axis_index("y")
    core = lax.axis_index("c")

    def place(shard):
        rows = shard.shape[0]
        return lax.dynamic_update_slice(jnp.zeros((N_CHIPS * rows, D), BF), shard.astype(BF), (chip * rows, 0))

    def place_small(a):
        blk = _pad_rows(a, 8)
        return lax.dynamic_update_slice(jnp.zeros((N_CHIPS,) + blk.shape, F32), blk[None], (chip, 0, 0))

    w_int, w_out_f, w_upt, w_down_f, cw_all, fw_all = _gather_weights(
        [place(w_in[0].T), place(w_out[0]), place(w_up[0].T), place(w_down[0])],
        [place_small(conv_w[0]), place_small(ffn_conv_w[0])])
    cw = jnp.transpose(cw_all[:, 0:3, :], (1, 0, 2)).reshape(3, CW)
    fw = jnp.transpose(fw_all[:, 0:3, :], (1, 0, 2)).reshape(3, 2 * DFF)

    sq, dx, big, small = _local_step(
        x[0], loss_target[0], norm_mix_g, w_int, cw, q_norm_g, k_norm_g, rel_bias_table, sinks, out_norm_conv_g,
        out_norm_attn_g, w_out_f, norm_ffn_g, w_upt, fw, ffn_conv_b, w_down_f)
    loss = lax.psum(0.5 * jnp.sum(sq) / D, ("x", "y", "c"))

    halves = [R // 2 for _, R in BIG]
    g32 = [g[0].reshape(N_DEV, h, D) for g, h in zip(big, halves)]
    gbf = [g[1].reshape(N_DEV, h, D) for g, h in zip(big, halves)]
    *from_sib, sm_all = _to_sibling(gbf, _pack(small, SMALL_ROWS_FULL))
    sums = [_chip_sum("chip_sum_" + n, g, s, core) for (n, _), g, s in zip(BIG, g32, from_sib)]
    from_chips = _to_chips([p for p, _ in sums])
    fin = [_final_sum("final_sum_" + n, own, r, core) for (n, _), (_, own), r in zip(BIG, sums, from_chips)]
    full = [f.reshape(2 * h, D) for f, h in zip(_swap_halves(fin), halves)]
    g_w_in, g_w_out, g_w_up, g_w_down = full[0].T, full[1], full[2].T, full[3]
    sg = _unpack(_sum_devices(sm_all), SMALL_FULL)
    sg[1] = lax.dynamic_slice_in_dim(sg[1], chip * (CW // N_CHIPS), CW // N_CHIPS, axis=1)
    sg[9] = lax.dynamic_slice_in_dim(sg[9], chip * (2 * DFF // N_CHIPS), 2 * DFF // N_CHIPS, axis=1)

    d_in, nm_in, nv_in = _adamw("adamw_w_in", w_in[0], g_w_in, m_w_in[0], v_w_in[0], 256)
    d_out, nm_out, nv_out = _adamw("adamw_w_out", w_out[0], g_w_out, m_w_out[0], v_w_out[0], 256)
    d_up, nm_up, nv_up = _adamw("adamw_w_up", w_up[0], g_w_up, m_w_up[0], v_w_up[0], 256)
    d_down, nm_down, nv_down = _adamw("adamw_w_down", w_down[0], g_w_down, m_w_down[0], v_w_down[0], 352)
    sw = [norm_mix_g, conv_w[0], q_norm_g, k_norm_g, rel_bias_table, sinks, out_norm_conv_g, out_norm_attn_g,
          norm_ffn_g, ffn_conv_w[0], ffn_conv_b]
    smm = [m_norm_mix_g, m_conv_w[0], m_q_norm_g, m_k_norm_g, m_rel_bias_table, m_sinks, m_out_norm_conv_g,
           m_out_norm_attn_g, m_norm_ffn_g, m_ffn_conv_w[0], m_ffn_conv_b]
    smv = [v_norm_mix_g, v_conv_w[0], v_q_norm_g, v_k_norm_g, v_rel_bias_table, v_sinks, v_out_norm_conv_g,
           v_out_norm_attn_g, v_norm_ffn_g, v_ffn_conv_w[0], v_ffn_conv_b]
    packed = [_pack(p, SMALL_ROWS_LOCAL) for p in (sw, sg, smm, smv)]
    sd, snm, snv = [_unpack(b, SMALL_LOCAL) for b in _adamw("adamw_small", *packed, SMALL_ROWS_LOCAL)]

    def order(s, b_in, b_out, b_up, b_down):
        return (s[0], b_in[None], s[1][None], s[2], s[3], s[4], s[5], s[6], s[7], b_out[None], s[8], b_up[None],
                s[9][None], s[10], b_down[None])

    return (loss, dx[None],
            *order(sg, g_w_in, g_w_out, g_w_up, g_w_down),
            *order(sd, d_in, d_out, d_up, d_down),
            *order(snm, nm_in, nm_out, nm_up, nm_down),
            *order(snv, nv_in, nv_out, nv_up, nv_down))
```

```python
import functools
import math

import numpy as np

import jax
import jax.numpy as jnp
from jax import lax
from jax.experimental import pallas as pl
from jax.experimental.pallas import tpu as pltpu

F32 = jnp.float32
BF = jnp.bfloat16
SDS = jax.ShapeDtypeStruct

T = 2048
D = 1024
CW = 512
AW = 512
HD = 64
NH = 8
NKV = 2
GQ = 4
INW = 2304
DFF = 2816
BLK = 128
NB = T // BLK
NBUCKET = 32
EPS = 1e-6
NEG_INF = -1e30
N_CHIPS = 4
N_DEV = 8

ADAM_LR = 0.001
ADAM_B1 = 0.9
ADAM_B2 = 0.999
ADAM_EPS = 1e-08
ADAM_WD = 0.01
ADAM_STEP = 10

MIB = 1024 * 1024
MESH = pl.DeviceIdType.MESH

_pcall = pl.pallas_call


def _params(sem=None, vmem_mib=None):
    kw = {}
    if sem is not None:
        kw["dimension_semantics"] = sem
    if vmem_mib is not None:
        kw["vmem_limit_bytes"] = vmem_mib * MIB
    return pltpu.CompilerParams(**kw)


def _dot(a, b, ca, cb):
    return lax.dot_general(a, b, (((ca,), (cb,)), ((), ())), preferred_element_type=F32)


def _rms_bwd(dy, x, r, g):
    dg = jnp.sum(dy * (x * r), axis=0, keepdims=True)
    dgx = dy * g
    dx = r * dgx - x * (r * r * r) * jnp.mean(x * dgx, axis=-1, keepdims=True)
    return dx, dg


def _inproj(x, g1, w_int):
    tm = 256

    def body(x_ref, g_ref, w_ref, proj_ref, u_ref):
        xf = x_ref[...]
        r = lax.rsqrt(jnp.mean(xf * xf, axis=-1, keepdims=True) + EPS)
        u = (xf * r * g_ref[...]).astype(BF)
        u_ref[...] = u
        proj_ref[...] = _dot(u, w_ref[...], 1, 1)

    return _pcall(
        body, name="inproj", grid=(T // tm,),
        in_specs=[pl.BlockSpec((tm, D), lambda i: (i, 0)), pl.BlockSpec((1, D), lambda i: (0, 0)),
                  pl.BlockSpec((INW, D), lambda i: (0, 0))],
        out_specs=[pl.BlockSpec((tm, INW), lambda i: (i, 0)), pl.BlockSpec((tm, D), lambda i: (i, 0))],
        out_shape=[SDS((T, INW), F32), SDS((T, D), BF)],
        compiler_params=_params(("parallel",), 40),
    )(x, g1, w_int)


def _outproj(y, w_out, x, g2):
    tm = 256

    def body(y_ref, w_ref, x_ref, g_ref, h1_ref, u2_ref):
        h1 = x_ref[...] + _dot(y_ref[...], w_ref[...], 1, 0)
        h1_ref[...] = h1
        r = lax.rsqrt(jnp.mean(h1 * h1, axis=-1, keepdims=True) + EPS)
        u2_ref[...] = (h1 * r * g_ref[...]).astype(BF)

    return _pcall(
        body, name="outproj", grid=(T // tm,),
        in_specs=[pl.BlockSpec((tm, D), lambda i: (i, 0)), pl.BlockSpec((D, D), lambda i: (0, 0)),
                  pl.BlockSpec((tm, D), lambda i: (i, 0)), pl.BlockSpec((1, D), lambda i: (0, 0))],
        out_specs=[pl.BlockSpec((tm, D), lambda i: (i, 0)), pl.BlockSpec((tm, D), lambda i: (i, 0))],
        out_shape=[SDS((T, D), F32), SDS((T, D), BF)],
        compiler_params=_params(("parallel",), 32),
    )(y, w_out, x, g2)


def _ffn_up(u2, w_upt):
    tm, tn = 1024, 512

    def body(u_ref, w_ref, o_ref):
        o_ref[...] = _dot(u_ref[...], w_ref[...], 1, 1)

    return _pcall(
        body, name="ffn_up", grid=(T // tm, 2 * DFF // tn),
        in_specs=[pl.BlockSpec((tm, D), lambda i, j: (i, 0)), pl.BlockSpec((tn, D), lambda i, j: (j, 0))],
        out_specs=pl.BlockSpec((tm, tn), lambda i, j: (i, j)),
        out_shape=SDS((T, 2 * DFF), F32),
        compiler_params=_params(("parallel", "parallel"), 32),
    )(u2, w_upt)


def _ffn_down(a, w_down, h1, tgt):
    tm = 256

    def body(a_ref, w_ref, h1_ref, t_ref, dh_ref, dhb_ref, l_ref):
        @pl.when(pl.program_id(0) == 0)
        def _():
            l_ref[...] = jnp.zeros_like(l_ref)

        h2 = h1_ref[...] + _dot(a_ref[...], w_ref[...], 1, 0)
        e = h2 - t_ref[...]
        dh = e * (1.0 / D)
        dh_ref[...] = dh
        dhb_ref[...] = dh.astype(BF)
        e2 = jnp.sum((e * e).reshape(tm // 8, 8, D), axis=0)
        acc = e2[:, 0:128]
        for k in range(1, D // 128):
            acc = acc + e2[:, k * 128:(k + 1) * 128]
        l_ref[...] += acc

    return _pcall(
        body, name="ffn_down", grid=(T // tm,),
        in_specs=[pl.BlockSpec((tm, DFF), lambda i: (i, 0)), pl.BlockSpec((DFF, D), lambda i: (0, 0)),
                  pl.BlockSpec((tm, D), lambda i: (i, 0)), pl.BlockSpec((tm, D), lambda i: (i, 0))],
        out_specs=[pl.BlockSpec((tm, D), lambda i: (i, 0)), pl.BlockSpec((tm, D), lambda i: (i, 0)),
                   pl.BlockSpec((8, 128), lambda i: (0, 0))],
        out_shape=[SDS((T, D), F32), SDS((T, D), BF), SDS((8, 128), F32)],
        compiler_params=_params(("arbitrary",), 40),
    )(a, w_down, h1, tgt)


def _bucket_table():
    q = np.arange(BLK, dtype=np.int32)[:, None]
    j = np.arange(2 * BLK, dtype=np.int32)[None, :]
    n = np.maximum(q + BLK - j, 0)
    nf = np.maximum(n, 1).astype(np.float32)
    max_exact = NBUCKET // 2
    large = max_exact + (np.log(nf / np.float32(max_exact)) / np.float32(math.log(BLK / max_exact))
                         * np.float32(NBUCKET - max_exact)).astype(np.int32)
    large = np.minimum(large, NBUCKET - 1)
    return np.where(n < max_exact, n, large).astype(np.int32)


def _band_bias(table, bucket):
    def body(tab_ref, bk_ref, o_ref):
        bk = bk_ref[...]
        eq = [bk == b for b in range(NBUCKET)]
        for h in range(NH):
            acc = jnp.zeros((BLK, 2 * BLK), F32)
            for b in range(NBUCKET):
                acc = jnp.where(eq[b], tab_ref[b, h], acc)
            o_ref[h * BLK:(h + 1) * BLK, :] = acc

    return _pcall(
        body, name="band_bias", out_shape=SDS((NH * BLK, 2 * BLK), F32),
        in_specs=[pl.BlockSpec(memory_space=pltpu.SMEM), pl.BlockSpec(memory_space=pltpu.VMEM)],
        out_specs=pl.BlockSpec(memory_space=pltpu.VMEM),
    )(table, bucket)


def _band_bias_bwd(dbias, bucket, me):
    def body(me_ref, db_ref, bk_ref, o_ref):
        bk = bk_ref[...]
        for b in range(NBUCKET):
            m = bk == b
            for h in range(NH):
                v = jnp.where(m, db_ref[h * BLK:(h + 1) * BLK, :], 0.0)
                s = jnp.sum(jnp.sum(v, axis=1, keepdims=True), axis=0, keepdims=True)
                o_ref[0, b:b + 1, h:h + 1] = s

    grid_spec = pltpu.PrefetchScalarGridSpec(
        num_scalar_prefetch=1, grid=(1,),
        in_specs=[pl.BlockSpec((NH * BLK, 2 * BLK), lambda i, me_ref: (0, 0)),
                  pl.BlockSpec((BLK, 2 * BLK), lambda i, me_ref: (0, 0))],
        out_specs=pl.BlockSpec((1, NBUCKET, NH), lambda i, me_ref: (me_ref[0], 0, 0)),
    )
    return _pcall(body, name="band_bias_bwd", grid_spec=grid_spec, out_shape=SDS((N_DEV, NBUCKET, NH), F32),
                  compiler_params=_params(("arbitrary",)))(me, dbias, bucket)


def _mix_forward(P, zc8, zh8, pkv, first, cw, qg, kg, gco, gao, sink_ref, bias_ref):
    gate_b = P[:, 0:CW]
    gate_c = P[:, CW:2 * CW]
    hc = P[:, 2 * CW:3 * CW]
    z = gate_c * hc
    keep = jnp.where(first, 0.0, 1.0)
    zp = zc8 * zh8 * keep
    p1 = zp[7:8, :]
    p2 = zp[6:7, :]
    row = lax.broadcasted_iota(jnp.int32, (BLK, 1), 0)
    z1 = jnp.where(row == 0, p1, pltpu.roll(z, 1, 0))
    z2 = jnp.where(row == 0, p2, jnp.where(row == 1, p1, pltpu.roll(z, 2, 0)))
    cz = cw[0:1, :] * z2 + cw[1:2, :] * z1 + cw[2:3, :] * z
    y_conv = gate_b * cz

    scale = HD ** -0.5
    qi = lax.broadcasted_iota(jnp.int32, (GQ * BLK, 2 * BLK), 0) & (BLK - 1)
    kj = lax.broadcasted_iota(jnp.int32, (GQ * BLK, 2 * BLK), 1)
    dd = qi + BLK - kj
    first_key = jnp.where(first, BLK, 0)
    valid = (dd >= 0) & (dd < BLK) & (kj >= first_key)

    q0 = 3 * CW
    k0 = q0 + AW
    v0 = k0 + NKV * HD
    heads = []
    outs = []
    for kv in range(NKV):
        kb_raw = jnp.concatenate([pkv[:, kv * HD:(kv + 1) * HD], P[:, k0 + kv * HD:k0 + (kv + 1) * HD]], axis=0)
        rk = lax.rsqrt(jnp.mean(kb_raw * kb_raw, axis=-1, keepdims=True) + EPS)
        kb = (kb_raw * rk * kg).astype(BF)
        vb = jnp.concatenate([pkv[:, NKV * HD + kv * HD:NKV * HD + (kv + 1) * HD],
                              P[:, v0 + kv * HD:v0 + (kv + 1) * HD]], axis=0).astype(BF)
        q_raw, rq, qn = [], [], []
        for g in range(GQ):
            h = kv * GQ + g
            qh = P[:, q0 + h * HD:q0 + (h + 1) * HD]
            r = lax.rsqrt(jnp.mean(qh * qh, axis=-1, keepdims=True) + EPS)
            q_raw.append(qh)
            rq.append(r)
            qn.append(qh * r * qg)
        Q = jnp.concatenate(qn, axis=0).astype(BF)
        S = _dot(Q, kb, 1, 1) * scale + bias_ref[kv * GQ * BLK:(kv + 1) * GQ * BLK, :]
        S = jnp.where(valid, S, NEG_INF)
        sink = jnp.concatenate([jnp.full((BLK, 1), sink_ref[0, kv * GQ + g], F32) for g in range(GQ)], axis=0)
        m = jnp.maximum(jnp.max(S, axis=-1, keepdims=True), sink)
        p = jnp.exp(S - m)
        es = jnp.exp(sink - m)
        denom = jnp.sum(p, axis=-1, keepdims=True) + es
        probs = p / denom
        O = _dot(probs.astype(BF), vb, 1, 0)
        heads.append(dict(kb_raw=kb_raw, rk=rk, kb=kb, vb=vb, q_raw=q_raw, rq=rq, Q=Q, probs=probs,
                          psink=es / denom, O=O))
        outs += [O[g * BLK:(g + 1) * BLK, :] for g in range(GQ)]
    y_attn = jnp.concatenate(outs, axis=1)

    rc = lax.rsqrt(jnp.mean(y_conv * y_conv, axis=-1, keepdims=True) + EPS)
    ra = lax.rsqrt(jnp.mean(y_attn * y_attn, axis=-1, keepdims=True) + EPS)
    y = jnp.concatenate([y_conv * rc * gco, y_attn * ra * gao], axis=1)
    return dict(gate_b=gate_b, gate_c=gate_c, hc=hc, z=z, z1=z1, z2=z2, cz=cz, y_conv=y_conv, y_attn=y_attn,
                rc=rc, ra=ra, heads=heads, y=y, row=row, scale=scale)


def _mix_in_specs(blk):
    return [
        pl.BlockSpec(memory_space=pltpu.SMEM),
        pl.BlockSpec((BLK, INW), lambda s: (blk(s), 0)),
        pl.BlockSpec((8, CW), lambda s: (jnp.maximum(blk(s) * (BLK // 8) - 1, 0), 1)),
        pl.BlockSpec((8, CW), lambda s: (jnp.maximum(blk(s) * (BLK // 8) - 1, 0), 2)),
        pl.BlockSpec((BLK, 2 * NKV * HD), lambda s: (jnp.maximum(blk(s) - 1, 0), (3 * CW + AW) // (2 * NKV * HD))),
    ]


def _mix_param_specs():
    return [
        pl.BlockSpec((8, CW), lambda s: (0, 0)),
        pl.BlockSpec((1, HD), lambda s: (0, 0)),
        pl.BlockSpec((1, HD), lambda s: (0, 0)),
        pl.BlockSpec((1, CW), lambda s: (0, 0)),
        pl.BlockSpec((1, AW), lambda s: (0, 0)),
        pl.BlockSpec((NH * BLK, 2 * BLK), lambda s: (0, 0)),
    ]


def _mix_fwd(proj, sinks, cw8, qg, kg, gco, gao, bias):
    def body(sink_ref, p_ref, zc_ref, zh_ref, pkv_ref, cw_ref, qg_ref, kg_ref, gco_ref, gao_ref, bias_ref, y_ref):
        first = pl.program_id(0) == 0
        f = _mix_forward(p_ref[...], zc_ref[...], zh_ref[...], pkv_ref[...], first, cw_ref[...], qg_ref[...],
                         kg_ref[...], gco_ref[...], gao_ref[...], sink_ref, bias_ref)
        y_ref[...] = f["y"].astype(BF)

    return _pcall(
        body, name="mix_fwd", grid=(NB,),
        in_specs=_mix_in_specs(lambda s: s) + _mix_param_specs(),
        out_specs=pl.BlockSpec((BLK, D), lambda s: (s, 0)),
        out_shape=SDS((T, D), BF),
        compiler_params=_params(("parallel",), 32),
    )(sinks, proj, proj, proj, proj, cw8, qg, kg, gco, gao, bias)


def _mix_bwd(proj, dy, sinks, cw8, qg, kg, gco, gao, bias):
    def blk(s):
        return NB - 1 - s

    def body(sink_ref, p_ref, zc_ref, zh_ref, pkv_ref, dy_ref, cw_ref, qg_ref, kg_ref, gco_ref, gao_ref, bias_ref,
             dproj_ref, dcw_ref, dqg_ref, dkg_ref, dgco_ref, dgao_ref, dsink_ref, dbias_ref,
             ndcz_ref, dkc_ref, dvc_ref):
        s = pl.program_id(0)
        first = s == NB - 1

        @pl.when(s == 0)
        def _():
            for r in (dcw_ref, dqg_ref, dkg_ref, dgco_ref, dgao_ref, dsink_ref, dbias_ref, ndcz_ref, dkc_ref, dvc_ref):
                r[...] = jnp.zeros_like(r)

        cw = cw_ref[...]
        qg_v, kg_v, gco_v, gao_v = qg_ref[...], kg_ref[...], gco_ref[...], gao_ref[...]
        f = _mix_forward(p_ref[...], zc_ref[...], zh_ref[...], pkv_ref[...], first, cw, qg_v, kg_v, gco_v, gao_v,
                         sink_ref, bias_ref)
        dy = dy_ref[...]
        dyc, dgco = _rms_bwd(dy[:, 0:CW], f["y_conv"], f["rc"], gco_v)
        dya, dgao = _rms_bwd(dy[:, CW:CW + AW], f["y_attn"], f["ra"], gao_v)
        dgco_ref[...] += dgco
        dgao_ref[...] += dgao

        row = f["row"]
        dgate_b = dyc * f["cz"]
        dcz = dyc * f["gate_b"]
        dcw_ref[0:1, :] += jnp.sum(dcz * f["z2"], axis=0, keepdims=True)
        dcw_ref[1:2, :] += jnp.sum(dcz * f["z1"], axis=0, keepdims=True)
        dcw_ref[2:3, :] += jnp.sum(dcz * f["z"], axis=0, keepdims=True)
        nxt = ndcz_ref[...]
        n0 = nxt[0:1, :]
        n1 = nxt[1:2, :]
        d1 = jnp.where(row == BLK - 1, n0, pltpu.roll(dcz, BLK - 1, 0))
        d2 = jnp.where(row == BLK - 1, n1, jnp.where(row == BLK - 2, n0, pltpu.roll(dcz, BLK - 2, 0)))
        dz = cw[2:3, :] * dcz + cw[1:2, :] * d1 + cw[0:1, :] * d2
        ndcz_ref[...] = dcz[0:8, :]
        dproj_ref[:, 0:CW] = dgate_b.astype(BF)
        dproj_ref[:, CW:2 * CW] = (dz * f["hc"]).astype(BF)
        dproj_ref[:, 2 * CW:3 * CW] = (dz * f["gate_c"]).astype(BF)

        scale = f["scale"]
        lane = lax.broadcasted_iota(jnp.int32, (1, 128), 1)
        dq_cols, dk_cols, dv_cols = [], [], []
        for kv in range(NKV):
            hd = f["heads"][kv]
            dO = jnp.concatenate([dya[:, (kv * GQ + g) * HD:(kv * GQ + g + 1) * HD] for g in range(GQ)], axis=0)
            delta = jnp.sum(dO * hd["O"], axis=-1, keepdims=True)
            dOb = dO.astype(BF)
            dP = _dot(dOb, hd["vb"], 1, 1)
            dS = hd["probs"] * (dP - delta)
            dsk = hd["psink"] * delta
            for g in range(GQ):
                h = kv * GQ + g
                tot = jnp.sum(dsk[g * BLK:(g + 1) * BLK, :], axis=0, keepdims=True)
                dsink_ref[...] -= jnp.where(lane == h, tot, 0.0)
            dbias_ref[kv * GQ * BLK:(kv + 1) * GQ * BLK, :] += dS
            dSs = (dS * scale).astype(BF)
            dQ = _dot(dSs, hd["kb"], 1, 0)
            dKb = _dot(dSs, hd["Q"], 0, 0)
            dVb = _dot(hd["probs"].astype(BF), dOb, 0, 0)
            dkn = dKb[BLK:, :] + dkc_ref[:, kv * HD:(kv + 1) * HD]
            dvn = dVb[BLK:, :] + dvc_ref[:, kv * HD:(kv + 1) * HD]
            dkc_ref[:, kv * HD:(kv + 1) * HD] = dKb[:BLK, :]
            dvc_ref[:, kv * HD:(kv + 1) * HD] = dVb[:BLK, :]
            dk_raw, dkg = _rms_bwd(dkn, hd["kb_raw"][BLK:, :], hd["rk"][BLK:, :], kg_v)
            dkg_ref[...] += dkg
            dk_cols.append(dk_raw)
            dv_cols.append(dvn)
            for g in range(GQ):
                dq_raw, dqg = _rms_bwd(dQ[g * BLK:(g + 1) * BLK, :], hd["q_raw"][g], hd["rq"][g], qg_v)
                dqg_ref[...] += dqg
                dq_cols.append(dq_raw)
        dproj_ref[:, 3 * CW:INW] = jnp.concatenate(dq_cols + dk_cols + dv_cols, axis=1).astype(BF)

    small = lambda r, c: pl.BlockSpec((r, c), lambda s: (0, 0))
    return _pcall(
        body, name="mix_bwd", grid=(NB,),
        in_specs=_mix_in_specs(blk) + [pl.BlockSpec((BLK, D), lambda s: (blk(s), 0))] + _mix_param_specs(),
        out_specs=[pl.BlockSpec((BLK, INW), lambda s: (blk(s), 0)), small(8, CW), small(1, HD), small(1, HD),
                   small(1, CW), small(1, AW), small(1, 128), small(NH * BLK, 2 * BLK)],
        out_shape=[SDS((T, INW), BF), SDS((8, CW), F32), SDS((1, HD), F32), SDS((1, HD), F32), SDS((1, CW), F32),
                   SDS((1, AW), F32), SDS((1, 128), F32), SDS((NH * BLK, 2 * BLK), F32)],
        scratch_shapes=[pltpu.VMEM((8, CW), F32), pltpu.VMEM((BLK, NKV * HD), F32), pltpu.VMEM((BLK, NKV * HD), F32)],
        compiler_params=_params(("arbitrary",), 40),
    )(sinks, proj, proj, proj, proj, dy, cw8, qg, kg, gco, gao, bias)


FT = 256
NFT = DFF // FT


def _shift_down(u, row, k):
    return jnp.where(row >= k, pltpu.roll(u, k, 0), 0.0)


def _shift_up(u, row, k):
    return jnp.where(row < T - k, pltpu.roll(u, T - k, 0), 0.0)


def _ffn_act_specs():
    return [
        pl.BlockSpec((T, FT), lambda j: (0, j)), pl.BlockSpec((T, FT), lambda j: (0, NFT + j)),
        pl.BlockSpec((8, FT), lambda j: (0, j)), pl.BlockSpec((8, FT), lambda j: (0, NFT + j)),
        pl.BlockSpec((1, FT), lambda j: (0, j)), pl.BlockSpec((1, FT), lambda j: (0, NFT + j)),
    ]


def _ffn_act(up, fw8, fb):
    def body(ug_ref, uv_ref, wg_ref, wv_ref, bg_ref, bv_ref, a_ref):
        row = lax.broadcasted_iota(jnp.int32, (T, 1), 0)

        def conv(u, w, b):
            return w[0:1, :] * _shift_down(u, row, 2) + w[1:2, :] * _shift_down(u, row, 1) + w[2:3, :] * u + b

        gp = conv(ug_ref[...], wg_ref[...], bg_ref[...])
        vp = conv(uv_ref[...], wv_ref[...], bv_ref[...])
        a_ref[...] = (gp * jax.nn.sigmoid(gp) * vp).astype(BF)

    return _pcall(
        body, name="ffn_act", grid=(NFT,), in_specs=_ffn_act_specs(),
        out_specs=pl.BlockSpec((T, FT), lambda j: (0, j)), out_shape=SDS((T, DFF), BF),
        compiler_params=_params(("parallel",), 48),
    )(up, up, fw8, fw8, fb, fb)


def _ffn_act_bwd(up, da, fw8, fb):
    def body(ug_ref, uv_ref, wg_ref, wv_ref, bg_ref, bv_ref, da_ref,
             dug_ref, duv_ref, dwg_ref, dwv_ref, dbg_ref, dbv_ref):
        row = lax.broadcasted_iota(jnp.int32, (T, 1), 0)
        wg, wv = wg_ref[...], wv_ref[...]
        ug, uv = ug_ref[...], uv_ref[...]
        ug1, ug2 = _shift_down(ug, row, 1), _shift_down(ug, row, 2)
        uv1, uv2 = _shift_down(uv, row, 1), _shift_down(uv, row, 2)
        gp = wg[0:1, :] * ug2 + wg[1:2, :] * ug1 + wg[2:3, :] * ug + bg_ref[...]
        vp = wv[0:1, :] * uv2 + wv[1:2, :] * uv1 + wv[2:3, :] * uv + bv_ref[...]
        sig = jax.nn.sigmoid(gp)
        da = da_ref[...]
        dvp = da * (gp * sig)
        dgp = da * vp * (sig * (1.0 + gp * (1.0 - sig)))

        def finish(dp, u, u1, u2, w, du_ref, dw_ref, db_ref):
            db_ref[...] = jnp.sum(dp, axis=0, keepdims=True)
            dw_ref[...] = jnp.zeros_like(dw_ref)
            dw_ref[0:1, :] = jnp.sum(dp * u2, axis=0, keepdims=True)
            dw_ref[1:2, :] = jnp.sum(dp * u1, axis=0, keepdims=True)
            dw_ref[2:3, :] = jnp.sum(dp * u, axis=0, keepdims=True)
            du = w[2:3, :] * dp + w[1:2, :] * _shift_up(dp, row, 1) + w[0:1, :] * _shift_up(dp, row, 2)
            du_ref[...] = du.astype(BF)

        finish(dgp, ug, ug1, ug2, wg, dug_ref, dwg_ref, dbg_ref)
        finish(dvp, uv, uv1, uv2, wv, duv_ref, dwv_ref, dbv_ref)

    col = lambda r: pl.BlockSpec((r, FT), lambda j: (0, j))
    return _pcall(
        body, name="ffn_act_bwd", grid=(NFT,),
        in_specs=_ffn_act_specs() + [pl.BlockSpec((T, FT), lambda j: (0, j))],
        out_specs=[col(T), col(T), col(8), col(8), col(1), col(1)],
        out_shape=[SDS((T, DFF), BF), SDS((T, DFF), BF), SDS((8, DFF), F32), SDS((8, DFF), F32),
                   SDS((1, DFF), F32), SDS((1, DFF), F32)],
        compiler_params=_params(("parallel",), 56),
    )(up, up, fw8, fw8, fb, fb, da)


def _ffn_down_bwd(dh2b, w_down):
    tm = 256

    def body(d_ref, w_ref, o_ref):
        o_ref[...] = _dot(d_ref[...], w_ref[...], 1, 1)

    return _pcall(
        body, name="ffn_down_bwd", grid=(T // tm,),
        in_specs=[pl.BlockSpec((tm, D), lambda i: (i, 0)), pl.BlockSpec((DFF, D), lambda i: (0, 0))],
        out_specs=pl.BlockSpec((tm, DFF), lambda i: (i, 0)), out_shape=SDS((T, DFF), F32),
        compiler_params=_params(("parallel",), 40),
    )(dh2b, w_down)


def _norm_matmul_bwd(name, a_list, w_t, k_offsets, xin, g, dres, want_bf16):
    tm = 256
    ks = [a.shape[1] for a in a_list]
    n_a = len(a_list)

    def body(*refs):
        a_refs = refs[:n_a]
        w_ref, x_ref, g_ref, r_ref = refs[n_a:n_a + 4]
        outs = refs[n_a + 4:]
        dx_ref, dg_ref = outs[0], outs[-1]

        @pl.when(pl.program_id(0) == 0)
        def _():
            dg_ref[...] = jnp.zeros_like(dg_ref)

        du = _dot(a_refs[0][...], w_ref[k_offsets[0]:k_offsets[0] + ks[0], :], 1, 0)
        for k in range(1, n_a):
            du = du + _dot(a_refs[k][...], w_ref[k_offsets[k]:k_offsets[k] + ks[k], :], 1, 0)
        x = x_ref[...]
        r = lax.rsqrt(jnp.mean(x * x, axis=-1, keepdims=True) + EPS)
        dx, dg = _rms_bwd(du, x, r, g_ref[...])
        dx = r_ref[...] + dx
        dx_ref[...] = dx
        if want_bf16:
            outs[1][...] = dx.astype(BF)
        dg_ref[...] += dg

    tile = lambda c: pl.BlockSpec((tm, c), lambda i: (i, 0))
    out_specs = [tile(D)] + ([tile(D)] if want_bf16 else []) + [pl.BlockSpec((1, D), lambda i: (0, 0))]
    out_shape = [SDS((T, D), F32)] + ([SDS((T, D), BF)] if want_bf16 else []) + [SDS((1, D), F32)]
    return _pcall(
        body, name=name, grid=(T // tm,),
        in_specs=[tile(k) for k in ks] + [pl.BlockSpec(w_t.shape, lambda i: (0, 0)), tile(D),
                                           pl.BlockSpec((1, D), lambda i: (0, 0)), tile(D)],
        out_specs=out_specs, out_shape=out_shape,
        compiler_params=_params(("arbitrary",), 56),
    )(*a_list, w_t, xin, g, dres)


def _out_bwd(dh1b, w_out):
    tm = 256

    def body(d_ref, w_ref, o_ref):
        o_ref[...] = _dot(d_ref[...], w_ref[...], 1, 1)

    return _pcall(
        body, name="out_bwd", grid=(T // tm,),
        in_specs=[pl.BlockSpec((tm, D), lambda i: (i, 0)), pl.BlockSpec((D, D), lambda i: (0, 0))],
        out_specs=pl.BlockSpec((tm, D), lambda i: (i, 0)), out_shape=SDS((T, D), F32),
        compiler_params=_params(("parallel",), 32),
    )(dh1b, w_out)


def _wgrad(name, a_list, b):
    tm = 256
    steps = [a.shape[1] // tm for a in a_list]
    starts = [sum(steps[:k]) for k in range(len(a_list))]
    n_a = len(a_list)

    def body(*refs):
        a_refs, b_ref, o32_ref, obf_ref = refs[:n_a], refs[n_a], refs[n_a + 1], refs[n_a + 2]
        i = pl.program_id(0)
        for k in range(n_a):
            @pl.when((i >= starts[k]) & (i < starts[k] + steps[k]))
            def _(k=k):
                r = _dot(a_refs[k][...], b_ref[...], 0, 0)
                o32_ref[...] = r
                obf_ref[...] = r.astype(BF)

    def a_spec(k):
        return pl.BlockSpec((T, tm), lambda i: (0, jnp.clip(i - starts[k], 0, steps[k] - 1)))

    m_total = tm * sum(steps)
    return _pcall(
        body, name=name, grid=(sum(steps),),
        in_specs=[a_spec(k) for k in range(n_a)] + [pl.BlockSpec((T, D), lambda i: (0, 0))],
        out_specs=[pl.BlockSpec((tm, D), lambda i: (i, 0)), pl.BlockSpec((tm, D), lambda i: (i, 0))],
        out_shape=[SDS((m_total, D), F32), SDS((m_total, D), BF)],
        compiler_params=_params(("parallel",), 40),
    )(*a_list, b)


ANY = pl.BlockSpec(memory_space=pl.ANY)
BIG = (("w_in", INW // N_CHIPS), ("w_out", D // N_CHIPS), ("w_up", 2 * DFF // N_CHIPS), ("w_down", DFF // N_CHIPS))
N_BIG = len(BIG)


def _place():
    x, y, c = lax.axis_index("x"), lax.axis_index("y"), lax.axis_index("c")
    return x, y, c, [(1 - x, y), (x, 1 - y), (1 - x, 1 - y)]


def _rcopy(src, dst, ssem, rsem, dev):
    return pltpu.make_async_remote_copy(src_ref=src, dst_ref=dst, send_sem=ssem, recv_sem=rsem, device_id=dev,
                                        device_id_type=MESH)


def _gather_weights(placed, placed_small):
    n_small = len(placed_small)

    def body(*refs):
        o_big = refs[N_BIG + n_small:2 * N_BIG + n_small]
        o_small = refs[2 * N_BIG + n_small:2 * (N_BIG + n_small)]
        send_sems, recv_sems = refs[2 * (N_BIG + n_small):]
        x, y, c, chips = _place()
        j = 2 * x + y
        sends = []

        def rows(w, chip_index, core):
            R = BIG[w][1]
            return pl.ds(pl.multiple_of(chip_index * R + core * (R // 2), 16), R // 2)

        for w in range(N_BIG):
            mine = o_big[w].at[rows(w, j, c)]
            for r, (px, py) in enumerate(chips):
                k = w * 6 + r
                cp = _rcopy(mine, mine, send_sems.at[k], recv_sems.at[k], (px, py, c))
                cp.start()
                sends.append(cp)
        for w in range(n_small):
            for r, (px, py) in enumerate(chips):
                k = N_BIG * 6 + w * 3 + r
                cp = _rcopy(o_small[w].at[j], o_small[w].at[j], send_sems.at[k], recv_sems.at[k], (px, py, c))
                cp.start()
                sends.append(cp)
        for w in range(N_BIG):
            for r, (px, py) in enumerate(chips):
                pj = 2 * px + py
                got = o_big[w].at[rows(w, pj, c)]
                _rcopy(got, got, send_sems.at[w * 6 + r], recv_sems.at[w * 6 + r], (px, py, c)).wait_recv()
                k = w * 6 + 3 + r
                cp = _rcopy(got, got, send_sems.at[k], recv_sems.at[k], (x, y, 1 - c))
                cp.start()
                sends.append(cp)
        for w in range(N_BIG):
            for r, (px, py) in enumerate(chips):
                pj = 2 * px + py
                k = w * 6 + 3 + r
                got = o_big[w].at[rows(w, pj, 1 - c)]
                _rcopy(got, got, send_sems.at[k], recv_sems.at[k], (x, y, 1 - c)).wait_recv()
        for w in range(n_small):
            for r, (px, py) in enumerate(chips):
                k = N_BIG * 6 + w * 3 + r
                got = o_small[w].at[2 * px + py]
                _rcopy(got, got, send_sems.at[k], recv_sems.at[k], (px, py, c)).wait_recv()
        for cp in sends:
            cp.wait_send()

    n_sem = N_BIG * 6 + n_small * 3
    n_arr = N_BIG + n_small
    return _pcall(
        body, name="gather_weights", out_shape=[SDS(a.shape, a.dtype) for a in (*placed, *placed_small)],
        in_specs=[ANY] * n_arr, out_specs=[ANY] * n_arr, input_output_aliases={k: k for k in range(n_arr)},
        scratch_shapes=[pltpu.SemaphoreType.DMA((n_sem,)), pltpu.SemaphoreType.DMA((n_sem,))],
    )(*placed, *placed_small)


def _to_sibling(gbf, smalls):
    n_small = len(smalls)

    def body(*refs):
        g_refs = refs[:N_BIG]
        o_refs = refs[N_BIG + n_small:2 * N_BIG + n_small]
        sm_refs = refs[2 * N_BIG + n_small:2 * (N_BIG + n_small)]
        send_sems, recv_sems = refs[2 * (N_BIG + n_small):]
        x, y, c, _ = _place()
        me = 4 * x + 2 * y + c
        sib = (x, y, 1 - c)
        sends = []
        for w in range(N_BIG):
            for jj in range(N_CHIPS):
                k = w * N_CHIPS + jj
                cp = _rcopy(g_refs[w].at[2 * jj + (1 - c)], o_refs[w].at[jj], send_sems.at[k], recv_sems.at[k], sib)
                cp.start()
                sends.append(cp)
        peers = []
        for s in range(n_small):
            for rel in range(1, N_DEV):
                fx, fy, fc = (rel >> 2) & 1, (rel >> 1) & 1, rel & 1
                px, py, pc = x ^ fx, y ^ fy, c ^ fc
                k = N_BIG * N_CHIPS + s * (N_DEV - 1) + rel - 1
                cp = _rcopy(sm_refs[s].at[me], sm_refs[s].at[me], send_sems.at[k], recv_sems.at[k], (px, py, pc))
                cp.start()
                sends.append(cp)
                peers.append((s, k, 4 * px + 2 * py + pc, (px, py, pc)))
        for w in range(N_BIG):
            for jj in range(N_CHIPS):
                k = w * N_CHIPS + jj
                _rcopy(o_refs[w].at[jj], o_refs[w].at[jj], send_sems.at[k], recv_sems.at[k], sib).wait_recv()
        for s, k, pidx, dev in peers:
            got = sm_refs[s].at[pidx]
            _rcopy(got, got, send_sems.at[k], recv_sems.at[k], dev).wait_recv()
        for cp in sends:
            cp.wait_send()

    n_sem = N_BIG * N_CHIPS + n_small * (N_DEV - 1)
    out_shape = [SDS((N_CHIPS,) + g.shape[1:], BF) for g in gbf] + [SDS(s.shape, s.dtype) for s in smalls]
    n_in = N_BIG + n_small
    return _pcall(
        body, name="to_sibling", out_shape=out_shape, in_specs=[ANY] * n_in, out_specs=[ANY] * n_in,
        input_output_aliases={N_BIG + s: N_BIG + s for s in range(n_small)},
        scratch_shapes=[pltpu.SemaphoreType.DMA((n_sem,)), pltpu.SemaphoreType.DMA((n_sem,))],
    )(*gbf, *smalls)


def _to_chips(pbf):
    def body(*refs):
        p_refs, o_refs = refs[:N_BIG], refs[N_BIG:2 * N_BIG]
        send_sems, recv_sems = refs[2 * N_BIG:]
        x, y, c, chips = _place()
        sends = []
        for w in range(N_BIG):
            for r, (px, py) in enumerate(chips):
                k = w * 3 + r
                cp = _rcopy(p_refs[w].at[2 * px + py], o_refs[w].at[r], send_sems.at[k], recv_sems.at[k], (px, py, c))
                cp.start()
                sends.append(cp)
        for w in range(N_BIG):
            for r, (px, py) in enumerate(chips):
                k = w * 3 + r
                _rcopy(o_refs[w].at[r], o_refs[w].at[r], send_sems.at[k], recv_sems.at[k], (px, py, c)).wait_recv()
        for cp in sends:
            cp.wait_send()

    return _pcall(
        body, name="to_chips", out_shape=[SDS((3,) + p.shape[1:], BF) for p in pbf],
        in_specs=[ANY] * N_BIG, out_specs=[ANY] * N_BIG,
        scratch_shapes=[pltpu.SemaphoreType.DMA((N_BIG * 3,)), pltpu.SemaphoreType.DMA((N_BIG * 3,))],
    )(*pbf)


def _swap_halves(fin):
    def body(*refs):
        o_refs = refs[N_BIG:2 * N_BIG]
        send_sems, recv_sems = refs[2 * N_BIG:]
        x, y, c, _ = _place()
        sib = (x, y, 1 - c)
        cps = []
        for w in range(N_BIG):
            cp = _rcopy(o_refs[w].at[c], o_refs[w].at[c], send_sems.at[w], recv_sems.at[w], sib)
            cp.start()
            cps.append(cp)
        for w in range(N_BIG):
            got = o_refs[w].at[1 - c]
            _rcopy(got, got, send_sems.at[w], recv_sems.at[w], sib).wait_recv()
        for cp in cps:
            cp.wait_send()

    return _pcall(
        body, name="swap_halves", out_shape=[SDS(f.shape, F32) for f in fin],
        in_specs=[ANY] * N_BIG, out_specs=[ANY] * N_BIG, input_output_aliases={k: k for k in range(N_BIG)},
        scratch_shapes=[pltpu.SemaphoreType.DMA((N_BIG,)), pltpu.SemaphoreType.DMA((N_BIG,))],
    )(*fin)


def _chip_sum(name, g32, from_sib, core):
    h = g32.shape[1]
    th = h // 2

    def body(core_ref, chip_ref, g_ref, s_ref, pbf_ref, own_ref):
        p = g_ref[0] + s_ref[0].astype(F32)
        pbf_ref[0] = p.astype(BF)

        @pl.when(pl.program_id(1) == chip_ref[0])
        def _():
            own_ref[...] = p

    chip = 2 * lax.axis_index("x") + lax.axis_index("y")
    grid_spec = pltpu.PrefetchScalarGridSpec(
        num_scalar_prefetch=2, grid=(h // th, N_CHIPS),
        in_specs=[pl.BlockSpec((1, th, D), lambda t, jj, core_ref, chip_ref: (2 * jj + core_ref[0], t, 0)),
                  pl.BlockSpec((1, th, D), lambda t, jj, core_ref, chip_ref: (jj, t, 0))],
        out_specs=[pl.BlockSpec((1, th, D), lambda t, jj, core_ref, chip_ref: (jj, t, 0)),
                   pl.BlockSpec((th, D), lambda t, jj, core_ref, chip_ref: (t, 0))],
    )
    return _pcall(
        body, name=name, grid_spec=grid_spec, out_shape=[SDS((N_CHIPS, h, D), BF), SDS((h, D), F32)],
        compiler_params=_params(("arbitrary", "arbitrary"), 32),
    )(jnp.reshape(core, (1,)).astype(jnp.int32), jnp.reshape(chip, (1,)).astype(jnp.int32), g32, from_sib)


def _final_sum(name, own, from_chips, core):
    h = own.shape[0]

    def body(core_ref, o_ref, r_ref, f_ref):
        f_ref[0] = ((o_ref[...] + r_ref[0].astype(F32)) + r_ref[1].astype(F32)) + r_ref[2].astype(F32)

    grid_spec = pltpu.PrefetchScalarGridSpec(
        num_scalar_prefetch=1, grid=(1,),
        in_specs=[pl.BlockSpec((h, D), lambda i, core_ref: (0, 0)), pl.BlockSpec((3, h, D), lambda i, core_ref: (0, 0, 0))],
        out_specs=pl.BlockSpec((1, h, D), lambda i, core_ref: (core_ref[0], 0, 0)),
    )
    return _pcall(body, name=name, grid_spec=grid_spec, out_shape=SDS((2, h, D), F32),
                  compiler_params=_params(("arbitrary",), 40))(jnp.reshape(core, (1,)).astype(jnp.int32), own, from_chips)


def _adam_math(w, g, m, v):
    nm = ADAM_B1 * m + (1.0 - ADAM_B1) * g
    nv = ADAM_B2 * v + (1.0 - ADAM_B2) * (g * g)
    m_hat = nm / (1.0 - ADAM_B1 ** ADAM_STEP)
    v_hat = nv / (1.0 - ADAM_B2 ** ADAM_STEP)
    return -ADAM_LR * (m_hat / (jnp.sqrt(v_hat) + ADAM_EPS) + ADAM_WD * w), nm, nv


def _adamw(name, w, g, m, v, tr):
    rows, cols = w.shape

    def body(w_ref, g_ref, m_ref, v_ref, d_ref, nm_ref, nv_ref):
        d_ref[...], nm_ref[...], nv_ref[...] = _adam_math(w_ref[...], g_ref[...], m_ref[...], v_ref[...])

    spec = pl.BlockSpec((tr, cols), lambda i: (i, 0))
    return _pcall(
        body, name=name, grid=(rows // tr,), in_specs=[spec] * 4, out_specs=[spec] * 3,
        out_shape=[SDS((rows, cols), F32)] * 3, compiler_params=_params(("parallel",), 32),
    )(w, g, m, v)


C_G1, C_G2, C_GCO, C_GAO, C_DCW, C_DQG, C_DKG, C_SINK, C_SQ = 0, 1024, 2048, 2560, 3072, 4608, 4736, 4864, 5632
P_W = C_SQ + 128


def _pack_small(me, dfwg, dfwv, dfbg, dfbv, dg1, dg2, dgco, dgao, dcw8, dqg, dkg, dsink, sq):
    def body(me_ref, dfwg_r, dfwv_r, dfbg_r, dfbv_r, dg1_r, dg2_r, dgco_r, dgao_r, dcw_r, dqg_r, dkg_r, dsink_r, sq_r, o):
        o[...] = jnp.zeros_like(o)
        o[0, :, 0:DFF] = dfwg_r[...]
        o[0, :, DFF:2 * DFF] = dfwv_r[...]
        o[0, 3:4, 0:DFF] = dfbg_r[...]
        o[0, 3:4, DFF:2 * DFF] = dfbv_r[...]
        o[0, 4:5, C_G1:C_G1 + D] = dg1_r[...]
        o[0, 4:5, C_G2:C_G2 + D] = dg2_r[...]
        o[0, 4:5, C_GCO:C_GCO + CW] = dgco_r[...]
        o[0, 4:5, C_GAO:C_GAO + AW] = dgao_r[...]
        for r in range(3):
            o[0, 4:5, C_DCW + r * CW:C_DCW + (r + 1) * CW] = dcw_r[r:r + 1, :]
        o[0, 4:5, C_DQG:C_DQG + HD] = dqg_r[...]
        o[0, 4:5, C_DKG:C_DKG + HD] = dkg_r[...]
        o[0, 4:5, C_SINK:C_SINK + 128] = dsink_r[...]
        o[0, :, C_SQ:C_SQ + 128] = sq_r[...]

    ins = (dfwg, dfwv, dfbg, dfbv, dg1, dg2, dgco, dgao, dcw8, dqg, dkg, dsink, sq)
    grid_spec = pltpu.PrefetchScalarGridSpec(
        num_scalar_prefetch=1, grid=(1,),
        in_specs=[pl.BlockSpec(a.shape, lambda i, me_ref: (0, 0)) for a in ins],
        out_specs=pl.BlockSpec((1, 8, P_W), lambda i, me_ref: (me_ref[0], 0, 0)),
    )
    return _pcall(body, name="pack_small", grid_spec=grid_spec, out_shape=SDS((N_DEV, 8, P_W), F32),
                  compiler_params=_params(("arbitrary",)))(me, *ins)


N_SMALL = 11


def _small_adam(chip, p_all, tbl_all, ws, ms, vs):
    fw_cols = 2 * DFF // N_CHIPS
    cw_cols = CW // N_CHIPS

    def body(chip_ref, p_ref, fw_ref, cw0_ref, cw1_ref, cw2_ref, tbl_ref, *refs):
        w_r, m_r, v_r = refs[0:N_SMALL], refs[N_SMALL:2 * N_SMALL], refs[2 * N_SMALL:3 * N_SMALL]
        outs = refs[3 * N_SMALL:]
        g_o, d_o, nm_o, nv_o = (outs[k * N_SMALL:(k + 1) * N_SMALL] for k in range(4))
        loss_o = outs[4 * N_SMALL]

        def total(ref):
            s = ref[0]
            for k in range(1, N_DEV):
                s = s + ref[k]
            return s

        S = total(p_ref)
        fw = total(fw_ref)
        cws = [total(r) for r in (cw0_ref, cw1_ref, cw2_ref)]

        def step(i, g, at):
            d, nm, nv = _adam_math(w_r[i][at], g, m_r[i][at], v_r[i][at])
            g_o[i][at], d_o[i][at], nm_o[i][at], nv_o[i][at] = g, d, nm, nv

        everything = (slice(None), slice(None))
        step(0, S[4:5, C_G1:C_G1 + D], everything)
        for r in range(3):
            step(1, cws[r][4:5, :], (0, slice(r, r + 1), slice(None)))
        step(2, S[4:5, C_DQG:C_DQG + HD], everything)
        step(3, S[4:5, C_DKG:C_DKG + HD], everything)
        step(4, total(tbl_ref), everything)
        step(5, S[4:5, C_SINK:C_SINK + NH], everything)
        step(6, S[4:5, C_GCO:C_GCO + CW], everything)
        step(7, S[4:5, C_GAO:C_GAO + AW], everything)
        step(8, S[4:5, C_G2:C_G2 + D], everything)
        step(9, fw[0:3, :], (0, slice(None), slice(None)))
        step(10, S[3:4, 0:2 * DFF], everything)
        sq = S[:, C_SQ:C_SQ + 128]
        loss_o[...] = jnp.sum(jnp.sum(sq, axis=1, keepdims=True), axis=0, keepdims=True) * (0.5 / D)

    def full(a):
        n = len(a.shape)
        return pl.BlockSpec(a.shape, lambda i, chip_ref: (0,) * n)

    params = [*ws, *ms, *vs]
    grid_spec = pltpu.PrefetchScalarGridSpec(
        num_scalar_prefetch=1, grid=(1,),
        in_specs=[full(p_all),
                  pl.BlockSpec((N_DEV, 8, fw_cols), lambda i, chip_ref: (0, 0, chip_ref[0])),
                  *[pl.BlockSpec((N_DEV, 8, cw_cols), lambda i, chip_ref, r=r: (0, 0, (C_DCW + r * CW) // cw_cols + chip_ref[0]))
                    for r in range(3)],
                  full(tbl_all), *[full(a) for a in params]],
        out_specs=[full(a) for a in ws] * 4 + [pl.BlockSpec((1, 1), lambda i, chip_ref: (0, 0))],
    )
    out = _pcall(
        body, name="small_adam", grid_spec=grid_spec,
        out_shape=[SDS(a.shape, F32) for a in ws] * 4 + [SDS((1, 1), F32)],
        compiler_params=_params(("arbitrary",), 32),
    )(chip, p_all, p_all, p_all, p_all, p_all, tbl_all, *params)
    return out[0:N_SMALL], out[N_SMALL:2 * N_SMALL], out[2 * N_SMALL:3 * N_SMALL], out[3 * N_SMALL:4 * N_SMALL], out[4 * N_SMALL]


def _place_weights(chip, shards, conv_w, ffn_conv_w):
    steps = 4

    def body(chip_ref, a0, a1, a2, a3, s0, s1, o0, o1, o2, o3, t0, t1):
        for a, o in ((a0, o0), (a1, o1), (a2, o2), (a3, o3)):
            o[...] = a[...].astype(BF)

        @pl.when(pl.program_id(0) == 0)
        def _():
            for s, t in ((s0, t0), (s1, t1)):
                t[...] = jnp.zeros_like(t)
                t[0, 0:3, :] = s[...]

    rows = [s.shape[0] // steps for s in shards]
    grid_spec = pltpu.PrefetchScalarGridSpec(
        num_scalar_prefetch=1, grid=(steps,),
        in_specs=[pl.BlockSpec((r, D), lambda i, chip_ref: (i, 0)) for r in rows]
        + [pl.BlockSpec(s.shape, lambda i, chip_ref: (0, 0)) for s in (conv_w, ffn_conv_w)],
        out_specs=[pl.BlockSpec((r, D), lambda i, chip_ref: (chip_ref[0] * steps + i, 0)) for r in rows]
        + [pl.BlockSpec((1, 8, s.shape[1]), lambda i, chip_ref: (chip_ref[0], 0, 0)) for s in (conv_w, ffn_conv_w)],
    )
    return _pcall(
        body, name="place_weights", grid_spec=grid_spec,
        out_shape=[SDS((N_CHIPS * s.shape[0], D), BF) for s in shards]
        + [SDS((N_CHIPS, 8, s.shape[1]), F32) for s in (conv_w, ffn_conv_w)],
        compiler_params=_params(("arbitrary",), 32),
    )(chip, *shards, conv_w, ffn_conv_w)


def _pad_rows(a, rows):
    return jnp.pad(a, ((0, rows - a.shape[0]), (0, 0)))


def _local_step(me, x, tgt, g1, w_int, cw8, qg, kg, table, sinks, gco, gao, w_out, g2, w_upt, fw8, fb, w_down):
    bucket = jnp.asarray(_bucket_table())
    bias = _band_bias(table, bucket)
    proj, u1 = _inproj(x, g1, w_int)
    y = _mix_fwd(proj, sinks, cw8, qg, kg, gco, gao, bias)
    h1, u2 = _outproj(y, w_out, x, g2)
    up = _ffn_up(u2, w_upt)
    a = _ffn_act(up, fw8, fb)
    dh2, dh2b, sq = _ffn_down(a, w_down, h1, tgt)

    g_down = _wgrad("wgrad_down", [a], dh2b)
    da = _ffn_down_bwd(dh2b, w_down)
    dug, duv, dfwg, dfwv, dfbg, dfbv = _ffn_act_bwd(up, da, fw8, fb)
    g_up = _wgrad("wgrad_up", [dug, duv], u2)
    dh1, dh1b, dg2 = _norm_matmul_bwd("ffn_up_bwd", [dug, duv], w_upt, [0, DFF], h1, g2, dh2, True)
    g_out = _wgrad("wgrad_out", [y], dh1b)
    dy = _out_bwd(dh1b, w_out)
    dproj, dcw8, dqg, dkg, dgco, dgao, dsink, dbias = _mix_bwd(proj, dy, sinks, cw8, qg, kg, gco, gao, bias)
    tbl_all = _band_bias_bwd(dbias, bucket, me)
    g_in = _wgrad("wgrad_in", [dproj], u1)
    dx, dg1 = _norm_matmul_bwd("in_bwd", [dproj], w_int, [0], x, g1, dh1, False)
    p_all = _pack_small(me, dfwg, dfwv, dfbg, dfbv, dg1, dg2, dgco, dgao, dcw8, dqg, dkg, dsink, sq)
    return dx, [g_in, g_out, g_up, g_down], p_all, tbl_all


def kernel(x, norm_mix_g, w_in, conv_w, q_norm_g, k_norm_g, rel_bias_table, sinks, out_norm_conv_g, out_norm_attn_g, w_out, norm_ffn_g, w_up, ffn_conv_w, ffn_conv_b, w_down, loss_target, m_norm_mix_g, m_w_in, m_conv_w, m_q_norm_g, m_k_norm_g, m_rel_bias_table, m_sinks, m_out_norm_conv_g, m_out_norm_attn_g, m_w_out, m_norm_ffn_g, m_w_up, m_ffn_conv_w, m_ffn_conv_b, m_w_down, v_norm_mix_g, v_w_in, v_conv_w, v_q_norm_g, v_k_norm_g, v_rel_bias_table, v_sinks, v_out_norm_conv_g, v_out_norm_attn_g, v_w_out, v_norm_ffn_g, v_w_up, v_ffn_conv_w, v_ffn_conv_b, v_w_down):
    as_arg = lambda i: jnp.reshape(i, (1,)).astype(jnp.int32)
    chip = 2 * lax.axis_index("x") + lax.axis_index("y")
    core = lax.axis_index("c")
    me = as_arg(2 * chip + core)

    placed = _place_weights(as_arg(chip), [w_in[0].T, w_out[0], w_up[0].T, w_down[0]], conv_w[0], ffn_conv_w[0])
    w_int, w_out_f, w_upt, w_down_f, cw_all, fw_all = _gather_weights(placed[:N_BIG], placed[N_BIG:])
    cw8 = jnp.transpose(cw_all, (1, 0, 2)).reshape(8, CW)
    fw8 = jnp.transpose(fw_all, (1, 0, 2)).reshape(8, 2 * DFF)

    dx, big, p_all, tbl_all = _local_step(
        me, x[0], loss_target[0], norm_mix_g, w_int, cw8, q_norm_g, k_norm_g, rel_bias_table, sinks, out_norm_conv_g,
        out_norm_attn_g, w_out_f, norm_ffn_g, w_upt, fw8, ffn_conv_b, w_down_f)

    halves = [R // 2 for _, R in BIG]
    g32 = [g[0].reshape(N_DEV, h, D) for g, h in zip(big, halves)]
    gbf = [g[1].reshape(N_DEV, h, D) for g, h in zip(big, halves)]
    *from_sib, p_all, tbl_all = _to_sibling(gbf, [p_all, tbl_all])
    sums = [_chip_sum("chip_sum_" + n, g, s, core) for (n, _), g, s in zip(BIG, g32, from_sib)]
    from_chips = _to_chips([p for p, _ in sums])
    fin = [_final_sum("final_sum_" + n, own, r, core) for (n, _), (_, own), r in zip(BIG, sums, from_chips)]
    full = [f.reshape(2 * h, D) for f, h in zip(_swap_halves(fin), halves)]
    g_w_in, g_w_out, g_w_up, g_w_down = full[0].T, full[1], full[2].T, full[3]

    d_in, nm_in, nv_in = _adamw("adamw_w_in", w_in[0], g_w_in, m_w_in[0], v_w_in[0], 256)
    d_out, nm_out, nv_out = _adamw("adamw_w_out", w_out[0], g_w_out, m_w_out[0], v_w_out[0], 256)
    d_up, nm_up, nv_up = _adamw("adamw_w_up", w_up[0], g_w_up, m_w_up[0], v_w_up[0], 256)
    d_down, nm_down, nv_down = _adamw("adamw_w_down", w_down[0], g_w_down, m_w_down[0], v_w_down[0], 352)
    sw = [norm_mix_g, conv_w, q_norm_g, k_norm_g, rel_bias_table, sinks, out_norm_conv_g, out_norm_attn_g,
          norm_ffn_g, ffn_conv_w, ffn_conv_b]
    smm = [m_norm_mix_g, m_conv_w, m_q_norm_g, m_k_norm_g, m_rel_bias_table, m_sinks, m_out_norm_conv_g,
           m_out_norm_attn_g, m_norm_ffn_g, m_ffn_conv_w, m_ffn_conv_b]
    smv = [v_norm_mix_g, v_conv_w, v_q_norm_g, v_k_norm_g, v_rel_bias_table, v_sinks, v_out_norm_conv_g,
           v_out_norm_attn_g, v_norm_ffn_g, v_ffn_conv_w, v_ffn_conv_b]
    sg, sd, snm, snv, loss = _small_adam(as_arg(chip), p_all, tbl_all, sw, smm, smv)
    loss = loss.reshape(())

    def order(s, b_in, b_out, b_up, b_down):
        return (s[0], b_in[None], s[1], s[2], s[3], s[4], s[5], s[6], s[7], b_out[None], s[8], b_up[None],
                s[9], s[10], b_down[None])

    return (loss, dx[None],
            *order(sg, g_w_in, g_w_out, g_w_up, g_w_down),
            *order(sd, d_in, d_out, d_up, d_down),
            *order(snm, nm_in, nm_out, nm_up, nm_down),
            *order(snv, nv_in, nv_out, nv_up, nv_down))
```

```python
import functools
import math

import numpy as np

import jax
import jax.numpy as jnp
from jax import lax
from jax.experimental import pallas as pl
from jax.experimental.pallas import tpu as pltpu

F32 = jnp.float32
BF = jnp.bfloat16
SDS = jax.ShapeDtypeStruct

T = 2048
D = 1024
CW = 512
AW = 512
HD = 64
NH = 8
NKV = 2
GQ = 4
INW = 2304
DFF = 2816
BLK = 128
NB = T // BLK
NBUCKET = 32
EPS = 1e-6
NEG_INF = -1e30
N_CHIPS = 4
N_DEV = 8

ADAM_LR = 0.001
ADAM_B1 = 0.9
ADAM_B2 = 0.999
ADAM_EPS = 1e-08
ADAM_WD = 0.01
ADAM_STEP = 10

MIB = 1024 * 1024
MESH = pl.DeviceIdType.MESH
ANY = pl.BlockSpec(memory_space=pl.ANY)

_pcall = pl.pallas_call


def _params(sem=None, vmem_mib=None):
    kw = {}
    if sem is not None:
        kw["dimension_semantics"] = sem
    if vmem_mib is not None:
        kw["vmem_limit_bytes"] = vmem_mib * MIB
    return pltpu.CompilerParams(**kw)


def _dot(a, b, ca, cb):
    return lax.dot_general(a, b, (((ca,), (cb,)), ((), ())), preferred_element_type=F32)


def _rms_bwd(dy, x, r, g):
    dg = jnp.sum(dy * (x * r), axis=0, keepdims=True)
    dgx = dy * g
    dx = r * dgx - x * (r * r * r) * jnp.mean(x * dgx, axis=-1, keepdims=True)
    return dx, dg


def _where():
    x, y, c = lax.axis_index("x"), lax.axis_index("y"), lax.axis_index("c")
    return x, y, c, [(1 - x, y), (x, 1 - y), (1 - x, 1 - y)]


def _rcopy(src, dst, ssem, rsem, dev):
    return pltpu.make_async_remote_copy(src_ref=src, dst_ref=dst, send_sem=ssem, recv_sem=rsem, device_id=dev,
                                        device_id_type=MESH)


class _Task:
    def __init__(self, ins, outs, alias, n_sem, start, finish):
        self.ins, self.outs, self.alias, self.n_sem, self.start, self.finish = ins, outs, alias, n_sem, start, finish


def _t_gather(placed):
    R = placed.shape[0] // N_CHIPS

    def rows(chip_index, core):
        return pl.ds(pl.multiple_of(chip_index * R + core * (R // 2), 16), R // 2)

    def start(cin, cout, ss, rs, b):
        x, y, c, chips = _where()
        mine = cout[0].at[rows(2 * x + y, c)]
        for r, (px, py) in enumerate(chips):
            _rcopy(mine, mine, ss.at[b + r], rs.at[b + r], (px, py, c)).start()

    def finish(cin, cout, ss, rs, b):
        x, y, c, chips = _where()
        buf = cout[0]
        sib = (x, y, 1 - c)
        for r, (px, py) in enumerate(chips):
            got = buf.at[rows(2 * px + py, c)]
            _rcopy(got, got, ss.at[b + r], rs.at[b + r], (px, py, c)).wait_recv()
            _rcopy(got, got, ss.at[b + 3 + r], rs.at[b + 3 + r], sib).start()
        for r, (px, py) in enumerate(chips):
            got = buf.at[rows(2 * px + py, 1 - c)]
            _rcopy(got, got, ss.at[b + 3 + r], rs.at[b + 3 + r], sib).wait_recv()
        mine = buf.at[rows(2 * x + y, c)]
        for r in range(6):
            _rcopy(mine, mine, ss.at[b + r], rs.at[b + r], sib).wait_send()

    return _Task([placed], [SDS(placed.shape, placed.dtype)], [(0, 0)], 6, start, finish)


def _t_small_weights(buf):
    def start(cin, cout, ss, rs, b):
        x, y, c, chips = _where()
        mine = cout[0].at[2 * x + y]
        for r, (px, py) in enumerate(chips):
            _rcopy(mine, mine, ss.at[b + r], rs.at[b + r], (px, py, c)).start()

    def finish(cin, cout, ss, rs, b):
        x, y, c, chips = _where()
        for r, (px, py) in enumerate(chips):
            got = cout[0].at[2 * px + py]
            _rcopy(got, got, ss.at[b + r], rs.at[b + r], (px, py, c)).wait_recv()
        for r, (px, py) in enumerate(chips):
            mine = cout[0].at[2 * x + y]
            _rcopy(mine, mine, ss.at[b + r], rs.at[b + r], (px, py, c)).wait_send()

    return _Task([buf], [SDS(buf.shape, buf.dtype)], [(0, 0)], 3, start, finish)


def _t_sibling(gbf):
    def start(cin, cout, ss, rs, b):
        x, y, c, _ = _where()
        for jj in range(N_CHIPS):
            _rcopy(cin[0].at[2 * jj + (1 - c)], cout[0].at[jj], ss.at[b + jj], rs.at[b + jj], (x, y, 1 - c)).start()

    def finish(cin, cout, ss, rs, b):
        x, y, c, _ = _where()
        for jj in range(N_CHIPS):
            got = cout[0].at[jj]
            _rcopy(got, got, ss.at[b + jj], rs.at[b + jj], (x, y, 1 - c)).wait_recv()
        for jj in range(N_CHIPS):
            got = cout[0].at[jj]
            _rcopy(got, got, ss.at[b + jj], rs.at[b + jj], (x, y, 1 - c)).wait_send()

    return _Task([gbf], [SDS((N_CHIPS,) + gbf.shape[1:], BF)], [], N_CHIPS, start, finish)


def _t_chips(pbf):
    def start(cin, cout, ss, rs, b):
        x, y, c, chips = _where()
        for r, (px, py) in enumerate(chips):
            _rcopy(cin[0].at[2 * px + py], cout[0].at[r], ss.at[b + r], rs.at[b + r], (px, py, c)).start()

    def finish(cin, cout, ss, rs, b):
        x, y, c, chips = _where()
        for r, (px, py) in enumerate(chips):
            got = cout[0].at[r]
            _rcopy(got, got, ss.at[b + r], rs.at[b + r], (px, py, c)).wait_recv()
        for r, (px, py) in enumerate(chips):
            got = cout[0].at[r]
            _rcopy(got, got, ss.at[b + r], rs.at[b + r], (px, py, c)).wait_send()

    return _Task([pbf], [SDS((3,) + pbf.shape[1:], BF)], [], 3, start, finish)


def _t_swap(fin):
    def start(cin, cout, ss, rs, b):
        x, y, c, _ = _where()
        mine = cout[0].at[c]
        _rcopy(mine, mine, ss.at[b], rs.at[b], (x, y, 1 - c)).start()

    def finish(cin, cout, ss, rs, b):
        x, y, c, _ = _where()
        got = cout[0].at[1 - c]
        _rcopy(got, got, ss.at[b], rs.at[b], (x, y, 1 - c)).wait_recv()
        _rcopy(got, got, ss.at[b], rs.at[b], (x, y, 1 - c)).wait_send()

    return _Task([fin], [SDS(fin.shape, fin.dtype)], [(0, 0)], 1, start, finish)


def _t_allgather(buf):
    def peers():
        x, y, c, _ = _where()
        out = []
        for rel in range(1, N_DEV):
            px, py, pc = x ^ ((rel >> 2) & 1), y ^ ((rel >> 1) & 1), c ^ (rel & 1)
            out.append((rel - 1, 4 * px + 2 * py + pc, (px, py, pc)))
        return 4 * x + 2 * y + c, out

    def start(cin, cout, ss, rs, b):
        me, ps = peers()
        mine = cout[0].at[me]
        for k, _, dev in ps:
            _rcopy(mine, mine, ss.at[b + k], rs.at[b + k], dev).start()

    def finish(cin, cout, ss, rs, b):
        me, ps = peers()
        for k, pidx, dev in ps:
            got = cout[0].at[pidx]
            _rcopy(got, got, ss.at[b + k], rs.at[b + k], dev).wait_recv()
        for k, _, dev in ps:
            mine = cout[0].at[me]
            _rcopy(mine, mine, ss.at[b + k], rs.at[b + k], dev).wait_send()

    return _Task([buf], [SDS(buf.shape, buf.dtype)], [(0, 0)], N_DEV - 1, start, finish)


def _run_tasks(comm, which, cin, cout, ss, rs):
    i0 = o0 = s0 = 0
    for t in comm:
        getattr(t, which)(cin[i0:i0 + len(t.ins)], cout[o0:o0 + len(t.outs)], ss, rs, s0)
        i0, o0, s0 = i0 + len(t.ins), o0 + len(t.outs), s0 + t.n_sem


def _comm_layout(comm, n_in, n_out):
    c_in = [a for t in comm for a in t.ins]
    c_out = [s for t in comm for s in t.outs]
    aliases, i0, o0 = {}, 0, 0
    for t in comm:
        for i, o in t.alias:
            aliases[n_in + i0 + i] = n_out + o0 + o
        i0, o0 = i0 + len(t.ins), o0 + len(t.outs)
    return c_in, c_out, aliases, sum(t.n_sem for t in comm)


def _call(body, operands, *, name, grid, in_specs, out_specs, out_shape, scratch_shapes=(), sem=None, vmem_mib=None, comm=()):
    if not comm:
        return _pcall(body, name=name, grid=grid, in_specs=in_specs, out_specs=out_specs, out_shape=out_shape,
                      scratch_shapes=list(scratch_shapes), compiler_params=_params(sem, vmem_mib))(*operands)
    n_in, n_out, n_scr = len(in_specs), len(out_specs), len(scratch_shapes)
    c_in, c_out, aliases, n_sem = _comm_layout(comm, n_in, n_out)

    def wrapped(*refs):
        ins, cin = refs[:n_in], refs[n_in:n_in + len(c_in)]
        rest = refs[n_in + len(c_in):]
        outs, cout = rest[:n_out], rest[n_out:n_out + len(c_out)]
        rest = rest[n_out + len(c_out):]
        scr, (ss, rs) = rest[:n_scr], rest[n_scr:]
        ids = [pl.program_id(k) for k in range(len(grid))]
        first = functools.reduce(jnp.logical_and, [i == 0 for i in ids])
        last = functools.reduce(jnp.logical_and, [i == n - 1 for i, n in zip(ids, grid)])
        pl.when(first)(lambda: _run_tasks(comm, "start", cin, cout, ss, rs))
        body(*ins, *outs, *scr)
        pl.when(last)(lambda: _run_tasks(comm, "finish", cin, cout, ss, rs))

    return _pcall(
        wrapped, name=name, grid=grid, in_specs=list(in_specs) + [ANY] * len(c_in),
        out_specs=list(out_specs) + [ANY] * len(c_out), out_shape=list(out_shape) + c_out,
        scratch_shapes=list(scratch_shapes) + [pltpu.SemaphoreType.DMA((n_sem,)), pltpu.SemaphoreType.DMA((n_sem,))],
        input_output_aliases=aliases, compiler_params=_params(("arbitrary",) * len(grid), vmem_mib),
    )(*operands, *c_in)


def _comm_call(name, comm):
    c_in, c_out, aliases, n_sem = _comm_layout(comm, 0, 0)

    def body(*refs):
        cin, cout, (ss, rs) = refs[:len(c_in)], refs[len(c_in):len(c_in) + len(c_out)], refs[len(c_in) + len(c_out):]
        _run_tasks(comm, "start", cin, cout, ss, rs)
        _run_tasks(comm, "finish", cin, cout, ss, rs)

    return _pcall(
        body, name=name, in_specs=[ANY] * len(c_in), out_specs=[ANY] * len(c_out), out_shape=c_out,
        scratch_shapes=[pltpu.SemaphoreType.DMA((n_sem,)), pltpu.SemaphoreType.DMA((n_sem,))],
        input_output_aliases=aliases,
    )(*c_in)


def _inproj(x, g1, w_int, comm=()):
    tm = 256

    def body(x_ref, g_ref, w_ref, proj_ref, u_ref):
        xf = x_ref[...]
        r = lax.rsqrt(jnp.mean(xf * xf, axis=-1, keepdims=True) + EPS)
        u = (xf * r * g_ref[...]).astype(BF)
        u_ref[...] = u
        proj_ref[...] = _dot(u, w_ref[...], 1, 1)

    return _call(
        body, (x, g1, w_int), name="inproj", grid=(T // tm,),
        in_specs=[pl.BlockSpec((tm, D), lambda i: (i, 0)), pl.BlockSpec((1, D), lambda i: (0, 0)),
                  pl.BlockSpec((INW, D), lambda i: (0, 0))],
        out_specs=[pl.BlockSpec((tm, INW), lambda i: (i, 0)), pl.BlockSpec((tm, D), lambda i: (i, 0))],
        out_shape=[SDS((T, INW), F32), SDS((T, D), BF)], sem=("parallel",), vmem_mib=40, comm=comm)


def _outproj(y, w_out, x, g2):
    tm = 256

    def body(y_ref, w_ref, x_ref, g_ref, h1_ref, u2_ref):
        h1 = x_ref[...] + _dot(y_ref[...], w_ref[...], 1, 0)
        h1_ref[...] = h1
        r = lax.rsqrt(jnp.mean(h1 * h1, axis=-1, keepdims=True) + EPS)
        u2_ref[...] = (h1 * r * g_ref[...]).astype(BF)

    return _call(
        body, (y, w_out, x, g2), name="outproj", grid=(T // tm,),
        in_specs=[pl.BlockSpec((tm, D), lambda i: (i, 0)), pl.BlockSpec((D, D), lambda i: (0, 0)),
                  pl.BlockSpec((tm, D), lambda i: (i, 0)), pl.BlockSpec((1, D), lambda i: (0, 0))],
        out_specs=[pl.BlockSpec((tm, D), lambda i: (i, 0)), pl.BlockSpec((tm, D), lambda i: (i, 0))],
        out_shape=[SDS((T, D), F32), SDS((T, D), BF)], sem=("parallel",), vmem_mib=32)


def _ffn_up(u2, w_upt, comm=()):
    tm, tn = 1024, 512

    def body(u_ref, w_ref, o_ref):
        o_ref[...] = _dot(u_ref[...], w_ref[...], 1, 1)

    return _call(
        body, (u2, w_upt), name="ffn_up", grid=(T // tm, 2 * DFF // tn),
        in_specs=[pl.BlockSpec((tm, D), lambda i, j: (i, 0)), pl.BlockSpec((tn, D), lambda i, j: (j, 0))],
        out_specs=[pl.BlockSpec((tm, tn), lambda i, j: (i, j))], out_shape=[SDS((T, 2 * DFF), F32)],
        sem=("parallel", "parallel"), vmem_mib=32, comm=comm)


def _ffn_down(a, w_down, h1, tgt):
    tm = 256

    def body(a_ref, w_ref, h1_ref, t_ref, dh_ref, dhb_ref, l_ref):
        @pl.when(pl.program_id(0) == 0)
        def _():
            l_ref[...] = jnp.zeros_like(l_ref)

        h2 = h1_ref[...] + _dot(a_ref[...], w_ref[...], 1, 0)
        e = h2 - t_ref[...]
        dh = e * (1.0 / D)
        dh_ref[...] = dh
        dhb_ref[...] = dh.astype(BF)
        e2 = jnp.sum((e * e).reshape(tm // 8, 8, D), axis=0)
        acc = e2[:, 0:128]
        for k in range(1, D // 128):
            acc = acc + e2[:, k * 128:(k + 1) * 128]
        l_ref[...] += acc

    return _call(
        body, (a, w_down, h1, tgt), name="ffn_down", grid=(T // tm,),
        in_specs=[pl.BlockSpec((tm, DFF), lambda i: (i, 0)), pl.BlockSpec((DFF, D), lambda i: (0, 0)),
                  pl.BlockSpec((tm, D), lambda i: (i, 0)), pl.BlockSpec((tm, D), lambda i: (i, 0))],
        out_specs=[pl.BlockSpec((tm, D), lambda i: (i, 0)), pl.BlockSpec((tm, D), lambda i: (i, 0)),
                   pl.BlockSpec((8, 128), lambda i: (0, 0))],
        out_shape=[SDS((T, D), F32), SDS((T, D), BF), SDS((8, 128), F32)], sem=("arbitrary",), vmem_mib=40)


def _bucket_table():
    q = np.arange(BLK, dtype=np.int32)[:, None]
    j = np.arange(2 * BLK, dtype=np.int32)[None, :]
    n = np.maximum(q + BLK - j, 0)
    nf = np.maximum(n, 1).astype(np.float32)
    max_exact = NBUCKET // 2
    large = max_exact + (np.log(nf / np.float32(max_exact)) / np.float32(math.log(BLK / max_exact))
                         * np.float32(NBUCKET - max_exact)).astype(np.int32)
    large = np.minimum(large, NBUCKET - 1)
    return np.where(n < max_exact, n, large).astype(np.int32)


def _band_bias(table, bucket):
    def body(tab_ref, bk_ref, o_ref):
        bk = bk_ref[...]
        eq = [bk == b for b in range(NBUCKET)]
        for h in range(NH):
            acc = jnp.zeros((BLK, 2 * BLK), F32)
            for b in range(NBUCKET):
                acc = jnp.where(eq[b], tab_ref[b, h], acc)
            o_ref[h * BLK:(h + 1) * BLK, :] = acc

    return _pcall(
        body, name="band_bias", out_shape=SDS((NH * BLK, 2 * BLK), F32),
        in_specs=[pl.BlockSpec(memory_space=pltpu.SMEM), pl.BlockSpec(memory_space=pltpu.VMEM)],
        out_specs=pl.BlockSpec(memory_space=pltpu.VMEM),
    )(table, bucket)


def _band_bias_bwd(dbias, bucket, me):
    def body(me_ref, db_ref, bk_ref, o_ref):
        bk = bk_ref[...]
        for b in range(NBUCKET):
            m = bk == b
            for h in range(NH):
                v = jnp.where(m, db_ref[h * BLK:(h + 1) * BLK, :], 0.0)
                s = jnp.sum(jnp.sum(v, axis=1, keepdims=True), axis=0, keepdims=True)
                o_ref[0, b:b + 1, h:h + 1] = s

    grid_spec = pltpu.PrefetchScalarGridSpec(
        num_scalar_prefetch=1, grid=(1,),
        in_specs=[pl.BlockSpec((NH * BLK, 2 * BLK), lambda i, me_ref: (0, 0)),
                  pl.BlockSpec((BLK, 2 * BLK), lambda i, me_ref: (0, 0))],
        out_specs=pl.BlockSpec((1, NBUCKET, NH), lambda i, me_ref: (me_ref[0], 0, 0)),
    )
    return _pcall(body, name="band_bias_bwd", grid_spec=grid_spec, out_shape=SDS((N_DEV, NBUCKET, NH), F32),
                  compiler_params=_params(("arbitrary",)))(me, dbias, bucket)


def _mix_forward(P, zc8, zh8, pkv, first, cw, qg, kg, gco, gao, sink_ref, bias_ref):
    gate_b = P[:, 0:CW]
    gate_c = P[:, CW:2 * CW]
    hc = P[:, 2 * CW:3 * CW]
    z = gate_c * hc
    keep = jnp.where(first, 0.0, 1.0)
    zp = zc8 * zh8 * keep
    p1 = zp[7:8, :]
    p2 = zp[6:7, :]
    row = lax.broadcasted_iota(jnp.int32, (BLK, 1), 0)
    z1 = jnp.where(row == 0, p1, pltpu.roll(z, 1, 0))
    z2 = jnp.where(row == 0, p2, jnp.where(row == 1, p1, pltpu.roll(z, 2, 0)))
    cz = cw[0:1, :] * z2 + cw[1:2, :] * z1 + cw[2:3, :] * z
    y_conv = gate_b * cz

    scale = HD ** -0.5
    qi = lax.broadcasted_iota(jnp.int32, (GQ * BLK, 2 * BLK), 0) & (BLK - 1)
    kj = lax.broadcasted_iota(jnp.int32, (GQ * BLK, 2 * BLK), 1)
    dd = qi + BLK - kj
    first_key = jnp.where(first, BLK, 0)
    valid = (dd >= 0) & (dd < BLK) & (kj >= first_key)

    q0 = 3 * CW
    k0 = q0 + AW
    v0 = k0 + NKV * HD
    heads = []
    outs = []
    for kv in range(NKV):
        kb_raw = jnp.concatenate([pkv[:, kv * HD:(kv + 1) * HD], P[:, k0 + kv * HD:k0 + (kv + 1) * HD]], axis=0)
        rk = lax.rsqrt(jnp.mean(kb_raw * kb_raw, axis=-1, keepdims=True) + EPS)
        kb = (kb_raw * rk * kg).astype(BF)
        vb = jnp.concatenate([pkv[:, NKV * HD + kv * HD:NKV * HD + (kv + 1) * HD],
                              P[:, v0 + kv * HD:v0 + (kv + 1) * HD]], axis=0).astype(BF)
        q_raw, rq, qn = [], [], []
        for g in range(GQ):
            h = kv * GQ + g
            qh = P[:, q0 + h * HD:q0 + (h + 1) * HD]
            r = lax.rsqrt(jnp.mean(qh * qh, axis=-1, keepdims=True) + EPS)
            q_raw.append(qh)
            rq.append(r)
            qn.append(qh * r * qg)
        Q = jnp.concatenate(qn, axis=0).astype(BF)
        S = _dot(Q, kb, 1, 1) * scale + bias_ref[kv * GQ * BLK:(kv + 1) * GQ * BLK, :]
        S = jnp.where(valid, S, NEG_INF)
        sink = jnp.concatenate([jnp.full((BLK, 1), sink_ref[0, kv * GQ + g], F32) for g in range(GQ)], axis=0)
        m = jnp.maximum(jnp.max(S, axis=-1, keepdims=True), sink)
        p = jnp.exp(S - m)
        es = jnp.exp(sink - m)
        denom = jnp.sum(p, axis=-1, keepdims=True) + es
        probs = p / denom
        O = _dot(probs.astype(BF), vb, 1, 0)
        heads.append(dict(kb_raw=kb_raw, rk=rk, kb=kb, vb=vb, q_raw=q_raw, rq=rq, Q=Q, probs=probs,
                          psink=es / denom, O=O))
        outs += [O[g * BLK:(g + 1) * BLK, :] for g in range(GQ)]
    y_attn = jnp.concatenate(outs, axis=1)

    rc = lax.rsqrt(jnp.mean(y_conv * y_conv, axis=-1, keepdims=True) + EPS)
    ra = lax.rsqrt(jnp.mean(y_attn * y_attn, axis=-1, keepdims=True) + EPS)
    y = jnp.concatenate([y_conv * rc * gco, y_attn * ra * gao], axis=1)
    return dict(gate_b=gate_b, gate_c=gate_c, hc=hc, z=z, z1=z1, z2=z2, cz=cz, y_conv=y_conv, y_attn=y_attn,
                rc=rc, ra=ra, heads=heads, y=y, row=row, scale=scale)


def _mix_in_specs(blk):
    return [
        pl.BlockSpec(memory_space=pltpu.SMEM),
        pl.BlockSpec((BLK, INW), lambda s: (blk(s), 0)),
        pl.BlockSpec((8, CW), lambda s: (jnp.maximum(blk(s) * (BLK // 8) - 1, 0), 1)),
        pl.BlockSpec((8, CW), lambda s: (jnp.maximum(blk(s) * (BLK // 8) - 1, 0), 2)),
        pl.BlockSpec((BLK, 2 * NKV * HD), lambda s: (jnp.maximum(blk(s) - 1, 0), (3 * CW + AW) // (2 * NKV * HD))),
    ]


def _mix_param_specs():
    return [
        pl.BlockSpec((8, CW), lambda s: (0, 0)),
        pl.BlockSpec((1, HD), lambda s: (0, 0)),
        pl.BlockSpec((1, HD), lambda s: (0, 0)),
        pl.BlockSpec((1, CW), lambda s: (0, 0)),
        pl.BlockSpec((1, AW), lambda s: (0, 0)),
        pl.BlockSpec((NH * BLK, 2 * BLK), lambda s: (0, 0)),
    ]


def _mix_fwd(proj, sinks, cw8, qg, kg, gco, gao, bias, comm=()):
    def body(sink_ref, p_ref, zc_ref, zh_ref, pkv_ref, cw_ref, qg_ref, kg_ref, gco_ref, gao_ref, bias_ref, y_ref):
        first = pl.program_id(0) == 0
        f = _mix_forward(p_ref[...], zc_ref[...], zh_ref[...], pkv_ref[...], first, cw_ref[...], qg_ref[...],
                         kg_ref[...], gco_ref[...], gao_ref[...], sink_ref, bias_ref)
        y_ref[...] = f["y"].astype(BF)

    return _call(
        body, (sinks, proj, proj, proj, proj, cw8, qg, kg, gco, gao, bias), name="mix_fwd", grid=(NB,),
        in_specs=_mix_in_specs(lambda s: s) + _mix_param_specs(),
        out_specs=[pl.BlockSpec((BLK, D), lambda s: (s, 0))], out_shape=[SDS((T, D), BF)],
        sem=("parallel",), vmem_mib=32, comm=comm)


def _mix_bwd(proj, dy, sinks, cw8, qg, kg, gco, gao, bias, comm=()):
    def blk(s):
        return NB - 1 - s

    def body(sink_ref, p_ref, zc_ref, zh_ref, pkv_ref, dy_ref, cw_ref, qg_ref, kg_ref, gco_ref, gao_ref, bias_ref,
             dproj_ref, dcw_ref, dqg_ref, dkg_ref, dgco_ref, dgao_ref, dsink_ref, dbias_ref,
             ndcz_ref, dkc_ref, dvc_ref):
        s = pl.program_id(0)
        first = s == NB - 1

        @pl.when(s == 0)
        def _():
            for r in (dcw_ref, dqg_ref, dkg_ref, dgco_ref, dgao_ref, dsink_ref, dbias_ref, ndcz_ref, dkc_ref, dvc_ref):
                r[...] = jnp.zeros_like(r)

        cw = cw_ref[...]
        qg_v, kg_v, gco_v, gao_v = qg_ref[...], kg_ref[...], gco_ref[...], gao_ref[...]
        f = _mix_forward(p_ref[...], zc_ref[...], zh_ref[...], pkv_ref[...], first, cw, qg_v, kg_v, gco_v, gao_v,
                         sink_ref, bias_ref)
        dy = dy_ref[...]
        dyc, dgco = _rms_bwd(dy[:, 0:CW], f["y_conv"], f["rc"], gco_v)
        dya, dgao = _rms_bwd(dy[:, CW:CW + AW], f["y_attn"], f["ra"], gao_v)
        dgco_ref[...] += dgco
        dgao_ref[...] += dgao

        row = f["row"]
        dgate_b = dyc * f["cz"]
        dcz = dyc * f["gate_b"]
        dcw_ref[0:1, :] += jnp.sum(dcz * f["z2"], axis=0, keepdims=True)
        dcw_ref[1:2, :] += jnp.sum(dcz * f["z1"], axis=0, keepdims=True)
        dcw_ref[2:3, :] += jnp.sum(dcz * f["z"], axis=0, keepdims=True)
        nxt = ndcz_ref[...]
        n0 = nxt[0:1, :]
        n1 = nxt[1:2, :]
        d1 = jnp.where(row == BLK - 1, n0, pltpu.roll(dcz, BLK - 1, 0))
        d2 = jnp.where(row == BLK - 1, n1, jnp.where(row == BLK - 2, n0, pltpu.roll(dcz, BLK - 2, 0)))
        dz = cw[2:3, :] * dcz + cw[1:2, :] * d1 + cw[0:1, :] * d2
        ndcz_ref[...] = dcz[0:8, :]
        dproj_ref[:, 0:CW] = dgate_b.astype(BF)
        dproj_ref[:, CW:2 * CW] = (dz * f["hc"]).astype(BF)
        dproj_ref[:, 2 * CW:3 * CW] = (dz * f["gate_c"]).astype(BF)

        scale = f["scale"]
        lane = lax.broadcasted_iota(jnp.int32, (1, 128), 1)
        dq_cols, dk_cols, dv_cols = [], [], []
        for kv in range(NKV):
            hd = f["heads"][kv]
            dO = jnp.concatenate([dya[:, (kv * GQ + g) * HD:(kv * GQ + g + 1) * HD] for g in range(GQ)], axis=0)
            delta = jnp.sum(dO * hd["O"], axis=-1, keepdims=True)
            dOb = dO.astype(BF)
            dP = _dot(dOb, hd["vb"], 1, 1)
            dS = hd["probs"] * (dP - delta)
            dsk = hd["psink"] * delta
            for g in range(GQ):
                h = kv * GQ + g
                tot = jnp.sum(dsk[g * BLK:(g + 1) * BLK, :], axis=0, keepdims=True)
                dsink_ref[...] -= jnp.where(lane == h, tot, 0.0)
            dbias_ref[kv * GQ * BLK:(kv + 1) * GQ * BLK, :] += dS
            dSs = (dS * scale).astype(BF)
            dQ = _dot(dSs, hd["kb"], 1, 0)
            dKb = _dot(dSs, hd["Q"], 0, 0)
            dVb = _dot(hd["probs"].astype(BF), dOb, 0, 0)
            dkn = dKb[BLK:, :] + dkc_ref[:, kv * HD:(kv + 1) * HD]
            dvn = dVb[BLK:, :] + dvc_ref[:, kv * HD:(kv + 1) * HD]
            dkc_ref[:, kv * HD:(kv + 1) * HD] = dKb[:BLK, :]
            dvc_ref[:, kv * HD:(kv + 1) * HD] = dVb[:BLK, :]
            dk_raw, dkg = _rms_bwd(dkn, hd["kb_raw"][BLK:, :], hd["rk"][BLK:, :], kg_v)
            dkg_ref[...] += dkg
            dk_cols.append(dk_raw)
            dv_cols.append(dvn)
            for g in range(GQ):
                dq_raw, dqg = _rms_bwd(dQ[g * BLK:(g + 1) * BLK, :], hd["q_raw"][g], hd["rq"][g], qg_v)
                dqg_ref[...] += dqg
                dq_cols.append(dq_raw)
        dproj_ref[:, 3 * CW:INW] = jnp.concatenate(dq_cols + dk_cols + dv_cols, axis=1).astype(BF)

    small = lambda r, c: pl.BlockSpec((r, c), lambda s: (0, 0))
    return _call(
        body, (sinks, proj, proj, proj, proj, dy, cw8, qg, kg, gco, gao, bias), name="mix_bwd", grid=(NB,),
        in_specs=_mix_in_specs(blk) + [pl.BlockSpec((BLK, D), lambda s: (blk(s), 0))] + _mix_param_specs(),
        out_specs=[pl.BlockSpec((BLK, INW), lambda s: (blk(s), 0)), small(8, CW), small(1, HD), small(1, HD),
                   small(1, CW), small(1, AW), small(1, 128), small(NH * BLK, 2 * BLK)],
        out_shape=[SDS((T, INW), BF), SDS((8, CW), F32), SDS((1, HD), F32), SDS((1, HD), F32), SDS((1, CW), F32),
                   SDS((1, AW), F32), SDS((1, 128), F32), SDS((NH * BLK, 2 * BLK), F32)],
        scratch_shapes=[pltpu.VMEM((8, CW), F32), pltpu.VMEM((BLK, NKV * HD), F32), pltpu.VMEM((BLK, NKV * HD), F32)],
        sem=("arbitrary",), vmem_mib=40, comm=comm)


FT = 256
NFT = DFF // FT


def _shift_down(u, row, k):
    return jnp.where(row >= k, pltpu.roll(u, k, 0), 0.0)


def _shift_up(u, row, k):
    return jnp.where(row < T - k, pltpu.roll(u, T - k, 0), 0.0)


def _ffn_act_specs():
    return [
        pl.BlockSpec((T, FT), lambda j: (0, j)), pl.BlockSpec((T, FT), lambda j: (0, NFT + j)),
        pl.BlockSpec((8, FT), lambda j: (0, j)), pl.BlockSpec((8, FT), lambda j: (0, NFT + j)),
        pl.BlockSpec((1, FT), lambda j: (0, j)), pl.BlockSpec((1, FT), lambda j: (0, NFT + j)),
    ]


def _ffn_act(up, fw8, fb):
    def body(ug_ref, uv_ref, wg_ref, wv_ref, bg_ref, bv_ref, a_ref):
        row = lax.broadcasted_iota(jnp.int32, (T, 1), 0)

        def conv(u, w, b):
            return w[0:1, :] * _shift_down(u, row, 2) + w[1:2, :] * _shift_down(u, row, 1) + w[2:3, :] * u + b

        gp = conv(ug_ref[...], wg_ref[...], bg_ref[...])
        vp = conv(uv_ref[...], wv_ref[...], bv_ref[...])
        a_ref[...] = (gp * jax.nn.sigmoid(gp) * vp).astype(BF)

    return _call(
        body, (up, up, fw8, fw8, fb, fb), name="ffn_act", grid=(NFT,), in_specs=_ffn_act_specs(),
        out_specs=[pl.BlockSpec((T, FT), lambda j: (0, j))], out_shape=[SDS((T, DFF), BF)],
        sem=("parallel",), vmem_mib=48)


def _ffn_act_bwd(up, da, fw8, fb, comm=()):
    def body(ug_ref, uv_ref, wg_ref, wv_ref, bg_ref, bv_ref, da_ref,
             dug_ref, duv_ref, dwg_ref, dwv_ref, dbg_ref, dbv_ref):
        row = lax.broadcasted_iota(jnp.int32, (T, 1), 0)
        wg, wv = wg_ref[...], wv_ref[...]
        ug, uv = ug_ref[...], uv_ref[...]
        ug1, ug2 = _shift_down(ug, row, 1), _shift_down(ug, row, 2)
        uv1, uv2 = _shift_down(uv, row, 1), _shift_down(uv, row, 2)
        gp = wg[0:1, :] * ug2 + wg[1:2, :] * ug1 + wg[2:3, :] * ug + bg_ref[...]
        vp = wv[0:1, :] * uv2 + wv[1:2, :] * uv1 + wv[2:3, :] * uv + bv_ref[...]
        sig = jax.nn.sigmoid(gp)
        da = da_ref[...]
        dvp = da * (gp * sig)
        dgp = da * vp * (sig * (1.0 + gp * (1.0 - sig)))

        def finish(dp, u, u1, u2, w, du_ref, dw_ref, db_ref):
            db_ref[...] = jnp.sum(dp, axis=0, keepdims=True)
            dw_ref[...] = jnp.zeros_like(dw_ref)
            dw_ref[0:1, :] = jnp.sum(dp * u2, axis=0, keepdims=True)
            dw_ref[1:2, :] = jnp.sum(dp * u1, axis=0, keepdims=True)
            dw_ref[2:3, :] = jnp.sum(dp * u, axis=0, keepdims=True)
            du = w[2:3, :] * dp + w[1:2, :] * _shift_up(dp, row, 1) + w[0:1, :] * _shift_up(dp, row, 2)
            du_ref[...] = du.astype(BF)

        finish(dgp, ug, ug1, ug2, wg, dug_ref, dwg_ref, dbg_ref)
        finish(dvp, uv, uv1, uv2, wv, duv_ref, dwv_ref, dbv_ref)

    col = lambda r: pl.BlockSpec((r, FT), lambda j: (0, j))
    return _call(
        body, (up, up, fw8, fw8, fb, fb, da), name="ffn_act_bwd", grid=(NFT,),
        in_specs=_ffn_act_specs() + [pl.BlockSpec((T, FT), lambda j: (0, j))],
        out_specs=[col(T), col(T), col(8), col(8), col(1), col(1)],
        out_shape=[SDS((T, DFF), BF), SDS((T, DFF), BF), SDS((8, DFF), F32), SDS((8, DFF), F32),
                   SDS((1, DFF), F32), SDS((1, DFF), F32)],
        sem=("parallel",), vmem_mib=56, comm=comm)


def _ffn_down_bwd(dh2b, w_down, comm=()):
    tm = 256

    def body(d_ref, w_ref, o_ref):
        o_ref[...] = _dot(d_ref[...], w_ref[...], 1, 1)

    return _call(
        body, (dh2b, w_down), name="ffn_down_bwd", grid=(T // tm,),
        in_specs=[pl.BlockSpec((tm, D), lambda i: (i, 0)), pl.BlockSpec((DFF, D), lambda i: (0, 0))],
        out_specs=[pl.BlockSpec((tm, DFF), lambda i: (i, 0))], out_shape=[SDS((T, DFF), F32)],
        sem=("parallel",), vmem_mib=40, comm=comm)


def _norm_matmul_bwd(name, a_list, w_t, k_offsets, xin, g, dres, want_bf16, comm=()):
    tm = 256
    ks = [a.shape[1] for a in a_list]
    n_a = len(a_list)

    def body(*refs):
        a_refs = refs[:n_a]
        w_ref, x_ref, g_ref, r_ref = refs[n_a:n_a + 4]
        outs = refs[n_a + 4:]
        dx_ref, dg_ref = outs[0], outs[-1]

        @pl.when(pl.program_id(0) == 0)
        def _():
            dg_ref[...] = jnp.zeros_like(dg_ref)

        du = _dot(a_refs[0][...], w_ref[k_offsets[0]:k_offsets[0] + ks[0], :], 1, 0)
        for k in range(1, n_a):
            du = du + _dot(a_refs[k][...], w_ref[k_offsets[k]:k_offsets[k] + ks[k], :], 1, 0)
        x = x_ref[...]
        r = lax.rsqrt(jnp.mean(x * x, axis=-1, keepdims=True) + EPS)
        dx, dg = _rms_bwd(du, x, r, g_ref[...])
        dx = r_ref[...] + dx
        dx_ref[...] = dx
        if want_bf16:
            outs[1][...] = dx.astype(BF)
        dg_ref[...] += dg

    tile = lambda c: pl.BlockSpec((tm, c), lambda i: (i, 0))
    out_specs = [tile(D)] + ([tile(D)] if want_bf16 else []) + [pl.BlockSpec((1, D), lambda i: (0, 0))]
    out_shape = [SDS((T, D), F32)] + ([SDS((T, D), BF)] if want_bf16 else []) + [SDS((1, D), F32)]
    return _call(
        body, (*a_list, w_t, xin, g, dres), name=name, grid=(T // tm,),
        in_specs=[tile(k) for k in ks] + [pl.BlockSpec(w_t.shape, lambda i: (0, 0)), tile(D),
                                           pl.BlockSpec((1, D), lambda i: (0, 0)), tile(D)],
        out_specs=out_specs, out_shape=out_shape, sem=("arbitrary",), vmem_mib=56, comm=comm)


def _out_bwd(dh1b, w_out):
    tm = 256

    def body(d_ref, w_ref, o_ref):
        o_ref[...] = _dot(d_ref[...], w_ref[...], 1, 1)

    return _call(
        body, (dh1b, w_out), name="out_bwd", grid=(T // tm,),
        in_specs=[pl.BlockSpec((tm, D), lambda i: (i, 0)), pl.BlockSpec((D, D), lambda i: (0, 0))],
        out_specs=[pl.BlockSpec((tm, D), lambda i: (i, 0))], out_shape=[SDS((T, D), F32)],
        sem=("parallel",), vmem_mib=32)


def _wgrad(name, a_list, b):
    tm = 256
    steps = [a.shape[1] // tm for a in a_list]
    starts = [sum(steps[:k]) for k in range(len(a_list))]
    n_a = len(a_list)

    def body(*refs):
        a_refs, b_ref, o32_ref, obf_ref = refs[:n_a], refs[n_a], refs[n_a + 1], refs[n_a + 2]
        i = pl.program_id(0)
        for k in range(n_a):
            @pl.when((i >= starts[k]) & (i < starts[k] + steps[k]))
            def _(k=k):
                r = _dot(a_refs[k][...], b_ref[...], 0, 0)
                o32_ref[...] = r
                obf_ref[...] = r.astype(BF)

    def a_spec(k):
        return pl.BlockSpec((T, tm), lambda i: (0, jnp.clip(i - starts[k], 0, steps[k] - 1)))

    m_total = tm * sum(steps)
    return _call(
        body, (*a_list, b), name=name, grid=(sum(steps),),
        in_specs=[a_spec(k) for k in range(n_a)] + [pl.BlockSpec((T, D), lambda i: (0, 0))],
        out_specs=[pl.BlockSpec((tm, D), lambda i: (i, 0)), pl.BlockSpec((tm, D), lambda i: (i, 0))],
        out_shape=[SDS((m_total, D), F32), SDS((m_total, D), BF)], sem=("parallel",), vmem_mib=40)


def _chip_sum(name, g32, from_sib, core, chip):
    h = g32.shape[1]
    th = h // 2

    def body(core_ref, chip_ref, g_ref, s_ref, pbf_ref, own_ref):
        p = g_ref[0] + s_ref[0].astype(F32)
        pbf_ref[0] = p.astype(BF)

        @pl.when(pl.program_id(1) == chip_ref[0])
        def _():
            own_ref[...] = p

    grid_spec = pltpu.PrefetchScalarGridSpec(
        num_scalar_prefetch=2, grid=(h // th, N_CHIPS),
        in_specs=[pl.BlockSpec((1, th, D), lambda t, jj, core_ref, chip_ref: (2 * jj + core_ref[0], t, 0)),
                  pl.BlockSpec((1, th, D), lambda t, jj, core_ref, chip_ref: (jj, t, 0))],
        out_specs=[pl.BlockSpec((1, th, D), lambda t, jj, core_ref, chip_ref: (jj, t, 0)),
                   pl.BlockSpec((th, D), lambda t, jj, core_ref, chip_ref: (t, 0))],
    )
    return _pcall(
        body, name=name, grid_spec=grid_spec, out_shape=[SDS((N_CHIPS, h, D), BF), SDS((h, D), F32)],
        compiler_params=_params(("arbitrary", "arbitrary"), 32),
    )(core, chip, g32, from_sib)


def _final_sum(name, own, from_chips, core):
    h = own.shape[0]

    def body(core_ref, o_ref, r_ref, f_ref):
        f_ref[0] = ((o_ref[...] + r_ref[0].astype(F32)) + r_ref[1].astype(F32)) + r_ref[2].astype(F32)

    grid_spec = pltpu.PrefetchScalarGridSpec(
        num_scalar_prefetch=1, grid=(1,),
        in_specs=[pl.BlockSpec((h, D), lambda i, core_ref: (0, 0)), pl.BlockSpec((3, h, D), lambda i, core_ref: (0, 0, 0))],
        out_specs=pl.BlockSpec((1, h, D), lambda i, core_ref: (core_ref[0], 0, 0)),
    )
    return _pcall(body, name=name, grid_spec=grid_spec, out_shape=SDS((2, h, D), F32),
                  compiler_params=_params(("arbitrary",), 40))(core, own, from_chips)


def _adam_math(w, g, m, v):
    nm = ADAM_B1 * m + (1.0 - ADAM_B1) * g
    nv = ADAM_B2 * v + (1.0 - ADAM_B2) * (g * g)
    m_hat = nm / (1.0 - ADAM_B1 ** ADAM_STEP)
    v_hat = nv / (1.0 - ADAM_B2 ** ADAM_STEP)
    return -ADAM_LR * (m_hat / (jnp.sqrt(v_hat) + ADAM_EPS) + ADAM_WD * w), nm, nv


def _adamw(name, w, g, m, v, tr, comm=()):
    rows, cols = w.shape

    def body(w_ref, g_ref, m_ref, v_ref, d_ref, nm_ref, nv_ref):
        d_ref[...], nm_ref[...], nv_ref[...] = _adam_math(w_ref[...], g_ref[...], m_ref[...], v_ref[...])

    spec = pl.BlockSpec((tr, cols), lambda i: (i, 0))
    return _call(body, (w, g, m, v), name=name, grid=(rows // tr,), in_specs=[spec] * 4, out_specs=[spec] * 3,
                 out_shape=[SDS((rows, cols), F32)] * 3, sem=("parallel",), vmem_mib=32, comm=comm)


C_G1, C_G2, C_GCO, C_GAO, C_DCW, C_DQG, C_DKG, C_SINK, C_SQ = 0, 1024, 2048, 2560, 3072, 4608, 4736, 4864, 5632
P_W = C_SQ + 128


def _pack_small(me, dfwg, dfwv, dfbg, dfbv, dg1, dg2, dgco, dgao, dcw8, dqg, dkg, dsink, sq):
    def body(me_ref, dfwg_r, dfwv_r, dfbg_r, dfbv_r, dg1_r, dg2_r, dgco_r, dgao_r, dcw_r, dqg_r, dkg_r, dsink_r, sq_r, o):
        o[...] = jnp.zeros_like(o)
        o[0, :, 0:DFF] = dfwg_r[...]
        o[0, :, DFF:2 * DFF] = dfwv_r[...]
        o[0, 3:4, 0:DFF] = dfbg_r[...]
        o[0, 3:4, DFF:2 * DFF] = dfbv_r[...]
        o[0, 4:5, C_G1:C_G1 + D] = dg1_r[...]
        o[0, 4:5, C_G2:C_G2 + D] = dg2_r[...]
        o[0, 4:5, C_GCO:C_GCO + CW] = dgco_r[...]
        o[0, 4:5, C_GAO:C_GAO + AW] = dgao_r[...]
        for r in range(3):
            o[0, 4:5, C_DCW + r * CW:C_DCW + (r + 1) * CW] = dcw_r[r:r + 1, :]
        o[0, 4:5, C_DQG:C_DQG + HD] = dqg_r[...]
        o[0, 4:5, C_DKG:C_DKG + HD] = dkg_r[...]
        o[0, 4:5, C_SINK:C_SINK + 128] = dsink_r[...]
        o[0, :, C_SQ:C_SQ + 128] = sq_r[...]

    ins = (dfwg, dfwv, dfbg, dfbv, dg1, dg2, dgco, dgao, dcw8, dqg, dkg, dsink, sq)
    grid_spec = pltpu.PrefetchScalarGridSpec(
        num_scalar_prefetch=1, grid=(1,),
        in_specs=[pl.BlockSpec(a.shape, lambda i, me_ref: (0, 0)) for a in ins],
        out_specs=pl.BlockSpec((1, 8, P_W), lambda i, me_ref: (me_ref[0], 0, 0)),
    )
    return _pcall(body, name="pack_small", grid_spec=grid_spec, out_shape=SDS((N_DEV, 8, P_W), F32),
                  compiler_params=_params(("arbitrary",)))(me, *ins)


N_SMALL = 11


def _small_adam(chip, p_all, tbl_all, ws, ms, vs):
    fw_cols = 2 * DFF // N_CHIPS
    cw_cols = CW // N_CHIPS

    def body(chip_ref, p_ref, fw_ref, cw0_ref, cw1_ref, cw2_ref, tbl_ref, *refs):
        w_r, m_r, v_r = refs[0:N_SMALL], refs[N_SMALL:2 * N_SMALL], refs[2 * N_SMALL:3 * N_SMALL]
        outs = refs[3 * N_SMALL:]
        g_o, d_o, nm_o, nv_o = (outs[k * N_SMALL:(k + 1) * N_SMALL] for k in range(4))
        loss_o = outs[4 * N_SMALL]

        def total(ref):
            s = ref[0]
            for k in range(1, N_DEV):
                s = s + ref[k]
            return s

        S = total(p_ref)
        fw = total(fw_ref)
        cws = [total(r) for r in (cw0_ref, cw1_ref, cw2_ref)]

        def step(i, g, at):
            d, nm, nv = _adam_math(w_r[i][at], g, m_r[i][at], v_r[i][at])
            g_o[i][at], d_o[i][at], nm_o[i][at], nv_o[i][at] = g, d, nm, nv

        everything = (slice(None), slice(None))
        step(0, S[4:5, C_G1:C_G1 + D], everything)
        for r in range(3):
            step(1, cws[r][4:5, :], (0, slice(r, r + 1), slice(None)))
        step(2, S[4:5, C_DQG:C_DQG + HD], everything)
        step(3, S[4:5, C_DKG:C_DKG + HD], everything)
        step(4, total(tbl_ref), everything)
        step(5, S[4:5, C_SINK:C_SINK + NH], everything)
        step(6, S[4:5, C_GCO:C_GCO + CW], everything)
        step(7, S[4:5, C_GAO:C_GAO + AW], everything)
        step(8, S[4:5, C_G2:C_G2 + D], everything)
        step(9, fw[0:3, :], (0, slice(None), slice(None)))
        step(10, S[3:4, 0:2 * DFF], everything)
        sq = S[:, C_SQ:C_SQ + 128]
        loss_o[...] = jnp.sum(jnp.sum(sq, axis=1, keepdims=True), axis=0, keepdims=True) * (0.5 / D)

    def full(a):
        n = len(a.shape)
        return pl.BlockSpec(a.shape, lambda i, chip_ref: (0,) * n)

    params = [*ws, *ms, *vs]
    grid_spec = pltpu.PrefetchScalarGridSpec(
        num_scalar_prefetch=1, grid=(1,),
        in_specs=[full(p_all),
                  pl.BlockSpec((N_DEV, 8, fw_cols), lambda i, chip_ref: (0, 0, chip_ref[0])),
                  *[pl.BlockSpec((N_DEV, 8, cw_cols), lambda i, chip_ref, r=r: (0, 0, (C_DCW + r * CW) // cw_cols + chip_ref[0]))
                    for r in range(3)],
                  full(tbl_all), *[full(a) for a in params]],
        out_specs=[full(a) for a in ws] * 4 + [pl.BlockSpec((1, 1), lambda i, chip_ref: (0, 0))],
    )
    out = _pcall(
        body, name="small_adam", grid_spec=grid_spec,
        out_shape=[SDS(a.shape, F32) for a in ws] * 4 + [SDS((1, 1), F32)],
        compiler_params=_params(("arbitrary",), 32),
    )(chip, p_all, p_all, p_all, p_all, p_all, tbl_all, *params)
    return out[0:N_SMALL], out[N_SMALL:2 * N_SMALL], out[2 * N_SMALL:3 * N_SMALL], out[3 * N_SMALL:4 * N_SMALL], out[4 * N_SMALL]


def _place_weights(chip, shards, conv_w, ffn_conv_w):
    steps = 4

    def body(chip_ref, a0, a1, a2, a3, s0, s1, o0, o1, o2, o3, t0, t1):
        for a, o in ((a0, o0), (a1, o1), (a2, o2), (a3, o3)):
            o[...] = a[...].astype(BF)

        @pl.when(pl.program_id(0) == 0)
        def _():
            for s, t in ((s0, t0), (s1, t1)):
                t[...] = jnp.zeros_like(t)
                t[0, 0:3, :] = s[...]

    rows = [s.shape[0] // steps for s in shards]
    grid_spec = pltpu.PrefetchScalarGridSpec(
        num_scalar_prefetch=1, grid=(steps,),
        in_specs=[pl.BlockSpec((r, D), lambda i, chip_ref: (i, 0)) for r in rows]
        + [pl.BlockSpec(s.shape, lambda i, chip_ref: (0, 0)) for s in (conv_w, ffn_conv_w)],
        out_specs=[pl.BlockSpec((r, D), lambda i, chip_ref: (chip_ref[0] * steps + i, 0)) for r in rows]
        + [pl.BlockSpec((1, 8, s.shape[1]), lambda i, chip_ref: (chip_ref[0], 0, 0)) for s in (conv_w, ffn_conv_w)],
    )
    return _pcall(
        body, name="place_weights", grid_spec=grid_spec,
        out_shape=[SDS((N_CHIPS * s.shape[0], D), BF) for s in shards]
        + [SDS((N_CHIPS, 8, s.shape[1]), F32) for s in (conv_w, ffn_conv_w)],
        compiler_params=_params(("arbitrary",), 32),
    )(chip, *shards, conv_w, ffn_conv_w)


def kernel(x, norm_mix_g, w_in, conv_w, q_norm_g, k_norm_g, rel_bias_table, sinks, out_norm_conv_g, out_norm_attn_g, w_out, norm_ffn_g, w_up, ffn_conv_w, ffn_conv_b, w_down, loss_target, m_norm_mix_g, m_w_in, m_conv_w, m_q_norm_g, m_k_norm_g, m_rel_bias_table, m_sinks, m_out_norm_conv_g, m_out_norm_attn_g, m_w_out, m_norm_ffn_g, m_w_up, m_ffn_conv_w, m_ffn_conv_b, m_w_down, v_norm_mix_g, v_w_in, v_conv_w, v_q_norm_g, v_k_norm_g, v_rel_bias_table, v_sinks, v_out_norm_conv_g, v_out_norm_attn_g, v_w_out, v_norm_ffn_g, v_w_up, v_ffn_conv_w, v_ffn_conv_b, v_w_down):
    as_arg = lambda i: jnp.reshape(i, (1,)).astype(jnp.int32)
    chip = as_arg(2 * lax.axis_index("x") + lax.axis_index("y"))
    core = as_arg(lax.axis_index("c"))
    me = 2 * chip + core
    xs, tgt = x[0], loss_target[0]
    qg, kg, gco, gao, g1, g2, fb = q_norm_g, k_norm_g, out_norm_conv_g, out_norm_attn_g, norm_mix_g, norm_ffn_g, ffn_conv_b
    pieces = lambda g: g.reshape(N_DEV, g.shape[0] // N_DEV, D)
    whole = lambda f: f.reshape(2 * f.shape[1], D)

    p_in, p_out, p_up, p_down, p_cw, p_fw = _place_weights(chip, [w_in[0].T, w_out[0], w_up[0].T, w_down[0]], conv_w[0], ffn_conv_w[0])
    w_int, cw_all, fw_all = _comm_call("gather_first", [_t_gather(p_in), _t_small_weights(p_cw), _t_small_weights(p_fw)])
    cw8 = jnp.transpose(cw_all, (1, 0, 2)).reshape(8, CW)
    fw8 = jnp.transpose(fw_all, (1, 0, 2)).reshape(8, 2 * DFF)
    bucket = jnp.asarray(_bucket_table())
    bias = _band_bias(rel_bias_table, bucket)

    proj, u1, w_out_f = _inproj(xs, g1, w_int, comm=[_t_gather(p_out)])
    y, w_upt = _mix_fwd(proj, sinks, cw8, qg, kg, gco, gao, bias, comm=[_t_gather(p_up)])
    h1, u2 = _outproj(y, w_out_f, xs, g2)
    up, w_down_f = _ffn_up(u2, w_upt, comm=[_t_gather(p_down)])
    a, = _ffn_act(up, fw8, fb)
    dh2, dh2b, sq = _ffn_down(a, w_down_f, h1, tgt)

    gd32, gdbf = _wgrad("wgrad_down", [a], dh2b)
    da, sib_down = _ffn_down_bwd(dh2b, w_down_f, comm=[_t_sibling(pieces(gdbf))])
    pbf_down, own_down = _chip_sum("chip_sum_w_down", pieces(gd32), sib_down, core, chip)
    dug, duv, dfwg, dfwv, dfbg, dfbv, chips_down = _ffn_act_bwd(up, da, fw8, fb, comm=[_t_chips(pbf_down)])
    fin_down = _final_sum("final_sum_w_down", own_down, chips_down, core)
    gu32, gubf = _wgrad("wgrad_up", [dug, duv], u2)
    dh1, dh1b, dg2, sib_up, fin_down = _norm_matmul_bwd(
        "ffn_up_bwd", [dug, duv], w_upt, [0, DFF], h1, g2, dh2, True, comm=[_t_sibling(pieces(gubf)), _t_swap(fin_down)])
    pbf_up, own_up = _chip_sum("chip_sum_w_up", pieces(gu32), sib_up, core, chip)
    go32, gobf = _wgrad("wgrad_out", [y], dh1b)
    dy, = _out_bwd(dh1b, w_out_f)
    dproj, dcw8, dqg, dkg, dgco, dgao, dsink, dbias, chips_up, sib_out = _mix_bwd(
        proj, dy, sinks, cw8, qg, kg, gco, gao, bias, comm=[_t_chips(pbf_up), _t_sibling(pieces(gobf))])
    fin_up = _final_sum("final_sum_w_up", own_up, chips_up, core)
    pbf_out, own_out = _chip_sum("chip_sum_w_out", pieces(go32), sib_out, core, chip)
    tbl_all = _band_bias_bwd(dbias, bucket, me)
    dx, dg1, chips_out, fin_up = _norm_matmul_bwd(
        "in_bwd", [dproj], w_int, [0], xs, g1, dh1, False, comm=[_t_chips(pbf_out), _t_swap(fin_up)])
    fin_out = _final_sum("final_sum_w_out", own_out, chips_out, core)
    gi32, gibf = _wgrad("wgrad_in", [dproj], u1)
    p_all = _pack_small(me, dfwg, dfwv, dfbg, dfbv, dg1, dg2, dgco, dgao, dcw8, dqg, dkg, dsink, sq)
    sib_in, p_all, tbl_all, fin_out = _comm_call(
        "to_sibling_last", [_t_sibling(pieces(gibf)), _t_allgather(p_all), _t_allgather(tbl_all), _t_swap(fin_out)])
    pbf_in, own_in = _chip_sum("chip_sum_w_in", pieces(gi32), sib_in, core, chip)

    g_w_out, g_w_up, g_w_down = whole(fin_out), whole(fin_up).T, whole(fin_down)
    d_down, nm_down, nv_down = _adamw("adamw_w_down", w_down[0], g_w_down, m_w_down[0], v_w_down[0], 352)
    d_up, nm_up, nv_up, chips_in = _adamw("adamw_w_up", w_up[0], g_w_up, m_w_up[0], v_w_up[0], 256, comm=[_t_chips(pbf_in)])
    d_out, nm_out, nv_out = _adamw("adamw_w_out", w_out[0], g_w_out, m_w_out[0], v_w_out[0], 256)
    sw = [norm_mix_g, conv_w, q_norm_g, k_norm_g, rel_bias_table, sinks, out_norm_conv_g, out_norm_attn_g,
          norm_ffn_g, ffn_conv_w, ffn_conv_b]
    smm = [m_norm_mix_g, m_conv_w, m_q_norm_g, m_k_norm_g, m_rel_bias_table, m_sinks, m_out_norm_conv_g,
           m_out_norm_attn_g, m_norm_ffn_g, m_ffn_conv_w, m_ffn_conv_b]
    smv = [v_norm_mix_g, v_conv_w, v_q_norm_g, v_k_norm_g, v_rel_bias_table, v_sinks, v_out_norm_conv_g,
           v_out_norm_attn_g, v_norm_ffn_g, v_ffn_conv_w, v_ffn_conv_b]
    sg, sd, snm, snv, loss = _small_adam(chip, p_all, tbl_all, sw, smm, smv)
    fin_in = _final_sum("final_sum_w_in", own_in, chips_in, core)
    fin_in, = _comm_call("swap_last", [_t_swap(fin_in)])
    g_w_in = whole(fin_in).T
    d_in, nm_in, nv_in = _adamw("adamw_w_in", w_in[0], g_w_in, m_w_in[0], v_w_in[0], 256)

    def order(s, b_in, b_out, b_up, b_down):
        return (s[0], b_in[None], s[1], s[2], s[3], s[4], s[5], s[6], s[7], b_out[None], s[8], b_up[None],
                s[9], s[10], b_down[None])

    return (loss.reshape(()), dx[None],
            *order(sg, g_w_in, g_w_out, g_w_up, g_w_down),
            *order(sd, d_in, d_out, d_up, d_down),
            *order(snm, nm_in, nm_out, nm_up, nm_down),
            *order(snv, nv_in, nv_out, nv_up, nv_down))
```

```python
import functools
import math

import numpy as np

import jax
import jax.numpy as jnp
from jax import lax
from jax.experimental import pallas as pl
from jax.experimental.pallas import tpu as pltpu

F32 = jnp.float32
BF = jnp.bfloat16
SDS = jax.ShapeDtypeStruct

T = 2048
D = 1024
CW = 512
AW = 512
HD = 64
NH = 8
NKV = 2
GQ = 4
INW = 2304
DFF = 2816
BLK = 128
NB = T // BLK
NBUCKET = 32
EPS = 1e-6
NEG_INF = -1e30
N_CHIPS = 4
N_DEV = 8

ADAM_LR = 0.001
ADAM_B1 = 0.9
ADAM_B2 = 0.999
ADAM_EPS = 1e-08
ADAM_WD = 0.01
ADAM_STEP = 10

MIB = 1024 * 1024
MESH = pl.DeviceIdType.MESH
ANY = pl.BlockSpec(memory_space=pl.ANY)

_pcall = pl.pallas_call


def _params(sem=None, vmem_mib=None):
    kw = {}
    if sem is not None:
        kw["dimension_semantics"] = sem
    if vmem_mib is not None:
        kw["vmem_limit_bytes"] = vmem_mib * MIB
    return pltpu.CompilerParams(**kw)


def _dot(a, b, ca, cb):
    return lax.dot_general(a, b, (((ca,), (cb,)), ((), ())), preferred_element_type=F32)


def _rms_bwd(dy, x, r, g):
    dg = jnp.sum(dy * (x * r), axis=0, keepdims=True)
    dgx = dy * g
    dx = r * dgx - x * (r * r * r) * jnp.mean(x * dgx, axis=-1, keepdims=True)
    return dx, dg


def _where():
    x, y, c = lax.axis_index("x"), lax.axis_index("y"), lax.axis_index("c")
    return x, y, c, [(1 - x, y), (x, 1 - y), (1 - x, 1 - y)]


def _rcopy(src, dst, ssem, rsem, dev):
    return pltpu.make_async_remote_copy(src_ref=src, dst_ref=dst, send_sem=ssem, recv_sem=rsem, device_id=dev,
                                        device_id_type=MESH)


class _Task:
    def __init__(self, ins, outs, alias, n_sem, start, finish):
        self.ins, self.outs, self.alias, self.n_sem, self.start, self.finish = ins, outs, alias, n_sem, start, finish


def _t_gather(placed):
    R = placed.shape[0] // N_CHIPS

    def rows(chip_index, core):
        return pl.ds(pl.multiple_of(chip_index * R + core * (R // 2), 16), R // 2)

    def start(cin, cout, ss, rs, b):
        x, y, c, chips = _where()
        mine = cout[0].at[rows(2 * x + y, c)]
        for r, (px, py) in enumerate(chips):
            _rcopy(mine, mine, ss.at[b + r], rs.at[b + r], (px, py, c)).start()

    def finish(cin, cout, ss, rs, b):
        x, y, c, chips = _where()
        buf = cout[0]
        sib = (x, y, 1 - c)
        for r, (px, py) in enumerate(chips):
            got = buf.at[rows(2 * px + py, c)]
            _rcopy(got, got, ss.at[b + r], rs.at[b + r], (px, py, c)).wait_recv()
            _rcopy(got, got, ss.at[b + 3 + r], rs.at[b + 3 + r], sib).start()
        for r, (px, py) in enumerate(chips):
            got = buf.at[rows(2 * px + py, 1 - c)]
            _rcopy(got, got, ss.at[b + 3 + r], rs.at[b + 3 + r], sib).wait_recv()
        mine = buf.at[rows(2 * x + y, c)]
        for r in range(6):
            _rcopy(mine, mine, ss.at[b + r], rs.at[b + r], sib).wait_send()

    return _Task([placed], [SDS(placed.shape, placed.dtype)], [(0, 0)], 6, start, finish)


def _t_small_weights(buf):
    def start(cin, cout, ss, rs, b):
        x, y, c, chips = _where()
        mine = cout[0].at[2 * x + y]
        for r, (px, py) in enumerate(chips):
            _rcopy(mine, mine, ss.at[b + r], rs.at[b + r], (px, py, c)).start()

    def finish(cin, cout, ss, rs, b):
        x, y, c, chips = _where()
        for r, (px, py) in enumerate(chips):
            got = cout[0].at[2 * px + py]
            _rcopy(got, got, ss.at[b + r], rs.at[b + r], (px, py, c)).wait_recv()
        for r, (px, py) in enumerate(chips):
            mine = cout[0].at[2 * x + y]
            _rcopy(mine, mine, ss.at[b + r], rs.at[b + r], (px, py, c)).wait_send()

    return _Task([buf], [SDS(buf.shape, buf.dtype)], [(0, 0)], 3, start, finish)


def _t_sibling(gbf):
    def start(cin, cout, ss, rs, b):
        x, y, c, _ = _where()
        for jj in range(N_CHIPS):
            _rcopy(cin[0].at[2 * jj + (1 - c)], cout[0].at[jj], ss.at[b + jj], rs.at[b + jj], (x, y, 1 - c)).start()

    def finish(cin, cout, ss, rs, b):
        x, y, c, _ = _where()
        for jj in range(N_CHIPS):
            got = cout[0].at[jj]
            _rcopy(got, got, ss.at[b + jj], rs.at[b + jj], (x, y, 1 - c)).wait_recv()
        for jj in range(N_CHIPS):
            got = cout[0].at[jj]
            _rcopy(got, got, ss.at[b + jj], rs.at[b + jj], (x, y, 1 - c)).wait_send()

    return _Task([gbf], [SDS((N_CHIPS,) + gbf.shape[1:], BF)], [], N_CHIPS, start, finish)


def _t_chips(pbf):
    def start(cin, cout, ss, rs, b):
        x, y, c, chips = _where()
        for r, (px, py) in enumerate(chips):
            _rcopy(cin[0].at[2 * px + py], cout[0].at[r], ss.at[b + r], rs.at[b + r], (px, py, c)).start()

    def finish(cin, cout, ss, rs, b):
        x, y, c, chips = _where()
        for r, (px, py) in enumerate(chips):
            got = cout[0].at[r]
            _rcopy(got, got, ss.at[b + r], rs.at[b + r], (px, py, c)).wait_recv()
        for r, (px, py) in enumerate(chips):
            got = cout[0].at[r]
            _rcopy(got, got, ss.at[b + r], rs.at[b + r], (px, py, c)).wait_send()

    return _Task([pbf], [SDS((3,) + pbf.shape[1:], BF)], [], 3, start, finish)


def _t_swap(fin):
    def start(cin, cout, ss, rs, b):
        x, y, c, _ = _where()
        mine = cout[0].at[c]
        _rcopy(mine, mine, ss.at[b], rs.at[b], (x, y, 1 - c)).start()

    def finish(cin, cout, ss, rs, b):
        x, y, c, _ = _where()
        got = cout[0].at[1 - c]
        _rcopy(got, got, ss.at[b], rs.at[b], (x, y, 1 - c)).wait_recv()
        _rcopy(got, got, ss.at[b], rs.at[b], (x, y, 1 - c)).wait_send()

    return _Task([fin], [SDS(fin.shape, fin.dtype)], [(0, 0)], 1, start, finish)


def _t_allgather(buf):
    def peers():
        x, y, c, _ = _where()
        out = []
        for rel in range(1, N_DEV):
            px, py, pc = x ^ ((rel >> 2) & 1), y ^ ((rel >> 1) & 1), c ^ (rel & 1)
            out.append((rel - 1, 4 * px + 2 * py + pc, (px, py, pc)))
        return 4 * x + 2 * y + c, out

    def start(cin, cout, ss, rs, b):
        me, ps = peers()
        mine = cout[0].at[me]
        for k, _, dev in ps:
            _rcopy(mine, mine, ss.at[b + k], rs.at[b + k], dev).start()

    def finish(cin, cout, ss, rs, b):
        me, ps = peers()
        for k, pidx, dev in ps:
            got = cout[0].at[pidx]
            _rcopy(got, got, ss.at[b + k], rs.at[b + k], dev).wait_recv()
        for k, _, dev in ps:
            mine = cout[0].at[me]
            _rcopy(mine, mine, ss.at[b + k], rs.at[b + k], dev).wait_send()

    return _Task([buf], [SDS(buf.shape, buf.dtype)], [(0, 0)], N_DEV - 1, start, finish)


def _run_tasks(comm, which, cin, cout, ss, rs):
    i0 = o0 = s0 = 0
    for t in comm:
        getattr(t, which)(cin[i0:i0 + len(t.ins)], cout[o0:o0 + len(t.outs)], ss, rs, s0)
        i0, o0, s0 = i0 + len(t.ins), o0 + len(t.outs), s0 + t.n_sem


def _from_hbm(*arrays):
    return [pltpu.with_memory_space_constraint(a, pltpu.HBM) for a in arrays]


def _in_hbm(shapes):
    return [pltpu.HBM(s.shape, s.dtype) for s in shapes]


def _comm_layout(comm, n_in, n_out):
    c_in = [a for t in comm for a in t.ins]
    c_out = [s for t in comm for s in t.outs]
    aliases, i0, o0 = {}, 0, 0
    for t in comm:
        for i, o in t.alias:
            aliases[n_in + i0 + i] = n_out + o0 + o
        i0, o0 = i0 + len(t.ins), o0 + len(t.outs)
    return c_in, c_out, aliases, sum(t.n_sem for t in comm)


def _call(body, operands, *, name, grid, in_specs, out_specs, out_shape, scratch_shapes=(), sem=None, vmem_mib=None, comm=()):
    operands = [o if s.memory_space == pltpu.SMEM else pltpu.with_memory_space_constraint(o, pltpu.HBM)
                for o, s in zip(operands, in_specs)]
    out_shape = _in_hbm(out_shape)
    if not comm:
        return _pcall(body, name=name, grid=grid, in_specs=in_specs, out_specs=out_specs, out_shape=out_shape,
                      scratch_shapes=list(scratch_shapes), compiler_params=_params(sem, vmem_mib))(*operands)
    n_in, n_out, n_scr = len(in_specs), len(out_specs), len(scratch_shapes)
    c_in, c_out, aliases, n_sem = _comm_layout(comm, n_in, n_out)

    def wrapped(*refs):
        ins, cin = refs[:n_in], refs[n_in:n_in + len(c_in)]
        rest = refs[n_in + len(c_in):]
        outs, cout = rest[:n_out], rest[n_out:n_out + len(c_out)]
        rest = rest[n_out + len(c_out):]
        scr, (ss, rs) = rest[:n_scr], rest[n_scr:]
        ids = [pl.program_id(k) for k in range(len(grid))]
        first = functools.reduce(jnp.logical_and, [i == 0 for i in ids])
        last = functools.reduce(jnp.logical_and, [i == n - 1 for i, n in zip(ids, grid)])
        pl.when(first)(lambda: _run_tasks(comm, "start", cin, cout, ss, rs))
        body(*ins, *outs, *scr)
        pl.when(last)(lambda: _run_tasks(comm, "finish", cin, cout, ss, rs))

    return _pcall(
        wrapped, name=name, grid=grid, in_specs=list(in_specs) + [ANY] * len(c_in),
        out_specs=list(out_specs) + [ANY] * len(c_out), out_shape=list(out_shape) + _in_hbm(c_out),
        scratch_shapes=list(scratch_shapes) + [pltpu.SemaphoreType.DMA((n_sem,)), pltpu.SemaphoreType.DMA((n_sem,))],
        input_output_aliases=aliases, compiler_params=_params(("arbitrary",) * len(grid), vmem_mib),
    )(*operands, *_from_hbm(*c_in))


def _comm_call(name, comm):
    c_in, c_out, aliases, n_sem = _comm_layout(comm, 0, 0)

    def body(*refs):
        cin, cout, (ss, rs) = refs[:len(c_in)], refs[len(c_in):len(c_in) + len(c_out)], refs[len(c_in) + len(c_out):]
        _run_tasks(comm, "start", cin, cout, ss, rs)
        _run_tasks(comm, "finish", cin, cout, ss, rs)

    return _pcall(
        body, name=name, in_specs=[ANY] * len(c_in), out_specs=[ANY] * len(c_out), out_shape=_in_hbm(c_out),
        scratch_shapes=[pltpu.SemaphoreType.DMA((n_sem,)), pltpu.SemaphoreType.DMA((n_sem,))],
        input_output_aliases=aliases,
    )(*_from_hbm(*c_in))


def _inproj(x, g1, w_int, comm=()):
    tm = 256

    def body(x_ref, g_ref, w_ref, proj_ref, u_ref):
        xf = x_ref[...]
        r = lax.rsqrt(jnp.mean(xf * xf, axis=-1, keepdims=True) + EPS)
        u = (xf * r * g_ref[...]).astype(BF)
        u_ref[...] = u
        proj_ref[...] = _dot(u, w_ref[...], 1, 1)

    return _call(
        body, (x, g1, w_int), name="inproj", grid=(T // tm,),
        in_specs=[pl.BlockSpec((tm, D), lambda i: (i, 0)), pl.BlockSpec((1, D), lambda i: (0, 0)),
                  pl.BlockSpec((INW, D), lambda i: (0, 0))],
        out_specs=[pl.BlockSpec((tm, INW), lambda i: (i, 0)), pl.BlockSpec((tm, D), lambda i: (i, 0))],
        out_shape=[SDS((T, INW), F32), SDS((T, D), BF)], sem=("parallel",), vmem_mib=40, comm=comm)


def _outproj(y, w_out, x, g2):
    tm = 256

    def body(y_ref, w_ref, x_ref, g_ref, h1_ref, u2_ref):
        h1 = x_ref[...] + _dot(y_ref[...], w_ref[...], 1, 0)
        h1_ref[...] = h1
        r = lax.rsqrt(jnp.mean(h1 * h1, axis=-1, keepdims=True) + EPS)
        u2_ref[...] = (h1 * r * g_ref[...]).astype(BF)

    return _call(
        body, (y, w_out, x, g2), name="outproj", grid=(T // tm,),
        in_specs=[pl.BlockSpec((tm, D), lambda i: (i, 0)), pl.BlockSpec((D, D), lambda i: (0, 0)),
                  pl.BlockSpec((tm, D), lambda i: (i, 0)), pl.BlockSpec((1, D), lambda i: (0, 0))],
        out_specs=[pl.BlockSpec((tm, D), lambda i: (i, 0)), pl.BlockSpec((tm, D), lambda i: (i, 0))],
        out_shape=[SDS((T, D), F32), SDS((T, D), BF)], sem=("parallel",), vmem_mib=32)


def _ffn_up(u2, w_upt, comm=()):
    tm, tn = 1024, 512

    def body(u_ref, w_ref, o_ref):
        o_ref[...] = _dot(u_ref[...], w_ref[...], 1, 1)

    return _call(
        body, (u2, w_upt), name="ffn_up", grid=(T // tm, 2 * DFF // tn),
        in_specs=[pl.BlockSpec((tm, D), lambda i, j: (i, 0)), pl.BlockSpec((tn, D), lambda i, j: (j, 0))],
        out_specs=[pl.BlockSpec((tm, tn), lambda i, j: (i, j))], out_shape=[SDS((T, 2 * DFF), F32)],
        sem=("parallel", "parallel"), vmem_mib=32, comm=comm)


def _ffn_down(a, w_down, h1, tgt):
    tm = 256

    def body(a_ref, w_ref, h1_ref, t_ref, dh_ref, dhb_ref, l_ref):
        @pl.when(pl.program_id(0) == 0)
        def _():
            l_ref[...] = jnp.zeros_like(l_ref)

        h2 = h1_ref[...] + _dot(a_ref[...], w_ref[...], 1, 0)
        e = h2 - t_ref[...]
        dh = e * (1.0 / D)
        dh_ref[...] = dh
        dhb_ref[...] = dh.astype(BF)
        e2 = jnp.sum((e * e).reshape(tm // 8, 8, D), axis=0)
        acc = e2[:, 0:128]
        for k in range(1, D // 128):
            acc = acc + e2[:, k * 128:(k + 1) * 128]
        l_ref[...] += acc

    return _call(
        body, (a, w_down, h1, tgt), name="ffn_down", grid=(T // tm,),
        in_specs=[pl.BlockSpec((tm, DFF), lambda i: (i, 0)), pl.BlockSpec((DFF, D), lambda i: (0, 0)),
                  pl.BlockSpec((tm, D), lambda i: (i, 0)), pl.BlockSpec((tm, D), lambda i: (i, 0))],
        out_specs=[pl.BlockSpec((tm, D), lambda i: (i, 0)), pl.BlockSpec((tm, D), lambda i: (i, 0)),
                   pl.BlockSpec((8, 128), lambda i: (0, 0))],
        out_shape=[SDS((T, D), F32), SDS((T, D), BF), SDS((8, 128), F32)], sem=("arbitrary",), vmem_mib=40)


def _bucket_table():
    q = np.arange(BLK, dtype=np.int32)[:, None]
    j = np.arange(2 * BLK, dtype=np.int32)[None, :]
    n = np.maximum(q + BLK - j, 0)
    nf = np.maximum(n, 1).astype(np.float32)
    max_exact = NBUCKET // 2
    large = max_exact + (np.log(nf / np.float32(max_exact)) / np.float32(math.log(BLK / max_exact))
                         * np.float32(NBUCKET - max_exact)).astype(np.int32)
    large = np.minimum(large, NBUCKET - 1)
    return np.where(n < max_exact, n, large).astype(np.int32)


def _band_bias(table, bucket):
    def body(tab_ref, bk_ref, o_ref):
        bk = bk_ref[...]
        eq = [bk == b for b in range(NBUCKET)]
        for h in range(NH):
            acc = jnp.zeros((BLK, 2 * BLK), F32)
            for b in range(NBUCKET):
                acc = jnp.where(eq[b], tab_ref[b, h], acc)
            o_ref[h * BLK:(h + 1) * BLK, :] = acc

    return _pcall(
        body, name="band_bias", out_shape=SDS((NH * BLK, 2 * BLK), F32),
        in_specs=[pl.BlockSpec(memory_space=pltpu.SMEM), pl.BlockSpec(memory_space=pltpu.VMEM)],
        out_specs=pl.BlockSpec(memory_space=pltpu.VMEM),
    )(table, bucket)


def _band_bias_bwd(dbias, bucket, me):
    def body(me_ref, db_ref, bk_ref, o_ref):
        bk = bk_ref[...]
        for b in range(NBUCKET):
            m = bk == b
            for h in range(NH):
                v = jnp.where(m, db_ref[h * BLK:(h + 1) * BLK, :], 0.0)
                s = jnp.sum(jnp.sum(v, axis=1, keepdims=True), axis=0, keepdims=True)
                o_ref[0, b:b + 1, h:h + 1] = s

    grid_spec = pltpu.PrefetchScalarGridSpec(
        num_scalar_prefetch=1, grid=(1,),
        in_specs=[pl.BlockSpec((NH * BLK, 2 * BLK), lambda i, me_ref: (0, 0)),
                  pl.BlockSpec((BLK, 2 * BLK), lambda i, me_ref: (0, 0))],
        out_specs=pl.BlockSpec((1, NBUCKET, NH), lambda i, me_ref: (me_ref[0], 0, 0)),
    )
    return _pcall(body, name="band_bias_bwd", grid_spec=grid_spec, out_shape=SDS((N_DEV, NBUCKET, NH), F32),
                  compiler_params=_params(("arbitrary",)))(me, dbias, bucket)


def _mix_forward(P, zc8, zh8, pkv, first, cw, qg, kg, gco, gao, sink_ref, bias_ref):
    gate_b = P[:, 0:CW]
    gate_c = P[:, CW:2 * CW]
    hc = P[:, 2 * CW:3 * CW]
    z = gate_c * hc
    keep = jnp.where(first, 0.0, 1.0)
    zp = zc8 * zh8 * keep
    p1 = zp[7:8, :]
    p2 = zp[6:7, :]
    row = lax.broadcasted_iota(jnp.int32, (BLK, 1), 0)
    z1 = jnp.where(row == 0, p1, pltpu.roll(z, 1, 0))
    z2 = jnp.where(row == 0, p2, jnp.where(row == 1, p1, pltpu.roll(z, 2, 0)))
    cz = cw[0:1, :] * z2 + cw[1:2, :] * z1 + cw[2:3, :] * z
    y_conv = gate_b * cz

    scale = HD ** -0.5
    qi = lax.broadcasted_iota(jnp.int32, (GQ * BLK, 2 * BLK), 0) & (BLK - 1)
    kj = lax.broadcasted_iota(jnp.int32, (GQ * BLK, 2 * BLK), 1)
    dd = qi + BLK - kj
    first_key = jnp.where(first, BLK, 0)
    valid = (dd >= 0) & (dd < BLK) & (kj >= first_key)

    q0 = 3 * CW
    k0 = q0 + AW
    v0 = k0 + NKV * HD
    heads = []
    outs = []
    for kv in range(NKV):
        kb_raw = jnp.concatenate([pkv[:, kv * HD:(kv + 1) * HD], P[:, k0 + kv * HD:k0 + (kv + 1) * HD]], axis=0)
        rk = lax.rsqrt(jnp.mean(kb_raw * kb_raw, axis=-1, keepdims=True) + EPS)
        kb = (kb_raw * rk * kg).astype(BF)
        vb = jnp.concatenate([pkv[:, NKV * HD + kv * HD:NKV * HD + (kv + 1) * HD],
                              P[:, v0 + kv * HD:v0 + (kv + 1) * HD]], axis=0).astype(BF)
        q_raw, rq, qn = [], [], []
        for g in range(GQ):
            h = kv * GQ + g
            qh = P[:, q0 + h * HD:q0 + (h + 1) * HD]
            r = lax.rsqrt(jnp.mean(qh * qh, axis=-1, keepdims=True) + EPS)
            q_raw.append(qh)
            rq.append(r)
            qn.append(qh * r * qg)
        Q = jnp.concatenate(qn, axis=0).astype(BF)
        S = _dot(Q, kb, 1, 1) * scale + bias_ref[kv * GQ * BLK:(kv + 1) * GQ * BLK, :]
        S = jnp.where(valid, S, NEG_INF)
        sink = jnp.concatenate([jnp.full((BLK, 1), sink_ref[0, kv * GQ + g], F32) for g in range(GQ)], axis=0)
        m = jnp.maximum(jnp.max(S, axis=-1, keepdims=True), sink)
        p = jnp.exp(S - m)
        es = jnp.exp(sink - m)
        denom = jnp.sum(p, axis=-1, keepdims=True) + es
        probs = p / denom
        O = _dot(probs.astype(BF), vb, 1, 0)
        heads.append(dict(kb_raw=kb_raw, rk=rk, kb=kb, vb=vb, q_raw=q_raw, rq=rq, Q=Q, probs=probs,
                          psink=es / denom, O=O))
        outs += [O[g * BLK:(g + 1) * BLK, :] for g in range(GQ)]
    y_attn = jnp.concatenate(outs, axis=1)

    rc = lax.rsqrt(jnp.mean(y_conv * y_conv, axis=-1, keepdims=True) + EPS)
    ra = lax.rsqrt(jnp.mean(y_attn * y_attn, axis=-1, keepdims=True) + EPS)
    y = jnp.concatenate([y_conv * rc * gco, y_attn * ra * gao], axis=1)
    return dict(gate_b=gate_b, gate_c=gate_c, hc=hc, z=z, z1=z1, z2=z2, cz=cz, y_conv=y_conv, y_attn=y_attn,
                rc=rc, ra=ra, heads=heads, y=y, row=row, scale=scale)


def _mix_in_specs(blk):
    return [
        pl.BlockSpec(memory_space=pltpu.SMEM),
        pl.BlockSpec((BLK, INW), lambda s: (blk(s), 0)),
        pl.BlockSpec((8, CW), lambda s: (jnp.maximum(blk(s) * (BLK // 8) - 1, 0), 1)),
        pl.BlockSpec((8, CW), lambda s: (jnp.maximum(blk(s) * (BLK // 8) - 1, 0), 2)),
        pl.BlockSpec((BLK, 2 * NKV * HD), lambda s: (jnp.maximum(blk(s) - 1, 0), (3 * CW + AW) // (2 * NKV * HD))),
    ]


def _mix_param_specs():
    return [
        pl.BlockSpec((8, CW), lambda s: (0, 0)),
        pl.BlockSpec((1, HD), lambda s: (0, 0)),
        pl.BlockSpec((1, HD), lambda s: (0, 0)),
        pl.BlockSpec((1, CW), lambda s: (0, 0)),
        pl.BlockSpec((1, AW), lambda s: (0, 0)),
        pl.BlockSpec((NH * BLK, 2 * BLK), lambda s: (0, 0)),
    ]


def _mix_fwd(proj, sinks, cw8, qg, kg, gco, gao, bias, comm=()):
    def body(sink_ref, p_ref, zc_ref, zh_ref, pkv_ref, cw_ref, qg_ref, kg_ref, gco_ref, gao_ref, bias_ref, y_ref):
        first = pl.program_id(0) == 0
        f = _mix_forward(p_ref[...], zc_ref[...], zh_ref[...], pkv_ref[...], first, cw_ref[...], qg_ref[...],
                         kg_ref[...], gco_ref[...], gao_ref[...], sink_ref, bias_ref)
        y_ref[...] = f["y"].astype(BF)

    return _call(
        body, (sinks, proj, proj, proj, proj, cw8, qg, kg, gco, gao, bias), name="mix_fwd", grid=(NB,),
        in_specs=_mix_in_specs(lambda s: s) + _mix_param_specs(),
        out_specs=[pl.BlockSpec((BLK, D), lambda s: (s, 0))], out_shape=[SDS((T, D), BF)],
        sem=("parallel",), vmem_mib=32, comm=comm)


def _mix_bwd(proj, dy, sinks, cw8, qg, kg, gco, gao, bias, comm=()):
    def blk(s):
        return NB - 1 - s

    def body(sink_ref, p_ref, zc_ref, zh_ref, pkv_ref, dy_ref, cw_ref, qg_ref, kg_ref, gco_ref, gao_ref, bias_ref,
             dproj_ref, dcw_ref, dqg_ref, dkg_ref, dgco_ref, dgao_ref, dsink_ref, dbias_ref,
             ndcz_ref, dkc_ref, dvc_ref):
        s = pl.program_id(0)
        first = s == NB - 1

        @pl.when(s == 0)
        def _():
            for r in (dcw_ref, dqg_ref, dkg_ref, dgco_ref, dgao_ref, dsink_ref, dbias_ref, ndcz_ref, dkc_ref, dvc_ref):
                r[...] = jnp.zeros_like(r)

        cw = cw_ref[...]
        qg_v, kg_v, gco_v, gao_v = qg_ref[...], kg_ref[...], gco_ref[...], gao_ref[...]
        f = _mix_forward(p_ref[...], zc_ref[...], zh_ref[...], pkv_ref[...], first, cw, qg_v, kg_v, gco_v, gao_v,
                         sink_ref, bias_ref)
        dy = dy_ref[...]
        dyc, dgco = _rms_bwd(dy[:, 0:CW], f["y_conv"], f["rc"], gco_v)
        dya, dgao = _rms_bwd(dy[:, CW:CW + AW], f["y_attn"], f["ra"], gao_v)
        dgco_ref[...] += dgco
        dgao_ref[...] += dgao

        row = f["row"]
        dgate_b = dyc * f["cz"]
        dcz = dyc * f["gate_b"]
        dcw_ref[0:1, :] += jnp.sum(dcz * f["z2"], axis=0, keepdims=True)
        dcw_ref[1:2, :] += jnp.sum(dcz * f["z1"], axis=0, keepdims=True)
        dcw_ref[2:3, :] += jnp.sum(dcz * f["z"], axis=0, keepdims=True)
        nxt = ndcz_ref[...]
        n0 = nxt[0:1, :]
        n1 = nxt[1:2, :]
        d1 = jnp.where(row == BLK - 1, n0, pltpu.roll(dcz, BLK - 1, 0))
        d2 = jnp.where(row == BLK - 1, n1, jnp.where(row == BLK - 2, n0, pltpu.roll(dcz, BLK - 2, 0)))
        dz = cw[2:3, :] * dcz + cw[1:2, :] * d1 + cw[0:1, :] * d2
        ndcz_ref[...] = dcz[0:8, :]
        dproj_ref[:, 0:CW] = dgate_b.astype(BF)
        dproj_ref[:, CW:2 * CW] = (dz * f["hc"]).astype(BF)
        dproj_ref[:, 2 * CW:3 * CW] = (dz * f["gate_c"]).astype(BF)

        scale = f["scale"]
        lane = lax.broadcasted_iota(jnp.int32, (1, 128), 1)
        dq_cols, dk_cols, dv_cols = [], [], []
        for kv in range(NKV):
            hd = f["heads"][kv]
            dO = jnp.concatenate([dya[:, (kv * GQ + g) * HD:(kv * GQ + g + 1) * HD] for g in range(GQ)], axis=0)
            delta = jnp.sum(dO * hd["O"], axis=-1, keepdims=True)
            dOb = dO.astype(BF)
            dP = _dot(dOb, hd["vb"], 1, 1)
            dS = hd["probs"] * (dP - delta)
            dsk = hd["psink"] * delta
            for g in range(GQ):
                h = kv * GQ + g
                tot = jnp.sum(dsk[g * BLK:(g + 1) * BLK, :], axis=0, keepdims=True)
                dsink_ref[...] -= jnp.where(lane == h, tot, 0.0)
            dbias_ref[kv * GQ * BLK:(kv + 1) * GQ * BLK, :] += dS
            dSs = (dS * scale).astype(BF)
            dQ = _dot(dSs, hd["kb"], 1, 0)
            dKb = _dot(dSs, hd["Q"], 0, 0)
            dVb = _dot(hd["probs"].astype(BF), dOb, 0, 0)
            dkn = dKb[BLK:, :] + dkc_ref[:, kv * HD:(kv + 1) * HD]
            dvn = dVb[BLK:, :] + dvc_ref[:, kv * HD:(kv + 1) * HD]
            dkc_ref[:, kv * HD:(kv + 1) * HD] = dKb[:BLK, :]
            dvc_ref[:, kv * HD:(kv + 1) * HD] = dVb[:BLK, :]
            dk_raw, dkg = _rms_bwd(dkn, hd["kb_raw"][BLK:, :], hd["rk"][BLK:, :], kg_v)
            dkg_ref[...] += dkg
            dk_cols.append(dk_raw)
            dv_cols.append(dvn)
            for g in range(GQ):
                dq_raw, dqg = _rms_bwd(dQ[g * BLK:(g + 1) * BLK, :], hd["q_raw"][g], hd["rq"][g], qg_v)
                dqg_ref[...] += dqg
                dq_cols.append(dq_raw)
        dproj_ref[:, 3 * CW:INW] = jnp.concatenate(dq_cols + dk_cols + dv_cols, axis=1).astype(BF)

    small = lambda r, c: pl.BlockSpec((r, c), lambda s: (0, 0))
    return _call(
        body, (sinks, proj, proj, proj, proj, dy, cw8, qg, kg, gco, gao, bias), name="mix_bwd", grid=(NB,),
        in_specs=_mix_in_specs(blk) + [pl.BlockSpec((BLK, D), lambda s: (blk(s), 0))] + _mix_param_specs(),
        out_specs=[pl.BlockSpec((BLK, INW), lambda s: (blk(s), 0)), small(8, CW), small(1, HD), small(1, HD),
                   small(1, CW), small(1, AW), small(1, 128), small(NH * BLK, 2 * BLK)],
        out_shape=[SDS((T, INW), BF), SDS((8, CW), F32), SDS((1, HD), F32), SDS((1, HD), F32), SDS((1, CW), F32),
                   SDS((1, AW), F32), SDS((1, 128), F32), SDS((NH * BLK, 2 * BLK), F32)],
        scratch_shapes=[pltpu.VMEM((8, CW), F32), pltpu.VMEM((BLK, NKV * HD), F32), pltpu.VMEM((BLK, NKV * HD), F32)],
        sem=("arbitrary",), vmem_mib=40, comm=comm)


FT = 256
NFT = DFF // FT
RC = 64
NCH = T // RC


def _rows8(x):
    return jnp.sum(x.reshape(x.shape[0] // 8, 8, x.shape[1]), axis=0)


def _ffn_act_specs():
    return [
        pl.BlockSpec((T, FT), lambda j: (0, j)), pl.BlockSpec((T, FT), lambda j: (0, NFT + j)),
        pl.BlockSpec((8, FT), lambda j: (0, j)), pl.BlockSpec((8, FT), lambda j: (0, NFT + j)),
        pl.BlockSpec((1, FT), lambda j: (0, j)), pl.BlockSpec((1, FT), lambda j: (0, NFT + j)),
    ]


def _conv_rows(win, w, b, n):
    u = win[8:8 + n]
    u1 = pltpu.roll(win, 1, 0)[8:8 + n]
    u2 = pltpu.roll(win, 2, 0)[8:8 + n]
    return u2, u1, u, w[0:1, :] * u2 + w[1:2, :] * u1 + w[2:3, :] * u + b


def _ffn_act(up, fw8, fb):
    def body(ug_ref, uv_ref, wg_ref, wv_ref, bg_ref, bv_ref, a_ref):
        wg, wv, bg, bv = wg_ref[...], wv_ref[...], bg_ref[...], bv_ref[...]

        def chunk(win_g, win_v):
            gp = _conv_rows(win_g, wg, bg, RC)[3]
            vp = _conv_rows(win_v, wv, bv, RC)[3]
            return (gp * jax.nn.sigmoid(gp) * vp).astype(BF)

        zero = jnp.zeros((8, FT), F32)
        a_ref[0:RC, :] = chunk(jnp.concatenate([zero, ug_ref[0:RC, :]], axis=0),
                               jnp.concatenate([zero, uv_ref[0:RC, :]], axis=0))

        def step(i, carry):
            r0 = pl.multiple_of(i * RC, RC)
            win = pl.ds(r0 - 8, RC + 8)
            a_ref[pl.ds(r0, RC), :] = chunk(ug_ref[win, :], uv_ref[win, :])
            return carry

        lax.fori_loop(1, NCH, step, 0)

    return _call(
        body, (up, up, fw8, fw8, fb, fb), name="ffn_act", grid=(NFT,), in_specs=_ffn_act_specs(),
        out_specs=[pl.BlockSpec((T, FT), lambda j: (0, j))], out_shape=[SDS((T, DFF), BF)],
        sem=("parallel",), vmem_mib=40)


def _ffn_act_bwd(up, da, fw8, fb, comm=()):
    ext = RC + 8

    def body(ug_ref, uv_ref, wg_ref, wv_ref, bg_ref, bv_ref, da_ref,
             dug_ref, duv_ref, dwg_ref, dwv_ref, dbg_ref, dbv_ref):
        wg, wv, bg, bv = wg_ref[...], wv_ref[...], bg_ref[...], bv_ref[...]

        def chunk(win_g, win_v, da_e):
            g2, g1, g0, gp = _conv_rows(win_g, wg, bg, ext)
            v2, v1, v0, vp = _conv_rows(win_v, wv, bv, ext)
            sig = jax.nn.sigmoid(gp)
            dvp = da_e * (gp * sig)
            dgp = da_e * vp * (sig * (1.0 + gp * (1.0 - sig)))

            def back(dp, w):
                return (w[2:3, :] * dp[0:RC] + w[1:2, :] * pltpu.roll(dp, ext - 1, 0)[0:RC]
                        + w[0:1, :] * pltpu.roll(dp, ext - 2, 0)[0:RC]).astype(BF)

            def sums(dp, u2, u1, u0):
                d = dp[0:RC]
                return [_rows8(d), _rows8(d * u2[0:RC]), _rows8(d * u1[0:RC]), _rows8(d * u0[0:RC])]

            return back(dgp, wg), back(dvp, wv), sums(dgp, g2, g1, g0) + sums(dvp, v2, v1, v0)

        zero = jnp.zeros((8, FT), F32)
        dug, duv, acc = chunk(jnp.concatenate([zero, ug_ref[0:ext, :]], axis=0),
                              jnp.concatenate([zero, uv_ref[0:ext, :]], axis=0), da_ref[0:ext, :])
        dug_ref[0:RC, :] = dug
        duv_ref[0:RC, :] = duv

        def step(i, acc):
            r0 = pl.multiple_of(i * RC, RC)
            win = pl.ds(r0 - 8, ext + 8)
            dug, duv, part = chunk(ug_ref[win, :], uv_ref[win, :], da_ref[pl.ds(r0, ext), :])
            dug_ref[pl.ds(r0, RC), :] = dug
            duv_ref[pl.ds(r0, RC), :] = duv
            return [a + p for a, p in zip(acc, part)]

        acc = lax.fori_loop(1, NCH - 1, step, acc)
        r0 = T - RC
        tail = lambda ref, lo: jnp.concatenate([ref[lo:T, :], zero], axis=0)
        dug, duv, part = chunk(tail(ug_ref, r0 - 8), tail(uv_ref, r0 - 8), tail(da_ref, r0))
        dug_ref[r0:T, :] = dug
        duv_ref[r0:T, :] = duv
        tot = [jnp.sum(a + p, axis=0, keepdims=True) for a, p in zip(acc, part)]
        for k, (dw_ref, db_ref) in enumerate(((dwg_ref, dbg_ref), (dwv_ref, dbv_ref))):
            db_ref[...] = tot[4 * k]
            dw_ref[...] = jnp.zeros_like(dw_ref)
            for r in range(3):
                dw_ref[r:r + 1, :] = tot[4 * k + 1 + r]

    col = lambda r: pl.BlockSpec((r, FT), lambda j: (0, j))
    return _call(
        body, (up, up, fw8, fw8, fb, fb, da), name="ffn_act_bwd", grid=(NFT,),
        in_specs=_ffn_act_specs() + [pl.BlockSpec((T, FT), lambda j: (0, j))],
        out_specs=[col(T), col(T), col(8), col(8), col(1), col(1)],
        out_shape=[SDS((T, DFF), BF), SDS((T, DFF), BF), SDS((8, DFF), F32), SDS((8, DFF), F32),
                   SDS((1, DFF), F32), SDS((1, DFF), F32)],
        sem=("parallel",), vmem_mib=40, comm=comm)


def _ffn_down_bwd(dh2b, w_down, comm=()):
    tm = 256

    def body(d_ref, w_ref, o_ref):
        o_ref[...] = _dot(d_ref[...], w_ref[...], 1, 1)

    return _call(
        body, (dh2b, w_down), name="ffn_down_bwd", grid=(T // tm,),
        in_specs=[pl.BlockSpec((tm, D), lambda i: (i, 0)), pl.BlockSpec((DFF, D), lambda i: (0, 0))],
        out_specs=[pl.BlockSpec((tm, DFF), lambda i: (i, 0))], out_shape=[SDS((T, DFF), F32)],
        sem=("parallel",), vmem_mib=40, comm=comm)


def _norm_matmul_bwd(name, a_list, w_t, k_offsets, xin, g, dres, want_bf16, comm=()):
    tm = 256
    ks = [a.shape[1] for a in a_list]
    n_a = len(a_list)

    def body(*refs):
        a_refs = refs[:n_a]
        w_ref, x_ref, g_ref, r_ref = refs[n_a:n_a + 4]
        outs = refs[n_a + 4:]
        dx_ref, dg_ref = outs[0], outs[-1]

        @pl.when(pl.program_id(0) == 0)
        def _():
            dg_ref[...] = jnp.zeros_like(dg_ref)

        du = _dot(a_refs[0][...], w_ref[k_offsets[0]:k_offsets[0] + ks[0], :], 1, 0)
        for k in range(1, n_a):
            du = du + _dot(a_refs[k][...], w_ref[k_offsets[k]:k_offsets[k] + ks[k], :], 1, 0)
        x = x_ref[...]
        r = lax.rsqrt(jnp.mean(x * x, axis=-1, keepdims=True) + EPS)
        dx, dg = _rms_bwd(du, x, r, g_ref[...])
        dx = r_ref[...] + dx
        dx_ref[...] = dx
        if want_bf16:
            outs[1][...] = dx.astype(BF)
        dg_ref[...] += dg

    tile = lambda c: pl.BlockSpec((tm, c), lambda i: (i, 0))
    out_specs = [tile(D)] + ([tile(D)] if want_bf16 else []) + [pl.BlockSpec((1, D), lambda i: (0, 0))]
    out_shape = [SDS((T, D), F32)] + ([SDS((T, D), BF)] if want_bf16 else []) + [SDS((1, D), F32)]
    return _call(
        body, (*a_list, w_t, xin, g, dres), name=name, grid=(T // tm,),
        in_specs=[tile(k) for k in ks] + [pl.BlockSpec(w_t.shape, lambda i: (0, 0)), tile(D),
                                           pl.BlockSpec((1, D), lambda i: (0, 0)), tile(D)],
        out_specs=out_specs, out_shape=out_shape, sem=("arbitrary",), vmem_mib=56, comm=comm)


def _out_bwd(dh1b, w_out):
    tm = 256

    def body(d_ref, w_ref, o_ref):
        o_ref[...] = _dot(d_ref[...], w_ref[...], 1, 1)

    return _call(
        body, (dh1b, w_out), name="out_bwd", grid=(T // tm,),
        in_specs=[pl.BlockSpec((tm, D), lambda i: (i, 0)), pl.BlockSpec((D, D), lambda i: (0, 0))],
        out_specs=[pl.BlockSpec((tm, D), lambda i: (i, 0))], out_shape=[SDS((T, D), F32)],
        sem=("parallel",), vmem_mib=32)


def _wgrad(name, a_list, b, comm=()):
    tm = 256
    steps = [a.shape[1] // tm for a in a_list]
    starts = [sum(steps[:k]) for k in range(len(a_list))]
    n_a = len(a_list)

    def body(*refs):
        a_refs, b_ref, o32_ref, obf_ref = refs[:n_a], refs[n_a], refs[n_a + 1], refs[n_a + 2]
        i = pl.program_id(0)
        for k in range(n_a):
            @pl.when((i >= starts[k]) & (i < starts[k] + steps[k]))
            def _(k=k):
                r = _dot(a_refs[k][...], b_ref[...], 0, 0)
                o32_ref[...] = r
                obf_ref[...] = r.astype(BF)

    def a_spec(k):
        return pl.BlockSpec((T, tm), lambda i: (0, jnp.clip(i - starts[k], 0, steps[k] - 1)))

    m_total = tm * sum(steps)
    return _call(
        body, (*a_list, b), name=name, grid=(sum(steps),),
        in_specs=[a_spec(k) for k in range(n_a)] + [pl.BlockSpec((T, D), lambda i: (0, 0))],
        out_specs=[pl.BlockSpec((tm, D), lambda i: (i, 0)), pl.BlockSpec((tm, D), lambda i: (i, 0))],
        out_shape=[SDS((m_total, D), F32), SDS((m_total, D), BF)], sem=("parallel",), vmem_mib=40, comm=comm)


def _chip_sum(name, g32, from_sib, core, chip):
    h = g32.shape[1]
    th = h // 2

    def body(core_ref, chip_ref, g_ref, s_ref, pbf_ref, own_ref):
        p = g_ref[0] + s_ref[0].astype(F32)
        pbf_ref[0] = p.astype(BF)

        @pl.when(pl.program_id(1) == chip_ref[0])
        def _():
            own_ref[...] = p

    grid_spec = pltpu.PrefetchScalarGridSpec(
        num_scalar_prefetch=2, grid=(h // th, N_CHIPS),
        in_specs=[pl.BlockSpec((1, th, D), lambda t, jj, core_ref, chip_ref: (2 * jj + core_ref[0], t, 0)),
                  pl.BlockSpec((1, th, D), lambda t, jj, core_ref, chip_ref: (jj, t, 0))],
        out_specs=[pl.BlockSpec((1, th, D), lambda t, jj, core_ref, chip_ref: (jj, t, 0)),
                   pl.BlockSpec((th, D), lambda t, jj, core_ref, chip_ref: (t, 0))],
    )
    return _pcall(
        body, name=name, grid_spec=grid_spec, out_shape=_in_hbm([SDS((N_CHIPS, h, D), BF), SDS((h, D), F32)]),
        compiler_params=_params(("arbitrary", "arbitrary"), 32),
    )(core, chip, *_from_hbm(g32, from_sib))


def _final_sum(name, own, from_chips, core):
    h = own.shape[0]

    def body(core_ref, o_ref, r_ref, f_ref):
        f_ref[0] = ((o_ref[...] + r_ref[0].astype(F32)) + r_ref[1].astype(F32)) + r_ref[2].astype(F32)

    grid_spec = pltpu.PrefetchScalarGridSpec(
        num_scalar_prefetch=1, grid=(1,),
        in_specs=[pl.BlockSpec((h, D), lambda i, core_ref: (0, 0)), pl.BlockSpec((3, h, D), lambda i, core_ref: (0, 0, 0))],
        out_specs=pl.BlockSpec((1, h, D), lambda i, core_ref: (core_ref[0], 0, 0)),
    )
    return _pcall(body, name=name, grid_spec=grid_spec, out_shape=pltpu.HBM((2, h, D), F32),
                  compiler_params=_params(("arbitrary",), 40))(core, *_from_hbm(own, from_chips))


def _adam_math(w, g, m, v):
    nm = ADAM_B1 * m + (1.0 - ADAM_B1) * g
    nv = ADAM_B2 * v + (1.0 - ADAM_B2) * (g * g)
    m_hat = nm / (1.0 - ADAM_B1 ** ADAM_STEP)
    v_hat = nv / (1.0 - ADAM_B2 ** ADAM_STEP)
    return -ADAM_LR * (m_hat / (jnp.sqrt(v_hat) + ADAM_EPS) + ADAM_WD * w), nm, nv


def _adamw(name, w, g, m, v, tr, comm=()):
    rows, cols = w.shape

    def body(w_ref, g_ref, m_ref, v_ref, d_ref, nm_ref, nv_ref):
        d_ref[...], nm_ref[...], nv_ref[...] = _adam_math(w_ref[...], g_ref[...], m_ref[...], v_ref[...])

    spec = pl.BlockSpec((tr, cols), lambda i: (i, 0))
    return _call(body, (w, g, m, v), name=name, grid=(rows // tr,), in_specs=[spec] * 4, out_specs=[spec] * 3,
                 out_shape=[SDS((rows, cols), F32)] * 3, sem=("parallel",), vmem_mib=32, comm=comm)


C_G1, C_G2, C_GCO, C_GAO, C_DCW, C_DQG, C_DKG, C_SINK, C_SQ = 0, 1024, 2048, 2560, 3072, 4608, 4736, 4864, 5632
P_W = C_SQ + 128


def _pack_small(me, dfwg, dfwv, dfbg, dfbv, dg1, dg2, dgco, dgao, dcw8, dqg, dkg, dsink, sq):
    def body(me_ref, dfwg_r, dfwv_r, dfbg_r, dfbv_r, dg1_r, dg2_r, dgco_r, dgao_r, dcw_r, dqg_r, dkg_r, dsink_r, sq_r, o):
        o[...] = jnp.zeros_like(o)
        o[0, :, 0:DFF] = dfwg_r[...]
        o[0, :, DFF:2 * DFF] = dfwv_r[...]
        o[0, 3:4, 0:DFF] = dfbg_r[...]
        o[0, 3:4, DFF:2 * DFF] = dfbv_r[...]
        o[0, 4:5, C_G1:C_G1 + D] = dg1_r[...]
        o[0, 4:5, C_G2:C_G2 + D] = dg2_r[...]
        o[0, 4:5, C_GCO:C_GCO + CW] = dgco_r[...]
        o[0, 4:5, C_GAO:C_GAO + AW] = dgao_r[...]
        for r in range(3):
            o[0, 4:5, C_DCW + r * CW:C_DCW + (r + 1) * CW] = dcw_r[r:r + 1, :]
        o[0, 4:5, C_DQG:C_DQG + HD] = dqg_r[...]
        o[0, 4:5, C_DKG:C_DKG + HD] = dkg_r[...]
        o[0, 4:5, C_SINK:C_SINK + 128] = dsink_r[...]
        o[0, :, C_SQ:C_SQ + 128] = sq_r[...]

    ins = (dfwg, dfwv, dfbg, dfbv, dg1, dg2, dgco, dgao, dcw8, dqg, dkg, dsink, sq)
    grid_spec = pltpu.PrefetchScalarGridSpec(
        num_scalar_prefetch=1, grid=(1,),
        in_specs=[pl.BlockSpec(a.shape, lambda i, me_ref: (0, 0)) for a in ins],
        out_specs=pl.BlockSpec((1, 8, P_W), lambda i, me_ref: (me_ref[0], 0, 0)),
    )
    return _pcall(body, name="pack_small", grid_spec=grid_spec, out_shape=SDS((N_DEV, 8, P_W), F32),
                  compiler_params=_params(("arbitrary",)))(me, *ins)


N_SMALL = 11


def _small_adam(chip, p_all, tbl_all, ws, ms, vs):
    fw_cols = 2 * DFF // N_CHIPS
    cw_cols = CW // N_CHIPS

    def body(chip_ref, p_ref, fw_ref, cw0_ref, cw1_ref, cw2_ref, tbl_ref, *refs):
        w_r, m_r, v_r = refs[0:N_SMALL], refs[N_SMALL:2 * N_SMALL], refs[2 * N_SMALL:3 * N_SMALL]
        outs = refs[3 * N_SMALL:]
        g_o, d_o, nm_o, nv_o = (outs[k * N_SMALL:(k + 1) * N_SMALL] for k in range(4))
        loss_o = outs[4 * N_SMALL]

        def total(ref):
            s = ref[0]
            for k in range(1, N_DEV):
                s = s + ref[k]
            return s

        S = total(p_ref)
        fw = total(fw_ref)
        cws = [total(r) for r in (cw0_ref, cw1_ref, cw2_ref)]

        def step(i, g, at):
            d, nm, nv = _adam_math(w_r[i][at], g, m_r[i][at], v_r[i][at])
            g_o[i][at], d_o[i][at], nm_o[i][at], nv_o[i][at] = g, d, nm, nv

        everything = (slice(None), slice(None))
        step(0, S[4:5, C_G1:C_G1 + D], everything)
        for r in range(3):
            step(1, cws[r][4:5, :], (0, slice(r, r + 1), slice(None)))
        step(2, S[4:5, C_DQG:C_DQG + HD], everything)
        step(3, S[4:5, C_DKG:C_DKG + HD], everything)
        step(4, total(tbl_ref), everything)
        step(5, S[4:5, C_SINK:C_SINK + NH], everything)
        step(6, S[4:5, C_GCO:C_GCO + CW], everything)
        step(7, S[4:5, C_GAO:C_GAO + AW], everything)
        step(8, S[4:5, C_G2:C_G2 + D], everything)
        step(9, fw[0:3, :], (0, slice(None), slice(None)))
        step(10, S[3:4, 0:2 * DFF], everything)
        sq = S[:, C_SQ:C_SQ + 128]
        loss_o[...] = jnp.sum(jnp.sum(sq, axis=1, keepdims=True), axis=0, keepdims=True) * (0.5 / D)

    def full(a):
        n = len(a.shape)
        return pl.BlockSpec(a.shape, lambda i, chip_ref: (0,) * n)

    params = [*ws, *ms, *vs]
    grid_spec = pltpu.PrefetchScalarGridSpec(
        num_scalar_prefetch=1, grid=(1,),
        in_specs=[full(p_all),
                  pl.BlockSpec((N_DEV, 8, fw_cols), lambda i, chip_ref: (0, 0, chip_ref[0])),
                  *[pl.BlockSpec((N_DEV, 8, cw_cols), lambda i, chip_ref, r=r: (0, 0, (C_DCW + r * CW) // cw_cols + chip_ref[0]))
                    for r in range(3)],
                  full(tbl_all), *[full(a) for a in params]],
        out_specs=[full(a) for a in ws] * 4 + [pl.BlockSpec((1, 1), lambda i, chip_ref: (0, 0))],
    )
    out = _pcall(
        body, name="small_adam", grid_spec=grid_spec,
        out_shape=[SDS(a.shape, F32) for a in ws] * 4 + [SDS((1, 1), F32)],
        compiler_params=_params(("arbitrary",), 32),
    )(chip, p_all, p_all, p_all, p_all, p_all, tbl_all, *params)
    return out[0:N_SMALL], out[N_SMALL:2 * N_SMALL], out[2 * N_SMALL:3 * N_SMALL], out[3 * N_SMALL:4 * N_SMALL], out[4 * N_SMALL]


def _place_weights(chip, shards, conv_w, ffn_conv_w):
    steps = 4

    def body(chip_ref, a0, a1, a2, a3, s0, s1, o0, o1, o2, o3, t0, t1):
        for a, o in ((a0, o0), (a1, o1), (a2, o2), (a3, o3)):
            o[...] = a[...].astype(BF)

        @pl.when(pl.program_id(0) == 0)
        def _():
            for s, t in ((s0, t0), (s1, t1)):
                t[...] = jnp.zeros_like(t)
                t[0, 0:3, :] = s[...]

    rows = [s.shape[0] // steps for s in shards]
    grid_spec = pltpu.PrefetchScalarGridSpec(
        num_scalar_prefetch=1, grid=(steps,),
        in_specs=[pl.BlockSpec((r, D), lambda i, chip_ref: (i, 0)) for r in rows]
        + [pl.BlockSpec(s.shape, lambda i, chip_ref: (0, 0)) for s in (conv_w, ffn_conv_w)],
        out_specs=[pl.BlockSpec((r, D), lambda i, chip_ref: (chip_ref[0] * steps + i, 0)) for r in rows]
        + [pl.BlockSpec((1, 8, s.shape[1]), lambda i, chip_ref: (chip_ref[0], 0, 0)) for s in (conv_w, ffn_conv_w)],
    )
    return _pcall(
        body, name="place_weights", grid_spec=grid_spec,
        out_shape=_in_hbm([SDS((N_CHIPS * s.shape[0], D), BF) for s in shards]
                          + [SDS((N_CHIPS, 8, s.shape[1]), F32) for s in (conv_w, ffn_conv_w)]),
        compiler_params=_params(("arbitrary",), 32),
    )(chip, *_from_hbm(*shards), conv_w, ffn_conv_w)


def kernel(x, norm_mix_g, w_in, conv_w, q_norm_g, k_norm_g, rel_bias_table, sinks, out_norm_conv_g, out_norm_attn_g, w_out, norm_ffn_g, w_up, ffn_conv_w, ffn_conv_b, w_down, loss_target, m_norm_mix_g, m_w_in, m_conv_w, m_q_norm_g, m_k_norm_g, m_rel_bias_table, m_sinks, m_out_norm_conv_g, m_out_norm_attn_g, m_w_out, m_norm_ffn_g, m_w_up, m_ffn_conv_w, m_ffn_conv_b, m_w_down, v_norm_mix_g, v_w_in, v_conv_w, v_q_norm_g, v_k_norm_g, v_rel_bias_table, v_sinks, v_out_norm_conv_g, v_out_norm_attn_g, v_w_out, v_norm_ffn_g, v_w_up, v_ffn_conv_w, v_ffn_conv_b, v_w_down):
    as_arg = lambda i: jnp.reshape(i, (1,)).astype(jnp.int32)
    chip = as_arg(2 * lax.axis_index("x") + lax.axis_index("y"))
    core = as_arg(lax.axis_index("c"))
    me = 2 * chip + core
    xs, tgt = x[0], loss_target[0]
    qg, kg, gco, gao, g1, g2, fb = q_norm_g, k_norm_g, out_norm_conv_g, out_norm_attn_g, norm_mix_g, norm_ffn_g, ffn_conv_b
    pieces = lambda g: g.reshape(N_DEV, g.shape[0] // N_DEV, D)
    whole = lambda f: f.reshape(2 * f.shape[1], D)

    p_in, p_out, p_up, p_down, p_cw, p_fw = _place_weights(chip, [w_in[0].T, w_out[0], w_up[0].T, w_down[0]], conv_w[0], ffn_conv_w[0])
    w_int, cw_all, fw_all = _comm_call("gather_first", [_t_gather(p_in), _t_small_weights(p_cw), _t_small_weights(p_fw)])
    cw8 = jnp.transpose(cw_all, (1, 0, 2)).reshape(8, CW)
    fw8 = jnp.transpose(fw_all, (1, 0, 2)).reshape(8, 2 * DFF)
    bucket = jnp.asarray(_bucket_table())
    bias = _band_bias(rel_bias_table, bucket)

    proj, u1, w_out_f = _inproj(xs, g1, w_int, comm=[_t_gather(p_out)])
    y, w_upt = _mix_fwd(proj, sinks, cw8, qg, kg, gco, gao, bias, comm=[_t_gather(p_up)])
    h1, u2 = _outproj(y, w_out_f, xs, g2)
    up, w_down_f = _ffn_up(u2, w_upt, comm=[_t_gather(p_down)])
    a, = _ffn_act(up, fw8, fb)
    dh2, dh2b, sq = _ffn_down(a, w_down_f, h1, tgt)

    gd32, gdbf = _wgrad("wgrad_down", [a], dh2b)
    da, sib_down = _ffn_down_bwd(dh2b, w_down_f, comm=[_t_sibling(pieces(gdbf))])
    pbf_down, own_down = _chip_sum("chip_sum_w_down", pieces(gd32), sib_down, core, chip)
    dug, duv, dfwg, dfwv, dfbg, dfbv, chips_down = _ffn_act_bwd(up, da, fw8, fb, comm=[_t_chips(pbf_down)])
    fin_down = _final_sum("final_sum_w_down", own_down, chips_down, core)
    gu32, gubf = _wgrad("wgrad_up", [dug, duv], u2)
    dh1, dh1b, dg2, sib_up, fin_down = _norm_matmul_bwd(
        "ffn_up_bwd", [dug, duv], w_upt, [0, DFF], h1, g2, dh2, True, comm=[_t_sibling(pieces(gubf)), _t_swap(fin_down)])
    pbf_up, own_up = _chip_sum("chip_sum_w_up", pieces(gu32), sib_up, core, chip)
    go32, gobf = _wgrad("wgrad_out", [y], dh1b)
    dy, = _out_bwd(dh1b, w_out_f)
    dproj, dcw8, dqg, dkg, dgco, dgao, dsink, dbias, chips_up, sib_out = _mix_bwd(
        proj, dy, sinks, cw8, qg, kg, gco, gao, bias, comm=[_t_chips(pbf_up), _t_sibling(pieces(gobf))])
    fin_up = _final_sum("final_sum_w_up", own_up, chips_up, core)
    pbf_out, own_out = _chip_sum("chip_sum_w_out", pieces(go32), sib_out, core, chip)
    tbl_all = _band_bias_bwd(dbias, bucket, me)
    dx, dg1 = _norm_matmul_bwd("in_bwd", [dproj], w_int, [0], xs, g1, dh1, False)
    gi32, gibf, chips_out, fin_up = _wgrad("wgrad_in", [dproj], u1, comm=[_t_chips(pbf_out), _t_swap(fin_up)])
    fin_out = _final_sum("final_sum_w_out", own_out, chips_out, core)
    p_all = _pack_small(me, dfwg, dfwv, dfbg, dfbv, dg1, dg2, dgco, dgao, dcw8, dqg, dkg, dsink, sq)
    sib_in, p_all, tbl_all, fin_out = _comm_call(
        "to_sibling_last", [_t_sibling(pieces(gibf)), _t_allgather(p_all), _t_allgather(tbl_all), _t_swap(fin_out)])
    pbf_in, own_in = _chip_sum("chip_sum_w_in", pieces(gi32), sib_in, core, chip)

    g_w_out, g_w_up, g_w_down = whole(fin_out), whole(fin_up).T, whole(fin_down)
    d_down, nm_down, nv_down = _adamw("adamw_w_down", w_down[0], g_w_down, m_w_down[0], v_w_down[0], 352)
    d_up, nm_up, nv_up, chips_in = _adamw("adamw_w_up", w_up[0], g_w_up, m_w_up[0], v_w_up[0], 256, comm=[_t_chips(pbf_in)])
    d_out, nm_out, nv_out = _adamw("adamw_w_out", w_out[0], g_w_out, m_w_out[0], v_w_out[0], 256)
    sw = [norm_mix_g, conv_w, q_norm_g, k_norm_g, rel_bias_table, sinks, out_norm_conv_g, out_norm_attn_g,
          norm_ffn_g, ffn_conv_w, ffn_conv_b]
    smm = [m_norm_mix_g, m_conv_w, m_q_norm_g, m_k_norm_g, m_rel_bias_table, m_sinks, m_out_norm_conv_g,
           m_out_norm_attn_g, m_norm_ffn_g, m_ffn_conv_w, m_ffn_conv_b]
    smv = [v_norm_mix_g, v_conv_w, v_q_norm_g, v_k_norm_g, v_rel_bias_table, v_sinks, v_out_norm_conv_g,
           v_out_norm_attn_g, v_norm_ffn_g, v_ffn_conv_w, v_ffn_conv_b]
    sg, sd, snm, snv, loss = _small_adam(chip, p_all, tbl_all, sw, smm, smv)
    fin_in = _final_sum("final_sum_w_in", own_in, chips_in, core)
    fin_in, = _comm_call("swap_last", [_t_swap(fin_in)])
    g_w_in = whole(fin_in).T
    d_in, nm_in, nv_in = _adamw("adamw_w_in", w_in[0], g_w_in, m_w_in[0], v_w_in[0], 256)

    def order(s, b_in, b_out, b_up, b_down):
        return (s[0], b_in[None], s[1], s[2], s[3], s[4], s[5], s[6], s[7], b_out[None], s[8], b_up[None],
                s[9], s[10], b_down[None])

    return (loss.reshape(()), dx[None],
            *order(sg, g_w_in, g_w_out, g_w_up, g_w_down),
            *order(sd, d_in, d_out, d_up, d_down),
            *order(snm, nm_in, nm_out, nm_up, nm_down),
            *order(snv, nv_in, nv_out, nv_up, nv_down))
```

```python
import functools
import math

import numpy as np

import jax
import jax.numpy as jnp
from jax import lax
from jax.experimental import pallas as pl
from jax.experimental.pallas import tpu as pltpu

F32 = jnp.float32
BF = jnp.bfloat16
SDS = jax.ShapeDtypeStruct

T = 2048
D = 1024
CW = 512
AW = 512
HD = 64
NH = 8
NKV = 2
GQ = 4
INW = 2304
DFF = 2816
BLK = 128
NB = T // BLK
NBUCKET = 32
EPS = 1e-6
NEG_INF = -1e30
N_CHIPS = 4
N_DEV = 8

ADAM_LR = 0.001
ADAM_B1 = 0.9
ADAM_B2 = 0.999
ADAM_EPS = 1e-08
ADAM_WD = 0.01
ADAM_STEP = 10

MIB = 1024 * 1024
MESH = pl.DeviceIdType.MESH
ANY = pl.BlockSpec(memory_space=pl.ANY)

_pcall = pl.pallas_call


def _params(sem=None, vmem_mib=None):
    kw = {}
    if sem is not None:
        kw["dimension_semantics"] = sem
    if vmem_mib is not None:
        kw["vmem_limit_bytes"] = vmem_mib * MIB
    return pltpu.CompilerParams(**kw)


def _dot(a, b, ca, cb):
    return lax.dot_general(a, b, (((ca,), (cb,)), ((), ())), preferred_element_type=F32)


def _rms_bwd(dy, x, r, g):
    dg = jnp.sum(dy * (x * r), axis=0, keepdims=True)
    dgx = dy * g
    dx = r * dgx - x * (r * r * r) * jnp.mean(x * dgx, axis=-1, keepdims=True)
    return dx, dg


def _where():
    x, y, c = lax.axis_index("x"), lax.axis_index("y"), lax.axis_index("c")
    return x, y, c, [(1 - x, y), (x, 1 - y), (1 - x, 1 - y)]


def _rcopy(src, dst, ssem, rsem, dev):
    return pltpu.make_async_remote_copy(src_ref=src, dst_ref=dst, send_sem=ssem, recv_sem=rsem, device_id=dev,
                                        device_id_type=MESH)


class _Task:
    def __init__(self, ins, outs, alias, n_sem, start, finish):
        self.ins, self.outs, self.alias, self.n_sem, self.start, self.finish = ins, outs, alias, n_sem, start, finish


def _t_gather(placed):
    R = placed.shape[0] // N_CHIPS

    def rows(chip_index, core):
        return pl.ds(pl.multiple_of(chip_index * R + core * (R // 2), 16), R // 2)

    def start(cin, cout, ss, rs, b):
        x, y, c, chips = _where()
        mine = cout[0].at[rows(2 * x + y, c)]
        for r, (px, py) in enumerate(chips):
            _rcopy(mine, mine, ss.at[b + r], rs.at[b + r], (px, py, c)).start()

    def finish(cin, cout, ss, rs, b):
        x, y, c, chips = _where()
        buf = cout[0]
        sib = (x, y, 1 - c)
        for r, (px, py) in enumerate(chips):
            got = buf.at[rows(2 * px + py, c)]
            _rcopy(got, got, ss.at[b + r], rs.at[b + r], (px, py, c)).wait_recv()
            _rcopy(got, got, ss.at[b + 3 + r], rs.at[b + 3 + r], sib).start()
        for r, (px, py) in enumerate(chips):
            got = buf.at[rows(2 * px + py, 1 - c)]
            _rcopy(got, got, ss.at[b + 3 + r], rs.at[b + 3 + r], sib).wait_recv()
        mine = buf.at[rows(2 * x + y, c)]
        for r in range(6):
            _rcopy(mine, mine, ss.at[b + r], rs.at[b + r], sib).wait_send()

    return _Task([placed], [SDS(placed.shape, placed.dtype)], [(0, 0)], 6, start, finish)


def _t_small_weights(buf):
    def start(cin, cout, ss, rs, b):
        x, y, c, chips = _where()
        mine = cout[0].at[2 * x + y]
        for r, (px, py) in enumerate(chips):
            _rcopy(mine, mine, ss.at[b + r], rs.at[b + r], (px, py, c)).start()

    def finish(cin, cout, ss, rs, b):
        x, y, c, chips = _where()
        for r, (px, py) in enumerate(chips):
            got = cout[0].at[2 * px + py]
            _rcopy(got, got, ss.at[b + r], rs.at[b + r], (px, py, c)).wait_recv()
        for r, (px, py) in enumerate(chips):
            mine = cout[0].at[2 * x + y]
            _rcopy(mine, mine, ss.at[b + r], rs.at[b + r], (px, py, c)).wait_send()

    return _Task([buf], [SDS(buf.shape, buf.dtype)], [(0, 0)], 3, start, finish)


def _t_sibling(gbf):
    def start(cin, cout, ss, rs, b):
        x, y, c, _ = _where()
        for jj in range(N_CHIPS):
            _rcopy(cin[0].at[2 * jj + (1 - c)], cout[0].at[jj], ss.at[b + jj], rs.at[b + jj], (x, y, 1 - c)).start()

    def finish(cin, cout, ss, rs, b):
        x, y, c, _ = _where()
        for jj in range(N_CHIPS):
            got = cout[0].at[jj]
            _rcopy(got, got, ss.at[b + jj], rs.at[b + jj], (x, y, 1 - c)).wait_recv()
        for jj in range(N_CHIPS):
            got = cout[0].at[jj]
            _rcopy(got, got, ss.at[b + jj], rs.at[b + jj], (x, y, 1 - c)).wait_send()

    return _Task([gbf], [SDS((N_CHIPS,) + gbf.shape[1:], BF)], [], N_CHIPS, start, finish)


def _t_chips(pbf):
    def start(cin, cout, ss, rs, b):
        x, y, c, chips = _where()
        for r, (px, py) in enumerate(chips):
            _rcopy(cin[0].at[2 * px + py], cout[0].at[r], ss.at[b + r], rs.at[b + r], (px, py, c)).start()

    def finish(cin, cout, ss, rs, b):
        x, y, c, chips = _where()
        for r, (px, py) in enumerate(chips):
            got = cout[0].at[r]
            _rcopy(got, got, ss.at[b + r], rs.at[b + r], (px, py, c)).wait_recv()
        for r, (px, py) in enumerate(chips):
            got = cout[0].at[r]
            _rcopy(got, got, ss.at[b + r], rs.at[b + r], (px, py, c)).wait_send()

    return _Task([pbf], [SDS((3,) + pbf.shape[1:], BF)], [], 3, start, finish)


def _t_swap(fin):
    def start(cin, cout, ss, rs, b):
        x, y, c, _ = _where()
        mine = cout[0].at[c]
        _rcopy(mine, mine, ss.at[b], rs.at[b], (x, y, 1 - c)).start()

    def finish(cin, cout, ss, rs, b):
        x, y, c, _ = _where()
        got = cout[0].at[1 - c]
        _rcopy(got, got, ss.at[b], rs.at[b], (x, y, 1 - c)).wait_recv()
        _rcopy(got, got, ss.at[b], rs.at[b], (x, y, 1 - c)).wait_send()

    return _Task([fin], [SDS(fin.shape, fin.dtype)], [(0, 0)], 1, start, finish)


def _t_allgather(buf):
    def peers():
        x, y, c, _ = _where()
        out = []
        for rel in range(1, N_DEV):
            px, py, pc = x ^ ((rel >> 2) & 1), y ^ ((rel >> 1) & 1), c ^ (rel & 1)
            out.append((rel - 1, 4 * px + 2 * py + pc, (px, py, pc)))
        return 4 * x + 2 * y + c, out

    def start(cin, cout, ss, rs, b):
        me, ps = peers()
        mine = cout[0].at[me]
        for k, _, dev in ps:
            _rcopy(mine, mine, ss.at[b + k], rs.at[b + k], dev).start()

    def finish(cin, cout, ss, rs, b):
        me, ps = peers()
        for k, pidx, dev in ps:
            got = cout[0].at[pidx]
            _rcopy(got, got, ss.at[b + k], rs.at[b + k], dev).wait_recv()
        for k, _, dev in ps:
            mine = cout[0].at[me]
            _rcopy(mine, mine, ss.at[b + k], rs.at[b + k], dev).wait_send()

    return _Task([buf], [SDS(buf.shape, buf.dtype)], [(0, 0)], N_DEV - 1, start, finish)


def _run_tasks(comm, which, cin, cout, ss, rs):
    i0 = o0 = s0 = 0
    for t in comm:
        getattr(t, which)(cin[i0:i0 + len(t.ins)], cout[o0:o0 + len(t.outs)], ss, rs, s0)
        i0, o0, s0 = i0 + len(t.ins), o0 + len(t.outs), s0 + t.n_sem


def _from_hbm(*arrays):
    return [pltpu.with_memory_space_constraint(a, pltpu.HBM) for a in arrays]


def _in_hbm(shapes):
    return [pltpu.HBM(s.shape, s.dtype) for s in shapes]


def _comm_layout(comm, n_in, n_out):
    c_in = [a for t in comm for a in t.ins]
    c_out = [s for t in comm for s in t.outs]
    aliases, i0, o0 = {}, 0, 0
    for t in comm:
        for i, o in t.alias:
            aliases[n_in + i0 + i] = n_out + o0 + o
        i0, o0 = i0 + len(t.ins), o0 + len(t.outs)
    return c_in, c_out, aliases, sum(t.n_sem for t in comm)


def _call(body, operands, *, name, grid, in_specs, out_specs, out_shape, scratch_shapes=(), sem=None, vmem_mib=None, comm=(),
          free=(), prefetch=()):
    operands = [o if s.memory_space == pltpu.SMEM or k in free else pltpu.with_memory_space_constraint(o, pltpu.HBM)
                for k, (o, s) in enumerate(zip(operands, in_specs))]
    n_pre, n_in, n_out, n_scr = len(prefetch), len(in_specs), len(out_specs), len(scratch_shapes)
    c_in, c_out, aliases, n_sem = _comm_layout(comm, n_pre + n_in, n_out)
    sems = [pltpu.SemaphoreType.DMA((n_sem,)), pltpu.SemaphoreType.DMA((n_sem,))] if comm else []

    def wrapped(*refs):
        pre, refs = refs[:n_pre], refs[n_pre:]
        ins, cin = refs[:n_in], refs[n_in:n_in + len(c_in)]
        rest = refs[n_in + len(c_in):]
        outs, cout = rest[:n_out], rest[n_out:n_out + len(c_out)]
        rest = rest[n_out + len(c_out):]
        scr, csem = rest[:n_scr], rest[n_scr:]
        if not comm:
            return body(*pre, *ins, *outs, *scr)
        ids = [pl.program_id(k) for k in range(len(grid))]
        first = functools.reduce(jnp.logical_and, [i == 0 for i in ids])
        last = functools.reduce(jnp.logical_and, [i == n - 1 for i, n in zip(ids, grid)])
        pl.when(first)(lambda: _run_tasks(comm, "start", cin, cout, *csem))
        body(*pre, *ins, *outs, *scr)
        pl.when(last)(lambda: _run_tasks(comm, "finish", cin, cout, *csem))

    grid_spec = pltpu.PrefetchScalarGridSpec(
        num_scalar_prefetch=n_pre, grid=grid, in_specs=list(in_specs) + [ANY] * len(c_in),
        out_specs=list(out_specs) + [ANY] * len(c_out), scratch_shapes=list(scratch_shapes) + sems)
    return _pcall(
        wrapped, name=name, grid_spec=grid_spec, out_shape=_in_hbm(list(out_shape) + c_out), input_output_aliases=aliases,
        compiler_params=_params(("arbitrary",) * len(grid) if comm else sem, vmem_mib),
    )(*prefetch, *operands, *_from_hbm(*c_in))


def _comm_call(name, comm):
    c_in, c_out, aliases, n_sem = _comm_layout(comm, 0, 0)

    def body(*refs):
        cin, cout, (ss, rs) = refs[:len(c_in)], refs[len(c_in):len(c_in) + len(c_out)], refs[len(c_in) + len(c_out):]
        _run_tasks(comm, "start", cin, cout, ss, rs)
        _run_tasks(comm, "finish", cin, cout, ss, rs)

    return _pcall(
        body, name=name, in_specs=[ANY] * len(c_in), out_specs=[ANY] * len(c_out), out_shape=_in_hbm(c_out),
        scratch_shapes=[pltpu.SemaphoreType.DMA((n_sem,)), pltpu.SemaphoreType.DMA((n_sem,))],
        input_output_aliases=aliases,
    )(*_from_hbm(*c_in))


def _inproj(x, g1, w_int, comm=()):
    tm = 256

    def body(x_ref, g_ref, w_ref, proj_ref, u_ref):
        xf = x_ref[...]
        r = lax.rsqrt(jnp.mean(xf * xf, axis=-1, keepdims=True) + EPS)
        u = (xf * r * g_ref[...]).astype(BF)
        u_ref[...] = u
        proj_ref[...] = _dot(u, w_ref[...], 1, 1)

    return _call(
        body, (x, g1, w_int), name="inproj", grid=(T // tm,),
        in_specs=[pl.BlockSpec((tm, D), lambda i: (i, 0)), pl.BlockSpec((1, D), lambda i: (0, 0)),
                  pl.BlockSpec((INW, D), lambda i: (0, 0))],
        out_specs=[pl.BlockSpec((tm, INW), lambda i: (i, 0)), pl.BlockSpec((tm, D), lambda i: (i, 0))],
        out_shape=[SDS((T, INW), F32), SDS((T, D), BF)], sem=("parallel",), vmem_mib=40, comm=comm)


def _outproj(y, w_out, x, g2):
    tm = 256

    def body(y_ref, w_ref, x_ref, g_ref, h1_ref, u2_ref):
        h1 = x_ref[...] + _dot(y_ref[...], w_ref[...], 1, 0)
        h1_ref[...] = h1
        r = lax.rsqrt(jnp.mean(h1 * h1, axis=-1, keepdims=True) + EPS)
        u2_ref[...] = (h1 * r * g_ref[...]).astype(BF)

    return _call(
        body, (y, w_out, x, g2), name="outproj", grid=(T // tm,),
        in_specs=[pl.BlockSpec((tm, D), lambda i: (i, 0)), pl.BlockSpec((D, D), lambda i: (0, 0)),
                  pl.BlockSpec((tm, D), lambda i: (i, 0)), pl.BlockSpec((1, D), lambda i: (0, 0))],
        out_specs=[pl.BlockSpec((tm, D), lambda i: (i, 0)), pl.BlockSpec((tm, D), lambda i: (i, 0))],
        out_shape=[SDS((T, D), F32), SDS((T, D), BF)], sem=("parallel",), vmem_mib=32)


def _ffn_up(u2, w_upt, comm=()):
    tm, tn = 1024, 512

    def body(u_ref, w_ref, o_ref):
        o_ref[...] = _dot(u_ref[...], w_ref[...], 1, 1)

    return _call(
        body, (u2, w_upt), name="ffn_up", grid=(T // tm, 2 * DFF // tn),
        in_specs=[pl.BlockSpec((tm, D), lambda i, j: (i, 0)), pl.BlockSpec((tn, D), lambda i, j: (j, 0))],
        out_specs=[pl.BlockSpec((tm, tn), lambda i, j: (i, j))], out_shape=[SDS((T, 2 * DFF), F32)],
        sem=("parallel", "parallel"), vmem_mib=32, comm=comm)


def _ffn_down(a, w_down, h1, tgt):
    tm = 256

    def body(a_ref, w_ref, h1_ref, t_ref, dh_ref, dhb_ref, l_ref):
        @pl.when(pl.program_id(0) == 0)
        def _():
            l_ref[...] = jnp.zeros_like(l_ref)

        h2 = h1_ref[...] + _dot(a_ref[...], w_ref[...], 1, 0)
        e = h2 - t_ref[...]
        dh = e * (1.0 / D)
        dh_ref[...] = dh
        dhb_ref[...] = dh.astype(BF)
        e2 = jnp.sum((e * e).reshape(tm // 8, 8, D), axis=0)
        acc = e2[:, 0:128]
        for k in range(1, D // 128):
            acc = acc + e2[:, k * 128:(k + 1) * 128]
        l_ref[...] += acc

    return _call(
        body, (a, w_down, h1, tgt), name="ffn_down", grid=(T // tm,),
        in_specs=[pl.BlockSpec((tm, DFF), lambda i: (i, 0)), pl.BlockSpec((DFF, D), lambda i: (0, 0)),
                  pl.BlockSpec((tm, D), lambda i: (i, 0)), pl.BlockSpec((tm, D), lambda i: (i, 0))],
        out_specs=[pl.BlockSpec((tm, D), lambda i: (i, 0)), pl.BlockSpec((tm, D), lambda i: (i, 0)),
                   pl.BlockSpec((8, 128), lambda i: (0, 0))],
        out_shape=[SDS((T, D), F32), SDS((T, D), BF), SDS((8, 128), F32)], sem=("arbitrary",), vmem_mib=40)


def _bucket_table():
    q = np.arange(BLK, dtype=np.int32)[:, None]
    j = np.arange(2 * BLK, dtype=np.int32)[None, :]
    n = np.maximum(q + BLK - j, 0)
    nf = np.maximum(n, 1).astype(np.float32)
    max_exact = NBUCKET // 2
    large = max_exact + (np.log(nf / np.float32(max_exact)) / np.float32(math.log(BLK / max_exact))
                         * np.float32(NBUCKET - max_exact)).astype(np.int32)
    large = np.minimum(large, NBUCKET - 1)
    return np.where(n < max_exact, n, large).astype(np.int32)


def _band_bias_bwd(dbias, bucket, me):
    def body(me_ref, db_ref, bk_ref, o_ref):
        bk = bk_ref[...]
        for b in range(NBUCKET):
            m = bk == b
            for h in range(NH):
                v = jnp.where(m, db_ref[h * BLK:(h + 1) * BLK, :], 0.0)
                s = jnp.sum(jnp.sum(v, axis=1, keepdims=True), axis=0, keepdims=True)
                o_ref[0, b:b + 1, h:h + 1] = s

    grid_spec = pltpu.PrefetchScalarGridSpec(
        num_scalar_prefetch=1, grid=(1,),
        in_specs=[pl.BlockSpec((NH * BLK, 2 * BLK), lambda i, me_ref: (0, 0)),
                  pl.BlockSpec((BLK, 2 * BLK), lambda i, me_ref: (0, 0))],
        out_specs=pl.BlockSpec((1, NBUCKET, NH), lambda i, me_ref: (me_ref[0], 0, 0)),
    )
    return _pcall(body, name="band_bias_bwd", grid_spec=grid_spec, out_shape=SDS((N_DEV, NBUCKET, NH), F32),
                  compiler_params=_params(("arbitrary",)))(me, dbias, bucket)


def _mix_forward(P, zc8, zh8, pkv, first, cw, qg, kg, gco, gao, sink_ref, bias_ref):
    gate_b = P[:, 0:CW]
    gate_c = P[:, CW:2 * CW]
    hc = P[:, 2 * CW:3 * CW]
    z = gate_c * hc
    keep = jnp.where(first, 0.0, 1.0)
    zp = zc8 * zh8 * keep
    p1 = zp[7:8, :]
    p2 = zp[6:7, :]
    row = lax.broadcasted_iota(jnp.int32, (BLK, 1), 0)
    z1 = jnp.where(row == 0, p1, pltpu.roll(z, 1, 0))
    z2 = jnp.where(row == 0, p2, jnp.where(row == 1, p1, pltpu.roll(z, 2, 0)))
    cz = cw[0:1, :] * z2 + cw[1:2, :] * z1 + cw[2:3, :] * z
    y_conv = gate_b * cz

    scale = HD ** -0.5
    qi = lax.broadcasted_iota(jnp.int32, (GQ * BLK, 2 * BLK), 0) & (BLK - 1)
    kj = lax.broadcasted_iota(jnp.int32, (GQ * BLK, 2 * BLK), 1)
    dd = qi + BLK - kj
    first_key = jnp.where(first, BLK, 0)
    valid = (dd >= 0) & (dd < BLK) & (kj >= first_key)

    q0 = 3 * CW
    k0 = q0 + AW
    v0 = k0 + NKV * HD
    heads = []
    outs = []
    for kv in range(NKV):
        kb_raw = jnp.concatenate([pkv[:, kv * HD:(kv + 1) * HD], P[:, k0 + kv * HD:k0 + (kv + 1) * HD]], axis=0)
        rk = lax.rsqrt(jnp.mean(kb_raw * kb_raw, axis=-1, keepdims=True) + EPS)
        kb = (kb_raw * rk * kg).astype(BF)
        vb = jnp.concatenate([pkv[:, NKV * HD + kv * HD:NKV * HD + (kv + 1) * HD],
                              P[:, v0 + kv * HD:v0 + (kv + 1) * HD]], axis=0).astype(BF)
        q_raw, rq, qn = [], [], []
        for g in range(GQ):
            h = kv * GQ + g
            qh = P[:, q0 + h * HD:q0 + (h + 1) * HD]
            r = lax.rsqrt(jnp.mean(qh * qh, axis=-1, keepdims=True) + EPS)
            q_raw.append(qh)
            rq.append(r)
            qn.append(qh * r * qg)
        Q = jnp.concatenate(qn, axis=0).astype(BF)
        S = _dot(Q, kb, 1, 1) * scale + bias_ref[kv * GQ * BLK:(kv + 1) * GQ * BLK, :]
        S = jnp.where(valid, S, NEG_INF)
        sink = jnp.concatenate([jnp.full((BLK, 1), sink_ref[0, kv * GQ + g], F32) for g in range(GQ)], axis=0)
        m = jnp.maximum(jnp.max(S, axis=-1, keepdims=True), sink)
        p = jnp.exp(S - m)
        es = jnp.exp(sink - m)
        denom = jnp.sum(p, axis=-1, keepdims=True) + es
        probs = p / denom
        O = _dot(probs.astype(BF), vb, 1, 0)
        heads.append(dict(kb_raw=kb_raw, rk=rk, kb=kb, vb=vb, q_raw=q_raw, rq=rq, Q=Q, probs=probs,
                          psink=es / denom, O=O))
        outs += [O[g * BLK:(g + 1) * BLK, :] for g in range(GQ)]
    y_attn = jnp.concatenate(outs, axis=1)

    rc = lax.rsqrt(jnp.mean(y_conv * y_conv, axis=-1, keepdims=True) + EPS)
    ra = lax.rsqrt(jnp.mean(y_attn * y_attn, axis=-1, keepdims=True) + EPS)
    y = jnp.concatenate([y_conv * rc * gco, y_attn * ra * gao], axis=1)
    return dict(gate_b=gate_b, gate_c=gate_c, hc=hc, z=z, z1=z1, z2=z2, cz=cz, y_conv=y_conv, y_attn=y_attn,
                rc=rc, ra=ra, heads=heads, y=y, row=row, scale=scale)


def _mix_in_specs(blk):
    return [
        pl.BlockSpec(memory_space=pltpu.SMEM),
        pl.BlockSpec((BLK, INW), lambda s: (blk(s), 0)),
        pl.BlockSpec((8, CW), lambda s: (jnp.maximum(blk(s) * (BLK // 8) - 1, 0), 1)),
        pl.BlockSpec((8, CW), lambda s: (jnp.maximum(blk(s) * (BLK // 8) - 1, 0), 2)),
        pl.BlockSpec((BLK, 2 * NKV * HD), lambda s: (jnp.maximum(blk(s) - 1, 0), (3 * CW + AW) // (2 * NKV * HD))),
    ]


def _mix_param_specs():
    return [
        pl.BlockSpec((8, CW), lambda s: (0, 0)),
        pl.BlockSpec((1, HD), lambda s: (0, 0)),
        pl.BlockSpec((1, HD), lambda s: (0, 0)),
        pl.BlockSpec((1, CW), lambda s: (0, 0)),
        pl.BlockSpec((1, AW), lambda s: (0, 0)),
        pl.BlockSpec((NH * BLK, 2 * BLK), lambda s: (0, 0)),
    ]


def _mix_fwd(proj, sinks, cw8, qg, kg, gco, gao, bias, comm=()):
    def body(sink_ref, p_ref, zc_ref, zh_ref, pkv_ref, cw_ref, qg_ref, kg_ref, gco_ref, gao_ref, bias_ref, y_ref):
        first = pl.program_id(0) == 0
        f = _mix_forward(p_ref[...], zc_ref[...], zh_ref[...], pkv_ref[...], first, cw_ref[...], qg_ref[...],
                         kg_ref[...], gco_ref[...], gao_ref[...], sink_ref, bias_ref)
        y_ref[...] = f["y"].astype(BF)

    return _call(
        body, (sinks, proj, proj, proj, proj, cw8, qg, kg, gco, gao, bias), name="mix_fwd", grid=(NB,),
        in_specs=_mix_in_specs(lambda s: s) + _mix_param_specs(),
        out_specs=[pl.BlockSpec((BLK, D), lambda s: (s, 0))], out_shape=[SDS((T, D), BF)],
        sem=("parallel",), vmem_mib=32, comm=comm)


def _mix_bwd(proj, dy, sinks, cw8, qg, kg, gco, gao, bias, comm=()):
    def blk(s):
        return NB - 1 - s

    def body(sink_ref, p_ref, zc_ref, zh_ref, pkv_ref, dy_ref, cw_ref, qg_ref, kg_ref, gco_ref, gao_ref, bias_ref,
             dproj_ref, dcw_ref, dqg_ref, dkg_ref, dgco_ref, dgao_ref, dsink_ref, dbias_ref,
             ndcz_ref, dkc_ref, dvc_ref):
        s = pl.program_id(0)
        first = s == NB - 1

        @pl.when(s == 0)
        def _():
            for r in (dcw_ref, dqg_ref, dkg_ref, dgco_ref, dgao_ref, dsink_ref, dbias_ref, ndcz_ref, dkc_ref, dvc_ref):
                r[...] = jnp.zeros_like(r)

        cw = cw_ref[...]
        qg_v, kg_v, gco_v, gao_v = qg_ref[...], kg_ref[...], gco_ref[...], gao_ref[...]
        f = _mix_forward(p_ref[...], zc_ref[...], zh_ref[...], pkv_ref[...], first, cw, qg_v, kg_v, gco_v, gao_v,
                         sink_ref, bias_ref)
        dy = dy_ref[...]
        dyc, dgco = _rms_bwd(dy[:, 0:CW], f["y_conv"], f["rc"], gco_v)
        dya, dgao = _rms_bwd(dy[:, CW:CW + AW], f["y_attn"], f["ra"], gao_v)
        dgco_ref[...] += dgco
        dgao_ref[...] += dgao

        row = f["row"]
        dgate_b = dyc * f["cz"]
        dcz = dyc * f["gate_b"]
        dcw_ref[0:1, :] += jnp.sum(dcz * f["z2"], axis=0, keepdims=True)
        dcw_ref[1:2, :] += jnp.sum(dcz * f["z1"], axis=0, keepdims=True)
        dcw_ref[2:3, :] += jnp.sum(dcz * f["z"], axis=0, keepdims=True)
        nxt = ndcz_ref[...]
        n0 = nxt[0:1, :]
        n1 = nxt[1:2, :]
        d1 = jnp.where(row == BLK - 1, n0, pltpu.roll(dcz, BLK - 1, 0))
        d2 = jnp.where(row == BLK - 1, n1, jnp.where(row == BLK - 2, n0, pltpu.roll(dcz, BLK - 2, 0)))
        dz = cw[2:3, :] * dcz + cw[1:2, :] * d1 + cw[0:1, :] * d2
        ndcz_ref[...] = dcz[0:8, :]
        dproj_ref[:, 0:CW] = dgate_b.astype(BF)
        dproj_ref[:, CW:2 * CW] = (dz * f["hc"]).astype(BF)
        dproj_ref[:, 2 * CW:3 * CW] = (dz * f["gate_c"]).astype(BF)

        scale = f["scale"]
        lane = lax.broadcasted_iota(jnp.int32, (1, 128), 1)
        dq_cols, dk_cols, dv_cols = [], [], []
        for kv in range(NKV):
            hd = f["heads"][kv]
            dO = jnp.concatenate([dya[:, (kv * GQ + g) * HD:(kv * GQ + g + 1) * HD] for g in range(GQ)], axis=0)
            delta = jnp.sum(dO * hd["O"], axis=-1, keepdims=True)
            dOb = dO.astype(BF)
            dP = _dot(dOb, hd["vb"], 1, 1)
            dS = hd["probs"] * (dP - delta)
            dsk = hd["psink"] * delta
            for g in range(GQ):
                h = kv * GQ + g
                tot = jnp.sum(dsk[g * BLK:(g + 1) * BLK, :], axis=0, keepdims=True)
                dsink_ref[...] -= jnp.where(lane == h, tot, 0.0)
            dbias_ref[kv * GQ * BLK:(kv + 1) * GQ * BLK, :] += dS
            dSs = (dS * scale).astype(BF)
            dQ = _dot(dSs, hd["kb"], 1, 0)
            dKb = _dot(dSs, hd["Q"], 0, 0)
            dVb = _dot(hd["probs"].astype(BF), dOb, 0, 0)
            dkn = dKb[BLK:, :] + dkc_ref[:, kv * HD:(kv + 1) * HD]
            dvn = dVb[BLK:, :] + dvc_ref[:, kv * HD:(kv + 1) * HD]
            dkc_ref[:, kv * HD:(kv + 1) * HD] = dKb[:BLK, :]
            dvc_ref[:, kv * HD:(kv + 1) * HD] = dVb[:BLK, :]
            dk_raw, dkg = _rms_bwd(dkn, hd["kb_raw"][BLK:, :], hd["rk"][BLK:, :], kg_v)
            dkg_ref[...] += dkg
            dk_cols.append(dk_raw)
            dv_cols.append(dvn)
            for g in range(GQ):
                dq_raw, dqg = _rms_bwd(dQ[g * BLK:(g + 1) * BLK, :], hd["q_raw"][g], hd["rq"][g], qg_v)
                dqg_ref[...] += dqg
                dq_cols.append(dq_raw)
        dproj_ref[:, 3 * CW:INW] = jnp.concatenate(dq_cols + dk_cols + dv_cols, axis=1).astype(BF)

    small = lambda r, c: pl.BlockSpec((r, c), lambda s: (0, 0))
    return _call(
        body, (sinks, proj, proj, proj, proj, dy, cw8, qg, kg, gco, gao, bias), name="mix_bwd", grid=(NB,),
        in_specs=_mix_in_specs(blk) + [pl.BlockSpec((BLK, D), lambda s: (blk(s), 0))] + _mix_param_specs(),
        out_specs=[pl.BlockSpec((BLK, INW), lambda s: (blk(s), 0)), small(8, CW), small(1, HD), small(1, HD),
                   small(1, CW), small(1, AW), small(1, 128), small(NH * BLK, 2 * BLK)],
        out_shape=[SDS((T, INW), BF), SDS((8, CW), F32), SDS((1, HD), F32), SDS((1, HD), F32), SDS((1, CW), F32),
                   SDS((1, AW), F32), SDS((1, 128), F32), SDS((NH * BLK, 2 * BLK), F32)],
        scratch_shapes=[pltpu.VMEM((8, CW), F32), pltpu.VMEM((BLK, NKV * HD), F32), pltpu.VMEM((BLK, NKV * HD), F32)],
        sem=("arbitrary",), vmem_mib=40, comm=comm)


FT = 256
NFT = DFF // FT
RC = 64
NCH = T // RC


def _rows8(x):
    return jnp.sum(x.reshape(x.shape[0] // 8, 8, x.shape[1]), axis=0)


def _ffn_act_specs():
    return [
        pl.BlockSpec((T, FT), lambda j: (0, j)), pl.BlockSpec((T, FT), lambda j: (0, NFT + j)),
        pl.BlockSpec((8, FT), lambda j: (0, j)), pl.BlockSpec((8, FT), lambda j: (0, NFT + j)),
        pl.BlockSpec((1, FT), lambda j: (0, j)), pl.BlockSpec((1, FT), lambda j: (0, NFT + j)),
    ]


def _conv_rows(win, w, b, n):
    u = win[8:8 + n]
    u1 = pltpu.roll(win, 1, 0)[8:8 + n]
    u2 = pltpu.roll(win, 2, 0)[8:8 + n]
    return u2, u1, u, w[0:1, :] * u2 + w[1:2, :] * u1 + w[2:3, :] * u + b


def _ffn_act(up, fw8, fb):
    def body(ug_ref, uv_ref, wg_ref, wv_ref, bg_ref, bv_ref, a_ref):
        wg, wv, bg, bv = wg_ref[...], wv_ref[...], bg_ref[...], bv_ref[...]

        def chunk(win_g, win_v):
            gp = _conv_rows(win_g, wg, bg, RC)[3]
            vp = _conv_rows(win_v, wv, bv, RC)[3]
            return (gp * jax.nn.sigmoid(gp) * vp).astype(BF)

        zero = jnp.zeros((8, FT), F32)
        a_ref[0:RC, :] = chunk(jnp.concatenate([zero, ug_ref[0:RC, :]], axis=0),
                               jnp.concatenate([zero, uv_ref[0:RC, :]], axis=0))

        def step(i, carry):
            r0 = pl.multiple_of(i * RC, RC)
            win = pl.ds(r0 - 8, RC + 8)
            a_ref[pl.ds(r0, RC), :] = chunk(ug_ref[win, :], uv_ref[win, :])
            return carry

        lax.fori_loop(1, NCH, step, 0)

    return _call(
        body, (up, up, fw8, fw8, fb, fb), name="ffn_act", grid=(NFT,), in_specs=_ffn_act_specs(),
        out_specs=[pl.BlockSpec((T, FT), lambda j: (0, j))], out_shape=[SDS((T, DFF), BF)],
        sem=("parallel",), vmem_mib=40)


def _ffn_act_bwd(up, da, fw8, fb, comm=()):
    ext = RC + 8

    def body(ug_ref, uv_ref, wg_ref, wv_ref, bg_ref, bv_ref, da_ref,
             dug_ref, duv_ref, dwg_ref, dwv_ref, dbg_ref, dbv_ref):
        wg, wv, bg, bv = wg_ref[...], wv_ref[...], bg_ref[...], bv_ref[...]

        def chunk(win_g, win_v, da_e):
            g2, g1, g0, gp = _conv_rows(win_g, wg, bg, ext)
            v2, v1, v0, vp = _conv_rows(win_v, wv, bv, ext)
            sig = jax.nn.sigmoid(gp)
            dvp = da_e * (gp * sig)
            dgp = da_e * vp * (sig * (1.0 + gp * (1.0 - sig)))

            def back(dp, w):
                return (w[2:3, :] * dp[0:RC] + w[1:2, :] * pltpu.roll(dp, ext - 1, 0)[0:RC]
                        + w[0:1, :] * pltpu.roll(dp, ext - 2, 0)[0:RC]).astype(BF)

            def sums(dp, u2, u1, u0):
                d = dp[0:RC]
                return [_rows8(d), _rows8(d * u2[0:RC]), _rows8(d * u1[0:RC]), _rows8(d * u0[0:RC])]

            return back(dgp, wg), back(dvp, wv), sums(dgp, g2, g1, g0) + sums(dvp, v2, v1, v0)

        zero = jnp.zeros((8, FT), F32)
        dug, duv, acc = chunk(jnp.concatenate([zero, ug_ref[0:ext, :]], axis=0),
                              jnp.concatenate([zero, uv_ref[0:ext, :]], axis=0), da_ref[0:ext, :])
        dug_ref[0:RC, :] = dug
        duv_ref[0:RC, :] = duv

        def step(i, acc):
            r0 = pl.multiple_of(i * RC, RC)
            win = pl.ds(r0 - 8, ext + 8)
            dug, duv, part = chunk(ug_ref[win, :], uv_ref[win, :], da_ref[pl.ds(r0, ext), :])
            dug_ref[pl.ds(r0, RC), :] = dug
            duv_ref[pl.ds(r0, RC), :] = duv
            return [a + p for a, p in zip(acc, part)]

        acc = lax.fori_loop(1, NCH - 1, step, acc)
        r0 = T - RC
        tail = lambda ref, lo: jnp.concatenate([ref[lo:T, :], zero], axis=0)
        dug, duv, part = chunk(tail(ug_ref, r0 - 8), tail(uv_ref, r0 - 8), tail(da_ref, r0))
        dug_ref[r0:T, :] = dug
        duv_ref[r0:T, :] = duv
        tot = [jnp.sum(a + p, axis=0, keepdims=True) for a, p in zip(acc, part)]
        for k, (dw_ref, db_ref) in enumerate(((dwg_ref, dbg_ref), (dwv_ref, dbv_ref))):
            db_ref[...] = tot[4 * k]
            dw_ref[...] = jnp.zeros_like(dw_ref)
            for r in range(3):
                dw_ref[r:r + 1, :] = tot[4 * k + 1 + r]

    col = lambda r: pl.BlockSpec((r, FT), lambda j: (0, j))
    return _call(
        body, (up, up, fw8, fw8, fb, fb, da), name="ffn_act_bwd", grid=(NFT,),
        in_specs=_ffn_act_specs() + [pl.BlockSpec((T, FT), lambda j: (0, j))],
        out_specs=[col(T), col(T), col(8), col(8), col(1), col(1)],
        out_shape=[SDS((T, DFF), BF), SDS((T, DFF), BF), SDS((8, DFF), F32), SDS((8, DFF), F32),
                   SDS((1, DFF), F32), SDS((1, DFF), F32)],
        sem=("parallel",), vmem_mib=40, comm=comm)


def _ffn_down_bwd(dh2b, w_down, comm=()):
    tm = 256

    def body(d_ref, w_ref, o_ref):
        o_ref[...] = _dot(d_ref[...], w_ref[...], 1, 1)

    return _call(
        body, (dh2b, w_down), name="ffn_down_bwd", grid=(T // tm,),
        in_specs=[pl.BlockSpec((tm, D), lambda i: (i, 0)), pl.BlockSpec((DFF, D), lambda i: (0, 0))],
        out_specs=[pl.BlockSpec((tm, DFF), lambda i: (i, 0))], out_shape=[SDS((T, DFF), F32)],
        sem=("parallel",), vmem_mib=40, comm=comm)


def _norm_matmul_bwd(name, a_list, w_t, k_offsets, xin, g, dres, want_bf16, comm=()):
    tm = 256
    ks = [a.shape[1] for a in a_list]
    n_a = len(a_list)

    def body(*refs):
        a_refs = refs[:n_a]
        w_ref, x_ref, g_ref, r_ref = refs[n_a:n_a + 4]
        outs = refs[n_a + 4:]
        dx_ref, dg_ref = outs[0], outs[-1]

        @pl.when(pl.program_id(0) == 0)
        def _():
            dg_ref[...] = jnp.zeros_like(dg_ref)

        du = _dot(a_refs[0][...], w_ref[k_offsets[0]:k_offsets[0] + ks[0], :], 1, 0)
        for k in range(1, n_a):
            du = du + _dot(a_refs[k][...], w_ref[k_offsets[k]:k_offsets[k] + ks[k], :], 1, 0)
        x = x_ref[...]
        r = lax.rsqrt(jnp.mean(x * x, axis=-1, keepdims=True) + EPS)
        dx, dg = _rms_bwd(du, x, r, g_ref[...])
        dx = r_ref[...] + dx
        dx_ref[...] = dx
        if want_bf16:
            outs[1][...] = dx.astype(BF)
        dg_ref[...] += dg

    tile = lambda c: pl.BlockSpec((tm, c), lambda i: (i, 0))
    out_specs = [tile(D)] + ([tile(D)] if want_bf16 else []) + [pl.BlockSpec((1, D), lambda i: (0, 0))]
    out_shape = [SDS((T, D), F32)] + ([SDS((T, D), BF)] if want_bf16 else []) + [SDS((1, D), F32)]
    return _call(
        body, (*a_list, w_t, xin, g, dres), name=name, grid=(T // tm,),
        in_specs=[tile(k) for k in ks] + [pl.BlockSpec(w_t.shape, lambda i: (0, 0)), tile(D),
                                           pl.BlockSpec((1, D), lambda i: (0, 0)), tile(D)],
        out_specs=out_specs, out_shape=out_shape, sem=("arbitrary",), vmem_mib=56, comm=comm)


def _out_bwd(dh1b, w_out):
    tm = 256

    def body(d_ref, w_ref, o_ref):
        o_ref[...] = _dot(d_ref[...], w_ref[...], 1, 1)

    return _call(
        body, (dh1b, w_out), name="out_bwd", grid=(T // tm,),
        in_specs=[pl.BlockSpec((tm, D), lambda i: (i, 0)), pl.BlockSpec((D, D), lambda i: (0, 0))],
        out_specs=[pl.BlockSpec((tm, D), lambda i: (i, 0))], out_shape=[SDS((T, D), F32)],
        sem=("parallel",), vmem_mib=32)


def _wgrad(name, a_list, b, comm=()):
    tm = 256
    steps = [a.shape[1] // tm for a in a_list]
    starts = [sum(steps[:k]) for k in range(len(a_list))]
    n_a = len(a_list)

    def body(*refs):
        a_refs, b_ref, o32_ref, obf_ref = refs[:n_a], refs[n_a], refs[n_a + 1], refs[n_a + 2]
        i = pl.program_id(0)
        for k in range(n_a):
            @pl.when((i >= starts[k]) & (i < starts[k] + steps[k]))
            def _(k=k):
                r = _dot(a_refs[k][...], b_ref[...], 0, 0)
                o32_ref[...] = r
                obf_ref[...] = r.astype(BF)

    def a_spec(k):
        return pl.BlockSpec((T, tm), lambda i: (0, jnp.clip(i - starts[k], 0, steps[k] - 1)))

    m_total = tm * sum(steps)
    return _call(
        body, (*a_list, b), name=name, grid=(sum(steps),),
        in_specs=[a_spec(k) for k in range(n_a)] + [pl.BlockSpec((T, D), lambda i: (0, 0))],
        out_specs=[pl.BlockSpec((tm, D), lambda i: (i, 0)), pl.BlockSpec((tm, D), lambda i: (i, 0))],
        out_shape=[SDS((m_total, D), F32), SDS((m_total, D), BF)], sem=("parallel",), vmem_mib=40, comm=comm)


def _chip_sum(name, g32, from_sib, core, chip):
    h = g32.shape[1]
    th = h // 2

    def body(core_ref, chip_ref, g_ref, s_ref, pbf_ref, own_ref):
        p = g_ref[0] + s_ref[0].astype(F32)
        pbf_ref[0] = p.astype(BF)

        @pl.when(pl.program_id(1) == chip_ref[0])
        def _():
            own_ref[...] = p

    grid_spec = pltpu.PrefetchScalarGridSpec(
        num_scalar_prefetch=2, grid=(h // th, N_CHIPS),
        in_specs=[pl.BlockSpec((1, th, D), lambda t, jj, core_ref, chip_ref: (2 * jj + core_ref[0], t, 0)),
                  pl.BlockSpec((1, th, D), lambda t, jj, core_ref, chip_ref: (jj, t, 0))],
        out_specs=[pl.BlockSpec((1, th, D), lambda t, jj, core_ref, chip_ref: (jj, t, 0)),
                   pl.BlockSpec((th, D), lambda t, jj, core_ref, chip_ref: (t, 0))],
    )
    return _pcall(
        body, name=name, grid_spec=grid_spec, out_shape=_in_hbm([SDS((N_CHIPS, h, D), BF), SDS((h, D), F32)]),
        compiler_params=_params(("arbitrary", "arbitrary"), 32),
    )(core, chip, *_from_hbm(g32, from_sib))


def _final_sum(name, own, from_chips, core):
    h = own.shape[0]

    def body(core_ref, o_ref, r_ref, f_ref):
        f_ref[0] = ((o_ref[...] + r_ref[0].astype(F32)) + r_ref[1].astype(F32)) + r_ref[2].astype(F32)

    grid_spec = pltpu.PrefetchScalarGridSpec(
        num_scalar_prefetch=1, grid=(1,),
        in_specs=[pl.BlockSpec((h, D), lambda i, core_ref: (0, 0)), pl.BlockSpec((3, h, D), lambda i, core_ref: (0, 0, 0))],
        out_specs=pl.BlockSpec((1, h, D), lambda i, core_ref: (core_ref[0], 0, 0)),
    )
    return _pcall(body, name=name, grid_spec=grid_spec, out_shape=pltpu.HBM((2, h, D), F32),
                  compiler_params=_params(("arbitrary",), 40))(core, *_from_hbm(own, from_chips))


def _adam_math(w, g, m, v):
    nm = ADAM_B1 * m + (1.0 - ADAM_B1) * g
    nv = ADAM_B2 * v + (1.0 - ADAM_B2) * (g * g)
    m_hat = nm / (1.0 - ADAM_B1 ** ADAM_STEP)
    v_hat = nv / (1.0 - ADAM_B2 ** ADAM_STEP)
    return -ADAM_LR * (m_hat / (jnp.sqrt(v_hat) + ADAM_EPS) + ADAM_WD * w), nm, nv


def _adamw(name, w, g, m, v, tr, comm=()):
    rows, cols = w.shape

    def body(w_ref, g_ref, m_ref, v_ref, d_ref, nm_ref, nv_ref):
        d_ref[...], nm_ref[...], nv_ref[...] = _adam_math(w_ref[...], g_ref[...], m_ref[...], v_ref[...])

    spec = pl.BlockSpec((tr, cols), lambda i: (i, 0))
    return _call(body, (w, g, m, v), name=name, grid=(rows // tr,), in_specs=[spec] * 4, out_specs=[spec] * 3,
                 out_shape=[SDS((rows, cols), F32)] * 3, sem=("parallel",), vmem_mib=32, comm=comm, free=(0, 2, 3))


C_G1, C_G2, C_GCO, C_GAO, C_DCW, C_DQG, C_DKG, C_SINK, C_SQ = 0, 1024, 2048, 2560, 3072, 4608, 4736, 4864, 5632
P_W = C_SQ + 128


def _pack_small(me, dfwg, dfwv, dfbg, dfbv, dg1, dg2, dgco, dgao, dcw8, dqg, dkg, dsink, sq):
    def body(me_ref, dfwg_r, dfwv_r, dfbg_r, dfbv_r, dg1_r, dg2_r, dgco_r, dgao_r, dcw_r, dqg_r, dkg_r, dsink_r, sq_r, o):
        o[...] = jnp.zeros_like(o)
        o[0, :, 0:DFF] = dfwg_r[...]
        o[0, :, DFF:2 * DFF] = dfwv_r[...]
        o[0, 3:4, 0:DFF] = dfbg_r[...]
        o[0, 3:4, DFF:2 * DFF] = dfbv_r[...]
        o[0, 4:5, C_G1:C_G1 + D] = dg1_r[...]
        o[0, 4:5, C_G2:C_G2 + D] = dg2_r[...]
        o[0, 4:5, C_GCO:C_GCO + CW] = dgco_r[...]
        o[0, 4:5, C_GAO:C_GAO + AW] = dgao_r[...]
        for r in range(3):
            o[0, 4:5, C_DCW + r * CW:C_DCW + (r + 1) * CW] = dcw_r[r:r + 1, :]
        o[0, 4:5, C_DQG:C_DQG + HD] = dqg_r[...]
        o[0, 4:5, C_DKG:C_DKG + HD] = dkg_r[...]
        o[0, 4:5, C_SINK:C_SINK + 128] = dsink_r[...]
        o[0, :, C_SQ:C_SQ + 128] = sq_r[...]

    ins = (dfwg, dfwv, dfbg, dfbv, dg1, dg2, dgco, dgao, dcw8, dqg, dkg, dsink, sq)
    grid_spec = pltpu.PrefetchScalarGridSpec(
        num_scalar_prefetch=1, grid=(1,),
        in_specs=[pl.BlockSpec(a.shape, lambda i, me_ref: (0, 0)) for a in ins],
        out_specs=pl.BlockSpec((1, 8, P_W), lambda i, me_ref: (me_ref[0], 0, 0)),
    )
    return _pcall(body, name="pack_small", grid_spec=grid_spec, out_shape=SDS((N_DEV, 8, P_W), F32),
                  compiler_params=_params(("arbitrary",)))(me, *ins)


N_SMALL = 11


def _small_adam(chip, p_all, tbl_all, ws, ms, vs):
    fw_cols = 2 * DFF // N_CHIPS
    cw_cols = CW // N_CHIPS

    def body(chip_ref, p_ref, fw_ref, cw0_ref, cw1_ref, cw2_ref, tbl_ref, *refs):
        w_r, m_r, v_r = refs[0:N_SMALL], refs[N_SMALL:2 * N_SMALL], refs[2 * N_SMALL:3 * N_SMALL]
        outs = refs[3 * N_SMALL:]
        g_o, d_o, nm_o, nv_o = (outs[k * N_SMALL:(k + 1) * N_SMALL] for k in range(4))
        loss_o = outs[4 * N_SMALL]

        def total(ref):
            s = ref[0]
            for k in range(1, N_DEV):
                s = s + ref[k]
            return s

        S = total(p_ref)
        fw = total(fw_ref)
        cws = [total(r) for r in (cw0_ref, cw1_ref, cw2_ref)]

        def step(i, g, at):
            d, nm, nv = _adam_math(w_r[i][at], g, m_r[i][at], v_r[i][at])
            g_o[i][at], d_o[i][at], nm_o[i][at], nv_o[i][at] = g, d, nm, nv

        everything = (slice(None), slice(None))
        step(0, S[4:5, C_G1:C_G1 + D], everything)
        for r in range(3):
            step(1, cws[r][4:5, :], (0, slice(r, r + 1), slice(None)))
        step(2, S[4:5, C_DQG:C_DQG + HD], everything)
        step(3, S[4:5, C_DKG:C_DKG + HD], everything)
        step(4, total(tbl_ref), everything)
        step(5, S[4:5, C_SINK:C_SINK + NH], everything)
        step(6, S[4:5, C_GCO:C_GCO + CW], everything)
        step(7, S[4:5, C_GAO:C_GAO + AW], everything)
        step(8, S[4:5, C_G2:C_G2 + D], everything)
        step(9, fw[0:3, :], (0, slice(None), slice(None)))
        step(10, S[3:4, 0:2 * DFF], everything)
        sq = S[:, C_SQ:C_SQ + 128]
        loss_o[...] = jnp.sum(jnp.sum(sq, axis=1, keepdims=True), axis=0, keepdims=True) * (0.5 / D)

    def full(a):
        n = len(a.shape)
        return pl.BlockSpec(a.shape, lambda i, chip_ref: (0,) * n)

    params = [*ws, *ms, *vs]
    grid_spec = pltpu.PrefetchScalarGridSpec(
        num_scalar_prefetch=1, grid=(1,),
        in_specs=[full(p_all),
                  pl.BlockSpec((N_DEV, 8, fw_cols), lambda i, chip_ref: (0, 0, chip_ref[0])),
                  *[pl.BlockSpec((N_DEV, 8, cw_cols), lambda i, chip_ref, r=r: (0, 0, (C_DCW + r * CW) // cw_cols + chip_ref[0]))
                    for r in range(3)],
                  full(tbl_all), *[full(a) for a in params]],
        out_specs=[full(a) for a in ws] * 4 + [pl.BlockSpec((1, 1), lambda i, chip_ref: (0, 0))],
    )
    out = _pcall(
        body, name="small_adam", grid_spec=grid_spec,
        out_shape=[SDS(a.shape, F32) for a in ws] * 4 + [SDS((1, 1), F32)],
        compiler_params=_params(("arbitrary",), 32),
    )(chip, p_all, p_all, p_all, p_all, p_all, tbl_all, *params)
    return out[0:N_SMALL], out[N_SMALL:2 * N_SMALL], out[2 * N_SMALL:3 * N_SMALL], out[3 * N_SMALL:4 * N_SMALL], out[4 * N_SMALL]


PLACE_STEPS = 4


def _place_specs(shards):
    rows = [s.shape[0] // PLACE_STEPS for s in shards]
    return ([pl.BlockSpec((r, D), lambda i, chip_ref: (i, 0)) for r in rows],
            [pl.BlockSpec((r, D), lambda i, chip_ref: (chip_ref[0] * PLACE_STEPS + i, 0)) for r in rows],
            [SDS((N_CHIPS * s.shape[0], D), BF) for s in shards])


def _place_first(chip, shard, conv_w, ffn_conv_w):
    def body(chip_ref, a, s0, s1, o, t0, t1):
        o[...] = a[...].astype(BF)

        @pl.when(pl.program_id(0) == 0)
        def _():
            for s, t in ((s0, t0), (s1, t1)):
                t[...] = jnp.zeros_like(t)
                t[0, 0:3, :] = s[...]

    ins, outs, shapes = _place_specs([shard])
    taps = (conv_w, ffn_conv_w)
    return _call(
        body, (shard, conv_w, ffn_conv_w), name="place_first", grid=(PLACE_STEPS,), prefetch=(chip,),
        in_specs=ins + [pl.BlockSpec(s.shape, lambda i, chip_ref: (0, 0)) for s in taps],
        out_specs=outs + [pl.BlockSpec((1, 8, s.shape[1]), lambda i, chip_ref: (chip_ref[0], 0, 0)) for s in taps],
        out_shape=shapes + [SDS((N_CHIPS, 8, s.shape[1]), F32) for s in taps],
        sem=("arbitrary",), vmem_mib=32, free=(0, 1, 2))


def _place_rest(chip, shards, table, bucket, comm):
    n = len(shards)

    def body(chip_ref, *refs):
        a, (tab_ref, bk_ref), o, bias_ref = refs[:n], refs[n:n + 2], refs[n + 2:2 * n + 2], refs[2 * n + 2]
        for src, dst in zip(a, o):
            dst[...] = src[...].astype(BF)

        @pl.when(pl.program_id(0) == 0)
        def _():
            bk = bk_ref[...]
            eq = [bk == b for b in range(NBUCKET)]
            for h in range(NH):
                acc = jnp.zeros((BLK, 2 * BLK), F32)
                for b in range(NBUCKET):
                    acc = jnp.where(eq[b], tab_ref[b, h], acc)
                bias_ref[h * BLK:(h + 1) * BLK, :] = acc

    ins, outs, shapes = _place_specs(shards)
    return _call(
        body, (*shards, table, bucket), name="place_rest", grid=(PLACE_STEPS,), prefetch=(chip,),
        in_specs=ins + [pl.BlockSpec(memory_space=pltpu.SMEM), pl.BlockSpec(bucket.shape, lambda i, chip_ref: (0, 0))],
        out_specs=outs + [pl.BlockSpec((NH * BLK, 2 * BLK), lambda i, chip_ref: (0, 0))],
        out_shape=shapes + [SDS((NH * BLK, 2 * BLK), F32)],
        sem=("arbitrary",), vmem_mib=32, comm=comm, free=tuple(range(n + 2)))


def kernel(x, norm_mix_g, w_in, conv_w, q_norm_g, k_norm_g, rel_bias_table, sinks, out_norm_conv_g, out_norm_attn_g, w_out, norm_ffn_g, w_up, ffn_conv_w, ffn_conv_b, w_down, loss_target, m_norm_mix_g, m_w_in, m_conv_w, m_q_norm_g, m_k_norm_g, m_rel_bias_table, m_sinks, m_out_norm_conv_g, m_out_norm_attn_g, m_w_out, m_norm_ffn_g, m_w_up, m_ffn_conv_w, m_ffn_conv_b, m_w_down, v_norm_mix_g, v_w_in, v_conv_w, v_q_norm_g, v_k_norm_g, v_rel_bias_table, v_sinks, v_out_norm_conv_g, v_out_norm_attn_g, v_w_out, v_norm_ffn_g, v_w_up, v_ffn_conv_w, v_ffn_conv_b, v_w_down):
    as_arg = lambda i: jnp.reshape(i, (1,)).astype(jnp.int32)
    chip = as_arg(2 * lax.axis_index("x") + lax.axis_index("y"))
    core = as_arg(lax.axis_index("c"))
    me = 2 * chip + core
    xs, tgt = x[0], loss_target[0]
    qg, kg, gco, gao, g1, g2, fb = q_norm_g, k_norm_g, out_norm_conv_g, out_norm_attn_g, norm_mix_g, norm_ffn_g, ffn_conv_b
    pieces = lambda g: g.reshape(N_DEV, g.shape[0] // N_DEV, D)
    whole = lambda f: f.reshape(2 * f.shape[1], D)

    bucket = jnp.asarray(_bucket_table())
    p_in, p_cw, p_fw = _place_first(chip, w_in[0].T, conv_w[0], ffn_conv_w[0])
    p_out, p_up, p_down, bias, w_int, cw_all, fw_all = _place_rest(
        chip, [w_out[0], w_up[0].T, w_down[0]], rel_bias_table, bucket,
        comm=[_t_gather(p_in), _t_small_weights(p_cw), _t_small_weights(p_fw)])
    cw8 = jnp.transpose(cw_all, (1, 0, 2)).reshape(8, CW)
    fw8 = jnp.transpose(fw_all, (1, 0, 2)).reshape(8, 2 * DFF)

    proj, u1, w_out_f = _inproj(xs, g1, w_int, comm=[_t_gather(p_out)])
    y, w_upt = _mix_fwd(proj, sinks, cw8, qg, kg, gco, gao, bias, comm=[_t_gather(p_up)])
    h1, u2 = _outproj(y, w_out_f, xs, g2)
    up, w_down_f = _ffn_up(u2, w_upt, comm=[_t_gather(p_down)])
    a, = _ffn_act(up, fw8, fb)
    dh2, dh2b, sq = _ffn_down(a, w_down_f, h1, tgt)

    gd32, gdbf = _wgrad("wgrad_down", [a], dh2b)
    da, sib_down = _ffn_down_bwd(dh2b, w_down_f, comm=[_t_sibling(pieces(gdbf))])
    pbf_down, own_down = _chip_sum("chip_sum_w_down", pieces(gd32), sib_down, core, chip)
    dug, duv, dfwg, dfwv, dfbg, dfbv, chips_down = _ffn_act_bwd(up, da, fw8, fb, comm=[_t_chips(pbf_down)])
    fin_down = _final_sum("final_sum_w_down", own_down, chips_down, core)
    gu32, gubf = _wgrad("wgrad_up", [dug, duv], u2)
    dh1, dh1b, dg2, sib_up, fin_down = _norm_matmul_bwd(
        "ffn_up_bwd", [dug, duv], w_upt, [0, DFF], h1, g2, dh2, True, comm=[_t_sibling(pieces(gubf)), _t_swap(fin_down)])
    pbf_up, own_up = _chip_sum("chip_sum_w_up", pieces(gu32), sib_up, core, chip)
    go32, gobf = _wgrad("wgrad_out", [y], dh1b)
    dy, = _out_bwd(dh1b, w_out_f)
    dproj, dcw8, dqg, dkg, dgco, dgao, dsink, dbias, chips_up, sib_out = _mix_bwd(
        proj, dy, sinks, cw8, qg, kg, gco, gao, bias, comm=[_t_chips(pbf_up), _t_sibling(pieces(gobf))])
    fin_up = _final_sum("final_sum_w_up", own_up, chips_up, core)
    pbf_out, own_out = _chip_sum("chip_sum_w_out", pieces(go32), sib_out, core, chip)
    tbl_all = _band_bias_bwd(dbias, bucket, me)
    gi32, gibf, chips_out, fin_up = _wgrad("wgrad_in", [dproj], u1, comm=[_t_chips(pbf_out), _t_swap(fin_up)])
    fin_out = _final_sum("final_sum_w_out", own_out, chips_out, core)
    sib_in, tbl_all, fin_out = _comm_call(
        "to_sibling_last", [_t_sibling(pieces(gibf)), _t_allgather(tbl_all), _t_swap(fin_out)])
    pbf_in, own_in = _chip_sum("chip_sum_w_in", pieces(gi32), sib_in, core, chip)
    dx, dg1, chips_in = _norm_matmul_bwd("in_bwd", [dproj], w_int, [0], xs, g1, dh1, False, comm=[_t_chips(pbf_in)])
    p_all = _pack_small(me, dfwg, dfwv, dfbg, dfbv, dg1, dg2, dgco, dgao, dcw8, dqg, dkg, dsink, sq)
    fin_in = _final_sum("final_sum_w_in", own_in, chips_in, core)
    p_all, fin_in = _comm_call("gather_small", [_t_allgather(p_all), _t_swap(fin_in)])

    g_w_in, g_w_out, g_w_up, g_w_down = whole(fin_in).T, whole(fin_out), whole(fin_up).T, whole(fin_down)
    d_down, nm_down, nv_down = _adamw("adamw_w_down", w_down[0], g_w_down, m_w_down[0], v_w_down[0], 352)
    d_up, nm_up, nv_up = _adamw("adamw_w_up", w_up[0], g_w_up, m_w_up[0], v_w_up[0], 256)
    d_out, nm_out, nv_out = _adamw("adamw_w_out", w_out[0], g_w_out, m_w_out[0], v_w_out[0], 256)
    d_in, nm_in, nv_in = _adamw("adamw_w_in", w_in[0], g_w_in, m_w_in[0], v_w_in[0], 256)
    sw = [norm_mix_g, conv_w, q_norm_g, k_norm_g, rel_bias_table, sinks, out_norm_conv_g, out_norm_attn_g,
          norm_ffn_g, ffn_conv_w, ffn_conv_b]
    smm = [m_norm_mix_g, m_conv_w, m_q_norm_g, m_k_norm_g, m_rel_bias_table, m_sinks, m_out_norm_conv_g,
           m_out_norm_attn_g, m_norm_ffn_g, m_ffn_conv_w, m_ffn_conv_b]
    smv = [v_norm_mix_g, v_conv_w, v_q_norm_g, v_k_norm_g, v_rel_bias_table, v_sinks, v_out_norm_conv_g,
           v_out_norm_attn_g, v_norm_ffn_g, v_ffn_conv_w, v_ffn_conv_b]
    sg, sd, snm, snv, loss = _small_adam(chip, p_all, tbl_all, sw, smm, smv)

    def order(s, b_in, b_out, b_up, b_down):
        return (s[0], b_in[None], s[1], s[2], s[3], s[4], s[5], s[6], s[7], b_out[None], s[8], b_up[None],
                s[9], s[10], b_down[None])

    return (loss.reshape(()), dx[None],
            *order(sg, g_w_in, g_w_out, g_w_up, g_w_down),
            *order(sd, d_in, d_out, d_up, d_down),
            *order(snm, nm_in, nm_out, nm_up, nm_down),
            *order(snv, nv_in, nv_out, nv_up, nv_down))
```

```python
import functools
import math

import numpy as np

import jax
import jax.numpy as jnp
from jax import lax
from jax.experimental import pallas as pl
from jax.experimental.pallas import tpu as pltpu

F32 = jnp.float32
BF = jnp.bfloat16
SDS = jax.ShapeDtypeStruct

T = 2048
D = 1024
CW = 512
AW = 512
HD = 64
NH = 8
NKV = 2
GQ = 4
INW = 2304
DFF = 2816
BLK = 128
NB = T // BLK
NBUCKET = 32
EPS = 1e-6
NEG_INF = -1e30
N_CHIPS = 4
N_DEV = 8

ADAM_LR = 0.001
ADAM_B1 = 0.9
ADAM_B2 = 0.999
ADAM_EPS = 1e-08
ADAM_WD = 0.01
ADAM_STEP = 10

MIB = 1024 * 1024
MESH = pl.DeviceIdType.MESH
ANY = pl.BlockSpec(memory_space=pl.ANY)

_pcall = pl.pallas_call


def _params(sem=None, vmem_mib=None):
    kw = {}
    if sem is not None:
        kw["dimension_semantics"] = sem
    if vmem_mib is not None:
        kw["vmem_limit_bytes"] = vmem_mib * MIB
    return pltpu.CompilerParams(**kw)


def _dot(a, b, ca, cb):
    return lax.dot_general(a, b, (((ca,), (cb,)), ((), ())), preferred_element_type=F32)


def _rms_bwd(dy, x, r, g):
    dg = jnp.sum(dy * (x * r), axis=0, keepdims=True)
    dgx = dy * g
    dx = r * dgx - x * (r * r * r) * jnp.mean(x * dgx, axis=-1, keepdims=True)
    return dx, dg


def _where():
    x, y, c = lax.axis_index("x"), lax.axis_index("y"), lax.axis_index("c")
    return x, y, c, [(1 - x, y), (x, 1 - y), (1 - x, 1 - y)]


def _rcopy(src, dst, ssem, rsem, dev):
    return pltpu.make_async_remote_copy(src_ref=src, dst_ref=dst, send_sem=ssem, recv_sem=rsem, device_id=dev,
                                        device_id_type=MESH)


class _Task:
    def __init__(self, ins, outs, alias, n_sem, start, finish):
        self.ins, self.outs, self.alias, self.n_sem, self.start, self.finish = ins, outs, alias, n_sem, start, finish


def _t_gather(placed):
    R = placed.shape[0] // N_CHIPS

    def rows(chip_index, core):
        return pl.ds(pl.multiple_of(chip_index * R + core * (R // 2), 16), R // 2)

    def start(cin, cout, ss, rs, b):
        x, y, c, chips = _where()
        mine = cout[0].at[rows(2 * x + y, c)]
        for r, (px, py) in enumerate(chips):
            _rcopy(mine, mine, ss.at[b + r], rs.at[b + r], (px, py, c)).start()

    def finish(cin, cout, ss, rs, b):
        x, y, c, chips = _where()
        buf = cout[0]
        sib = (x, y, 1 - c)
        for r, (px, py) in enumerate(chips):
            got = buf.at[rows(2 * px + py, c)]
            _rcopy(got, got, ss.at[b + r], rs.at[b + r], (px, py, c)).wait_recv()
            _rcopy(got, got, ss.at[b + 3 + r], rs.at[b + 3 + r], sib).start()
        for r, (px, py) in enumerate(chips):
            got = buf.at[rows(2 * px + py, 1 - c)]
            _rcopy(got, got, ss.at[b + 3 + r], rs.at[b + 3 + r], sib).wait_recv()
        mine = buf.at[rows(2 * x + y, c)]
        for r in range(6):
            _rcopy(mine, mine, ss.at[b + r], rs.at[b + r], sib).wait_send()

    return _Task([placed], [SDS(placed.shape, placed.dtype)], [(0, 0)], 6, start, finish)


def _t_small_weights(buf):
    def start(cin, cout, ss, rs, b):
        x, y, c, chips = _where()
        mine = cout[0].at[2 * x + y]
        for r, (px, py) in enumerate(chips):
            _rcopy(mine, mine, ss.at[b + r], rs.at[b + r], (px, py, c)).start()

    def finish(cin, cout, ss, rs, b):
        x, y, c, chips = _where()
        for r, (px, py) in enumerate(chips):
            got = cout[0].at[2 * px + py]
            _rcopy(got, got, ss.at[b + r], rs.at[b + r], (px, py, c)).wait_recv()
        for r, (px, py) in enumerate(chips):
            mine = cout[0].at[2 * x + y]
            _rcopy(mine, mine, ss.at[b + r], rs.at[b + r], (px, py, c)).wait_send()

    return _Task([buf], [SDS(buf.shape, buf.dtype)], [(0, 0)], 3, start, finish)


def _t_sibling(gbf):
    def start(cin, cout, ss, rs, b):
        x, y, c, _ = _where()
        for jj in range(N_CHIPS):
            _rcopy(cin[0].at[2 * jj + (1 - c)], cout[0].at[jj], ss.at[b + jj], rs.at[b + jj], (x, y, 1 - c)).start()

    def finish(cin, cout, ss, rs, b):
        x, y, c, _ = _where()
        for jj in range(N_CHIPS):
            got = cout[0].at[jj]
            _rcopy(got, got, ss.at[b + jj], rs.at[b + jj], (x, y, 1 - c)).wait_recv()
        for jj in range(N_CHIPS):
            got = cout[0].at[jj]
            _rcopy(got, got, ss.at[b + jj], rs.at[b + jj], (x, y, 1 - c)).wait_send()

    return _Task([gbf], [SDS((N_CHIPS,) + gbf.shape[1:], BF)], [], N_CHIPS, start, finish)


def _t_chips(pbf):
    def start(cin, cout, ss, rs, b):
        x, y, c, chips = _where()
        for r, (px, py) in enumerate(chips):
            _rcopy(cin[0].at[2 * px + py], cout[0].at[r], ss.at[b + r], rs.at[b + r], (px, py, c)).start()

    def finish(cin, cout, ss, rs, b):
        x, y, c, chips = _where()
        for r, (px, py) in enumerate(chips):
            got = cout[0].at[r]
            _rcopy(got, got, ss.at[b + r], rs.at[b + r], (px, py, c)).wait_recv()
        for r, (px, py) in enumerate(chips):
            got = cout[0].at[r]
            _rcopy(got, got, ss.at[b + r], rs.at[b + r], (px, py, c)).wait_send()

    return _Task([pbf], [SDS((3,) + pbf.shape[1:], BF)], [], 3, start, finish)


def _t_swap(fin):
    def start(cin, cout, ss, rs, b):
        x, y, c, _ = _where()
        mine = cout[0].at[c]
        _rcopy(mine, mine, ss.at[b], rs.at[b], (x, y, 1 - c)).start()

    def finish(cin, cout, ss, rs, b):
        x, y, c, _ = _where()
        got = cout[0].at[1 - c]
        _rcopy(got, got, ss.at[b], rs.at[b], (x, y, 1 - c)).wait_recv()
        _rcopy(got, got, ss.at[b], rs.at[b], (x, y, 1 - c)).wait_send()

    return _Task([fin], [SDS(fin.shape, fin.dtype)], [(0, 0)], 1, start, finish)


def _t_allgather(buf):
    def peers():
        x, y, c, _ = _where()
        out = []
        for rel in range(1, N_DEV):
            px, py, pc = x ^ ((rel >> 2) & 1), y ^ ((rel >> 1) & 1), c ^ (rel & 1)
            out.append((rel - 1, 4 * px + 2 * py + pc, (px, py, pc)))
        return 4 * x + 2 * y + c, out

    def start(cin, cout, ss, rs, b):
        me, ps = peers()
        mine = cout[0].at[me]
        for k, _, dev in ps:
            _rcopy(mine, mine, ss.at[b + k], rs.at[b + k], dev).start()

    def finish(cin, cout, ss, rs, b):
        me, ps = peers()
        for k, pidx, dev in ps:
            got = cout[0].at[pidx]
            _rcopy(got, got, ss.at[b + k], rs.at[b + k], dev).wait_recv()
        for k, _, dev in ps:
            mine = cout[0].at[me]
            _rcopy(mine, mine, ss.at[b + k], rs.at[b + k], dev).wait_send()

    return _Task([buf], [SDS(buf.shape, buf.dtype)], [(0, 0)], N_DEV - 1, start, finish)


def _run_tasks(comm, which, cin, cout, ss, rs):
    i0 = o0 = s0 = 0
    for t in comm:
        getattr(t, which)(cin[i0:i0 + len(t.ins)], cout[o0:o0 + len(t.outs)], ss, rs, s0)
        i0, o0, s0 = i0 + len(t.ins), o0 + len(t.outs), s0 + t.n_sem


def _from_hbm(*arrays):
    return [pltpu.with_memory_space_constraint(a, pltpu.HBM) for a in arrays]


def _in_hbm(shapes):
    return [pltpu.HBM(s.shape, s.dtype) for s in shapes]


def _comm_layout(comm, n_in, n_out):
    c_in = [a for t in comm for a in t.ins]
    c_out = [s for t in comm for s in t.outs]
    aliases, i0, o0 = {}, 0, 0
    for t in comm:
        for i, o in t.alias:
            aliases[n_in + i0 + i] = n_out + o0 + o
        i0, o0 = i0 + len(t.ins), o0 + len(t.outs)
    return c_in, c_out, aliases, sum(t.n_sem for t in comm)


def _call(body, operands, *, name, grid, in_specs, out_specs, out_shape, scratch_shapes=(), sem=None, vmem_mib=None, comm=(),
          free=(), prefetch=()):
    operands = [o if s.memory_space == pltpu.SMEM or k in free else pltpu.with_memory_space_constraint(o, pltpu.HBM)
                for k, (o, s) in enumerate(zip(operands, in_specs))]
    n_pre, n_in, n_out, n_scr = len(prefetch), len(in_specs), len(out_specs), len(scratch_shapes)
    c_in, c_out, aliases, n_sem = _comm_layout(comm, n_pre + n_in, n_out)
    sems = [pltpu.SemaphoreType.DMA((n_sem,)), pltpu.SemaphoreType.DMA((n_sem,))] if comm else []

    def wrapped(*refs):
        pre, refs = refs[:n_pre], refs[n_pre:]
        ins, cin = refs[:n_in], refs[n_in:n_in + len(c_in)]
        rest = refs[n_in + len(c_in):]
        outs, cout = rest[:n_out], rest[n_out:n_out + len(c_out)]
        rest = rest[n_out + len(c_out):]
        scr, csem = rest[:n_scr], rest[n_scr:]
        if not comm:
            return body(*pre, *ins, *outs, *scr)
        ids = [pl.program_id(k) for k in range(len(grid))]
        first = functools.reduce(jnp.logical_and, [i == 0 for i in ids])
        last = functools.reduce(jnp.logical_and, [i == n - 1 for i, n in zip(ids, grid)])
        pl.when(first)(lambda: _run_tasks(comm, "start", cin, cout, *csem))
        body(*pre, *ins, *outs, *scr)
        pl.when(last)(lambda: _run_tasks(comm, "finish", cin, cout, *csem))

    grid_spec = pltpu.PrefetchScalarGridSpec(
        num_scalar_prefetch=n_pre, grid=grid, in_specs=list(in_specs) + [ANY] * len(c_in),
        out_specs=list(out_specs) + [ANY] * len(c_out), scratch_shapes=list(scratch_shapes) + sems)
    return _pcall(
        wrapped, name=name, grid_spec=grid_spec, out_shape=_in_hbm(list(out_shape) + c_out), input_output_aliases=aliases,
        compiler_params=_params(("arbitrary",) * len(grid) if comm else sem, vmem_mib),
    )(*prefetch, *operands, *_from_hbm(*c_in))


def _comm_call(name, comm):
    c_in, c_out, aliases, n_sem = _comm_layout(comm, 0, 0)

    def body(*refs):
        cin, cout, (ss, rs) = refs[:len(c_in)], refs[len(c_in):len(c_in) + len(c_out)], refs[len(c_in) + len(c_out):]
        _run_tasks(comm, "start", cin, cout, ss, rs)
        _run_tasks(comm, "finish", cin, cout, ss, rs)

    return _pcall(
        body, name=name, in_specs=[ANY] * len(c_in), out_specs=[ANY] * len(c_out), out_shape=_in_hbm(c_out),
        scratch_shapes=[pltpu.SemaphoreType.DMA((n_sem,)), pltpu.SemaphoreType.DMA((n_sem,))],
        input_output_aliases=aliases,
    )(*_from_hbm(*c_in))


def _inproj(x, g1, w_int, comm=()):
    tm = 256

    def body(x_ref, g_ref, w_ref, proj_ref, u_ref):
        xf = x_ref[...]
        r = lax.rsqrt(jnp.mean(xf * xf, axis=-1, keepdims=True) + EPS)
        u = (xf * r * g_ref[...]).astype(BF)
        u_ref[...] = u
        proj_ref[...] = _dot(u, w_ref[...], 1, 1)

    return _call(
        body, (x, g1, w_int), name="inproj", grid=(T // tm,),
        in_specs=[pl.BlockSpec((tm, D), lambda i: (i, 0)), pl.BlockSpec((1, D), lambda i: (0, 0)),
                  pl.BlockSpec((INW, D), lambda i: (0, 0))],
        out_specs=[pl.BlockSpec((tm, INW), lambda i: (i, 0)), pl.BlockSpec((tm, D), lambda i: (i, 0))],
        out_shape=[SDS((T, INW), F32), SDS((T, D), BF)], sem=("parallel",), vmem_mib=40, comm=comm)


def _outproj(y, w_out, x, g2):
    tm = 256

    def body(y_ref, w_ref, x_ref, g_ref, h1_ref, u2_ref):
        h1 = x_ref[...] + _dot(y_ref[...], w_ref[...], 1, 0)
        h1_ref[...] = h1
        r = lax.rsqrt(jnp.mean(h1 * h1, axis=-1, keepdims=True) + EPS)
        u2_ref[...] = (h1 * r * g_ref[...]).astype(BF)

    return _call(
        body, (y, w_out, x, g2), name="outproj", grid=(T // tm,),
        in_specs=[pl.BlockSpec((tm, D), lambda i: (i, 0)), pl.BlockSpec((D, D), lambda i: (0, 0)),
                  pl.BlockSpec((tm, D), lambda i: (i, 0)), pl.BlockSpec((1, D), lambda i: (0, 0))],
        out_specs=[pl.BlockSpec((tm, D), lambda i: (i, 0)), pl.BlockSpec((tm, D), lambda i: (i, 0))],
        out_shape=[SDS((T, D), F32), SDS((T, D), BF)], sem=("parallel",), vmem_mib=32)


def _ffn_up(u2, w_upt, comm=()):
    tm, tn = 1024, 512

    def body(u_ref, w_ref, o_ref):
        o_ref[...] = _dot(u_ref[...], w_ref[...], 1, 1)

    return _call(
        body, (u2, w_upt), name="ffn_up", grid=(T // tm, 2 * DFF // tn),
        in_specs=[pl.BlockSpec((tm, D), lambda i, j: (i, 0)), pl.BlockSpec((tn, D), lambda i, j: (j, 0))],
        out_specs=[pl.BlockSpec((tm, tn), lambda i, j: (i, j))], out_shape=[SDS((T, 2 * DFF), F32)],
        sem=("parallel", "parallel"), vmem_mib=32, comm=comm)


def _ffn_down(a, w_down, h1, tgt):
    tm = 256

    def body(a_ref, w_ref, h1_ref, t_ref, dh_ref, dhb_ref, l_ref):
        @pl.when(pl.program_id(0) == 0)
        def _():
            l_ref[...] = jnp.zeros_like(l_ref)

        h2 = h1_ref[...] + _dot(a_ref[...], w_ref[...], 1, 0)
        e = h2 - t_ref[...]
        dh = e * (1.0 / D)
        dh_ref[...] = dh
        dhb_ref[...] = dh.astype(BF)
        e2 = jnp.sum((e * e).reshape(tm // 8, 8, D), axis=0)
        acc = e2[:, 0:128]
        for k in range(1, D // 128):
            acc = acc + e2[:, k * 128:(k + 1) * 128]
        l_ref[...] += acc

    return _call(
        body, (a, w_down, h1, tgt), name="ffn_down", grid=(T // tm,),
        in_specs=[pl.BlockSpec((tm, DFF), lambda i: (i, 0)), pl.BlockSpec((DFF, D), lambda i: (0, 0)),
                  pl.BlockSpec((tm, D), lambda i: (i, 0)), pl.BlockSpec((tm, D), lambda i: (i, 0))],
        out_specs=[pl.BlockSpec((tm, D), lambda i: (i, 0)), pl.BlockSpec((tm, D), lambda i: (i, 0)),
                   pl.BlockSpec((8, 128), lambda i: (0, 0))],
        out_shape=[SDS((T, D), F32), SDS((T, D), BF), SDS((8, 128), F32)], sem=("arbitrary",), vmem_mib=40)


def _bucket_table():
    q = np.arange(BLK, dtype=np.int32)[:, None]
    j = np.arange(2 * BLK, dtype=np.int32)[None, :]
    n = np.maximum(q + BLK - j, 0)
    nf = np.maximum(n, 1).astype(np.float32)
    max_exact = NBUCKET // 2
    large = max_exact + (np.log(nf / np.float32(max_exact)) / np.float32(math.log(BLK / max_exact))
                         * np.float32(NBUCKET - max_exact)).astype(np.int32)
    large = np.minimum(large, NBUCKET - 1)
    return np.where(n < max_exact, n, large).astype(np.int32)


def _band_bias_bwd(dbias, bucket, me):
    def body(me_ref, db_ref, bk_ref, o_ref):
        bk = bk_ref[...]
        for b in range(NBUCKET):
            m = bk == b
            for h in range(NH):
                v = jnp.where(m, db_ref[h * BLK:(h + 1) * BLK, :], 0.0)
                s = jnp.sum(jnp.sum(v, axis=1, keepdims=True), axis=0, keepdims=True)
                o_ref[0, b:b + 1, h:h + 1] = s

    grid_spec = pltpu.PrefetchScalarGridSpec(
        num_scalar_prefetch=1, grid=(1,),
        in_specs=[pl.BlockSpec((NH * BLK, 2 * BLK), lambda i, me_ref: (0, 0)),
                  pl.BlockSpec((BLK, 2 * BLK), lambda i, me_ref: (0, 0))],
        out_specs=pl.BlockSpec((1, NBUCKET, NH), lambda i, me_ref: (me_ref[0], 0, 0)),
    )
    return _pcall(body, name="band_bias_bwd", grid_spec=grid_spec, out_shape=SDS((N_DEV, NBUCKET, NH), F32),
                  compiler_params=_params(("arbitrary",)))(me, dbias, bucket)


def _mix_forward(P, zc8, zh8, pkv, first, cw, qg, kg, gco, gao, sink_ref, bias_ref):
    gate_b = P[:, 0:CW]
    gate_c = P[:, CW:2 * CW]
    hc = P[:, 2 * CW:3 * CW]
    z = gate_c * hc
    keep = jnp.where(first, 0.0, 1.0)
    zp = zc8 * zh8 * keep
    p1 = zp[7:8, :]
    p2 = zp[6:7, :]
    row = lax.broadcasted_iota(jnp.int32, (BLK, 1), 0)
    z1 = jnp.where(row == 0, p1, pltpu.roll(z, 1, 0))
    z2 = jnp.where(row == 0, p2, jnp.where(row == 1, p1, pltpu.roll(z, 2, 0)))
    cz = cw[0:1, :] * z2 + cw[1:2, :] * z1 + cw[2:3, :] * z
    y_conv = gate_b * cz

    scale = HD ** -0.5
    qi = lax.broadcasted_iota(jnp.int32, (GQ * BLK, 2 * BLK), 0) & (BLK - 1)
    kj = lax.broadcasted_iota(jnp.int32, (GQ * BLK, 2 * BLK), 1)
    dd = qi + BLK - kj
    first_key = jnp.where(first, BLK, 0)
    valid = (dd >= 0) & (dd < BLK) & (kj >= first_key)

    q0 = 3 * CW
    k0 = q0 + AW
    v0 = k0 + NKV * HD
    heads = []
    outs = []
    for kv in range(NKV):
        kb_raw = jnp.concatenate([pkv[:, kv * HD:(kv + 1) * HD], P[:, k0 + kv * HD:k0 + (kv + 1) * HD]], axis=0)
        rk = lax.rsqrt(jnp.mean(kb_raw * kb_raw, axis=-1, keepdims=True) + EPS)
        kb = (kb_raw * rk * kg).astype(BF)
        vb = jnp.concatenate([pkv[:, NKV * HD + kv * HD:NKV * HD + (kv + 1) * HD],
                              P[:, v0 + kv * HD:v0 + (kv + 1) * HD]], axis=0).astype(BF)
        q_raw, rq, qn = [], [], []
        for g in range(GQ):
            h = kv * GQ + g
            qh = P[:, q0 + h * HD:q0 + (h + 1) * HD]
            r = lax.rsqrt(jnp.mean(qh * qh, axis=-1, keepdims=True) + EPS)
            q_raw.append(qh)
            rq.append(r)
            qn.append(qh * r * qg)
        Q = jnp.concatenate(qn, axis=0).astype(BF)
        S = _dot(Q, kb, 1, 1) * scale + bias_ref[kv * GQ * BLK:(kv + 1) * GQ * BLK, :]
        S = jnp.where(valid, S, NEG_INF)
        sink = jnp.concatenate([jnp.full((BLK, 1), sink_ref[0, kv * GQ + g], F32) for g in range(GQ)], axis=0)
        m = jnp.maximum(jnp.max(S, axis=-1, keepdims=True), sink)
        p = jnp.exp(S - m)
        es = jnp.exp(sink - m)
        denom = jnp.sum(p, axis=-1, keepdims=True) + es
        probs = p / denom
        O = _dot(probs.astype(BF), vb, 1, 0)
        heads.append(dict(kb_raw=kb_raw, rk=rk, kb=kb, vb=vb, q_raw=q_raw, rq=rq, Q=Q, probs=probs,
                          psink=es / denom, O=O))
        outs += [O[g * BLK:(g + 1) * BLK, :] for g in range(GQ)]
    y_attn = jnp.concatenate(outs, axis=1)

    rc = lax.rsqrt(jnp.mean(y_conv * y_conv, axis=-1, keepdims=True) + EPS)
    ra = lax.rsqrt(jnp.mean(y_attn * y_attn, axis=-1, keepdims=True) + EPS)
    y = jnp.concatenate([y_conv * rc * gco, y_attn * ra * gao], axis=1)
    return dict(gate_b=gate_b, gate_c=gate_c, hc=hc, z=z, z1=z1, z2=z2, cz=cz, y_conv=y_conv, y_attn=y_attn,
                rc=rc, ra=ra, heads=heads, y=y, row=row, scale=scale)


def _mix_in_specs(blk):
    return [
        pl.BlockSpec(memory_space=pltpu.SMEM),
        pl.BlockSpec((BLK, INW), lambda s: (blk(s), 0)),
        pl.BlockSpec((8, CW), lambda s: (jnp.maximum(blk(s) * (BLK // 8) - 1, 0), 1)),
        pl.BlockSpec((8, CW), lambda s: (jnp.maximum(blk(s) * (BLK // 8) - 1, 0), 2)),
        pl.BlockSpec((BLK, 2 * NKV * HD), lambda s: (jnp.maximum(blk(s) - 1, 0), (3 * CW + AW) // (2 * NKV * HD))),
    ]


def _mix_param_specs():
    return [
        pl.BlockSpec((8, CW), lambda s: (0, 0)),
        pl.BlockSpec((1, HD), lambda s: (0, 0)),
        pl.BlockSpec((1, HD), lambda s: (0, 0)),
        pl.BlockSpec((1, CW), lambda s: (0, 0)),
        pl.BlockSpec((1, AW), lambda s: (0, 0)),
        pl.BlockSpec((NH * BLK, 2 * BLK), lambda s: (0, 0)),
    ]


def _mix_fwd(proj, sinks, cw8, qg, kg, gco, gao, bias, comm=()):
    def body(sink_ref, p_ref, zc_ref, zh_ref, pkv_ref, cw_ref, qg_ref, kg_ref, gco_ref, gao_ref, bias_ref, y_ref):
        first = pl.program_id(0) == 0
        f = _mix_forward(p_ref[...], zc_ref[...], zh_ref[...], pkv_ref[...], first, cw_ref[...], qg_ref[...],
                         kg_ref[...], gco_ref[...], gao_ref[...], sink_ref, bias_ref)
        y_ref[...] = f["y"].astype(BF)

    return _call(
        body, (sinks, proj, proj, proj, proj, cw8, qg, kg, gco, gao, bias), name="mix_fwd", grid=(NB,),
        in_specs=_mix_in_specs(lambda s: s) + _mix_param_specs(),
        out_specs=[pl.BlockSpec((BLK, D), lambda s: (s, 0))], out_shape=[SDS((T, D), BF)],
        sem=("parallel",), vmem_mib=32, comm=comm)


def _mix_bwd(proj, dy, sinks, cw8, qg, kg, gco, gao, bias, comm=()):
    def blk(s):
        return NB - 1 - s

    def body(sink_ref, p_ref, zc_ref, zh_ref, pkv_ref, dy_ref, cw_ref, qg_ref, kg_ref, gco_ref, gao_ref, bias_ref,
             dproj_ref, dcw_ref, dqg_ref, dkg_ref, dgco_ref, dgao_ref, dsink_ref, dbias_ref,
             ndcz_ref, dkc_ref, dvc_ref):
        s = pl.program_id(0)
        first = s == NB - 1

        @pl.when(s == 0)
        def _():
            for r in (dcw_ref, dqg_ref, dkg_ref, dgco_ref, dgao_ref, dsink_ref, dbias_ref, ndcz_ref, dkc_ref, dvc_ref):
                r[...] = jnp.zeros_like(r)

        cw = cw_ref[...]
        qg_v, kg_v, gco_v, gao_v = qg_ref[...], kg_ref[...], gco_ref[...], gao_ref[...]
        f = _mix_forward(p_ref[...], zc_ref[...], zh_ref[...], pkv_ref[...], first, cw, qg_v, kg_v, gco_v, gao_v,
                         sink_ref, bias_ref)
        dy = dy_ref[...]
        dyc, dgco = _rms_bwd(dy[:, 0:CW], f["y_conv"], f["rc"], gco_v)
        dya, dgao = _rms_bwd(dy[:, CW:CW + AW], f["y_attn"], f["ra"], gao_v)
        dgco_ref[...] += dgco
        dgao_ref[...] += dgao

        row = f["row"]
        dgate_b = dyc * f["cz"]
        dcz = dyc * f["gate_b"]
        dcw_ref[0:1, :] += jnp.sum(dcz * f["z2"], axis=0, keepdims=True)
        dcw_ref[1:2, :] += jnp.sum(dcz * f["z1"], axis=0, keepdims=True)
        dcw_ref[2:3, :] += jnp.sum(dcz * f["z"], axis=0, keepdims=True)
        nxt = ndcz_ref[...]
        n0 = nxt[0:1, :]
        n1 = nxt[1:2, :]
        d1 = jnp.where(row == BLK - 1, n0, pltpu.roll(dcz, BLK - 1, 0))
        d2 = jnp.where(row == BLK - 1, n1, jnp.where(row == BLK - 2, n0, pltpu.roll(dcz, BLK - 2, 0)))
        dz = cw[2:3, :] * dcz + cw[1:2, :] * d1 + cw[0:1, :] * d2
        ndcz_ref[...] = dcz[0:8, :]
        dproj_ref[:, 0:CW] = dgate_b.astype(BF)
        dproj_ref[:, CW:2 * CW] = (dz * f["hc"]).astype(BF)
        dproj_ref[:, 2 * CW:3 * CW] = (dz * f["gate_c"]).astype(BF)

        scale = f["scale"]
        lane = lax.broadcasted_iota(jnp.int32, (1, 128), 1)
        dq_cols, dk_cols, dv_cols = [], [], []
        for kv in range(NKV):
            hd = f["heads"][kv]
            dO = jnp.concatenate([dya[:, (kv * GQ + g) * HD:(kv * GQ + g + 1) * HD] for g in range(GQ)], axis=0)
            delta = jnp.sum(dO * hd["O"], axis=-1, keepdims=True)
            dOb = dO.astype(BF)
            dP = _dot(dOb, hd["vb"], 1, 1)
            dS = hd["probs"] * (dP - delta)
            dsk = hd["psink"] * delta
            for g in range(GQ):
                h = kv * GQ + g
                tot = jnp.sum(dsk[g * BLK:(g + 1) * BLK, :], axis=0, keepdims=True)
                dsink_ref[...] -= jnp.where(lane == h, tot, 0.0)
            dbias_ref[kv * GQ * BLK:(kv + 1) * GQ * BLK, :] += dS
            dSs = (dS * scale).astype(BF)
            dQ = _dot(dSs, hd["kb"], 1, 0)
            dKb = _dot(dSs, hd["Q"], 0, 0)
            dVb = _dot(hd["probs"].astype(BF), dOb, 0, 0)
            dkn = dKb[BLK:, :] + dkc_ref[:, kv * HD:(kv + 1) * HD]
            dvn = dVb[BLK:, :] + dvc_ref[:, kv * HD:(kv + 1) * HD]
            dkc_ref[:, kv * HD:(kv + 1) * HD] = dKb[:BLK, :]
            dvc_ref[:, kv * HD:(kv + 1) * HD] = dVb[:BLK, :]
            dk_raw, dkg = _rms_bwd(dkn, hd["kb_raw"][BLK:, :], hd["rk"][BLK:, :], kg_v)
            dkg_ref[...] += dkg
            dk_cols.append(dk_raw)
            dv_cols.append(dvn)
            for g in range(GQ):
                dq_raw, dqg = _rms_bwd(dQ[g * BLK:(g + 1) * BLK, :], hd["q_raw"][g], hd["rq"][g], qg_v)
                dqg_ref[...] += dqg
                dq_cols.append(dq_raw)
        dproj_ref[:, 3 * CW:INW] = jnp.concatenate(dq_cols + dk_cols + dv_cols, axis=1).astype(BF)

    small = lambda r, c: pl.BlockSpec((r, c), lambda s: (0, 0))
    return _call(
        body, (sinks, proj, proj, proj, proj, dy, cw8, qg, kg, gco, gao, bias), name="mix_bwd", grid=(NB,),
        in_specs=_mix_in_specs(blk) + [pl.BlockSpec((BLK, D), lambda s: (blk(s), 0))] + _mix_param_specs(),
        out_specs=[pl.BlockSpec((BLK, INW), lambda s: (blk(s), 0)), small(8, CW), small(1, HD), small(1, HD),
                   small(1, CW), small(1, AW), small(1, 128), small(NH * BLK, 2 * BLK)],
        out_shape=[SDS((T, INW), BF), SDS((8, CW), F32), SDS((1, HD), F32), SDS((1, HD), F32), SDS((1, CW), F32),
                   SDS((1, AW), F32), SDS((1, 128), F32), SDS((NH * BLK, 2 * BLK), F32)],
        scratch_shapes=[pltpu.VMEM((8, CW), F32), pltpu.VMEM((BLK, NKV * HD), F32), pltpu.VMEM((BLK, NKV * HD), F32)],
        sem=("arbitrary",), vmem_mib=40, comm=comm)


FT = 256
NFT = DFF // FT
RC = 64
NCH = T // RC


def _rows8(x):
    return jnp.sum(x.reshape(x.shape[0] // 8, 8, x.shape[1]), axis=0)


def _ffn_act_specs():
    return [
        pl.BlockSpec((T, FT), lambda j: (0, j)), pl.BlockSpec((T, FT), lambda j: (0, NFT + j)),
        pl.BlockSpec((8, FT), lambda j: (0, j)), pl.BlockSpec((8, FT), lambda j: (0, NFT + j)),
        pl.BlockSpec((1, FT), lambda j: (0, j)), pl.BlockSpec((1, FT), lambda j: (0, NFT + j)),
    ]


def _conv_rows(win, w, b, n):
    u = win[8:8 + n]
    u1 = pltpu.roll(win, 1, 0)[8:8 + n]
    u2 = pltpu.roll(win, 2, 0)[8:8 + n]
    return u2, u1, u, w[0:1, :] * u2 + w[1:2, :] * u1 + w[2:3, :] * u + b


def _ffn_act(up, fw8, fb):
    def body(ug_ref, uv_ref, wg_ref, wv_ref, bg_ref, bv_ref, a_ref):
        wg, wv, bg, bv = wg_ref[...], wv_ref[...], bg_ref[...], bv_ref[...]

        def chunk(win_g, win_v):
            gp = _conv_rows(win_g, wg, bg, RC)[3]
            vp = _conv_rows(win_v, wv, bv, RC)[3]
            return (gp * jax.nn.sigmoid(gp) * vp).astype(BF)

        zero = jnp.zeros((8, FT), F32)
        a_ref[0:RC, :] = chunk(jnp.concatenate([zero, ug_ref[0:RC, :]], axis=0),
                               jnp.concatenate([zero, uv_ref[0:RC, :]], axis=0))

        def step(i, carry):
            r0 = pl.multiple_of(i * RC, RC)
            win = pl.ds(r0 - 8, RC + 8)
            a_ref[pl.ds(r0, RC), :] = chunk(ug_ref[win, :], uv_ref[win, :])
            return carry

        lax.fori_loop(1, NCH, step, 0)

    return _call(
        body, (up, up, fw8, fw8, fb, fb), name="ffn_act", grid=(NFT,), in_specs=_ffn_act_specs(),
        out_specs=[pl.BlockSpec((T, FT), lambda j: (0, j))], out_shape=[SDS((T, DFF), BF)],
        sem=("parallel",), vmem_mib=40)


def _ffn_act_bwd(up, da, fw8, fb, comm=()):
    ext = RC + 8

    def body(ug_ref, uv_ref, wg_ref, wv_ref, bg_ref, bv_ref, da_ref,
             dug_ref, duv_ref, dwg_ref, dwv_ref, dbg_ref, dbv_ref):
        wg, wv, bg, bv = wg_ref[...], wv_ref[...], bg_ref[...], bv_ref[...]

        def chunk(win_g, win_v, da_e):
            g2, g1, g0, gp = _conv_rows(win_g, wg, bg, ext)
            v2, v1, v0, vp = _conv_rows(win_v, wv, bv, ext)
            sig = jax.nn.sigmoid(gp)
            dvp = da_e * (gp * sig)
            dgp = da_e * vp * (sig * (1.0 + gp * (1.0 - sig)))

            def back(dp, w):
                return (w[2:3, :] * dp[0:RC] + w[1:2, :] * pltpu.roll(dp, ext - 1, 0)[0:RC]
                        + w[0:1, :] * pltpu.roll(dp, ext - 2, 0)[0:RC]).astype(BF)

            def sums(dp, u2, u1, u0):
                d = dp[0:RC]
                return [_rows8(d), _rows8(d * u2[0:RC]), _rows8(d * u1[0:RC]), _rows8(d * u0[0:RC])]

            return back(dgp, wg), back(dvp, wv), sums(dgp, g2, g1, g0) + sums(dvp, v2, v1, v0)

        zero = jnp.zeros((8, FT), F32)
        dug, duv, acc = chunk(jnp.concatenate([zero, ug_ref[0:ext, :]], axis=0),
                              jnp.concatenate([zero, uv_ref[0:ext, :]], axis=0), da_ref[0:ext, :])
        dug_ref[0:RC, :] = dug
        duv_ref[0:RC, :] = duv

        def step(i, acc):
            r0 = pl.multiple_of(i * RC, RC)
            win = pl.ds(r0 - 8, ext + 8)
            dug, duv, part = chunk(ug_ref[win, :], uv_ref[win, :], da_ref[pl.ds(r0, ext), :])
            dug_ref[pl.ds(r0, RC), :] = dug
            duv_ref[pl.ds(r0, RC), :] = duv
            return [a + p for a, p in zip(acc, part)]

        acc = lax.fori_loop(1, NCH - 1, step, acc)
        r0 = T - RC
        tail = lambda ref, lo: jnp.concatenate([ref[lo:T, :], zero], axis=0)
        dug, duv, part = chunk(tail(ug_ref, r0 - 8), tail(uv_ref, r0 - 8), tail(da_ref, r0))
        dug_ref[r0:T, :] = dug
        duv_ref[r0:T, :] = duv
        tot = [jnp.sum(a + p, axis=0, keepdims=True) for a, p in zip(acc, part)]
        for k, (dw_ref, db_ref) in enumerate(((dwg_ref, dbg_ref), (dwv_ref, dbv_ref))):
            db_ref[...] = tot[4 * k]
            dw_ref[...] = jnp.zeros_like(dw_ref)
            for r in range(3):
                dw_ref[r:r + 1, :] = tot[4 * k + 1 + r]

    col = lambda r: pl.BlockSpec((r, FT), lambda j: (0, j))
    return _call(
        body, (up, up, fw8, fw8, fb, fb, da), name="ffn_act_bwd", grid=(NFT,),
        in_specs=_ffn_act_specs() + [pl.BlockSpec((T, FT), lambda j: (0, j))],
        out_specs=[col(T), col(T), col(8), col(8), col(1), col(1)],
        out_shape=[SDS((T, DFF), BF), SDS((T, DFF), BF), SDS((8, DFF), F32), SDS((8, DFF), F32),
                   SDS((1, DFF), F32), SDS((1, DFF), F32)],
        sem=("parallel",), vmem_mib=40, comm=comm)


def _ffn_down_bwd(dh2b, w_down, comm=()):
    tm = 256

    def body(d_ref, w_ref, o_ref):
        o_ref[...] = _dot(d_ref[...], w_ref[...], 1, 1)

    return _call(
        body, (dh2b, w_down), name="ffn_down_bwd", grid=(T // tm,),
        in_specs=[pl.BlockSpec((tm, D), lambda i: (i, 0)), pl.BlockSpec((DFF, D), lambda i: (0, 0))],
        out_specs=[pl.BlockSpec((tm, DFF), lambda i: (i, 0))], out_shape=[SDS((T, DFF), F32)],
        sem=("parallel",), vmem_mib=40, comm=comm)


def _norm_matmul_bwd(name, a_list, w_t, k_offsets, xin, g, dres, want_bf16, comm=(), slot=None):
    tm = 256
    ks = [a.shape[1] for a in a_list]
    n_a = len(a_list)
    n_pre = 0 if slot is None else 1

    def body(*refs):
        refs = refs[n_pre:]
        a_refs = refs[:n_a]
        w_ref, x_ref, g_ref, r_ref = refs[n_a:n_a + 4]
        outs = refs[n_a + 4:]
        dx_ref, dg_ref = outs[0], (outs[-1] if slot is None else outs[-1].at[0])

        @pl.when(pl.program_id(0) == 0)
        def _():
            dg_ref[...] = jnp.zeros_like(dg_ref)

        du = _dot(a_refs[0][...], w_ref[k_offsets[0]:k_offsets[0] + ks[0], :], 1, 0)
        for k in range(1, n_a):
            du = du + _dot(a_refs[k][...], w_ref[k_offsets[k]:k_offsets[k] + ks[k], :], 1, 0)
        x = x_ref[...]
        r = lax.rsqrt(jnp.mean(x * x, axis=-1, keepdims=True) + EPS)
        dx, dg = _rms_bwd(du, x, r, g_ref[...])
        dx = r_ref[...] + dx
        dx_ref[...] = dx
        if want_bf16:
            outs[1][...] = dx.astype(BF)
        dg_ref[...] += dg

    tile = lambda c: pl.BlockSpec((tm, c), lambda i, *_: (i, 0))
    if slot is None:
        dg_spec, dg_shape = pl.BlockSpec((1, D), lambda i: (0, 0)), SDS((1, D), F32)
    else:
        dg_spec, dg_shape = pl.BlockSpec((1, 1, D), lambda i, slot_ref: (slot_ref[0], 0, 0)), SDS((N_DEV, 1, D), F32)
    out_specs = [tile(D)] + ([tile(D)] if want_bf16 else []) + [dg_spec]
    out_shape = [SDS((T, D), F32)] + ([SDS((T, D), BF)] if want_bf16 else []) + [dg_shape]
    return _call(
        body, (*a_list, w_t, xin, g, dres), name=name, grid=(T // tm,), prefetch=() if slot is None else (slot,),
        in_specs=[tile(k) for k in ks] + [pl.BlockSpec(w_t.shape, lambda i, *_: (0, 0)), tile(D),
                                           pl.BlockSpec((1, D), lambda i, *_: (0, 0)), tile(D)],
        out_specs=out_specs, out_shape=out_shape, sem=("arbitrary",), vmem_mib=56, comm=comm)


def _out_bwd(dh1b, w_out):
    tm = 256

    def body(d_ref, w_ref, o_ref):
        o_ref[...] = _dot(d_ref[...], w_ref[...], 1, 1)

    return _call(
        body, (dh1b, w_out), name="out_bwd", grid=(T // tm,),
        in_specs=[pl.BlockSpec((tm, D), lambda i: (i, 0)), pl.BlockSpec((D, D), lambda i: (0, 0))],
        out_specs=[pl.BlockSpec((tm, D), lambda i: (i, 0))], out_shape=[SDS((T, D), F32)],
        sem=("parallel",), vmem_mib=32)


def _wgrad(name, a_list, b, comm=()):
    tm = 256
    steps = [a.shape[1] // tm for a in a_list]
    starts = [sum(steps[:k]) for k in range(len(a_list))]
    n_a = len(a_list)

    def body(*refs):
        a_refs, b_ref, o32_ref, obf_ref = refs[:n_a], refs[n_a], refs[n_a + 1], refs[n_a + 2]
        i = pl.program_id(0)
        for k in range(n_a):
            @pl.when((i >= starts[k]) & (i < starts[k] + steps[k]))
            def _(k=k):
                r = _dot(a_refs[k][...], b_ref[...], 0, 0)
                o32_ref[...] = r
                obf_ref[...] = r.astype(BF)

    def a_spec(k):
        return pl.BlockSpec((T, tm), lambda i: (0, jnp.clip(i - starts[k], 0, steps[k] - 1)))

    m_total = tm * sum(steps)
    return _call(
        body, (*a_list, b), name=name, grid=(sum(steps),),
        in_specs=[a_spec(k) for k in range(n_a)] + [pl.BlockSpec((T, D), lambda i: (0, 0))],
        out_specs=[pl.BlockSpec((tm, D), lambda i: (i, 0)), pl.BlockSpec((tm, D), lambda i: (i, 0))],
        out_shape=[SDS((m_total, D), F32), SDS((m_total, D), BF)], sem=("parallel",), vmem_mib=40, comm=comm)


def _chip_sum(name, g32, from_sib, core, chip):
    h = g32.shape[1]
    th = h // 2

    def body(core_ref, chip_ref, g_ref, s_ref, pbf_ref, own_ref):
        p = g_ref[0] + s_ref[0].astype(F32)
        pbf_ref[0] = p.astype(BF)

        @pl.when(pl.program_id(1) == chip_ref[0])
        def _():
            own_ref[...] = p

    grid_spec = pltpu.PrefetchScalarGridSpec(
        num_scalar_prefetch=2, grid=(h // th, N_CHIPS),
        in_specs=[pl.BlockSpec((1, th, D), lambda t, jj, core_ref, chip_ref: (2 * jj + core_ref[0], t, 0)),
                  pl.BlockSpec((1, th, D), lambda t, jj, core_ref, chip_ref: (jj, t, 0))],
        out_specs=[pl.BlockSpec((1, th, D), lambda t, jj, core_ref, chip_ref: (jj, t, 0)),
                   pl.BlockSpec((th, D), lambda t, jj, core_ref, chip_ref: (t, 0))],
    )
    return _pcall(
        body, name=name, grid_spec=grid_spec, out_shape=_in_hbm([SDS((N_CHIPS, h, D), BF), SDS((h, D), F32)]),
        compiler_params=_params(("arbitrary", "arbitrary"), 32),
    )(core, chip, *_from_hbm(g32, from_sib))


def _final_sum(name, own, from_chips, core):
    h = own.shape[0]

    def body(core_ref, o_ref, r_ref, f_ref):
        f_ref[0] = ((o_ref[...] + r_ref[0].astype(F32)) + r_ref[1].astype(F32)) + r_ref[2].astype(F32)

    grid_spec = pltpu.PrefetchScalarGridSpec(
        num_scalar_prefetch=1, grid=(1,),
        in_specs=[pl.BlockSpec((h, D), lambda i, core_ref: (0, 0)), pl.BlockSpec((3, h, D), lambda i, core_ref: (0, 0, 0))],
        out_specs=pl.BlockSpec((1, h, D), lambda i, core_ref: (core_ref[0], 0, 0)),
    )
    return _pcall(body, name=name, grid_spec=grid_spec, out_shape=pltpu.HBM((2, h, D), F32),
                  compiler_params=_params(("arbitrary",), 40))(core, *_from_hbm(own, from_chips))


def _adam_math(w, g, m, v):
    nm = ADAM_B1 * m + (1.0 - ADAM_B1) * g
    nv = ADAM_B2 * v + (1.0 - ADAM_B2) * (g * g)
    m_hat = nm / (1.0 - ADAM_B1 ** ADAM_STEP)
    v_hat = nv / (1.0 - ADAM_B2 ** ADAM_STEP)
    return -ADAM_LR * (m_hat / (jnp.sqrt(v_hat) + ADAM_EPS) + ADAM_WD * w), nm, nv


def _adamw(name, w, g, m, v, tr, copy_g=False):
    rows, cols = w.shape

    def body(w_ref, g_ref, m_ref, v_ref, *outs):
        g_val = g_ref[...]
        if copy_g:
            outs[0][...] = g_val
        d_ref, nm_ref, nv_ref = outs[-3:]
        d_ref[...], nm_ref[...], nv_ref[...] = _adam_math(w_ref[...], g_val, m_ref[...], v_ref[...])

    spec = pl.BlockSpec((tr, cols), lambda i: (i, 0))
    n_out = 4 if copy_g else 3
    return _call(body, (w, g, m, v), name=name, grid=(rows // tr,), in_specs=[spec] * 4, out_specs=[spec] * n_out,
                 out_shape=[SDS((rows, cols), F32)] * n_out, sem=("parallel",), vmem_mib=32, free=(0, 2, 3))


C_G1, C_G2, C_GCO, C_GAO, C_DCW, C_DQG, C_DKG, C_SINK, C_SQ = 0, 1024, 2048, 2560, 3072, 4608, 4736, 4864, 5632
P_W = C_SQ + 128


def _pack_small(me, dfwg, dfwv, dfbg, dfbv, dg2, dgco, dgao, dcw8, dqg, dkg, dsink, sq):
    def body(me_ref, dfwg_r, dfwv_r, dfbg_r, dfbv_r, dg2_r, dgco_r, dgao_r, dcw_r, dqg_r, dkg_r, dsink_r, sq_r, o):
        o[...] = jnp.zeros_like(o)
        o[0, :, 0:DFF] = dfwg_r[...]
        o[0, :, DFF:2 * DFF] = dfwv_r[...]
        o[0, 3:4, 0:DFF] = dfbg_r[...]
        o[0, 3:4, DFF:2 * DFF] = dfbv_r[...]
        o[0, 4:5, C_G2:C_G2 + D] = dg2_r[...]
        o[0, 4:5, C_GCO:C_GCO + CW] = dgco_r[...]
        o[0, 4:5, C_GAO:C_GAO + AW] = dgao_r[...]
        for r in range(3):
            o[0, 4:5, C_DCW + r * CW:C_DCW + (r + 1) * CW] = dcw_r[r:r + 1, :]
        o[0, 4:5, C_DQG:C_DQG + HD] = dqg_r[...]
        o[0, 4:5, C_DKG:C_DKG + HD] = dkg_r[...]
        o[0, 4:5, C_SINK:C_SINK + 128] = dsink_r[...]
        o[0, :, C_SQ:C_SQ + 128] = sq_r[...]

    ins = (dfwg, dfwv, dfbg, dfbv, dg2, dgco, dgao, dcw8, dqg, dkg, dsink, sq)
    return _call(body, ins, name="pack_small", grid=(1,), prefetch=(me,),
                 in_specs=[pl.BlockSpec(a.shape, lambda i, me_ref: (0, 0)) for a in ins],
                 out_specs=[pl.BlockSpec((1, 8, P_W), lambda i, me_ref: (me_ref[0], 0, 0))],
                 out_shape=[SDS((N_DEV, 8, P_W), F32)], sem=("arbitrary",))[0]


N_SMALL = 11


def _small_adam(chip, p_all, g1_all, tbl_all, ws, ms, vs):
    fw_cols = 2 * DFF // N_CHIPS
    cw_cols = CW // N_CHIPS

    def body(chip_ref, p_ref, fw_ref, cw0_ref, cw1_ref, cw2_ref, g1_ref, tbl_ref, *refs):
        w_r, m_r, v_r = refs[0:N_SMALL], refs[N_SMALL:2 * N_SMALL], refs[2 * N_SMALL:3 * N_SMALL]
        outs = refs[3 * N_SMALL:]
        g_o, d_o, nm_o, nv_o = (outs[k * N_SMALL:(k + 1) * N_SMALL] for k in range(4))
        loss_o = outs[4 * N_SMALL]

        def total(ref):
            s = ref[0]
            for k in range(1, N_DEV):
                s = s + ref[k]
            return s

        S = total(p_ref)
        fw = total(fw_ref)
        cws = [total(r) for r in (cw0_ref, cw1_ref, cw2_ref)]

        def step(i, g, at):
            d, nm, nv = _adam_math(w_r[i][at], g, m_r[i][at], v_r[i][at])
            g_o[i][at], d_o[i][at], nm_o[i][at], nv_o[i][at] = g, d, nm, nv

        everything = (slice(None), slice(None))
        step(0, total(g1_ref), everything)
        for r in range(3):
            step(1, cws[r][4:5, :], (0, slice(r, r + 1), slice(None)))
        step(2, S[4:5, C_DQG:C_DQG + HD], everything)
        step(3, S[4:5, C_DKG:C_DKG + HD], everything)
        step(4, total(tbl_ref), everything)
        step(5, S[4:5, C_SINK:C_SINK + NH], everything)
        step(6, S[4:5, C_GCO:C_GCO + CW], everything)
        step(7, S[4:5, C_GAO:C_GAO + AW], everything)
        step(8, S[4:5, C_G2:C_G2 + D], everything)
        step(9, fw[0:3, :], (0, slice(None), slice(None)))
        step(10, S[3:4, 0:2 * DFF], everything)
        sq = S[:, C_SQ:C_SQ + 128]
        loss_o[...] = jnp.sum(jnp.sum(sq, axis=1, keepdims=True), axis=0, keepdims=True) * (0.5 / D)

    def full(a):
        n = len(a.shape)
        return pl.BlockSpec(a.shape, lambda i, chip_ref: (0,) * n)

    params = [*ws, *ms, *vs]
    out = _call(
        body, (p_all, p_all, p_all, p_all, p_all, g1_all, tbl_all, *params), name="small_adam", grid=(1,), prefetch=(chip,),
        in_specs=[full(p_all),
                  pl.BlockSpec((N_DEV, 8, fw_cols), lambda i, chip_ref: (0, 0, chip_ref[0])),
                  *[pl.BlockSpec((N_DEV, 8, cw_cols), lambda i, chip_ref, r=r: (0, 0, (C_DCW + r * CW) // cw_cols + chip_ref[0]))
                    for r in range(3)],
                  full(g1_all), full(tbl_all), *[full(a) for a in params]],
        out_specs=[full(a) for a in ws] * 4 + [pl.BlockSpec((1, 1), lambda i, chip_ref: (0, 0))],
        out_shape=[SDS(a.shape, F32) for a in ws] * 4 + [SDS((1, 1), F32)], sem=("arbitrary",), vmem_mib=32)
    return out[0:N_SMALL], out[N_SMALL:2 * N_SMALL], out[2 * N_SMALL:3 * N_SMALL], out[3 * N_SMALL:4 * N_SMALL], out[4 * N_SMALL]


PLACE_STEPS = 4


def _place_specs(shards):
    rows = [s.shape[0] // PLACE_STEPS for s in shards]
    return ([pl.BlockSpec((r, D), lambda i, chip_ref: (i, 0)) for r in rows],
            [pl.BlockSpec((r, D), lambda i, chip_ref: (chip_ref[0] * PLACE_STEPS + i, 0)) for r in rows],
            [SDS((N_CHIPS * s.shape[0], D), BF) for s in shards])


def _place_first(chip, shard, conv_w, ffn_conv_w):
    def body(chip_ref, a, s0, s1, o, t0, t1):
        o[...] = a[...].astype(BF)

        @pl.when(pl.program_id(0) == 0)
        def _():
            for s, t in ((s0, t0), (s1, t1)):
                t[...] = jnp.zeros_like(t)
                t[0, 0:3, :] = s[...]

    ins, outs, shapes = _place_specs([shard])
    taps = (conv_w, ffn_conv_w)
    return _call(
        body, (shard, conv_w, ffn_conv_w), name="place_first", grid=(PLACE_STEPS,), prefetch=(chip,),
        in_specs=ins + [pl.BlockSpec(s.shape, lambda i, chip_ref: (0, 0)) for s in taps],
        out_specs=outs + [pl.BlockSpec((1, 8, s.shape[1]), lambda i, chip_ref: (chip_ref[0], 0, 0)) for s in taps],
        out_shape=shapes + [SDS((N_CHIPS, 8, s.shape[1]), F32) for s in taps],
        sem=("arbitrary",), vmem_mib=32, free=(0, 1, 2))


def _place_rest(chip, shards, table, bucket, comm):
    n = len(shards)

    def body(chip_ref, *refs):
        a, (tab_ref, bk_ref), o, bias_ref = refs[:n], refs[n:n + 2], refs[n + 2:2 * n + 2], refs[2 * n + 2]
        for src, dst in zip(a, o):
            dst[...] = src[...].astype(BF)

        @pl.when(pl.program_id(0) == 0)
        def _():
            bk = bk_ref[...]
            eq = [bk == b for b in range(NBUCKET)]
            for h in range(NH):
                acc = jnp.zeros((BLK, 2 * BLK), F32)
                for b in range(NBUCKET):
                    acc = jnp.where(eq[b], tab_ref[b, h], acc)
                bias_ref[h * BLK:(h + 1) * BLK, :] = acc

    ins, outs, shapes = _place_specs(shards)
    return _call(
        body, (*shards, table, bucket), name="place_rest", grid=(PLACE_STEPS,), prefetch=(chip,),
        in_specs=ins + [pl.BlockSpec(memory_space=pltpu.SMEM), pl.BlockSpec(bucket.shape, lambda i, chip_ref: (0, 0))],
        out_specs=outs + [pl.BlockSpec((NH * BLK, 2 * BLK), lambda i, chip_ref: (0, 0))],
        out_shape=shapes + [SDS((NH * BLK, 2 * BLK), F32)],
        sem=("arbitrary",), vmem_mib=32, comm=comm, free=tuple(range(n + 2)))


def kernel(x, norm_mix_g, w_in, conv_w, q_norm_g, k_norm_g, rel_bias_table, sinks, out_norm_conv_g, out_norm_attn_g, w_out, norm_ffn_g, w_up, ffn_conv_w, ffn_conv_b, w_down, loss_target, m_norm_mix_g, m_w_in, m_conv_w, m_q_norm_g, m_k_norm_g, m_rel_bias_table, m_sinks, m_out_norm_conv_g, m_out_norm_attn_g, m_w_out, m_norm_ffn_g, m_w_up, m_ffn_conv_w, m_ffn_conv_b, m_w_down, v_norm_mix_g, v_w_in, v_conv_w, v_q_norm_g, v_k_norm_g, v_rel_bias_table, v_sinks, v_out_norm_conv_g, v_out_norm_attn_g, v_w_out, v_norm_ffn_g, v_w_up, v_ffn_conv_w, v_ffn_conv_b, v_w_down):
    as_arg = lambda i: jnp.reshape(i, (1,)).astype(jnp.int32)
    chip = as_arg(2 * lax.axis_index("x") + lax.axis_index("y"))
    core = as_arg(lax.axis_index("c"))
    me = 2 * chip + core
    xs, tgt = x[0], loss_target[0]
    qg, kg, gco, gao, g1, g2, fb = q_norm_g, k_norm_g, out_norm_conv_g, out_norm_attn_g, norm_mix_g, norm_ffn_g, ffn_conv_b
    pieces = lambda g: g.reshape(N_DEV, g.shape[0] // N_DEV, D)
    whole = lambda f: f.reshape(2 * f.shape[1], D)

    bucket = jnp.asarray(_bucket_table())
    p_in, p_cw, p_fw = _place_first(chip, w_in[0].T, conv_w[0], ffn_conv_w[0])
    p_out, p_up, p_down, bias, w_int, cw_all, fw_all = _place_rest(
        chip, [w_out[0], w_up[0].T, w_down[0]], rel_bias_table, bucket,
        comm=[_t_gather(p_in), _t_small_weights(p_cw), _t_small_weights(p_fw)])
    cw8 = jnp.transpose(cw_all, (1, 0, 2)).reshape(8, CW)
    fw8 = jnp.transpose(fw_all, (1, 0, 2)).reshape(8, 2 * DFF)

    proj, u1, w_out_f = _inproj(xs, g1, w_int, comm=[_t_gather(p_out)])
    y, w_upt = _mix_fwd(proj, sinks, cw8, qg, kg, gco, gao, bias, comm=[_t_gather(p_up)])
    h1, u2 = _outproj(y, w_out_f, xs, g2)
    up, w_down_f = _ffn_up(u2, w_upt, comm=[_t_gather(p_down)])
    a, = _ffn_act(up, fw8, fb)
    dh2, dh2b, sq = _ffn_down(a, w_down_f, h1, tgt)

    gd32, gdbf = _wgrad("wgrad_down", [a], dh2b)
    da, sib_down = _ffn_down_bwd(dh2b, w_down_f, comm=[_t_sibling(pieces(gdbf))])
    pbf_down, own_down = _chip_sum("chip_sum_w_down", pieces(gd32), sib_down, core, chip)
    dug, duv, dfwg, dfwv, dfbg, dfbv, chips_down = _ffn_act_bwd(up, da, fw8, fb, comm=[_t_chips(pbf_down)])
    fin_down = _final_sum("final_sum_w_down", own_down, chips_down, core)
    gu32, gubf = _wgrad("wgrad_up", [dug, duv], u2)
    dh1, dh1b, dg2, sib_up, fin_down = _norm_matmul_bwd(
        "ffn_up_bwd", [dug, duv], w_upt, [0, DFF], h1, g2, dh2, True, comm=[_t_sibling(pieces(gubf)), _t_swap(fin_down)])
    pbf_up, own_up = _chip_sum("chip_sum_w_up", pieces(gu32), sib_up, core, chip)
    go32, gobf = _wgrad("wgrad_out", [y], dh1b)
    dy, = _out_bwd(dh1b, w_out_f)
    dproj, dcw8, dqg, dkg, dgco, dgao, dsink, dbias, chips_up, sib_out = _mix_bwd(
        proj, dy, sinks, cw8, qg, kg, gco, gao, bias, comm=[_t_chips(pbf_up), _t_sibling(pieces(gobf))])
    fin_up = _final_sum("final_sum_w_up", own_up, chips_up, core)
    pbf_out, own_out = _chip_sum("chip_sum_w_out", pieces(go32), sib_out, core, chip)
    tbl_all = _band_bias_bwd(dbias, bucket, me)
    gi32, gibf, chips_out, fin_up = _wgrad("wgrad_in", [dproj], u1, comm=[_t_chips(pbf_out), _t_swap(fin_up)])
    fin_out = _final_sum("final_sum_w_out", own_out, chips_out, core)
    sib_in, tbl_all, fin_out = _comm_call(
        "to_sibling_last", [_t_sibling(pieces(gibf)), _t_allgather(tbl_all), _t_swap(fin_out)])
    pbf_in, own_in = _chip_sum("chip_sum_w_in", pieces(gi32), sib_in, core, chip)
    p_all = _pack_small(me, dfwg, dfwv, dfbg, dfbv, dg2, dgco, dgao, dcw8, dqg, dkg, dsink, sq)
    dx, g1_all, chips_in, p_all = _norm_matmul_bwd(
        "in_bwd", [dproj], w_int, [0], xs, g1, dh1, False, comm=[_t_chips(pbf_in), _t_allgather(p_all)], slot=me)
    fin_in = _final_sum("final_sum_w_in", own_in, chips_in, core)
    g1_all, fin_in = _comm_call("gather_last", [_t_allgather(g1_all), _t_swap(fin_in)])

    g_w_in, g_w_out, g_w_up, g_w_down = whole(fin_in).T, whole(fin_out), whole(fin_up).T, whole(fin_down)
    g_w_down, d_down, nm_down, nv_down = _adamw("adamw_w_down", w_down[0], g_w_down, m_w_down[0], v_w_down[0], 352, True)
    d_up, nm_up, nv_up = _adamw("adamw_w_up", w_up[0], g_w_up, m_w_up[0], v_w_up[0], 256)
    g_w_out, d_out, nm_out, nv_out = _adamw("adamw_w_out", w_out[0], g_w_out, m_w_out[0], v_w_out[0], 256, True)
    d_in, nm_in, nv_in = _adamw("adamw_w_in", w_in[0], g_w_in, m_w_in[0], v_w_in[0], 256)
    sw = [norm_mix_g, conv_w, q_norm_g, k_norm_g, rel_bias_table, sinks, out_norm_conv_g, out_norm_attn_g,
          norm_ffn_g, ffn_conv_w, ffn_conv_b]
    smm = [m_norm_mix_g, m_conv_w, m_q_norm_g, m_k_norm_g, m_rel_bias_table, m_sinks, m_out_norm_conv_g,
           m_out_norm_attn_g, m_norm_ffn_g, m_ffn_conv_w, m_ffn_conv_b]
    smv = [v_norm_mix_g, v_conv_w, v_q_norm_g, v_k_norm_g, v_rel_bias_table, v_sinks, v_out_norm_conv_g,
           v_out_norm_attn_g, v_norm_ffn_g, v_ffn_conv_w, v_ffn_conv_b]
    sg, sd, snm, snv, loss = _small_adam(chip, p_all, g1_all, tbl_all, sw, smm, smv)

    def order(s, b_in, b_out, b_up, b_down):
        return (s[0], b_in[None], s[1], s[2], s[3], s[4], s[5], s[6], s[7], b_out[None], s[8], b_up[None],
                s[9], s[10], b_down[None])

    return (loss.reshape(()), dx[None],
            *order(sg, g_w_in, g_w_out, g_w_up, g_w_down),
            *order(sd, d_in, d_out, d_up, d_down),
            *order(snm, nm_in, nm_out, nm_up, nm_down),
            *order(snv, nv_in, nv_out, nv_up, nv_down))
```

```python
import functools
import math

import numpy as np

import jax
import jax.numpy as jnp
from jax import lax
from jax.experimental import pallas as pl
from jax.experimental.pallas import tpu as pltpu

F32 = jnp.float32
BF = jnp.bfloat16
SDS = jax.ShapeDtypeStruct

T = 2048
D = 1024
CW = 512
AW = 512
HD = 64
NH = 8
NKV = 2
GQ = 4
INW = 2304
DFF = 2816
BLK = 128
NB = T // BLK
NBUCKET = 32
EPS = 1e-6
NEG_INF = -1e30
N_CHIPS = 4
N_DEV = 8

ADAM_LR = 0.001
ADAM_B1 = 0.9
ADAM_B2 = 0.999
ADAM_EPS = 1e-08
ADAM_WD = 0.01
ADAM_STEP = 10

MIB = 1024 * 1024
MESH = pl.DeviceIdType.MESH
ANY = pl.BlockSpec(memory_space=pl.ANY)

_pcall = pl.pallas_call


def _params(sem=None, vmem_mib=None):
    kw = {}
    if sem is not None:
        kw["dimension_semantics"] = sem
    if vmem_mib is not None:
        kw["vmem_limit_bytes"] = vmem_mib * MIB
    return pltpu.CompilerParams(**kw)


def _dot(a, b, ca, cb):
    return lax.dot_general(a, b, (((ca,), (cb,)), ((), ())), preferred_element_type=F32)


def _rms_bwd(dy, x, r, g):
    dg = jnp.sum(dy * (x * r), axis=0, keepdims=True)
    dgx = dy * g
    dx = r * dgx - x * (r * r * r) * jnp.mean(x * dgx, axis=-1, keepdims=True)
    return dx, dg


def _where():
    x, y, c = lax.axis_index("x"), lax.axis_index("y"), lax.axis_index("c")
    return x, y, c, [(1 - x, y), (x, 1 - y), (1 - x, 1 - y)]


def _rcopy(src, dst, ssem, rsem, dev):
    return pltpu.make_async_remote_copy(src_ref=src, dst_ref=dst, send_sem=ssem, recv_sem=rsem, device_id=dev,
                                        device_id_type=MESH)


class _Task:
    def __init__(self, ins, outs, alias, n_sem, start, finish):
        self.ins, self.outs, self.alias, self.n_sem, self.start, self.finish = ins, outs, alias, n_sem, start, finish


def _t_gather(placed):
    R = placed.shape[0] // N_CHIPS

    def rows(chip_index, core):
        return pl.ds(pl.multiple_of(chip_index * R + core * (R // 2), 16), R // 2)

    def start(cin, cout, ss, rs, b):
        x, y, c, chips = _where()
        mine = cout[0].at[rows(2 * x + y, c)]
        for r, (px, py) in enumerate(chips):
            _rcopy(mine, mine, ss.at[b + r], rs.at[b + r], (px, py, c)).start()

    def finish(cin, cout, ss, rs, b):
        x, y, c, chips = _where()
        buf = cout[0]
        sib = (x, y, 1 - c)
        for r, (px, py) in enumerate(chips):
            got = buf.at[rows(2 * px + py, c)]
            _rcopy(got, got, ss.at[b + r], rs.at[b + r], (px, py, c)).wait_recv()
            _rcopy(got, got, ss.at[b + 3 + r], rs.at[b + 3 + r], sib).start()
        for r, (px, py) in enumerate(chips):
            got = buf.at[rows(2 * px + py, 1 - c)]
            _rcopy(got, got, ss.at[b + 3 + r], rs.at[b + 3 + r], sib).wait_recv()
        mine = buf.at[rows(2 * x + y, c)]
        for r in range(6):
            _rcopy(mine, mine, ss.at[b + r], rs.at[b + r], sib).wait_send()

    return _Task([placed], [SDS(placed.shape, placed.dtype)], [(0, 0)], 6, start, finish)


def _t_small_weights(buf):
    def start(cin, cout, ss, rs, b):
        x, y, c, chips = _where()
        mine = cout[0].at[2 * x + y]
        for r, (px, py) in enumerate(chips):
            _rcopy(mine, mine, ss.at[b + r], rs.at[b + r], (px, py, c)).start()

    def finish(cin, cout, ss, rs, b):
        x, y, c, chips = _where()
        for r, (px, py) in enumerate(chips):
            got = cout[0].at[2 * px + py]
            _rcopy(got, got, ss.at[b + r], rs.at[b + r], (px, py, c)).wait_recv()
        for r, (px, py) in enumerate(chips):
            mine = cout[0].at[2 * x + y]
            _rcopy(mine, mine, ss.at[b + r], rs.at[b + r], (px, py, c)).wait_send()

    return _Task([buf], [SDS(buf.shape, buf.dtype)], [(0, 0)], 3, start, finish)


def _t_sibling(gbf):
    def start(cin, cout, ss, rs, b):
        x, y, c, _ = _where()
        for jj in range(N_CHIPS):
            _rcopy(cin[0].at[2 * jj + (1 - c)], cout[0].at[jj], ss.at[b + jj], rs.at[b + jj], (x, y, 1 - c)).start()

    def finish(cin, cout, ss, rs, b):
        x, y, c, _ = _where()
        for jj in range(N_CHIPS):
            got = cout[0].at[jj]
            _rcopy(got, got, ss.at[b + jj], rs.at[b + jj], (x, y, 1 - c)).wait_recv()
        for jj in range(N_CHIPS):
            got = cout[0].at[jj]
            _rcopy(got, got, ss.at[b + jj], rs.at[b + jj], (x, y, 1 - c)).wait_send()

    return _Task([gbf], [SDS((N_CHIPS,) + gbf.shape[1:], BF)], [], N_CHIPS, start, finish)


def _t_chips(pbf):
    def start(cin, cout, ss, rs, b):
        x, y, c, chips = _where()
        for r, (px, py) in enumerate(chips):
            _rcopy(cin[0].at[2 * px + py], cout[0].at[r], ss.at[b + r], rs.at[b + r], (px, py, c)).start()

    def finish(cin, cout, ss, rs, b):
        x, y, c, chips = _where()
        for r, (px, py) in enumerate(chips):
            got = cout[0].at[r]
            _rcopy(got, got, ss.at[b + r], rs.at[b + r], (px, py, c)).wait_recv()
        for r, (px, py) in enumerate(chips):
            got = cout[0].at[r]
            _rcopy(got, got, ss.at[b + r], rs.at[b + r], (px, py, c)).wait_send()

    return _Task([pbf], [SDS((3,) + pbf.shape[1:], BF)], [], 3, start, finish)


def _t_swap(fin):
    def start(cin, cout, ss, rs, b):
        x, y, c, _ = _where()
        mine = cout[0].at[c]
        _rcopy(mine, mine, ss.at[b], rs.at[b], (x, y, 1 - c)).start()

    def finish(cin, cout, ss, rs, b):
        x, y, c, _ = _where()
        got = cout[0].at[1 - c]
        _rcopy(got, got, ss.at[b], rs.at[b], (x, y, 1 - c)).wait_recv()
        _rcopy(got, got, ss.at[b], rs.at[b], (x, y, 1 - c)).wait_send()

    return _Task([fin], [SDS(fin.shape, fin.dtype)], [(0, 0)], 1, start, finish)


def _t_allgather(buf):
    def peers():
        x, y, c, _ = _where()
        out = []
        for rel in range(1, N_DEV):
            px, py, pc = x ^ ((rel >> 2) & 1), y ^ ((rel >> 1) & 1), c ^ (rel & 1)
            out.append((rel - 1, 4 * px + 2 * py + pc, (px, py, pc)))
        return 4 * x + 2 * y + c, out

    def start(cin, cout, ss, rs, b):
        me, ps = peers()
        mine = cout[0].at[me]
        for k, _, dev in ps:
            _rcopy(mine, mine, ss.at[b + k], rs.at[b + k], dev).start()

    def finish(cin, cout, ss, rs, b):
        me, ps = peers()
        for k, pidx, dev in ps:
            got = cout[0].at[pidx]
            _rcopy(got, got, ss.at[b + k], rs.at[b + k], dev).wait_recv()
        for k, _, dev in ps:
            mine = cout[0].at[me]
            _rcopy(mine, mine, ss.at[b + k], rs.at[b + k], dev).wait_send()

    return _Task([buf], [SDS(buf.shape, buf.dtype)], [(0, 0)], N_DEV - 1, start, finish)


def _run_tasks(comm, which, cin, cout, ss, rs):
    i0 = o0 = s0 = 0
    for t in comm:
        getattr(t, which)(cin[i0:i0 + len(t.ins)], cout[o0:o0 + len(t.outs)], ss, rs, s0)
        i0, o0, s0 = i0 + len(t.ins), o0 + len(t.outs), s0 + t.n_sem


def _from_hbm(*arrays):
    return [pltpu.with_memory_space_constraint(a, pltpu.HBM) for a in arrays]


def _in_hbm(shapes):
    return [pltpu.HBM(s.shape, s.dtype) for s in shapes]


def _comm_layout(comm, n_in, n_out):
    c_in = [a for t in comm for a in t.ins]
    c_out = [s for t in comm for s in t.outs]
    aliases, i0, o0 = {}, 0, 0
    for t in comm:
        for i, o in t.alias:
            aliases[n_in + i0 + i] = n_out + o0 + o
        i0, o0 = i0 + len(t.ins), o0 + len(t.outs)
    return c_in, c_out, aliases, sum(t.n_sem for t in comm)


def _call(body, operands, *, name, grid, in_specs, out_specs, out_shape, scratch_shapes=(), sem=None, vmem_mib=None, comm=(),
          free=(), prefetch=()):
    operands = [o if s.memory_space == pltpu.SMEM or k in free else pltpu.with_memory_space_constraint(o, pltpu.HBM)
                for k, (o, s) in enumerate(zip(operands, in_specs))]
    n_pre, n_in, n_out, n_scr = len(prefetch), len(in_specs), len(out_specs), len(scratch_shapes)
    c_in, c_out, aliases, n_sem = _comm_layout(comm, n_pre + n_in, n_out)
    sems = [pltpu.SemaphoreType.DMA((n_sem,)), pltpu.SemaphoreType.DMA((n_sem,))] if comm else []

    def wrapped(*refs):
        pre, refs = refs[:n_pre], refs[n_pre:]
        ins, cin = refs[:n_in], refs[n_in:n_in + len(c_in)]
        rest = refs[n_in + len(c_in):]
        outs, cout = rest[:n_out], rest[n_out:n_out + len(c_out)]
        rest = rest[n_out + len(c_out):]
        scr, csem = rest[:n_scr], rest[n_scr:]
        if not comm:
            return body(*pre, *ins, *outs, *scr)
        ids = [pl.program_id(k) for k in range(len(grid))]
        first = functools.reduce(jnp.logical_and, [i == 0 for i in ids])
        last = functools.reduce(jnp.logical_and, [i == n - 1 for i, n in zip(ids, grid)])
        pl.when(first)(lambda: _run_tasks(comm, "start", cin, cout, *csem))
        body(*pre, *ins, *outs, *scr)
        pl.when(last)(lambda: _run_tasks(comm, "finish", cin, cout, *csem))

    grid_spec = pltpu.PrefetchScalarGridSpec(
        num_scalar_prefetch=n_pre, grid=grid, in_specs=list(in_specs) + [ANY] * len(c_in),
        out_specs=list(out_specs) + [ANY] * len(c_out), scratch_shapes=list(scratch_shapes) + sems)
    return _pcall(
        wrapped, name=name, grid_spec=grid_spec, out_shape=_in_hbm(list(out_shape) + c_out), input_output_aliases=aliases,
        compiler_params=_params(("arbitrary",) * len(grid) if comm else sem, vmem_mib),
    )(*prefetch, *operands, *_from_hbm(*c_in))


def _comm_call(name, comm):
    c_in, c_out, aliases, n_sem = _comm_layout(comm, 0, 0)

    def body(*refs):
        cin, cout, (ss, rs) = refs[:len(c_in)], refs[len(c_in):len(c_in) + len(c_out)], refs[len(c_in) + len(c_out):]
        _run_tasks(comm, "start", cin, cout, ss, rs)
        _run_tasks(comm, "finish", cin, cout, ss, rs)

    return _pcall(
        body, name=name, in_specs=[ANY] * len(c_in), out_specs=[ANY] * len(c_out), out_shape=_in_hbm(c_out),
        scratch_shapes=[pltpu.SemaphoreType.DMA((n_sem,)), pltpu.SemaphoreType.DMA((n_sem,))],
        input_output_aliases=aliases,
    )(*_from_hbm(*c_in))


def _inproj(x, g1, w_int, comm=()):
    tm = 256

    def body(x_ref, g_ref, w_ref, proj_ref, u_ref):
        xf = x_ref[...]
        r = lax.rsqrt(jnp.mean(xf * xf, axis=-1, keepdims=True) + EPS)
        u = (xf * r * g_ref[...]).astype(BF)
        u_ref[...] = u
        proj_ref[...] = _dot(u, w_ref[...], 1, 1)

    return _call(
        body, (x, g1, w_int), name="inproj", grid=(T // tm,),
        in_specs=[pl.BlockSpec((tm, D), lambda i: (i, 0)), pl.BlockSpec((1, D), lambda i: (0, 0)),
                  pl.BlockSpec((INW, D), lambda i: (0, 0))],
        out_specs=[pl.BlockSpec((tm, INW), lambda i: (i, 0)), pl.BlockSpec((tm, D), lambda i: (i, 0))],
        out_shape=[SDS((T, INW), F32), SDS((T, D), BF)], sem=("parallel",), vmem_mib=40, comm=comm)


def _outproj(y, w_out, x, g2):
    tm = 256

    def body(y_ref, w_ref, x_ref, g_ref, h1_ref, u2_ref):
        h1 = x_ref[...] + _dot(y_ref[...], w_ref[...], 1, 0)
        h1_ref[...] = h1
        r = lax.rsqrt(jnp.mean(h1 * h1, axis=-1, keepdims=True) + EPS)
        u2_ref[...] = (h1 * r * g_ref[...]).astype(BF)

    return _call(
        body, (y, w_out, x, g2), name="outproj", grid=(T // tm,),
        in_specs=[pl.BlockSpec((tm, D), lambda i: (i, 0)), pl.BlockSpec((D, D), lambda i: (0, 0)),
                  pl.BlockSpec((tm, D), lambda i: (i, 0)), pl.BlockSpec((1, D), lambda i: (0, 0))],
        out_specs=[pl.BlockSpec((tm, D), lambda i: (i, 0)), pl.BlockSpec((tm, D), lambda i: (i, 0))],
        out_shape=[SDS((T, D), F32), SDS((T, D), BF)], sem=("parallel",), vmem_mib=32)


def _ffn_up(u2, w_upt, comm=()):
    tm, tn = 1024, 512

    def body(u_ref, w_ref, o_ref):
        o_ref[...] = _dot(u_ref[...], w_ref[...], 1, 1)

    return _call(
        body, (u2, w_upt), name="ffn_up", grid=(T // tm, 2 * DFF // tn),
        in_specs=[pl.BlockSpec((tm, D), lambda i, j: (i, 0)), pl.BlockSpec((tn, D), lambda i, j: (j, 0))],
        out_specs=[pl.BlockSpec((tm, tn), lambda i, j: (i, j))], out_shape=[SDS((T, 2 * DFF), F32)],
        sem=("parallel", "parallel"), vmem_mib=32, comm=comm)


def _ffn_down(a, w_down, h1, tgt):
    tm = 256

    def body(a_ref, w_ref, h1_ref, t_ref, dh_ref, dhb_ref, l_ref):
        @pl.when(pl.program_id(0) == 0)
        def _():
            l_ref[...] = jnp.zeros_like(l_ref)

        h2 = h1_ref[...] + _dot(a_ref[...], w_ref[...], 1, 0)
        e = h2 - t_ref[...]
        dh = e * (1.0 / D)
        dh_ref[...] = dh
        dhb_ref[...] = dh.astype(BF)
        e2 = jnp.sum((e * e).reshape(tm // 8, 8, D), axis=0)
        acc = e2[:, 0:128]
        for k in range(1, D // 128):
            acc = acc + e2[:, k * 128:(k + 1) * 128]
        l_ref[...] += acc

    return _call(
        body, (a, w_down, h1, tgt), name="ffn_down", grid=(T // tm,),
        in_specs=[pl.BlockSpec((tm, DFF), lambda i: (i, 0)), pl.BlockSpec((DFF, D), lambda i: (0, 0)),
                  pl.BlockSpec((tm, D), lambda i: (i, 0)), pl.BlockSpec((tm, D), lambda i: (i, 0))],
        out_specs=[pl.BlockSpec((tm, D), lambda i: (i, 0)), pl.BlockSpec((tm, D), lambda i: (i, 0)),
                   pl.BlockSpec((8, 128), lambda i: (0, 0))],
        out_shape=[SDS((T, D), F32), SDS((T, D), BF), SDS((8, 128), F32)], sem=("arbitrary",), vmem_mib=40)


def _bucket_table():
    q = np.arange(BLK, dtype=np.int32)[:, None]
    j = np.arange(2 * BLK, dtype=np.int32)[None, :]
    n = np.maximum(q + BLK - j, 0)
    nf = np.maximum(n, 1).astype(np.float32)
    max_exact = NBUCKET // 2
    large = max_exact + (np.log(nf / np.float32(max_exact)) / np.float32(math.log(BLK / max_exact))
                         * np.float32(NBUCKET - max_exact)).astype(np.int32)
    large = np.minimum(large, NBUCKET - 1)
    return np.where(n < max_exact, n, large).astype(np.int32)


def _band_bias_bwd(dbias, bucket, me):
    def body(me_ref, db_ref, bk_ref, o_ref):
        bk = bk_ref[...]
        for b in range(NBUCKET):
            m = bk == b
            for h in range(NH):
                v = jnp.where(m, db_ref[h * BLK:(h + 1) * BLK, :], 0.0)
                s = jnp.sum(jnp.sum(v, axis=1, keepdims=True), axis=0, keepdims=True)
                o_ref[0, b:b + 1, h:h + 1] = s

    grid_spec = pltpu.PrefetchScalarGridSpec(
        num_scalar_prefetch=1, grid=(1,),
        in_specs=[pl.BlockSpec((NH * BLK, 2 * BLK), lambda i, me_ref: (0, 0)),
                  pl.BlockSpec((BLK, 2 * BLK), lambda i, me_ref: (0, 0))],
        out_specs=pl.BlockSpec((1, NBUCKET, NH), lambda i, me_ref: (me_ref[0], 0, 0)),
    )
    return _pcall(body, name="band_bias_bwd", grid_spec=grid_spec, out_shape=SDS((N_DEV, NBUCKET, NH), F32),
                  compiler_params=_params(("arbitrary",)))(me, dbias, bucket)


def _mix_forward(P, zc8, zh8, pkv, first, cw, qg, kg, gco, gao, sink_ref, bias_ref):
    gate_b = P[:, 0:CW]
    gate_c = P[:, CW:2 * CW]
    hc = P[:, 2 * CW:3 * CW]
    z = gate_c * hc
    keep = jnp.where(first, 0.0, 1.0)
    zp = zc8 * zh8 * keep
    p1 = zp[7:8, :]
    p2 = zp[6:7, :]
    row = lax.broadcasted_iota(jnp.int32, (BLK, 1), 0)
    z1 = jnp.where(row == 0, p1, pltpu.roll(z, 1, 0))
    z2 = jnp.where(row == 0, p2, jnp.where(row == 1, p1, pltpu.roll(z, 2, 0)))
    cz = cw[0:1, :] * z2 + cw[1:2, :] * z1 + cw[2:3, :] * z
    y_conv = gate_b * cz

    scale = HD ** -0.5
    qi = lax.broadcasted_iota(jnp.int32, (GQ * BLK, 2 * BLK), 0) & (BLK - 1)
    kj = lax.broadcasted_iota(jnp.int32, (GQ * BLK, 2 * BLK), 1)
    dd = qi + BLK - kj
    first_key = jnp.where(first, BLK, 0)
    valid = (dd >= 0) & (dd < BLK) & (kj >= first_key)

    q0 = 3 * CW
    k0 = q0 + AW
    v0 = k0 + NKV * HD
    heads = []
    outs = []
    for kv in range(NKV):
        kb_raw = jnp.concatenate([pkv[:, kv * HD:(kv + 1) * HD], P[:, k0 + kv * HD:k0 + (kv + 1) * HD]], axis=0)
        rk = lax.rsqrt(jnp.mean(kb_raw * kb_raw, axis=-1, keepdims=True) + EPS)
        kb = (kb_raw * rk * kg).astype(BF)
        vb = jnp.concatenate([pkv[:, NKV * HD + kv * HD:NKV * HD + (kv + 1) * HD],
                              P[:, v0 + kv * HD:v0 + (kv + 1) * HD]], axis=0).astype(BF)
        q_raw, rq, qn = [], [], []
        for g in range(GQ):
            h = kv * GQ + g
            qh = P[:, q0 + h * HD:q0 + (h + 1) * HD]
            r = lax.rsqrt(jnp.mean(qh * qh, axis=-1, keepdims=True) + EPS)
            q_raw.append(qh)
            rq.append(r)
            qn.append(qh * r * qg)
        Q = jnp.concatenate(qn, axis=0).astype(BF)
        S = _dot(Q, kb, 1, 1) * scale + bias_ref[kv * GQ * BLK:(kv + 1) * GQ * BLK, :]
        S = jnp.where(valid, S, NEG_INF)
        sink = jnp.concatenate([jnp.full((BLK, 1), sink_ref[0, kv * GQ + g], F32) for g in range(GQ)], axis=0)
        m = jnp.maximum(jnp.max(S, axis=-1, keepdims=True), sink)
        p = jnp.exp(S - m)
        es = jnp.exp(sink - m)
        denom = jnp.sum(p, axis=-1, keepdims=True) + es
        probs = p / denom
        O = _dot(probs.astype(BF), vb, 1, 0)
        heads.append(dict(kb_raw=kb_raw, rk=rk, kb=kb, vb=vb, q_raw=q_raw, rq=rq, Q=Q, probs=probs,
                          psink=es / denom, O=O))
        outs += [O[g * BLK:(g + 1) * BLK, :] for g in range(GQ)]
    y_attn = jnp.concatenate(outs, axis=1)

    rc = lax.rsqrt(jnp.mean(y_conv * y_conv, axis=-1, keepdims=True) + EPS)
    ra = lax.rsqrt(jnp.mean(y_attn * y_attn, axis=-1, keepdims=True) + EPS)
    y = jnp.concatenate([y_conv * rc * gco, y_attn * ra * gao], axis=1)
    return dict(gate_b=gate_b, gate_c=gate_c, hc=hc, z=z, z1=z1, z2=z2, cz=cz, y_conv=y_conv, y_attn=y_attn,
                rc=rc, ra=ra, heads=heads, y=y, row=row, scale=scale)


def _mix_in_specs(blk):
    return [
        pl.BlockSpec(memory_space=pltpu.SMEM),
        pl.BlockSpec((BLK, INW), lambda s: (blk(s), 0)),
        pl.BlockSpec((8, CW), lambda s: (jnp.maximum(blk(s) * (BLK // 8) - 1, 0), 1)),
        pl.BlockSpec((8, CW), lambda s: (jnp.maximum(blk(s) * (BLK // 8) - 1, 0), 2)),
        pl.BlockSpec((BLK, 2 * NKV * HD), lambda s: (jnp.maximum(blk(s) - 1, 0), (3 * CW + AW) // (2 * NKV * HD))),
    ]


def _mix_param_specs():
    return [
        pl.BlockSpec((8, CW), lambda s: (0, 0)),
        pl.BlockSpec((1, HD), lambda s: (0, 0)),
        pl.BlockSpec((1, HD), lambda s: (0, 0)),
        pl.BlockSpec((1, CW), lambda s: (0, 0)),
        pl.BlockSpec((1, AW), lambda s: (0, 0)),
        pl.BlockSpec((NH * BLK, 2 * BLK), lambda s: (0, 0)),
    ]


def _mix_fwd(proj, sinks, cw8, qg, kg, gco, gao, bias, comm=()):
    def body(sink_ref, p_ref, zc_ref, zh_ref, pkv_ref, cw_ref, qg_ref, kg_ref, gco_ref, gao_ref, bias_ref, y_ref):
        first = pl.program_id(0) == 0
        f = _mix_forward(p_ref[...], zc_ref[...], zh_ref[...], pkv_ref[...], first, cw_ref[...], qg_ref[...],
                         kg_ref[...], gco_ref[...], gao_ref[...], sink_ref, bias_ref)
        y_ref[...] = f["y"].astype(BF)

    return _call(
        body, (sinks, proj, proj, proj, proj, cw8, qg, kg, gco, gao, bias), name="mix_fwd", grid=(NB,),
        in_specs=_mix_in_specs(lambda s: s) + _mix_param_specs(),
        out_specs=[pl.BlockSpec((BLK, D), lambda s: (s, 0))], out_shape=[SDS((T, D), BF)],
        sem=("parallel",), vmem_mib=32, comm=comm)


def _mix_bwd(proj, dy, sinks, cw8, qg, kg, gco, gao, bias, comm=()):
    def blk(s):
        return NB - 1 - s

    def body(sink_ref, p_ref, zc_ref, zh_ref, pkv_ref, dy_ref, cw_ref, qg_ref, kg_ref, gco_ref, gao_ref, bias_ref,
             dproj_ref, dcw_ref, dqg_ref, dkg_ref, dgco_ref, dgao_ref, dsink_ref, dbias_ref,
             ndcz_ref, dkc_ref, dvc_ref):
        s = pl.program_id(0)
        first = s == NB - 1

        @pl.when(s == 0)
        def _():
            for r in (dcw_ref, dqg_ref, dkg_ref, dgco_ref, dgao_ref, dsink_ref, dbias_ref, ndcz_ref, dkc_ref, dvc_ref):
                r[...] = jnp.zeros_like(r)

        cw = cw_ref[...]
        qg_v, kg_v, gco_v, gao_v = qg_ref[...], kg_ref[...], gco_ref[...], gao_ref[...]
        f = _mix_forward(p_ref[...], zc_ref[...], zh_ref[...], pkv_ref[...], first, cw, qg_v, kg_v, gco_v, gao_v,
                         sink_ref, bias_ref)
        dy = dy_ref[...]
        dyc, dgco = _rms_bwd(dy[:, 0:CW], f["y_conv"], f["rc"], gco_v)
        dya, dgao = _rms_bwd(dy[:, CW:CW + AW], f["y_attn"], f["ra"], gao_v)
        dgco_ref[...] += dgco
        dgao_ref[...] += dgao

        row = f["row"]
        dgate_b = dyc * f["cz"]
        dcz = dyc * f["gate_b"]
        dcw_ref[0:1, :] += jnp.sum(dcz * f["z2"], axis=0, keepdims=True)
        dcw_ref[1:2, :] += jnp.sum(dcz * f["z1"], axis=0, keepdims=True)
        dcw_ref[2:3, :] += jnp.sum(dcz * f["z"], axis=0, keepdims=True)
        nxt = ndcz_ref[...]
        n0 = nxt[0:1, :]
        n1 = nxt[1:2, :]
        d1 = jnp.where(row == BLK - 1, n0, pltpu.roll(dcz, BLK - 1, 0))
        d2 = jnp.where(row == BLK - 1, n1, jnp.where(row == BLK - 2, n0, pltpu.roll(dcz, BLK - 2, 0)))
        dz = cw[2:3, :] * dcz + cw[1:2, :] * d1 + cw[0:1, :] * d2
        ndcz_ref[...] = dcz[0:8, :]
        dproj_ref[:, 0:CW] = dgate_b.astype(BF)
        dproj_ref[:, CW:2 * CW] = (dz * f["hc"]).astype(BF)
        dproj_ref[:, 2 * CW:3 * CW] = (dz * f["gate_c"]).astype(BF)

        scale = f["scale"]
        lane = lax.broadcasted_iota(jnp.int32, (1, 128), 1)
        dq_cols, dk_cols, dv_cols = [], [], []
        for kv in range(NKV):
            hd = f["heads"][kv]
            dO = jnp.concatenate([dya[:, (kv * GQ + g) * HD:(kv * GQ + g + 1) * HD] for g in range(GQ)], axis=0)
            delta = jnp.sum(dO * hd["O"], axis=-1, keepdims=True)
            dOb = dO.astype(BF)
            dP = _dot(dOb, hd["vb"], 1, 1)
            dS = hd["probs"] * (dP - delta)
            dsk = hd["psink"] * delta
            for g in range(GQ):
                h = kv * GQ + g
                tot = jnp.sum(dsk[g * BLK:(g + 1) * BLK, :], axis=0, keepdims=True)
                dsink_ref[...] -= jnp.where(lane == h, tot, 0.0)
            dbias_ref[kv * GQ * BLK:(kv + 1) * GQ * BLK, :] += dS
            dSs = (dS * scale).astype(BF)
            dQ = _dot(dSs, hd["kb"], 1, 0)
            dKb = _dot(dSs, hd["Q"], 0, 0)
            dVb = _dot(hd["probs"].astype(BF), dOb, 0, 0)
            dkn = dKb[BLK:, :] + dkc_ref[:, kv * HD:(kv + 1) * HD]
            dvn = dVb[BLK:, :] + dvc_ref[:, kv * HD:(kv + 1) * HD]
            dkc_ref[:, kv * HD:(kv + 1) * HD] = dKb[:BLK, :]
            dvc_ref[:, kv * HD:(kv + 1) * HD] = dVb[:BLK, :]
            dk_raw, dkg = _rms_bwd(dkn, hd["kb_raw"][BLK:, :], hd["rk"][BLK:, :], kg_v)
            dkg_ref[...] += dkg
            dk_cols.append(dk_raw)
            dv_cols.append(dvn)
            for g in range(GQ):
                dq_raw, dqg = _rms_bwd(dQ[g * BLK:(g + 1) * BLK, :], hd["q_raw"][g], hd["rq"][g], qg_v)
                dqg_ref[...] += dqg
                dq_cols.append(dq_raw)
        dproj_ref[:, 3 * CW:INW] = jnp.concatenate(dq_cols + dk_cols + dv_cols, axis=1).astype(BF)

    small = lambda r, c: pl.BlockSpec((r, c), lambda s: (0, 0))
    return _call(
        body, (sinks, proj, proj, proj, proj, dy, cw8, qg, kg, gco, gao, bias), name="mix_bwd", grid=(NB,),
        in_specs=_mix_in_specs(blk) + [pl.BlockSpec((BLK, D), lambda s: (blk(s), 0))] + _mix_param_specs(),
        out_specs=[pl.BlockSpec((BLK, INW), lambda s: (blk(s), 0)), small(8, CW), small(1, HD), small(1, HD),
                   small(1, CW), small(1, AW), small(1, 128), small(NH * BLK, 2 * BLK)],
        out_shape=[SDS((T, INW), BF), SDS((8, CW), F32), SDS((1, HD), F32), SDS((1, HD), F32), SDS((1, CW), F32),
                   SDS((1, AW), F32), SDS((1, 128), F32), SDS((NH * BLK, 2 * BLK), F32)],
        scratch_shapes=[pltpu.VMEM((8, CW), F32), pltpu.VMEM((BLK, NKV * HD), F32), pltpu.VMEM((BLK, NKV * HD), F32)],
        sem=("arbitrary",), vmem_mib=40, comm=comm)


FT = 256
NFT = DFF // FT
RC = 64
NCH = T // RC


def _rows8(x):
    return jnp.sum(x.reshape(x.shape[0] // 8, 8, x.shape[1]), axis=0)


def _ffn_act_specs():
    return [
        pl.BlockSpec((T, FT), lambda j: (0, j)), pl.BlockSpec((T, FT), lambda j: (0, NFT + j)),
        pl.BlockSpec((8, FT), lambda j: (0, j)), pl.BlockSpec((8, FT), lambda j: (0, NFT + j)),
        pl.BlockSpec((1, FT), lambda j: (0, j)), pl.BlockSpec((1, FT), lambda j: (0, NFT + j)),
    ]


def _conv_rows(win, w, b, n):
    u = win[8:8 + n]
    u1 = pltpu.roll(win, 1, 0)[8:8 + n]
    u2 = pltpu.roll(win, 2, 0)[8:8 + n]
    return u2, u1, u, w[0:1, :] * u2 + w[1:2, :] * u1 + w[2:3, :] * u + b


def _ffn_act(up, fw8, fb):
    def body(ug_ref, uv_ref, wg_ref, wv_ref, bg_ref, bv_ref, a_ref):
        wg, wv, bg, bv = wg_ref[...], wv_ref[...], bg_ref[...], bv_ref[...]

        def chunk(win_g, win_v):
            gp = _conv_rows(win_g, wg, bg, RC)[3]
            vp = _conv_rows(win_v, wv, bv, RC)[3]
            return (gp * jax.nn.sigmoid(gp) * vp).astype(BF)

        zero = jnp.zeros((8, FT), F32)
        a_ref[0:RC, :] = chunk(jnp.concatenate([zero, ug_ref[0:RC, :]], axis=0),
                               jnp.concatenate([zero, uv_ref[0:RC, :]], axis=0))

        def step(i, carry):
            r0 = pl.multiple_of(i * RC, RC)
            win = pl.ds(r0 - 8, RC + 8)
            a_ref[pl.ds(r0, RC), :] = chunk(ug_ref[win, :], uv_ref[win, :])
            return carry

        lax.fori_loop(1, NCH, step, 0)

    return _call(
        body, (up, up, fw8, fw8, fb, fb), name="ffn_act", grid=(NFT,), in_specs=_ffn_act_specs(),
        out_specs=[pl.BlockSpec((T, FT), lambda j: (0, j))], out_shape=[SDS((T, DFF), BF)],
        sem=("parallel",), vmem_mib=40)


def _ffn_act_bwd(up, da, fw8, fb, comm=()):
    ext = RC + 8

    def body(ug_ref, uv_ref, wg_ref, wv_ref, bg_ref, bv_ref, da_ref,
             dug_ref, duv_ref, dwg_ref, dwv_ref, dbg_ref, dbv_ref):
        wg, wv, bg, bv = wg_ref[...], wv_ref[...], bg_ref[...], bv_ref[...]

        def chunk(win_g, win_v, da_e):
            g2, g1, g0, gp = _conv_rows(win_g, wg, bg, ext)
            v2, v1, v0, vp = _conv_rows(win_v, wv, bv, ext)
            sig = jax.nn.sigmoid(gp)
            dvp = da_e * (gp * sig)
            dgp = da_e * vp * (sig * (1.0 + gp * (1.0 - sig)))

            def back(dp, w):
                return (w[2:3, :] * dp[0:RC] + w[1:2, :] * pltpu.roll(dp, ext - 1, 0)[0:RC]
                        + w[0:1, :] * pltpu.roll(dp, ext - 2, 0)[0:RC]).astype(BF)

            def sums(dp, u2, u1, u0):
                d = dp[0:RC]
                return [_rows8(d), _rows8(d * u2[0:RC]), _rows8(d * u1[0:RC]), _rows8(d * u0[0:RC])]

            return back(dgp, wg), back(dvp, wv), sums(dgp, g2, g1, g0) + sums(dvp, v2, v1, v0)

        zero = jnp.zeros((8, FT), F32)
        dug, duv, acc = chunk(jnp.concatenate([zero, ug_ref[0:ext, :]], axis=0),
                              jnp.concatenate([zero, uv_ref[0:ext, :]], axis=0), da_ref[0:ext, :])
        dug_ref[0:RC, :] = dug
        duv_ref[0:RC, :] = duv

        def step(i, acc):
            r0 = pl.multiple_of(i * RC, RC)
            win = pl.ds(r0 - 8, ext + 8)
            dug, duv, part = chunk(ug_ref[win, :], uv_ref[win, :], da_ref[pl.ds(r0, ext), :])
            dug_ref[pl.ds(r0, RC), :] = dug
            duv_ref[pl.ds(r0, RC), :] = duv
            return [a + p for a, p in zip(acc, part)]

        acc = lax.fori_loop(1, NCH - 1, step, acc)
        r0 = T - RC
        tail = lambda ref, lo: jnp.concatenate([ref[lo:T, :], zero], axis=0)
        dug, duv, part = chunk(tail(ug_ref, r0 - 8), tail(uv_ref, r0 - 8), tail(da_ref, r0))
        dug_ref[r0:T, :] = dug
        duv_ref[r0:T, :] = duv
        tot = [jnp.sum(a + p, axis=0, keepdims=True) for a, p in zip(acc, part)]
        for k, (dw_ref, db_ref) in enumerate(((dwg_ref, dbg_ref), (dwv_ref, dbv_ref))):
            db_ref[...] = tot[4 * k]
            dw_ref[...] = jnp.zeros_like(dw_ref)
            for r in range(3):
                dw_ref[r:r + 1, :] = tot[4 * k + 1 + r]

    col = lambda r: pl.BlockSpec((r, FT), lambda j: (0, j))
    return _call(
        body, (up, up, fw8, fw8, fb, fb, da), name="ffn_act_bwd", grid=(NFT,),
        in_specs=_ffn_act_specs() + [pl.BlockSpec((T, FT), lambda j: (0, j))],
        out_specs=[col(T), col(T), col(8), col(8), col(1), col(1)],
        out_shape=[SDS((T, DFF), BF), SDS((T, DFF), BF), SDS((8, DFF), F32), SDS((8, DFF), F32),
                   SDS((1, DFF), F32), SDS((1, DFF), F32)],
        sem=("parallel",), vmem_mib=40, comm=comm)


def _ffn_down_bwd(dh2b, w_down, comm=()):
    tm = 256

    def body(d_ref, w_ref, o_ref):
        o_ref[...] = _dot(d_ref[...], w_ref[...], 1, 1)

    return _call(
        body, (dh2b, w_down), name="ffn_down_bwd", grid=(T // tm,),
        in_specs=[pl.BlockSpec((tm, D), lambda i: (i, 0)), pl.BlockSpec((DFF, D), lambda i: (0, 0))],
        out_specs=[pl.BlockSpec((tm, DFF), lambda i: (i, 0))], out_shape=[SDS((T, DFF), F32)],
        sem=("parallel",), vmem_mib=40, comm=comm)


def _norm_matmul_bwd(name, a_list, w_t, k_offsets, xin, g, dres, want_bf16, comm=(), slot=None):
    tm = 256
    ks = [a.shape[1] for a in a_list]
    n_a = len(a_list)
    n_pre = 0 if slot is None else 1

    def body(*refs):
        refs = refs[n_pre:]
        a_refs = refs[:n_a]
        w_ref, x_ref, g_ref, r_ref = refs[n_a:n_a + 4]
        outs = refs[n_a + 4:]
        dx_ref, dg_ref = outs[0], (outs[-1] if slot is None else outs[-1].at[0])

        @pl.when(pl.program_id(0) == 0)
        def _():
            dg_ref[...] = jnp.zeros_like(dg_ref)

        du = _dot(a_refs[0][...], w_ref[k_offsets[0]:k_offsets[0] + ks[0], :], 1, 0)
        for k in range(1, n_a):
            du = du + _dot(a_refs[k][...], w_ref[k_offsets[k]:k_offsets[k] + ks[k], :], 1, 0)
        x = x_ref[...]
        r = lax.rsqrt(jnp.mean(x * x, axis=-1, keepdims=True) + EPS)
        dx, dg = _rms_bwd(du, x, r, g_ref[...])
        dx = r_ref[...] + dx
        dx_ref[...] = dx
        if want_bf16:
            outs[1][...] = dx.astype(BF)
        dg_ref[...] += dg

    tile = lambda c: pl.BlockSpec((tm, c), lambda i, *_: (i, 0))
    if slot is None:
        dg_spec, dg_shape = pl.BlockSpec((1, D), lambda i: (0, 0)), SDS((1, D), F32)
    else:
        dg_spec, dg_shape = pl.BlockSpec((1, 1, D), lambda i, slot_ref: (slot_ref[0], 0, 0)), SDS((N_DEV, 1, D), F32)
    out_specs = [tile(D)] + ([tile(D)] if want_bf16 else []) + [dg_spec]
    out_shape = [SDS((T, D), F32)] + ([SDS((T, D), BF)] if want_bf16 else []) + [dg_shape]
    return _call(
        body, (*a_list, w_t, xin, g, dres), name=name, grid=(T // tm,), prefetch=() if slot is None else (slot,),
        in_specs=[tile(k) for k in ks] + [pl.BlockSpec(w_t.shape, lambda i, *_: (0, 0)), tile(D),
                                           pl.BlockSpec((1, D), lambda i, *_: (0, 0)), tile(D)],
        out_specs=out_specs, out_shape=out_shape, sem=("arbitrary",), vmem_mib=56, comm=comm)


def _out_bwd(dh1b, w_out):
    tm = 256

    def body(d_ref, w_ref, o_ref):
        o_ref[...] = _dot(d_ref[...], w_ref[...], 1, 1)

    return _call(
        body, (dh1b, w_out), name="out_bwd", grid=(T // tm,),
        in_specs=[pl.BlockSpec((tm, D), lambda i: (i, 0)), pl.BlockSpec((D, D), lambda i: (0, 0))],
        out_specs=[pl.BlockSpec((tm, D), lambda i: (i, 0))], out_shape=[SDS((T, D), F32)],
        sem=("parallel",), vmem_mib=32)


def _wgrad(name, a_list, b, comm=()):
    tm = 256
    steps = [a.shape[1] // tm for a in a_list]
    starts = [sum(steps[:k]) for k in range(len(a_list))]
    n_a = len(a_list)

    def body(*refs):
        a_refs, b_ref, o32_ref, obf_ref = refs[:n_a], refs[n_a], refs[n_a + 1], refs[n_a + 2]
        i = pl.program_id(0)
        for k in range(n_a):
            @pl.when((i >= starts[k]) & (i < starts[k] + steps[k]))
            def _(k=k):
                r = _dot(a_refs[k][...], b_ref[...], 0, 0)
                o32_ref[...] = r
                obf_ref[...] = r.astype(BF)

    def a_spec(k):
        return pl.BlockSpec((T, tm), lambda i: (0, jnp.clip(i - starts[k], 0, steps[k] - 1)))

    m_total = tm * sum(steps)
    return _call(
        body, (*a_list, b), name=name, grid=(sum(steps),),
        in_specs=[a_spec(k) for k in range(n_a)] + [pl.BlockSpec((T, D), lambda i: (0, 0))],
        out_specs=[pl.BlockSpec((tm, D), lambda i: (i, 0)), pl.BlockSpec((tm, D), lambda i: (i, 0))],
        out_shape=[SDS((m_total, D), F32), SDS((m_total, D), BF)], sem=("parallel",), vmem_mib=40, comm=comm)


def _chip_sum(name, g32, from_sib, core, chip):
    h = g32.shape[1]
    th = h // 2

    def body(core_ref, chip_ref, g_ref, s_ref, pbf_ref, own_ref):
        p = g_ref[0] + s_ref[0].astype(F32)
        pbf_ref[0] = p.astype(BF)

        @pl.when(pl.program_id(1) == chip_ref[0])
        def _():
            own_ref[...] = p

    grid_spec = pltpu.PrefetchScalarGridSpec(
        num_scalar_prefetch=2, grid=(h // th, N_CHIPS),
        in_specs=[pl.BlockSpec((1, th, D), lambda t, jj, core_ref, chip_ref: (2 * jj + core_ref[0], t, 0)),
                  pl.BlockSpec((1, th, D), lambda t, jj, core_ref, chip_ref: (jj, t, 0))],
        out_specs=[pl.BlockSpec((1, th, D), lambda t, jj, core_ref, chip_ref: (jj, t, 0)),
                   pl.BlockSpec((th, D), lambda t, jj, core_ref, chip_ref: (t, 0))],
    )
    return _pcall(
        body, name=name, grid_spec=grid_spec, out_shape=_in_hbm([SDS((N_CHIPS, h, D), BF), SDS((h, D), F32)]),
        compiler_params=_params(("arbitrary", "arbitrary"), 32),
    )(core, chip, *_from_hbm(g32, from_sib))


def _final_sum(name, own, from_chips, core):
    h = own.shape[0]

    def body(core_ref, o_ref, r_ref, f_ref):
        f_ref[0] = ((o_ref[...] + r_ref[0].astype(F32)) + r_ref[1].astype(F32)) + r_ref[2].astype(F32)

    grid_spec = pltpu.PrefetchScalarGridSpec(
        num_scalar_prefetch=1, grid=(1,),
        in_specs=[pl.BlockSpec((h, D), lambda i, core_ref: (0, 0)), pl.BlockSpec((3, h, D), lambda i, core_ref: (0, 0, 0))],
        out_specs=pl.BlockSpec((1, h, D), lambda i, core_ref: (core_ref[0], 0, 0)),
    )
    return _pcall(body, name=name, grid_spec=grid_spec, out_shape=pltpu.HBM((2, h, D), F32),
                  compiler_params=_params(("arbitrary",), 40))(core, *_from_hbm(own, from_chips))


def _adam_math(w, g, m, v):
    nm = ADAM_B1 * m + (1.0 - ADAM_B1) * g
    nv = ADAM_B2 * v + (1.0 - ADAM_B2) * (g * g)
    m_hat = nm / (1.0 - ADAM_B1 ** ADAM_STEP)
    v_hat = nv / (1.0 - ADAM_B2 ** ADAM_STEP)
    return -ADAM_LR * (m_hat / (jnp.sqrt(v_hat) + ADAM_EPS) + ADAM_WD * w), nm, nv


def _adamw(name, w, g, m, v, tr, copy_g=False):
    rows, cols = w.shape

    def body(w_ref, g_ref, m_ref, v_ref, *outs):
        g_val = g_ref[...]
        if copy_g:
            outs[0][...] = g_val
        d_ref, nm_ref, nv_ref = outs[-3:]
        d_ref[...], nm_ref[...], nv_ref[...] = _adam_math(w_ref[...], g_val, m_ref[...], v_ref[...])

    spec = pl.BlockSpec((tr, cols), lambda i: (i, 0))
    n_out = 4 if copy_g else 3
    return _call(body, (w, g, m, v), name=name, grid=(rows // tr,), in_specs=[spec] * 4, out_specs=[spec] * n_out,
                 out_shape=[SDS((rows, cols), F32)] * n_out, sem=("parallel",), vmem_mib=32, free=(0, 2, 3))


C_G1, C_G2, C_GCO, C_GAO, C_DCW, C_DQG, C_DKG, C_SINK, C_SQ = 0, 1024, 2048, 2560, 3072, 4608, 4736, 4864, 5632
P_W = C_SQ + 128


def _pack_small(me, dfwg, dfwv, dfbg, dfbv, dg2, dgco, dgao, dcw8, dqg, dkg, dsink, sq):
    def body(me_ref, dfwg_r, dfwv_r, dfbg_r, dfbv_r, dg2_r, dgco_r, dgao_r, dcw_r, dqg_r, dkg_r, dsink_r, sq_r, o):
        o[...] = jnp.zeros_like(o)
        o[0, :, 0:DFF] = dfwg_r[...]
        o[0, :, DFF:2 * DFF] = dfwv_r[...]
        o[0, 3:4, 0:DFF] = dfbg_r[...]
        o[0, 3:4, DFF:2 * DFF] = dfbv_r[...]
        o[0, 4:5, C_G2:C_G2 + D] = dg2_r[...]
        o[0, 4:5, C_GCO:C_GCO + CW] = dgco_r[...]
        o[0, 4:5, C_GAO:C_GAO + AW] = dgao_r[...]
        for r in range(3):
            o[0, 4:5, C_DCW + r * CW:C_DCW + (r + 1) * CW] = dcw_r[r:r + 1, :]
        o[0, 4:5, C_DQG:C_DQG + HD] = dqg_r[...]
        o[0, 4:5, C_DKG:C_DKG + HD] = dkg_r[...]
        o[0, 4:5, C_SINK:C_SINK + 128] = dsink_r[...]
        o[0, :, C_SQ:C_SQ + 128] = sq_r[...]

    ins = (dfwg, dfwv, dfbg, dfbv, dg2, dgco, dgao, dcw8, dqg, dkg, dsink, sq)
    return _call(body, ins, name="pack_small", grid=(1,), prefetch=(me,),
                 in_specs=[pl.BlockSpec(a.shape, lambda i, me_ref: (0, 0)) for a in ins],
                 out_specs=[pl.BlockSpec((1, 8, P_W), lambda i, me_ref: (me_ref[0], 0, 0))],
                 out_shape=[SDS((N_DEV, 8, P_W), F32)], sem=("arbitrary",))[0]


N_SMALL = 11


def _small_adam(chip, p_all, g1_all, tbl_all, ws, ms, vs):
    fw_cols = 2 * DFF // N_CHIPS
    cw_cols = CW // N_CHIPS

    def body(chip_ref, p_ref, fw_ref, cw0_ref, cw1_ref, cw2_ref, g1_ref, tbl_ref, *refs):
        w_r, m_r, v_r = refs[0:N_SMALL], refs[N_SMALL:2 * N_SMALL], refs[2 * N_SMALL:3 * N_SMALL]
        outs = refs[3 * N_SMALL:]
        g_o, d_o, nm_o, nv_o = (outs[k * N_SMALL:(k + 1) * N_SMALL] for k in range(4))
        loss_o = outs[4 * N_SMALL]

        def total(ref):
            s = ref[0]
            for k in range(1, N_DEV):
                s = s + ref[k]
            return s

        S = total(p_ref)
        fw = total(fw_ref)
        cws = [total(r) for r in (cw0_ref, cw1_ref, cw2_ref)]

        def step(i, g, at):
            d, nm, nv = _adam_math(w_r[i][at], g, m_r[i][at], v_r[i][at])
            g_o[i][at], d_o[i][at], nm_o[i][at], nv_o[i][at] = g, d, nm, nv

        everything = (slice(None), slice(None))
        step(0, total(g1_ref), everything)
        for r in range(3):
            step(1, cws[r][4:5, :], (0, slice(r, r + 1), slice(None)))
        step(2, S[4:5, C_DQG:C_DQG + HD], everything)
        step(3, S[4:5, C_DKG:C_DKG + HD], everything)
        step(4, total(tbl_ref), everything)
        step(5, S[4:5, C_SINK:C_SINK + NH], everything)
        step(6, S[4:5, C_GCO:C_GCO + CW], everything)
        step(7, S[4:5, C_GAO:C_GAO + AW], everything)
        step(8, S[4:5, C_G2:C_G2 + D], everything)
        step(9, fw[0:3, :], (0, slice(None), slice(None)))
        step(10, S[3:4, 0:2 * DFF], everything)
        sq = S[:, C_SQ:C_SQ + 128]
        loss_o[...] = jnp.sum(jnp.sum(sq, axis=1, keepdims=True), axis=0, keepdims=True) * (0.5 / D)

    def full(a):
        n = len(a.shape)
        return pl.BlockSpec(a.shape, lambda i, chip_ref: (0,) * n)

    params = [*ws, *ms, *vs]
    out = _call(
        body, (p_all, p_all, p_all, p_all, p_all, g1_all, tbl_all, *params), name="small_adam", grid=(1,), prefetch=(chip,),
        in_specs=[full(p_all),
                  pl.BlockSpec((N_DEV, 8, fw_cols), lambda i, chip_ref: (0, 0, chip_ref[0])),
                  *[pl.BlockSpec((N_DEV, 8, cw_cols), lambda i, chip_ref, r=r: (0, 0, (C_DCW + r * CW) // cw_cols + chip_ref[0]))
                    for r in range(3)],
                  full(g1_all), full(tbl_all), *[full(a) for a in params]],
        out_specs=[full(a) for a in ws] * 4 + [pl.BlockSpec((1, 1), lambda i, chip_ref: (0, 0))],
        out_shape=[SDS(a.shape, F32) for a in ws] * 4 + [SDS((1, 1), F32)], sem=("arbitrary",), vmem_mib=32)
    return out[0:N_SMALL], out[N_SMALL:2 * N_SMALL], out[2 * N_SMALL:3 * N_SMALL], out[3 * N_SMALL:4 * N_SMALL], out[4 * N_SMALL]


PLACE_STEPS = 4


def _place_specs(shards):
    rows = [s.shape[0] // PLACE_STEPS for s in shards]
    return ([pl.BlockSpec((r, D), lambda i, chip_ref: (i, 0)) for r in rows],
            [pl.BlockSpec((r, D), lambda i, chip_ref: (chip_ref[0] * PLACE_STEPS + i, 0)) for r in rows],
            [SDS((N_CHIPS * s.shape[0], D), BF) for s in shards])


def _place_first(chip, shard, conv_w, ffn_conv_w):
    def body(chip_ref, a, s0, s1, o, t0, t1):
        o[...] = a[...].astype(BF)

        @pl.when(pl.program_id(0) == 0)
        def _():
            for s, t in ((s0, t0), (s1, t1)):
                t[...] = jnp.zeros_like(t)
                t[0, 0:3, :] = s[...]

    ins, outs, shapes = _place_specs([shard])
    taps = (conv_w, ffn_conv_w)
    return _call(
        body, (shard, conv_w, ffn_conv_w), name="place_first", grid=(PLACE_STEPS,), prefetch=(chip,),
        in_specs=ins + [pl.BlockSpec(s.shape, lambda i, chip_ref: (0, 0)) for s in taps],
        out_specs=outs + [pl.BlockSpec((1, 8, s.shape[1]), lambda i, chip_ref: (chip_ref[0], 0, 0)) for s in taps],
        out_shape=shapes + [SDS((N_CHIPS, 8, s.shape[1]), F32) for s in taps],
        sem=("arbitrary",), vmem_mib=32, free=(0, 1, 2))


def _place_rest(chip, shards, table, bucket, comm):
    n = len(shards)

    def body(chip_ref, *refs):
        a, (tab_ref, bk_ref), o, bias_ref = refs[:n], refs[n:n + 2], refs[n + 2:2 * n + 2], refs[2 * n + 2]
        for src, dst in zip(a, o):
            dst[...] = src[...].astype(BF)

        @pl.when(pl.program_id(0) == 0)
        def _():
            bk = bk_ref[...]
            eq = [bk == b for b in range(NBUCKET)]
            for h in range(NH):
                acc = jnp.zeros((BLK, 2 * BLK), F32)
                for b in range(NBUCKET):
                    acc = jnp.where(eq[b], tab_ref[b, h], acc)
                bias_ref[h * BLK:(h + 1) * BLK, :] = acc

    ins, outs, shapes = _place_specs(shards)
    return _call(
        body, (*shards, table, bucket), name="place_rest", grid=(PLACE_STEPS,), prefetch=(chip,),
        in_specs=ins + [pl.BlockSpec(memory_space=pltpu.SMEM), pl.BlockSpec(bucket.shape, lambda i, chip_ref: (0, 0))],
        out_specs=outs + [pl.BlockSpec((NH * BLK, 2 * BLK), lambda i, chip_ref: (0, 0))],
        out_shape=shapes + [SDS((NH * BLK, 2 * BLK), F32)],
        sem=("arbitrary",), vmem_mib=32, comm=comm, free=tuple(range(n + 2)))


def kernel(x, norm_mix_g, w_in, conv_w, q_norm_g, k_norm_g, rel_bias_table, sinks, out_norm_conv_g, out_norm_attn_g, w_out, norm_ffn_g, w_up, ffn_conv_w, ffn_conv_b, w_down, loss_target, m_norm_mix_g, m_w_in, m_conv_w, m_q_norm_g, m_k_norm_g, m_rel_bias_table, m_sinks, m_out_norm_conv_g, m_out_norm_attn_g, m_w_out, m_norm_ffn_g, m_w_up, m_ffn_conv_w, m_ffn_conv_b, m_w_down, v_norm_mix_g, v_w_in, v_conv_w, v_q_norm_g, v_k_norm_g, v_rel_bias_table, v_sinks, v_out_norm_conv_g, v_out_norm_attn_g, v_w_out, v_norm_ffn_g, v_w_up, v_ffn_conv_w, v_ffn_conv_b, v_w_down):
    as_arg = lambda i: jnp.reshape(i, (1,)).astype(jnp.int32)
    chip = as_arg(2 * lax.axis_index("x") + lax.axis_index("y"))
    core = as_arg(lax.axis_index("c"))
    me = 2 * chip + core
    xs, tgt = x[0], loss_target[0]
    qg, kg, gco, gao, g1, g2, fb = q_norm_g, k_norm_g, out_norm_conv_g, out_norm_attn_g, norm_mix_g, norm_ffn_g, ffn_conv_b
    pieces = lambda g: g.reshape(N_DEV, g.shape[0] // N_DEV, D)
    whole = lambda f: f.reshape(2 * f.shape[1], D)

    bucket = jnp.asarray(_bucket_table())
    p_in, p_cw, p_fw = _place_first(chip, w_in[0].T, conv_w[0], ffn_conv_w[0])
    p_out, p_up, p_down, bias, w_int, cw_all, fw_all = _place_rest(
        chip, [w_out[0], w_up[0].T, w_down[0]], rel_bias_table, bucket,
        comm=[_t_gather(p_in), _t_small_weights(p_cw), _t_small_weights(p_fw)])
    cw8 = jnp.transpose(cw_all, (1, 0, 2)).reshape(8, CW)
    fw8 = jnp.transpose(fw_all, (1, 0, 2)).reshape(8, 2 * DFF)

    proj, u1, w_out_f = _inproj(xs, g1, w_int, comm=[_t_gather(p_out)])
    y, w_upt = _mix_fwd(proj, sinks, cw8, qg, kg, gco, gao, bias, comm=[_t_gather(p_up)])
    h1, u2 = _outproj(y, w_out_f, xs, g2)
    up, w_down_f = _ffn_up(u2, w_upt, comm=[_t_gather(p_down)])
    a, = _ffn_act(up, fw8, fb)
    dh2, dh2b, sq = _ffn_down(a, w_down_f, h1, tgt)

    gd32, gdbf = _wgrad("wgrad_down", [a], dh2b)
    da, sib_down = _ffn_down_bwd(dh2b, w_down_f, comm=[_t_sibling(pieces(gdbf))])
    pbf_down, own_down = _chip_sum("chip_sum_w_down", pieces(gd32), sib_down, core, chip)
    dug, duv, dfwg, dfwv, dfbg, dfbv, chips_down = _ffn_act_bwd(up, da, fw8, fb, comm=[_t_chips(pbf_down)])
    fin_down = _final_sum("final_sum_w_down", own_down, chips_down, core)
    gu32, gubf = _wgrad("wgrad_up", [dug, duv], u2)
    dh1, dh1b, dg2, sib_up, fin_down = _norm_matmul_bwd(
        "ffn_up_bwd", [dug, duv], w_upt, [0, DFF], h1, g2, dh2, True, comm=[_t_sibling(pieces(gubf)), _t_swap(fin_down)])
    pbf_up, own_up = _chip_sum("chip_sum_w_up", pieces(gu32), sib_up, core, chip)
    go32, gobf = _wgrad("wgrad_out", [y], dh1b)
    dy, = _out_bwd(dh1b, w_out_f)
    dproj, dcw8, dqg, dkg, dgco, dgao, dsink, dbias, chips_up, sib_out = _mix_bwd(
        proj, dy, sinks, cw8, qg, kg, gco, gao, bias, comm=[_t_chips(pbf_up), _t_sibling(pieces(gobf))])
    fin_up = _final_sum("final_sum_w_up", own_up, chips_up, core)
    pbf_out, own_out = _chip_sum("chip_sum_w_out", pieces(go32), sib_out, core, chip)
    tbl_all = _band_bias_bwd(dbias, bucket, me)
    p_all = _pack_small(me, dfwg, dfwv, dfbg, dfbv, dg2, dgco, dgao, dcw8, dqg, dkg, dsink, sq)
    gi32, gibf, chips_out, fin_up, p_all, tbl_all = _wgrad(
        "wgrad_in", [dproj], u1, comm=[_t_chips(pbf_out), _t_swap(fin_up), _t_allgather(p_all), _t_allgather(tbl_all)])
    fin_out = _final_sum("final_sum_w_out", own_out, chips_out, core)
    sib_in, fin_out = _comm_call("to_sibling_last", [_t_sibling(pieces(gibf)), _t_swap(fin_out)])
    pbf_in, own_in = _chip_sum("chip_sum_w_in", pieces(gi32), sib_in, core, chip)
    dx, g1_all, chips_in = _norm_matmul_bwd(
        "in_bwd", [dproj], w_int, [0], xs, g1, dh1, False, comm=[_t_chips(pbf_in)], slot=me)
    fin_in = _final_sum("final_sum_w_in", own_in, chips_in, core)
    g1_all, fin_in = _comm_call("gather_last", [_t_allgather(g1_all), _t_swap(fin_in)])

    g_w_in, g_w_out, g_w_up, g_w_down = whole(fin_in).T, whole(fin_out), whole(fin_up).T, whole(fin_down)
    g_w_down, d_down, nm_down, nv_down = _adamw("adamw_w_down", w_down[0], g_w_down, m_w_down[0], v_w_down[0], 352, True)
    d_up, nm_up, nv_up = _adamw("adamw_w_up", w_up[0], g_w_up, m_w_up[0], v_w_up[0], 256)
    g_w_out, d_out, nm_out, nv_out = _adamw("adamw_w_out", w_out[0], g_w_out, m_w_out[0], v_w_out[0], 256, True)
    d_in, nm_in, nv_in = _adamw("adamw_w_in", w_in[0], g_w_in, m_w_in[0], v_w_in[0], 256)
    sw = [norm_mix_g, conv_w, q_norm_g, k_norm_g, rel_bias_table, sinks, out_norm_conv_g, out_norm_attn_g,
          norm_ffn_g, ffn_conv_w, ffn_conv_b]
    smm = [m_norm_mix_g, m_conv_w, m_q_norm_g, m_k_norm_g, m_rel_bias_table, m_sinks, m_out_norm_conv_g,
           m_out_norm_attn_g, m_norm_ffn_g, m_ffn_conv_w, m_ffn_conv_b]
    smv = [v_norm_mix_g, v_conv_w, v_q_norm_g, v_k_norm_g, v_rel_bias_table, v_sinks, v_out_norm_conv_g,
           v_out_norm_attn_g, v_norm_ffn_g, v_ffn_conv_w, v_ffn_conv_b]
    sg, sd, snm, snv, loss = _small_adam(chip, p_all, g1_all, tbl_all, sw, smm, smv)

    def order(s, b_in, b_out, b_up, b_down):
        return (s[0], b_in[None], s[1], s[2], s[3], s[4], s[5], s[6], s[7], b_out[None], s[8], b_up[None],
                s[9], s[10], b_down[None])

    return (loss.reshape(()), dx[None],
            *order(sg, g_w_in, g_w_out, g_w_up, g_w_down),
            *order(sd, d_in, d_out, d_up, d_down),
            *order(snm, nm_in, nm_out, nm_up, nm_down),
            *order(snv, nv_in, nv_out, nv_up, nv_down))
```

```python
import functools
import math

import numpy as np

import jax
import jax.numpy as jnp
from jax import lax
from jax.experimental import pallas as pl
from jax.experimental.pallas import tpu as pltpu

F32 = jnp.float32
BF = jnp.bfloat16
SDS = jax.ShapeDtypeStruct

T = 2048
D = 1024
CW = 512
AW = 512
HD = 64
NH = 8
NKV = 2
GQ = 4
INW = 2304
DFF = 2816
BLK = 128
NB = T // BLK
NBUCKET = 32
EPS = 1e-6
NEG_INF = -1e30
N_CHIPS = 4
N_DEV = 8

ADAM_LR = 0.001
ADAM_B1 = 0.9
ADAM_B2 = 0.999
ADAM_EPS = 1e-08
ADAM_WD = 0.01
ADAM_STEP = 10

MIB = 1024 * 1024
MESH = pl.DeviceIdType.MESH
ANY = pl.BlockSpec(memory_space=pl.ANY)

_pcall = pl.pallas_call


def _params(sem=None, vmem_mib=None):
    kw = {}
    if sem is not None:
        kw["dimension_semantics"] = sem
    if vmem_mib is not None:
        kw["vmem_limit_bytes"] = vmem_mib * MIB
    return pltpu.CompilerParams(**kw)


def _dot(a, b, ca, cb):
    return lax.dot_general(a, b, (((ca,), (cb,)), ((), ())), preferred_element_type=F32)


def _rms_bwd(dy, x, r, g):
    dg = jnp.sum(dy * (x * r), axis=0, keepdims=True)
    dgx = dy * g
    dx = r * dgx - x * (r * r * r) * jnp.mean(x * dgx, axis=-1, keepdims=True)
    return dx, dg


def _where():
    x, y, c = lax.axis_index("x"), lax.axis_index("y"), lax.axis_index("c")
    return x, y, c, [(1 - x, y), (x, 1 - y), (1 - x, 1 - y)]


def _rcopy(src, dst, ssem, rsem, dev):
    return pltpu.make_async_remote_copy(src_ref=src, dst_ref=dst, send_sem=ssem, recv_sem=rsem, device_id=dev,
                                        device_id_type=MESH)


class _Task:
    def __init__(self, ins, outs, alias, n_sem, start, finish):
        self.ins, self.outs, self.alias, self.n_sem, self.start, self.finish = ins, outs, alias, n_sem, start, finish


def _t_gather(placed):
    R = placed.shape[0] // N_CHIPS

    def rows(chip_index, core):
        return pl.ds(pl.multiple_of(chip_index * R + core * (R // 2), 16), R // 2)

    def start(cin, cout, ss, rs, b):
        x, y, c, chips = _where()
        mine = cout[0].at[rows(2 * x + y, c)]
        for r, (px, py) in enumerate(chips):
            _rcopy(mine, mine, ss.at[b + r], rs.at[b + r], (px, py, c)).start()

    def finish(cin, cout, ss, rs, b):
        x, y, c, chips = _where()
        buf = cout[0]
        sib = (x, y, 1 - c)
        for r, (px, py) in enumerate(chips):
            got = buf.at[rows(2 * px + py, c)]
            _rcopy(got, got, ss.at[b + r], rs.at[b + r], (px, py, c)).wait_recv()
            _rcopy(got, got, ss.at[b + 3 + r], rs.at[b + 3 + r], sib).start()
        for r, (px, py) in enumerate(chips):
            got = buf.at[rows(2 * px + py, 1 - c)]
            _rcopy(got, got, ss.at[b + 3 + r], rs.at[b + 3 + r], sib).wait_recv()
        mine = buf.at[rows(2 * x + y, c)]
        for r in range(6):
            _rcopy(mine, mine, ss.at[b + r], rs.at[b + r], sib).wait_send()

    return _Task([placed], [SDS(placed.shape, placed.dtype)], [(0, 0)], 6, start, finish)


def _t_small_weights(buf):
    def start(cin, cout, ss, rs, b):
        x, y, c, chips = _where()
        mine = cout[0].at[2 * x + y]
        for r, (px, py) in enumerate(chips):
            _rcopy(mine, mine, ss.at[b + r], rs.at[b + r], (px, py, c)).start()

    def finish(cin, cout, ss, rs, b):
        x, y, c, chips = _where()
        for r, (px, py) in enumerate(chips):
            got = cout[0].at[2 * px + py]
            _rcopy(got, got, ss.at[b + r], rs.at[b + r], (px, py, c)).wait_recv()
        for r, (px, py) in enumerate(chips):
            mine = cout[0].at[2 * x + y]
            _rcopy(mine, mine, ss.at[b + r], rs.at[b + r], (px, py, c)).wait_send()

    return _Task([buf], [SDS(buf.shape, buf.dtype)], [(0, 0)], 3, start, finish)


def _t_sibling(gbf):
    def start(cin, cout, ss, rs, b):
        x, y, c, _ = _where()
        for jj in range(N_CHIPS):
            _rcopy(cin[0].at[2 * jj + (1 - c)], cout[0].at[jj], ss.at[b + jj], rs.at[b + jj], (x, y, 1 - c)).start()

    def finish(cin, cout, ss, rs, b):
        x, y, c, _ = _where()
        for jj in range(N_CHIPS):
            got = cout[0].at[jj]
            _rcopy(got, got, ss.at[b + jj], rs.at[b + jj], (x, y, 1 - c)).wait_recv()
        for jj in range(N_CHIPS):
            got = cout[0].at[jj]
            _rcopy(got, got, ss.at[b + jj], rs.at[b + jj], (x, y, 1 - c)).wait_send()

    return _Task([gbf], [SDS((N_CHIPS,) + gbf.shape[1:], BF)], [], N_CHIPS, start, finish)


def _t_chips(pbf):
    def start(cin, cout, ss, rs, b):
        x, y, c, chips = _where()
        for r, (px, py) in enumerate(chips):
            _rcopy(cin[0].at[2 * px + py], cout[0].at[r], ss.at[b + r], rs.at[b + r], (px, py, c)).start()

    def finish(cin, cout, ss, rs, b):
        x, y, c, chips = _where()
        for r, (px, py) in enumerate(chips):
            got = cout[0].at[r]
            _rcopy(got, got, ss.at[b + r], rs.at[b + r], (px, py, c)).wait_recv()
        for r, (px, py) in enumerate(chips):
            got = cout[0].at[r]
            _rcopy(got, got, ss.at[b + r], rs.at[b + r], (px, py, c)).wait_send()

    return _Task([pbf], [SDS((3,) + pbf.shape[1:], BF)], [], 3, start, finish)


def _t_swap(fin):
    def start(cin, cout, ss, rs, b):
        x, y, c, _ = _where()
        mine = cout[0].at[c]
        _rcopy(mine, mine, ss.at[b], rs.at[b], (x, y, 1 - c)).start()

    def finish(cin, cout, ss, rs, b):
        x, y, c, _ = _where()
        got = cout[0].at[1 - c]
        _rcopy(got, got, ss.at[b], rs.at[b], (x, y, 1 - c)).wait_recv()
        _rcopy(got, got, ss.at[b], rs.at[b], (x, y, 1 - c)).wait_send()

    return _Task([fin], [SDS(fin.shape, fin.dtype)], [(0, 0)], 1, start, finish)


def _t_allgather(buf):
    def peers():
        x, y, c, _ = _where()
        out = []
        for rel in range(1, N_DEV):
            px, py, pc = x ^ ((rel >> 2) & 1), y ^ ((rel >> 1) & 1), c ^ (rel & 1)
            out.append((rel - 1, 4 * px + 2 * py + pc, (px, py, pc)))
        return 4 * x + 2 * y + c, out

    def start(cin, cout, ss, rs, b):
        me, ps = peers()
        mine = cout[0].at[me]
        for k, _, dev in ps:
            _rcopy(mine, mine, ss.at[b + k], rs.at[b + k], dev).start()

    def finish(cin, cout, ss, rs, b):
        me, ps = peers()
        for k, pidx, dev in ps:
            got = cout[0].at[pidx]
            _rcopy(got, got, ss.at[b + k], rs.at[b + k], dev).wait_recv()
        for k, _, dev in ps:
            mine = cout[0].at[me]
            _rcopy(mine, mine, ss.at[b + k], rs.at[b + k], dev).wait_send()

    return _Task([buf], [SDS(buf.shape, buf.dtype)], [(0, 0)], N_DEV - 1, start, finish)


def _run_tasks(comm, which, cin, cout, ss, rs):
    i0 = o0 = s0 = 0
    for t in comm:
        getattr(t, which)(cin[i0:i0 + len(t.ins)], cout[o0:o0 + len(t.outs)], ss, rs, s0)
        i0, o0, s0 = i0 + len(t.ins), o0 + len(t.outs), s0 + t.n_sem


def _from_hbm(*arrays):
    return [pltpu.with_memory_space_constraint(a, pltpu.HBM) for a in arrays]


def _in_hbm(shapes):
    return [pltpu.HBM(s.shape, s.dtype) for s in shapes]


def _comm_layout(comm, n_in, n_out):
    c_in = [a for t in comm for a in t.ins]
    c_out = [s for t in comm for s in t.outs]
    aliases, i0, o0 = {}, 0, 0
    for t in comm:
        for i, o in t.alias:
            aliases[n_in + i0 + i] = n_out + o0 + o
        i0, o0 = i0 + len(t.ins), o0 + len(t.outs)
    return c_in, c_out, aliases, sum(t.n_sem for t in comm)


def _call(body, operands, *, name, grid, in_specs, out_specs, out_shape, scratch_shapes=(), sem=None, vmem_mib=None, comm=(),
          free=(), prefetch=()):
    operands = [o if s.memory_space == pltpu.SMEM or k in free else pltpu.with_memory_space_constraint(o, pltpu.HBM)
                for k, (o, s) in enumerate(zip(operands, in_specs))]
    n_pre, n_in, n_out, n_scr = len(prefetch), len(in_specs), len(out_specs), len(scratch_shapes)
    c_in, c_out, aliases, n_sem = _comm_layout(comm, n_pre + n_in, n_out)
    sems = [pltpu.SemaphoreType.DMA((n_sem,)), pltpu.SemaphoreType.DMA((n_sem,))] if comm else []

    def wrapped(*refs):
        pre, refs = refs[:n_pre], refs[n_pre:]
        ins, cin = refs[:n_in], refs[n_in:n_in + len(c_in)]
        rest = refs[n_in + len(c_in):]
        outs, cout = rest[:n_out], rest[n_out:n_out + len(c_out)]
        rest = rest[n_out + len(c_out):]
        scr, csem = rest[:n_scr], rest[n_scr:]
        if not comm:
            return body(*pre, *ins, *outs, *scr)
        ids = [pl.program_id(k) for k in range(len(grid))]
        first = functools.reduce(jnp.logical_and, [i == 0 for i in ids])
        last = functools.reduce(jnp.logical_and, [i == n - 1 for i, n in zip(ids, grid)])
        pl.when(first)(lambda: _run_tasks(comm, "start", cin, cout, *csem))
        body(*pre, *ins, *outs, *scr)
        pl.when(last)(lambda: _run_tasks(comm, "finish", cin, cout, *csem))

    grid_spec = pltpu.PrefetchScalarGridSpec(
        num_scalar_prefetch=n_pre, grid=grid, in_specs=list(in_specs) + [ANY] * len(c_in),
        out_specs=list(out_specs) + [ANY] * len(c_out), scratch_shapes=list(scratch_shapes) + sems)
    return _pcall(
        wrapped, name=name, grid_spec=grid_spec, out_shape=_in_hbm(list(out_shape) + c_out), input_output_aliases=aliases,
        compiler_params=_params(("arbitrary",) * len(grid) if comm else sem, vmem_mib),
    )(*prefetch, *operands, *_from_hbm(*c_in))


def _comm_call(name, comm):
    c_in, c_out, aliases, n_sem = _comm_layout(comm, 0, 0)

    def body(*refs):
        cin, cout, (ss, rs) = refs[:len(c_in)], refs[len(c_in):len(c_in) + len(c_out)], refs[len(c_in) + len(c_out):]
        _run_tasks(comm, "start", cin, cout, ss, rs)
        _run_tasks(comm, "finish", cin, cout, ss, rs)

    return _pcall(
        body, name=name, in_specs=[ANY] * len(c_in), out_specs=[ANY] * len(c_out), out_shape=_in_hbm(c_out),
        scratch_shapes=[pltpu.SemaphoreType.DMA((n_sem,)), pltpu.SemaphoreType.DMA((n_sem,))],
        input_output_aliases=aliases,
    )(*_from_hbm(*c_in))


def _inproj(x, g1, w_int, comm=()):
    tm = 256

    def body(x_ref, g_ref, w_ref, proj_ref, u_ref):
        xf = x_ref[...]
        r = lax.rsqrt(jnp.mean(xf * xf, axis=-1, keepdims=True) + EPS)
        u = (xf * r * g_ref[...]).astype(BF)
        u_ref[...] = u
        proj_ref[...] = _dot(u, w_ref[...], 1, 1)

    return _call(
        body, (x, g1, w_int), name="inproj", grid=(T // tm,),
        in_specs=[pl.BlockSpec((tm, D), lambda i: (i, 0)), pl.BlockSpec((1, D), lambda i: (0, 0)),
                  pl.BlockSpec((INW, D), lambda i: (0, 0))],
        out_specs=[pl.BlockSpec((tm, INW), lambda i: (i, 0)), pl.BlockSpec((tm, D), lambda i: (i, 0))],
        out_shape=[SDS((T, INW), F32), SDS((T, D), BF)], sem=("parallel",), vmem_mib=40, comm=comm)


def _outproj(y, w_out, x, g2):
    tm = 256

    def body(y_ref, w_ref, x_ref, g_ref, h1_ref, u2_ref):
        h1 = x_ref[...] + _dot(y_ref[...], w_ref[...], 1, 0)
        h1_ref[...] = h1
        r = lax.rsqrt(jnp.mean(h1 * h1, axis=-1, keepdims=True) + EPS)
        u2_ref[...] = (h1 * r * g_ref[...]).astype(BF)

    return _call(
        body, (y, w_out, x, g2), name="outproj", grid=(T // tm,),
        in_specs=[pl.BlockSpec((tm, D), lambda i: (i, 0)), pl.BlockSpec((D, D), lambda i: (0, 0)),
                  pl.BlockSpec((tm, D), lambda i: (i, 0)), pl.BlockSpec((1, D), lambda i: (0, 0))],
        out_specs=[pl.BlockSpec((tm, D), lambda i: (i, 0)), pl.BlockSpec((tm, D), lambda i: (i, 0))],
        out_shape=[SDS((T, D), F32), SDS((T, D), BF)], sem=("parallel",), vmem_mib=32)


def _ffn_up(u2, w_upt, comm=()):
    tm, tn = 1024, 512

    def body(u_ref, w_ref, o_ref):
        o_ref[...] = _dot(u_ref[...], w_ref[...], 1, 1).astype(BF)

    return _call(
        body, (u2, w_upt), name="ffn_up", grid=(T // tm, 2 * DFF // tn),
        in_specs=[pl.BlockSpec((tm, D), lambda i, j: (i, 0)), pl.BlockSpec((tn, D), lambda i, j: (j, 0))],
        out_specs=[pl.BlockSpec((tm, tn), lambda i, j: (i, j))], out_shape=[SDS((T, 2 * DFF), BF)],
        sem=("parallel", "parallel"), vmem_mib=32, comm=comm)


def _ffn_down(a, w_down, h1, tgt):
    tm = 256

    def body(a_ref, w_ref, h1_ref, t_ref, dh_ref, dhb_ref, l_ref):
        @pl.when(pl.program_id(0) == 0)
        def _():
            l_ref[...] = jnp.zeros_like(l_ref)

        h2 = h1_ref[...] + _dot(a_ref[...], w_ref[...], 1, 0)
        e = h2 - t_ref[...]
        dh = e * (1.0 / D)
        dh_ref[...] = dh
        dhb_ref[...] = dh.astype(BF)
        e2 = jnp.sum((e * e).reshape(tm // 8, 8, D), axis=0)
        acc = e2[:, 0:128]
        for k in range(1, D // 128):
            acc = acc + e2[:, k * 128:(k + 1) * 128]
        l_ref[...] += acc

    return _call(
        body, (a, w_down, h1, tgt), name="ffn_down", grid=(T // tm,),
        in_specs=[pl.BlockSpec((tm, DFF), lambda i: (i, 0)), pl.BlockSpec((DFF, D), lambda i: (0, 0)),
                  pl.BlockSpec((tm, D), lambda i: (i, 0)), pl.BlockSpec((tm, D), lambda i: (i, 0))],
        out_specs=[pl.BlockSpec((tm, D), lambda i: (i, 0)), pl.BlockSpec((tm, D), lambda i: (i, 0)),
                   pl.BlockSpec((8, 128), lambda i: (0, 0))],
        out_shape=[SDS((T, D), F32), SDS((T, D), BF), SDS((8, 128), F32)], sem=("arbitrary",), vmem_mib=40)


def _bucket_table():
    q = np.arange(BLK, dtype=np.int32)[:, None]
    j = np.arange(2 * BLK, dtype=np.int32)[None, :]
    n = np.maximum(q + BLK - j, 0)
    nf = np.maximum(n, 1).astype(np.float32)
    max_exact = NBUCKET // 2
    large = max_exact + (np.log(nf / np.float32(max_exact)) / np.float32(math.log(BLK / max_exact))
                         * np.float32(NBUCKET - max_exact)).astype(np.int32)
    large = np.minimum(large, NBUCKET - 1)
    return np.where(n < max_exact, n, large).astype(np.int32)


def _band_bias_bwd(dbias, bucket, me):
    def body(me_ref, db_ref, bk_ref, o_ref):
        bk = bk_ref[...]
        for b in range(NBUCKET):
            m = bk == b
            for h in range(NH):
                v = jnp.where(m, db_ref[h * BLK:(h + 1) * BLK, :], 0.0)
                s = jnp.sum(jnp.sum(v, axis=1, keepdims=True), axis=0, keepdims=True)
                o_ref[0, b:b + 1, h:h + 1] = s

    grid_spec = pltpu.PrefetchScalarGridSpec(
        num_scalar_prefetch=1, grid=(1,),
        in_specs=[pl.BlockSpec((NH * BLK, 2 * BLK), lambda i, me_ref: (0, 0)),
                  pl.BlockSpec((BLK, 2 * BLK), lambda i, me_ref: (0, 0))],
        out_specs=pl.BlockSpec((1, NBUCKET, NH), lambda i, me_ref: (me_ref[0], 0, 0)),
    )
    return _pcall(body, name="band_bias_bwd", grid_spec=grid_spec, out_shape=SDS((N_DEV, NBUCKET, NH), F32),
                  compiler_params=_params(("arbitrary",)))(me, dbias, bucket)


def _mix_forward(P, zc8, zh8, pkv, first, cw, qg, kg, gco, gao, sink_ref, bias_ref):
    gate_b = P[:, 0:CW]
    gate_c = P[:, CW:2 * CW]
    hc = P[:, 2 * CW:3 * CW]
    z = gate_c * hc
    keep = jnp.where(first, 0.0, 1.0)
    zp = zc8 * zh8 * keep
    p1 = zp[7:8, :]
    p2 = zp[6:7, :]
    row = lax.broadcasted_iota(jnp.int32, (BLK, 1), 0)
    z1 = jnp.where(row == 0, p1, pltpu.roll(z, 1, 0))
    z2 = jnp.where(row == 0, p2, jnp.where(row == 1, p1, pltpu.roll(z, 2, 0)))
    cz = cw[0:1, :] * z2 + cw[1:2, :] * z1 + cw[2:3, :] * z
    y_conv = gate_b * cz

    scale = HD ** -0.5
    qi = lax.broadcasted_iota(jnp.int32, (GQ * BLK, 2 * BLK), 0) & (BLK - 1)
    kj = lax.broadcasted_iota(jnp.int32, (GQ * BLK, 2 * BLK), 1)
    dd = qi + BLK - kj
    first_key = jnp.where(first, BLK, 0)
    valid = (dd >= 0) & (dd < BLK) & (kj >= first_key)

    q0 = 3 * CW
    k0 = q0 + AW
    v0 = k0 + NKV * HD
    heads = []
    outs = []
    for kv in range(NKV):
        kb_raw = jnp.concatenate([pkv[:, kv * HD:(kv + 1) * HD], P[:, k0 + kv * HD:k0 + (kv + 1) * HD]], axis=0)
        rk = lax.rsqrt(jnp.mean(kb_raw * kb_raw, axis=-1, keepdims=True) + EPS)
        kb = (kb_raw * rk * kg).astype(BF)
        vb = jnp.concatenate([pkv[:, NKV * HD + kv * HD:NKV * HD + (kv + 1) * HD],
                              P[:, v0 + kv * HD:v0 + (kv + 1) * HD]], axis=0).astype(BF)
        q_raw, rq, qn = [], [], []
        for g in range(GQ):
            h = kv * GQ + g
            qh = P[:, q0 + h * HD:q0 + (h + 1) * HD]
            r = lax.rsqrt(jnp.mean(qh * qh, axis=-1, keepdims=True) + EPS)
            q_raw.append(qh)
            rq.append(r)
            qn.append(qh * r * qg)
        Q = jnp.concatenate(qn, axis=0).astype(BF)
        S = _dot(Q, kb, 1, 1) * scale + bias_ref[kv * GQ * BLK:(kv + 1) * GQ * BLK, :]
        S = jnp.where(valid, S, NEG_INF)
        sink = jnp.concatenate([jnp.full((BLK, 1), sink_ref[0, kv * GQ + g], F32) for g in range(GQ)], axis=0)
        m = jnp.maximum(jnp.max(S, axis=-1, keepdims=True), sink)
        p = jnp.exp(S - m)
        es = jnp.exp(sink - m)
        denom = jnp.sum(p, axis=-1, keepdims=True) + es
        probs = p / denom
        O = _dot(probs.astype(BF), vb, 1, 0)
        heads.append(dict(kb_raw=kb_raw, rk=rk, kb=kb, vb=vb, q_raw=q_raw, rq=rq, Q=Q, probs=probs,
                          psink=es / denom, O=O))
        outs += [O[g * BLK:(g + 1) * BLK, :] for g in range(GQ)]
    y_attn = jnp.concatenate(outs, axis=1)

    rc = lax.rsqrt(jnp.mean(y_conv * y_conv, axis=-1, keepdims=True) + EPS)
    ra = lax.rsqrt(jnp.mean(y_attn * y_attn, axis=-1, keepdims=True) + EPS)
    y = jnp.concatenate([y_conv * rc * gco, y_attn * ra * gao], axis=1)
    return dict(gate_b=gate_b, gate_c=gate_c, hc=hc, z=z, z1=z1, z2=z2, cz=cz, y_conv=y_conv, y_attn=y_attn,
                rc=rc, ra=ra, heads=heads, y=y, row=row, scale=scale)


def _mix_in_specs(blk):
    return [
        pl.BlockSpec(memory_space=pltpu.SMEM),
        pl.BlockSpec((BLK, INW), lambda s: (blk(s), 0)),
        pl.BlockSpec((8, CW), lambda s: (jnp.maximum(blk(s) * (BLK // 8) - 1, 0), 1)),
        pl.BlockSpec((8, CW), lambda s: (jnp.maximum(blk(s) * (BLK // 8) - 1, 0), 2)),
        pl.BlockSpec((BLK, 2 * NKV * HD), lambda s: (jnp.maximum(blk(s) - 1, 0), (3 * CW + AW) // (2 * NKV * HD))),
    ]


def _mix_param_specs():
    return [
        pl.BlockSpec((8, CW), lambda s: (0, 0)),
        pl.BlockSpec((1, HD), lambda s: (0, 0)),
        pl.BlockSpec((1, HD), lambda s: (0, 0)),
        pl.BlockSpec((1, CW), lambda s: (0, 0)),
        pl.BlockSpec((1, AW), lambda s: (0, 0)),
        pl.BlockSpec((NH * BLK, 2 * BLK), lambda s: (0, 0)),
    ]


def _mix_fwd(proj, sinks, cw8, qg, kg, gco, gao, bias, comm=()):
    def body(sink_ref, p_ref, zc_ref, zh_ref, pkv_ref, cw_ref, qg_ref, kg_ref, gco_ref, gao_ref, bias_ref, y_ref):
        first = pl.program_id(0) == 0
        f = _mix_forward(p_ref[...], zc_ref[...], zh_ref[...], pkv_ref[...], first, cw_ref[...], qg_ref[...],
                         kg_ref[...], gco_ref[...], gao_ref[...], sink_ref, bias_ref)
        y_ref[...] = f["y"].astype(BF)

    return _call(
        body, (sinks, proj, proj, proj, proj, cw8, qg, kg, gco, gao, bias), name="mix_fwd", grid=(NB,),
        in_specs=_mix_in_specs(lambda s: s) + _mix_param_specs(),
        out_specs=[pl.BlockSpec((BLK, D), lambda s: (s, 0))], out_shape=[SDS((T, D), BF)],
        sem=("parallel",), vmem_mib=32, comm=comm)


def _mix_bwd(proj, dy, sinks, cw8, qg, kg, gco, gao, bias, comm=()):
    def blk(s):
        return NB - 1 - s

    def body(sink_ref, p_ref, zc_ref, zh_ref, pkv_ref, dy_ref, cw_ref, qg_ref, kg_ref, gco_ref, gao_ref, bias_ref,
             dproj_ref, dcw_ref, dqg_ref, dkg_ref, dgco_ref, dgao_ref, dsink_ref, dbias_ref,
             ndcz_ref, dkc_ref, dvc_ref):
        s = pl.program_id(0)
        first = s == NB - 1

        @pl.when(s == 0)
        def _():
            for r in (dcw_ref, dqg_ref, dkg_ref, dgco_ref, dgao_ref, dsink_ref, dbias_ref, ndcz_ref, dkc_ref, dvc_ref):
                r[...] = jnp.zeros_like(r)

        cw = cw_ref[...]
        qg_v, kg_v, gco_v, gao_v = qg_ref[...], kg_ref[...], gco_ref[...], gao_ref[...]
        f = _mix_forward(p_ref[...], zc_ref[...], zh_ref[...], pkv_ref[...], first, cw, qg_v, kg_v, gco_v, gao_v,
                         sink_ref, bias_ref)
        dy = dy_ref[...]
        dyc, dgco = _rms_bwd(dy[:, 0:CW], f["y_conv"], f["rc"], gco_v)
        dya, dgao = _rms_bwd(dy[:, CW:CW + AW], f["y_attn"], f["ra"], gao_v)
        dgco_ref[...] += dgco
        dgao_ref[...] += dgao

        row = f["row"]
        dgate_b = dyc * f["cz"]
        dcz = dyc * f["gate_b"]
        dcw_ref[0:1, :] += jnp.sum(dcz * f["z2"], axis=0, keepdims=True)
        dcw_ref[1:2, :] += jnp.sum(dcz * f["z1"], axis=0, keepdims=True)
        dcw_ref[2:3, :] += jnp.sum(dcz * f["z"], axis=0, keepdims=True)
        nxt = ndcz_ref[...]
        n0 = nxt[0:1, :]
        n1 = nxt[1:2, :]
        d1 = jnp.where(row == BLK - 1, n0, pltpu.roll(dcz, BLK - 1, 0))
        d2 = jnp.where(row == BLK - 1, n1, jnp.where(row == BLK - 2, n0, pltpu.roll(dcz, BLK - 2, 0)))
        dz = cw[2:3, :] * dcz + cw[1:2, :] * d1 + cw[0:1, :] * d2
        ndcz_ref[...] = dcz[0:8, :]
        dproj_ref[:, 0:CW] = dgate_b.astype(BF)
        dproj_ref[:, CW:2 * CW] = (dz * f["hc"]).astype(BF)
        dproj_ref[:, 2 * CW:3 * CW] = (dz * f["gate_c"]).astype(BF)

        scale = f["scale"]
        lane = lax.broadcasted_iota(jnp.int32, (1, 128), 1)
        dq_cols, dk_cols, dv_cols = [], [], []
        for kv in range(NKV):
            hd = f["heads"][kv]
            dO = jnp.concatenate([dya[:, (kv * GQ + g) * HD:(kv * GQ + g + 1) * HD] for g in range(GQ)], axis=0)
            delta = jnp.sum(dO * hd["O"], axis=-1, keepdims=True)
            dOb = dO.astype(BF)
            dP = _dot(dOb, hd["vb"], 1, 1)
            dS = hd["probs"] * (dP - delta)
            dsk = hd["psink"] * delta
            for g in range(GQ):
                h = kv * GQ + g
                tot = jnp.sum(dsk[g * BLK:(g + 1) * BLK, :], axis=0, keepdims=True)
                dsink_ref[...] -= jnp.where(lane == h, tot, 0.0)
            dbias_ref[kv * GQ * BLK:(kv + 1) * GQ * BLK, :] += dS
            dSs = (dS * scale).astype(BF)
            dQ = _dot(dSs, hd["kb"], 1, 0)
            dKb = _dot(dSs, hd["Q"], 0, 0)
            dVb = _dot(hd["probs"].astype(BF), dOb, 0, 0)
            dkn = dKb[BLK:, :] + dkc_ref[:, kv * HD:(kv + 1) * HD]
            dvn = dVb[BLK:, :] + dvc_ref[:, kv * HD:(kv + 1) * HD]
            dkc_ref[:, kv * HD:(kv + 1) * HD] = dKb[:BLK, :]
            dvc_ref[:, kv * HD:(kv + 1) * HD] = dVb[:BLK, :]
            dk_raw, dkg = _rms_bwd(dkn, hd["kb_raw"][BLK:, :], hd["rk"][BLK:, :], kg_v)
            dkg_ref[...] += dkg
            dk_cols.append(dk_raw)
            dv_cols.append(dvn)
            for g in range(GQ):
                dq_raw, dqg = _rms_bwd(dQ[g * BLK:(g + 1) * BLK, :], hd["q_raw"][g], hd["rq"][g], qg_v)
                dqg_ref[...] += dqg
                dq_cols.append(dq_raw)
        dproj_ref[:, 3 * CW:INW] = jnp.concatenate(dq_cols + dk_cols + dv_cols, axis=1).astype(BF)

    small = lambda r, c: pl.BlockSpec((r, c), lambda s: (0, 0))
    return _call(
        body, (sinks, proj, proj, proj, proj, dy, cw8, qg, kg, gco, gao, bias), name="mix_bwd", grid=(NB,),
        in_specs=_mix_in_specs(blk) + [pl.BlockSpec((BLK, D), lambda s: (blk(s), 0))] + _mix_param_specs(),
        out_specs=[pl.BlockSpec((BLK, INW), lambda s: (blk(s), 0)), small(8, CW), small(1, HD), small(1, HD),
                   small(1, CW), small(1, AW), small(1, 128), small(NH * BLK, 2 * BLK)],
        out_shape=[SDS((T, INW), BF), SDS((8, CW), F32), SDS((1, HD), F32), SDS((1, HD), F32), SDS((1, CW), F32),
                   SDS((1, AW), F32), SDS((1, 128), F32), SDS((NH * BLK, 2 * BLK), F32)],
        scratch_shapes=[pltpu.VMEM((8, CW), F32), pltpu.VMEM((BLK, NKV * HD), F32), pltpu.VMEM((BLK, NKV * HD), F32)],
        sem=("arbitrary",), vmem_mib=40, comm=comm)


FT = 256
NFT = DFF // FT
RC = 128
NCH = T // RC
LEAD = 16


def _rows8(x):
    return jnp.sum(x.reshape(x.shape[0] // 8, 8, x.shape[1]), axis=0)


def _ffn_act_specs():
    return [
        pl.BlockSpec((T, FT), lambda j: (0, j)), pl.BlockSpec((T, FT), lambda j: (0, NFT + j)),
        pl.BlockSpec((8, FT), lambda j: (0, j)), pl.BlockSpec((8, FT), lambda j: (0, NFT + j)),
        pl.BlockSpec((1, FT), lambda j: (0, j)), pl.BlockSpec((1, FT), lambda j: (0, NFT + j)),
    ]


def _conv_rows(win, w, b, n):
    win = win.astype(F32)
    u = win[LEAD:LEAD + n]
    u1 = pltpu.roll(win, 1, 0)[LEAD:LEAD + n]
    u2 = pltpu.roll(win, 2, 0)[LEAD:LEAD + n]
    return u2, u1, u, w[0:1, :] * u2 + w[1:2, :] * u1 + w[2:3, :] * u + b


def _ffn_act(up, fw8, fb):
    def body(ug_ref, uv_ref, wg_ref, wv_ref, bg_ref, bv_ref, a_ref):
        wg, wv, bg, bv = wg_ref[...], wv_ref[...], bg_ref[...], bv_ref[...]

        def chunk(win_g, win_v):
            gp = _conv_rows(win_g, wg, bg, RC)[3]
            vp = _conv_rows(win_v, wv, bv, RC)[3]
            return (gp * jax.nn.sigmoid(gp) * vp).astype(BF)

        zero = jnp.zeros((LEAD, FT), BF)
        a_ref[0:RC, :] = chunk(jnp.concatenate([zero, ug_ref[0:RC, :]], axis=0),
                               jnp.concatenate([zero, uv_ref[0:RC, :]], axis=0))

        def step(i, carry):
            r0 = pl.multiple_of(i * RC, RC)
            win = pl.ds(r0 - LEAD, RC + LEAD)
            a_ref[pl.ds(r0, RC), :] = chunk(ug_ref[win, :], uv_ref[win, :])
            return carry

        lax.fori_loop(1, NCH, step, 0)

    return _call(
        body, (up, up, fw8, fw8, fb, fb), name="ffn_act", grid=(NFT,), in_specs=_ffn_act_specs(),
        out_specs=[pl.BlockSpec((T, FT), lambda j: (0, j))], out_shape=[SDS((T, DFF), BF)],
        sem=("parallel",), vmem_mib=40)


def _ffn_act_bwd(up, da, fw8, fb, comm=()):
    ext = RC + LEAD

    def body(ug_ref, uv_ref, wg_ref, wv_ref, bg_ref, bv_ref, da_ref,
             dug_ref, duv_ref, dwg_ref, dwv_ref, dbg_ref, dbv_ref):
        wg, wv, bg, bv = wg_ref[...], wv_ref[...], bg_ref[...], bv_ref[...]

        def chunk(win_g, win_v, da_e):
            g2, g1, g0, gp = _conv_rows(win_g, wg, bg, ext)
            v2, v1, v0, vp = _conv_rows(win_v, wv, bv, ext)
            da_e = da_e.astype(F32)
            sig = jax.nn.sigmoid(gp)
            dvp = da_e * (gp * sig)
            dgp = da_e * vp * (sig * (1.0 + gp * (1.0 - sig)))

            def back(dp, w):
                return (w[2:3, :] * dp[0:RC] + w[1:2, :] * pltpu.roll(dp, ext - 1, 0)[0:RC]
                        + w[0:1, :] * pltpu.roll(dp, ext - 2, 0)[0:RC]).astype(BF)

            def sums(dp, u2, u1, u0):
                d = dp[0:RC]
                return [_rows8(d), _rows8(d * u2[0:RC]), _rows8(d * u1[0:RC]), _rows8(d * u0[0:RC])]

            return back(dgp, wg), back(dvp, wv), sums(dgp, g2, g1, g0) + sums(dvp, v2, v1, v0)

        zero = jnp.zeros((LEAD, FT), BF)
        dug, duv, acc = chunk(jnp.concatenate([zero, ug_ref[0:ext, :]], axis=0),
                              jnp.concatenate([zero, uv_ref[0:ext, :]], axis=0), da_ref[0:ext, :])
        dug_ref[0:RC, :] = dug
        duv_ref[0:RC, :] = duv

        def step(i, acc):
            r0 = pl.multiple_of(i * RC, RC)
            win = pl.ds(r0 - LEAD, ext + LEAD)
            dug, duv, part = chunk(ug_ref[win, :], uv_ref[win, :], da_ref[pl.ds(r0, ext), :])
            dug_ref[pl.ds(r0, RC), :] = dug
            duv_ref[pl.ds(r0, RC), :] = duv
            return [a + p for a, p in zip(acc, part)]

        acc = lax.fori_loop(1, NCH - 1, step, acc)
        r0 = T - RC
        tail = lambda ref, lo: jnp.concatenate([ref[lo:T, :], zero], axis=0)
        dug, duv, part = chunk(tail(ug_ref, r0 - LEAD), tail(uv_ref, r0 - LEAD), tail(da_ref, r0))
        dug_ref[r0:T, :] = dug
        duv_ref[r0:T, :] = duv
        tot = [jnp.sum(a + p, axis=0, keepdims=True) for a, p in zip(acc, part)]
        for k, (dw_ref, db_ref) in enumerate(((dwg_ref, dbg_ref), (dwv_ref, dbv_ref))):
            db_ref[...] = tot[4 * k]
            dw_ref[...] = jnp.zeros_like(dw_ref)
            for r in range(3):
                dw_ref[r:r + 1, :] = tot[4 * k + 1 + r]

    col = lambda r: pl.BlockSpec((r, FT), lambda j: (0, j))
    return _call(
        body, (up, up, fw8, fw8, fb, fb, da), name="ffn_act_bwd", grid=(NFT,),
        in_specs=_ffn_act_specs() + [pl.BlockSpec((T, FT), lambda j: (0, j))],
        out_specs=[col(T), col(T), col(8), col(8), col(1), col(1)],
        out_shape=[SDS((T, DFF), BF), SDS((T, DFF), BF), SDS((8, DFF), F32), SDS((8, DFF), F32),
                   SDS((1, DFF), F32), SDS((1, DFF), F32)],
        sem=("parallel",), vmem_mib=40, comm=comm)


def _ffn_down_bwd(dh2b, w_down, comm=()):
    tm = 256

    def body(d_ref, w_ref, o_ref):
        o_ref[...] = _dot(d_ref[...], w_ref[...], 1, 1).astype(BF)

    return _call(
        body, (dh2b, w_down), name="ffn_down_bwd", grid=(T // tm,),
        in_specs=[pl.BlockSpec((tm, D), lambda i: (i, 0)), pl.BlockSpec((DFF, D), lambda i: (0, 0))],
        out_specs=[pl.BlockSpec((tm, DFF), lambda i: (i, 0))], out_shape=[SDS((T, DFF), BF)],
        sem=("parallel",), vmem_mib=40, comm=comm)


def _norm_matmul_bwd(name, a_list, w_t, k_offsets, xin, g, dres, want_bf16, comm=(), slot=None):
    tm = 256
    ks = [a.shape[1] for a in a_list]
    n_a = len(a_list)
    n_pre = 0 if slot is None else 1

    def body(*refs):
        refs = refs[n_pre:]
        a_refs = refs[:n_a]
        w_ref, x_ref, g_ref, r_ref = refs[n_a:n_a + 4]
        outs = refs[n_a + 4:]
        dx_ref, dg_ref = outs[0], (outs[-1] if slot is None else outs[-1].at[0])

        @pl.when(pl.program_id(0) == 0)
        def _():
            dg_ref[...] = jnp.zeros_like(dg_ref)

        du = _dot(a_refs[0][...], w_ref[k_offsets[0]:k_offsets[0] + ks[0], :], 1, 0)
        for k in range(1, n_a):
            du = du + _dot(a_refs[k][...], w_ref[k_offsets[k]:k_offsets[k] + ks[k], :], 1, 0)
        x = x_ref[...]
        r = lax.rsqrt(jnp.mean(x * x, axis=-1, keepdims=True) + EPS)
        dx, dg = _rms_bwd(du, x, r, g_ref[...])
        dx = r_ref[...] + dx
        dx_ref[...] = dx
        if want_bf16:
            outs[1][...] = dx.astype(BF)
        dg_ref[...] += dg

    tile = lambda c: pl.BlockSpec((tm, c), lambda i, *_: (i, 0))
    if slot is None:
        dg_spec, dg_shape = pl.BlockSpec((1, D), lambda i: (0, 0)), SDS((1, D), F32)
    else:
        dg_spec, dg_shape = pl.BlockSpec((1, 1, D), lambda i, slot_ref: (slot_ref[0], 0, 0)), SDS((N_DEV, 1, D), F32)
    out_specs = [tile(D)] + ([tile(D)] if want_bf16 else []) + [dg_spec]
    out_shape = [SDS((T, D), F32)] + ([SDS((T, D), BF)] if want_bf16 else []) + [dg_shape]
    return _call(
        body, (*a_list, w_t, xin, g, dres), name=name, grid=(T // tm,), prefetch=() if slot is None else (slot,),
        in_specs=[tile(k) for k in ks] + [pl.BlockSpec(w_t.shape, lambda i, *_: (0, 0)), tile(D),
                                           pl.BlockSpec((1, D), lambda i, *_: (0, 0)), tile(D)],
        out_specs=out_specs, out_shape=out_shape, sem=("arbitrary",), vmem_mib=56, comm=comm)


def _out_bwd(dh1b, w_out):
    tm = 256

    def body(d_ref, w_ref, o_ref):
        o_ref[...] = _dot(d_ref[...], w_ref[...], 1, 1)

    return _call(
        body, (dh1b, w_out), name="out_bwd", grid=(T // tm,),
        in_specs=[pl.BlockSpec((tm, D), lambda i: (i, 0)), pl.BlockSpec((D, D), lambda i: (0, 0))],
        out_specs=[pl.BlockSpec((tm, D), lambda i: (i, 0))], out_shape=[SDS((T, D), F32)],
        sem=("parallel",), vmem_mib=32)


def _wgrad(name, a_list, b, comm=()):
    tm = 256
    steps = [a.shape[1] // tm for a in a_list]
    starts = [sum(steps[:k]) for k in range(len(a_list))]
    n_a = len(a_list)

    def body(*refs):
        a_refs, b_ref, o_ref = refs[:n_a], refs[n_a], refs[n_a + 1]
        i = pl.program_id(0)
        for k in range(n_a):
            @pl.when((i >= starts[k]) & (i < starts[k] + steps[k]))
            def _(k=k):
                o_ref[...] = _dot(a_refs[k][...], b_ref[...], 0, 0).astype(BF)

    def a_spec(k):
        return pl.BlockSpec((T, tm), lambda i: (0, jnp.clip(i - starts[k], 0, steps[k] - 1)))

    m_total = tm * sum(steps)
    return _call(
        body, (*a_list, b), name=name, grid=(sum(steps),),
        in_specs=[a_spec(k) for k in range(n_a)] + [pl.BlockSpec((T, D), lambda i: (0, 0))],
        out_specs=[pl.BlockSpec((tm, D), lambda i: (i, 0))], out_shape=[SDS((m_total, D), BF)],
        sem=("parallel",), vmem_mib=40, comm=comm)


def _chip_sum(name, gbf, from_sib, core, chip):
    h = gbf.shape[1]
    th = h // 2

    def body(core_ref, chip_ref, g_ref, s_ref, pbf_ref, own_ref):
        p = g_ref[0].astype(F32) + s_ref[0].astype(F32)
        pbf_ref[0] = p.astype(BF)

        @pl.when(pl.program_id(1) == chip_ref[0])
        def _():
            own_ref[...] = p

    grid_spec = pltpu.PrefetchScalarGridSpec(
        num_scalar_prefetch=2, grid=(h // th, N_CHIPS),
        in_specs=[pl.BlockSpec((1, th, D), lambda t, jj, core_ref, chip_ref: (2 * jj + core_ref[0], t, 0)),
                  pl.BlockSpec((1, th, D), lambda t, jj, core_ref, chip_ref: (jj, t, 0))],
        out_specs=[pl.BlockSpec((1, th, D), lambda t, jj, core_ref, chip_ref: (jj, t, 0)),
                   pl.BlockSpec((th, D), lambda t, jj, core_ref, chip_ref: (t, 0))],
    )
    return _pcall(
        body, name=name, grid_spec=grid_spec, out_shape=_in_hbm([SDS((N_CHIPS, h, D), BF), SDS((h, D), F32)]),
        compiler_params=_params(("arbitrary", "arbitrary"), 32),
    )(core, chip, *_from_hbm(gbf, from_sib))


def _final_sum(name, own, from_chips, core):
    h = own.shape[0]

    def body(core_ref, o_ref, r_ref, f_ref):
        f_ref[0] = ((o_ref[...] + r_ref[0].astype(F32)) + r_ref[1].astype(F32)) + r_ref[2].astype(F32)

    grid_spec = pltpu.PrefetchScalarGridSpec(
        num_scalar_prefetch=1, grid=(1,),
        in_specs=[pl.BlockSpec((h, D), lambda i, core_ref: (0, 0)), pl.BlockSpec((3, h, D), lambda i, core_ref: (0, 0, 0))],
        out_specs=pl.BlockSpec((1, h, D), lambda i, core_ref: (core_ref[0], 0, 0)),
    )
    return _pcall(body, name=name, grid_spec=grid_spec, out_shape=pltpu.HBM((2, h, D), F32),
                  compiler_params=_params(("arbitrary",), 40))(core, *_from_hbm(own, from_chips))


def _adam_math(w, g, m, v):
    nm = ADAM_B1 * m + (1.0 - ADAM_B1) * g
    nv = ADAM_B2 * v + (1.0 - ADAM_B2) * (g * g)
    m_hat = nm / (1.0 - ADAM_B1 ** ADAM_STEP)
    v_hat = nv / (1.0 - ADAM_B2 ** ADAM_STEP)
    return -ADAM_LR * (m_hat / (jnp.sqrt(v_hat) + ADAM_EPS) + ADAM_WD * w), nm, nv


def _adamw(name, w, g, m, v, tr, copy_g=False):
    rows, cols = w.shape

    def body(w_ref, g_ref, m_ref, v_ref, *outs):
        g_val = g_ref[...]
        if copy_g:
            outs[0][...] = g_val
        d_ref, nm_ref, nv_ref = outs[-3:]
        d_ref[...], nm_ref[...], nv_ref[...] = _adam_math(w_ref[...], g_val, m_ref[...], v_ref[...])

    spec = pl.BlockSpec((tr, cols), lambda i: (i, 0))
    n_out = 4 if copy_g else 3
    return _call(body, (w, g, m, v), name=name, grid=(rows // tr,), in_specs=[spec] * 4, out_specs=[spec] * n_out,
                 out_shape=[SDS((rows, cols), F32)] * n_out, sem=("parallel",), vmem_mib=32, free=(0, 2, 3))


C_G1, C_G2, C_GCO, C_GAO, C_DCW, C_DQG, C_DKG, C_SINK, C_SQ = 0, 1024, 2048, 2560, 3072, 4608, 4736, 4864, 5632
P_W = C_SQ + 128


def _pack_small(me, dfwg, dfwv, dfbg, dfbv, dg2, dgco, dgao, dcw8, dqg, dkg, dsink, sq):
    def body(me_ref, dfwg_r, dfwv_r, dfbg_r, dfbv_r, dg2_r, dgco_r, dgao_r, dcw_r, dqg_r, dkg_r, dsink_r, sq_r, o):
        o[...] = jnp.zeros_like(o)
        o[0, :, 0:DFF] = dfwg_r[...]
        o[0, :, DFF:2 * DFF] = dfwv_r[...]
        o[0, 3:4, 0:DFF] = dfbg_r[...]
        o[0, 3:4, DFF:2 * DFF] = dfbv_r[...]
        o[0, 4:5, C_G2:C_G2 + D] = dg2_r[...]
        o[0, 4:5, C_GCO:C_GCO + CW] = dgco_r[...]
        o[0, 4:5, C_GAO:C_GAO + AW] = dgao_r[...]
        for r in range(3):
            o[0, 4:5, C_DCW + r * CW:C_DCW + (r + 1) * CW] = dcw_r[r:r + 1, :]
        o[0, 4:5, C_DQG:C_DQG + HD] = dqg_r[...]
        o[0, 4:5, C_DKG:C_DKG + HD] = dkg_r[...]
        o[0, 4:5, C_SINK:C_SINK + 128] = dsink_r[...]
        o[0, :, C_SQ:C_SQ + 128] = sq_r[...]

    ins = (dfwg, dfwv, dfbg, dfbv, dg2, dgco, dgao, dcw8, dqg, dkg, dsink, sq)
    return _call(body, ins, name="pack_small", grid=(1,), prefetch=(me,),
                 in_specs=[pl.BlockSpec(a.shape, lambda i, me_ref: (0, 0)) for a in ins],
                 out_specs=[pl.BlockSpec((1, 8, P_W), lambda i, me_ref: (me_ref[0], 0, 0))],
                 out_shape=[SDS((N_DEV, 8, P_W), F32)], sem=("arbitrary",))[0]


N_SMALL = 11


def _small_adam(chip, p_all, g1_all, tbl_all, ws, ms, vs):
    fw_cols = 2 * DFF // N_CHIPS
    cw_cols = CW // N_CHIPS

    def body(chip_ref, p_ref, fw_ref, cw0_ref, cw1_ref, cw2_ref, g1_ref, tbl_ref, *refs):
        w_r, m_r, v_r = refs[0:N_SMALL], refs[N_SMALL:2 * N_SMALL], refs[2 * N_SMALL:3 * N_SMALL]
        outs = refs[3 * N_SMALL:]
        g_o, d_o, nm_o, nv_o = (outs[k * N_SMALL:(k + 1) * N_SMALL] for k in range(4))
        loss_o = outs[4 * N_SMALL]

        def total(ref):
            s = ref[0]
            for k in range(1, N_DEV):
                s = s + ref[k]
            return s

        S = total(p_ref)
        fw = total(fw_ref)
        cws = [total(r) for r in (cw0_ref, cw1_ref, cw2_ref)]

        def step(i, g, at):
            d, nm, nv = _adam_math(w_r[i][at], g, m_r[i][at], v_r[i][at])
            g_o[i][at], d_o[i][at], nm_o[i][at], nv_o[i][at] = g, d, nm, nv

        everything = (slice(None), slice(None))
        step(0, total(g1_ref), everything)
        for r in range(3):
            step(1, cws[r][4:5, :], (0, slice(r, r + 1), slice(None)))
        step(2, S[4:5, C_DQG:C_DQG + HD], everything)
        step(3, S[4:5, C_DKG:C_DKG + HD], everything)
        step(4, total(tbl_ref), everything)
        step(5, S[4:5, C_SINK:C_SINK + NH], everything)
        step(6, S[4:5, C_GCO:C_GCO + CW], everything)
        step(7, S[4:5, C_GAO:C_GAO + AW], everything)
        step(8, S[4:5, C_G2:C_G2 + D], everything)
        step(9, fw[0:3, :], (0, slice(None), slice(None)))
        step(10, S[3:4, 0:2 * DFF], everything)
        sq = S[:, C_SQ:C_SQ + 128]
        loss_o[...] = jnp.sum(jnp.sum(sq, axis=1, keepdims=True), axis=0, keepdims=True) * (0.5 / D)

    def full(a):
        n = len(a.shape)
        return pl.BlockSpec(a.shape, lambda i, chip_ref: (0,) * n)

    params = [*ws, *ms, *vs]
    out = _call(
        body, (p_all, p_all, p_all, p_all, p_all, g1_all, tbl_all, *params), name="small_adam", grid=(1,), prefetch=(chip,),
        in_specs=[full(p_all),
                  pl.BlockSpec((N_DEV, 8, fw_cols), lambda i, chip_ref: (0, 0, chip_ref[0])),
                  *[pl.BlockSpec((N_DEV, 8, cw_cols), lambda i, chip_ref, r=r: (0, 0, (C_DCW + r * CW) // cw_cols + chip_ref[0]))
                    for r in range(3)],
                  full(g1_all), full(tbl_all), *[full(a) for a in params]],
        out_specs=[full(a) for a in ws] * 4 + [pl.BlockSpec((1, 1), lambda i, chip_ref: (0, 0))],
        out_shape=[SDS(a.shape, F32) for a in ws] * 4 + [SDS((1, 1), F32)], sem=("arbitrary",), vmem_mib=32)
    return out[0:N_SMALL], out[N_SMALL:2 * N_SMALL], out[2 * N_SMALL:3 * N_SMALL], out[3 * N_SMALL:4 * N_SMALL], out[4 * N_SMALL]


PLACE_STEPS = 4


def _place_specs(shards):
    rows = [s.shape[0] // PLACE_STEPS for s in shards]
    return ([pl.BlockSpec((r, D), lambda i, chip_ref: (i, 0)) for r in rows],
            [pl.BlockSpec((r, D), lambda i, chip_ref: (chip_ref[0] * PLACE_STEPS + i, 0)) for r in rows],
            [SDS((N_CHIPS * s.shape[0], D), BF) for s in shards])


def _place_first(chip, shard, conv_w, ffn_conv_w):
    def body(chip_ref, a, s0, s1, o, t0, t1):
        o[...] = a[...].astype(BF)

        @pl.when(pl.program_id(0) == 0)
        def _():
            for s, t in ((s0, t0), (s1, t1)):
                t[...] = jnp.zeros_like(t)
                t[0, 0:3, :] = s[...]

    ins, outs, shapes = _place_specs([shard])
    taps = (conv_w, ffn_conv_w)
    return _call(
        body, (shard, conv_w, ffn_conv_w), name="place_first", grid=(PLACE_STEPS,), prefetch=(chip,),
        in_specs=ins + [pl.BlockSpec(s.shape, lambda i, chip_ref: (0, 0)) for s in taps],
        out_specs=outs + [pl.BlockSpec((1, 8, s.shape[1]), lambda i, chip_ref: (chip_ref[0], 0, 0)) for s in taps],
        out_shape=shapes + [SDS((N_CHIPS, 8, s.shape[1]), F32) for s in taps],
        sem=("arbitrary",), vmem_mib=32, free=(0, 1, 2))


def _place_rest(chip, shards, table, bucket, comm):
    n = len(shards)

    def body(chip_ref, *refs):
        a, (tab_ref, bk_ref), o, bias_ref = refs[:n], refs[n:n + 2], refs[n + 2:2 * n + 2], refs[2 * n + 2]
        for src, dst in zip(a, o):
            dst[...] = src[...].astype(BF)

        @pl.when(pl.program_id(0) == 0)
        def _():
            bk = bk_ref[...]
            eq = [bk == b for b in range(NBUCKET)]
            for h in range(NH):
                acc = jnp.zeros((BLK, 2 * BLK), F32)
                for b in range(NBUCKET):
                    acc = jnp.where(eq[b], tab_ref[b, h], acc)
                bias_ref[h * BLK:(h + 1) * BLK, :] = acc

    ins, outs, shapes = _place_specs(shards)
    return _call(
        body, (*shards, table, bucket), name="place_rest", grid=(PLACE_STEPS,), prefetch=(chip,),
        in_specs=ins + [pl.BlockSpec(memory_space=pltpu.SMEM), pl.BlockSpec(bucket.shape, lambda i, chip_ref: (0, 0))],
        out_specs=outs + [pl.BlockSpec((NH * BLK, 2 * BLK), lambda i, chip_ref: (0, 0))],
        out_shape=shapes + [SDS((NH * BLK, 2 * BLK), F32)],
        sem=("arbitrary",), vmem_mib=32, comm=comm, free=tuple(range(n + 2)))


def kernel(x, norm_mix_g, w_in, conv_w, q_norm_g, k_norm_g, rel_bias_table, sinks, out_norm_conv_g, out_norm_attn_g, w_out, norm_ffn_g, w_up, ffn_conv_w, ffn_conv_b, w_down, loss_target, m_norm_mix_g, m_w_in, m_conv_w, m_q_norm_g, m_k_norm_g, m_rel_bias_table, m_sinks, m_out_norm_conv_g, m_out_norm_attn_g, m_w_out, m_norm_ffn_g, m_w_up, m_ffn_conv_w, m_ffn_conv_b, m_w_down, v_norm_mix_g, v_w_in, v_conv_w, v_q_norm_g, v_k_norm_g, v_rel_bias_table, v_sinks, v_out_norm_conv_g, v_out_norm_attn_g, v_w_out, v_norm_ffn_g, v_w_up, v_ffn_conv_w, v_ffn_conv_b, v_w_down):
    as_arg = lambda i: jnp.reshape(i, (1,)).astype(jnp.int32)
    chip = as_arg(2 * lax.axis_index("x") + lax.axis_index("y"))
    core = as_arg(lax.axis_index("c"))
    me = 2 * chip + core
    xs, tgt = x[0], loss_target[0]
    qg, kg, gco, gao, g1, g2, fb = q_norm_g, k_norm_g, out_norm_conv_g, out_norm_attn_g, norm_mix_g, norm_ffn_g, ffn_conv_b
    pieces = lambda g: g.reshape(N_DEV, g.shape[0] // N_DEV, D)
    whole = lambda f: f.reshape(2 * f.shape[1], D)

    bucket = jnp.asarray(_bucket_table())
    p_in, p_cw, p_fw = _place_first(chip, w_in[0].T, conv_w[0], ffn_conv_w[0])
    p_out, p_up, p_down, bias, w_int, cw_all, fw_all = _place_rest(
        chip, [w_out[0], w_up[0].T, w_down[0]], rel_bias_table, bucket,
        comm=[_t_gather(p_in), _t_small_weights(p_cw), _t_small_weights(p_fw)])
    cw8 = jnp.transpose(cw_all, (1, 0, 2)).reshape(8, CW)
    fw8 = jnp.transpose(fw_all, (1, 0, 2)).reshape(8, 2 * DFF)

    proj, u1, w_out_f = _inproj(xs, g1, w_int, comm=[_t_gather(p_out)])
    y, w_upt = _mix_fwd(proj, sinks, cw8, qg, kg, gco, gao, bias, comm=[_t_gather(p_up)])
    h1, u2 = _outproj(y, w_out_f, xs, g2)
    up, w_down_f = _ffn_up(u2, w_upt, comm=[_t_gather(p_down)])
    a, = _ffn_act(up, fw8, fb)
    dh2, dh2b, sq = _ffn_down(a, w_down_f, h1, tgt)

    gdbf, = _wgrad("wgrad_down", [a], dh2b)
    da, sib_down = _ffn_down_bwd(dh2b, w_down_f, comm=[_t_sibling(pieces(gdbf))])
    pbf_down, own_down = _chip_sum("chip_sum_w_down", pieces(gdbf), sib_down, core, chip)
    dug, duv, dfwg, dfwv, dfbg, dfbv, chips_down = _ffn_act_bwd(up, da, fw8, fb, comm=[_t_chips(pbf_down)])
    fin_down = _final_sum("final_sum_w_down", own_down, chips_down, core)
    gubf, = _wgrad("wgrad_up", [dug, duv], u2)
    dh1, dh1b, dg2, sib_up, fin_down = _norm_matmul_bwd(
        "ffn_up_bwd", [dug, duv], w_upt, [0, DFF], h1, g2, dh2, True, comm=[_t_sibling(pieces(gubf)), _t_swap(fin_down)])
    pbf_up, own_up = _chip_sum("chip_sum_w_up", pieces(gubf), sib_up, core, chip)
    gobf, = _wgrad("wgrad_out", [y], dh1b)
    dy, = _out_bwd(dh1b, w_out_f)
    dproj, dcw8, dqg, dkg, dgco, dgao, dsink, dbias, chips_up, sib_out = _mix_bwd(
        proj, dy, sinks, cw8, qg, kg, gco, gao, bias, comm=[_t_chips(pbf_up), _t_sibling(pieces(gobf))])
    fin_up = _final_sum("final_sum_w_up", own_up, chips_up, core)
    pbf_out, own_out = _chip_sum("chip_sum_w_out", pieces(gobf), sib_out, core, chip)
    tbl_all = _band_bias_bwd(dbias, bucket, me)
    p_all = _pack_small(me, dfwg, dfwv, dfbg, dfbv, dg2, dgco, dgao, dcw8, dqg, dkg, dsink, sq)
    gibf, chips_out, fin_up, p_all, tbl_all = _wgrad(
        "wgrad_in", [dproj], u1, comm=[_t_chips(pbf_out), _t_swap(fin_up), _t_allgather(p_all), _t_allgather(tbl_all)])
    fin_out = _final_sum("final_sum_w_out", own_out, chips_out, core)
    sib_in, fin_out = _comm_call("to_sibling_last", [_t_sibling(pieces(gibf)), _t_swap(fin_out)])
    pbf_in, own_in = _chip_sum("chip_sum_w_in", pieces(gibf), sib_in, core, chip)
    dx, g1_all, chips_in = _norm_matmul_bwd(
        "in_bwd", [dproj], w_int, [0], xs, g1, dh1, False, comm=[_t_chips(pbf_in)], slot=me)
    fin_in = _final_sum("final_sum_w_in", own_in, chips_in, core)
    g1_all, fin_in = _comm_call("gather_last", [_t_allgather(g1_all), _t_swap(fin_in)])

    g_w_in, g_w_out, g_w_up, g_w_down = whole(fin_in).T, whole(fin_out), whole(fin_up).T, whole(fin_down)
    g_w_down, d_down, nm_down, nv_down = _adamw("adamw_w_down", w_down[0], g_w_down, m_w_down[0], v_w_down[0], 352, True)
    d_up, nm_up, nv_up = _adamw("adamw_w_up", w_up[0], g_w_up, m_w_up[0], v_w_up[0], 256)
    g_w_out, d_out, nm_out, nv_out = _adamw("adamw_w_out", w_out[0], g_w_out, m_w_out[0], v_w_out[0], 256, True)
    d_in, nm_in, nv_in = _adamw("adamw_w_in", w_in[0], g_w_in, m_w_in[0], v_w_in[0], 256)
    sw = [norm_mix_g, conv_w, q_norm_g, k_norm_g, rel_bias_table, sinks, out_norm_conv_g, out_norm_attn_g,
          norm_ffn_g, ffn_conv_w, ffn_conv_b]
    smm = [m_norm_mix_g, m_conv_w, m_q_norm_g, m_k_norm_g, m_rel_bias_table, m_sinks, m_out_norm_conv_g,
           m_out_norm_attn_g, m_norm_ffn_g, m_ffn_conv_w, m_ffn_conv_b]
    smv = [v_norm_mix_g, v_conv_w, v_q_norm_g, v_k_norm_g, v_rel_bias_table, v_sinks, v_out_norm_conv_g,
           v_out_norm_attn_g, v_norm_ffn_g, v_ffn_conv_w, v_ffn_conv_b]
    sg, sd, snm, snv, loss = _small_adam(chip, p_all, g1_all, tbl_all, sw, smm, smv)

    def order(s, b_in, b_out, b_up, b_down):
        return (s[0], b_in[None], s[1], s[2], s[3], s[4], s[5], s[6], s[7], b_out[None], s[8], b_up[None],
                s[9], s[10], b_down[None])

    return (loss.reshape(()), dx[None],
            *order(sg, g_w_in, g_w_out, g_w_up, g_w_down),
            *order(sd, d_in, d_out, d_up, d_down),
            *order(snm, nm_in, nm_out, nm_up, nm_down),
            *order(snv, nv_in, nv_out, nv_up, nv_down))
```

```python
import functools
import math

import numpy as np

import jax
import jax.numpy as jnp
from jax import lax
from jax.experimental import pallas as pl
from jax.experimental.pallas import tpu as pltpu

F32 = jnp.float32
BF = jnp.bfloat16
SDS = jax.ShapeDtypeStruct

T = 2048
D = 1024
CW = 512
AW = 512
HD = 64
NH = 8
NKV = 2
GQ = 4
INW = 2304
DFF = 2816
BLK = 128
NB = T // BLK
NBUCKET = 32
EPS = 1e-6
NEG_INF = -1e30
N_CHIPS = 4
N_DEV = 8

ADAM_LR = 0.001
ADAM_B1 = 0.9
ADAM_B2 = 0.999
ADAM_EPS = 1e-08
ADAM_WD = 0.01
ADAM_STEP = 10

MIB = 1024 * 1024
MESH = pl.DeviceIdType.MESH
ANY = pl.BlockSpec(memory_space=pl.ANY)

_pcall = pl.pallas_call


def _params(sem=None, vmem_mib=None):
    kw = {}
    if sem is not None:
        kw["dimension_semantics"] = sem
    if vmem_mib is not None:
        kw["vmem_limit_bytes"] = vmem_mib * MIB
    return pltpu.CompilerParams(**kw)


def _dot(a, b, ca, cb):
    return lax.dot_general(a, b, (((ca,), (cb,)), ((), ())), preferred_element_type=F32)


def _rms_bwd(dy, x, r, g):
    dg = jnp.sum(dy * (x * r), axis=0, keepdims=True)
    dgx = dy * g
    dx = r * dgx - x * (r * r * r) * jnp.mean(x * dgx, axis=-1, keepdims=True)
    return dx, dg


def _where():
    x, y, c = lax.axis_index("x"), lax.axis_index("y"), lax.axis_index("c")
    return x, y, c, [(1 - x, y), (x, 1 - y), (1 - x, 1 - y)]


def _rcopy(src, dst, ssem, rsem, dev):
    return pltpu.make_async_remote_copy(src_ref=src, dst_ref=dst, send_sem=ssem, recv_sem=rsem, device_id=dev,
                                        device_id_type=MESH)


class _Task:
    def __init__(self, ins, outs, alias, n_sem, start, finish):
        self.ins, self.outs, self.alias, self.n_sem, self.start, self.finish = ins, outs, alias, n_sem, start, finish


def _t_gather(placed):
    R = placed.shape[0] // N_CHIPS

    def rows(chip_index, core):
        return pl.ds(pl.multiple_of(chip_index * R + core * (R // 2), 16), R // 2)

    def start(cin, cout, ss, rs, b):
        x, y, c, chips = _where()
        mine = cout[0].at[rows(2 * x + y, c)]
        for r, (px, py) in enumerate(chips):
            _rcopy(mine, mine, ss.at[b + r], rs.at[b + r], (px, py, c)).start()

    def finish(cin, cout, ss, rs, b):
        x, y, c, chips = _where()
        buf = cout[0]
        sib = (x, y, 1 - c)
        for r, (px, py) in enumerate(chips):
            got = buf.at[rows(2 * px + py, c)]
            _rcopy(got, got, ss.at[b + r], rs.at[b + r], (px, py, c)).wait_recv()
            _rcopy(got, got, ss.at[b + 3 + r], rs.at[b + 3 + r], sib).start()
        for r, (px, py) in enumerate(chips):
            got = buf.at[rows(2 * px + py, 1 - c)]
            _rcopy(got, got, ss.at[b + 3 + r], rs.at[b + 3 + r], sib).wait_recv()
        mine = buf.at[rows(2 * x + y, c)]
        for r in range(6):
            _rcopy(mine, mine, ss.at[b + r], rs.at[b + r], sib).wait_send()

    return _Task([placed], [SDS(placed.shape, placed.dtype)], [(0, 0)], 6, start, finish)


def _t_small_weights(buf):
    def start(cin, cout, ss, rs, b):
        x, y, c, chips = _where()
        mine = cout[0].at[2 * x + y]
        for r, (px, py) in enumerate(chips):
            _rcopy(mine, mine, ss.at[b + r], rs.at[b + r], (px, py, c)).start()

    def finish(cin, cout, ss, rs, b):
        x, y, c, chips = _where()
        for r, (px, py) in enumerate(chips):
            got = cout[0].at[2 * px + py]
            _rcopy(got, got, ss.at[b + r], rs.at[b + r], (px, py, c)).wait_recv()
        for r, (px, py) in enumerate(chips):
            mine = cout[0].at[2 * x + y]
            _rcopy(mine, mine, ss.at[b + r], rs.at[b + r], (px, py, c)).wait_send()

    return _Task([buf], [SDS(buf.shape, buf.dtype)], [(0, 0)], 3, start, finish)


def _t_sibling(gbf):
    def start(cin, cout, ss, rs, b):
        x, y, c, _ = _where()
        for jj in range(N_CHIPS):
            _rcopy(cin[0].at[2 * jj + (1 - c)], cout[0].at[jj], ss.at[b + jj], rs.at[b + jj], (x, y, 1 - c)).start()

    def finish(cin, cout, ss, rs, b):
        x, y, c, _ = _where()
        for jj in range(N_CHIPS):
            got = cout[0].at[jj]
            _rcopy(got, got, ss.at[b + jj], rs.at[b + jj], (x, y, 1 - c)).wait_recv()
        for jj in range(N_CHIPS):
            got = cout[0].at[jj]
            _rcopy(got, got, ss.at[b + jj], rs.at[b + jj], (x, y, 1 - c)).wait_send()

    return _Task([gbf], [SDS((N_CHIPS,) + gbf.shape[1:], BF)], [], N_CHIPS, start, finish)


def _t_chips(pbf):
    def start(cin, cout, ss, rs, b):
        x, y, c, chips = _where()
        for r, (px, py) in enumerate(chips):
            _rcopy(cin[0].at[2 * px + py], cout[0].at[r], ss.at[b + r], rs.at[b + r], (px, py, c)).start()

    def finish(cin, cout, ss, rs, b):
        x, y, c, chips = _where()
        for r, (px, py) in enumerate(chips):
            got = cout[0].at[r]
            _rcopy(got, got, ss.at[b + r], rs.at[b + r], (px, py, c)).wait_recv()
        for r, (px, py) in enumerate(chips):
            got = cout[0].at[r]
            _rcopy(got, got, ss.at[b + r], rs.at[b + r], (px, py, c)).wait_send()

    return _Task([pbf], [SDS((3,) + pbf.shape[1:], BF)], [], 3, start, finish)


def _t_swap(fin):
    def start(cin, cout, ss, rs, b):
        x, y, c, _ = _where()
        mine = cout[0].at[c]
        _rcopy(mine, mine, ss.at[b], rs.at[b], (x, y, 1 - c)).start()

    def finish(cin, cout, ss, rs, b):
        x, y, c, _ = _where()
        got = cout[0].at[1 - c]
        _rcopy(got, got, ss.at[b], rs.at[b], (x, y, 1 - c)).wait_recv()
        _rcopy(got, got, ss.at[b], rs.at[b], (x, y, 1 - c)).wait_send()

    return _Task([fin], [SDS(fin.shape, fin.dtype)], [(0, 0)], 1, start, finish)


def _t_allgather(buf):
    def peers():
        x, y, c, _ = _where()
        out = []
        for rel in range(1, N_DEV):
            px, py, pc = x ^ ((rel >> 2) & 1), y ^ ((rel >> 1) & 1), c ^ (rel & 1)
            out.append((rel - 1, 4 * px + 2 * py + pc, (px, py, pc)))
        return 4 * x + 2 * y + c, out

    def start(cin, cout, ss, rs, b):
        me, ps = peers()
        mine = cout[0].at[me]
        for k, _, dev in ps:
            _rcopy(mine, mine, ss.at[b + k], rs.at[b + k], dev).start()

    def finish(cin, cout, ss, rs, b):
        me, ps = peers()
        for k, pidx, dev in ps:
            got = cout[0].at[pidx]
            _rcopy(got, got, ss.at[b + k], rs.at[b + k], dev).wait_recv()
        for k, _, dev in ps:
            mine = cout[0].at[me]
            _rcopy(mine, mine, ss.at[b + k], rs.at[b + k], dev).wait_send()

    return _Task([buf], [SDS(buf.shape, buf.dtype)], [(0, 0)], N_DEV - 1, start, finish)


def _run_tasks(comm, which, cin, cout, ss, rs):
    i0 = o0 = s0 = 0
    for t in comm:
        getattr(t, which)(cin[i0:i0 + len(t.ins)], cout[o0:o0 + len(t.outs)], ss, rs, s0)
        i0, o0, s0 = i0 + len(t.ins), o0 + len(t.outs), s0 + t.n_sem


def _from_hbm(*arrays):
    return [pltpu.with_memory_space_constraint(a, pltpu.HBM) for a in arrays]


def _in_hbm(shapes):
    return [pltpu.HBM(s.shape, s.dtype) for s in shapes]


def _comm_layout(comm, n_in, n_out):
    c_in = [a for t in comm for a in t.ins]
    c_out = [s for t in comm for s in t.outs]
    aliases, i0, o0 = {}, 0, 0
    for t in comm:
        for i, o in t.alias:
            aliases[n_in + i0 + i] = n_out + o0 + o
        i0, o0 = i0 + len(t.ins), o0 + len(t.outs)
    return c_in, c_out, aliases, sum(t.n_sem for t in comm)


def _call(body, operands, *, name, grid, in_specs, out_specs, out_shape, scratch_shapes=(), sem=None, vmem_mib=None, comm=(),
          free=(), prefetch=()):
    operands = [o if s.memory_space == pltpu.SMEM or k in free else pltpu.with_memory_space_constraint(o, pltpu.HBM)
                for k, (o, s) in enumerate(zip(operands, in_specs))]
    n_pre, n_in, n_out, n_scr = len(prefetch), len(in_specs), len(out_specs), len(scratch_shapes)
    c_in, c_out, aliases, n_sem = _comm_layout(comm, n_pre + n_in, n_out)
    sems = [pltpu.SemaphoreType.DMA((n_sem,)), pltpu.SemaphoreType.DMA((n_sem,))] if comm else []

    def wrapped(*refs):
        pre, refs = refs[:n_pre], refs[n_pre:]
        ins, cin = refs[:n_in], refs[n_in:n_in + len(c_in)]
        rest = refs[n_in + len(c_in):]
        outs, cout = rest[:n_out], rest[n_out:n_out + len(c_out)]
        rest = rest[n_out + len(c_out):]
        scr, csem = rest[:n_scr], rest[n_scr:]
        if not comm:
            return body(*pre, *ins, *outs, *scr)
        ids = [pl.program_id(k) for k in range(len(grid))]
        first = functools.reduce(jnp.logical_and, [i == 0 for i in ids])
        last = functools.reduce(jnp.logical_and, [i == n - 1 for i, n in zip(ids, grid)])
        pl.when(first)(lambda: _run_tasks(comm, "start", cin, cout, *csem))
        body(*pre, *ins, *outs, *scr)
        pl.when(last)(lambda: _run_tasks(comm, "finish", cin, cout, *csem))

    grid_spec = pltpu.PrefetchScalarGridSpec(
        num_scalar_prefetch=n_pre, grid=grid, in_specs=list(in_specs) + [ANY] * len(c_in),
        out_specs=list(out_specs) + [ANY] * len(c_out), scratch_shapes=list(scratch_shapes) + sems)
    return _pcall(
        wrapped, name=name, grid_spec=grid_spec, out_shape=_in_hbm(list(out_shape) + c_out), input_output_aliases=aliases,
        compiler_params=_params(("arbitrary",) * len(grid) if comm else sem, vmem_mib),
    )(*prefetch, *operands, *_from_hbm(*c_in))


def _comm_call(name, comm):
    c_in, c_out, aliases, n_sem = _comm_layout(comm, 0, 0)

    def body(*refs):
        cin, cout, (ss, rs) = refs[:len(c_in)], refs[len(c_in):len(c_in) + len(c_out)], refs[len(c_in) + len(c_out):]
        _run_tasks(comm, "start", cin, cout, ss, rs)
        _run_tasks(comm, "finish", cin, cout, ss, rs)

    return _pcall(
        body, name=name, in_specs=[ANY] * len(c_in), out_specs=[ANY] * len(c_out), out_shape=_in_hbm(c_out),
        scratch_shapes=[pltpu.SemaphoreType.DMA((n_sem,)), pltpu.SemaphoreType.DMA((n_sem,))],
        input_output_aliases=aliases,
    )(*_from_hbm(*c_in))


def _inproj(x, g1, w_int, comm=()):
    tm = 256

    def body(x_ref, g_ref, w_ref, proj_ref, u_ref):
        xf = x_ref[...]
        r = lax.rsqrt(jnp.mean(xf * xf, axis=-1, keepdims=True) + EPS)
        u = (xf * r * g_ref[...]).astype(BF)
        u_ref[...] = u
        proj_ref[...] = _dot(u, w_ref[...], 1, 1)

    return _call(
        body, (x, g1, w_int), name="inproj", grid=(T // tm,),
        in_specs=[pl.BlockSpec((tm, D), lambda i: (i, 0)), pl.BlockSpec((1, D), lambda i: (0, 0)),
                  pl.BlockSpec((INW, D), lambda i: (0, 0))],
        out_specs=[pl.BlockSpec((tm, INW), lambda i: (i, 0)), pl.BlockSpec((tm, D), lambda i: (i, 0))],
        out_shape=[SDS((T, INW), F32), SDS((T, D), BF)], sem=("parallel",), vmem_mib=40, comm=comm)


def _outproj(y, w_out, x, g2):
    tm = 256

    def body(y_ref, w_ref, x_ref, g_ref, h1_ref, u2_ref):
        h1 = x_ref[...] + _dot(y_ref[...], w_ref[...], 1, 0)
        h1_ref[...] = h1
        r = lax.rsqrt(jnp.mean(h1 * h1, axis=-1, keepdims=True) + EPS)
        u2_ref[...] = (h1 * r * g_ref[...]).astype(BF)

    return _call(
        body, (y, w_out, x, g2), name="outproj", grid=(T // tm,),
        in_specs=[pl.BlockSpec((tm, D), lambda i: (i, 0)), pl.BlockSpec((D, D), lambda i: (0, 0)),
                  pl.BlockSpec((tm, D), lambda i: (i, 0)), pl.BlockSpec((1, D), lambda i: (0, 0))],
        out_specs=[pl.BlockSpec((tm, D), lambda i: (i, 0)), pl.BlockSpec((tm, D), lambda i: (i, 0))],
        out_shape=[SDS((T, D), F32), SDS((T, D), BF)], sem=("parallel",), vmem_mib=32)


def _ffn_up(u2, w_upt, comm=()):
    tm, tn = 1024, 512

    def body(u_ref, w_ref, o_ref):
        o_ref[...] = _dot(u_ref[...], w_ref[...], 1, 1).astype(BF)

    return _call(
        body, (u2, w_upt), name="ffn_up", grid=(T // tm, 2 * DFF // tn),
        in_specs=[pl.BlockSpec((tm, D), lambda i, j: (i, 0)), pl.BlockSpec((tn, D), lambda i, j: (j, 0))],
        out_specs=[pl.BlockSpec((tm, tn), lambda i, j: (i, j))], out_shape=[SDS((T, 2 * DFF), BF)],
        sem=("parallel", "parallel"), vmem_mib=32, comm=comm)


def _ffn_down(a, w_down, h1, tgt):
    tm = 256

    def body(a_ref, w_ref, h1_ref, t_ref, dh_ref, dhb_ref, l_ref):
        @pl.when(pl.program_id(0) == 0)
        def _():
            l_ref[...] = jnp.zeros_like(l_ref)

        h2 = h1_ref[...] + _dot(a_ref[...], w_ref[...], 1, 0)
        e = h2 - t_ref[...]
        dh = e * (1.0 / D)
        dh_ref[...] = dh
        dhb_ref[...] = dh.astype(BF)
        e2 = jnp.sum((e * e).reshape(tm // 8, 8, D), axis=0)
        acc = e2[:, 0:128]
        for k in range(1, D // 128):
            acc = acc + e2[:, k * 128:(k + 1) * 128]
        l_ref[...] += acc

    return _call(
        body, (a, w_down, h1, tgt), name="ffn_down", grid=(T // tm,),
        in_specs=[pl.BlockSpec((tm, DFF), lambda i: (i, 0)), pl.BlockSpec((DFF, D), lambda i: (0, 0)),
                  pl.BlockSpec((tm, D), lambda i: (i, 0)), pl.BlockSpec((tm, D), lambda i: (i, 0))],
        out_specs=[pl.BlockSpec((tm, D), lambda i: (i, 0)), pl.BlockSpec((tm, D), lambda i: (i, 0)),
                   pl.BlockSpec((8, 128), lambda i: (0, 0))],
        out_shape=[SDS((T, D), F32), SDS((T, D), BF), SDS((8, 128), F32)], sem=("arbitrary",), vmem_mib=40)


def _bucket_table():
    q = np.arange(BLK, dtype=np.int32)[:, None]
    j = np.arange(2 * BLK, dtype=np.int32)[None, :]
    n = np.maximum(q + BLK - j, 0)
    nf = np.maximum(n, 1).astype(np.float32)
    max_exact = NBUCKET // 2
    large = max_exact + (np.log(nf / np.float32(max_exact)) / np.float32(math.log(BLK / max_exact))
                         * np.float32(NBUCKET - max_exact)).astype(np.int32)
    large = np.minimum(large, NBUCKET - 1)
    return np.where(n < max_exact, n, large).astype(np.int32)


def _band_bias_bwd(dbias, bucket, me):
    def body(me_ref, db_ref, bk_ref, o_ref):
        bk = bk_ref[...]
        for b in range(NBUCKET):
            m = bk == b
            for h in range(NH):
                v = jnp.where(m, db_ref[h * BLK:(h + 1) * BLK, :], 0.0)
                s = jnp.sum(jnp.sum(v, axis=1, keepdims=True), axis=0, keepdims=True)
                o_ref[0, h:h + 1, b:b + 1] = s

    grid_spec = pltpu.PrefetchScalarGridSpec(
        num_scalar_prefetch=1, grid=(1,),
        in_specs=[pl.BlockSpec((NH * BLK, 2 * BLK), lambda i, me_ref: (0, 0)),
                  pl.BlockSpec((BLK, 2 * BLK), lambda i, me_ref: (0, 0))],
        out_specs=pl.BlockSpec((1, NH, NBUCKET), lambda i, me_ref: (me_ref[0], 0, 0)),
    )
    return _pcall(body, name="band_bias_bwd", grid_spec=grid_spec, out_shape=SDS((N_DEV, NH, NBUCKET), F32),
                  compiler_params=_params(("arbitrary",)))(me, dbias, bucket)


def _mix_forward(P, zc8, zh8, pkv, first, cw, qg, kg, gco, gao, sink_ref, bias_ref):
    gate_b = P[:, 0:CW]
    gate_c = P[:, CW:2 * CW]
    hc = P[:, 2 * CW:3 * CW]
    z = gate_c * hc
    keep = jnp.where(first, 0.0, 1.0)
    zp = zc8 * zh8 * keep
    p1 = zp[7:8, :]
    p2 = zp[6:7, :]
    row = lax.broadcasted_iota(jnp.int32, (BLK, 1), 0)
    z1 = jnp.where(row == 0, p1, pltpu.roll(z, 1, 0))
    z2 = jnp.where(row == 0, p2, jnp.where(row == 1, p1, pltpu.roll(z, 2, 0)))
    cz = cw[0:1, :] * z2 + cw[1:2, :] * z1 + cw[2:3, :] * z
    y_conv = gate_b * cz

    scale = HD ** -0.5
    qi = lax.broadcasted_iota(jnp.int32, (GQ * BLK, 2 * BLK), 0) & (BLK - 1)
    kj = lax.broadcasted_iota(jnp.int32, (GQ * BLK, 2 * BLK), 1)
    dd = qi + BLK - kj
    first_key = jnp.where(first, BLK, 0)
    valid = (dd >= 0) & (dd < BLK) & (kj >= first_key)

    q0 = 3 * CW
    k0 = q0 + AW
    v0 = k0 + NKV * HD
    heads = []
    outs = []
    for kv in range(NKV):
        kb_raw = jnp.concatenate([pkv[:, kv * HD:(kv + 1) * HD], P[:, k0 + kv * HD:k0 + (kv + 1) * HD]], axis=0)
        rk = lax.rsqrt(jnp.mean(kb_raw * kb_raw, axis=-1, keepdims=True) + EPS)
        kb = (kb_raw * rk * kg).astype(BF)
        vb = jnp.concatenate([pkv[:, NKV * HD + kv * HD:NKV * HD + (kv + 1) * HD],
                              P[:, v0 + kv * HD:v0 + (kv + 1) * HD]], axis=0).astype(BF)
        q_raw, rq, qn = [], [], []
        for g in range(GQ):
            h = kv * GQ + g
            qh = P[:, q0 + h * HD:q0 + (h + 1) * HD]
            r = lax.rsqrt(jnp.mean(qh * qh, axis=-1, keepdims=True) + EPS)
            q_raw.append(qh)
            rq.append(r)
            qn.append(qh * r * qg)
        Q = jnp.concatenate(qn, axis=0).astype(BF)
        S = _dot(Q, kb, 1, 1) * scale + bias_ref[kv * GQ * BLK:(kv + 1) * GQ * BLK, :]
        S = jnp.where(valid, S, NEG_INF)
        sink = jnp.concatenate([jnp.full((BLK, 1), sink_ref[0, kv * GQ + g], F32) for g in range(GQ)], axis=0)
        m = jnp.maximum(jnp.max(S, axis=-1, keepdims=True), sink)
        p = jnp.exp(S - m)
        es = jnp.exp(sink - m)
        denom = jnp.sum(p, axis=-1, keepdims=True) + es
        probs = p / denom
        O = _dot(probs.astype(BF), vb, 1, 0)
        heads.append(dict(kb_raw=kb_raw, rk=rk, kb=kb, vb=vb, q_raw=q_raw, rq=rq, Q=Q, probs=probs,
                          psink=es / denom, O=O))
        outs += [O[g * BLK:(g + 1) * BLK, :] for g in range(GQ)]
    y_attn = jnp.concatenate(outs, axis=1)

    rc = lax.rsqrt(jnp.mean(y_conv * y_conv, axis=-1, keepdims=True) + EPS)
    ra = lax.rsqrt(jnp.mean(y_attn * y_attn, axis=-1, keepdims=True) + EPS)
    y = jnp.concatenate([y_conv * rc * gco, y_attn * ra * gao], axis=1)
    return dict(gate_b=gate_b, gate_c=gate_c, hc=hc, z=z, z1=z1, z2=z2, cz=cz, y_conv=y_conv, y_attn=y_attn,
                rc=rc, ra=ra, heads=heads, y=y, row=row, scale=scale)


BPS = 2
TILE = BPS * BLK
KV0 = 3 * CW + AW


def _mix_in_specs(tile_of):
    return [
        pl.BlockSpec(memory_space=pltpu.SMEM),
        pl.BlockSpec((TILE, INW), lambda s: (tile_of(s), 0)),
        pl.BlockSpec((8, CW), lambda s: (jnp.maximum(tile_of(s) * (TILE // 8) - 1, 0), 1)),
        pl.BlockSpec((8, CW), lambda s: (jnp.maximum(tile_of(s) * (TILE // 8) - 1, 0), 2)),
        pl.BlockSpec((BLK, 2 * NKV * HD), lambda s: (jnp.maximum(tile_of(s) * BPS - 1, 0), KV0 // (2 * NKV * HD))),
    ]


def _block_inputs(tile, b, zc_ref, zh_ref, pkv_ref, first_tile):
    P = tile[b * BLK:(b + 1) * BLK, :]
    if b == 0:
        return P, zc_ref[...], zh_ref[...], pkv_ref[...], first_tile
    lo = b * BLK
    return P, tile[lo - 8:lo, CW:2 * CW], tile[lo - 8:lo, 2 * CW:3 * CW], tile[lo - BLK:lo, KV0:KV0 + 2 * NKV * HD], False


def _mix_param_specs():
    return [
        pl.BlockSpec((8, CW), lambda s: (0, 0)),
        pl.BlockSpec((1, HD), lambda s: (0, 0)),
        pl.BlockSpec((1, HD), lambda s: (0, 0)),
        pl.BlockSpec((1, CW), lambda s: (0, 0)),
        pl.BlockSpec((1, AW), lambda s: (0, 0)),
        pl.BlockSpec((NH * BLK, 2 * BLK), lambda s: (0, 0)),
    ]


def _mix_fwd(proj, sinks, cw8, qg, kg, gco, gao, bias, comm=()):
    def body(sink_ref, p_ref, zc_ref, zh_ref, pkv_ref, cw_ref, qg_ref, kg_ref, gco_ref, gao_ref, bias_ref, y_ref):
        tile = p_ref[...]
        for b in range(BPS):
            f = _mix_forward(*_block_inputs(tile, b, zc_ref, zh_ref, pkv_ref, pl.program_id(0) == 0), cw_ref[...],
                             qg_ref[...], kg_ref[...], gco_ref[...], gao_ref[...], sink_ref, bias_ref)
            y_ref[b * BLK:(b + 1) * BLK, :] = f["y"].astype(BF)

    return _call(
        body, (sinks, proj, proj, proj, proj, cw8, qg, kg, gco, gao, bias), name="mix_fwd", grid=(T // TILE,),
        in_specs=_mix_in_specs(lambda s: s) + _mix_param_specs(),
        out_specs=[pl.BlockSpec((TILE, D), lambda s: (s, 0))], out_shape=[SDS((T, D), BF)],
        sem=("parallel",), vmem_mib=40, comm=comm)


def _mix_bwd(proj, dy, sinks, cw8, qg, kg, gco, gao, bias, comm=()):
    n_steps = T // TILE

    def tile_of(s):
        return n_steps - 1 - s

    def body(sink_ref, p_ref, zc_ref, zh_ref, pkv_ref, dy_ref, cw_ref, qg_ref, kg_ref, gco_ref, gao_ref, bias_ref,
             dproj_ref, dcw_ref, dqg_ref, dkg_ref, dgco_ref, dgao_ref, dsink_ref, dbias_ref,
             ndcz_ref, dkc_ref, dvc_ref):
        s = pl.program_id(0)

        @pl.when(s == 0)
        def _():
            for r in (dcw_ref, dqg_ref, dkg_ref, dgco_ref, dgao_ref, dsink_ref, dbias_ref, ndcz_ref, dkc_ref, dvc_ref):
                r[...] = jnp.zeros_like(r)

        tile = p_ref[...]
        for b in reversed(range(BPS)):
            one_block(b, _block_inputs(tile, b, zc_ref, zh_ref, pkv_ref, s == n_steps - 1),
                      dy_ref[b * BLK:(b + 1) * BLK, :], sink_ref, cw_ref, qg_ref, kg_ref, gco_ref, gao_ref, bias_ref,
                      dproj_ref.at[b * BLK:(b + 1) * BLK, :], dcw_ref, dqg_ref, dkg_ref, dgco_ref, dgao_ref, dsink_ref,
                      dbias_ref, ndcz_ref, dkc_ref, dvc_ref)

    def one_block(b, inputs, dy, sink_ref, cw_ref, qg_ref, kg_ref, gco_ref, gao_ref, bias_ref,
                  dproj_ref, dcw_ref, dqg_ref, dkg_ref, dgco_ref, dgao_ref, dsink_ref, dbias_ref,
                  ndcz_ref, dkc_ref, dvc_ref):
        cw = cw_ref[...]
        qg_v, kg_v, gco_v, gao_v = qg_ref[...], kg_ref[...], gco_ref[...], gao_ref[...]
        f = _mix_forward(*inputs, cw, qg_v, kg_v, gco_v, gao_v, sink_ref, bias_ref)
        dyc, dgco = _rms_bwd(dy[:, 0:CW], f["y_conv"], f["rc"], gco_v)
        dya, dgao = _rms_bwd(dy[:, CW:CW + AW], f["y_attn"], f["ra"], gao_v)
        dgco_ref[...] += dgco
        dgao_ref[...] += dgao

        row = f["row"]
        dgate_b = dyc * f["cz"]
        dcz = dyc * f["gate_b"]
        dcw_ref[0:1, :] += jnp.sum(dcz * f["z2"], axis=0, keepdims=True)
        dcw_ref[1:2, :] += jnp.sum(dcz * f["z1"], axis=0, keepdims=True)
        dcw_ref[2:3, :] += jnp.sum(dcz * f["z"], axis=0, keepdims=True)
        nxt = ndcz_ref[...]
        n0 = nxt[0:1, :]
        n1 = nxt[1:2, :]
        d1 = jnp.where(row == BLK - 1, n0, pltpu.roll(dcz, BLK - 1, 0))
        d2 = jnp.where(row == BLK - 1, n1, jnp.where(row == BLK - 2, n0, pltpu.roll(dcz, BLK - 2, 0)))
        dz = cw[2:3, :] * dcz + cw[1:2, :] * d1 + cw[0:1, :] * d2
        ndcz_ref[...] = dcz[0:8, :]
        dproj_ref[:, 0:CW] = dgate_b.astype(BF)
        dproj_ref[:, CW:2 * CW] = (dz * f["hc"]).astype(BF)
        dproj_ref[:, 2 * CW:3 * CW] = (dz * f["gate_c"]).astype(BF)

        scale = f["scale"]
        lane = lax.broadcasted_iota(jnp.int32, (1, 128), 1)
        dq_cols, dk_cols, dv_cols = [], [], []
        for kv in range(NKV):
            hd = f["heads"][kv]
            dO = jnp.concatenate([dya[:, (kv * GQ + g) * HD:(kv * GQ + g + 1) * HD] for g in range(GQ)], axis=0)
            delta = jnp.sum(dO * hd["O"], axis=-1, keepdims=True)
            dOb = dO.astype(BF)
            dP = _dot(dOb, hd["vb"], 1, 1)
            dS = hd["probs"] * (dP - delta)
            dsk = hd["psink"] * delta
            for g in range(GQ):
                h = kv * GQ + g
                tot = jnp.sum(dsk[g * BLK:(g + 1) * BLK, :], axis=0, keepdims=True)
                dsink_ref[...] -= jnp.where(lane == h, tot, 0.0)
            dbias_ref[kv * GQ * BLK:(kv + 1) * GQ * BLK, :] += dS
            dSs = (dS * scale).astype(BF)
            dQ = _dot(dSs, hd["kb"], 1, 0)
            dKb = _dot(dSs, hd["Q"], 0, 0)
            dVb = _dot(hd["probs"].astype(BF), dOb, 0, 0)
            dkn = dKb[BLK:, :] + dkc_ref[:, kv * HD:(kv + 1) * HD]
            dvn = dVb[BLK:, :] + dvc_ref[:, kv * HD:(kv + 1) * HD]
            dkc_ref[:, kv * HD:(kv + 1) * HD] = dKb[:BLK, :]
            dvc_ref[:, kv * HD:(kv + 1) * HD] = dVb[:BLK, :]
            dk_raw, dkg = _rms_bwd(dkn, hd["kb_raw"][BLK:, :], hd["rk"][BLK:, :], kg_v)
            dkg_ref[...] += dkg
            dk_cols.append(dk_raw)
            dv_cols.append(dvn)
            for g in range(GQ):
                dq_raw, dqg = _rms_bwd(dQ[g * BLK:(g + 1) * BLK, :], hd["q_raw"][g], hd["rq"][g], qg_v)
                dqg_ref[...] += dqg
                dq_cols.append(dq_raw)
        dproj_ref[:, 3 * CW:INW] = jnp.concatenate(dq_cols + dk_cols + dv_cols, axis=1).astype(BF)

    small = lambda r, c: pl.BlockSpec((r, c), lambda s: (0, 0))
    return _call(
        body, (sinks, proj, proj, proj, proj, dy, cw8, qg, kg, gco, gao, bias), name="mix_bwd", grid=(n_steps,),
        in_specs=_mix_in_specs(tile_of) + [pl.BlockSpec((TILE, D), lambda s: (tile_of(s), 0))] + _mix_param_specs(),
        out_specs=[pl.BlockSpec((TILE, INW), lambda s: (tile_of(s), 0)), small(8, CW), small(1, HD), small(1, HD),
                   small(1, CW), small(1, AW), small(1, 128), small(NH * BLK, 2 * BLK)],
        out_shape=[SDS((T, INW), BF), SDS((8, CW), F32), SDS((1, HD), F32), SDS((1, HD), F32), SDS((1, CW), F32),
                   SDS((1, AW), F32), SDS((1, 128), F32), SDS((NH * BLK, 2 * BLK), F32)],
        scratch_shapes=[pltpu.VMEM((8, CW), F32), pltpu.VMEM((BLK, NKV * HD), F32), pltpu.VMEM((BLK, NKV * HD), F32)],
        sem=("arbitrary",), vmem_mib=56, comm=comm)


FT = 256
NFT = DFF // FT
RC = 128
NCH = T // RC
LEAD = 16


def _rows8(x):
    return jnp.sum(x.reshape(x.shape[0] // 8, 8, x.shape[1]), axis=0)


def _ffn_act_specs():
    return [
        pl.BlockSpec((T, FT), lambda j: (0, j)), pl.BlockSpec((T, FT), lambda j: (0, NFT + j)),
        pl.BlockSpec((8, FT), lambda j: (0, j)), pl.BlockSpec((8, FT), lambda j: (0, NFT + j)),
        pl.BlockSpec((1, FT), lambda j: (0, j)), pl.BlockSpec((1, FT), lambda j: (0, NFT + j)),
    ]


def _conv_rows(win, w, b, n):
    win = win.astype(F32)
    u = win[LEAD:LEAD + n]
    u1 = pltpu.roll(win, 1, 0)[LEAD:LEAD + n]
    u2 = pltpu.roll(win, 2, 0)[LEAD:LEAD + n]
    return u2, u1, u, w[0:1, :] * u2 + w[1:2, :] * u1 + w[2:3, :] * u + b


def _ffn_act(up, fw8, fb):
    def body(ug_ref, uv_ref, wg_ref, wv_ref, bg_ref, bv_ref, a_ref):
        wg, wv, bg, bv = wg_ref[...], wv_ref[...], bg_ref[...], bv_ref[...]

        def chunk(win_g, win_v):
            gp = _conv_rows(win_g, wg, bg, RC)[3]
            vp = _conv_rows(win_v, wv, bv, RC)[3]
            return (gp * jax.nn.sigmoid(gp) * vp).astype(BF)

        zero = jnp.zeros((LEAD, FT), BF)
        a_ref[0:RC, :] = chunk(jnp.concatenate([zero, ug_ref[0:RC, :]], axis=0),
                               jnp.concatenate([zero, uv_ref[0:RC, :]], axis=0))

        def step(i, carry):
            r0 = pl.multiple_of(i * RC, RC)
            win = pl.ds(r0 - LEAD, RC + LEAD)
            a_ref[pl.ds(r0, RC), :] = chunk(ug_ref[win, :], uv_ref[win, :])
            return carry

        lax.fori_loop(1, NCH, step, 0)

    return _call(
        body, (up, up, fw8, fw8, fb, fb), name="ffn_act", grid=(NFT,), in_specs=_ffn_act_specs(),
        out_specs=[pl.BlockSpec((T, FT), lambda j: (0, j))], out_shape=[SDS((T, DFF), BF)],
        sem=("parallel",), vmem_mib=40)


def _ffn_act_bwd(up, da, fw8, fb, comm=()):
    ext = RC + LEAD

    def body(ug_ref, uv_ref, wg_ref, wv_ref, bg_ref, bv_ref, da_ref,
             dug_ref, duv_ref, dwg_ref, dwv_ref, dbg_ref, dbv_ref):
        wg, wv, bg, bv = wg_ref[...], wv_ref[...], bg_ref[...], bv_ref[...]

        def chunk(win_g, win_v, da_e):
            g2, g1, g0, gp = _conv_rows(win_g, wg, bg, ext)
            v2, v1, v0, vp = _conv_rows(win_v, wv, bv, ext)
            da_e = da_e.astype(F32)
            sig = jax.nn.sigmoid(gp)
            dvp = da_e * (gp * sig)
            dgp = da_e * vp * (sig * (1.0 + gp * (1.0 - sig)))

            def back(dp, w):
                return (w[2:3, :] * dp[0:RC] + w[1:2, :] * pltpu.roll(dp, ext - 1, 0)[0:RC]
                        + w[0:1, :] * pltpu.roll(dp, ext - 2, 0)[0:RC]).astype(BF)

            def sums(dp, u2, u1, u0):
                d = dp[0:RC]
                return [_rows8(d), _rows8(d * u2[0:RC]), _rows8(d * u1[0:RC]), _rows8(d * u0[0:RC])]

            return back(dgp, wg), back(dvp, wv), sums(dgp, g2, g1, g0) + sums(dvp, v2, v1, v0)

        zero = jnp.zeros((LEAD, FT), BF)
        dug, duv, acc = chunk(jnp.concatenate([zero, ug_ref[0:ext, :]], axis=0),
                              jnp.concatenate([zero, uv_ref[0:ext, :]], axis=0), da_ref[0:ext, :])
        dug_ref[0:RC, :] = dug
        duv_ref[0:RC, :] = duv

        def step(i, acc):
            r0 = pl.multiple_of(i * RC, RC)
            win = pl.ds(r0 - LEAD, ext + LEAD)
            dug, duv, part = chunk(ug_ref[win, :], uv_ref[win, :], da_ref[pl.ds(r0, ext), :])
            dug_ref[pl.ds(r0, RC), :] = dug
            duv_ref[pl.ds(r0, RC), :] = duv
            return [a + p for a, p in zip(acc, part)]

        acc = lax.fori_loop(1, NCH - 1, step, acc)
        r0 = T - RC
        tail = lambda ref, lo: jnp.concatenate([ref[lo:T, :], zero], axis=0)
        dug, duv, part = chunk(tail(ug_ref, r0 - LEAD), tail(uv_ref, r0 - LEAD), tail(da_ref, r0))
        dug_ref[r0:T, :] = dug
        duv_ref[r0:T, :] = duv
        tot = [jnp.sum(a + p, axis=0, keepdims=True) for a, p in zip(acc, part)]
        for k, (dw_ref, db_ref) in enumerate(((dwg_ref, dbg_ref), (dwv_ref, dbv_ref))):
            db_ref[...] = tot[4 * k]
            dw_ref[...] = jnp.zeros_like(dw_ref)
            for r in range(3):
                dw_ref[r:r + 1, :] = tot[4 * k + 1 + r]

    col = lambda r: pl.BlockSpec((r, FT), lambda j: (0, j))
    return _call(
        body, (up, up, fw8, fw8, fb, fb, da), name="ffn_act_bwd", grid=(NFT,),
        in_specs=_ffn_act_specs() + [pl.BlockSpec((T, FT), lambda j: (0, j))],
        out_specs=[col(T), col(T), col(8), col(8), col(1), col(1)],
        out_shape=[SDS((T, DFF), BF), SDS((T, DFF), BF), SDS((8, DFF), F32), SDS((8, DFF), F32),
                   SDS((1, DFF), F32), SDS((1, DFF), F32)],
        sem=("parallel",), vmem_mib=40, comm=comm)


def _ffn_down_bwd(dh2b, w_down, comm=()):
    tm = 256

    def body(d_ref, w_ref, o_ref):
        o_ref[...] = _dot(d_ref[...], w_ref[...], 1, 1).astype(BF)

    return _call(
        body, (dh2b, w_down), name="ffn_down_bwd", grid=(T // tm,),
        in_specs=[pl.BlockSpec((tm, D), lambda i: (i, 0)), pl.BlockSpec((DFF, D), lambda i: (0, 0))],
        out_specs=[pl.BlockSpec((tm, DFF), lambda i: (i, 0))], out_shape=[SDS((T, DFF), BF)],
        sem=("parallel",), vmem_mib=40, comm=comm)


def _norm_matmul_bwd(name, a_list, w_t, k_offsets, xin, g, dres, want_bf16, comm=(), slot=None):
    tm = 256
    ks = [a.shape[1] for a in a_list]
    n_a = len(a_list)
    n_pre = 0 if slot is None else 1

    def body(*refs):
        refs = refs[n_pre:]
        a_refs = refs[:n_a]
        w_ref, x_ref, g_ref, r_ref = refs[n_a:n_a + 4]
        outs = refs[n_a + 4:]
        dx_ref, dg_ref = outs[0], (outs[-1] if slot is None else outs[-1].at[0])

        @pl.when(pl.program_id(0) == 0)
        def _():
            dg_ref[...] = jnp.zeros_like(dg_ref)

        du = _dot(a_refs[0][...], w_ref[k_offsets[0]:k_offsets[0] + ks[0], :], 1, 0)
        for k in range(1, n_a):
            du = du + _dot(a_refs[k][...], w_ref[k_offsets[k]:k_offsets[k] + ks[k], :], 1, 0)
        x = x_ref[...]
        r = lax.rsqrt(jnp.mean(x * x, axis=-1, keepdims=True) + EPS)
        dx, dg = _rms_bwd(du, x, r, g_ref[...])
        dx = r_ref[...] + dx
        dx_ref[...] = dx
        if want_bf16:
            outs[1][...] = dx.astype(BF)
        dg_ref[...] += dg

    tile = lambda c: pl.BlockSpec((tm, c), lambda i, *_: (i, 0))
    if slot is None:
        dg_spec, dg_shape = pl.BlockSpec((1, D), lambda i: (0, 0)), SDS((1, D), F32)
    else:
        dg_spec, dg_shape = pl.BlockSpec((1, 1, D), lambda i, slot_ref: (slot_ref[0], 0, 0)), SDS((N_DEV, 1, D), F32)
    out_specs = [tile(D)] + ([tile(D)] if want_bf16 else []) + [dg_spec]
    out_shape = [SDS((T, D), F32)] + ([SDS((T, D), BF)] if want_bf16 else []) + [dg_shape]
    return _call(
        body, (*a_list, w_t, xin, g, dres), name=name, grid=(T // tm,), prefetch=() if slot is None else (slot,),
        in_specs=[tile(k) for k in ks] + [pl.BlockSpec(w_t.shape, lambda i, *_: (0, 0)), tile(D),
                                           pl.BlockSpec((1, D), lambda i, *_: (0, 0)), tile(D)],
        out_specs=out_specs, out_shape=out_shape, sem=("arbitrary",), vmem_mib=56, comm=comm)


def _out_bwd(dh1b, w_out):
    tm = 256

    def body(d_ref, w_ref, o_ref):
        o_ref[...] = _dot(d_ref[...], w_ref[...], 1, 1)

    return _call(
        body, (dh1b, w_out), name="out_bwd", grid=(T // tm,),
        in_specs=[pl.BlockSpec((tm, D), lambda i: (i, 0)), pl.BlockSpec((D, D), lambda i: (0, 0))],
        out_specs=[pl.BlockSpec((tm, D), lambda i: (i, 0))], out_shape=[SDS((T, D), F32)],
        sem=("parallel",), vmem_mib=32)


def _wgrad(name, a_list, b, comm=()):
    tm = 256
    steps = [a.shape[1] // tm for a in a_list]
    starts = [sum(steps[:k]) for k in range(len(a_list))]
    n_a = len(a_list)

    def body(*refs):
        a_refs, b_ref, o_ref = refs[:n_a], refs[n_a], refs[n_a + 1]
        i = pl.program_id(0)
        for k in range(n_a):
            @pl.when((i >= starts[k]) & (i < starts[k] + steps[k]))
            def _(k=k):
                o_ref[...] = _dot(a_refs[k][...], b_ref[...], 0, 0).astype(BF)

    def a_spec(k):
        return pl.BlockSpec((T, tm), lambda i: (0, jnp.clip(i - starts[k], 0, steps[k] - 1)))

    m_total = tm * sum(steps)
    return _call(
        body, (*a_list, b), name=name, grid=(sum(steps),),
        in_specs=[a_spec(k) for k in range(n_a)] + [pl.BlockSpec((T, D), lambda i: (0, 0))],
        out_specs=[pl.BlockSpec((tm, D), lambda i: (i, 0))], out_shape=[SDS((m_total, D), BF)],
        sem=("parallel",), vmem_mib=40, comm=comm)


def _chip_sum(name, gbf, from_sib, core, chip):
    h = gbf.shape[1]
    th = h // 2

    def body(core_ref, chip_ref, g_ref, s_ref, pbf_ref, own_ref):
        p = g_ref[0].astype(F32) + s_ref[0].astype(F32)
        pbf_ref[0] = p.astype(BF)

        @pl.when(pl.program_id(1) == chip_ref[0])
        def _():
            own_ref[...] = p

    grid_spec = pltpu.PrefetchScalarGridSpec(
        num_scalar_prefetch=2, grid=(h // th, N_CHIPS),
        in_specs=[pl.BlockSpec((1, th, D), lambda t, jj, core_ref, chip_ref: (2 * jj + core_ref[0], t, 0)),
                  pl.BlockSpec((1, th, D), lambda t, jj, core_ref, chip_ref: (jj, t, 0))],
        out_specs=[pl.BlockSpec((1, th, D), lambda t, jj, core_ref, chip_ref: (jj, t, 0)),
                   pl.BlockSpec((th, D), lambda t, jj, core_ref, chip_ref: (t, 0))],
    )
    return _pcall(
        body, name=name, grid_spec=grid_spec, out_shape=_in_hbm([SDS((N_CHIPS, h, D), BF), SDS((h, D), F32)]),
        compiler_params=_params(("arbitrary", "arbitrary"), 32),
    )(core, chip, *_from_hbm(gbf, from_sib))


def _final_sum(name, own, from_chips, core):
    h = own.shape[0]

    def body(core_ref, o_ref, r_ref, f_ref):
        f_ref[0] = ((o_ref[...] + r_ref[0].astype(F32)) + r_ref[1].astype(F32)) + r_ref[2].astype(F32)

    grid_spec = pltpu.PrefetchScalarGridSpec(
        num_scalar_prefetch=1, grid=(1,),
        in_specs=[pl.BlockSpec((h, D), lambda i, core_ref: (0, 0)), pl.BlockSpec((3, h, D), lambda i, core_ref: (0, 0, 0))],
        out_specs=pl.BlockSpec((1, h, D), lambda i, core_ref: (core_ref[0], 0, 0)),
    )
    return _pcall(body, name=name, grid_spec=grid_spec, out_shape=pltpu.HBM((2, h, D), F32),
                  compiler_params=_params(("arbitrary",), 40))(core, *_from_hbm(own, from_chips))


def _adam_math(w, g, m, v):
    nm = ADAM_B1 * m + (1.0 - ADAM_B1) * g
    nv = ADAM_B2 * v + (1.0 - ADAM_B2) * (g * g)
    m_hat = nm / (1.0 - ADAM_B1 ** ADAM_STEP)
    v_hat = nv / (1.0 - ADAM_B2 ** ADAM_STEP)
    return -ADAM_LR * (m_hat / (jnp.sqrt(v_hat) + ADAM_EPS) + ADAM_WD * w), nm, nv


def _adamw(name, w, g, m, v, tr, copy_g=False):
    rows, cols = w.shape

    def body(w_ref, g_ref, m_ref, v_ref, *outs):
        g_val = g_ref[...]
        if copy_g:
            outs[0][...] = g_val
        d_ref, nm_ref, nv_ref = outs[-3:]
        d_ref[...], nm_ref[...], nv_ref[...] = _adam_math(w_ref[...], g_val, m_ref[...], v_ref[...])

    spec = pl.BlockSpec((tr, cols), lambda i: (i, 0))
    n_out = 4 if copy_g else 3
    return _call(body, (w, g, m, v), name=name, grid=(rows // tr,), in_specs=[spec] * 4, out_specs=[spec] * n_out,
                 out_shape=[SDS((rows, cols), F32)] * n_out, sem=("parallel",), vmem_mib=32, free=(0, 2, 3))


C_G1, C_G2, C_GCO, C_GAO, C_DCW, C_DQG, C_DKG, C_SINK, C_SQ = 0, 1024, 2048, 2560, 3072, 4608, 4736, 4864, 5632
P_W = C_SQ + 128


def _pack_small(me, dfwg, dfwv, dfbg, dfbv, dg2, dgco, dgao, dcw8, dqg, dkg, dsink, sq):
    def body(me_ref, dfwg_r, dfwv_r, dfbg_r, dfbv_r, dg2_r, dgco_r, dgao_r, dcw_r, dqg_r, dkg_r, dsink_r, sq_r, o):
        o[...] = jnp.zeros_like(o)
        o[0, :, 0:DFF] = dfwg_r[...]
        o[0, :, DFF:2 * DFF] = dfwv_r[...]
        o[0, 3:4, 0:DFF] = dfbg_r[...]
        o[0, 3:4, DFF:2 * DFF] = dfbv_r[...]
        o[0, 4:5, C_G2:C_G2 + D] = dg2_r[...]
        o[0, 4:5, C_GCO:C_GCO + CW] = dgco_r[...]
        o[0, 4:5, C_GAO:C_GAO + AW] = dgao_r[...]
        for r in range(3):
            o[0, 4:5, C_DCW + r * CW:C_DCW + (r + 1) * CW] = dcw_r[r:r + 1, :]
        o[0, 4:5, C_DQG:C_DQG + HD] = dqg_r[...]
        o[0, 4:5, C_DKG:C_DKG + HD] = dkg_r[...]
        o[0, 4:5, C_SINK:C_SINK + 128] = dsink_r[...]
        o[0, :, C_SQ:C_SQ + 128] = sq_r[...]

    ins = (dfwg, dfwv, dfbg, dfbv, dg2, dgco, dgao, dcw8, dqg, dkg, dsink, sq)
    return _call(body, ins, name="pack_small", grid=(1,), prefetch=(me,),
                 in_specs=[pl.BlockSpec(a.shape, lambda i, me_ref: (0, 0)) for a in ins],
                 out_specs=[pl.BlockSpec((1, 8, P_W), lambda i, me_ref: (me_ref[0], 0, 0))],
                 out_shape=[SDS((N_DEV, 8, P_W), F32)], sem=("arbitrary",))[0]


N_SMALL = 11


def _small_adam(chip, p_all, g1_all, tbl_all, ws, ms, vs):
    fw_cols = 2 * DFF // N_CHIPS
    cw_cols = CW // N_CHIPS

    def body(chip_ref, p_ref, fw_ref, cw0_ref, cw1_ref, cw2_ref, g1_ref, tbl_ref, *refs):
        w_r, m_r, v_r = refs[0:N_SMALL], refs[N_SMALL:2 * N_SMALL], refs[2 * N_SMALL:3 * N_SMALL]
        outs = refs[3 * N_SMALL:]
        g_o, d_o, nm_o, nv_o = (outs[k * N_SMALL:(k + 1) * N_SMALL] for k in range(4))
        loss_o = outs[4 * N_SMALL]

        def total(ref):
            s = ref[0]
            for k in range(1, N_DEV):
                s = s + ref[k]
            return s

        S = total(p_ref)
        fw = total(fw_ref)
        cws = [total(r) for r in (cw0_ref, cw1_ref, cw2_ref)]

        def step(i, g, at):
            d, nm, nv = _adam_math(w_r[i][at], g, m_r[i][at], v_r[i][at])
            g_o[i][at], d_o[i][at], nm_o[i][at], nv_o[i][at] = g, d, nm, nv

        everything = (slice(None), slice(None))
        step(0, total(g1_ref), everything)
        for r in range(3):
            step(1, cws[r][4:5, :], (0, slice(r, r + 1), slice(None)))
        step(2, S[4:5, C_DQG:C_DQG + HD], everything)
        step(3, S[4:5, C_DKG:C_DKG + HD], everything)
        step(4, total(tbl_ref), everything)
        step(5, S[4:5, C_SINK:C_SINK + NH], everything)
        step(6, S[4:5, C_GCO:C_GCO + CW], everything)
        step(7, S[4:5, C_GAO:C_GAO + AW], everything)
        step(8, S[4:5, C_G2:C_G2 + D], everything)
        step(9, fw[0:3, :], (0, slice(None), slice(None)))
        step(10, S[3:4, 0:2 * DFF], everything)
        sq = S[:, C_SQ:C_SQ + 128]
        loss_o[...] = jnp.sum(jnp.sum(sq, axis=1, keepdims=True), axis=0, keepdims=True) * (0.5 / D)

    def full(a):
        n = len(a.shape)
        return pl.BlockSpec(a.shape, lambda i, chip_ref: (0,) * n)

    params = [*ws, *ms, *vs]
    out = _call(
        body, (p_all, p_all, p_all, p_all, p_all, g1_all, tbl_all, *params), name="small_adam", grid=(1,), prefetch=(chip,),
        in_specs=[full(p_all),
                  pl.BlockSpec((N_DEV, 8, fw_cols), lambda i, chip_ref: (0, 0, chip_ref[0])),
                  *[pl.BlockSpec((N_DEV, 8, cw_cols), lambda i, chip_ref, r=r: (0, 0, (C_DCW + r * CW) // cw_cols + chip_ref[0]))
                    for r in range(3)],
                  full(g1_all), full(tbl_all), *[full(a) for a in params]],
        out_specs=[full(a) for a in ws] * 4 + [pl.BlockSpec((1, 1), lambda i, chip_ref: (0, 0))],
        out_shape=[SDS(a.shape, F32) for a in ws] * 4 + [SDS((1, 1), F32)], sem=("arbitrary",), vmem_mib=32)
    return out[0:N_SMALL], out[N_SMALL:2 * N_SMALL], out[2 * N_SMALL:3 * N_SMALL], out[3 * N_SMALL:4 * N_SMALL], out[4 * N_SMALL]


PLACE_STEPS = 4


def _place_specs(shards):
    rows = [s.shape[0] // PLACE_STEPS for s in shards]
    return ([pl.BlockSpec((r, D), lambda i, chip_ref: (i, 0)) for r in rows],
            [pl.BlockSpec((r, D), lambda i, chip_ref: (chip_ref[0] * PLACE_STEPS + i, 0)) for r in rows],
            [SDS((N_CHIPS * s.shape[0], D), BF) for s in shards])


def _place_first(chip, shard, conv_w, ffn_conv_w):
    def body(chip_ref, a, s0, s1, o, t0, t1):
        o[...] = a[...].astype(BF)

        @pl.when(pl.program_id(0) == 0)
        def _():
            for s, t in ((s0, t0), (s1, t1)):
                t[...] = jnp.zeros_like(t)
                t[0, 0:3, :] = s[...]

    ins, outs, shapes = _place_specs([shard])
    taps = (conv_w, ffn_conv_w)
    return _call(
        body, (shard, conv_w, ffn_conv_w), name="place_first", grid=(PLACE_STEPS,), prefetch=(chip,),
        in_specs=ins + [pl.BlockSpec(s.shape, lambda i, chip_ref: (0, 0)) for s in taps],
        out_specs=outs + [pl.BlockSpec((1, 8, s.shape[1]), lambda i, chip_ref: (chip_ref[0], 0, 0)) for s in taps],
        out_shape=shapes + [SDS((N_CHIPS, 8, s.shape[1]), F32) for s in taps],
        sem=("arbitrary",), vmem_mib=32, free=(0, 1, 2))


def _place_rest(chip, shards, table, bucket, comm):
    n = len(shards)

    def body(chip_ref, *refs):
        a, (tab_ref, bk_ref), o, bias_ref = refs[:n], refs[n:n + 2], refs[n + 2:2 * n + 2], refs[2 * n + 2]
        for src, dst in zip(a, o):
            dst[...] = src[...].astype(BF)

        @pl.when(pl.program_id(0) == 0)
        def _():
            bk = bk_ref[...]
            eq = [bk == b for b in range(NBUCKET)]
            for h in range(NH):
                acc = jnp.zeros((BLK, 2 * BLK), F32)
                for b in range(NBUCKET):
                    acc = jnp.where(eq[b], tab_ref[h, b], acc)
                bias_ref[h * BLK:(h + 1) * BLK, :] = acc

    ins, outs, shapes = _place_specs(shards)
    return _call(
        body, (*shards, table, bucket), name="place_rest", grid=(PLACE_STEPS,), prefetch=(chip,),
        in_specs=ins + [pl.BlockSpec(memory_space=pltpu.SMEM), pl.BlockSpec(bucket.shape, lambda i, chip_ref: (0, 0))],
        out_specs=outs + [pl.BlockSpec((NH * BLK, 2 * BLK), lambda i, chip_ref: (0, 0))],
        out_shape=shapes + [SDS((NH * BLK, 2 * BLK), F32)],
        sem=("arbitrary",), vmem_mib=32, comm=comm, free=tuple(range(n + 2)))


def kernel(x, norm_mix_g, w_in, conv_w, q_norm_g, k_norm_g, rel_bias_table, sinks, out_norm_conv_g, out_norm_attn_g, w_out, norm_ffn_g, w_up, ffn_conv_w, ffn_conv_b, w_down, loss_target, m_norm_mix_g, m_w_in, m_conv_w, m_q_norm_g, m_k_norm_g, m_rel_bias_table, m_sinks, m_out_norm_conv_g, m_out_norm_attn_g, m_w_out, m_norm_ffn_g, m_w_up, m_ffn_conv_w, m_ffn_conv_b, m_w_down, v_norm_mix_g, v_w_in, v_conv_w, v_q_norm_g, v_k_norm_g, v_rel_bias_table, v_sinks, v_out_norm_conv_g, v_out_norm_attn_g, v_w_out, v_norm_ffn_g, v_w_up, v_ffn_conv_w, v_ffn_conv_b, v_w_down):
    as_arg = lambda i: jnp.reshape(i, (1,)).astype(jnp.int32)
    chip = as_arg(2 * lax.axis_index("x") + lax.axis_index("y"))
    core = as_arg(lax.axis_index("c"))
    me = 2 * chip + core
    xs, tgt = x[0], loss_target[0]
    qg, kg, gco, gao, g1, g2, fb = q_norm_g, k_norm_g, out_norm_conv_g, out_norm_attn_g, norm_mix_g, norm_ffn_g, ffn_conv_b
    pieces = lambda g: g.reshape(N_DEV, g.shape[0] // N_DEV, D)
    whole = lambda f: f.reshape(2 * f.shape[1], D)

    bucket = jnp.asarray(_bucket_table())
    p_in, p_cw, p_fw = _place_first(chip, w_in[0].T, conv_w[0], ffn_conv_w[0])
    p_out, p_up, p_down, bias, w_int, cw_all, fw_all = _place_rest(
        chip, [w_out[0], w_up[0].T, w_down[0]], rel_bias_table.T, bucket,
        comm=[_t_gather(p_in), _t_small_weights(p_cw), _t_small_weights(p_fw)])
    cw8 = jnp.transpose(cw_all, (1, 0, 2)).reshape(8, CW)
    fw8 = jnp.transpose(fw_all, (1, 0, 2)).reshape(8, 2 * DFF)

    proj, u1, w_out_f = _inproj(xs, g1, w_int, comm=[_t_gather(p_out)])
    y, w_upt = _mix_fwd(proj, sinks, cw8, qg, kg, gco, gao, bias, comm=[_t_gather(p_up)])
    h1, u2 = _outproj(y, w_out_f, xs, g2)
    up, w_down_f = _ffn_up(u2, w_upt, comm=[_t_gather(p_down)])
    a, = _ffn_act(up, fw8, fb)
    dh2, dh2b, sq = _ffn_down(a, w_down_f, h1, tgt)

    gdbf, = _wgrad("wgrad_down", [a], dh2b)
    da, sib_down = _ffn_down_bwd(dh2b, w_down_f, comm=[_t_sibling(pieces(gdbf))])
    pbf_down, own_down = _chip_sum("chip_sum_w_down", pieces(gdbf), sib_down, core, chip)
    dug, duv, dfwg, dfwv, dfbg, dfbv, chips_down = _ffn_act_bwd(up, da, fw8, fb, comm=[_t_chips(pbf_down)])
    fin_down = _final_sum("final_sum_w_down", own_down, chips_down, core)
    gubf, = _wgrad("wgrad_up", [dug, duv], u2)
    dh1, dh1b, dg2, sib_up, fin_down = _norm_matmul_bwd(
        "ffn_up_bwd", [dug, duv], w_upt, [0, DFF], h1, g2, dh2, True, comm=[_t_sibling(pieces(gubf)), _t_swap(fin_down)])
    pbf_up, own_up = _chip_sum("chip_sum_w_up", pieces(gubf), sib_up, core, chip)
    gobf, = _wgrad("wgrad_out", [y], dh1b)
    dy, = _out_bwd(dh1b, w_out_f)
    dproj, dcw8, dqg, dkg, dgco, dgao, dsink, dbias, chips_up, sib_out = _mix_bwd(
        proj, dy, sinks, cw8, qg, kg, gco, gao, bias, comm=[_t_chips(pbf_up), _t_sibling(pieces(gobf))])
    fin_up = _final_sum("final_sum_w_up", own_up, chips_up, core)
    pbf_out, own_out = _chip_sum("chip_sum_w_out", pieces(gobf), sib_out, core, chip)
    tbl_all = _band_bias_bwd(dbias, bucket, me)
    p_all = _pack_small(me, dfwg, dfwv, dfbg, dfbv, dg2, dgco, dgao, dcw8, dqg, dkg, dsink, sq)
    gibf, chips_out, fin_up, p_all, tbl_all = _wgrad(
        "wgrad_in", [dproj], u1, comm=[_t_chips(pbf_out), _t_swap(fin_up), _t_allgather(p_all), _t_allgather(tbl_all)])
    fin_out = _final_sum("final_sum_w_out", own_out, chips_out, core)
    sib_in, fin_out = _comm_call("to_sibling_last", [_t_sibling(pieces(gibf)), _t_swap(fin_out)])
    pbf_in, own_in = _chip_sum("chip_sum_w_in", pieces(gibf), sib_in, core, chip)
    dx, g1_all, chips_in = _norm_matmul_bwd(
        "in_bwd", [dproj], w_int, [0], xs, g1, dh1, False, comm=[_t_chips(pbf_in)], slot=me)
    fin_in = _final_sum("final_sum_w_in", own_in, chips_in, core)
    g1_all, fin_in = _comm_call("gather_last", [_t_allgather(g1_all), _t_swap(fin_in)])

    g_w_in, g_w_out, g_w_up, g_w_down = whole(fin_in).T, whole(fin_out), whole(fin_up).T, whole(fin_down)
    g_w_down, d_down, nm_down, nv_down = _adamw("adamw_w_down", w_down[0], g_w_down, m_w_down[0], v_w_down[0], 352, True)
    d_up, nm_up, nv_up = _adamw("adamw_w_up", w_up[0], g_w_up, m_w_up[0], v_w_up[0], 256)
    g_w_out, d_out, nm_out, nv_out = _adamw("adamw_w_out", w_out[0], g_w_out, m_w_out[0], v_w_out[0], 256, True)
    d_in, nm_in, nv_in = _adamw("adamw_w_in", w_in[0], g_w_in, m_w_in[0], v_w_in[0], 256)
    sw = [norm_mix_g, conv_w, q_norm_g, k_norm_g, rel_bias_table.T, sinks, out_norm_conv_g, out_norm_attn_g,
          norm_ffn_g, ffn_conv_w, ffn_conv_b]
    smm = [m_norm_mix_g, m_conv_w, m_q_norm_g, m_k_norm_g, m_rel_bias_table.T, m_sinks, m_out_norm_conv_g,
           m_out_norm_attn_g, m_norm_ffn_g, m_ffn_conv_w, m_ffn_conv_b]
    smv = [v_norm_mix_g, v_conv_w, v_q_norm_g, v_k_norm_g, v_rel_bias_table.T, v_sinks, v_out_norm_conv_g,
           v_out_norm_attn_g, v_norm_ffn_g, v_ffn_conv_w, v_ffn_conv_b]
    *small_out, loss = _small_adam(chip, p_all, g1_all, tbl_all, sw, smm, smv)
    sg, sd, snm, snv = [list(r) for r in small_out]
    for r in (sg, sd, snm, snv):
        r[4] = r[4].T

    def order(s, b_in, b_out, b_up, b_down):
        return (s[0], b_in[None], s[1], s[2], s[3], s[4], s[5], s[6], s[7], b_out[None], s[8], b_up[None],
                s[9], s[10], b_down[None])

    return (loss.reshape(()), dx[None],
            *order(sg, g_w_in, g_w_out, g_w_up, g_w_down),
            *order(sd, d_in, d_out, d_up, d_down),
            *order(snm, nm_in, nm_out, nm_up, nm_down),
            *order(snv, nv_in, nv_out, nv_up, nv_down))
```

```python
import functools
import math

import numpy as np

import jax
import jax.numpy as jnp
from jax import lax
from jax.experimental import pallas as pl
from jax.experimental.pallas import tpu as pltpu

F32 = jnp.float32
BF = jnp.bfloat16
SDS = jax.ShapeDtypeStruct

T = 2048
D = 1024
CW = 512
AW = 512
HD = 64
NH = 8
NKV = 2
GQ = 4
INW = 2304
DFF = 2816
BLK = 128
NB = T // BLK
NBUCKET = 32
EPS = 1e-6
NEG_INF = -1e30
N_CHIPS = 4
N_DEV = 8

ADAM_LR = 0.001
ADAM_B1 = 0.9
ADAM_B2 = 0.999
ADAM_EPS = 1e-08
ADAM_WD = 0.01
ADAM_STEP = 10

TM = 512
MIB = 1024 * 1024
MESH = pl.DeviceIdType.MESH
ANY = pl.BlockSpec(memory_space=pl.ANY)

_pcall = pl.pallas_call


def _params(sem=None, vmem_mib=None):
    kw = {}
    if sem is not None:
        kw["dimension_semantics"] = sem
    if vmem_mib is not None:
        kw["vmem_limit_bytes"] = vmem_mib * MIB
    return pltpu.CompilerParams(**kw)


def _resident(shape):
    return pl.BlockSpec(shape, lambda *_: (0,) * len(shape), pipeline_mode=pl.Buffered(1))


def _dot(a, b, ca, cb):
    return lax.dot_general(a, b, (((ca,), (cb,)), ((), ())), preferred_element_type=F32)


def _rms_bwd(dy, x, r, g):
    dg = jnp.sum(dy * (x * r), axis=0, keepdims=True)
    dgx = dy * g
    dx = r * dgx - x * (r * r * r) * jnp.mean(x * dgx, axis=-1, keepdims=True)
    return dx, dg


def _where():
    x, y, c = lax.axis_index("x"), lax.axis_index("y"), lax.axis_index("c")
    return x, y, c, [(1 - x, y), (x, 1 - y), (1 - x, 1 - y)]


def _rcopy(src, dst, ssem, rsem, dev):
    return pltpu.make_async_remote_copy(src_ref=src, dst_ref=dst, send_sem=ssem, recv_sem=rsem, device_id=dev,
                                        device_id_type=MESH)


class _Task:
    def __init__(self, ins, outs, alias, n_sem, start, finish):
        self.ins, self.outs, self.alias, self.n_sem, self.start, self.finish = ins, outs, alias, n_sem, start, finish


def _t_gather(placed):
    R = placed.shape[0] // N_CHIPS

    def rows(chip_index, core):
        return pl.ds(pl.multiple_of(chip_index * R + core * (R // 2), 16), R // 2)

    def start(cin, cout, ss, rs, b):
        x, y, c, chips = _where()
        mine = cout[0].at[rows(2 * x + y, c)]
        for r, (px, py) in enumerate(chips):
            _rcopy(mine, mine, ss.at[b + r], rs.at[b + r], (px, py, c)).start()

    def finish(cin, cout, ss, rs, b):
        x, y, c, chips = _where()
        buf = cout[0]
        sib = (x, y, 1 - c)
        for r, (px, py) in enumerate(chips):
            got = buf.at[rows(2 * px + py, c)]
            _rcopy(got, got, ss.at[b + r], rs.at[b + r], (px, py, c)).wait_recv()
            _rcopy(got, got, ss.at[b + 3 + r], rs.at[b + 3 + r], sib).start()
        for r, (px, py) in enumerate(chips):
            got = buf.at[rows(2 * px + py, 1 - c)]
            _rcopy(got, got, ss.at[b + 3 + r], rs.at[b + 3 + r], sib).wait_recv()
        mine = buf.at[rows(2 * x + y, c)]
        for r in range(6):
            _rcopy(mine, mine, ss.at[b + r], rs.at[b + r], sib).wait_send()

    return _Task([placed], [SDS(placed.shape, placed.dtype)], [(0, 0)], 6, start, finish)


def _t_small_weights(buf):
    def start(cin, cout, ss, rs, b):
        x, y, c, chips = _where()
        mine = cout[0].at[2 * x + y]
        for r, (px, py) in enumerate(chips):
            _rcopy(mine, mine, ss.at[b + r], rs.at[b + r], (px, py, c)).start()

    def finish(cin, cout, ss, rs, b):
        x, y, c, chips = _where()
        for r, (px, py) in enumerate(chips):
            got = cout[0].at[2 * px + py]
            _rcopy(got, got, ss.at[b + r], rs.at[b + r], (px, py, c)).wait_recv()
        for r, (px, py) in enumerate(chips):
            mine = cout[0].at[2 * x + y]
            _rcopy(mine, mine, ss.at[b + r], rs.at[b + r], (px, py, c)).wait_send()

    return _Task([buf], [SDS(buf.shape, buf.dtype)], [(0, 0)], 3, start, finish)


def _t_sibling(gbf):
    def start(cin, cout, ss, rs, b):
        x, y, c, _ = _where()
        for jj in range(N_CHIPS):
            _rcopy(cin[0].at[2 * jj + (1 - c)], cout[0].at[jj], ss.at[b + jj], rs.at[b + jj], (x, y, 1 - c)).start()

    def finish(cin, cout, ss, rs, b):
        x, y, c, _ = _where()
        for jj in range(N_CHIPS):
            got = cout[0].at[jj]
            _rcopy(got, got, ss.at[b + jj], rs.at[b + jj], (x, y, 1 - c)).wait_recv()
        for jj in range(N_CHIPS):
            got = cout[0].at[jj]
            _rcopy(got, got, ss.at[b + jj], rs.at[b + jj], (x, y, 1 - c)).wait_send()

    return _Task([gbf], [SDS((N_CHIPS,) + gbf.shape[1:], BF)], [], N_CHIPS, start, finish)


def _t_chips(pbf):
    def start(cin, cout, ss, rs, b):
        x, y, c, chips = _where()
        for r, (px, py) in enumerate(chips):
            _rcopy(cin[0].at[2 * px + py], cout[0].at[r], ss.at[b + r], rs.at[b + r], (px, py, c)).start()

    def finish(cin, cout, ss, rs, b):
        x, y, c, chips = _where()
        for r, (px, py) in enumerate(chips):
            got = cout[0].at[r]
            _rcopy(got, got, ss.at[b + r], rs.at[b + r], (px, py, c)).wait_recv()
        for r, (px, py) in enumerate(chips):
            got = cout[0].at[r]
            _rcopy(got, got, ss.at[b + r], rs.at[b + r], (px, py, c)).wait_send()

    return _Task([pbf], [SDS((3,) + pbf.shape[1:], BF)], [], 3, start, finish)


def _t_swap(fin):
    def start(cin, cout, ss, rs, b):
        x, y, c, _ = _where()
        mine = cout[0].at[c]
        _rcopy(mine, mine, ss.at[b], rs.at[b], (x, y, 1 - c)).start()

    def finish(cin, cout, ss, rs, b):
        x, y, c, _ = _where()
        got = cout[0].at[1 - c]
        _rcopy(got, got, ss.at[b], rs.at[b], (x, y, 1 - c)).wait_recv()
        _rcopy(got, got, ss.at[b], rs.at[b], (x, y, 1 - c)).wait_send()

    return _Task([fin], [SDS(fin.shape, fin.dtype)], [(0, 0)], 1, start, finish)


def _t_allgather(buf):
    def peers():
        x, y, c, _ = _where()
        out = []
        for rel in range(1, N_DEV):
            px, py, pc = x ^ ((rel >> 2) & 1), y ^ ((rel >> 1) & 1), c ^ (rel & 1)
            out.append((rel - 1, 4 * px + 2 * py + pc, (px, py, pc)))
        return 4 * x + 2 * y + c, out

    def start(cin, cout, ss, rs, b):
        me, ps = peers()
        mine = cout[0].at[me]
        for k, _, dev in ps:
            _rcopy(mine, mine, ss.at[b + k], rs.at[b + k], dev).start()

    def finish(cin, cout, ss, rs, b):
        me, ps = peers()
        for k, pidx, dev in ps:
            got = cout[0].at[pidx]
            _rcopy(got, got, ss.at[b + k], rs.at[b + k], dev).wait_recv()
        for k, _, dev in ps:
            mine = cout[0].at[me]
            _rcopy(mine, mine, ss.at[b + k], rs.at[b + k], dev).wait_send()

    return _Task([buf], [SDS(buf.shape, buf.dtype)], [(0, 0)], N_DEV - 1, start, finish)


def _run_tasks(comm, which, cin, cout, ss, rs):
    i0 = o0 = s0 = 0
    for t in comm:
        getattr(t, which)(cin[i0:i0 + len(t.ins)], cout[o0:o0 + len(t.outs)], ss, rs, s0)
        i0, o0, s0 = i0 + len(t.ins), o0 + len(t.outs), s0 + t.n_sem


def _from_hbm(*arrays):
    return [pltpu.with_memory_space_constraint(a, pltpu.HBM) for a in arrays]


def _in_hbm(shapes):
    return [pltpu.HBM(s.shape, s.dtype) for s in shapes]


def _comm_layout(comm, n_in, n_out):
    c_in = [a for t in comm for a in t.ins]
    c_out = [s for t in comm for s in t.outs]
    aliases, i0, o0 = {}, 0, 0
    for t in comm:
        for i, o in t.alias:
            aliases[n_in + i0 + i] = n_out + o0 + o
        i0, o0 = i0 + len(t.ins), o0 + len(t.outs)
    return c_in, c_out, aliases, sum(t.n_sem for t in comm)


def _call(body, operands, *, name, grid, in_specs, out_specs, out_shape, scratch_shapes=(), sem=None, vmem_mib=None, comm=(),
          free=(), prefetch=()):
    operands = [o if s.memory_space == pltpu.SMEM or k in free else pltpu.with_memory_space_constraint(o, pltpu.HBM)
                for k, (o, s) in enumerate(zip(operands, in_specs))]
    n_pre, n_in, n_out, n_scr = len(prefetch), len(in_specs), len(out_specs), len(scratch_shapes)
    c_in, c_out, aliases, n_sem = _comm_layout(comm, n_pre + n_in, n_out)
    sems = [pltpu.SemaphoreType.DMA((n_sem,)), pltpu.SemaphoreType.DMA((n_sem,))] if comm else []

    def wrapped(*refs):
        pre, refs = refs[:n_pre], refs[n_pre:]
        ins, cin = refs[:n_in], refs[n_in:n_in + len(c_in)]
        rest = refs[n_in + len(c_in):]
        outs, cout = rest[:n_out], rest[n_out:n_out + len(c_out)]
        rest = rest[n_out + len(c_out):]
        scr, csem = rest[:n_scr], rest[n_scr:]
        if not comm:
            return body(*pre, *ins, *outs, *scr)
        ids = [pl.program_id(k) for k in range(len(grid))]
        first = functools.reduce(jnp.logical_and, [i == 0 for i in ids])
        last = functools.reduce(jnp.logical_and, [i == n - 1 for i, n in zip(ids, grid)])
        pl.when(first)(lambda: _run_tasks(comm, "start", cin, cout, *csem))
        body(*pre, *ins, *outs, *scr)
        pl.when(last)(lambda: _run_tasks(comm, "finish", cin, cout, *csem))

    grid_spec = pltpu.PrefetchScalarGridSpec(
        num_scalar_prefetch=n_pre, grid=grid, in_specs=list(in_specs) + [ANY] * len(c_in),
        out_specs=list(out_specs) + [ANY] * len(c_out), scratch_shapes=list(scratch_shapes) + sems)
    return _pcall(
        wrapped, name=name, grid_spec=grid_spec, out_shape=_in_hbm(list(out_shape) + c_out), input_output_aliases=aliases,
        compiler_params=_params(("arbitrary",) * len(grid) if comm else sem, vmem_mib),
    )(*prefetch, *operands, *_from_hbm(*c_in))


def _comm_call(name, comm):
    c_in, c_out, aliases, n_sem = _comm_layout(comm, 0, 0)

    def body(*refs):
        cin, cout, (ss, rs) = refs[:len(c_in)], refs[len(c_in):len(c_in) + len(c_out)], refs[len(c_in) + len(c_out):]
        _run_tasks(comm, "start", cin, cout, ss, rs)
        _run_tasks(comm, "finish", cin, cout, ss, rs)

    return _pcall(
        body, name=name, in_specs=[ANY] * len(c_in), out_specs=[ANY] * len(c_out), out_shape=_in_hbm(c_out),
        scratch_shapes=[pltpu.SemaphoreType.DMA((n_sem,)), pltpu.SemaphoreType.DMA((n_sem,))],
        input_output_aliases=aliases,
    )(*_from_hbm(*c_in))


def _inproj(x, g1, w_int, comm=()):
    tm = TM

    def body(x_ref, g_ref, w_ref, proj_ref, u_ref):
        xf = x_ref[...]
        r = lax.rsqrt(jnp.mean(xf * xf, axis=-1, keepdims=True) + EPS)
        u = (xf * r * g_ref[...]).astype(BF)
        u_ref[...] = u
        proj_ref[...] = _dot(u, w_ref[...], 1, 1)

    return _call(
        body, (x, g1, w_int), name="inproj", grid=(T // tm,),
        in_specs=[pl.BlockSpec((tm, D), lambda i: (i, 0)), pl.BlockSpec((1, D), lambda i: (0, 0)),
                  _resident((INW, D))],
        out_specs=[pl.BlockSpec((tm, INW), lambda i: (i, 0)), pl.BlockSpec((tm, D), lambda i: (i, 0))],
        out_shape=[SDS((T, INW), F32), SDS((T, D), BF)], sem=("parallel",), vmem_mib=40, comm=comm)


def _outproj(y, w_out, x, g2):
    tm = TM

    def body(y_ref, w_ref, x_ref, g_ref, h1_ref, u2_ref):
        h1 = x_ref[...] + _dot(y_ref[...], w_ref[...], 1, 0)
        h1_ref[...] = h1
        r = lax.rsqrt(jnp.mean(h1 * h1, axis=-1, keepdims=True) + EPS)
        u2_ref[...] = (h1 * r * g_ref[...]).astype(BF)

    return _call(
        body, (y, w_out, x, g2), name="outproj", grid=(T // tm,),
        in_specs=[pl.BlockSpec((tm, D), lambda i: (i, 0)), _resident((D, D)),
                  pl.BlockSpec((tm, D), lambda i: (i, 0)), pl.BlockSpec((1, D), lambda i: (0, 0))],
        out_specs=[pl.BlockSpec((tm, D), lambda i: (i, 0)), pl.BlockSpec((tm, D), lambda i: (i, 0))],
        out_shape=[SDS((T, D), F32), SDS((T, D), BF)], sem=("parallel",), vmem_mib=32)


def _ffn_up(u2, w_upt, comm=()):
    tm, tn = 1024, 512

    def body(u_ref, w_ref, o_ref):
        o_ref[...] = _dot(u_ref[...], w_ref[...], 1, 1).astype(BF)

    return _call(
        body, (u2, w_upt), name="ffn_up", grid=(T // tm, 2 * DFF // tn),
        in_specs=[pl.BlockSpec((tm, D), lambda i, j: (i, 0)), pl.BlockSpec((tn, D), lambda i, j: (j, 0))],
        out_specs=[pl.BlockSpec((tm, tn), lambda i, j: (i, j))], out_shape=[SDS((T, 2 * DFF), BF)],
        sem=("parallel", "parallel"), vmem_mib=32, comm=comm)


def _ffn_down(a, w_down, h1, tgt):
    tm = TM

    def body(a_ref, w_ref, h1_ref, t_ref, dh_ref, dhb_ref, l_ref):
        @pl.when(pl.program_id(0) == 0)
        def _():
            l_ref[...] = jnp.zeros_like(l_ref)

        h2 = h1_ref[...] + _dot(a_ref[...], w_ref[...], 1, 0)
        e = h2 - t_ref[...]
        dh = e * (1.0 / D)
        dh_ref[...] = dh
        dhb_ref[...] = dh.astype(BF)
        e2 = jnp.sum((e * e).reshape(tm // 8, 8, D), axis=0)
        acc = e2[:, 0:128]
        for k in range(1, D // 128):
            acc = acc + e2[:, k * 128:(k + 1) * 128]
        l_ref[...] += acc

    return _call(
        body, (a, w_down, h1, tgt), name="ffn_down", grid=(T // tm,),
        in_specs=[pl.BlockSpec((tm, DFF), lambda i: (i, 0)), _resident((DFF, D)),
                  pl.BlockSpec((tm, D), lambda i: (i, 0)), pl.BlockSpec((tm, D), lambda i: (i, 0))],
        out_specs=[pl.BlockSpec((tm, D), lambda i: (i, 0)), pl.BlockSpec((tm, D), lambda i: (i, 0)),
                   pl.BlockSpec((8, 128), lambda i: (0, 0))],
        out_shape=[SDS((T, D), F32), SDS((T, D), BF), SDS((8, 128), F32)], sem=("arbitrary",), vmem_mib=40)


def _bucket_table():
    q = np.arange(BLK, dtype=np.int32)[:, None]
    j = np.arange(2 * BLK, dtype=np.int32)[None, :]
    n = np.maximum(q + BLK - j, 0)
    nf = np.maximum(n, 1).astype(np.float32)
    max_exact = NBUCKET // 2
    large = max_exact + (np.log(nf / np.float32(max_exact)) / np.float32(math.log(BLK / max_exact))
                         * np.float32(NBUCKET - max_exact)).astype(np.int32)
    large = np.minimum(large, NBUCKET - 1)
    return np.where(n < max_exact, n, large).astype(np.int32)


def _band_bias_bwd(dbias, bucket, me):
    def body(me_ref, db_ref, bk_ref, o_ref):
        bk = bk_ref[...]
        for b in range(NBUCKET):
            m = bk == b
            for h in range(NH):
                v = jnp.where(m, db_ref[h * BLK:(h + 1) * BLK, :], 0.0)
                s = jnp.sum(jnp.sum(v, axis=1, keepdims=True), axis=0, keepdims=True)
                o_ref[0, h:h + 1, b:b + 1] = s

    grid_spec = pltpu.PrefetchScalarGridSpec(
        num_scalar_prefetch=1, grid=(1,),
        in_specs=[pl.BlockSpec((NH * BLK, 2 * BLK), lambda i, me_ref: (0, 0)),
                  pl.BlockSpec((BLK, 2 * BLK), lambda i, me_ref: (0, 0))],
        out_specs=pl.BlockSpec((1, NH, NBUCKET), lambda i, me_ref: (me_ref[0], 0, 0)),
    )
    return _pcall(body, name="band_bias_bwd", grid_spec=grid_spec, out_shape=SDS((N_DEV, NH, NBUCKET), F32),
                  compiler_params=_params(("arbitrary",)))(me, dbias, bucket)


def _mix_forward(P, zc8, zh8, pkv, first, cw, qg, kg, gco, gao, sink_ref, bias_ref):
    gate_b = P[:, 0:CW]
    gate_c = P[:, CW:2 * CW]
    hc = P[:, 2 * CW:3 * CW]
    z = gate_c * hc
    keep = jnp.where(first, 0.0, 1.0)
    zp = zc8 * zh8 * keep
    p1 = zp[7:8, :]
    p2 = zp[6:7, :]
    row = lax.broadcasted_iota(jnp.int32, (BLK, 1), 0)
    z1 = jnp.where(row == 0, p1, pltpu.roll(z, 1, 0))
    z2 = jnp.where(row == 0, p2, jnp.where(row == 1, p1, pltpu.roll(z, 2, 0)))
    cz = cw[0:1, :] * z2 + cw[1:2, :] * z1 + cw[2:3, :] * z
    y_conv = gate_b * cz

    scale = HD ** -0.5
    qi = lax.broadcasted_iota(jnp.int32, (GQ * BLK, 2 * BLK), 0) & (BLK - 1)
    kj = lax.broadcasted_iota(jnp.int32, (GQ * BLK, 2 * BLK), 1)
    dd = qi + BLK - kj
    first_key = jnp.where(first, BLK, 0)
    valid = (dd >= 0) & (dd < BLK) & (kj >= first_key)

    q0 = 3 * CW
    k0 = q0 + AW
    v0 = k0 + NKV * HD
    heads = []
    outs = []
    for kv in range(NKV):
        kb_raw = jnp.concatenate([pkv[:, kv * HD:(kv + 1) * HD], P[:, k0 + kv * HD:k0 + (kv + 1) * HD]], axis=0)
        rk = lax.rsqrt(jnp.mean(kb_raw * kb_raw, axis=-1, keepdims=True) + EPS)
        kb = (kb_raw * rk * kg).astype(BF)
        vb = jnp.concatenate([pkv[:, NKV * HD + kv * HD:NKV * HD + (kv + 1) * HD],
                              P[:, v0 + kv * HD:v0 + (kv + 1) * HD]], axis=0).astype(BF)
        q_raw, rq, qn = [], [], []
        for g in range(GQ):
            h = kv * GQ + g
            qh = P[:, q0 + h * HD:q0 + (h + 1) * HD]
            r = lax.rsqrt(jnp.mean(qh * qh, axis=-1, keepdims=True) + EPS)
            q_raw.append(qh)
            rq.append(r)
            qn.append(qh * r * qg)
        Q = jnp.concatenate(qn, axis=0).astype(BF)
        S = _dot(Q, kb, 1, 1) * scale + bias_ref[kv * GQ * BLK:(kv + 1) * GQ * BLK, :]
        S = jnp.where(valid, S, NEG_INF)
        sink = jnp.concatenate([jnp.full((BLK, 1), sink_ref[0, kv * GQ + g], F32) for g in range(GQ)], axis=0)
        m = jnp.maximum(jnp.max(S, axis=-1, keepdims=True), sink)
        p = jnp.exp(S - m)
        es = jnp.exp(sink - m)
        denom = jnp.sum(p, axis=-1, keepdims=True) + es
        probs = p / denom
        O = _dot(probs.astype(BF), vb, 1, 0)
        heads.append(dict(kb_raw=kb_raw, rk=rk, kb=kb, vb=vb, q_raw=q_raw, rq=rq, Q=Q, probs=probs,
                          psink=es / denom, O=O))
        outs += [O[g * BLK:(g + 1) * BLK, :] for g in range(GQ)]
    y_attn = jnp.concatenate(outs, axis=1)

    rc = lax.rsqrt(jnp.mean(y_conv * y_conv, axis=-1, keepdims=True) + EPS)
    ra = lax.rsqrt(jnp.mean(y_attn * y_attn, axis=-1, keepdims=True) + EPS)
    y = jnp.concatenate([y_conv * rc * gco, y_attn * ra * gao], axis=1)
    return dict(gate_b=gate_b, gate_c=gate_c, hc=hc, z=z, z1=z1, z2=z2, cz=cz, y_conv=y_conv, y_attn=y_attn,
                rc=rc, ra=ra, heads=heads, y=y, row=row, scale=scale)


BPS = 2
TILE = BPS * BLK
KV0 = 3 * CW + AW


def _mix_in_specs(tile_of):
    return [
        pl.BlockSpec(memory_space=pltpu.SMEM),
        pl.BlockSpec((TILE, INW), lambda s: (tile_of(s), 0)),
        pl.BlockSpec((8, CW), lambda s: (jnp.maximum(tile_of(s) * (TILE // 8) - 1, 0), 1)),
        pl.BlockSpec((8, CW), lambda s: (jnp.maximum(tile_of(s) * (TILE // 8) - 1, 0), 2)),
        pl.BlockSpec((BLK, 2 * NKV * HD), lambda s: (jnp.maximum(tile_of(s) * BPS - 1, 0), KV0 // (2 * NKV * HD))),
    ]


def _block_inputs(tile, b, zc_ref, zh_ref, pkv_ref, first_tile):
    P = tile[b * BLK:(b + 1) * BLK, :]
    if b == 0:
        return P, zc_ref[...], zh_ref[...], pkv_ref[...], first_tile
    lo = b * BLK
    return P, tile[lo - 8:lo, CW:2 * CW], tile[lo - 8:lo, 2 * CW:3 * CW], tile[lo - BLK:lo, KV0:KV0 + 2 * NKV * HD], False


def _mix_param_specs():
    return [
        pl.BlockSpec((8, CW), lambda s: (0, 0)),
        pl.BlockSpec((1, HD), lambda s: (0, 0)),
        pl.BlockSpec((1, HD), lambda s: (0, 0)),
        pl.BlockSpec((1, CW), lambda s: (0, 0)),
        pl.BlockSpec((1, AW), lambda s: (0, 0)),
        pl.BlockSpec((NH * BLK, 2 * BLK), lambda s: (0, 0)),
    ]


def _mix_fwd(proj, sinks, cw8, qg, kg, gco, gao, bias, comm=()):
    def body(sink_ref, p_ref, zc_ref, zh_ref, pkv_ref, cw_ref, qg_ref, kg_ref, gco_ref, gao_ref, bias_ref, y_ref):
        tile = p_ref[...]
        for b in range(BPS):
            f = _mix_forward(*_block_inputs(tile, b, zc_ref, zh_ref, pkv_ref, pl.program_id(0) == 0), cw_ref[...],
                             qg_ref[...], kg_ref[...], gco_ref[...], gao_ref[...], sink_ref, bias_ref)
            y_ref[b * BLK:(b + 1) * BLK, :] = f["y"].astype(BF)

    return _call(
        body, (sinks, proj, proj, proj, proj, cw8, qg, kg, gco, gao, bias), name="mix_fwd", grid=(T // TILE,),
        in_specs=_mix_in_specs(lambda s: s) + _mix_param_specs(),
        out_specs=[pl.BlockSpec((TILE, D), lambda s: (s, 0))], out_shape=[SDS((T, D), BF)],
        sem=("parallel",), vmem_mib=40, comm=comm)


def _mix_bwd(proj, dy, sinks, cw8, qg, kg, gco, gao, bias, comm=()):
    n_steps = T // TILE

    def tile_of(s):
        return n_steps - 1 - s

    def body(sink_ref, p_ref, zc_ref, zh_ref, pkv_ref, dy_ref, cw_ref, qg_ref, kg_ref, gco_ref, gao_ref, bias_ref,
             dproj_ref, dcw_ref, dqg_ref, dkg_ref, dgco_ref, dgao_ref, dsink_ref, dbias_ref,
             ndcz_ref, dkc_ref, dvc_ref):
        s = pl.program_id(0)

        @pl.when(s == 0)
        def _():
            for r in (dcw_ref, dqg_ref, dkg_ref, dgco_ref, dgao_ref, dsink_ref, dbias_ref, ndcz_ref, dkc_ref, dvc_ref):
                r[...] = jnp.zeros_like(r)

        tile = p_ref[...]
        for b in reversed(range(BPS)):
            one_block(b, _block_inputs(tile, b, zc_ref, zh_ref, pkv_ref, s == n_steps - 1),
                      dy_ref[b * BLK:(b + 1) * BLK, :], sink_ref, cw_ref, qg_ref, kg_ref, gco_ref, gao_ref, bias_ref,
                      dproj_ref.at[b * BLK:(b + 1) * BLK, :], dcw_ref, dqg_ref, dkg_ref, dgco_ref, dgao_ref, dsink_ref,
                      dbias_ref, ndcz_ref, dkc_ref, dvc_ref)

    def one_block(b, inputs, dy, sink_ref, cw_ref, qg_ref, kg_ref, gco_ref, gao_ref, bias_ref,
                  dproj_ref, dcw_ref, dqg_ref, dkg_ref, dgco_ref, dgao_ref, dsink_ref, dbias_ref,
                  ndcz_ref, dkc_ref, dvc_ref):
        cw = cw_ref[...]
        qg_v, kg_v, gco_v, gao_v = qg_ref[...], kg_ref[...], gco_ref[...], gao_ref[...]
        f = _mix_forward(*inputs, cw, qg_v, kg_v, gco_v, gao_v, sink_ref, bias_ref)
        dyc, dgco = _rms_bwd(dy[:, 0:CW], f["y_conv"], f["rc"], gco_v)
        dya, dgao = _rms_bwd(dy[:, CW:CW + AW], f["y_attn"], f["ra"], gao_v)
        dgco_ref[...] += dgco
        dgao_ref[...] += dgao

        row = f["row"]
        dgate_b = dyc * f["cz"]
        dcz = dyc * f["gate_b"]
        dcw_ref[0:1, :] += jnp.sum(dcz * f["z2"], axis=0, keepdims=True)
        dcw_ref[1:2, :] += jnp.sum(dcz * f["z1"], axis=0, keepdims=True)
        dcw_ref[2:3, :] += jnp.sum(dcz * f["z"], axis=0, keepdims=True)
        nxt = ndcz_ref[...]
        n0 = nxt[0:1, :]
        n1 = nxt[1:2, :]
        d1 = jnp.where(row == BLK - 1, n0, pltpu.roll(dcz, BLK - 1, 0))
        d2 = jnp.where(row == BLK - 1, n1, jnp.where(row == BLK - 2, n0, pltpu.roll(dcz, BLK - 2, 0)))
        dz = cw[2:3, :] * dcz + cw[1:2, :] * d1 + cw[0:1, :] * d2
        ndcz_ref[...] = dcz[0:8, :]
        dproj_ref[:, 0:CW] = dgate_b.astype(BF)
        dproj_ref[:, CW:2 * CW] = (dz * f["hc"]).astype(BF)
        dproj_ref[:, 2 * CW:3 * CW] = (dz * f["gate_c"]).astype(BF)

        scale = f["scale"]
        lane = lax.broadcasted_iota(jnp.int32, (1, 128), 1)
        dq_cols, dk_cols, dv_cols = [], [], []
        for kv in range(NKV):
            hd = f["heads"][kv]
            dO = jnp.concatenate([dya[:, (kv * GQ + g) * HD:(kv * GQ + g + 1) * HD] for g in range(GQ)], axis=0)
            delta = jnp.sum(dO * hd["O"], axis=-1, keepdims=True)
            dOb = dO.astype(BF)
            dP = _dot(dOb, hd["vb"], 1, 1)
            dS = hd["probs"] * (dP - delta)
            dsk = hd["psink"] * delta
            for g in range(GQ):
                h = kv * GQ + g
                tot = jnp.sum(dsk[g * BLK:(g + 1) * BLK, :], axis=0, keepdims=True)
                dsink_ref[...] -= jnp.where(lane == h, tot, 0.0)
            dbias_ref[kv * GQ * BLK:(kv + 1) * GQ * BLK, :] += dS
            dSs = (dS * scale).astype(BF)
            dQ = _dot(dSs, hd["kb"], 1, 0)
            dKb = _dot(dSs, hd["Q"], 0, 0)
            dVb = _dot(hd["probs"].astype(BF), dOb, 0, 0)
            dkn = dKb[BLK:, :] + dkc_ref[:, kv * HD:(kv + 1) * HD]
            dvn = dVb[BLK:, :] + dvc_ref[:, kv * HD:(kv + 1) * HD]
            dkc_ref[:, kv * HD:(kv + 1) * HD] = dKb[:BLK, :]
            dvc_ref[:, kv * HD:(kv + 1) * HD] = dVb[:BLK, :]
            dk_raw, dkg = _rms_bwd(dkn, hd["kb_raw"][BLK:, :], hd["rk"][BLK:, :], kg_v)
            dkg_ref[...] += dkg
            dk_cols.append(dk_raw)
            dv_cols.append(dvn)
            for g in range(GQ):
                dq_raw, dqg = _rms_bwd(dQ[g * BLK:(g + 1) * BLK, :], hd["q_raw"][g], hd["rq"][g], qg_v)
                dqg_ref[...] += dqg
                dq_cols.append(dq_raw)
        dproj_ref[:, 3 * CW:INW] = jnp.concatenate(dq_cols + dk_cols + dv_cols, axis=1).astype(BF)

    small = lambda r, c: pl.BlockSpec((r, c), lambda s: (0, 0))
    return _call(
        body, (sinks, proj, proj, proj, proj, dy, cw8, qg, kg, gco, gao, bias), name="mix_bwd", grid=(n_steps,),
        in_specs=_mix_in_specs(tile_of) + [pl.BlockSpec((TILE, D), lambda s: (tile_of(s), 0))] + _mix_param_specs(),
        out_specs=[pl.BlockSpec((TILE, INW), lambda s: (tile_of(s), 0)), small(8, CW), small(1, HD), small(1, HD),
                   small(1, CW), small(1, AW), small(1, 128), small(NH * BLK, 2 * BLK)],
        out_shape=[SDS((T, INW), BF), SDS((8, CW), F32), SDS((1, HD), F32), SDS((1, HD), F32), SDS((1, CW), F32),
                   SDS((1, AW), F32), SDS((1, 128), F32), SDS((NH * BLK, 2 * BLK), F32)],
        scratch_shapes=[pltpu.VMEM((8, CW), F32), pltpu.VMEM((BLK, NKV * HD), F32), pltpu.VMEM((BLK, NKV * HD), F32)],
        sem=("arbitrary",), vmem_mib=56, comm=comm)


FT = 256
NFT = DFF // FT
RC = 128
NCH = T // RC
LEAD = 16


def _rows8(x):
    return jnp.sum(x.reshape(x.shape[0] // 8, 8, x.shape[1]), axis=0)


def _ffn_act_specs():
    return [
        pl.BlockSpec((T, FT), lambda j: (0, j)), pl.BlockSpec((T, FT), lambda j: (0, NFT + j)),
        pl.BlockSpec((8, FT), lambda j: (0, j)), pl.BlockSpec((8, FT), lambda j: (0, NFT + j)),
        pl.BlockSpec((1, FT), lambda j: (0, j)), pl.BlockSpec((1, FT), lambda j: (0, NFT + j)),
    ]


def _conv_rows(win, w, b, n):
    win = win.astype(F32)
    u = win[LEAD:LEAD + n]
    u1 = pltpu.roll(win, 1, 0)[LEAD:LEAD + n]
    u2 = pltpu.roll(win, 2, 0)[LEAD:LEAD + n]
    return u2, u1, u, w[0:1, :] * u2 + w[1:2, :] * u1 + w[2:3, :] * u + b


def _ffn_act(up, fw8, fb):
    def body(ug_ref, uv_ref, wg_ref, wv_ref, bg_ref, bv_ref, a_ref):
        wg, wv, bg, bv = wg_ref[...], wv_ref[...], bg_ref[...], bv_ref[...]

        def chunk(win_g, win_v):
            gp = _conv_rows(win_g, wg, bg, RC)[3]
            vp = _conv_rows(win_v, wv, bv, RC)[3]
            return (gp * jax.nn.sigmoid(gp) * vp).astype(BF)

        zero = jnp.zeros((LEAD, FT), BF)
        a_ref[0:RC, :] = chunk(jnp.concatenate([zero, ug_ref[0:RC, :]], axis=0),
                               jnp.concatenate([zero, uv_ref[0:RC, :]], axis=0))

        def step(i, carry):
            r0 = pl.multiple_of(i * RC, RC)
            win = pl.ds(r0 - LEAD, RC + LEAD)
            a_ref[pl.ds(r0, RC), :] = chunk(ug_ref[win, :], uv_ref[win, :])
            return carry

        lax.fori_loop(1, NCH, step, 0)

    return _call(
        body, (up, up, fw8, fw8, fb, fb), name="ffn_act", grid=(NFT,), in_specs=_ffn_act_specs(),
        out_specs=[pl.BlockSpec((T, FT), lambda j: (0, j))], out_shape=[SDS((T, DFF), BF)],
        sem=("parallel",), vmem_mib=40)


def _ffn_act_bwd(up, da, fw8, fb, comm=()):
    ext = RC + LEAD

    def body(ug_ref, uv_ref, wg_ref, wv_ref, bg_ref, bv_ref, da_ref,
             dug_ref, duv_ref, dwg_ref, dwv_ref, dbg_ref, dbv_ref):
        wg, wv, bg, bv = wg_ref[...], wv_ref[...], bg_ref[...], bv_ref[...]

        def chunk(win_g, win_v, da_e):
            g2, g1, g0, gp = _conv_rows(win_g, wg, bg, ext)
            v2, v1, v0, vp = _conv_rows(win_v, wv, bv, ext)
            da_e = da_e.astype(F32)
            sig = jax.nn.sigmoid(gp)
            dvp = da_e * (gp * sig)
            dgp = da_e * vp * (sig * (1.0 + gp * (1.0 - sig)))

            def back(dp, w):
                return (w[2:3, :] * dp[0:RC] + w[1:2, :] * pltpu.roll(dp, ext - 1, 0)[0:RC]
                        + w[0:1, :] * pltpu.roll(dp, ext - 2, 0)[0:RC]).astype(BF)

            def sums(dp, u2, u1, u0):
                d = dp[0:RC]
                return [_rows8(d), _rows8(d * u2[0:RC]), _rows8(d * u1[0:RC]), _rows8(d * u0[0:RC])]

            return back(dgp, wg), back(dvp, wv), sums(dgp, g2, g1, g0) + sums(dvp, v2, v1, v0)

        zero = jnp.zeros((LEAD, FT), BF)
        dug, duv, acc = chunk(jnp.concatenate([zero, ug_ref[0:ext, :]], axis=0),
                              jnp.concatenate([zero, uv_ref[0:ext, :]], axis=0), da_ref[0:ext, :])
        dug_ref[0:RC, :] = dug
        duv_ref[0:RC, :] = duv

        def step(i, acc):
            r0 = pl.multiple_of(i * RC, RC)
            win = pl.ds(r0 - LEAD, ext + LEAD)
            dug, duv, part = chunk(ug_ref[win, :], uv_ref[win, :], da_ref[pl.ds(r0, ext), :])
            dug_ref[pl.ds(r0, RC), :] = dug
            duv_ref[pl.ds(r0, RC), :] = duv
            return [a + p for a, p in zip(acc, part)]

        acc = lax.fori_loop(1, NCH - 1, step, acc)
        r0 = T - RC
        tail = lambda ref, lo: jnp.concatenate([ref[lo:T, :], zero], axis=0)
        dug, duv, part = chunk(tail(ug_ref, r0 - LEAD), tail(uv_ref, r0 - LEAD), tail(da_ref, r0))
        dug_ref[r0:T, :] = dug
        duv_ref[r0:T, :] = duv
        tot = [jnp.sum(a + p, axis=0, keepdims=True) for a, p in zip(acc, part)]
        for k, (dw_ref, db_ref) in enumerate(((dwg_ref, dbg_ref), (dwv_ref, dbv_ref))):
            db_ref[...] = tot[4 * k]
            dw_ref[...] = jnp.zeros_like(dw_ref)
            for r in range(3):
                dw_ref[r:r + 1, :] = tot[4 * k + 1 + r]

    col = lambda r: pl.BlockSpec((r, FT), lambda j: (0, j))
    return _call(
        body, (up, up, fw8, fw8, fb, fb, da), name="ffn_act_bwd", grid=(NFT,),
        in_specs=_ffn_act_specs() + [pl.BlockSpec((T, FT), lambda j: (0, j))],
        out_specs=[col(T), col(T), col(8), col(8), col(1), col(1)],
        out_shape=[SDS((T, DFF), BF), SDS((T, DFF), BF), SDS((8, DFF), F32), SDS((8, DFF), F32),
                   SDS((1, DFF), F32), SDS((1, DFF), F32)],
        sem=("parallel",), vmem_mib=40, comm=comm)


def _ffn_down_bwd(dh2b, w_down, comm=()):
    tm = TM

    def body(d_ref, w_ref, o_ref):
        o_ref[...] = _dot(d_ref[...], w_ref[...], 1, 1).astype(BF)

    return _call(
        body, (dh2b, w_down), name="ffn_down_bwd", grid=(T // tm,),
        in_specs=[pl.BlockSpec((tm, D), lambda i: (i, 0)), _resident((DFF, D))],
        out_specs=[pl.BlockSpec((tm, DFF), lambda i: (i, 0))], out_shape=[SDS((T, DFF), BF)],
        sem=("parallel",), vmem_mib=40, comm=comm)


def _norm_matmul_bwd(name, a_list, w_t, k_offsets, xin, g, dres, want_bf16, comm=(), slot=None):
    tm = TM
    ks = [a.shape[1] for a in a_list]
    n_a = len(a_list)
    n_pre = 0 if slot is None else 1

    def body(*refs):
        refs = refs[n_pre:]
        a_refs = refs[:n_a]
        w_ref, x_ref, g_ref, r_ref = refs[n_a:n_a + 4]
        outs = refs[n_a + 4:]
        dx_ref, dg_ref = outs[0], (outs[-1] if slot is None else outs[-1].at[0])

        @pl.when(pl.program_id(0) == 0)
        def _():
            dg_ref[...] = jnp.zeros_like(dg_ref)

        du = _dot(a_refs[0][...], w_ref[k_offsets[0]:k_offsets[0] + ks[0], :], 1, 0)
        for k in range(1, n_a):
            du = du + _dot(a_refs[k][...], w_ref[k_offsets[k]:k_offsets[k] + ks[k], :], 1, 0)
        x = x_ref[...]
        r = lax.rsqrt(jnp.mean(x * x, axis=-1, keepdims=True) + EPS)
        dx, dg = _rms_bwd(du, x, r, g_ref[...])
        dx = r_ref[...] + dx
        dx_ref[...] = dx
        if want_bf16:
            outs[1][...] = dx.astype(BF)
        dg_ref[...] += dg

    tile = lambda c: pl.BlockSpec((tm, c), lambda i, *_: (i, 0))
    if slot is None:
        dg_spec, dg_shape = pl.BlockSpec((1, D), lambda i: (0, 0)), SDS((1, D), F32)
    else:
        dg_spec, dg_shape = pl.BlockSpec((1, 1, D), lambda i, slot_ref: (slot_ref[0], 0, 0)), SDS((N_DEV, 1, D), F32)
    out_specs = [tile(D)] + ([tile(D)] if want_bf16 else []) + [dg_spec]
    out_shape = [SDS((T, D), F32)] + ([SDS((T, D), BF)] if want_bf16 else []) + [dg_shape]
    return _call(
        body, (*a_list, w_t, xin, g, dres), name=name, grid=(T // tm,), prefetch=() if slot is None else (slot,),
        in_specs=[tile(k) for k in ks] + [_resident(w_t.shape), tile(D),
                                           pl.BlockSpec((1, D), lambda i, *_: (0, 0)), tile(D)],
        out_specs=out_specs, out_shape=out_shape, sem=("arbitrary",), vmem_mib=56, comm=comm)


def _out_bwd(dh1b, w_out):
    tm = TM

    def body(d_ref, w_ref, o_ref):
        o_ref[...] = _dot(d_ref[...], w_ref[...], 1, 1)

    return _call(
        body, (dh1b, w_out), name="out_bwd", grid=(T // tm,),
        in_specs=[pl.BlockSpec((tm, D), lambda i: (i, 0)), _resident((D, D))],
        out_specs=[pl.BlockSpec((tm, D), lambda i: (i, 0))], out_shape=[SDS((T, D), F32)],
        sem=("parallel",), vmem_mib=32)


def _wgrad(name, a_list, b, comm=()):
    m_k = a_list[0].shape[1]
    tm = max(t for t in range(128, m_k // 2 + 1, 128) if m_k % t == 0)
    steps = [a.shape[1] // tm for a in a_list]
    starts = [sum(steps[:k]) for k in range(len(a_list))]
    n_a = len(a_list)

    def body(*refs):
        a_refs, b_ref, o_ref = refs[:n_a], refs[n_a], refs[n_a + 1]
        i = pl.program_id(0)
        for k in range(n_a):
            @pl.when((i >= starts[k]) & (i < starts[k] + steps[k]))
            def _(k=k):
                o_ref[...] = _dot(a_refs[k][...], b_ref[...], 0, 0).astype(BF)

    def a_spec(k):
        return pl.BlockSpec((T, tm), lambda i: (0, jnp.clip(i - starts[k], 0, steps[k] - 1)))

    m_total = tm * sum(steps)
    return _call(
        body, (*a_list, b), name=name, grid=(sum(steps),),
        in_specs=[a_spec(k) for k in range(n_a)] + [_resident((T, D))],
        out_specs=[pl.BlockSpec((tm, D), lambda i: (i, 0))], out_shape=[SDS((m_total, D), BF)],
        sem=("parallel",), vmem_mib=40, comm=comm)


def _chip_sum(name, gbf, from_sib, core, chip):
    h = gbf.shape[1]
    th = h // 2

    def body(core_ref, chip_ref, g_ref, s_ref, pbf_ref, own_ref):
        p = g_ref[0].astype(F32) + s_ref[0].astype(F32)
        pbf_ref[0] = p.astype(BF)

        @pl.when(pl.program_id(1) == chip_ref[0])
        def _():
            own_ref[...] = p

    grid_spec = pltpu.PrefetchScalarGridSpec(
        num_scalar_prefetch=2, grid=(h // th, N_CHIPS),
        in_specs=[pl.BlockSpec((1, th, D), lambda t, jj, core_ref, chip_ref: (2 * jj + core_ref[0], t, 0)),
                  pl.BlockSpec((1, th, D), lambda t, jj, core_ref, chip_ref: (jj, t, 0))],
        out_specs=[pl.BlockSpec((1, th, D), lambda t, jj, core_ref, chip_ref: (jj, t, 0)),
                   pl.BlockSpec((th, D), lambda t, jj, core_ref, chip_ref: (t, 0))],
    )
    return _pcall(
        body, name=name, grid_spec=grid_spec, out_shape=_in_hbm([SDS((N_CHIPS, h, D), BF), SDS((h, D), F32)]),
        compiler_params=_params(("arbitrary", "arbitrary"), 32),
    )(core, chip, *_from_hbm(gbf, from_sib))


def _final_sum(name, own, from_chips, core):
    h = own.shape[0]

    def body(core_ref, o_ref, r_ref, f_ref):
        f_ref[0] = ((o_ref[...] + r_ref[0].astype(F32)) + r_ref[1].astype(F32)) + r_ref[2].astype(F32)

    grid_spec = pltpu.PrefetchScalarGridSpec(
        num_scalar_prefetch=1, grid=(1,),
        in_specs=[pl.BlockSpec((h, D), lambda i, core_ref: (0, 0)), pl.BlockSpec((3, h, D), lambda i, core_ref: (0, 0, 0))],
        out_specs=pl.BlockSpec((1, h, D), lambda i, core_ref: (core_ref[0], 0, 0)),
    )
    return _pcall(body, name=name, grid_spec=grid_spec, out_shape=pltpu.HBM((2, h, D), F32),
                  compiler_params=_params(("arbitrary",), 40))(core, *_from_hbm(own, from_chips))


def _adam_math(w, g, m, v):
    nm = ADAM_B1 * m + (1.0 - ADAM_B1) * g
    nv = ADAM_B2 * v + (1.0 - ADAM_B2) * (g * g)
    m_hat = nm / (1.0 - ADAM_B1 ** ADAM_STEP)
    v_hat = nv / (1.0 - ADAM_B2 ** ADAM_STEP)
    return -ADAM_LR * (m_hat / (jnp.sqrt(v_hat) + ADAM_EPS) + ADAM_WD * w), nm, nv


def _adamw(name, w, g, m, v, tr, copy_g=False):
    rows, cols = w.shape

    def body(w_ref, g_ref, m_ref, v_ref, *outs):
        g_val = g_ref[...]
        if copy_g:
            outs[0][...] = g_val
        d_ref, nm_ref, nv_ref = outs[-3:]
        d_ref[...], nm_ref[...], nv_ref[...] = _adam_math(w_ref[...], g_val, m_ref[...], v_ref[...])

    spec = pl.BlockSpec((tr, cols), lambda i: (i, 0))
    n_out = 4 if copy_g else 3
    return _call(body, (w, g, m, v), name=name, grid=(rows // tr,), in_specs=[spec] * 4, out_specs=[spec] * n_out,
                 out_shape=[SDS((rows, cols), F32)] * n_out, sem=("parallel",), vmem_mib=32, free=(0, 2, 3))


C_G1, C_G2, C_GCO, C_GAO, C_DCW, C_DQG, C_DKG, C_SINK, C_SQ = 0, 1024, 2048, 2560, 3072, 4608, 4736, 4864, 5632
P_W = C_SQ + 128


def _pack_small(me, dfwg, dfwv, dfbg, dfbv, dg2, dgco, dgao, dcw8, dqg, dkg, dsink, sq):
    def body(me_ref, dfwg_r, dfwv_r, dfbg_r, dfbv_r, dg2_r, dgco_r, dgao_r, dcw_r, dqg_r, dkg_r, dsink_r, sq_r, o):
        o[...] = jnp.zeros_like(o)
        o[0, :, 0:DFF] = dfwg_r[...]
        o[0, :, DFF:2 * DFF] = dfwv_r[...]
        o[0, 3:4, 0:DFF] = dfbg_r[...]
        o[0, 3:4, DFF:2 * DFF] = dfbv_r[...]
        o[0, 4:5, C_G2:C_G2 + D] = dg2_r[...]
        o[0, 4:5, C_GCO:C_GCO + CW] = dgco_r[...]
        o[0, 4:5, C_GAO:C_GAO + AW] = dgao_r[...]
        for r in range(3):
            o[0, 4:5, C_DCW + r * CW:C_DCW + (r + 1) * CW] = dcw_r[r:r + 1, :]
        o[0, 4:5, C_DQG:C_DQG + HD] = dqg_r[...]
        o[0, 4:5, C_DKG:C_DKG + HD] = dkg_r[...]
        o[0, 4:5, C_SINK:C_SINK + 128] = dsink_r[...]
        o[0, :, C_SQ:C_SQ + 128] = sq_r[...]

    ins = (dfwg, dfwv, dfbg, dfbv, dg2, dgco, dgao, dcw8, dqg, dkg, dsink, sq)
    return _call(body, ins, name="pack_small", grid=(1,), prefetch=(me,),
                 in_specs=[pl.BlockSpec(a.shape, lambda i, me_ref: (0, 0)) for a in ins],
                 out_specs=[pl.BlockSpec((1, 8, P_W), lambda i, me_ref: (me_ref[0], 0, 0))],
                 out_shape=[SDS((N_DEV, 8, P_W), F32)], sem=("arbitrary",))[0]


N_SMALL = 11


def _small_adam(chip, p_all, g1_all, tbl_all, ws, ms, vs):
    fw_cols = 2 * DFF // N_CHIPS
    cw_cols = CW // N_CHIPS

    def body(chip_ref, p_ref, fw_ref, cw0_ref, cw1_ref, cw2_ref, g1_ref, tbl_ref, *refs):
        w_r, m_r, v_r = refs[0:N_SMALL], refs[N_SMALL:2 * N_SMALL], refs[2 * N_SMALL:3 * N_SMALL]
        outs = refs[3 * N_SMALL:]
        g_o, d_o, nm_o, nv_o = (outs[k * N_SMALL:(k + 1) * N_SMALL] for k in range(4))
        loss_o = outs[4 * N_SMALL]

        def total(ref):
            s = ref[0]
            for k in range(1, N_DEV):
                s = s + ref[k]
            return s

        S = total(p_ref)
        fw = total(fw_ref)
        cws = [total(r) for r in (cw0_ref, cw1_ref, cw2_ref)]

        def step(i, g, at):
            d, nm, nv = _adam_math(w_r[i][at], g, m_r[i][at], v_r[i][at])
            g_o[i][at], d_o[i][at], nm_o[i][at], nv_o[i][at] = g, d, nm, nv

        everything = (slice(None), slice(None))
        step(0, total(g1_ref), everything)
        for r in range(3):
            step(1, cws[r][4:5, :], (0, slice(r, r + 1), slice(None)))
        step(2, S[4:5, C_DQG:C_DQG + HD], everything)
        step(3, S[4:5, C_DKG:C_DKG + HD], everything)
        step(4, total(tbl_ref), everything)
        step(5, S[4:5, C_SINK:C_SINK + NH], everything)
        step(6, S[4:5, C_GCO:C_GCO + CW], everything)
        step(7, S[4:5, C_GAO:C_GAO + AW], everything)
        step(8, S[4:5, C_G2:C_G2 + D], everything)
        step(9, fw[0:3, :], (0, slice(None), slice(None)))
        step(10, S[3:4, 0:2 * DFF], everything)
        sq = S[:, C_SQ:C_SQ + 128]
        loss_o[...] = jnp.sum(jnp.sum(sq, axis=1, keepdims=True), axis=0, keepdims=True) * (0.5 / D)

    def full(a):
        n = len(a.shape)
        return pl.BlockSpec(a.shape, lambda i, chip_ref: (0,) * n)

    params = [*ws, *ms, *vs]
    out = _call(
        body, (p_all, p_all, p_all, p_all, p_all, g1_all, tbl_all, *params), name="small_adam", grid=(1,), prefetch=(chip,),
        in_specs=[full(p_all),
                  pl.BlockSpec((N_DEV, 8, fw_cols), lambda i, chip_ref: (0, 0, chip_ref[0])),
                  *[pl.BlockSpec((N_DEV, 8, cw_cols), lambda i, chip_ref, r=r: (0, 0, (C_DCW + r * CW) // cw_cols + chip_ref[0]))
                    for r in range(3)],
                  full(g1_all), full(tbl_all), *[full(a) for a in params]],
        out_specs=[full(a) for a in ws] * 4 + [pl.BlockSpec((1, 1), lambda i, chip_ref: (0, 0))],
        out_shape=[SDS(a.shape, F32) for a in ws] * 4 + [SDS((1, 1), F32)], sem=("arbitrary",), vmem_mib=32)
    return out[0:N_SMALL], out[N_SMALL:2 * N_SMALL], out[2 * N_SMALL:3 * N_SMALL], out[3 * N_SMALL:4 * N_SMALL], out[4 * N_SMALL]


PLACE_STEPS = 4


def _place_specs(shards):
    rows = [s.shape[0] // PLACE_STEPS for s in shards]
    return ([pl.BlockSpec((r, D), lambda i, chip_ref: (i, 0)) for r in rows],
            [pl.BlockSpec((r, D), lambda i, chip_ref: (chip_ref[0] * PLACE_STEPS + i, 0)) for r in rows],
            [SDS((N_CHIPS * s.shape[0], D), BF) for s in shards])


def _place_first(chip, shard, conv_w, ffn_conv_w):
    def body(chip_ref, a, s0, s1, o, t0, t1):
        o[...] = a[...].astype(BF)

        @pl.when(pl.program_id(0) == 0)
        def _():
            for s, t in ((s0, t0), (s1, t1)):
                t[...] = jnp.zeros_like(t)
                t[0, 0:3, :] = s[...]

    ins, outs, shapes = _place_specs([shard])
    taps = (conv_w, ffn_conv_w)
    return _call(
        body, (shard, conv_w, ffn_conv_w), name="place_first", grid=(PLACE_STEPS,), prefetch=(chip,),
        in_specs=ins + [pl.BlockSpec(s.shape, lambda i, chip_ref: (0, 0)) for s in taps],
        out_specs=outs + [pl.BlockSpec((1, 8, s.shape[1]), lambda i, chip_ref: (chip_ref[0], 0, 0)) for s in taps],
        out_shape=shapes + [SDS((N_CHIPS, 8, s.shape[1]), F32) for s in taps],
        sem=("arbitrary",), vmem_mib=32, free=(0, 1, 2))


def _place_rest(chip, shards, table, bucket, comm):
    n = len(shards)

    def body(chip_ref, *refs):
        a, (tab_ref, bk_ref), o, bias_ref = refs[:n], refs[n:n + 2], refs[n + 2:2 * n + 2], refs[2 * n + 2]
        for src, dst in zip(a, o):
            dst[...] = src[...].astype(BF)

        @pl.when(pl.program_id(0) == 0)
        def _():
            bk = bk_ref[...]
            eq = [bk == b for b in range(NBUCKET)]
            for h in range(NH):
                acc = jnp.zeros((BLK, 2 * BLK), F32)
                for b in range(NBUCKET):
                    acc = jnp.where(eq[b], tab_ref[h, b], acc)
                bias_ref[h * BLK:(h + 1) * BLK, :] = acc

    ins, outs, shapes = _place_specs(shards)
    return _call(
        body, (*shards, table, bucket), name="place_rest", grid=(PLACE_STEPS,), prefetch=(chip,),
        in_specs=ins + [pl.BlockSpec(memory_space=pltpu.SMEM), pl.BlockSpec(bucket.shape, lambda i, chip_ref: (0, 0))],
        out_specs=outs + [pl.BlockSpec((NH * BLK, 2 * BLK), lambda i, chip_ref: (0, 0))],
        out_shape=shapes + [SDS((NH * BLK, 2 * BLK), F32)],
        sem=("arbitrary",), vmem_mib=32, comm=comm, free=tuple(range(n + 2)))


def kernel(x, norm_mix_g, w_in, conv_w, q_norm_g, k_norm_g, rel_bias_table, sinks, out_norm_conv_g, out_norm_attn_g, w_out, norm_ffn_g, w_up, ffn_conv_w, ffn_conv_b, w_down, loss_target, m_norm_mix_g, m_w_in, m_conv_w, m_q_norm_g, m_k_norm_g, m_rel_bias_table, m_sinks, m_out_norm_conv_g, m_out_norm_attn_g, m_w_out, m_norm_ffn_g, m_w_up, m_ffn_conv_w, m_ffn_conv_b, m_w_down, v_norm_mix_g, v_w_in, v_conv_w, v_q_norm_g, v_k_norm_g, v_rel_bias_table, v_sinks, v_out_norm_conv_g, v_out_norm_attn_g, v_w_out, v_norm_ffn_g, v_w_up, v_ffn_conv_w, v_ffn_conv_b, v_w_down):
    as_arg = lambda i: jnp.reshape(i, (1,)).astype(jnp.int32)
    chip = as_arg(2 * lax.axis_index("x") + lax.axis_index("y"))
    core = as_arg(lax.axis_index("c"))
    me = 2 * chip + core
    xs, tgt = x[0], loss_target[0]
    qg, kg, gco, gao, g1, g2, fb = q_norm_g, k_norm_g, out_norm_conv_g, out_norm_attn_g, norm_mix_g, norm_ffn_g, ffn_conv_b
    pieces = lambda g: g.reshape(N_DEV, g.shape[0] // N_DEV, D)
    whole = lambda f: f.reshape(2 * f.shape[1], D)

    bucket = jnp.asarray(_bucket_table())
    p_in, p_cw, p_fw = _place_first(chip, w_in[0].T, conv_w[0], ffn_conv_w[0])
    p_out, p_up, p_down, bias, w_int, cw_all, fw_all = _place_rest(
        chip, [w_out[0], w_up[0].T, w_down[0]], rel_bias_table.T, bucket,
        comm=[_t_gather(p_in), _t_small_weights(p_cw), _t_small_weights(p_fw)])
    cw8 = jnp.transpose(cw_all, (1, 0, 2)).reshape(8, CW)
    fw8 = jnp.transpose(fw_all, (1, 0, 2)).reshape(8, 2 * DFF)

    proj, u1, w_out_f = _inproj(xs, g1, w_int, comm=[_t_gather(p_out)])
    y, w_upt = _mix_fwd(proj, sinks, cw8, qg, kg, gco, gao, bias, comm=[_t_gather(p_up)])
    h1, u2 = _outproj(y, w_out_f, xs, g2)
    up, w_down_f = _ffn_up(u2, w_upt, comm=[_t_gather(p_down)])
    a, = _ffn_act(up, fw8, fb)
    dh2, dh2b, sq = _ffn_down(a, w_down_f, h1, tgt)

    gdbf, = _wgrad("wgrad_down", [a], dh2b)
    da, sib_down = _ffn_down_bwd(dh2b, w_down_f, comm=[_t_sibling(pieces(gdbf))])
    pbf_down, own_down = _chip_sum("chip_sum_w_down", pieces(gdbf), sib_down, core, chip)
    dug, duv, dfwg, dfwv, dfbg, dfbv, chips_down = _ffn_act_bwd(up, da, fw8, fb, comm=[_t_chips(pbf_down)])
    fin_down = _final_sum("final_sum_w_down", own_down, chips_down, core)
    gubf, = _wgrad("wgrad_up", [dug, duv], u2)
    dh1, dh1b, dg2, sib_up, fin_down = _norm_matmul_bwd(
        "ffn_up_bwd", [dug, duv], w_upt, [0, DFF], h1, g2, dh2, True, comm=[_t_sibling(pieces(gubf)), _t_swap(fin_down)])
    pbf_up, own_up = _chip_sum("chip_sum_w_up", pieces(gubf), sib_up, core, chip)
    gobf, = _wgrad("wgrad_out", [y], dh1b)
    dy, = _out_bwd(dh1b, w_out_f)
    dproj, dcw8, dqg, dkg, dgco, dgao, dsink, dbias, chips_up, sib_out = _mix_bwd(
        proj, dy, sinks, cw8, qg, kg, gco, gao, bias, comm=[_t_chips(pbf_up), _t_sibling(pieces(gobf))])
    fin_up = _final_sum("final_sum_w_up", own_up, chips_up, core)
    pbf_out, own_out = _chip_sum("chip_sum_w_out", pieces(gobf), sib_out, core, chip)
    tbl_all = _band_bias_bwd(dbias, bucket, me)
    p_all = _pack_small(me, dfwg, dfwv, dfbg, dfbv, dg2, dgco, dgao, dcw8, dqg, dkg, dsink, sq)
    gibf, chips_out, fin_up, p_all, tbl_all = _wgrad(
        "wgrad_in", [dproj], u1, comm=[_t_chips(pbf_out), _t_swap(fin_up), _t_allgather(p_all), _t_allgather(tbl_all)])
    fin_out = _final_sum("final_sum_w_out", own_out, chips_out, core)
    sib_in, fin_out = _comm_call("to_sibling_last", [_t_sibling(pieces(gibf)), _t_swap(fin_out)])
    pbf_in, own_in = _chip_sum("chip_sum_w_in", pieces(gibf), sib_in, core, chip)
    dx, g1_all, chips_in = _norm_matmul_bwd(
        "in_bwd", [dproj], w_int, [0], xs, g1, dh1, False, comm=[_t_chips(pbf_in)], slot=me)
    fin_in = _final_sum("final_sum_w_in", own_in, chips_in, core)
    g1_all, fin_in = _comm_call("gather_last", [_t_allgather(g1_all), _t_swap(fin_in)])

    g_w_in, g_w_out, g_w_up, g_w_down = whole(fin_in).T, whole(fin_out), whole(fin_up).T, whole(fin_down)
    g_w_down, d_down, nm_down, nv_down = _adamw("adamw_w_down", w_down[0], g_w_down, m_w_down[0], v_w_down[0], 352, True)
    d_up, nm_up, nv_up = _adamw("adamw_w_up", w_up[0], g_w_up, m_w_up[0], v_w_up[0], 256)
    g_w_out, d_out, nm_out, nv_out = _adamw("adamw_w_out", w_out[0], g_w_out, m_w_out[0], v_w_out[0], 256, True)
    d_in, nm_in, nv_in = _adamw("adamw_w_in", w_in[0], g_w_in, m_w_in[0], v_w_in[0], 256)
    sw = [norm_mix_g, conv_w, q_norm_g, k_norm_g, rel_bias_table.T, sinks, out_norm_conv_g, out_norm_attn_g,
          norm_ffn_g, ffn_conv_w, ffn_conv_b]
    smm = [m_norm_mix_g, m_conv_w, m_q_norm_g, m_k_norm_g, m_rel_bias_table.T, m_sinks, m_out_norm_conv_g,
           m_out_norm_attn_g, m_norm_ffn_g, m_ffn_conv_w, m_ffn_conv_b]
    smv = [v_norm_mix_g, v_conv_w, v_q_norm_g, v_k_norm_g, v_rel_bias_table.T, v_sinks, v_out_norm_conv_g,
           v_out_norm_attn_g, v_norm_ffn_g, v_ffn_conv_w, v_ffn_conv_b]
    *small_out, loss = _small_adam(chip, p_all, g1_all, tbl_all, sw, smm, smv)
    sg, sd, snm, snv = [list(r) for r in small_out]
    for r in (sg, sd, snm, snv):
        r[4] = r[4].T

    def order(s, b_in, b_out, b_up, b_down):
        return (s[0], b_in[None], s[1], s[2], s[3], s[4], s[5], s[6], s[7], b_out[None], s[8], b_up[None],
                s[9], s[10], b_down[None])

    return (loss.reshape(()), dx[None],
            *order(sg, g_w_in, g_w_out, g_w_up, g_w_down),
            *order(sd, d_in, d_out, d_up, d_down),
            *order(snm, nm_in, nm_out, nm_up, nm_down),
            *order(snv, nv_in, nv_out, nv_up, nv_down))
```

```python
import functools
import math

import numpy as np

import jax
import jax.numpy as jnp
from jax import lax
from jax.experimental import pallas as pl
from jax.experimental.pallas import tpu as pltpu

F32 = jnp.float32
BF = jnp.bfloat16
SDS = jax.ShapeDtypeStruct

T = 2048
D = 1024
CW = 512
AW = 512
HD = 64
NH = 8
NKV = 2
GQ = 4
INW = 2304
DFF = 2816
BLK = 128
NB = T // BLK
NBUCKET = 32
EPS = 1e-6
NEG_INF = -1e30
N_CHIPS = 4
N_DEV = 8

ADAM_LR = 0.001
ADAM_B1 = 0.9
ADAM_B2 = 0.999
ADAM_EPS = 1e-08
ADAM_WD = 0.01
ADAM_STEP = 10

TM = 512
MIB = 1024 * 1024
MESH = pl.DeviceIdType.MESH
ANY = pl.BlockSpec(memory_space=pl.ANY)

_pcall = pl.pallas_call


def _params(sem=None, vmem_mib=None):
    kw = {}
    if sem is not None:
        kw["dimension_semantics"] = sem
    if vmem_mib is not None:
        kw["vmem_limit_bytes"] = vmem_mib * MIB
    return pltpu.CompilerParams(**kw)


def _resident(shape):
    return pl.BlockSpec(shape, lambda *_: (0,) * len(shape), pipeline_mode=pl.Buffered(1))


def _dot(a, b, ca, cb):
    return lax.dot_general(a, b, (((ca,), (cb,)), ((), ())), preferred_element_type=F32)


def _rms_bwd(dy, x, r, g):
    dg = jnp.sum(dy * (x * r), axis=0, keepdims=True)
    dgx = dy * g
    dx = r * dgx - x * (r * r * r) * jnp.mean(x * dgx, axis=-1, keepdims=True)
    return dx, dg


def _where():
    x, y, c = lax.axis_index("x"), lax.axis_index("y"), lax.axis_index("c")
    return x, y, c, [(1 - x, y), (x, 1 - y), (1 - x, 1 - y)]


def _rcopy(src, dst, ssem, rsem, dev):
    return pltpu.make_async_remote_copy(src_ref=src, dst_ref=dst, send_sem=ssem, recv_sem=rsem, device_id=dev,
                                        device_id_type=MESH)


class _Task:
    def __init__(self, ins, outs, alias, n_sem, start, finish):
        self.ins, self.outs, self.alias, self.n_sem, self.start, self.finish = ins, outs, alias, n_sem, start, finish


def _t_gather(placed):
    R = placed.shape[0] // N_CHIPS

    def rows(chip_index, core):
        return pl.ds(pl.multiple_of(chip_index * R + core * (R // 2), 16), R // 2)

    def start(cin, cout, ss, rs, b):
        x, y, c, chips = _where()
        mine = cout[0].at[rows(2 * x + y, c)]
        for r, (px, py) in enumerate(chips):
            _rcopy(mine, mine, ss.at[b + r], rs.at[b + r], (px, py, c)).start()

    def finish(cin, cout, ss, rs, b):
        x, y, c, chips = _where()
        buf = cout[0]
        sib = (x, y, 1 - c)
        for r, (px, py) in enumerate(chips):
            got = buf.at[rows(2 * px + py, c)]
            _rcopy(got, got, ss.at[b + r], rs.at[b + r], (px, py, c)).wait_recv()
            _rcopy(got, got, ss.at[b + 3 + r], rs.at[b + 3 + r], sib).start()
        for r, (px, py) in enumerate(chips):
            got = buf.at[rows(2 * px + py, 1 - c)]
            _rcopy(got, got, ss.at[b + 3 + r], rs.at[b + 3 + r], sib).wait_recv()
        mine = buf.at[rows(2 * x + y, c)]
        for r in range(6):
            _rcopy(mine, mine, ss.at[b + r], rs.at[b + r], sib).wait_send()

    return _Task([placed], [SDS(placed.shape, placed.dtype)], [(0, 0)], 6, start, finish)


def _t_small_weights(buf):
    def start(cin, cout, ss, rs, b):
        x, y, c, chips = _where()
        mine = cout[0].at[2 * x + y]
        for r, (px, py) in enumerate(chips):
            _rcopy(mine, mine, ss.at[b + r], rs.at[b + r], (px, py, c)).start()

    def finish(cin, cout, ss, rs, b):
        x, y, c, chips = _where()
        for r, (px, py) in enumerate(chips):
            got = cout[0].at[2 * px + py]
            _rcopy(got, got, ss.at[b + r], rs.at[b + r], (px, py, c)).wait_recv()
        for r, (px, py) in enumerate(chips):
            mine = cout[0].at[2 * x + y]
            _rcopy(mine, mine, ss.at[b + r], rs.at[b + r], (px, py, c)).wait_send()

    return _Task([buf], [SDS(buf.shape, buf.dtype)], [(0, 0)], 3, start, finish)


def _t_sibling(gbf):
    def start(cin, cout, ss, rs, b):
        x, y, c, _ = _where()
        for jj in range(N_CHIPS):
            _rcopy(cin[0].at[2 * jj + (1 - c)], cout[0].at[jj], ss.at[b + jj], rs.at[b + jj], (x, y, 1 - c)).start()

    def finish(cin, cout, ss, rs, b):
        x, y, c, _ = _where()
        for jj in range(N_CHIPS):
            got = cout[0].at[jj]
            _rcopy(got, got, ss.at[b + jj], rs.at[b + jj], (x, y, 1 - c)).wait_recv()
        for jj in range(N_CHIPS):
            got = cout[0].at[jj]
            _rcopy(got, got, ss.at[b + jj], rs.at[b + jj], (x, y, 1 - c)).wait_send()

    return _Task([gbf], [SDS((N_CHIPS,) + gbf.shape[1:], BF)], [], N_CHIPS, start, finish)


def _t_chips(pbf):
    def start(cin, cout, ss, rs, b):
        x, y, c, chips = _where()
        for r, (px, py) in enumerate(chips):
            _rcopy(cin[0].at[2 * px + py], cout[0].at[r], ss.at[b + r], rs.at[b + r], (px, py, c)).start()

    def finish(cin, cout, ss, rs, b):
        x, y, c, chips = _where()
        for r, (px, py) in enumerate(chips):
            got = cout[0].at[r]
            _rcopy(got, got, ss.at[b + r], rs.at[b + r], (px, py, c)).wait_recv()
        for r, (px, py) in enumerate(chips):
            got = cout[0].at[r]
            _rcopy(got, got, ss.at[b + r], rs.at[b + r], (px, py, c)).wait_send()

    return _Task([pbf], [SDS((3,) + pbf.shape[1:], BF)], [], 3, start, finish)


def _t_swap(fin):
    def start(cin, cout, ss, rs, b):
        x, y, c, _ = _where()
        mine = cout[0].at[c]
        _rcopy(mine, mine, ss.at[b], rs.at[b], (x, y, 1 - c)).start()

    def finish(cin, cout, ss, rs, b):
        x, y, c, _ = _where()
        got = cout[0].at[1 - c]
        _rcopy(got, got, ss.at[b], rs.at[b], (x, y, 1 - c)).wait_recv()
        _rcopy(got, got, ss.at[b], rs.at[b], (x, y, 1 - c)).wait_send()

    return _Task([fin], [SDS(fin.shape, fin.dtype)], [(0, 0)], 1, start, finish)


def _t_allgather(buf):
    def peers():
        x, y, c, _ = _where()
        out = []
        for rel in range(1, N_DEV):
            px, py, pc = x ^ ((rel >> 2) & 1), y ^ ((rel >> 1) & 1), c ^ (rel & 1)
            out.append((rel - 1, 4 * px + 2 * py + pc, (px, py, pc)))
        return 4 * x + 2 * y + c, out

    def start(cin, cout, ss, rs, b):
        me, ps = peers()
        mine = cout[0].at[me]
        for k, _, dev in ps:
            _rcopy(mine, mine, ss.at[b + k], rs.at[b + k], dev).start()

    def finish(cin, cout, ss, rs, b):
        me, ps = peers()
        for k, pidx, dev in ps:
            got = cout[0].at[pidx]
            _rcopy(got, got, ss.at[b + k], rs.at[b + k], dev).wait_recv()
        for k, _, dev in ps:
            mine = cout[0].at[me]
            _rcopy(mine, mine, ss.at[b + k], rs.at[b + k], dev).wait_send()

    return _Task([buf], [SDS(buf.shape, buf.dtype)], [(0, 0)], N_DEV - 1, start, finish)


def _run_tasks(comm, which, cin, cout, ss, rs):
    i0 = o0 = s0 = 0
    for t in comm:
        getattr(t, which)(cin[i0:i0 + len(t.ins)], cout[o0:o0 + len(t.outs)], ss, rs, s0)
        i0, o0, s0 = i0 + len(t.ins), o0 + len(t.outs), s0 + t.n_sem


def _from_hbm(*arrays):
    return [pltpu.with_memory_space_constraint(a, pltpu.HBM) for a in arrays]


def _in_hbm(shapes):
    return [pltpu.HBM(s.shape, s.dtype) for s in shapes]


def _comm_layout(comm, n_in, n_out):
    c_in = [a for t in comm for a in t.ins]
    c_out = [s for t in comm for s in t.outs]
    aliases, i0, o0 = {}, 0, 0
    for t in comm:
        for i, o in t.alias:
            aliases[n_in + i0 + i] = n_out + o0 + o
        i0, o0 = i0 + len(t.ins), o0 + len(t.outs)
    return c_in, c_out, aliases, sum(t.n_sem for t in comm)


def _call(body, operands, *, name, grid, in_specs, out_specs, out_shape, scratch_shapes=(), sem=None, vmem_mib=None, comm=(),
          free=(), prefetch=()):
    operands = [o if s.memory_space == pltpu.SMEM or k in free else pltpu.with_memory_space_constraint(o, pltpu.HBM)
                for k, (o, s) in enumerate(zip(operands, in_specs))]
    n_pre, n_in, n_out, n_scr = len(prefetch), len(in_specs), len(out_specs), len(scratch_shapes)
    c_in, c_out, aliases, n_sem = _comm_layout(comm, n_pre + n_in, n_out)
    sems = [pltpu.SemaphoreType.DMA((n_sem,)), pltpu.SemaphoreType.DMA((n_sem,))] if comm else []

    def wrapped(*refs):
        pre, refs = refs[:n_pre], refs[n_pre:]
        ins, cin = refs[:n_in], refs[n_in:n_in + len(c_in)]
        rest = refs[n_in + len(c_in):]
        outs, cout = rest[:n_out], rest[n_out:n_out + len(c_out)]
        rest = rest[n_out + len(c_out):]
        scr, csem = rest[:n_scr], rest[n_scr:]
        if not comm:
            return body(*pre, *ins, *outs, *scr)
        ids = [pl.program_id(k) for k in range(len(grid))]
        first = functools.reduce(jnp.logical_and, [i == 0 for i in ids])
        last = functools.reduce(jnp.logical_and, [i == n - 1 for i, n in zip(ids, grid)])
        pl.when(first)(lambda: _run_tasks(comm, "start", cin, cout, *csem))
        body(*pre, *ins, *outs, *scr)
        pl.when(last)(lambda: _run_tasks(comm, "finish", cin, cout, *csem))

    grid_spec = pltpu.PrefetchScalarGridSpec(
        num_scalar_prefetch=n_pre, grid=grid, in_specs=list(in_specs) + [ANY] * len(c_in),
        out_specs=list(out_specs) + [ANY] * len(c_out), scratch_shapes=list(scratch_shapes) + sems)
    return _pcall(
        wrapped, name=name, grid_spec=grid_spec, out_shape=_in_hbm(list(out_shape) + c_out), input_output_aliases=aliases,
        compiler_params=_params(("arbitrary",) * len(grid) if comm else sem, vmem_mib),
    )(*prefetch, *operands, *_from_hbm(*c_in))


def _comm_call(name, comm):
    c_in, c_out, aliases, n_sem = _comm_layout(comm, 0, 0)

    def body(*refs):
        cin, cout, (ss, rs) = refs[:len(c_in)], refs[len(c_in):len(c_in) + len(c_out)], refs[len(c_in) + len(c_out):]
        _run_tasks(comm, "start", cin, cout, ss, rs)
        _run_tasks(comm, "finish", cin, cout, ss, rs)

    return _pcall(
        body, name=name, in_specs=[ANY] * len(c_in), out_specs=[ANY] * len(c_out), out_shape=_in_hbm(c_out),
        scratch_shapes=[pltpu.SemaphoreType.DMA((n_sem,)), pltpu.SemaphoreType.DMA((n_sem,))],
        input_output_aliases=aliases,
    )(*_from_hbm(*c_in))


def _inproj(x, g1, w_int, comm=()):
    tm = TM

    def body(x_ref, g_ref, w_ref, proj_ref, u_ref):
        xf = x_ref[...]
        r = lax.rsqrt(jnp.mean(xf * xf, axis=-1, keepdims=True) + EPS)
        u = (xf * r * g_ref[...]).astype(BF)
        u_ref[...] = u
        proj_ref[...] = _dot(u, w_ref[...], 1, 1)

    return _call(
        body, (x, g1, w_int), name="inproj", grid=(T // tm,),
        in_specs=[pl.BlockSpec((tm, D), lambda i: (i, 0)), pl.BlockSpec((1, D), lambda i: (0, 0)),
                  _resident((INW, D))],
        out_specs=[pl.BlockSpec((tm, INW), lambda i: (i, 0)), pl.BlockSpec((tm, D), lambda i: (i, 0))],
        out_shape=[SDS((T, INW), F32), SDS((T, D), BF)], sem=("parallel",), vmem_mib=40, comm=comm)


def _outproj(y, w_out, x, g2):
    tm = TM

    def body(y_ref, w_ref, x_ref, g_ref, h1_ref, u2_ref):
        h1 = x_ref[...] + _dot(y_ref[...], w_ref[...], 1, 0)
        h1_ref[...] = h1
        r = lax.rsqrt(jnp.mean(h1 * h1, axis=-1, keepdims=True) + EPS)
        u2_ref[...] = (h1 * r * g_ref[...]).astype(BF)

    return _call(
        body, (y, w_out, x, g2), name="outproj", grid=(T // tm,),
        in_specs=[pl.BlockSpec((tm, D), lambda i: (i, 0)), _resident((D, D)),
                  pl.BlockSpec((tm, D), lambda i: (i, 0)), pl.BlockSpec((1, D), lambda i: (0, 0))],
        out_specs=[pl.BlockSpec((tm, D), lambda i: (i, 0)), pl.BlockSpec((tm, D), lambda i: (i, 0))],
        out_shape=[SDS((T, D), F32), SDS((T, D), BF)], sem=("parallel",), vmem_mib=32)


def _ffn_up(u2, w_upt, comm=()):
    tm, tn = 1024, 512

    def body(u_ref, w_ref, o_ref):
        o_ref[...] = _dot(u_ref[...], w_ref[...], 1, 1).astype(BF)

    return _call(
        body, (u2, w_upt), name="ffn_up", grid=(T // tm, 2 * DFF // tn),
        in_specs=[pl.BlockSpec((tm, D), lambda i, j: (i, 0)), pl.BlockSpec((tn, D), lambda i, j: (j, 0))],
        out_specs=[pl.BlockSpec((tm, tn), lambda i, j: (i, j))], out_shape=[SDS((T, 2 * DFF), BF)],
        sem=("parallel", "parallel"), vmem_mib=32, comm=comm)


def _ffn_down(a, w_down, h1, tgt):
    tm = TM

    def body(a_ref, w_ref, h1_ref, t_ref, dh_ref, dhb_ref, l_ref):
        @pl.when(pl.program_id(0) == 0)
        def _():
            l_ref[...] = jnp.zeros_like(l_ref)

        h2 = h1_ref[...] + _dot(a_ref[...], w_ref[...], 1, 0)
        e = h2 - t_ref[...]
        dh = e * (1.0 / D)
        dh_ref[...] = dh
        dhb_ref[...] = dh.astype(BF)
        e2 = jnp.sum((e * e).reshape(tm // 8, 8, D), axis=0)
        acc = e2[:, 0:128]
        for k in range(1, D // 128):
            acc = acc + e2[:, k * 128:(k + 1) * 128]
        l_ref[...] += acc

    return _call(
        body, (a, w_down, h1, tgt), name="ffn_down", grid=(T // tm,),
        in_specs=[pl.BlockSpec((tm, DFF), lambda i: (i, 0)), _resident((DFF, D)),
                  pl.BlockSpec((tm, D), lambda i: (i, 0)), pl.BlockSpec((tm, D), lambda i: (i, 0))],
        out_specs=[pl.BlockSpec((tm, D), lambda i: (i, 0)), pl.BlockSpec((tm, D), lambda i: (i, 0)),
                   pl.BlockSpec((8, 128), lambda i: (0, 0))],
        out_shape=[SDS((T, D), F32), SDS((T, D), BF), SDS((8, 128), F32)], sem=("arbitrary",), vmem_mib=40)


def _bucket_table():
    q = np.arange(BLK, dtype=np.int32)[:, None]
    j = np.arange(2 * BLK, dtype=np.int32)[None, :]
    n = np.maximum(q + BLK - j, 0)
    nf = np.maximum(n, 1).astype(np.float32)
    max_exact = NBUCKET // 2
    large = max_exact + (np.log(nf / np.float32(max_exact)) / np.float32(math.log(BLK / max_exact))
                         * np.float32(NBUCKET - max_exact)).astype(np.int32)
    large = np.minimum(large, NBUCKET - 1)
    return np.where(n < max_exact, n, large).astype(np.int32)


def _band_bias_bwd(dbias, bucket, me):
    def body(me_ref, db_ref, bk_ref, o_ref):
        bk = bk_ref[...]
        for b in range(NBUCKET):
            m = bk == b
            for h in range(NH):
                v = jnp.where(m, db_ref[h * BLK:(h + 1) * BLK, :], 0.0)
                s = jnp.sum(jnp.sum(v, axis=1, keepdims=True), axis=0, keepdims=True)
                o_ref[0, h:h + 1, b:b + 1] = s

    grid_spec = pltpu.PrefetchScalarGridSpec(
        num_scalar_prefetch=1, grid=(1,),
        in_specs=[pl.BlockSpec((NH * BLK, 2 * BLK), lambda i, me_ref: (0, 0)),
                  pl.BlockSpec((BLK, 2 * BLK), lambda i, me_ref: (0, 0))],
        out_specs=pl.BlockSpec((1, NH, NBUCKET), lambda i, me_ref: (me_ref[0], 0, 0)),
    )
    return _pcall(body, name="band_bias_bwd", grid_spec=grid_spec, out_shape=SDS((N_DEV, NH, NBUCKET), F32),
                  compiler_params=_params(("arbitrary",)))(me, dbias, bucket)


def _mix_forward(P, zc8, zh8, pkv, first, cw, qg, kg, gco, gao, sink_ref, bias_ref):
    gate_b = P[:, 0:CW]
    gate_c = P[:, CW:2 * CW]
    hc = P[:, 2 * CW:3 * CW]
    z = gate_c * hc
    keep = jnp.where(first, 0.0, 1.0)
    zp = zc8 * zh8 * keep
    p1 = zp[7:8, :]
    p2 = zp[6:7, :]
    row = lax.broadcasted_iota(jnp.int32, (BLK, 1), 0)
    z1 = jnp.where(row == 0, p1, pltpu.roll(z, 1, 0))
    z2 = jnp.where(row == 0, p2, jnp.where(row == 1, p1, pltpu.roll(z, 2, 0)))
    cz = cw[0:1, :] * z2 + cw[1:2, :] * z1 + cw[2:3, :] * z
    y_conv = gate_b * cz

    scale = HD ** -0.5
    qi = lax.broadcasted_iota(jnp.int32, (GQ * BLK, 2 * BLK), 0) & (BLK - 1)
    kj = lax.broadcasted_iota(jnp.int32, (GQ * BLK, 2 * BLK), 1)
    dd = qi + BLK - kj
    first_key = jnp.where(first, BLK, 0)
    valid = (dd >= 0) & (dd < BLK) & (kj >= first_key)

    q0 = 3 * CW
    k0 = q0 + AW
    v0 = k0 + NKV * HD
    heads = []
    outs = []
    for kv in range(NKV):
        kb_raw = jnp.concatenate([pkv[:, kv * HD:(kv + 1) * HD], P[:, k0 + kv * HD:k0 + (kv + 1) * HD]], axis=0)
        rk = lax.rsqrt(jnp.mean(kb_raw * kb_raw, axis=-1, keepdims=True) + EPS)
        kb = (kb_raw * rk * kg).astype(BF)
        vb = jnp.concatenate([pkv[:, NKV * HD + kv * HD:NKV * HD + (kv + 1) * HD],
                              P[:, v0 + kv * HD:v0 + (kv + 1) * HD]], axis=0).astype(BF)
        q_raw, rq, qn = [], [], []
        for g in range(GQ):
            h = kv * GQ + g
            qh = P[:, q0 + h * HD:q0 + (h + 1) * HD]
            r = lax.rsqrt(jnp.mean(qh * qh, axis=-1, keepdims=True) + EPS)
            q_raw.append(qh)
            rq.append(r)
            qn.append(qh * r * qg)
        Q = jnp.concatenate(qn, axis=0).astype(BF)
        S = _dot(Q, kb, 1, 1) * scale + bias_ref[kv * GQ * BLK:(kv + 1) * GQ * BLK, :]
        S = jnp.where(valid, S, NEG_INF)
        sink = jnp.concatenate([jnp.full((BLK, 1), sink_ref[0, kv * GQ + g], F32) for g in range(GQ)], axis=0)
        m = jnp.maximum(jnp.max(S, axis=-1, keepdims=True), sink)
        p = jnp.exp(S - m)
        es = jnp.exp(sink - m)
        denom = jnp.sum(p, axis=-1, keepdims=True) + es
        probs = p / denom
        O = _dot(probs.astype(BF), vb, 1, 0)
        heads.append(dict(kb_raw=kb_raw, rk=rk, kb=kb, vb=vb, q_raw=q_raw, rq=rq, Q=Q, probs=probs,
                          psink=es / denom, O=O))
        outs += [O[g * BLK:(g + 1) * BLK, :] for g in range(GQ)]
    y_attn = jnp.concatenate(outs, axis=1)

    rc = lax.rsqrt(jnp.mean(y_conv * y_conv, axis=-1, keepdims=True) + EPS)
    ra = lax.rsqrt(jnp.mean(y_attn * y_attn, axis=-1, keepdims=True) + EPS)
    y = jnp.concatenate([y_conv * rc * gco, y_attn * ra * gao], axis=1)
    return dict(gate_b=gate_b, gate_c=gate_c, hc=hc, z=z, z1=z1, z2=z2, cz=cz, y_conv=y_conv, y_attn=y_attn,
                rc=rc, ra=ra, heads=heads, y=y, row=row, scale=scale)


BPS = 2
TILE = BPS * BLK
KV0 = 3 * CW + AW


def _mix_in_specs(tile_of):
    return [
        pl.BlockSpec(memory_space=pltpu.SMEM),
        pl.BlockSpec((TILE, INW), lambda s: (tile_of(s), 0)),
        pl.BlockSpec((8, CW), lambda s: (jnp.maximum(tile_of(s) * (TILE // 8) - 1, 0), 1)),
        pl.BlockSpec((8, CW), lambda s: (jnp.maximum(tile_of(s) * (TILE // 8) - 1, 0), 2)),
        pl.BlockSpec((BLK, 2 * NKV * HD), lambda s: (jnp.maximum(tile_of(s) * BPS - 1, 0), KV0 // (2 * NKV * HD))),
    ]


def _block_inputs(tile, b, zc_ref, zh_ref, pkv_ref, first_tile):
    P = tile[b * BLK:(b + 1) * BLK, :]
    if b == 0:
        return P, zc_ref[...], zh_ref[...], pkv_ref[...], first_tile
    lo = b * BLK
    return P, tile[lo - 8:lo, CW:2 * CW], tile[lo - 8:lo, 2 * CW:3 * CW], tile[lo - BLK:lo, KV0:KV0 + 2 * NKV * HD], False


def _mix_param_specs():
    return [
        pl.BlockSpec((8, CW), lambda s: (0, 0)),
        pl.BlockSpec((1, HD), lambda s: (0, 0)),
        pl.BlockSpec((1, HD), lambda s: (0, 0)),
        pl.BlockSpec((1, CW), lambda s: (0, 0)),
        pl.BlockSpec((1, AW), lambda s: (0, 0)),
        pl.BlockSpec((NH * BLK, 2 * BLK), lambda s: (0, 0)),
    ]


def _mix_fwd(proj, sinks, cw8, qg, kg, gco, gao, bias, comm=()):
    def body(sink_ref, p_ref, zc_ref, zh_ref, pkv_ref, cw_ref, qg_ref, kg_ref, gco_ref, gao_ref, bias_ref, y_ref):
        tile = p_ref[...]
        for b in range(BPS):
            f = _mix_forward(*_block_inputs(tile, b, zc_ref, zh_ref, pkv_ref, pl.program_id(0) == 0), cw_ref[...],
                             qg_ref[...], kg_ref[...], gco_ref[...], gao_ref[...], sink_ref, bias_ref)
            y_ref[b * BLK:(b + 1) * BLK, :] = f["y"].astype(BF)

    return _call(
        body, (sinks, proj, proj, proj, proj, cw8, qg, kg, gco, gao, bias), name="mix_fwd", grid=(T // TILE,),
        in_specs=_mix_in_specs(lambda s: s) + _mix_param_specs(),
        out_specs=[pl.BlockSpec((TILE, D), lambda s: (s, 0))], out_shape=[SDS((T, D), BF)],
        sem=("parallel",), vmem_mib=40, comm=comm)


def _mix_bwd(proj, dy, sinks, cw8, qg, kg, gco, gao, bias, comm=()):
    n_steps = T // TILE

    def tile_of(s):
        return n_steps - 1 - s

    def body(sink_ref, p_ref, zc_ref, zh_ref, pkv_ref, dy_ref, cw_ref, qg_ref, kg_ref, gco_ref, gao_ref, bias_ref,
             dproj_ref, dcw_ref, dqg_ref, dkg_ref, dgco_ref, dgao_ref, dsink_ref, dbias_ref,
             ndcz_ref, dkc_ref, dvc_ref):
        s = pl.program_id(0)

        @pl.when(s == 0)
        def _():
            for r in (dcw_ref, dqg_ref, dkg_ref, dgco_ref, dgao_ref, dsink_ref, dbias_ref, ndcz_ref, dkc_ref, dvc_ref):
                r[...] = jnp.zeros_like(r)

        tile = p_ref[...]
        for b in reversed(range(BPS)):
            one_block(b, _block_inputs(tile, b, zc_ref, zh_ref, pkv_ref, s == n_steps - 1),
                      dy_ref[b * BLK:(b + 1) * BLK, :], sink_ref, cw_ref, qg_ref, kg_ref, gco_ref, gao_ref, bias_ref,
                      dproj_ref.at[b * BLK:(b + 1) * BLK, :], dcw_ref, dqg_ref, dkg_ref, dgco_ref, dgao_ref, dsink_ref,
                      dbias_ref, ndcz_ref, dkc_ref, dvc_ref)

    def one_block(b, inputs, dy, sink_ref, cw_ref, qg_ref, kg_ref, gco_ref, gao_ref, bias_ref,
                  dproj_ref, dcw_ref, dqg_ref, dkg_ref, dgco_ref, dgao_ref, dsink_ref, dbias_ref,
                  ndcz_ref, dkc_ref, dvc_ref):
        cw = cw_ref[...]
        qg_v, kg_v, gco_v, gao_v = qg_ref[...], kg_ref[...], gco_ref[...], gao_ref[...]
        f = _mix_forward(*inputs, cw, qg_v, kg_v, gco_v, gao_v, sink_ref, bias_ref)
        dyc, dgco = _rms_bwd(dy[:, 0:CW], f["y_conv"], f["rc"], gco_v)
        dya, dgao = _rms_bwd(dy[:, CW:CW + AW], f["y_attn"], f["ra"], gao_v)
        dgco_ref[...] += dgco
        dgao_ref[...] += dgao

        row = f["row"]
        dgate_b = dyc * f["cz"]
        dcz = dyc * f["gate_b"]
        dcw_ref[0:1, :] += jnp.sum(dcz * f["z2"], axis=0, keepdims=True)
        dcw_ref[1:2, :] += jnp.sum(dcz * f["z1"], axis=0, keepdims=True)
        dcw_ref[2:3, :] += jnp.sum(dcz * f["z"], axis=0, keepdims=True)
        nxt = ndcz_ref[...]
        n0 = nxt[0:1, :]
        n1 = nxt[1:2, :]
        d1 = jnp.where(row == BLK - 1, n0, pltpu.roll(dcz, BLK - 1, 0))
        d2 = jnp.where(row == BLK - 1, n1, jnp.where(row == BLK - 2, n0, pltpu.roll(dcz, BLK - 2, 0)))
        dz = cw[2:3, :] * dcz + cw[1:2, :] * d1 + cw[0:1, :] * d2
        ndcz_ref[...] = dcz[0:8, :]
        dproj_ref[:, 0:CW] = dgate_b.astype(BF)
        dproj_ref[:, CW:2 * CW] = (dz * f["hc"]).astype(BF)
        dproj_ref[:, 2 * CW:3 * CW] = (dz * f["gate_c"]).astype(BF)

        scale = f["scale"]
        lane = lax.broadcasted_iota(jnp.int32, (1, 128), 1)
        dq_cols, dk_cols, dv_cols = [], [], []
        for kv in range(NKV):
            hd = f["heads"][kv]
            dO = jnp.concatenate([dya[:, (kv * GQ + g) * HD:(kv * GQ + g + 1) * HD] for g in range(GQ)], axis=0)
            delta = jnp.sum(dO * hd["O"], axis=-1, keepdims=True)
            dOb = dO.astype(BF)
            dP = _dot(dOb, hd["vb"], 1, 1)
            dS = hd["probs"] * (dP - delta)
            dsk = hd["psink"] * delta
            for g in range(GQ):
                h = kv * GQ + g
                tot = jnp.sum(dsk[g * BLK:(g + 1) * BLK, :], axis=0, keepdims=True)
                dsink_ref[...] -= jnp.where(lane == h, tot, 0.0)
            dbias_ref[kv * GQ * BLK:(kv + 1) * GQ * BLK, :] += dS
            dSs = (dS * scale).astype(BF)
            dQ = _dot(dSs, hd["kb"], 1, 0)
            dKb = _dot(dSs, hd["Q"], 0, 0)
            dVb = _dot(hd["probs"].astype(BF), dOb, 0, 0)
            dkn = dKb[BLK:, :] + dkc_ref[:, kv * HD:(kv + 1) * HD]
            dvn = dVb[BLK:, :] + dvc_ref[:, kv * HD:(kv + 1) * HD]
            dkc_ref[:, kv * HD:(kv + 1) * HD] = dKb[:BLK, :]
            dvc_ref[:, kv * HD:(kv + 1) * HD] = dVb[:BLK, :]
            dk_raw, dkg = _rms_bwd(dkn, hd["kb_raw"][BLK:, :], hd["rk"][BLK:, :], kg_v)
            dkg_ref[...] += dkg
            dk_cols.append(dk_raw)
            dv_cols.append(dvn)
            for g in range(GQ):
                dq_raw, dqg = _rms_bwd(dQ[g * BLK:(g + 1) * BLK, :], hd["q_raw"][g], hd["rq"][g], qg_v)
                dqg_ref[...] += dqg
                dq_cols.append(dq_raw)
        dproj_ref[:, 3 * CW:INW] = jnp.concatenate(dq_cols + dk_cols + dv_cols, axis=1).astype(BF)

    small = lambda r, c: pl.BlockSpec((r, c), lambda s: (0, 0))
    return _call(
        body, (sinks, proj, proj, proj, proj, dy, cw8, qg, kg, gco, gao, bias), name="mix_bwd", grid=(n_steps,),
        in_specs=_mix_in_specs(tile_of) + [pl.BlockSpec((TILE, D), lambda s: (tile_of(s), 0))] + _mix_param_specs(),
        out_specs=[pl.BlockSpec((TILE, INW), lambda s: (tile_of(s), 0)), small(8, CW), small(1, HD), small(1, HD),
                   small(1, CW), small(1, AW), small(1, 128), small(NH * BLK, 2 * BLK)],
        out_shape=[SDS((T, INW), BF), SDS((8, CW), F32), SDS((1, HD), F32), SDS((1, HD), F32), SDS((1, CW), F32),
                   SDS((1, AW), F32), SDS((1, 128), F32), SDS((NH * BLK, 2 * BLK), F32)],
        scratch_shapes=[pltpu.VMEM((8, CW), F32), pltpu.VMEM((BLK, NKV * HD), F32), pltpu.VMEM((BLK, NKV * HD), F32)],
        sem=("arbitrary",), vmem_mib=56, comm=comm)


FT = 256
NFT = DFF // FT
RC = 128
NCH = T // RC
LEAD = 16


def _rows8(x):
    return jnp.sum(x.reshape(x.shape[0] // 8, 8, x.shape[1]), axis=0)


def _ffn_act_specs():
    return [
        pl.BlockSpec((T, FT), lambda j: (0, j)), pl.BlockSpec((T, FT), lambda j: (0, NFT + j)),
        pl.BlockSpec((8, FT), lambda j: (0, j)), pl.BlockSpec((8, FT), lambda j: (0, NFT + j)),
        pl.BlockSpec((1, FT), lambda j: (0, j)), pl.BlockSpec((1, FT), lambda j: (0, NFT + j)),
    ]


def _conv_rows(win, w, b, n):
    win = win.astype(F32)
    u = win[LEAD:LEAD + n]
    u1 = pltpu.roll(win, 1, 0)[LEAD:LEAD + n]
    u2 = pltpu.roll(win, 2, 0)[LEAD:LEAD + n]
    return u2, u1, u, w[0:1, :] * u2 + w[1:2, :] * u1 + w[2:3, :] * u + b


def _ffn_act(up, fw8, fb):
    def body(ug_ref, uv_ref, wg_ref, wv_ref, bg_ref, bv_ref, a_ref):
        wg, wv, bg, bv = wg_ref[...], wv_ref[...], bg_ref[...], bv_ref[...]

        def chunk(win_g, win_v):
            gp = _conv_rows(win_g, wg, bg, RC)[3]
            vp = _conv_rows(win_v, wv, bv, RC)[3]
            return (gp * jax.nn.sigmoid(gp) * vp).astype(BF)

        zero = jnp.zeros((LEAD, FT), BF)
        a_ref[0:RC, :] = chunk(jnp.concatenate([zero, ug_ref[0:RC, :]], axis=0),
                               jnp.concatenate([zero, uv_ref[0:RC, :]], axis=0))

        def step(i, carry):
            r0 = pl.multiple_of(i * RC, RC)
            win = pl.ds(r0 - LEAD, RC + LEAD)
            a_ref[pl.ds(r0, RC), :] = chunk(ug_ref[win, :], uv_ref[win, :])
            return carry

        lax.fori_loop(1, NCH, step, 0)

    return _call(
        body, (up, up, fw8, fw8, fb, fb), name="ffn_act", grid=(NFT,), in_specs=_ffn_act_specs(),
        out_specs=[pl.BlockSpec((T, FT), lambda j: (0, j))], out_shape=[SDS((T, DFF), BF)],
        sem=("parallel",), vmem_mib=40)


def _ffn_act_bwd(up, da, fw8, fb, comm=()):
    ext = RC + LEAD

    def body(ug_ref, uv_ref, wg_ref, wv_ref, bg_ref, bv_ref, da_ref,
             dug_ref, duv_ref, dwg_ref, dwv_ref, dbg_ref, dbv_ref):
        wg, wv, bg, bv = wg_ref[...], wv_ref[...], bg_ref[...], bv_ref[...]

        def chunk(win_g, win_v, da_e):
            g2, g1, g0, gp = _conv_rows(win_g, wg, bg, ext)
            v2, v1, v0, vp = _conv_rows(win_v, wv, bv, ext)
            da_e = da_e.astype(F32)
            sig = jax.nn.sigmoid(gp)
            dvp = da_e * (gp * sig)
            dgp = da_e * vp * (sig * (1.0 + gp * (1.0 - sig)))

            def back(dp, w):
                return (w[2:3, :] * dp[0:RC] + w[1:2, :] * pltpu.roll(dp, ext - 1, 0)[0:RC]
                        + w[0:1, :] * pltpu.roll(dp, ext - 2, 0)[0:RC]).astype(BF)

            def sums(dp, u2, u1, u0):
                d = dp[0:RC]
                return [_rows8(d), _rows8(d * u2[0:RC]), _rows8(d * u1[0:RC]), _rows8(d * u0[0:RC])]

            return back(dgp, wg), back(dvp, wv), sums(dgp, g2, g1, g0) + sums(dvp, v2, v1, v0)

        zero = jnp.zeros((LEAD, FT), BF)
        dug, duv, acc = chunk(jnp.concatenate([zero, ug_ref[0:ext, :]], axis=0),
                              jnp.concatenate([zero, uv_ref[0:ext, :]], axis=0), da_ref[0:ext, :])
        dug_ref[0:RC, :] = dug
        duv_ref[0:RC, :] = duv

        def step(i, acc):
            r0 = pl.multiple_of(i * RC, RC)
            win = pl.ds(r0 - LEAD, ext + LEAD)
            dug, duv, part = chunk(ug_ref[win, :], uv_ref[win, :], da_ref[pl.ds(r0, ext), :])
            dug_ref[pl.ds(r0, RC), :] = dug
            duv_ref[pl.ds(r0, RC), :] = duv
            return [a + p for a, p in zip(acc, part)]

        acc = lax.fori_loop(1, NCH - 1, step, acc)
        r0 = T - RC
        tail = lambda ref, lo: jnp.concatenate([ref[lo:T, :], zero], axis=0)
        dug, duv, part = chunk(tail(ug_ref, r0 - LEAD), tail(uv_ref, r0 - LEAD), tail(da_ref, r0))
        dug_ref[r0:T, :] = dug
        duv_ref[r0:T, :] = duv
        tot = [jnp.sum(a + p, axis=0, keepdims=True) for a, p in zip(acc, part)]
        for k, (dw_ref, db_ref) in enumerate(((dwg_ref, dbg_ref), (dwv_ref, dbv_ref))):
            db_ref[...] = tot[4 * k]
            dw_ref[...] = jnp.zeros_like(dw_ref)
            for r in range(3):
                dw_ref[r:r + 1, :] = tot[4 * k + 1 + r]

    col = lambda r: pl.BlockSpec((r, FT), lambda j: (0, j))
    return _call(
        body, (up, up, fw8, fw8, fb, fb, da), name="ffn_act_bwd", grid=(NFT,),
        in_specs=_ffn_act_specs() + [pl.BlockSpec((T, FT), lambda j: (0, j))],
        out_specs=[col(T), col(T), col(8), col(8), col(1), col(1)],
        out_shape=[SDS((T, DFF), BF), SDS((T, DFF), BF), SDS((8, DFF), F32), SDS((8, DFF), F32),
                   SDS((1, DFF), F32), SDS((1, DFF), F32)],
        sem=("parallel",), vmem_mib=40, comm=comm)


def _ffn_down_bwd(dh2b, w_down, comm=()):
    tm = TM

    def body(d_ref, w_ref, o_ref):
        o_ref[...] = _dot(d_ref[...], w_ref[...], 1, 1).astype(BF)

    return _call(
        body, (dh2b, w_down), name="ffn_down_bwd", grid=(T // tm,),
        in_specs=[pl.BlockSpec((tm, D), lambda i: (i, 0)), _resident((DFF, D))],
        out_specs=[pl.BlockSpec((tm, DFF), lambda i: (i, 0))], out_shape=[SDS((T, DFF), BF)],
        sem=("parallel",), vmem_mib=40, comm=comm)


def _norm_matmul_bwd(name, a_list, w_t, k_offsets, xin, g, dres, want_bf16, comm=(), slot=None):
    tm = TM
    ks = [a.shape[1] for a in a_list]
    n_a = len(a_list)
    n_pre = 0 if slot is None else 1

    def body(*refs):
        refs = refs[n_pre:]
        a_refs = refs[:n_a]
        w_ref, x_ref, g_ref, r_ref = refs[n_a:n_a + 4]
        outs = refs[n_a + 4:]
        dx_ref, dg_ref = outs[0], (outs[-1] if slot is None else outs[-1].at[0])

        @pl.when(pl.program_id(0) == 0)
        def _():
            dg_ref[...] = jnp.zeros_like(dg_ref)

        du = _dot(a_refs[0][...], w_ref[k_offsets[0]:k_offsets[0] + ks[0], :], 1, 0)
        for k in range(1, n_a):
            du = du + _dot(a_refs[k][...], w_ref[k_offsets[k]:k_offsets[k] + ks[k], :], 1, 0)
        x = x_ref[...]
        r = lax.rsqrt(jnp.mean(x * x, axis=-1, keepdims=True) + EPS)
        dx, dg = _rms_bwd(du, x, r, g_ref[...])
        dx = r_ref[...] + dx
        dx_ref[...] = dx
        if want_bf16:
            outs[1][...] = dx.astype(BF)
        dg_ref[...] += dg

    tile = lambda c: pl.BlockSpec((tm, c), lambda i, *_: (i, 0))
    if slot is None:
        dg_spec, dg_shape = pl.BlockSpec((1, D), lambda i: (0, 0)), SDS((1, D), F32)
    else:
        dg_spec, dg_shape = pl.BlockSpec((1, 1, D), lambda i, slot_ref: (slot_ref[0], 0, 0)), SDS((N_DEV, 1, D), F32)
    out_specs = [tile(D)] + ([tile(D)] if want_bf16 else []) + [dg_spec]
    out_shape = [SDS((T, D), F32)] + ([SDS((T, D), BF)] if want_bf16 else []) + [dg_shape]
    return _call(
        body, (*a_list, w_t, xin, g, dres), name=name, grid=(T // tm,), prefetch=() if slot is None else (slot,),
        in_specs=[tile(k) for k in ks] + [_resident(w_t.shape), tile(D),
                                           pl.BlockSpec((1, D), lambda i, *_: (0, 0)), tile(D)],
        out_specs=out_specs, out_shape=out_shape, sem=("arbitrary",), vmem_mib=56, comm=comm)


def _out_bwd(dh1b, w_out, comm=()):
    tm = TM

    def body(d_ref, w_ref, o_ref):
        o_ref[...] = _dot(d_ref[...], w_ref[...], 1, 1)

    return _call(
        body, (dh1b, w_out), name="out_bwd", grid=(T // tm,),
        in_specs=[pl.BlockSpec((tm, D), lambda i: (i, 0)), _resident((D, D))],
        out_specs=[pl.BlockSpec((tm, D), lambda i: (i, 0))], out_shape=[SDS((T, D), F32)],
        sem=("parallel",), vmem_mib=32, comm=comm)


def _wgrad(name, a_list, b, comm=()):
    m_k = a_list[0].shape[1]
    tm = max(t for t in range(128, m_k // 2 + 1, 128) if m_k % t == 0)
    steps = [a.shape[1] // tm for a in a_list]
    starts = [sum(steps[:k]) for k in range(len(a_list))]
    n_a = len(a_list)

    def body(*refs):
        a_refs, b_ref, o_ref = refs[:n_a], refs[n_a], refs[n_a + 1]
        i = pl.program_id(0)
        for k in range(n_a):
            @pl.when((i >= starts[k]) & (i < starts[k] + steps[k]))
            def _(k=k):
                o_ref[...] = _dot(a_refs[k][...], b_ref[...], 0, 0).astype(BF)

    def a_spec(k):
        return pl.BlockSpec((T, tm), lambda i: (0, jnp.clip(i - starts[k], 0, steps[k] - 1)))

    m_total = tm * sum(steps)
    return _call(
        body, (*a_list, b), name=name, grid=(sum(steps),),
        in_specs=[a_spec(k) for k in range(n_a)] + [_resident((T, D))],
        out_specs=[pl.BlockSpec((tm, D), lambda i: (i, 0))], out_shape=[SDS((m_total, D), BF)],
        sem=("parallel",), vmem_mib=40, comm=comm)


def _chip_sum(name, gbf, from_sib, core, chip):
    h = gbf.shape[1]
    th = h // 2

    def body(core_ref, chip_ref, g_ref, s_ref, pbf_ref, own_ref):
        p = g_ref[0].astype(F32) + s_ref[0].astype(F32)
        pbf_ref[0] = p.astype(BF)

        @pl.when(pl.program_id(1) == chip_ref[0])
        def _():
            own_ref[...] = p

    grid_spec = pltpu.PrefetchScalarGridSpec(
        num_scalar_prefetch=2, grid=(h // th, N_CHIPS),
        in_specs=[pl.BlockSpec((1, th, D), lambda t, jj, core_ref, chip_ref: (2 * jj + core_ref[0], t, 0)),
                  pl.BlockSpec((1, th, D), lambda t, jj, core_ref, chip_ref: (jj, t, 0))],
        out_specs=[pl.BlockSpec((1, th, D), lambda t, jj, core_ref, chip_ref: (jj, t, 0)),
                   pl.BlockSpec((th, D), lambda t, jj, core_ref, chip_ref: (t, 0))],
    )
    return _pcall(
        body, name=name, grid_spec=grid_spec, out_shape=_in_hbm([SDS((N_CHIPS, h, D), BF), SDS((h, D), F32)]),
        compiler_params=_params(("arbitrary", "arbitrary"), 32),
    )(core, chip, *_from_hbm(gbf, from_sib))


def _final_sum(name, own, from_chips, core):
    h = own.shape[0]

    def body(core_ref, o_ref, r_ref, f_ref):
        f_ref[0] = ((o_ref[...] + r_ref[0].astype(F32)) + r_ref[1].astype(F32)) + r_ref[2].astype(F32)

    grid_spec = pltpu.PrefetchScalarGridSpec(
        num_scalar_prefetch=1, grid=(1,),
        in_specs=[pl.BlockSpec((h, D), lambda i, core_ref: (0, 0)), pl.BlockSpec((3, h, D), lambda i, core_ref: (0, 0, 0))],
        out_specs=pl.BlockSpec((1, h, D), lambda i, core_ref: (core_ref[0], 0, 0)),
    )
    return _pcall(body, name=name, grid_spec=grid_spec, out_shape=pltpu.HBM((2, h, D), F32),
                  compiler_params=_params(("arbitrary",), 40))(core, *_from_hbm(own, from_chips))


def _adam_math(w, g, m, v):
    nm = ADAM_B1 * m + (1.0 - ADAM_B1) * g
    nv = ADAM_B2 * v + (1.0 - ADAM_B2) * (g * g)
    m_hat = nm / (1.0 - ADAM_B1 ** ADAM_STEP)
    v_hat = nv / (1.0 - ADAM_B2 ** ADAM_STEP)
    return -ADAM_LR * (m_hat / (jnp.sqrt(v_hat) + ADAM_EPS) + ADAM_WD * w), nm, nv


def _adamw(name, w, g, m, v, tr, copy_g=False):
    rows, cols = w.shape

    def body(w_ref, g_ref, m_ref, v_ref, *outs):
        g_val = g_ref[...]
        if copy_g:
            outs[0][...] = g_val
        d_ref, nm_ref, nv_ref = outs[-3:]
        d_ref[...], nm_ref[...], nv_ref[...] = _adam_math(w_ref[...], g_val, m_ref[...], v_ref[...])

    spec = pl.BlockSpec((tr, cols), lambda i: (i, 0))
    n_out = 4 if copy_g else 3
    return _call(body, (w, g, m, v), name=name, grid=(rows // tr,), in_specs=[spec] * 4, out_specs=[spec] * n_out,
                 out_shape=[SDS((rows, cols), F32)] * n_out, sem=("parallel",), vmem_mib=32, free=(0, 2, 3))


C_G1, C_G2, C_GCO, C_GAO, C_DCW, C_DQG, C_DKG, C_SINK, C_SQ = 0, 1024, 2048, 2560, 3072, 4608, 4736, 4864, 5632
P_W = C_SQ + 128


def _pack_small(me, dfwg, dfwv, dfbg, dfbv, dg2, dgco, dgao, dcw8, dqg, dkg, dsink, sq):
    def body(me_ref, dfwg_r, dfwv_r, dfbg_r, dfbv_r, dg2_r, dgco_r, dgao_r, dcw_r, dqg_r, dkg_r, dsink_r, sq_r, o):
        o[...] = jnp.zeros_like(o)
        o[0, :, 0:DFF] = dfwg_r[...]
        o[0, :, DFF:2 * DFF] = dfwv_r[...]
        o[0, 3:4, 0:DFF] = dfbg_r[...]
        o[0, 3:4, DFF:2 * DFF] = dfbv_r[...]
        o[0, 4:5, C_G2:C_G2 + D] = dg2_r[...]
        o[0, 4:5, C_GCO:C_GCO + CW] = dgco_r[...]
        o[0, 4:5, C_GAO:C_GAO + AW] = dgao_r[...]
        for r in range(3):
            o[0, 4:5, C_DCW + r * CW:C_DCW + (r + 1) * CW] = dcw_r[r:r + 1, :]
        o[0, 4:5, C_DQG:C_DQG + HD] = dqg_r[...]
        o[0, 4:5, C_DKG:C_DKG + HD] = dkg_r[...]
        o[0, 4:5, C_SINK:C_SINK + 128] = dsink_r[...]
        o[0, :, C_SQ:C_SQ + 128] = sq_r[...]

    ins = (dfwg, dfwv, dfbg, dfbv, dg2, dgco, dgao, dcw8, dqg, dkg, dsink, sq)
    return _call(body, ins, name="pack_small", grid=(1,), prefetch=(me,),
                 in_specs=[pl.BlockSpec(a.shape, lambda i, me_ref: (0, 0)) for a in ins],
                 out_specs=[pl.BlockSpec((1, 8, P_W), lambda i, me_ref: (me_ref[0], 0, 0))],
                 out_shape=[SDS((N_DEV, 8, P_W), F32)], sem=("arbitrary",))[0]


N_SMALL = 11


def _small_adam(chip, p_all, g1_all, tbl_all, ws, ms, vs):
    fw_cols = 2 * DFF // N_CHIPS
    cw_cols = CW // N_CHIPS

    def body(chip_ref, p_ref, fw_ref, cw0_ref, cw1_ref, cw2_ref, g1_ref, tbl_ref, *refs):
        w_r, m_r, v_r = refs[0:N_SMALL], refs[N_SMALL:2 * N_SMALL], refs[2 * N_SMALL:3 * N_SMALL]
        outs = refs[3 * N_SMALL:]
        g_o, d_o, nm_o, nv_o = (outs[k * N_SMALL:(k + 1) * N_SMALL] for k in range(4))
        loss_o = outs[4 * N_SMALL]

        def total(ref):
            s = ref[0]
            for k in range(1, N_DEV):
                s = s + ref[k]
            return s

        S = total(p_ref)
        fw = total(fw_ref)
        cws = [total(r) for r in (cw0_ref, cw1_ref, cw2_ref)]

        def step(i, g, at):
            d, nm, nv = _adam_math(w_r[i][at], g, m_r[i][at], v_r[i][at])
            g_o[i][at], d_o[i][at], nm_o[i][at], nv_o[i][at] = g, d, nm, nv

        everything = (slice(None), slice(None))
        step(0, total(g1_ref), everything)
        for r in range(3):
            step(1, cws[r][4:5, :], (0, slice(r, r + 1), slice(None)))
        step(2, S[4:5, C_DQG:C_DQG + HD], everything)
        step(3, S[4:5, C_DKG:C_DKG + HD], everything)
        step(4, total(tbl_ref), everything)
        step(5, S[4:5, C_SINK:C_SINK + NH], everything)
        step(6, S[4:5, C_GCO:C_GCO + CW], everything)
        step(7, S[4:5, C_GAO:C_GAO + AW], everything)
        step(8, S[4:5, C_G2:C_G2 + D], everything)
        step(9, fw[0:3, :], (0, slice(None), slice(None)))
        step(10, S[3:4, 0:2 * DFF], everything)
        sq = S[:, C_SQ:C_SQ + 128]
        loss_o[...] = jnp.sum(jnp.sum(sq, axis=1, keepdims=True), axis=0, keepdims=True) * (0.5 / D)

    def full(a):
        n = len(a.shape)
        return pl.BlockSpec(a.shape, lambda i, chip_ref: (0,) * n)

    params = [*ws, *ms, *vs]
    out = _call(
        body, (p_all, p_all, p_all, p_all, p_all, g1_all, tbl_all, *params), name="small_adam", grid=(1,), prefetch=(chip,),
        in_specs=[full(p_all),
                  pl.BlockSpec((N_DEV, 8, fw_cols), lambda i, chip_ref: (0, 0, chip_ref[0])),
                  *[pl.BlockSpec((N_DEV, 8, cw_cols), lambda i, chip_ref, r=r: (0, 0, (C_DCW + r * CW) // cw_cols + chip_ref[0]))
                    for r in range(3)],
                  full(g1_all), full(tbl_all), *[full(a) for a in params]],
        out_specs=[full(a) for a in ws] * 4 + [pl.BlockSpec((1, 1), lambda i, chip_ref: (0, 0))],
        out_shape=[SDS(a.shape, F32) for a in ws] * 4 + [SDS((1, 1), F32)], sem=("arbitrary",), vmem_mib=32)
    return out[0:N_SMALL], out[N_SMALL:2 * N_SMALL], out[2 * N_SMALL:3 * N_SMALL], out[3 * N_SMALL:4 * N_SMALL], out[4 * N_SMALL]


PLACE_STEPS = 4


def _place_specs(shards):
    rows = [s.shape[0] // PLACE_STEPS for s in shards]
    return ([pl.BlockSpec((r, D), lambda i, chip_ref: (i, 0)) for r in rows],
            [pl.BlockSpec((r, D), lambda i, chip_ref: (chip_ref[0] * PLACE_STEPS + i, 0)) for r in rows],
            [SDS((N_CHIPS * s.shape[0], D), BF) for s in shards])


def _place_first(chip, shard, conv_w, ffn_conv_w):
    def body(chip_ref, a, s0, s1, o, t0, t1):
        o[...] = a[...].astype(BF)

        @pl.when(pl.program_id(0) == 0)
        def _():
            for s, t in ((s0, t0), (s1, t1)):
                t[...] = jnp.zeros_like(t)
                t[0, 0:3, :] = s[...]

    ins, outs, shapes = _place_specs([shard])
    taps = (conv_w, ffn_conv_w)
    return _call(
        body, (shard, conv_w, ffn_conv_w), name="place_first", grid=(PLACE_STEPS,), prefetch=(chip,),
        in_specs=ins + [pl.BlockSpec(s.shape, lambda i, chip_ref: (0, 0)) for s in taps],
        out_specs=outs + [pl.BlockSpec((1, 8, s.shape[1]), lambda i, chip_ref: (chip_ref[0], 0, 0)) for s in taps],
        out_shape=shapes + [SDS((N_CHIPS, 8, s.shape[1]), F32) for s in taps],
        sem=("arbitrary",), vmem_mib=32, free=(0, 1, 2))


def _place_rest(chip, shards, table, bucket, comm):
    n = len(shards)

    def body(chip_ref, *refs):
        a, (tab_ref, bk_ref), o, bias_ref = refs[:n], refs[n:n + 2], refs[n + 2:2 * n + 2], refs[2 * n + 2]
        for src, dst in zip(a, o):
            dst[...] = src[...].astype(BF)

        @pl.when(pl.program_id(0) == 0)
        def _():
            bk = bk_ref[...]
            eq = [bk == b for b in range(NBUCKET)]
            for h in range(NH):
                acc = jnp.zeros((BLK, 2 * BLK), F32)
                for b in range(NBUCKET):
                    acc = jnp.where(eq[b], tab_ref[h, b], acc)
                bias_ref[h * BLK:(h + 1) * BLK, :] = acc

    ins, outs, shapes = _place_specs(shards)
    return _call(
        body, (*shards, table, bucket), name="place_rest", grid=(PLACE_STEPS,), prefetch=(chip,),
        in_specs=ins + [pl.BlockSpec(memory_space=pltpu.SMEM), pl.BlockSpec(bucket.shape, lambda i, chip_ref: (0, 0))],
        out_specs=outs + [pl.BlockSpec((NH * BLK, 2 * BLK), lambda i, chip_ref: (0, 0))],
        out_shape=shapes + [SDS((NH * BLK, 2 * BLK), F32)],
        sem=("arbitrary",), vmem_mib=32, comm=comm, free=tuple(range(n + 2)))


def kernel(x, norm_mix_g, w_in, conv_w, q_norm_g, k_norm_g, rel_bias_table, sinks, out_norm_conv_g, out_norm_attn_g, w_out, norm_ffn_g, w_up, ffn_conv_w, ffn_conv_b, w_down, loss_target, m_norm_mix_g, m_w_in, m_conv_w, m_q_norm_g, m_k_norm_g, m_rel_bias_table, m_sinks, m_out_norm_conv_g, m_out_norm_attn_g, m_w_out, m_norm_ffn_g, m_w_up, m_ffn_conv_w, m_ffn_conv_b, m_w_down, v_norm_mix_g, v_w_in, v_conv_w, v_q_norm_g, v_k_norm_g, v_rel_bias_table, v_sinks, v_out_norm_conv_g, v_out_norm_attn_g, v_w_out, v_norm_ffn_g, v_w_up, v_ffn_conv_w, v_ffn_conv_b, v_w_down):
    as_arg = lambda i: jnp.reshape(i, (1,)).astype(jnp.int32)
    chip = as_arg(2 * lax.axis_index("x") + lax.axis_index("y"))
    core = as_arg(lax.axis_index("c"))
    me = 2 * chip + core
    xs, tgt = x[0], loss_target[0]
    qg, kg, gco, gao, g1, g2, fb = q_norm_g, k_norm_g, out_norm_conv_g, out_norm_attn_g, norm_mix_g, norm_ffn_g, ffn_conv_b
    pieces = lambda g: g.reshape(N_DEV, g.shape[0] // N_DEV, D)
    whole = lambda f: f.reshape(2 * f.shape[1], D)

    bucket = jnp.asarray(_bucket_table())
    p_in, p_cw, p_fw = _place_first(chip, w_in[0].T, conv_w[0], ffn_conv_w[0])
    p_out, p_up, p_down, bias, w_int, cw_all, fw_all = _place_rest(
        chip, [w_out[0], w_up[0].T, w_down[0]], rel_bias_table.T, bucket,
        comm=[_t_gather(p_in), _t_small_weights(p_cw), _t_small_weights(p_fw)])
    cw8 = jnp.transpose(cw_all, (1, 0, 2)).reshape(8, CW)
    fw8 = jnp.transpose(fw_all, (1, 0, 2)).reshape(8, 2 * DFF)

    proj, u1, w_out_f = _inproj(xs, g1, w_int, comm=[_t_gather(p_out)])
    y, w_upt = _mix_fwd(proj, sinks, cw8, qg, kg, gco, gao, bias, comm=[_t_gather(p_up)])
    h1, u2 = _outproj(y, w_out_f, xs, g2)
    up, w_down_f = _ffn_up(u2, w_upt, comm=[_t_gather(p_down)])
    a, = _ffn_act(up, fw8, fb)
    dh2, dh2b, sq = _ffn_down(a, w_down_f, h1, tgt)

    gdbf, = _wgrad("wgrad_down", [a], dh2b)
    da, sib_down = _ffn_down_bwd(dh2b, w_down_f, comm=[_t_sibling(pieces(gdbf))])
    pbf_down, own_down = _chip_sum("chip_sum_w_down", pieces(gdbf), sib_down, core, chip)
    dug, duv, dfwg, dfwv, dfbg, dfbv, chips_down = _ffn_act_bwd(up, da, fw8, fb, comm=[_t_chips(pbf_down)])
    fin_down = _final_sum("final_sum_w_down", own_down, chips_down, core)
    gubf, = _wgrad("wgrad_up", [dug, duv], u2)
    dh1, dh1b, dg2, sib_up, fin_down = _norm_matmul_bwd(
        "ffn_up_bwd", [dug, duv], w_upt, [0, DFF], h1, g2, dh2, True, comm=[_t_sibling(pieces(gubf)), _t_swap(fin_down)])
    pbf_up, own_up = _chip_sum("chip_sum_w_up", pieces(gubf), sib_up, core, chip)
    gobf, = _wgrad("wgrad_out", [y], dh1b)
    dy, sib_out = _out_bwd(dh1b, w_out_f, comm=[_t_sibling(pieces(gobf))])
    pbf_out, own_out = _chip_sum("chip_sum_w_out", pieces(gobf), sib_out, core, chip)
    dproj, dcw8, dqg, dkg, dgco, dgao, dsink, dbias, chips_up, chips_out = _mix_bwd(
        proj, dy, sinks, cw8, qg, kg, gco, gao, bias, comm=[_t_chips(pbf_up), _t_chips(pbf_out)])
    fin_up = _final_sum("final_sum_w_up", own_up, chips_up, core)
    fin_out = _final_sum("final_sum_w_out", own_out, chips_out, core)
    tbl_all = _band_bias_bwd(dbias, bucket, me)
    p_all = _pack_small(me, dfwg, dfwv, dfbg, dfbv, dg2, dgco, dgao, dcw8, dqg, dkg, dsink, sq)
    gibf, fin_up, p_all, tbl_all = _wgrad(
        "wgrad_in", [dproj], u1, comm=[_t_swap(fin_up), _t_allgather(p_all), _t_allgather(tbl_all)])
    sib_in, fin_out = _comm_call("to_sibling_last", [_t_sibling(pieces(gibf)), _t_swap(fin_out)])
    pbf_in, own_in = _chip_sum("chip_sum_w_in", pieces(gibf), sib_in, core, chip)
    dx, g1_all, chips_in = _norm_matmul_bwd(
        "in_bwd", [dproj], w_int, [0], xs, g1, dh1, False, comm=[_t_chips(pbf_in)], slot=me)
    fin_in = _final_sum("final_sum_w_in", own_in, chips_in, core)
    g1_all, fin_in = _comm_call("gather_last", [_t_allgather(g1_all), _t_swap(fin_in)])

    g_w_in, g_w_out, g_w_up, g_w_down = whole(fin_in).T, whole(fin_out), whole(fin_up).T, whole(fin_down)
    g_w_down, d_down, nm_down, nv_down = _adamw("adamw_w_down", w_down[0], g_w_down, m_w_down[0], v_w_down[0], 352, True)
    d_up, nm_up, nv_up = _adamw("adamw_w_up", w_up[0], g_w_up, m_w_up[0], v_w_up[0], 256)
    g_w_out, d_out, nm_out, nv_out = _adamw("adamw_w_out", w_out[0], g_w_out, m_w_out[0], v_w_out[0], 256, True)
    d_in, nm_in, nv_in = _adamw("adamw_w_in", w_in[0], g_w_in, m_w_in[0], v_w_in[0], 256)
    sw = [norm_mix_g, conv_w, q_norm_g, k_norm_g, rel_bias_table.T, sinks, out_norm_conv_g, out_norm_attn_g,
          norm_ffn_g, ffn_conv_w, ffn_conv_b]
    smm = [m_norm_mix_g, m_conv_w, m_q_norm_g, m_k_norm_g, m_rel_bias_table.T, m_sinks, m_out_norm_conv_g,
           m_out_norm_attn_g, m_norm_ffn_g, m_ffn_conv_w, m_ffn_conv_b]
    smv = [v_norm_mix_g, v_conv_w, v_q_norm_g, v_k_norm_g, v_rel_bias_table.T, v_sinks, v_out_norm_conv_g,
           v_out_norm_attn_g, v_norm_ffn_g, v_ffn_conv_w, v_ffn_conv_b]
    *small_out, loss = _small_adam(chip, p_all, g1_all, tbl_all, sw, smm, smv)
    sg, sd, snm, snv = [list(r) for r in small_out]
    for r in (sg, sd, snm, snv):
        r[4] = r[4].T

    def order(s, b_in, b_out, b_up, b_down):
        return (s[0], b_in[None], s[1], s[2], s[3], s[4], s[5], s[6], s[7], b_out[None], s[8], b_up[None],
                s[9], s[10], b_down[None])

    return (loss.reshape(()), dx[None],
            *order(sg, g_w_in, g_w_out, g_w_up, g_w_down),
            *order(sd, d_in, d_out, d_up, d_down),
            *order(snm, nm_in, nm_out, nm_up, nm_down),
            *order(snv, nv_in, nv_out, nv_up, nv_down))
```

```python
import functools
import math

import numpy as np

import jax
import jax.numpy as jnp
from jax import lax
from jax.experimental import pallas as pl
from jax.experimental.pallas import tpu as pltpu

F32 = jnp.float32
BF = jnp.bfloat16
SDS = jax.ShapeDtypeStruct

T = 2048
D = 1024
CW = 512
AW = 512
HD = 64
NH = 8
NKV = 2
GQ = 4
INW = 2304
DFF = 2816
BLK = 128
NB = T // BLK
NBUCKET = 32
EPS = 1e-6
NEG_INF = -1e30
N_CHIPS = 4
N_DEV = 8

ADAM_LR = 0.001
ADAM_B1 = 0.9
ADAM_B2 = 0.999
ADAM_EPS = 1e-08
ADAM_WD = 0.01
ADAM_STEP = 10

TM = 512
MIB = 1024 * 1024
MESH = pl.DeviceIdType.MESH
ANY = pl.BlockSpec(memory_space=pl.ANY)

_pcall = pl.pallas_call


def _params(sem=None, vmem_mib=None):
    kw = {}
    if sem is not None:
        kw["dimension_semantics"] = sem
    if vmem_mib is not None:
        kw["vmem_limit_bytes"] = vmem_mib * MIB
    return pltpu.CompilerParams(**kw)


def _resident(shape):
    return pl.BlockSpec(shape, lambda *_: (0,) * len(shape), pipeline_mode=pl.Buffered(1))


def _dot(a, b, ca, cb):
    return lax.dot_general(a, b, (((ca,), (cb,)), ((), ())), preferred_element_type=F32)


def _rms_bwd(dy, x, r, g):
    dg = jnp.sum(dy * (x * r), axis=0, keepdims=True)
    dgx = dy * g
    dx = r * dgx - x * (r * r * r) * jnp.mean(x * dgx, axis=-1, keepdims=True)
    return dx, dg


def _where():
    x, y, c = lax.axis_index("x"), lax.axis_index("y"), lax.axis_index("c")
    return x, y, c, [(1 - x, y), (x, 1 - y), (1 - x, 1 - y)]


def _rcopy(src, dst, ssem, rsem, dev):
    return pltpu.make_async_remote_copy(src_ref=src, dst_ref=dst, send_sem=ssem, recv_sem=rsem, device_id=dev,
                                        device_id_type=MESH)


class _Task:
    def __init__(self, ins, outs, alias, n_sem, start, finish):
        self.ins, self.outs, self.alias, self.n_sem, self.start, self.finish = ins, outs, alias, n_sem, start, finish


def _t_gather(placed):
    R = placed.shape[0] // N_CHIPS

    def rows(chip_index, core):
        return pl.ds(pl.multiple_of(chip_index * R + core * (R // 2), 16), R // 2)

    def start(cin, cout, ss, rs, b):
        x, y, c, chips = _where()
        mine = cout[0].at[rows(2 * x + y, c)]
        for r, (px, py) in enumerate(chips):
            _rcopy(mine, mine, ss.at[b + r], rs.at[b + r], (px, py, c)).start()

    def finish(cin, cout, ss, rs, b):
        x, y, c, chips = _where()
        buf = cout[0]
        sib = (x, y, 1 - c)
        for r, (px, py) in enumerate(chips):
            got = buf.at[rows(2 * px + py, c)]
            _rcopy(got, got, ss.at[b + r], rs.at[b + r], (px, py, c)).wait_recv()
            _rcopy(got, got, ss.at[b + 3 + r], rs.at[b + 3 + r], sib).start()
        for r, (px, py) in enumerate(chips):
            got = buf.at[rows(2 * px + py, 1 - c)]
            _rcopy(got, got, ss.at[b + 3 + r], rs.at[b + 3 + r], sib).wait_recv()
        mine = buf.at[rows(2 * x + y, c)]
        for r in range(6):
            _rcopy(mine, mine, ss.at[b + r], rs.at[b + r], sib).wait_send()

    return _Task([placed], [SDS(placed.shape, placed.dtype)], [(0, 0)], 6, start, finish)


def _t_small_weights(buf):
    def start(cin, cout, ss, rs, b):
        x, y, c, chips = _where()
        mine = cout[0].at[2 * x + y]
        for r, (px, py) in enumerate(chips):
            _rcopy(mine, mine, ss.at[b + r], rs.at[b + r], (px, py, c)).start()

    def finish(cin, cout, ss, rs, b):
        x, y, c, chips = _where()
        for r, (px, py) in enumerate(chips):
            got = cout[0].at[2 * px + py]
            _rcopy(got, got, ss.at[b + r], rs.at[b + r], (px, py, c)).wait_recv()
        for r, (px, py) in enumerate(chips):
            mine = cout[0].at[2 * x + y]
            _rcopy(mine, mine, ss.at[b + r], rs.at[b + r], (px, py, c)).wait_send()

    return _Task([buf], [SDS(buf.shape, buf.dtype)], [(0, 0)], 3, start, finish)


def _t_sibling(gbf):
    def start(cin, cout, ss, rs, b):
        x, y, c, _ = _where()
        for jj in range(N_CHIPS):
            _rcopy(cin[0].at[2 * jj + (1 - c)], cout[0].at[jj], ss.at[b + jj], rs.at[b + jj], (x, y, 1 - c)).start()

    def finish(cin, cout, ss, rs, b):
        x, y, c, _ = _where()
        for jj in range(N_CHIPS):
            got = cout[0].at[jj]
            _rcopy(got, got, ss.at[b + jj], rs.at[b + jj], (x, y, 1 - c)).wait_recv()
        for jj in range(N_CHIPS):
            got = cout[0].at[jj]
            _rcopy(got, got, ss.at[b + jj], rs.at[b + jj], (x, y, 1 - c)).wait_send()

    return _Task([gbf], [SDS((N_CHIPS,) + gbf.shape[1:], BF)], [], N_CHIPS, start, finish)


def _t_chips(pbf):
    def start(cin, cout, ss, rs, b):
        x, y, c, chips = _where()
        for r, (px, py) in enumerate(chips):
            _rcopy(cin[0].at[2 * px + py], cout[0].at[r], ss.at[b + r], rs.at[b + r], (px, py, c)).start()

    def finish(cin, cout, ss, rs, b):
        x, y, c, chips = _where()
        for r, (px, py) in enumerate(chips):
            got = cout[0].at[r]
            _rcopy(got, got, ss.at[b + r], rs.at[b + r], (px, py, c)).wait_recv()
        for r, (px, py) in enumerate(chips):
            got = cout[0].at[r]
            _rcopy(got, got, ss.at[b + r], rs.at[b + r], (px, py, c)).wait_send()

    return _Task([pbf], [SDS((3,) + pbf.shape[1:], BF)], [], 3, start, finish)


def _t_swap(fin):
    def start(cin, cout, ss, rs, b):
        x, y, c, _ = _where()
        mine = cout[0].at[c]
        _rcopy(mine, mine, ss.at[b], rs.at[b], (x, y, 1 - c)).start()

    def finish(cin, cout, ss, rs, b):
        x, y, c, _ = _where()
        got = cout[0].at[1 - c]
        _rcopy(got, got, ss.at[b], rs.at[b], (x, y, 1 - c)).wait_recv()
        _rcopy(got, got, ss.at[b], rs.at[b], (x, y, 1 - c)).wait_send()

    return _Task([fin], [SDS(fin.shape, fin.dtype)], [(0, 0)], 1, start, finish)


def _t_allgather(buf):
    def peers():
        x, y, c, _ = _where()
        out = []
        for rel in range(1, N_DEV):
            px, py, pc = x ^ ((rel >> 2) & 1), y ^ ((rel >> 1) & 1), c ^ (rel & 1)
            out.append((rel - 1, 4 * px + 2 * py + pc, (px, py, pc)))
        return 4 * x + 2 * y + c, out

    def start(cin, cout, ss, rs, b):
        me, ps = peers()
        mine = cout[0].at[me]
        for k, _, dev in ps:
            _rcopy(mine, mine, ss.at[b + k], rs.at[b + k], dev).start()

    def finish(cin, cout, ss, rs, b):
        me, ps = peers()
        for k, pidx, dev in ps:
            got = cout[0].at[pidx]
            _rcopy(got, got, ss.at[b + k], rs.at[b + k], dev).wait_recv()
        for k, _, dev in ps:
            mine = cout[0].at[me]
            _rcopy(mine, mine, ss.at[b + k], rs.at[b + k], dev).wait_send()

    return _Task([buf], [SDS(buf.shape, buf.dtype)], [(0, 0)], N_DEV - 1, start, finish)


def _run_tasks(comm, which, cin, cout, ss, rs):
    i0 = o0 = s0 = 0
    for t in comm:
        getattr(t, which)(cin[i0:i0 + len(t.ins)], cout[o0:o0 + len(t.outs)], ss, rs, s0)
        i0, o0, s0 = i0 + len(t.ins), o0 + len(t.outs), s0 + t.n_sem


def _from_hbm(*arrays):
    return [pltpu.with_memory_space_constraint(a, pltpu.HBM) for a in arrays]


def _in_hbm(shapes):
    return [pltpu.HBM(s.shape, s.dtype) for s in shapes]


def _comm_layout(comm, n_in, n_out):
    c_in = [a for t in comm for a in t.ins]
    c_out = [s for t in comm for s in t.outs]
    aliases, i0, o0 = {}, 0, 0
    for t in comm:
        for i, o in t.alias:
            aliases[n_in + i0 + i] = n_out + o0 + o
        i0, o0 = i0 + len(t.ins), o0 + len(t.outs)
    return c_in, c_out, aliases, sum(t.n_sem for t in comm)


def _call(body, operands, *, name, grid, in_specs, out_specs, out_shape, scratch_shapes=(), sem=None, vmem_mib=None, comm=(),
          free=(), prefetch=()):
    operands = [o if s.memory_space == pltpu.SMEM or k in free else pltpu.with_memory_space_constraint(o, pltpu.HBM)
                for k, (o, s) in enumerate(zip(operands, in_specs))]
    n_pre, n_in, n_out, n_scr = len(prefetch), len(in_specs), len(out_specs), len(scratch_shapes)
    c_in, c_out, aliases, n_sem = _comm_layout(comm, n_pre + n_in, n_out)
    sems = [pltpu.SemaphoreType.DMA((n_sem,)), pltpu.SemaphoreType.DMA((n_sem,))] if comm else []

    def wrapped(*refs):
        pre, refs = refs[:n_pre], refs[n_pre:]
        ins, cin = refs[:n_in], refs[n_in:n_in + len(c_in)]
        rest = refs[n_in + len(c_in):]
        outs, cout = rest[:n_out], rest[n_out:n_out + len(c_out)]
        rest = rest[n_out + len(c_out):]
        scr, csem = rest[:n_scr], rest[n_scr:]
        if not comm:
            return body(*pre, *ins, *outs, *scr)
        ids = [pl.program_id(k) for k in range(len(grid))]
        first = functools.reduce(jnp.logical_and, [i == 0 for i in ids])
        last = functools.reduce(jnp.logical_and, [i == n - 1 for i, n in zip(ids, grid)])
        pl.when(first)(lambda: _run_tasks(comm, "start", cin, cout, *csem))
        body(*pre, *ins, *outs, *scr)
        pl.when(last)(lambda: _run_tasks(comm, "finish", cin, cout, *csem))

    grid_spec = pltpu.PrefetchScalarGridSpec(
        num_scalar_prefetch=n_pre, grid=grid, in_specs=list(in_specs) + [ANY] * len(c_in),
        out_specs=list(out_specs) + [ANY] * len(c_out), scratch_shapes=list(scratch_shapes) + sems)
    return _pcall(
        wrapped, name=name, grid_spec=grid_spec, out_shape=_in_hbm(list(out_shape) + c_out), input_output_aliases=aliases,
        compiler_params=_params(("arbitrary",) * len(grid) if comm else sem, vmem_mib),
    )(*prefetch, *operands, *_from_hbm(*c_in))


def _comm_call(name, comm):
    c_in, c_out, aliases, n_sem = _comm_layout(comm, 0, 0)

    def body(*refs):
        cin, cout, (ss, rs) = refs[:len(c_in)], refs[len(c_in):len(c_in) + len(c_out)], refs[len(c_in) + len(c_out):]
        _run_tasks(comm, "start", cin, cout, ss, rs)
        _run_tasks(comm, "finish", cin, cout, ss, rs)

    return _pcall(
        body, name=name, in_specs=[ANY] * len(c_in), out_specs=[ANY] * len(c_out), out_shape=_in_hbm(c_out),
        scratch_shapes=[pltpu.SemaphoreType.DMA((n_sem,)), pltpu.SemaphoreType.DMA((n_sem,))],
        input_output_aliases=aliases,
    )(*_from_hbm(*c_in))


def _inproj(x, g1, w_int, comm=()):
    tm = TM

    def body(x_ref, g_ref, w_ref, proj_ref, u_ref):
        xf = x_ref[...]
        r = lax.rsqrt(jnp.mean(xf * xf, axis=-1, keepdims=True) + EPS)
        u = (xf * r * g_ref[...]).astype(BF)
        u_ref[...] = u
        proj_ref[...] = _dot(u, w_ref[...], 1, 1)

    return _call(
        body, (x, g1, w_int), name="inproj", grid=(T // tm,),
        in_specs=[pl.BlockSpec((tm, D), lambda i: (i, 0)), pl.BlockSpec((1, D), lambda i: (0, 0)),
                  _resident((INW, D))],
        out_specs=[pl.BlockSpec((tm, INW), lambda i: (i, 0)), pl.BlockSpec((tm, D), lambda i: (i, 0))],
        out_shape=[SDS((T, INW), F32), SDS((T, D), BF)], sem=("parallel",), vmem_mib=40, comm=comm)


def _outproj(y, w_out, x, g2):
    tm = TM

    def body(y_ref, w_ref, x_ref, g_ref, h1_ref, u2_ref):
        h1 = x_ref[...] + _dot(y_ref[...], w_ref[...], 1, 0)
        h1_ref[...] = h1
        r = lax.rsqrt(jnp.mean(h1 * h1, axis=-1, keepdims=True) + EPS)
        u2_ref[...] = (h1 * r * g_ref[...]).astype(BF)

    return _call(
        body, (y, w_out, x, g2), name="outproj", grid=(T // tm,),
        in_specs=[pl.BlockSpec((tm, D), lambda i: (i, 0)), _resident((D, D)),
                  pl.BlockSpec((tm, D), lambda i: (i, 0)), pl.BlockSpec((1, D), lambda i: (0, 0))],
        out_specs=[pl.BlockSpec((tm, D), lambda i: (i, 0)), pl.BlockSpec((tm, D), lambda i: (i, 0))],
        out_shape=[SDS((T, D), F32), SDS((T, D), BF)], sem=("parallel",), vmem_mib=32)


def _ffn_up(u2, w_upt, comm=()):
    tm, tn = 1024, 512

    def body(u_ref, w_ref, o_ref):
        o_ref[...] = _dot(u_ref[...], w_ref[...], 1, 1).astype(BF)

    return _call(
        body, (u2, w_upt), name="ffn_up", grid=(T // tm, 2 * DFF // tn),
        in_specs=[pl.BlockSpec((tm, D), lambda i, j: (i, 0)), pl.BlockSpec((tn, D), lambda i, j: (j, 0))],
        out_specs=[pl.BlockSpec((tm, tn), lambda i, j: (i, j))], out_shape=[SDS((T, 2 * DFF), BF)],
        sem=("parallel", "parallel"), vmem_mib=32, comm=comm)


def _ffn_down(a, w_down, h1, tgt):
    tm = TM

    def body(a_ref, w_ref, h1_ref, t_ref, dh_ref, dhb_ref, l_ref):
        @pl.when(pl.program_id(0) == 0)
        def _():
            l_ref[...] = jnp.zeros_like(l_ref)

        h2 = h1_ref[...] + _dot(a_ref[...], w_ref[...], 1, 0)
        e = h2 - t_ref[...]
        dh = e * (1.0 / D)
        dh_ref[...] = dh
        dhb_ref[...] = dh.astype(BF)
        e2 = jnp.sum((e * e).reshape(tm // 8, 8, D), axis=0)
        acc = e2[:, 0:128]
        for k in range(1, D // 128):
            acc = acc + e2[:, k * 128:(k + 1) * 128]
        l_ref[...] += acc

    return _call(
        body, (a, w_down, h1, tgt), name="ffn_down", grid=(T // tm,),
        in_specs=[pl.BlockSpec((tm, DFF), lambda i: (i, 0)), _resident((DFF, D)),
                  pl.BlockSpec((tm, D), lambda i: (i, 0)), pl.BlockSpec((tm, D), lambda i: (i, 0))],
        out_specs=[pl.BlockSpec((tm, D), lambda i: (i, 0)), pl.BlockSpec((tm, D), lambda i: (i, 0)),
                   pl.BlockSpec((8, 128), lambda i: (0, 0))],
        out_shape=[SDS((T, D), F32), SDS((T, D), BF), SDS((8, 128), F32)], sem=("arbitrary",), vmem_mib=40)


def _bucket_table():
    q = np.arange(BLK, dtype=np.int32)[:, None]
    j = np.arange(2 * BLK, dtype=np.int32)[None, :]
    n = np.maximum(q + BLK - j, 0)
    nf = np.maximum(n, 1).astype(np.float32)
    max_exact = NBUCKET // 2
    large = max_exact + (np.log(nf / np.float32(max_exact)) / np.float32(math.log(BLK / max_exact))
                         * np.float32(NBUCKET - max_exact)).astype(np.int32)
    large = np.minimum(large, NBUCKET - 1)
    return np.where(n < max_exact, n, large).astype(np.int32)


def _band_bias_bwd(dbias, bucket, me):
    def body(me_ref, db_ref, bk_ref, o_ref):
        bk = bk_ref[...]
        for b in range(NBUCKET):
            m = bk == b
            for h in range(NH):
                v = jnp.where(m, db_ref[h * BLK:(h + 1) * BLK, :], 0.0)
                s = jnp.sum(jnp.sum(v, axis=1, keepdims=True), axis=0, keepdims=True)
                o_ref[0, h:h + 1, b:b + 1] = s

    grid_spec = pltpu.PrefetchScalarGridSpec(
        num_scalar_prefetch=1, grid=(1,),
        in_specs=[pl.BlockSpec((NH * BLK, 2 * BLK), lambda i, me_ref: (0, 0)),
                  pl.BlockSpec((BLK, 2 * BLK), lambda i, me_ref: (0, 0))],
        out_specs=pl.BlockSpec((1, NH, NBUCKET), lambda i, me_ref: (me_ref[0], 0, 0)),
    )
    return _pcall(body, name="band_bias_bwd", grid_spec=grid_spec, out_shape=SDS((N_DEV, NH, NBUCKET), F32),
                  compiler_params=_params(("arbitrary",)))(me, dbias, bucket)


def _two_bf16(x):
    hi = x.astype(BF)
    return hi, (x - hi.astype(F32)).astype(BF)


def _head_sums(x, seg):
    hi, lo = _two_bf16(x)
    s = seg[0:x.shape[1], :]
    return _dot(hi, s, 1, 0) + _dot(lo, s, 1, 0)


def _head_spread(v, seg, width):
    hi, lo = _two_bf16(v)
    s = seg[0:width, :]
    return _dot(hi, s, 1, 1) + _dot(lo, s, 1, 1)


def _head_norm(x, g_t, seg):
    r = lax.rsqrt(_head_sums(x * x, seg) * (1.0 / HD) + EPS)
    r = _head_spread(r, seg, x.shape[1])
    return x * r * g_t, r


def _head_norm_bwd(dy, x, r, g_t, seg):
    dg_t = jnp.sum(dy * (x * r), axis=0, keepdims=True)
    dgx = dy * g_t
    mean = _head_spread(_head_sums(x * dgx, seg) * (1.0 / HD), seg, x.shape[1])
    return r * dgx - x * (r * r * r) * mean, dg_t


def _fold_heads(v):
    out = v[:, 0:HD]
    for h in range(1, v.shape[1] // HD):
        out = out + v[:, h * HD:(h + 1) * HD]
    return out


def _mix_forward(P, zc8, zh8, pkv, first, cw, qg_t, kg_t, gco, gao, seg, sink_ref, bias_ref):
    gate_b = P[:, 0:CW]
    gate_c = P[:, CW:2 * CW]
    hc = P[:, 2 * CW:3 * CW]
    z = gate_c * hc
    keep = jnp.where(first, 0.0, 1.0)
    zp = zc8 * zh8 * keep
    p1 = zp[7:8, :]
    p2 = zp[6:7, :]
    row = lax.broadcasted_iota(jnp.int32, (BLK, 1), 0)
    z1 = jnp.where(row == 0, p1, pltpu.roll(z, 1, 0))
    z2 = jnp.where(row == 0, p2, jnp.where(row == 1, p1, pltpu.roll(z, 2, 0)))
    cz = cw[0:1, :] * z2 + cw[1:2, :] * z1 + cw[2:3, :] * z
    y_conv = gate_b * cz

    scale = HD ** -0.5
    qi = lax.broadcasted_iota(jnp.int32, (GQ * BLK, 2 * BLK), 0) & (BLK - 1)
    kj = lax.broadcasted_iota(jnp.int32, (GQ * BLK, 2 * BLK), 1)
    dd = qi + BLK - kj
    first_key = jnp.where(first, BLK, 0)
    valid = (dd >= 0) & (dd < BLK) & (kj >= first_key)

    q0 = 3 * CW
    k0 = q0 + AW
    v0 = k0 + NKV * HD
    q_raw = P[:, q0:k0]
    qn, rq = _head_norm(q_raw, qg_t, seg)
    qs = (qn * scale).astype(BF)
    k_raw = jnp.concatenate([pkv[:, 0:NKV * HD], P[:, k0:v0]], axis=0)
    kn, rk = _head_norm(k_raw, kg_t, seg)
    knb = kn.astype(BF)
    heads = []
    outs = []
    for kv in range(NKV):
        kb = knb[:, kv * HD:(kv + 1) * HD]
        vb = jnp.concatenate([pkv[:, NKV * HD + kv * HD:NKV * HD + (kv + 1) * HD],
                              P[:, v0 + kv * HD:v0 + (kv + 1) * HD]], axis=0).astype(BF)
        Q = jnp.concatenate([qs[:, (kv * GQ + g) * HD:(kv * GQ + g + 1) * HD] for g in range(GQ)], axis=0)
        S = _dot(Q, kb, 1, 1) + bias_ref[kv * GQ * BLK:(kv + 1) * GQ * BLK, :]
        S = jnp.where(valid, S, NEG_INF)
        sink = jnp.concatenate([jnp.full((BLK, 1), sink_ref[0, kv * GQ + g], F32) for g in range(GQ)], axis=0)
        m = jnp.maximum(jnp.max(S, axis=-1, keepdims=True), sink)
        p = jnp.exp(S - m)
        es = jnp.exp(sink - m)
        denom = jnp.sum(p, axis=-1, keepdims=True) + es
        probs = p / denom
        O = _dot(probs.astype(BF), vb, 1, 0)
        heads.append(dict(kb=kb, vb=vb, Q=Q, probs=probs, psink=es / denom, O=O))
        outs += [O[g * BLK:(g + 1) * BLK, :] for g in range(GQ)]
    y_attn = jnp.concatenate(outs, axis=1)

    rc = lax.rsqrt(jnp.mean(y_conv * y_conv, axis=-1, keepdims=True) + EPS)
    ra = lax.rsqrt(jnp.mean(y_attn * y_attn, axis=-1, keepdims=True) + EPS)
    y = jnp.concatenate([y_conv * rc * gco, y_attn * ra * gao], axis=1)
    return dict(gate_b=gate_b, gate_c=gate_c, hc=hc, z=z, z1=z1, z2=z2, cz=cz, y_conv=y_conv, y_attn=y_attn,
                rc=rc, ra=ra, heads=heads, y=y, row=row, scale=scale, q_raw=q_raw, rq=rq, k_raw=k_raw, rk=rk)


BPS = 2
TILE = BPS * BLK
KV0 = 3 * CW + AW


def _mix_in_specs(tile_of):
    return [
        pl.BlockSpec(memory_space=pltpu.SMEM),
        pl.BlockSpec((TILE, INW), lambda s: (tile_of(s), 0)),
        pl.BlockSpec((8, CW), lambda s: (jnp.maximum(tile_of(s) * (TILE // 8) - 1, 0), 1)),
        pl.BlockSpec((8, CW), lambda s: (jnp.maximum(tile_of(s) * (TILE // 8) - 1, 0), 2)),
        pl.BlockSpec((BLK, 2 * NKV * HD), lambda s: (jnp.maximum(tile_of(s) * BPS - 1, 0), KV0 // (2 * NKV * HD))),
    ]


def _block_inputs(tile, b, zc_ref, zh_ref, pkv_ref, first_tile):
    P = tile[b * BLK:(b + 1) * BLK, :]
    if b == 0:
        return P, zc_ref[...], zh_ref[...], pkv_ref[...], first_tile
    lo = b * BLK
    return P, tile[lo - 8:lo, CW:2 * CW], tile[lo - 8:lo, 2 * CW:3 * CW], tile[lo - BLK:lo, KV0:KV0 + 2 * NKV * HD], False


def _mix_param_specs():
    return [
        pl.BlockSpec((8, CW), lambda s: (0, 0)),
        pl.BlockSpec((1, AW), lambda s: (0, 0)),
        pl.BlockSpec((1, NKV * HD), lambda s: (0, 0)),
        pl.BlockSpec((1, CW), lambda s: (0, 0)),
        pl.BlockSpec((1, AW), lambda s: (0, 0)),
        pl.BlockSpec((AW, 128), lambda s: (0, 0)),
        pl.BlockSpec((NH * BLK, 2 * BLK), lambda s: (0, 0)),
    ]


def _mix_params(cw8, qg, kg, gco, gao, bias):
    seg = np.zeros((AW, 128), np.float32)
    seg[np.arange(AW), np.arange(AW) // HD] = 1.0
    return (cw8, jnp.tile(qg, (1, NH)), jnp.tile(kg, (1, NKV)), gco, gao, jnp.asarray(seg, BF), bias)


def _mix_fwd(proj, sinks, cw8, qg, kg, gco, gao, bias, comm=()):
    def body(sink_ref, p_ref, zc_ref, zh_ref, pkv_ref, cw_ref, qg_ref, kg_ref, gco_ref, gao_ref, seg_ref, bias_ref, y_ref):
        tile = p_ref[...]
        for b in range(BPS):
            f = _mix_forward(*_block_inputs(tile, b, zc_ref, zh_ref, pkv_ref, pl.program_id(0) == 0), cw_ref[...],
                             qg_ref[...], kg_ref[...], gco_ref[...], gao_ref[...], seg_ref[...], sink_ref, bias_ref)
            y_ref[b * BLK:(b + 1) * BLK, :] = f["y"].astype(BF)

    return _call(
        body, (sinks, proj, proj, proj, proj, *_mix_params(cw8, qg, kg, gco, gao, bias)), name="mix_fwd", grid=(T // TILE,),
        in_specs=_mix_in_specs(lambda s: s) + _mix_param_specs(),
        out_specs=[pl.BlockSpec((TILE, D), lambda s: (s, 0))], out_shape=[SDS((T, D), BF)],
        sem=("parallel",), vmem_mib=40, comm=comm)


def _mix_bwd(proj, dy, sinks, cw8, qg, kg, gco, gao, bias, comm=()):
    n_steps = T // TILE

    def tile_of(s):
        return n_steps - 1 - s

    def body(sink_ref, p_ref, zc_ref, zh_ref, pkv_ref, dy_ref, cw_ref, qg_ref, kg_ref, gco_ref, gao_ref, seg_ref, bias_ref,
             dproj_ref, dcw_ref, dqg_ref, dkg_ref, dgco_ref, dgao_ref, dsink_ref, dbias_ref,
             ndcz_ref, dkc_ref, dvc_ref):
        s = pl.program_id(0)

        @pl.when(s == 0)
        def _():
            for r in (dcw_ref, dqg_ref, dkg_ref, dgco_ref, dgao_ref, dsink_ref, dbias_ref, ndcz_ref, dkc_ref, dvc_ref):
                r[...] = jnp.zeros_like(r)

        tile = p_ref[...]
        for b in reversed(range(BPS)):
            one_block(b, _block_inputs(tile, b, zc_ref, zh_ref, pkv_ref, s == n_steps - 1),
                      dy_ref[b * BLK:(b + 1) * BLK, :], sink_ref, cw_ref, qg_ref, kg_ref, gco_ref, gao_ref, seg_ref, bias_ref,
                      dproj_ref.at[b * BLK:(b + 1) * BLK, :], dcw_ref, dqg_ref, dkg_ref, dgco_ref, dgao_ref, dsink_ref,
                      dbias_ref, ndcz_ref, dkc_ref, dvc_ref)

    def one_block(b, inputs, dy, sink_ref, cw_ref, qg_ref, kg_ref, gco_ref, gao_ref, seg_ref, bias_ref,
                  dproj_ref, dcw_ref, dqg_ref, dkg_ref, dgco_ref, dgao_ref, dsink_ref, dbias_ref,
                  ndcz_ref, dkc_ref, dvc_ref):
        cw = cw_ref[...]
        qg_v, kg_v, gco_v, gao_v, seg = qg_ref[...], kg_ref[...], gco_ref[...], gao_ref[...], seg_ref[...]
        f = _mix_forward(*inputs, cw, qg_v, kg_v, gco_v, gao_v, seg, sink_ref, bias_ref)
        dyc, dgco = _rms_bwd(dy[:, 0:CW], f["y_conv"], f["rc"], gco_v)
        dya, dgao = _rms_bwd(dy[:, CW:CW + AW], f["y_attn"], f["ra"], gao_v)
        dgco_ref[...] += dgco
        dgao_ref[...] += dgao

        row = f["row"]
        dgate_b = dyc * f["cz"]
        dcz = dyc * f["gate_b"]
        dcw_ref[0:1, :] += jnp.sum(dcz * f["z2"], axis=0, keepdims=True)
        dcw_ref[1:2, :] += jnp.sum(dcz * f["z1"], axis=0, keepdims=True)
        dcw_ref[2:3, :] += jnp.sum(dcz * f["z"], axis=0, keepdims=True)
        nxt = ndcz_ref[...]
        n0 = nxt[0:1, :]
        n1 = nxt[1:2, :]
        d1 = jnp.where(row == BLK - 1, n0, pltpu.roll(dcz, BLK - 1, 0))
        d2 = jnp.where(row == BLK - 1, n1, jnp.where(row == BLK - 2, n0, pltpu.roll(dcz, BLK - 2, 0)))
        dz = cw[2:3, :] * dcz + cw[1:2, :] * d1 + cw[0:1, :] * d2
        ndcz_ref[...] = dcz[0:8, :]
        dproj_ref[:, 0:CW] = dgate_b.astype(BF)
        dproj_ref[:, CW:2 * CW] = (dz * f["hc"]).astype(BF)
        dproj_ref[:, 2 * CW:3 * CW] = (dz * f["gate_c"]).astype(BF)

        scale = f["scale"]
        lane = lax.broadcasted_iota(jnp.int32, (1, 128), 1)
        dq_cols, dk_cols, dv_cols = [], [], []
        for kv in range(NKV):
            hd = f["heads"][kv]
            dO = jnp.concatenate([dya[:, (kv * GQ + g) * HD:(kv * GQ + g + 1) * HD] for g in range(GQ)], axis=0)
            delta = jnp.sum(dO * hd["O"], axis=-1, keepdims=True)
            dOb = dO.astype(BF)
            dP = _dot(dOb, hd["vb"], 1, 1)
            dS = hd["probs"] * (dP - delta)
            dsk = hd["psink"] * delta
            for g in range(GQ):
                h = kv * GQ + g
                tot = jnp.sum(dsk[g * BLK:(g + 1) * BLK, :], axis=0, keepdims=True)
                dsink_ref[...] -= jnp.where(lane == h, tot, 0.0)
            dbias_ref[kv * GQ * BLK:(kv + 1) * GQ * BLK, :] += dS
            dSb = dS.astype(BF)
            dQ = _dot(dSb, hd["kb"], 1, 0)
            dKb = _dot(dSb, hd["Q"], 0, 0)
            dVb = _dot(hd["probs"].astype(BF), dOb, 0, 0)
            dk_cols.append(dKb[BLK:, :] + dkc_ref[:, kv * HD:(kv + 1) * HD])
            dv_cols.append(dVb[BLK:, :] + dvc_ref[:, kv * HD:(kv + 1) * HD])
            dkc_ref[:, kv * HD:(kv + 1) * HD] = dKb[:BLK, :]
            dvc_ref[:, kv * HD:(kv + 1) * HD] = dVb[:BLK, :]
            dq_cols += [dQ[g * BLK:(g + 1) * BLK, :] for g in range(GQ)]
        dq_raw, dqg_t = _head_norm_bwd(jnp.concatenate(dq_cols, axis=1) * scale, f["q_raw"], f["rq"], qg_v, seg)
        dk_raw, dkg_t = _head_norm_bwd(jnp.concatenate(dk_cols, axis=1), f["k_raw"][BLK:, :], f["rk"][BLK:, :], kg_v, seg)
        dqg_ref[...] += _fold_heads(dqg_t)
        dkg_ref[...] += _fold_heads(dkg_t)
        dproj_ref[:, 3 * CW:INW] = jnp.concatenate([dq_raw, dk_raw] + dv_cols, axis=1).astype(BF)

    small = lambda r, c: pl.BlockSpec((r, c), lambda s: (0, 0))
    return _call(
        body, (sinks, proj, proj, proj, proj, dy, *_mix_params(cw8, qg, kg, gco, gao, bias)), name="mix_bwd", grid=(n_steps,),
        in_specs=_mix_in_specs(tile_of) + [pl.BlockSpec((TILE, D), lambda s: (tile_of(s), 0))] + _mix_param_specs(),
        out_specs=[pl.BlockSpec((TILE, INW), lambda s: (tile_of(s), 0)), small(8, CW), small(1, HD), small(1, HD),
                   small(1, CW), small(1, AW), small(1, 128), small(NH * BLK, 2 * BLK)],
        out_shape=[SDS((T, INW), BF), SDS((8, CW), F32), SDS((1, HD), F32), SDS((1, HD), F32), SDS((1, CW), F32),
                   SDS((1, AW), F32), SDS((1, 128), F32), SDS((NH * BLK, 2 * BLK), F32)],
        scratch_shapes=[pltpu.VMEM((8, CW), F32), pltpu.VMEM((BLK, NKV * HD), F32), pltpu.VMEM((BLK, NKV * HD), F32)],
        sem=("arbitrary",), vmem_mib=56, comm=comm)


FT = 256
NFT = DFF // FT
RC = 128
NCH = T // RC
LEAD = 16


def _rows8(x):
    return jnp.sum(x.reshape(x.shape[0] // 8, 8, x.shape[1]), axis=0)


def _ffn_act_specs():
    return [
        pl.BlockSpec((T, FT), lambda j: (0, j)), pl.BlockSpec((T, FT), lambda j: (0, NFT + j)),
        pl.BlockSpec((8, FT), lambda j: (0, j)), pl.BlockSpec((8, FT), lambda j: (0, NFT + j)),
        pl.BlockSpec((1, FT), lambda j: (0, j)), pl.BlockSpec((1, FT), lambda j: (0, NFT + j)),
    ]


def _conv_rows(win, w, b, n):
    win = win.astype(F32)
    u = win[LEAD:LEAD + n]
    u1 = pltpu.roll(win, 1, 0)[LEAD:LEAD + n]
    u2 = pltpu.roll(win, 2, 0)[LEAD:LEAD + n]
    return u2, u1, u, w[0:1, :] * u2 + w[1:2, :] * u1 + w[2:3, :] * u + b


def _ffn_act(up, fw8, fb):
    def body(ug_ref, uv_ref, wg_ref, wv_ref, bg_ref, bv_ref, a_ref):
        wg, wv, bg, bv = wg_ref[...], wv_ref[...], bg_ref[...], bv_ref[...]

        def chunk(win_g, win_v):
            gp = _conv_rows(win_g, wg, bg, RC)[3]
            vp = _conv_rows(win_v, wv, bv, RC)[3]
            return (gp * jax.nn.sigmoid(gp) * vp).astype(BF)

        zero = jnp.zeros((LEAD, FT), BF)
        a_ref[0:RC, :] = chunk(jnp.concatenate([zero, ug_ref[0:RC, :]], axis=0),
                               jnp.concatenate([zero, uv_ref[0:RC, :]], axis=0))

        def step(i, carry):
            r0 = pl.multiple_of(i * RC, RC)
            win = pl.ds(r0 - LEAD, RC + LEAD)
            a_ref[pl.ds(r0, RC), :] = chunk(ug_ref[win, :], uv_ref[win, :])
            return carry

        lax.fori_loop(1, NCH, step, 0)

    return _call(
        body, (up, up, fw8, fw8, fb, fb), name="ffn_act", grid=(NFT,), in_specs=_ffn_act_specs(),
        out_specs=[pl.BlockSpec((T, FT), lambda j: (0, j))], out_shape=[SDS((T, DFF), BF)],
        sem=("parallel",), vmem_mib=40)


def _ffn_act_bwd(up, da, fw8, fb, comm=()):
    ext = RC + LEAD

    def body(ug_ref, uv_ref, wg_ref, wv_ref, bg_ref, bv_ref, da_ref,
             dug_ref, duv_ref, dwg_ref, dwv_ref, dbg_ref, dbv_ref):
        wg, wv, bg, bv = wg_ref[...], wv_ref[...], bg_ref[...], bv_ref[...]

        def chunk(win_g, win_v, da_e):
            g2, g1, g0, gp = _conv_rows(win_g, wg, bg, ext)
            v2, v1, v0, vp = _conv_rows(win_v, wv, bv, ext)
            da_e = da_e.astype(F32)
            sig = jax.nn.sigmoid(gp)
            dvp = da_e * (gp * sig)
            dgp = da_e * vp * (sig * (1.0 + gp * (1.0 - sig)))

            def back(dp, w):
                return (w[2:3, :] * dp[0:RC] + w[1:2, :] * pltpu.roll(dp, ext - 1, 0)[0:RC]
                        + w[0:1, :] * pltpu.roll(dp, ext - 2, 0)[0:RC]).astype(BF)

            def sums(dp, u2, u1, u0):
                d = dp[0:RC]
                return [_rows8(d), _rows8(d * u2[0:RC]), _rows8(d * u1[0:RC]), _rows8(d * u0[0:RC])]

            return back(dgp, wg), back(dvp, wv), sums(dgp, g2, g1, g0) + sums(dvp, v2, v1, v0)

        zero = jnp.zeros((LEAD, FT), BF)
        dug, duv, acc = chunk(jnp.concatenate([zero, ug_ref[0:ext, :]], axis=0),
                              jnp.concatenate([zero, uv_ref[0:ext, :]], axis=0), da_ref[0:ext, :])
        dug_ref[0:RC, :] = dug
        duv_ref[0:RC, :] = duv

        def step(i, acc):
            r0 = pl.multiple_of(i * RC, RC)
            win = pl.ds(r0 - LEAD, ext + LEAD)
            dug, duv, part = chunk(ug_ref[win, :], uv_ref[win, :], da_ref[pl.ds(r0, ext), :])
            dug_ref[pl.ds(r0, RC), :] = dug
            duv_ref[pl.ds(r0, RC), :] = duv
            return [a + p for a, p in zip(acc, part)]

        acc = lax.fori_loop(1, NCH - 1, step, acc)
        r0 = T - RC
        tail = lambda ref, lo: jnp.concatenate([ref[lo:T, :], zero], axis=0)
        dug, duv, part = chunk(tail(ug_ref, r0 - LEAD), tail(uv_ref, r0 - LEAD), tail(da_ref, r0))
        dug_ref[r0:T, :] = dug
        duv_ref[r0:T, :] = duv
        tot = [jnp.sum(a + p, axis=0, keepdims=True) for a, p in zip(acc, part)]
        for k, (dw_ref, db_ref) in enumerate(((dwg_ref, dbg_ref), (dwv_ref, dbv_ref))):
            db_ref[...] = tot[4 * k]
            dw_ref[...] = jnp.zeros_like(dw_ref)
            for r in range(3):
                dw_ref[r:r + 1, :] = tot[4 * k + 1 + r]

    col = lambda r: pl.BlockSpec((r, FT), lambda j: (0, j))
    return _call(
        body, (up, up, fw8, fw8, fb, fb, da), name="ffn_act_bwd", grid=(NFT,),
        in_specs=_ffn_act_specs() + [pl.BlockSpec((T, FT), lambda j: (0, j))],
        out_specs=[col(T), col(T), col(8), col(8), col(1), col(1)],
        out_shape=[SDS((T, DFF), BF), SDS((T, DFF), BF), SDS((8, DFF), F32), SDS((8, DFF), F32),
                   SDS((1, DFF), F32), SDS((1, DFF), F32)],
        sem=("parallel",), vmem_mib=40, comm=comm)


def _ffn_down_bwd(dh2b, w_down, comm=()):
    tm = TM

    def body(d_ref, w_ref, o_ref):
        o_ref[...] = _dot(d_ref[...], w_ref[...], 1, 1).astype(BF)

    return _call(
        body, (dh2b, w_down), name="ffn_down_bwd", grid=(T // tm,),
        in_specs=[pl.BlockSpec((tm, D), lambda i: (i, 0)), _resident((DFF, D))],
        out_specs=[pl.BlockSpec((tm, DFF), lambda i: (i, 0))], out_shape=[SDS((T, DFF), BF)],
        sem=("parallel",), vmem_mib=40, comm=comm)


def _norm_matmul_bwd(name, a_list, w_t, k_offsets, xin, g, dres, want_bf16, comm=(), slot=None):
    tm = TM
    ks = [a.shape[1] for a in a_list]
    n_a = len(a_list)
    n_pre = 0 if slot is None else 1

    def body(*refs):
        refs = refs[n_pre:]
        a_refs = refs[:n_a]
        w_ref, x_ref, g_ref, r_ref = refs[n_a:n_a + 4]
        outs = refs[n_a + 4:]
        dx_ref, dg_ref = outs[0], (outs[-1] if slot is None else outs[-1].at[0])

        @pl.when(pl.program_id(0) == 0)
        def _():
            dg_ref[...] = jnp.zeros_like(dg_ref)

        du = _dot(a_refs[0][...], w_ref[k_offsets[0]:k_offsets[0] + ks[0], :], 1, 0)
        for k in range(1, n_a):
            du = du + _dot(a_refs[k][...], w_ref[k_offsets[k]:k_offsets[k] + ks[k], :], 1, 0)
        x = x_ref[...]
        r = lax.rsqrt(jnp.mean(x * x, axis=-1, keepdims=True) + EPS)
        dx, dg = _rms_bwd(du, x, r, g_ref[...])
        dx = r_ref[...] + dx
        dx_ref[...] = dx
        if want_bf16:
            outs[1][...] = dx.astype(BF)
        dg_ref[...] += dg

    tile = lambda c: pl.BlockSpec((tm, c), lambda i, *_: (i, 0))
    if slot is None:
        dg_spec, dg_shape = pl.BlockSpec((1, D), lambda i: (0, 0)), SDS((1, D), F32)
    else:
        dg_spec, dg_shape = pl.BlockSpec((1, 1, D), lambda i, slot_ref: (slot_ref[0], 0, 0)), SDS((N_DEV, 1, D), F32)
    out_specs = [tile(D)] + ([tile(D)] if want_bf16 else []) + [dg_spec]
    out_shape = [SDS((T, D), F32)] + ([SDS((T, D), BF)] if want_bf16 else []) + [dg_shape]
    return _call(
        body, (*a_list, w_t, xin, g, dres), name=name, grid=(T // tm,), prefetch=() if slot is None else (slot,),
        in_specs=[tile(k) for k in ks] + [_resident(w_t.shape), tile(D),
                                           pl.BlockSpec((1, D), lambda i, *_: (0, 0)), tile(D)],
        out_specs=out_specs, out_shape=out_shape, sem=("arbitrary",), vmem_mib=56, comm=comm)


def _out_bwd(dh1b, w_out, comm=()):
    tm = TM

    def body(d_ref, w_ref, o_ref):
        o_ref[...] = _dot(d_ref[...], w_ref[...], 1, 1)

    return _call(
        body, (dh1b, w_out), name="out_bwd", grid=(T // tm,),
        in_specs=[pl.BlockSpec((tm, D), lambda i: (i, 0)), _resident((D, D))],
        out_specs=[pl.BlockSpec((tm, D), lambda i: (i, 0))], out_shape=[SDS((T, D), F32)],
        sem=("parallel",), vmem_mib=32, comm=comm)


def _wgrad(name, a_list, b, comm=()):
    m_k = a_list[0].shape[1]
    tm = max(t for t in range(128, m_k // 2 + 1, 128) if m_k % t == 0)
    steps = [a.shape[1] // tm for a in a_list]
    starts = [sum(steps[:k]) for k in range(len(a_list))]
    n_a = len(a_list)

    def body(*refs):
        a_refs, b_ref, o_ref = refs[:n_a], refs[n_a], refs[n_a + 1]
        i = pl.program_id(0)
        for k in range(n_a):
            @pl.when((i >= starts[k]) & (i < starts[k] + steps[k]))
            def _(k=k):
                o_ref[...] = _dot(a_refs[k][...], b_ref[...], 0, 0).astype(BF)

    def a_spec(k):
        return pl.BlockSpec((T, tm), lambda i: (0, jnp.clip(i - starts[k], 0, steps[k] - 1)))

    m_total = tm * sum(steps)
    return _call(
        body, (*a_list, b), name=name, grid=(sum(steps),),
        in_specs=[a_spec(k) for k in range(n_a)] + [_resident((T, D))],
        out_specs=[pl.BlockSpec((tm, D), lambda i: (i, 0))], out_shape=[SDS((m_total, D), BF)],
        sem=("parallel",), vmem_mib=40, comm=comm)


def _chip_sum(name, gbf, from_sib, core, chip):
    h = gbf.shape[1]
    th = h // 2

    def body(core_ref, chip_ref, g_ref, s_ref, pbf_ref, own_ref):
        p = g_ref[0].astype(F32) + s_ref[0].astype(F32)
        pbf_ref[0] = p.astype(BF)

        @pl.when(pl.program_id(1) == chip_ref[0])
        def _():
            own_ref[...] = p

    grid_spec = pltpu.PrefetchScalarGridSpec(
        num_scalar_prefetch=2, grid=(h // th, N_CHIPS),
        in_specs=[pl.BlockSpec((1, th, D), lambda t, jj, core_ref, chip_ref: (2 * jj + core_ref[0], t, 0)),
                  pl.BlockSpec((1, th, D), lambda t, jj, core_ref, chip_ref: (jj, t, 0))],
        out_specs=[pl.BlockSpec((1, th, D), lambda t, jj, core_ref, chip_ref: (jj, t, 0)),
                   pl.BlockSpec((th, D), lambda t, jj, core_ref, chip_ref: (t, 0))],
    )
    return _pcall(
        body, name=name, grid_spec=grid_spec, out_shape=_in_hbm([SDS((N_CHIPS, h, D), BF), SDS((h, D), F32)]),
        compiler_params=_params(("arbitrary", "arbitrary"), 32),
    )(core, chip, *_from_hbm(gbf, from_sib))


def _final_sum(name, own, from_chips, core):
    h = own.shape[0]

    def body(core_ref, o_ref, r_ref, f_ref):
        f_ref[0] = ((o_ref[...] + r_ref[0].astype(F32)) + r_ref[1].astype(F32)) + r_ref[2].astype(F32)

    grid_spec = pltpu.PrefetchScalarGridSpec(
        num_scalar_prefetch=1, grid=(1,),
        in_specs=[pl.BlockSpec((h, D), lambda i, core_ref: (0, 0)), pl.BlockSpec((3, h, D), lambda i, core_ref: (0, 0, 0))],
        out_specs=pl.BlockSpec((1, h, D), lambda i, core_ref: (core_ref[0], 0, 0)),
    )
    return _pcall(body, name=name, grid_spec=grid_spec, out_shape=pltpu.HBM((2, h, D), F32),
                  compiler_params=_params(("arbitrary",), 40))(core, *_from_hbm(own, from_chips))


def _adam_math(w, g, m, v):
    nm = ADAM_B1 * m + (1.0 - ADAM_B1) * g
    nv = ADAM_B2 * v + (1.0 - ADAM_B2) * (g * g)
    m_hat = nm / (1.0 - ADAM_B1 ** ADAM_STEP)
    v_hat = nv / (1.0 - ADAM_B2 ** ADAM_STEP)
    return -ADAM_LR * (m_hat / (jnp.sqrt(v_hat) + ADAM_EPS) + ADAM_WD * w), nm, nv


def _adamw(name, w, g, m, v, tr, copy_g=False):
    rows, cols = w.shape

    def body(w_ref, g_ref, m_ref, v_ref, *outs):
        g_val = g_ref[...]
        if copy_g:
            outs[0][...] = g_val
        d_ref, nm_ref, nv_ref = outs[-3:]
        d_ref[...], nm_ref[...], nv_ref[...] = _adam_math(w_ref[...], g_val, m_ref[...], v_ref[...])

    spec = pl.BlockSpec((tr, cols), lambda i: (i, 0))
    n_out = 4 if copy_g else 3
    return _call(body, (w, g, m, v), name=name, grid=(rows // tr,), in_specs=[spec] * 4, out_specs=[spec] * n_out,
                 out_shape=[SDS((rows, cols), F32)] * n_out, sem=("parallel",), vmem_mib=32, free=(0, 2, 3))


C_G1, C_G2, C_GCO, C_GAO, C_DCW, C_DQG, C_DKG, C_SINK, C_SQ = 0, 1024, 2048, 2560, 3072, 4608, 4736, 4864, 5632
P_W = C_SQ + 128


def _pack_small(me, dfwg, dfwv, dfbg, dfbv, dg2, dgco, dgao, dcw8, dqg, dkg, dsink, sq):
    def body(me_ref, dfwg_r, dfwv_r, dfbg_r, dfbv_r, dg2_r, dgco_r, dgao_r, dcw_r, dqg_r, dkg_r, dsink_r, sq_r, o):
        o[...] = jnp.zeros_like(o)
        o[0, :, 0:DFF] = dfwg_r[...]
        o[0, :, DFF:2 * DFF] = dfwv_r[...]
        o[0, 3:4, 0:DFF] = dfbg_r[...]
        o[0, 3:4, DFF:2 * DFF] = dfbv_r[...]
        o[0, 4:5, C_G2:C_G2 + D] = dg2_r[...]
        o[0, 4:5, C_GCO:C_GCO + CW] = dgco_r[...]
        o[0, 4:5, C_GAO:C_GAO + AW] = dgao_r[...]
        for r in range(3):
            o[0, 4:5, C_DCW + r * CW:C_DCW + (r + 1) * CW] = dcw_r[r:r + 1, :]
        o[0, 4:5, C_DQG:C_DQG + HD] = dqg_r[...]
        o[0, 4:5, C_DKG:C_DKG + HD] = dkg_r[...]
        o[0, 4:5, C_SINK:C_SINK + 128] = dsink_r[...]
        o[0, :, C_SQ:C_SQ + 128] = sq_r[...]

    ins = (dfwg, dfwv, dfbg, dfbv, dg2, dgco, dgao, dcw8, dqg, dkg, dsink, sq)
    return _call(body, ins, name="pack_small", grid=(1,), prefetch=(me,),
                 in_specs=[pl.BlockSpec(a.shape, lambda i, me_ref: (0, 0)) for a in ins],
                 out_specs=[pl.BlockSpec((1, 8, P_W), lambda i, me_ref: (me_ref[0], 0, 0))],
                 out_shape=[SDS((N_DEV, 8, P_W), F32)], sem=("arbitrary",))[0]


N_SMALL = 11


def _small_adam(chip, p_all, g1_all, tbl_all, ws, ms, vs):
    fw_cols = 2 * DFF // N_CHIPS
    cw_cols = CW // N_CHIPS

    def body(chip_ref, p_ref, fw_ref, cw0_ref, cw1_ref, cw2_ref, g1_ref, tbl_ref, *refs):
        w_r, m_r, v_r = refs[0:N_SMALL], refs[N_SMALL:2 * N_SMALL], refs[2 * N_SMALL:3 * N_SMALL]
        outs = refs[3 * N_SMALL:]
        g_o, d_o, nm_o, nv_o = (outs[k * N_SMALL:(k + 1) * N_SMALL] for k in range(4))
        loss_o = outs[4 * N_SMALL]

        def total(ref):
            s = ref[0]
            for k in range(1, N_DEV):
                s = s + ref[k]
            return s

        S = total(p_ref)
        fw = total(fw_ref)
        cws = [total(r) for r in (cw0_ref, cw1_ref, cw2_ref)]

        def step(i, g, at):
            d, nm, nv = _adam_math(w_r[i][at], g, m_r[i][at], v_r[i][at])
            g_o[i][at], d_o[i][at], nm_o[i][at], nv_o[i][at] = g, d, nm, nv

        everything = (slice(None), slice(None))
        step(0, total(g1_ref), everything)
        for r in range(3):
            step(1, cws[r][4:5, :], (0, slice(r, r + 1), slice(None)))
        step(2, S[4:5, C_DQG:C_DQG + HD], everything)
        step(3, S[4:5, C_DKG:C_DKG + HD], everything)
        step(4, total(tbl_ref), everything)
        step(5, S[4:5, C_SINK:C_SINK + NH], everything)
        step(6, S[4:5, C_GCO:C_GCO + CW], everything)
        step(7, S[4:5, C_GAO:C_GAO + AW], everything)
        step(8, S[4:5, C_G2:C_G2 + D], everything)
        step(9, fw[0:3, :], (0, slice(None), slice(None)))
        step(10, S[3:4, 0:2 * DFF], everything)
        sq = S[:, C_SQ:C_SQ + 128]
        loss_o[...] = jnp.sum(jnp.sum(sq, axis=1, keepdims=True), axis=0, keepdims=True) * (0.5 / D)

    def full(a):
        n = len(a.shape)
        return pl.BlockSpec(a.shape, lambda i, chip_ref: (0,) * n)

    params = [*ws, *ms, *vs]
    out = _call(
        body, (p_all, p_all, p_all, p_all, p_all, g1_all, tbl_all, *params), name="small_adam", grid=(1,), prefetch=(chip,),
        in_specs=[full(p_all),
                  pl.BlockSpec((N_DEV, 8, fw_cols), lambda i, chip_ref: (0, 0, chip_ref[0])),
                  *[pl.BlockSpec((N_DEV, 8, cw_cols), lambda i, chip_ref, r=r: (0, 0, (C_DCW + r * CW) // cw_cols + chip_ref[0]))
                    for r in range(3)],
                  full(g1_all), full(tbl_all), *[full(a) for a in params]],
        out_specs=[full(a) for a in ws] * 4 + [pl.BlockSpec((1, 1), lambda i, chip_ref: (0, 0))],
        out_shape=[SDS(a.shape, F32) for a in ws] * 4 + [SDS((1, 1), F32)], sem=("arbitrary",), vmem_mib=32)
    return out[0:N_SMALL], out[N_SMALL:2 * N_SMALL], out[2 * N_SMALL:3 * N_SMALL], out[3 * N_SMALL:4 * N_SMALL], out[4 * N_SMALL]


PLACE_STEPS = 4


def _place_specs(shards):
    rows = [s.shape[0] // PLACE_STEPS for s in shards]
    return ([pl.BlockSpec((r, D), lambda i, chip_ref: (i, 0)) for r in rows],
            [pl.BlockSpec((r, D), lambda i, chip_ref: (chip_ref[0] * PLACE_STEPS + i, 0)) for r in rows],
            [SDS((N_CHIPS * s.shape[0], D), BF) for s in shards])


def _place_first(chip, shard, conv_w, ffn_conv_w):
    def body(chip_ref, a, s0, s1, o, t0, t1):
        o[...] = a[...].astype(BF)

        @pl.when(pl.program_id(0) == 0)
        def _():
            for s, t in ((s0, t0), (s1, t1)):
                t[...] = jnp.zeros_like(t)
                t[0, 0:3, :] = s[...]

    ins, outs, shapes = _place_specs([shard])
    taps = (conv_w, ffn_conv_w)
    return _call(
        body, (shard, conv_w, ffn_conv_w), name="place_first", grid=(PLACE_STEPS,), prefetch=(chip,),
        in_specs=ins + [pl.BlockSpec(s.shape, lambda i, chip_ref: (0, 0)) for s in taps],
        out_specs=outs + [pl.BlockSpec((1, 8, s.shape[1]), lambda i, chip_ref: (chip_ref[0], 0, 0)) for s in taps],
        out_shape=shapes + [SDS((N_CHIPS, 8, s.shape[1]), F32) for s in taps],
        sem=("arbitrary",), vmem_mib=32, free=(0, 1, 2))


def _place_rest(chip, shards, table, bucket, comm):
    n = len(shards)

    def body(chip_ref, *refs):
        a, (tab_ref, bk_ref), o, bias_ref = refs[:n], refs[n:n + 2], refs[n + 2:2 * n + 2], refs[2 * n + 2]
        for src, dst in zip(a, o):
            dst[...] = src[...].astype(BF)

        @pl.when(pl.program_id(0) == 0)
        def _():
            bk = bk_ref[...]
            eq = [bk == b for b in range(NBUCKET)]
            for h in range(NH):
                acc = jnp.zeros((BLK, 2 * BLK), F32)
                for b in range(NBUCKET):
                    acc = jnp.where(eq[b], tab_ref[h, b], acc)
                bias_ref[h * BLK:(h + 1) * BLK, :] = acc

    ins, outs, shapes = _place_specs(shards)
    return _call(
        body, (*shards, table, bucket), name="place_rest", grid=(PLACE_STEPS,), prefetch=(chip,),
        in_specs=ins + [pl.BlockSpec(memory_space=pltpu.SMEM), pl.BlockSpec(bucket.shape, lambda i, chip_ref: (0, 0))],
        out_specs=outs + [pl.BlockSpec((NH * BLK, 2 * BLK), lambda i, chip_ref: (0, 0))],
        out_shape=shapes + [SDS((NH * BLK, 2 * BLK), F32)],
        sem=("arbitrary",), vmem_mib=32, comm=comm, free=tuple(range(n + 2)))


def kernel(x, norm_mix_g, w_in, conv_w, q_norm_g, k_norm_g, rel_bias_table, sinks, out_norm_conv_g, out_norm_attn_g, w_out, norm_ffn_g, w_up, ffn_conv_w, ffn_conv_b, w_down, loss_target, m_norm_mix_g, m_w_in, m_conv_w, m_q_norm_g, m_k_norm_g, m_rel_bias_table, m_sinks, m_out_norm_conv_g, m_out_norm_attn_g, m_w_out, m_norm_ffn_g, m_w_up, m_ffn_conv_w, m_ffn_conv_b, m_w_down, v_norm_mix_g, v_w_in, v_conv_w, v_q_norm_g, v_k_norm_g, v_rel_bias_table, v_sinks, v_out_norm_conv_g, v_out_norm_attn_g, v_w_out, v_norm_ffn_g, v_w_up, v_ffn_conv_w, v_ffn_conv_b, v_w_down):
    as_arg = lambda i: jnp.reshape(i, (1,)).astype(jnp.int32)
    chip = as_arg(2 * lax.axis_index("x") + lax.axis_index("y"))
    core = as_arg(lax.axis_index("c"))
    me = 2 * chip + core
    xs, tgt = x[0], loss_target[0]
    qg, kg, gco, gao, g1, g2, fb = q_norm_g, k_norm_g, out_norm_conv_g, out_norm_attn_g, norm_mix_g, norm_ffn_g, ffn_conv_b
    pieces = lambda g: g.reshape(N_DEV, g.shape[0] // N_DEV, D)
    whole = lambda f: f.reshape(2 * f.shape[1], D)

    bucket = jnp.asarray(_bucket_table())
    p_in, p_cw, p_fw = _place_first(chip, w_in[0].T, conv_w[0], ffn_conv_w[0])
    p_out, p_up, p_down, bias, w_int, cw_all, fw_all = _place_rest(
        chip, [w_out[0], w_up[0].T, w_down[0]], rel_bias_table.T, bucket,
        comm=[_t_gather(p_in), _t_small_weights(p_cw), _t_small_weights(p_fw)])
    cw8 = jnp.transpose(cw_all, (1, 0, 2)).reshape(8, CW)
    fw8 = jnp.transpose(fw_all, (1, 0, 2)).reshape(8, 2 * DFF)

    proj, u1, w_out_f = _inproj(xs, g1, w_int, comm=[_t_gather(p_out)])
    y, w_upt = _mix_fwd(proj, sinks, cw8, qg, kg, gco, gao, bias, comm=[_t_gather(p_up)])
    h1, u2 = _outproj(y, w_out_f, xs, g2)
    up, w_down_f = _ffn_up(u2, w_upt, comm=[_t_gather(p_down)])
    a, = _ffn_act(up, fw8, fb)
    dh2, dh2b, sq = _ffn_down(a, w_down_f, h1, tgt)

    gdbf, = _wgrad("wgrad_down", [a], dh2b)
    da, sib_down = _ffn_down_bwd(dh2b, w_down_f, comm=[_t_sibling(pieces(gdbf))])
    pbf_down, own_down = _chip_sum("chip_sum_w_down", pieces(gdbf), sib_down, core, chip)
    dug, duv, dfwg, dfwv, dfbg, dfbv, chips_down = _ffn_act_bwd(up, da, fw8, fb, comm=[_t_chips(pbf_down)])
    fin_down = _final_sum("final_sum_w_down", own_down, chips_down, core)
    gubf, = _wgrad("wgrad_up", [dug, duv], u2)
    dh1, dh1b, dg2, sib_up, fin_down = _norm_matmul_bwd(
        "ffn_up_bwd", [dug, duv], w_upt, [0, DFF], h1, g2, dh2, True, comm=[_t_sibling(pieces(gubf)), _t_swap(fin_down)])
    pbf_up, own_up = _chip_sum("chip_sum_w_up", pieces(gubf), sib_up, core, chip)
    gobf, = _wgrad("wgrad_out", [y], dh1b)
    dy, sib_out = _out_bwd(dh1b, w_out_f, comm=[_t_sibling(pieces(gobf))])
    pbf_out, own_out = _chip_sum("chip_sum_w_out", pieces(gobf), sib_out, core, chip)
    dproj, dcw8, dqg, dkg, dgco, dgao, dsink, dbias, chips_up, chips_out = _mix_bwd(
        proj, dy, sinks, cw8, qg, kg, gco, gao, bias, comm=[_t_chips(pbf_up), _t_chips(pbf_out)])
    fin_up = _final_sum("final_sum_w_up", own_up, chips_up, core)
    fin_out = _final_sum("final_sum_w_out", own_out, chips_out, core)
    tbl_all = _band_bias_bwd(dbias, bucket, me)
    p_all = _pack_small(me, dfwg, dfwv, dfbg, dfbv, dg2, dgco, dgao, dcw8, dqg, dkg, dsink, sq)
    gibf, fin_up, p_all, tbl_all = _wgrad(
        "wgrad_in", [dproj], u1, comm=[_t_swap(fin_up), _t_allgather(p_all), _t_allgather(tbl_all)])
    sib_in, fin_out = _comm_call("to_sibling_last", [_t_sibling(pieces(gibf)), _t_swap(fin_out)])
    pbf_in, own_in = _chip_sum("chip_sum_w_in", pieces(gibf), sib_in, core, chip)
    dx, g1_all, chips_in = _norm_matmul_bwd(
        "in_bwd", [dproj], w_int, [0], xs, g1, dh1, False, comm=[_t_chips(pbf_in)], slot=me)
    fin_in = _final_sum("final_sum_w_in", own_in, chips_in, core)
    g1_all, fin_in = _comm_call("gather_last", [_t_allgather(g1_all), _t_swap(fin_in)])

    g_w_in, g_w_out, g_w_up, g_w_down = whole(fin_in).T, whole(fin_out), whole(fin_up).T, whole(fin_down)
    g_w_down, d_down, nm_down, nv_down = _adamw("adamw_w_down", w_down[0], g_w_down, m_w_down[0], v_w_down[0], 352, True)
    d_up, nm_up, nv_up = _adamw("adamw_w_up", w_up[0], g_w_up, m_w_up[0], v_w_up[0], 256)
    g_w_out, d_out, nm_out, nv_out = _adamw("adamw_w_out", w_out[0], g_w_out, m_w_out[0], v_w_out[0], 256, True)
    d_in, nm_in, nv_in = _adamw("adamw_w_in", w_in[0], g_w_in, m_w_in[0], v_w_in[0], 256)
    sw = [norm_mix_g, conv_w, q_norm_g, k_norm_g, rel_bias_table.T, sinks, out_norm_conv_g, out_norm_attn_g,
          norm_ffn_g, ffn_conv_w, ffn_conv_b]
    smm = [m_norm_mix_g, m_conv_w, m_q_norm_g, m_k_norm_g, m_rel_bias_table.T, m_sinks, m_out_norm_conv_g,
           m_out_norm_attn_g, m_norm_ffn_g, m_ffn_conv_w, m_ffn_conv_b]
    smv = [v_norm_mix_g, v_conv_w, v_q_norm_g, v_k_norm_g, v_rel_bias_table.T, v_sinks, v_out_norm_conv_g,
           v_out_norm_attn_g, v_norm_ffn_g, v_ffn_conv_w, v_ffn_conv_b]
    *small_out, loss = _small_adam(chip, p_all, g1_all, tbl_all, sw, smm, smv)
    sg, sd, snm, snv = [list(r) for r in small_out]
    for r in (sg, sd, snm, snv):
        r[4] = r[4].T

    def order(s, b_in, b_out, b_up, b_down):
        return (s[0], b_in[None], s[1], s[2], s[3], s[4], s[5], s[6], s[7], b_out[None], s[8], b_up[None],
                s[9], s[10], b_down[None])

    return (loss.reshape(()), dx[None],
            *order(sg, g_w_in, g_w_out, g_w_up, g_w_down),
            *order(sd, d_in, d_out, d_up, d_down),
            *order(snm, nm_in, nm_out, nm_up, nm_down),
            *order(snv, nv_in, nv_out, nv_up, nv_down))
```

```python
import functools
import math

import numpy as np

import jax
import jax.numpy as jnp
from jax import lax
from jax.experimental import pallas as pl
from jax.experimental.pallas import tpu as pltpu

F32 = jnp.float32
BF = jnp.bfloat16
SDS = jax.ShapeDtypeStruct

T = 2048
D = 1024
CW = 512
AW = 512
HD = 64
NH = 8
NKV = 2
GQ = 4
INW = 2304
DFF = 2816
BLK = 128
NB = T // BLK
NBUCKET = 32
EPS = 1e-6
NEG_INF = -1e30
N_CHIPS = 4
N_DEV = 8

ADAM_LR = 0.001
ADAM_B1 = 0.9
ADAM_B2 = 0.999
ADAM_EPS = 1e-08
ADAM_WD = 0.01
ADAM_STEP = 10

TM = 512
MIB = 1024 * 1024
MESH = pl.DeviceIdType.MESH
ANY = pl.BlockSpec(memory_space=pl.ANY)

_pcall = pl.pallas_call


def _params(sem=None, vmem_mib=None):
    kw = {}
    if sem is not None:
        kw["dimension_semantics"] = sem
    if vmem_mib is not None:
        kw["vmem_limit_bytes"] = vmem_mib * MIB
    return pltpu.CompilerParams(**kw)


def _resident(shape):
    return pl.BlockSpec(shape, lambda *_: (0,) * len(shape), pipeline_mode=pl.Buffered(1))


def _dot(a, b, ca, cb):
    return lax.dot_general(a, b, (((ca,), (cb,)), ((), ())), preferred_element_type=F32)


def _rms_bwd(dy, x, r, g):
    dg = jnp.sum(dy * (x * r), axis=0, keepdims=True)
    dgx = dy * g
    dx = r * dgx - x * (r * r * r) * jnp.mean(x * dgx, axis=-1, keepdims=True)
    return dx, dg


def _where():
    x, y, c = lax.axis_index("x"), lax.axis_index("y"), lax.axis_index("c")
    return x, y, c, [(1 - x, y), (x, 1 - y), (1 - x, 1 - y)]


def _rcopy(src, dst, ssem, rsem, dev):
    return pltpu.make_async_remote_copy(src_ref=src, dst_ref=dst, send_sem=ssem, recv_sem=rsem, device_id=dev,
                                        device_id_type=MESH)


class _Task:
    def __init__(self, ins, outs, alias, n_sem, start, finish):
        self.ins, self.outs, self.alias, self.n_sem, self.start, self.finish = ins, outs, alias, n_sem, start, finish


def _t_gather(placed):
    R = placed.shape[0] // N_CHIPS

    def rows(chip_index, core):
        return pl.ds(pl.multiple_of(chip_index * R + core * (R // 2), 16), R // 2)

    def start(cin, cout, ss, rs, b):
        x, y, c, chips = _where()
        mine = cout[0].at[rows(2 * x + y, c)]
        for r, (px, py) in enumerate(chips):
            _rcopy(mine, mine, ss.at[b + r], rs.at[b + r], (px, py, c)).start()

    def finish(cin, cout, ss, rs, b):
        x, y, c, chips = _where()
        buf = cout[0]
        sib = (x, y, 1 - c)
        for r, (px, py) in enumerate(chips):
            got = buf.at[rows(2 * px + py, c)]
            _rcopy(got, got, ss.at[b + r], rs.at[b + r], (px, py, c)).wait_recv()
            _rcopy(got, got, ss.at[b + 3 + r], rs.at[b + 3 + r], sib).start()
        for r, (px, py) in enumerate(chips):
            got = buf.at[rows(2 * px + py, 1 - c)]
            _rcopy(got, got, ss.at[b + 3 + r], rs.at[b + 3 + r], sib).wait_recv()
        mine = buf.at[rows(2 * x + y, c)]
        for r in range(6):
            _rcopy(mine, mine, ss.at[b + r], rs.at[b + r], sib).wait_send()

    return _Task([placed], [SDS(placed.shape, placed.dtype)], [(0, 0)], 6, start, finish)


def _t_small_weights(buf):
    def start(cin, cout, ss, rs, b):
        x, y, c, chips = _where()
        mine = cout[0].at[2 * x + y]
        for r, (px, py) in enumerate(chips):
            _rcopy(mine, mine, ss.at[b + r], rs.at[b + r], (px, py, c)).start()

    def finish(cin, cout, ss, rs, b):
        x, y, c, chips = _where()
        for r, (px, py) in enumerate(chips):
            got = cout[0].at[2 * px + py]
            _rcopy(got, got, ss.at[b + r], rs.at[b + r], (px, py, c)).wait_recv()
        for r, (px, py) in enumerate(chips):
            mine = cout[0].at[2 * x + y]
            _rcopy(mine, mine, ss.at[b + r], rs.at[b + r], (px, py, c)).wait_send()

    return _Task([buf], [SDS(buf.shape, buf.dtype)], [(0, 0)], 3, start, finish)


def _t_sibling(gbf):
    def start(cin, cout, ss, rs, b):
        x, y, c, _ = _where()
        for jj in range(N_CHIPS):
            _rcopy(cin[0].at[2 * jj + (1 - c)], cout[0].at[jj], ss.at[b + jj], rs.at[b + jj], (x, y, 1 - c)).start()

    def finish(cin, cout, ss, rs, b):
        x, y, c, _ = _where()
        for jj in range(N_CHIPS):
            got = cout[0].at[jj]
            _rcopy(got, got, ss.at[b + jj], rs.at[b + jj], (x, y, 1 - c)).wait_recv()
        for jj in range(N_CHIPS):
            got = cout[0].at[jj]
            _rcopy(got, got, ss.at[b + jj], rs.at[b + jj], (x, y, 1 - c)).wait_send()

    return _Task([gbf], [SDS((N_CHIPS,) + gbf.shape[1:], BF)], [], N_CHIPS, start, finish)


def _t_chips(pbf):
    def start(cin, cout, ss, rs, b):
        x, y, c, chips = _where()
        for r, (px, py) in enumerate(chips):
            _rcopy(cin[0].at[2 * px + py], cout[0].at[r], ss.at[b + r], rs.at[b + r], (px, py, c)).start()

    def finish(cin, cout, ss, rs, b):
        x, y, c, chips = _where()
        for r, (px, py) in enumerate(chips):
            got = cout[0].at[r]
            _rcopy(got, got, ss.at[b + r], rs.at[b + r], (px, py, c)).wait_recv()
        for r, (px, py) in enumerate(chips):
            got = cout[0].at[r]
            _rcopy(got, got, ss.at[b + r], rs.at[b + r], (px, py, c)).wait_send()

    return _Task([pbf], [SDS((3,) + pbf.shape[1:], BF)], [], 3, start, finish)


def _t_swap(fin):
    def start(cin, cout, ss, rs, b):
        x, y, c, _ = _where()
        mine = cout[0].at[c]
        _rcopy(mine, mine, ss.at[b], rs.at[b], (x, y, 1 - c)).start()

    def finish(cin, cout, ss, rs, b):
        x, y, c, _ = _where()
        got = cout[0].at[1 - c]
        _rcopy(got, got, ss.at[b], rs.at[b], (x, y, 1 - c)).wait_recv()
        _rcopy(got, got, ss.at[b], rs.at[b], (x, y, 1 - c)).wait_send()

    return _Task([fin], [SDS(fin.shape, fin.dtype)], [(0, 0)], 1, start, finish)


def _t_allgather(buf):
    def peers():
        x, y, c, _ = _where()
        out = []
        for rel in range(1, N_DEV):
            px, py, pc = x ^ ((rel >> 2) & 1), y ^ ((rel >> 1) & 1), c ^ (rel & 1)
            out.append((rel - 1, 4 * px + 2 * py + pc, (px, py, pc)))
        return 4 * x + 2 * y + c, out

    def start(cin, cout, ss, rs, b):
        me, ps = peers()
        mine = cout[0].at[me]
        for k, _, dev in ps:
            _rcopy(mine, mine, ss.at[b + k], rs.at[b + k], dev).start()

    def finish(cin, cout, ss, rs, b):
        me, ps = peers()
        for k, pidx, dev in ps:
            got = cout[0].at[pidx]
            _rcopy(got, got, ss.at[b + k], rs.at[b + k], dev).wait_recv()
        for k, _, dev in ps:
            mine = cout[0].at[me]
            _rcopy(mine, mine, ss.at[b + k], rs.at[b + k], dev).wait_send()

    return _Task([buf], [SDS(buf.shape, buf.dtype)], [(0, 0)], N_DEV - 1, start, finish)


def _run_tasks(comm, which, cin, cout, ss, rs):
    i0 = o0 = s0 = 0
    for t in comm:
        getattr(t, which)(cin[i0:i0 + len(t.ins)], cout[o0:o0 + len(t.outs)], ss, rs, s0)
        i0, o0, s0 = i0 + len(t.ins), o0 + len(t.outs), s0 + t.n_sem


def _from_hbm(*arrays):
    return [pltpu.with_memory_space_constraint(a, pltpu.HBM) for a in arrays]


def _in_hbm(shapes):
    return [pltpu.HBM(s.shape, s.dtype) for s in shapes]


def _comm_layout(comm, n_in, n_out):
    c_in = [a for t in comm for a in t.ins]
    c_out = [s for t in comm for s in t.outs]
    aliases, i0, o0 = {}, 0, 0
    for t in comm:
        for i, o in t.alias:
            aliases[n_in + i0 + i] = n_out + o0 + o
        i0, o0 = i0 + len(t.ins), o0 + len(t.outs)
    return c_in, c_out, aliases, sum(t.n_sem for t in comm)


def _call(body, operands, *, name, grid, in_specs, out_specs, out_shape, scratch_shapes=(), sem=None, vmem_mib=None, comm=(),
          free=(), prefetch=()):
    operands = [o if s.memory_space == pltpu.SMEM or k in free else pltpu.with_memory_space_constraint(o, pltpu.HBM)
                for k, (o, s) in enumerate(zip(operands, in_specs))]
    n_pre, n_in, n_out, n_scr = len(prefetch), len(in_specs), len(out_specs), len(scratch_shapes)
    c_in, c_out, aliases, n_sem = _comm_layout(comm, n_pre + n_in, n_out)
    sems = [pltpu.SemaphoreType.DMA((n_sem,)), pltpu.SemaphoreType.DMA((n_sem,))] if comm else []

    def wrapped(*refs):
        pre, refs = refs[:n_pre], refs[n_pre:]
        ins, cin = refs[:n_in], refs[n_in:n_in + len(c_in)]
        rest = refs[n_in + len(c_in):]
        outs, cout = rest[:n_out], rest[n_out:n_out + len(c_out)]
        rest = rest[n_out + len(c_out):]
        scr, csem = rest[:n_scr], rest[n_scr:]
        if not comm:
            return body(*pre, *ins, *outs, *scr)
        ids = [pl.program_id(k) for k in range(len(grid))]
        first = functools.reduce(jnp.logical_and, [i == 0 for i in ids])
        last = functools.reduce(jnp.logical_and, [i == n - 1 for i, n in zip(ids, grid)])
        pl.when(first)(lambda: _run_tasks(comm, "start", cin, cout, *csem))
        body(*pre, *ins, *outs, *scr)
        pl.when(last)(lambda: _run_tasks(comm, "finish", cin, cout, *csem))

    grid_spec = pltpu.PrefetchScalarGridSpec(
        num_scalar_prefetch=n_pre, grid=grid, in_specs=list(in_specs) + [ANY] * len(c_in),
        out_specs=list(out_specs) + [ANY] * len(c_out), scratch_shapes=list(scratch_shapes) + sems)
    return _pcall(
        wrapped, name=name, grid_spec=grid_spec, out_shape=_in_hbm(list(out_shape) + c_out), input_output_aliases=aliases,
        compiler_params=_params(("arbitrary",) * len(grid) if comm else sem, vmem_mib),
    )(*prefetch, *operands, *_from_hbm(*c_in))


def _comm_call(name, comm):
    c_in, c_out, aliases, n_sem = _comm_layout(comm, 0, 0)

    def body(*refs):
        cin, cout, (ss, rs) = refs[:len(c_in)], refs[len(c_in):len(c_in) + len(c_out)], refs[len(c_in) + len(c_out):]
        _run_tasks(comm, "start", cin, cout, ss, rs)
        _run_tasks(comm, "finish", cin, cout, ss, rs)

    return _pcall(
        body, name=name, in_specs=[ANY] * len(c_in), out_specs=[ANY] * len(c_out), out_shape=_in_hbm(c_out),
        scratch_shapes=[pltpu.SemaphoreType.DMA((n_sem,)), pltpu.SemaphoreType.DMA((n_sem,))],
        input_output_aliases=aliases,
    )(*_from_hbm(*c_in))


def _inproj(x, g1, w_int, comm=()):
    tm = TM

    def body(x_ref, g_ref, w_ref, proj_ref, u_ref):
        xf = x_ref[...]
        r = lax.rsqrt(jnp.mean(xf * xf, axis=-1, keepdims=True) + EPS)
        u = (xf * r * g_ref[...]).astype(BF)
        u_ref[...] = u
        proj_ref[...] = _dot(u, w_ref[...], 1, 1)

    return _call(
        body, (x, g1, w_int), name="inproj", grid=(T // tm,),
        in_specs=[pl.BlockSpec((tm, D), lambda i: (i, 0)), pl.BlockSpec((1, D), lambda i: (0, 0)),
                  _resident((INW, D))],
        out_specs=[pl.BlockSpec((tm, INW), lambda i: (i, 0)), pl.BlockSpec((tm, D), lambda i: (i, 0))],
        out_shape=[SDS((T, INW), F32), SDS((T, D), BF)], sem=("parallel",), vmem_mib=40, comm=comm)


def _outproj(y, w_out, x, g2):
    tm = TM

    def body(y_ref, w_ref, x_ref, g_ref, h1_ref, u2_ref):
        h1 = x_ref[...] + _dot(y_ref[...], w_ref[...], 1, 0)
        h1_ref[...] = h1
        r = lax.rsqrt(jnp.mean(h1 * h1, axis=-1, keepdims=True) + EPS)
        u2_ref[...] = (h1 * r * g_ref[...]).astype(BF)

    return _call(
        body, (y, w_out, x, g2), name="outproj", grid=(T // tm,),
        in_specs=[pl.BlockSpec((tm, D), lambda i: (i, 0)), _resident((D, D)),
                  pl.BlockSpec((tm, D), lambda i: (i, 0)), pl.BlockSpec((1, D), lambda i: (0, 0))],
        out_specs=[pl.BlockSpec((tm, D), lambda i: (i, 0)), pl.BlockSpec((tm, D), lambda i: (i, 0))],
        out_shape=[SDS((T, D), F32), SDS((T, D), BF)], sem=("parallel",), vmem_mib=32)


def _ffn_up(u2, w_upt, comm=()):
    tm, tn = 1024, 512

    def body(u_ref, w_ref, o_ref):
        o_ref[...] = _dot(u_ref[...], w_ref[...], 1, 1).astype(BF)

    return _call(
        body, (u2, w_upt), name="ffn_up", grid=(T // tm, 2 * DFF // tn),
        in_specs=[pl.BlockSpec((tm, D), lambda i, j: (i, 0)), pl.BlockSpec((tn, D), lambda i, j: (j, 0))],
        out_specs=[pl.BlockSpec((tm, tn), lambda i, j: (i, j))], out_shape=[SDS((T, 2 * DFF), BF)],
        sem=("parallel", "parallel"), vmem_mib=32, comm=comm)


def _ffn_down(a, w_down, h1, tgt):
    tm = TM

    def body(a_ref, w_ref, h1_ref, t_ref, dh_ref, dhb_ref, l_ref):
        @pl.when(pl.program_id(0) == 0)
        def _():
            l_ref[...] = jnp.zeros_like(l_ref)

        h2 = h1_ref[...] + _dot(a_ref[...], w_ref[...], 1, 0)
        e = h2 - t_ref[...]
        dh = e * (1.0 / D)
        dh_ref[...] = dh
        dhb_ref[...] = dh.astype(BF)
        e2 = jnp.sum((e * e).reshape(tm // 8, 8, D), axis=0)
        acc = e2[:, 0:128]
        for k in range(1, D // 128):
            acc = acc + e2[:, k * 128:(k + 1) * 128]
        l_ref[...] += acc

    return _call(
        body, (a, w_down, h1, tgt), name="ffn_down", grid=(T // tm,),
        in_specs=[pl.BlockSpec((tm, DFF), lambda i: (i, 0)), _resident((DFF, D)),
                  pl.BlockSpec((tm, D), lambda i: (i, 0)), pl.BlockSpec((tm, D), lambda i: (i, 0))],
        out_specs=[pl.BlockSpec((tm, D), lambda i: (i, 0)), pl.BlockSpec((tm, D), lambda i: (i, 0)),
                   pl.BlockSpec((8, 128), lambda i: (0, 0))],
        out_shape=[SDS((T, D), F32), SDS((T, D), BF), SDS((8, 128), F32)], sem=("arbitrary",), vmem_mib=40)


def _bucket_table():
    q = np.arange(BLK, dtype=np.int32)[:, None]
    j = np.arange(2 * BLK, dtype=np.int32)[None, :]
    n = np.maximum(q + BLK - j, 0)
    nf = np.maximum(n, 1).astype(np.float32)
    max_exact = NBUCKET // 2
    large = max_exact + (np.log(nf / np.float32(max_exact)) / np.float32(math.log(BLK / max_exact))
                         * np.float32(NBUCKET - max_exact)).astype(np.int32)
    large = np.minimum(large, NBUCKET - 1)
    return np.where(n < max_exact, n, large).astype(np.int32)


def _band_bias_bwd(dbias, bucket, me):
    def body(me_ref, db_ref, bk_ref, o_ref):
        bk = bk_ref[...]
        for b in range(NBUCKET):
            m = bk == b
            for h in range(NH):
                v = jnp.where(m, db_ref[h * BLK:(h + 1) * BLK, :], 0.0)
                s = jnp.sum(jnp.sum(v, axis=1, keepdims=True), axis=0, keepdims=True)
                o_ref[0, h:h + 1, b:b + 1] = s

    grid_spec = pltpu.PrefetchScalarGridSpec(
        num_scalar_prefetch=1, grid=(1,),
        in_specs=[pl.BlockSpec((NH * BLK, 2 * BLK), lambda i, me_ref: (0, 0)),
                  pl.BlockSpec((BLK, 2 * BLK), lambda i, me_ref: (0, 0))],
        out_specs=pl.BlockSpec((1, NH, NBUCKET), lambda i, me_ref: (me_ref[0], 0, 0)),
    )
    return _pcall(body, name="band_bias_bwd", grid_spec=grid_spec, out_shape=SDS((N_DEV, NH, NBUCKET), F32),
                  compiler_params=_params(("arbitrary",)))(me, dbias, bucket)


def _two_bf16(x):
    hi = x.astype(BF)
    return hi, (x - hi.astype(F32)).astype(BF)


def _head_sums(x, seg):
    hi, lo = _two_bf16(x)
    s = seg[0:x.shape[1], :]
    return _dot(hi, s, 1, 0) + _dot(lo, s, 1, 0)


def _head_spread(v, seg, width):
    hi, lo = _two_bf16(v)
    s = seg[0:width, :]
    return _dot(hi, s, 1, 1) + _dot(lo, s, 1, 1)


def _head_norm(x, g_t, seg):
    r = lax.rsqrt(_head_sums(x * x, seg) * (1.0 / HD) + EPS)
    r = _head_spread(r, seg, x.shape[1])
    return x * r * g_t, r


def _head_norm_bwd(dy, x, r, g_t, seg):
    dg_t = jnp.sum(dy * (x * r), axis=0, keepdims=True)
    dgx = dy * g_t
    mean = _head_spread(_head_sums(x * dgx, seg) * (1.0 / HD), seg, x.shape[1])
    return r * dgx - x * (r * r * r) * mean, dg_t


def _fold_heads(v):
    out = v[:, 0:HD]
    for h in range(1, v.shape[1] // HD):
        out = out + v[:, h * HD:(h + 1) * HD]
    return out


def _mix_forward(P, zc8, zh8, pkv, first, cw, qg_t, kg_t, gco, gao, seg, sink_ref, bias_ref):
    gate_b = P[:, 0:CW]
    gate_c = P[:, CW:2 * CW]
    hc = P[:, 2 * CW:3 * CW]
    z = gate_c * hc
    keep = jnp.where(first, 0.0, 1.0)
    zp = zc8 * zh8 * keep
    p1 = zp[7:8, :]
    p2 = zp[6:7, :]
    row = lax.broadcasted_iota(jnp.int32, (BLK, 1), 0)
    z1 = jnp.where(row == 0, p1, pltpu.roll(z, 1, 0))
    z2 = jnp.where(row == 0, p2, jnp.where(row == 1, p1, pltpu.roll(z, 2, 0)))
    cz = cw[0:1, :] * z2 + cw[1:2, :] * z1 + cw[2:3, :] * z
    y_conv = gate_b * cz

    scale = HD ** -0.5
    qi = lax.broadcasted_iota(jnp.int32, (GQ * BLK, 2 * BLK), 0) & (BLK - 1)
    kj = lax.broadcasted_iota(jnp.int32, (GQ * BLK, 2 * BLK), 1)
    dd = qi + BLK - kj
    first_key = jnp.where(first, BLK, 0)
    valid = (dd >= 0) & (dd < BLK) & (kj >= first_key)

    q0 = 3 * CW
    k0 = q0 + AW
    v0 = k0 + NKV * HD
    q_raw = P[:, q0:k0]
    qn, rq = _head_norm(q_raw, qg_t, seg)
    qs = (qn * scale).astype(BF)
    k_raw = jnp.concatenate([pkv[:, 0:NKV * HD], P[:, k0:v0]], axis=0)
    kn, rk = _head_norm(k_raw, kg_t, seg)
    knb = kn.astype(BF)
    heads = []
    outs = []
    for kv in range(NKV):
        kb = knb[:, kv * HD:(kv + 1) * HD]
        vb = jnp.concatenate([pkv[:, NKV * HD + kv * HD:NKV * HD + (kv + 1) * HD],
                              P[:, v0 + kv * HD:v0 + (kv + 1) * HD]], axis=0).astype(BF)
        Q = jnp.concatenate([qs[:, (kv * GQ + g) * HD:(kv * GQ + g + 1) * HD] for g in range(GQ)], axis=0)
        S = _dot(Q, kb, 1, 1) + bias_ref[kv * GQ * BLK:(kv + 1) * GQ * BLK, :]
        S = jnp.where(valid, S, NEG_INF)
        sink = jnp.concatenate([jnp.full((BLK, 1), sink_ref[0, kv * GQ + g], F32) for g in range(GQ)], axis=0)
        m = jnp.maximum(jnp.max(S, axis=-1, keepdims=True), sink)
        p = jnp.exp(S - m)
        es = jnp.exp(sink - m)
        denom = jnp.sum(p, axis=-1, keepdims=True) + es
        probs = p / denom
        O = _dot(probs.astype(BF), vb, 1, 0)
        heads.append(dict(kb=kb, vb=vb, Q=Q, probs=probs, psink=es / denom, O=O))
        outs += [O[g * BLK:(g + 1) * BLK, :] for g in range(GQ)]
    y_attn = jnp.concatenate(outs, axis=1)

    rc = lax.rsqrt(jnp.mean(y_conv * y_conv, axis=-1, keepdims=True) + EPS)
    ra = lax.rsqrt(jnp.mean(y_attn * y_attn, axis=-1, keepdims=True) + EPS)
    y = jnp.concatenate([y_conv * rc * gco, y_attn * ra * gao], axis=1)
    return dict(gate_b=gate_b, gate_c=gate_c, hc=hc, z=z, z1=z1, z2=z2, cz=cz, y_conv=y_conv, y_attn=y_attn,
                rc=rc, ra=ra, heads=heads, y=y, row=row, scale=scale, q_raw=q_raw, rq=rq, k_raw=k_raw, rk=rk)


BPS = 2
TILE = BPS * BLK
KV0 = 3 * CW + AW


def _mix_in_specs(tile_of):
    return [
        pl.BlockSpec(memory_space=pltpu.SMEM),
        pl.BlockSpec((TILE, INW), lambda s: (tile_of(s), 0)),
        pl.BlockSpec((8, CW), lambda s: (jnp.maximum(tile_of(s) * (TILE // 8) - 1, 0), 1)),
        pl.BlockSpec((8, CW), lambda s: (jnp.maximum(tile_of(s) * (TILE // 8) - 1, 0), 2)),
        pl.BlockSpec((BLK, 2 * NKV * HD), lambda s: (jnp.maximum(tile_of(s) * BPS - 1, 0), KV0 // (2 * NKV * HD))),
    ]


def _block_inputs(tile, b, zc_ref, zh_ref, pkv_ref, first_tile):
    P = tile[b * BLK:(b + 1) * BLK, :]
    if b == 0:
        return P, zc_ref[...], zh_ref[...], pkv_ref[...], first_tile
    lo = b * BLK
    return P, tile[lo - 8:lo, CW:2 * CW], tile[lo - 8:lo, 2 * CW:3 * CW], tile[lo - BLK:lo, KV0:KV0 + 2 * NKV * HD], False


def _mix_param_specs():
    return [
        pl.BlockSpec((8, CW), lambda s: (0, 0)),
        pl.BlockSpec((1, AW), lambda s: (0, 0)),
        pl.BlockSpec((1, NKV * HD), lambda s: (0, 0)),
        pl.BlockSpec((1, CW), lambda s: (0, 0)),
        pl.BlockSpec((1, AW), lambda s: (0, 0)),
        pl.BlockSpec((AW, 128), lambda s: (0, 0)),
        pl.BlockSpec((NH * BLK, 2 * BLK), lambda s: (0, 0)),
    ]


def _mix_params(cw8, qg, kg, gco, gao, bias):
    seg = np.zeros((AW, 128), np.float32)
    seg[np.arange(AW), np.arange(AW) // HD] = 1.0
    return (cw8, jnp.tile(qg, (1, NH)), jnp.tile(kg, (1, NKV)), gco, gao, jnp.asarray(seg, BF), bias)


def _mix_fwd(proj, sinks, cw8, qg, kg, gco, gao, bias, comm=()):
    def body(sink_ref, p_ref, zc_ref, zh_ref, pkv_ref, cw_ref, qg_ref, kg_ref, gco_ref, gao_ref, seg_ref, bias_ref, y_ref):
        tile = p_ref[...]
        for b in range(BPS):
            f = _mix_forward(*_block_inputs(tile, b, zc_ref, zh_ref, pkv_ref, pl.program_id(0) == 0), cw_ref[...],
                             qg_ref[...], kg_ref[...], gco_ref[...], gao_ref[...], seg_ref[...], sink_ref, bias_ref)
            y_ref[b * BLK:(b + 1) * BLK, :] = f["y"].astype(BF)

    return _call(
        body, (sinks, proj, proj, proj, proj, *_mix_params(cw8, qg, kg, gco, gao, bias)), name="mix_fwd", grid=(T // TILE,),
        in_specs=_mix_in_specs(lambda s: s) + _mix_param_specs(),
        out_specs=[pl.BlockSpec((TILE, D), lambda s: (s, 0))], out_shape=[SDS((T, D), BF)],
        sem=("parallel",), vmem_mib=40, comm=comm)


def _mix_bwd(proj, dy, sinks, cw8, qg, kg, gco, gao, bias, comm=()):
    n_steps = T // TILE

    def tile_of(s):
        return n_steps - 1 - s

    def body(sink_ref, p_ref, zc_ref, zh_ref, pkv_ref, dy_ref, cw_ref, qg_ref, kg_ref, gco_ref, gao_ref, seg_ref, bias_ref,
             dproj_ref, dcw_ref, dqg_ref, dkg_ref, dgco_ref, dgao_ref, dsink_ref, dbias_ref,
             ndcz_ref, dkc_ref, dvc_ref):
        s = pl.program_id(0)

        @pl.when(s == 0)
        def _():
            for r in (dcw_ref, dqg_ref, dkg_ref, dgco_ref, dgao_ref, dsink_ref, dbias_ref, ndcz_ref, dkc_ref, dvc_ref):
                r[...] = jnp.zeros_like(r)

        tile = p_ref[...]
        for b in reversed(range(BPS)):
            one_block(b, _block_inputs(tile, b, zc_ref, zh_ref, pkv_ref, s == n_steps - 1),
                      dy_ref[b * BLK:(b + 1) * BLK, :], sink_ref, cw_ref, qg_ref, kg_ref, gco_ref, gao_ref, seg_ref, bias_ref,
                      dproj_ref.at[b * BLK:(b + 1) * BLK, :], dcw_ref, dqg_ref, dkg_ref, dgco_ref, dgao_ref, dsink_ref,
                      dbias_ref, ndcz_ref, dkc_ref, dvc_ref)

    def one_block(b, inputs, dy, sink_ref, cw_ref, qg_ref, kg_ref, gco_ref, gao_ref, seg_ref, bias_ref,
                  dproj_ref, dcw_ref, dqg_ref, dkg_ref, dgco_ref, dgao_ref, dsink_ref, dbias_ref,
                  ndcz_ref, dkc_ref, dvc_ref):
        cw = cw_ref[...]
        qg_v, kg_v, gco_v, gao_v, seg = qg_ref[...], kg_ref[...], gco_ref[...], gao_ref[...], seg_ref[...]
        f = _mix_forward(*inputs, cw, qg_v, kg_v, gco_v, gao_v, seg, sink_ref, bias_ref)
        dyc, dgco = _rms_bwd(dy[:, 0:CW], f["y_conv"], f["rc"], gco_v)
        dya, dgao = _rms_bwd(dy[:, CW:CW + AW], f["y_attn"], f["ra"], gao_v)
        dgco_ref[...] += dgco
        dgao_ref[...] += dgao

        row = f["row"]
        dgate_b = dyc * f["cz"]
        dcz = dyc * f["gate_b"]
        dcw_ref[0:1, :] += jnp.sum(dcz * f["z2"], axis=0, keepdims=True)
        dcw_ref[1:2, :] += jnp.sum(dcz * f["z1"], axis=0, keepdims=True)
        dcw_ref[2:3, :] += jnp.sum(dcz * f["z"], axis=0, keepdims=True)
        nxt = ndcz_ref[...]
        n0 = nxt[0:1, :]
        n1 = nxt[1:2, :]
        d1 = jnp.where(row == BLK - 1, n0, pltpu.roll(dcz, BLK - 1, 0))
        d2 = jnp.where(row == BLK - 1, n1, jnp.where(row == BLK - 2, n0, pltpu.roll(dcz, BLK - 2, 0)))
        dz = cw[2:3, :] * dcz + cw[1:2, :] * d1 + cw[0:1, :] * d2
        ndcz_ref[...] = dcz[0:8, :]
        dproj_ref[:, 0:CW] = dgate_b.astype(BF)
        dproj_ref[:, CW:2 * CW] = (dz * f["hc"]).astype(BF)
        dproj_ref[:, 2 * CW:3 * CW] = (dz * f["gate_c"]).astype(BF)

        scale = f["scale"]
        lane = lax.broadcasted_iota(jnp.int32, (1, 128), 1)
        dq_cols, dk_cols, dv_cols = [], [], []
        for kv in range(NKV):
            hd = f["heads"][kv]
            dO = jnp.concatenate([dya[:, (kv * GQ + g) * HD:(kv * GQ + g + 1) * HD] for g in range(GQ)], axis=0)
            delta = jnp.sum(dO * hd["O"], axis=-1, keepdims=True)
            dOb = dO.astype(BF)
            dP = _dot(dOb, hd["vb"], 1, 1)
            dS = hd["probs"] * (dP - delta)
            dsk = hd["psink"] * delta
            for g in range(GQ):
                h = kv * GQ + g
                tot = jnp.sum(dsk[g * BLK:(g + 1) * BLK, :], axis=0, keepdims=True)
                dsink_ref[...] -= jnp.where(lane == h, tot, 0.0)
            dbias_ref[kv * GQ * BLK:(kv + 1) * GQ * BLK, :] += dS
            dSb = dS.astype(BF)
            dQ = _dot(dSb, hd["kb"], 1, 0)
            dKb = _dot(dSb, hd["Q"], 0, 0)
            dVb = _dot(hd["probs"].astype(BF), dOb, 0, 0)
            dk_cols.append(dKb[BLK:, :] + dkc_ref[:, kv * HD:(kv + 1) * HD])
            dv_cols.append(dVb[BLK:, :] + dvc_ref[:, kv * HD:(kv + 1) * HD])
            dkc_ref[:, kv * HD:(kv + 1) * HD] = dKb[:BLK, :]
            dvc_ref[:, kv * HD:(kv + 1) * HD] = dVb[:BLK, :]
            dq_cols += [dQ[g * BLK:(g + 1) * BLK, :] for g in range(GQ)]
        dq_raw, dqg_t = _head_norm_bwd(jnp.concatenate(dq_cols, axis=1) * scale, f["q_raw"], f["rq"], qg_v, seg)
        dk_raw, dkg_t = _head_norm_bwd(jnp.concatenate(dk_cols, axis=1), f["k_raw"][BLK:, :], f["rk"][BLK:, :], kg_v, seg)
        dqg_ref[...] += _fold_heads(dqg_t)
        dkg_ref[...] += _fold_heads(dkg_t)
        dproj_ref[:, 3 * CW:INW] = jnp.concatenate([dq_raw, dk_raw] + dv_cols, axis=1).astype(BF)

    small = lambda r, c: pl.BlockSpec((r, c), lambda s: (0, 0))
    return _call(
        body, (sinks, proj, proj, proj, proj, dy, *_mix_params(cw8, qg, kg, gco, gao, bias)), name="mix_bwd", grid=(n_steps,),
        in_specs=_mix_in_specs(tile_of) + [pl.BlockSpec((TILE, D), lambda s: (tile_of(s), 0))] + _mix_param_specs(),
        out_specs=[pl.BlockSpec((TILE, INW), lambda s: (tile_of(s), 0)), small(8, CW), small(1, HD), small(1, HD),
                   small(1, CW), small(1, AW), small(1, 128), small(NH * BLK, 2 * BLK)],
        out_shape=[SDS((T, INW), BF), SDS((8, CW), F32), SDS((1, HD), F32), SDS((1, HD), F32), SDS((1, CW), F32),
                   SDS((1, AW), F32), SDS((1, 128), F32), SDS((NH * BLK, 2 * BLK), F32)],
        scratch_shapes=[pltpu.VMEM((8, CW), F32), pltpu.VMEM((BLK, NKV * HD), F32), pltpu.VMEM((BLK, NKV * HD), F32)],
        sem=("arbitrary",), vmem_mib=56, comm=comm)


FT = 256
NFT = DFF // FT
RC = 128
NCH = T // RC
LEAD = 16


def _rows8(x):
    return jnp.sum(x.reshape(x.shape[0] // 8, 8, x.shape[1]), axis=0)


def _ffn_act_specs():
    return [
        pl.BlockSpec((T, FT), lambda j: (0, j)), pl.BlockSpec((T, FT), lambda j: (0, NFT + j)),
        pl.BlockSpec((8, FT), lambda j: (0, j)), pl.BlockSpec((8, FT), lambda j: (0, NFT + j)),
        pl.BlockSpec((1, FT), lambda j: (0, j)), pl.BlockSpec((1, FT), lambda j: (0, NFT + j)),
    ]


def _conv_rows(win, w, b, n):
    win = win.astype(F32)
    u = win[LEAD:LEAD + n]
    u1 = pltpu.roll(win, 1, 0)[LEAD:LEAD + n]
    u2 = pltpu.roll(win, 2, 0)[LEAD:LEAD + n]
    return u2, u1, u, w[0:1, :] * u2 + w[1:2, :] * u1 + w[2:3, :] * u + b


def _ffn_act(up, fw8, fb):
    def body(ug_ref, uv_ref, wg_ref, wv_ref, bg_ref, bv_ref, a_ref):
        wg, wv, bg, bv = wg_ref[...], wv_ref[...], bg_ref[...], bv_ref[...]

        def chunk(win_g, win_v):
            gp = _conv_rows(win_g, wg, bg, RC)[3]
            vp = _conv_rows(win_v, wv, bv, RC)[3]
            return (gp * jax.nn.sigmoid(gp) * vp).astype(BF)

        zero = jnp.zeros((LEAD, FT), BF)
        a_ref[0:RC, :] = chunk(jnp.concatenate([zero, ug_ref[0:RC, :]], axis=0),
                               jnp.concatenate([zero, uv_ref[0:RC, :]], axis=0))

        def step(i, carry):
            r0 = pl.multiple_of(i * RC, RC)
            win = pl.ds(r0 - LEAD, RC + LEAD)
            a_ref[pl.ds(r0, RC), :] = chunk(ug_ref[win, :], uv_ref[win, :])
            return carry

        lax.fori_loop(1, NCH, step, 0)

    return _call(
        body, (up, up, fw8, fw8, fb, fb), name="ffn_act", grid=(NFT,), in_specs=_ffn_act_specs(),
        out_specs=[pl.BlockSpec((T, FT), lambda j: (0, j))], out_shape=[SDS((T, DFF), BF)],
        sem=("parallel",), vmem_mib=40)


def _ffn_act_bwd(up, da, fw8, fb, comm=()):
    ext = RC + LEAD

    def body(ug_ref, uv_ref, wg_ref, wv_ref, bg_ref, bv_ref, da_ref,
             dug_ref, duv_ref, dwg_ref, dwv_ref, dbg_ref, dbv_ref):
        wg, wv, bg, bv = wg_ref[...], wv_ref[...], bg_ref[...], bv_ref[...]

        def chunk(win_g, win_v, da_e):
            g2, g1, g0, gp = _conv_rows(win_g, wg, bg, ext)
            v2, v1, v0, vp = _conv_rows(win_v, wv, bv, ext)
            da_e = da_e.astype(F32)
            sig = jax.nn.sigmoid(gp)
            dvp = da_e * (gp * sig)
            dgp = da_e * vp * (sig * (1.0 + gp * (1.0 - sig)))

            def back(dp, w):
                return (w[2:3, :] * dp[0:RC] + w[1:2, :] * pltpu.roll(dp, ext - 1, 0)[0:RC]
                        + w[0:1, :] * pltpu.roll(dp, ext - 2, 0)[0:RC]).astype(BF)

            def sums(dp, u2, u1, u0):
                d = dp[0:RC]
                return [_rows8(d), _rows8(d * u2[0:RC]), _rows8(d * u1[0:RC]), _rows8(d * u0[0:RC])]

            return back(dgp, wg), back(dvp, wv), sums(dgp, g2, g1, g0) + sums(dvp, v2, v1, v0)

        zero = jnp.zeros((LEAD, FT), BF)
        dug, duv, acc = chunk(jnp.concatenate([zero, ug_ref[0:ext, :]], axis=0),
                              jnp.concatenate([zero, uv_ref[0:ext, :]], axis=0), da_ref[0:ext, :])
        dug_ref[0:RC, :] = dug
        duv_ref[0:RC, :] = duv

        def step(i, acc):
            r0 = pl.multiple_of(i * RC, RC)
            win = pl.ds(r0 - LEAD, ext + LEAD)
            dug, duv, part = chunk(ug_ref[win, :], uv_ref[win, :], da_ref[pl.ds(r0, ext), :])
            dug_ref[pl.ds(r0, RC), :] = dug
            duv_ref[pl.ds(r0, RC), :] = duv
            return [a + p for a, p in zip(acc, part)]

        acc = lax.fori_loop(1, NCH - 1, step, acc)
        r0 = T - RC
        tail = lambda ref, lo: jnp.concatenate([ref[lo:T, :], zero], axis=0)
        dug, duv, part = chunk(tail(ug_ref, r0 - LEAD), tail(uv_ref, r0 - LEAD), tail(da_ref, r0))
        dug_ref[r0:T, :] = dug
        duv_ref[r0:T, :] = duv
        tot = [jnp.sum(a + p, axis=0, keepdims=True) for a, p in zip(acc, part)]
        for k, (dw_ref, db_ref) in enumerate(((dwg_ref, dbg_ref), (dwv_ref, dbv_ref))):
            db_ref[...] = tot[4 * k]
            dw_ref[...] = jnp.zeros_like(dw_ref)
            for r in range(3):
                dw_ref[r:r + 1, :] = tot[4 * k + 1 + r]

    col = lambda r: pl.BlockSpec((r, FT), lambda j: (0, j))
    return _call(
        body, (up, up, fw8, fw8, fb, fb, da), name="ffn_act_bwd", grid=(NFT,),
        in_specs=_ffn_act_specs() + [pl.BlockSpec((T, FT), lambda j: (0, j))],
        out_specs=[col(T), col(T), col(8), col(8), col(1), col(1)],
        out_shape=[SDS((T, DFF), BF), SDS((T, DFF), BF), SDS((8, DFF), F32), SDS((8, DFF), F32),
                   SDS((1, DFF), F32), SDS((1, DFF), F32)],
        sem=("parallel",), vmem_mib=40, comm=comm)


def _ffn_down_bwd(dh2b, w_down, comm=()):
    tm = TM

    def body(d_ref, w_ref, o_ref):
        o_ref[...] = _dot(d_ref[...], w_ref[...], 1, 1).astype(BF)

    return _call(
        body, (dh2b, w_down), name="ffn_down_bwd", grid=(T // tm,),
        in_specs=[pl.BlockSpec((tm, D), lambda i: (i, 0)), _resident((DFF, D))],
        out_specs=[pl.BlockSpec((tm, DFF), lambda i: (i, 0))], out_shape=[SDS((T, DFF), BF)],
        sem=("parallel",), vmem_mib=40, comm=comm)


def _norm_matmul_bwd(name, a_list, w_t, k_offsets, xin, g, dres, want_bf16, comm=(), slot=None):
    tm = TM
    ks = [a.shape[1] for a in a_list]
    n_a = len(a_list)
    n_pre = 0 if slot is None else 1

    def body(*refs):
        refs = refs[n_pre:]
        a_refs = refs[:n_a]
        w_ref, x_ref, g_ref, r_ref = refs[n_a:n_a + 4]
        outs = refs[n_a + 4:]
        dx_ref, dg_ref = outs[0], (outs[-1] if slot is None else outs[-1].at[0])

        @pl.when(pl.program_id(0) == 0)
        def _():
            dg_ref[...] = jnp.zeros_like(dg_ref)

        du = _dot(a_refs[0][...], w_ref[k_offsets[0]:k_offsets[0] + ks[0], :], 1, 0)
        for k in range(1, n_a):
            du = du + _dot(a_refs[k][...], w_ref[k_offsets[k]:k_offsets[k] + ks[k], :], 1, 0)
        x = x_ref[...]
        r = lax.rsqrt(jnp.mean(x * x, axis=-1, keepdims=True) + EPS)
        dx, dg = _rms_bwd(du, x, r, g_ref[...])
        dx = r_ref[...] + dx
        dx_ref[...] = dx
        if want_bf16:
            outs[1][...] = dx.astype(BF)
        dg_ref[...] += dg

    tile = lambda c: pl.BlockSpec((tm, c), lambda i, *_: (i, 0))
    if slot is None:
        dg_spec, dg_shape = pl.BlockSpec((1, D), lambda i: (0, 0)), SDS((1, D), F32)
    else:
        dg_spec, dg_shape = pl.BlockSpec((1, 1, D), lambda i, slot_ref: (slot_ref[0], 0, 0)), SDS((N_DEV, 1, D), F32)
    out_specs = [tile(D)] + ([tile(D)] if want_bf16 else []) + [dg_spec]
    out_shape = [SDS((T, D), F32)] + ([SDS((T, D), BF)] if want_bf16 else []) + [dg_shape]
    return _call(
        body, (*a_list, w_t, xin, g, dres), name=name, grid=(T // tm,), prefetch=() if slot is None else (slot,),
        in_specs=[tile(k) for k in ks] + [_resident(w_t.shape), tile(D),
                                           pl.BlockSpec((1, D), lambda i, *_: (0, 0)), tile(D)],
        out_specs=out_specs, out_shape=out_shape, sem=("arbitrary",), vmem_mib=56, comm=comm)


def _out_bwd(dh1b, w_out, comm=()):
    tm = TM

    def body(d_ref, w_ref, o_ref):
        o_ref[...] = _dot(d_ref[...], w_ref[...], 1, 1)

    return _call(
        body, (dh1b, w_out), name="out_bwd", grid=(T // tm,),
        in_specs=[pl.BlockSpec((tm, D), lambda i: (i, 0)), _resident((D, D))],
        out_specs=[pl.BlockSpec((tm, D), lambda i: (i, 0))], out_shape=[SDS((T, D), F32)],
        sem=("parallel",), vmem_mib=32, comm=comm)


def _wgrad(name, a_list, b, comm=()):
    m_k = a_list[0].shape[1]
    tm = max(t for t in range(128, m_k // 2 + 1, 128) if m_k % t == 0)
    steps = [a.shape[1] // tm for a in a_list]
    starts = [sum(steps[:k]) for k in range(len(a_list))]
    n_a = len(a_list)

    def body(*refs):
        a_refs, b_ref, o_ref = refs[:n_a], refs[n_a], refs[n_a + 1]
        i = pl.program_id(0)
        for k in range(n_a):
            @pl.when((i >= starts[k]) & (i < starts[k] + steps[k]))
            def _(k=k):
                o_ref[...] = _dot(a_refs[k][...], b_ref[...], 0, 0).astype(BF)

    def a_spec(k):
        return pl.BlockSpec((T, tm), lambda i: (0, jnp.clip(i - starts[k], 0, steps[k] - 1)))

    m_total = tm * sum(steps)
    return _call(
        body, (*a_list, b), name=name, grid=(sum(steps),),
        in_specs=[a_spec(k) for k in range(n_a)] + [_resident((T, D))],
        out_specs=[pl.BlockSpec((tm, D), lambda i: (i, 0))], out_shape=[SDS((m_total, D), BF)],
        sem=("parallel",), vmem_mib=40, comm=comm)


def _chip_sum(name, gbf, from_sib, core, chip):
    h = gbf.shape[1]
    th = h // 2

    def body(core_ref, chip_ref, g_ref, s_ref, pbf_ref, own_ref):
        p = g_ref[0].astype(F32) + s_ref[0].astype(F32)
        pbf_ref[0] = p.astype(BF)

        @pl.when(pl.program_id(1) == chip_ref[0])
        def _():
            own_ref[...] = p

    grid_spec = pltpu.PrefetchScalarGridSpec(
        num_scalar_prefetch=2, grid=(h // th, N_CHIPS),
        in_specs=[pl.BlockSpec((1, th, D), lambda t, jj, core_ref, chip_ref: (2 * jj + core_ref[0], t, 0)),
                  pl.BlockSpec((1, th, D), lambda t, jj, core_ref, chip_ref: (jj, t, 0))],
        out_specs=[pl.BlockSpec((1, th, D), lambda t, jj, core_ref, chip_ref: (jj, t, 0)),
                   pl.BlockSpec((th, D), lambda t, jj, core_ref, chip_ref: (t, 0))],
    )
    return _pcall(
        body, name=name, grid_spec=grid_spec, out_shape=_in_hbm([SDS((N_CHIPS, h, D), BF), SDS((h, D), F32)]),
        compiler_params=_params(("arbitrary", "arbitrary"), 32),
    )(core, chip, *_from_hbm(gbf, from_sib))


def _final_sum(name, own, from_chips, core):
    h = own.shape[0]

    def body(core_ref, o_ref, r_ref, f_ref):
        f_ref[0] = ((o_ref[...] + r_ref[0].astype(F32)) + r_ref[1].astype(F32)) + r_ref[2].astype(F32)

    grid_spec = pltpu.PrefetchScalarGridSpec(
        num_scalar_prefetch=1, grid=(1,),
        in_specs=[pl.BlockSpec((h, D), lambda i, core_ref: (0, 0)), pl.BlockSpec((3, h, D), lambda i, core_ref: (0, 0, 0))],
        out_specs=pl.BlockSpec((1, h, D), lambda i, core_ref: (core_ref[0], 0, 0)),
    )
    return _pcall(body, name=name, grid_spec=grid_spec, out_shape=pltpu.HBM((2, h, D), F32),
                  compiler_params=_params(("arbitrary",), 40))(core, *_from_hbm(own, from_chips))


def _adam_math(w, g, m, v):
    nm = ADAM_B1 * m + (1.0 - ADAM_B1) * g
    nv = ADAM_B2 * v + (1.0 - ADAM_B2) * (g * g)
    m_hat = nm / (1.0 - ADAM_B1 ** ADAM_STEP)
    v_hat = nv / (1.0 - ADAM_B2 ** ADAM_STEP)
    return -ADAM_LR * (m_hat / (jnp.sqrt(v_hat) + ADAM_EPS) + ADAM_WD * w), nm, nv


def _adamw(name, w, g, m, v, tr, copy_g=False):
    rows, cols = w.shape

    def body(w_ref, g_ref, m_ref, v_ref, *outs):
        g_val = g_ref[...]
        if copy_g:
            outs[0][...] = g_val
        d_ref, nm_ref, nv_ref = outs[-3:]
        d_ref[...], nm_ref[...], nv_ref[...] = _adam_math(w_ref[...], g_val, m_ref[...], v_ref[...])

    spec = pl.BlockSpec((tr, cols), lambda i: (i, 0))
    n_out = 4 if copy_g else 3
    return _call(body, (w, g, m, v), name=name, grid=(rows // tr,), in_specs=[spec] * 4, out_specs=[spec] * n_out,
                 out_shape=[SDS((rows, cols), F32)] * n_out, sem=("parallel",), vmem_mib=32, free=(0, 2, 3))


C_G1, C_G2, C_GCO, C_GAO, C_DCW, C_DQG, C_DKG, C_SINK, C_SQ = 0, 1024, 2048, 2560, 3072, 4608, 4736, 4864, 5632
P_W = C_SQ + 128


def _pack_small(me, dfwg, dfwv, dfbg, dfbv, dg2, dgco, dgao, dcw8, dqg, dkg, dsink, sq):
    def body(me_ref, dfwg_r, dfwv_r, dfbg_r, dfbv_r, dg2_r, dgco_r, dgao_r, dcw_r, dqg_r, dkg_r, dsink_r, sq_r, o):
        o[...] = jnp.zeros_like(o)
        o[0, :, 0:DFF] = dfwg_r[...]
        o[0, :, DFF:2 * DFF] = dfwv_r[...]
        o[0, 3:4, 0:DFF] = dfbg_r[...]
        o[0, 3:4, DFF:2 * DFF] = dfbv_r[...]
        o[0, 4:5, C_G2:C_G2 + D] = dg2_r[...]
        o[0, 4:5, C_GCO:C_GCO + CW] = dgco_r[...]
        o[0, 4:5, C_GAO:C_GAO + AW] = dgao_r[...]
        for r in range(3):
            o[0, 4:5, C_DCW + r * CW:C_DCW + (r + 1) * CW] = dcw_r[r:r + 1, :]
        o[0, 4:5, C_DQG:C_DQG + HD] = dqg_r[...]
        o[0, 4:5, C_DKG:C_DKG + HD] = dkg_r[...]
        o[0, 4:5, C_SINK:C_SINK + 128] = dsink_r[...]
        o[0, :, C_SQ:C_SQ + 128] = sq_r[...]

    ins = (dfwg, dfwv, dfbg, dfbv, dg2, dgco, dgao, dcw8, dqg, dkg, dsink, sq)
    return _call(body, ins, name="pack_small", grid=(1,), prefetch=(me,),
                 in_specs=[pl.BlockSpec(a.shape, lambda i, me_ref: (0, 0)) for a in ins],
                 out_specs=[pl.BlockSpec((1, 8, P_W), lambda i, me_ref: (me_ref[0], 0, 0))],
                 out_shape=[SDS((N_DEV, 8, P_W), F32)], sem=("arbitrary",))[0]


N_SMALL = 11


def _small_adam(chip, p_all, g1_all, tbl_all, ws, ms, vs):
    fw_cols = 2 * DFF // N_CHIPS
    cw_cols = CW // N_CHIPS

    def body(chip_ref, p_ref, fw_ref, cw0_ref, cw1_ref, cw2_ref, g1_ref, tbl_ref, *refs):
        w_r, m_r, v_r = refs[0:N_SMALL], refs[N_SMALL:2 * N_SMALL], refs[2 * N_SMALL:3 * N_SMALL]
        outs = refs[3 * N_SMALL:]
        g_o, d_o, nm_o, nv_o = (outs[k * N_SMALL:(k + 1) * N_SMALL] for k in range(4))
        loss_o = outs[4 * N_SMALL]

        def total(ref):
            s = ref[0]
            for k in range(1, N_DEV):
                s = s + ref[k]
            return s

        S = total(p_ref)
        fw = total(fw_ref)
        cws = [total(r) for r in (cw0_ref, cw1_ref, cw2_ref)]

        def step(i, g, at):
            d, nm, nv = _adam_math(w_r[i][at], g, m_r[i][at], v_r[i][at])
            g_o[i][at], d_o[i][at], nm_o[i][at], nv_o[i][at] = g, d, nm, nv

        everything = (slice(None), slice(None))
        step(0, total(g1_ref), everything)
        for r in range(3):
            step(1, cws[r][4:5, :], (r, slice(None), slice(None)))
        step(2, S[4:5, C_DQG:C_DQG + HD], everything)
        step(3, S[4:5, C_DKG:C_DKG + HD], everything)
        step(4, total(tbl_ref), everything)
        step(5, S[4:5, C_SINK:C_SINK + NH], everything)
        step(6, S[4:5, C_GCO:C_GCO + CW], everything)
        step(7, S[4:5, C_GAO:C_GAO + AW], everything)
        step(8, S[4:5, C_G2:C_G2 + D], everything)
        for r in range(3):
            step(9, fw[r:r + 1, :], (r, slice(None), slice(None)))
        step(10, S[3:4, 0:2 * DFF], everything)
        sq = S[:, C_SQ:C_SQ + 128]
        loss_o[...] = jnp.sum(jnp.sum(sq, axis=1, keepdims=True), axis=0, keepdims=True) * (0.5 / D)

    def full(a):
        n = len(a.shape)
        return pl.BlockSpec(a.shape, lambda i, chip_ref: (0,) * n)

    params = [*ws, *ms, *vs]
    out = _call(
        body, (p_all, p_all, p_all, p_all, p_all, g1_all, tbl_all, *params), name="small_adam", grid=(1,), prefetch=(chip,),
        in_specs=[full(p_all),
                  pl.BlockSpec((N_DEV, 8, fw_cols), lambda i, chip_ref: (0, 0, chip_ref[0])),
                  *[pl.BlockSpec((N_DEV, 8, cw_cols), lambda i, chip_ref, r=r: (0, 0, (C_DCW + r * CW) // cw_cols + chip_ref[0]))
                    for r in range(3)],
                  full(g1_all), full(tbl_all), *[full(a) for a in params]],
        out_specs=[full(a) for a in ws] * 4 + [pl.BlockSpec((1, 1), lambda i, chip_ref: (0, 0))],
        out_shape=[SDS(a.shape, F32) for a in ws] * 4 + [SDS((1, 1), F32)], sem=("arbitrary",), vmem_mib=32)
    return out[0:N_SMALL], out[N_SMALL:2 * N_SMALL], out[2 * N_SMALL:3 * N_SMALL], out[3 * N_SMALL:4 * N_SMALL], out[4 * N_SMALL]


PLACE_STEPS = 4


def _place_specs(shards):
    rows = [s.shape[0] // PLACE_STEPS for s in shards]
    return ([pl.BlockSpec((r, D), lambda i, chip_ref: (i, 0)) for r in rows],
            [pl.BlockSpec((r, D), lambda i, chip_ref: (chip_ref[0] * PLACE_STEPS + i, 0)) for r in rows],
            [SDS((N_CHIPS * s.shape[0], D), BF) for s in shards])


def _place_first(chip, shard, conv_w, ffn_conv_w):
    def body(chip_ref, a, s0, s1, o, t0, t1):
        o[...] = a[...].astype(BF)

        @pl.when(pl.program_id(0) == 0)
        def _():
            for s, t in ((s0, t0), (s1, t1)):
                t[...] = jnp.zeros_like(t)
                t[0, 0:3, :] = s[...]

    ins, outs, shapes = _place_specs([shard])
    taps = (conv_w, ffn_conv_w)
    return _call(
        body, (shard, conv_w, ffn_conv_w), name="place_first", grid=(PLACE_STEPS,), prefetch=(chip,),
        in_specs=ins + [pl.BlockSpec(s.shape, lambda i, chip_ref: (0, 0)) for s in taps],
        out_specs=outs + [pl.BlockSpec((1, 8, s.shape[1]), lambda i, chip_ref: (chip_ref[0], 0, 0)) for s in taps],
        out_shape=shapes + [SDS((N_CHIPS, 8, s.shape[1]), F32) for s in taps],
        sem=("arbitrary",), vmem_mib=32, free=(0, 1, 2))


def _place_rest(chip, shards, table, bucket, comm):
    n = len(shards)

    def body(chip_ref, *refs):
        a, (tab_ref, bk_ref), o, bias_ref = refs[:n], refs[n:n + 2], refs[n + 2:2 * n + 2], refs[2 * n + 2]
        for src, dst in zip(a, o):
            dst[...] = src[...].astype(BF)

        @pl.when(pl.program_id(0) == 0)
        def _():
            bk = bk_ref[...]
            eq = [bk == b for b in range(NBUCKET)]
            for h in range(NH):
                acc = jnp.zeros((BLK, 2 * BLK), F32)
                for b in range(NBUCKET):
                    acc = jnp.where(eq[b], tab_ref[h, b], acc)
                bias_ref[h * BLK:(h + 1) * BLK, :] = acc

    ins, outs, shapes = _place_specs(shards)
    return _call(
        body, (*shards, table, bucket), name="place_rest", grid=(PLACE_STEPS,), prefetch=(chip,),
        in_specs=ins + [pl.BlockSpec(memory_space=pltpu.SMEM), pl.BlockSpec(bucket.shape, lambda i, chip_ref: (0, 0))],
        out_specs=outs + [pl.BlockSpec((NH * BLK, 2 * BLK), lambda i, chip_ref: (0, 0))],
        out_shape=shapes + [SDS((NH * BLK, 2 * BLK), F32)],
        sem=("arbitrary",), vmem_mib=32, comm=comm, free=tuple(range(n + 2)))


def kernel(x, norm_mix_g, w_in, conv_w, q_norm_g, k_norm_g, rel_bias_table, sinks, out_norm_conv_g, out_norm_attn_g, w_out, norm_ffn_g, w_up, ffn_conv_w, ffn_conv_b, w_down, loss_target, m_norm_mix_g, m_w_in, m_conv_w, m_q_norm_g, m_k_norm_g, m_rel_bias_table, m_sinks, m_out_norm_conv_g, m_out_norm_attn_g, m_w_out, m_norm_ffn_g, m_w_up, m_ffn_conv_w, m_ffn_conv_b, m_w_down, v_norm_mix_g, v_w_in, v_conv_w, v_q_norm_g, v_k_norm_g, v_rel_bias_table, v_sinks, v_out_norm_conv_g, v_out_norm_attn_g, v_w_out, v_norm_ffn_g, v_w_up, v_ffn_conv_w, v_ffn_conv_b, v_w_down):
    as_arg = lambda i: jnp.reshape(i, (1,)).astype(jnp.int32)
    chip = as_arg(2 * lax.axis_index("x") + lax.axis_index("y"))
    core = as_arg(lax.axis_index("c"))
    me = 2 * chip + core
    xs, tgt = x[0], loss_target[0]
    qg, kg, gco, gao, g1, g2, fb = q_norm_g, k_norm_g, out_norm_conv_g, out_norm_attn_g, norm_mix_g, norm_ffn_g, ffn_conv_b
    pieces = lambda g: g.reshape(N_DEV, g.shape[0] // N_DEV, D)
    whole = lambda f: f.reshape(2 * f.shape[1], D)

    bucket = jnp.asarray(_bucket_table())
    p_in, p_cw, p_fw = _place_first(chip, w_in[0].T, conv_w[0], ffn_conv_w[0])
    p_out, p_up, p_down, bias, w_int, cw_all, fw_all = _place_rest(
        chip, [w_out[0], w_up[0].T, w_down[0]], rel_bias_table.T, bucket,
        comm=[_t_gather(p_in), _t_small_weights(p_cw), _t_small_weights(p_fw)])
    cw8 = jnp.transpose(cw_all, (1, 0, 2)).reshape(8, CW)
    fw8 = jnp.transpose(fw_all, (1, 0, 2)).reshape(8, 2 * DFF)

    proj, u1, w_out_f = _inproj(xs, g1, w_int, comm=[_t_gather(p_out)])
    y, w_upt = _mix_fwd(proj, sinks, cw8, qg, kg, gco, gao, bias, comm=[_t_gather(p_up)])
    h1, u2 = _outproj(y, w_out_f, xs, g2)
    up, w_down_f = _ffn_up(u2, w_upt, comm=[_t_gather(p_down)])
    a, = _ffn_act(up, fw8, fb)
    dh2, dh2b, sq = _ffn_down(a, w_down_f, h1, tgt)

    gdbf, = _wgrad("wgrad_down", [a], dh2b)
    da, sib_down = _ffn_down_bwd(dh2b, w_down_f, comm=[_t_sibling(pieces(gdbf))])
    pbf_down, own_down = _chip_sum("chip_sum_w_down", pieces(gdbf), sib_down, core, chip)
    dug, duv, dfwg, dfwv, dfbg, dfbv, chips_down = _ffn_act_bwd(up, da, fw8, fb, comm=[_t_chips(pbf_down)])
    fin_down = _final_sum("final_sum_w_down", own_down, chips_down, core)
    gubf, = _wgrad("wgrad_up", [dug, duv], u2)
    dh1, dh1b, dg2, sib_up, fin_down = _norm_matmul_bwd(
        "ffn_up_bwd", [dug, duv], w_upt, [0, DFF], h1, g2, dh2, True, comm=[_t_sibling(pieces(gubf)), _t_swap(fin_down)])
    pbf_up, own_up = _chip_sum("chip_sum_w_up", pieces(gubf), sib_up, core, chip)
    gobf, = _wgrad("wgrad_out", [y], dh1b)
    dy, sib_out = _out_bwd(dh1b, w_out_f, comm=[_t_sibling(pieces(gobf))])
    pbf_out, own_out = _chip_sum("chip_sum_w_out", pieces(gobf), sib_out, core, chip)
    dproj, dcw8, dqg, dkg, dgco, dgao, dsink, dbias, chips_up, chips_out = _mix_bwd(
        proj, dy, sinks, cw8, qg, kg, gco, gao, bias, comm=[_t_chips(pbf_up), _t_chips(pbf_out)])
    fin_up = _final_sum("final_sum_w_up", own_up, chips_up, core)
    fin_out = _final_sum("final_sum_w_out", own_out, chips_out, core)
    tbl_all = _band_bias_bwd(dbias, bucket, me)
    p_all = _pack_small(me, dfwg, dfwv, dfbg, dfbv, dg2, dgco, dgao, dcw8, dqg, dkg, dsink, sq)
    gibf, fin_up, p_all, tbl_all = _wgrad(
        "wgrad_in", [dproj], u1, comm=[_t_swap(fin_up), _t_allgather(p_all), _t_allgather(tbl_all)])
    sib_in, fin_out = _comm_call("to_sibling_last", [_t_sibling(pieces(gibf)), _t_swap(fin_out)])
    pbf_in, own_in = _chip_sum("chip_sum_w_in", pieces(gibf), sib_in, core, chip)
    dx, g1_all, chips_in = _norm_matmul_bwd(
        "in_bwd", [dproj], w_int, [0], xs, g1, dh1, False, comm=[_t_chips(pbf_in)], slot=me)
    fin_in = _final_sum("final_sum_w_in", own_in, chips_in, core)
    g1_all, fin_in = _comm_call("gather_last", [_t_allgather(g1_all), _t_swap(fin_in)])

    g_w_out, g_w_up, g_w_down = whole(fin_out), whole(fin_up).T, whole(fin_down)
    g_w_down, d_down, nm_down, nv_down = _adamw("adamw_w_down", w_down[0], g_w_down, m_w_down[0], v_w_down[0], 352, True)
    d_up, nm_up, nv_up = _adamw("adamw_w_up", w_up[0], g_w_up, m_w_up[0], v_w_up[0], 256)
    g_w_out, d_out, nm_out, nv_out = _adamw("adamw_w_out", w_out[0], g_w_out, m_w_out[0], v_w_out[0], 256, True)
    g_w_in, d_in, nm_in, nv_in = [a.T for a in _adamw(
        "adamw_w_in", w_in[0].T, whole(fin_in), m_w_in[0].T, v_w_in[0].T, INW // N_CHIPS // 3, True)]
    taps = lambda a: jnp.transpose(a, (1, 0, 2))
    sw = [norm_mix_g, taps(conv_w), q_norm_g, k_norm_g, rel_bias_table.T, sinks, out_norm_conv_g, out_norm_attn_g,
          norm_ffn_g, taps(ffn_conv_w), ffn_conv_b]
    smm = [m_norm_mix_g, taps(m_conv_w), m_q_norm_g, m_k_norm_g, m_rel_bias_table.T, m_sinks, m_out_norm_conv_g,
           m_out_norm_attn_g, m_norm_ffn_g, taps(m_ffn_conv_w), m_ffn_conv_b]
    smv = [v_norm_mix_g, taps(v_conv_w), v_q_norm_g, v_k_norm_g, v_rel_bias_table.T, v_sinks, v_out_norm_conv_g,
           v_out_norm_attn_g, v_norm_ffn_g, taps(v_ffn_conv_w), v_ffn_conv_b]
    *small_out, loss = _small_adam(chip, p_all, g1_all, tbl_all, sw, smm, smv)
    sg, sd, snm, snv = [list(r) for r in small_out]
    for r in (sg, sd, snm, snv):
        r[1], r[4], r[9] = taps(r[1]), r[4].T, taps(r[9])

    def order(s, b_in, b_out, b_up, b_down):
        return (s[0], b_in[None], s[1], s[2], s[3], s[4], s[5], s[6], s[7], b_out[None], s[8], b_up[None],
                s[9], s[10], b_down[None])

    return (loss.reshape(()), dx[None],
            *order(sg, g_w_in, g_w_out, g_w_up, g_w_down),
            *order(sd, d_in, d_out, d_up, d_down),
            *order(snm, nm_in, nm_out, nm_up, nm_down),
            *order(snv, nv_in, nv_out, nv_up, nv_down))
```

```python
import functools
import math

import numpy as np

import jax
import jax.numpy as jnp
from jax import lax
from jax.experimental import pallas as pl
from jax.experimental.pallas import tpu as pltpu

F32 = jnp.float32
BF = jnp.bfloat16
SDS = jax.ShapeDtypeStruct

T = 2048
D = 1024
CW = 512
AW = 512
HD = 64
NH = 8
NKV = 2
GQ = 4
INW = 2304
DFF = 2816
BLK = 128
NB = T // BLK
NBUCKET = 32
EPS = 1e-6
NEG_INF = -1e30
N_CHIPS = 4
N_DEV = 8

ADAM_LR = 0.001
ADAM_B1 = 0.9
ADAM_B2 = 0.999
ADAM_EPS = 1e-08
ADAM_WD = 0.01
ADAM_STEP = 10

TM = 512
MIB = 1024 * 1024
MESH = pl.DeviceIdType.MESH
ANY = pl.BlockSpec(memory_space=pl.ANY)

_pcall = pl.pallas_call


def _params(sem=None, vmem_mib=None):
    kw = {}
    if sem is not None:
        kw["dimension_semantics"] = sem
    if vmem_mib is not None:
        kw["vmem_limit_bytes"] = vmem_mib * MIB
    return pltpu.CompilerParams(**kw)


def _resident(shape):
    return pl.BlockSpec(shape, lambda *_: (0,) * len(shape), pipeline_mode=pl.Buffered(1))


def _dot(a, b, ca, cb):
    return lax.dot_general(a, b, (((ca,), (cb,)), ((), ())), preferred_element_type=F32)


def _rms_bwd(dy, x, r, g):
    dg = jnp.sum(dy * (x * r), axis=0, keepdims=True)
    dgx = dy * g
    dx = r * dgx - x * (r * r * r) * jnp.mean(x * dgx, axis=-1, keepdims=True)
    return dx, dg


def _where():
    x, y, c = lax.axis_index("x"), lax.axis_index("y"), lax.axis_index("c")
    return x, y, c, [(1 - x, y), (x, 1 - y), (1 - x, 1 - y)]


def _rcopy(src, dst, ssem, rsem, dev):
    return pltpu.make_async_remote_copy(src_ref=src, dst_ref=dst, send_sem=ssem, recv_sem=rsem, device_id=dev,
                                        device_id_type=MESH)


class _Task:
    def __init__(self, ins, outs, alias, n_sem, start, finish, middle=None):
        self.ins, self.outs, self.alias, self.n_sem, self.start, self.finish = ins, outs, alias, n_sem, start, finish
        self.middle = middle if middle is not None else (lambda *args: None)


def _t_gather(placed):
    R = placed.shape[0] // N_CHIPS
    q = R // 4

    def quarter(chip_index, core, k):
        return pl.ds(pl.multiple_of(chip_index * R + core * 2 * q + k * q, 16), q)

    def half(chip_index, core):
        return pl.ds(pl.multiple_of(chip_index * R + core * 2 * q, 16), 2 * q)

    def places():
        x, y, c, _ = _where()
        return c, 2 * x + y, 2 * (1 - x) + y, 2 * x + (1 - y), 2 * (1 - x) + (1 - y), (1 - x, y, c), (x, 1 - y, c), (x, y, 1 - c)

    def start(cin, cout, ss, rs, b):
        c, me, _, _, _, x_nbr, y_nbr, _ = places()
        buf = cout[0]
        for k, (quart, dev) in enumerate(((0, x_nbr), (1, y_nbr), (1, x_nbr), (0, y_nbr))):
            part = buf.at[quarter(me, c, quart)]
            _rcopy(part, part, ss.at[b + k], rs.at[b + k], dev).start()

    def middle(cin, cout, ss, rs, b):
        c, _, xc, yc, _, x_nbr, y_nbr, _ = places()
        buf = cout[0]
        for k, chip_index, quart, dev in ((0, xc, 0, y_nbr), (1, yc, 1, x_nbr)):
            part = buf.at[quarter(chip_index, c, quart)]
            _rcopy(part, part, ss.at[b + k], rs.at[b + k], dev).wait_recv()
            _rcopy(part, part, ss.at[b + 4 + k], rs.at[b + 4 + k], dev).start()

    def finish(cin, cout, ss, rs, b):
        c, me, xc, yc, dc, x_nbr, y_nbr, sib = places()
        buf = cout[0]
        for k, chip_index, quart in ((2, xc, 1), (3, yc, 0), (4, dc, 0), (5, dc, 1)):
            part = buf.at[quarter(chip_index, c, quart)]
            _rcopy(part, part, ss.at[b + k], rs.at[b + k], sib).wait_recv()
        for k, chip_index in ((6, xc), (7, yc), (8, dc)):
            got = buf.at[half(chip_index, c)]
            _rcopy(got, got, ss.at[b + k], rs.at[b + k], sib).start()
        for k, chip_index in ((6, xc), (7, yc), (8, dc)):
            got = buf.at[half(chip_index, 1 - c)]
            _rcopy(got, got, ss.at[b + k], rs.at[b + k], sib).wait_recv()
        for k in range(6):
            part = buf.at[quarter(me, c, 0)]
            _rcopy(part, part, ss.at[b + k], rs.at[b + k], sib).wait_send()
        for k in range(6, 9):
            got = buf.at[half(me, c)]
            _rcopy(got, got, ss.at[b + k], rs.at[b + k], sib).wait_send()

    return _Task([placed], [SDS(placed.shape, placed.dtype)], [(0, 0)], 9, start, finish, middle)


def _t_small_weights(buf):
    def start(cin, cout, ss, rs, b):
        x, y, c, chips = _where()
        mine = cout[0].at[2 * x + y]
        for r, (px, py) in enumerate(chips):
            _rcopy(mine, mine, ss.at[b + r], rs.at[b + r], (px, py, c)).start()

    def finish(cin, cout, ss, rs, b):
        x, y, c, chips = _where()
        for r, (px, py) in enumerate(chips):
            got = cout[0].at[2 * px + py]
            _rcopy(got, got, ss.at[b + r], rs.at[b + r], (px, py, c)).wait_recv()
        for r, (px, py) in enumerate(chips):
            mine = cout[0].at[2 * x + y]
            _rcopy(mine, mine, ss.at[b + r], rs.at[b + r], (px, py, c)).wait_send()

    return _Task([buf], [SDS(buf.shape, buf.dtype)], [(0, 0)], 3, start, finish)


def _t_sibling(gbf):
    def start(cin, cout, ss, rs, b):
        x, y, c, _ = _where()
        for jj in range(N_CHIPS):
            _rcopy(cin[0].at[2 * jj + (1 - c)], cout[0].at[jj], ss.at[b + jj], rs.at[b + jj], (x, y, 1 - c)).start()

    def finish(cin, cout, ss, rs, b):
        x, y, c, _ = _where()
        for jj in range(N_CHIPS):
            got = cout[0].at[jj]
            _rcopy(got, got, ss.at[b + jj], rs.at[b + jj], (x, y, 1 - c)).wait_recv()
        for jj in range(N_CHIPS):
            got = cout[0].at[jj]
            _rcopy(got, got, ss.at[b + jj], rs.at[b + jj], (x, y, 1 - c)).wait_send()

    return _Task([gbf], [SDS((N_CHIPS,) + gbf.shape[1:], BF)], [], N_CHIPS, start, finish)


def _t_chips(pbf):
    def start(cin, cout, ss, rs, b):
        x, y, c, chips = _where()
        for r, (px, py) in enumerate(chips):
            _rcopy(cin[0].at[2 * px + py], cout[0].at[r], ss.at[b + r], rs.at[b + r], (px, py, c)).start()

    def finish(cin, cout, ss, rs, b):
        x, y, c, chips = _where()
        for r, (px, py) in enumerate(chips):
            got = cout[0].at[r]
            _rcopy(got, got, ss.at[b + r], rs.at[b + r], (px, py, c)).wait_recv()
        for r, (px, py) in enumerate(chips):
            got = cout[0].at[r]
            _rcopy(got, got, ss.at[b + r], rs.at[b + r], (px, py, c)).wait_send()

    return _Task([pbf], [SDS((3,) + pbf.shape[1:], BF)], [], 3, start, finish)


def _t_swap(fin):
    def start(cin, cout, ss, rs, b):
        x, y, c, _ = _where()
        mine = cout[0].at[c]
        _rcopy(mine, mine, ss.at[b], rs.at[b], (x, y, 1 - c)).start()

    def finish(cin, cout, ss, rs, b):
        x, y, c, _ = _where()
        got = cout[0].at[1 - c]
        _rcopy(got, got, ss.at[b], rs.at[b], (x, y, 1 - c)).wait_recv()
        _rcopy(got, got, ss.at[b], rs.at[b], (x, y, 1 - c)).wait_send()

    return _Task([fin], [SDS(fin.shape, fin.dtype)], [(0, 0)], 1, start, finish)


def _t_allgather(buf):
    def peers():
        x, y, c, _ = _where()
        out = []
        for rel in range(1, N_DEV):
            px, py, pc = x ^ ((rel >> 2) & 1), y ^ ((rel >> 1) & 1), c ^ (rel & 1)
            out.append((rel - 1, 4 * px + 2 * py + pc, (px, py, pc)))
        return 4 * x + 2 * y + c, out

    def start(cin, cout, ss, rs, b):
        me, ps = peers()
        mine = cout[0].at[me]
        for k, _, dev in ps:
            _rcopy(mine, mine, ss.at[b + k], rs.at[b + k], dev).start()

    def finish(cin, cout, ss, rs, b):
        me, ps = peers()
        for k, pidx, dev in ps:
            got = cout[0].at[pidx]
            _rcopy(got, got, ss.at[b + k], rs.at[b + k], dev).wait_recv()
        for k, _, dev in ps:
            mine = cout[0].at[me]
            _rcopy(mine, mine, ss.at[b + k], rs.at[b + k], dev).wait_send()

    return _Task([buf], [SDS(buf.shape, buf.dtype)], [(0, 0)], N_DEV - 1, start, finish)


def _run_tasks(comm, which, cin, cout, ss, rs):
    i0 = o0 = s0 = 0
    for t in comm:
        getattr(t, which)(cin[i0:i0 + len(t.ins)], cout[o0:o0 + len(t.outs)], ss, rs, s0)
        i0, o0, s0 = i0 + len(t.ins), o0 + len(t.outs), s0 + t.n_sem


def _from_hbm(*arrays):
    return [pltpu.with_memory_space_constraint(a, pltpu.HBM) for a in arrays]


def _in_hbm(shapes):
    return [pltpu.HBM(s.shape, s.dtype) for s in shapes]


def _comm_layout(comm, n_in, n_out):
    c_in = [a for t in comm for a in t.ins]
    c_out = [s for t in comm for s in t.outs]
    aliases, i0, o0 = {}, 0, 0
    for t in comm:
        for i, o in t.alias:
            aliases[n_in + i0 + i] = n_out + o0 + o
        i0, o0 = i0 + len(t.ins), o0 + len(t.outs)
    return c_in, c_out, aliases, sum(t.n_sem for t in comm)


def _call(body, operands, *, name, grid, in_specs, out_specs, out_shape, scratch_shapes=(), sem=None, vmem_mib=None, comm=(),
          free=(), prefetch=()):
    operands = [o if s.memory_space == pltpu.SMEM or k in free else pltpu.with_memory_space_constraint(o, pltpu.HBM)
                for k, (o, s) in enumerate(zip(operands, in_specs))]
    n_pre, n_in, n_out, n_scr = len(prefetch), len(in_specs), len(out_specs), len(scratch_shapes)
    c_in, c_out, aliases, n_sem = _comm_layout(comm, n_pre + n_in, n_out)
    sems = [pltpu.SemaphoreType.DMA((n_sem,)), pltpu.SemaphoreType.DMA((n_sem,))] if comm else []

    def wrapped(*refs):
        pre, refs = refs[:n_pre], refs[n_pre:]
        ins, cin = refs[:n_in], refs[n_in:n_in + len(c_in)]
        rest = refs[n_in + len(c_in):]
        outs, cout = rest[:n_out], rest[n_out:n_out + len(c_out)]
        rest = rest[n_out + len(c_out):]
        scr, csem = rest[:n_scr], rest[n_scr:]
        if not comm:
            return body(*pre, *ins, *outs, *scr)
        step = functools.reduce(lambda acc, k: acc * grid[k] + pl.program_id(k), range(len(grid)), 0)
        n_steps = math.prod(grid)
        pl.when(step == 0)(lambda: _run_tasks(comm, "start", cin, cout, *csem))
        pl.when(step == n_steps // 2)(lambda: _run_tasks(comm, "middle", cin, cout, *csem))
        body(*pre, *ins, *outs, *scr)
        pl.when(step == n_steps - 1)(lambda: _run_tasks(comm, "finish", cin, cout, *csem))

    grid_spec = pltpu.PrefetchScalarGridSpec(
        num_scalar_prefetch=n_pre, grid=grid, in_specs=list(in_specs) + [ANY] * len(c_in),
        out_specs=list(out_specs) + [ANY] * len(c_out), scratch_shapes=list(scratch_shapes) + sems)
    return _pcall(
        wrapped, name=name, grid_spec=grid_spec, out_shape=_in_hbm(list(out_shape) + c_out), input_output_aliases=aliases,
        compiler_params=_params(("arbitrary",) * len(grid) if comm else sem, vmem_mib),
    )(*prefetch, *operands, *_from_hbm(*c_in))


def _comm_call(name, comm):
    c_in, c_out, aliases, n_sem = _comm_layout(comm, 0, 0)

    def body(*refs):
        cin, cout, (ss, rs) = refs[:len(c_in)], refs[len(c_in):len(c_in) + len(c_out)], refs[len(c_in) + len(c_out):]
        for phase in ("start", "middle", "finish"):
            _run_tasks(comm, phase, cin, cout, ss, rs)

    return _pcall(
        body, name=name, in_specs=[ANY] * len(c_in), out_specs=[ANY] * len(c_out), out_shape=_in_hbm(c_out),
        scratch_shapes=[pltpu.SemaphoreType.DMA((n_sem,)), pltpu.SemaphoreType.DMA((n_sem,))],
        input_output_aliases=aliases,
    )(*_from_hbm(*c_in))


def _inproj(x, g1, w_int, comm=()):
    tm = TM

    def body(x_ref, g_ref, w_ref, proj_ref, u_ref):
        xf = x_ref[...]
        r = lax.rsqrt(jnp.mean(xf * xf, axis=-1, keepdims=True) + EPS)
        u = (xf * r * g_ref[...]).astype(BF)
        u_ref[...] = u
        proj_ref[...] = _dot(u, w_ref[...], 1, 1)

    return _call(
        body, (x, g1, w_int), name="inproj", grid=(T // tm,),
        in_specs=[pl.BlockSpec((tm, D), lambda i: (i, 0)), pl.BlockSpec((1, D), lambda i: (0, 0)),
                  _resident((INW, D))],
        out_specs=[pl.BlockSpec((tm, INW), lambda i: (i, 0)), pl.BlockSpec((tm, D), lambda i: (i, 0))],
        out_shape=[SDS((T, INW), F32), SDS((T, D), BF)], sem=("parallel",), vmem_mib=40, comm=comm)


def _outproj(y, w_out, x, g2):
    tm = TM

    def body(y_ref, w_ref, x_ref, g_ref, h1_ref, u2_ref):
        h1 = x_ref[...] + _dot(y_ref[...], w_ref[...], 1, 0)
        h1_ref[...] = h1
        r = lax.rsqrt(jnp.mean(h1 * h1, axis=-1, keepdims=True) + EPS)
        u2_ref[...] = (h1 * r * g_ref[...]).astype(BF)

    return _call(
        body, (y, w_out, x, g2), name="outproj", grid=(T // tm,),
        in_specs=[pl.BlockSpec((tm, D), lambda i: (i, 0)), _resident((D, D)),
                  pl.BlockSpec((tm, D), lambda i: (i, 0)), pl.BlockSpec((1, D), lambda i: (0, 0))],
        out_specs=[pl.BlockSpec((tm, D), lambda i: (i, 0)), pl.BlockSpec((tm, D), lambda i: (i, 0))],
        out_shape=[SDS((T, D), F32), SDS((T, D), BF)], sem=("parallel",), vmem_mib=32)


def _ffn_up(u2, w_upt, comm=()):
    tm, tn = 1024, 512

    def body(u_ref, w_ref, o_ref):
        o_ref[...] = _dot(u_ref[...], w_ref[...], 1, 1).astype(BF)

    return _call(
        body, (u2, w_upt), name="ffn_up", grid=(T // tm, 2 * DFF // tn),
        in_specs=[pl.BlockSpec((tm, D), lambda i, j: (i, 0)), pl.BlockSpec((tn, D), lambda i, j: (j, 0))],
        out_specs=[pl.BlockSpec((tm, tn), lambda i, j: (i, j))], out_shape=[SDS((T, 2 * DFF), BF)],
        sem=("parallel", "parallel"), vmem_mib=32, comm=comm)


def _ffn_down(a, w_down, h1, tgt):
    tm = TM

    def body(a_ref, w_ref, h1_ref, t_ref, dh_ref, dhb_ref, l_ref):
        @pl.when(pl.program_id(0) == 0)
        def _():
            l_ref[...] = jnp.zeros_like(l_ref)

        h2 = h1_ref[...] + _dot(a_ref[...], w_ref[...], 1, 0)
        e = h2 - t_ref[...]
        dh = e * (1.0 / D)
        dh_ref[...] = dh
        dhb_ref[...] = dh.astype(BF)
        e2 = jnp.sum((e * e).reshape(tm // 8, 8, D), axis=0)
        acc = e2[:, 0:128]
        for k in range(1, D // 128):
            acc = acc + e2[:, k * 128:(k + 1) * 128]
        l_ref[...] += acc

    return _call(
        body, (a, w_down, h1, tgt), name="ffn_down", grid=(T // tm,),
        in_specs=[pl.BlockSpec((tm, DFF), lambda i: (i, 0)), _resident((DFF, D)),
                  pl.BlockSpec((tm, D), lambda i: (i, 0)), pl.BlockSpec((tm, D), lambda i: (i, 0))],
        out_specs=[pl.BlockSpec((tm, D), lambda i: (i, 0)), pl.BlockSpec((tm, D), lambda i: (i, 0)),
                   pl.BlockSpec((8, 128), lambda i: (0, 0))],
        out_shape=[SDS((T, D), F32), SDS((T, D), BF), SDS((8, 128), F32)], sem=("arbitrary",), vmem_mib=40)


def _bucket_table():
    q = np.arange(BLK, dtype=np.int32)[:, None]
    j = np.arange(2 * BLK, dtype=np.int32)[None, :]
    n = np.maximum(q + BLK - j, 0)
    nf = np.maximum(n, 1).astype(np.float32)
    max_exact = NBUCKET // 2
    large = max_exact + (np.log(nf / np.float32(max_exact)) / np.float32(math.log(BLK / max_exact))
                         * np.float32(NBUCKET - max_exact)).astype(np.int32)
    large = np.minimum(large, NBUCKET - 1)
    return np.where(n < max_exact, n, large).astype(np.int32)


def _band_bias_bwd(dbias, bucket, me):
    def body(me_ref, db_ref, bk_ref, o_ref):
        bk = bk_ref[...]
        for b in range(NBUCKET):
            m = bk == b
            for h in range(NH):
                v = jnp.where(m, db_ref[h * BLK:(h + 1) * BLK, :], 0.0)
                s = jnp.sum(jnp.sum(v, axis=1, keepdims=True), axis=0, keepdims=True)
                o_ref[0, h:h + 1, b:b + 1] = s

    grid_spec = pltpu.PrefetchScalarGridSpec(
        num_scalar_prefetch=1, grid=(1,),
        in_specs=[pl.BlockSpec((NH * BLK, 2 * BLK), lambda i, me_ref: (0, 0)),
                  pl.BlockSpec((BLK, 2 * BLK), lambda i, me_ref: (0, 0))],
        out_specs=pl.BlockSpec((1, NH, NBUCKET), lambda i, me_ref: (me_ref[0], 0, 0)),
    )
    return _pcall(body, name="band_bias_bwd", grid_spec=grid_spec, out_shape=SDS((N_DEV, NH, NBUCKET), F32),
                  compiler_params=_params(("arbitrary",)))(me, dbias, bucket)


def _two_bf16(x):
    hi = x.astype(BF)
    return hi, (x - hi.astype(F32)).astype(BF)


def _head_sums(x, seg):
    hi, lo = _two_bf16(x)
    s = seg[0:x.shape[1], :]
    return _dot(hi, s, 1, 0) + _dot(lo, s, 1, 0)


def _head_spread(v, seg, width):
    hi, lo = _two_bf16(v)
    s = seg[0:width, :]
    return _dot(hi, s, 1, 1) + _dot(lo, s, 1, 1)


def _head_norm(x, g_t, seg):
    r = lax.rsqrt(_head_sums(x * x, seg) * (1.0 / HD) + EPS)
    r = _head_spread(r, seg, x.shape[1])
    return x * r * g_t, r


def _head_norm_bwd(dy, x, r, g_t, seg):
    dg_t = jnp.sum(dy * (x * r), axis=0, keepdims=True)
    dgx = dy * g_t
    mean = _head_spread(_head_sums(x * dgx, seg) * (1.0 / HD), seg, x.shape[1])
    return r * dgx - x * (r * r * r) * mean, dg_t


def _fold_heads(v):
    out = v[:, 0:HD]
    for h in range(1, v.shape[1] // HD):
        out = out + v[:, h * HD:(h + 1) * HD]
    return out


def _mix_forward(P, zc8, zh8, pkv, first, cw, qg_t, kg_t, gco, gao, seg, sink_ref, bias_ref):
    gate_b = P[:, 0:CW]
    gate_c = P[:, CW:2 * CW]
    hc = P[:, 2 * CW:3 * CW]
    z = gate_c * hc
    keep = jnp.where(first, 0.0, 1.0)
    zp = zc8 * zh8 * keep
    p1 = zp[7:8, :]
    p2 = zp[6:7, :]
    row = lax.broadcasted_iota(jnp.int32, (BLK, 1), 0)
    z1 = jnp.where(row == 0, p1, pltpu.roll(z, 1, 0))
    z2 = jnp.where(row == 0, p2, jnp.where(row == 1, p1, pltpu.roll(z, 2, 0)))
    cz = cw[0:1, :] * z2 + cw[1:2, :] * z1 + cw[2:3, :] * z
    y_conv = gate_b * cz

    scale = HD ** -0.5
    qi = lax.broadcasted_iota(jnp.int32, (GQ * BLK, 2 * BLK), 0) & (BLK - 1)
    kj = lax.broadcasted_iota(jnp.int32, (GQ * BLK, 2 * BLK), 1)
    dd = qi + BLK - kj
    first_key = jnp.where(first, BLK, 0)
    valid = (dd >= 0) & (dd < BLK) & (kj >= first_key)

    q0 = 3 * CW
    k0 = q0 + AW
    v0 = k0 + NKV * HD
    q_raw = P[:, q0:k0]
    qn, rq = _head_norm(q_raw, qg_t, seg)
    qs = (qn * scale).astype(BF)
    k_raw = jnp.concatenate([pkv[:, 0:NKV * HD], P[:, k0:v0]], axis=0)
    kn, rk = _head_norm(k_raw, kg_t, seg)
    knb = kn.astype(BF)
    heads = []
    outs = []
    for kv in range(NKV):
        kb = knb[:, kv * HD:(kv + 1) * HD]
        vb = jnp.concatenate([pkv[:, NKV * HD + kv * HD:NKV * HD + (kv + 1) * HD],
                              P[:, v0 + kv * HD:v0 + (kv + 1) * HD]], axis=0).astype(BF)
        Q = jnp.concatenate([qs[:, (kv * GQ + g) * HD:(kv * GQ + g + 1) * HD] for g in range(GQ)], axis=0)
        S = _dot(Q, kb, 1, 1) + bias_ref[kv * GQ * BLK:(kv + 1) * GQ * BLK, :]
        S = jnp.where(valid, S, NEG_INF)
        sink = jnp.concatenate([jnp.full((BLK, 1), sink_ref[0, kv * GQ + g], F32) for g in range(GQ)], axis=0)
        m = jnp.maximum(jnp.max(S, axis=-1, keepdims=True), sink)
        p = jnp.exp(S - m)
        es = jnp.exp(sink - m)
        denom = jnp.sum(p, axis=-1, keepdims=True) + es
        probs = p / denom
        O = _dot(probs.astype(BF), vb, 1, 0)
        heads.append(dict(kb=kb, vb=vb, Q=Q, probs=probs, psink=es / denom, O=O))
        outs += [O[g * BLK:(g + 1) * BLK, :] for g in range(GQ)]
    y_attn = jnp.concatenate(outs, axis=1)

    rc = lax.rsqrt(jnp.mean(y_conv * y_conv, axis=-1, keepdims=True) + EPS)
    ra = lax.rsqrt(jnp.mean(y_attn * y_attn, axis=-1, keepdims=True) + EPS)
    y = jnp.concatenate([y_conv * rc * gco, y_attn * ra * gao], axis=1)
    return dict(gate_b=gate_b, gate_c=gate_c, hc=hc, z=z, z1=z1, z2=z2, cz=cz, y_conv=y_conv, y_attn=y_attn,
                rc=rc, ra=ra, heads=heads, y=y, row=row, scale=scale, q_raw=q_raw, rq=rq, k_raw=k_raw, rk=rk)


BPS = 2
TILE = BPS * BLK
KV0 = 3 * CW + AW


def _mix_in_specs(tile_of):
    return [
        pl.BlockSpec(memory_space=pltpu.SMEM),
        pl.BlockSpec((TILE, INW), lambda s: (tile_of(s), 0)),
        pl.BlockSpec((8, CW), lambda s: (jnp.maximum(tile_of(s) * (TILE // 8) - 1, 0), 1)),
        pl.BlockSpec((8, CW), lambda s: (jnp.maximum(tile_of(s) * (TILE // 8) - 1, 0), 2)),
        pl.BlockSpec((BLK, 2 * NKV * HD), lambda s: (jnp.maximum(tile_of(s) * BPS - 1, 0), KV0 // (2 * NKV * HD))),
    ]


def _block_inputs(tile, b, zc_ref, zh_ref, pkv_ref, first_tile):
    P = tile[b * BLK:(b + 1) * BLK, :]
    if b == 0:
        return P, zc_ref[...], zh_ref[...], pkv_ref[...], first_tile
    lo = b * BLK
    return P, tile[lo - 8:lo, CW:2 * CW], tile[lo - 8:lo, 2 * CW:3 * CW], tile[lo - BLK:lo, KV0:KV0 + 2 * NKV * HD], False


def _mix_param_specs():
    return [
        pl.BlockSpec((8, CW), lambda s: (0, 0)),
        pl.BlockSpec((1, AW), lambda s: (0, 0)),
        pl.BlockSpec((1, NKV * HD), lambda s: (0, 0)),
        pl.BlockSpec((1, CW), lambda s: (0, 0)),
        pl.BlockSpec((1, AW), lambda s: (0, 0)),
        pl.BlockSpec((AW, 128), lambda s: (0, 0)),
        pl.BlockSpec((NH * BLK, 2 * BLK), lambda s: (0, 0)),
    ]


def _mix_params(cw8, qg, kg, gco, gao, bias):
    seg = np.zeros((AW, 128), np.float32)
    seg[np.arange(AW), np.arange(AW) // HD] = 1.0
    return (cw8, jnp.tile(qg, (1, NH)), jnp.tile(kg, (1, NKV)), gco, gao, jnp.asarray(seg, BF), bias)


def _mix_fwd(proj, sinks, cw8, qg, kg, gco, gao, bias, comm=()):
    def body(sink_ref, p_ref, zc_ref, zh_ref, pkv_ref, cw_ref, qg_ref, kg_ref, gco_ref, gao_ref, seg_ref, bias_ref, y_ref):
        tile = p_ref[...]
        for b in range(BPS):
            f = _mix_forward(*_block_inputs(tile, b, zc_ref, zh_ref, pkv_ref, pl.program_id(0) == 0), cw_ref[...],
                             qg_ref[...], kg_ref[...], gco_ref[...], gao_ref[...], seg_ref[...], sink_ref, bias_ref)
            y_ref[b * BLK:(b + 1) * BLK, :] = f["y"].astype(BF)

    return _call(
        body, (sinks, proj, proj, proj, proj, *_mix_params(cw8, qg, kg, gco, gao, bias)), name="mix_fwd", grid=(T // TILE,),
        in_specs=_mix_in_specs(lambda s: s) + _mix_param_specs(),
        out_specs=[pl.BlockSpec((TILE, D), lambda s: (s, 0))], out_shape=[SDS((T, D), BF)],
        sem=("parallel",), vmem_mib=40, comm=comm)


def _mix_bwd(proj, dy, sinks, cw8, qg, kg, gco, gao, bias, comm=()):
    n_steps = T // TILE

    def tile_of(s):
        return n_steps - 1 - s

    def body(sink_ref, p_ref, zc_ref, zh_ref, pkv_ref, dy_ref, cw_ref, qg_ref, kg_ref, gco_ref, gao_ref, seg_ref, bias_ref,
             dproj_ref, dcw_ref, dqg_ref, dkg_ref, dgco_ref, dgao_ref, dsink_ref, dbias_ref,
             ndcz_ref, dkc_ref, dvc_ref):
        s = pl.program_id(0)

        @pl.when(s == 0)
        def _():
            for r in (dcw_ref, dqg_ref, dkg_ref, dgco_ref, dgao_ref, dsink_ref, dbias_ref, ndcz_ref, dkc_ref, dvc_ref):
                r[...] = jnp.zeros_like(r)

        tile = p_ref[...]
        for b in reversed(range(BPS)):
            one_block(b, _block_inputs(tile, b, zc_ref, zh_ref, pkv_ref, s == n_steps - 1),
                      dy_ref[b * BLK:(b + 1) * BLK, :], sink_ref, cw_ref, qg_ref, kg_ref, gco_ref, gao_ref, seg_ref, bias_ref,
                      dproj_ref.at[b * BLK:(b + 1) * BLK, :], dcw_ref, dqg_ref, dkg_ref, dgco_ref, dgao_ref, dsink_ref,
                      dbias_ref, ndcz_ref, dkc_ref, dvc_ref)

    def one_block(b, inputs, dy, sink_ref, cw_ref, qg_ref, kg_ref, gco_ref, gao_ref, seg_ref, bias_ref,
                  dproj_ref, dcw_ref, dqg_ref, dkg_ref, dgco_ref, dgao_ref, dsink_ref, dbias_ref,
                  ndcz_ref, dkc_ref, dvc_ref):
        cw = cw_ref[...]
        qg_v, kg_v, gco_v, gao_v, seg = qg_ref[...], kg_ref[...], gco_ref[...], gao_ref[...], seg_ref[...]
        f = _mix_forward(*inputs, cw, qg_v, kg_v, gco_v, gao_v, seg, sink_ref, bias_ref)
        dyc, dgco = _rms_bwd(dy[:, 0:CW], f["y_conv"], f["rc"], gco_v)
        dya, dgao = _rms_bwd(dy[:, CW:CW + AW], f["y_attn"], f["ra"], gao_v)
        dgco_ref[...] += dgco
        dgao_ref[...] += dgao

        row = f["row"]
        dgate_b = dyc * f["cz"]
        dcz = dyc * f["gate_b"]
        dcw_ref[0:1, :] += jnp.sum(dcz * f["z2"], axis=0, keepdims=True)
        dcw_ref[1:2, :] += jnp.sum(dcz * f["z1"], axis=0, keepdims=True)
        dcw_ref[2:3, :] += jnp.sum(dcz * f["z"], axis=0, keepdims=True)
        nxt = ndcz_ref[...]
        n0 = nxt[0:1, :]
        n1 = nxt[1:2, :]
        d1 = jnp.where(row == BLK - 1, n0, pltpu.roll(dcz, BLK - 1, 0))
        d2 = jnp.where(row == BLK - 1, n1, jnp.where(row == BLK - 2, n0, pltpu.roll(dcz, BLK - 2, 0)))
        dz = cw[2:3, :] * dcz + cw[1:2, :] * d1 + cw[0:1, :] * d2
        ndcz_ref[...] = dcz[0:8, :]
        dproj_ref[:, 0:CW] = dgate_b.astype(BF)
        dproj_ref[:, CW:2 * CW] = (dz * f["hc"]).astype(BF)
        dproj_ref[:, 2 * CW:3 * CW] = (dz * f["gate_c"]).astype(BF)

        scale = f["scale"]
        lane = lax.broadcasted_iota(jnp.int32, (1, 128), 1)
        dq_cols, dk_cols, dv_cols = [], [], []
        for kv in range(NKV):
            hd = f["heads"][kv]
            dO = jnp.concatenate([dya[:, (kv * GQ + g) * HD:(kv * GQ + g + 1) * HD] for g in range(GQ)], axis=0)
            delta = jnp.sum(dO * hd["O"], axis=-1, keepdims=True)
            dOb = dO.astype(BF)
            dP = _dot(dOb, hd["vb"], 1, 1)
            dS = hd["probs"] * (dP - delta)
            dsk = hd["psink"] * delta
            for g in range(GQ):
                h = kv * GQ + g
                tot = jnp.sum(dsk[g * BLK:(g + 1) * BLK, :], axis=0, keepdims=True)
                dsink_ref[...] -= jnp.where(lane == h, tot, 0.0)
            dbias_ref[kv * GQ * BLK:(kv + 1) * GQ * BLK, :] += dS
            dSb = dS.astype(BF)
            dQ = _dot(dSb, hd["kb"], 1, 0)
            dKb = _dot(dSb, hd["Q"], 0, 0)
            dVb = _dot(hd["probs"].astype(BF), dOb, 0, 0)
            dk_cols.append(dKb[BLK:, :] + dkc_ref[:, kv * HD:(kv + 1) * HD])
            dv_cols.append(dVb[BLK:, :] + dvc_ref[:, kv * HD:(kv + 1) * HD])
            dkc_ref[:, kv * HD:(kv + 1) * HD] = dKb[:BLK, :]
            dvc_ref[:, kv * HD:(kv + 1) * HD] = dVb[:BLK, :]
            dq_cols += [dQ[g * BLK:(g + 1) * BLK, :] for g in range(GQ)]
        dq_raw, dqg_t = _head_norm_bwd(jnp.concatenate(dq_cols, axis=1) * scale, f["q_raw"], f["rq"], qg_v, seg)
        dk_raw, dkg_t = _head_norm_bwd(jnp.concatenate(dk_cols, axis=1), f["k_raw"][BLK:, :], f["rk"][BLK:, :], kg_v, seg)
        dqg_ref[...] += _fold_heads(dqg_t)
        dkg_ref[...] += _fold_heads(dkg_t)
        dproj_ref[:, 3 * CW:INW] = jnp.concatenate([dq_raw, dk_raw] + dv_cols, axis=1).astype(BF)

    small = lambda r, c: pl.BlockSpec((r, c), lambda s: (0, 0))
    return _call(
        body, (sinks, proj, proj, proj, proj, dy, *_mix_params(cw8, qg, kg, gco, gao, bias)), name="mix_bwd", grid=(n_steps,),
        in_specs=_mix_in_specs(tile_of) + [pl.BlockSpec((TILE, D), lambda s: (tile_of(s), 0))] + _mix_param_specs(),
        out_specs=[pl.BlockSpec((TILE, INW), lambda s: (tile_of(s), 0)), small(8, CW), small(1, HD), small(1, HD),
                   small(1, CW), small(1, AW), small(1, 128), small(NH * BLK, 2 * BLK)],
        out_shape=[SDS((T, INW), BF), SDS((8, CW), F32), SDS((1, HD), F32), SDS((1, HD), F32), SDS((1, CW), F32),
                   SDS((1, AW), F32), SDS((1, 128), F32), SDS((NH * BLK, 2 * BLK), F32)],
        scratch_shapes=[pltpu.VMEM((8, CW), F32), pltpu.VMEM((BLK, NKV * HD), F32), pltpu.VMEM((BLK, NKV * HD), F32)],
        sem=("arbitrary",), vmem_mib=56, comm=comm)


FT = 256
NFT = DFF // FT
RC = 128
NCH = T // RC
LEAD = 16


def _rows8(x):
    return jnp.sum(x.reshape(x.shape[0] // 8, 8, x.shape[1]), axis=0)


def _ffn_act_specs():
    return [
        pl.BlockSpec((T, FT), lambda j: (0, j)), pl.BlockSpec((T, FT), lambda j: (0, NFT + j)),
        pl.BlockSpec((8, FT), lambda j: (0, j)), pl.BlockSpec((8, FT), lambda j: (0, NFT + j)),
        pl.BlockSpec((1, FT), lambda j: (0, j)), pl.BlockSpec((1, FT), lambda j: (0, NFT + j)),
    ]


def _conv_rows(win, w, b, n):
    win = win.astype(F32)
    u = win[LEAD:LEAD + n]
    u1 = pltpu.roll(win, 1, 0)[LEAD:LEAD + n]
    u2 = pltpu.roll(win, 2, 0)[LEAD:LEAD + n]
    return u2, u1, u, w[0:1, :] * u2 + w[1:2, :] * u1 + w[2:3, :] * u + b


def _ffn_act(up, fw8, fb):
    def body(ug_ref, uv_ref, wg_ref, wv_ref, bg_ref, bv_ref, a_ref):
        wg, wv, bg, bv = wg_ref[...], wv_ref[...], bg_ref[...], bv_ref[...]

        def chunk(win_g, win_v):
            gp = _conv_rows(win_g, wg, bg, RC)[3]
            vp = _conv_rows(win_v, wv, bv, RC)[3]
            return (gp * jax.nn.sigmoid(gp) * vp).astype(BF)

        zero = jnp.zeros((LEAD, FT), BF)
        a_ref[0:RC, :] = chunk(jnp.concatenate([zero, ug_ref[0:RC, :]], axis=0),
                               jnp.concatenate([zero, uv_ref[0:RC, :]], axis=0))

        def step(i, carry):
            r0 = pl.multiple_of(i * RC, RC)
            win = pl.ds(r0 - LEAD, RC + LEAD)
            a_ref[pl.ds(r0, RC), :] = chunk(ug_ref[win, :], uv_ref[win, :])
            return carry

        lax.fori_loop(1, NCH, step, 0)

    return _call(
        body, (up, up, fw8, fw8, fb, fb), name="ffn_act", grid=(NFT,), in_specs=_ffn_act_specs(),
        out_specs=[pl.BlockSpec((T, FT), lambda j: (0, j))], out_shape=[SDS((T, DFF), BF)],
        sem=("parallel",), vmem_mib=40)


def _ffn_act_bwd(up, da, fw8, fb, comm=()):
    ext = RC + LEAD

    def body(ug_ref, uv_ref, wg_ref, wv_ref, bg_ref, bv_ref, da_ref,
             dug_ref, duv_ref, dwg_ref, dwv_ref, dbg_ref, dbv_ref):
        wg, wv, bg, bv = wg_ref[...], wv_ref[...], bg_ref[...], bv_ref[...]

        def chunk(win_g, win_v, da_e):
            g2, g1, g0, gp = _conv_rows(win_g, wg, bg, ext)
            v2, v1, v0, vp = _conv_rows(win_v, wv, bv, ext)
            da_e = da_e.astype(F32)
            sig = jax.nn.sigmoid(gp)
            dvp = da_e * (gp * sig)
            dgp = da_e * vp * (sig * (1.0 + gp * (1.0 - sig)))

            def back(dp, w):
                return (w[2:3, :] * dp[0:RC] + w[1:2, :] * pltpu.roll(dp, ext - 1, 0)[0:RC]
                        + w[0:1, :] * pltpu.roll(dp, ext - 2, 0)[0:RC]).astype(BF)

            def sums(dp, u2, u1, u0):
                d = dp[0:RC]
                return [_rows8(d), _rows8(d * u2[0:RC]), _rows8(d * u1[0:RC]), _rows8(d * u0[0:RC])]

            return back(dgp, wg), back(dvp, wv), sums(dgp, g2, g1, g0) + sums(dvp, v2, v1, v0)

        zero = jnp.zeros((LEAD, FT), BF)
        dug, duv, acc = chunk(jnp.concatenate([zero, ug_ref[0:ext, :]], axis=0),
                              jnp.concatenate([zero, uv_ref[0:ext, :]], axis=0), da_ref[0:ext, :])
        dug_ref[0:RC, :] = dug
        duv_ref[0:RC, :] = duv

        def step(i, acc):
            r0 = pl.multiple_of(i * RC, RC)
            win = pl.ds(r0 - LEAD, ext + LEAD)
            dug, duv, part = chunk(ug_ref[win, :], uv_ref[win, :], da_ref[pl.ds(r0, ext), :])
            dug_ref[pl.ds(r0, RC), :] = dug
            duv_ref[pl.ds(r0, RC), :] = duv
            return [a + p for a, p in zip(acc, part)]

        acc = lax.fori_loop(1, NCH - 1, step, acc)
        r0 = T - RC
        tail = lambda ref, lo: jnp.concatenate([ref[lo:T, :], zero], axis=0)
        dug, duv, part = chunk(tail(ug_ref, r0 - LEAD), tail(uv_ref, r0 - LEAD), tail(da_ref, r0))
        dug_ref[r0:T, :] = dug
        duv_ref[r0:T, :] = duv
        tot = [jnp.sum(a + p, axis=0, keepdims=True) for a, p in zip(acc, part)]
        for k, (dw_ref, db_ref) in enumerate(((dwg_ref, dbg_ref), (dwv_ref, dbv_ref))):
            db_ref[...] = tot[4 * k]
            dw_ref[...] = jnp.zeros_like(dw_ref)
            for r in range(3):
                dw_ref[r:r + 1, :] = tot[4 * k + 1 + r]

    col = lambda r: pl.BlockSpec((r, FT), lambda j: (0, j))
    return _call(
        body, (up, up, fw8, fw8, fb, fb, da), name="ffn_act_bwd", grid=(NFT,),
        in_specs=_ffn_act_specs() + [pl.BlockSpec((T, FT), lambda j: (0, j))],
        out_specs=[col(T), col(T), col(8), col(8), col(1), col(1)],
        out_shape=[SDS((T, DFF), BF), SDS((T, DFF), BF), SDS((8, DFF), F32), SDS((8, DFF), F32),
                   SDS((1, DFF), F32), SDS((1, DFF), F32)],
        sem=("parallel",), vmem_mib=40, comm=comm)


def _ffn_down_bwd(dh2b, w_down, comm=()):
    tm = TM

    def body(d_ref, w_ref, o_ref):
        o_ref[...] = _dot(d_ref[...], w_ref[...], 1, 1).astype(BF)

    return _call(
        body, (dh2b, w_down), name="ffn_down_bwd", grid=(T // tm,),
        in_specs=[pl.BlockSpec((tm, D), lambda i: (i, 0)), _resident((DFF, D))],
        out_specs=[pl.BlockSpec((tm, DFF), lambda i: (i, 0))], out_shape=[SDS((T, DFF), BF)],
        sem=("parallel",), vmem_mib=40, comm=comm)


def _norm_matmul_bwd(name, a_list, w_t, k_offsets, xin, g, dres, want_bf16, comm=(), slot=None):
    tm = TM
    ks = [a.shape[1] for a in a_list]
    n_a = len(a_list)
    n_pre = 0 if slot is None else 1

    def body(*refs):
        refs = refs[n_pre:]
        a_refs = refs[:n_a]
        w_ref, x_ref, g_ref, r_ref = refs[n_a:n_a + 4]
        outs = refs[n_a + 4:]
        dx_ref, dg_ref = outs[0], (outs[-1] if slot is None else outs[-1].at[0])

        @pl.when(pl.program_id(0) == 0)
        def _():
            dg_ref[...] = jnp.zeros_like(dg_ref)

        du = _dot(a_refs[0][...], w_ref[k_offsets[0]:k_offsets[0] + ks[0], :], 1, 0)
        for k in range(1, n_a):
            du = du + _dot(a_refs[k][...], w_ref[k_offsets[k]:k_offsets[k] + ks[k], :], 1, 0)
        x = x_ref[...]
        r = lax.rsqrt(jnp.mean(x * x, axis=-1, keepdims=True) + EPS)
        dx, dg = _rms_bwd(du, x, r, g_ref[...])
        dx = r_ref[...] + dx
        dx_ref[...] = dx
        if want_bf16:
            outs[1][...] = dx.astype(BF)
        dg_ref[...] += dg

    tile = lambda c: pl.BlockSpec((tm, c), lambda i, *_: (i, 0))
    if slot is None:
        dg_spec, dg_shape = pl.BlockSpec((1, D), lambda i: (0, 0)), SDS((1, D), F32)
    else:
        dg_spec, dg_shape = pl.BlockSpec((1, 1, D), lambda i, slot_ref: (slot_ref[0], 0, 0)), SDS((N_DEV, 1, D), F32)
    out_specs = [tile(D)] + ([tile(D)] if want_bf16 else []) + [dg_spec]
    out_shape = [SDS((T, D), F32)] + ([SDS((T, D), BF)] if want_bf16 else []) + [dg_shape]
    return _call(
        body, (*a_list, w_t, xin, g, dres), name=name, grid=(T // tm,), prefetch=() if slot is None else (slot,),
        in_specs=[tile(k) for k in ks] + [_resident(w_t.shape), tile(D),
                                           pl.BlockSpec((1, D), lambda i, *_: (0, 0)), tile(D)],
        out_specs=out_specs, out_shape=out_shape, sem=("arbitrary",), vmem_mib=56, comm=comm)


def _out_bwd(dh1b, w_out, comm=()):
    tm = TM

    def body(d_ref, w_ref, o_ref):
        o_ref[...] = _dot(d_ref[...], w_ref[...], 1, 1)

    return _call(
        body, (dh1b, w_out), name="out_bwd", grid=(T // tm,),
        in_specs=[pl.BlockSpec((tm, D), lambda i: (i, 0)), _resident((D, D))],
        out_specs=[pl.BlockSpec((tm, D), lambda i: (i, 0))], out_shape=[SDS((T, D), F32)],
        sem=("parallel",), vmem_mib=32, comm=comm)


def _wgrad(name, a_list, b, comm=()):
    m_k = a_list[0].shape[1]
    tm = max(t for t in range(128, m_k // 2 + 1, 128) if m_k % t == 0)
    steps = [a.shape[1] // tm for a in a_list]
    starts = [sum(steps[:k]) for k in range(len(a_list))]
    n_a = len(a_list)

    def body(*refs):
        a_refs, b_ref, o_ref = refs[:n_a], refs[n_a], refs[n_a + 1]
        i = pl.program_id(0)
        for k in range(n_a):
            @pl.when((i >= starts[k]) & (i < starts[k] + steps[k]))
            def _(k=k):
                o_ref[...] = _dot(a_refs[k][...], b_ref[...], 0, 0).astype(BF)

    def a_spec(k):
        return pl.BlockSpec((T, tm), lambda i: (0, jnp.clip(i - starts[k], 0, steps[k] - 1)))

    m_total = tm * sum(steps)
    return _call(
        body, (*a_list, b), name=name, grid=(sum(steps),),
        in_specs=[a_spec(k) for k in range(n_a)] + [_resident((T, D))],
        out_specs=[pl.BlockSpec((tm, D), lambda i: (i, 0))], out_shape=[SDS((m_total, D), BF)],
        sem=("parallel",), vmem_mib=40, comm=comm)


def _chip_sum(name, gbf, from_sib, core, chip):
    h = gbf.shape[1]
    th = h // 2

    def body(core_ref, chip_ref, g_ref, s_ref, pbf_ref, own_ref):
        p = g_ref[0].astype(F32) + s_ref[0].astype(F32)
        pbf_ref[0] = p.astype(BF)

        @pl.when(pl.program_id(1) == chip_ref[0])
        def _():
            own_ref[...] = p

    grid_spec = pltpu.PrefetchScalarGridSpec(
        num_scalar_prefetch=2, grid=(h // th, N_CHIPS),
        in_specs=[pl.BlockSpec((1, th, D), lambda t, jj, core_ref, chip_ref: (2 * jj + core_ref[0], t, 0)),
                  pl.BlockSpec((1, th, D), lambda t, jj, core_ref, chip_ref: (jj, t, 0))],
        out_specs=[pl.BlockSpec((1, th, D), lambda t, jj, core_ref, chip_ref: (jj, t, 0)),
                   pl.BlockSpec((th, D), lambda t, jj, core_ref, chip_ref: (t, 0))],
    )
    return _pcall(
        body, name=name, grid_spec=grid_spec, out_shape=_in_hbm([SDS((N_CHIPS, h, D), BF), SDS((h, D), F32)]),
        compiler_params=_params(("arbitrary", "arbitrary"), 32),
    )(core, chip, *_from_hbm(gbf, from_sib))


def _final_sum(name, own, from_chips, core):
    h = own.shape[0]

    def body(core_ref, o_ref, r_ref, f_ref):
        f_ref[0] = ((o_ref[...] + r_ref[0].astype(F32)) + r_ref[1].astype(F32)) + r_ref[2].astype(F32)

    grid_spec = pltpu.PrefetchScalarGridSpec(
        num_scalar_prefetch=1, grid=(1,),
        in_specs=[pl.BlockSpec((h, D), lambda i, core_ref: (0, 0)), pl.BlockSpec((3, h, D), lambda i, core_ref: (0, 0, 0))],
        out_specs=pl.BlockSpec((1, h, D), lambda i, core_ref: (core_ref[0], 0, 0)),
    )
    return _pcall(body, name=name, grid_spec=grid_spec, out_shape=pltpu.HBM((2, h, D), F32),
                  compiler_params=_params(("arbitrary",), 40))(core, *_from_hbm(own, from_chips))


def _adam_math(w, g, m, v):
    nm = ADAM_B1 * m + (1.0 - ADAM_B1) * g
    nv = ADAM_B2 * v + (1.0 - ADAM_B2) * (g * g)
    m_hat = nm / (1.0 - ADAM_B1 ** ADAM_STEP)
    v_hat = nv / (1.0 - ADAM_B2 ** ADAM_STEP)
    return -ADAM_LR * (m_hat / (jnp.sqrt(v_hat) + ADAM_EPS) + ADAM_WD * w), nm, nv


def _adamw(name, w, g, m, v, tr, copy_g=False, stage=True):
    rows, cols = w.shape

    def body(w_ref, g_ref, m_ref, v_ref, *outs):
        g_val = g_ref[...]
        if copy_g:
            outs[0][...] = g_val
        d_ref, nm_ref, nv_ref = outs[-3:]
        d_ref[...], nm_ref[...], nv_ref[...] = _adam_math(w_ref[...], g_val, m_ref[...], v_ref[...])

    spec = pl.BlockSpec((tr, cols), lambda i: (i, 0))
    n_out = 4 if copy_g else 3
    return _call(body, (w, g, m, v), name=name, grid=(rows // tr,), in_specs=[spec] * 4, out_specs=[spec] * n_out,
                 out_shape=[SDS((rows, cols), F32)] * n_out, sem=("parallel",), vmem_mib=32,
                 free=(0, 2, 3) if stage else ())


C_G1, C_G2, C_GCO, C_GAO, C_DCW, C_DQG, C_DKG, C_SINK, C_SQ = 0, 1024, 2048, 2560, 3072, 4608, 4736, 4864, 5632
P_W = C_SQ + 128


def _pack_small(me, dfwg, dfwv, dfbg, dfbv, dg2, dgco, dgao, dcw8, dqg, dkg, dsink, sq):
    def body(me_ref, dfwg_r, dfwv_r, dfbg_r, dfbv_r, dg2_r, dgco_r, dgao_r, dcw_r, dqg_r, dkg_r, dsink_r, sq_r, o):
        o[...] = jnp.zeros_like(o)
        o[0, :, 0:DFF] = dfwg_r[...]
        o[0, :, DFF:2 * DFF] = dfwv_r[...]
        o[0, 3:4, 0:DFF] = dfbg_r[...]
        o[0, 3:4, DFF:2 * DFF] = dfbv_r[...]
        o[0, 4:5, C_G2:C_G2 + D] = dg2_r[...]
        o[0, 4:5, C_GCO:C_GCO + CW] = dgco_r[...]
        o[0, 4:5, C_GAO:C_GAO + AW] = dgao_r[...]
        for r in range(3):
            o[0, 4:5, C_DCW + r * CW:C_DCW + (r + 1) * CW] = dcw_r[r:r + 1, :]
        o[0, 4:5, C_DQG:C_DQG + HD] = dqg_r[...]
        o[0, 4:5, C_DKG:C_DKG + HD] = dkg_r[...]
        o[0, 4:5, C_SINK:C_SINK + 128] = dsink_r[...]
        o[0, :, C_SQ:C_SQ + 128] = sq_r[...]

    ins = (dfwg, dfwv, dfbg, dfbv, dg2, dgco, dgao, dcw8, dqg, dkg, dsink, sq)
    return _call(body, ins, name="pack_small", grid=(1,), prefetch=(me,),
                 in_specs=[pl.BlockSpec(a.shape, lambda i, me_ref: (0, 0)) for a in ins],
                 out_specs=[pl.BlockSpec((1, 8, P_W), lambda i, me_ref: (me_ref[0], 0, 0))],
                 out_shape=[SDS((N_DEV, 8, P_W), F32)], sem=("arbitrary",))[0]


N_SMALL = 11


def _small_adam(chip, p_all, g1_all, tbl_all, ws, ms, vs):
    fw_cols = 2 * DFF // N_CHIPS
    cw_cols = CW // N_CHIPS

    def body(chip_ref, p_ref, fw_ref, cw0_ref, cw1_ref, cw2_ref, g1_ref, tbl_ref, *refs):
        w_r, m_r, v_r = refs[0:N_SMALL], refs[N_SMALL:2 * N_SMALL], refs[2 * N_SMALL:3 * N_SMALL]
        outs = refs[3 * N_SMALL:]
        g_o, d_o, nm_o, nv_o = (outs[k * N_SMALL:(k + 1) * N_SMALL] for k in range(4))
        loss_o = outs[4 * N_SMALL]

        def total(ref):
            s = ref[0]
            for k in range(1, N_DEV):
                s = s + ref[k]
            return s

        S = total(p_ref)
        fw = total(fw_ref)
        cws = [total(r) for r in (cw0_ref, cw1_ref, cw2_ref)]

        def step(i, g, at):
            d, nm, nv = _adam_math(w_r[i][at], g, m_r[i][at], v_r[i][at])
            g_o[i][at], d_o[i][at], nm_o[i][at], nv_o[i][at] = g, d, nm, nv

        everything = (slice(None), slice(None))
        step(0, total(g1_ref), everything)
        for r in range(3):
            step(1, cws[r][4:5, :], (r, slice(None), slice(None)))
        step(2, S[4:5, C_DQG:C_DQG + HD], everything)
        step(3, S[4:5, C_DKG:C_DKG + HD], everything)
        step(4, total(tbl_ref), everything)
        step(5, S[4:5, C_SINK:C_SINK + NH], everything)
        step(6, S[4:5, C_GCO:C_GCO + CW], everything)
        step(7, S[4:5, C_GAO:C_GAO + AW], everything)
        step(8, S[4:5, C_G2:C_G2 + D], everything)
        for r in range(3):
            step(9, fw[r:r + 1, :], (r, slice(None), slice(None)))
        step(10, S[3:4, 0:2 * DFF], everything)
        sq = S[:, C_SQ:C_SQ + 128]
        loss_o[...] = jnp.sum(jnp.sum(sq, axis=1, keepdims=True), axis=0, keepdims=True) * (0.5 / D)

    def full(a):
        n = len(a.shape)
        return pl.BlockSpec(a.shape, lambda i, chip_ref: (0,) * n)

    params = [*ws, *ms, *vs]
    out = _call(
        body, (p_all, p_all, p_all, p_all, p_all, g1_all, tbl_all, *params), name="small_adam", grid=(1,), prefetch=(chip,),
        in_specs=[full(p_all),
                  pl.BlockSpec((N_DEV, 8, fw_cols), lambda i, chip_ref: (0, 0, chip_ref[0])),
                  *[pl.BlockSpec((N_DEV, 8, cw_cols), lambda i, chip_ref, r=r: (0, 0, (C_DCW + r * CW) // cw_cols + chip_ref[0]))
                    for r in range(3)],
                  full(g1_all), full(tbl_all), *[full(a) for a in params]],
        out_specs=[full(a) for a in ws] * 4 + [pl.BlockSpec((1, 1), lambda i, chip_ref: (0, 0))],
        out_shape=[SDS(a.shape, F32) for a in ws] * 4 + [SDS((1, 1), F32)], sem=("arbitrary",), vmem_mib=32)
    return out[0:N_SMALL], out[N_SMALL:2 * N_SMALL], out[2 * N_SMALL:3 * N_SMALL], out[3 * N_SMALL:4 * N_SMALL], out[4 * N_SMALL]


PLACE_STEPS = 4


def _place_specs(shards):
    rows = [s.shape[0] // PLACE_STEPS for s in shards]
    return ([pl.BlockSpec((r, D), lambda i, chip_ref: (i, 0)) for r in rows],
            [pl.BlockSpec((r, D), lambda i, chip_ref: (chip_ref[0] * PLACE_STEPS + i, 0)) for r in rows],
            [SDS((N_CHIPS * s.shape[0], D), BF) for s in shards])


def _place_first(chip, shard, conv_w, ffn_conv_w):
    def body(chip_ref, a, s0, s1, o, t0, t1):
        o[...] = a[...].astype(BF)

        @pl.when(pl.program_id(0) == 0)
        def _():
            for s, t in ((s0, t0), (s1, t1)):
                t[...] = jnp.zeros_like(t)
                t[0, 0:3, :] = s[...]

    ins, outs, shapes = _place_specs([shard])
    taps = (conv_w, ffn_conv_w)
    return _call(
        body, (shard, conv_w, ffn_conv_w), name="place_first", grid=(PLACE_STEPS,), prefetch=(chip,),
        in_specs=ins + [pl.BlockSpec(s.shape, lambda i, chip_ref: (0, 0)) for s in taps],
        out_specs=outs + [pl.BlockSpec((1, 8, s.shape[1]), lambda i, chip_ref: (chip_ref[0], 0, 0)) for s in taps],
        out_shape=shapes + [SDS((N_CHIPS, 8, s.shape[1]), F32) for s in taps],
        sem=("arbitrary",), vmem_mib=32, free=(0, 1, 2))


def _place_rest(chip, shards, table, bucket, comm):
    n = len(shards)

    def body(chip_ref, *refs):
        a, (tab_ref, bk_ref), o, bias_ref = refs[:n], refs[n:n + 2], refs[n + 2:2 * n + 2], refs[2 * n + 2]
        for src, dst in zip(a, o):
            dst[...] = src[...].astype(BF)

        @pl.when(pl.program_id(0) == 0)
        def _():
            bk = bk_ref[...]
            eq = [bk == b for b in range(NBUCKET)]
            for h in range(NH):
                acc = jnp.zeros((BLK, 2 * BLK), F32)
                for b in range(NBUCKET):
                    acc = jnp.where(eq[b], tab_ref[h, b], acc)
                bias_ref[h * BLK:(h + 1) * BLK, :] = acc

    ins, outs, shapes = _place_specs(shards)
    return _call(
        body, (*shards, table, bucket), name="place_rest", grid=(PLACE_STEPS,), prefetch=(chip,),
        in_specs=ins + [pl.BlockSpec(memory_space=pltpu.SMEM), pl.BlockSpec(bucket.shape, lambda i, chip_ref: (0, 0))],
        out_specs=outs + [pl.BlockSpec((NH * BLK, 2 * BLK), lambda i, chip_ref: (0, 0))],
        out_shape=shapes + [SDS((NH * BLK, 2 * BLK), F32)],
        sem=("arbitrary",), vmem_mib=32, comm=comm, free=tuple(range(n + 2)))


def kernel(x, norm_mix_g, w_in, conv_w, q_norm_g, k_norm_g, rel_bias_table, sinks, out_norm_conv_g, out_norm_attn_g, w_out, norm_ffn_g, w_up, ffn_conv_w, ffn_conv_b, w_down, loss_target, m_norm_mix_g, m_w_in, m_conv_w, m_q_norm_g, m_k_norm_g, m_rel_bias_table, m_sinks, m_out_norm_conv_g, m_out_norm_attn_g, m_w_out, m_norm_ffn_g, m_w_up, m_ffn_conv_w, m_ffn_conv_b, m_w_down, v_norm_mix_g, v_w_in, v_conv_w, v_q_norm_g, v_k_norm_g, v_rel_bias_table, v_sinks, v_out_norm_conv_g, v_out_norm_attn_g, v_w_out, v_norm_ffn_g, v_w_up, v_ffn_conv_w, v_ffn_conv_b, v_w_down):
    as_arg = lambda i: jnp.reshape(i, (1,)).astype(jnp.int32)
    chip = as_arg(2 * lax.axis_index("x") + lax.axis_index("y"))
    core = as_arg(lax.axis_index("c"))
    me = 2 * chip + core
    xs, tgt = x[0], loss_target[0]
    qg, kg, gco, gao, g1, g2, fb = q_norm_g, k_norm_g, out_norm_conv_g, out_norm_attn_g, norm_mix_g, norm_ffn_g, ffn_conv_b
    pieces = lambda g: g.reshape(N_DEV, g.shape[0] // N_DEV, D)
    whole = lambda f: f.reshape(2 * f.shape[1], D)

    bucket = jnp.asarray(_bucket_table())
    p_in, p_cw, p_fw = _place_first(chip, w_in[0].T, conv_w[0], ffn_conv_w[0])
    p_out, p_up, p_down, bias, w_int, cw_all, fw_all = _place_rest(
        chip, [w_out[0], w_up[0].T, w_down[0]], rel_bias_table.T, bucket,
        comm=[_t_gather(p_in), _t_small_weights(p_cw), _t_small_weights(p_fw)])
    cw8 = jnp.transpose(cw_all, (1, 0, 2)).reshape(8, CW)
    fw8 = jnp.transpose(fw_all, (1, 0, 2)).reshape(8, 2 * DFF)

    proj, u1, w_out_f = _inproj(xs, g1, w_int, comm=[_t_gather(p_out)])
    y, w_upt = _mix_fwd(proj, sinks, cw8, qg, kg, gco, gao, bias, comm=[_t_gather(p_up)])
    h1, u2 = _outproj(y, w_out_f, xs, g2)
    up, w_down_f = _ffn_up(u2, w_upt, comm=[_t_gather(p_down)])
    a, = _ffn_act(up, fw8, fb)
    dh2, dh2b, sq = _ffn_down(a, w_down_f, h1, tgt)

    gdbf, = _wgrad("wgrad_down", [a], dh2b)
    da, sib_down = _ffn_down_bwd(dh2b, w_down_f, comm=[_t_sibling(pieces(gdbf))])
    pbf_down, own_down = _chip_sum("chip_sum_w_down", pieces(gdbf), sib_down, core, chip)
    dug, duv, dfwg, dfwv, dfbg, dfbv, chips_down = _ffn_act_bwd(up, da, fw8, fb, comm=[_t_chips(pbf_down)])
    fin_down = _final_sum("final_sum_w_down", own_down, chips_down, core)
    gubf, = _wgrad("wgrad_up", [dug, duv], u2)
    dh1, dh1b, dg2, sib_up, fin_down = _norm_matmul_bwd(
        "ffn_up_bwd", [dug, duv], w_upt, [0, DFF], h1, g2, dh2, True, comm=[_t_sibling(pieces(gubf)), _t_swap(fin_down)])
    pbf_up, own_up = _chip_sum("chip_sum_w_up", pieces(gubf), sib_up, core, chip)
    gobf, = _wgrad("wgrad_out", [y], dh1b)
    dy, sib_out = _out_bwd(dh1b, w_out_f, comm=[_t_sibling(pieces(gobf))])
    pbf_out, own_out = _chip_sum("chip_sum_w_out", pieces(gobf), sib_out, core, chip)
    dproj, dcw8, dqg, dkg, dgco, dgao, dsink, dbias, chips_up, chips_out = _mix_bwd(
        proj, dy, sinks, cw8, qg, kg, gco, gao, bias, comm=[_t_chips(pbf_up), _t_chips(pbf_out)])
    fin_up = _final_sum("final_sum_w_up", own_up, chips_up, core)
    fin_out = _final_sum("final_sum_w_out", own_out, chips_out, core)
    tbl_all = _band_bias_bwd(dbias, bucket, me)
    p_all = _pack_small(me, dfwg, dfwv, dfbg, dfbv, dg2, dgco, dgao, dcw8, dqg, dkg, dsink, sq)
    gibf, fin_up, p_all, tbl_all = _wgrad(
        "wgrad_in", [dproj], u1, comm=[_t_swap(fin_up), _t_allgather(p_all), _t_allgather(tbl_all)])
    sib_in, fin_out = _comm_call("to_sibling_last", [_t_sibling(pieces(gibf)), _t_swap(fin_out)])
    pbf_in, own_in = _chip_sum("chip_sum_w_in", pieces(gibf), sib_in, core, chip)
    dx, g1_all, chips_in = _norm_matmul_bwd(
        "in_bwd", [dproj], w_int, [0], xs, g1, dh1, False, comm=[_t_chips(pbf_in)], slot=me)
    fin_in = _final_sum("final_sum_w_in", own_in, chips_in, core)
    g1_all, fin_in = _comm_call("gather_last", [_t_allgather(g1_all), _t_swap(fin_in)])

    g_w_out, g_w_up, g_w_down = whole(fin_out), whole(fin_up).T, whole(fin_down)
    g_w_down, d_down, nm_down, nv_down = _adamw("adamw_w_down", w_down[0], g_w_down, m_w_down[0], v_w_down[0], 352, True)
    d_up, nm_up, nv_up = _adamw("adamw_w_up", w_up[0], g_w_up, m_w_up[0], v_w_up[0], 256, stage=False)
    g_w_out, d_out, nm_out, nv_out = _adamw("adamw_w_out", w_out[0], g_w_out, m_w_out[0], v_w_out[0], 256, True)
    g_w_in, d_in, nm_in, nv_in = [a.T for a in _adamw(
        "adamw_w_in", w_in[0].T, whole(fin_in), m_w_in[0].T, v_w_in[0].T, INW // N_CHIPS // 3, True)]
    taps = lambda a: jnp.transpose(a, (1, 0, 2))
    sw = [norm_mix_g, taps(conv_w), q_norm_g, k_norm_g, rel_bias_table.T, sinks, out_norm_conv_g, out_norm_attn_g,
          norm_ffn_g, taps(ffn_conv_w), ffn_conv_b]
    smm = [m_norm_mix_g, taps(m_conv_w), m_q_norm_g, m_k_norm_g, m_rel_bias_table.T, m_sinks, m_out_norm_conv_g,
           m_out_norm_attn_g, m_norm_ffn_g, taps(m_ffn_conv_w), m_ffn_conv_b]
    smv = [v_norm_mix_g, taps(v_conv_w), v_q_norm_g, v_k_norm_g, v_rel_bias_table.T, v_sinks, v_out_norm_conv_g,
           v_out_norm_attn_g, v_norm_ffn_g, taps(v_ffn_conv_w), v_ffn_conv_b]
    *small_out, loss = _small_adam(chip, p_all, g1_all, tbl_all, sw, smm, smv)
    sg, sd, snm, snv = [list(r) for r in small_out]
    for r in (sg, sd, snm, snv):
        r[1], r[4], r[9] = taps(r[1]), r[4].T, taps(r[9])

    def order(s, b_in, b_out, b_up, b_down):
        return (s[0], b_in[None], s[1], s[2], s[3], s[4], s[5], s[6], s[7], b_out[None], s[8], b_up[None],
                s[9], s[10], b_down[None])

    return (loss.reshape(()), dx[None],
            *order(sg, g_w_in, g_w_out, g_w_up, g_w_down),
            *order(sd, d_in, d_out, d_up, d_down),
            *order(snm, nm_in, nm_out, nm_up, nm_down),
            *order(snv, nv_in, nv_out, nv_up, nv_down))
```

```python
import functools
import math

import numpy as np

import jax
import jax.numpy as jnp
from jax import lax
from jax.experimental import pallas as pl
from jax.experimental.pallas import tpu as pltpu

F32 = jnp.float32
BF = jnp.bfloat16
SDS = jax.ShapeDtypeStruct

T = 2048
D = 1024
CW = 512
AW = 512
HD = 64
NH = 8
NKV = 2
GQ = 4
INW = 2304
DFF = 2816
BLK = 128
NB = T // BLK
NBUCKET = 32
EPS = 1e-6
NEG_INF = -1e30
N_CHIPS = 4
N_DEV = 8

ADAM_LR = 0.001
ADAM_B1 = 0.9
ADAM_B2 = 0.999
ADAM_EPS = 1e-08
ADAM_WD = 0.01
ADAM_STEP = 10

TM = 512
MIB = 1024 * 1024
MESH = pl.DeviceIdType.MESH
ANY = pl.BlockSpec(memory_space=pl.ANY)

_pcall = pl.pallas_call


def _params(sem=None, vmem_mib=None):
    kw = {}
    if sem is not None:
        kw["dimension_semantics"] = sem
    if vmem_mib is not None:
        kw["vmem_limit_bytes"] = vmem_mib * MIB
    return pltpu.CompilerParams(**kw)


def _resident(shape):
    return pl.BlockSpec(shape, lambda *_: (0,) * len(shape), pipeline_mode=pl.Buffered(1))


def _dot(a, b, ca, cb):
    return lax.dot_general(a, b, (((ca,), (cb,)), ((), ())), preferred_element_type=F32)


def _rms_bwd(dy, x, r, g):
    dg = jnp.sum(dy * (x * r), axis=0, keepdims=True)
    dgx = dy * g
    dx = r * dgx - x * (r * r * r) * jnp.mean(x * dgx, axis=-1, keepdims=True)
    return dx, dg


def _where():
    x, y, c = lax.axis_index("x"), lax.axis_index("y"), lax.axis_index("c")
    return x, y, c, [(1 - x, y), (x, 1 - y), (1 - x, 1 - y)]


def _rcopy(src, dst, ssem, rsem, dev):
    return pltpu.make_async_remote_copy(src_ref=src, dst_ref=dst, send_sem=ssem, recv_sem=rsem, device_id=dev,
                                        device_id_type=MESH)


class _Task:
    def __init__(self, ins, outs, alias, n_sem, start, finish, middle=None):
        self.ins, self.outs, self.alias, self.n_sem, self.start, self.finish = ins, outs, alias, n_sem, start, finish
        self.middle = middle if middle is not None else (lambda *args: None)


def _t_gather(placed):
    R = placed.shape[0] // N_CHIPS
    q = R // 4

    def quarter(chip_index, core, k):
        return pl.ds(pl.multiple_of(chip_index * R + core * 2 * q + k * q, 16), q)

    def half(chip_index, core):
        return pl.ds(pl.multiple_of(chip_index * R + core * 2 * q, 16), 2 * q)

    def places():
        x, y, c, _ = _where()
        return c, 2 * x + y, 2 * (1 - x) + y, 2 * x + (1 - y), 2 * (1 - x) + (1 - y), (1 - x, y, c), (x, 1 - y, c), (x, y, 1 - c)

    def start(cin, cout, ss, rs, b):
        c, me, _, _, _, x_nbr, y_nbr, _ = places()
        buf = cout[0]
        for k, (quart, dev) in enumerate(((0, x_nbr), (1, y_nbr), (1, x_nbr), (0, y_nbr))):
            part = buf.at[quarter(me, c, quart)]
            _rcopy(part, part, ss.at[b + k], rs.at[b + k], dev).start()

    def middle(cin, cout, ss, rs, b):
        c, _, xc, yc, _, x_nbr, y_nbr, _ = places()
        buf = cout[0]
        for k, chip_index, quart, dev in ((0, xc, 0, y_nbr), (1, yc, 1, x_nbr)):
            part = buf.at[quarter(chip_index, c, quart)]
            _rcopy(part, part, ss.at[b + k], rs.at[b + k], dev).wait_recv()
            _rcopy(part, part, ss.at[b + 4 + k], rs.at[b + 4 + k], dev).start()

    def finish(cin, cout, ss, rs, b):
        c, me, xc, yc, dc, x_nbr, y_nbr, sib = places()
        buf = cout[0]
        for k, chip_index, quart in ((2, xc, 1), (3, yc, 0), (4, dc, 0), (5, dc, 1)):
            part = buf.at[quarter(chip_index, c, quart)]
            _rcopy(part, part, ss.at[b + k], rs.at[b + k], sib).wait_recv()
        for k, chip_index in ((6, xc), (7, yc), (8, dc)):
            got = buf.at[half(chip_index, c)]
            _rcopy(got, got, ss.at[b + k], rs.at[b + k], sib).start()
        for k, chip_index in ((6, xc), (7, yc), (8, dc)):
            got = buf.at[half(chip_index, 1 - c)]
            _rcopy(got, got, ss.at[b + k], rs.at[b + k], sib).wait_recv()
        for k in range(6):
            part = buf.at[quarter(me, c, 0)]
            _rcopy(part, part, ss.at[b + k], rs.at[b + k], sib).wait_send()
        for k in range(6, 9):
            got = buf.at[half(me, c)]
            _rcopy(got, got, ss.at[b + k], rs.at[b + k], sib).wait_send()

    return _Task([placed], [SDS(placed.shape, placed.dtype)], [(0, 0)], 9, start, finish, middle)


def _t_small_weights(buf):
    def start(cin, cout, ss, rs, b):
        x, y, c, chips = _where()
        mine = cout[0].at[2 * x + y]
        for r, (px, py) in enumerate(chips):
            _rcopy(mine, mine, ss.at[b + r], rs.at[b + r], (px, py, c)).start()

    def finish(cin, cout, ss, rs, b):
        x, y, c, chips = _where()
        for r, (px, py) in enumerate(chips):
            got = cout[0].at[2 * px + py]
            _rcopy(got, got, ss.at[b + r], rs.at[b + r], (px, py, c)).wait_recv()
        for r, (px, py) in enumerate(chips):
            mine = cout[0].at[2 * x + y]
            _rcopy(mine, mine, ss.at[b + r], rs.at[b + r], (px, py, c)).wait_send()

    return _Task([buf], [SDS(buf.shape, buf.dtype)], [(0, 0)], 3, start, finish)


def _t_sibling(gbf):
    def start(cin, cout, ss, rs, b):
        x, y, c, _ = _where()
        for jj in range(N_CHIPS):
            _rcopy(cin[0].at[2 * jj + (1 - c)], cout[0].at[jj], ss.at[b + jj], rs.at[b + jj], (x, y, 1 - c)).start()

    def finish(cin, cout, ss, rs, b):
        x, y, c, _ = _where()
        for jj in range(N_CHIPS):
            got = cout[0].at[jj]
            _rcopy(got, got, ss.at[b + jj], rs.at[b + jj], (x, y, 1 - c)).wait_recv()
        for jj in range(N_CHIPS):
            got = cout[0].at[jj]
            _rcopy(got, got, ss.at[b + jj], rs.at[b + jj], (x, y, 1 - c)).wait_send()

    return _Task([gbf], [SDS((N_CHIPS,) + gbf.shape[1:], BF)], [], N_CHIPS, start, finish)


def _t_chips(pbf):
    h = pbf.shape[1]
    halves = (pl.ds(0, h // 2), pl.ds(h // 2, h // 2))

    def places():
        x, y, c, _ = _where()
        return 2 * (1 - x) + y, 2 * x + (1 - y), 2 * (1 - x) + (1 - y), (1 - x, y, c), (x, 1 - y, c)

    def start(cin, cout, ss, rs, b):
        xc, yc, dc, x_nbr, y_nbr = places()
        pbf_ref, got, relay = cin[0], cout[0], cout[1]
        _rcopy(pbf_ref.at[dc, halves[0]], relay.at[0], ss.at[b + 2], rs.at[b + 2], x_nbr).start()
        _rcopy(pbf_ref.at[dc, halves[1]], relay.at[1], ss.at[b + 3], rs.at[b + 3], y_nbr).start()
        _rcopy(pbf_ref.at[xc], got.at[0], ss.at[b + 0], rs.at[b + 0], x_nbr).start()
        _rcopy(pbf_ref.at[yc], got.at[1], ss.at[b + 1], rs.at[b + 1], y_nbr).start()

    def middle(cin, cout, ss, rs, b):
        _, _, _, x_nbr, y_nbr = places()
        got, relay = cout[0], cout[1]
        for k, dev in ((0, y_nbr), (1, x_nbr)):
            _rcopy(relay.at[k], relay.at[k], ss.at[b + 2 + k], rs.at[b + 2 + k], dev).wait_recv()
            _rcopy(relay.at[k], got.at[2, halves[k]], ss.at[b + 4 + k], rs.at[b + 4 + k], dev).start()

    def finish(cin, cout, ss, rs, b):
        _, _, _, x_nbr, _ = places()
        got, relay = cout[0], cout[1]
        for k in (0, 1):
            _rcopy(got.at[k], got.at[k], ss.at[b + k], rs.at[b + k], x_nbr).wait_recv()
            _rcopy(relay.at[k], got.at[2, halves[k]], ss.at[b + 4 + k], rs.at[b + 4 + k], x_nbr).wait_recv()
        for k in (0, 1):
            _rcopy(got.at[k], got.at[k], ss.at[b + k], rs.at[b + k], x_nbr).wait_send()
        for k in (2, 3, 4, 5):
            _rcopy(relay.at[0], relay.at[0], ss.at[b + k], rs.at[b + k], x_nbr).wait_send()

    return _Task([pbf], [SDS((3,) + pbf.shape[1:], BF), SDS((2, h // 2) + pbf.shape[2:], BF)], [], 6, start, finish, middle)


def _t_swap(fin):
    def start(cin, cout, ss, rs, b):
        x, y, c, _ = _where()
        mine = cout[0].at[c]
        _rcopy(mine, mine, ss.at[b], rs.at[b], (x, y, 1 - c)).start()

    def finish(cin, cout, ss, rs, b):
        x, y, c, _ = _where()
        got = cout[0].at[1 - c]
        _rcopy(got, got, ss.at[b], rs.at[b], (x, y, 1 - c)).wait_recv()
        _rcopy(got, got, ss.at[b], rs.at[b], (x, y, 1 - c)).wait_send()

    return _Task([fin], [SDS(fin.shape, fin.dtype)], [(0, 0)], 1, start, finish)


def _t_allgather(buf):
    def peers():
        x, y, c, _ = _where()
        out = []
        for rel in range(1, N_DEV):
            px, py, pc = x ^ ((rel >> 2) & 1), y ^ ((rel >> 1) & 1), c ^ (rel & 1)
            out.append((rel - 1, 4 * px + 2 * py + pc, (px, py, pc)))
        return 4 * x + 2 * y + c, out

    def start(cin, cout, ss, rs, b):
        me, ps = peers()
        mine = cout[0].at[me]
        for k, _, dev in ps:
            _rcopy(mine, mine, ss.at[b + k], rs.at[b + k], dev).start()

    def finish(cin, cout, ss, rs, b):
        me, ps = peers()
        for k, pidx, dev in ps:
            got = cout[0].at[pidx]
            _rcopy(got, got, ss.at[b + k], rs.at[b + k], dev).wait_recv()
        for k, _, dev in ps:
            mine = cout[0].at[me]
            _rcopy(mine, mine, ss.at[b + k], rs.at[b + k], dev).wait_send()

    return _Task([buf], [SDS(buf.shape, buf.dtype)], [(0, 0)], N_DEV - 1, start, finish)


def _run_tasks(comm, which, cin, cout, ss, rs):
    i0 = o0 = s0 = 0
    for t in comm:
        getattr(t, which)(cin[i0:i0 + len(t.ins)], cout[o0:o0 + len(t.outs)], ss, rs, s0)
        i0, o0, s0 = i0 + len(t.ins), o0 + len(t.outs), s0 + t.n_sem


def _from_hbm(*arrays):
    return [pltpu.with_memory_space_constraint(a, pltpu.HBM) for a in arrays]


def _in_hbm(shapes):
    return [pltpu.HBM(s.shape, s.dtype) for s in shapes]


def _comm_layout(comm, n_in, n_out):
    c_in = [a for t in comm for a in t.ins]
    c_out = [s for t in comm for s in t.outs]
    aliases, i0, o0 = {}, 0, 0
    for t in comm:
        for i, o in t.alias:
            aliases[n_in + i0 + i] = n_out + o0 + o
        i0, o0 = i0 + len(t.ins), o0 + len(t.outs)
    return c_in, c_out, aliases, sum(t.n_sem for t in comm)


def _call(body, operands, *, name, grid, in_specs, out_specs, out_shape, scratch_shapes=(), sem=None, vmem_mib=None, comm=(),
          free=(), prefetch=()):
    operands = [o if s.memory_space == pltpu.SMEM or k in free else pltpu.with_memory_space_constraint(o, pltpu.HBM)
                for k, (o, s) in enumerate(zip(operands, in_specs))]
    n_pre, n_in, n_out, n_scr = len(prefetch), len(in_specs), len(out_specs), len(scratch_shapes)
    c_in, c_out, aliases, n_sem = _comm_layout(comm, n_pre + n_in, n_out)
    sems = [pltpu.SemaphoreType.DMA((n_sem,)), pltpu.SemaphoreType.DMA((n_sem,))] if comm else []

    def wrapped(*refs):
        pre, refs = refs[:n_pre], refs[n_pre:]
        ins, cin = refs[:n_in], refs[n_in:n_in + len(c_in)]
        rest = refs[n_in + len(c_in):]
        outs, cout = rest[:n_out], rest[n_out:n_out + len(c_out)]
        rest = rest[n_out + len(c_out):]
        scr, csem = rest[:n_scr], rest[n_scr:]
        if not comm:
            return body(*pre, *ins, *outs, *scr)
        step = functools.reduce(lambda acc, k: acc * grid[k] + pl.program_id(k), range(len(grid)), 0)
        n_steps = math.prod(grid)
        pl.when(step == 0)(lambda: _run_tasks(comm, "start", cin, cout, *csem))
        pl.when(step == n_steps // 2)(lambda: _run_tasks(comm, "middle", cin, cout, *csem))
        body(*pre, *ins, *outs, *scr)
        pl.when(step == n_steps - 1)(lambda: _run_tasks(comm, "finish", cin, cout, *csem))

    grid_spec = pltpu.PrefetchScalarGridSpec(
        num_scalar_prefetch=n_pre, grid=grid, in_specs=list(in_specs) + [ANY] * len(c_in),
        out_specs=list(out_specs) + [ANY] * len(c_out), scratch_shapes=list(scratch_shapes) + sems)
    return _pcall(
        wrapped, name=name, grid_spec=grid_spec, out_shape=_in_hbm(list(out_shape) + c_out), input_output_aliases=aliases,
        compiler_params=_params(("arbitrary",) * len(grid) if comm else sem, vmem_mib),
    )(*prefetch, *operands, *_from_hbm(*c_in))


def _comm_call(name, comm):
    c_in, c_out, aliases, n_sem = _comm_layout(comm, 0, 0)

    def body(*refs):
        cin, cout, (ss, rs) = refs[:len(c_in)], refs[len(c_in):len(c_in) + len(c_out)], refs[len(c_in) + len(c_out):]
        for phase in ("start", "middle", "finish"):
            _run_tasks(comm, phase, cin, cout, ss, rs)

    return _pcall(
        body, name=name, in_specs=[ANY] * len(c_in), out_specs=[ANY] * len(c_out), out_shape=_in_hbm(c_out),
        scratch_shapes=[pltpu.SemaphoreType.DMA((n_sem,)), pltpu.SemaphoreType.DMA((n_sem,))],
        input_output_aliases=aliases,
    )(*_from_hbm(*c_in))


def _inproj(x, g1, w_int, comm=()):
    tm = TM

    def body(x_ref, g_ref, w_ref, proj_ref, u_ref):
        xf = x_ref[...]
        r = lax.rsqrt(jnp.mean(xf * xf, axis=-1, keepdims=True) + EPS)
        u = (xf * r * g_ref[...]).astype(BF)
        u_ref[...] = u
        proj_ref[...] = _dot(u, w_ref[...], 1, 1)

    return _call(
        body, (x, g1, w_int), name="inproj", grid=(T // tm,),
        in_specs=[pl.BlockSpec((tm, D), lambda i: (i, 0)), pl.BlockSpec((1, D), lambda i: (0, 0)),
                  _resident((INW, D))],
        out_specs=[pl.BlockSpec((tm, INW), lambda i: (i, 0)), pl.BlockSpec((tm, D), lambda i: (i, 0))],
        out_shape=[SDS((T, INW), F32), SDS((T, D), BF)], sem=("parallel",), vmem_mib=40, comm=comm)


def _outproj(y, w_out, x, g2):
    tm = TM

    def body(y_ref, w_ref, x_ref, g_ref, h1_ref, u2_ref):
        h1 = x_ref[...] + _dot(y_ref[...], w_ref[...], 1, 0)
        h1_ref[...] = h1
        r = lax.rsqrt(jnp.mean(h1 * h1, axis=-1, keepdims=True) + EPS)
        u2_ref[...] = (h1 * r * g_ref[...]).astype(BF)

    return _call(
        body, (y, w_out, x, g2), name="outproj", grid=(T // tm,),
        in_specs=[pl.BlockSpec((tm, D), lambda i: (i, 0)), _resident((D, D)),
                  pl.BlockSpec((tm, D), lambda i: (i, 0)), pl.BlockSpec((1, D), lambda i: (0, 0))],
        out_specs=[pl.BlockSpec((tm, D), lambda i: (i, 0)), pl.BlockSpec((tm, D), lambda i: (i, 0))],
        out_shape=[SDS((T, D), F32), SDS((T, D), BF)], sem=("parallel",), vmem_mib=32)


def _ffn_up(u2, w_upt, comm=()):
    tm, tn = 1024, 512

    def body(u_ref, w_ref, o_ref):
        o_ref[...] = _dot(u_ref[...], w_ref[...], 1, 1).astype(BF)

    return _call(
        body, (u2, w_upt), name="ffn_up", grid=(T // tm, 2 * DFF // tn),
        in_specs=[pl.BlockSpec((tm, D), lambda i, j: (i, 0)), pl.BlockSpec((tn, D), lambda i, j: (j, 0))],
        out_specs=[pl.BlockSpec((tm, tn), lambda i, j: (i, j))], out_shape=[SDS((T, 2 * DFF), BF)],
        sem=("parallel", "parallel"), vmem_mib=32, comm=comm)


def _ffn_down(a, w_down, h1, tgt):
    tm = TM

    def body(a_ref, w_ref, h1_ref, t_ref, dh_ref, dhb_ref, l_ref):
        @pl.when(pl.program_id(0) == 0)
        def _():
            l_ref[...] = jnp.zeros_like(l_ref)

        h2 = h1_ref[...] + _dot(a_ref[...], w_ref[...], 1, 0)
        e = h2 - t_ref[...]
        dh = e * (1.0 / D)
        dh_ref[...] = dh
        dhb_ref[...] = dh.astype(BF)
        e2 = jnp.sum((e * e).reshape(tm // 8, 8, D), axis=0)
        acc = e2[:, 0:128]
        for k in range(1, D // 128):
            acc = acc + e2[:, k * 128:(k + 1) * 128]
        l_ref[...] += acc

    return _call(
        body, (a, w_down, h1, tgt), name="ffn_down", grid=(T // tm,),
        in_specs=[pl.BlockSpec((tm, DFF), lambda i: (i, 0)), _resident((DFF, D)),
                  pl.BlockSpec((tm, D), lambda i: (i, 0)), pl.BlockSpec((tm, D), lambda i: (i, 0))],
        out_specs=[pl.BlockSpec((tm, D), lambda i: (i, 0)), pl.BlockSpec((tm, D), lambda i: (i, 0)),
                   pl.BlockSpec((8, 128), lambda i: (0, 0))],
        out_shape=[SDS((T, D), F32), SDS((T, D), BF), SDS((8, 128), F32)], sem=("arbitrary",), vmem_mib=40)


def _bucket_table():
    q = np.arange(BLK, dtype=np.int32)[:, None]
    j = np.arange(2 * BLK, dtype=np.int32)[None, :]
    n = np.maximum(q + BLK - j, 0)
    nf = np.maximum(n, 1).astype(np.float32)
    max_exact = NBUCKET // 2
    large = max_exact + (np.log(nf / np.float32(max_exact)) / np.float32(math.log(BLK / max_exact))
                         * np.float32(NBUCKET - max_exact)).astype(np.int32)
    large = np.minimum(large, NBUCKET - 1)
    return np.where(n < max_exact, n, large).astype(np.int32)


def _band_bias_bwd(dbias, bucket, me):
    def body(me_ref, db_ref, bk_ref, o_ref):
        bk = bk_ref[...]
        for b in range(NBUCKET):
            m = bk == b
            for h in range(NH):
                v = jnp.where(m, db_ref[h * BLK:(h + 1) * BLK, :], 0.0)
                s = jnp.sum(jnp.sum(v, axis=1, keepdims=True), axis=0, keepdims=True)
                o_ref[0, h:h + 1, b:b + 1] = s

    grid_spec = pltpu.PrefetchScalarGridSpec(
        num_scalar_prefetch=1, grid=(1,),
        in_specs=[pl.BlockSpec((NH * BLK, 2 * BLK), lambda i, me_ref: (0, 0)),
                  pl.BlockSpec((BLK, 2 * BLK), lambda i, me_ref: (0, 0))],
        out_specs=pl.BlockSpec((1, NH, NBUCKET), lambda i, me_ref: (me_ref[0], 0, 0)),
    )
    return _pcall(body, name="band_bias_bwd", grid_spec=grid_spec, out_shape=SDS((N_DEV, NH, NBUCKET), F32),
                  compiler_params=_params(("arbitrary",)))(me, dbias, bucket)


def _two_bf16(x):
    hi = x.astype(BF)
    return hi, (x - hi.astype(F32)).astype(BF)


def _head_sums(x, seg):
    hi, lo = _two_bf16(x)
    s = seg[0:x.shape[1], :]
    return _dot(hi, s, 1, 0) + _dot(lo, s, 1, 0)


def _head_spread(v, seg, width):
    hi, lo = _two_bf16(v)
    s = seg[0:width, :]
    return _dot(hi, s, 1, 1) + _dot(lo, s, 1, 1)


def _head_norm(x, g_t, seg):
    r = lax.rsqrt(_head_sums(x * x, seg) * (1.0 / HD) + EPS)
    r = _head_spread(r, seg, x.shape[1])
    return x * r * g_t, r


def _head_norm_bwd(dy, x, r, g_t, seg):
    dg_t = jnp.sum(dy * (x * r), axis=0, keepdims=True)
    dgx = dy * g_t
    mean = _head_spread(_head_sums(x * dgx, seg) * (1.0 / HD), seg, x.shape[1])
    return r * dgx - x * (r * r * r) * mean, dg_t


def _fold_heads(v):
    out = v[:, 0:HD]
    for h in range(1, v.shape[1] // HD):
        out = out + v[:, h * HD:(h + 1) * HD]
    return out


def _mix_forward(P, zc8, zh8, pkv, first, cw, qg_t, kg_t, gco, gao, seg, sink_ref, bias_ref):
    gate_b = P[:, 0:CW]
    gate_c = P[:, CW:2 * CW]
    hc = P[:, 2 * CW:3 * CW]
    z = gate_c * hc
    keep = jnp.where(first, 0.0, 1.0)
    zp = zc8 * zh8 * keep
    p1 = zp[7:8, :]
    p2 = zp[6:7, :]
    row = lax.broadcasted_iota(jnp.int32, (BLK, 1), 0)
    z1 = jnp.where(row == 0, p1, pltpu.roll(z, 1, 0))
    z2 = jnp.where(row == 0, p2, jnp.where(row == 1, p1, pltpu.roll(z, 2, 0)))
    cz = cw[0:1, :] * z2 + cw[1:2, :] * z1 + cw[2:3, :] * z
    y_conv = gate_b * cz

    scale = HD ** -0.5
    qi = lax.broadcasted_iota(jnp.int32, (GQ * BLK, 2 * BLK), 0) & (BLK - 1)
    kj = lax.broadcasted_iota(jnp.int32, (GQ * BLK, 2 * BLK), 1)
    dd = qi + BLK - kj
    first_key = jnp.where(first, BLK, 0)
    valid = (dd >= 0) & (dd < BLK) & (kj >= first_key)

    q0 = 3 * CW
    k0 = q0 + AW
    v0 = k0 + NKV * HD
    q_raw = P[:, q0:k0]
    qn, rq = _head_norm(q_raw, qg_t, seg)
    qs = (qn * scale).astype(BF)
    k_raw = jnp.concatenate([pkv[:, 0:NKV * HD], P[:, k0:v0]], axis=0)
    kn, rk = _head_norm(k_raw, kg_t, seg)
    knb = kn.astype(BF)
    heads = []
    outs = []
    for kv in range(NKV):
        kb = knb[:, kv * HD:(kv + 1) * HD]
        vb = jnp.concatenate([pkv[:, NKV * HD + kv * HD:NKV * HD + (kv + 1) * HD],
                              P[:, v0 + kv * HD:v0 + (kv + 1) * HD]], axis=0).astype(BF)
        Q = jnp.concatenate([qs[:, (kv * GQ + g) * HD:(kv * GQ + g + 1) * HD] for g in range(GQ)], axis=0)
        S = _dot(Q, kb, 1, 1) + bias_ref[kv * GQ * BLK:(kv + 1) * GQ * BLK, :]
        S = jnp.where(valid, S, NEG_INF)
        sink = jnp.concatenate([jnp.full((BLK, 1), sink_ref[0, kv * GQ + g], F32) for g in range(GQ)], axis=0)
        m = jnp.maximum(jnp.max(S, axis=-1, keepdims=True), sink)
        p = jnp.exp(S - m)
        es = jnp.exp(sink - m)
        denom = jnp.sum(p, axis=-1, keepdims=True) + es
        probs = p / denom
        O = _dot(probs.astype(BF), vb, 1, 0)
        heads.append(dict(kb=kb, vb=vb, Q=Q, probs=probs, psink=es / denom, O=O))
        outs += [O[g * BLK:(g + 1) * BLK, :] for g in range(GQ)]
    y_attn = jnp.concatenate(outs, axis=1)

    rc = lax.rsqrt(jnp.mean(y_conv * y_conv, axis=-1, keepdims=True) + EPS)
    ra = lax.rsqrt(jnp.mean(y_attn * y_attn, axis=-1, keepdims=True) + EPS)
    y = jnp.concatenate([y_conv * rc * gco, y_attn * ra * gao], axis=1)
    return dict(gate_b=gate_b, gate_c=gate_c, hc=hc, z=z, z1=z1, z2=z2, cz=cz, y_conv=y_conv, y_attn=y_attn,
                rc=rc, ra=ra, heads=heads, y=y, row=row, scale=scale, q_raw=q_raw, rq=rq, k_raw=k_raw, rk=rk)


BPS = 2
TILE = BPS * BLK
KV0 = 3 * CW + AW


def _mix_in_specs(tile_of):
    return [
        pl.BlockSpec(memory_space=pltpu.SMEM),
        pl.BlockSpec((TILE, INW), lambda s: (tile_of(s), 0)),
        pl.BlockSpec((8, CW), lambda s: (jnp.maximum(tile_of(s) * (TILE // 8) - 1, 0), 1)),
        pl.BlockSpec((8, CW), lambda s: (jnp.maximum(tile_of(s) * (TILE // 8) - 1, 0), 2)),
        pl.BlockSpec((BLK, 2 * NKV * HD), lambda s: (jnp.maximum(tile_of(s) * BPS - 1, 0), KV0 // (2 * NKV * HD))),
    ]


def _block_inputs(tile, b, zc_ref, zh_ref, pkv_ref, first_tile):
    P = tile[b * BLK:(b + 1) * BLK, :]
    if b == 0:
        return P, zc_ref[...], zh_ref[...], pkv_ref[...], first_tile
    lo = b * BLK
    return P, tile[lo - 8:lo, CW:2 * CW], tile[lo - 8:lo, 2 * CW:3 * CW], tile[lo - BLK:lo, KV0:KV0 + 2 * NKV * HD], False


def _mix_param_specs():
    return [
        pl.BlockSpec((8, CW), lambda s: (0, 0)),
        pl.BlockSpec((1, AW), lambda s: (0, 0)),
        pl.BlockSpec((1, NKV * HD), lambda s: (0, 0)),
        pl.BlockSpec((1, CW), lambda s: (0, 0)),
        pl.BlockSpec((1, AW), lambda s: (0, 0)),
        pl.BlockSpec((AW, 128), lambda s: (0, 0)),
        pl.BlockSpec((NH * BLK, 2 * BLK), lambda s: (0, 0)),
    ]


def _mix_params(cw8, qg, kg, gco, gao, bias):
    seg = np.zeros((AW, 128), np.float32)
    seg[np.arange(AW), np.arange(AW) // HD] = 1.0
    return (cw8, jnp.tile(qg, (1, NH)), jnp.tile(kg, (1, NKV)), gco, gao, jnp.asarray(seg, BF), bias)


def _mix_fwd(proj, sinks, cw8, qg, kg, gco, gao, bias, comm=()):
    def body(sink_ref, p_ref, zc_ref, zh_ref, pkv_ref, cw_ref, qg_ref, kg_ref, gco_ref, gao_ref, seg_ref, bias_ref, y_ref):
        tile = p_ref[...]
        for b in range(BPS):
            f = _mix_forward(*_block_inputs(tile, b, zc_ref, zh_ref, pkv_ref, pl.program_id(0) == 0), cw_ref[...],
                             qg_ref[...], kg_ref[...], gco_ref[...], gao_ref[...], seg_ref[...], sink_ref, bias_ref)
            y_ref[b * BLK:(b + 1) * BLK, :] = f["y"].astype(BF)

    return _call(
        body, (sinks, proj, proj, proj, proj, *_mix_params(cw8, qg, kg, gco, gao, bias)), name="mix_fwd", grid=(T // TILE,),
        in_specs=_mix_in_specs(lambda s: s) + _mix_param_specs(),
        out_specs=[pl.BlockSpec((TILE, D), lambda s: (s, 0))], out_shape=[SDS((T, D), BF)],
        sem=("parallel",), vmem_mib=40, comm=comm)


def _mix_bwd(proj, dy, sinks, cw8, qg, kg, gco, gao, bias, comm=()):
    n_steps = T // TILE

    def tile_of(s):
        return n_steps - 1 - s

    def body(sink_ref, p_ref, zc_ref, zh_ref, pkv_ref, dy_ref, cw_ref, qg_ref, kg_ref, gco_ref, gao_ref, seg_ref, bias_ref,
             dproj_ref, dcw_ref, dqg_ref, dkg_ref, dgco_ref, dgao_ref, dsink_ref, dbias_ref,
             ndcz_ref, dkc_ref, dvc_ref):
        s = pl.program_id(0)

        @pl.when(s == 0)
        def _():
            for r in (dcw_ref, dqg_ref, dkg_ref, dgco_ref, dgao_ref, dsink_ref, dbias_ref, ndcz_ref, dkc_ref, dvc_ref):
                r[...] = jnp.zeros_like(r)

        tile = p_ref[...]
        for b in reversed(range(BPS)):
            one_block(b, _block_inputs(tile, b, zc_ref, zh_ref, pkv_ref, s == n_steps - 1),
                      dy_ref[b * BLK:(b + 1) * BLK, :], sink_ref, cw_ref, qg_ref, kg_ref, gco_ref, gao_ref, seg_ref, bias_ref,
                      dproj_ref.at[b * BLK:(b + 1) * BLK, :], dcw_ref, dqg_ref, dkg_ref, dgco_ref, dgao_ref, dsink_ref,
                      dbias_ref, ndcz_ref, dkc_ref, dvc_ref)

    def one_block(b, inputs, dy, sink_ref, cw_ref, qg_ref, kg_ref, gco_ref, gao_ref, seg_ref, bias_ref,
                  dproj_ref, dcw_ref, dqg_ref, dkg_ref, dgco_ref, dgao_ref, dsink_ref, dbias_ref,
                  ndcz_ref, dkc_ref, dvc_ref):
        cw = cw_ref[...]
        qg_v, kg_v, gco_v, gao_v, seg = qg_ref[...], kg_ref[...], gco_ref[...], gao_ref[...], seg_ref[...]
        f = _mix_forward(*inputs, cw, qg_v, kg_v, gco_v, gao_v, seg, sink_ref, bias_ref)
        dyc, dgco = _rms_bwd(dy[:, 0:CW], f["y_conv"], f["rc"], gco_v)
        dya, dgao = _rms_bwd(dy[:, CW:CW + AW], f["y_attn"], f["ra"], gao_v)
        dgco_ref[...] += dgco
        dgao_ref[...] += dgao

        row = f["row"]
        dgate_b = dyc * f["cz"]
        dcz = dyc * f["gate_b"]
        dcw_ref[0:1, :] += jnp.sum(dcz * f["z2"], axis=0, keepdims=True)
        dcw_ref[1:2, :] += jnp.sum(dcz * f["z1"], axis=0, keepdims=True)
        dcw_ref[2:3, :] += jnp.sum(dcz * f["z"], axis=0, keepdims=True)
        nxt = ndcz_ref[...]
        n0 = nxt[0:1, :]
        n1 = nxt[1:2, :]
        d1 = jnp.where(row == BLK - 1, n0, pltpu.roll(dcz, BLK - 1, 0))
        d2 = jnp.where(row == BLK - 1, n1, jnp.where(row == BLK - 2, n0, pltpu.roll(dcz, BLK - 2, 0)))
        dz = cw[2:3, :] * dcz + cw[1:2, :] * d1 + cw[0:1, :] * d2
        ndcz_ref[...] = dcz[0:8, :]
        dproj_ref[:, 0:CW] = dgate_b.astype(BF)
        dproj_ref[:, CW:2 * CW] = (dz * f["hc"]).astype(BF)
        dproj_ref[:, 2 * CW:3 * CW] = (dz * f["gate_c"]).astype(BF)

        scale = f["scale"]
        lane = lax.broadcasted_iota(jnp.int32, (1, 128), 1)
        dq_cols, dk_cols, dv_cols = [], [], []
        for kv in range(NKV):
            hd = f["heads"][kv]
            dO = jnp.concatenate([dya[:, (kv * GQ + g) * HD:(kv * GQ + g + 1) * HD] for g in range(GQ)], axis=0)
            delta = jnp.sum(dO * hd["O"], axis=-1, keepdims=True)
            dOb = dO.astype(BF)
            dP = _dot(dOb, hd["vb"], 1, 1)
            dS = hd["probs"] * (dP - delta)
            dsk = hd["psink"] * delta
            for g in range(GQ):
                h = kv * GQ + g
                tot = jnp.sum(dsk[g * BLK:(g + 1) * BLK, :], axis=0, keepdims=True)
                dsink_ref[...] -= jnp.where(lane == h, tot, 0.0)
            dbias_ref[kv * GQ * BLK:(kv + 1) * GQ * BLK, :] += dS
            dSb = dS.astype(BF)
            dQ = _dot(dSb, hd["kb"], 1, 0)
            dKb = _dot(dSb, hd["Q"], 0, 0)
            dVb = _dot(hd["probs"].astype(BF), dOb, 0, 0)
            dk_cols.append(dKb[BLK:, :] + dkc_ref[:, kv * HD:(kv + 1) * HD])
            dv_cols.append(dVb[BLK:, :] + dvc_ref[:, kv * HD:(kv + 1) * HD])
            dkc_ref[:, kv * HD:(kv + 1) * HD] = dKb[:BLK, :]
            dvc_ref[:, kv * HD:(kv + 1) * HD] = dVb[:BLK, :]
            dq_cols += [dQ[g * BLK:(g + 1) * BLK, :] for g in range(GQ)]
        dq_raw, dqg_t = _head_norm_bwd(jnp.concatenate(dq_cols, axis=1) * scale, f["q_raw"], f["rq"], qg_v, seg)
        dk_raw, dkg_t = _head_norm_bwd(jnp.concatenate(dk_cols, axis=1), f["k_raw"][BLK:, :], f["rk"][BLK:, :], kg_v, seg)
        dqg_ref[...] += _fold_heads(dqg_t)
        dkg_ref[...] += _fold_heads(dkg_t)
        dproj_ref[:, 3 * CW:INW] = jnp.concatenate([dq_raw, dk_raw] + dv_cols, axis=1).astype(BF)

    small = lambda r, c: pl.BlockSpec((r, c), lambda s: (0, 0))
    return _call(
        body, (sinks, proj, proj, proj, proj, dy, *_mix_params(cw8, qg, kg, gco, gao, bias)), name="mix_bwd", grid=(n_steps,),
        in_specs=_mix_in_specs(tile_of) + [pl.BlockSpec((TILE, D), lambda s: (tile_of(s), 0))] + _mix_param_specs(),
        out_specs=[pl.BlockSpec((TILE, INW), lambda s: (tile_of(s), 0)), small(8, CW), small(1, HD), small(1, HD),
                   small(1, CW), small(1, AW), small(1, 128), small(NH * BLK, 2 * BLK)],
        out_shape=[SDS((T, INW), BF), SDS((8, CW), F32), SDS((1, HD), F32), SDS((1, HD), F32), SDS((1, CW), F32),
                   SDS((1, AW), F32), SDS((1, 128), F32), SDS((NH * BLK, 2 * BLK), F32)],
        scratch_shapes=[pltpu.VMEM((8, CW), F32), pltpu.VMEM((BLK, NKV * HD), F32), pltpu.VMEM((BLK, NKV * HD), F32)],
        sem=("arbitrary",), vmem_mib=56, comm=comm)


FT = 256
NFT = DFF // FT
RC = 128
NCH = T // RC
LEAD = 16


def _rows8(x):
    return jnp.sum(x.reshape(x.shape[0] // 8, 8, x.shape[1]), axis=0)


def _ffn_act_specs():
    return [
        pl.BlockSpec((T, FT), lambda j: (0, j)), pl.BlockSpec((T, FT), lambda j: (0, NFT + j)),
        pl.BlockSpec((8, FT), lambda j: (0, j)), pl.BlockSpec((8, FT), lambda j: (0, NFT + j)),
        pl.BlockSpec((1, FT), lambda j: (0, j)), pl.BlockSpec((1, FT), lambda j: (0, NFT + j)),
    ]


def _conv_rows(win, w, b, n):
    win = win.astype(F32)
    u = win[LEAD:LEAD + n]
    u1 = pltpu.roll(win, 1, 0)[LEAD:LEAD + n]
    u2 = pltpu.roll(win, 2, 0)[LEAD:LEAD + n]
    return u2, u1, u, w[0:1, :] * u2 + w[1:2, :] * u1 + w[2:3, :] * u + b


def _ffn_act(up, fw8, fb):
    def body(ug_ref, uv_ref, wg_ref, wv_ref, bg_ref, bv_ref, a_ref):
        wg, wv, bg, bv = wg_ref[...], wv_ref[...], bg_ref[...], bv_ref[...]

        def chunk(win_g, win_v):
            gp = _conv_rows(win_g, wg, bg, RC)[3]
            vp = _conv_rows(win_v, wv, bv, RC)[3]
            return (gp * jax.nn.sigmoid(gp) * vp).astype(BF)

        zero = jnp.zeros((LEAD, FT), BF)
        a_ref[0:RC, :] = chunk(jnp.concatenate([zero, ug_ref[0:RC, :]], axis=0),
                               jnp.concatenate([zero, uv_ref[0:RC, :]], axis=0))

        def step(i, carry):
            r0 = pl.multiple_of(i * RC, RC)
            win = pl.ds(r0 - LEAD, RC + LEAD)
            a_ref[pl.ds(r0, RC), :] = chunk(ug_ref[win, :], uv_ref[win, :])
            return carry

        lax.fori_loop(1, NCH, step, 0)

    return _call(
        body, (up, up, fw8, fw8, fb, fb), name="ffn_act", grid=(NFT,), in_specs=_ffn_act_specs(),
        out_specs=[pl.BlockSpec((T, FT), lambda j: (0, j))], out_shape=[SDS((T, DFF), BF)],
        sem=("parallel",), vmem_mib=40)


def _ffn_act_bwd(up, da, fw8, fb, comm=()):
    ext = RC + LEAD

    def body(ug_ref, uv_ref, wg_ref, wv_ref, bg_ref, bv_ref, da_ref,
             dug_ref, duv_ref, dwg_ref, dwv_ref, dbg_ref, dbv_ref):
        wg, wv, bg, bv = wg_ref[...], wv_ref[...], bg_ref[...], bv_ref[...]

        def chunk(win_g, win_v, da_e):
            g2, g1, g0, gp = _conv_rows(win_g, wg, bg, ext)
            v2, v1, v0, vp = _conv_rows(win_v, wv, bv, ext)
            da_e = da_e.astype(F32)
            sig = jax.nn.sigmoid(gp)
            dvp = da_e * (gp * sig)
            dgp = da_e * vp * (sig * (1.0 + gp * (1.0 - sig)))

            def back(dp, w):
                return (w[2:3, :] * dp[0:RC] + w[1:2, :] * pltpu.roll(dp, ext - 1, 0)[0:RC]
                        + w[0:1, :] * pltpu.roll(dp, ext - 2, 0)[0:RC]).astype(BF)

            def sums(dp, u2, u1, u0):
                d = dp[0:RC]
                return [_rows8(d), _rows8(d * u2[0:RC]), _rows8(d * u1[0:RC]), _rows8(d * u0[0:RC])]

            return back(dgp, wg), back(dvp, wv), sums(dgp, g2, g1, g0) + sums(dvp, v2, v1, v0)

        zero = jnp.zeros((LEAD, FT), BF)
        dug, duv, acc = chunk(jnp.concatenate([zero, ug_ref[0:ext, :]], axis=0),
                              jnp.concatenate([zero, uv_ref[0:ext, :]], axis=0), da_ref[0:ext, :])
        dug_ref[0:RC, :] = dug
        duv_ref[0:RC, :] = duv

        def step(i, acc):
            r0 = pl.multiple_of(i * RC, RC)
            win = pl.ds(r0 - LEAD, ext + LEAD)
            dug, duv, part = chunk(ug_ref[win, :], uv_ref[win, :], da_ref[pl.ds(r0, ext), :])
            dug_ref[pl.ds(r0, RC), :] = dug
            duv_ref[pl.ds(r0, RC), :] = duv
            return [a + p for a, p in zip(acc, part)]

        acc = lax.fori_loop(1, NCH - 1, step, acc)
        r0 = T - RC
        tail = lambda ref, lo: jnp.concatenate([ref[lo:T, :], zero], axis=0)
        dug, duv, part = chunk(tail(ug_ref, r0 - LEAD), tail(uv_ref, r0 - LEAD), tail(da_ref, r0))
        dug_ref[r0:T, :] = dug
        duv_ref[r0:T, :] = duv
        tot = [jnp.sum(a + p, axis=0, keepdims=True) for a, p in zip(acc, part)]
        for k, (dw_ref, db_ref) in enumerate(((dwg_ref, dbg_ref), (dwv_ref, dbv_ref))):
            db_ref[...] = tot[4 * k]
            dw_ref[...] = jnp.zeros_like(dw_ref)
            for r in range(3):
                dw_ref[r:r + 1, :] = tot[4 * k + 1 + r]

    col = lambda r: pl.BlockSpec((r, FT), lambda j: (0, j))
    return _call(
        body, (up, up, fw8, fw8, fb, fb, da), name="ffn_act_bwd", grid=(NFT,),
        in_specs=_ffn_act_specs() + [pl.BlockSpec((T, FT), lambda j: (0, j))],
        out_specs=[col(T), col(T), col(8), col(8), col(1), col(1)],
        out_shape=[SDS((T, DFF), BF), SDS((T, DFF), BF), SDS((8, DFF), F32), SDS((8, DFF), F32),
                   SDS((1, DFF), F32), SDS((1, DFF), F32)],
        sem=("parallel",), vmem_mib=40, comm=comm)


def _ffn_down_bwd(dh2b, w_down, comm=()):
    tm = TM

    def body(d_ref, w_ref, o_ref):
        o_ref[...] = _dot(d_ref[...], w_ref[...], 1, 1).astype(BF)

    return _call(
        body, (dh2b, w_down), name="ffn_down_bwd", grid=(T // tm,),
        in_specs=[pl.BlockSpec((tm, D), lambda i: (i, 0)), _resident((DFF, D))],
        out_specs=[pl.BlockSpec((tm, DFF), lambda i: (i, 0))], out_shape=[SDS((T, DFF), BF)],
        sem=("parallel",), vmem_mib=40, comm=comm)


def _norm_matmul_bwd(name, a_list, w_t, k_offsets, xin, g, dres, want_bf16, comm=(), slot=None):
    tm = TM
    ks = [a.shape[1] for a in a_list]
    n_a = len(a_list)
    n_pre = 0 if slot is None else 1

    def body(*refs):
        refs = refs[n_pre:]
        a_refs = refs[:n_a]
        w_ref, x_ref, g_ref, r_ref = refs[n_a:n_a + 4]
        outs = refs[n_a + 4:]
        dx_ref, dg_ref = outs[0], (outs[-1] if slot is None else outs[-1].at[0])

        @pl.when(pl.program_id(0) == 0)
        def _():
            dg_ref[...] = jnp.zeros_like(dg_ref)

        du = _dot(a_refs[0][...], w_ref[k_offsets[0]:k_offsets[0] + ks[0], :], 1, 0)
        for k in range(1, n_a):
            du = du + _dot(a_refs[k][...], w_ref[k_offsets[k]:k_offsets[k] + ks[k], :], 1, 0)
        x = x_ref[...]
        r = lax.rsqrt(jnp.mean(x * x, axis=-1, keepdims=True) + EPS)
        dx, dg = _rms_bwd(du, x, r, g_ref[...])
        dx = r_ref[...] + dx
        dx_ref[...] = dx
        if want_bf16:
            outs[1][...] = dx.astype(BF)
        dg_ref[...] += dg

    tile = lambda c: pl.BlockSpec((tm, c), lambda i, *_: (i, 0))
    if slot is None:
        dg_spec, dg_shape = pl.BlockSpec((1, D), lambda i: (0, 0)), SDS((1, D), F32)
    else:
        dg_spec, dg_shape = pl.BlockSpec((1, 1, D), lambda i, slot_ref: (slot_ref[0], 0, 0)), SDS((N_DEV, 1, D), F32)
    out_specs = [tile(D)] + ([tile(D)] if want_bf16 else []) + [dg_spec]
    out_shape = [SDS((T, D), F32)] + ([SDS((T, D), BF)] if want_bf16 else []) + [dg_shape]
    return _call(
        body, (*a_list, w_t, xin, g, dres), name=name, grid=(T // tm,), prefetch=() if slot is None else (slot,),
        in_specs=[tile(k) for k in ks] + [_resident(w_t.shape), tile(D),
                                           pl.BlockSpec((1, D), lambda i, *_: (0, 0)), tile(D)],
        out_specs=out_specs, out_shape=out_shape, sem=("arbitrary",), vmem_mib=56, comm=comm)


def _out_bwd(dh1b, w_out, comm=()):
    tm = TM

    def body(d_ref, w_ref, o_ref):
        o_ref[...] = _dot(d_ref[...], w_ref[...], 1, 1)

    return _call(
        body, (dh1b, w_out), name="out_bwd", grid=(T // tm,),
        in_specs=[pl.BlockSpec((tm, D), lambda i: (i, 0)), _resident((D, D))],
        out_specs=[pl.BlockSpec((tm, D), lambda i: (i, 0))], out_shape=[SDS((T, D), F32)],
        sem=("parallel",), vmem_mib=32, comm=comm)


def _wgrad(name, a_list, b, comm=()):
    m_k = a_list[0].shape[1]
    tm = max(t for t in range(128, m_k // 2 + 1, 128) if m_k % t == 0)
    steps = [a.shape[1] // tm for a in a_list]
    starts = [sum(steps[:k]) for k in range(len(a_list))]
    n_a = len(a_list)

    def body(*refs):
        a_refs, b_ref, o_ref = refs[:n_a], refs[n_a], refs[n_a + 1]
        i = pl.program_id(0)
        for k in range(n_a):
            @pl.when((i >= starts[k]) & (i < starts[k] + steps[k]))
            def _(k=k):
                o_ref[...] = _dot(a_refs[k][...], b_ref[...], 0, 0).astype(BF)

    def a_spec(k):
        return pl.BlockSpec((T, tm), lambda i: (0, jnp.clip(i - starts[k], 0, steps[k] - 1)))

    m_total = tm * sum(steps)
    return _call(
        body, (*a_list, b), name=name, grid=(sum(steps),),
        in_specs=[a_spec(k) for k in range(n_a)] + [_resident((T, D))],
        out_specs=[pl.BlockSpec((tm, D), lambda i: (i, 0))], out_shape=[SDS((m_total, D), BF)],
        sem=("parallel",), vmem_mib=40, comm=comm)


def _chip_sum(name, gbf, from_sib, core, chip):
    h = gbf.shape[1]
    th = h // 2

    def body(core_ref, chip_ref, g_ref, s_ref, pbf_ref, own_ref):
        p = g_ref[0].astype(F32) + s_ref[0].astype(F32)
        pbf_ref[0] = p.astype(BF)

        @pl.when(pl.program_id(1) == chip_ref[0])
        def _():
            own_ref[...] = p

    grid_spec = pltpu.PrefetchScalarGridSpec(
        num_scalar_prefetch=2, grid=(h // th, N_CHIPS),
        in_specs=[pl.BlockSpec((1, th, D), lambda t, jj, core_ref, chip_ref: (2 * jj + core_ref[0], t, 0)),
                  pl.BlockSpec((1, th, D), lambda t, jj, core_ref, chip_ref: (jj, t, 0))],
        out_specs=[pl.BlockSpec((1, th, D), lambda t, jj, core_ref, chip_ref: (jj, t, 0)),
                   pl.BlockSpec((th, D), lambda t, jj, core_ref, chip_ref: (t, 0))],
    )
    return _pcall(
        body, name=name, grid_spec=grid_spec, out_shape=_in_hbm([SDS((N_CHIPS, h, D), BF), SDS((h, D), F32)]),
        compiler_params=_params(("arbitrary", "arbitrary"), 32),
    )(core, chip, *_from_hbm(gbf, from_sib))


def _final_sum(name, own, from_chips, core):
    h = own.shape[0]

    def body(core_ref, o_ref, r_ref, f_ref):
        f_ref[0] = ((o_ref[...] + r_ref[0].astype(F32)) + r_ref[1].astype(F32)) + r_ref[2].astype(F32)

    grid_spec = pltpu.PrefetchScalarGridSpec(
        num_scalar_prefetch=1, grid=(1,),
        in_specs=[pl.BlockSpec((h, D), lambda i, core_ref: (0, 0)), pl.BlockSpec((3, h, D), lambda i, core_ref: (0, 0, 0))],
        out_specs=pl.BlockSpec((1, h, D), lambda i, core_ref: (core_ref[0], 0, 0)),
    )
    return _pcall(body, name=name, grid_spec=grid_spec, out_shape=pltpu.HBM((2, h, D), F32),
                  compiler_params=_params(("arbitrary",), 40))(core, *_from_hbm(own, from_chips))


def _adam_math(w, g, m, v):
    nm = ADAM_B1 * m + (1.0 - ADAM_B1) * g
    nv = ADAM_B2 * v + (1.0 - ADAM_B2) * (g * g)
    m_hat = nm / (1.0 - ADAM_B1 ** ADAM_STEP)
    v_hat = nv / (1.0 - ADAM_B2 ** ADAM_STEP)
    return -ADAM_LR * (m_hat / (jnp.sqrt(v_hat) + ADAM_EPS) + ADAM_WD * w), nm, nv


def _adamw(name, w, g, m, v, tr, copy_g=False, stage=True):
    rows, cols = w.shape

    def body(w_ref, g_ref, m_ref, v_ref, *outs):
        g_val = g_ref[...]
        if copy_g:
            outs[0][...] = g_val
        d_ref, nm_ref, nv_ref = outs[-3:]
        d_ref[...], nm_ref[...], nv_ref[...] = _adam_math(w_ref[...], g_val, m_ref[...], v_ref[...])

    spec = pl.BlockSpec((tr, cols), lambda i: (i, 0))
    n_out = 4 if copy_g else 3
    return _call(body, (w, g, m, v), name=name, grid=(rows // tr,), in_specs=[spec] * 4, out_specs=[spec] * n_out,
                 out_shape=[SDS((rows, cols), F32)] * n_out, sem=("parallel",), vmem_mib=32,
                 free=(0, 2, 3) if stage else ())


C_G1, C_G2, C_GCO, C_GAO, C_DCW, C_DQG, C_DKG, C_SINK, C_SQ = 0, 1024, 2048, 2560, 3072, 4608, 4736, 4864, 5632
P_W = C_SQ + 128


def _pack_small(me, dfwg, dfwv, dfbg, dfbv, dg2, dgco, dgao, dcw8, dqg, dkg, dsink, sq):
    def body(me_ref, dfwg_r, dfwv_r, dfbg_r, dfbv_r, dg2_r, dgco_r, dgao_r, dcw_r, dqg_r, dkg_r, dsink_r, sq_r, o):
        o[...] = jnp.zeros_like(o)
        o[0, :, 0:DFF] = dfwg_r[...]
        o[0, :, DFF:2 * DFF] = dfwv_r[...]
        o[0, 3:4, 0:DFF] = dfbg_r[...]
        o[0, 3:4, DFF:2 * DFF] = dfbv_r[...]
        o[0, 4:5, C_G2:C_G2 + D] = dg2_r[...]
        o[0, 4:5, C_GCO:C_GCO + CW] = dgco_r[...]
        o[0, 4:5, C_GAO:C_GAO + AW] = dgao_r[...]
        for r in range(3):
            o[0, 4:5, C_DCW + r * CW:C_DCW + (r + 1) * CW] = dcw_r[r:r + 1, :]
        o[0, 4:5, C_DQG:C_DQG + HD] = dqg_r[...]
        o[0, 4:5, C_DKG:C_DKG + HD] = dkg_r[...]
        o[0, 4:5, C_SINK:C_SINK + 128] = dsink_r[...]
        o[0, :, C_SQ:C_SQ + 128] = sq_r[...]

    ins = (dfwg, dfwv, dfbg, dfbv, dg2, dgco, dgao, dcw8, dqg, dkg, dsink, sq)
    return _call(body, ins, name="pack_small", grid=(1,), prefetch=(me,),
                 in_specs=[pl.BlockSpec(a.shape, lambda i, me_ref: (0, 0)) for a in ins],
                 out_specs=[pl.BlockSpec((1, 8, P_W), lambda i, me_ref: (me_ref[0], 0, 0))],
                 out_shape=[SDS((N_DEV, 8, P_W), F32)], sem=("arbitrary",))[0]


N_SMALL = 11


def _small_adam(chip, p_all, g1_all, tbl_all, ws, ms, vs):
    fw_cols = 2 * DFF // N_CHIPS
    cw_cols = CW // N_CHIPS

    def body(chip_ref, p_ref, fw_ref, cw0_ref, cw1_ref, cw2_ref, g1_ref, tbl_ref, *refs):
        w_r, m_r, v_r = refs[0:N_SMALL], refs[N_SMALL:2 * N_SMALL], refs[2 * N_SMALL:3 * N_SMALL]
        outs = refs[3 * N_SMALL:]
        g_o, d_o, nm_o, nv_o = (outs[k * N_SMALL:(k + 1) * N_SMALL] for k in range(4))
        loss_o = outs[4 * N_SMALL]

        def total(ref):
            s = ref[0]
            for k in range(1, N_DEV):
                s = s + ref[k]
            return s

        S = total(p_ref)
        fw = total(fw_ref)
        cws = [total(r) for r in (cw0_ref, cw1_ref, cw2_ref)]

        def step(i, g, at):
            d, nm, nv = _adam_math(w_r[i][at], g, m_r[i][at], v_r[i][at])
            g_o[i][at], d_o[i][at], nm_o[i][at], nv_o[i][at] = g, d, nm, nv

        everything = (slice(None), slice(None))
        step(0, total(g1_ref), everything)
        for r in range(3):
            step(1, cws[r][4:5, :], (r, slice(None), slice(None)))
        step(2, S[4:5, C_DQG:C_DQG + HD], everything)
        step(3, S[4:5, C_DKG:C_DKG + HD], everything)
        step(4, total(tbl_ref), everything)
        step(5, S[4:5, C_SINK:C_SINK + NH], everything)
        step(6, S[4:5, C_GCO:C_GCO + CW], everything)
        step(7, S[4:5, C_GAO:C_GAO + AW], everything)
        step(8, S[4:5, C_G2:C_G2 + D], everything)
        for r in range(3):
            step(9, fw[r:r + 1, :], (r, slice(None), slice(None)))
        step(10, S[3:4, 0:2 * DFF], everything)
        sq = S[:, C_SQ:C_SQ + 128]
        loss_o[...] = jnp.sum(jnp.sum(sq, axis=1, keepdims=True), axis=0, keepdims=True) * (0.5 / D)

    def full(a):
        n = len(a.shape)
        return pl.BlockSpec(a.shape, lambda i, chip_ref: (0,) * n)

    params = [*ws, *ms, *vs]
    out = _call(
        body, (p_all, p_all, p_all, p_all, p_all, g1_all, tbl_all, *params), name="small_adam", grid=(1,), prefetch=(chip,),
        in_specs=[full(p_all),
                  pl.BlockSpec((N_DEV, 8, fw_cols), lambda i, chip_ref: (0, 0, chip_ref[0])),
                  *[pl.BlockSpec((N_DEV, 8, cw_cols), lambda i, chip_ref, r=r: (0, 0, (C_DCW + r * CW) // cw_cols + chip_ref[0]))
                    for r in range(3)],
                  full(g1_all), full(tbl_all), *[full(a) for a in params]],
        out_specs=[full(a) for a in ws] * 4 + [pl.BlockSpec((1, 1), lambda i, chip_ref: (0, 0))],
        out_shape=[SDS(a.shape, F32) for a in ws] * 4 + [SDS((1, 1), F32)], sem=("arbitrary",), vmem_mib=32)
    return out[0:N_SMALL], out[N_SMALL:2 * N_SMALL], out[2 * N_SMALL:3 * N_SMALL], out[3 * N_SMALL:4 * N_SMALL], out[4 * N_SMALL]


PLACE_STEPS = 4


def _place_specs(shards):
    rows = [s.shape[0] // PLACE_STEPS for s in shards]
    return ([pl.BlockSpec((r, D), lambda i, chip_ref: (i, 0)) for r in rows],
            [pl.BlockSpec((r, D), lambda i, chip_ref: (chip_ref[0] * PLACE_STEPS + i, 0)) for r in rows],
            [SDS((N_CHIPS * s.shape[0], D), BF) for s in shards])


def _place_first(chip, shard, conv_w, ffn_conv_w):
    def body(chip_ref, a, s0, s1, o, t0, t1):
        o[...] = a[...].astype(BF)

        @pl.when(pl.program_id(0) == 0)
        def _():
            for s, t in ((s0, t0), (s1, t1)):
                t[...] = jnp.zeros_like(t)
                t[0, 0:3, :] = s[...]

    ins, outs, shapes = _place_specs([shard])
    taps = (conv_w, ffn_conv_w)
    return _call(
        body, (shard, conv_w, ffn_conv_w), name="place_first", grid=(PLACE_STEPS,), prefetch=(chip,),
        in_specs=ins + [pl.BlockSpec(s.shape, lambda i, chip_ref: (0, 0)) for s in taps],
        out_specs=outs + [pl.BlockSpec((1, 8, s.shape[1]), lambda i, chip_ref: (chip_ref[0], 0, 0)) for s in taps],
        out_shape=shapes + [SDS((N_CHIPS, 8, s.shape[1]), F32) for s in taps],
        sem=("arbitrary",), vmem_mib=32, free=(0, 1, 2))


def _place_rest(chip, shards, table, bucket, comm):
    n = len(shards)

    def body(chip_ref, *refs):
        a, (tab_ref, bk_ref), o, bias_ref = refs[:n], refs[n:n + 2], refs[n + 2:2 * n + 2], refs[2 * n + 2]
        for src, dst in zip(a, o):
            dst[...] = src[...].astype(BF)

        @pl.when(pl.program_id(0) == 0)
        def _():
            bk = bk_ref[...]
            eq = [bk == b for b in range(NBUCKET)]
            for h in range(NH):
                acc = jnp.zeros((BLK, 2 * BLK), F32)
                for b in range(NBUCKET):
                    acc = jnp.where(eq[b], tab_ref[h, b], acc)
                bias_ref[h * BLK:(h + 1) * BLK, :] = acc

    ins, outs, shapes = _place_specs(shards)
    return _call(
        body, (*shards, table, bucket), name="place_rest", grid=(PLACE_STEPS,), prefetch=(chip,),
        in_specs=ins + [pl.BlockSpec(memory_space=pltpu.SMEM), pl.BlockSpec(bucket.shape, lambda i, chip_ref: (0, 0))],
        out_specs=outs + [pl.BlockSpec((NH * BLK, 2 * BLK), lambda i, chip_ref: (0, 0))],
        out_shape=shapes + [SDS((NH * BLK, 2 * BLK), F32)],
        sem=("arbitrary",), vmem_mib=32, comm=comm, free=tuple(range(n + 2)))


def kernel(x, norm_mix_g, w_in, conv_w, q_norm_g, k_norm_g, rel_bias_table, sinks, out_norm_conv_g, out_norm_attn_g, w_out, norm_ffn_g, w_up, ffn_conv_w, ffn_conv_b, w_down, loss_target, m_norm_mix_g, m_w_in, m_conv_w, m_q_norm_g, m_k_norm_g, m_rel_bias_table, m_sinks, m_out_norm_conv_g, m_out_norm_attn_g, m_w_out, m_norm_ffn_g, m_w_up, m_ffn_conv_w, m_ffn_conv_b, m_w_down, v_norm_mix_g, v_w_in, v_conv_w, v_q_norm_g, v_k_norm_g, v_rel_bias_table, v_sinks, v_out_norm_conv_g, v_out_norm_attn_g, v_w_out, v_norm_ffn_g, v_w_up, v_ffn_conv_w, v_ffn_conv_b, v_w_down):
    as_arg = lambda i: jnp.reshape(i, (1,)).astype(jnp.int32)
    chip = as_arg(2 * lax.axis_index("x") + lax.axis_index("y"))
    core = as_arg(lax.axis_index("c"))
    me = 2 * chip + core
    xs, tgt = x[0], loss_target[0]
    qg, kg, gco, gao, g1, g2, fb = q_norm_g, k_norm_g, out_norm_conv_g, out_norm_attn_g, norm_mix_g, norm_ffn_g, ffn_conv_b
    pieces = lambda g: g.reshape(N_DEV, g.shape[0] // N_DEV, D)
    whole = lambda f: f.reshape(2 * f.shape[1], D)

    bucket = jnp.asarray(_bucket_table())
    p_in, p_cw, p_fw = _place_first(chip, w_in[0].T, conv_w[0], ffn_conv_w[0])
    p_out, p_up, p_down, bias, w_int, cw_all, fw_all = _place_rest(
        chip, [w_out[0], w_up[0].T, w_down[0]], rel_bias_table.T, bucket,
        comm=[_t_gather(p_in), _t_small_weights(p_cw), _t_small_weights(p_fw)])
    cw8 = jnp.transpose(cw_all, (1, 0, 2)).reshape(8, CW)
    fw8 = jnp.transpose(fw_all, (1, 0, 2)).reshape(8, 2 * DFF)

    proj, u1, w_out_f = _inproj(xs, g1, w_int, comm=[_t_gather(p_out)])
    y, w_upt = _mix_fwd(proj, sinks, cw8, qg, kg, gco, gao, bias, comm=[_t_gather(p_up)])
    h1, u2 = _outproj(y, w_out_f, xs, g2)
    up, w_down_f = _ffn_up(u2, w_upt, comm=[_t_gather(p_down)])
    a, = _ffn_act(up, fw8, fb)
    dh2, dh2b, sq = _ffn_down(a, w_down_f, h1, tgt)

    gdbf, = _wgrad("wgrad_down", [a], dh2b)
    da, sib_down = _ffn_down_bwd(dh2b, w_down_f, comm=[_t_sibling(pieces(gdbf))])
    pbf_down, own_down = _chip_sum("chip_sum_w_down", pieces(gdbf), sib_down, core, chip)
    dug, duv, dfwg, dfwv, dfbg, dfbv, chips_down, _ = _ffn_act_bwd(up, da, fw8, fb, comm=[_t_chips(pbf_down)])
    fin_down = _final_sum("final_sum_w_down", own_down, chips_down, core)
    gubf, = _wgrad("wgrad_up", [dug, duv], u2)
    dh1, dh1b, dg2, sib_up, fin_down = _norm_matmul_bwd(
        "ffn_up_bwd", [dug, duv], w_upt, [0, DFF], h1, g2, dh2, True, comm=[_t_sibling(pieces(gubf)), _t_swap(fin_down)])
    pbf_up, own_up = _chip_sum("chip_sum_w_up", pieces(gubf), sib_up, core, chip)
    gobf, = _wgrad("wgrad_out", [y], dh1b)
    dy, sib_out = _out_bwd(dh1b, w_out_f, comm=[_t_sibling(pieces(gobf))])
    pbf_out, own_out = _chip_sum("chip_sum_w_out", pieces(gobf), sib_out, core, chip)
    dproj, dcw8, dqg, dkg, dgco, dgao, dsink, dbias, chips_up, _, chips_out, _ = _mix_bwd(
        proj, dy, sinks, cw8, qg, kg, gco, gao, bias, comm=[_t_chips(pbf_up), _t_chips(pbf_out)])
    fin_up = _final_sum("final_sum_w_up", own_up, chips_up, core)
    fin_out = _final_sum("final_sum_w_out", own_out, chips_out, core)
    tbl_all = _band_bias_bwd(dbias, bucket, me)
    p_all = _pack_small(me, dfwg, dfwv, dfbg, dfbv, dg2, dgco, dgao, dcw8, dqg, dkg, dsink, sq)
    gibf, fin_up, p_all, tbl_all = _wgrad(
        "wgrad_in", [dproj], u1, comm=[_t_swap(fin_up), _t_allgather(p_all), _t_allgather(tbl_all)])
    sib_in, fin_out = _comm_call("to_sibling_last", [_t_sibling(pieces(gibf)), _t_swap(fin_out)])
    pbf_in, own_in = _chip_sum("chip_sum_w_in", pieces(gibf), sib_in, core, chip)
    dx, g1_all, chips_in, _ = _norm_matmul_bwd(
        "in_bwd", [dproj], w_int, [0], xs, g1, dh1, False, comm=[_t_chips(pbf_in)], slot=me)
    fin_in = _final_sum("final_sum_w_in", own_in, chips_in, core)
    g1_all, fin_in = _comm_call("gather_last", [_t_allgather(g1_all), _t_swap(fin_in)])

    g_w_out, g_w_up, g_w_down = whole(fin_out), whole(fin_up).T, whole(fin_down)
    g_w_down, d_down, nm_down, nv_down = _adamw("adamw_w_down", w_down[0], g_w_down, m_w_down[0], v_w_down[0], 352, True)
    d_up, nm_up, nv_up = _adamw("adamw_w_up", w_up[0], g_w_up, m_w_up[0], v_w_up[0], 256, stage=False)
    g_w_out, d_out, nm_out, nv_out = _adamw("adamw_w_out", w_out[0], g_w_out, m_w_out[0], v_w_out[0], 256, True)
    g_w_in, d_in, nm_in, nv_in = [a.T for a in _adamw(
        "adamw_w_in", w_in[0].T, whole(fin_in), m_w_in[0].T, v_w_in[0].T, INW // N_CHIPS // 3, True)]
    taps = lambda a: jnp.transpose(a, (1, 0, 2))
    sw = [norm_mix_g, taps(conv_w), q_norm_g, k_norm_g, rel_bias_table.T, sinks, out_norm_conv_g, out_norm_attn_g,
          norm_ffn_g, taps(ffn_conv_w), ffn_conv_b]
    smm = [m_norm_mix_g, taps(m_conv_w), m_q_norm_g, m_k_norm_g, m_rel_bias_table.T, m_sinks, m_out_norm_conv_g,
           m_out_norm_attn_g, m_norm_ffn_g, taps(m_ffn_conv_w), m_ffn_conv_b]
    smv = [v_norm_mix_g, taps(v_conv_w), v_q_norm_g, v_k_norm_g, v_rel_bias_table.T, v_sinks, v_out_norm_conv_g,
           v_out_norm_attn_g, v_norm_ffn_g, taps(v_ffn_conv_w), v_ffn_conv_b]
    *small_out, loss = _small_adam(chip, p_all, g1_all, tbl_all, sw, smm, smv)
    sg, sd, snm, snv = [list(r) for r in small_out]
    for r in (sg, sd, snm, snv):
        r[1], r[4], r[9] = taps(r[1]), r[4].T, taps(r[9])

    def order(s, b_in, b_out, b_up, b_down):
        return (s[0], b_in[None], s[1], s[2], s[3], s[4], s[5], s[6], s[7], b_out[None], s[8], b_up[None],
                s[9], s[10], b_down[None])

    return (loss.reshape(()), dx[None],
            *order(sg, g_w_in, g_w_out, g_w_up, g_w_down),
            *order(sd, d_in, d_out, d_up, d_down),
            *order(snm, nm_in, nm_out, nm_up, nm_down),
            *order(snv, nv_in, nv_out, nv_up, nv_down))
```

```python
import functools
import math

import numpy as np

import jax
import jax.numpy as jnp
from jax import lax
from jax.experimental import pallas as pl
from jax.experimental.pallas import tpu as pltpu

F32 = jnp.float32
BF = jnp.bfloat16
SDS = jax.ShapeDtypeStruct

T = 2048
D = 1024
CW = 512
AW = 512
HD = 64
NH = 8
NKV = 2
GQ = 4
INW = 2304
DFF = 2816
BLK = 128
NB = T // BLK
NBUCKET = 32
EPS = 1e-6
NEG_INF = -1e30
N_CHIPS = 4
N_DEV = 8

ADAM_LR = 0.001
ADAM_B1 = 0.9
ADAM_B2 = 0.999
ADAM_EPS = 1e-08
ADAM_WD = 0.01
ADAM_STEP = 10

TM = 512
MIB = 1024 * 1024
MESH = pl.DeviceIdType.MESH
ANY = pl.BlockSpec(memory_space=pl.ANY)

_pcall = pl.pallas_call


def _params(sem=None, vmem_mib=None):
    kw = {}
    if sem is not None:
        kw["dimension_semantics"] = sem
    if vmem_mib is not None:
        kw["vmem_limit_bytes"] = vmem_mib * MIB
    return pltpu.CompilerParams(**kw)


def _resident(shape):
    return pl.BlockSpec(shape, lambda *_: (0,) * len(shape), pipeline_mode=pl.Buffered(1))


def _dot(a, b, ca, cb):
    return lax.dot_general(a, b, (((ca,), (cb,)), ((), ())), preferred_element_type=F32)


def _rms_bwd(dy, x, r, g):
    dg = jnp.sum(dy * (x * r), axis=0, keepdims=True)
    dgx = dy * g
    dx = r * dgx - x * (r * r * r) * jnp.mean(x * dgx, axis=-1, keepdims=True)
    return dx, dg


def _where():
    x, y, c = lax.axis_index("x"), lax.axis_index("y"), lax.axis_index("c")
    return x, y, c, [(1 - x, y), (x, 1 - y), (1 - x, 1 - y)]


def _rcopy(src, dst, ssem, rsem, dev):
    return pltpu.make_async_remote_copy(src_ref=src, dst_ref=dst, send_sem=ssem, recv_sem=rsem, device_id=dev,
                                        device_id_type=MESH)


class _Task:
    def __init__(self, ins, outs, alias, n_sem, start, finish, middle=None, relay=None):
        nothing = lambda *args: None
        self.ins, self.outs, self.alias, self.n_sem, self.start, self.finish = ins, outs, alias, n_sem, start, finish
        self.middle, self.relay = middle or nothing, relay or nothing


def _t_gather(placed):
    R = placed.shape[0] // N_CHIPS
    q = R // 4

    def quarter(chip_index, core, k):
        return pl.ds(pl.multiple_of(chip_index * R + core * 2 * q + k * q, 16), q)

    def half(chip_index, core):
        return pl.ds(pl.multiple_of(chip_index * R + core * 2 * q, 16), 2 * q)

    def places():
        x, y, c, _ = _where()
        return c, 2 * x + y, 2 * (1 - x) + y, 2 * x + (1 - y), 2 * (1 - x) + (1 - y), (1 - x, y, c), (x, 1 - y, c), (x, y, 1 - c)

    def send(cout, ss, rs, b, copies):
        c, me, _, _, _, x_nbr, y_nbr, _ = places()
        for k, quart, along_x in copies:
            part = cout[0].at[quarter(me, c, quart)]
            _rcopy(part, part, ss.at[b + k], rs.at[b + k], x_nbr if along_x else y_nbr).start()

    def relay(cin, cout, ss, rs, b):
        send(cout, ss, rs, b, ((0, 0, True), (1, 1, False)))

    def start(cin, cout, ss, rs, b):
        send(cout, ss, rs, b, ((2, 1, True), (3, 0, False)))

    def middle(cin, cout, ss, rs, b):
        c, _, xc, yc, _, x_nbr, y_nbr, _ = places()
        buf = cout[0]
        for k, chip_index, quart, dev in ((0, xc, 0, y_nbr), (1, yc, 1, x_nbr)):
            part = buf.at[quarter(chip_index, c, quart)]
            _rcopy(part, part, ss.at[b + k], rs.at[b + k], dev).wait_recv()
            _rcopy(part, part, ss.at[b + 4 + k], rs.at[b + 4 + k], dev).start()

    def finish(cin, cout, ss, rs, b):
        c, me, xc, yc, dc, x_nbr, y_nbr, sib = places()
        buf = cout[0]
        for k, chip_index, quart in ((2, xc, 1), (3, yc, 0), (4, dc, 0), (5, dc, 1)):
            part = buf.at[quarter(chip_index, c, quart)]
            _rcopy(part, part, ss.at[b + k], rs.at[b + k], sib).wait_recv()
        for k, chip_index in ((6, xc), (7, yc), (8, dc)):
            got = buf.at[half(chip_index, c)]
            _rcopy(got, got, ss.at[b + k], rs.at[b + k], sib).start()
        for k, chip_index in ((6, xc), (7, yc), (8, dc)):
            got = buf.at[half(chip_index, 1 - c)]
            _rcopy(got, got, ss.at[b + k], rs.at[b + k], sib).wait_recv()
        for k in range(6):
            part = buf.at[quarter(me, c, 0)]
            _rcopy(part, part, ss.at[b + k], rs.at[b + k], sib).wait_send()
        for k in range(6, 9):
            got = buf.at[half(me, c)]
            _rcopy(got, got, ss.at[b + k], rs.at[b + k], sib).wait_send()

    return _Task([placed], [SDS(placed.shape, placed.dtype)], [(0, 0)], 9, start, finish, middle, relay)


def _t_small_weights(buf):
    def start(cin, cout, ss, rs, b):
        x, y, c, chips = _where()
        mine = cout[0].at[2 * x + y]
        for r, (px, py) in enumerate(chips):
            _rcopy(mine, mine, ss.at[b + r], rs.at[b + r], (px, py, c)).start()

    def finish(cin, cout, ss, rs, b):
        x, y, c, chips = _where()
        for r, (px, py) in enumerate(chips):
            got = cout[0].at[2 * px + py]
            _rcopy(got, got, ss.at[b + r], rs.at[b + r], (px, py, c)).wait_recv()
        for r, (px, py) in enumerate(chips):
            mine = cout[0].at[2 * x + y]
            _rcopy(mine, mine, ss.at[b + r], rs.at[b + r], (px, py, c)).wait_send()

    return _Task([buf], [SDS(buf.shape, buf.dtype)], [(0, 0)], 3, start, finish)


def _t_sibling(gbf):
    def start(cin, cout, ss, rs, b):
        x, y, c, _ = _where()
        for jj in range(N_CHIPS):
            _rcopy(cin[0].at[2 * jj + (1 - c)], cout[0].at[jj], ss.at[b + jj], rs.at[b + jj], (x, y, 1 - c)).start()

    def finish(cin, cout, ss, rs, b):
        x, y, c, _ = _where()
        for jj in range(N_CHIPS):
            got = cout[0].at[jj]
            _rcopy(got, got, ss.at[b + jj], rs.at[b + jj], (x, y, 1 - c)).wait_recv()
        for jj in range(N_CHIPS):
            got = cout[0].at[jj]
            _rcopy(got, got, ss.at[b + jj], rs.at[b + jj], (x, y, 1 - c)).wait_send()

    return _Task([gbf], [SDS((N_CHIPS,) + gbf.shape[1:], BF)], [], N_CHIPS, start, finish)


def _t_chips(pbf):
    h = pbf.shape[1]
    halves = (pl.ds(0, h // 2), pl.ds(h // 2, h // 2))

    def places():
        x, y, c, _ = _where()
        return 2 * (1 - x) + y, 2 * x + (1 - y), 2 * (1 - x) + (1 - y), (1 - x, y, c), (x, 1 - y, c)

    def relay_out(cin, cout, ss, rs, b):
        _, _, dc, x_nbr, y_nbr = places()
        _rcopy(cin[0].at[dc, halves[0]], cout[1].at[0], ss.at[b + 2], rs.at[b + 2], x_nbr).start()
        _rcopy(cin[0].at[dc, halves[1]], cout[1].at[1], ss.at[b + 3], rs.at[b + 3], y_nbr).start()

    def start(cin, cout, ss, rs, b):
        xc, yc, _, x_nbr, y_nbr = places()
        _rcopy(cin[0].at[xc], cout[0].at[0], ss.at[b + 0], rs.at[b + 0], x_nbr).start()
        _rcopy(cin[0].at[yc], cout[0].at[1], ss.at[b + 1], rs.at[b + 1], y_nbr).start()

    def middle(cin, cout, ss, rs, b):
        _, _, _, x_nbr, y_nbr = places()
        got, relay = cout[0], cout[1]
        for k, dev in ((0, y_nbr), (1, x_nbr)):
            _rcopy(relay.at[k], relay.at[k], ss.at[b + 2 + k], rs.at[b + 2 + k], dev).wait_recv()
            _rcopy(relay.at[k], got.at[2, halves[k]], ss.at[b + 4 + k], rs.at[b + 4 + k], dev).start()

    def finish(cin, cout, ss, rs, b):
        _, _, _, x_nbr, _ = places()
        got, relay = cout[0], cout[1]
        for k in (0, 1):
            _rcopy(got.at[k], got.at[k], ss.at[b + k], rs.at[b + k], x_nbr).wait_recv()
            _rcopy(relay.at[k], got.at[2, halves[k]], ss.at[b + 4 + k], rs.at[b + 4 + k], x_nbr).wait_recv()
        for k in (0, 1):
            _rcopy(got.at[k], got.at[k], ss.at[b + k], rs.at[b + k], x_nbr).wait_send()
        for k in (2, 3, 4, 5):
            _rcopy(relay.at[0], relay.at[0], ss.at[b + k], rs.at[b + k], x_nbr).wait_send()

    return _Task([pbf], [SDS((3,) + pbf.shape[1:], BF), SDS((2, h // 2) + pbf.shape[2:], BF)], [], 6, start, finish, middle,
                 relay_out)


def _t_swap(fin):
    def start(cin, cout, ss, rs, b):
        x, y, c, _ = _where()
        mine = cout[0].at[c]
        _rcopy(mine, mine, ss.at[b], rs.at[b], (x, y, 1 - c)).start()

    def finish(cin, cout, ss, rs, b):
        x, y, c, _ = _where()
        got = cout[0].at[1 - c]
        _rcopy(got, got, ss.at[b], rs.at[b], (x, y, 1 - c)).wait_recv()
        _rcopy(got, got, ss.at[b], rs.at[b], (x, y, 1 - c)).wait_send()

    return _Task([fin], [SDS(fin.shape, fin.dtype)], [(0, 0)], 1, start, finish)


def _t_allgather(buf):
    def peers():
        x, y, c, _ = _where()
        out = []
        for rel in range(1, N_DEV):
            px, py, pc = x ^ ((rel >> 2) & 1), y ^ ((rel >> 1) & 1), c ^ (rel & 1)
            out.append((rel - 1, 4 * px + 2 * py + pc, (px, py, pc)))
        return 4 * x + 2 * y + c, out

    def start(cin, cout, ss, rs, b):
        me, ps = peers()
        mine = cout[0].at[me]
        for k, _, dev in ps:
            _rcopy(mine, mine, ss.at[b + k], rs.at[b + k], dev).start()

    def finish(cin, cout, ss, rs, b):
        me, ps = peers()
        for k, pidx, dev in ps:
            got = cout[0].at[pidx]
            _rcopy(got, got, ss.at[b + k], rs.at[b + k], dev).wait_recv()
        for k, _, dev in ps:
            mine = cout[0].at[me]
            _rcopy(mine, mine, ss.at[b + k], rs.at[b + k], dev).wait_send()

    return _Task([buf], [SDS(buf.shape, buf.dtype)], [(0, 0)], N_DEV - 1, start, finish)


def _run_tasks(comm, which, cin, cout, ss, rs):
    i0 = o0 = s0 = 0
    for t in comm:
        getattr(t, which)(cin[i0:i0 + len(t.ins)], cout[o0:o0 + len(t.outs)], ss, rs, s0)
        i0, o0, s0 = i0 + len(t.ins), o0 + len(t.outs), s0 + t.n_sem


def _from_hbm(*arrays):
    return [pltpu.with_memory_space_constraint(a, pltpu.HBM) for a in arrays]


def _in_hbm(shapes):
    return [pltpu.HBM(s.shape, s.dtype) for s in shapes]


def _comm_layout(comm, n_in, n_out):
    c_in = [a for t in comm for a in t.ins]
    c_out = [s for t in comm for s in t.outs]
    aliases, i0, o0 = {}, 0, 0
    for t in comm:
        for i, o in t.alias:
            aliases[n_in + i0 + i] = n_out + o0 + o
        i0, o0 = i0 + len(t.ins), o0 + len(t.outs)
    return c_in, c_out, aliases, sum(t.n_sem for t in comm)


def _call(body, operands, *, name, grid, in_specs, out_specs, out_shape, scratch_shapes=(), sem=None, vmem_mib=None, comm=(),
          free=(), prefetch=()):
    operands = [o if s.memory_space == pltpu.SMEM or k in free else pltpu.with_memory_space_constraint(o, pltpu.HBM)
                for k, (o, s) in enumerate(zip(operands, in_specs))]
    n_pre, n_in, n_out, n_scr = len(prefetch), len(in_specs), len(out_specs), len(scratch_shapes)
    c_in, c_out, aliases, n_sem = _comm_layout(comm, n_pre + n_in, n_out)
    sems = [pltpu.SemaphoreType.DMA((n_sem,)), pltpu.SemaphoreType.DMA((n_sem,))] if comm else []

    def wrapped(*refs):
        pre, refs = refs[:n_pre], refs[n_pre:]
        ins, cin = refs[:n_in], refs[n_in:n_in + len(c_in)]
        rest = refs[n_in + len(c_in):]
        outs, cout = rest[:n_out], rest[n_out:n_out + len(c_out)]
        rest = rest[n_out + len(c_out):]
        scr, csem = rest[:n_scr], rest[n_scr:]
        if not comm:
            return body(*pre, *ins, *outs, *scr)
        step = functools.reduce(lambda acc, k: acc * grid[k] + pl.program_id(k), range(len(grid)), 0)
        n_steps = math.prod(grid)
        def begin():
            _run_tasks(comm, "relay", cin, cout, *csem)
            _run_tasks(comm, "start", cin, cout, *csem)

        pl.when(step == 0)(begin)
        pl.when(step == min((n_steps + 3) // 4, n_steps - 1))(lambda: _run_tasks(comm, "middle", cin, cout, *csem))
        body(*pre, *ins, *outs, *scr)
        pl.when(step == n_steps - 1)(lambda: _run_tasks(comm, "finish", cin, cout, *csem))

    grid_spec = pltpu.PrefetchScalarGridSpec(
        num_scalar_prefetch=n_pre, grid=grid, in_specs=list(in_specs) + [ANY] * len(c_in),
        out_specs=list(out_specs) + [ANY] * len(c_out), scratch_shapes=list(scratch_shapes) + sems)
    return _pcall(
        wrapped, name=name, grid_spec=grid_spec, out_shape=_in_hbm(list(out_shape) + c_out), input_output_aliases=aliases,
        compiler_params=_params(("arbitrary",) * len(grid) if comm else sem, vmem_mib),
    )(*prefetch, *operands, *_from_hbm(*c_in))


def _comm_call(name, comm):
    c_in, c_out, aliases, n_sem = _comm_layout(comm, 0, 0)

    def body(*refs):
        cin, cout, (ss, rs) = refs[:len(c_in)], refs[len(c_in):len(c_in) + len(c_out)], refs[len(c_in) + len(c_out):]
        for phase in ("relay", "start", "middle", "finish"):
            _run_tasks(comm, phase, cin, cout, ss, rs)

    return _pcall(
        body, name=name, in_specs=[ANY] * len(c_in), out_specs=[ANY] * len(c_out), out_shape=_in_hbm(c_out),
        scratch_shapes=[pltpu.SemaphoreType.DMA((n_sem,)), pltpu.SemaphoreType.DMA((n_sem,))],
        input_output_aliases=aliases,
    )(*_from_hbm(*c_in))


def _inproj(x, g1, w_int, comm=()):
    tm = TM

    def body(x_ref, g_ref, w_ref, proj_ref, u_ref):
        xf = x_ref[...]
        r = lax.rsqrt(jnp.mean(xf * xf, axis=-1, keepdims=True) + EPS)
        u = (xf * r * g_ref[...]).astype(BF)
        u_ref[...] = u
        proj_ref[...] = _dot(u, w_ref[...], 1, 1)

    return _call(
        body, (x, g1, w_int), name="inproj", grid=(T // tm,),
        in_specs=[pl.BlockSpec((tm, D), lambda i: (i, 0)), pl.BlockSpec((1, D), lambda i: (0, 0)),
                  _resident((INW, D))],
        out_specs=[pl.BlockSpec((tm, INW), lambda i: (i, 0)), pl.BlockSpec((tm, D), lambda i: (i, 0))],
        out_shape=[SDS((T, INW), F32), SDS((T, D), BF)], sem=("parallel",), vmem_mib=40, comm=comm)


def _outproj(y, w_out, x, g2):
    tm = TM

    def body(y_ref, w_ref, x_ref, g_ref, h1_ref, u2_ref):
        h1 = x_ref[...] + _dot(y_ref[...], w_ref[...], 1, 0)
        h1_ref[...] = h1
        r = lax.rsqrt(jnp.mean(h1 * h1, axis=-1, keepdims=True) + EPS)
        u2_ref[...] = (h1 * r * g_ref[...]).astype(BF)

    return _call(
        body, (y, w_out, x, g2), name="outproj", grid=(T // tm,),
        in_specs=[pl.BlockSpec((tm, D), lambda i: (i, 0)), _resident((D, D)),
                  pl.BlockSpec((tm, D), lambda i: (i, 0)), pl.BlockSpec((1, D), lambda i: (0, 0))],
        out_specs=[pl.BlockSpec((tm, D), lambda i: (i, 0)), pl.BlockSpec((tm, D), lambda i: (i, 0))],
        out_shape=[SDS((T, D), F32), SDS((T, D), BF)], sem=("parallel",), vmem_mib=32)


def _ffn_up(u2, w_upt, comm=()):
    tm, tn = 1024, 512

    def body(u_ref, w_ref, o_ref):
        o_ref[...] = _dot(u_ref[...], w_ref[...], 1, 1).astype(BF)

    return _call(
        body, (u2, w_upt), name="ffn_up", grid=(T // tm, 2 * DFF // tn),
        in_specs=[pl.BlockSpec((tm, D), lambda i, j: (i, 0)), pl.BlockSpec((tn, D), lambda i, j: (j, 0))],
        out_specs=[pl.BlockSpec((tm, tn), lambda i, j: (i, j))], out_shape=[SDS((T, 2 * DFF), BF)],
        sem=("parallel", "parallel"), vmem_mib=32, comm=comm)


def _ffn_down(a, w_down, h1, tgt):
    tm = TM

    def body(a_ref, w_ref, h1_ref, t_ref, dh_ref, dhb_ref, l_ref):
        @pl.when(pl.program_id(0) == 0)
        def _():
            l_ref[...] = jnp.zeros_like(l_ref)

        h2 = h1_ref[...] + _dot(a_ref[...], w_ref[...], 1, 0)
        e = h2 - t_ref[...]
        dh = e * (1.0 / D)
        dh_ref[...] = dh
        dhb_ref[...] = dh.astype(BF)
        e2 = jnp.sum((e * e).reshape(tm // 8, 8, D), axis=0)
        acc = e2[:, 0:128]
        for k in range(1, D // 128):
            acc = acc + e2[:, k * 128:(k + 1) * 128]
        l_ref[...] += acc

    return _call(
        body, (a, w_down, h1, tgt), name="ffn_down", grid=(T // tm,),
        in_specs=[pl.BlockSpec((tm, DFF), lambda i: (i, 0)), _resident((DFF, D)),
                  pl.BlockSpec((tm, D), lambda i: (i, 0)), pl.BlockSpec((tm, D), lambda i: (i, 0))],
        out_specs=[pl.BlockSpec((tm, D), lambda i: (i, 0)), pl.BlockSpec((tm, D), lambda i: (i, 0)),
                   pl.BlockSpec((8, 128), lambda i: (0, 0))],
        out_shape=[SDS((T, D), F32), SDS((T, D), BF), SDS((8, 128), F32)], sem=("arbitrary",), vmem_mib=40)


def _bucket_table():
    q = np.arange(BLK, dtype=np.int32)[:, None]
    j = np.arange(2 * BLK, dtype=np.int32)[None, :]
    n = np.maximum(q + BLK - j, 0)
    nf = np.maximum(n, 1).astype(np.float32)
    max_exact = NBUCKET // 2
    large = max_exact + (np.log(nf / np.float32(max_exact)) / np.float32(math.log(BLK / max_exact))
                         * np.float32(NBUCKET - max_exact)).astype(np.int32)
    large = np.minimum(large, NBUCKET - 1)
    return np.where(n < max_exact, n, large).astype(np.int32)


def _band_bias_bwd(dbias, bucket, me):
    def body(me_ref, db_ref, bk_ref, o_ref):
        bk = bk_ref[...]
        for b in range(NBUCKET):
            m = bk == b
            for h in range(NH):
                v = jnp.where(m, db_ref[h * BLK:(h + 1) * BLK, :], 0.0)
                s = jnp.sum(jnp.sum(v, axis=1, keepdims=True), axis=0, keepdims=True)
                o_ref[0, h:h + 1, b:b + 1] = s

    grid_spec = pltpu.PrefetchScalarGridSpec(
        num_scalar_prefetch=1, grid=(1,),
        in_specs=[pl.BlockSpec((NH * BLK, 2 * BLK), lambda i, me_ref: (0, 0)),
                  pl.BlockSpec((BLK, 2 * BLK), lambda i, me_ref: (0, 0))],
        out_specs=pl.BlockSpec((1, NH, NBUCKET), lambda i, me_ref: (me_ref[0], 0, 0)),
    )
    return _pcall(body, name="band_bias_bwd", grid_spec=grid_spec, out_shape=SDS((N_DEV, NH, NBUCKET), F32),
                  compiler_params=_params(("arbitrary",)))(me, dbias, bucket)


def _two_bf16(x):
    hi = x.astype(BF)
    return hi, (x - hi.astype(F32)).astype(BF)


def _head_sums(x, seg):
    hi, lo = _two_bf16(x)
    s = seg[0:x.shape[1], :]
    return _dot(hi, s, 1, 0) + _dot(lo, s, 1, 0)


def _head_spread(v, seg, width):
    hi, lo = _two_bf16(v)
    s = seg[0:width, :]
    return _dot(hi, s, 1, 1) + _dot(lo, s, 1, 1)


def _head_norm(x, g_t, seg):
    r = lax.rsqrt(_head_sums(x * x, seg) * (1.0 / HD) + EPS)
    r = _head_spread(r, seg, x.shape[1])
    return x * r * g_t, r


def _head_norm_bwd(dy, x, r, g_t, seg):
    dg_t = jnp.sum(dy * (x * r), axis=0, keepdims=True)
    dgx = dy * g_t
    mean = _head_spread(_head_sums(x * dgx, seg) * (1.0 / HD), seg, x.shape[1])
    return r * dgx - x * (r * r * r) * mean, dg_t


def _fold_heads(v):
    out = v[:, 0:HD]
    for h in range(1, v.shape[1] // HD):
        out = out + v[:, h * HD:(h + 1) * HD]
    return out


def _mix_forward(P, zc8, zh8, pkv, first, cw, qg_t, kg_t, gco, gao, seg, sink_ref, bias_ref):
    gate_b = P[:, 0:CW]
    gate_c = P[:, CW:2 * CW]
    hc = P[:, 2 * CW:3 * CW]
    z = gate_c * hc
    keep = jnp.where(first, 0.0, 1.0)
    zp = zc8 * zh8 * keep
    p1 = zp[7:8, :]
    p2 = zp[6:7, :]
    row = lax.broadcasted_iota(jnp.int32, (BLK, 1), 0)
    z1 = jnp.where(row == 0, p1, pltpu.roll(z, 1, 0))
    z2 = jnp.where(row == 0, p2, jnp.where(row == 1, p1, pltpu.roll(z, 2, 0)))
    cz = cw[0:1, :] * z2 + cw[1:2, :] * z1 + cw[2:3, :] * z
    y_conv = gate_b * cz

    scale = HD ** -0.5
    qi = lax.broadcasted_iota(jnp.int32, (GQ * BLK, 2 * BLK), 0) & (BLK - 1)
    kj = lax.broadcasted_iota(jnp.int32, (GQ * BLK, 2 * BLK), 1)
    dd = qi + BLK - kj
    first_key = jnp.where(first, BLK, 0)
    valid = (dd >= 0) & (dd < BLK) & (kj >= first_key)

    q0 = 3 * CW
    k0 = q0 + AW
    v0 = k0 + NKV * HD
    q_raw = P[:, q0:k0]
    qn, rq = _head_norm(q_raw, qg_t, seg)
    qs = (qn * scale).astype(BF)
    k_raw = jnp.concatenate([pkv[:, 0:NKV * HD], P[:, k0:v0]], axis=0)
    kn, rk = _head_norm(k_raw, kg_t, seg)
    knb = kn.astype(BF)
    heads = []
    outs = []
    for kv in range(NKV):
        kb = knb[:, kv * HD:(kv + 1) * HD]
        vb = jnp.concatenate([pkv[:, NKV * HD + kv * HD:NKV * HD + (kv + 1) * HD],
                              P[:, v0 + kv * HD:v0 + (kv + 1) * HD]], axis=0).astype(BF)
        Q = jnp.concatenate([qs[:, (kv * GQ + g) * HD:(kv * GQ + g + 1) * HD] for g in range(GQ)], axis=0)
        S = _dot(Q, kb, 1, 1) + bias_ref[kv * GQ * BLK:(kv + 1) * GQ * BLK, :]
        S = jnp.where(valid, S, NEG_INF)
        sink = jnp.concatenate([jnp.full((BLK, 1), sink_ref[0, kv * GQ + g], F32) for g in range(GQ)], axis=0)
        m = jnp.maximum(jnp.max(S, axis=-1, keepdims=True), sink)
        p = jnp.exp(S - m)
        es = jnp.exp(sink - m)
        denom = jnp.sum(p, axis=-1, keepdims=True) + es
        probs = p / denom
        O = _dot(probs.astype(BF), vb, 1, 0)
        heads.append(dict(kb=kb, vb=vb, Q=Q, probs=probs, psink=es / denom, O=O))
        outs += [O[g * BLK:(g + 1) * BLK, :] for g in range(GQ)]
    y_attn = jnp.concatenate(outs, axis=1)

    rc = lax.rsqrt(jnp.mean(y_conv * y_conv, axis=-1, keepdims=True) + EPS)
    ra = lax.rsqrt(jnp.mean(y_attn * y_attn, axis=-1, keepdims=True) + EPS)
    y = jnp.concatenate([y_conv * rc * gco, y_attn * ra * gao], axis=1)
    return dict(gate_b=gate_b, gate_c=gate_c, hc=hc, z=z, z1=z1, z2=z2, cz=cz, y_conv=y_conv, y_attn=y_attn,
                rc=rc, ra=ra, heads=heads, y=y, row=row, scale=scale, q_raw=q_raw, rq=rq, k_raw=k_raw, rk=rk)


BPS = 2
TILE = BPS * BLK
KV0 = 3 * CW + AW


def _mix_in_specs(tile_of):
    return [
        pl.BlockSpec(memory_space=pltpu.SMEM),
        pl.BlockSpec((TILE, INW), lambda s: (tile_of(s), 0)),
        pl.BlockSpec((8, CW), lambda s: (jnp.maximum(tile_of(s) * (TILE // 8) - 1, 0), 1)),
        pl.BlockSpec((8, CW), lambda s: (jnp.maximum(tile_of(s) * (TILE // 8) - 1, 0), 2)),
        pl.BlockSpec((BLK, 2 * NKV * HD), lambda s: (jnp.maximum(tile_of(s) * BPS - 1, 0), KV0 // (2 * NKV * HD))),
    ]


def _block_inputs(tile, b, zc_ref, zh_ref, pkv_ref, first_tile):
    P = tile[b * BLK:(b + 1) * BLK, :]
    if b == 0:
        return P, zc_ref[...], zh_ref[...], pkv_ref[...], first_tile
    lo = b * BLK
    return P, tile[lo - 8:lo, CW:2 * CW], tile[lo - 8:lo, 2 * CW:3 * CW], tile[lo - BLK:lo, KV0:KV0 + 2 * NKV * HD], False


def _mix_param_specs():
    return [
        pl.BlockSpec((8, CW), lambda s: (0, 0)),
        pl.BlockSpec((1, AW), lambda s: (0, 0)),
        pl.BlockSpec((1, NKV * HD), lambda s: (0, 0)),
        pl.BlockSpec((1, CW), lambda s: (0, 0)),
        pl.BlockSpec((1, AW), lambda s: (0, 0)),
        pl.BlockSpec((AW, 128), lambda s: (0, 0)),
        pl.BlockSpec((NH * BLK, 2 * BLK), lambda s: (0, 0)),
    ]


def _mix_params(cw8, qg, kg, gco, gao, bias):
    seg = np.zeros((AW, 128), np.float32)
    seg[np.arange(AW), np.arange(AW) // HD] = 1.0
    return (cw8, jnp.tile(qg, (1, NH)), jnp.tile(kg, (1, NKV)), gco, gao, jnp.asarray(seg, BF), bias)


def _mix_fwd(proj, sinks, cw8, qg, kg, gco, gao, bias, comm=()):
    def body(sink_ref, p_ref, zc_ref, zh_ref, pkv_ref, cw_ref, qg_ref, kg_ref, gco_ref, gao_ref, seg_ref, bias_ref, y_ref):
        tile = p_ref[...]
        for b in range(BPS):
            f = _mix_forward(*_block_inputs(tile, b, zc_ref, zh_ref, pkv_ref, pl.program_id(0) == 0), cw_ref[...],
                             qg_ref[...], kg_ref[...], gco_ref[...], gao_ref[...], seg_ref[...], sink_ref, bias_ref)
            y_ref[b * BLK:(b + 1) * BLK, :] = f["y"].astype(BF)

    return _call(
        body, (sinks, proj, proj, proj, proj, *_mix_params(cw8, qg, kg, gco, gao, bias)), name="mix_fwd", grid=(T // TILE,),
        in_specs=_mix_in_specs(lambda s: s) + _mix_param_specs(),
        out_specs=[pl.BlockSpec((TILE, D), lambda s: (s, 0))], out_shape=[SDS((T, D), BF)],
        sem=("parallel",), vmem_mib=40, comm=comm)


def _mix_bwd(proj, dy, sinks, cw8, qg, kg, gco, gao, bias, comm=()):
    n_steps = T // TILE

    def tile_of(s):
        return n_steps - 1 - s

    def body(sink_ref, p_ref, zc_ref, zh_ref, pkv_ref, dy_ref, cw_ref, qg_ref, kg_ref, gco_ref, gao_ref, seg_ref, bias_ref,
             dproj_ref, dcw_ref, dqg_ref, dkg_ref, dgco_ref, dgao_ref, dsink_ref, dbias_ref,
             ndcz_ref, dkc_ref, dvc_ref):
        s = pl.program_id(0)

        @pl.when(s == 0)
        def _():
            for r in (dcw_ref, dqg_ref, dkg_ref, dgco_ref, dgao_ref, dsink_ref, dbias_ref, ndcz_ref, dkc_ref, dvc_ref):
                r[...] = jnp.zeros_like(r)

        tile = p_ref[...]
        for b in reversed(range(BPS)):
            one_block(b, _block_inputs(tile, b, zc_ref, zh_ref, pkv_ref, s == n_steps - 1),
                      dy_ref[b * BLK:(b + 1) * BLK, :], sink_ref, cw_ref, qg_ref, kg_ref, gco_ref, gao_ref, seg_ref, bias_ref,
                      dproj_ref.at[b * BLK:(b + 1) * BLK, :], dcw_ref, dqg_ref, dkg_ref, dgco_ref, dgao_ref, dsink_ref,
                      dbias_ref, ndcz_ref, dkc_ref, dvc_ref)

    def one_block(b, inputs, dy, sink_ref, cw_ref, qg_ref, kg_ref, gco_ref, gao_ref, seg_ref, bias_ref,
                  dproj_ref, dcw_ref, dqg_ref, dkg_ref, dgco_ref, dgao_ref, dsink_ref, dbias_ref,
                  ndcz_ref, dkc_ref, dvc_ref):
        cw = cw_ref[...]
        qg_v, kg_v, gco_v, gao_v, seg = qg_ref[...], kg_ref[...], gco_ref[...], gao_ref[...], seg_ref[...]
        f = _mix_forward(*inputs, cw, qg_v, kg_v, gco_v, gao_v, seg, sink_ref, bias_ref)
        dyc, dgco = _rms_bwd(dy[:, 0:CW], f["y_conv"], f["rc"], gco_v)
        dya, dgao = _rms_bwd(dy[:, CW:CW + AW], f["y_attn"], f["ra"], gao_v)
        dgco_ref[...] += dgco
        dgao_ref[...] += dgao

        row = f["row"]
        dgate_b = dyc * f["cz"]
        dcz = dyc * f["gate_b"]
        dcw_ref[0:1, :] += jnp.sum(dcz * f["z2"], axis=0, keepdims=True)
        dcw_ref[1:2, :] += jnp.sum(dcz * f["z1"], axis=0, keepdims=True)
        dcw_ref[2:3, :] += jnp.sum(dcz * f["z"], axis=0, keepdims=True)
        nxt = ndcz_ref[...]
        n0 = nxt[0:1, :]
        n1 = nxt[1:2, :]
        d1 = jnp.where(row == BLK - 1, n0, pltpu.roll(dcz, BLK - 1, 0))
        d2 = jnp.where(row == BLK - 1, n1, jnp.where(row == BLK - 2, n0, pltpu.roll(dcz, BLK - 2, 0)))
        dz = cw[2:3, :] * dcz + cw[1:2, :] * d1 + cw[0:1, :] * d2
        ndcz_ref[...] = dcz[0:8, :]
        dproj_ref[:, 0:CW] = dgate_b.astype(BF)
        dproj_ref[:, CW:2 * CW] = (dz * f["hc"]).astype(BF)
        dproj_ref[:, 2 * CW:3 * CW] = (dz * f["gate_c"]).astype(BF)

        scale = f["scale"]
        lane = lax.broadcasted_iota(jnp.int32, (1, 128), 1)
        dq_cols, dk_cols, dv_cols = [], [], []
        for kv in range(NKV):
            hd = f["heads"][kv]
            dO = jnp.concatenate([dya[:, (kv * GQ + g) * HD:(kv * GQ + g + 1) * HD] for g in range(GQ)], axis=0)
            delta = jnp.sum(dO * hd["O"], axis=-1, keepdims=True)
            dOb = dO.astype(BF)
            dP = _dot(dOb, hd["vb"], 1, 1)
            dS = hd["probs"] * (dP - delta)
            dsk = hd["psink"] * delta
            for g in range(GQ):
                h = kv * GQ + g
                tot = jnp.sum(dsk[g * BLK:(g + 1) * BLK, :], axis=0, keepdims=True)
                dsink_ref[...] -= jnp.where(lane == h, tot, 0.0)
            dbias_ref[kv * GQ * BLK:(kv + 1) * GQ * BLK, :] += dS
            dSb = dS.astype(BF)
            dQ = _dot(dSb, hd["kb"], 1, 0)
            dKb = _dot(dSb, hd["Q"], 0, 0)
            dVb = _dot(hd["probs"].astype(BF), dOb, 0, 0)
            dk_cols.append(dKb[BLK:, :] + dkc_ref[:, kv * HD:(kv + 1) * HD])
            dv_cols.append(dVb[BLK:, :] + dvc_ref[:, kv * HD:(kv + 1) * HD])
            dkc_ref[:, kv * HD:(kv + 1) * HD] = dKb[:BLK, :]
            dvc_ref[:, kv * HD:(kv + 1) * HD] = dVb[:BLK, :]
            dq_cols += [dQ[g * BLK:(g + 1) * BLK, :] for g in range(GQ)]
        dq_raw, dqg_t = _head_norm_bwd(jnp.concatenate(dq_cols, axis=1) * scale, f["q_raw"], f["rq"], qg_v, seg)
        dk_raw, dkg_t = _head_norm_bwd(jnp.concatenate(dk_cols, axis=1), f["k_raw"][BLK:, :], f["rk"][BLK:, :], kg_v, seg)
        dqg_ref[...] += _fold_heads(dqg_t)
        dkg_ref[...] += _fold_heads(dkg_t)
        dproj_ref[:, 3 * CW:INW] = jnp.concatenate([dq_raw, dk_raw] + dv_cols, axis=1).astype(BF)

    small = lambda r, c: pl.BlockSpec((r, c), lambda s: (0, 0))
    return _call(
        body, (sinks, proj, proj, proj, proj, dy, *_mix_params(cw8, qg, kg, gco, gao, bias)), name="mix_bwd", grid=(n_steps,),
        in_specs=_mix_in_specs(tile_of) + [pl.BlockSpec((TILE, D), lambda s: (tile_of(s), 0))] + _mix_param_specs(),
        out_specs=[pl.BlockSpec((TILE, INW), lambda s: (tile_of(s), 0)), small(8, CW), small(1, HD), small(1, HD),
                   small(1, CW), small(1, AW), small(1, 128), small(NH * BLK, 2 * BLK)],
        out_shape=[SDS((T, INW), BF), SDS((8, CW), F32), SDS((1, HD), F32), SDS((1, HD), F32), SDS((1, CW), F32),
                   SDS((1, AW), F32), SDS((1, 128), F32), SDS((NH * BLK, 2 * BLK), F32)],
        scratch_shapes=[pltpu.VMEM((8, CW), F32), pltpu.VMEM((BLK, NKV * HD), F32), pltpu.VMEM((BLK, NKV * HD), F32)],
        sem=("arbitrary",), vmem_mib=56, comm=comm)


FT = 256
NFT = DFF // FT
RC = 128
NCH = T // RC
LEAD = 16


def _rows8(x):
    return jnp.sum(x.reshape(x.shape[0] // 8, 8, x.shape[1]), axis=0)


def _ffn_act_specs():
    return [
        pl.BlockSpec((T, FT), lambda j: (0, j)), pl.BlockSpec((T, FT), lambda j: (0, NFT + j)),
        pl.BlockSpec((8, FT), lambda j: (0, j)), pl.BlockSpec((8, FT), lambda j: (0, NFT + j)),
        pl.BlockSpec((1, FT), lambda j: (0, j)), pl.BlockSpec((1, FT), lambda j: (0, NFT + j)),
    ]


def _conv_rows(win, w, b, n):
    win = win.astype(F32)
    u = win[LEAD:LEAD + n]
    u1 = pltpu.roll(win, 1, 0)[LEAD:LEAD + n]
    u2 = pltpu.roll(win, 2, 0)[LEAD:LEAD + n]
    return u2, u1, u, w[0:1, :] * u2 + w[1:2, :] * u1 + w[2:3, :] * u + b


def _ffn_act(up, fw8, fb):
    def body(ug_ref, uv_ref, wg_ref, wv_ref, bg_ref, bv_ref, a_ref):
        wg, wv, bg, bv = wg_ref[...], wv_ref[...], bg_ref[...], bv_ref[...]

        def chunk(win_g, win_v):
            gp = _conv_rows(win_g, wg, bg, RC)[3]
            vp = _conv_rows(win_v, wv, bv, RC)[3]
            return (gp * jax.nn.sigmoid(gp) * vp).astype(BF)

        zero = jnp.zeros((LEAD, FT), BF)
        a_ref[0:RC, :] = chunk(jnp.concatenate([zero, ug_ref[0:RC, :]], axis=0),
                               jnp.concatenate([zero, uv_ref[0:RC, :]], axis=0))

        def step(i, carry):
            r0 = pl.multiple_of(i * RC, RC)
            win = pl.ds(r0 - LEAD, RC + LEAD)
            a_ref[pl.ds(r0, RC), :] = chunk(ug_ref[win, :], uv_ref[win, :])
            return carry

        lax.fori_loop(1, NCH, step, 0)

    return _call(
        body, (up, up, fw8, fw8, fb, fb), name="ffn_act", grid=(NFT,), in_specs=_ffn_act_specs(),
        out_specs=[pl.BlockSpec((T, FT), lambda j: (0, j))], out_shape=[SDS((T, DFF), BF)],
        sem=("parallel",), vmem_mib=40)


def _ffn_act_bwd(up, da, fw8, fb, comm=()):
    ext = RC + LEAD

    def body(ug_ref, uv_ref, wg_ref, wv_ref, bg_ref, bv_ref, da_ref,
             dug_ref, duv_ref, dwg_ref, dwv_ref, dbg_ref, dbv_ref):
        wg, wv, bg, bv = wg_ref[...], wv_ref[...], bg_ref[...], bv_ref[...]

        def chunk(win_g, win_v, da_e):
            g2, g1, g0, gp = _conv_rows(win_g, wg, bg, ext)
            v2, v1, v0, vp = _conv_rows(win_v, wv, bv, ext)
            da_e = da_e.astype(F32)
            sig = jax.nn.sigmoid(gp)
            dvp = da_e * (gp * sig)
            dgp = da_e * vp * (sig * (1.0 + gp * (1.0 - sig)))

            def back(dp, w):
                return (w[2:3, :] * dp[0:RC] + w[1:2, :] * pltpu.roll(dp, ext - 1, 0)[0:RC]
                        + w[0:1, :] * pltpu.roll(dp, ext - 2, 0)[0:RC]).astype(BF)

            def sums(dp, u2, u1, u0):
                d = dp[0:RC]
                return [_rows8(d), _rows8(d * u2[0:RC]), _rows8(d * u1[0:RC]), _rows8(d * u0[0:RC])]

            return back(dgp, wg), back(dvp, wv), sums(dgp, g2, g1, g0) + sums(dvp, v2, v1, v0)

        zero = jnp.zeros((LEAD, FT), BF)
        dug, duv, acc = chunk(jnp.concatenate([zero, ug_ref[0:ext, :]], axis=0),
                              jnp.concatenate([zero, uv_ref[0:ext, :]], axis=0), da_ref[0:ext, :])
        dug_ref[0:RC, :] = dug
        duv_ref[0:RC, :] = duv

        def step(i, acc):
            r0 = pl.multiple_of(i * RC, RC)
            win = pl.ds(r0 - LEAD, ext + LEAD)
            dug, duv, part = chunk(ug_ref[win, :], uv_ref[win, :], da_ref[pl.ds(r0, ext), :])
            dug_ref[pl.ds(r0, RC), :] = dug
            duv_ref[pl.ds(r0, RC), :] = duv
            return [a + p for a, p in zip(acc, part)]

        acc = lax.fori_loop(1, NCH - 1, step, acc)
        r0 = T - RC
        tail = lambda ref, lo: jnp.concatenate([ref[lo:T, :], zero], axis=0)
        dug, duv, part = chunk(tail(ug_ref, r0 - LEAD), tail(uv_ref, r0 - LEAD), tail(da_ref, r0))
        dug_ref[r0:T, :] = dug
        duv_ref[r0:T, :] = duv
        tot = [jnp.sum(a + p, axis=0, keepdims=True) for a, p in zip(acc, part)]
        for k, (dw_ref, db_ref) in enumerate(((dwg_ref, dbg_ref), (dwv_ref, dbv_ref))):
            db_ref[...] = tot[4 * k]
            dw_ref[...] = jnp.zeros_like(dw_ref)
            for r in range(3):
                dw_ref[r:r + 1, :] = tot[4 * k + 1 + r]

    col = lambda r: pl.BlockSpec((r, FT), lambda j: (0, j))
    return _call(
        body, (up, up, fw8, fw8, fb, fb, da), name="ffn_act_bwd", grid=(NFT,),
        in_specs=_ffn_act_specs() + [pl.BlockSpec((T, FT), lambda j: (0, j))],
        out_specs=[col(T), col(T), col(8), col(8), col(1), col(1)],
        out_shape=[SDS((T, DFF), BF), SDS((T, DFF), BF), SDS((8, DFF), F32), SDS((8, DFF), F32),
                   SDS((1, DFF), F32), SDS((1, DFF), F32)],
        sem=("parallel",), vmem_mib=40, comm=comm)


def _ffn_down_bwd(dh2b, w_down, comm=()):
    tm = TM

    def body(d_ref, w_ref, o_ref):
        o_ref[...] = _dot(d_ref[...], w_ref[...], 1, 1).astype(BF)

    return _call(
        body, (dh2b, w_down), name="ffn_down_bwd", grid=(T // tm,),
        in_specs=[pl.BlockSpec((tm, D), lambda i: (i, 0)), _resident((DFF, D))],
        out_specs=[pl.BlockSpec((tm, DFF), lambda i: (i, 0))], out_shape=[SDS((T, DFF), BF)],
        sem=("parallel",), vmem_mib=40, comm=comm)


def _norm_matmul_bwd(name, a_list, w_t, k_offsets, xin, g, dres, want_bf16, comm=(), slot=None):
    tm = TM
    ks = [a.shape[1] for a in a_list]
    n_a = len(a_list)
    n_pre = 0 if slot is None else 1

    def body(*refs):
        refs = refs[n_pre:]
        a_refs = refs[:n_a]
        w_ref, x_ref, g_ref, r_ref = refs[n_a:n_a + 4]
        outs = refs[n_a + 4:]
        dx_ref, dg_ref = outs[0], (outs[-1] if slot is None else outs[-1].at[0])

        @pl.when(pl.program_id(0) == 0)
        def _():
            dg_ref[...] = jnp.zeros_like(dg_ref)

        du = _dot(a_refs[0][...], w_ref[k_offsets[0]:k_offsets[0] + ks[0], :], 1, 0)
        for k in range(1, n_a):
            du = du + _dot(a_refs[k][...], w_ref[k_offsets[k]:k_offsets[k] + ks[k], :], 1, 0)
        x = x_ref[...]
        r = lax.rsqrt(jnp.mean(x * x, axis=-1, keepdims=True) + EPS)
        dx, dg = _rms_bwd(du, x, r, g_ref[...])
        dx = r_ref[...] + dx
        dx_ref[...] = dx
        if want_bf16:
            outs[1][...] = dx.astype(BF)
        dg_ref[...] += dg

    tile = lambda c: pl.BlockSpec((tm, c), lambda i, *_: (i, 0))
    if slot is None:
        dg_spec, dg_shape = pl.BlockSpec((1, D), lambda i: (0, 0)), SDS((1, D), F32)
    else:
        dg_spec, dg_shape = pl.BlockSpec((1, 1, D), lambda i, slot_ref: (slot_ref[0], 0, 0)), SDS((N_DEV, 1, D), F32)
    out_specs = [tile(D)] + ([tile(D)] if want_bf16 else []) + [dg_spec]
    out_shape = [SDS((T, D), F32)] + ([SDS((T, D), BF)] if want_bf16 else []) + [dg_shape]
    return _call(
        body, (*a_list, w_t, xin, g, dres), name=name, grid=(T // tm,), prefetch=() if slot is None else (slot,),
        in_specs=[tile(k) for k in ks] + [_resident(w_t.shape), tile(D),
                                           pl.BlockSpec((1, D), lambda i, *_: (0, 0)), tile(D)],
        out_specs=out_specs, out_shape=out_shape, sem=("arbitrary",), vmem_mib=56, comm=comm)


def _out_bwd(dh1b, w_out, comm=()):
    tm = TM

    def body(d_ref, w_ref, o_ref):
        o_ref[...] = _dot(d_ref[...], w_ref[...], 1, 1)

    return _call(
        body, (dh1b, w_out), name="out_bwd", grid=(T // tm,),
        in_specs=[pl.BlockSpec((tm, D), lambda i: (i, 0)), _resident((D, D))],
        out_specs=[pl.BlockSpec((tm, D), lambda i: (i, 0))], out_shape=[SDS((T, D), F32)],
        sem=("parallel",), vmem_mib=32, comm=comm)


def _wgrad(name, a_list, b, comm=()):
    m_k = a_list[0].shape[1]
    tm = max(t for t in range(128, m_k // 2 + 1, 128) if m_k % t == 0)
    steps = [a.shape[1] // tm for a in a_list]
    starts = [sum(steps[:k]) for k in range(len(a_list))]
    n_a = len(a_list)

    def body(*refs):
        a_refs, b_ref, o_ref = refs[:n_a], refs[n_a], refs[n_a + 1]
        i = pl.program_id(0)
        for k in range(n_a):
            @pl.when((i >= starts[k]) & (i < starts[k] + steps[k]))
            def _(k=k):
                o_ref[...] = _dot(a_refs[k][...], b_ref[...], 0, 0).astype(BF)

    def a_spec(k):
        return pl.BlockSpec((T, tm), lambda i: (0, jnp.clip(i - starts[k], 0, steps[k] - 1)))

    m_total = tm * sum(steps)
    return _call(
        body, (*a_list, b), name=name, grid=(sum(steps),),
        in_specs=[a_spec(k) for k in range(n_a)] + [_resident((T, D))],
        out_specs=[pl.BlockSpec((tm, D), lambda i: (i, 0))], out_shape=[SDS((m_total, D), BF)],
        sem=("parallel",), vmem_mib=40, comm=comm)


def _chip_sum(name, gbf, from_sib, core, chip):
    h = gbf.shape[1]
    th = h // 2

    def body(core_ref, chip_ref, g_ref, s_ref, pbf_ref, own_ref):
        p = g_ref[0].astype(F32) + s_ref[0].astype(F32)
        pbf_ref[0] = p.astype(BF)

        @pl.when(pl.program_id(1) == chip_ref[0])
        def _():
            own_ref[...] = p

    grid_spec = pltpu.PrefetchScalarGridSpec(
        num_scalar_prefetch=2, grid=(h // th, N_CHIPS),
        in_specs=[pl.BlockSpec((1, th, D), lambda t, jj, core_ref, chip_ref: (2 * jj + core_ref[0], t, 0)),
                  pl.BlockSpec((1, th, D), lambda t, jj, core_ref, chip_ref: (jj, t, 0))],
        out_specs=[pl.BlockSpec((1, th, D), lambda t, jj, core_ref, chip_ref: (jj, t, 0)),
                   pl.BlockSpec((th, D), lambda t, jj, core_ref, chip_ref: (t, 0))],
    )
    return _pcall(
        body, name=name, grid_spec=grid_spec, out_shape=_in_hbm([SDS((N_CHIPS, h, D), BF), SDS((h, D), F32)]),
        compiler_params=_params(("arbitrary", "arbitrary"), 32),
    )(core, chip, *_from_hbm(gbf, from_sib))


def _final_sum(name, own, from_chips, core):
    h = own.shape[0]

    def body(core_ref, o_ref, r_ref, f_ref):
        f_ref[0] = ((o_ref[...] + r_ref[0].astype(F32)) + r_ref[1].astype(F32)) + r_ref[2].astype(F32)

    grid_spec = pltpu.PrefetchScalarGridSpec(
        num_scalar_prefetch=1, grid=(1,),
        in_specs=[pl.BlockSpec((h, D), lambda i, core_ref: (0, 0)), pl.BlockSpec((3, h, D), lambda i, core_ref: (0, 0, 0))],
        out_specs=pl.BlockSpec((1, h, D), lambda i, core_ref: (core_ref[0], 0, 0)),
    )
    return _pcall(body, name=name, grid_spec=grid_spec, out_shape=pltpu.HBM((2, h, D), F32),
                  compiler_params=_params(("arbitrary",), 40))(core, *_from_hbm(own, from_chips))


def _adam_math(w, g, m, v):
    nm = ADAM_B1 * m + (1.0 - ADAM_B1) * g
    nv = ADAM_B2 * v + (1.0 - ADAM_B2) * (g * g)
    m_hat = nm / (1.0 - ADAM_B1 ** ADAM_STEP)
    v_hat = nv / (1.0 - ADAM_B2 ** ADAM_STEP)
    return -ADAM_LR * (m_hat / (jnp.sqrt(v_hat) + ADAM_EPS) + ADAM_WD * w), nm, nv


def _adamw(name, w, g, m, v, tr, copy_g=False, stage=True):
    rows, cols = w.shape

    def body(w_ref, g_ref, m_ref, v_ref, *outs):
        g_val = g_ref[...]
        if copy_g:
            outs[0][...] = g_val
        d_ref, nm_ref, nv_ref = outs[-3:]
        d_ref[...], nm_ref[...], nv_ref[...] = _adam_math(w_ref[...], g_val, m_ref[...], v_ref[...])

    spec = pl.BlockSpec((tr, cols), lambda i: (i, 0))
    n_out = 4 if copy_g else 3
    return _call(body, (w, g, m, v), name=name, grid=(rows // tr,), in_specs=[spec] * 4, out_specs=[spec] * n_out,
                 out_shape=[SDS((rows, cols), F32)] * n_out, sem=("parallel",), vmem_mib=32,
                 free=(0, 2, 3) if stage else ())


C_G1, C_G2, C_GCO, C_GAO, C_DCW, C_DQG, C_DKG, C_SINK, C_SQ = 0, 1024, 2048, 2560, 3072, 4608, 4736, 4864, 5632
P_W = C_SQ + 128


def _pack_small(me, dfwg, dfwv, dfbg, dfbv, dg2, dgco, dgao, dcw8, dqg, dkg, dsink, sq):
    def body(me_ref, dfwg_r, dfwv_r, dfbg_r, dfbv_r, dg2_r, dgco_r, dgao_r, dcw_r, dqg_r, dkg_r, dsink_r, sq_r, o):
        o[...] = jnp.zeros_like(o)
        o[0, :, 0:DFF] = dfwg_r[...]
        o[0, :, DFF:2 * DFF] = dfwv_r[...]
        o[0, 3:4, 0:DFF] = dfbg_r[...]
        o[0, 3:4, DFF:2 * DFF] = dfbv_r[...]
        o[0, 4:5, C_G2:C_G2 + D] = dg2_r[...]
        o[0, 4:5, C_GCO:C_GCO + CW] = dgco_r[...]
        o[0, 4:5, C_GAO:C_GAO + AW] = dgao_r[...]
        for r in range(3):
            o[0, 4:5, C_DCW + r * CW:C_DCW + (r + 1) * CW] = dcw_r[r:r + 1, :]
        o[0, 4:5, C_DQG:C_DQG + HD] = dqg_r[...]
        o[0, 4:5, C_DKG:C_DKG + HD] = dkg_r[...]
        o[0, 4:5, C_SINK:C_SINK + 128] = dsink_r[...]
        o[0, :, C_SQ:C_SQ + 128] = sq_r[...]

    ins = (dfwg, dfwv, dfbg, dfbv, dg2, dgco, dgao, dcw8, dqg, dkg, dsink, sq)
    return _call(body, ins, name="pack_small", grid=(1,), prefetch=(me,),
                 in_specs=[pl.BlockSpec(a.shape, lambda i, me_ref: (0, 0)) for a in ins],
                 out_specs=[pl.BlockSpec((1, 8, P_W), lambda i, me_ref: (me_ref[0], 0, 0))],
                 out_shape=[SDS((N_DEV, 8, P_W), F32)], sem=("arbitrary",))[0]


N_SMALL = 11


def _small_adam(chip, p_all, g1_all, tbl_all, ws, ms, vs):
    fw_cols = 2 * DFF // N_CHIPS
    cw_cols = CW // N_CHIPS

    def body(chip_ref, p_ref, fw_ref, cw0_ref, cw1_ref, cw2_ref, g1_ref, tbl_ref, *refs):
        w_r, m_r, v_r = refs[0:N_SMALL], refs[N_SMALL:2 * N_SMALL], refs[2 * N_SMALL:3 * N_SMALL]
        outs = refs[3 * N_SMALL:]
        g_o, d_o, nm_o, nv_o = (outs[k * N_SMALL:(k + 1) * N_SMALL] for k in range(4))
        loss_o = outs[4 * N_SMALL]

        def total(ref):
            s = ref[0]
            for k in range(1, N_DEV):
                s = s + ref[k]
            return s

        S = total(p_ref)
        fw = total(fw_ref)
        cws = [total(r) for r in (cw0_ref, cw1_ref, cw2_ref)]

        def step(i, g, at):
            d, nm, nv = _adam_math(w_r[i][at], g, m_r[i][at], v_r[i][at])
            g_o[i][at], d_o[i][at], nm_o[i][at], nv_o[i][at] = g, d, nm, nv

        everything = (slice(None), slice(None))
        step(0, total(g1_ref), everything)
        for r in range(3):
            step(1, cws[r][4:5, :], (r, slice(None), slice(None)))
        step(2, S[4:5, C_DQG:C_DQG + HD], everything)
        step(3, S[4:5, C_DKG:C_DKG + HD], everything)
        step(4, total(tbl_ref), everything)
        step(5, S[4:5, C_SINK:C_SINK + NH], everything)
        step(6, S[4:5, C_GCO:C_GCO + CW], everything)
        step(7, S[4:5, C_GAO:C_GAO + AW], everything)
        step(8, S[4:5, C_G2:C_G2 + D], everything)
        for r in range(3):
            step(9, fw[r:r + 1, :], (r, slice(None), slice(None)))
        step(10, S[3:4, 0:2 * DFF], everything)
        sq = S[:, C_SQ:C_SQ + 128]
        loss_o[...] = jnp.sum(jnp.sum(sq, axis=1, keepdims=True), axis=0, keepdims=True) * (0.5 / D)

    def full(a):
        n = len(a.shape)
        return pl.BlockSpec(a.shape, lambda i, chip_ref: (0,) * n)

    params = [*ws, *ms, *vs]
    out = _call(
        body, (p_all, p_all, p_all, p_all, p_all, g1_all, tbl_all, *params), name="small_adam", grid=(1,), prefetch=(chip,),
        in_specs=[full(p_all),
                  pl.BlockSpec((N_DEV, 8, fw_cols), lambda i, chip_ref: (0, 0, chip_ref[0])),
                  *[pl.BlockSpec((N_DEV, 8, cw_cols), lambda i, chip_ref, r=r: (0, 0, (C_DCW + r * CW) // cw_cols + chip_ref[0]))
                    for r in range(3)],
                  full(g1_all), full(tbl_all), *[full(a) for a in params]],
        out_specs=[full(a) for a in ws] * 4 + [pl.BlockSpec((1, 1), lambda i, chip_ref: (0, 0))],
        out_shape=[SDS(a.shape, F32) for a in ws] * 4 + [SDS((1, 1), F32)], sem=("arbitrary",), vmem_mib=32)
    return out[0:N_SMALL], out[N_SMALL:2 * N_SMALL], out[2 * N_SMALL:3 * N_SMALL], out[3 * N_SMALL:4 * N_SMALL], out[4 * N_SMALL]


PLACE_STEPS = 4


def _place_specs(shards):
    rows = [s.shape[0] // PLACE_STEPS for s in shards]
    return ([pl.BlockSpec((r, D), lambda i, chip_ref: (i, 0)) for r in rows],
            [pl.BlockSpec((r, D), lambda i, chip_ref: (chip_ref[0] * PLACE_STEPS + i, 0)) for r in rows],
            [SDS((N_CHIPS * s.shape[0], D), BF) for s in shards])


def _place_first(chip, shard, conv_w, ffn_conv_w):
    def body(chip_ref, a, s0, s1, o, t0, t1):
        o[...] = a[...].astype(BF)

        @pl.when(pl.program_id(0) == 0)
        def _():
            for s, t in ((s0, t0), (s1, t1)):
                t[...] = jnp.zeros_like(t)
                t[0, 0:3, :] = s[...]

    ins, outs, shapes = _place_specs([shard])
    taps = (conv_w, ffn_conv_w)
    return _call(
        body, (shard, conv_w, ffn_conv_w), name="place_first", grid=(PLACE_STEPS,), prefetch=(chip,),
        in_specs=ins + [pl.BlockSpec(s.shape, lambda i, chip_ref: (0, 0)) for s in taps],
        out_specs=outs + [pl.BlockSpec((1, 8, s.shape[1]), lambda i, chip_ref: (chip_ref[0], 0, 0)) for s in taps],
        out_shape=shapes + [SDS((N_CHIPS, 8, s.shape[1]), F32) for s in taps],
        sem=("arbitrary",), vmem_mib=32, free=(0, 1, 2))


def _place_rest(chip, shards, table, bucket, comm):
    n = len(shards)

    def body(chip_ref, *refs):
        a, (tab_ref, bk_ref), o, bias_ref = refs[:n], refs[n:n + 2], refs[n + 2:2 * n + 2], refs[2 * n + 2]
        for src, dst in zip(a, o):
            dst[...] = src[...].astype(BF)

        @pl.when(pl.program_id(0) == 0)
        def _():
            bk = bk_ref[...]
            eq = [bk == b for b in range(NBUCKET)]
            for h in range(NH):
                acc = jnp.zeros((BLK, 2 * BLK), F32)
                for b in range(NBUCKET):
                    acc = jnp.where(eq[b], tab_ref[h, b], acc)
                bias_ref[h * BLK:(h + 1) * BLK, :] = acc

    ins, outs, shapes = _place_specs(shards)
    return _call(
        body, (*shards, table, bucket), name="place_rest", grid=(PLACE_STEPS,), prefetch=(chip,),
        in_specs=ins + [pl.BlockSpec(memory_space=pltpu.SMEM), pl.BlockSpec(bucket.shape, lambda i, chip_ref: (0, 0))],
        out_specs=outs + [pl.BlockSpec((NH * BLK, 2 * BLK), lambda i, chip_ref: (0, 0))],
        out_shape=shapes + [SDS((NH * BLK, 2 * BLK), F32)],
        sem=("arbitrary",), vmem_mib=32, comm=comm, free=tuple(range(n + 2)))


def kernel(x, norm_mix_g, w_in, conv_w, q_norm_g, k_norm_g, rel_bias_table, sinks, out_norm_conv_g, out_norm_attn_g, w_out, norm_ffn_g, w_up, ffn_conv_w, ffn_conv_b, w_down, loss_target, m_norm_mix_g, m_w_in, m_conv_w, m_q_norm_g, m_k_norm_g, m_rel_bias_table, m_sinks, m_out_norm_conv_g, m_out_norm_attn_g, m_w_out, m_norm_ffn_g, m_w_up, m_ffn_conv_w, m_ffn_conv_b, m_w_down, v_norm_mix_g, v_w_in, v_conv_w, v_q_norm_g, v_k_norm_g, v_rel_bias_table, v_sinks, v_out_norm_conv_g, v_out_norm_attn_g, v_w_out, v_norm_ffn_g, v_w_up, v_ffn_conv_w, v_ffn_conv_b, v_w_down):
    as_arg = lambda i: jnp.reshape(i, (1,)).astype(jnp.int32)
    chip = as_arg(2 * lax.axis_index("x") + lax.axis_index("y"))
    core = as_arg(lax.axis_index("c"))
    me = 2 * chip + core
    xs, tgt = x[0], loss_target[0]
    qg, kg, gco, gao, g1, g2, fb = q_norm_g, k_norm_g, out_norm_conv_g, out_norm_attn_g, norm_mix_g, norm_ffn_g, ffn_conv_b
    pieces = lambda g: g.reshape(N_DEV, g.shape[0] // N_DEV, D)
    whole = lambda f: f.reshape(2 * f.shape[1], D)

    bucket = jnp.asarray(_bucket_table())
    p_in, p_cw, p_fw = _place_first(chip, w_in[0].T, conv_w[0], ffn_conv_w[0])
    p_out, p_up, p_down, bias, w_int, cw_all, fw_all = _place_rest(
        chip, [w_out[0], w_up[0].T, w_down[0]], rel_bias_table.T, bucket,
        comm=[_t_gather(p_in), _t_small_weights(p_cw), _t_small_weights(p_fw)])
    cw8 = jnp.transpose(cw_all, (1, 0, 2)).reshape(8, CW)
    fw8 = jnp.transpose(fw_all, (1, 0, 2)).reshape(8, 2 * DFF)

    proj, u1, w_out_f = _inproj(xs, g1, w_int, comm=[_t_gather(p_out)])
    y, w_upt = _mix_fwd(proj, sinks, cw8, qg, kg, gco, gao, bias, comm=[_t_gather(p_up)])
    h1, u2 = _outproj(y, w_out_f, xs, g2)
    up, w_down_f = _ffn_up(u2, w_upt, comm=[_t_gather(p_down)])
    a, = _ffn_act(up, fw8, fb)
    dh2, dh2b, sq = _ffn_down(a, w_down_f, h1, tgt)

    gdbf, = _wgrad("wgrad_down", [a], dh2b)
    da, sib_down = _ffn_down_bwd(dh2b, w_down_f, comm=[_t_sibling(pieces(gdbf))])
    pbf_down, own_down = _chip_sum("chip_sum_w_down", pieces(gdbf), sib_down, core, chip)
    dug, duv, dfwg, dfwv, dfbg, dfbv, chips_down, _ = _ffn_act_bwd(up, da, fw8, fb, comm=[_t_chips(pbf_down)])
    fin_down = _final_sum("final_sum_w_down", own_down, chips_down, core)
    gubf, = _wgrad("wgrad_up", [dug, duv], u2)
    dh1, dh1b, dg2, sib_up, fin_down = _norm_matmul_bwd(
        "ffn_up_bwd", [dug, duv], w_upt, [0, DFF], h1, g2, dh2, True, comm=[_t_sibling(pieces(gubf)), _t_swap(fin_down)])
    pbf_up, own_up = _chip_sum("chip_sum_w_up", pieces(gubf), sib_up, core, chip)
    gobf, = _wgrad("wgrad_out", [y], dh1b)
    dy, sib_out = _out_bwd(dh1b, w_out_f, comm=[_t_sibling(pieces(gobf))])
    pbf_out, own_out = _chip_sum("chip_sum_w_out", pieces(gobf), sib_out, core, chip)
    dproj, dcw8, dqg, dkg, dgco, dgao, dsink, dbias, chips_up, _, chips_out, _ = _mix_bwd(
        proj, dy, sinks, cw8, qg, kg, gco, gao, bias, comm=[_t_chips(pbf_up), _t_chips(pbf_out)])
    fin_up = _final_sum("final_sum_w_up", own_up, chips_up, core)
    fin_out = _final_sum("final_sum_w_out", own_out, chips_out, core)
    tbl_all = _band_bias_bwd(dbias, bucket, me)
    p_all = _pack_small(me, dfwg, dfwv, dfbg, dfbv, dg2, dgco, dgao, dcw8, dqg, dkg, dsink, sq)
    gibf, fin_up, p_all, tbl_all = _wgrad(
        "wgrad_in", [dproj], u1, comm=[_t_swap(fin_up), _t_allgather(p_all), _t_allgather(tbl_all)])
    sib_in, fin_out = _comm_call("to_sibling_last", [_t_sibling(pieces(gibf)), _t_swap(fin_out)])
    pbf_in, own_in = _chip_sum("chip_sum_w_in", pieces(gibf), sib_in, core, chip)
    dx, g1_all, chips_in, _ = _norm_matmul_bwd(
        "in_bwd", [dproj], w_int, [0], xs, g1, dh1, False, comm=[_t_chips(pbf_in)], slot=me)
    fin_in = _final_sum("final_sum_w_in", own_in, chips_in, core)
    g1_all, fin_in = _comm_call("gather_last", [_t_allgather(g1_all), _t_swap(fin_in)])

    g_w_out, g_w_up, g_w_down = whole(fin_out), whole(fin_up).T, whole(fin_down)
    g_w_down, d_down, nm_down, nv_down = _adamw("adamw_w_down", w_down[0], g_w_down, m_w_down[0], v_w_down[0], 352, True)
    d_up, nm_up, nv_up = _adamw("adamw_w_up", w_up[0], g_w_up, m_w_up[0], v_w_up[0], 256, stage=False)
    g_w_out, d_out, nm_out, nv_out = _adamw("adamw_w_out", w_out[0], g_w_out, m_w_out[0], v_w_out[0], 256, True)
    g_w_in, d_in, nm_in, nv_in = [a.T for a in _adamw(
        "adamw_w_in", w_in[0].T, whole(fin_in), m_w_in[0].T, v_w_in[0].T, INW // N_CHIPS // 3, True)]
    taps = lambda a: jnp.transpose(a, (1, 0, 2))
    sw = [norm_mix_g, taps(conv_w), q_norm_g, k_norm_g, rel_bias_table.T, sinks, out_norm_conv_g, out_norm_attn_g,
          norm_ffn_g, taps(ffn_conv_w), ffn_conv_b]
    smm = [m_norm_mix_g, taps(m_conv_w), m_q_norm_g, m_k_norm_g, m_rel_bias_table.T, m_sinks, m_out_norm_conv_g,
           m_out_norm_attn_g, m_norm_ffn_g, taps(m_ffn_conv_w), m_ffn_conv_b]
    smv = [v_norm_mix_g, taps(v_conv_w), v_q_norm_g, v_k_norm_g, v_rel_bias_table.T, v_sinks, v_out_norm_conv_g,
           v_out_norm_attn_g, v_norm_ffn_g, taps(v_ffn_conv_w), v_ffn_conv_b]
    *small_out, loss = _small_adam(chip, p_all, g1_all, tbl_all, sw, smm, smv)
    sg, sd, snm, snv = [list(r) for r in small_out]
    for r in (sg, sd, snm, snv):
        r[1], r[4], r[9] = taps(r[1]), r[4].T, taps(r[9])

    def order(s, b_in, b_out, b_up, b_down):
        return (s[0], b_in[None], s[1], s[2], s[3], s[4], s[5], s[6], s[7], b_out[None], s[8], b_up[None],
                s[9], s[10], b_down[None])

    return (loss.reshape(()), dx[None],
            *order(sg, g_w_in, g_w_out, g_w_up, g_w_down),
            *order(sd, d_in, d_out, d_up, d_down),
            *order(snm, nm_in, nm_out, nm_up, nm_down),
            *order(snv, nv_in, nv_out, nv_up, nv_down))
```

```python
import functools
import math

import numpy as np

import jax
import jax.numpy as jnp
from jax import lax
from jax.experimental import pallas as pl
from jax.experimental.pallas import tpu as pltpu

F32 = jnp.float32
BF = jnp.bfloat16
SDS = jax.ShapeDtypeStruct

T = 2048
D = 1024
CW = 512
AW = 512
HD = 64
NH = 8
NKV = 2
GQ = 4
INW = 2304
DFF = 2816
BLK = 128
NB = T // BLK
NBUCKET = 32
EPS = 1e-6
NEG_INF = -1e30
N_CHIPS = 4
N_DEV = 8

ADAM_LR = 0.001
ADAM_B1 = 0.9
ADAM_B2 = 0.999
ADAM_EPS = 1e-08
ADAM_WD = 0.01
ADAM_STEP = 10

TM = 512
MIB = 1024 * 1024
MESH = pl.DeviceIdType.MESH
ANY = pl.BlockSpec(memory_space=pl.ANY)

_pcall = pl.pallas_call


def _params(sem=None, vmem_mib=None):
    kw = {}
    if sem is not None:
        kw["dimension_semantics"] = sem
    if vmem_mib is not None:
        kw["vmem_limit_bytes"] = vmem_mib * MIB
    return pltpu.CompilerParams(**kw)


def _resident(shape):
    return pl.BlockSpec(shape, lambda *_: (0,) * len(shape), pipeline_mode=pl.Buffered(1))


def _dot(a, b, ca, cb):
    return lax.dot_general(a, b, (((ca,), (cb,)), ((), ())), preferred_element_type=F32)


def _rms_bwd(dy, x, r, g):
    dg = jnp.sum(dy * (x * r), axis=0, keepdims=True)
    dgx = dy * g
    dx = r * dgx - x * (r * r * r) * jnp.mean(x * dgx, axis=-1, keepdims=True)
    return dx, dg


def _where():
    x, y, c = lax.axis_index("x"), lax.axis_index("y"), lax.axis_index("c")
    return x, y, c, [(1 - x, y), (x, 1 - y), (1 - x, 1 - y)]


def _rcopy(src, dst, ssem, rsem, dev):
    return pltpu.make_async_remote_copy(src_ref=src, dst_ref=dst, send_sem=ssem, recv_sem=rsem, device_id=dev,
                                        device_id_type=MESH)


class _Task:
    def __init__(self, ins, outs, alias, n_sem, start, finish, middle=None):
        self.ins, self.outs, self.alias, self.n_sem, self.start, self.finish = ins, outs, alias, n_sem, start, finish
        self.middle = middle if middle is not None else (lambda *args: None)


def _t_gather(placed):
    R = placed.shape[0] // N_CHIPS
    q = R // 4

    def quarter(chip_index, core, k):
        return pl.ds(pl.multiple_of(chip_index * R + core * 2 * q + k * q, 16), q)

    def half(chip_index, core):
        return pl.ds(pl.multiple_of(chip_index * R + core * 2 * q, 16), 2 * q)

    def places():
        x, y, c, _ = _where()
        return c, 2 * x + y, 2 * (1 - x) + y, 2 * x + (1 - y), 2 * (1 - x) + (1 - y), (1 - x, y, c), (x, 1 - y, c), (x, y, 1 - c)

    def start(cin, cout, ss, rs, b):
        c, me, _, _, _, x_nbr, y_nbr, _ = places()
        buf = cout[0]
        for k, (quart, dev) in enumerate(((0, x_nbr), (1, y_nbr), (1, x_nbr), (0, y_nbr))):
            part = buf.at[quarter(me, c, quart)]
            _rcopy(part, part, ss.at[b + k], rs.at[b + k], dev).start()

    def middle(cin, cout, ss, rs, b):
        c, _, xc, yc, _, x_nbr, y_nbr, _ = places()
        buf = cout[0]
        for k, chip_index, quart, dev in ((0, xc, 0, y_nbr), (1, yc, 1, x_nbr)):
            part = buf.at[quarter(chip_index, c, quart)]
            _rcopy(part, part, ss.at[b + k], rs.at[b + k], dev).wait_recv()
            _rcopy(part, part, ss.at[b + 4 + k], rs.at[b + 4 + k], dev).start()

    def finish(cin, cout, ss, rs, b):
        c, me, xc, yc, dc, x_nbr, y_nbr, sib = places()
        buf = cout[0]
        for k, chip_index, quart in ((2, xc, 1), (3, yc, 0), (4, dc, 0), (5, dc, 1)):
            part = buf.at[quarter(chip_index, c, quart)]
            _rcopy(part, part, ss.at[b + k], rs.at[b + k], sib).wait_recv()
        for k, chip_index in ((6, xc), (7, yc), (8, dc)):
            got = buf.at[half(chip_index, c)]
            _rcopy(got, got, ss.at[b + k], rs.at[b + k], sib).start()
        for k, chip_index in ((6, xc), (7, yc), (8, dc)):
            got = buf.at[half(chip_index, 1 - c)]
            _rcopy(got, got, ss.at[b + k], rs.at[b + k], sib).wait_recv()
        for k in range(6):
            part = buf.at[quarter(me, c, 0)]
            _rcopy(part, part, ss.at[b + k], rs.at[b + k], sib).wait_send()
        for k in range(6, 9):
            got = buf.at[half(me, c)]
            _rcopy(got, got, ss.at[b + k], rs.at[b + k], sib).wait_send()

    return _Task([placed], [SDS(placed.shape, placed.dtype)], [(0, 0)], 9, start, finish, middle)


def _t_small_weights(buf):
    def start(cin, cout, ss, rs, b):
        x, y, c, chips = _where()
        mine = cout[0].at[2 * x + y]
        for r, (px, py) in enumerate(chips):
            _rcopy(mine, mine, ss.at[b + r], rs.at[b + r], (px, py, c)).start()

    def finish(cin, cout, ss, rs, b):
        x, y, c, chips = _where()
        for r, (px, py) in enumerate(chips):
            got = cout[0].at[2 * px + py]
            _rcopy(got, got, ss.at[b + r], rs.at[b + r], (px, py, c)).wait_recv()
        for r, (px, py) in enumerate(chips):
            mine = cout[0].at[2 * x + y]
            _rcopy(mine, mine, ss.at[b + r], rs.at[b + r], (px, py, c)).wait_send()

    return _Task([buf], [SDS(buf.shape, buf.dtype)], [(0, 0)], 3, start, finish)


def _t_sibling(gbf):
    def start(cin, cout, ss, rs, b):
        x, y, c, _ = _where()
        for jj in range(N_CHIPS):
            _rcopy(cin[0].at[2 * jj + (1 - c)], cout[0].at[jj], ss.at[b + jj], rs.at[b + jj], (x, y, 1 - c)).start()

    def finish(cin, cout, ss, rs, b):
        x, y, c, _ = _where()
        for jj in range(N_CHIPS):
            got = cout[0].at[jj]
            _rcopy(got, got, ss.at[b + jj], rs.at[b + jj], (x, y, 1 - c)).wait_recv()
        for jj in range(N_CHIPS):
            got = cout[0].at[jj]
            _rcopy(got, got, ss.at[b + jj], rs.at[b + jj], (x, y, 1 - c)).wait_send()

    return _Task([gbf], [SDS((N_CHIPS,) + gbf.shape[1:], BF)], [], N_CHIPS, start, finish)


def _t_chips(pbf):
    def start(cin, cout, ss, rs, b):
        x, y, c, chips = _where()
        for r, (px, py) in enumerate(chips):
            _rcopy(cin[0].at[2 * px + py], cout[0].at[r], ss.at[b + r], rs.at[b + r], (px, py, c)).start()

    def finish(cin, cout, ss, rs, b):
        x, y, c, chips = _where()
        for r, (px, py) in enumerate(chips):
            got = cout[0].at[r]
            _rcopy(got, got, ss.at[b + r], rs.at[b + r], (px, py, c)).wait_recv()
        for r, (px, py) in enumerate(chips):
            got = cout[0].at[r]
            _rcopy(got, got, ss.at[b + r], rs.at[b + r], (px, py, c)).wait_send()

    return _Task([pbf], [SDS((3,) + pbf.shape[1:], BF)], [], 3, start, finish)


def _t_swap(fin):
    def start(cin, cout, ss, rs, b):
        x, y, c, _ = _where()
        mine = cout[0].at[c]
        _rcopy(mine, mine, ss.at[b], rs.at[b], (x, y, 1 - c)).start()

    def finish(cin, cout, ss, rs, b):
        x, y, c, _ = _where()
        got = cout[0].at[1 - c]
        _rcopy(got, got, ss.at[b], rs.at[b], (x, y, 1 - c)).wait_recv()
        _rcopy(got, got, ss.at[b], rs.at[b], (x, y, 1 - c)).wait_send()

    return _Task([fin], [SDS(fin.shape, fin.dtype)], [(0, 0)], 1, start, finish)


def _t_allgather(buf):
    def peers():
        x, y, c, _ = _where()
        out = []
        for rel in range(1, N_DEV):
            px, py, pc = x ^ ((rel >> 2) & 1), y ^ ((rel >> 1) & 1), c ^ (rel & 1)
            out.append((rel - 1, 4 * px + 2 * py + pc, (px, py, pc)))
        return 4 * x + 2 * y + c, out

    def start(cin, cout, ss, rs, b):
        me, ps = peers()
        mine = cout[0].at[me]
        for k, _, dev in ps:
            _rcopy(mine, mine, ss.at[b + k], rs.at[b + k], dev).start()

    def finish(cin, cout, ss, rs, b):
        me, ps = peers()
        for k, pidx, dev in ps:
            got = cout[0].at[pidx]
            _rcopy(got, got, ss.at[b + k], rs.at[b + k], dev).wait_recv()
        for k, _, dev in ps:
            mine = cout[0].at[me]
            _rcopy(mine, mine, ss.at[b + k], rs.at[b + k], dev).wait_send()

    return _Task([buf], [SDS(buf.shape, buf.dtype)], [(0, 0)], N_DEV - 1, start, finish)


def _run_tasks(comm, which, cin, cout, ss, rs):
    i0 = o0 = s0 = 0
    for t in comm:
        getattr(t, which)(cin[i0:i0 + len(t.ins)], cout[o0:o0 + len(t.outs)], ss, rs, s0)
        i0, o0, s0 = i0 + len(t.ins), o0 + len(t.outs), s0 + t.n_sem


def _from_hbm(*arrays):
    return [pltpu.with_memory_space_constraint(a, pltpu.HBM) for a in arrays]


def _in_hbm(shapes):
    return [pltpu.HBM(s.shape, s.dtype) for s in shapes]


def _comm_layout(comm, n_in, n_out):
    c_in = [a for t in comm for a in t.ins]
    c_out = [s for t in comm for s in t.outs]
    aliases, i0, o0 = {}, 0, 0
    for t in comm:
        for i, o in t.alias:
            aliases[n_in + i0 + i] = n_out + o0 + o
        i0, o0 = i0 + len(t.ins), o0 + len(t.outs)
    return c_in, c_out, aliases, sum(t.n_sem for t in comm)


def _call(body, operands, *, name, grid, in_specs, out_specs, out_shape, scratch_shapes=(), sem=None, vmem_mib=None, comm=(),
          free=(), prefetch=()):
    operands = [o if s.memory_space == pltpu.SMEM or k in free else pltpu.with_memory_space_constraint(o, pltpu.HBM)
                for k, (o, s) in enumerate(zip(operands, in_specs))]
    n_pre, n_in, n_out, n_scr = len(prefetch), len(in_specs), len(out_specs), len(scratch_shapes)
    c_in, c_out, aliases, n_sem = _comm_layout(comm, n_pre + n_in, n_out)
    sems = [pltpu.SemaphoreType.DMA((n_sem,)), pltpu.SemaphoreType.DMA((n_sem,))] if comm else []

    def wrapped(*refs):
        pre, refs = refs[:n_pre], refs[n_pre:]
        ins, cin = refs[:n_in], refs[n_in:n_in + len(c_in)]
        rest = refs[n_in + len(c_in):]
        outs, cout = rest[:n_out], rest[n_out:n_out + len(c_out)]
        rest = rest[n_out + len(c_out):]
        scr, csem = rest[:n_scr], rest[n_scr:]
        if not comm:
            return body(*pre, *ins, *outs, *scr)
        step = functools.reduce(lambda acc, k: acc * grid[k] + pl.program_id(k), range(len(grid)), 0)
        n_steps = math.prod(grid)
        pl.when(step == 0)(lambda: _run_tasks(comm, "start", cin, cout, *csem))
        pl.when(step == n_steps // 2)(lambda: _run_tasks(comm, "middle", cin, cout, *csem))
        body(*pre, *ins, *outs, *scr)
        pl.when(step == n_steps - 1)(lambda: _run_tasks(comm, "finish", cin, cout, *csem))

    grid_spec = pltpu.PrefetchScalarGridSpec(
        num_scalar_prefetch=n_pre, grid=grid, in_specs=list(in_specs) + [ANY] * len(c_in),
        out_specs=list(out_specs) + [ANY] * len(c_out), scratch_shapes=list(scratch_shapes) + sems)
    return _pcall(
        wrapped, name=name, grid_spec=grid_spec, out_shape=_in_hbm(list(out_shape) + c_out), input_output_aliases=aliases,
        compiler_params=_params(("arbitrary",) * len(grid) if comm else sem, vmem_mib),
    )(*prefetch, *operands, *_from_hbm(*c_in))


def _comm_call(name, comm):
    c_in, c_out, aliases, n_sem = _comm_layout(comm, 0, 0)

    def body(*refs):
        cin, cout, (ss, rs) = refs[:len(c_in)], refs[len(c_in):len(c_in) + len(c_out)], refs[len(c_in) + len(c_out):]
        for phase in ("start", "middle", "finish"):
            _run_tasks(comm, phase, cin, cout, ss, rs)

    return _pcall(
        body, name=name, in_specs=[ANY] * len(c_in), out_specs=[ANY] * len(c_out), out_shape=_in_hbm(c_out),
        scratch_shapes=[pltpu.SemaphoreType.DMA((n_sem,)), pltpu.SemaphoreType.DMA((n_sem,))],
        input_output_aliases=aliases,
    )(*_from_hbm(*c_in))


def _inproj(x, g1, w_int, comm=()):
    tm = TM

    def body(x_ref, g_ref, w_ref, proj_ref, u_ref):
        xf = x_ref[...]
        r = lax.rsqrt(jnp.mean(xf * xf, axis=-1, keepdims=True) + EPS)
        u = (xf * r * g_ref[...]).astype(BF)
        u_ref[...] = u
        proj_ref[...] = _dot(u, w_ref[...], 1, 1)

    return _call(
        body, (x, g1, w_int), name="inproj", grid=(T // tm,),
        in_specs=[pl.BlockSpec((tm, D), lambda i: (i, 0)), pl.BlockSpec((1, D), lambda i: (0, 0)),
                  _resident((INW, D))],
        out_specs=[pl.BlockSpec((tm, INW), lambda i: (i, 0)), pl.BlockSpec((tm, D), lambda i: (i, 0))],
        out_shape=[SDS((T, INW), F32), SDS((T, D), BF)], sem=("parallel",), vmem_mib=40, comm=comm)


def _outproj(y, w_out, x, g2):
    tm = TM

    def body(y_ref, w_ref, x_ref, g_ref, h1_ref, u2_ref):
        h1 = x_ref[...] + _dot(y_ref[...], w_ref[...], 1, 0)
        h1_ref[...] = h1
        r = lax.rsqrt(jnp.mean(h1 * h1, axis=-1, keepdims=True) + EPS)
        u2_ref[...] = (h1 * r * g_ref[...]).astype(BF)

    return _call(
        body, (y, w_out, x, g2), name="outproj", grid=(T // tm,),
        in_specs=[pl.BlockSpec((tm, D), lambda i: (i, 0)), _resident((D, D)),
                  pl.BlockSpec((tm, D), lambda i: (i, 0)), pl.BlockSpec((1, D), lambda i: (0, 0))],
        out_specs=[pl.BlockSpec((tm, D), lambda i: (i, 0)), pl.BlockSpec((tm, D), lambda i: (i, 0))],
        out_shape=[SDS((T, D), F32), SDS((T, D), BF)], sem=("parallel",), vmem_mib=32)


def _ffn_up(u2, w_upt, comm=()):
    tm, tn = 1024, 512

    def body(u_ref, w_ref, o_ref):
        o_ref[...] = _dot(u_ref[...], w_ref[...], 1, 1).astype(BF)

    return _call(
        body, (u2, w_upt), name="ffn_up", grid=(T // tm, 2 * DFF // tn),
        in_specs=[pl.BlockSpec((tm, D), lambda i, j: (i, 0)), pl.BlockSpec((tn, D), lambda i, j: (j, 0))],
        out_specs=[pl.BlockSpec((tm, tn), lambda i, j: (i, j))], out_shape=[SDS((T, 2 * DFF), BF)],
        sem=("parallel", "parallel"), vmem_mib=32, comm=comm)


def _ffn_down(a, w_down, h1, tgt):
    tm = TM

    def body(a_ref, w_ref, h1_ref, t_ref, dh_ref, dhb_ref, l_ref):
        @pl.when(pl.program_id(0) == 0)
        def _():
            l_ref[...] = jnp.zeros_like(l_ref)

        h2 = h1_ref[...] + _dot(a_ref[...], w_ref[...], 1, 0)
        e = h2 - t_ref[...]
        dh = e * (1.0 / D)
        dh_ref[...] = dh
        dhb_ref[...] = dh.astype(BF)
        e2 = jnp.sum((e * e).reshape(tm // 8, 8, D), axis=0)
        acc = e2[:, 0:128]
        for k in range(1, D // 128):
            acc = acc + e2[:, k * 128:(k + 1) * 128]
        l_ref[...] += acc

    return _call(
        body, (a, w_down, h1, tgt), name="ffn_down", grid=(T // tm,),
        in_specs=[pl.BlockSpec((tm, DFF), lambda i: (i, 0)), _resident((DFF, D)),
                  pl.BlockSpec((tm, D), lambda i: (i, 0)), pl.BlockSpec((tm, D), lambda i: (i, 0))],
        out_specs=[pl.BlockSpec((tm, D), lambda i: (i, 0)), pl.BlockSpec((tm, D), lambda i: (i, 0)),
                   pl.BlockSpec((8, 128), lambda i: (0, 0))],
        out_shape=[SDS((T, D), F32), SDS((T, D), BF), SDS((8, 128), F32)], sem=("arbitrary",), vmem_mib=40)


def _bucket_table():
    q = np.arange(BLK, dtype=np.int32)[:, None]
    j = np.arange(2 * BLK, dtype=np.int32)[None, :]
    n = np.maximum(q + BLK - j, 0)
    nf = np.maximum(n, 1).astype(np.float32)
    max_exact = NBUCKET // 2
    large = max_exact + (np.log(nf / np.float32(max_exact)) / np.float32(math.log(BLK / max_exact))
                         * np.float32(NBUCKET - max_exact)).astype(np.int32)
    large = np.minimum(large, NBUCKET - 1)
    return np.where(n < max_exact, n, large).astype(np.int32)


def _band_bias_bwd(dbias, bucket, me):
    def body(me_ref, db_ref, bk_ref, o_ref):
        bk = bk_ref[...]
        for b in range(NBUCKET):
            m = bk == b
            for h in range(NH):
                v = jnp.where(m, db_ref[h * BLK:(h + 1) * BLK, :], 0.0)
                s = jnp.sum(jnp.sum(v, axis=1, keepdims=True), axis=0, keepdims=True)
                o_ref[0, h:h + 1, b:b + 1] = s

    grid_spec = pltpu.PrefetchScalarGridSpec(
        num_scalar_prefetch=1, grid=(1,),
        in_specs=[pl.BlockSpec((NH * BLK, 2 * BLK), lambda i, me_ref: (0, 0)),
                  pl.BlockSpec((BLK, 2 * BLK), lambda i, me_ref: (0, 0))],
        out_specs=pl.BlockSpec((1, NH, NBUCKET), lambda i, me_ref: (me_ref[0], 0, 0)),
    )
    return _pcall(body, name="band_bias_bwd", grid_spec=grid_spec, out_shape=SDS((N_DEV, NH, NBUCKET), F32),
                  compiler_params=_params(("arbitrary",)))(me, dbias, bucket)


def _two_bf16(x):
    hi = x.astype(BF)
    return hi, (x - hi.astype(F32)).astype(BF)


def _head_sums(x, seg):
    hi, lo = _two_bf16(x)
    s = seg[0:x.shape[1], :]
    return _dot(hi, s, 1, 0) + _dot(lo, s, 1, 0)


def _head_spread(v, seg, width):
    hi, lo = _two_bf16(v)
    s = seg[0:width, :]
    return _dot(hi, s, 1, 1) + _dot(lo, s, 1, 1)


def _head_norm(x, g_t, seg, by_head=False):
    if by_head:
        heads = [x[:, h * HD:(h + 1) * HD] for h in range(x.shape[1] // HD)]
        r = jnp.concatenate([jnp.broadcast_to(lax.rsqrt(jnp.mean(v * v, axis=-1, keepdims=True) + EPS), v.shape)
                             for v in heads], axis=1)
    else:
        r = lax.rsqrt(_head_sums(x * x, seg) * (1.0 / HD) + EPS)
        r = _head_spread(r, seg, x.shape[1])
    return x * r * g_t, r


def _head_norm_bwd(dy, x, r, g_t, seg):
    dg_t = jnp.sum(dy * (x * r), axis=0, keepdims=True)
    dgx = dy * g_t
    mean = _head_spread(_head_sums(x * dgx, seg) * (1.0 / HD), seg, x.shape[1])
    return r * dgx - x * (r * r * r) * mean, dg_t


def _fold_heads(v):
    out = v[:, 0:HD]
    for h in range(1, v.shape[1] // HD):
        out = out + v[:, h * HD:(h + 1) * HD]
    return out


def _mix_forward(P, zc8, zh8, pkv, first, cw, qg_t, kg_t, gco, gao, seg, sink_ref, bias_ref, by_head=False):
    gate_b = P[:, 0:CW]
    gate_c = P[:, CW:2 * CW]
    hc = P[:, 2 * CW:3 * CW]
    z = gate_c * hc
    keep = jnp.where(first, 0.0, 1.0)
    zp = zc8 * zh8 * keep
    p1 = zp[7:8, :]
    p2 = zp[6:7, :]
    row = lax.broadcasted_iota(jnp.int32, (BLK, 1), 0)
    z1 = jnp.where(row == 0, p1, pltpu.roll(z, 1, 0))
    z2 = jnp.where(row == 0, p2, jnp.where(row == 1, p1, pltpu.roll(z, 2, 0)))
    cz = cw[0:1, :] * z2 + cw[1:2, :] * z1 + cw[2:3, :] * z
    y_conv = gate_b * cz

    scale = HD ** -0.5
    qi = lax.broadcasted_iota(jnp.int32, (GQ * BLK, 2 * BLK), 0) & (BLK - 1)
    kj = lax.broadcasted_iota(jnp.int32, (GQ * BLK, 2 * BLK), 1)
    dd = qi + BLK - kj
    first_key = jnp.where(first, BLK, 0)
    valid = (dd >= 0) & (dd < BLK) & (kj >= first_key)

    q0 = 3 * CW
    k0 = q0 + AW
    v0 = k0 + NKV * HD
    q_raw = P[:, q0:k0]
    qn, rq = _head_norm(q_raw, qg_t, seg, by_head)
    qs = (qn * scale).astype(BF)
    k_raw = jnp.concatenate([pkv[:, 0:NKV * HD], P[:, k0:v0]], axis=0)
    kn, rk = _head_norm(k_raw, kg_t, seg, by_head)
    knb = kn.astype(BF)
    heads = []
    outs = []
    for kv in range(NKV):
        kb = knb[:, kv * HD:(kv + 1) * HD]
        vb = jnp.concatenate([pkv[:, NKV * HD + kv * HD:NKV * HD + (kv + 1) * HD],
                              P[:, v0 + kv * HD:v0 + (kv + 1) * HD]], axis=0).astype(BF)
        Q = jnp.concatenate([qs[:, (kv * GQ + g) * HD:(kv * GQ + g + 1) * HD] for g in range(GQ)], axis=0)
        S = _dot(Q, kb, 1, 1) + bias_ref[kv * GQ * BLK:(kv + 1) * GQ * BLK, :]
        S = jnp.where(valid, S, NEG_INF)
        sink = jnp.concatenate([jnp.full((BLK, 1), sink_ref[0, kv * GQ + g], F32) for g in range(GQ)], axis=0)
        m = jnp.maximum(jnp.max(S, axis=-1, keepdims=True), sink)
        p = jnp.exp(S - m)
        es = jnp.exp(sink - m)
        denom = jnp.sum(p, axis=-1, keepdims=True) + es
        probs = p / denom
        O = _dot(probs.astype(BF), vb, 1, 0)
        heads.append(dict(kb=kb, vb=vb, Q=Q, probs=probs, psink=es / denom, O=O))
        outs += [O[g * BLK:(g + 1) * BLK, :] for g in range(GQ)]
    y_attn = jnp.concatenate(outs, axis=1)

    rc = lax.rsqrt(jnp.mean(y_conv * y_conv, axis=-1, keepdims=True) + EPS)
    ra = lax.rsqrt(jnp.mean(y_attn * y_attn, axis=-1, keepdims=True) + EPS)
    y = jnp.concatenate([y_conv * rc * gco, y_attn * ra * gao], axis=1)
    return dict(gate_b=gate_b, gate_c=gate_c, hc=hc, z=z, z1=z1, z2=z2, cz=cz, y_conv=y_conv, y_attn=y_attn,
                rc=rc, ra=ra, heads=heads, y=y, row=row, scale=scale, q_raw=q_raw, rq=rq, k_raw=k_raw, rk=rk)


BPS = 2
TILE = BPS * BLK
KV0 = 3 * CW + AW


def _mix_in_specs(tile_of):
    return [
        pl.BlockSpec(memory_space=pltpu.SMEM),
        pl.BlockSpec((TILE, INW), lambda s: (tile_of(s), 0)),
        pl.BlockSpec((8, CW), lambda s: (jnp.maximum(tile_of(s) * (TILE // 8) - 1, 0), 1)),
        pl.BlockSpec((8, CW), lambda s: (jnp.maximum(tile_of(s) * (TILE // 8) - 1, 0), 2)),
        pl.BlockSpec((BLK, 2 * NKV * HD), lambda s: (jnp.maximum(tile_of(s) * BPS - 1, 0), KV0 // (2 * NKV * HD))),
    ]


def _block_inputs(tile, b, zc_ref, zh_ref, pkv_ref, first_tile):
    P = tile[b * BLK:(b + 1) * BLK, :]
    if b == 0:
        return P, zc_ref[...], zh_ref[...], pkv_ref[...], first_tile
    lo = b * BLK
    return P, tile[lo - 8:lo, CW:2 * CW], tile[lo - 8:lo, 2 * CW:3 * CW], tile[lo - BLK:lo, KV0:KV0 + 2 * NKV * HD], False


def _mix_param_specs():
    return [
        pl.BlockSpec((8, CW), lambda s: (0, 0)),
        pl.BlockSpec((1, AW), lambda s: (0, 0)),
        pl.BlockSpec((1, NKV * HD), lambda s: (0, 0)),
        pl.BlockSpec((1, CW), lambda s: (0, 0)),
        pl.BlockSpec((1, AW), lambda s: (0, 0)),
        pl.BlockSpec((AW, 128), lambda s: (0, 0)),
        pl.BlockSpec((NH * BLK, 2 * BLK), lambda s: (0, 0)),
    ]


def _mix_params(cw8, qg, kg, gco, gao, bias):
    seg = np.zeros((AW, 128), np.float32)
    seg[np.arange(AW), np.arange(AW) // HD] = 1.0
    return (cw8, jnp.tile(qg, (1, NH)), jnp.tile(kg, (1, NKV)), gco, gao, jnp.asarray(seg, BF), bias)


def _mix_fwd(proj, sinks, cw8, qg, kg, gco, gao, bias, comm=()):
    def body(sink_ref, p_ref, zc_ref, zh_ref, pkv_ref, cw_ref, qg_ref, kg_ref, gco_ref, gao_ref, seg_ref, bias_ref, y_ref):
        tile = p_ref[...]
        for b in range(BPS):
            f = _mix_forward(*_block_inputs(tile, b, zc_ref, zh_ref, pkv_ref, pl.program_id(0) == 0), cw_ref[...],
                             qg_ref[...], kg_ref[...], gco_ref[...], gao_ref[...], seg_ref[...], sink_ref, bias_ref, by_head=True)
            y_ref[b * BLK:(b + 1) * BLK, :] = f["y"].astype(BF)

    return _call(
        body, (sinks, proj, proj, proj, proj, *_mix_params(cw8, qg, kg, gco, gao, bias)), name="mix_fwd", grid=(T // TILE,),
        in_specs=_mix_in_specs(lambda s: s) + _mix_param_specs(),
        out_specs=[pl.BlockSpec((TILE, D), lambda s: (s, 0))], out_shape=[SDS((T, D), BF)],
        sem=("parallel",), vmem_mib=40, comm=comm)


def _mix_bwd(proj, dy, sinks, cw8, qg, kg, gco, gao, bias, comm=()):
    n_steps = T // TILE

    def tile_of(s):
        return n_steps - 1 - s

    def body(sink_ref, p_ref, zc_ref, zh_ref, pkv_ref, dy_ref, cw_ref, qg_ref, kg_ref, gco_ref, gao_ref, seg_ref, bias_ref,
             dproj_ref, dcw_ref, dqg_ref, dkg_ref, dgco_ref, dgao_ref, dsink_ref, dbias_ref,
             ndcz_ref, dkc_ref, dvc_ref):
        s = pl.program_id(0)

        @pl.when(s == 0)
        def _():
            for r in (dcw_ref, dqg_ref, dkg_ref, dgco_ref, dgao_ref, dsink_ref, dbias_ref, ndcz_ref, dkc_ref, dvc_ref):
                r[...] = jnp.zeros_like(r)

        tile = p_ref[...]
        for b in reversed(range(BPS)):
            one_block(b, _block_inputs(tile, b, zc_ref, zh_ref, pkv_ref, s == n_steps - 1),
                      dy_ref[b * BLK:(b + 1) * BLK, :], sink_ref, cw_ref, qg_ref, kg_ref, gco_ref, gao_ref, seg_ref, bias_ref,
                      dproj_ref.at[b * BLK:(b + 1) * BLK, :], dcw_ref, dqg_ref, dkg_ref, dgco_ref, dgao_ref, dsink_ref,
                      dbias_ref, ndcz_ref, dkc_ref, dvc_ref)

    def one_block(b, inputs, dy, sink_ref, cw_ref, qg_ref, kg_ref, gco_ref, gao_ref, seg_ref, bias_ref,
                  dproj_ref, dcw_ref, dqg_ref, dkg_ref, dgco_ref, dgao_ref, dsink_ref, dbias_ref,
                  ndcz_ref, dkc_ref, dvc_ref):
        cw = cw_ref[...]
        qg_v, kg_v, gco_v, gao_v, seg = qg_ref[...], kg_ref[...], gco_ref[...], gao_ref[...], seg_ref[...]
        f = _mix_forward(*inputs, cw, qg_v, kg_v, gco_v, gao_v, seg, sink_ref, bias_ref)
        dyc, dgco = _rms_bwd(dy[:, 0:CW], f["y_conv"], f["rc"], gco_v)
        dya, dgao = _rms_bwd(dy[:, CW:CW + AW], f["y_attn"], f["ra"], gao_v)
        dgco_ref[...] += dgco
        dgao_ref[...] += dgao

        row = f["row"]
        dgate_b = dyc * f["cz"]
        dcz = dyc * f["gate_b"]
        dcw_ref[0:1, :] += jnp.sum(dcz * f["z2"], axis=0, keepdims=True)
        dcw_ref[1:2, :] += jnp.sum(dcz * f["z1"], axis=0, keepdims=True)
        dcw_ref[2:3, :] += jnp.sum(dcz * f["z"], axis=0, keepdims=True)
        nxt = ndcz_ref[...]
        n0 = nxt[0:1, :]
        n1 = nxt[1:2, :]
        d1 = jnp.where(row == BLK - 1, n0, pltpu.roll(dcz, BLK - 1, 0))
        d2 = jnp.where(row == BLK - 1, n1, jnp.where(row == BLK - 2, n0, pltpu.roll(dcz, BLK - 2, 0)))
        dz = cw[2:3, :] * dcz + cw[1:2, :] * d1 + cw[0:1, :] * d2
        ndcz_ref[...] = dcz[0:8, :]
        dproj_ref[:, 0:CW] = dgate_b.astype(BF)
        dproj_ref[:, CW:2 * CW] = (dz * f["hc"]).astype(BF)
        dproj_ref[:, 2 * CW:3 * CW] = (dz * f["gate_c"]).astype(BF)

        scale = f["scale"]
        lane = lax.broadcasted_iota(jnp.int32, (1, 128), 1)
        dq_cols, dk_cols, dv_cols = [], [], []
        for kv in range(NKV):
            hd = f["heads"][kv]
            dO = jnp.concatenate([dya[:, (kv * GQ + g) * HD:(kv * GQ + g + 1) * HD] for g in range(GQ)], axis=0)
            delta = jnp.sum(dO * hd["O"], axis=-1, keepdims=True)
            dOb = dO.astype(BF)
            dP = _dot(dOb, hd["vb"], 1, 1)
            dS = hd["probs"] * (dP - delta)
            dsk = hd["psink"] * delta
            for g in range(GQ):
                h = kv * GQ + g
                tot = jnp.sum(dsk[g * BLK:(g + 1) * BLK, :], axis=0, keepdims=True)
                dsink_ref[...] -= jnp.where(lane == h, tot, 0.0)
            dbias_ref[kv * GQ * BLK:(kv + 1) * GQ * BLK, :] += dS
            dSb = dS.astype(BF)
            dQ = _dot(dSb, hd["kb"], 1, 0)
            dKb = _dot(dSb, hd["Q"], 0, 0)
            dVb = _dot(hd["probs"].astype(BF), dOb, 0, 0)
            dk_cols.append(dKb[BLK:, :] + dkc_ref[:, kv * HD:(kv + 1) * HD])
            dv_cols.append(dVb[BLK:, :] + dvc_ref[:, kv * HD:(kv + 1) * HD])
            dkc_ref[:, kv * HD:(kv + 1) * HD] = dKb[:BLK, :]
            dvc_ref[:, kv * HD:(kv + 1) * HD] = dVb[:BLK, :]
            dq_cols += [dQ[g * BLK:(g + 1) * BLK, :] for g in range(GQ)]
        dq_raw, dqg_t = _head_norm_bwd(jnp.concatenate(dq_cols, axis=1) * scale, f["q_raw"], f["rq"], qg_v, seg)
        dk_raw, dkg_t = _head_norm_bwd(jnp.concatenate(dk_cols, axis=1), f["k_raw"][BLK:, :], f["rk"][BLK:, :], kg_v, seg)
        dqg_ref[...] += _fold_heads(dqg_t)
        dkg_ref[...] += _fold_heads(dkg_t)
        dproj_ref[:, 3 * CW:INW] = jnp.concatenate([dq_raw, dk_raw] + dv_cols, axis=1).astype(BF)

    small = lambda r, c: pl.BlockSpec((r, c), lambda s: (0, 0))
    return _call(
        body, (sinks, proj, proj, proj, proj, dy, *_mix_params(cw8, qg, kg, gco, gao, bias)), name="mix_bwd", grid=(n_steps,),
        in_specs=_mix_in_specs(tile_of) + [pl.BlockSpec((TILE, D), lambda s: (tile_of(s), 0))] + _mix_param_specs(),
        out_specs=[pl.BlockSpec((TILE, INW), lambda s: (tile_of(s), 0)), small(8, CW), small(1, HD), small(1, HD),
                   small(1, CW), small(1, AW), small(1, 128), small(NH * BLK, 2 * BLK)],
        out_shape=[SDS((T, INW), BF), SDS((8, CW), F32), SDS((1, HD), F32), SDS((1, HD), F32), SDS((1, CW), F32),
                   SDS((1, AW), F32), SDS((1, 128), F32), SDS((NH * BLK, 2 * BLK), F32)],
        scratch_shapes=[pltpu.VMEM((8, CW), F32), pltpu.VMEM((BLK, NKV * HD), F32), pltpu.VMEM((BLK, NKV * HD), F32)],
        sem=("arbitrary",), vmem_mib=56, comm=comm)


FT = 256
NFT = DFF // FT
RC = 128
NCH = T // RC
LEAD = 16


def _rows8(x):
    return jnp.sum(x.reshape(x.shape[0] // 8, 8, x.shape[1]), axis=0)


def _ffn_act_specs():
    return [
        pl.BlockSpec((T, FT), lambda j: (0, j)), pl.BlockSpec((T, FT), lambda j: (0, NFT + j)),
        pl.BlockSpec((8, FT), lambda j: (0, j)), pl.BlockSpec((8, FT), lambda j: (0, NFT + j)),
        pl.BlockSpec((1, FT), lambda j: (0, j)), pl.BlockSpec((1, FT), lambda j: (0, NFT + j)),
    ]


def _conv_rows(win, w, b, n):
    win = win.astype(F32)
    u = win[LEAD:LEAD + n]
    u1 = pltpu.roll(win, 1, 0)[LEAD:LEAD + n]
    u2 = pltpu.roll(win, 2, 0)[LEAD:LEAD + n]
    return u2, u1, u, w[0:1, :] * u2 + w[1:2, :] * u1 + w[2:3, :] * u + b


def _ffn_act(up, fw8, fb):
    def body(ug_ref, uv_ref, wg_ref, wv_ref, bg_ref, bv_ref, a_ref):
        wg, wv, bg, bv = wg_ref[...], wv_ref[...], bg_ref[...], bv_ref[...]

        def chunk(win_g, win_v):
            gp = _conv_rows(win_g, wg, bg, RC)[3]
            vp = _conv_rows(win_v, wv, bv, RC)[3]
            return (gp * jax.nn.sigmoid(gp) * vp).astype(BF)

        zero = jnp.zeros((LEAD, FT), BF)
        a_ref[0:RC, :] = chunk(jnp.concatenate([zero, ug_ref[0:RC, :]], axis=0),
                               jnp.concatenate([zero, uv_ref[0:RC, :]], axis=0))

        def step(i, carry):
            r0 = pl.multiple_of(i * RC, RC)
            win = pl.ds(r0 - LEAD, RC + LEAD)
            a_ref[pl.ds(r0, RC), :] = chunk(ug_ref[win, :], uv_ref[win, :])
            return carry

        lax.fori_loop(1, NCH, step, 0)

    return _call(
        body, (up, up, fw8, fw8, fb, fb), name="ffn_act", grid=(NFT,), in_specs=_ffn_act_specs(),
        out_specs=[pl.BlockSpec((T, FT), lambda j: (0, j))], out_shape=[SDS((T, DFF), BF)],
        sem=("parallel",), vmem_mib=40)


def _ffn_act_bwd(up, da, fw8, fb, comm=()):
    ext = RC + LEAD

    def body(ug_ref, uv_ref, wg_ref, wv_ref, bg_ref, bv_ref, da_ref,
             dug_ref, duv_ref, dwg_ref, dwv_ref, dbg_ref, dbv_ref):
        wg, wv, bg, bv = wg_ref[...], wv_ref[...], bg_ref[...], bv_ref[...]

        def chunk(win_g, win_v, da_e):
            g2, g1, g0, gp = _conv_rows(win_g, wg, bg, ext)
            v2, v1, v0, vp = _conv_rows(win_v, wv, bv, ext)
            da_e = da_e.astype(F32)
            sig = jax.nn.sigmoid(gp)
            dvp = da_e * (gp * sig)
            dgp = da_e * vp * (sig * (1.0 + gp * (1.0 - sig)))

            def back(dp, w):
                return (w[2:3, :] * dp[0:RC] + w[1:2, :] * pltpu.roll(dp, ext - 1, 0)[0:RC]
                        + w[0:1, :] * pltpu.roll(dp, ext - 2, 0)[0:RC]).astype(BF)

            def sums(dp, u2, u1, u0):
                d = dp[0:RC]
                return [_rows8(d), _rows8(d * u2[0:RC]), _rows8(d * u1[0:RC]), _rows8(d * u0[0:RC])]

            return back(dgp, wg), back(dvp, wv), sums(dgp, g2, g1, g0) + sums(dvp, v2, v1, v0)

        zero = jnp.zeros((LEAD, FT), BF)
        dug, duv, acc = chunk(jnp.concatenate([zero, ug_ref[0:ext, :]], axis=0),
                              jnp.concatenate([zero, uv_ref[0:ext, :]], axis=0), da_ref[0:ext, :])
        dug_ref[0:RC, :] = dug
        duv_ref[0:RC, :] = duv

        def step(i, acc):
            r0 = pl.multiple_of(i * RC, RC)
            win = pl.ds(r0 - LEAD, ext + LEAD)
            dug, duv, part = chunk(ug_ref[win, :], uv_ref[win, :], da_ref[pl.ds(r0, ext), :])
            dug_ref[pl.ds(r0, RC), :] = dug
            duv_ref[pl.ds(r0, RC), :] = duv
            return [a + p for a, p in zip(acc, part)]

        acc = lax.fori_loop(1, NCH - 1, step, acc)
        r0 = T - RC
        tail = lambda ref, lo: jnp.concatenate([ref[lo:T, :], zero], axis=0)
        dug, duv, part = chunk(tail(ug_ref, r0 - LEAD), tail(uv_ref, r0 - LEAD), tail(da_ref, r0))
        dug_ref[r0:T, :] = dug
        duv_ref[r0:T, :] = duv
        tot = [jnp.sum(a + p, axis=0, keepdims=True) for a, p in zip(acc, part)]
        for k, (dw_ref, db_ref) in enumerate(((dwg_ref, dbg_ref), (dwv_ref, dbv_ref))):
            db_ref[...] = tot[4 * k]
            dw_ref[...] = jnp.zeros_like(dw_ref)
            for r in range(3):
                dw_ref[r:r + 1, :] = tot[4 * k + 1 + r]

    col = lambda r: pl.BlockSpec((r, FT), lambda j: (0, j))
    return _call(
        body, (up, up, fw8, fw8, fb, fb, da), name="ffn_act_bwd", grid=(NFT,),
        in_specs=_ffn_act_specs() + [pl.BlockSpec((T, FT), lambda j: (0, j))],
        out_specs=[col(T), col(T), col(8), col(8), col(1), col(1)],
        out_shape=[SDS((T, DFF), BF), SDS((T, DFF), BF), SDS((8, DFF), F32), SDS((8, DFF), F32),
                   SDS((1, DFF), F32), SDS((1, DFF), F32)],
        sem=("parallel",), vmem_mib=40, comm=comm)


def _ffn_down_bwd(dh2b, w_down, comm=()):
    tm = TM

    def body(d_ref, w_ref, o_ref):
        o_ref[...] = _dot(d_ref[...], w_ref[...], 1, 1).astype(BF)

    return _call(
        body, (dh2b, w_down), name="ffn_down_bwd", grid=(T // tm,),
        in_specs=[pl.BlockSpec((tm, D), lambda i: (i, 0)), _resident((DFF, D))],
        out_specs=[pl.BlockSpec((tm, DFF), lambda i: (i, 0))], out_shape=[SDS((T, DFF), BF)],
        sem=("parallel",), vmem_mib=40, comm=comm)


def _norm_matmul_bwd(name, a_list, w_t, k_offsets, xin, g, dres, want_bf16, comm=(), slot=None):
    tm = TM
    ks = [a.shape[1] for a in a_list]
    n_a = len(a_list)
    n_pre = 0 if slot is None else 1

    def body(*refs):
        refs = refs[n_pre:]
        a_refs = refs[:n_a]
        w_ref, x_ref, g_ref, r_ref = refs[n_a:n_a + 4]
        outs = refs[n_a + 4:]
        dx_ref, dg_ref = outs[0], (outs[-1] if slot is None else outs[-1].at[0])

        @pl.when(pl.program_id(0) == 0)
        def _():
            dg_ref[...] = jnp.zeros_like(dg_ref)

        du = _dot(a_refs[0][...], w_ref[k_offsets[0]:k_offsets[0] + ks[0], :], 1, 0)
        for k in range(1, n_a):
            du = du + _dot(a_refs[k][...], w_ref[k_offsets[k]:k_offsets[k] + ks[k], :], 1, 0)
        x = x_ref[...]
        r = lax.rsqrt(jnp.mean(x * x, axis=-1, keepdims=True) + EPS)
        dx, dg = _rms_bwd(du, x, r, g_ref[...])
        dx = r_ref[...] + dx
        dx_ref[...] = dx
        if want_bf16:
            outs[1][...] = dx.astype(BF)
        dg_ref[...] += dg

    tile = lambda c: pl.BlockSpec((tm, c), lambda i, *_: (i, 0))
    if slot is None:
        dg_spec, dg_shape = pl.BlockSpec((1, D), lambda i: (0, 0)), SDS((1, D), F32)
    else:
        dg_spec, dg_shape = pl.BlockSpec((1, 1, D), lambda i, slot_ref: (slot_ref[0], 0, 0)), SDS((N_DEV, 1, D), F32)
    out_specs = [tile(D)] + ([tile(D)] if want_bf16 else []) + [dg_spec]
    out_shape = [SDS((T, D), F32)] + ([SDS((T, D), BF)] if want_bf16 else []) + [dg_shape]
    return _call(
        body, (*a_list, w_t, xin, g, dres), name=name, grid=(T // tm,), prefetch=() if slot is None else (slot,),
        in_specs=[tile(k) for k in ks] + [_resident(w_t.shape), tile(D),
                                           pl.BlockSpec((1, D), lambda i, *_: (0, 0)), tile(D)],
        out_specs=out_specs, out_shape=out_shape, sem=("arbitrary",), vmem_mib=56, comm=comm)


def _out_bwd(dh1b, w_out, comm=()):
    tm = TM

    def body(d_ref, w_ref, o_ref):
        o_ref[...] = _dot(d_ref[...], w_ref[...], 1, 1)

    return _call(
        body, (dh1b, w_out), name="out_bwd", grid=(T // tm,),
        in_specs=[pl.BlockSpec((tm, D), lambda i: (i, 0)), _resident((D, D))],
        out_specs=[pl.BlockSpec((tm, D), lambda i: (i, 0))], out_shape=[SDS((T, D), F32)],
        sem=("parallel",), vmem_mib=32, comm=comm)


def _wgrad(name, a_list, b, comm=()):
    m_k = a_list[0].shape[1]
    tm = max(t for t in range(128, m_k // 2 + 1, 128) if m_k % t == 0)
    steps = [a.shape[1] // tm for a in a_list]
    starts = [sum(steps[:k]) for k in range(len(a_list))]
    n_a = len(a_list)

    def body(*refs):
        a_refs, b_ref, o_ref = refs[:n_a], refs[n_a], refs[n_a + 1]
        i = pl.program_id(0)
        for k in range(n_a):
            @pl.when((i >= starts[k]) & (i < starts[k] + steps[k]))
            def _(k=k):
                o_ref[...] = _dot(a_refs[k][...], b_ref[...], 0, 0).astype(BF)

    def a_spec(k):
        return pl.BlockSpec((T, tm), lambda i: (0, jnp.clip(i - starts[k], 0, steps[k] - 1)))

    m_total = tm * sum(steps)
    return _call(
        body, (*a_list, b), name=name, grid=(sum(steps),),
        in_specs=[a_spec(k) for k in range(n_a)] + [_resident((T, D))],
        out_specs=[pl.BlockSpec((tm, D), lambda i: (i, 0))], out_shape=[SDS((m_total, D), BF)],
        sem=("parallel",), vmem_mib=40, comm=comm)


def _chip_sum(name, gbf, from_sib, core, chip):
    h = gbf.shape[1]
    th = h // 2

    def body(core_ref, chip_ref, g_ref, s_ref, pbf_ref, own_ref):
        p = g_ref[0].astype(F32) + s_ref[0].astype(F32)
        pbf_ref[0] = p.astype(BF)

        @pl.when(pl.program_id(1) == chip_ref[0])
        def _():
            own_ref[...] = p

    grid_spec = pltpu.PrefetchScalarGridSpec(
        num_scalar_prefetch=2, grid=(h // th, N_CHIPS),
        in_specs=[pl.BlockSpec((1, th, D), lambda t, jj, core_ref, chip_ref: (2 * jj + core_ref[0], t, 0)),
                  pl.BlockSpec((1, th, D), lambda t, jj, core_ref, chip_ref: (jj, t, 0))],
        out_specs=[pl.BlockSpec((1, th, D), lambda t, jj, core_ref, chip_ref: (jj, t, 0)),
                   pl.BlockSpec((th, D), lambda t, jj, core_ref, chip_ref: (t, 0))],
    )
    return _pcall(
        body, name=name, grid_spec=grid_spec, out_shape=_in_hbm([SDS((N_CHIPS, h, D), BF), SDS((h, D), F32)]),
        compiler_params=_params(("arbitrary", "arbitrary"), 32),
    )(core, chip, *_from_hbm(gbf, from_sib))


def _final_sum(name, own, from_chips, core):
    h = own.shape[0]

    def body(core_ref, o_ref, r_ref, f_ref):
        f_ref[0] = ((o_ref[...] + r_ref[0].astype(F32)) + r_ref[1].astype(F32)) + r_ref[2].astype(F32)

    grid_spec = pltpu.PrefetchScalarGridSpec(
        num_scalar_prefetch=1, grid=(1,),
        in_specs=[pl.BlockSpec((h, D), lambda i, core_ref: (0, 0)), pl.BlockSpec((3, h, D), lambda i, core_ref: (0, 0, 0))],
        out_specs=pl.BlockSpec((1, h, D), lambda i, core_ref: (core_ref[0], 0, 0)),
    )
    return _pcall(body, name=name, grid_spec=grid_spec, out_shape=pltpu.HBM((2, h, D), F32),
                  compiler_params=_params(("arbitrary",), 40))(core, *_from_hbm(own, from_chips))


def _adam_math(w, g, m, v):
    nm = ADAM_B1 * m + (1.0 - ADAM_B1) * g
    nv = ADAM_B2 * v + (1.0 - ADAM_B2) * (g * g)
    m_hat = nm / (1.0 - ADAM_B1 ** ADAM_STEP)
    v_hat = nv / (1.0 - ADAM_B2 ** ADAM_STEP)
    return -ADAM_LR * (m_hat / (jnp.sqrt(v_hat) + ADAM_EPS) + ADAM_WD * w), nm, nv


def _adamw(name, w, g, m, v, tr, copy_g=False, stage=True):
    rows, cols = w.shape

    def body(w_ref, g_ref, m_ref, v_ref, *outs):
        g_val = g_ref[...]
        if copy_g:
            outs[0][...] = g_val
        d_ref, nm_ref, nv_ref = outs[-3:]
        d_ref[...], nm_ref[...], nv_ref[...] = _adam_math(w_ref[...], g_val, m_ref[...], v_ref[...])

    spec = pl.BlockSpec((tr, cols), lambda i: (i, 0))
    n_out = 4 if copy_g else 3
    return _call(body, (w, g, m, v), name=name, grid=(rows // tr,), in_specs=[spec] * 4, out_specs=[spec] * n_out,
                 out_shape=[SDS((rows, cols), F32)] * n_out, sem=("parallel",), vmem_mib=32,
                 free=(0, 2, 3) if stage else ())


C_G1, C_G2, C_GCO, C_GAO, C_DCW, C_DQG, C_DKG, C_SINK, C_SQ = 0, 1024, 2048, 2560, 3072, 4608, 4736, 4864, 5632
P_W = C_SQ + 128


def _pack_small(me, dfwg, dfwv, dfbg, dfbv, dg2, dgco, dgao, dcw8, dqg, dkg, dsink, sq):
    def body(me_ref, dfwg_r, dfwv_r, dfbg_r, dfbv_r, dg2_r, dgco_r, dgao_r, dcw_r, dqg_r, dkg_r, dsink_r, sq_r, o):
        o[...] = jnp.zeros_like(o)
        o[0, :, 0:DFF] = dfwg_r[...]
        o[0, :, DFF:2 * DFF] = dfwv_r[...]
        o[0, 3:4, 0:DFF] = dfbg_r[...]
        o[0, 3:4, DFF:2 * DFF] = dfbv_r[...]
        o[0, 4:5, C_G2:C_G2 + D] = dg2_r[...]
        o[0, 4:5, C_GCO:C_GCO + CW] = dgco_r[...]
        o[0, 4:5, C_GAO:C_GAO + AW] = dgao_r[...]
        for r in range(3):
            o[0, 4:5, C_DCW + r * CW:C_DCW + (r + 1) * CW] = dcw_r[r:r + 1, :]
        o[0, 4:5, C_DQG:C_DQG + HD] = dqg_r[...]
        o[0, 4:5, C_DKG:C_DKG + HD] = dkg_r[...]
        o[0, 4:5, C_SINK:C_SINK + 128] = dsink_r[...]
        o[0, :, C_SQ:C_SQ + 128] = sq_r[...]

    ins = (dfwg, dfwv, dfbg, dfbv, dg2, dgco, dgao, dcw8, dqg, dkg, dsink, sq)
    return _call(body, ins, name="pack_small", grid=(1,), prefetch=(me,),
                 in_specs=[pl.BlockSpec(a.shape, lambda i, me_ref: (0, 0)) for a in ins],
                 out_specs=[pl.BlockSpec((1, 8, P_W), lambda i, me_ref: (me_ref[0], 0, 0))],
                 out_shape=[SDS((N_DEV, 8, P_W), F32)], sem=("arbitrary",))[0]


N_SMALL = 11


def _small_adam(chip, p_all, g1_all, tbl_all, ws, ms, vs):
    fw_cols = 2 * DFF // N_CHIPS
    cw_cols = CW // N_CHIPS

    def body(chip_ref, p_ref, fw_ref, cw0_ref, cw1_ref, cw2_ref, g1_ref, tbl_ref, *refs):
        w_r, m_r, v_r = refs[0:N_SMALL], refs[N_SMALL:2 * N_SMALL], refs[2 * N_SMALL:3 * N_SMALL]
        outs = refs[3 * N_SMALL:]
        g_o, d_o, nm_o, nv_o = (outs[k * N_SMALL:(k + 1) * N_SMALL] for k in range(4))
        loss_o = outs[4 * N_SMALL]

        def total(ref):
            s = ref[0]
            for k in range(1, N_DEV):
                s = s + ref[k]
            return s

        S = total(p_ref)
        fw = total(fw_ref)
        cws = [total(r) for r in (cw0_ref, cw1_ref, cw2_ref)]

        def step(i, g, at):
            d, nm, nv = _adam_math(w_r[i][at], g, m_r[i][at], v_r[i][at])
            g_o[i][at], d_o[i][at], nm_o[i][at], nv_o[i][at] = g, d, nm, nv

        everything = (slice(None), slice(None))
        step(0, total(g1_ref), everything)
        for r in range(3):
            step(1, cws[r][4:5, :], (r, slice(None), slice(None)))
        step(2, S[4:5, C_DQG:C_DQG + HD], everything)
        step(3, S[4:5, C_DKG:C_DKG + HD], everything)
        step(4, total(tbl_ref), everything)
        step(5, S[4:5, C_SINK:C_SINK + NH], everything)
        step(6, S[4:5, C_GCO:C_GCO + CW], everything)
        step(7, S[4:5, C_GAO:C_GAO + AW], everything)
        step(8, S[4:5, C_G2:C_G2 + D], everything)
        for r in range(3):
            step(9, fw[r:r + 1, :], (r, slice(None), slice(None)))
        step(10, S[3:4, 0:2 * DFF], everything)
        sq = S[:, C_SQ:C_SQ + 128]
        loss_o[...] = jnp.sum(jnp.sum(sq, axis=1, keepdims=True), axis=0, keepdims=True) * (0.5 / D)

    def full(a):
        n = len(a.shape)
        return pl.BlockSpec(a.shape, lambda i, chip_ref: (0,) * n)

    params = [*ws, *ms, *vs]
    out = _call(
        body, (p_all, p_all, p_all, p_all, p_all, g1_all, tbl_all, *params), name="small_adam", grid=(1,), prefetch=(chip,),
        in_specs=[full(p_all),
                  pl.BlockSpec((N_DEV, 8, fw_cols), lambda i, chip_ref: (0, 0, chip_ref[0])),
                  *[pl.BlockSpec((N_DEV, 8, cw_cols), lambda i, chip_ref, r=r: (0, 0, (C_DCW + r * CW) // cw_cols + chip_ref[0]))
                    for r in range(3)],
                  full(g1_all), full(tbl_all), *[full(a) for a in params]],
        out_specs=[full(a) for a in ws] * 4 + [pl.BlockSpec((1, 1), lambda i, chip_ref: (0, 0))],
        out_shape=[SDS(a.shape, F32) for a in ws] * 4 + [SDS((1, 1), F32)], sem=("arbitrary",), vmem_mib=32)
    return out[0:N_SMALL], out[N_SMALL:2 * N_SMALL], out[2 * N_SMALL:3 * N_SMALL], out[3 * N_SMALL:4 * N_SMALL], out[4 * N_SMALL]


PLACE_STEPS = 4


def _place_specs(shards):
    rows = [s.shape[0] // PLACE_STEPS for s in shards]
    return ([pl.BlockSpec((r, D), lambda i, chip_ref: (i, 0)) for r in rows],
            [pl.BlockSpec((r, D), lambda i, chip_ref: (chip_ref[0] * PLACE_STEPS + i, 0)) for r in rows],
            [SDS((N_CHIPS * s.shape[0], D), BF) for s in shards])


def _place_first(chip, shard, conv_w, ffn_conv_w):
    def body(chip_ref, a, s0, s1, o, t0, t1):
        o[...] = a[...].astype(BF)

        @pl.when(pl.program_id(0) == 0)
        def _():
            for s, t in ((s0, t0), (s1, t1)):
                t[...] = jnp.zeros_like(t)
                t[0, 0:3, :] = s[...]

    ins, outs, shapes = _place_specs([shard])
    taps = (conv_w, ffn_conv_w)
    return _call(
        body, (shard, conv_w, ffn_conv_w), name="place_first", grid=(PLACE_STEPS,), prefetch=(chip,),
        in_specs=ins + [pl.BlockSpec(s.shape, lambda i, chip_ref: (0, 0)) for s in taps],
        out_specs=outs + [pl.BlockSpec((1, 8, s.shape[1]), lambda i, chip_ref: (chip_ref[0], 0, 0)) for s in taps],
        out_shape=shapes + [SDS((N_CHIPS, 8, s.shape[1]), F32) for s in taps],
        sem=("arbitrary",), vmem_mib=32, free=(0, 1, 2))


def _place_rest(chip, shards, table, bucket, comm):
    n = len(shards)

    def body(chip_ref, *refs):
        a, (tab_ref, bk_ref), o, bias_ref = refs[:n], refs[n:n + 2], refs[n + 2:2 * n + 2], refs[2 * n + 2]
        for src, dst in zip(a, o):
            dst[...] = src[...].astype(BF)

        @pl.when(pl.program_id(0) == 0)
        def _():
            bk = bk_ref[...]
            eq = [bk == b for b in range(NBUCKET)]
            for h in range(NH):
                acc = jnp.zeros((BLK, 2 * BLK), F32)
                for b in range(NBUCKET):
                    acc = jnp.where(eq[b], tab_ref[h, b], acc)
                bias_ref[h * BLK:(h + 1) * BLK, :] = acc

    ins, outs, shapes = _place_specs(shards)
    return _call(
        body, (*shards, table, bucket), name="place_rest", grid=(PLACE_STEPS,), prefetch=(chip,),
        in_specs=ins + [pl.BlockSpec(memory_space=pltpu.SMEM), pl.BlockSpec(bucket.shape, lambda i, chip_ref: (0, 0))],
        out_specs=outs + [pl.BlockSpec((NH * BLK, 2 * BLK), lambda i, chip_ref: (0, 0))],
        out_shape=shapes + [SDS((NH * BLK, 2 * BLK), F32)],
        sem=("arbitrary",), vmem_mib=32, comm=comm, free=tuple(range(n + 2)))


def kernel(x, norm_mix_g, w_in, conv_w, q_norm_g, k_norm_g, rel_bias_table, sinks, out_norm_conv_g, out_norm_attn_g, w_out, norm_ffn_g, w_up, ffn_conv_w, ffn_conv_b, w_down, loss_target, m_norm_mix_g, m_w_in, m_conv_w, m_q_norm_g, m_k_norm_g, m_rel_bias_table, m_sinks, m_out_norm_conv_g, m_out_norm_attn_g, m_w_out, m_norm_ffn_g, m_w_up, m_ffn_conv_w, m_ffn_conv_b, m_w_down, v_norm_mix_g, v_w_in, v_conv_w, v_q_norm_g, v_k_norm_g, v_rel_bias_table, v_sinks, v_out_norm_conv_g, v_out_norm_attn_g, v_w_out, v_norm_ffn_g, v_w_up, v_ffn_conv_w, v_ffn_conv_b, v_w_down):
    as_arg = lambda i: jnp.reshape(i, (1,)).astype(jnp.int32)
    chip = as_arg(2 * lax.axis_index("x") + lax.axis_index("y"))
    core = as_arg(lax.axis_index("c"))
    me = 2 * chip + core
    xs, tgt = x[0], loss_target[0]
    qg, kg, gco, gao, g1, g2, fb = q_norm_g, k_norm_g, out_norm_conv_g, out_norm_attn_g, norm_mix_g, norm_ffn_g, ffn_conv_b
    pieces = lambda g: g.reshape(N_DEV, g.shape[0] // N_DEV, D)
    whole = lambda f: f.reshape(2 * f.shape[1], D)

    bucket = jnp.asarray(_bucket_table())
    p_in, p_cw, p_fw = _place_first(chip, w_in[0].T, conv_w[0], ffn_conv_w[0])
    p_out, p_up, p_down, bias, w_int, cw_all, fw_all = _place_rest(
        chip, [w_out[0], w_up[0].T, w_down[0]], rel_bias_table.T, bucket,
        comm=[_t_gather(p_in), _t_small_weights(p_cw), _t_small_weights(p_fw)])
    cw8 = jnp.transpose(cw_all, (1, 0, 2)).reshape(8, CW)
    fw8 = jnp.transpose(fw_all, (1, 0, 2)).reshape(8, 2 * DFF)

    proj, u1, w_out_f = _inproj(xs, g1, w_int, comm=[_t_gather(p_out)])
    y, w_upt = _mix_fwd(proj, sinks, cw8, qg, kg, gco, gao, bias, comm=[_t_gather(p_up)])
    h1, u2 = _outproj(y, w_out_f, xs, g2)
    up, w_down_f = _ffn_up(u2, w_upt, comm=[_t_gather(p_down)])
    a, = _ffn_act(up, fw8, fb)
    dh2, dh2b, sq = _ffn_down(a, w_down_f, h1, tgt)

    gdbf, = _wgrad("wgrad_down", [a], dh2b)
    da, sib_down = _ffn_down_bwd(dh2b, w_down_f, comm=[_t_sibling(pieces(gdbf))])
    pbf_down, own_down = _chip_sum("chip_sum_w_down", pieces(gdbf), sib_down, core, chip)
    dug, duv, dfwg, dfwv, dfbg, dfbv, chips_down = _ffn_act_bwd(up, da, fw8, fb, comm=[_t_chips(pbf_down)])
    fin_down = _final_sum("final_sum_w_down", own_down, chips_down, core)
    gubf, = _wgrad("wgrad_up", [dug, duv], u2)
    dh1, dh1b, dg2, sib_up, fin_down = _norm_matmul_bwd(
        "ffn_up_bwd", [dug, duv], w_upt, [0, DFF], h1, g2, dh2, True, comm=[_t_sibling(pieces(gubf)), _t_swap(fin_down)])
    pbf_up, own_up = _chip_sum("chip_sum_w_up", pieces(gubf), sib_up, core, chip)
    gobf, = _wgrad("wgrad_out", [y], dh1b)
    dy, sib_out = _out_bwd(dh1b, w_out_f, comm=[_t_sibling(pieces(gobf))])
    pbf_out, own_out = _chip_sum("chip_sum_w_out", pieces(gobf), sib_out, core, chip)
    dproj, dcw8, dqg, dkg, dgco, dgao, dsink, dbias, chips_up, chips_out = _mix_bwd(
        proj, dy, sinks, cw8, qg, kg, gco, gao, bias, comm=[_t_chips(pbf_up), _t_chips(pbf_out)])
    fin_up = _final_sum("final_sum_w_up", own_up, chips_up, core)
    fin_out = _final_sum("final_sum_w_out", own_out, chips_out, core)
    tbl_all = _band_bias_bwd(dbias, bucket, me)
    p_all = _pack_small(me, dfwg, dfwv, dfbg, dfbv, dg2, dgco, dgao, dcw8, dqg, dkg, dsink, sq)
    gibf, fin_up, p_all, tbl_all = _wgrad(
        "wgrad_in", [dproj], u1, comm=[_t_swap(fin_up), _t_allgather(p_all), _t_allgather(tbl_all)])
    sib_in, fin_out = _comm_call("to_sibling_last", [_t_sibling(pieces(gibf)), _t_swap(fin_out)])
    pbf_in, own_in = _chip_sum("chip_sum_w_in", pieces(gibf), sib_in, core, chip)
    dx, g1_all, chips_in = _norm_matmul_bwd(
        "in_bwd", [dproj], w_int, [0], xs, g1, dh1, False, comm=[_t_chips(pbf_in)], slot=me)
    fin_in = _final_sum("final_sum_w_in", own_in, chips_in, core)
    g1_all, fin_in = _comm_call("gather_last", [_t_allgather(g1_all), _t_swap(fin_in)])

    g_w_out, g_w_up, g_w_down = whole(fin_out), whole(fin_up).T, whole(fin_down)
    g_w_down, d_down, nm_down, nv_down = _adamw("adamw_w_down", w_down[0], g_w_down, m_w_down[0], v_w_down[0], 352, True)
    d_up, nm_up, nv_up = _adamw("adamw_w_up", w_up[0], g_w_up, m_w_up[0], v_w_up[0], 256, stage=False)
    g_w_out, d_out, nm_out, nv_out = _adamw("adamw_w_out", w_out[0], g_w_out, m_w_out[0], v_w_out[0], 256, True)
    g_w_in, d_in, nm_in, nv_in = [a.T for a in _adamw(
        "adamw_w_in", w_in[0].T, whole(fin_in), m_w_in[0].T, v_w_in[0].T, INW // N_CHIPS // 3, True)]
    taps = lambda a: jnp.transpose(a, (1, 0, 2))
    sw = [norm_mix_g, taps(conv_w), q_norm_g, k_norm_g, rel_bias_table.T, sinks, out_norm_conv_g, out_norm_attn_g,
          norm_ffn_g, taps(ffn_conv_w), ffn_conv_b]
    smm = [m_norm_mix_g, taps(m_conv_w), m_q_norm_g, m_k_norm_g, m_rel_bias_table.T, m_sinks, m_out_norm_conv_g,
           m_out_norm_attn_g, m_norm_ffn_g, taps(m_ffn_conv_w), m_ffn_conv_b]
    smv = [v_norm_mix_g, taps(v_conv_w), v_q_norm_g, v_k_norm_g, v_rel_bias_table.T, v_sinks, v_out_norm_conv_g,
           v_out_norm_attn_g, v_norm_ffn_g, taps(v_ffn_conv_w), v_ffn_conv_b]
    *small_out, loss = _small_adam(chip, p_all, g1_all, tbl_all, sw, smm, smv)
    sg, sd, snm, snv = [list(r) for r in small_out]
    for r in (sg, sd, snm, snv):
        r[1], r[4], r[9] = taps(r[1]), r[4].T, taps(r[9])

    def order(s, b_in, b_out, b_up, b_down):
        return (s[0], b_in[None], s[1], s[2], s[3], s[4], s[5], s[6], s[7], b_out[None], s[8], b_up[None],
                s[9], s[10], b_down[None])

    return (loss.reshape(()), dx[None],
            *order(sg, g_w_in, g_w_out, g_w_up, g_w_down),
            *order(sd, d_in, d_out, d_up, d_down),
            *order(snm, nm_in, nm_out, nm_up, nm_down),
            *order(snv, nv_in, nv_out, nv_up, nv_down))
```

```python
import functools
import math

import numpy as np

import jax
import jax.numpy as jnp
from jax import lax
from jax.experimental import pallas as pl
from jax.experimental.pallas import tpu as pltpu

F32 = jnp.float32
BF = jnp.bfloat16
SDS = jax.ShapeDtypeStruct

T = 2048
D = 1024
CW = 512
AW = 512
HD = 64
NH = 8
NKV = 2
GQ = 4
INW = 2304
DFF = 2816
BLK = 128
NB = T // BLK
NBUCKET = 32
EPS = 1e-6
NEG_INF = -1e30
N_CHIPS = 4
N_DEV = 8

ADAM_LR = 0.001
ADAM_B1 = 0.9
ADAM_B2 = 0.999
ADAM_EPS = 1e-08
ADAM_WD = 0.01
ADAM_STEP = 10

TM = 512
MIB = 1024 * 1024
MESH = pl.DeviceIdType.MESH
ANY = pl.BlockSpec(memory_space=pl.ANY)

_pcall = pl.pallas_call


def _params(sem=None, vmem_mib=None):
    kw = {}
    if sem is not None:
        kw["dimension_semantics"] = sem
    if vmem_mib is not None:
        kw["vmem_limit_bytes"] = vmem_mib * MIB
    return pltpu.CompilerParams(**kw)


def _resident(shape):
    return pl.BlockSpec(shape, lambda *_: (0,) * len(shape), pipeline_mode=pl.Buffered(1))


def _dot(a, b, ca, cb):
    return lax.dot_general(a, b, (((ca,), (cb,)), ((), ())), preferred_element_type=F32)


def _rms_bwd(dy, x, r, g):
    dg = jnp.sum(dy * (x * r), axis=0, keepdims=True)
    dgx = dy * g
    dx = r * dgx - x * (r * r * r) * jnp.mean(x * dgx, axis=-1, keepdims=True)
    return dx, dg


def _where():
    x, y, c = lax.axis_index("x"), lax.axis_index("y"), lax.axis_index("c")
    return x, y, c, [(1 - x, y), (x, 1 - y), (1 - x, 1 - y)]


def _rcopy(src, dst, ssem, rsem, dev):
    return pltpu.make_async_remote_copy(src_ref=src, dst_ref=dst, send_sem=ssem, recv_sem=rsem, device_id=dev,
                                        device_id_type=MESH)


class _Task:
    def __init__(self, ins, outs, alias, n_sem, start, finish, middle=None):
        self.ins, self.outs, self.alias, self.n_sem, self.start, self.finish = ins, outs, alias, n_sem, start, finish
        self.middle = middle if middle is not None else (lambda *args: None)


ROWS16 = 16


def _t_gather(placed, part=(0, 1)):
    R = placed.shape[0] // N_CHIPS
    q = R // 4
    lo, hi = (round(f * (q // ROWS16)) * ROWS16 for f in part)

    def quarter(chip_index, core, k):
        return pl.ds(pl.multiple_of(chip_index * R + core * 2 * q + k * q + lo, ROWS16), hi - lo)

    def places():
        x, y, c, _ = _where()
        return c, 2 * x + y, 2 * (1 - x) + y, 2 * x + (1 - y), 2 * (1 - x) + (1 - y), (1 - x, y, c), (x, 1 - y, c), (x, y, 1 - c)

    def copy(buf, k, chip_index, core, quart, ss, rs, b, dev):
        window = buf.at[quarter(chip_index, core, quart)]
        return _rcopy(window, window, ss.at[b + k], rs.at[b + k], dev)

    def start(cin, cout, ss, rs, b):
        c, me, _, _, _, x_nbr, y_nbr, _ = places()
        for k, (quart, dev) in enumerate(((0, x_nbr), (1, y_nbr), (1, x_nbr), (0, y_nbr))):
            copy(cout[0], k, me, c, quart, ss, rs, b, dev).start()

    def middle(cin, cout, ss, rs, b):
        c, _, xc, yc, _, x_nbr, y_nbr, sib = places()
        for k, chip_index, quart, dev in ((0, xc, 0, y_nbr), (1, yc, 1, x_nbr)):
            copy(cout[0], k, chip_index, c, quart, ss, rs, b, dev).wait_recv()
            copy(cout[0], 4 + k, chip_index, c, quart, ss, rs, b, dev).start()
            copy(cout[0], 6 + k, chip_index, c, quart, ss, rs, b, sib).start()

    later = ((2, 1, 1), (3, 2, 0), (4, 3, 0), (5, 3, 1))

    def finish(cin, cout, ss, rs, b):
        c, me, xc, yc, dc, _, _, sib = places()
        chip_of = {1: xc, 2: yc, 3: dc}
        for k, whose, quart in later:
            copy(cout[0], k, chip_of[whose], c, quart, ss, rs, b, sib).wait_recv()
            copy(cout[0], 6 + k, chip_of[whose], c, quart, ss, rs, b, sib).start()
        for k, whose, quart in ((0, 1, 0), (1, 2, 1)) + later:
            copy(cout[0], 6 + k, chip_of[whose], 1 - c, quart, ss, rs, b, sib).wait_recv()
        for k in range(12):
            copy(cout[0], k, me, c, 0, ss, rs, b, sib).wait_send()

    return _Task([placed], [SDS(placed.shape, placed.dtype)], [(0, 0)], 12, start, finish, middle)


def _t_small_weights(buf):
    def start(cin, cout, ss, rs, b):
        x, y, c, chips = _where()
        mine = cout[0].at[2 * x + y]
        for r, (px, py) in enumerate(chips):
            _rcopy(mine, mine, ss.at[b + r], rs.at[b + r], (px, py, c)).start()

    def finish(cin, cout, ss, rs, b):
        x, y, c, chips = _where()
        for r, (px, py) in enumerate(chips):
            got = cout[0].at[2 * px + py]
            _rcopy(got, got, ss.at[b + r], rs.at[b + r], (px, py, c)).wait_recv()
        for r, (px, py) in enumerate(chips):
            mine = cout[0].at[2 * x + y]
            _rcopy(mine, mine, ss.at[b + r], rs.at[b + r], (px, py, c)).wait_send()

    return _Task([buf], [SDS(buf.shape, buf.dtype)], [(0, 0)], 3, start, finish)


def _t_sibling(gbf):
    def start(cin, cout, ss, rs, b):
        x, y, c, _ = _where()
        for jj in range(N_CHIPS):
            _rcopy(cin[0].at[2 * jj + (1 - c)], cout[0].at[jj], ss.at[b + jj], rs.at[b + jj], (x, y, 1 - c)).start()

    def finish(cin, cout, ss, rs, b):
        x, y, c, _ = _where()
        for jj in range(N_CHIPS):
            got = cout[0].at[jj]
            _rcopy(got, got, ss.at[b + jj], rs.at[b + jj], (x, y, 1 - c)).wait_recv()
        for jj in range(N_CHIPS):
            got = cout[0].at[jj]
            _rcopy(got, got, ss.at[b + jj], rs.at[b + jj], (x, y, 1 - c)).wait_send()

    return _Task([gbf], [SDS((N_CHIPS,) + gbf.shape[1:], BF)], [], N_CHIPS, start, finish)


def _t_chips(pbf):
    def start(cin, cout, ss, rs, b):
        x, y, c, chips = _where()
        for r, (px, py) in enumerate(chips):
            _rcopy(cin[0].at[2 * px + py], cout[0].at[r], ss.at[b + r], rs.at[b + r], (px, py, c)).start()

    def finish(cin, cout, ss, rs, b):
        x, y, c, chips = _where()
        for r, (px, py) in enumerate(chips):
            got = cout[0].at[r]
            _rcopy(got, got, ss.at[b + r], rs.at[b + r], (px, py, c)).wait_recv()
        for r, (px, py) in enumerate(chips):
            got = cout[0].at[r]
            _rcopy(got, got, ss.at[b + r], rs.at[b + r], (px, py, c)).wait_send()

    return _Task([pbf], [SDS((3,) + pbf.shape[1:], BF)], [], 3, start, finish)


def _t_swap(fin):
    def start(cin, cout, ss, rs, b):
        x, y, c, _ = _where()
        mine = cout[0].at[c]
        _rcopy(mine, mine, ss.at[b], rs.at[b], (x, y, 1 - c)).start()

    def finish(cin, cout, ss, rs, b):
        x, y, c, _ = _where()
        got = cout[0].at[1 - c]
        _rcopy(got, got, ss.at[b], rs.at[b], (x, y, 1 - c)).wait_recv()
        _rcopy(got, got, ss.at[b], rs.at[b], (x, y, 1 - c)).wait_send()

    return _Task([fin], [SDS(fin.shape, fin.dtype)], [(0, 0)], 1, start, finish)


def _t_allgather(buf):
    def peers():
        x, y, c, _ = _where()
        out = []
        for rel in range(1, N_DEV):
            px, py, pc = x ^ ((rel >> 2) & 1), y ^ ((rel >> 1) & 1), c ^ (rel & 1)
            out.append((rel - 1, 4 * px + 2 * py + pc, (px, py, pc)))
        return 4 * x + 2 * y + c, out

    def start(cin, cout, ss, rs, b):
        me, ps = peers()
        mine = cout[0].at[me]
        for k, _, dev in ps:
            _rcopy(mine, mine, ss.at[b + k], rs.at[b + k], dev).start()

    def finish(cin, cout, ss, rs, b):
        me, ps = peers()
        for k, pidx, dev in ps:
            got = cout[0].at[pidx]
            _rcopy(got, got, ss.at[b + k], rs.at[b + k], dev).wait_recv()
        for k, _, dev in ps:
            mine = cout[0].at[me]
            _rcopy(mine, mine, ss.at[b + k], rs.at[b + k], dev).wait_send()

    return _Task([buf], [SDS(buf.shape, buf.dtype)], [(0, 0)], N_DEV - 1, start, finish)


def _run_tasks(comm, which, cin, cout, ss, rs):
    i0 = o0 = s0 = 0
    for t in comm:
        getattr(t, which)(cin[i0:i0 + len(t.ins)], cout[o0:o0 + len(t.outs)], ss, rs, s0)
        i0, o0, s0 = i0 + len(t.ins), o0 + len(t.outs), s0 + t.n_sem


def _from_hbm(*arrays):
    return [pltpu.with_memory_space_constraint(a, pltpu.HBM) for a in arrays]


def _in_hbm(shapes):
    return [pltpu.HBM(s.shape, s.dtype) for s in shapes]


def _comm_layout(comm, n_in, n_out):
    c_in = [a for t in comm for a in t.ins]
    c_out = [s for t in comm for s in t.outs]
    aliases, i0, o0 = {}, 0, 0
    for t in comm:
        for i, o in t.alias:
            aliases[n_in + i0 + i] = n_out + o0 + o
        i0, o0 = i0 + len(t.ins), o0 + len(t.outs)
    return c_in, c_out, aliases, sum(t.n_sem for t in comm)


def _call(body, operands, *, name, grid, in_specs, out_specs, out_shape, scratch_shapes=(), sem=None, vmem_mib=None, comm=(),
          free=(), prefetch=()):
    operands = [o if s.memory_space == pltpu.SMEM or k in free else pltpu.with_memory_space_constraint(o, pltpu.HBM)
                for k, (o, s) in enumerate(zip(operands, in_specs))]
    n_pre, n_in, n_out, n_scr = len(prefetch), len(in_specs), len(out_specs), len(scratch_shapes)
    c_in, c_out, aliases, n_sem = _comm_layout(comm, n_pre + n_in, n_out)
    sems = [pltpu.SemaphoreType.DMA((n_sem,)), pltpu.SemaphoreType.DMA((n_sem,))] if comm else []

    def wrapped(*refs):
        pre, refs = refs[:n_pre], refs[n_pre:]
        ins, cin = refs[:n_in], refs[n_in:n_in + len(c_in)]
        rest = refs[n_in + len(c_in):]
        outs, cout = rest[:n_out], rest[n_out:n_out + len(c_out)]
        rest = rest[n_out + len(c_out):]
        scr, csem = rest[:n_scr], rest[n_scr:]
        if not comm:
            return body(*pre, *ins, *outs, *scr)
        step = functools.reduce(lambda acc, k: acc * grid[k] + pl.program_id(k), range(len(grid)), 0)
        n_steps = math.prod(grid)
        pl.when(step == 0)(lambda: _run_tasks(comm, "start", cin, cout, *csem))
        pl.when(step == n_steps // 2)(lambda: _run_tasks(comm, "middle", cin, cout, *csem))
        body(*pre, *ins, *outs, *scr)
        pl.when(step == n_steps - 1)(lambda: _run_tasks(comm, "finish", cin, cout, *csem))

    grid_spec = pltpu.PrefetchScalarGridSpec(
        num_scalar_prefetch=n_pre, grid=grid, in_specs=list(in_specs) + [ANY] * len(c_in),
        out_specs=list(out_specs) + [ANY] * len(c_out), scratch_shapes=list(scratch_shapes) + sems)
    return _pcall(
        wrapped, name=name, grid_spec=grid_spec, out_shape=_in_hbm(list(out_shape) + c_out), input_output_aliases=aliases,
        compiler_params=_params(("arbitrary",) * len(grid) if comm else sem, vmem_mib),
    )(*prefetch, *operands, *_from_hbm(*c_in))


def _comm_call(name, comm):
    c_in, c_out, aliases, n_sem = _comm_layout(comm, 0, 0)

    def body(*refs):
        cin, cout, (ss, rs) = refs[:len(c_in)], refs[len(c_in):len(c_in) + len(c_out)], refs[len(c_in) + len(c_out):]
        for phase in ("start", "middle", "finish"):
            _run_tasks(comm, phase, cin, cout, ss, rs)

    return _pcall(
        body, name=name, in_specs=[ANY] * len(c_in), out_specs=[ANY] * len(c_out), out_shape=_in_hbm(c_out),
        scratch_shapes=[pltpu.SemaphoreType.DMA((n_sem,)), pltpu.SemaphoreType.DMA((n_sem,))],
        input_output_aliases=aliases,
    )(*_from_hbm(*c_in))


def _inproj(x, g1, w_int, comm=()):
    tm = TM

    def body(x_ref, g_ref, w_ref, proj_ref, u_ref):
        xf = x_ref[...]
        r = lax.rsqrt(jnp.mean(xf * xf, axis=-1, keepdims=True) + EPS)
        u = (xf * r * g_ref[...]).astype(BF)
        u_ref[...] = u
        proj_ref[...] = _dot(u, w_ref[...], 1, 1)

    return _call(
        body, (x, g1, w_int), name="inproj", grid=(T // tm,),
        in_specs=[pl.BlockSpec((tm, D), lambda i: (i, 0)), pl.BlockSpec((1, D), lambda i: (0, 0)),
                  _resident((INW, D))],
        out_specs=[pl.BlockSpec((tm, INW), lambda i: (i, 0)), pl.BlockSpec((tm, D), lambda i: (i, 0))],
        out_shape=[SDS((T, INW), F32), SDS((T, D), BF)], sem=("parallel",), vmem_mib=40, comm=comm)


def _outproj(y, w_out, x, g2):
    tm = TM

    def body(y_ref, w_ref, x_ref, g_ref, h1_ref, u2_ref):
        h1 = x_ref[...] + _dot(y_ref[...], w_ref[...], 1, 0)
        h1_ref[...] = h1
        r = lax.rsqrt(jnp.mean(h1 * h1, axis=-1, keepdims=True) + EPS)
        u2_ref[...] = (h1 * r * g_ref[...]).astype(BF)

    return _call(
        body, (y, w_out, x, g2), name="outproj", grid=(T // tm,),
        in_specs=[pl.BlockSpec((tm, D), lambda i: (i, 0)), _resident((D, D)),
                  pl.BlockSpec((tm, D), lambda i: (i, 0)), pl.BlockSpec((1, D), lambda i: (0, 0))],
        out_specs=[pl.BlockSpec((tm, D), lambda i: (i, 0)), pl.BlockSpec((tm, D), lambda i: (i, 0))],
        out_shape=[SDS((T, D), F32), SDS((T, D), BF)], sem=("parallel",), vmem_mib=32)


def _ffn_up(u2, w_upt, comm=()):
    tm, tn = 1024, 512

    def body(u_ref, w_ref, o_ref):
        o_ref[...] = _dot(u_ref[...], w_ref[...], 1, 1).astype(BF)

    return _call(
        body, (u2, w_upt), name="ffn_up", grid=(T // tm, 2 * DFF // tn),
        in_specs=[pl.BlockSpec((tm, D), lambda i, j: (i, 0)), pl.BlockSpec((tn, D), lambda i, j: (j, 0))],
        out_specs=[pl.BlockSpec((tm, tn), lambda i, j: (i, j))], out_shape=[SDS((T, 2 * DFF), BF)],
        sem=("parallel", "parallel"), vmem_mib=32, comm=comm)


def _ffn_down(a, w_down, h1, tgt):
    tm = TM

    def body(a_ref, w_ref, h1_ref, t_ref, dh_ref, dhb_ref, l_ref):
        @pl.when(pl.program_id(0) == 0)
        def _():
            l_ref[...] = jnp.zeros_like(l_ref)

        h2 = h1_ref[...] + _dot(a_ref[...], w_ref[...], 1, 0)
        e = h2 - t_ref[...]
        dh = e * (1.0 / D)
        dh_ref[...] = dh
        dhb_ref[...] = dh.astype(BF)
        e2 = jnp.sum((e * e).reshape(tm // 8, 8, D), axis=0)
        acc = e2[:, 0:128]
        for k in range(1, D // 128):
            acc = acc + e2[:, k * 128:(k + 1) * 128]
        l_ref[...] += acc

    return _call(
        body, (a, w_down, h1, tgt), name="ffn_down", grid=(T // tm,),
        in_specs=[pl.BlockSpec((tm, DFF), lambda i: (i, 0)), _resident((DFF, D)),
                  pl.BlockSpec((tm, D), lambda i: (i, 0)), pl.BlockSpec((tm, D), lambda i: (i, 0))],
        out_specs=[pl.BlockSpec((tm, D), lambda i: (i, 0)), pl.BlockSpec((tm, D), lambda i: (i, 0)),
                   pl.BlockSpec((8, 128), lambda i: (0, 0))],
        out_shape=[SDS((T, D), F32), SDS((T, D), BF), SDS((8, 128), F32)], sem=("arbitrary",), vmem_mib=40)


def _bucket_table():
    q = np.arange(BLK, dtype=np.int32)[:, None]
    j = np.arange(2 * BLK, dtype=np.int32)[None, :]
    n = np.maximum(q + BLK - j, 0)
    nf = np.maximum(n, 1).astype(np.float32)
    max_exact = NBUCKET // 2
    large = max_exact + (np.log(nf / np.float32(max_exact)) / np.float32(math.log(BLK / max_exact))
                         * np.float32(NBUCKET - max_exact)).astype(np.int32)
    large = np.minimum(large, NBUCKET - 1)
    return np.where(n < max_exact, n, large).astype(np.int32)


def _band_bias_bwd(dbias, bucket, me):
    def body(me_ref, db_ref, bk_ref, o_ref):
        bk = bk_ref[...]
        for b in range(NBUCKET):
            m = bk == b
            for h in range(NH):
                v = jnp.where(m, db_ref[h * BLK:(h + 1) * BLK, :], 0.0)
                s = jnp.sum(jnp.sum(v, axis=1, keepdims=True), axis=0, keepdims=True)
                o_ref[0, h:h + 1, b:b + 1] = s

    grid_spec = pltpu.PrefetchScalarGridSpec(
        num_scalar_prefetch=1, grid=(1,),
        in_specs=[pl.BlockSpec((NH * BLK, 2 * BLK), lambda i, me_ref: (0, 0)),
                  pl.BlockSpec((BLK, 2 * BLK), lambda i, me_ref: (0, 0))],
        out_specs=pl.BlockSpec((1, NH, NBUCKET), lambda i, me_ref: (me_ref[0], 0, 0)),
    )
    return _pcall(body, name="band_bias_bwd", grid_spec=grid_spec, out_shape=SDS((N_DEV, NH, NBUCKET), F32),
                  compiler_params=_params(("arbitrary",)))(me, dbias, bucket)


def _two_bf16(x):
    hi = x.astype(BF)
    return hi, (x - hi.astype(F32)).astype(BF)


def _head_sums(x, seg):
    hi, lo = _two_bf16(x)
    s = seg[0:x.shape[1], :]
    return _dot(hi, s, 1, 0) + _dot(lo, s, 1, 0)


def _head_spread(v, seg, width):
    hi, lo = _two_bf16(v)
    s = seg[0:width, :]
    return _dot(hi, s, 1, 1) + _dot(lo, s, 1, 1)


def _head_norm(x, g_t, seg, by_head=False):
    if by_head:
        heads = [x[:, h * HD:(h + 1) * HD] for h in range(x.shape[1] // HD)]
        r = jnp.concatenate([jnp.broadcast_to(lax.rsqrt(jnp.mean(v * v, axis=-1, keepdims=True) + EPS), v.shape)
                             for v in heads], axis=1)
    else:
        r = lax.rsqrt(_head_sums(x * x, seg) * (1.0 / HD) + EPS)
        r = _head_spread(r, seg, x.shape[1])
    return x * r * g_t, r


def _head_norm_bwd(dy, x, r, g_t, seg):
    dg_t = jnp.sum(dy * (x * r), axis=0, keepdims=True)
    dgx = dy * g_t
    mean = _head_spread(_head_sums(x * dgx, seg) * (1.0 / HD), seg, x.shape[1])
    return r * dgx - x * (r * r * r) * mean, dg_t


def _fold_heads(v):
    out = v[:, 0:HD]
    for h in range(1, v.shape[1] // HD):
        out = out + v[:, h * HD:(h + 1) * HD]
    return out


def _mix_forward(P, zc8, zh8, pkv, first, cw, qg_t, kg_t, gco, gao, seg, sink_ref, bias_ref, by_head=False):
    gate_b = P[:, 0:CW]
    gate_c = P[:, CW:2 * CW]
    hc = P[:, 2 * CW:3 * CW]
    z = gate_c * hc
    keep = jnp.where(first, 0.0, 1.0)
    zp = zc8 * zh8 * keep
    p1 = zp[7:8, :]
    p2 = zp[6:7, :]
    row = lax.broadcasted_iota(jnp.int32, (BLK, 1), 0)
    z1 = jnp.where(row == 0, p1, pltpu.roll(z, 1, 0))
    z2 = jnp.where(row == 0, p2, jnp.where(row == 1, p1, pltpu.roll(z, 2, 0)))
    cz = cw[0:1, :] * z2 + cw[1:2, :] * z1 + cw[2:3, :] * z
    y_conv = gate_b * cz

    scale = HD ** -0.5
    qi = lax.broadcasted_iota(jnp.int32, (GQ * BLK, 2 * BLK), 0) & (BLK - 1)
    kj = lax.broadcasted_iota(jnp.int32, (GQ * BLK, 2 * BLK), 1)
    dd = qi + BLK - kj
    first_key = jnp.where(first, BLK, 0)
    valid = (dd >= 0) & (dd < BLK) & (kj >= first_key)

    q0 = 3 * CW
    k0 = q0 + AW
    v0 = k0 + NKV * HD
    q_raw = P[:, q0:k0]
    qn, rq = _head_norm(q_raw, qg_t, seg, by_head)
    qs = (qn * scale).astype(BF)
    k_raw = jnp.concatenate([pkv[:, 0:NKV * HD], P[:, k0:v0]], axis=0)
    kn, rk = _head_norm(k_raw, kg_t, seg, by_head)
    knb = kn.astype(BF)
    heads = []
    outs = []
    for kv in range(NKV):
        kb = knb[:, kv * HD:(kv + 1) * HD]
        vb = jnp.concatenate([pkv[:, NKV * HD + kv * HD:NKV * HD + (kv + 1) * HD],
                              P[:, v0 + kv * HD:v0 + (kv + 1) * HD]], axis=0).astype(BF)
        Q = jnp.concatenate([qs[:, (kv * GQ + g) * HD:(kv * GQ + g + 1) * HD] for g in range(GQ)], axis=0)
        S = _dot(Q, kb, 1, 1) + bias_ref[kv * GQ * BLK:(kv + 1) * GQ * BLK, :]
        S = jnp.where(valid, S, NEG_INF)
        sink = jnp.concatenate([jnp.full((BLK, 1), sink_ref[0, kv * GQ + g], F32) for g in range(GQ)], axis=0)
        m = jnp.maximum(jnp.max(S, axis=-1, keepdims=True), sink)
        p = jnp.exp(S - m)
        es = jnp.exp(sink - m)
        denom = jnp.sum(p, axis=-1, keepdims=True) + es
        probs = p / denom
        O = _dot(probs.astype(BF), vb, 1, 0)
        heads.append(dict(kb=kb, vb=vb, Q=Q, probs=probs, psink=es / denom, O=O))
        outs += [O[g * BLK:(g + 1) * BLK, :] for g in range(GQ)]
    y_attn = jnp.concatenate(outs, axis=1)

    rc = lax.rsqrt(jnp.mean(y_conv * y_conv, axis=-1, keepdims=True) + EPS)
    ra = lax.rsqrt(jnp.mean(y_attn * y_attn, axis=-1, keepdims=True) + EPS)
    y = jnp.concatenate([y_conv * rc * gco, y_attn * ra * gao], axis=1)
    return dict(gate_b=gate_b, gate_c=gate_c, hc=hc, z=z, z1=z1, z2=z2, cz=cz, y_conv=y_conv, y_attn=y_attn,
                rc=rc, ra=ra, heads=heads, y=y, row=row, scale=scale, q_raw=q_raw, rq=rq, k_raw=k_raw, rk=rk)


BPS = 2
TILE = BPS * BLK
KV0 = 3 * CW + AW


def _mix_in_specs(tile_of):
    return [
        pl.BlockSpec(memory_space=pltpu.SMEM),
        pl.BlockSpec((TILE, INW), lambda s: (tile_of(s), 0)),
        pl.BlockSpec((8, CW), lambda s: (jnp.maximum(tile_of(s) * (TILE // 8) - 1, 0), 1)),
        pl.BlockSpec((8, CW), lambda s: (jnp.maximum(tile_of(s) * (TILE // 8) - 1, 0), 2)),
        pl.BlockSpec((BLK, 2 * NKV * HD), lambda s: (jnp.maximum(tile_of(s) * BPS - 1, 0), KV0 // (2 * NKV * HD))),
    ]


def _block_inputs(tile, b, zc_ref, zh_ref, pkv_ref, first_tile):
    P = tile[b * BLK:(b + 1) * BLK, :]
    if b == 0:
        return P, zc_ref[...], zh_ref[...], pkv_ref[...], first_tile
    lo = b * BLK
    return P, tile[lo - 8:lo, CW:2 * CW], tile[lo - 8:lo, 2 * CW:3 * CW], tile[lo - BLK:lo, KV0:KV0 + 2 * NKV * HD], False


def _mix_param_specs():
    return [
        pl.BlockSpec((8, CW), lambda s: (0, 0)),
        pl.BlockSpec((1, AW), lambda s: (0, 0)),
        pl.BlockSpec((1, NKV * HD), lambda s: (0, 0)),
        pl.BlockSpec((1, CW), lambda s: (0, 0)),
        pl.BlockSpec((1, AW), lambda s: (0, 0)),
        pl.BlockSpec((AW, 128), lambda s: (0, 0)),
        pl.BlockSpec((NH * BLK, 2 * BLK), lambda s: (0, 0)),
    ]


def _mix_params(cw8, qg, kg, gco, gao, bias):
    seg = np.zeros((AW, 128), np.float32)
    seg[np.arange(AW), np.arange(AW) // HD] = 1.0
    return (cw8, jnp.tile(qg, (1, NH)), jnp.tile(kg, (1, NKV)), gco, gao, jnp.asarray(seg, BF), bias)


def _mix_fwd(proj, sinks, cw8, qg, kg, gco, gao, bias, comm=()):
    def body(sink_ref, p_ref, zc_ref, zh_ref, pkv_ref, cw_ref, qg_ref, kg_ref, gco_ref, gao_ref, seg_ref, bias_ref, y_ref):
        tile = p_ref[...]
        for b in range(BPS):
            f = _mix_forward(*_block_inputs(tile, b, zc_ref, zh_ref, pkv_ref, pl.program_id(0) == 0), cw_ref[...],
                             qg_ref[...], kg_ref[...], gco_ref[...], gao_ref[...], seg_ref[...], sink_ref, bias_ref, by_head=True)
            y_ref[b * BLK:(b + 1) * BLK, :] = f["y"].astype(BF)

    return _call(
        body, (sinks, proj, proj, proj, proj, *_mix_params(cw8, qg, kg, gco, gao, bias)), name="mix_fwd", grid=(T // TILE,),
        in_specs=_mix_in_specs(lambda s: s) + _mix_param_specs(),
        out_specs=[pl.BlockSpec((TILE, D), lambda s: (s, 0))], out_shape=[SDS((T, D), BF)],
        sem=("parallel",), vmem_mib=40, comm=comm)


def _mix_bwd(proj, dy, sinks, cw8, qg, kg, gco, gao, bias, comm=()):
    n_steps = T // TILE

    def tile_of(s):
        return n_steps - 1 - s

    def body(sink_ref, p_ref, zc_ref, zh_ref, pkv_ref, dy_ref, cw_ref, qg_ref, kg_ref, gco_ref, gao_ref, seg_ref, bias_ref,
             dproj_ref, dcw_ref, dqg_ref, dkg_ref, dgco_ref, dgao_ref, dsink_ref, dbias_ref,
             ndcz_ref, dkc_ref, dvc_ref):
        s = pl.program_id(0)

        @pl.when(s == 0)
        def _():
            for r in (dcw_ref, dqg_ref, dkg_ref, dgco_ref, dgao_ref, dsink_ref, dbias_ref, ndcz_ref, dkc_ref, dvc_ref):
                r[...] = jnp.zeros_like(r)

        tile = p_ref[...]
        for b in reversed(range(BPS)):
            one_block(b, _block_inputs(tile, b, zc_ref, zh_ref, pkv_ref, s == n_steps - 1),
                      dy_ref[b * BLK:(b + 1) * BLK, :], sink_ref, cw_ref, qg_ref, kg_ref, gco_ref, gao_ref, seg_ref, bias_ref,
                      dproj_ref.at[b * BLK:(b + 1) * BLK, :], dcw_ref, dqg_ref, dkg_ref, dgco_ref, dgao_ref, dsink_ref,
                      dbias_ref, ndcz_ref, dkc_ref, dvc_ref)

    def one_block(b, inputs, dy, sink_ref, cw_ref, qg_ref, kg_ref, gco_ref, gao_ref, seg_ref, bias_ref,
                  dproj_ref, dcw_ref, dqg_ref, dkg_ref, dgco_ref, dgao_ref, dsink_ref, dbias_ref,
                  ndcz_ref, dkc_ref, dvc_ref):
        cw = cw_ref[...]
        qg_v, kg_v, gco_v, gao_v, seg = qg_ref[...], kg_ref[...], gco_ref[...], gao_ref[...], seg_ref[...]
        f = _mix_forward(*inputs, cw, qg_v, kg_v, gco_v, gao_v, seg, sink_ref, bias_ref)
        dyc, dgco = _rms_bwd(dy[:, 0:CW], f["y_conv"], f["rc"], gco_v)
        dya, dgao = _rms_bwd(dy[:, CW:CW + AW], f["y_attn"], f["ra"], gao_v)
        dgco_ref[...] += dgco
        dgao_ref[...] += dgao

        row = f["row"]
        dgate_b = dyc * f["cz"]
        dcz = dyc * f["gate_b"]
        dcw_ref[0:1, :] += jnp.sum(dcz * f["z2"], axis=0, keepdims=True)
        dcw_ref[1:2, :] += jnp.sum(dcz * f["z1"], axis=0, keepdims=True)
        dcw_ref[2:3, :] += jnp.sum(dcz * f["z"], axis=0, keepdims=True)
        nxt = ndcz_ref[...]
        n0 = nxt[0:1, :]
        n1 = nxt[1:2, :]
        d1 = jnp.where(row == BLK - 1, n0, pltpu.roll(dcz, BLK - 1, 0))
        d2 = jnp.where(row == BLK - 1, n1, jnp.where(row == BLK - 2, n0, pltpu.roll(dcz, BLK - 2, 0)))
        dz = cw[2:3, :] * dcz + cw[1:2, :] * d1 + cw[0:1, :] * d2
        ndcz_ref[...] = dcz[0:8, :]
        dproj_ref[:, 0:CW] = dgate_b.astype(BF)
        dproj_ref[:, CW:2 * CW] = (dz * f["hc"]).astype(BF)
        dproj_ref[:, 2 * CW:3 * CW] = (dz * f["gate_c"]).astype(BF)

        scale = f["scale"]
        lane = lax.broadcasted_iota(jnp.int32, (1, 128), 1)
        dq_cols, dk_cols, dv_cols = [], [], []
        for kv in range(NKV):
            hd = f["heads"][kv]
            dO = jnp.concatenate([dya[:, (kv * GQ + g) * HD:(kv * GQ + g + 1) * HD] for g in range(GQ)], axis=0)
            delta = jnp.sum(dO * hd["O"], axis=-1, keepdims=True)
            dOb = dO.astype(BF)
            dP = _dot(dOb, hd["vb"], 1, 1)
            dS = hd["probs"] * (dP - delta)
            dsk = hd["psink"] * delta
            for g in range(GQ):
                h = kv * GQ + g
                tot = jnp.sum(dsk[g * BLK:(g + 1) * BLK, :], axis=0, keepdims=True)
                dsink_ref[...] -= jnp.where(lane == h, tot, 0.0)
            dbias_ref[kv * GQ * BLK:(kv + 1) * GQ * BLK, :] += dS
            dSb = dS.astype(BF)
            dQ = _dot(dSb, hd["kb"], 1, 0)
            dKb = _dot(dSb, hd["Q"], 0, 0)
            dVb = _dot(hd["probs"].astype(BF), dOb, 0, 0)
            dk_cols.append(dKb[BLK:, :] + dkc_ref[:, kv * HD:(kv + 1) * HD])
            dv_cols.append(dVb[BLK:, :] + dvc_ref[:, kv * HD:(kv + 1) * HD])
            dkc_ref[:, kv * HD:(kv + 1) * HD] = dKb[:BLK, :]
            dvc_ref[:, kv * HD:(kv + 1) * HD] = dVb[:BLK, :]
            dq_cols += [dQ[g * BLK:(g + 1) * BLK, :] for g in range(GQ)]
        dq_raw, dqg_t = _head_norm_bwd(jnp.concatenate(dq_cols, axis=1) * scale, f["q_raw"], f["rq"], qg_v, seg)
        dk_raw, dkg_t = _head_norm_bwd(jnp.concatenate(dk_cols, axis=1), f["k_raw"][BLK:, :], f["rk"][BLK:, :], kg_v, seg)
        dqg_ref[...] += _fold_heads(dqg_t)
        dkg_ref[...] += _fold_heads(dkg_t)
        dproj_ref[:, 3 * CW:INW] = jnp.concatenate([dq_raw, dk_raw] + dv_cols, axis=1).astype(BF)

    small = lambda r, c: pl.BlockSpec((r, c), lambda s: (0, 0))
    return _call(
        body, (sinks, proj, proj, proj, proj, dy, *_mix_params(cw8, qg, kg, gco, gao, bias)), name="mix_bwd", grid=(n_steps,),
        in_specs=_mix_in_specs(tile_of) + [pl.BlockSpec((TILE, D), lambda s: (tile_of(s), 0))] + _mix_param_specs(),
        out_specs=[pl.BlockSpec((TILE, INW), lambda s: (tile_of(s), 0)), small(8, CW), small(1, HD), small(1, HD),
                   small(1, CW), small(1, AW), small(1, 128), small(NH * BLK, 2 * BLK)],
        out_shape=[SDS((T, INW), BF), SDS((8, CW), F32), SDS((1, HD), F32), SDS((1, HD), F32), SDS((1, CW), F32),
                   SDS((1, AW), F32), SDS((1, 128), F32), SDS((NH * BLK, 2 * BLK), F32)],
        scratch_shapes=[pltpu.VMEM((8, CW), F32), pltpu.VMEM((BLK, NKV * HD), F32), pltpu.VMEM((BLK, NKV * HD), F32)],
        sem=("arbitrary",), vmem_mib=56, comm=comm)


FT = 256
NFT = DFF // FT
RC = 128
NCH = T // RC
LEAD = 16


def _rows8(x):
    return jnp.sum(x.reshape(x.shape[0] // 8, 8, x.shape[1]), axis=0)


def _ffn_act_specs():
    return [
        pl.BlockSpec((T, FT), lambda j: (0, j)), pl.BlockSpec((T, FT), lambda j: (0, NFT + j)),
        pl.BlockSpec((8, FT), lambda j: (0, j)), pl.BlockSpec((8, FT), lambda j: (0, NFT + j)),
        pl.BlockSpec((1, FT), lambda j: (0, j)), pl.BlockSpec((1, FT), lambda j: (0, NFT + j)),
    ]


def _conv_rows(win, w, b, n):
    win = win.astype(F32)
    u = win[LEAD:LEAD + n]
    u1 = pltpu.roll(win, 1, 0)[LEAD:LEAD + n]
    u2 = pltpu.roll(win, 2, 0)[LEAD:LEAD + n]
    return u2, u1, u, w[0:1, :] * u2 + w[1:2, :] * u1 + w[2:3, :] * u + b


def _ffn_act(up, fw8, fb, comm=()):
    def body(ug_ref, uv_ref, wg_ref, wv_ref, bg_ref, bv_ref, a_ref):
        wg, wv, bg, bv = wg_ref[...], wv_ref[...], bg_ref[...], bv_ref[...]

        def chunk(win_g, win_v):
            gp = _conv_rows(win_g, wg, bg, RC)[3]
            vp = _conv_rows(win_v, wv, bv, RC)[3]
            return (gp * jax.nn.sigmoid(gp) * vp).astype(BF)

        zero = jnp.zeros((LEAD, FT), BF)
        a_ref[0:RC, :] = chunk(jnp.concatenate([zero, ug_ref[0:RC, :]], axis=0),
                               jnp.concatenate([zero, uv_ref[0:RC, :]], axis=0))

        def step(i, carry):
            r0 = pl.multiple_of(i * RC, RC)
            win = pl.ds(r0 - LEAD, RC + LEAD)
            a_ref[pl.ds(r0, RC), :] = chunk(ug_ref[win, :], uv_ref[win, :])
            return carry

        lax.fori_loop(1, NCH, step, 0)

    return _call(
        body, (up, up, fw8, fw8, fb, fb), name="ffn_act", grid=(NFT,), in_specs=_ffn_act_specs(),
        out_specs=[pl.BlockSpec((T, FT), lambda j: (0, j))], out_shape=[SDS((T, DFF), BF)],
        sem=("parallel",), vmem_mib=40, comm=comm)


def _ffn_act_bwd(up, da, fw8, fb, comm=()):
    ext = RC + LEAD

    def body(ug_ref, uv_ref, wg_ref, wv_ref, bg_ref, bv_ref, da_ref,
             dug_ref, duv_ref, dwg_ref, dwv_ref, dbg_ref, dbv_ref):
        wg, wv, bg, bv = wg_ref[...], wv_ref[...], bg_ref[...], bv_ref[...]

        def chunk(win_g, win_v, da_e):
            g2, g1, g0, gp = _conv_rows(win_g, wg, bg, ext)
            v2, v1, v0, vp = _conv_rows(win_v, wv, bv, ext)
            da_e = da_e.astype(F32)
            sig = jax.nn.sigmoid(gp)
            dvp = da_e * (gp * sig)
            dgp = da_e * vp * (sig * (1.0 + gp * (1.0 - sig)))

            def back(dp, w):
                return (w[2:3, :] * dp[0:RC] + w[1:2, :] * pltpu.roll(dp, ext - 1, 0)[0:RC]
                        + w[0:1, :] * pltpu.roll(dp, ext - 2, 0)[0:RC]).astype(BF)

            def sums(dp, u2, u1, u0):
                d = dp[0:RC]
                return [_rows8(d), _rows8(d * u2[0:RC]), _rows8(d * u1[0:RC]), _rows8(d * u0[0:RC])]

            return back(dgp, wg), back(dvp, wv), sums(dgp, g2, g1, g0) + sums(dvp, v2, v1, v0)

        zero = jnp.zeros((LEAD, FT), BF)
        dug, duv, acc = chunk(jnp.concatenate([zero, ug_ref[0:ext, :]], axis=0),
                              jnp.concatenate([zero, uv_ref[0:ext, :]], axis=0), da_ref[0:ext, :])
        dug_ref[0:RC, :] = dug
        duv_ref[0:RC, :] = duv

        def step(i, acc):
            r0 = pl.multiple_of(i * RC, RC)
            win = pl.ds(r0 - LEAD, ext + LEAD)
            dug, duv, part = chunk(ug_ref[win, :], uv_ref[win, :], da_ref[pl.ds(r0, ext), :])
            dug_ref[pl.ds(r0, RC), :] = dug
            duv_ref[pl.ds(r0, RC), :] = duv
            return [a + p for a, p in zip(acc, part)]

        acc = lax.fori_loop(1, NCH - 1, step, acc)
        r0 = T - RC
        tail = lambda ref, lo: jnp.concatenate([ref[lo:T, :], zero], axis=0)
        dug, duv, part = chunk(tail(ug_ref, r0 - LEAD), tail(uv_ref, r0 - LEAD), tail(da_ref, r0))
        dug_ref[r0:T, :] = dug
        duv_ref[r0:T, :] = duv
        tot = [jnp.sum(a + p, axis=0, keepdims=True) for a, p in zip(acc, part)]
        for k, (dw_ref, db_ref) in enumerate(((dwg_ref, dbg_ref), (dwv_ref, dbv_ref))):
            db_ref[...] = tot[4 * k]
            dw_ref[...] = jnp.zeros_like(dw_ref)
            for r in range(3):
                dw_ref[r:r + 1, :] = tot[4 * k + 1 + r]

    col = lambda r: pl.BlockSpec((r, FT), lambda j: (0, j))
    return _call(
        body, (up, up, fw8, fw8, fb, fb, da), name="ffn_act_bwd", grid=(NFT,),
        in_specs=_ffn_act_specs() + [pl.BlockSpec((T, FT), lambda j: (0, j))],
        out_specs=[col(T), col(T), col(8), col(8), col(1), col(1)],
        out_shape=[SDS((T, DFF), BF), SDS((T, DFF), BF), SDS((8, DFF), F32), SDS((8, DFF), F32),
                   SDS((1, DFF), F32), SDS((1, DFF), F32)],
        sem=("parallel",), vmem_mib=40, comm=comm)


def _ffn_down_bwd(dh2b, w_down, comm=()):
    tm = TM

    def body(d_ref, w_ref, o_ref):
        o_ref[...] = _dot(d_ref[...], w_ref[...], 1, 1).astype(BF)

    return _call(
        body, (dh2b, w_down), name="ffn_down_bwd", grid=(T // tm,),
        in_specs=[pl.BlockSpec((tm, D), lambda i: (i, 0)), _resident((DFF, D))],
        out_specs=[pl.BlockSpec((tm, DFF), lambda i: (i, 0))], out_shape=[SDS((T, DFF), BF)],
        sem=("parallel",), vmem_mib=40, comm=comm)


def _norm_matmul_bwd(name, a_list, w_t, k_offsets, xin, g, dres, want_bf16, comm=(), slot=None):
    tm = TM
    ks = [a.shape[1] for a in a_list]
    n_a = len(a_list)
    n_pre = 0 if slot is None else 1

    def body(*refs):
        refs = refs[n_pre:]
        a_refs = refs[:n_a]
        w_ref, x_ref, g_ref, r_ref = refs[n_a:n_a + 4]
        outs = refs[n_a + 4:]
        dx_ref, dg_ref = outs[0], (outs[-1] if slot is None else outs[-1].at[0])

        @pl.when(pl.program_id(0) == 0)
        def _():
            dg_ref[...] = jnp.zeros_like(dg_ref)

        du = _dot(a_refs[0][...], w_ref[k_offsets[0]:k_offsets[0] + ks[0], :], 1, 0)
        for k in range(1, n_a):
            du = du + _dot(a_refs[k][...], w_ref[k_offsets[k]:k_offsets[k] + ks[k], :], 1, 0)
        x = x_ref[...]
        r = lax.rsqrt(jnp.mean(x * x, axis=-1, keepdims=True) + EPS)
        dx, dg = _rms_bwd(du, x, r, g_ref[...])
        dx = r_ref[...] + dx
        dx_ref[...] = dx
        if want_bf16:
            outs[1][...] = dx.astype(BF)
        dg_ref[...] += dg

    tile = lambda c: pl.BlockSpec((tm, c), lambda i, *_: (i, 0))
    if slot is None:
        dg_spec, dg_shape = pl.BlockSpec((1, D), lambda i: (0, 0)), SDS((1, D), F32)
    else:
        dg_spec, dg_shape = pl.BlockSpec((1, 1, D), lambda i, slot_ref: (slot_ref[0], 0, 0)), SDS((N_DEV, 1, D), F32)
    out_specs = [tile(D)] + ([tile(D)] if want_bf16 else []) + [dg_spec]
    out_shape = [SDS((T, D), F32)] + ([SDS((T, D), BF)] if want_bf16 else []) + [dg_shape]
    return _call(
        body, (*a_list, w_t, xin, g, dres), name=name, grid=(T // tm,), prefetch=() if slot is None else (slot,),
        in_specs=[tile(k) for k in ks] + [_resident(w_t.shape), tile(D),
                                           pl.BlockSpec((1, D), lambda i, *_: (0, 0)), tile(D)],
        out_specs=out_specs, out_shape=out_shape, sem=("arbitrary",), vmem_mib=56, comm=comm)


def _out_bwd(dh1b, w_out, comm=()):
    tm = TM

    def body(d_ref, w_ref, o_ref):
        o_ref[...] = _dot(d_ref[...], w_ref[...], 1, 1)

    return _call(
        body, (dh1b, w_out), name="out_bwd", grid=(T // tm,),
        in_specs=[pl.BlockSpec((tm, D), lambda i: (i, 0)), _resident((D, D))],
        out_specs=[pl.BlockSpec((tm, D), lambda i: (i, 0))], out_shape=[SDS((T, D), F32)],
        sem=("parallel",), vmem_mib=32, comm=comm)


def _wgrad(name, a_list, b, comm=()):
    m_k = a_list[0].shape[1]
    tm = max(t for t in range(128, m_k // 2 + 1, 128) if m_k % t == 0)
    steps = [a.shape[1] // tm for a in a_list]
    starts = [sum(steps[:k]) for k in range(len(a_list))]
    n_a = len(a_list)

    def body(*refs):
        a_refs, b_ref, o_ref = refs[:n_a], refs[n_a], refs[n_a + 1]
        i = pl.program_id(0)
        for k in range(n_a):
            @pl.when((i >= starts[k]) & (i < starts[k] + steps[k]))
            def _(k=k):
                o_ref[...] = _dot(a_refs[k][...], b_ref[...], 0, 0).astype(BF)

    def a_spec(k):
        return pl.BlockSpec((T, tm), lambda i: (0, jnp.clip(i - starts[k], 0, steps[k] - 1)))

    m_total = tm * sum(steps)
    return _call(
        body, (*a_list, b), name=name, grid=(sum(steps),),
        in_specs=[a_spec(k) for k in range(n_a)] + [_resident((T, D))],
        out_specs=[pl.BlockSpec((tm, D), lambda i: (i, 0))], out_shape=[SDS((m_total, D), BF)],
        sem=("parallel",), vmem_mib=40, comm=comm)


def _chip_sum(name, gbf, from_sib, core, chip):
    h = gbf.shape[1]
    th = h // 2

    def body(core_ref, chip_ref, g_ref, s_ref, pbf_ref, own_ref):
        p = g_ref[0].astype(F32) + s_ref[0].astype(F32)
        pbf_ref[0] = p.astype(BF)

        @pl.when(pl.program_id(1) == chip_ref[0])
        def _():
            own_ref[...] = p

    grid_spec = pltpu.PrefetchScalarGridSpec(
        num_scalar_prefetch=2, grid=(h // th, N_CHIPS),
        in_specs=[pl.BlockSpec((1, th, D), lambda t, jj, core_ref, chip_ref: (2 * jj + core_ref[0], t, 0)),
                  pl.BlockSpec((1, th, D), lambda t, jj, core_ref, chip_ref: (jj, t, 0))],
        out_specs=[pl.BlockSpec((1, th, D), lambda t, jj, core_ref, chip_ref: (jj, t, 0)),
                   pl.BlockSpec((th, D), lambda t, jj, core_ref, chip_ref: (t, 0))],
    )
    return _pcall(
        body, name=name, grid_spec=grid_spec, out_shape=_in_hbm([SDS((N_CHIPS, h, D), BF), SDS((h, D), F32)]),
        compiler_params=_params(("arbitrary", "arbitrary"), 32),
    )(core, chip, *_from_hbm(gbf, from_sib))


def _final_sum(name, own, from_chips, core):
    h = own.shape[0]

    def body(core_ref, o_ref, r_ref, f_ref):
        f_ref[0] = ((o_ref[...] + r_ref[0].astype(F32)) + r_ref[1].astype(F32)) + r_ref[2].astype(F32)

    grid_spec = pltpu.PrefetchScalarGridSpec(
        num_scalar_prefetch=1, grid=(1,),
        in_specs=[pl.BlockSpec((h, D), lambda i, core_ref: (0, 0)), pl.BlockSpec((3, h, D), lambda i, core_ref: (0, 0, 0))],
        out_specs=pl.BlockSpec((1, h, D), lambda i, core_ref: (core_ref[0], 0, 0)),
    )
    return _pcall(body, name=name, grid_spec=grid_spec, out_shape=pltpu.HBM((2, h, D), F32),
                  compiler_params=_params(("arbitrary",), 40))(core, *_from_hbm(own, from_chips))


def _adam_math(w, g, m, v):
    nm = ADAM_B1 * m + (1.0 - ADAM_B1) * g
    nv = ADAM_B2 * v + (1.0 - ADAM_B2) * (g * g)
    m_hat = nm / (1.0 - ADAM_B1 ** ADAM_STEP)
    v_hat = nv / (1.0 - ADAM_B2 ** ADAM_STEP)
    return -ADAM_LR * (m_hat / (jnp.sqrt(v_hat) + ADAM_EPS) + ADAM_WD * w), nm, nv


def _adamw(name, w, g, m, v, tr, copy_g=False, stage=True):
    rows, cols = w.shape

    def body(w_ref, g_ref, m_ref, v_ref, *outs):
        g_val = g_ref[...]
        if copy_g:
            outs[0][...] = g_val
        d_ref, nm_ref, nv_ref = outs[-3:]
        d_ref[...], nm_ref[...], nv_ref[...] = _adam_math(w_ref[...], g_val, m_ref[...], v_ref[...])

    spec = pl.BlockSpec((tr, cols), lambda i: (i, 0))
    n_out = 4 if copy_g else 3
    return _call(body, (w, g, m, v), name=name, grid=(rows // tr,), in_specs=[spec] * 4, out_specs=[spec] * n_out,
                 out_shape=[SDS((rows, cols), F32)] * n_out, sem=("parallel",), vmem_mib=32,
                 free=(0, 2, 3) if stage else ())


C_G1, C_G2, C_GCO, C_GAO, C_DCW, C_DQG, C_DKG, C_SINK, C_SQ = 0, 1024, 2048, 2560, 3072, 4608, 4736, 4864, 5632
P_W = C_SQ + 128


def _pack_small(me, dfwg, dfwv, dfbg, dfbv, dg2, dgco, dgao, dcw8, dqg, dkg, dsink, sq):
    def body(me_ref, dfwg_r, dfwv_r, dfbg_r, dfbv_r, dg2_r, dgco_r, dgao_r, dcw_r, dqg_r, dkg_r, dsink_r, sq_r, o):
        o[...] = jnp.zeros_like(o)
        o[0, :, 0:DFF] = dfwg_r[...]
        o[0, :, DFF:2 * DFF] = dfwv_r[...]
        o[0, 3:4, 0:DFF] = dfbg_r[...]
        o[0, 3:4, DFF:2 * DFF] = dfbv_r[...]
        o[0, 4:5, C_G2:C_G2 + D] = dg2_r[...]
        o[0, 4:5, C_GCO:C_GCO + CW] = dgco_r[...]
        o[0, 4:5, C_GAO:C_GAO + AW] = dgao_r[...]
        for r in range(3):
            o[0, 4:5, C_DCW + r * CW:C_DCW + (r + 1) * CW] = dcw_r[r:r + 1, :]
        o[0, 4:5, C_DQG:C_DQG + HD] = dqg_r[...]
        o[0, 4:5, C_DKG:C_DKG + HD] = dkg_r[...]
        o[0, 4:5, C_SINK:C_SINK + 128] = dsink_r[...]
        o[0, :, C_SQ:C_SQ + 128] = sq_r[...]

    ins = (dfwg, dfwv, dfbg, dfbv, dg2, dgco, dgao, dcw8, dqg, dkg, dsink, sq)
    return _call(body, ins, name="pack_small", grid=(1,), prefetch=(me,),
                 in_specs=[pl.BlockSpec(a.shape, lambda i, me_ref: (0, 0)) for a in ins],
                 out_specs=[pl.BlockSpec((1, 8, P_W), lambda i, me_ref: (me_ref[0], 0, 0))],
                 out_shape=[SDS((N_DEV, 8, P_W), F32)], sem=("arbitrary",))[0]


N_SMALL = 11


def _small_adam(chip, p_all, g1_all, tbl_all, ws, ms, vs):
    fw_cols = 2 * DFF // N_CHIPS
    cw_cols = CW // N_CHIPS

    def body(chip_ref, p_ref, fw_ref, cw0_ref, cw1_ref, cw2_ref, g1_ref, tbl_ref, *refs):
        w_r, m_r, v_r = refs[0:N_SMALL], refs[N_SMALL:2 * N_SMALL], refs[2 * N_SMALL:3 * N_SMALL]
        outs = refs[3 * N_SMALL:]
        g_o, d_o, nm_o, nv_o = (outs[k * N_SMALL:(k + 1) * N_SMALL] for k in range(4))
        loss_o = outs[4 * N_SMALL]

        def total(ref):
            s = ref[0]
            for k in range(1, N_DEV):
                s = s + ref[k]
            return s

        S = total(p_ref)
        fw = total(fw_ref)
        cws = [total(r) for r in (cw0_ref, cw1_ref, cw2_ref)]

        def step(i, g, at):
            d, nm, nv = _adam_math(w_r[i][at], g, m_r[i][at], v_r[i][at])
            g_o[i][at], d_o[i][at], nm_o[i][at], nv_o[i][at] = g, d, nm, nv

        everything = (slice(None), slice(None))
        step(0, total(g1_ref), everything)
        for r in range(3):
            step(1, cws[r][4:5, :], (r, slice(None), slice(None)))
        step(2, S[4:5, C_DQG:C_DQG + HD], everything)
        step(3, S[4:5, C_DKG:C_DKG + HD], everything)
        step(4, total(tbl_ref), everything)
        step(5, S[4:5, C_SINK:C_SINK + NH], everything)
        step(6, S[4:5, C_GCO:C_GCO + CW], everything)
        step(7, S[4:5, C_GAO:C_GAO + AW], everything)
        step(8, S[4:5, C_G2:C_G2 + D], everything)
        for r in range(3):
            step(9, fw[r:r + 1, :], (r, slice(None), slice(None)))
        step(10, S[3:4, 0:2 * DFF], everything)
        sq = S[:, C_SQ:C_SQ + 128]
        loss_o[...] = jnp.sum(jnp.sum(sq, axis=1, keepdims=True), axis=0, keepdims=True) * (0.5 / D)

    def full(a):
        n = len(a.shape)
        return pl.BlockSpec(a.shape, lambda i, chip_ref: (0,) * n)

    params = [*ws, *ms, *vs]
    out = _call(
        body, (p_all, p_all, p_all, p_all, p_all, g1_all, tbl_all, *params), name="small_adam", grid=(1,), prefetch=(chip,),
        in_specs=[full(p_all),
                  pl.BlockSpec((N_DEV, 8, fw_cols), lambda i, chip_ref: (0, 0, chip_ref[0])),
                  *[pl.BlockSpec((N_DEV, 8, cw_cols), lambda i, chip_ref, r=r: (0, 0, (C_DCW + r * CW) // cw_cols + chip_ref[0]))
                    for r in range(3)],
                  full(g1_all), full(tbl_all), *[full(a) for a in params]],
        out_specs=[full(a) for a in ws] * 4 + [pl.BlockSpec((1, 1), lambda i, chip_ref: (0, 0))],
        out_shape=[SDS(a.shape, F32) for a in ws] * 4 + [SDS((1, 1), F32)], sem=("arbitrary",), vmem_mib=32)
    return out[0:N_SMALL], out[N_SMALL:2 * N_SMALL], out[2 * N_SMALL:3 * N_SMALL], out[3 * N_SMALL:4 * N_SMALL], out[4 * N_SMALL]


PLACE_STEPS = 4


def _place_specs(shards):
    rows = [s.shape[0] // PLACE_STEPS for s in shards]
    return ([pl.BlockSpec((r, D), lambda i, chip_ref: (i, 0)) for r in rows],
            [pl.BlockSpec((r, D), lambda i, chip_ref: (chip_ref[0] * PLACE_STEPS + i, 0)) for r in rows],
            [SDS((N_CHIPS * s.shape[0], D), BF) for s in shards])


def _place_first(chip, shard, conv_w, ffn_conv_w):
    def body(chip_ref, a, s0, s1, o, t0, t1):
        o[...] = a[...].astype(BF)

        @pl.when(pl.program_id(0) == 0)
        def _():
            for s, t in ((s0, t0), (s1, t1)):
                t[...] = jnp.zeros_like(t)
                t[0, 0:3, :] = s[...]

    ins, outs, shapes = _place_specs([shard])
    taps = (conv_w, ffn_conv_w)
    return _call(
        body, (shard, conv_w, ffn_conv_w), name="place_first", grid=(PLACE_STEPS,), prefetch=(chip,),
        in_specs=ins + [pl.BlockSpec(s.shape, lambda i, chip_ref: (0, 0)) for s in taps],
        out_specs=outs + [pl.BlockSpec((1, 8, s.shape[1]), lambda i, chip_ref: (chip_ref[0], 0, 0)) for s in taps],
        out_shape=shapes + [SDS((N_CHIPS, 8, s.shape[1]), F32) for s in taps],
        sem=("arbitrary",), vmem_mib=32, free=(0, 1, 2))


def _place_rest(chip, shards, table, bucket, comm):
    n = len(shards)

    def body(chip_ref, *refs):
        a, (tab_ref, bk_ref), o, bias_ref = refs[:n], refs[n:n + 2], refs[n + 2:2 * n + 2], refs[2 * n + 2]
        for src, dst in zip(a, o):
            dst[...] = src[...].astype(BF)

        @pl.when(pl.program_id(0) == 0)
        def _():
            bk = bk_ref[...]
            eq = [bk == b for b in range(NBUCKET)]
            for h in range(NH):
                acc = jnp.zeros((BLK, 2 * BLK), F32)
                for b in range(NBUCKET):
                    acc = jnp.where(eq[b], tab_ref[h, b], acc)
                bias_ref[h * BLK:(h + 1) * BLK, :] = acc

    ins, outs, shapes = _place_specs(shards)
    return _call(
        body, (*shards, table, bucket), name="place_rest", grid=(PLACE_STEPS,), prefetch=(chip,),
        in_specs=ins + [pl.BlockSpec(memory_space=pltpu.SMEM), pl.BlockSpec(bucket.shape, lambda i, chip_ref: (0, 0))],
        out_specs=outs + [pl.BlockSpec((NH * BLK, 2 * BLK), lambda i, chip_ref: (0, 0))],
        out_shape=shapes + [SDS((NH * BLK, 2 * BLK), F32)],
        sem=("arbitrary",), vmem_mib=32, comm=comm, free=tuple(range(n + 2)))


def kernel(x, norm_mix_g, w_in, conv_w, q_norm_g, k_norm_g, rel_bias_table, sinks, out_norm_conv_g, out_norm_attn_g, w_out, norm_ffn_g, w_up, ffn_conv_w, ffn_conv_b, w_down, loss_target, m_norm_mix_g, m_w_in, m_conv_w, m_q_norm_g, m_k_norm_g, m_rel_bias_table, m_sinks, m_out_norm_conv_g, m_out_norm_attn_g, m_w_out, m_norm_ffn_g, m_w_up, m_ffn_conv_w, m_ffn_conv_b, m_w_down, v_norm_mix_g, v_w_in, v_conv_w, v_q_norm_g, v_k_norm_g, v_rel_bias_table, v_sinks, v_out_norm_conv_g, v_out_norm_attn_g, v_w_out, v_norm_ffn_g, v_w_up, v_ffn_conv_w, v_ffn_conv_b, v_w_down):
    as_arg = lambda i: jnp.reshape(i, (1,)).astype(jnp.int32)
    chip = as_arg(2 * lax.axis_index("x") + lax.axis_index("y"))
    core = as_arg(lax.axis_index("c"))
    me = 2 * chip + core
    xs, tgt = x[0], loss_target[0]
    qg, kg, gco, gao, g1, g2, fb = q_norm_g, k_norm_g, out_norm_conv_g, out_norm_attn_g, norm_mix_g, norm_ffn_g, ffn_conv_b
    pieces = lambda g: g.reshape(N_DEV, g.shape[0] // N_DEV, D)
    whole = lambda f: f.reshape(2 * f.shape[1], D)

    bucket = jnp.asarray(_bucket_table())
    p_in, p_cw, p_fw = _place_first(chip, w_in[0].T, conv_w[0], ffn_conv_w[0])
    p_out, p_up, p_down, bias, w_int, cw_all, fw_all = _place_rest(
        chip, [w_out[0], w_up[0].T, w_down[0]], rel_bias_table.T, bucket,
        comm=[_t_gather(p_in), _t_small_weights(p_cw), _t_small_weights(p_fw)])
    cw8 = jnp.transpose(cw_all, (1, 0, 2)).reshape(8, CW)
    fw8 = jnp.transpose(fw_all, (1, 0, 2)).reshape(8, 2 * DFF)

    early = 3 / 11
    proj, u1, w_out_f, p_up = _inproj(xs, g1, w_int, comm=[_t_gather(p_out), _t_gather(p_up, (0, early))])
    y, w_upt = _mix_fwd(proj, sinks, cw8, qg, kg, gco, gao, bias, comm=[_t_gather(p_up, (early, 1))])
    h1, u2 = _outproj(y, w_out_f, xs, g2)
    up, = _ffn_up(u2, w_upt)
    a, w_down_f = _ffn_act(up, fw8, fb, comm=[_t_gather(p_down)])
    dh2, dh2b, sq = _ffn_down(a, w_down_f, h1, tgt)

    gdbf, = _wgrad("wgrad_down", [a], dh2b)
    da, sib_down = _ffn_down_bwd(dh2b, w_down_f, comm=[_t_sibling(pieces(gdbf))])
    pbf_down, own_down = _chip_sum("chip_sum_w_down", pieces(gdbf), sib_down, core, chip)
    dug, duv, dfwg, dfwv, dfbg, dfbv, chips_down = _ffn_act_bwd(up, da, fw8, fb, comm=[_t_chips(pbf_down)])
    fin_down = _final_sum("final_sum_w_down", own_down, chips_down, core)
    gubf, = _wgrad("wgrad_up", [dug, duv], u2)
    dh1, dh1b, dg2, sib_up, fin_down = _norm_matmul_bwd(
        "ffn_up_bwd", [dug, duv], w_upt, [0, DFF], h1, g2, dh2, True, comm=[_t_sibling(pieces(gubf)), _t_swap(fin_down)])
    pbf_up, own_up = _chip_sum("chip_sum_w_up", pieces(gubf), sib_up, core, chip)
    gobf, = _wgrad("wgrad_out", [y], dh1b)
    dy, sib_out = _out_bwd(dh1b, w_out_f, comm=[_t_sibling(pieces(gobf))])
    pbf_out, own_out = _chip_sum("chip_sum_w_out", pieces(gobf), sib_out, core, chip)
    dproj, dcw8, dqg, dkg, dgco, dgao, dsink, dbias, chips_up, chips_out = _mix_bwd(
        proj, dy, sinks, cw8, qg, kg, gco, gao, bias, comm=[_t_chips(pbf_up), _t_chips(pbf_out)])
    fin_up = _final_sum("final_sum_w_up", own_up, chips_up, core)
    fin_out = _final_sum("final_sum_w_out", own_out, chips_out, core)
    tbl_all = _band_bias_bwd(dbias, bucket, me)
    p_all = _pack_small(me, dfwg, dfwv, dfbg, dfbv, dg2, dgco, dgao, dcw8, dqg, dkg, dsink, sq)
    gibf, fin_up, p_all, tbl_all = _wgrad(
        "wgrad_in", [dproj], u1, comm=[_t_swap(fin_up), _t_allgather(p_all), _t_allgather(tbl_all)])
    sib_in, fin_out = _comm_call("to_sibling_last", [_t_sibling(pieces(gibf)), _t_swap(fin_out)])
    pbf_in, own_in = _chip_sum("chip_sum_w_in", pieces(gibf), sib_in, core, chip)
    dx, g1_all, chips_in = _norm_matmul_bwd(
        "in_bwd", [dproj], w_int, [0], xs, g1, dh1, False, comm=[_t_chips(pbf_in)], slot=me)
    fin_in = _final_sum("final_sum_w_in", own_in, chips_in, core)
    g1_all, fin_in = _comm_call("gather_last", [_t_allgather(g1_all), _t_swap(fin_in)])

    g_w_out, g_w_up, g_w_down = whole(fin_out), whole(fin_up).T, whole(fin_down)
    g_w_down, d_down, nm_down, nv_down = _adamw("adamw_w_down", w_down[0], g_w_down, m_w_down[0], v_w_down[0], 352, True)
    d_up, nm_up, nv_up = _adamw("adamw_w_up", w_up[0], g_w_up, m_w_up[0], v_w_up[0], 256, stage=False)
    g_w_out, d_out, nm_out, nv_out = _adamw("adamw_w_out", w_out[0], g_w_out, m_w_out[0], v_w_out[0], 256, True)
    g_w_in, d_in, nm_in, nv_in = [a.T for a in _adamw(
        "adamw_w_in", w_in[0].T, whole(fin_in), m_w_in[0].T, v_w_in[0].T, INW // N_CHIPS // 3, True)]
    taps = lambda a: jnp.transpose(a, (1, 0, 2))
    sw = [norm_mix_g, taps(conv_w), q_norm_g, k_norm_g, rel_bias_table.T, sinks, out_norm_conv_g, out_norm_attn_g,
          norm_ffn_g, taps(ffn_conv_w), ffn_conv_b]
    smm = [m_norm_mix_g, taps(m_conv_w), m_q_norm_g, m_k_norm_g, m_rel_bias_table.T, m_sinks, m_out_norm_conv_g,
           m_out_norm_attn_g, m_norm_ffn_g, taps(m_ffn_conv_w), m_ffn_conv_b]
    smv = [v_norm_mix_g, taps(v_conv_w), v_q_norm_g, v_k_norm_g, v_rel_bias_table.T, v_sinks, v_out_norm_conv_g,
           v_out_norm_attn_g, v_norm_ffn_g, taps(v_ffn_conv_w), v_ffn_conv_b]
    *small_out, loss = _small_adam(chip, p_all, g1_all, tbl_all, sw, smm, smv)
    sg, sd, snm, snv = [list(r) for r in small_out]
    for r in (sg, sd, snm, snv):
        r[1], r[4], r[9] = taps(r[1]), r[4].T, taps(r[9])

    def order(s, b_in, b_out, b_up, b_down):
        return (s[0], b_in[None], s[1], s[2], s[3], s[4], s[5], s[6], s[7], b_out[None], s[8], b_up[None],
                s[9], s[10], b_down[None])

    return (loss.reshape(()), dx[None],
            *order(sg, g_w_in, g_w_out, g_w_up, g_w_down),
            *order(sd, d_in, d_out, d_up, d_down),
            *order(snm, nm_in, nm_out, nm_up, nm_down),
            *order(snv, nv_in, nv_out, nv_up, nv_down))
```

```python
import functools
import math

import numpy as np

import jax
import jax.numpy as jnp
from jax import lax
from jax.experimental import pallas as pl
from jax.experimental.pallas import tpu as pltpu

F32 = jnp.float32
BF = jnp.bfloat16
SDS = jax.ShapeDtypeStruct

T = 2048
D = 1024
CW = 512
AW = 512
HD = 64
NH = 8
NKV = 2
GQ = 4
INW = 2304
DFF = 2816
BLK = 128
NB = T // BLK
NBUCKET = 32
EPS = 1e-6
NEG_INF = -1e30
N_CHIPS = 4
N_DEV = 8

ADAM_LR = 0.001
ADAM_B1 = 0.9
ADAM_B2 = 0.999
ADAM_EPS = 1e-08
ADAM_WD = 0.01
ADAM_STEP = 10

TM = 512
MIB = 1024 * 1024
MESH = pl.DeviceIdType.MESH
ANY = pl.BlockSpec(memory_space=pl.ANY)

_pcall = pl.pallas_call


def _params(sem=None, vmem_mib=None):
    kw = {}
    if sem is not None:
        kw["dimension_semantics"] = sem
    if vmem_mib is not None:
        kw["vmem_limit_bytes"] = vmem_mib * MIB
    return pltpu.CompilerParams(**kw)


def _resident(shape):
    return pl.BlockSpec(shape, lambda *_: (0,) * len(shape), pipeline_mode=pl.Buffered(1))


def _dot(a, b, ca, cb):
    return lax.dot_general(a, b, (((ca,), (cb,)), ((), ())), preferred_element_type=F32)


def _rms_bwd(dy, x, r, g):
    dg = jnp.sum(dy * (x * r), axis=0, keepdims=True)
    dgx = dy * g
    dx = r * dgx - x * (r * r * r) * jnp.mean(x * dgx, axis=-1, keepdims=True)
    return dx, dg


def _where():
    x, y, c = lax.axis_index("x"), lax.axis_index("y"), lax.axis_index("c")
    return x, y, c, [(1 - x, y), (x, 1 - y), (1 - x, 1 - y)]


def _rcopy(src, dst, ssem, rsem, dev):
    return pltpu.make_async_remote_copy(src_ref=src, dst_ref=dst, send_sem=ssem, recv_sem=rsem, device_id=dev,
                                        device_id_type=MESH)


class _Task:
    def __init__(self, ins, outs, alias, n_sem, start, finish, middle=None):
        self.ins, self.outs, self.alias, self.n_sem, self.start, self.finish = ins, outs, alias, n_sem, start, finish
        self.middle = middle if middle is not None else (lambda *args: None)


ROWS16 = 16


def _t_gather(placed, part=(0, 1)):
    R = placed.shape[0] // N_CHIPS
    q = R // 4
    lo, hi = (round(f * (q // ROWS16)) * ROWS16 for f in part)

    def quarter(chip_index, core, k):
        return pl.ds(pl.multiple_of(chip_index * R + core * 2 * q + k * q + lo, ROWS16), hi - lo)

    def places():
        x, y, c, _ = _where()
        return c, 2 * x + y, 2 * (1 - x) + y, 2 * x + (1 - y), 2 * (1 - x) + (1 - y), (1 - x, y, c), (x, 1 - y, c), (x, y, 1 - c)

    def copy(buf, k, chip_index, core, quart, ss, rs, b, dev):
        window = buf.at[quarter(chip_index, core, quart)]
        return _rcopy(window, window, ss.at[b + k], rs.at[b + k], dev)

    def start(cin, cout, ss, rs, b):
        c, me, _, _, _, x_nbr, y_nbr, _ = places()
        for k, (quart, dev) in enumerate(((0, x_nbr), (1, y_nbr), (1, x_nbr), (0, y_nbr))):
            copy(cout[0], k, me, c, quart, ss, rs, b, dev).start()

    def middle(cin, cout, ss, rs, b):
        c, _, xc, yc, _, x_nbr, y_nbr, sib = places()
        for k, chip_index, quart, dev in ((0, xc, 0, y_nbr), (1, yc, 1, x_nbr)):
            copy(cout[0], k, chip_index, c, quart, ss, rs, b, dev).wait_recv()
            copy(cout[0], 4 + k, chip_index, c, quart, ss, rs, b, dev).start()
            copy(cout[0], 6 + k, chip_index, c, quart, ss, rs, b, sib).start()

    later = ((2, 1, 1), (3, 2, 0), (4, 3, 0), (5, 3, 1))

    def finish(cin, cout, ss, rs, b):
        c, me, xc, yc, dc, _, _, sib = places()
        chip_of = {1: xc, 2: yc, 3: dc}
        for k, whose, quart in later:
            copy(cout[0], k, chip_of[whose], c, quart, ss, rs, b, sib).wait_recv()
            copy(cout[0], 6 + k, chip_of[whose], c, quart, ss, rs, b, sib).start()
        for k, whose, quart in ((0, 1, 0), (1, 2, 1)) + later:
            copy(cout[0], 6 + k, chip_of[whose], 1 - c, quart, ss, rs, b, sib).wait_recv()
        for k in range(12):
            copy(cout[0], k, me, c, 0, ss, rs, b, sib).wait_send()

    return _Task([placed], [SDS(placed.shape, placed.dtype)], [(0, 0)], 12, start, finish, middle)


def _t_small_weights(buf):
    def start(cin, cout, ss, rs, b):
        x, y, c, chips = _where()
        mine = cout[0].at[2 * x + y]
        for r, (px, py) in enumerate(chips):
            _rcopy(mine, mine, ss.at[b + r], rs.at[b + r], (px, py, c)).start()

    def finish(cin, cout, ss, rs, b):
        x, y, c, chips = _where()
        for r, (px, py) in enumerate(chips):
            got = cout[0].at[2 * px + py]
            _rcopy(got, got, ss.at[b + r], rs.at[b + r], (px, py, c)).wait_recv()
        for r, (px, py) in enumerate(chips):
            mine = cout[0].at[2 * x + y]
            _rcopy(mine, mine, ss.at[b + r], rs.at[b + r], (px, py, c)).wait_send()

    return _Task([buf], [SDS(buf.shape, buf.dtype)], [(0, 0)], 3, start, finish)


def _t_sibling(gbf):
    def start(cin, cout, ss, rs, b):
        x, y, c, _ = _where()
        for jj in range(N_CHIPS):
            _rcopy(cin[0].at[2 * jj + (1 - c)], cout[0].at[jj], ss.at[b + jj], rs.at[b + jj], (x, y, 1 - c)).start()

    def finish(cin, cout, ss, rs, b):
        x, y, c, _ = _where()
        for jj in range(N_CHIPS):
            got = cout[0].at[jj]
            _rcopy(got, got, ss.at[b + jj], rs.at[b + jj], (x, y, 1 - c)).wait_recv()
        for jj in range(N_CHIPS):
            got = cout[0].at[jj]
            _rcopy(got, got, ss.at[b + jj], rs.at[b + jj], (x, y, 1 - c)).wait_send()

    return _Task([gbf], [SDS((N_CHIPS,) + gbf.shape[1:], BF)], [], N_CHIPS, start, finish)


def _t_chips(pbf):
    def start(cin, cout, ss, rs, b):
        x, y, c, chips = _where()
        for r, (px, py) in enumerate(chips):
            _rcopy(cin[0].at[2 * px + py], cout[0].at[r], ss.at[b + r], rs.at[b + r], (px, py, c)).start()

    def finish(cin, cout, ss, rs, b):
        x, y, c, chips = _where()
        for r, (px, py) in enumerate(chips):
            got = cout[0].at[r]
            _rcopy(got, got, ss.at[b + r], rs.at[b + r], (px, py, c)).wait_recv()
        for r, (px, py) in enumerate(chips):
            got = cout[0].at[r]
            _rcopy(got, got, ss.at[b + r], rs.at[b + r], (px, py, c)).wait_send()

    return _Task([pbf], [SDS((3,) + pbf.shape[1:], BF)], [], 3, start, finish)


def _t_swap(fin):
    def start(cin, cout, ss, rs, b):
        x, y, c, _ = _where()
        mine = cout[0].at[c]
        _rcopy(mine, mine, ss.at[b], rs.at[b], (x, y, 1 - c)).start()

    def finish(cin, cout, ss, rs, b):
        x, y, c, _ = _where()
        got = cout[0].at[1 - c]
        _rcopy(got, got, ss.at[b], rs.at[b], (x, y, 1 - c)).wait_recv()
        _rcopy(got, got, ss.at[b], rs.at[b], (x, y, 1 - c)).wait_send()

    return _Task([fin], [SDS(fin.shape, fin.dtype)], [(0, 0)], 1, start, finish)


def _t_allgather(buf):
    def peers():
        x, y, c, _ = _where()
        out = []
        for rel in range(1, N_DEV):
            px, py, pc = x ^ ((rel >> 2) & 1), y ^ ((rel >> 1) & 1), c ^ (rel & 1)
            out.append((rel - 1, 4 * px + 2 * py + pc, (px, py, pc)))
        return 4 * x + 2 * y + c, out

    def start(cin, cout, ss, rs, b):
        me, ps = peers()
        mine = cout[0].at[me]
        for k, _, dev in ps:
            _rcopy(mine, mine, ss.at[b + k], rs.at[b + k], dev).start()

    def finish(cin, cout, ss, rs, b):
        me, ps = peers()
        for k, pidx, dev in ps:
            got = cout[0].at[pidx]
            _rcopy(got, got, ss.at[b + k], rs.at[b + k], dev).wait_recv()
        for k, _, dev in ps:
            mine = cout[0].at[me]
            _rcopy(mine, mine, ss.at[b + k], rs.at[b + k], dev).wait_send()

    return _Task([buf], [SDS(buf.shape, buf.dtype)], [(0, 0)], N_DEV - 1, start, finish)


def _run_tasks(comm, which, cin, cout, ss, rs):
    i0 = o0 = s0 = 0
    for t in comm:
        getattr(t, which)(cin[i0:i0 + len(t.ins)], cout[o0:o0 + len(t.outs)], ss, rs, s0)
        i0, o0, s0 = i0 + len(t.ins), o0 + len(t.outs), s0 + t.n_sem


def _from_hbm(*arrays):
    return [pltpu.with_memory_space_constraint(a, pltpu.HBM) for a in arrays]


def _in_hbm(shapes):
    return [pltpu.HBM(s.shape, s.dtype) for s in shapes]


def _comm_layout(comm, n_in, n_out):
    c_in = [a for t in comm for a in t.ins]
    c_out = [s for t in comm for s in t.outs]
    aliases, i0, o0 = {}, 0, 0
    for t in comm:
        for i, o in t.alias:
            aliases[n_in + i0 + i] = n_out + o0 + o
        i0, o0 = i0 + len(t.ins), o0 + len(t.outs)
    return c_in, c_out, aliases, sum(t.n_sem for t in comm)


def _call(body, operands, *, name, grid, in_specs, out_specs, out_shape, scratch_shapes=(), sem=None, vmem_mib=None, comm=(),
          free=(), prefetch=()):
    operands = [o if s.memory_space == pltpu.SMEM or k in free else pltpu.with_memory_space_constraint(o, pltpu.HBM)
                for k, (o, s) in enumerate(zip(operands, in_specs))]
    n_pre, n_in, n_out, n_scr = len(prefetch), len(in_specs), len(out_specs), len(scratch_shapes)
    c_in, c_out, aliases, n_sem = _comm_layout(comm, n_pre + n_in, n_out)
    sems = [pltpu.SemaphoreType.DMA((n_sem,)), pltpu.SemaphoreType.DMA((n_sem,))] if comm else []

    def wrapped(*refs):
        pre, refs = refs[:n_pre], refs[n_pre:]
        ins, cin = refs[:n_in], refs[n_in:n_in + len(c_in)]
        rest = refs[n_in + len(c_in):]
        outs, cout = rest[:n_out], rest[n_out:n_out + len(c_out)]
        rest = rest[n_out + len(c_out):]
        scr, csem = rest[:n_scr], rest[n_scr:]
        if not comm:
            return body(*pre, *ins, *outs, *scr)
        step = functools.reduce(lambda acc, k: acc * grid[k] + pl.program_id(k), range(len(grid)), 0)
        n_steps = math.prod(grid)
        pl.when(step == 0)(lambda: _run_tasks(comm, "start", cin, cout, *csem))
        pl.when(step == n_steps // 2)(lambda: _run_tasks(comm, "middle", cin, cout, *csem))
        body(*pre, *ins, *outs, *scr)
        pl.when(step == n_steps - 1)(lambda: _run_tasks(comm, "finish", cin, cout, *csem))

    grid_spec = pltpu.PrefetchScalarGridSpec(
        num_scalar_prefetch=n_pre, grid=grid, in_specs=list(in_specs) + [ANY] * len(c_in),
        out_specs=list(out_specs) + [ANY] * len(c_out), scratch_shapes=list(scratch_shapes) + sems)
    return _pcall(
        wrapped, name=name, grid_spec=grid_spec, out_shape=_in_hbm(list(out_shape) + c_out), input_output_aliases=aliases,
        compiler_params=_params(("arbitrary",) * len(grid) if comm else sem, vmem_mib),
    )(*prefetch, *operands, *_from_hbm(*c_in))


def _comm_call(name, comm):
    c_in, c_out, aliases, n_sem = _comm_layout(comm, 0, 0)

    def body(*refs):
        cin, cout, (ss, rs) = refs[:len(c_in)], refs[len(c_in):len(c_in) + len(c_out)], refs[len(c_in) + len(c_out):]
        for phase in ("start", "middle", "finish"):
            _run_tasks(comm, phase, cin, cout, ss, rs)

    return _pcall(
        body, name=name, in_specs=[ANY] * len(c_in), out_specs=[ANY] * len(c_out), out_shape=_in_hbm(c_out),
        scratch_shapes=[pltpu.SemaphoreType.DMA((n_sem,)), pltpu.SemaphoreType.DMA((n_sem,))],
        input_output_aliases=aliases,
    )(*_from_hbm(*c_in))


def _inproj(x, g1, w_int, comm=()):
    tm = TM

    def body(x_ref, g_ref, w_ref, proj_ref, u_ref):
        xf = x_ref[...]
        r = lax.rsqrt(jnp.mean(xf * xf, axis=-1, keepdims=True) + EPS)
        u = (xf * r * g_ref[...]).astype(BF)
        u_ref[...] = u
        proj_ref[...] = _dot(u, w_ref[...], 1, 1)

    return _call(
        body, (x, g1, w_int), name="inproj", grid=(T // tm,),
        in_specs=[pl.BlockSpec((tm, D), lambda i: (i, 0)), pl.BlockSpec((1, D), lambda i: (0, 0)),
                  _resident((INW, D))],
        out_specs=[pl.BlockSpec((tm, INW), lambda i: (i, 0)), pl.BlockSpec((tm, D), lambda i: (i, 0))],
        out_shape=[SDS((T, INW), F32), SDS((T, D), BF)], sem=("parallel",), vmem_mib=40, comm=comm, free=(0, 1))


def _outproj(y, w_out, x, g2):
    tm = TM

    def body(y_ref, w_ref, x_ref, g_ref, h1_ref, u2_ref):
        h1 = x_ref[...] + _dot(y_ref[...], w_ref[...], 1, 0)
        h1_ref[...] = h1
        r = lax.rsqrt(jnp.mean(h1 * h1, axis=-1, keepdims=True) + EPS)
        u2_ref[...] = (h1 * r * g_ref[...]).astype(BF)

    return _call(
        body, (y, w_out, x, g2), name="outproj", grid=(T // tm,),
        in_specs=[pl.BlockSpec((tm, D), lambda i: (i, 0)), _resident((D, D)),
                  pl.BlockSpec((tm, D), lambda i: (i, 0)), pl.BlockSpec((1, D), lambda i: (0, 0))],
        out_specs=[pl.BlockSpec((tm, D), lambda i: (i, 0)), pl.BlockSpec((tm, D), lambda i: (i, 0))],
        out_shape=[SDS((T, D), F32), SDS((T, D), BF)], sem=("parallel",), vmem_mib=32, free=(1, 2, 3))


def _ffn_up(u2, w_upt, comm=()):
    tm, tn = 1024, 512

    def body(u_ref, w_ref, o_ref):
        o_ref[...] = _dot(u_ref[...], w_ref[...], 1, 1).astype(BF)

    return _call(
        body, (u2, w_upt), name="ffn_up", grid=(T // tm, 2 * DFF // tn),
        in_specs=[pl.BlockSpec((tm, D), lambda i, j: (i, 0)), pl.BlockSpec((tn, D), lambda i, j: (j, 0))],
        out_specs=[pl.BlockSpec((tm, tn), lambda i, j: (i, j))], out_shape=[SDS((T, 2 * DFF), BF)],
        sem=("parallel", "parallel"), vmem_mib=32, comm=comm, free=(1,))


def _ffn_down(a, w_down, h1, tgt):
    tm = TM

    def body(a_ref, w_ref, h1_ref, t_ref, dh_ref, dhb_ref, l_ref):
        @pl.when(pl.program_id(0) == 0)
        def _():
            l_ref[...] = jnp.zeros_like(l_ref)

        h2 = h1_ref[...] + _dot(a_ref[...], w_ref[...], 1, 0)
        e = h2 - t_ref[...]
        dh = e * (1.0 / D)
        dh_ref[...] = dh
        dhb_ref[...] = dh.astype(BF)
        e2 = jnp.sum((e * e).reshape(tm // 8, 8, D), axis=0)
        acc = e2[:, 0:128]
        for k in range(1, D // 128):
            acc = acc + e2[:, k * 128:(k + 1) * 128]
        l_ref[...] += acc

    return _call(
        body, (a, w_down, h1, tgt), name="ffn_down", grid=(T // tm,),
        in_specs=[pl.BlockSpec((tm, DFF), lambda i: (i, 0)), _resident((DFF, D)),
                  pl.BlockSpec((tm, D), lambda i: (i, 0)), pl.BlockSpec((tm, D), lambda i: (i, 0))],
        out_specs=[pl.BlockSpec((tm, D), lambda i: (i, 0)), pl.BlockSpec((tm, D), lambda i: (i, 0)),
                   pl.BlockSpec((8, 128), lambda i: (0, 0))],
        out_shape=[SDS((T, D), F32), SDS((T, D), BF), SDS((8, 128), F32)], sem=("arbitrary",), vmem_mib=40, free=(2, 3))


def _bucket_table():
    q = np.arange(BLK, dtype=np.int32)[:, None]
    j = np.arange(2 * BLK, dtype=np.int32)[None, :]
    n = np.maximum(q + BLK - j, 0)
    nf = np.maximum(n, 1).astype(np.float32)
    max_exact = NBUCKET // 2
    large = max_exact + (np.log(nf / np.float32(max_exact)) / np.float32(math.log(BLK / max_exact))
                         * np.float32(NBUCKET - max_exact)).astype(np.int32)
    large = np.minimum(large, NBUCKET - 1)
    return np.where(n < max_exact, n, large).astype(np.int32)


def _band_bias_bwd(dbias, bucket, me):
    def body(me_ref, db_ref, bk_ref, o_ref):
        bk = bk_ref[...]
        for b in range(NBUCKET):
            m = bk == b
            for h in range(NH):
                v = jnp.where(m, db_ref[h * BLK:(h + 1) * BLK, :], 0.0)
                s = jnp.sum(jnp.sum(v, axis=1, keepdims=True), axis=0, keepdims=True)
                o_ref[0, h:h + 1, b:b + 1] = s

    grid_spec = pltpu.PrefetchScalarGridSpec(
        num_scalar_prefetch=1, grid=(1,),
        in_specs=[pl.BlockSpec((NH * BLK, 2 * BLK), lambda i, me_ref: (0, 0)),
                  pl.BlockSpec((BLK, 2 * BLK), lambda i, me_ref: (0, 0))],
        out_specs=pl.BlockSpec((1, NH, NBUCKET), lambda i, me_ref: (me_ref[0], 0, 0)),
    )
    return _pcall(body, name="band_bias_bwd", grid_spec=grid_spec, out_shape=SDS((N_DEV, NH, NBUCKET), F32),
                  compiler_params=_params(("arbitrary",)))(me, dbias, bucket)


def _two_bf16(x):
    hi = x.astype(BF)
    return hi, (x - hi.astype(F32)).astype(BF)


def _head_sums(x, seg):
    hi, lo = _two_bf16(x)
    s = seg[0:x.shape[1], :]
    return _dot(hi, s, 1, 0) + _dot(lo, s, 1, 0)


def _head_spread(v, seg, width):
    hi, lo = _two_bf16(v)
    s = seg[0:width, :]
    return _dot(hi, s, 1, 1) + _dot(lo, s, 1, 1)


def _head_norm(x, g_t, seg, by_head=False):
    if by_head:
        heads = [x[:, h * HD:(h + 1) * HD] for h in range(x.shape[1] // HD)]
        r = jnp.concatenate([jnp.broadcast_to(lax.rsqrt(jnp.mean(v * v, axis=-1, keepdims=True) + EPS), v.shape)
                             for v in heads], axis=1)
    else:
        r = lax.rsqrt(_head_sums(x * x, seg) * (1.0 / HD) + EPS)
        r = _head_spread(r, seg, x.shape[1])
    return x * r * g_t, r


def _head_norm_bwd(dy, x, r, g_t, seg):
    dg_t = jnp.sum(dy * (x * r), axis=0, keepdims=True)
    dgx = dy * g_t
    mean = _head_spread(_head_sums(x * dgx, seg) * (1.0 / HD), seg, x.shape[1])
    return r * dgx - x * (r * r * r) * mean, dg_t


def _fold_heads(v):
    out = v[:, 0:HD]
    for h in range(1, v.shape[1] // HD):
        out = out + v[:, h * HD:(h + 1) * HD]
    return out


def _mix_forward(P, zc8, zh8, pkv, first, cw, qg_t, kg_t, gco, gao, seg, sink_ref, bias_ref, by_head=False):
    gate_b = P[:, 0:CW]
    gate_c = P[:, CW:2 * CW]
    hc = P[:, 2 * CW:3 * CW]
    z = gate_c * hc
    keep = jnp.where(first, 0.0, 1.0)
    zp = zc8 * zh8 * keep
    p1 = zp[7:8, :]
    p2 = zp[6:7, :]
    row = lax.broadcasted_iota(jnp.int32, (BLK, 1), 0)
    z1 = jnp.where(row == 0, p1, pltpu.roll(z, 1, 0))
    z2 = jnp.where(row == 0, p2, jnp.where(row == 1, p1, pltpu.roll(z, 2, 0)))
    cz = cw[0:1, :] * z2 + cw[1:2, :] * z1 + cw[2:3, :] * z
    y_conv = gate_b * cz

    scale = HD ** -0.5
    qi = lax.broadcasted_iota(jnp.int32, (GQ * BLK, 2 * BLK), 0) & (BLK - 1)
    kj = lax.broadcasted_iota(jnp.int32, (GQ * BLK, 2 * BLK), 1)
    dd = qi + BLK - kj
    first_key = jnp.where(first, BLK, 0)
    valid = (dd >= 0) & (dd < BLK) & (kj >= first_key)

    q0 = 3 * CW
    k0 = q0 + AW
    v0 = k0 + NKV * HD
    q_raw = P[:, q0:k0]
    qn, rq = _head_norm(q_raw, qg_t, seg, by_head)
    qs = (qn * scale).astype(BF)
    k_raw = jnp.concatenate([pkv[:, 0:NKV * HD], P[:, k0:v0]], axis=0)
    kn, rk = _head_norm(k_raw, kg_t, seg, by_head)
    knb = kn.astype(BF)
    heads = []
    outs = []
    for kv in range(NKV):
        kb = knb[:, kv * HD:(kv + 1) * HD]
        vb = jnp.concatenate([pkv[:, NKV * HD + kv * HD:NKV * HD + (kv + 1) * HD],
                              P[:, v0 + kv * HD:v0 + (kv + 1) * HD]], axis=0).astype(BF)
        Q = jnp.concatenate([qs[:, (kv * GQ + g) * HD:(kv * GQ + g + 1) * HD] for g in range(GQ)], axis=0)
        S = _dot(Q, kb, 1, 1) + bias_ref[kv * GQ * BLK:(kv + 1) * GQ * BLK, :]
        S = jnp.where(valid, S, NEG_INF)
        sink = jnp.concatenate([jnp.full((BLK, 1), sink_ref[0, kv * GQ + g], F32) for g in range(GQ)], axis=0)
        m = jnp.maximum(jnp.max(S, axis=-1, keepdims=True), sink)
        p = jnp.exp(S - m)
        es = jnp.exp(sink - m)
        denom = jnp.sum(p, axis=-1, keepdims=True) + es
        probs = p / denom
        O = _dot(probs.astype(BF), vb, 1, 0)
        heads.append(dict(kb=kb, vb=vb, Q=Q, probs=probs, psink=es / denom, O=O))
        outs += [O[g * BLK:(g + 1) * BLK, :] for g in range(GQ)]
    y_attn = jnp.concatenate(outs, axis=1)

    rc = lax.rsqrt(jnp.mean(y_conv * y_conv, axis=-1, keepdims=True) + EPS)
    ra = lax.rsqrt(jnp.mean(y_attn * y_attn, axis=-1, keepdims=True) + EPS)
    y = jnp.concatenate([y_conv * rc * gco, y_attn * ra * gao], axis=1)
    return dict(gate_b=gate_b, gate_c=gate_c, hc=hc, z=z, z1=z1, z2=z2, cz=cz, y_conv=y_conv, y_attn=y_attn,
                rc=rc, ra=ra, heads=heads, y=y, row=row, scale=scale, q_raw=q_raw, rq=rq, k_raw=k_raw, rk=rk)


BPS = 2
TILE = BPS * BLK
KV0 = 3 * CW + AW


def _mix_in_specs(tile_of):
    return [
        pl.BlockSpec(memory_space=pltpu.SMEM),
        pl.BlockSpec((TILE, INW), lambda s: (tile_of(s), 0)),
        pl.BlockSpec((8, CW), lambda s: (jnp.maximum(tile_of(s) * (TILE // 8) - 1, 0), 1)),
        pl.BlockSpec((8, CW), lambda s: (jnp.maximum(tile_of(s) * (TILE // 8) - 1, 0), 2)),
        pl.BlockSpec((BLK, 2 * NKV * HD), lambda s: (jnp.maximum(tile_of(s) * BPS - 1, 0), KV0 // (2 * NKV * HD))),
    ]


def _block_inputs(tile, b, zc_ref, zh_ref, pkv_ref, first_tile):
    P = tile[b * BLK:(b + 1) * BLK, :]
    if b == 0:
        return P, zc_ref[...], zh_ref[...], pkv_ref[...], first_tile
    lo = b * BLK
    return P, tile[lo - 8:lo, CW:2 * CW], tile[lo - 8:lo, 2 * CW:3 * CW], tile[lo - BLK:lo, KV0:KV0 + 2 * NKV * HD], False


def _mix_param_specs():
    return [
        pl.BlockSpec((8, CW), lambda s: (0, 0)),
        pl.BlockSpec((1, AW), lambda s: (0, 0)),
        pl.BlockSpec((1, NKV * HD), lambda s: (0, 0)),
        pl.BlockSpec((1, CW), lambda s: (0, 0)),
        pl.BlockSpec((1, AW), lambda s: (0, 0)),
        pl.BlockSpec((AW, 128), lambda s: (0, 0)),
        pl.BlockSpec((NH * BLK, 2 * BLK), lambda s: (0, 0)),
    ]


def _mix_params(cw8, qg, kg, gco, gao, bias):
    seg = np.zeros((AW, 128), np.float32)
    seg[np.arange(AW), np.arange(AW) // HD] = 1.0
    return (cw8, jnp.tile(qg, (1, NH)), jnp.tile(kg, (1, NKV)), gco, gao, jnp.asarray(seg, BF), bias)


def _mix_fwd(proj, sinks, cw8, qg, kg, gco, gao, bias, comm=()):
    def body(sink_ref, p_ref, zc_ref, zh_ref, pkv_ref, cw_ref, qg_ref, kg_ref, gco_ref, gao_ref, seg_ref, bias_ref, y_ref):
        tile = p_ref[...]
        for b in range(BPS):
            f = _mix_forward(*_block_inputs(tile, b, zc_ref, zh_ref, pkv_ref, pl.program_id(0) == 0), cw_ref[...],
                             qg_ref[...], kg_ref[...], gco_ref[...], gao_ref[...], seg_ref[...], sink_ref, bias_ref, by_head=True)
            y_ref[b * BLK:(b + 1) * BLK, :] = f["y"].astype(BF)

    return _call(
        body, (sinks, proj, proj, proj, proj, *_mix_params(cw8, qg, kg, gco, gao, bias)), name="mix_fwd", grid=(T // TILE,),
        in_specs=_mix_in_specs(lambda s: s) + _mix_param_specs(),
        out_specs=[pl.BlockSpec((TILE, D), lambda s: (s, 0))], out_shape=[SDS((T, D), BF)],
        sem=("parallel",), vmem_mib=40, comm=comm, free=tuple(range(5, 12)))


def _mix_bwd(proj, dy, sinks, cw8, qg, kg, gco, gao, bias, comm=()):
    n_steps = T // TILE

    def tile_of(s):
        return n_steps - 1 - s

    def body(sink_ref, p_ref, zc_ref, zh_ref, pkv_ref, dy_ref, cw_ref, qg_ref, kg_ref, gco_ref, gao_ref, seg_ref, bias_ref,
             dproj_ref, dcw_ref, dqg_ref, dkg_ref, dgco_ref, dgao_ref, dsink_ref, dbias_ref,
             ndcz_ref, dkc_ref, dvc_ref):
        s = pl.program_id(0)

        @pl.when(s == 0)
        def _():
            for r in (dcw_ref, dqg_ref, dkg_ref, dgco_ref, dgao_ref, dsink_ref, dbias_ref, ndcz_ref, dkc_ref, dvc_ref):
                r[...] = jnp.zeros_like(r)

        tile = p_ref[...]
        for b in reversed(range(BPS)):
            one_block(b, _block_inputs(tile, b, zc_ref, zh_ref, pkv_ref, s == n_steps - 1),
                      dy_ref[b * BLK:(b + 1) * BLK, :], sink_ref, cw_ref, qg_ref, kg_ref, gco_ref, gao_ref, seg_ref, bias_ref,
                      dproj_ref.at[b * BLK:(b + 1) * BLK, :], dcw_ref, dqg_ref, dkg_ref, dgco_ref, dgao_ref, dsink_ref,
                      dbias_ref, ndcz_ref, dkc_ref, dvc_ref)

    def one_block(b, inputs, dy, sink_ref, cw_ref, qg_ref, kg_ref, gco_ref, gao_ref, seg_ref, bias_ref,
                  dproj_ref, dcw_ref, dqg_ref, dkg_ref, dgco_ref, dgao_ref, dsink_ref, dbias_ref,
                  ndcz_ref, dkc_ref, dvc_ref):
        cw = cw_ref[...]
        qg_v, kg_v, gco_v, gao_v, seg = qg_ref[...], kg_ref[...], gco_ref[...], gao_ref[...], seg_ref[...]
        f = _mix_forward(*inputs, cw, qg_v, kg_v, gco_v, gao_v, seg, sink_ref, bias_ref)
        dyc, dgco = _rms_bwd(dy[:, 0:CW], f["y_conv"], f["rc"], gco_v)
        dya, dgao = _rms_bwd(dy[:, CW:CW + AW], f["y_attn"], f["ra"], gao_v)
        dgco_ref[...] += dgco
        dgao_ref[...] += dgao

        row = f["row"]
        dgate_b = dyc * f["cz"]
        dcz = dyc * f["gate_b"]
        dcw_ref[0:1, :] += jnp.sum(dcz * f["z2"], axis=0, keepdims=True)
        dcw_ref[1:2, :] += jnp.sum(dcz * f["z1"], axis=0, keepdims=True)
        dcw_ref[2:3, :] += jnp.sum(dcz * f["z"], axis=0, keepdims=True)
        nxt = ndcz_ref[...]
        n0 = nxt[0:1, :]
        n1 = nxt[1:2, :]
        d1 = jnp.where(row == BLK - 1, n0, pltpu.roll(dcz, BLK - 1, 0))
        d2 = jnp.where(row == BLK - 1, n1, jnp.where(row == BLK - 2, n0, pltpu.roll(dcz, BLK - 2, 0)))
        dz = cw[2:3, :] * dcz + cw[1:2, :] * d1 + cw[0:1, :] * d2
        ndcz_ref[...] = dcz[0:8, :]
        dproj_ref[:, 0:CW] = dgate_b.astype(BF)
        dproj_ref[:, CW:2 * CW] = (dz * f["hc"]).astype(BF)
        dproj_ref[:, 2 * CW:3 * CW] = (dz * f["gate_c"]).astype(BF)

        scale = f["scale"]
        lane = lax.broadcasted_iota(jnp.int32, (1, 128), 1)
        dq_cols, dk_cols, dv_cols = [], [], []
        for kv in range(NKV):
            hd = f["heads"][kv]
            dO = jnp.concatenate([dya[:, (kv * GQ + g) * HD:(kv * GQ + g + 1) * HD] for g in range(GQ)], axis=0)
            delta = jnp.sum(dO * hd["O"], axis=-1, keepdims=True)
            dOb = dO.astype(BF)
            dP = _dot(dOb, hd["vb"], 1, 1)
            dS = hd["probs"] * (dP - delta)
            dsk = hd["psink"] * delta
            for g in range(GQ):
                h = kv * GQ + g
                tot = jnp.sum(dsk[g * BLK:(g + 1) * BLK, :], axis=0, keepdims=True)
                dsink_ref[...] -= jnp.where(lane == h, tot, 0.0)
            dbias_ref[kv * GQ * BLK:(kv + 1) * GQ * BLK, :] += dS
            dSb = dS.astype(BF)
            dQ = _dot(dSb, hd["kb"], 1, 0)
            dKb = _dot(dSb, hd["Q"], 0, 0)
            dVb = _dot(hd["probs"].astype(BF), dOb, 0, 0)
            dk_cols.append(dKb[BLK:, :] + dkc_ref[:, kv * HD:(kv + 1) * HD])
            dv_cols.append(dVb[BLK:, :] + dvc_ref[:, kv * HD:(kv + 1) * HD])
            dkc_ref[:, kv * HD:(kv + 1) * HD] = dKb[:BLK, :]
            dvc_ref[:, kv * HD:(kv + 1) * HD] = dVb[:BLK, :]
            dq_cols += [dQ[g * BLK:(g + 1) * BLK, :] for g in range(GQ)]
        dq_raw, dqg_t = _head_norm_bwd(jnp.concatenate(dq_cols, axis=1) * scale, f["q_raw"], f["rq"], qg_v, seg)
        dk_raw, dkg_t = _head_norm_bwd(jnp.concatenate(dk_cols, axis=1), f["k_raw"][BLK:, :], f["rk"][BLK:, :], kg_v, seg)
        dqg_ref[...] += _fold_heads(dqg_t)
        dkg_ref[...] += _fold_heads(dkg_t)
        dproj_ref[:, 3 * CW:INW] = jnp.concatenate([dq_raw, dk_raw] + dv_cols, axis=1).astype(BF)

    small = lambda r, c: pl.BlockSpec((r, c), lambda s: (0, 0))
    return _call(
        body, (sinks, proj, proj, proj, proj, dy, *_mix_params(cw8, qg, kg, gco, gao, bias)), name="mix_bwd", grid=(n_steps,),
        in_specs=_mix_in_specs(tile_of) + [pl.BlockSpec((TILE, D), lambda s: (tile_of(s), 0))] + _mix_param_specs(),
        out_specs=[pl.BlockSpec((TILE, INW), lambda s: (tile_of(s), 0)), small(8, CW), small(1, HD), small(1, HD),
                   small(1, CW), small(1, AW), small(1, 128), small(NH * BLK, 2 * BLK)],
        out_shape=[SDS((T, INW), BF), SDS((8, CW), F32), SDS((1, HD), F32), SDS((1, HD), F32), SDS((1, CW), F32),
                   SDS((1, AW), F32), SDS((1, 128), F32), SDS((NH * BLK, 2 * BLK), F32)],
        scratch_shapes=[pltpu.VMEM((8, CW), F32), pltpu.VMEM((BLK, NKV * HD), F32), pltpu.VMEM((BLK, NKV * HD), F32)],
        sem=("arbitrary",), vmem_mib=56, comm=comm, free=(1, 2, 3, 4) + tuple(range(6, 13)))


FT = 256
NFT = DFF // FT
RC = 128
NCH = T // RC
LEAD = 16


def _rows8(x):
    return jnp.sum(x.reshape(x.shape[0] // 8, 8, x.shape[1]), axis=0)


def _ffn_act_specs():
    return [
        pl.BlockSpec((T, FT), lambda j: (0, j)), pl.BlockSpec((T, FT), lambda j: (0, NFT + j)),
        pl.BlockSpec((8, FT), lambda j: (0, j)), pl.BlockSpec((8, FT), lambda j: (0, NFT + j)),
        pl.BlockSpec((1, FT), lambda j: (0, j)), pl.BlockSpec((1, FT), lambda j: (0, NFT + j)),
    ]


def _conv_rows(win, w, b, n):
    win = win.astype(F32)
    u = win[LEAD:LEAD + n]
    u1 = pltpu.roll(win, 1, 0)[LEAD:LEAD + n]
    u2 = pltpu.roll(win, 2, 0)[LEAD:LEAD + n]
    return u2, u1, u, w[0:1, :] * u2 + w[1:2, :] * u1 + w[2:3, :] * u + b


def _ffn_act(up, fw8, fb, comm=()):
    def body(ug_ref, uv_ref, wg_ref, wv_ref, bg_ref, bv_ref, a_ref):
        wg, wv, bg, bv = wg_ref[...], wv_ref[...], bg_ref[...], bv_ref[...]

        def chunk(win_g, win_v):
            gp = _conv_rows(win_g, wg, bg, RC)[3]
            vp = _conv_rows(win_v, wv, bv, RC)[3]
            return (gp * jax.nn.sigmoid(gp) * vp).astype(BF)

        zero = jnp.zeros((LEAD, FT), BF)
        a_ref[0:RC, :] = chunk(jnp.concatenate([zero, ug_ref[0:RC, :]], axis=0),
                               jnp.concatenate([zero, uv_ref[0:RC, :]], axis=0))

        def step(i, carry):
            r0 = pl.multiple_of(i * RC, RC)
            win = pl.ds(r0 - LEAD, RC + LEAD)
            a_ref[pl.ds(r0, RC), :] = chunk(ug_ref[win, :], uv_ref[win, :])
            return carry

        lax.fori_loop(1, NCH, step, 0)

    return _call(
        body, (up, up, fw8, fw8, fb, fb), name="ffn_act", grid=(NFT,), in_specs=_ffn_act_specs(),
        out_specs=[pl.BlockSpec((T, FT), lambda j: (0, j))], out_shape=[SDS((T, DFF), BF)],
        sem=("parallel",), vmem_mib=40, comm=comm, free=(2, 3, 4, 5))


def _ffn_act_bwd(up, da, fw8, fb, comm=()):
    ext = RC + LEAD

    def body(ug_ref, uv_ref, wg_ref, wv_ref, bg_ref, bv_ref, da_ref,
             dug_ref, duv_ref, dwg_ref, dwv_ref, dbg_ref, dbv_ref):
        wg, wv, bg, bv = wg_ref[...], wv_ref[...], bg_ref[...], bv_ref[...]

        def chunk(win_g, win_v, da_e):
            g2, g1, g0, gp = _conv_rows(win_g, wg, bg, ext)
            v2, v1, v0, vp = _conv_rows(win_v, wv, bv, ext)
            da_e = da_e.astype(F32)
            sig = jax.nn.sigmoid(gp)
            dvp = da_e * (gp * sig)
            dgp = da_e * vp * (sig * (1.0 + gp * (1.0 - sig)))

            def back(dp, w):
                return (w[2:3, :] * dp[0:RC] + w[1:2, :] * pltpu.roll(dp, ext - 1, 0)[0:RC]
                        + w[0:1, :] * pltpu.roll(dp, ext - 2, 0)[0:RC]).astype(BF)

            def sums(dp, u2, u1, u0):
                d = dp[0:RC]
                return [_rows8(d), _rows8(d * u2[0:RC]), _rows8(d * u1[0:RC]), _rows8(d * u0[0:RC])]

            return back(dgp, wg), back(dvp, wv), sums(dgp, g2, g1, g0) + sums(dvp, v2, v1, v0)

        zero = jnp.zeros((LEAD, FT), BF)
        dug, duv, acc = chunk(jnp.concatenate([zero, ug_ref[0:ext, :]], axis=0),
                              jnp.concatenate([zero, uv_ref[0:ext, :]], axis=0), da_ref[0:ext, :])
        dug_ref[0:RC, :] = dug
        duv_ref[0:RC, :] = duv

        def step(i, acc):
            r0 = pl.multiple_of(i * RC, RC)
            win = pl.ds(r0 - LEAD, ext + LEAD)
            dug, duv, part = chunk(ug_ref[win, :], uv_ref[win, :], da_ref[pl.ds(r0, ext), :])
            dug_ref[pl.ds(r0, RC), :] = dug
            duv_ref[pl.ds(r0, RC), :] = duv
            return [a + p for a, p in zip(acc, part)]

        acc = lax.fori_loop(1, NCH - 1, step, acc)
        r0 = T - RC
        tail = lambda ref, lo: jnp.concatenate([ref[lo:T, :], zero], axis=0)
        dug, duv, part = chunk(tail(ug_ref, r0 - LEAD), tail(uv_ref, r0 - LEAD), tail(da_ref, r0))
        dug_ref[r0:T, :] = dug
        duv_ref[r0:T, :] = duv
        tot = [jnp.sum(a + p, axis=0, keepdims=True) for a, p in zip(acc, part)]
        for k, (dw_ref, db_ref) in enumerate(((dwg_ref, dbg_ref), (dwv_ref, dbv_ref))):
            db_ref[...] = tot[4 * k]
            dw_ref[...] = jnp.zeros_like(dw_ref)
            for r in range(3):
                dw_ref[r:r + 1, :] = tot[4 * k + 1 + r]

    col = lambda r: pl.BlockSpec((r, FT), lambda j: (0, j))
    return _call(
        body, (up, up, fw8, fw8, fb, fb, da), name="ffn_act_bwd", grid=(NFT,),
        in_specs=_ffn_act_specs() + [pl.BlockSpec((T, FT), lambda j: (0, j))],
        out_specs=[col(T), col(T), col(8), col(8), col(1), col(1)],
        out_shape=[SDS((T, DFF), BF), SDS((T, DFF), BF), SDS((8, DFF), F32), SDS((8, DFF), F32),
                   SDS((1, DFF), F32), SDS((1, DFF), F32)],
        sem=("parallel",), vmem_mib=40, comm=comm, free=(0, 1, 2, 3, 4, 5))


def _ffn_down_bwd(dh2b, w_down, comm=()):
    tm = TM

    def body(d_ref, w_ref, o_ref):
        o_ref[...] = _dot(d_ref[...], w_ref[...], 1, 1).astype(BF)

    return _call(
        body, (dh2b, w_down), name="ffn_down_bwd", grid=(T // tm,),
        in_specs=[pl.BlockSpec((tm, D), lambda i: (i, 0)), _resident((DFF, D))],
        out_specs=[pl.BlockSpec((tm, DFF), lambda i: (i, 0))], out_shape=[SDS((T, DFF), BF)],
        sem=("parallel",), vmem_mib=40, comm=comm, free=(0, 1))


def _norm_matmul_bwd(name, a_list, w_t, k_offsets, xin, g, dres, want_bf16, comm=(), slot=None):
    tm = TM
    ks = [a.shape[1] for a in a_list]
    n_a = len(a_list)
    n_pre = 0 if slot is None else 1

    def body(*refs):
        refs = refs[n_pre:]
        a_refs = refs[:n_a]
        w_ref, x_ref, g_ref, r_ref = refs[n_a:n_a + 4]
        outs = refs[n_a + 4:]
        dx_ref, dg_ref = outs[0], (outs[-1] if slot is None else outs[-1].at[0])

        @pl.when(pl.program_id(0) == 0)
        def _():
            dg_ref[...] = jnp.zeros_like(dg_ref)

        du = _dot(a_refs[0][...], w_ref[k_offsets[0]:k_offsets[0] + ks[0], :], 1, 0)
        for k in range(1, n_a):
            du = du + _dot(a_refs[k][...], w_ref[k_offsets[k]:k_offsets[k] + ks[k], :], 1, 0)
        x = x_ref[...]
        r = lax.rsqrt(jnp.mean(x * x, axis=-1, keepdims=True) + EPS)
        dx, dg = _rms_bwd(du, x, r, g_ref[...])
        dx = r_ref[...] + dx
        dx_ref[...] = dx
        if want_bf16:
            outs[1][...] = dx.astype(BF)
        dg_ref[...] += dg

    tile = lambda c: pl.BlockSpec((tm, c), lambda i, *_: (i, 0))
    if slot is None:
        dg_spec, dg_shape = pl.BlockSpec((1, D), lambda i: (0, 0)), SDS((1, D), F32)
    else:
        dg_spec, dg_shape = pl.BlockSpec((1, 1, D), lambda i, slot_ref: (slot_ref[0], 0, 0)), SDS((N_DEV, 1, D), F32)
    out_specs = [tile(D)] + ([tile(D)] if want_bf16 else []) + [dg_spec]
    out_shape = [SDS((T, D), F32)] + ([SDS((T, D), BF)] if want_bf16 else []) + [dg_shape]
    return _call(
        body, (*a_list, w_t, xin, g, dres), name=name, grid=(T // tm,), prefetch=() if slot is None else (slot,),
        in_specs=[tile(k) for k in ks] + [_resident(w_t.shape), tile(D),
                                           pl.BlockSpec((1, D), lambda i, *_: (0, 0)), tile(D)],
        out_specs=out_specs, out_shape=out_shape, sem=("arbitrary",), vmem_mib=56, comm=comm, free=tuple(range(n_a + 4)))


def _out_bwd(dh1b, w_out, comm=()):
    tm = TM

    def body(d_ref, w_ref, o_ref):
        o_ref[...] = _dot(d_ref[...], w_ref[...], 1, 1)

    return _call(
        body, (dh1b, w_out), name="out_bwd", grid=(T // tm,),
        in_specs=[pl.BlockSpec((tm, D), lambda i: (i, 0)), _resident((D, D))],
        out_specs=[pl.BlockSpec((tm, D), lambda i: (i, 0))], out_shape=[SDS((T, D), F32)],
        sem=("parallel",), vmem_mib=32, comm=comm, free=(0, 1))


def _wgrad(name, a_list, b, old_a, comm=()):
    m_k = a_list[0].shape[1]
    tm = max(t for t in range(128, m_k // 2 + 1, 128) if m_k % t == 0)
    steps = [a.shape[1] // tm for a in a_list]
    starts = [sum(steps[:k]) for k in range(len(a_list))]
    n_a = len(a_list)

    def body(*refs):
        a_refs, b_ref, o_ref = refs[:n_a], refs[n_a], refs[n_a + 1]
        i = pl.program_id(0)
        for k in range(n_a):
            @pl.when((i >= starts[k]) & (i < starts[k] + steps[k]))
            def _(k=k):
                o_ref[...] = _dot(a_refs[k][...], b_ref[...], 0, 0).astype(BF)

    def a_spec(k):
        return pl.BlockSpec((T, tm), lambda i: (0, jnp.clip(i - starts[k], 0, steps[k] - 1)))

    m_total = tm * sum(steps)
    return _call(
        body, (*a_list, b), name=name, grid=(sum(steps),),
        in_specs=[a_spec(k) for k in range(n_a)] + [_resident((T, D))],
        out_specs=[pl.BlockSpec((tm, D), lambda i: (i, 0))], out_shape=[SDS((m_total, D), BF)],
        sem=("parallel",), vmem_mib=40, comm=comm, free=tuple(range(n_a)) if old_a else (n_a,))


def _chip_sum(name, gbf, from_sib, core, chip):
    h = gbf.shape[1]
    th = h // 2

    def body(core_ref, chip_ref, g_ref, s_ref, pbf_ref, own_ref):
        p = g_ref[0].astype(F32) + s_ref[0].astype(F32)
        pbf_ref[0] = p.astype(BF)

        @pl.when(pl.program_id(1) == chip_ref[0])
        def _():
            own_ref[...] = p

    grid_spec = pltpu.PrefetchScalarGridSpec(
        num_scalar_prefetch=2, grid=(h // th, N_CHIPS),
        in_specs=[pl.BlockSpec((1, th, D), lambda t, jj, core_ref, chip_ref: (2 * jj + core_ref[0], t, 0)),
                  pl.BlockSpec((1, th, D), lambda t, jj, core_ref, chip_ref: (jj, t, 0))],
        out_specs=[pl.BlockSpec((1, th, D), lambda t, jj, core_ref, chip_ref: (jj, t, 0)),
                   pl.BlockSpec((th, D), lambda t, jj, core_ref, chip_ref: (t, 0))],
    )
    return _pcall(
        body, name=name, grid_spec=grid_spec, out_shape=_in_hbm([SDS((N_CHIPS, h, D), BF), SDS((h, D), F32)]),
        compiler_params=_params(("arbitrary", "arbitrary"), 32),
    )(core, chip, *_from_hbm(gbf, from_sib))


def _final_sum(name, own, from_chips, core):
    h = own.shape[0]

    def body(core_ref, o_ref, r_ref, f_ref):
        f_ref[0] = ((o_ref[...] + r_ref[0].astype(F32)) + r_ref[1].astype(F32)) + r_ref[2].astype(F32)

    grid_spec = pltpu.PrefetchScalarGridSpec(
        num_scalar_prefetch=1, grid=(1,),
        in_specs=[pl.BlockSpec((h, D), lambda i, core_ref: (0, 0)), pl.BlockSpec((3, h, D), lambda i, core_ref: (0, 0, 0))],
        out_specs=pl.BlockSpec((1, h, D), lambda i, core_ref: (core_ref[0], 0, 0)),
    )
    return _pcall(body, name=name, grid_spec=grid_spec, out_shape=pltpu.HBM((2, h, D), F32),
                  compiler_params=_params(("arbitrary",), 40))(core, *_from_hbm(own, from_chips))


def _adam_math(w, g, m, v):
    nm = ADAM_B1 * m + (1.0 - ADAM_B1) * g
    nv = ADAM_B2 * v + (1.0 - ADAM_B2) * (g * g)
    m_hat = nm / (1.0 - ADAM_B1 ** ADAM_STEP)
    v_hat = nv / (1.0 - ADAM_B2 ** ADAM_STEP)
    return -ADAM_LR * (m_hat / (jnp.sqrt(v_hat) + ADAM_EPS) + ADAM_WD * w), nm, nv


def _adamw(name, w, g, m, v, tr, copy_g=False, stage=True):
    rows, cols = w.shape

    def body(w_ref, g_ref, m_ref, v_ref, *outs):
        g_val = g_ref[...]
        if copy_g:
            outs[0][...] = g_val
        d_ref, nm_ref, nv_ref = outs[-3:]
        d_ref[...], nm_ref[...], nv_ref[...] = _adam_math(w_ref[...], g_val, m_ref[...], v_ref[...])

    spec = pl.BlockSpec((tr, cols), lambda i: (i, 0))
    n_out = 4 if copy_g else 3
    return _call(body, (w, g, m, v), name=name, grid=(rows // tr,), in_specs=[spec] * 4, out_specs=[spec] * n_out,
                 out_shape=[SDS((rows, cols), F32)] * n_out, sem=("parallel",), vmem_mib=32,
                 free=(0, 2, 3) if stage else ())


C_G1, C_G2, C_GCO, C_GAO, C_DCW, C_DQG, C_DKG, C_SINK, C_SQ = 0, 1024, 2048, 2560, 3072, 4608, 4736, 4864, 5632
P_W = C_SQ + 128


def _pack_small(me, dfwg, dfwv, dfbg, dfbv, dg2, dgco, dgao, dcw8, dqg, dkg, dsink, sq):
    def body(me_ref, dfwg_r, dfwv_r, dfbg_r, dfbv_r, dg2_r, dgco_r, dgao_r, dcw_r, dqg_r, dkg_r, dsink_r, sq_r, o):
        o[...] = jnp.zeros_like(o)
        o[0, :, 0:DFF] = dfwg_r[...]
        o[0, :, DFF:2 * DFF] = dfwv_r[...]
        o[0, 3:4, 0:DFF] = dfbg_r[...]
        o[0, 3:4, DFF:2 * DFF] = dfbv_r[...]
        o[0, 4:5, C_G2:C_G2 + D] = dg2_r[...]
        o[0, 4:5, C_GCO:C_GCO + CW] = dgco_r[...]
        o[0, 4:5, C_GAO:C_GAO + AW] = dgao_r[...]
        for r in range(3):
            o[0, 4:5, C_DCW + r * CW:C_DCW + (r + 1) * CW] = dcw_r[r:r + 1, :]
        o[0, 4:5, C_DQG:C_DQG + HD] = dqg_r[...]
        o[0, 4:5, C_DKG:C_DKG + HD] = dkg_r[...]
        o[0, 4:5, C_SINK:C_SINK + 128] = dsink_r[...]
        o[0, :, C_SQ:C_SQ + 128] = sq_r[...]

    ins = (dfwg, dfwv, dfbg, dfbv, dg2, dgco, dgao, dcw8, dqg, dkg, dsink, sq)
    return _call(body, ins, name="pack_small", grid=(1,), prefetch=(me,),
                 in_specs=[pl.BlockSpec(a.shape, lambda i, me_ref: (0, 0)) for a in ins],
                 out_specs=[pl.BlockSpec((1, 8, P_W), lambda i, me_ref: (me_ref[0], 0, 0))],
                 out_shape=[SDS((N_DEV, 8, P_W), F32)], sem=("arbitrary",))[0]


N_SMALL = 11


def _small_adam(chip, p_all, g1_all, tbl_all, ws, ms, vs):
    fw_cols = 2 * DFF // N_CHIPS
    cw_cols = CW // N_CHIPS

    def body(chip_ref, p_ref, fw_ref, cw0_ref, cw1_ref, cw2_ref, g1_ref, tbl_ref, *refs):
        w_r, m_r, v_r = refs[0:N_SMALL], refs[N_SMALL:2 * N_SMALL], refs[2 * N_SMALL:3 * N_SMALL]
        outs = refs[3 * N_SMALL:]
        g_o, d_o, nm_o, nv_o = (outs[k * N_SMALL:(k + 1) * N_SMALL] for k in range(4))
        loss_o = outs[4 * N_SMALL]

        def total(ref):
            s = ref[0]
            for k in range(1, N_DEV):
                s = s + ref[k]
            return s

        S = total(p_ref)
        fw = total(fw_ref)
        cws = [total(r) for r in (cw0_ref, cw1_ref, cw2_ref)]

        def step(i, g, at):
            d, nm, nv = _adam_math(w_r[i][at], g, m_r[i][at], v_r[i][at])
            g_o[i][at], d_o[i][at], nm_o[i][at], nv_o[i][at] = g, d, nm, nv

        everything = (slice(None), slice(None))
        step(0, total(g1_ref), everything)
        for r in range(3):
            step(1, cws[r][4:5, :], (r, slice(None), slice(None)))
        step(2, S[4:5, C_DQG:C_DQG + HD], everything)
        step(3, S[4:5, C_DKG:C_DKG + HD], everything)
        step(4, total(tbl_ref), everything)
        step(5, S[4:5, C_SINK:C_SINK + NH], everything)
        step(6, S[4:5, C_GCO:C_GCO + CW], everything)
        step(7, S[4:5, C_GAO:C_GAO + AW], everything)
        step(8, S[4:5, C_G2:C_G2 + D], everything)
        for r in range(3):
            step(9, fw[r:r + 1, :], (r, slice(None), slice(None)))
        step(10, S[3:4, 0:2 * DFF], everything)
        sq = S[:, C_SQ:C_SQ + 128]
        loss_o[...] = jnp.sum(jnp.sum(sq, axis=1, keepdims=True), axis=0, keepdims=True) * (0.5 / D)

    def full(a):
        n = len(a.shape)
        return pl.BlockSpec(a.shape, lambda i, chip_ref: (0,) * n)

    params = [*ws, *ms, *vs]
    out = _call(
        body, (p_all, p_all, p_all, p_all, p_all, g1_all, tbl_all, *params), name="small_adam", grid=(1,), prefetch=(chip,),
        in_specs=[full(p_all),
                  pl.BlockSpec((N_DEV, 8, fw_cols), lambda i, chip_ref: (0, 0, chip_ref[0])),
                  *[pl.BlockSpec((N_DEV, 8, cw_cols), lambda i, chip_ref, r=r: (0, 0, (C_DCW + r * CW) // cw_cols + chip_ref[0]))
                    for r in range(3)],
                  full(g1_all), full(tbl_all), *[full(a) for a in params]],
        out_specs=[full(a) for a in ws] * 4 + [pl.BlockSpec((1, 1), lambda i, chip_ref: (0, 0))],
        out_shape=[SDS(a.shape, F32) for a in ws] * 4 + [SDS((1, 1), F32)], sem=("arbitrary",), vmem_mib=32)
    return out[0:N_SMALL], out[N_SMALL:2 * N_SMALL], out[2 * N_SMALL:3 * N_SMALL], out[3 * N_SMALL:4 * N_SMALL], out[4 * N_SMALL]


PLACE_STEPS = 4


def _place_specs(shards):
    rows = [s.shape[0] // PLACE_STEPS for s in shards]
    return ([pl.BlockSpec((r, D), lambda i, chip_ref: (i, 0)) for r in rows],
            [pl.BlockSpec((r, D), lambda i, chip_ref: (chip_ref[0] * PLACE_STEPS + i, 0)) for r in rows],
            [SDS((N_CHIPS * s.shape[0], D), BF) for s in shards])


def _place_first(chip, shard, conv_w, ffn_conv_w):
    def body(chip_ref, a, s0, s1, o, t0, t1):
        o[...] = a[...].astype(BF)

        @pl.when(pl.program_id(0) == 0)
        def _():
            for s, t in ((s0, t0), (s1, t1)):
                t[...] = jnp.zeros_like(t)
                t[0, 0:3, :] = s[...]

    ins, outs, shapes = _place_specs([shard])
    taps = (conv_w, ffn_conv_w)
    return _call(
        body, (shard, conv_w, ffn_conv_w), name="place_first", grid=(PLACE_STEPS,), prefetch=(chip,),
        in_specs=ins + [pl.BlockSpec(s.shape, lambda i, chip_ref: (0, 0)) for s in taps],
        out_specs=outs + [pl.BlockSpec((1, 8, s.shape[1]), lambda i, chip_ref: (chip_ref[0], 0, 0)) for s in taps],
        out_shape=shapes + [SDS((N_CHIPS, 8, s.shape[1]), F32) for s in taps],
        sem=("arbitrary",), vmem_mib=32, free=(0, 1, 2))


def _place_rest(chip, shards, table, bucket, comm):
    n = len(shards)

    def body(chip_ref, *refs):
        a, (tab_ref, bk_ref), o, bias_ref = refs[:n], refs[n:n + 2], refs[n + 2:2 * n + 2], refs[2 * n + 2]
        for src, dst in zip(a, o):
            dst[...] = src[...].astype(BF)

        @pl.when(pl.program_id(0) == 0)
        def _():
            bk = bk_ref[...]
            eq = [bk == b for b in range(NBUCKET)]
            for h in range(NH):
                acc = jnp.zeros((BLK, 2 * BLK), F32)
                for b in range(NBUCKET):
                    acc = jnp.where(eq[b], tab_ref[h, b], acc)
                bias_ref[h * BLK:(h + 1) * BLK, :] = acc

    ins, outs, shapes = _place_specs(shards)
    return _call(
        body, (*shards, table, bucket), name="place_rest", grid=(PLACE_STEPS,), prefetch=(chip,),
        in_specs=ins + [pl.BlockSpec(memory_space=pltpu.SMEM), pl.BlockSpec(bucket.shape, lambda i, chip_ref: (0, 0))],
        out_specs=outs + [pl.BlockSpec((NH * BLK, 2 * BLK), lambda i, chip_ref: (0, 0))],
        out_shape=shapes + [SDS((NH * BLK, 2 * BLK), F32)],
        sem=("arbitrary",), vmem_mib=32, comm=comm, free=tuple(range(n + 2)))


def kernel(x, norm_mix_g, w_in, conv_w, q_norm_g, k_norm_g, rel_bias_table, sinks, out_norm_conv_g, out_norm_attn_g, w_out, norm_ffn_g, w_up, ffn_conv_w, ffn_conv_b, w_down, loss_target, m_norm_mix_g, m_w_in, m_conv_w, m_q_norm_g, m_k_norm_g, m_rel_bias_table, m_sinks, m_out_norm_conv_g, m_out_norm_attn_g, m_w_out, m_norm_ffn_g, m_w_up, m_ffn_conv_w, m_ffn_conv_b, m_w_down, v_norm_mix_g, v_w_in, v_conv_w, v_q_norm_g, v_k_norm_g, v_rel_bias_table, v_sinks, v_out_norm_conv_g, v_out_norm_attn_g, v_w_out, v_norm_ffn_g, v_w_up, v_ffn_conv_w, v_ffn_conv_b, v_w_down):
    as_arg = lambda i: jnp.reshape(i, (1,)).astype(jnp.int32)
    chip = as_arg(2 * lax.axis_index("x") + lax.axis_index("y"))
    core = as_arg(lax.axis_index("c"))
    me = 2 * chip + core
    xs, tgt = x[0], loss_target[0]
    qg, kg, gco, gao, g1, g2, fb = q_norm_g, k_norm_g, out_norm_conv_g, out_norm_attn_g, norm_mix_g, norm_ffn_g, ffn_conv_b
    pieces = lambda g: g.reshape(N_DEV, g.shape[0] // N_DEV, D)
    whole = lambda f: f.reshape(2 * f.shape[1], D)

    bucket = jnp.asarray(_bucket_table())
    p_in, p_cw, p_fw = _place_first(chip, w_in[0].T, conv_w[0], ffn_conv_w[0])
    p_out, p_up, p_down, bias, w_int, cw_all, fw_all = _place_rest(
        chip, [w_out[0], w_up[0].T, w_down[0]], rel_bias_table.T, bucket,
        comm=[_t_gather(p_in), _t_small_weights(p_cw), _t_small_weights(p_fw)])
    cw8 = jnp.transpose(cw_all, (1, 0, 2)).reshape(8, CW)
    fw8 = jnp.transpose(fw_all, (1, 0, 2)).reshape(8, 2 * DFF)

    early = 3 / 11
    proj, u1, w_out_f, p_up = _inproj(xs, g1, w_int, comm=[_t_gather(p_out), _t_gather(p_up, (0, early))])
    y, w_upt = _mix_fwd(proj, sinks, cw8, qg, kg, gco, gao, bias, comm=[_t_gather(p_up, (early, 1))])
    h1, u2 = _outproj(y, w_out_f, xs, g2)
    up, = _ffn_up(u2, w_upt)
    a, w_down_f = _ffn_act(up, fw8, fb, comm=[_t_gather(p_down)])
    dh2, dh2b, sq = _ffn_down(a, w_down_f, h1, tgt)

    gdbf, = _wgrad("wgrad_down", [a], dh2b, True)
    da, sib_down = _ffn_down_bwd(dh2b, w_down_f, comm=[_t_sibling(pieces(gdbf))])
    pbf_down, own_down = _chip_sum("chip_sum_w_down", pieces(gdbf), sib_down, core, chip)
    dug, duv, dfwg, dfwv, dfbg, dfbv, chips_down = _ffn_act_bwd(up, da, fw8, fb, comm=[_t_chips(pbf_down)])
    fin_down = _final_sum("final_sum_w_down", own_down, chips_down, core)
    gubf, = _wgrad("wgrad_up", [dug, duv], u2, False)
    dh1, dh1b, dg2, sib_up, fin_down = _norm_matmul_bwd(
        "ffn_up_bwd", [dug, duv], w_upt, [0, DFF], h1, g2, dh2, True, comm=[_t_sibling(pieces(gubf)), _t_swap(fin_down)])
    pbf_up, own_up = _chip_sum("chip_sum_w_up", pieces(gubf), sib_up, core, chip)
    gobf, = _wgrad("wgrad_out", [y], dh1b, True)
    dy, sib_out = _out_bwd(dh1b, w_out_f, comm=[_t_sibling(pieces(gobf))])
    pbf_out, own_out = _chip_sum("chip_sum_w_out", pieces(gobf), sib_out, core, chip)
    dproj, dcw8, dqg, dkg, dgco, dgao, dsink, dbias, chips_up, chips_out = _mix_bwd(
        proj, dy, sinks, cw8, qg, kg, gco, gao, bias, comm=[_t_chips(pbf_up), _t_chips(pbf_out)])
    fin_up = _final_sum("final_sum_w_up", own_up, chips_up, core)
    fin_out = _final_sum("final_sum_w_out", own_out, chips_out, core)
    tbl_all = _band_bias_bwd(dbias, bucket, me)
    p_all = _pack_small(me, dfwg, dfwv, dfbg, dfbv, dg2, dgco, dgao, dcw8, dqg, dkg, dsink, sq)
    gibf, fin_up, p_all, tbl_all = _wgrad(
        "wgrad_in", [dproj], u1, False, comm=[_t_swap(fin_up), _t_allgather(p_all), _t_allgather(tbl_all)])
    sib_in, fin_out = _comm_call("to_sibling_last", [_t_sibling(pieces(gibf)), _t_swap(fin_out)])
    pbf_in, own_in = _chip_sum("chip_sum_w_in", pieces(gibf), sib_in, core, chip)
    dx, g1_all, chips_in = _norm_matmul_bwd(
        "in_bwd", [dproj], w_int, [0], xs, g1, dh1, False, comm=[_t_chips(pbf_in)], slot=me)
    fin_in = _final_sum("final_sum_w_in", own_in, chips_in, core)
    g1_all, fin_in = _comm_call("gather_last", [_t_allgather(g1_all), _t_swap(fin_in)])

    g_w_out, g_w_up, g_w_down = whole(fin_out), whole(fin_up).T, whole(fin_down)
    g_w_down, d_down, nm_down, nv_down = _adamw("adamw_w_down", w_down[0], g_w_down, m_w_down[0], v_w_down[0], 352, True)
    d_up, nm_up, nv_up = _adamw("adamw_w_up", w_up[0], g_w_up, m_w_up[0], v_w_up[0], 256, stage=False)
    g_w_out, d_out, nm_out, nv_out = _adamw("adamw_w_out", w_out[0], g_w_out, m_w_out[0], v_w_out[0], 256, True)
    g_w_in, d_in, nm_in, nv_in = [a.T for a in _adamw(
        "adamw_w_in", w_in[0].T, whole(fin_in), m_w_in[0].T, v_w_in[0].T, INW // N_CHIPS // 3, True)]
    taps = lambda a: jnp.transpose(a, (1, 0, 2))
    sw = [norm_mix_g, taps(conv_w), q_norm_g, k_norm_g, rel_bias_table.T, sinks, out_norm_conv_g, out_norm_attn_g,
          norm_ffn_g, taps(ffn_conv_w), ffn_conv_b]
    smm = [m_norm_mix_g, taps(m_conv_w), m_q_norm_g, m_k_norm_g, m_rel_bias_table.T, m_sinks, m_out_norm_conv_g,
           m_out_norm_attn_g, m_norm_ffn_g, taps(m_ffn_conv_w), m_ffn_conv_b]
    smv = [v_norm_mix_g, taps(v_conv_w), v_q_norm_g, v_k_norm_g, v_rel_bias_table.T, v_sinks, v_out_norm_conv_g,
           v_out_norm_attn_g, v_norm_ffn_g, taps(v_ffn_conv_w), v_ffn_conv_b]
    *small_out, loss = _small_adam(chip, p_all, g1_all, tbl_all, sw, smm, smv)
    sg, sd, snm, snv = [list(r) for r in small_out]
    for r in (sg, sd, snm, snv):
        r[1], r[4], r[9] = taps(r[1]), r[4].T, taps(r[9])

    def order(s, b_in, b_out, b_up, b_down):
        return (s[0], b_in[None], s[1], s[2], s[3], s[4], s[5], s[6], s[7], b_out[None], s[8], b_up[None],
                s[9], s[10], b_down[None])

    return (loss.reshape(()), dx[None],
            *order(sg, g_w_in, g_w_out, g_w_up, g_w_down),
            *order(sd, d_in, d_out, d_up, d_down),
            *order(snm, nm_in, nm_out, nm_up, nm_down),
            *order(snv, nv_in, nv_out, nv_up, nv_down))
```

```python
import functools
import math

import numpy as np

import jax
import jax.numpy as jnp
from jax import lax
from jax.experimental import pallas as pl
from jax.experimental.pallas import tpu as pltpu

F32 = jnp.float32
BF = jnp.bfloat16
SDS = jax.ShapeDtypeStruct

T = 2048
D = 1024
CW = 512
AW = 512
HD = 64
NH = 8
NKV = 2
GQ = 4
INW = 2304
DFF = 2816
BLK = 128
NB = T // BLK
NBUCKET = 32
EPS = 1e-6
NEG_INF = -1e30
N_CHIPS = 4
N_DEV = 8

ADAM_LR = 0.001
ADAM_B1 = 0.9
ADAM_B2 = 0.999
ADAM_EPS = 1e-08
ADAM_WD = 0.01
ADAM_STEP = 10

TM = 512
MIB = 1024 * 1024
MESH = pl.DeviceIdType.MESH
ANY = pl.BlockSpec(memory_space=pl.ANY)

_pcall = pl.pallas_call


def _params(sem=None, vmem_mib=None):
    kw = {}
    if sem is not None:
        kw["dimension_semantics"] = sem
    if vmem_mib is not None:
        kw["vmem_limit_bytes"] = vmem_mib * MIB
    return pltpu.CompilerParams(**kw)


def _resident(shape):
    return pl.BlockSpec(shape, lambda *_: (0,) * len(shape), pipeline_mode=pl.Buffered(1))


def _dot(a, b, ca, cb):
    return lax.dot_general(a, b, (((ca,), (cb,)), ((), ())), preferred_element_type=F32)


def _rms_bwd(dy, x, r, g):
    dg = jnp.sum(dy * (x * r), axis=0, keepdims=True)
    dgx = dy * g
    dx = r * dgx - x * (r * r * r) * jnp.mean(x * dgx, axis=-1, keepdims=True)
    return dx, dg


def _where():
    x, y, c = lax.axis_index("x"), lax.axis_index("y"), lax.axis_index("c")
    return x, y, c, [(1 - x, y), (x, 1 - y), (1 - x, 1 - y)]


def _rcopy(src, dst, ssem, rsem, dev):
    return pltpu.make_async_remote_copy(src_ref=src, dst_ref=dst, send_sem=ssem, recv_sem=rsem, device_id=dev,
                                        device_id_type=MESH)


class _Task:
    def __init__(self, ins, outs, alias, n_sem, start, finish, middle=None):
        self.ins, self.outs, self.alias, self.n_sem, self.start, self.finish = ins, outs, alias, n_sem, start, finish
        self.middle = middle if middle is not None else (lambda *args: None)


ROWS16 = 16


def _t_gather(placed, part=(0, 1)):
    R = placed.shape[0] // N_CHIPS
    q = R // 4
    lo, hi = (round(f * (q // ROWS16)) * ROWS16 for f in part)

    def quarter(chip_index, core, k):
        return pl.ds(pl.multiple_of(chip_index * R + core * 2 * q + k * q + lo, ROWS16), hi - lo)

    def places():
        x, y, c, _ = _where()
        return c, 2 * x + y, 2 * (1 - x) + y, 2 * x + (1 - y), 2 * (1 - x) + (1 - y), (1 - x, y, c), (x, 1 - y, c), (x, y, 1 - c)

    def copy(buf, k, chip_index, core, quart, ss, rs, b, dev):
        window = buf.at[quarter(chip_index, core, quart)]
        return _rcopy(window, window, ss.at[b + k], rs.at[b + k], dev)

    def start(cin, cout, ss, rs, b):
        c, me, _, _, _, x_nbr, y_nbr, _ = places()
        for k, (quart, dev) in enumerate(((0, x_nbr), (1, y_nbr), (1, x_nbr), (0, y_nbr))):
            copy(cout[0], k, me, c, quart, ss, rs, b, dev).start()

    def middle(cin, cout, ss, rs, b):
        c, _, xc, yc, _, x_nbr, y_nbr, sib = places()
        for k, chip_index, quart, dev in ((0, xc, 0, y_nbr), (1, yc, 1, x_nbr)):
            copy(cout[0], k, chip_index, c, quart, ss, rs, b, dev).wait_recv()
            copy(cout[0], 4 + k, chip_index, c, quart, ss, rs, b, dev).start()
            copy(cout[0], 6 + k, chip_index, c, quart, ss, rs, b, sib).start()

    later = ((2, 1, 1), (3, 2, 0), (4, 3, 0), (5, 3, 1))

    def finish(cin, cout, ss, rs, b):
        c, me, xc, yc, dc, _, _, sib = places()
        chip_of = {1: xc, 2: yc, 3: dc}
        for k, whose, quart in later:
            copy(cout[0], k, chip_of[whose], c, quart, ss, rs, b, sib).wait_recv()
            copy(cout[0], 6 + k, chip_of[whose], c, quart, ss, rs, b, sib).start()
        for k, whose, quart in ((0, 1, 0), (1, 2, 1)) + later:
            copy(cout[0], 6 + k, chip_of[whose], 1 - c, quart, ss, rs, b, sib).wait_recv()
        for k in range(12):
            copy(cout[0], k, me, c, 0, ss, rs, b, sib).wait_send()

    return _Task([placed], [SDS(placed.shape, placed.dtype)], [(0, 0)], 12, start, finish, middle)


def _t_small_weights(buf):
    def start(cin, cout, ss, rs, b):
        x, y, c, chips = _where()
        mine = cout[0].at[2 * x + y]
        for r, (px, py) in enumerate(chips):
            _rcopy(mine, mine, ss.at[b + r], rs.at[b + r], (px, py, c)).start()

    def finish(cin, cout, ss, rs, b):
        x, y, c, chips = _where()
        for r, (px, py) in enumerate(chips):
            got = cout[0].at[2 * px + py]
            _rcopy(got, got, ss.at[b + r], rs.at[b + r], (px, py, c)).wait_recv()
        for r, (px, py) in enumerate(chips):
            mine = cout[0].at[2 * x + y]
            _rcopy(mine, mine, ss.at[b + r], rs.at[b + r], (px, py, c)).wait_send()

    return _Task([buf], [SDS(buf.shape, buf.dtype)], [(0, 0)], 3, start, finish)


def _t_sibling(gbf):
    def start(cin, cout, ss, rs, b):
        x, y, c, _ = _where()
        for jj in range(N_CHIPS):
            _rcopy(cin[0].at[2 * jj + (1 - c)], cout[0].at[jj], ss.at[b + jj], rs.at[b + jj], (x, y, 1 - c)).start()

    def finish(cin, cout, ss, rs, b):
        x, y, c, _ = _where()
        for jj in range(N_CHIPS):
            got = cout[0].at[jj]
            _rcopy(got, got, ss.at[b + jj], rs.at[b + jj], (x, y, 1 - c)).wait_recv()
        for jj in range(N_CHIPS):
            got = cout[0].at[jj]
            _rcopy(got, got, ss.at[b + jj], rs.at[b + jj], (x, y, 1 - c)).wait_send()

    return _Task([gbf], [SDS((N_CHIPS,) + gbf.shape[1:], BF)], [], N_CHIPS, start, finish)


def _t_chips(pbf):
    def start(cin, cout, ss, rs, b):
        x, y, c, chips = _where()
        for r, (px, py) in enumerate(chips):
            _rcopy(cin[0].at[2 * px + py], cout[0].at[r], ss.at[b + r], rs.at[b + r], (px, py, c)).start()

    def finish(cin, cout, ss, rs, b):
        x, y, c, chips = _where()
        for r, (px, py) in enumerate(chips):
            got = cout[0].at[r]
            _rcopy(got, got, ss.at[b + r], rs.at[b + r], (px, py, c)).wait_recv()
        for r, (px, py) in enumerate(chips):
            got = cout[0].at[r]
            _rcopy(got, got, ss.at[b + r], rs.at[b + r], (px, py, c)).wait_send()

    return _Task([pbf], [SDS((3,) + pbf.shape[1:], BF)], [], 3, start, finish)


def _t_swap(fin):
    def start(cin, cout, ss, rs, b):
        x, y, c, _ = _where()
        mine = cout[0].at[c]
        _rcopy(mine, mine, ss.at[b], rs.at[b], (x, y, 1 - c)).start()

    def finish(cin, cout, ss, rs, b):
        x, y, c, _ = _where()
        got = cout[0].at[1 - c]
        _rcopy(got, got, ss.at[b], rs.at[b], (x, y, 1 - c)).wait_recv()
        _rcopy(got, got, ss.at[b], rs.at[b], (x, y, 1 - c)).wait_send()

    return _Task([fin], [SDS(fin.shape, fin.dtype)], [(0, 0)], 1, start, finish)


def _t_allgather(buf):
    def peers():
        x, y, c, _ = _where()
        out = []
        for rel in range(1, N_DEV):
            px, py, pc = x ^ ((rel >> 2) & 1), y ^ ((rel >> 1) & 1), c ^ (rel & 1)
            out.append((rel - 1, 4 * px + 2 * py + pc, (px, py, pc)))
        return 4 * x + 2 * y + c, out

    def start(cin, cout, ss, rs, b):
        me, ps = peers()
        mine = cout[0].at[me]
        for k, _, dev in ps:
            _rcopy(mine, mine, ss.at[b + k], rs.at[b + k], dev).start()

    def finish(cin, cout, ss, rs, b):
        me, ps = peers()
        for k, pidx, dev in ps:
            got = cout[0].at[pidx]
            _rcopy(got, got, ss.at[b + k], rs.at[b + k], dev).wait_recv()
        for k, _, dev in ps:
            mine = cout[0].at[me]
            _rcopy(mine, mine, ss.at[b + k], rs.at[b + k], dev).wait_send()

    return _Task([buf], [SDS(buf.shape, buf.dtype)], [(0, 0)], N_DEV - 1, start, finish)


def _run_tasks(comm, which, cin, cout, ss, rs):
    i0 = o0 = s0 = 0
    for t in comm:
        getattr(t, which)(cin[i0:i0 + len(t.ins)], cout[o0:o0 + len(t.outs)], ss, rs, s0)
        i0, o0, s0 = i0 + len(t.ins), o0 + len(t.outs), s0 + t.n_sem


def _from_hbm(*arrays):
    return [pltpu.with_memory_space_constraint(a, pltpu.HBM) for a in arrays]


def _in_hbm(shapes):
    return [pltpu.HBM(s.shape, s.dtype) for s in shapes]


def _comm_layout(comm, n_in, n_out):
    c_in = [a for t in comm for a in t.ins]
    c_out = [s for t in comm for s in t.outs]
    aliases, i0, o0 = {}, 0, 0
    for t in comm:
        for i, o in t.alias:
            aliases[n_in + i0 + i] = n_out + o0 + o
        i0, o0 = i0 + len(t.ins), o0 + len(t.outs)
    return c_in, c_out, aliases, sum(t.n_sem for t in comm)


def _call(body, operands, *, name, grid, in_specs, out_specs, out_shape, scratch_shapes=(), sem=None, vmem_mib=None, comm=(),
          free=(), prefetch=()):
    operands = [o if s.memory_space == pltpu.SMEM or k in free else pltpu.with_memory_space_constraint(o, pltpu.HBM)
                for k, (o, s) in enumerate(zip(operands, in_specs))]
    n_pre, n_in, n_out, n_scr = len(prefetch), len(in_specs), len(out_specs), len(scratch_shapes)
    c_in, c_out, aliases, n_sem = _comm_layout(comm, n_pre + n_in, n_out)
    sems = [pltpu.SemaphoreType.DMA((n_sem,)), pltpu.SemaphoreType.DMA((n_sem,))] if comm else []

    def wrapped(*refs):
        pre, refs = refs[:n_pre], refs[n_pre:]
        ins, cin = refs[:n_in], refs[n_in:n_in + len(c_in)]
        rest = refs[n_in + len(c_in):]
        outs, cout = rest[:n_out], rest[n_out:n_out + len(c_out)]
        rest = rest[n_out + len(c_out):]
        scr, csem = rest[:n_scr], rest[n_scr:]
        if not comm:
            return body(*pre, *ins, *outs, *scr)
        step = functools.reduce(lambda acc, k: acc * grid[k] + pl.program_id(k), range(len(grid)), 0)
        n_steps = math.prod(grid)
        pl.when(step == 0)(lambda: _run_tasks(comm, "start", cin, cout, *csem))
        pl.when(step == n_steps // 2)(lambda: _run_tasks(comm, "middle", cin, cout, *csem))
        body(*pre, *ins, *outs, *scr)
        pl.when(step == n_steps - 1)(lambda: _run_tasks(comm, "finish", cin, cout, *csem))

    grid_spec = pltpu.PrefetchScalarGridSpec(
        num_scalar_prefetch=n_pre, grid=grid, in_specs=list(in_specs) + [ANY] * len(c_in),
        out_specs=list(out_specs) + [ANY] * len(c_out), scratch_shapes=list(scratch_shapes) + sems)
    return _pcall(
        wrapped, name=name, grid_spec=grid_spec, out_shape=_in_hbm(list(out_shape) + c_out), input_output_aliases=aliases,
        compiler_params=_params(("arbitrary",) * len(grid) if comm else sem, vmem_mib),
    )(*prefetch, *operands, *_from_hbm(*c_in))


def _comm_call(name, comm):
    c_in, c_out, aliases, n_sem = _comm_layout(comm, 0, 0)

    def body(*refs):
        cin, cout, (ss, rs) = refs[:len(c_in)], refs[len(c_in):len(c_in) + len(c_out)], refs[len(c_in) + len(c_out):]
        for phase in ("start", "middle", "finish"):
            _run_tasks(comm, phase, cin, cout, ss, rs)

    return _pcall(
        body, name=name, in_specs=[ANY] * len(c_in), out_specs=[ANY] * len(c_out), out_shape=_in_hbm(c_out),
        scratch_shapes=[pltpu.SemaphoreType.DMA((n_sem,)), pltpu.SemaphoreType.DMA((n_sem,))],
        input_output_aliases=aliases,
    )(*_from_hbm(*c_in))


def _inproj(x, g1, w_int, comm=()):
    tm = TM

    def body(x_ref, g_ref, w_ref, proj_ref, u_ref):
        xf = x_ref[...]
        r = lax.rsqrt(jnp.mean(xf * xf, axis=-1, keepdims=True) + EPS)
        u = (xf * r * g_ref[...]).astype(BF)
        u_ref[...] = u
        proj_ref[...] = _dot(u, w_ref[...], 1, 1)

    return _call(
        body, (x, g1, w_int), name="inproj", grid=(T // tm,),
        in_specs=[pl.BlockSpec((tm, D), lambda i: (i, 0)), pl.BlockSpec((1, D), lambda i: (0, 0)),
                  _resident((INW, D))],
        out_specs=[pl.BlockSpec((tm, INW), lambda i: (i, 0)), pl.BlockSpec((tm, D), lambda i: (i, 0))],
        out_shape=[SDS((T, INW), F32), SDS((T, D), BF)], sem=("parallel",), vmem_mib=40, comm=comm, free=(0, 1))


def _outproj(y, w_out, x, g2):
    tm = TM

    def body(y_ref, w_ref, x_ref, g_ref, h1_ref, u2_ref):
        h1 = x_ref[...] + _dot(y_ref[...], w_ref[...], 1, 0)
        h1_ref[...] = h1
        r = lax.rsqrt(jnp.mean(h1 * h1, axis=-1, keepdims=True) + EPS)
        u2_ref[...] = (h1 * r * g_ref[...]).astype(BF)

    return _call(
        body, (y, w_out, x, g2), name="outproj", grid=(T // tm,),
        in_specs=[pl.BlockSpec((tm, D), lambda i: (i, 0)), _resident((D, D)),
                  pl.BlockSpec((tm, D), lambda i: (i, 0)), pl.BlockSpec((1, D), lambda i: (0, 0))],
        out_specs=[pl.BlockSpec((tm, D), lambda i: (i, 0)), pl.BlockSpec((tm, D), lambda i: (i, 0))],
        out_shape=[SDS((T, D), F32), SDS((T, D), BF)], sem=("parallel",), vmem_mib=32, free=(1, 2, 3))


def _ffn_up(u2, w_upt, comm=()):
    tm, tn = 1024, 512

    def body(u_ref, w_ref, o_ref):
        o_ref[...] = _dot(u_ref[...], w_ref[...], 1, 1).astype(BF)

    return _call(
        body, (u2, w_upt), name="ffn_up", grid=(T // tm, 2 * DFF // tn),
        in_specs=[pl.BlockSpec((tm, D), lambda i, j: (i, 0)), pl.BlockSpec((tn, D), lambda i, j: (j, 0))],
        out_specs=[pl.BlockSpec((tm, tn), lambda i, j: (i, j))], out_shape=[SDS((T, 2 * DFF), BF)],
        sem=("parallel", "parallel"), vmem_mib=32, comm=comm, free=(1,))


def _ffn_down(a, w_down, h1, tgt):
    tm = TM

    def body(a_ref, w_ref, h1_ref, t_ref, dh_ref, dhb_ref, l_ref):
        @pl.when(pl.program_id(0) == 0)
        def _():
            l_ref[...] = jnp.zeros_like(l_ref)

        h2 = h1_ref[...] + _dot(a_ref[...], w_ref[...], 1, 0)
        e = h2 - t_ref[...]
        dh = e * (1.0 / D)
        dh_ref[...] = dh
        dhb_ref[...] = dh.astype(BF)
        e2 = jnp.sum((e * e).reshape(tm // 8, 8, D), axis=0)
        acc = e2[:, 0:128]
        for k in range(1, D // 128):
            acc = acc + e2[:, k * 128:(k + 1) * 128]
        l_ref[...] += acc

    return _call(
        body, (a, w_down, h1, tgt), name="ffn_down", grid=(T // tm,),
        in_specs=[pl.BlockSpec((tm, DFF), lambda i: (i, 0)), _resident((DFF, D)),
                  pl.BlockSpec((tm, D), lambda i: (i, 0)), pl.BlockSpec((tm, D), lambda i: (i, 0))],
        out_specs=[pl.BlockSpec((tm, D), lambda i: (i, 0)), pl.BlockSpec((tm, D), lambda i: (i, 0)),
                   pl.BlockSpec((8, 128), lambda i: (0, 0))],
        out_shape=[SDS((T, D), F32), SDS((T, D), BF), SDS((8, 128), F32)], sem=("arbitrary",), vmem_mib=40, free=(2, 3))


def _bucket_table():
    q = np.arange(BLK, dtype=np.int32)[:, None]
    j = np.arange(2 * BLK, dtype=np.int32)[None, :]
    n = np.maximum(q + BLK - j, 0)
    nf = np.maximum(n, 1).astype(np.float32)
    max_exact = NBUCKET // 2
    large = max_exact + (np.log(nf / np.float32(max_exact)) / np.float32(math.log(BLK / max_exact))
                         * np.float32(NBUCKET - max_exact)).astype(np.int32)
    large = np.minimum(large, NBUCKET - 1)
    return np.where(n < max_exact, n, large).astype(np.int32)


def _band_bias_bwd(dbias, bucket, me):
    def body(me_ref, db_ref, bk_ref, o_ref):
        bk = bk_ref[...]
        for b in range(NBUCKET):
            m = bk == b
            for h in range(NH):
                v = jnp.where(m, db_ref[h * BLK:(h + 1) * BLK, :], 0.0)
                s = jnp.sum(jnp.sum(v, axis=1, keepdims=True), axis=0, keepdims=True)
                o_ref[0, h:h + 1, b:b + 1] = s

    grid_spec = pltpu.PrefetchScalarGridSpec(
        num_scalar_prefetch=1, grid=(1,),
        in_specs=[pl.BlockSpec((NH * BLK, 2 * BLK), lambda i, me_ref: (0, 0)),
                  pl.BlockSpec((BLK, 2 * BLK), lambda i, me_ref: (0, 0))],
        out_specs=pl.BlockSpec((1, NH, NBUCKET), lambda i, me_ref: (me_ref[0], 0, 0)),
    )
    return _pcall(body, name="band_bias_bwd", grid_spec=grid_spec, out_shape=SDS((N_DEV, NH, NBUCKET), F32),
                  compiler_params=_params(("arbitrary",)))(me, dbias, bucket)


def _two_bf16(x):
    hi = x.astype(BF)
    return hi, (x - hi.astype(F32)).astype(BF)


def _head_sums(x, seg):
    hi, lo = _two_bf16(x)
    s = seg[0:x.shape[1], :]
    return _dot(hi, s, 1, 0) + _dot(lo, s, 1, 0)


def _head_spread(v, seg, width):
    hi, lo = _two_bf16(v)
    s = seg[0:width, :]
    return _dot(hi, s, 1, 1) + _dot(lo, s, 1, 1)


def _head_norm(x, g_t, seg, by_head=False):
    if by_head:
        heads = [x[:, h * HD:(h + 1) * HD] for h in range(x.shape[1] // HD)]
        r = jnp.concatenate([jnp.broadcast_to(lax.rsqrt(jnp.mean(v * v, axis=-1, keepdims=True) + EPS), v.shape)
                             for v in heads], axis=1)
    else:
        r = lax.rsqrt(_head_sums(x * x, seg) * (1.0 / HD) + EPS)
        r = _head_spread(r, seg, x.shape[1])
    return x * r * g_t, r


def _head_norm_bwd(dy, x, r, g_t, seg):
    dg_t = jnp.sum(dy * (x * r), axis=0, keepdims=True)
    dgx = dy * g_t
    mean = _head_spread(_head_sums(x * dgx, seg) * (1.0 / HD), seg, x.shape[1])
    return r * dgx - x * (r * r * r) * mean, dg_t


def _fold_heads(v):
    out = v[:, 0:HD]
    for h in range(1, v.shape[1] // HD):
        out = out + v[:, h * HD:(h + 1) * HD]
    return out


def _mix_forward(P, zc8, zh8, pkv, first, cw, qg_t, kg_t, gco, gao, seg, sink_ref, bias_ref, by_head=False):
    gate_b = P[:, 0:CW]
    gate_c = P[:, CW:2 * CW]
    hc = P[:, 2 * CW:3 * CW]
    z = gate_c * hc
    keep = jnp.where(first, 0.0, 1.0)
    zp = zc8 * zh8 * keep
    p1 = zp[7:8, :]
    p2 = zp[6:7, :]
    row = lax.broadcasted_iota(jnp.int32, (BLK, 1), 0)
    z1 = jnp.where(row == 0, p1, pltpu.roll(z, 1, 0))
    z2 = jnp.where(row == 0, p2, jnp.where(row == 1, p1, pltpu.roll(z, 2, 0)))
    cz = cw[0:1, :] * z2 + cw[1:2, :] * z1 + cw[2:3, :] * z
    y_conv = gate_b * cz

    scale = HD ** -0.5
    qi = lax.broadcasted_iota(jnp.int32, (GQ * BLK, 2 * BLK), 0) & (BLK - 1)
    kj = lax.broadcasted_iota(jnp.int32, (GQ * BLK, 2 * BLK), 1)
    dd = qi + BLK - kj
    first_key = jnp.where(first, BLK, 0)
    valid = (dd >= 0) & (dd < BLK) & (kj >= first_key)

    q0 = 3 * CW
    k0 = q0 + AW
    v0 = k0 + NKV * HD
    q_raw = P[:, q0:k0]
    qn, rq = _head_norm(q_raw, qg_t, seg, by_head)
    qs = (qn * scale).astype(BF)
    k_raw = jnp.concatenate([pkv[:, 0:NKV * HD], P[:, k0:v0]], axis=0)
    kn, rk = _head_norm(k_raw, kg_t, seg, by_head)
    knb = kn.astype(BF)
    heads = []
    outs = []
    for kv in range(NKV):
        kb = knb[:, kv * HD:(kv + 1) * HD]
        vb = jnp.concatenate([pkv[:, NKV * HD + kv * HD:NKV * HD + (kv + 1) * HD],
                              P[:, v0 + kv * HD:v0 + (kv + 1) * HD]], axis=0).astype(BF)
        Q = jnp.concatenate([qs[:, (kv * GQ + g) * HD:(kv * GQ + g + 1) * HD] for g in range(GQ)], axis=0)
        S = _dot(Q, kb, 1, 1) + bias_ref[kv * GQ * BLK:(kv + 1) * GQ * BLK, :]
        S = jnp.where(valid, S, NEG_INF)
        sink = jnp.concatenate([jnp.full((BLK, 1), sink_ref[0, kv * GQ + g], F32) for g in range(GQ)], axis=0)
        m = jnp.maximum(jnp.max(S, axis=-1, keepdims=True), sink)
        p = jnp.exp(S - m)
        es = jnp.exp(sink - m)
        denom = jnp.sum(p, axis=-1, keepdims=True) + es
        probs = p / denom
        O = _dot(probs.astype(BF), vb, 1, 0)
        heads.append(dict(kb=kb, vb=vb, Q=Q, probs=probs, psink=es / denom, O=O))
        outs += [O[g * BLK:(g + 1) * BLK, :] for g in range(GQ)]
    y_attn = jnp.concatenate(outs, axis=1)

    rc = lax.rsqrt(jnp.mean(y_conv * y_conv, axis=-1, keepdims=True) + EPS)
    ra = lax.rsqrt(jnp.mean(y_attn * y_attn, axis=-1, keepdims=True) + EPS)
    y = jnp.concatenate([y_conv * rc * gco, y_attn * ra * gao], axis=1)
    return dict(gate_b=gate_b, gate_c=gate_c, hc=hc, z=z, z1=z1, z2=z2, cz=cz, y_conv=y_conv, y_attn=y_attn,
                rc=rc, ra=ra, heads=heads, y=y, row=row, scale=scale, q_raw=q_raw, rq=rq, k_raw=k_raw, rk=rk)


BPS = 2
TILE = BPS * BLK
KV0 = 3 * CW + AW


def _mix_in_specs(tile_of):
    return [
        pl.BlockSpec(memory_space=pltpu.SMEM),
        pl.BlockSpec((TILE, INW), lambda s: (tile_of(s), 0)),
        pl.BlockSpec((8, CW), lambda s: (jnp.maximum(tile_of(s) * (TILE // 8) - 1, 0), 1)),
        pl.BlockSpec((8, CW), lambda s: (jnp.maximum(tile_of(s) * (TILE // 8) - 1, 0), 2)),
        pl.BlockSpec((BLK, 2 * NKV * HD), lambda s: (jnp.maximum(tile_of(s) * BPS - 1, 0), KV0 // (2 * NKV * HD))),
    ]


def _block_inputs(tile, b, zc_ref, zh_ref, pkv_ref, first_tile):
    P = tile[b * BLK:(b + 1) * BLK, :]
    if b == 0:
        return P, zc_ref[...], zh_ref[...], pkv_ref[...], first_tile
    lo = b * BLK
    return P, tile[lo - 8:lo, CW:2 * CW], tile[lo - 8:lo, 2 * CW:3 * CW], tile[lo - BLK:lo, KV0:KV0 + 2 * NKV * HD], False


def _mix_param_specs():
    return [
        pl.BlockSpec((8, CW), lambda s: (0, 0)),
        pl.BlockSpec((1, AW), lambda s: (0, 0)),
        pl.BlockSpec((1, NKV * HD), lambda s: (0, 0)),
        pl.BlockSpec((1, CW), lambda s: (0, 0)),
        pl.BlockSpec((1, AW), lambda s: (0, 0)),
        pl.BlockSpec((AW, 128), lambda s: (0, 0)),
        pl.BlockSpec((NH * BLK, 2 * BLK), lambda s: (0, 0)),
    ]


def _mix_params(cw8, qg, kg, gco, gao, bias):
    seg = np.zeros((AW, 128), np.float32)
    seg[np.arange(AW), np.arange(AW) // HD] = 1.0
    return (cw8, jnp.tile(qg, (1, NH)), jnp.tile(kg, (1, NKV)), gco, gao, jnp.asarray(seg, BF), bias)


def _mix_fwd(proj, sinks, cw8, qg, kg, gco, gao, bias, comm=()):
    def body(sink_ref, p_ref, zc_ref, zh_ref, pkv_ref, cw_ref, qg_ref, kg_ref, gco_ref, gao_ref, seg_ref, bias_ref, y_ref):
        tile = p_ref[...]
        for b in range(BPS):
            f = _mix_forward(*_block_inputs(tile, b, zc_ref, zh_ref, pkv_ref, pl.program_id(0) == 0), cw_ref[...],
                             qg_ref[...], kg_ref[...], gco_ref[...], gao_ref[...], seg_ref[...], sink_ref, bias_ref, by_head=True)
            y_ref[b * BLK:(b + 1) * BLK, :] = f["y"].astype(BF)

    return _call(
        body, (sinks, proj, proj, proj, proj, *_mix_params(cw8, qg, kg, gco, gao, bias)), name="mix_fwd", grid=(T // TILE,),
        in_specs=_mix_in_specs(lambda s: s) + _mix_param_specs(),
        out_specs=[pl.BlockSpec((TILE, D), lambda s: (s, 0))], out_shape=[SDS((T, D), BF)],
        sem=("parallel",), vmem_mib=40, comm=comm, free=tuple(range(5, 12)))


def _mix_bwd(proj, dy, sinks, cw8, qg, kg, gco, gao, bias, comm=()):
    n_steps = T // TILE

    def tile_of(s):
        return n_steps - 1 - s

    def body(sink_ref, p_ref, zc_ref, zh_ref, pkv_ref, dy_ref, cw_ref, qg_ref, kg_ref, gco_ref, gao_ref, seg_ref, bias_ref,
             dproj_ref, dcw_ref, dqg_ref, dkg_ref, dgco_ref, dgao_ref, dsink_ref, dbias_ref,
             ndcz_ref, dkc_ref, dvc_ref):
        s = pl.program_id(0)

        @pl.when(s == 0)
        def _():
            for r in (dcw_ref, dqg_ref, dkg_ref, dgco_ref, dgao_ref, dsink_ref, dbias_ref, ndcz_ref, dkc_ref, dvc_ref):
                r[...] = jnp.zeros_like(r)

        tile = p_ref[...]
        for b in reversed(range(BPS)):
            one_block(b, _block_inputs(tile, b, zc_ref, zh_ref, pkv_ref, s == n_steps - 1),
                      dy_ref[b * BLK:(b + 1) * BLK, :], sink_ref, cw_ref, qg_ref, kg_ref, gco_ref, gao_ref, seg_ref, bias_ref,
                      dproj_ref.at[b * BLK:(b + 1) * BLK, :], dcw_ref, dqg_ref, dkg_ref, dgco_ref, dgao_ref, dsink_ref,
                      dbias_ref, ndcz_ref, dkc_ref, dvc_ref)

    def one_block(b, inputs, dy, sink_ref, cw_ref, qg_ref, kg_ref, gco_ref, gao_ref, seg_ref, bias_ref,
                  dproj_ref, dcw_ref, dqg_ref, dkg_ref, dgco_ref, dgao_ref, dsink_ref, dbias_ref,
                  ndcz_ref, dkc_ref, dvc_ref):
        cw = cw_ref[...]
        qg_v, kg_v, gco_v, gao_v, seg = qg_ref[...], kg_ref[...], gco_ref[...], gao_ref[...], seg_ref[...]
        f = _mix_forward(*inputs, cw, qg_v, kg_v, gco_v, gao_v, seg, sink_ref, bias_ref)
        dyc, dgco = _rms_bwd(dy[:, 0:CW], f["y_conv"], f["rc"], gco_v)
        dya, dgao = _rms_bwd(dy[:, CW:CW + AW], f["y_attn"], f["ra"], gao_v)
        dgco_ref[...] += dgco
        dgao_ref[...] += dgao

        row = f["row"]
        dgate_b = dyc * f["cz"]
        dcz = dyc * f["gate_b"]
        dcw_ref[0:1, :] += jnp.sum(dcz * f["z2"], axis=0, keepdims=True)
        dcw_ref[1:2, :] += jnp.sum(dcz * f["z1"], axis=0, keepdims=True)
        dcw_ref[2:3, :] += jnp.sum(dcz * f["z"], axis=0, keepdims=True)
        nxt = ndcz_ref[...]
        n0 = nxt[0:1, :]
        n1 = nxt[1:2, :]
        d1 = jnp.where(row == BLK - 1, n0, pltpu.roll(dcz, BLK - 1, 0))
        d2 = jnp.where(row == BLK - 1, n1, jnp.where(row == BLK - 2, n0, pltpu.roll(dcz, BLK - 2, 0)))
        dz = cw[2:3, :] * dcz + cw[1:2, :] * d1 + cw[0:1, :] * d2
        ndcz_ref[...] = dcz[0:8, :]
        dproj_ref[:, 0:CW] = dgate_b.astype(BF)
        dproj_ref[:, CW:2 * CW] = (dz * f["hc"]).astype(BF)
        dproj_ref[:, 2 * CW:3 * CW] = (dz * f["gate_c"]).astype(BF)

        scale = f["scale"]
        lane = lax.broadcasted_iota(jnp.int32, (1, 128), 1)
        dq_cols, dk_cols, dv_cols = [], [], []
        for kv in range(NKV):
            hd = f["heads"][kv]
            dO = jnp.concatenate([dya[:, (kv * GQ + g) * HD:(kv * GQ + g + 1) * HD] for g in range(GQ)], axis=0)
            delta = jnp.sum(dO * hd["O"], axis=-1, keepdims=True)
            dOb = dO.astype(BF)
            dP = _dot(dOb, hd["vb"], 1, 1)
            dS = hd["probs"] * (dP - delta)
            dsk = hd["psink"] * delta
            for g in range(GQ):
                h = kv * GQ + g
                tot = jnp.sum(dsk[g * BLK:(g + 1) * BLK, :], axis=0, keepdims=True)
                dsink_ref[...] -= jnp.where(lane == h, tot, 0.0)
            dbias_ref[kv * GQ * BLK:(kv + 1) * GQ * BLK, :] += dS
            dSb = dS.astype(BF)
            dQ = _dot(dSb, hd["kb"], 1, 0)
            dKb = _dot(dSb, hd["Q"], 0, 0)
            dVb = _dot(hd["probs"].astype(BF), dOb, 0, 0)
            dk_cols.append(dKb[BLK:, :] + dkc_ref[:, kv * HD:(kv + 1) * HD])
            dv_cols.append(dVb[BLK:, :] + dvc_ref[:, kv * HD:(kv + 1) * HD])
            dkc_ref[:, kv * HD:(kv + 1) * HD] = dKb[:BLK, :]
            dvc_ref[:, kv * HD:(kv + 1) * HD] = dVb[:BLK, :]
            dq_cols += [dQ[g * BLK:(g + 1) * BLK, :] for g in range(GQ)]
        dq_raw, dqg_t = _head_norm_bwd(jnp.concatenate(dq_cols, axis=1) * scale, f["q_raw"], f["rq"], qg_v, seg)
        dk_raw, dkg_t = _head_norm_bwd(jnp.concatenate(dk_cols, axis=1), f["k_raw"][BLK:, :], f["rk"][BLK:, :], kg_v, seg)
        dqg_ref[...] += _fold_heads(dqg_t)
        dkg_ref[...] += _fold_heads(dkg_t)
        dproj_ref[:, 3 * CW:INW] = jnp.concatenate([dq_raw, dk_raw] + dv_cols, axis=1).astype(BF)

    small = lambda r, c: pl.BlockSpec((r, c), lambda s: (0, 0))
    return _call(
        body, (sinks, proj, proj, proj, proj, dy, *_mix_params(cw8, qg, kg, gco, gao, bias)), name="mix_bwd", grid=(n_steps,),
        in_specs=_mix_in_specs(tile_of) + [pl.BlockSpec((TILE, D), lambda s: (tile_of(s), 0))] + _mix_param_specs(),
        out_specs=[pl.BlockSpec((TILE, INW), lambda s: (tile_of(s), 0)), small(8, CW), small(1, HD), small(1, HD),
                   small(1, CW), small(1, AW), small(1, 128), small(NH * BLK, 2 * BLK)],
        out_shape=[SDS((T, INW), BF), SDS((8, CW), F32), SDS((1, HD), F32), SDS((1, HD), F32), SDS((1, CW), F32),
                   SDS((1, AW), F32), SDS((1, 128), F32), SDS((NH * BLK, 2 * BLK), F32)],
        scratch_shapes=[pltpu.VMEM((8, CW), F32), pltpu.VMEM((BLK, NKV * HD), F32), pltpu.VMEM((BLK, NKV * HD), F32)],
        sem=("arbitrary",), vmem_mib=56, comm=comm, free=(1, 2, 3, 4) + tuple(range(6, 13)))


FT = 256
NFT = DFF // FT
RC = 128
NCH = T // RC
LEAD = 16


def _rows8(x):
    return jnp.sum(x.reshape(x.shape[0] // 8, 8, x.shape[1]), axis=0)


def _ffn_act_specs():
    return [
        pl.BlockSpec((T, FT), lambda j: (0, j)), pl.BlockSpec((T, FT), lambda j: (0, NFT + j)),
        pl.BlockSpec((8, FT), lambda j: (0, j)), pl.BlockSpec((8, FT), lambda j: (0, NFT + j)),
        pl.BlockSpec((1, FT), lambda j: (0, j)), pl.BlockSpec((1, FT), lambda j: (0, NFT + j)),
    ]


def _conv_rows(win, w, b, n):
    win = win.astype(F32)
    u = win[LEAD:LEAD + n]
    u1 = pltpu.roll(win, 1, 0)[LEAD:LEAD + n]
    u2 = pltpu.roll(win, 2, 0)[LEAD:LEAD + n]
    return u2, u1, u, w[0:1, :] * u2 + w[1:2, :] * u1 + w[2:3, :] * u + b


def _ffn_act(up, fw8, fb, comm=()):
    def body(ug_ref, uv_ref, wg_ref, wv_ref, bg_ref, bv_ref, a_ref):
        wg, wv, bg, bv = wg_ref[...], wv_ref[...], bg_ref[...], bv_ref[...]

        def chunk(win_g, win_v):
            gp = _conv_rows(win_g, wg, bg, RC)[3]
            vp = _conv_rows(win_v, wv, bv, RC)[3]
            return (gp * jax.nn.sigmoid(gp) * vp).astype(BF)

        zero = jnp.zeros((LEAD, FT), BF)
        a_ref[0:RC, :] = chunk(jnp.concatenate([zero, ug_ref[0:RC, :]], axis=0),
                               jnp.concatenate([zero, uv_ref[0:RC, :]], axis=0))

        def step(i, carry):
            r0 = pl.multiple_of(i * RC, RC)
            win = pl.ds(r0 - LEAD, RC + LEAD)
            a_ref[pl.ds(r0, RC), :] = chunk(ug_ref[win, :], uv_ref[win, :])
            return carry

        lax.fori_loop(1, NCH, step, 0)

    return _call(
        body, (up, up, fw8, fw8, fb, fb), name="ffn_act", grid=(NFT,), in_specs=_ffn_act_specs(),
        out_specs=[pl.BlockSpec((T, FT), lambda j: (0, j))], out_shape=[SDS((T, DFF), BF)],
        sem=("parallel",), vmem_mib=40, comm=comm, free=(2, 3, 4, 5))


def _ffn_act_bwd(up, da, fw8, fb, comm=()):
    ext = RC + LEAD

    def body(ug_ref, uv_ref, wg_ref, wv_ref, bg_ref, bv_ref, da_ref,
             dug_ref, duv_ref, dwg_ref, dwv_ref, dbg_ref, dbv_ref):
        wg, wv, bg, bv = wg_ref[...], wv_ref[...], bg_ref[...], bv_ref[...]

        def chunk(win_g, win_v, da_e):
            g2, g1, g0, gp = _conv_rows(win_g, wg, bg, ext)
            v2, v1, v0, vp = _conv_rows(win_v, wv, bv, ext)
            da_e = da_e.astype(F32)
            sig = jax.nn.sigmoid(gp)
            dvp = da_e * (gp * sig)
            dgp = da_e * vp * (sig * (1.0 + gp * (1.0 - sig)))

            def back(dp, w):
                return (w[2:3, :] * dp[0:RC] + w[1:2, :] * pltpu.roll(dp, ext - 1, 0)[0:RC]
                        + w[0:1, :] * pltpu.roll(dp, ext - 2, 0)[0:RC]).astype(BF)

            def sums(dp, u2, u1, u0):
                d = dp[0:RC]
                return [_rows8(d), _rows8(d * u2[0:RC]), _rows8(d * u1[0:RC]), _rows8(d * u0[0:RC])]

            return back(dgp, wg), back(dvp, wv), sums(dgp, g2, g1, g0) + sums(dvp, v2, v1, v0)

        zero = jnp.zeros((LEAD, FT), BF)
        dug, duv, acc = chunk(jnp.concatenate([zero, ug_ref[0:ext, :]], axis=0),
                              jnp.concatenate([zero, uv_ref[0:ext, :]], axis=0), da_ref[0:ext, :])
        dug_ref[0:RC, :] = dug
        duv_ref[0:RC, :] = duv

        def step(i, acc):
            r0 = pl.multiple_of(i * RC, RC)
            win = pl.ds(r0 - LEAD, ext + LEAD)
            dug, duv, part = chunk(ug_ref[win, :], uv_ref[win, :], da_ref[pl.ds(r0, ext), :])
            dug_ref[pl.ds(r0, RC), :] = dug
            duv_ref[pl.ds(r0, RC), :] = duv
            return [a + p for a, p in zip(acc, part)]

        acc = lax.fori_loop(1, NCH - 1, step, acc)
        r0 = T - RC
        tail = lambda ref, lo: jnp.concatenate([ref[lo:T, :], zero], axis=0)
        dug, duv, part = chunk(tail(ug_ref, r0 - LEAD), tail(uv_ref, r0 - LEAD), tail(da_ref, r0))
        dug_ref[r0:T, :] = dug
        duv_ref[r0:T, :] = duv
        tot = [jnp.sum(a + p, axis=0, keepdims=True) for a, p in zip(acc, part)]
        for k, (dw_ref, db_ref) in enumerate(((dwg_ref, dbg_ref), (dwv_ref, dbv_ref))):
            db_ref[...] = tot[4 * k]
            dw_ref[...] = jnp.zeros_like(dw_ref)
            for r in range(3):
                dw_ref[r:r + 1, :] = tot[4 * k + 1 + r]

    col = lambda r: pl.BlockSpec((r, FT), lambda j: (0, j))
    return _call(
        body, (up, up, fw8, fw8, fb, fb, da), name="ffn_act_bwd", grid=(NFT,),
        in_specs=_ffn_act_specs() + [pl.BlockSpec((T, FT), lambda j: (0, j))],
        out_specs=[col(T), col(T), col(8), col(8), col(1), col(1)],
        out_shape=[SDS((T, DFF), BF), SDS((T, DFF), BF), SDS((8, DFF), F32), SDS((8, DFF), F32),
                   SDS((1, DFF), F32), SDS((1, DFF), F32)],
        sem=("parallel",), vmem_mib=40, comm=comm, free=(0, 1, 2, 3, 4, 5))


def _ffn_down_bwd(dh2b, w_down, comm=()):
    tm = TM

    def body(d_ref, w_ref, o_ref):
        o_ref[...] = _dot(d_ref[...], w_ref[...], 1, 1).astype(BF)

    return _call(
        body, (dh2b, w_down), name="ffn_down_bwd", grid=(T // tm,),
        in_specs=[pl.BlockSpec((tm, D), lambda i: (i, 0)), _resident((DFF, D))],
        out_specs=[pl.BlockSpec((tm, DFF), lambda i: (i, 0))], out_shape=[SDS((T, DFF), BF)],
        sem=("parallel",), vmem_mib=40, comm=comm, free=(0, 1))


def _norm_matmul_bwd(name, a_list, w_t, k_offsets, xin, g, dres, want_bf16, comm=(), slot=None):
    tm = TM
    ks = [a.shape[1] for a in a_list]
    n_a = len(a_list)
    n_pre = 0 if slot is None else 1

    def body(*refs):
        refs = refs[n_pre:]
        a_refs = refs[:n_a]
        w_ref, x_ref, g_ref, r_ref = refs[n_a:n_a + 4]
        outs = refs[n_a + 4:]
        dx_ref, dg_ref = outs[0], (outs[-1] if slot is None else outs[-1].at[0])

        @pl.when(pl.program_id(0) == 0)
        def _():
            dg_ref[...] = jnp.zeros_like(dg_ref)

        du = _dot(a_refs[0][...], w_ref[k_offsets[0]:k_offsets[0] + ks[0], :], 1, 0)
        for k in range(1, n_a):
            du = du + _dot(a_refs[k][...], w_ref[k_offsets[k]:k_offsets[k] + ks[k], :], 1, 0)
        x = x_ref[...]
        r = lax.rsqrt(jnp.mean(x * x, axis=-1, keepdims=True) + EPS)
        dx, dg = _rms_bwd(du, x, r, g_ref[...])
        dx = r_ref[...] + dx
        dx_ref[...] = dx
        if want_bf16:
            outs[1][...] = dx.astype(BF)
        dg_ref[...] += dg

    tile = lambda c: pl.BlockSpec((tm, c), lambda i, *_: (i, 0))
    if slot is None:
        dg_spec, dg_shape = pl.BlockSpec((1, D), lambda i: (0, 0)), SDS((1, D), F32)
    else:
        dg_spec, dg_shape = pl.BlockSpec((1, 1, D), lambda i, slot_ref: (slot_ref[0], 0, 0)), SDS((N_DEV, 1, D), F32)
    out_specs = [tile(D)] + ([tile(D)] if want_bf16 else []) + [dg_spec]
    out_shape = [SDS((T, D), F32)] + ([SDS((T, D), BF)] if want_bf16 else []) + [dg_shape]
    return _call(
        body, (*a_list, w_t, xin, g, dres), name=name, grid=(T // tm,), prefetch=() if slot is None else (slot,),
        in_specs=[tile(k) for k in ks] + [_resident(w_t.shape), tile(D),
                                           pl.BlockSpec((1, D), lambda i, *_: (0, 0)), tile(D)],
        out_specs=out_specs, out_shape=out_shape, sem=("arbitrary",), vmem_mib=56, comm=comm, free=tuple(range(n_a + 4)))


def _out_bwd(dh1b, w_out, comm=()):
    tm = TM

    def body(d_ref, w_ref, o_ref):
        o_ref[...] = _dot(d_ref[...], w_ref[...], 1, 1)

    return _call(
        body, (dh1b, w_out), name="out_bwd", grid=(T // tm,),
        in_specs=[pl.BlockSpec((tm, D), lambda i: (i, 0)), _resident((D, D))],
        out_specs=[pl.BlockSpec((tm, D), lambda i: (i, 0))], out_shape=[SDS((T, D), F32)],
        sem=("parallel",), vmem_mib=32, comm=comm, free=(0, 1))


def _wgrad(name, a_list, b, old_a, comm=()):
    m_k = a_list[0].shape[1]
    tm = max(t for t in range(128, m_k // 2 + 1, 128) if m_k % t == 0)
    steps = [a.shape[1] // tm for a in a_list]
    starts = [sum(steps[:k]) for k in range(len(a_list))]
    n_a = len(a_list)

    def body(*refs):
        a_refs, b_ref, o_ref = refs[:n_a], refs[n_a], refs[n_a + 1]
        i = pl.program_id(0)
        for k in range(n_a):
            @pl.when((i >= starts[k]) & (i < starts[k] + steps[k]))
            def _(k=k):
                o_ref[...] = _dot(a_refs[k][...], b_ref[...], 0, 0).astype(BF)

    def a_spec(k):
        return pl.BlockSpec((T, tm), lambda i: (0, jnp.clip(i - starts[k], 0, steps[k] - 1)))

    m_total = tm * sum(steps)
    return _call(
        body, (*a_list, b), name=name, grid=(sum(steps),),
        in_specs=[a_spec(k) for k in range(n_a)] + [_resident((T, D))],
        out_specs=[pl.BlockSpec((tm, D), lambda i: (i, 0))], out_shape=[SDS((m_total, D), BF)],
        sem=("parallel",), vmem_mib=40, comm=comm, free=tuple(range(n_a)) if old_a else (n_a,))


def _chip_sum(name, gbf, from_sib, core, chip):
    h = gbf.shape[1]
    th = h // 2

    def body(core_ref, chip_ref, g_ref, s_ref, pbf_ref, own_ref):
        p = g_ref[0].astype(F32) + s_ref[0].astype(F32)
        pbf_ref[0] = p.astype(BF)

        @pl.when(pl.program_id(1) == chip_ref[0])
        def _():
            own_ref[...] = p

    grid_spec = pltpu.PrefetchScalarGridSpec(
        num_scalar_prefetch=2, grid=(h // th, N_CHIPS),
        in_specs=[pl.BlockSpec((1, th, D), lambda t, jj, core_ref, chip_ref: (2 * jj + core_ref[0], t, 0)),
                  pl.BlockSpec((1, th, D), lambda t, jj, core_ref, chip_ref: (jj, t, 0))],
        out_specs=[pl.BlockSpec((1, th, D), lambda t, jj, core_ref, chip_ref: (jj, t, 0)),
                   pl.BlockSpec((th, D), lambda t, jj, core_ref, chip_ref: (t, 0))],
    )
    return _pcall(
        body, name=name, grid_spec=grid_spec, out_shape=_in_hbm([SDS((N_CHIPS, h, D), BF), SDS((h, D), F32)]),
        compiler_params=_params(("arbitrary", "arbitrary"), 32),
    )(core, chip, *_from_hbm(gbf, from_sib))


def _final_sum(name, own, from_chips, core, comm=()):
    h = own.shape[0]

    def body(core_ref, o_ref, r_ref, f_ref):
        f_ref[0] = ((o_ref[...] + r_ref[0].astype(F32)) + r_ref[1].astype(F32)) + r_ref[2].astype(F32)

    return _call(
        body, (own, from_chips), name=name, grid=(1,), prefetch=(core,),
        in_specs=[pl.BlockSpec((h, D), lambda i, core_ref: (0, 0)), pl.BlockSpec((3, h, D), lambda i, core_ref: (0, 0, 0))],
        out_specs=[pl.BlockSpec((1, h, D), lambda i, core_ref: (core_ref[0], 0, 0))], out_shape=[SDS((2, h, D), F32)],
        sem=("arbitrary",), vmem_mib=40, comm=comm)


def _adam_math(w, g, m, v):
    nm = ADAM_B1 * m + (1.0 - ADAM_B1) * g
    nv = ADAM_B2 * v + (1.0 - ADAM_B2) * (g * g)
    m_hat = nm / (1.0 - ADAM_B1 ** ADAM_STEP)
    v_hat = nv / (1.0 - ADAM_B2 ** ADAM_STEP)
    return -ADAM_LR * (m_hat / (jnp.sqrt(v_hat) + ADAM_EPS) + ADAM_WD * w), nm, nv


def _adamw(name, w, g, m, v, tr, copy_g=False, stage=True):
    rows, cols = w.shape

    def body(w_ref, g_ref, m_ref, v_ref, *outs):
        g_val = g_ref[...]
        if copy_g:
            outs[0][...] = g_val
        d_ref, nm_ref, nv_ref = outs[-3:]
        d_ref[...], nm_ref[...], nv_ref[...] = _adam_math(w_ref[...], g_val, m_ref[...], v_ref[...])

    spec = pl.BlockSpec((tr, cols), lambda i: (i, 0))
    n_out = 4 if copy_g else 3
    return _call(body, (w, g, m, v), name=name, grid=(rows // tr,), in_specs=[spec] * 4, out_specs=[spec] * n_out,
                 out_shape=[SDS((rows, cols), F32)] * n_out, sem=("parallel",), vmem_mib=32,
                 free=(0, 2, 3) if stage else ())


C_G1, C_G2, C_GCO, C_GAO, C_DCW, C_DQG, C_DKG, C_SINK, C_SQ = 0, 1024, 2048, 2560, 3072, 4608, 4736, 4864, 5632
P_W = C_SQ + 128


def _pack_small(me, dfwg, dfwv, dfbg, dfbv, dg2, dgco, dgao, dcw8, dqg, dkg, dsink, sq):
    def body(me_ref, dfwg_r, dfwv_r, dfbg_r, dfbv_r, dg2_r, dgco_r, dgao_r, dcw_r, dqg_r, dkg_r, dsink_r, sq_r, o):
        o[...] = jnp.zeros_like(o)
        o[0, :, 0:DFF] = dfwg_r[...]
        o[0, :, DFF:2 * DFF] = dfwv_r[...]
        o[0, 3:4, 0:DFF] = dfbg_r[...]
        o[0, 3:4, DFF:2 * DFF] = dfbv_r[...]
        o[0, 4:5, C_G2:C_G2 + D] = dg2_r[...]
        o[0, 4:5, C_GCO:C_GCO + CW] = dgco_r[...]
        o[0, 4:5, C_GAO:C_GAO + AW] = dgao_r[...]
        for r in range(3):
            o[0, 4:5, C_DCW + r * CW:C_DCW + (r + 1) * CW] = dcw_r[r:r + 1, :]
        o[0, 4:5, C_DQG:C_DQG + HD] = dqg_r[...]
        o[0, 4:5, C_DKG:C_DKG + HD] = dkg_r[...]
        o[0, 4:5, C_SINK:C_SINK + 128] = dsink_r[...]
        o[0, :, C_SQ:C_SQ + 128] = sq_r[...]

    ins = (dfwg, dfwv, dfbg, dfbv, dg2, dgco, dgao, dcw8, dqg, dkg, dsink, sq)
    return _call(body, ins, name="pack_small", grid=(1,), prefetch=(me,),
                 in_specs=[pl.BlockSpec(a.shape, lambda i, me_ref: (0, 0)) for a in ins],
                 out_specs=[pl.BlockSpec((1, 8, P_W), lambda i, me_ref: (me_ref[0], 0, 0))],
                 out_shape=[SDS((N_DEV, 8, P_W), F32)], sem=("arbitrary",))[0]


N_SMALL = 11


def _small_adam(chip, p_all, g1_all, tbl_all, ws, ms, vs):
    fw_cols = 2 * DFF // N_CHIPS
    cw_cols = CW // N_CHIPS

    def body(chip_ref, p_ref, fw_ref, cw0_ref, cw1_ref, cw2_ref, g1_ref, tbl_ref, *refs):
        w_r, m_r, v_r = refs[0:N_SMALL], refs[N_SMALL:2 * N_SMALL], refs[2 * N_SMALL:3 * N_SMALL]
        outs = refs[3 * N_SMALL:]
        g_o, d_o, nm_o, nv_o = (outs[k * N_SMALL:(k + 1) * N_SMALL] for k in range(4))
        loss_o = outs[4 * N_SMALL]

        def total(ref):
            s = ref[0]
            for k in range(1, N_DEV):
                s = s + ref[k]
            return s

        S = total(p_ref)
        fw = total(fw_ref)
        cws = [total(r) for r in (cw0_ref, cw1_ref, cw2_ref)]

        def step(i, g, at):
            d, nm, nv = _adam_math(w_r[i][at], g, m_r[i][at], v_r[i][at])
            g_o[i][at], d_o[i][at], nm_o[i][at], nv_o[i][at] = g, d, nm, nv

        everything = (slice(None), slice(None))
        step(0, total(g1_ref), everything)
        for r in range(3):
            step(1, cws[r][4:5, :], (r, slice(None), slice(None)))
        step(2, S[4:5, C_DQG:C_DQG + HD], everything)
        step(3, S[4:5, C_DKG:C_DKG + HD], everything)
        step(4, total(tbl_ref), everything)
        step(5, S[4:5, C_SINK:C_SINK + NH], everything)
        step(6, S[4:5, C_GCO:C_GCO + CW], everything)
        step(7, S[4:5, C_GAO:C_GAO + AW], everything)
        step(8, S[4:5, C_G2:C_G2 + D], everything)
        for r in range(3):
            step(9, fw[r:r + 1, :], (r, slice(None), slice(None)))
        step(10, S[3:4, 0:2 * DFF], everything)
        sq = S[:, C_SQ:C_SQ + 128]
        loss_o[...] = jnp.sum(jnp.sum(sq, axis=1, keepdims=True), axis=0, keepdims=True) * (0.5 / D)

    def full(a):
        n = len(a.shape)
        return pl.BlockSpec(a.shape, lambda i, chip_ref: (0,) * n)

    params = [*ws, *ms, *vs]
    out = _call(
        body, (p_all, p_all, p_all, p_all, p_all, g1_all, tbl_all, *params), name="small_adam", grid=(1,), prefetch=(chip,),
        in_specs=[full(p_all),
                  pl.BlockSpec((N_DEV, 8, fw_cols), lambda i, chip_ref: (0, 0, chip_ref[0])),
                  *[pl.BlockSpec((N_DEV, 8, cw_cols), lambda i, chip_ref, r=r: (0, 0, (C_DCW + r * CW) // cw_cols + chip_ref[0]))
                    for r in range(3)],
                  full(g1_all), full(tbl_all), *[full(a) for a in params]],
        out_specs=[full(a) for a in ws] * 4 + [pl.BlockSpec((1, 1), lambda i, chip_ref: (0, 0))],
        out_shape=[SDS(a.shape, F32) for a in ws] * 4 + [SDS((1, 1), F32)], sem=("arbitrary",), vmem_mib=32)
    return out[0:N_SMALL], out[N_SMALL:2 * N_SMALL], out[2 * N_SMALL:3 * N_SMALL], out[3 * N_SMALL:4 * N_SMALL], out[4 * N_SMALL]


PLACE_STEPS = 4


def _place_specs(shards):
    rows = [s.shape[0] // PLACE_STEPS for s in shards]
    return ([pl.BlockSpec((r, D), lambda i, chip_ref: (i, 0)) for r in rows],
            [pl.BlockSpec((r, D), lambda i, chip_ref: (chip_ref[0] * PLACE_STEPS + i, 0)) for r in rows],
            [SDS((N_CHIPS * s.shape[0], D), BF) for s in shards])


def _place_first(chip, shard, conv_w, ffn_conv_w):
    def body(chip_ref, a, s0, s1, o, t0, t1):
        o[...] = a[...].astype(BF)

        @pl.when(pl.program_id(0) == 0)
        def _():
            for s, t in ((s0, t0), (s1, t1)):
                t[...] = jnp.zeros_like(t)
                t[0, 0:3, :] = s[...]

    ins, outs, shapes = _place_specs([shard])
    taps = (conv_w, ffn_conv_w)
    return _call(
        body, (shard, conv_w, ffn_conv_w), name="place_first", grid=(PLACE_STEPS,), prefetch=(chip,),
        in_specs=ins + [pl.BlockSpec(s.shape, lambda i, chip_ref: (0, 0)) for s in taps],
        out_specs=outs + [pl.BlockSpec((1, 8, s.shape[1]), lambda i, chip_ref: (chip_ref[0], 0, 0)) for s in taps],
        out_shape=shapes + [SDS((N_CHIPS, 8, s.shape[1]), F32) for s in taps],
        sem=("arbitrary",), vmem_mib=32, free=(0, 1, 2))


def _place_rest(chip, shards, table, bucket, comm):
    n = len(shards)

    def body(chip_ref, *refs):
        a, (tab_ref, bk_ref), o, bias_ref = refs[:n], refs[n:n + 2], refs[n + 2:2 * n + 2], refs[2 * n + 2]
        for src, dst in zip(a, o):
            dst[...] = src[...].astype(BF)

        @pl.when(pl.program_id(0) == 0)
        def _():
            bk = bk_ref[...]
            eq = [bk == b for b in range(NBUCKET)]
            for h in range(NH):
                acc = jnp.zeros((BLK, 2 * BLK), F32)
                for b in range(NBUCKET):
                    acc = jnp.where(eq[b], tab_ref[h, b], acc)
                bias_ref[h * BLK:(h + 1) * BLK, :] = acc

    ins, outs, shapes = _place_specs(shards)
    return _call(
        body, (*shards, table, bucket), name="place_rest", grid=(PLACE_STEPS,), prefetch=(chip,),
        in_specs=ins + [pl.BlockSpec(memory_space=pltpu.SMEM), pl.BlockSpec(bucket.shape, lambda i, chip_ref: (0, 0))],
        out_specs=outs + [pl.BlockSpec((NH * BLK, 2 * BLK), lambda i, chip_ref: (0, 0))],
        out_shape=shapes + [SDS((NH * BLK, 2 * BLK), F32)],
        sem=("arbitrary",), vmem_mib=32, comm=comm, free=tuple(range(n + 2)))


def kernel(x, norm_mix_g, w_in, conv_w, q_norm_g, k_norm_g, rel_bias_table, sinks, out_norm_conv_g, out_norm_attn_g, w_out, norm_ffn_g, w_up, ffn_conv_w, ffn_conv_b, w_down, loss_target, m_norm_mix_g, m_w_in, m_conv_w, m_q_norm_g, m_k_norm_g, m_rel_bias_table, m_sinks, m_out_norm_conv_g, m_out_norm_attn_g, m_w_out, m_norm_ffn_g, m_w_up, m_ffn_conv_w, m_ffn_conv_b, m_w_down, v_norm_mix_g, v_w_in, v_conv_w, v_q_norm_g, v_k_norm_g, v_rel_bias_table, v_sinks, v_out_norm_conv_g, v_out_norm_attn_g, v_w_out, v_norm_ffn_g, v_w_up, v_ffn_conv_w, v_ffn_conv_b, v_w_down):
    as_arg = lambda i: jnp.reshape(i, (1,)).astype(jnp.int32)
    chip = as_arg(2 * lax.axis_index("x") + lax.axis_index("y"))
    core = as_arg(lax.axis_index("c"))
    me = 2 * chip + core
    xs, tgt = x[0], loss_target[0]
    qg, kg, gco, gao, g1, g2, fb = q_norm_g, k_norm_g, out_norm_conv_g, out_norm_attn_g, norm_mix_g, norm_ffn_g, ffn_conv_b
    pieces = lambda g: g.reshape(N_DEV, g.shape[0] // N_DEV, D)
    whole = lambda f: f.reshape(2 * f.shape[1], D)

    bucket = jnp.asarray(_bucket_table())
    p_in, p_cw, p_fw = _place_first(chip, w_in[0].T, conv_w[0], ffn_conv_w[0])
    p_out, p_up, p_down, bias, w_int, cw_all, fw_all = _place_rest(
        chip, [w_out[0], w_up[0].T, w_down[0]], rel_bias_table.T, bucket,
        comm=[_t_gather(p_in), _t_small_weights(p_cw), _t_small_weights(p_fw)])
    cw8 = jnp.transpose(cw_all, (1, 0, 2)).reshape(8, CW)
    fw8 = jnp.transpose(fw_all, (1, 0, 2)).reshape(8, 2 * DFF)

    early = 3 / 11
    proj, u1, w_out_f, p_up = _inproj(xs, g1, w_int, comm=[_t_gather(p_out), _t_gather(p_up, (0, early))])
    y, w_upt = _mix_fwd(proj, sinks, cw8, qg, kg, gco, gao, bias, comm=[_t_gather(p_up, (early, 1))])
    h1, u2 = _outproj(y, w_out_f, xs, g2)
    up, = _ffn_up(u2, w_upt)
    a, w_down_f = _ffn_act(up, fw8, fb, comm=[_t_gather(p_down)])
    dh2, dh2b, sq = _ffn_down(a, w_down_f, h1, tgt)

    gdbf, = _wgrad("wgrad_down", [a], dh2b, True)
    da, sib_down = _ffn_down_bwd(dh2b, w_down_f, comm=[_t_sibling(pieces(gdbf))])
    pbf_down, own_down = _chip_sum("chip_sum_w_down", pieces(gdbf), sib_down, core, chip)
    dug, duv, dfwg, dfwv, dfbg, dfbv, chips_down = _ffn_act_bwd(up, da, fw8, fb, comm=[_t_chips(pbf_down)])
    fin_down, = _final_sum("final_sum_w_down", own_down, chips_down, core)
    gubf, = _wgrad("wgrad_up", [dug, duv], u2, False)
    dh1, dh1b, dg2, sib_up, fin_down = _norm_matmul_bwd(
        "ffn_up_bwd", [dug, duv], w_upt, [0, DFF], h1, g2, dh2, True, comm=[_t_sibling(pieces(gubf)), _t_swap(fin_down)])
    pbf_up, own_up = _chip_sum("chip_sum_w_up", pieces(gubf), sib_up, core, chip)
    gobf, = _wgrad("wgrad_out", [y], dh1b, True)
    dy, sib_out = _out_bwd(dh1b, w_out_f, comm=[_t_sibling(pieces(gobf))])
    pbf_out, own_out = _chip_sum("chip_sum_w_out", pieces(gobf), sib_out, core, chip)
    dproj, dcw8, dqg, dkg, dgco, dgao, dsink, dbias, chips_up, chips_out = _mix_bwd(
        proj, dy, sinks, cw8, qg, kg, gco, gao, bias, comm=[_t_chips(pbf_up), _t_chips(pbf_out)])
    fin_up, = _final_sum("final_sum_w_up", own_up, chips_up, core)
    tbl_all = _band_bias_bwd(dbias, bucket, me)
    p_all = _pack_small(me, dfwg, dfwv, dfbg, dfbv, dg2, dgco, dgao, dcw8, dqg, dkg, dsink, sq)
    gibf, fin_up, p_all, tbl_all = _wgrad(
        "wgrad_in", [dproj], u1, False, comm=[_t_swap(fin_up), _t_allgather(p_all), _t_allgather(tbl_all)])
    fin_out, sib_in = _final_sum("final_sum_w_out", own_out, chips_out, core, comm=[_t_sibling(pieces(gibf))])
    pbf_in, own_in = _chip_sum("chip_sum_w_in", pieces(gibf), sib_in, core, chip)
    dx, g1_all, chips_in, fin_out = _norm_matmul_bwd(
        "in_bwd", [dproj], w_int, [0], xs, g1, dh1, False, comm=[_t_chips(pbf_in), _t_swap(fin_out)], slot=me)
    fin_in, = _final_sum("final_sum_w_in", own_in, chips_in, core)
    g1_all, fin_in = _comm_call("gather_last", [_t_allgather(g1_all), _t_swap(fin_in)])

    g_w_out, g_w_up, g_w_down = whole(fin_out), whole(fin_up).T, whole(fin_down)
    g_w_down, d_down, nm_down, nv_down = _adamw("adamw_w_down", w_down[0], g_w_down, m_w_down[0], v_w_down[0], 352, True)
    d_up, nm_up, nv_up = _adamw("adamw_w_up", w_up[0], g_w_up, m_w_up[0], v_w_up[0], 256, stage=False)
    g_w_out, d_out, nm_out, nv_out = _adamw("adamw_w_out", w_out[0], g_w_out, m_w_out[0], v_w_out[0], 256, True)
    g_w_in, d_in, nm_in, nv_in = [a.T for a in _adamw(
        "adamw_w_in", w_in[0].T, whole(fin_in), m_w_in[0].T, v_w_in[0].T, INW // N_CHIPS // 3, True)]
    taps = lambda a: jnp.transpose(a, (1, 0, 2))
    sw = [norm_mix_g, taps(conv_w), q_norm_g, k_norm_g, rel_bias_table.T, sinks, out_norm_conv_g, out_norm_attn_g,
          norm_ffn_g, taps(ffn_conv_w), ffn_conv_b]
    smm = [m_norm_mix_g, taps(m_conv_w), m_q_norm_g, m_k_norm_g, m_rel_bias_table.T, m_sinks, m_out_norm_conv_g,
           m_out_norm_attn_g, m_norm_ffn_g, taps(m_ffn_conv_w), m_ffn_conv_b]
    smv = [v_norm_mix_g, taps(v_conv_w), v_q_norm_g, v_k_norm_g, v_rel_bias_table.T, v_sinks, v_out_norm_conv_g,
           v_out_norm_attn_g, v_norm_ffn_g, taps(v_ffn_conv_w), v_ffn_conv_b]
    *small_out, loss = _small_adam(chip, p_all, g1_all, tbl_all, sw, smm, smv)
    sg, sd, snm, snv = [list(r) for r in small_out]
    for r in (sg, sd, snm, snv):
        r[1], r[4], r[9] = taps(r[1]), r[4].T, taps(r[9])

    def order(s, b_in, b_out, b_up, b_down):
        return (s[0], b_in[None], s[1], s[2], s[3], s[4], s[5], s[6], s[7], b_out[None], s[8], b_up[None],
                s[9], s[10], b_down[None])

    return (loss.reshape(()), dx[None],
            *order(sg, g_w_in, g_w_out, g_w_up, g_w_down),
            *order(sd, d_in, d_out, d_up, d_down),
            *order(snm, nm_in, nm_out, nm_up, nm_down),
            *order(snv, nv_in, nv_out, nv_up, nv_down))
```

```python
import functools
import math

import numpy as np

import jax
import jax.numpy as jnp
from jax import lax
from jax.experimental import pallas as pl
from jax.experimental.pallas import tpu as pltpu

F32 = jnp.float32
BF = jnp.bfloat16
SDS = jax.ShapeDtypeStruct

T = 2048
D = 1024
CW = 512
AW = 512
HD = 64
NH = 8
NKV = 2
GQ = 4
INW = 2304
DFF = 2816
BLK = 128
NB = T // BLK
NBUCKET = 32
EPS = 1e-6
NEG_INF = -1e30
N_CHIPS = 4
N_DEV = 8

ADAM_LR = 0.001
ADAM_B1 = 0.9
ADAM_B2 = 0.999
ADAM_EPS = 1e-08
ADAM_WD = 0.01
ADAM_STEP = 10

TM = 512
MIB = 1024 * 1024
MESH = pl.DeviceIdType.MESH
ANY = pl.BlockSpec(memory_space=pl.ANY)

_pcall = pl.pallas_call


def _params(sem=None, vmem_mib=None):
    kw = {}
    if sem is not None:
        kw["dimension_semantics"] = sem
    if vmem_mib is not None:
        kw["vmem_limit_bytes"] = vmem_mib * MIB
    return pltpu.CompilerParams(**kw)


def _resident(shape):
    return pl.BlockSpec(shape, lambda *_: (0,) * len(shape), pipeline_mode=pl.Buffered(1))


def _dot(a, b, ca, cb):
    return lax.dot_general(a, b, (((ca,), (cb,)), ((), ())), preferred_element_type=F32)


def _rms_bwd(dy, x, r, g):
    dg = jnp.sum(dy * (x * r), axis=0, keepdims=True)
    dgx = dy * g
    dx = r * dgx - x * (r * r * r) * jnp.mean(x * dgx, axis=-1, keepdims=True)
    return dx, dg


def _where():
    x, y, c = lax.axis_index("x"), lax.axis_index("y"), lax.axis_index("c")
    return x, y, c, [(1 - x, y), (x, 1 - y), (1 - x, 1 - y)]


def _rcopy(src, dst, ssem, rsem, dev):
    return pltpu.make_async_remote_copy(src_ref=src, dst_ref=dst, send_sem=ssem, recv_sem=rsem, device_id=dev,
                                        device_id_type=MESH)


class _Task:
    def __init__(self, ins, outs, alias, n_sem, start, finish, middle=None):
        self.ins, self.outs, self.alias, self.n_sem, self.start, self.finish = ins, outs, alias, n_sem, start, finish
        self.middle = middle if middle is not None else (lambda *args: None)


ROWS16 = 16


def _t_gather(placed, part=(0, 1)):
    R = placed.shape[0] // N_CHIPS
    q = R // 4
    lo, hi = (round(f * (q // ROWS16)) * ROWS16 for f in part)

    def quarter(chip_index, core, k):
        return pl.ds(pl.multiple_of(chip_index * R + core * 2 * q + k * q + lo, ROWS16), hi - lo)

    def places():
        x, y, c, _ = _where()
        return c, 2 * x + y, 2 * (1 - x) + y, 2 * x + (1 - y), 2 * (1 - x) + (1 - y), (1 - x, y, c), (x, 1 - y, c), (x, y, 1 - c)

    def copy(buf, k, chip_index, core, quart, ss, rs, b, dev):
        window = buf.at[quarter(chip_index, core, quart)]
        return _rcopy(window, window, ss.at[b + k], rs.at[b + k], dev)

    def start(cin, cout, ss, rs, b):
        c, me, _, _, _, x_nbr, y_nbr, _ = places()
        for k, (quart, dev) in enumerate(((0, x_nbr), (1, y_nbr), (1, x_nbr), (0, y_nbr))):
            copy(cout[0], k, me, c, quart, ss, rs, b, dev).start()

    def middle(cin, cout, ss, rs, b):
        c, _, xc, yc, _, x_nbr, y_nbr, sib = places()
        for k, chip_index, quart, dev in ((0, xc, 0, y_nbr), (1, yc, 1, x_nbr)):
            copy(cout[0], k, chip_index, c, quart, ss, rs, b, dev).wait_recv()
            copy(cout[0], 4 + k, chip_index, c, quart, ss, rs, b, dev).start()
            copy(cout[0], 6 + k, chip_index, c, quart, ss, rs, b, sib).start()

    later = ((2, 1, 1), (3, 2, 0), (4, 3, 0), (5, 3, 1))

    def finish(cin, cout, ss, rs, b):
        c, me, xc, yc, dc, _, _, sib = places()
        chip_of = {1: xc, 2: yc, 3: dc}
        for k, whose, quart in later:
            copy(cout[0], k, chip_of[whose], c, quart, ss, rs, b, sib).wait_recv()
            copy(cout[0], 6 + k, chip_of[whose], c, quart, ss, rs, b, sib).start()
        for k, whose, quart in ((0, 1, 0), (1, 2, 1)) + later:
            copy(cout[0], 6 + k, chip_of[whose], 1 - c, quart, ss, rs, b, sib).wait_recv()
        for k in range(12):
            copy(cout[0], k, me, c, 0, ss, rs, b, sib).wait_send()

    return _Task([placed], [SDS(placed.shape, placed.dtype)], [(0, 0)], 12, start, finish, middle)


def _t_small_weights(buf):
    def start(cin, cout, ss, rs, b):
        x, y, c, chips = _where()
        mine = cout[0].at[2 * x + y]
        for r, (px, py) in enumerate(chips):
            _rcopy(mine, mine, ss.at[b + r], rs.at[b + r], (px, py, c)).start()

    def finish(cin, cout, ss, rs, b):
        x, y, c, chips = _where()
        for r, (px, py) in enumerate(chips):
            got = cout[0].at[2 * px + py]
            _rcopy(got, got, ss.at[b + r], rs.at[b + r], (px, py, c)).wait_recv()
        for r, (px, py) in enumerate(chips):
            mine = cout[0].at[2 * x + y]
            _rcopy(mine, mine, ss.at[b + r], rs.at[b + r], (px, py, c)).wait_send()

    return _Task([buf], [SDS(buf.shape, buf.dtype)], [(0, 0)], 3, start, finish)


def _t_sibling(gbf):
    def start(cin, cout, ss, rs, b):
        x, y, c, _ = _where()
        for jj in range(N_CHIPS):
            _rcopy(cin[0].at[2 * jj + (1 - c)], cout[0].at[jj], ss.at[b + jj], rs.at[b + jj], (x, y, 1 - c)).start()

    def finish(cin, cout, ss, rs, b):
        x, y, c, _ = _where()
        for jj in range(N_CHIPS):
            got = cout[0].at[jj]
            _rcopy(got, got, ss.at[b + jj], rs.at[b + jj], (x, y, 1 - c)).wait_recv()
        for jj in range(N_CHIPS):
            got = cout[0].at[jj]
            _rcopy(got, got, ss.at[b + jj], rs.at[b + jj], (x, y, 1 - c)).wait_send()

    return _Task([gbf], [SDS((N_CHIPS,) + gbf.shape[1:], BF)], [], N_CHIPS, start, finish)


def _t_chips(pbf):
    def start(cin, cout, ss, rs, b):
        x, y, c, chips = _where()
        for r, (px, py) in enumerate(chips):
            _rcopy(cin[0].at[2 * px + py], cout[0].at[r], ss.at[b + r], rs.at[b + r], (px, py, c)).start()

    def finish(cin, cout, ss, rs, b):
        x, y, c, chips = _where()
        for r, (px, py) in enumerate(chips):
            got = cout[0].at[r]
            _rcopy(got, got, ss.at[b + r], rs.at[b + r], (px, py, c)).wait_recv()
        for r, (px, py) in enumerate(chips):
            got = cout[0].at[r]
            _rcopy(got, got, ss.at[b + r], rs.at[b + r], (px, py, c)).wait_send()

    return _Task([pbf], [SDS((3,) + pbf.shape[1:], BF)], [], 3, start, finish)


def _t_swap(fin):
    def start(cin, cout, ss, rs, b):
        x, y, c, _ = _where()
        mine = cout[0].at[c]
        _rcopy(mine, mine, ss.at[b], rs.at[b], (x, y, 1 - c)).start()

    def finish(cin, cout, ss, rs, b):
        x, y, c, _ = _where()
        got = cout[0].at[1 - c]
        _rcopy(got, got, ss.at[b], rs.at[b], (x, y, 1 - c)).wait_recv()
        _rcopy(got, got, ss.at[b], rs.at[b], (x, y, 1 - c)).wait_send()

    return _Task([fin], [SDS(fin.shape, fin.dtype)], [(0, 0)], 1, start, finish)


def _t_allgather(buf):
    def peers():
        x, y, c, _ = _where()
        out = []
        for rel in range(1, N_DEV):
            px, py, pc = x ^ ((rel >> 2) & 1), y ^ ((rel >> 1) & 1), c ^ (rel & 1)
            out.append((rel - 1, 4 * px + 2 * py + pc, (px, py, pc)))
        return 4 * x + 2 * y + c, out

    def start(cin, cout, ss, rs, b):
        me, ps = peers()
        mine = cout[0].at[me]
        for k, _, dev in ps:
            _rcopy(mine, mine, ss.at[b + k], rs.at[b + k], dev).start()

    def finish(cin, cout, ss, rs, b):
        me, ps = peers()
        for k, pidx, dev in ps:
            got = cout[0].at[pidx]
            _rcopy(got, got, ss.at[b + k], rs.at[b + k], dev).wait_recv()
        for k, _, dev in ps:
            mine = cout[0].at[me]
            _rcopy(mine, mine, ss.at[b + k], rs.at[b + k], dev).wait_send()

    return _Task([buf], [SDS(buf.shape, buf.dtype)], [(0, 0)], N_DEV - 1, start, finish)


def _run_tasks(comm, which, cin, cout, ss, rs):
    i0 = o0 = s0 = 0
    for t in comm:
        getattr(t, which)(cin[i0:i0 + len(t.ins)], cout[o0:o0 + len(t.outs)], ss, rs, s0)
        i0, o0, s0 = i0 + len(t.ins), o0 + len(t.outs), s0 + t.n_sem


def _from_hbm(*arrays):
    return [pltpu.with_memory_space_constraint(a, pltpu.HBM) for a in arrays]


def _in_hbm(shapes):
    return [pltpu.HBM(s.shape, s.dtype) for s in shapes]


def _comm_layout(comm, n_in, n_out):
    c_in = [a for t in comm for a in t.ins]
    c_out = [s for t in comm for s in t.outs]
    aliases, i0, o0 = {}, 0, 0
    for t in comm:
        for i, o in t.alias:
            aliases[n_in + i0 + i] = n_out + o0 + o
        i0, o0 = i0 + len(t.ins), o0 + len(t.outs)
    return c_in, c_out, aliases, sum(t.n_sem for t in comm)


def _call(body, operands, *, name, grid, in_specs, out_specs, out_shape, scratch_shapes=(), sem=None, vmem_mib=None, comm=(),
          free=(), prefetch=()):
    operands = [o if s.memory_space == pltpu.SMEM or k in free else pltpu.with_memory_space_constraint(o, pltpu.HBM)
                for k, (o, s) in enumerate(zip(operands, in_specs))]
    n_pre, n_in, n_out, n_scr = len(prefetch), len(in_specs), len(out_specs), len(scratch_shapes)
    c_in, c_out, aliases, n_sem = _comm_layout(comm, n_pre + n_in, n_out)
    sems = [pltpu.SemaphoreType.DMA((n_sem,)), pltpu.SemaphoreType.DMA((n_sem,))] if comm else []

    def wrapped(*refs):
        pre, refs = refs[:n_pre], refs[n_pre:]
        ins, cin = refs[:n_in], refs[n_in:n_in + len(c_in)]
        rest = refs[n_in + len(c_in):]
        outs, cout = rest[:n_out], rest[n_out:n_out + len(c_out)]
        rest = rest[n_out + len(c_out):]
        scr, csem = rest[:n_scr], rest[n_scr:]
        if not comm:
            return body(*pre, *ins, *outs, *scr)
        step = functools.reduce(lambda acc, k: acc * grid[k] + pl.program_id(k), range(len(grid)), 0)
        n_steps = math.prod(grid)
        pl.when(step == 0)(lambda: _run_tasks(comm, "start", cin, cout, *csem))
        pl.when(step == n_steps // 2)(lambda: _run_tasks(comm, "middle", cin, cout, *csem))
        body(*pre, *ins, *outs, *scr)
        pl.when(step == n_steps - 1)(lambda: _run_tasks(comm, "finish", cin, cout, *csem))

    grid_spec = pltpu.PrefetchScalarGridSpec(
        num_scalar_prefetch=n_pre, grid=grid, in_specs=list(in_specs) + [ANY] * len(c_in),
        out_specs=list(out_specs) + [ANY] * len(c_out), scratch_shapes=list(scratch_shapes) + sems)
    return _pcall(
        wrapped, name=name, grid_spec=grid_spec, out_shape=_in_hbm(list(out_shape) + c_out), input_output_aliases=aliases,
        compiler_params=_params(("arbitrary",) * len(grid) if comm else sem, vmem_mib),
    )(*prefetch, *operands, *_from_hbm(*c_in))


def _comm_call(name, comm):
    c_in, c_out, aliases, n_sem = _comm_layout(comm, 0, 0)

    def body(*refs):
        cin, cout, (ss, rs) = refs[:len(c_in)], refs[len(c_in):len(c_in) + len(c_out)], refs[len(c_in) + len(c_out):]
        for phase in ("start", "middle", "finish"):
            _run_tasks(comm, phase, cin, cout, ss, rs)

    return _pcall(
        body, name=name, in_specs=[ANY] * len(c_in), out_specs=[ANY] * len(c_out), out_shape=_in_hbm(c_out),
        scratch_shapes=[pltpu.SemaphoreType.DMA((n_sem,)), pltpu.SemaphoreType.DMA((n_sem,))],
        input_output_aliases=aliases,
    )(*_from_hbm(*c_in))


def _inproj(x, g1, w_int, comm=()):
    tm = TM

    def body(x_ref, g_ref, w_ref, proj_ref, u_ref):
        xf = x_ref[...]
        r = lax.rsqrt(jnp.mean(xf * xf, axis=-1, keepdims=True) + EPS)
        u = (xf * r * g_ref[...]).astype(BF)
        u_ref[...] = u
        proj_ref[...] = _dot(u, w_ref[...], 1, 1)

    return _call(
        body, (x, g1, w_int), name="inproj", grid=(T // tm,),
        in_specs=[pl.BlockSpec((tm, D), lambda i: (i, 0)), pl.BlockSpec((1, D), lambda i: (0, 0)),
                  _resident((INW, D))],
        out_specs=[pl.BlockSpec((tm, INW), lambda i: (i, 0)), pl.BlockSpec((tm, D), lambda i: (i, 0))],
        out_shape=[SDS((T, INW), F32), SDS((T, D), BF)], sem=("parallel",), vmem_mib=40, comm=comm, free=(0, 1))


def _outproj(y, w_out, x, g2):
    tm = TM

    def body(y_ref, w_ref, x_ref, g_ref, h1_ref, u2_ref):
        h1 = x_ref[...] + _dot(y_ref[...], w_ref[...], 1, 0)
        h1_ref[...] = h1
        r = lax.rsqrt(jnp.mean(h1 * h1, axis=-1, keepdims=True) + EPS)
        u2_ref[...] = (h1 * r * g_ref[...]).astype(BF)

    return _call(
        body, (y, w_out, x, g2), name="outproj", grid=(T // tm,),
        in_specs=[pl.BlockSpec((tm, D), lambda i: (i, 0)), _resident((D, D)),
                  pl.BlockSpec((tm, D), lambda i: (i, 0)), pl.BlockSpec((1, D), lambda i: (0, 0))],
        out_specs=[pl.BlockSpec((tm, D), lambda i: (i, 0)), pl.BlockSpec((tm, D), lambda i: (i, 0))],
        out_shape=[SDS((T, D), F32), SDS((T, D), BF)], sem=("parallel",), vmem_mib=32, free=(1, 2, 3))


def _ffn_up(u2, w_upt, comm=()):
    tm, tn = 1024, 512

    def body(u_ref, w_ref, o_ref):
        o_ref[...] = _dot(u_ref[...], w_ref[...], 1, 1).astype(BF)

    return _call(
        body, (u2, w_upt), name="ffn_up", grid=(T // tm, 2 * DFF // tn),
        in_specs=[pl.BlockSpec((tm, D), lambda i, j: (i, 0)), pl.BlockSpec((tn, D), lambda i, j: (j, 0))],
        out_specs=[pl.BlockSpec((tm, tn), lambda i, j: (i, j))], out_shape=[SDS((T, 2 * DFF), BF)],
        sem=("parallel", "parallel"), vmem_mib=32, comm=comm, free=(1,))


def _ffn_down(a, w_down, h1, tgt):
    tm = TM

    def body(a_ref, w_ref, h1_ref, t_ref, dh_ref, dhb_ref, l_ref):
        @pl.when(pl.program_id(0) == 0)
        def _():
            l_ref[...] = jnp.zeros_like(l_ref)

        h2 = h1_ref[...] + _dot(a_ref[...], w_ref[...], 1, 0)
        e = h2 - t_ref[...]
        dh = e * (1.0 / D)
        dh_ref[...] = dh
        dhb_ref[...] = dh.astype(BF)
        e2 = jnp.sum((e * e).reshape(tm // 8, 8, D), axis=0)
        acc = e2[:, 0:128]
        for k in range(1, D // 128):
            acc = acc + e2[:, k * 128:(k + 1) * 128]
        l_ref[...] += acc

    return _call(
        body, (a, w_down, h1, tgt), name="ffn_down", grid=(T // tm,),
        in_specs=[pl.BlockSpec((tm, DFF), lambda i: (i, 0)), _resident((DFF, D)),
                  pl.BlockSpec((tm, D), lambda i: (i, 0)), pl.BlockSpec((tm, D), lambda i: (i, 0))],
        out_specs=[pl.BlockSpec((tm, D), lambda i: (i, 0)), pl.BlockSpec((tm, D), lambda i: (i, 0)),
                   pl.BlockSpec((8, 128), lambda i: (0, 0))],
        out_shape=[SDS((T, D), F32), SDS((T, D), BF), SDS((8, 128), F32)], sem=("arbitrary",), vmem_mib=40, free=(2, 3))


def _bucket_table():
    q = np.arange(BLK, dtype=np.int32)[:, None]
    j = np.arange(2 * BLK, dtype=np.int32)[None, :]
    n = np.maximum(q + BLK - j, 0)
    nf = np.maximum(n, 1).astype(np.float32)
    max_exact = NBUCKET // 2
    large = max_exact + (np.log(nf / np.float32(max_exact)) / np.float32(math.log(BLK / max_exact))
                         * np.float32(NBUCKET - max_exact)).astype(np.int32)
    large = np.minimum(large, NBUCKET - 1)
    return np.where(n < max_exact, n, large).astype(np.int32)


def _band_bias_bwd(dbias, bucket, me):
    def body(me_ref, db_ref, bk_ref, o_ref):
        bk = bk_ref[...]
        for b in range(NBUCKET):
            m = bk == b
            for h in range(NH):
                v = jnp.where(m, db_ref[h * BLK:(h + 1) * BLK, :], 0.0)
                s = jnp.sum(jnp.sum(v, axis=1, keepdims=True), axis=0, keepdims=True)
                o_ref[0, h:h + 1, b:b + 1] = s

    grid_spec = pltpu.PrefetchScalarGridSpec(
        num_scalar_prefetch=1, grid=(1,),
        in_specs=[pl.BlockSpec((NH * BLK, 2 * BLK), lambda i, me_ref: (0, 0)),
                  pl.BlockSpec((BLK, 2 * BLK), lambda i, me_ref: (0, 0))],
        out_specs=pl.BlockSpec((1, NH, NBUCKET), lambda i, me_ref: (me_ref[0], 0, 0)),
    )
    return _pcall(body, name="band_bias_bwd", grid_spec=grid_spec, out_shape=SDS((N_DEV, NH, NBUCKET), F32),
                  compiler_params=_params(("arbitrary",)))(me, dbias, bucket)


def _two_bf16(x):
    hi = x.astype(BF)
    return hi, (x - hi.astype(F32)).astype(BF)


def _head_sums(x, seg):
    hi, lo = _two_bf16(x)
    s = seg[0:x.shape[1], :]
    return _dot(hi, s, 1, 0) + _dot(lo, s, 1, 0)


def _head_spread(v, seg, width):
    hi, lo = _two_bf16(v)
    s = seg[0:width, :]
    return _dot(hi, s, 1, 1) + _dot(lo, s, 1, 1)


def _head_norm(x, g_t, seg, by_head=False):
    if by_head:
        heads = [x[:, h * HD:(h + 1) * HD] for h in range(x.shape[1] // HD)]
        r = jnp.concatenate([jnp.broadcast_to(lax.rsqrt(jnp.mean(v * v, axis=-1, keepdims=True) + EPS), v.shape)
                             for v in heads], axis=1)
    else:
        r = lax.rsqrt(_head_sums(x * x, seg) * (1.0 / HD) + EPS)
        r = _head_spread(r, seg, x.shape[1])
    return x * r * g_t, r


def _head_norm_bwd(dy, x, r, g_t, seg):
    dg_t = jnp.sum(dy * (x * r), axis=0, keepdims=True)
    dgx = dy * g_t
    mean = _head_spread(_head_sums(x * dgx, seg) * (1.0 / HD), seg, x.shape[1])
    return r * dgx - x * (r * r * r) * mean, dg_t


def _fold_heads(v):
    out = v[:, 0:HD]
    for h in range(1, v.shape[1] // HD):
        out = out + v[:, h * HD:(h + 1) * HD]
    return out


def _mix_forward(P, zc8, zh8, pkv, first, cw, qg_t, kg_t, gco, gao, seg, sink_ref, bias_ref, by_head=False):
    gate_b = P[:, 0:CW]
    gate_c = P[:, CW:2 * CW]
    hc = P[:, 2 * CW:3 * CW]
    z = gate_c * hc
    keep = jnp.where(first, 0.0, 1.0)
    zp = zc8 * zh8 * keep
    p1 = zp[7:8, :]
    p2 = zp[6:7, :]
    row = lax.broadcasted_iota(jnp.int32, (BLK, 1), 0)
    z1 = jnp.where(row == 0, p1, pltpu.roll(z, 1, 0))
    z2 = jnp.where(row == 0, p2, jnp.where(row == 1, p1, pltpu.roll(z, 2, 0)))
    cz = cw[0:1, :] * z2 + cw[1:2, :] * z1 + cw[2:3, :] * z
    y_conv = gate_b * cz

    scale = HD ** -0.5
    qi = lax.broadcasted_iota(jnp.int32, (GQ * BLK, 2 * BLK), 0) & (BLK - 1)
    kj = lax.broadcasted_iota(jnp.int32, (GQ * BLK, 2 * BLK), 1)
    dd = qi + BLK - kj
    first_key = jnp.where(first, BLK, 0)
    valid = (dd >= 0) & (dd < BLK) & (kj >= first_key)

    q0 = 3 * CW
    k0 = q0 + AW
    v0 = k0 + NKV * HD
    q_raw = P[:, q0:k0]
    qn, rq = _head_norm(q_raw, qg_t, seg, by_head)
    qs = (qn * scale).astype(BF)
    k_raw = jnp.concatenate([pkv[:, 0:NKV * HD], P[:, k0:v0]], axis=0)
    kn, rk = _head_norm(k_raw, kg_t, seg, by_head)
    knb = kn.astype(BF)
    heads = []
    outs = []
    for kv in range(NKV):
        kb = knb[:, kv * HD:(kv + 1) * HD]
        vb = jnp.concatenate([pkv[:, NKV * HD + kv * HD:NKV * HD + (kv + 1) * HD],
                              P[:, v0 + kv * HD:v0 + (kv + 1) * HD]], axis=0).astype(BF)
        Q = jnp.concatenate([qs[:, (kv * GQ + g) * HD:(kv * GQ + g + 1) * HD] for g in range(GQ)], axis=0)
        S = _dot(Q, kb, 1, 1) + bias_ref[kv * GQ * BLK:(kv + 1) * GQ * BLK, :]
        S = jnp.where(valid, S, NEG_INF)
        sink = jnp.concatenate([jnp.full((BLK, 1), sink_ref[0, kv * GQ + g], F32) for g in range(GQ)], axis=0)
        m = jnp.maximum(jnp.max(S, axis=-1, keepdims=True), sink)
        p = jnp.exp(S - m)
        es = jnp.exp(sink - m)
        denom = jnp.sum(p, axis=-1, keepdims=True) + es
        probs = p / denom
        O = _dot(probs.astype(BF), vb, 1, 0)
        heads.append(dict(kb=kb, vb=vb, Q=Q, probs=probs, psink=es / denom, O=O))
        outs += [O[g * BLK:(g + 1) * BLK, :] for g in range(GQ)]
    y_attn = jnp.concatenate(outs, axis=1)

    rc = lax.rsqrt(jnp.mean(y_conv * y_conv, axis=-1, keepdims=True) + EPS)
    ra = lax.rsqrt(jnp.mean(y_attn * y_attn, axis=-1, keepdims=True) + EPS)
    y = jnp.concatenate([y_conv * rc * gco, y_attn * ra * gao], axis=1)
    return dict(gate_b=gate_b, gate_c=gate_c, hc=hc, z=z, z1=z1, z2=z2, cz=cz, y_conv=y_conv, y_attn=y_attn,
                rc=rc, ra=ra, heads=heads, y=y, row=row, scale=scale, q_raw=q_raw, rq=rq, k_raw=k_raw, rk=rk)


BPS = 2
TILE = BPS * BLK
KV0 = 3 * CW + AW


def _mix_in_specs(tile_of):
    return [
        pl.BlockSpec(memory_space=pltpu.SMEM),
        pl.BlockSpec((TILE, INW), lambda s: (tile_of(s), 0)),
        pl.BlockSpec((8, CW), lambda s: (jnp.maximum(tile_of(s) * (TILE // 8) - 1, 0), 1)),
        pl.BlockSpec((8, CW), lambda s: (jnp.maximum(tile_of(s) * (TILE // 8) - 1, 0), 2)),
        pl.BlockSpec((BLK, 2 * NKV * HD), lambda s: (jnp.maximum(tile_of(s) * BPS - 1, 0), KV0 // (2 * NKV * HD))),
    ]


def _block_inputs(tile, b, zc_ref, zh_ref, pkv_ref, first_tile):
    P = tile[b * BLK:(b + 1) * BLK, :]
    if b == 0:
        return P, zc_ref[...], zh_ref[...], pkv_ref[...], first_tile
    lo = b * BLK
    return P, tile[lo - 8:lo, CW:2 * CW], tile[lo - 8:lo, 2 * CW:3 * CW], tile[lo - BLK:lo, KV0:KV0 + 2 * NKV * HD], False


def _mix_param_specs():
    return [
        pl.BlockSpec((8, CW), lambda s: (0, 0)),
        pl.BlockSpec((1, AW), lambda s: (0, 0)),
        pl.BlockSpec((1, NKV * HD), lambda s: (0, 0)),
        pl.BlockSpec((1, CW), lambda s: (0, 0)),
        pl.BlockSpec((1, AW), lambda s: (0, 0)),
        pl.BlockSpec((AW, 128), lambda s: (0, 0)),
        pl.BlockSpec((NH * BLK, 2 * BLK), lambda s: (0, 0)),
    ]


def _mix_params(cw8, qg, kg, gco, gao, bias):
    seg = np.zeros((AW, 128), np.float32)
    seg[np.arange(AW), np.arange(AW) // HD] = 1.0
    return (cw8, jnp.tile(qg, (1, NH)), jnp.tile(kg, (1, NKV)), gco, gao, jnp.asarray(seg, BF), bias)


def _mix_fwd(proj, sinks, cw8, qg, kg, gco, gao, bias, comm=()):
    def body(sink_ref, p_ref, zc_ref, zh_ref, pkv_ref, cw_ref, qg_ref, kg_ref, gco_ref, gao_ref, seg_ref, bias_ref, y_ref):
        tile = p_ref[...]
        for b in range(BPS):
            f = _mix_forward(*_block_inputs(tile, b, zc_ref, zh_ref, pkv_ref, pl.program_id(0) == 0), cw_ref[...],
                             qg_ref[...], kg_ref[...], gco_ref[...], gao_ref[...], seg_ref[...], sink_ref, bias_ref, by_head=True)
            y_ref[b * BLK:(b + 1) * BLK, :] = f["y"].astype(BF)

    return _call(
        body, (sinks, proj, proj, proj, proj, *_mix_params(cw8, qg, kg, gco, gao, bias)), name="mix_fwd", grid=(T // TILE,),
        in_specs=_mix_in_specs(lambda s: s) + _mix_param_specs(),
        out_specs=[pl.BlockSpec((TILE, D), lambda s: (s, 0))], out_shape=[SDS((T, D), BF)],
        sem=("parallel",), vmem_mib=40, comm=comm, free=tuple(range(5, 12)))


def _mix_bwd(proj, dy, sinks, cw8, qg, kg, gco, gao, bias, comm=()):
    n_steps = T // TILE

    def tile_of(s):
        return n_steps - 1 - s

    def body(sink_ref, p_ref, zc_ref, zh_ref, pkv_ref, dy_ref, cw_ref, qg_ref, kg_ref, gco_ref, gao_ref, seg_ref, bias_ref,
             dproj_ref, dcw_ref, dqg_ref, dkg_ref, dgco_ref, dgao_ref, dsink_ref, dbias_ref,
             ndcz_ref, dkc_ref, dvc_ref):
        s = pl.program_id(0)

        @pl.when(s == 0)
        def _():
            for r in (dcw_ref, dqg_ref, dkg_ref, dgco_ref, dgao_ref, dsink_ref, dbias_ref, ndcz_ref, dkc_ref, dvc_ref):
                r[...] = jnp.zeros_like(r)

        params = (cw_ref[...], qg_ref[...], kg_ref[...], gco_ref[...], gao_ref[...], seg_ref[...])
        tile = p_ref[...]
        carry = (ndcz_ref[...], dkc_ref[...], dvc_ref[...])
        total = None
        for b in reversed(range(BPS)):
            f = _mix_forward(*_block_inputs(tile, b, zc_ref, zh_ref, pkv_ref, s == n_steps - 1), *params, sink_ref, bias_ref)
            pieces, sums, carry = one_block(f, dy_ref[b * BLK:(b + 1) * BLK, :], params, carry)
            for lo, piece in pieces:
                dproj_ref[b * BLK:(b + 1) * BLK, lo:lo + piece.shape[1]] = piece
            total = sums if total is None else [t + v for t, v in zip(total, sums)]
        ndcz_ref[...], dkc_ref[...], dvc_ref[...] = carry
        dcw, dqg_t, dkg_t, dgco, dgao, dsink, *ds = total
        dcw_ref[0:3, :] += dcw
        dqg_ref[...] += _fold_heads(dqg_t)
        dkg_ref[...] += _fold_heads(dkg_t)
        dgco_ref[...] += dgco
        dgao_ref[...] += dgao
        dsink_ref[...] += dsink
        for kv in range(NKV):
            dbias_ref[kv * GQ * BLK:(kv + 1) * GQ * BLK, :] += ds[kv]

    def one_block(f, dy, params, carry):
        cw, qg_v, kg_v, gco_v, gao_v, seg = params
        nxt, dk_carry, dv_carry = carry
        dyc, dgco = _rms_bwd(dy[:, 0:CW], f["y_conv"], f["rc"], gco_v)
        dya, dgao = _rms_bwd(dy[:, CW:CW + AW], f["y_attn"], f["ra"], gao_v)

        row = f["row"]
        dgate_b = dyc * f["cz"]
        dcz = dyc * f["gate_b"]
        dcw = jnp.concatenate([jnp.sum(dcz * f[k], axis=0, keepdims=True) for k in ("z2", "z1", "z")], axis=0)
        n0 = nxt[0:1, :]
        n1 = nxt[1:2, :]
        d1 = jnp.where(row == BLK - 1, n0, pltpu.roll(dcz, BLK - 1, 0))
        d2 = jnp.where(row == BLK - 1, n1, jnp.where(row == BLK - 2, n0, pltpu.roll(dcz, BLK - 2, 0)))
        dz = cw[2:3, :] * dcz + cw[1:2, :] * d1 + cw[0:1, :] * d2
        pieces = [(0, dgate_b.astype(BF)), (CW, (dz * f["hc"]).astype(BF)), (2 * CW, (dz * f["gate_c"]).astype(BF))]

        scale = f["scale"]
        lane = lax.broadcasted_iota(jnp.int32, (1, 128), 1)
        dsink = jnp.zeros((1, 128), F32)
        dq_cols, dk_cols, dv_cols, dk_prev, dv_prev, ds = [], [], [], [], [], []
        for kv in range(NKV):
            hd = f["heads"][kv]
            dO = jnp.concatenate([dya[:, (kv * GQ + g) * HD:(kv * GQ + g + 1) * HD] for g in range(GQ)], axis=0)
            delta = jnp.sum(dO * hd["O"], axis=-1, keepdims=True)
            dOb = dO.astype(BF)
            dP = _dot(dOb, hd["vb"], 1, 1)
            dS = hd["probs"] * (dP - delta)
            dsk = hd["psink"] * delta
            for g in range(GQ):
                tot = jnp.sum(dsk[g * BLK:(g + 1) * BLK, :], axis=0, keepdims=True)
                dsink = dsink - jnp.where(lane == kv * GQ + g, tot, 0.0)
            ds.append(dS)
            dSb = dS.astype(BF)
            dQ = _dot(dSb, hd["kb"], 1, 0)
            dKb = _dot(dSb, hd["Q"], 0, 0)
            dVb = _dot(hd["probs"].astype(BF), dOb, 0, 0)
            dk_cols.append(dKb[BLK:, :] + dk_carry[:, kv * HD:(kv + 1) * HD])
            dv_cols.append(dVb[BLK:, :] + dv_carry[:, kv * HD:(kv + 1) * HD])
            dk_prev.append(dKb[:BLK, :])
            dv_prev.append(dVb[:BLK, :])
            dq_cols += [dQ[g * BLK:(g + 1) * BLK, :] for g in range(GQ)]
        dq_raw, dqg_t = _head_norm_bwd(jnp.concatenate(dq_cols, axis=1) * scale, f["q_raw"], f["rq"], qg_v, seg)
        dk_raw, dkg_t = _head_norm_bwd(jnp.concatenate(dk_cols, axis=1), f["k_raw"][BLK:, :], f["rk"][BLK:, :], kg_v, seg)
        pieces.append((3 * CW, jnp.concatenate([dq_raw, dk_raw] + dv_cols, axis=1).astype(BF)))
        owed = (dcz[0:8, :], jnp.concatenate(dk_prev, axis=1), jnp.concatenate(dv_prev, axis=1))
        return pieces, [dcw, dqg_t, dkg_t, dgco, dgao, dsink, *ds], owed

    small = lambda r, c: pl.BlockSpec((r, c), lambda s: (0, 0))
    return _call(
        body, (sinks, proj, proj, proj, proj, dy, *_mix_params(cw8, qg, kg, gco, gao, bias)), name="mix_bwd", grid=(n_steps,),
        in_specs=_mix_in_specs(tile_of) + [pl.BlockSpec((TILE, D), lambda s: (tile_of(s), 0))] + _mix_param_specs(),
        out_specs=[pl.BlockSpec((TILE, INW), lambda s: (tile_of(s), 0)), small(8, CW), small(1, HD), small(1, HD),
                   small(1, CW), small(1, AW), small(1, 128), small(NH * BLK, 2 * BLK)],
        out_shape=[SDS((T, INW), BF), SDS((8, CW), F32), SDS((1, HD), F32), SDS((1, HD), F32), SDS((1, CW), F32),
                   SDS((1, AW), F32), SDS((1, 128), F32), SDS((NH * BLK, 2 * BLK), F32)],
        scratch_shapes=[pltpu.VMEM((8, CW), F32), pltpu.VMEM((BLK, NKV * HD), F32), pltpu.VMEM((BLK, NKV * HD), F32)],
        sem=("arbitrary",), vmem_mib=56, comm=comm, free=(1, 2, 3, 4) + tuple(range(6, 13)))


FT = 256
NFT = DFF // FT
RC = 128
NCH = T // RC
LEAD = 16


def _rows8(x):
    return jnp.sum(x.reshape(x.shape[0] // 8, 8, x.shape[1]), axis=0)


def _ffn_act_specs():
    return [
        pl.BlockSpec((T, FT), lambda j: (0, j)), pl.BlockSpec((T, FT), lambda j: (0, NFT + j)),
        pl.BlockSpec((8, FT), lambda j: (0, j)), pl.BlockSpec((8, FT), lambda j: (0, NFT + j)),
        pl.BlockSpec((1, FT), lambda j: (0, j)), pl.BlockSpec((1, FT), lambda j: (0, NFT + j)),
    ]


def _conv_rows(win, w, b, n):
    win = win.astype(F32)
    u = win[LEAD:LEAD + n]
    u1 = pltpu.roll(win, 1, 0)[LEAD:LEAD + n]
    u2 = pltpu.roll(win, 2, 0)[LEAD:LEAD + n]
    return u2, u1, u, w[0:1, :] * u2 + w[1:2, :] * u1 + w[2:3, :] * u + b


def _ffn_act(up, fw8, fb, comm=()):
    def body(ug_ref, uv_ref, wg_ref, wv_ref, bg_ref, bv_ref, a_ref):
        wg, wv, bg, bv = wg_ref[...], wv_ref[...], bg_ref[...], bv_ref[...]

        def chunk(win_g, win_v):
            gp = _conv_rows(win_g, wg, bg, RC)[3]
            vp = _conv_rows(win_v, wv, bv, RC)[3]
            return (gp * jax.nn.sigmoid(gp) * vp).astype(BF)

        zero = jnp.zeros((LEAD, FT), BF)
        a_ref[0:RC, :] = chunk(jnp.concatenate([zero, ug_ref[0:RC, :]], axis=0),
                               jnp.concatenate([zero, uv_ref[0:RC, :]], axis=0))

        def step(i, carry):
            r0 = pl.multiple_of(i * RC, RC)
            win = pl.ds(r0 - LEAD, RC + LEAD)
            a_ref[pl.ds(r0, RC), :] = chunk(ug_ref[win, :], uv_ref[win, :])
            return carry

        lax.fori_loop(1, NCH, step, 0)

    return _call(
        body, (up, up, fw8, fw8, fb, fb), name="ffn_act", grid=(NFT,), in_specs=_ffn_act_specs(),
        out_specs=[pl.BlockSpec((T, FT), lambda j: (0, j))], out_shape=[SDS((T, DFF), BF)],
        sem=("parallel",), vmem_mib=40, comm=comm, free=(2, 3, 4, 5))


def _ffn_act_bwd(up, da, fw8, fb, comm=()):
    ext = RC + LEAD

    def body(ug_ref, uv_ref, wg_ref, wv_ref, bg_ref, bv_ref, da_ref,
             dug_ref, duv_ref, dwg_ref, dwv_ref, dbg_ref, dbv_ref):
        wg, wv, bg, bv = wg_ref[...], wv_ref[...], bg_ref[...], bv_ref[...]

        def chunk(win_g, win_v, da_e):
            g2, g1, g0, gp = _conv_rows(win_g, wg, bg, ext)
            v2, v1, v0, vp = _conv_rows(win_v, wv, bv, ext)
            da_e = da_e.astype(F32)
            sig = jax.nn.sigmoid(gp)
            dvp = da_e * (gp * sig)
            dgp = da_e * vp * (sig * (1.0 + gp * (1.0 - sig)))

            def back(dp, w):
                return (w[2:3, :] * dp[0:RC] + w[1:2, :] * pltpu.roll(dp, ext - 1, 0)[0:RC]
                        + w[0:1, :] * pltpu.roll(dp, ext - 2, 0)[0:RC]).astype(BF)

            def sums(dp, u2, u1, u0):
                d = dp[0:RC]
                return [_rows8(d), _rows8(d * u2[0:RC]), _rows8(d * u1[0:RC]), _rows8(d * u0[0:RC])]

            return back(dgp, wg), back(dvp, wv), sums(dgp, g2, g1, g0) + sums(dvp, v2, v1, v0)

        zero = jnp.zeros((LEAD, FT), BF)
        dug, duv, acc = chunk(jnp.concatenate([zero, ug_ref[0:ext, :]], axis=0),
                              jnp.concatenate([zero, uv_ref[0:ext, :]], axis=0), da_ref[0:ext, :])
        dug_ref[0:RC, :] = dug
        duv_ref[0:RC, :] = duv

        def step(i, acc):
            r0 = pl.multiple_of(i * RC, RC)
            win = pl.ds(r0 - LEAD, ext + LEAD)
            dug, duv, part = chunk(ug_ref[win, :], uv_ref[win, :], da_ref[pl.ds(r0, ext), :])
            dug_ref[pl.ds(r0, RC), :] = dug
            duv_ref[pl.ds(r0, RC), :] = duv
            return [a + p for a, p in zip(acc, part)]

        acc = lax.fori_loop(1, NCH - 1, step, acc)
        r0 = T - RC
        tail = lambda ref, lo: jnp.concatenate([ref[lo:T, :], zero], axis=0)
        dug, duv, part = chunk(tail(ug_ref, r0 - LEAD), tail(uv_ref, r0 - LEAD), tail(da_ref, r0))
        dug_ref[r0:T, :] = dug
        duv_ref[r0:T, :] = duv
        tot = [jnp.sum(a + p, axis=0, keepdims=True) for a, p in zip(acc, part)]
        for k, (dw_ref, db_ref) in enumerate(((dwg_ref, dbg_ref), (dwv_ref, dbv_ref))):
            db_ref[...] = tot[4 * k]
            dw_ref[...] = jnp.zeros_like(dw_ref)
            for r in range(3):
                dw_ref[r:r + 1, :] = tot[4 * k + 1 + r]

    col = lambda r: pl.BlockSpec((r, FT), lambda j: (0, j))
    return _call(
        body, (up, up, fw8, fw8, fb, fb, da), name="ffn_act_bwd", grid=(NFT,),
        in_specs=_ffn_act_specs() + [pl.BlockSpec((T, FT), lambda j: (0, j))],
        out_specs=[col(T), col(T), col(8), col(8), col(1), col(1)],
        out_shape=[SDS((T, DFF), BF), SDS((T, DFF), BF), SDS((8, DFF), F32), SDS((8, DFF), F32),
                   SDS((1, DFF), F32), SDS((1, DFF), F32)],
        sem=("parallel",), vmem_mib=40, comm=comm, free=(0, 1, 2, 3, 4, 5))


def _ffn_down_bwd(dh2b, w_down, comm=()):
    tm = TM

    def body(d_ref, w_ref, o_ref):
        o_ref[...] = _dot(d_ref[...], w_ref[...], 1, 1).astype(BF)

    return _call(
        body, (dh2b, w_down), name="ffn_down_bwd", grid=(T // tm,),
        in_specs=[pl.BlockSpec((tm, D), lambda i: (i, 0)), _resident((DFF, D))],
        out_specs=[pl.BlockSpec((tm, DFF), lambda i: (i, 0))], out_shape=[SDS((T, DFF), BF)],
        sem=("parallel",), vmem_mib=40, comm=comm, free=(0, 1))


def _norm_matmul_bwd(name, a_list, w_t, k_offsets, xin, g, dres, want_bf16, comm=(), slot=None):
    tm = TM
    ks = [a.shape[1] for a in a_list]
    n_a = len(a_list)
    n_pre = 0 if slot is None else 1

    def body(*refs):
        refs = refs[n_pre:]
        a_refs = refs[:n_a]
        w_ref, x_ref, g_ref, r_ref = refs[n_a:n_a + 4]
        outs = refs[n_a + 4:]
        dx_ref, dg_ref = outs[0], (outs[-1] if slot is None else outs[-1].at[0])

        @pl.when(pl.program_id(0) == 0)
        def _():
            dg_ref[...] = jnp.zeros_like(dg_ref)

        du = _dot(a_refs[0][...], w_ref[k_offsets[0]:k_offsets[0] + ks[0], :], 1, 0)
        for k in range(1, n_a):
            du = du + _dot(a_refs[k][...], w_ref[k_offsets[k]:k_offsets[k] + ks[k], :], 1, 0)
        x = x_ref[...]
        r = lax.rsqrt(jnp.mean(x * x, axis=-1, keepdims=True) + EPS)
        dx, dg = _rms_bwd(du, x, r, g_ref[...])
        dx = r_ref[...] + dx
        dx_ref[...] = dx
        if want_bf16:
            outs[1][...] = dx.astype(BF)
        dg_ref[...] += dg

    tile = lambda c: pl.BlockSpec((tm, c), lambda i, *_: (i, 0))
    if slot is None:
        dg_spec, dg_shape = pl.BlockSpec((1, D), lambda i: (0, 0)), SDS((1, D), F32)
    else:
        dg_spec, dg_shape = pl.BlockSpec((1, 1, D), lambda i, slot_ref: (slot_ref[0], 0, 0)), SDS((N_DEV, 1, D), F32)
    out_specs = [tile(D)] + ([tile(D)] if want_bf16 else []) + [dg_spec]
    out_shape = [SDS((T, D), F32)] + ([SDS((T, D), BF)] if want_bf16 else []) + [dg_shape]
    return _call(
        body, (*a_list, w_t, xin, g, dres), name=name, grid=(T // tm,), prefetch=() if slot is None else (slot,),
        in_specs=[tile(k) for k in ks] + [_resident(w_t.shape), tile(D),
                                           pl.BlockSpec((1, D), lambda i, *_: (0, 0)), tile(D)],
        out_specs=out_specs, out_shape=out_shape, sem=("arbitrary",), vmem_mib=56, comm=comm, free=tuple(range(n_a + 4)))


def _out_bwd(dh1b, w_out, comm=()):
    tm = TM

    def body(d_ref, w_ref, o_ref):
        o_ref[...] = _dot(d_ref[...], w_ref[...], 1, 1)

    return _call(
        body, (dh1b, w_out), name="out_bwd", grid=(T // tm,),
        in_specs=[pl.BlockSpec((tm, D), lambda i: (i, 0)), _resident((D, D))],
        out_specs=[pl.BlockSpec((tm, D), lambda i: (i, 0))], out_shape=[SDS((T, D), F32)],
        sem=("parallel",), vmem_mib=32, comm=comm, free=(0, 1))


def _wgrad(name, a_list, b, old_a, comm=()):
    m_k = a_list[0].shape[1]
    tm = max(t for t in range(128, m_k // 2 + 1, 128) if m_k % t == 0)
    steps = [a.shape[1] // tm for a in a_list]
    starts = [sum(steps[:k]) for k in range(len(a_list))]
    n_a = len(a_list)

    def body(*refs):
        a_refs, b_ref, o_ref = refs[:n_a], refs[n_a], refs[n_a + 1]
        i = pl.program_id(0)
        for k in range(n_a):
            @pl.when((i >= starts[k]) & (i < starts[k] + steps[k]))
            def _(k=k):
                o_ref[...] = _dot(a_refs[k][...], b_ref[...], 0, 0).astype(BF)

    def a_spec(k):
        return pl.BlockSpec((T, tm), lambda i: (0, jnp.clip(i - starts[k], 0, steps[k] - 1)))

    m_total = tm * sum(steps)
    return _call(
        body, (*a_list, b), name=name, grid=(sum(steps),),
        in_specs=[a_spec(k) for k in range(n_a)] + [_resident((T, D))],
        out_specs=[pl.BlockSpec((tm, D), lambda i: (i, 0))], out_shape=[SDS((m_total, D), BF)],
        sem=("parallel",), vmem_mib=40, comm=comm, free=tuple(range(n_a)) if old_a else (n_a,))


def _chip_sum(name, gbf, from_sib, core, chip):
    h = gbf.shape[1]
    th = h // 2

    def body(core_ref, chip_ref, g_ref, s_ref, pbf_ref, own_ref):
        p = g_ref[0].astype(F32) + s_ref[0].astype(F32)
        pbf_ref[0] = p.astype(BF)

        @pl.when(pl.program_id(1) == chip_ref[0])
        def _():
            own_ref[...] = p

    grid_spec = pltpu.PrefetchScalarGridSpec(
        num_scalar_prefetch=2, grid=(h // th, N_CHIPS),
        in_specs=[pl.BlockSpec((1, th, D), lambda t, jj, core_ref, chip_ref: (2 * jj + core_ref[0], t, 0)),
                  pl.BlockSpec((1, th, D), lambda t, jj, core_ref, chip_ref: (jj, t, 0))],
        out_specs=[pl.BlockSpec((1, th, D), lambda t, jj, core_ref, chip_ref: (jj, t, 0)),
                   pl.BlockSpec((th, D), lambda t, jj, core_ref, chip_ref: (t, 0))],
    )
    return _pcall(
        body, name=name, grid_spec=grid_spec, out_shape=_in_hbm([SDS((N_CHIPS, h, D), BF), SDS((h, D), F32)]),
        compiler_params=_params(("arbitrary", "arbitrary"), 32),
    )(core, chip, *_from_hbm(gbf, from_sib))


def _final_sum(name, own, from_chips, core, comm=()):
    h = own.shape[0]

    def body(core_ref, o_ref, r_ref, f_ref):
        f_ref[0] = ((o_ref[...] + r_ref[0].astype(F32)) + r_ref[1].astype(F32)) + r_ref[2].astype(F32)

    return _call(
        body, (own, from_chips), name=name, grid=(1,), prefetch=(core,),
        in_specs=[pl.BlockSpec((h, D), lambda i, core_ref: (0, 0)), pl.BlockSpec((3, h, D), lambda i, core_ref: (0, 0, 0))],
        out_specs=[pl.BlockSpec((1, h, D), lambda i, core_ref: (core_ref[0], 0, 0))], out_shape=[SDS((2, h, D), F32)],
        sem=("arbitrary",), vmem_mib=40, comm=comm)


def _adam_math(w, g, m, v):
    nm = ADAM_B1 * m + (1.0 - ADAM_B1) * g
    nv = ADAM_B2 * v + (1.0 - ADAM_B2) * (g * g)
    m_hat = nm / (1.0 - ADAM_B1 ** ADAM_STEP)
    v_hat = nv / (1.0 - ADAM_B2 ** ADAM_STEP)
    return -ADAM_LR * (m_hat / (jnp.sqrt(v_hat) + ADAM_EPS) + ADAM_WD * w), nm, nv


def _adamw(name, w, g, m, v, tr, copy_g=False, stage=True):
    rows, cols = w.shape

    def body(w_ref, g_ref, m_ref, v_ref, *outs):
        g_val = g_ref[...]
        if copy_g:
            outs[0][...] = g_val
        d_ref, nm_ref, nv_ref = outs[-3:]
        d_ref[...], nm_ref[...], nv_ref[...] = _adam_math(w_ref[...], g_val, m_ref[...], v_ref[...])

    spec = pl.BlockSpec((tr, cols), lambda i: (i, 0))
    n_out = 4 if copy_g else 3
    return _call(body, (w, g, m, v), name=name, grid=(rows // tr,), in_specs=[spec] * 4, out_specs=[spec] * n_out,
                 out_shape=[SDS((rows, cols), F32)] * n_out, sem=("parallel",), vmem_mib=32,
                 free=(0, 2, 3) if stage else ())


C_G1, C_G2, C_GCO, C_GAO, C_DCW, C_DQG, C_DKG, C_SINK, C_SQ = 0, 1024, 2048, 2560, 3072, 4608, 4736, 4864, 5632
P_W = C_SQ + 128


def _pack_small(me, dfwg, dfwv, dfbg, dfbv, dg2, dgco, dgao, dcw8, dqg, dkg, dsink, sq):
    def body(me_ref, dfwg_r, dfwv_r, dfbg_r, dfbv_r, dg2_r, dgco_r, dgao_r, dcw_r, dqg_r, dkg_r, dsink_r, sq_r, o):
        o[...] = jnp.zeros_like(o)
        o[0, :, 0:DFF] = dfwg_r[...]
        o[0, :, DFF:2 * DFF] = dfwv_r[...]
        o[0, 3:4, 0:DFF] = dfbg_r[...]
        o[0, 3:4, DFF:2 * DFF] = dfbv_r[...]
        o[0, 4:5, C_G2:C_G2 + D] = dg2_r[...]
        o[0, 4:5, C_GCO:C_GCO + CW] = dgco_r[...]
        o[0, 4:5, C_GAO:C_GAO + AW] = dgao_r[...]
        for r in range(3):
            o[0, 4:5, C_DCW + r * CW:C_DCW + (r + 1) * CW] = dcw_r[r:r + 1, :]
        o[0, 4:5, C_DQG:C_DQG + HD] = dqg_r[...]
        o[0, 4:5, C_DKG:C_DKG + HD] = dkg_r[...]
        o[0, 4:5, C_SINK:C_SINK + 128] = dsink_r[...]
        o[0, :, C_SQ:C_SQ + 128] = sq_r[...]

    ins = (dfwg, dfwv, dfbg, dfbv, dg2, dgco, dgao, dcw8, dqg, dkg, dsink, sq)
    return _call(body, ins, name="pack_small", grid=(1,), prefetch=(me,),
                 in_specs=[pl.BlockSpec(a.shape, lambda i, me_ref: (0, 0)) for a in ins],
                 out_specs=[pl.BlockSpec((1, 8, P_W), lambda i, me_ref: (me_ref[0], 0, 0))],
                 out_shape=[SDS((N_DEV, 8, P_W), F32)], sem=("arbitrary",))[0]


N_SMALL = 11


def _small_adam(chip, p_all, g1_all, tbl_all, ws, ms, vs):
    fw_cols = 2 * DFF // N_CHIPS
    cw_cols = CW // N_CHIPS

    def body(chip_ref, p_ref, fw_ref, cw0_ref, cw1_ref, cw2_ref, g1_ref, tbl_ref, *refs):
        w_r, m_r, v_r = refs[0:N_SMALL], refs[N_SMALL:2 * N_SMALL], refs[2 * N_SMALL:3 * N_SMALL]
        outs = refs[3 * N_SMALL:]
        g_o, d_o, nm_o, nv_o = (outs[k * N_SMALL:(k + 1) * N_SMALL] for k in range(4))
        loss_o = outs[4 * N_SMALL]

        def total(ref):
            s = ref[0]
            for k in range(1, N_DEV):
                s = s + ref[k]
            return s

        S = total(p_ref)
        fw = total(fw_ref)
        cws = [total(r) for r in (cw0_ref, cw1_ref, cw2_ref)]

        def step(i, g, at):
            d, nm, nv = _adam_math(w_r[i][at], g, m_r[i][at], v_r[i][at])
            g_o[i][at], d_o[i][at], nm_o[i][at], nv_o[i][at] = g, d, nm, nv

        everything = (slice(None), slice(None))
        step(0, total(g1_ref), everything)
        for r in range(3):
            step(1, cws[r][4:5, :], (r, slice(None), slice(None)))
        step(2, S[4:5, C_DQG:C_DQG + HD], everything)
        step(3, S[4:5, C_DKG:C_DKG + HD], everything)
        step(4, total(tbl_ref), everything)
        step(5, S[4:5, C_SINK:C_SINK + NH], everything)
        step(6, S[4:5, C_GCO:C_GCO + CW], everything)
        step(7, S[4:5, C_GAO:C_GAO + AW], everything)
        step(8, S[4:5, C_G2:C_G2 + D], everything)
        for r in range(3):
            step(9, fw[r:r + 1, :], (r, slice(None), slice(None)))
        step(10, S[3:4, 0:2 * DFF], everything)
        sq = S[:, C_SQ:C_SQ + 128]
        loss_o[...] = jnp.sum(jnp.sum(sq, axis=1, keepdims=True), axis=0, keepdims=True) * (0.5 / D)

    def full(a):
        n = len(a.shape)
        return pl.BlockSpec(a.shape, lambda i, chip_ref: (0,) * n)

    params = [*ws, *ms, *vs]
    out = _call(
        body, (p_all, p_all, p_all, p_all, p_all, g1_all, tbl_all, *params), name="small_adam", grid=(1,), prefetch=(chip,),
        in_specs=[full(p_all),
                  pl.BlockSpec((N_DEV, 8, fw_cols), lambda i, chip_ref: (0, 0, chip_ref[0])),
                  *[pl.BlockSpec((N_DEV, 8, cw_cols), lambda i, chip_ref, r=r: (0, 0, (C_DCW + r * CW) // cw_cols + chip_ref[0]))
                    for r in range(3)],
                  full(g1_all), full(tbl_all), *[full(a) for a in params]],
        out_specs=[full(a) for a in ws] * 4 + [pl.BlockSpec((1, 1), lambda i, chip_ref: (0, 0))],
        out_shape=[SDS(a.shape, F32) for a in ws] * 4 + [SDS((1, 1), F32)], sem=("arbitrary",), vmem_mib=32)
    return out[0:N_SMALL], out[N_SMALL:2 * N_SMALL], out[2 * N_SMALL:3 * N_SMALL], out[3 * N_SMALL:4 * N_SMALL], out[4 * N_SMALL]


PLACE_STEPS = 4


def _place_specs(shards):
    rows = [s.shape[0] // PLACE_STEPS for s in shards]
    return ([pl.BlockSpec((r, D), lambda i, chip_ref: (i, 0)) for r in rows],
            [pl.BlockSpec((r, D), lambda i, chip_ref: (chip_ref[0] * PLACE_STEPS + i, 0)) for r in rows],
            [SDS((N_CHIPS * s.shape[0], D), BF) for s in shards])


def _place_first(chip, shard, conv_w, ffn_conv_w):
    def body(chip_ref, a, s0, s1, o, t0, t1):
        o[...] = a[...].astype(BF)

        @pl.when(pl.program_id(0) == 0)
        def _():
            for s, t in ((s0, t0), (s1, t1)):
                t[...] = jnp.zeros_like(t)
                t[0, 0:3, :] = s[...]

    ins, outs, shapes = _place_specs([shard])
    taps = (conv_w, ffn_conv_w)
    return _call(
        body, (shard, conv_w, ffn_conv_w), name="place_first", grid=(PLACE_STEPS,), prefetch=(chip,),
        in_specs=ins + [pl.BlockSpec(s.shape, lambda i, chip_ref: (0, 0)) for s in taps],
        out_specs=outs + [pl.BlockSpec((1, 8, s.shape[1]), lambda i, chip_ref: (chip_ref[0], 0, 0)) for s in taps],
        out_shape=shapes + [SDS((N_CHIPS, 8, s.shape[1]), F32) for s in taps],
        sem=("arbitrary",), vmem_mib=32, free=(0, 1, 2))


def _place_rest(chip, shards, table, bucket, comm):
    n = len(shards)

    def body(chip_ref, *refs):
        a, (tab_ref, bk_ref), o, bias_ref = refs[:n], refs[n:n + 2], refs[n + 2:2 * n + 2], refs[2 * n + 2]
        for src, dst in zip(a, o):
            dst[...] = src[...].astype(BF)

        @pl.when(pl.program_id(0) == 0)
        def _():
            bk = bk_ref[...]
            eq = [bk == b for b in range(NBUCKET)]
            for h in range(NH):
                acc = jnp.zeros((BLK, 2 * BLK), F32)
                for b in range(NBUCKET):
                    acc = jnp.where(eq[b], tab_ref[h, b], acc)
                bias_ref[h * BLK:(h + 1) * BLK, :] = acc

    ins, outs, shapes = _place_specs(shards)
    return _call(
        body, (*shards, table, bucket), name="place_rest", grid=(PLACE_STEPS,), prefetch=(chip,),
        in_specs=ins + [pl.BlockSpec(memory_space=pltpu.SMEM), pl.BlockSpec(bucket.shape, lambda i, chip_ref: (0, 0))],
        out_specs=outs + [pl.BlockSpec((NH * BLK, 2 * BLK), lambda i, chip_ref: (0, 0))],
        out_shape=shapes + [SDS((NH * BLK, 2 * BLK), F32)],
        sem=("arbitrary",), vmem_mib=32, comm=comm, free=tuple(range(n + 2)))


def kernel(x, norm_mix_g, w_in, conv_w, q_norm_g, k_norm_g, rel_bias_table, sinks, out_norm_conv_g, out_norm_attn_g, w_out, norm_ffn_g, w_up, ffn_conv_w, ffn_conv_b, w_down, loss_target, m_norm_mix_g, m_w_in, m_conv_w, m_q_norm_g, m_k_norm_g, m_rel_bias_table, m_sinks, m_out_norm_conv_g, m_out_norm_attn_g, m_w_out, m_norm_ffn_g, m_w_up, m_ffn_conv_w, m_ffn_conv_b, m_w_down, v_norm_mix_g, v_w_in, v_conv_w, v_q_norm_g, v_k_norm_g, v_rel_bias_table, v_sinks, v_out_norm_conv_g, v_out_norm_attn_g, v_w_out, v_norm_ffn_g, v_w_up, v_ffn_conv_w, v_ffn_conv_b, v_w_down):
    as_arg = lambda i: jnp.reshape(i, (1,)).astype(jnp.int32)
    chip = as_arg(2 * lax.axis_index("x") + lax.axis_index("y"))
    core = as_arg(lax.axis_index("c"))
    me = 2 * chip + core
    xs, tgt = x[0], loss_target[0]
    qg, kg, gco, gao, g1, g2, fb = q_norm_g, k_norm_g, out_norm_conv_g, out_norm_attn_g, norm_mix_g, norm_ffn_g, ffn_conv_b
    pieces = lambda g: g.reshape(N_DEV, g.shape[0] // N_DEV, D)
    whole = lambda f: f.reshape(2 * f.shape[1], D)

    bucket = jnp.asarray(_bucket_table())
    p_in, p_cw, p_fw = _place_first(chip, w_in[0].T, conv_w[0], ffn_conv_w[0])
    p_out, p_up, p_down, bias, w_int, cw_all, fw_all = _place_rest(
        chip, [w_out[0], w_up[0].T, w_down[0]], rel_bias_table.T, bucket,
        comm=[_t_gather(p_in), _t_small_weights(p_cw), _t_small_weights(p_fw)])
    cw8 = jnp.transpose(cw_all, (1, 0, 2)).reshape(8, CW)
    fw8 = jnp.transpose(fw_all, (1, 0, 2)).reshape(8, 2 * DFF)

    early = 3 / 11
    proj, u1, w_out_f, p_up = _inproj(xs, g1, w_int, comm=[_t_gather(p_out), _t_gather(p_up, (0, early))])
    y, w_upt = _mix_fwd(proj, sinks, cw8, qg, kg, gco, gao, bias, comm=[_t_gather(p_up, (early, 1))])
    h1, u2 = _outproj(y, w_out_f, xs, g2)
    up, = _ffn_up(u2, w_upt)
    a, w_down_f = _ffn_act(up, fw8, fb, comm=[_t_gather(p_down)])
    dh2, dh2b, sq = _ffn_down(a, w_down_f, h1, tgt)

    gdbf, = _wgrad("wgrad_down", [a], dh2b, True)
    da, sib_down = _ffn_down_bwd(dh2b, w_down_f, comm=[_t_sibling(pieces(gdbf))])
    pbf_down, own_down = _chip_sum("chip_sum_w_down", pieces(gdbf), sib_down, core, chip)
    dug, duv, dfwg, dfwv, dfbg, dfbv, chips_down = _ffn_act_bwd(up, da, fw8, fb, comm=[_t_chips(pbf_down)])
    fin_down, = _final_sum("final_sum_w_down", own_down, chips_down, core)
    gubf, = _wgrad("wgrad_up", [dug, duv], u2, False)
    dh1, dh1b, dg2, sib_up, fin_down = _norm_matmul_bwd(
        "ffn_up_bwd", [dug, duv], w_upt, [0, DFF], h1, g2, dh2, True, comm=[_t_sibling(pieces(gubf)), _t_swap(fin_down)])
    pbf_up, own_up = _chip_sum("chip_sum_w_up", pieces(gubf), sib_up, core, chip)
    gobf, = _wgrad("wgrad_out", [y], dh1b, True)
    dy, sib_out = _out_bwd(dh1b, w_out_f, comm=[_t_sibling(pieces(gobf))])
    pbf_out, own_out = _chip_sum("chip_sum_w_out", pieces(gobf), sib_out, core, chip)
    dproj, dcw8, dqg, dkg, dgco, dgao, dsink, dbias, chips_up, chips_out = _mix_bwd(
        proj, dy, sinks, cw8, qg, kg, gco, gao, bias, comm=[_t_chips(pbf_up), _t_chips(pbf_out)])
    fin_up, = _final_sum("final_sum_w_up", own_up, chips_up, core)
    tbl_all = _band_bias_bwd(dbias, bucket, me)
    p_all = _pack_small(me, dfwg, dfwv, dfbg, dfbv, dg2, dgco, dgao, dcw8, dqg, dkg, dsink, sq)
    gibf, fin_up, p_all, tbl_all = _wgrad(
        "wgrad_in", [dproj], u1, False, comm=[_t_swap(fin_up), _t_allgather(p_all), _t_allgather(tbl_all)])
    fin_out, sib_in = _final_sum("final_sum_w_out", own_out, chips_out, core, comm=[_t_sibling(pieces(gibf))])
    pbf_in, own_in = _chip_sum("chip_sum_w_in", pieces(gibf), sib_in, core, chip)
    dx, g1_all, chips_in, fin_out = _norm_matmul_bwd(
        "in_bwd", [dproj], w_int, [0], xs, g1, dh1, False, comm=[_t_chips(pbf_in), _t_swap(fin_out)], slot=me)
    fin_in, = _final_sum("final_sum_w_in", own_in, chips_in, core)
    g1_all, fin_in = _comm_call("gather_last", [_t_allgather(g1_all), _t_swap(fin_in)])

    g_w_out, g_w_up, g_w_down = whole(fin_out), whole(fin_up).T, whole(fin_down)
    g_w_down, d_down, nm_down, nv_down = _adamw("adamw_w_down", w_down[0], g_w_down, m_w_down[0], v_w_down[0], 352, True)
    d_up, nm_up, nv_up = _adamw("adamw_w_up", w_up[0], g_w_up, m_w_up[0], v_w_up[0], 256, stage=False)
    g_w_out, d_out, nm_out, nv_out = _adamw("adamw_w_out", w_out[0], g_w_out, m_w_out[0], v_w_out[0], 256, True)
    g_w_in, d_in, nm_in, nv_in = [a.T for a in _adamw(
        "adamw_w_in", w_in[0].T, whole(fin_in), m_w_in[0].T, v_w_in[0].T, INW // N_CHIPS // 3, True)]
    taps = lambda a: jnp.transpose(a, (1, 0, 2))
    sw = [norm_mix_g, taps(conv_w), q_norm_g, k_norm_g, rel_bias_table.T, sinks, out_norm_conv_g, out_norm_attn_g,
          norm_ffn_g, taps(ffn_conv_w), ffn_conv_b]
    smm = [m_norm_mix_g, taps(m_conv_w), m_q_norm_g, m_k_norm_g, m_rel_bias_table.T, m_sinks, m_out_norm_conv_g,
           m_out_norm_attn_g, m_norm_ffn_g, taps(m_ffn_conv_w), m_ffn_conv_b]
    smv = [v_norm_mix_g, taps(v_conv_w), v_q_norm_g, v_k_norm_g, v_rel_bias_table.T, v_sinks, v_out_norm_conv_g,
           v_out_norm_attn_g, v_norm_ffn_g, taps(v_ffn_conv_w), v_ffn_conv_b]
    *small_out, loss = _small_adam(chip, p_all, g1_all, tbl_all, sw, smm, smv)
    sg, sd, snm, snv = [list(r) for r in small_out]
    for r in (sg, sd, snm, snv):
        r[1], r[4], r[9] = taps(r[1]), r[4].T, taps(r[9])

    def order(s, b_in, b_out, b_up, b_down):
        return (s[0], b_in[None], s[1], s[2], s[3], s[4], s[5], s[6], s[7], b_out[None], s[8], b_up[None],
                s[9], s[10], b_down[None])

    return (loss.reshape(()), dx[None],
            *order(sg, g_w_in, g_w_out, g_w_up, g_w_down),
            *order(sd, d_in, d_out, d_up, d_down),
            *order(snm, nm_in, nm_out, nm_up, nm_down),
            *order(snv, nv_in, nv_out, nv_up, nv_down))
```

```python
import functools
import math

import numpy as np

import jax
import jax.numpy as jnp
from jax import lax
from jax.experimental import pallas as pl
from jax.experimental.pallas import tpu as pltpu

F32 = jnp.float32
BF = jnp.bfloat16
SDS = jax.ShapeDtypeStruct

T = 2048
D = 1024
CW = 512
AW = 512
HD = 64
NH = 8
NKV = 2
GQ = 4
INW = 2304
DFF = 2816
BLK = 128
NB = T // BLK
NBUCKET = 32
EPS = 1e-6
NEG_INF = -1e30
N_CHIPS = 4
N_DEV = 8

ADAM_LR = 0.001
ADAM_B1 = 0.9
ADAM_B2 = 0.999
ADAM_EPS = 1e-08
ADAM_WD = 0.01
ADAM_STEP = 10

TM = 512
MIB = 1024 * 1024
MESH = pl.DeviceIdType.MESH
ANY = pl.BlockSpec(memory_space=pl.ANY)

_pcall = pl.pallas_call


def _params(sem=None, vmem_mib=None, collective_id=None):
    kw = {} if collective_id is None else {"collective_id": collective_id}
    if sem is not None:
        kw["dimension_semantics"] = sem
    if vmem_mib is not None:
        kw["vmem_limit_bytes"] = vmem_mib * MIB
    return pltpu.CompilerParams(**kw)


def _resident(shape):
    return pl.BlockSpec(shape, lambda *_: (0,) * len(shape), pipeline_mode=pl.Buffered(1))


def _dot(a, b, ca, cb):
    return lax.dot_general(a, b, (((ca,), (cb,)), ((), ())), preferred_element_type=F32)


def _rms_bwd(dy, x, r, g):
    dg = jnp.sum(dy * (x * r), axis=0, keepdims=True)
    dgx = dy * g
    dx = r * dgx - x * (r * r * r) * jnp.mean(x * dgx, axis=-1, keepdims=True)
    return dx, dg


def _where():
    x, y, c = lax.axis_index("x"), lax.axis_index("y"), lax.axis_index("c")
    return x, y, c, [(1 - x, y), (x, 1 - y), (1 - x, 1 - y)]


def _rcopy(src, dst, ssem, rsem, dev):
    return pltpu.make_async_remote_copy(src_ref=src, dst_ref=dst, send_sem=ssem, recv_sem=rsem, device_id=dev,
                                        device_id_type=MESH)


SIBLING, Y_CHIP, X_CHIP, DIAGONAL_CHIP = 1, 2, 4, 6
OTHER_CHIPS = (Y_CHIP, X_CHIP, DIAGONAL_CHIP)
EVERYONE = tuple(range(1, N_DEV))
BARRIER_OF = {(SIBLING,): 0, (SIBLING, Y_CHIP, X_CHIP): 1, OTHER_CHIPS: 2, (SIBLING,) + OTHER_CHIPS: 3, EVERYONE: 4}


def _peer(rel):
    x, y, c, _ = _where()
    return x ^ ((rel >> 2) & 1), y ^ ((rel >> 1) & 1), c ^ (rel & 1)


class _Task:
    def __init__(self, ins, outs, alias, n_sem, start, finish, middle=None, peers=()):
        self.ins, self.outs, self.alias, self.n_sem, self.start, self.finish = ins, outs, alias, n_sem, start, finish
        self.middle = middle if middle is not None else (lambda *args: None)
        self.peers = peers


def _peers_of(comm):
    return tuple(sorted({p for t in comm for p in t.peers}))


def _enter(comm):
    peers = _peers_of(comm)
    barrier = pltpu.get_barrier_semaphore()
    for rel in peers:
        pl.semaphore_signal(barrier, inc=1, device_id=_peer(rel), device_id_type=MESH)
    pl.semaphore_wait(barrier, len(peers))


ROWS16 = 16


def _t_gather(placed, part=(0, 1)):
    R = placed.shape[0] // N_CHIPS
    q = R // 4
    lo, hi = (round(f * (q // ROWS16)) * ROWS16 for f in part)

    def quarter(chip_index, core, k):
        return pl.ds(pl.multiple_of(chip_index * R + core * 2 * q + k * q + lo, ROWS16), hi - lo)

    def places():
        x, y, c, _ = _where()
        return c, 2 * x + y, 2 * (1 - x) + y, 2 * x + (1 - y), 2 * (1 - x) + (1 - y), (1 - x, y, c), (x, 1 - y, c), (x, y, 1 - c)

    def copy(buf, k, chip_index, core, quart, ss, rs, b, dev):
        window = buf.at[quarter(chip_index, core, quart)]
        return _rcopy(window, window, ss.at[b + k], rs.at[b + k], dev)

    def start(cin, cout, ss, rs, b):
        c, me, _, _, _, x_nbr, y_nbr, _ = places()
        for k, (quart, dev) in enumerate(((0, x_nbr), (1, y_nbr), (1, x_nbr), (0, y_nbr))):
            copy(cout[0], k, me, c, quart, ss, rs, b, dev).start()

    def middle(cin, cout, ss, rs, b):
        c, _, xc, yc, _, x_nbr, y_nbr, sib = places()
        for k, chip_index, quart, dev in ((0, xc, 0, y_nbr), (1, yc, 1, x_nbr)):
            copy(cout[0], k, chip_index, c, quart, ss, rs, b, dev).wait_recv()
            copy(cout[0], 4 + k, chip_index, c, quart, ss, rs, b, dev).start()
            copy(cout[0], 6 + k, chip_index, c, quart, ss, rs, b, sib).start()

    later = ((2, 1, 1), (3, 2, 0), (4, 3, 0), (5, 3, 1))

    def finish(cin, cout, ss, rs, b):
        c, me, xc, yc, dc, _, _, sib = places()
        chip_of = {1: xc, 2: yc, 3: dc}
        for k, whose, quart in later:
            copy(cout[0], k, chip_of[whose], c, quart, ss, rs, b, sib).wait_recv()
            copy(cout[0], 6 + k, chip_of[whose], c, quart, ss, rs, b, sib).start()
        for k, whose, quart in ((0, 1, 0), (1, 2, 1)) + later:
            copy(cout[0], 6 + k, chip_of[whose], 1 - c, quart, ss, rs, b, sib).wait_recv()
        for k in range(12):
            copy(cout[0], k, me, c, 0, ss, rs, b, sib).wait_send()

    return _Task([placed], [SDS(placed.shape, placed.dtype)], [(0, 0)], 12, start, finish, middle, peers=(SIBLING, Y_CHIP, X_CHIP))


def _t_small_weights(buf):
    def start(cin, cout, ss, rs, b):
        x, y, c, chips = _where()
        mine = cout[0].at[2 * x + y]
        for r, (px, py) in enumerate(chips):
            _rcopy(mine, mine, ss.at[b + r], rs.at[b + r], (px, py, c)).start()

    def finish(cin, cout, ss, rs, b):
        x, y, c, chips = _where()
        for r, (px, py) in enumerate(chips):
            got = cout[0].at[2 * px + py]
            _rcopy(got, got, ss.at[b + r], rs.at[b + r], (px, py, c)).wait_recv()
        for r, (px, py) in enumerate(chips):
            mine = cout[0].at[2 * x + y]
            _rcopy(mine, mine, ss.at[b + r], rs.at[b + r], (px, py, c)).wait_send()

    return _Task([buf], [SDS(buf.shape, buf.dtype)], [(0, 0)], 3, start, finish, peers=OTHER_CHIPS)


def _t_sibling(gbf):
    def start(cin, cout, ss, rs, b):
        x, y, c, _ = _where()
        for jj in range(N_CHIPS):
            _rcopy(cin[0].at[2 * jj + (1 - c)], cout[0].at[jj], ss.at[b + jj], rs.at[b + jj], (x, y, 1 - c)).start()

    def finish(cin, cout, ss, rs, b):
        x, y, c, _ = _where()
        for jj in range(N_CHIPS):
            got = cout[0].at[jj]
            _rcopy(got, got, ss.at[b + jj], rs.at[b + jj], (x, y, 1 - c)).wait_recv()
        for jj in range(N_CHIPS):
            got = cout[0].at[jj]
            _rcopy(got, got, ss.at[b + jj], rs.at[b + jj], (x, y, 1 - c)).wait_send()

    return _Task([gbf], [SDS((N_CHIPS,) + gbf.shape[1:], BF)], [], N_CHIPS, start, finish, peers=(SIBLING,))


def _t_chips(pbf):
    def start(cin, cout, ss, rs, b):
        x, y, c, chips = _where()
        for r, (px, py) in enumerate(chips):
            _rcopy(cin[0].at[2 * px + py], cout[0].at[r], ss.at[b + r], rs.at[b + r], (px, py, c)).start()

    def finish(cin, cout, ss, rs, b):
        x, y, c, chips = _where()
        for r, (px, py) in enumerate(chips):
            got = cout[0].at[r]
            _rcopy(got, got, ss.at[b + r], rs.at[b + r], (px, py, c)).wait_recv()
        for r, (px, py) in enumerate(chips):
            got = cout[0].at[r]
            _rcopy(got, got, ss.at[b + r], rs.at[b + r], (px, py, c)).wait_send()

    return _Task([pbf], [SDS((3,) + pbf.shape[1:], BF)], [], 3, start, finish, peers=OTHER_CHIPS)


def _t_swap(fin):
    def start(cin, cout, ss, rs, b):
        x, y, c, _ = _where()
        mine = cout[0].at[c]
        _rcopy(mine, mine, ss.at[b], rs.at[b], (x, y, 1 - c)).start()

    def finish(cin, cout, ss, rs, b):
        x, y, c, _ = _where()
        got = cout[0].at[1 - c]
        _rcopy(got, got, ss.at[b], rs.at[b], (x, y, 1 - c)).wait_recv()
        _rcopy(got, got, ss.at[b], rs.at[b], (x, y, 1 - c)).wait_send()

    return _Task([fin], [SDS(fin.shape, fin.dtype)], [(0, 0)], 1, start, finish, peers=(SIBLING,))


def _t_allgather(buf):
    def peers():
        x, y, c, _ = _where()
        out = []
        for rel in range(1, N_DEV):
            px, py, pc = x ^ ((rel >> 2) & 1), y ^ ((rel >> 1) & 1), c ^ (rel & 1)
            out.append((rel - 1, 4 * px + 2 * py + pc, (px, py, pc)))
        return 4 * x + 2 * y + c, out

    def start(cin, cout, ss, rs, b):
        me, ps = peers()
        mine = cout[0].at[me]
        for k, _, dev in ps:
            _rcopy(mine, mine, ss.at[b + k], rs.at[b + k], dev).start()

    def finish(cin, cout, ss, rs, b):
        me, ps = peers()
        for k, pidx, dev in ps:
            got = cout[0].at[pidx]
            _rcopy(got, got, ss.at[b + k], rs.at[b + k], dev).wait_recv()
        for k, _, dev in ps:
            mine = cout[0].at[me]
            _rcopy(mine, mine, ss.at[b + k], rs.at[b + k], dev).wait_send()

    return _Task([buf], [SDS(buf.shape, buf.dtype)], [(0, 0)], N_DEV - 1, start, finish, peers=EVERYONE)


def _run_tasks(comm, which, cin, cout, ss, rs):
    i0 = o0 = s0 = 0
    for t in comm:
        getattr(t, which)(cin[i0:i0 + len(t.ins)], cout[o0:o0 + len(t.outs)], ss, rs, s0)
        i0, o0, s0 = i0 + len(t.ins), o0 + len(t.outs), s0 + t.n_sem


def _from_hbm(*arrays):
    return [pltpu.with_memory_space_constraint(a, pltpu.HBM) for a in arrays]


def _in_hbm(shapes):
    return [pltpu.HBM(s.shape, s.dtype) for s in shapes]


def _comm_layout(comm, n_in, n_out):
    c_in = [a for t in comm for a in t.ins]
    c_out = [s for t in comm for s in t.outs]
    aliases, i0, o0 = {}, 0, 0
    for t in comm:
        for i, o in t.alias:
            aliases[n_in + i0 + i] = n_out + o0 + o
        i0, o0 = i0 + len(t.ins), o0 + len(t.outs)
    return c_in, c_out, aliases, sum(t.n_sem for t in comm)


def _call(body, operands, *, name, grid, in_specs, out_specs, out_shape, scratch_shapes=(), sem=None, vmem_mib=None, comm=(),
          free=(), prefetch=()):
    operands = [o if s.memory_space == pltpu.SMEM or k in free else pltpu.with_memory_space_constraint(o, pltpu.HBM)
                for k, (o, s) in enumerate(zip(operands, in_specs))]
    n_pre, n_in, n_out, n_scr = len(prefetch), len(in_specs), len(out_specs), len(scratch_shapes)
    c_in, c_out, aliases, n_sem = _comm_layout(comm, n_pre + n_in, n_out)
    sems = [pltpu.SemaphoreType.DMA((n_sem,)), pltpu.SemaphoreType.DMA((n_sem,))] if comm else []

    def wrapped(*refs):
        pre, refs = refs[:n_pre], refs[n_pre:]
        ins, cin = refs[:n_in], refs[n_in:n_in + len(c_in)]
        rest = refs[n_in + len(c_in):]
        outs, cout = rest[:n_out], rest[n_out:n_out + len(c_out)]
        rest = rest[n_out + len(c_out):]
        scr, csem = rest[:n_scr], rest[n_scr:]
        if not comm:
            return body(*pre, *ins, *outs, *scr)
        step = functools.reduce(lambda acc, k: acc * grid[k] + pl.program_id(k), range(len(grid)), 0)
        n_steps = math.prod(grid)

        @pl.when(step == 0)
        def _():
            _enter(comm)
            _run_tasks(comm, "start", cin, cout, *csem)

        pl.when(step == n_steps // 2)(lambda: _run_tasks(comm, "middle", cin, cout, *csem))
        body(*pre, *ins, *outs, *scr)
        pl.when(step == n_steps - 1)(lambda: _run_tasks(comm, "finish", cin, cout, *csem))

    grid_spec = pltpu.PrefetchScalarGridSpec(
        num_scalar_prefetch=n_pre, grid=grid, in_specs=list(in_specs) + [ANY] * len(c_in),
        out_specs=list(out_specs) + [ANY] * len(c_out), scratch_shapes=list(scratch_shapes) + sems)
    return _pcall(
        wrapped, name=name, grid_spec=grid_spec, out_shape=_in_hbm(list(out_shape) + c_out), input_output_aliases=aliases,
        compiler_params=_params(("arbitrary",) * len(grid) if comm else sem, vmem_mib,
                                BARRIER_OF[_peers_of(comm)] if comm else None),
    )(*prefetch, *operands, *_from_hbm(*c_in))


def _comm_call(name, comm):
    c_in, c_out, aliases, n_sem = _comm_layout(comm, 0, 0)

    def body(*refs):
        cin, cout, (ss, rs) = refs[:len(c_in)], refs[len(c_in):len(c_in) + len(c_out)], refs[len(c_in) + len(c_out):]
        _enter(comm)
        for phase in ("start", "middle", "finish"):
            _run_tasks(comm, phase, cin, cout, ss, rs)

    return _pcall(
        body, name=name, in_specs=[ANY] * len(c_in), out_specs=[ANY] * len(c_out), out_shape=_in_hbm(c_out),
        scratch_shapes=[pltpu.SemaphoreType.DMA((n_sem,)), pltpu.SemaphoreType.DMA((n_sem,))],
        input_output_aliases=aliases, compiler_params=_params(collective_id=BARRIER_OF[_peers_of(comm)]),
    )(*_from_hbm(*c_in))


def _inproj(x, g1, w_int, comm=()):
    tm = TM

    def body(x_ref, g_ref, w_ref, proj_ref, u_ref):
        xf = x_ref[...]
        r = lax.rsqrt(jnp.mean(xf * xf, axis=-1, keepdims=True) + EPS)
        u = (xf * r * g_ref[...]).astype(BF)
        u_ref[...] = u
        proj_ref[...] = _dot(u, w_ref[...], 1, 1)

    return _call(
        body, (x, g1, w_int), name="inproj", grid=(T // tm,),
        in_specs=[pl.BlockSpec((tm, D), lambda i: (i, 0)), pl.BlockSpec((1, D), lambda i: (0, 0)),
                  _resident((INW, D))],
        out_specs=[pl.BlockSpec((tm, INW), lambda i: (i, 0)), pl.BlockSpec((tm, D), lambda i: (i, 0))],
        out_shape=[SDS((T, INW), F32), SDS((T, D), BF)], sem=("parallel",), vmem_mib=40, comm=comm, free=(0, 1))


def _outproj(y, w_out, x, g2):
    tm = TM

    def body(y_ref, w_ref, x_ref, g_ref, h1_ref, u2_ref):
        h1 = x_ref[...] + _dot(y_ref[...], w_ref[...], 1, 0)
        h1_ref[...] = h1
        r = lax.rsqrt(jnp.mean(h1 * h1, axis=-1, keepdims=True) + EPS)
        u2_ref[...] = (h1 * r * g_ref[...]).astype(BF)

    return _call(
        body, (y, w_out, x, g2), name="outproj", grid=(T // tm,),
        in_specs=[pl.BlockSpec((tm, D), lambda i: (i, 0)), _resident((D, D)),
                  pl.BlockSpec((tm, D), lambda i: (i, 0)), pl.BlockSpec((1, D), lambda i: (0, 0))],
        out_specs=[pl.BlockSpec((tm, D), lambda i: (i, 0)), pl.BlockSpec((tm, D), lambda i: (i, 0))],
        out_shape=[SDS((T, D), F32), SDS((T, D), BF)], sem=("parallel",), vmem_mib=32, free=(1, 2, 3))


def _ffn_up(u2, w_upt, comm=()):
    tm, tn = 1024, 512

    def body(u_ref, w_ref, o_ref):
        o_ref[...] = _dot(u_ref[...], w_ref[...], 1, 1).astype(BF)

    return _call(
        body, (u2, w_upt), name="ffn_up", grid=(T // tm, 2 * DFF // tn),
        in_specs=[pl.BlockSpec((tm, D), lambda i, j: (i, 0)), pl.BlockSpec((tn, D), lambda i, j: (j, 0))],
        out_specs=[pl.BlockSpec((tm, tn), lambda i, j: (i, j))], out_shape=[SDS((T, 2 * DFF), BF)],
        sem=("parallel", "parallel"), vmem_mib=32, comm=comm, free=(1,))


def _ffn_down(a, w_down, h1, tgt):
    tm = TM

    def body(a_ref, w_ref, h1_ref, t_ref, dh_ref, dhb_ref, l_ref):
        @pl.when(pl.program_id(0) == 0)
        def _():
            l_ref[...] = jnp.zeros_like(l_ref)

        h2 = h1_ref[...] + _dot(a_ref[...], w_ref[...], 1, 0)
        e = h2 - t_ref[...]
        dh = e * (1.0 / D)
        dh_ref[...] = dh
        dhb_ref[...] = dh.astype(BF)
        e2 = jnp.sum((e * e).reshape(tm // 8, 8, D), axis=0)
        acc = e2[:, 0:128]
        for k in range(1, D // 128):
            acc = acc + e2[:, k * 128:(k + 1) * 128]
        l_ref[...] += acc

    return _call(
        body, (a, w_down, h1, tgt), name="ffn_down", grid=(T // tm,),
        in_specs=[pl.BlockSpec((tm, DFF), lambda i: (i, 0)), _resident((DFF, D)),
                  pl.BlockSpec((tm, D), lambda i: (i, 0)), pl.BlockSpec((tm, D), lambda i: (i, 0))],
        out_specs=[pl.BlockSpec((tm, D), lambda i: (i, 0)), pl.BlockSpec((tm, D), lambda i: (i, 0)),
                   pl.BlockSpec((8, 128), lambda i: (0, 0))],
        out_shape=[SDS((T, D), F32), SDS((T, D), BF), SDS((8, 128), F32)], sem=("arbitrary",), vmem_mib=40, free=(2, 3))


def _bucket_table():
    q = np.arange(BLK, dtype=np.int32)[:, None]
    j = np.arange(2 * BLK, dtype=np.int32)[None, :]
    n = np.maximum(q + BLK - j, 0)
    nf = np.maximum(n, 1).astype(np.float32)
    max_exact = NBUCKET // 2
    large = max_exact + (np.log(nf / np.float32(max_exact)) / np.float32(math.log(BLK / max_exact))
                         * np.float32(NBUCKET - max_exact)).astype(np.int32)
    large = np.minimum(large, NBUCKET - 1)
    return np.where(n < max_exact, n, large).astype(np.int32)


def _band_bias_bwd(dbias, bucket, me):
    def body(me_ref, db_ref, bk_ref, o_ref):
        bk = bk_ref[...]
        for b in range(NBUCKET):
            m = bk == b
            for h in range(NH):
                v = jnp.where(m, db_ref[h * BLK:(h + 1) * BLK, :], 0.0)
                s = jnp.sum(jnp.sum(v, axis=1, keepdims=True), axis=0, keepdims=True)
                o_ref[0, h:h + 1, b:b + 1] = s

    grid_spec = pltpu.PrefetchScalarGridSpec(
        num_scalar_prefetch=1, grid=(1,),
        in_specs=[pl.BlockSpec((NH * BLK, 2 * BLK), lambda i, me_ref: (0, 0)),
                  pl.BlockSpec((BLK, 2 * BLK), lambda i, me_ref: (0, 0))],
        out_specs=pl.BlockSpec((1, NH, NBUCKET), lambda i, me_ref: (me_ref[0], 0, 0)),
    )
    return _pcall(body, name="band_bias_bwd", grid_spec=grid_spec, out_shape=SDS((N_DEV, NH, NBUCKET), F32),
                  compiler_params=_params(("arbitrary",)))(me, dbias, bucket)


def _two_bf16(x):
    hi = x.astype(BF)
    return hi, (x - hi.astype(F32)).astype(BF)


def _head_sums(x, seg):
    hi, lo = _two_bf16(x)
    s = seg[0:x.shape[1], :]
    return _dot(hi, s, 1, 0) + _dot(lo, s, 1, 0)


def _head_spread(v, seg, width):
    hi, lo = _two_bf16(v)
    s = seg[0:width, :]
    return _dot(hi, s, 1, 1) + _dot(lo, s, 1, 1)


def _head_norm(x, g_t, seg, by_head=False):
    if by_head:
        heads = [x[:, h * HD:(h + 1) * HD] for h in range(x.shape[1] // HD)]
        r = jnp.concatenate([jnp.broadcast_to(lax.rsqrt(jnp.mean(v * v, axis=-1, keepdims=True) + EPS), v.shape)
                             for v in heads], axis=1)
    else:
        r = lax.rsqrt(_head_sums(x * x, seg) * (1.0 / HD) + EPS)
        r = _head_spread(r, seg, x.shape[1])
    return x * r * g_t, r


def _head_norm_bwd(dy, x, r, g_t, seg):
    dg_t = jnp.sum(dy * (x * r), axis=0, keepdims=True)
    dgx = dy * g_t
    mean = _head_spread(_head_sums(x * dgx, seg) * (1.0 / HD), seg, x.shape[1])
    return r * dgx - x * (r * r * r) * mean, dg_t


def _fold_heads(v):
    out = v[:, 0:HD]
    for h in range(1, v.shape[1] // HD):
        out = out + v[:, h * HD:(h + 1) * HD]
    return out


def _mix_forward(P, zc8, zh8, pkv, first, cw, qg_t, kg_t, gco, gao, seg, sink_ref, bias_ref, by_head=False):
    gate_b = P[:, 0:CW]
    gate_c = P[:, CW:2 * CW]
    hc = P[:, 2 * CW:3 * CW]
    z = gate_c * hc
    keep = jnp.where(first, 0.0, 1.0)
    zp = zc8 * zh8 * keep
    p1 = zp[7:8, :]
    p2 = zp[6:7, :]
    row = lax.broadcasted_iota(jnp.int32, (BLK, 1), 0)
    z1 = jnp.where(row == 0, p1, pltpu.roll(z, 1, 0))
    z2 = jnp.where(row == 0, p2, jnp.where(row == 1, p1, pltpu.roll(z, 2, 0)))
    cz = cw[0:1, :] * z2 + cw[1:2, :] * z1 + cw[2:3, :] * z
    y_conv = gate_b * cz

    scale = HD ** -0.5
    qi = lax.broadcasted_iota(jnp.int32, (GQ * BLK, 2 * BLK), 0) & (BLK - 1)
    kj = lax.broadcasted_iota(jnp.int32, (GQ * BLK, 2 * BLK), 1)
    dd = qi + BLK - kj
    first_key = jnp.where(first, BLK, 0)
    valid = (dd >= 0) & (dd < BLK) & (kj >= first_key)

    q0 = 3 * CW
    k0 = q0 + AW
    v0 = k0 + NKV * HD
    q_raw = P[:, q0:k0]
    qn, rq = _head_norm(q_raw, qg_t, seg, by_head)
    qs = (qn * scale).astype(BF)
    k_raw = jnp.concatenate([pkv[:, 0:NKV * HD], P[:, k0:v0]], axis=0)
    kn, rk = _head_norm(k_raw, kg_t, seg, by_head)
    knb = kn.astype(BF)
    heads = []
    outs = []
    for kv in range(NKV):
        kb = knb[:, kv * HD:(kv + 1) * HD]
        vb = jnp.concatenate([pkv[:, NKV * HD + kv * HD:NKV * HD + (kv + 1) * HD],
                              P[:, v0 + kv * HD:v0 + (kv + 1) * HD]], axis=0).astype(BF)
        Q = jnp.concatenate([qs[:, (kv * GQ + g) * HD:(kv * GQ + g + 1) * HD] for g in range(GQ)], axis=0)
        S = _dot(Q, kb, 1, 1) + bias_ref[kv * GQ * BLK:(kv + 1) * GQ * BLK, :]
        S = jnp.where(valid, S, NEG_INF)
        sink = jnp.concatenate([jnp.full((BLK, 1), sink_ref[0, kv * GQ + g], F32) for g in range(GQ)], axis=0)
        m = jnp.maximum(jnp.max(S, axis=-1, keepdims=True), sink)
        p = jnp.exp(S - m)
        es = jnp.exp(sink - m)
        denom = jnp.sum(p, axis=-1, keepdims=True) + es
        probs = p / denom
        O = _dot(probs.astype(BF), vb, 1, 0)
        heads.append(dict(kb=kb, vb=vb, Q=Q, probs=probs, psink=es / denom, O=O))
        outs += [O[g * BLK:(g + 1) * BLK, :] for g in range(GQ)]
    y_attn = jnp.concatenate(outs, axis=1)

    rc = lax.rsqrt(jnp.mean(y_conv * y_conv, axis=-1, keepdims=True) + EPS)
    ra = lax.rsqrt(jnp.mean(y_attn * y_attn, axis=-1, keepdims=True) + EPS)
    y = jnp.concatenate([y_conv * rc * gco, y_attn * ra * gao], axis=1)
    return dict(gate_b=gate_b, gate_c=gate_c, hc=hc, z=z, z1=z1, z2=z2, cz=cz, y_conv=y_conv, y_attn=y_attn,
                rc=rc, ra=ra, heads=heads, y=y, row=row, scale=scale, q_raw=q_raw, rq=rq, k_raw=k_raw, rk=rk)


BPS = 2
TILE = BPS * BLK
KV0 = 3 * CW + AW


def _mix_in_specs(tile_of):
    return [
        pl.BlockSpec(memory_space=pltpu.SMEM),
        pl.BlockSpec((TILE, INW), lambda s: (tile_of(s), 0)),
        pl.BlockSpec((8, CW), lambda s: (jnp.maximum(tile_of(s) * (TILE // 8) - 1, 0), 1)),
        pl.BlockSpec((8, CW), lambda s: (jnp.maximum(tile_of(s) * (TILE // 8) - 1, 0), 2)),
        pl.BlockSpec((BLK, 2 * NKV * HD), lambda s: (jnp.maximum(tile_of(s) * BPS - 1, 0), KV0 // (2 * NKV * HD))),
    ]


def _block_inputs(tile, b, zc_ref, zh_ref, pkv_ref, first_tile):
    P = tile[b * BLK:(b + 1) * BLK, :]
    if b == 0:
        return P, zc_ref[...], zh_ref[...], pkv_ref[...], first_tile
    lo = b * BLK
    return P, tile[lo - 8:lo, CW:2 * CW], tile[lo - 8:lo, 2 * CW:3 * CW], tile[lo - BLK:lo, KV0:KV0 + 2 * NKV * HD], False


def _mix_param_specs():
    return [
        pl.BlockSpec((8, CW), lambda s: (0, 0)),
        pl.BlockSpec((1, AW), lambda s: (0, 0)),
        pl.BlockSpec((1, NKV * HD), lambda s: (0, 0)),
        pl.BlockSpec((1, CW), lambda s: (0, 0)),
        pl.BlockSpec((1, AW), lambda s: (0, 0)),
        pl.BlockSpec((AW, 128), lambda s: (0, 0)),
        pl.BlockSpec((NH * BLK, 2 * BLK), lambda s: (0, 0)),
    ]


def _mix_params(cw8, qg, kg, gco, gao, bias):
    seg = np.zeros((AW, 128), np.float32)
    seg[np.arange(AW), np.arange(AW) // HD] = 1.0
    return (cw8, jnp.tile(qg, (1, NH)), jnp.tile(kg, (1, NKV)), gco, gao, jnp.asarray(seg, BF), bias)


def _mix_fwd(proj, sinks, cw8, qg, kg, gco, gao, bias, comm=()):
    def body(sink_ref, p_ref, zc_ref, zh_ref, pkv_ref, cw_ref, qg_ref, kg_ref, gco_ref, gao_ref, seg_ref, bias_ref, y_ref):
        tile = p_ref[...]
        for b in range(BPS):
            f = _mix_forward(*_block_inputs(tile, b, zc_ref, zh_ref, pkv_ref, pl.program_id(0) == 0), cw_ref[...],
                             qg_ref[...], kg_ref[...], gco_ref[...], gao_ref[...], seg_ref[...], sink_ref, bias_ref, by_head=True)
            y_ref[b * BLK:(b + 1) * BLK, :] = f["y"].astype(BF)

    return _call(
        body, (sinks, proj, proj, proj, proj, *_mix_params(cw8, qg, kg, gco, gao, bias)), name="mix_fwd", grid=(T // TILE,),
        in_specs=_mix_in_specs(lambda s: s) + _mix_param_specs(),
        out_specs=[pl.BlockSpec((TILE, D), lambda s: (s, 0))], out_shape=[SDS((T, D), BF)],
        sem=("parallel",), vmem_mib=40, comm=comm, free=tuple(range(5, 12)))


def _mix_bwd(proj, dy, sinks, cw8, qg, kg, gco, gao, bias, comm=()):
    n_steps = T // TILE

    def tile_of(s):
        return n_steps - 1 - s

    def body(sink_ref, p_ref, zc_ref, zh_ref, pkv_ref, dy_ref, cw_ref, qg_ref, kg_ref, gco_ref, gao_ref, seg_ref, bias_ref,
             dproj_ref, dcw_ref, dqg_ref, dkg_ref, dgco_ref, dgao_ref, dsink_ref, dbias_ref,
             ndcz_ref, dkc_ref, dvc_ref):
        s = pl.program_id(0)

        @pl.when(s == 0)
        def _():
            for r in (dcw_ref, dqg_ref, dkg_ref, dgco_ref, dgao_ref, dsink_ref, dbias_ref, ndcz_ref, dkc_ref, dvc_ref):
                r[...] = jnp.zeros_like(r)

        params = (cw_ref[...], qg_ref[...], kg_ref[...], gco_ref[...], gao_ref[...], seg_ref[...])
        tile = p_ref[...]
        carry = (ndcz_ref[...], dkc_ref[...], dvc_ref[...])
        total = None
        for b in reversed(range(BPS)):
            f = _mix_forward(*_block_inputs(tile, b, zc_ref, zh_ref, pkv_ref, s == n_steps - 1), *params, sink_ref, bias_ref)
            pieces, sums, carry = one_block(f, dy_ref[b * BLK:(b + 1) * BLK, :], params, carry)
            for lo, piece in pieces:
                dproj_ref[b * BLK:(b + 1) * BLK, lo:lo + piece.shape[1]] = piece
            total = sums if total is None else [t + v for t, v in zip(total, sums)]
        ndcz_ref[...], dkc_ref[...], dvc_ref[...] = carry
        dcw, dqg_t, dkg_t, dgco, dgao, dsink, *ds = total
        dcw_ref[0:3, :] += dcw
        dqg_ref[...] += _fold_heads(dqg_t)
        dkg_ref[...] += _fold_heads(dkg_t)
        dgco_ref[...] += dgco
        dgao_ref[...] += dgao
        dsink_ref[...] += dsink
        for kv in range(NKV):
            dbias_ref[kv * GQ * BLK:(kv + 1) * GQ * BLK, :] += ds[kv]

    def one_block(f, dy, params, carry):
        cw, qg_v, kg_v, gco_v, gao_v, seg = params
        nxt, dk_carry, dv_carry = carry
        dyc, dgco = _rms_bwd(dy[:, 0:CW], f["y_conv"], f["rc"], gco_v)
        dya, dgao = _rms_bwd(dy[:, CW:CW + AW], f["y_attn"], f["ra"], gao_v)

        row = f["row"]
        dgate_b = dyc * f["cz"]
        dcz = dyc * f["gate_b"]
        dcw = jnp.concatenate([jnp.sum(dcz * f[k], axis=0, keepdims=True) for k in ("z2", "z1", "z")], axis=0)
        n0 = nxt[0:1, :]
        n1 = nxt[1:2, :]
        d1 = jnp.where(row == BLK - 1, n0, pltpu.roll(dcz, BLK - 1, 0))
        d2 = jnp.where(row == BLK - 1, n1, jnp.where(row == BLK - 2, n0, pltpu.roll(dcz, BLK - 2, 0)))
        dz = cw[2:3, :] * dcz + cw[1:2, :] * d1 + cw[0:1, :] * d2
        pieces = [(0, dgate_b.astype(BF)), (CW, (dz * f["hc"]).astype(BF)), (2 * CW, (dz * f["gate_c"]).astype(BF))]

        scale = f["scale"]
        lane = lax.broadcasted_iota(jnp.int32, (1, 128), 1)
        dsink = jnp.zeros((1, 128), F32)
        dq_cols, dk_cols, dv_cols, dk_prev, dv_prev, ds = [], [], [], [], [], []
        for kv in range(NKV):
            hd = f["heads"][kv]
            dO = jnp.concatenate([dya[:, (kv * GQ + g) * HD:(kv * GQ + g + 1) * HD] for g in range(GQ)], axis=0)
            delta = jnp.sum(dO * hd["O"], axis=-1, keepdims=True)
            dOb = dO.astype(BF)
            dP = _dot(dOb, hd["vb"], 1, 1)
            dS = hd["probs"] * (dP - delta)
            dsk = hd["psink"] * delta
            for g in range(GQ):
                tot = jnp.sum(dsk[g * BLK:(g + 1) * BLK, :], axis=0, keepdims=True)
                dsink = dsink - jnp.where(lane == kv * GQ + g, tot, 0.0)
            ds.append(dS)
            dSb = dS.astype(BF)
            dQ = _dot(dSb, hd["kb"], 1, 0)
            dKb = _dot(dSb, hd["Q"], 0, 0)
            dVb = _dot(hd["probs"].astype(BF), dOb, 0, 0)
            dk_cols.append(dKb[BLK:, :] + dk_carry[:, kv * HD:(kv + 1) * HD])
            dv_cols.append(dVb[BLK:, :] + dv_carry[:, kv * HD:(kv + 1) * HD])
            dk_prev.append(dKb[:BLK, :])
            dv_prev.append(dVb[:BLK, :])
            dq_cols += [dQ[g * BLK:(g + 1) * BLK, :] for g in range(GQ)]
        dq_raw, dqg_t = _head_norm_bwd(jnp.concatenate(dq_cols, axis=1) * scale, f["q_raw"], f["rq"], qg_v, seg)
        dk_raw, dkg_t = _head_norm_bwd(jnp.concatenate(dk_cols, axis=1), f["k_raw"][BLK:, :], f["rk"][BLK:, :], kg_v, seg)
        pieces.append((3 * CW, jnp.concatenate([dq_raw, dk_raw] + dv_cols, axis=1).astype(BF)))
        owed = (dcz[0:8, :], jnp.concatenate(dk_prev, axis=1), jnp.concatenate(dv_prev, axis=1))
        return pieces, [dcw, dqg_t, dkg_t, dgco, dgao, dsink, *ds], owed

    small = lambda r, c: pl.BlockSpec((r, c), lambda s: (0, 0))
    return _call(
        body, (sinks, proj, proj, proj, proj, dy, *_mix_params(cw8, qg, kg, gco, gao, bias)), name="mix_bwd", grid=(n_steps,),
        in_specs=_mix_in_specs(tile_of) + [pl.BlockSpec((TILE, D), lambda s: (tile_of(s), 0))] + _mix_param_specs(),
        out_specs=[pl.BlockSpec((TILE, INW), lambda s: (tile_of(s), 0)), small(8, CW), small(1, HD), small(1, HD),
                   small(1, CW), small(1, AW), small(1, 128), small(NH * BLK, 2 * BLK)],
        out_shape=[SDS((T, INW), BF), SDS((8, CW), F32), SDS((1, HD), F32), SDS((1, HD), F32), SDS((1, CW), F32),
                   SDS((1, AW), F32), SDS((1, 128), F32), SDS((NH * BLK, 2 * BLK), F32)],
        scratch_shapes=[pltpu.VMEM((8, CW), F32), pltpu.VMEM((BLK, NKV * HD), F32), pltpu.VMEM((BLK, NKV * HD), F32)],
        sem=("arbitrary",), vmem_mib=56, comm=comm, free=(1, 2, 3, 4) + tuple(range(6, 13)))


FT = 256
NFT = DFF // FT
RC = 128
NCH = T // RC
LEAD = 16


def _rows8(x):
    return jnp.sum(x.reshape(x.shape[0] // 8, 8, x.shape[1]), axis=0)


def _ffn_act_specs():
    return [
        pl.BlockSpec((T, FT), lambda j: (0, j)), pl.BlockSpec((T, FT), lambda j: (0, NFT + j)),
        pl.BlockSpec((8, FT), lambda j: (0, j)), pl.BlockSpec((8, FT), lambda j: (0, NFT + j)),
        pl.BlockSpec((1, FT), lambda j: (0, j)), pl.BlockSpec((1, FT), lambda j: (0, NFT + j)),
    ]


def _conv_rows(win, w, b, n):
    win = win.astype(F32)
    u = win[LEAD:LEAD + n]
    u1 = pltpu.roll(win, 1, 0)[LEAD:LEAD + n]
    u2 = pltpu.roll(win, 2, 0)[LEAD:LEAD + n]
    return u2, u1, u, w[0:1, :] * u2 + w[1:2, :] * u1 + w[2:3, :] * u + b


def _ffn_act(up, fw8, fb, comm=()):
    def body(ug_ref, uv_ref, wg_ref, wv_ref, bg_ref, bv_ref, a_ref):
        wg, wv, bg, bv = wg_ref[...], wv_ref[...], bg_ref[...], bv_ref[...]

        def chunk(win_g, win_v):
            gp = _conv_rows(win_g, wg, bg, RC)[3]
            vp = _conv_rows(win_v, wv, bv, RC)[3]
            return (gp * jax.nn.sigmoid(gp) * vp).astype(BF)

        zero = jnp.zeros((LEAD, FT), BF)
        a_ref[0:RC, :] = chunk(jnp.concatenate([zero, ug_ref[0:RC, :]], axis=0),
                               jnp.concatenate([zero, uv_ref[0:RC, :]], axis=0))

        def step(i, carry):
            r0 = pl.multiple_of(i * RC, RC)
            win = pl.ds(r0 - LEAD, RC + LEAD)
            a_ref[pl.ds(r0, RC), :] = chunk(ug_ref[win, :], uv_ref[win, :])
            return carry

        lax.fori_loop(1, NCH, step, 0)

    return _call(
        body, (up, up, fw8, fw8, fb, fb), name="ffn_act", grid=(NFT,), in_specs=_ffn_act_specs(),
        out_specs=[pl.BlockSpec((T, FT), lambda j: (0, j))], out_shape=[SDS((T, DFF), BF)],
        sem=("parallel",), vmem_mib=40, comm=comm, free=(2, 3, 4, 5))


def _ffn_act_bwd(up, da, fw8, fb, comm=()):
    ext = RC + LEAD

    def body(ug_ref, uv_ref, wg_ref, wv_ref, bg_ref, bv_ref, da_ref,
             dug_ref, duv_ref, dwg_ref, dwv_ref, dbg_ref, dbv_ref):
        wg, wv, bg, bv = wg_ref[...], wv_ref[...], bg_ref[...], bv_ref[...]

        def chunk(win_g, win_v, da_e):
            g2, g1, g0, gp = _conv_rows(win_g, wg, bg, ext)
            v2, v1, v0, vp = _conv_rows(win_v, wv, bv, ext)
            da_e = da_e.astype(F32)
            sig = jax.nn.sigmoid(gp)
            dvp = da_e * (gp * sig)
            dgp = da_e * vp * (sig * (1.0 + gp * (1.0 - sig)))

            def back(dp, w):
                return (w[2:3, :] * dp[0:RC] + w[1:2, :] * pltpu.roll(dp, ext - 1, 0)[0:RC]
                        + w[0:1, :] * pltpu.roll(dp, ext - 2, 0)[0:RC]).astype(BF)

            def sums(dp, u2, u1, u0):
                d = dp[0:RC]
                return [_rows8(d), _rows8(d * u2[0:RC]), _rows8(d * u1[0:RC]), _rows8(d * u0[0:RC])]

            return back(dgp, wg), back(dvp, wv), sums(dgp, g2, g1, g0) + sums(dvp, v2, v1, v0)

        zero = jnp.zeros((LEAD, FT), BF)
        dug, duv, acc = chunk(jnp.concatenate([zero, ug_ref[0:ext, :]], axis=0),
                              jnp.concatenate([zero, uv_ref[0:ext, :]], axis=0), da_ref[0:ext, :])
        dug_ref[0:RC, :] = dug
        duv_ref[0:RC, :] = duv

        def step(i, acc):
            r0 = pl.multiple_of(i * RC, RC)
            win = pl.ds(r0 - LEAD, ext + LEAD)
            dug, duv, part = chunk(ug_ref[win, :], uv_ref[win, :], da_ref[pl.ds(r0, ext), :])
            dug_ref[pl.ds(r0, RC), :] = dug
            duv_ref[pl.ds(r0, RC), :] = duv
            return [a + p for a, p in zip(acc, part)]

        acc = lax.fori_loop(1, NCH - 1, step, acc)
        r0 = T - RC
        tail = lambda ref, lo: jnp.concatenate([ref[lo:T, :], zero], axis=0)
        dug, duv, part = chunk(tail(ug_ref, r0 - LEAD), tail(uv_ref, r0 - LEAD), tail(da_ref, r0))
        dug_ref[r0:T, :] = dug
        duv_ref[r0:T, :] = duv
        tot = [jnp.sum(a + p, axis=0, keepdims=True) for a, p in zip(acc, part)]
        for k, (dw_ref, db_ref) in enumerate(((dwg_ref, dbg_ref), (dwv_ref, dbv_ref))):
            db_ref[...] = tot[4 * k]
            dw_ref[...] = jnp.zeros_like(dw_ref)
            for r in range(3):
                dw_ref[r:r + 1, :] = tot[4 * k + 1 + r]

    col = lambda r: pl.BlockSpec((r, FT), lambda j: (0, j))
    return _call(
        body, (up, up, fw8, fw8, fb, fb, da), name="ffn_act_bwd", grid=(NFT,),
        in_specs=_ffn_act_specs() + [pl.BlockSpec((T, FT), lambda j: (0, j))],
        out_specs=[col(T), col(T), col(8), col(8), col(1), col(1)],
        out_shape=[SDS((T, DFF), BF), SDS((T, DFF), BF), SDS((8, DFF), F32), SDS((8, DFF), F32),
                   SDS((1, DFF), F32), SDS((1, DFF), F32)],
        sem=("parallel",), vmem_mib=40, comm=comm, free=(0, 1, 2, 3, 4, 5))


def _ffn_down_bwd(dh2b, w_down, comm=()):
    tm = TM

    def body(d_ref, w_ref, o_ref):
        o_ref[...] = _dot(d_ref[...], w_ref[...], 1, 1).astype(BF)

    return _call(
        body, (dh2b, w_down), name="ffn_down_bwd", grid=(T // tm,),
        in_specs=[pl.BlockSpec((tm, D), lambda i: (i, 0)), _resident((DFF, D))],
        out_specs=[pl.BlockSpec((tm, DFF), lambda i: (i, 0))], out_shape=[SDS((T, DFF), BF)],
        sem=("parallel",), vmem_mib=40, comm=comm, free=(0, 1))


def _norm_matmul_bwd(name, a_list, w_t, k_offsets, xin, g, dres, want_bf16, comm=(), slot=None):
    tm = TM
    ks = [a.shape[1] for a in a_list]
    n_a = len(a_list)
    n_pre = 0 if slot is None else 1

    def body(*refs):
        refs = refs[n_pre:]
        a_refs = refs[:n_a]
        w_ref, x_ref, g_ref, r_ref = refs[n_a:n_a + 4]
        outs = refs[n_a + 4:]
        dx_ref, dg_ref = outs[0], (outs[-1] if slot is None else outs[-1].at[0])

        @pl.when(pl.program_id(0) == 0)
        def _():
            dg_ref[...] = jnp.zeros_like(dg_ref)

        du = _dot(a_refs[0][...], w_ref[k_offsets[0]:k_offsets[0] + ks[0], :], 1, 0)
        for k in range(1, n_a):
            du = du + _dot(a_refs[k][...], w_ref[k_offsets[k]:k_offsets[k] + ks[k], :], 1, 0)
        x = x_ref[...]
        r = lax.rsqrt(jnp.mean(x * x, axis=-1, keepdims=True) + EPS)
        dx, dg = _rms_bwd(du, x, r, g_ref[...])
        dx = r_ref[...] + dx
        dx_ref[...] = dx
        if want_bf16:
            outs[1][...] = dx.astype(BF)
        dg_ref[...] += dg

    tile = lambda c: pl.BlockSpec((tm, c), lambda i, *_: (i, 0))
    if slot is None:
        dg_spec, dg_shape = pl.BlockSpec((1, D), lambda i: (0, 0)), SDS((1, D), F32)
    else:
        dg_spec, dg_shape = pl.BlockSpec((1, 1, D), lambda i, slot_ref: (slot_ref[0], 0, 0)), SDS((N_DEV, 1, D), F32)
    out_specs = [tile(D)] + ([tile(D)] if want_bf16 else []) + [dg_spec]
    out_shape = [SDS((T, D), F32)] + ([SDS((T, D), BF)] if want_bf16 else []) + [dg_shape]
    return _call(
        body, (*a_list, w_t, xin, g, dres), name=name, grid=(T // tm,), prefetch=() if slot is None else (slot,),
        in_specs=[tile(k) for k in ks] + [_resident(w_t.shape), tile(D),
                                           pl.BlockSpec((1, D), lambda i, *_: (0, 0)), tile(D)],
        out_specs=out_specs, out_shape=out_shape, sem=("arbitrary",), vmem_mib=56, comm=comm, free=tuple(range(n_a + 4)))


def _out_bwd(dh1b, w_out, comm=()):
    tm = TM

    def body(d_ref, w_ref, o_ref):
        o_ref[...] = _dot(d_ref[...], w_ref[...], 1, 1)

    return _call(
        body, (dh1b, w_out), name="out_bwd", grid=(T // tm,),
        in_specs=[pl.BlockSpec((tm, D), lambda i: (i, 0)), _resident((D, D))],
        out_specs=[pl.BlockSpec((tm, D), lambda i: (i, 0))], out_shape=[SDS((T, D), F32)],
        sem=("parallel",), vmem_mib=32, comm=comm, free=(0, 1))


def _wgrad(name, a_list, b, old_a, comm=()):
    m_k = a_list[0].shape[1]
    tm = max(t for t in range(128, m_k // 2 + 1, 128) if m_k % t == 0)
    steps = [a.shape[1] // tm for a in a_list]
    starts = [sum(steps[:k]) for k in range(len(a_list))]
    n_a = len(a_list)

    def body(*refs):
        a_refs, b_ref, o_ref = refs[:n_a], refs[n_a], refs[n_a + 1]
        i = pl.program_id(0)
        for k in range(n_a):
            @pl.when((i >= starts[k]) & (i < starts[k] + steps[k]))
            def _(k=k):
                o_ref[...] = _dot(a_refs[k][...], b_ref[...], 0, 0).astype(BF)

    def a_spec(k):
        return pl.BlockSpec((T, tm), lambda i: (0, jnp.clip(i - starts[k], 0, steps[k] - 1)))

    m_total = tm * sum(steps)
    return _call(
        body, (*a_list, b), name=name, grid=(sum(steps),),
        in_specs=[a_spec(k) for k in range(n_a)] + [_resident((T, D))],
        out_specs=[pl.BlockSpec((tm, D), lambda i: (i, 0))], out_shape=[SDS((m_total, D), BF)],
        sem=("parallel",), vmem_mib=40, comm=comm, free=tuple(range(n_a)) if old_a else (n_a,))


def _chip_sum(name, gbf, from_sib, core, chip):
    h = gbf.shape[1]
    th = h // 2

    def body(core_ref, chip_ref, g_ref, s_ref, pbf_ref, own_ref):
        p = g_ref[0].astype(F32) + s_ref[0].astype(F32)
        pbf_ref[0] = p.astype(BF)

        @pl.when(pl.program_id(1) == chip_ref[0])
        def _():
            own_ref[...] = p

    grid_spec = pltpu.PrefetchScalarGridSpec(
        num_scalar_prefetch=2, grid=(h // th, N_CHIPS),
        in_specs=[pl.BlockSpec((1, th, D), lambda t, jj, core_ref, chip_ref: (2 * jj + core_ref[0], t, 0)),
                  pl.BlockSpec((1, th, D), lambda t, jj, core_ref, chip_ref: (jj, t, 0))],
        out_specs=[pl.BlockSpec((1, th, D), lambda t, jj, core_ref, chip_ref: (jj, t, 0)),
                   pl.BlockSpec((th, D), lambda t, jj, core_ref, chip_ref: (t, 0))],
    )
    return _pcall(
        body, name=name, grid_spec=grid_spec, out_shape=_in_hbm([SDS((N_CHIPS, h, D), BF), SDS((h, D), F32)]),
        compiler_params=_params(("arbitrary", "arbitrary"), 32),
    )(core, chip, *_from_hbm(gbf, from_sib))


def _final_sum(name, own, from_chips, core, comm=()):
    h = own.shape[0]

    def body(core_ref, o_ref, r_ref, f_ref):
        f_ref[0] = ((o_ref[...] + r_ref[0].astype(F32)) + r_ref[1].astype(F32)) + r_ref[2].astype(F32)

    return _call(
        body, (own, from_chips), name=name, grid=(1,), prefetch=(core,),
        in_specs=[pl.BlockSpec((h, D), lambda i, core_ref: (0, 0)), pl.BlockSpec((3, h, D), lambda i, core_ref: (0, 0, 0))],
        out_specs=[pl.BlockSpec((1, h, D), lambda i, core_ref: (core_ref[0], 0, 0))], out_shape=[SDS((2, h, D), F32)],
        sem=("arbitrary",), vmem_mib=40, comm=comm)


def _adam_math(w, g, m, v):
    nm = ADAM_B1 * m + (1.0 - ADAM_B1) * g
    nv = ADAM_B2 * v + (1.0 - ADAM_B2) * (g * g)
    m_hat = nm / (1.0 - ADAM_B1 ** ADAM_STEP)
    v_hat = nv / (1.0 - ADAM_B2 ** ADAM_STEP)
    return -ADAM_LR * (m_hat / (jnp.sqrt(v_hat) + ADAM_EPS) + ADAM_WD * w), nm, nv


def _adamw(name, w, g, m, v, tr, copy_g=False, stage=True):
    rows, cols = w.shape

    def body(w_ref, g_ref, m_ref, v_ref, *outs):
        g_val = g_ref[...]
        if copy_g:
            outs[0][...] = g_val
        d_ref, nm_ref, nv_ref = outs[-3:]
        d_ref[...], nm_ref[...], nv_ref[...] = _adam_math(w_ref[...], g_val, m_ref[...], v_ref[...])

    spec = pl.BlockSpec((tr, cols), lambda i: (i, 0))
    n_out = 4 if copy_g else 3
    return _call(body, (w, g, m, v), name=name, grid=(rows // tr,), in_specs=[spec] * 4, out_specs=[spec] * n_out,
                 out_shape=[SDS((rows, cols), F32)] * n_out, sem=("parallel",), vmem_mib=32,
                 free=(0, 2, 3) if stage else ())


C_G1, C_G2, C_GCO, C_GAO, C_DCW, C_DQG, C_DKG, C_SINK, C_SQ = 0, 1024, 2048, 2560, 3072, 4608, 4736, 4864, 5632
P_W = C_SQ + 128


def _pack_small(me, dfwg, dfwv, dfbg, dfbv, dg2, dgco, dgao, dcw8, dqg, dkg, dsink, sq):
    def body(me_ref, dfwg_r, dfwv_r, dfbg_r, dfbv_r, dg2_r, dgco_r, dgao_r, dcw_r, dqg_r, dkg_r, dsink_r, sq_r, o):
        o[...] = jnp.zeros_like(o)
        o[0, :, 0:DFF] = dfwg_r[...]
        o[0, :, DFF:2 * DFF] = dfwv_r[...]
        o[0, 3:4, 0:DFF] = dfbg_r[...]
        o[0, 3:4, DFF:2 * DFF] = dfbv_r[...]
        o[0, 4:5, C_G2:C_G2 + D] = dg2_r[...]
        o[0, 4:5, C_GCO:C_GCO + CW] = dgco_r[...]
        o[0, 4:5, C_GAO:C_GAO + AW] = dgao_r[...]
        for r in range(3):
            o[0, 4:5, C_DCW + r * CW:C_DCW + (r + 1) * CW] = dcw_r[r:r + 1, :]
        o[0, 4:5, C_DQG:C_DQG + HD] = dqg_r[...]
        o[0, 4:5, C_DKG:C_DKG + HD] = dkg_r[...]
        o[0, 4:5, C_SINK:C_SINK + 128] = dsink_r[...]
        o[0, :, C_SQ:C_SQ + 128] = sq_r[...]

    ins = (dfwg, dfwv, dfbg, dfbv, dg2, dgco, dgao, dcw8, dqg, dkg, dsink, sq)
    return _call(body, ins, name="pack_small", grid=(1,), prefetch=(me,),
                 in_specs=[pl.BlockSpec(a.shape, lambda i, me_ref: (0, 0)) for a in ins],
                 out_specs=[pl.BlockSpec((1, 8, P_W), lambda i, me_ref: (me_ref[0], 0, 0))],
                 out_shape=[SDS((N_DEV, 8, P_W), F32)], sem=("arbitrary",))[0]


N_SMALL = 11


def _small_adam(chip, p_all, g1_all, tbl_all, ws, ms, vs):
    fw_cols = 2 * DFF // N_CHIPS
    cw_cols = CW // N_CHIPS

    def body(chip_ref, p_ref, fw_ref, cw0_ref, cw1_ref, cw2_ref, g1_ref, tbl_ref, *refs):
        w_r, m_r, v_r = refs[0:N_SMALL], refs[N_SMALL:2 * N_SMALL], refs[2 * N_SMALL:3 * N_SMALL]
        outs = refs[3 * N_SMALL:]
        g_o, d_o, nm_o, nv_o = (outs[k * N_SMALL:(k + 1) * N_SMALL] for k in range(4))
        loss_o = outs[4 * N_SMALL]

        def total(ref):
            s = ref[0]
            for k in range(1, N_DEV):
                s = s + ref[k]
            return s

        S = total(p_ref)
        fw = total(fw_ref)
        cws = [total(r) for r in (cw0_ref, cw1_ref, cw2_ref)]

        def step(i, g, at):
            d, nm, nv = _adam_math(w_r[i][at], g, m_r[i][at], v_r[i][at])
            g_o[i][at], d_o[i][at], nm_o[i][at], nv_o[i][at] = g, d, nm, nv

        everything = (slice(None), slice(None))
        step(0, total(g1_ref), everything)
        for r in range(3):
            step(1, cws[r][4:5, :], (r, slice(None), slice(None)))
        step(2, S[4:5, C_DQG:C_DQG + HD], everything)
        step(3, S[4:5, C_DKG:C_DKG + HD], everything)
        step(4, total(tbl_ref), everything)
        step(5, S[4:5, C_SINK:C_SINK + NH], everything)
        step(6, S[4:5, C_GCO:C_GCO + CW], everything)
        step(7, S[4:5, C_GAO:C_GAO + AW], everything)
        step(8, S[4:5, C_G2:C_G2 + D], everything)
        for r in range(3):
            step(9, fw[r:r + 1, :], (r, slice(None), slice(None)))
        step(10, S[3:4, 0:2 * DFF], everything)
        sq = S[:, C_SQ:C_SQ + 128]
        loss_o[...] = jnp.sum(jnp.sum(sq, axis=1, keepdims=True), axis=0, keepdims=True) * (0.5 / D)

    def full(a):
        n = len(a.shape)
        return pl.BlockSpec(a.shape, lambda i, chip_ref: (0,) * n)

    params = [*ws, *ms, *vs]
    out = _call(
        body, (p_all, p_all, p_all, p_all, p_all, g1_all, tbl_all, *params), name="small_adam", grid=(1,), prefetch=(chip,),
        in_specs=[full(p_all),
                  pl.BlockSpec((N_DEV, 8, fw_cols), lambda i, chip_ref: (0, 0, chip_ref[0])),
                  *[pl.BlockSpec((N_DEV, 8, cw_cols), lambda i, chip_ref, r=r: (0, 0, (C_DCW + r * CW) // cw_cols + chip_ref[0]))
                    for r in range(3)],
                  full(g1_all), full(tbl_all), *[full(a) for a in params]],
        out_specs=[full(a) for a in ws] * 4 + [pl.BlockSpec((1, 1), lambda i, chip_ref: (0, 0))],
        out_shape=[SDS(a.shape, F32) for a in ws] * 4 + [SDS((1, 1), F32)], sem=("arbitrary",), vmem_mib=32)
    return out[0:N_SMALL], out[N_SMALL:2 * N_SMALL], out[2 * N_SMALL:3 * N_SMALL], out[3 * N_SMALL:4 * N_SMALL], out[4 * N_SMALL]


PLACE_STEPS = 4


def _place_specs(shards):
    rows = [s.shape[0] // PLACE_STEPS for s in shards]
    return ([pl.BlockSpec((r, D), lambda i, chip_ref: (i, 0)) for r in rows],
            [pl.BlockSpec((r, D), lambda i, chip_ref: (chip_ref[0] * PLACE_STEPS + i, 0)) for r in rows],
            [SDS((N_CHIPS * s.shape[0], D), BF) for s in shards])


def _place_first(chip, shard, conv_w, ffn_conv_w):
    def body(chip_ref, a, s0, s1, o, t0, t1):
        o[...] = a[...].astype(BF)

        @pl.when(pl.program_id(0) == 0)
        def _():
            for s, t in ((s0, t0), (s1, t1)):
                t[...] = jnp.zeros_like(t)
                t[0, 0:3, :] = s[...]

    ins, outs, shapes = _place_specs([shard])
    taps = (conv_w, ffn_conv_w)
    return _call(
        body, (shard, conv_w, ffn_conv_w), name="place_first", grid=(PLACE_STEPS,), prefetch=(chip,),
        in_specs=ins + [pl.BlockSpec(s.shape, lambda i, chip_ref: (0, 0)) for s in taps],
        out_specs=outs + [pl.BlockSpec((1, 8, s.shape[1]), lambda i, chip_ref: (chip_ref[0], 0, 0)) for s in taps],
        out_shape=shapes + [SDS((N_CHIPS, 8, s.shape[1]), F32) for s in taps],
        sem=("arbitrary",), vmem_mib=32, free=(0, 1, 2))


def _place_rest(chip, shards, table, bucket, comm):
    n = len(shards)

    def body(chip_ref, *refs):
        a, (tab_ref, bk_ref), o, bias_ref = refs[:n], refs[n:n + 2], refs[n + 2:2 * n + 2], refs[2 * n + 2]
        for src, dst in zip(a, o):
            dst[...] = src[...].astype(BF)

        @pl.when(pl.program_id(0) == 0)
        def _():
            bk = bk_ref[...]
            eq = [bk == b for b in range(NBUCKET)]
            for h in range(NH):
                acc = jnp.zeros((BLK, 2 * BLK), F32)
                for b in range(NBUCKET):
                    acc = jnp.where(eq[b], tab_ref[h, b], acc)
                bias_ref[h * BLK:(h + 1) * BLK, :] = acc

    ins, outs, shapes = _place_specs(shards)
    return _call(
        body, (*shards, table, bucket), name="place_rest", grid=(PLACE_STEPS,), prefetch=(chip,),
        in_specs=ins + [pl.BlockSpec(memory_space=pltpu.SMEM), pl.BlockSpec(bucket.shape, lambda i, chip_ref: (0, 0))],
        out_specs=outs + [pl.BlockSpec((NH * BLK, 2 * BLK), lambda i, chip_ref: (0, 0))],
        out_shape=shapes + [SDS((NH * BLK, 2 * BLK), F32)],
        sem=("arbitrary",), vmem_mib=32, comm=comm, free=tuple(range(n + 2)))


def kernel(x, norm_mix_g, w_in, conv_w, q_norm_g, k_norm_g, rel_bias_table, sinks, out_norm_conv_g, out_norm_attn_g, w_out, norm_ffn_g, w_up, ffn_conv_w, ffn_conv_b, w_down, loss_target, m_norm_mix_g, m_w_in, m_conv_w, m_q_norm_g, m_k_norm_g, m_rel_bias_table, m_sinks, m_out_norm_conv_g, m_out_norm_attn_g, m_w_out, m_norm_ffn_g, m_w_up, m_ffn_conv_w, m_ffn_conv_b, m_w_down, v_norm_mix_g, v_w_in, v_conv_w, v_q_norm_g, v_k_norm_g, v_rel_bias_table, v_sinks, v_out_norm_conv_g, v_out_norm_attn_g, v_w_out, v_norm_ffn_g, v_w_up, v_ffn_conv_w, v_ffn_conv_b, v_w_down):
    as_arg = lambda i: jnp.reshape(i, (1,)).astype(jnp.int32)
    chip = as_arg(2 * lax.axis_index("x") + lax.axis_index("y"))
    core = as_arg(lax.axis_index("c"))
    me = 2 * chip + core
    xs, tgt = x[0], loss_target[0]
    qg, kg, gco, gao, g1, g2, fb = q_norm_g, k_norm_g, out_norm_conv_g, out_norm_attn_g, norm_mix_g, norm_ffn_g, ffn_conv_b
    pieces = lambda g: g.reshape(N_DEV, g.shape[0] // N_DEV, D)
    whole = lambda f: f.reshape(2 * f.shape[1], D)

    bucket = jnp.asarray(_bucket_table())
    p_in, p_cw, p_fw = _place_first(chip, w_in[0].T, conv_w[0], ffn_conv_w[0])
    p_out, p_up, p_down, bias, w_int, cw_all, fw_all = _place_rest(
        chip, [w_out[0], w_up[0].T, w_down[0]], rel_bias_table.T, bucket,
        comm=[_t_gather(p_in), _t_small_weights(p_cw), _t_small_weights(p_fw)])
    cw8 = jnp.transpose(cw_all, (1, 0, 2)).reshape(8, CW)
    fw8 = jnp.transpose(fw_all, (1, 0, 2)).reshape(8, 2 * DFF)

    early = 3 / 11
    proj, u1, w_out_f, p_up = _inproj(xs, g1, w_int, comm=[_t_gather(p_out), _t_gather(p_up, (0, early))])
    y, w_upt = _mix_fwd(proj, sinks, cw8, qg, kg, gco, gao, bias, comm=[_t_gather(p_up, (early, 1))])
    h1, u2 = _outproj(y, w_out_f, xs, g2)
    up, = _ffn_up(u2, w_upt)
    a, w_down_f = _ffn_act(up, fw8, fb, comm=[_t_gather(p_down)])
    dh2, dh2b, sq = _ffn_down(a, w_down_f, h1, tgt)

    gdbf, = _wgrad("wgrad_down", [a], dh2b, True)
    da, sib_down = _ffn_down_bwd(dh2b, w_down_f, comm=[_t_sibling(pieces(gdbf))])
    pbf_down, own_down = _chip_sum("chip_sum_w_down", pieces(gdbf), sib_down, core, chip)
    dug, duv, dfwg, dfwv, dfbg, dfbv, chips_down = _ffn_act_bwd(up, da, fw8, fb, comm=[_t_chips(pbf_down)])
    fin_down, = _final_sum("final_sum_w_down", own_down, chips_down, core)
    gubf, = _wgrad("wgrad_up", [dug, duv], u2, False)
    dh1, dh1b, dg2, sib_up, fin_down = _norm_matmul_bwd(
        "ffn_up_bwd", [dug, duv], w_upt, [0, DFF], h1, g2, dh2, True, comm=[_t_sibling(pieces(gubf)), _t_swap(fin_down)])
    pbf_up, own_up = _chip_sum("chip_sum_w_up", pieces(gubf), sib_up, core, chip)
    gobf, = _wgrad("wgrad_out", [y], dh1b, True)
    dy, sib_out = _out_bwd(dh1b, w_out_f, comm=[_t_sibling(pieces(gobf))])
    pbf_out, own_out = _chip_sum("chip_sum_w_out", pieces(gobf), sib_out, core, chip)
    dproj, dcw8, dqg, dkg, dgco, dgao, dsink, dbias, chips_up, chips_out = _mix_bwd(
        proj, dy, sinks, cw8, qg, kg, gco, gao, bias, comm=[_t_chips(pbf_up), _t_chips(pbf_out)])
    fin_up, = _final_sum("final_sum_w_up", own_up, chips_up, core)
    tbl_all = _band_bias_bwd(dbias, bucket, me)
    p_all = _pack_small(me, dfwg, dfwv, dfbg, dfbv, dg2, dgco, dgao, dcw8, dqg, dkg, dsink, sq)
    gibf, fin_up, p_all, tbl_all = _wgrad(
        "wgrad_in", [dproj], u1, False, comm=[_t_swap(fin_up), _t_allgather(p_all), _t_allgather(tbl_all)])
    fin_out, sib_in = _final_sum("final_sum_w_out", own_out, chips_out, core, comm=[_t_sibling(pieces(gibf))])
    pbf_in, own_in = _chip_sum("chip_sum_w_in", pieces(gibf), sib_in, core, chip)
    dx, g1_all, chips_in, fin_out = _norm_matmul_bwd(
        "in_bwd", [dproj], w_int, [0], xs, g1, dh1, False, comm=[_t_chips(pbf_in), _t_swap(fin_out)], slot=me)
    fin_in, = _final_sum("final_sum_w_in", own_in, chips_in, core)
    g1_all, fin_in = _comm_call("gather_last", [_t_allgather(g1_all), _t_swap(fin_in)])

    g_w_out, g_w_up, g_w_down = whole(fin_out), whole(fin_up).T, whole(fin_down)
    g_w_down, d_down, nm_down, nv_down = _adamw("adamw_w_down", w_down[0], g_w_down, m_w_down[0], v_w_down[0], 352, True)
    d_up, nm_up, nv_up = _adamw("adamw_w_up", w_up[0], g_w_up, m_w_up[0], v_w_up[0], 256, stage=False)
    g_w_out, d_out, nm_out, nv_out = _adamw("adamw_w_out", w_out[0], g_w_out, m_w_out[0], v_w_out[0], 256, True)
    g_w_in, d_in, nm_in, nv_in = [a.T for a in _adamw(
        "adamw_w_in", w_in[0].T, whole(fin_in), m_w_in[0].T, v_w_in[0].T, INW // N_CHIPS // 3, True)]
    taps = lambda a: jnp.transpose(a, (1, 0, 2))
    sw = [norm_mix_g, taps(conv_w), q_norm_g, k_norm_g, rel_bias_table.T, sinks, out_norm_conv_g, out_norm_attn_g,
          norm_ffn_g, taps(ffn_conv_w), ffn_conv_b]
    smm = [m_norm_mix_g, taps(m_conv_w), m_q_norm_g, m_k_norm_g, m_rel_bias_table.T, m_sinks, m_out_norm_conv_g,
           m_out_norm_attn_g, m_norm_ffn_g, taps(m_ffn_conv_w), m_ffn_conv_b]
    smv = [v_norm_mix_g, taps(v_conv_w), v_q_norm_g, v_k_norm_g, v_rel_bias_table.T, v_sinks, v_out_norm_conv_g,
           v_out_norm_attn_g, v_norm_ffn_g, taps(v_ffn_conv_w), v_ffn_conv_b]
    *small_out, loss = _small_adam(chip, p_all, g1_all, tbl_all, sw, smm, smv)
    sg, sd, snm, snv = [list(r) for r in small_out]
    for r in (sg, sd, snm, snv):
        r[1], r[4], r[9] = taps(r[1]), r[4].T, taps(r[9])

    def order(s, b_in, b_out, b_up, b_down):
        return (s[0], b_in[None], s[1], s[2], s[3], s[4], s[5], s[6], s[7], b_out[None], s[8], b_up[None],
                s[9], s[10], b_down[None])

    return (loss.reshape(()), dx[None],
            *order(sg, g_w_in, g_w_out, g_w_up, g_w_down),
            *order(sd, d_in, d_out, d_up, d_down),
            *order(snm, nm_in, nm_out, nm_up, nm_down),
            *order(snv, nv_in, nv_out, nv_up, nv_down))
```

```python
import functools
import math

import numpy as np

import jax
import jax.numpy as jnp
from jax import lax
from jax.experimental import pallas as pl
from jax.experimental.pallas import tpu as pltpu

F32 = jnp.float32
BF = jnp.bfloat16
SDS = jax.ShapeDtypeStruct

T = 2048
D = 1024
CW = 512
AW = 512
HD = 64
NH = 8
NKV = 2
GQ = 4
INW = 2304
DFF = 2816
BLK = 128
NB = T // BLK
NBUCKET = 32
EPS = 1e-6
NEG_INF = -1e30
N_CHIPS = 4
N_DEV = 8

ADAM_LR = 0.001
ADAM_B1 = 0.9
ADAM_B2 = 0.999
ADAM_EPS = 1e-08
ADAM_WD = 0.01
ADAM_STEP = 10

TM = 512
MIB = 1024 * 1024
MESH = pl.DeviceIdType.MESH
ANY = pl.BlockSpec(memory_space=pl.ANY)

_pcall = pl.pallas_call


def _params(sem=None, vmem_mib=None, collective_id=None):
    kw = {} if collective_id is None else {"collective_id": collective_id}
    if sem is not None:
        kw["dimension_semantics"] = sem
    if vmem_mib is not None:
        kw["vmem_limit_bytes"] = vmem_mib * MIB
    return pltpu.CompilerParams(**kw)


def _resident(shape):
    return pl.BlockSpec(shape, lambda *_: (0,) * len(shape), pipeline_mode=pl.Buffered(1))


def _dot(a, b, ca, cb):
    return lax.dot_general(a, b, (((ca,), (cb,)), ((), ())), preferred_element_type=F32)


def _rms_bwd(dy, x, r, g):
    dg = jnp.sum(dy * (x * r), axis=0, keepdims=True)
    dgx = dy * g
    dx = r * dgx - x * (r * r * r) * jnp.mean(x * dgx, axis=-1, keepdims=True)
    return dx, dg


def _where():
    x, y, c = lax.axis_index("x"), lax.axis_index("y"), lax.axis_index("c")
    return x, y, c, [(1 - x, y), (x, 1 - y), (1 - x, 1 - y)]


def _rcopy(src, dst, ssem, rsem, dev):
    return pltpu.make_async_remote_copy(src_ref=src, dst_ref=dst, send_sem=ssem, recv_sem=rsem, device_id=dev,
                                        device_id_type=MESH)


SIBLING, Y_CHIP, X_CHIP, DIAGONAL_CHIP = 1, 2, 4, 6
OTHER_CHIPS = (Y_CHIP, X_CHIP, DIAGONAL_CHIP)
EVERYONE = tuple(range(1, N_DEV))
BARRIER_OF = {(SIBLING,): 0, (SIBLING, Y_CHIP, X_CHIP): 1, OTHER_CHIPS: 2, (SIBLING,) + OTHER_CHIPS: 3, EVERYONE: 4}


def _peer(rel):
    x, y, c, _ = _where()
    return x ^ ((rel >> 2) & 1), y ^ ((rel >> 1) & 1), c ^ (rel & 1)


class _Task:
    def __init__(self, ins, outs, alias, n_sem, start, finish, middle=None, peers=()):
        self.ins, self.outs, self.alias, self.n_sem, self.start, self.finish = ins, outs, alias, n_sem, start, finish
        self.middle = middle if middle is not None else (lambda *args: None)
        self.peers = peers


def _peers_of(comm):
    return tuple(sorted({p for t in comm for p in t.peers}))


def _enter(comm):
    peers = _peers_of(comm)
    barrier = pltpu.get_barrier_semaphore()
    for rel in peers:
        pl.semaphore_signal(barrier, inc=1, device_id=_peer(rel), device_id_type=MESH)
    pl.semaphore_wait(barrier, len(peers))


ROWS16 = 16


def _t_gather(placed, part=(0, 1)):
    R = placed.shape[0] // N_CHIPS
    q = R // 4
    lo, hi = (round(f * (q // ROWS16)) * ROWS16 for f in part)

    def quarter(chip_index, core, k):
        return pl.ds(pl.multiple_of(chip_index * R + core * 2 * q + k * q + lo, ROWS16), hi - lo)

    def places():
        x, y, c, _ = _where()
        return c, 2 * x + y, 2 * (1 - x) + y, 2 * x + (1 - y), 2 * (1 - x) + (1 - y), (1 - x, y, c), (x, 1 - y, c), (x, y, 1 - c)

    def copy(buf, k, chip_index, core, quart, ss, rs, b, dev):
        window = buf.at[quarter(chip_index, core, quart)]
        return _rcopy(window, window, ss.at[b + k], rs.at[b + k], dev)

    def start(cin, cout, ss, rs, b):
        c, me, _, _, _, x_nbr, y_nbr, _ = places()
        for k, (quart, dev) in enumerate(((0, x_nbr), (1, y_nbr), (1, x_nbr), (0, y_nbr))):
            copy(cout[0], k, me, c, quart, ss, rs, b, dev).start()

    def middle(cin, cout, ss, rs, b):
        c, _, xc, yc, _, x_nbr, y_nbr, sib = places()
        for k, chip_index, quart, dev in ((0, xc, 0, y_nbr), (1, yc, 1, x_nbr)):
            copy(cout[0], k, chip_index, c, quart, ss, rs, b, dev).wait_recv()
            copy(cout[0], 4 + k, chip_index, c, quart, ss, rs, b, dev).start()
            copy(cout[0], 6 + k, chip_index, c, quart, ss, rs, b, sib).start()

    later = ((2, 1, 1), (3, 2, 0), (4, 3, 0), (5, 3, 1))

    def finish(cin, cout, ss, rs, b):
        c, me, xc, yc, dc, _, _, sib = places()
        chip_of = {1: xc, 2: yc, 3: dc}
        for k, whose, quart in later:
            copy(cout[0], k, chip_of[whose], c, quart, ss, rs, b, sib).wait_recv()
            copy(cout[0], 6 + k, chip_of[whose], c, quart, ss, rs, b, sib).start()
        for k, whose, quart in ((0, 1, 0), (1, 2, 1)) + later:
            copy(cout[0], 6 + k, chip_of[whose], 1 - c, quart, ss, rs, b, sib).wait_recv()
        for k in range(12):
            copy(cout[0], k, me, c, 0, ss, rs, b, sib).wait_send()

    return _Task([placed], [SDS(placed.shape, placed.dtype)], [(0, 0)], 12, start, finish, middle, peers=(SIBLING, Y_CHIP, X_CHIP))


def _t_small_weights(buf):
    def start(cin, cout, ss, rs, b):
        x, y, c, chips = _where()
        mine = cout[0].at[2 * x + y]
        for r, (px, py) in enumerate(chips):
            _rcopy(mine, mine, ss.at[b + r], rs.at[b + r], (px, py, c)).start()

    def finish(cin, cout, ss, rs, b):
        x, y, c, chips = _where()
        for r, (px, py) in enumerate(chips):
            got = cout[0].at[2 * px + py]
            _rcopy(got, got, ss.at[b + r], rs.at[b + r], (px, py, c)).wait_recv()
        for r, (px, py) in enumerate(chips):
            mine = cout[0].at[2 * x + y]
            _rcopy(mine, mine, ss.at[b + r], rs.at[b + r], (px, py, c)).wait_send()

    return _Task([buf], [SDS(buf.shape, buf.dtype)], [(0, 0)], 3, start, finish, peers=OTHER_CHIPS)


def _t_sibling(gbf):
    def start(cin, cout, ss, rs, b):
        x, y, c, _ = _where()
        for jj in range(N_CHIPS):
            _rcopy(cin[0].at[2 * jj + (1 - c)], cout[0].at[jj], ss.at[b + jj], rs.at[b + jj], (x, y, 1 - c)).start()

    def finish(cin, cout, ss, rs, b):
        x, y, c, _ = _where()
        for jj in range(N_CHIPS):
            got = cout[0].at[jj]
            _rcopy(got, got, ss.at[b + jj], rs.at[b + jj], (x, y, 1 - c)).wait_recv()
        for jj in range(N_CHIPS):
            got = cout[0].at[jj]
            _rcopy(got, got, ss.at[b + jj], rs.at[b + jj], (x, y, 1 - c)).wait_send()

    return _Task([gbf], [SDS((N_CHIPS,) + gbf.shape[1:], BF)], [], N_CHIPS, start, finish, peers=(SIBLING,))


def _t_chips(pbf):
    def start(cin, cout, ss, rs, b):
        x, y, c, chips = _where()
        for r, (px, py) in enumerate(chips):
            _rcopy(cin[0].at[2 * px + py], cout[0].at[r], ss.at[b + r], rs.at[b + r], (px, py, c)).start()

    def finish(cin, cout, ss, rs, b):
        x, y, c, chips = _where()
        for r, (px, py) in enumerate(chips):
            got = cout[0].at[r]
            _rcopy(got, got, ss.at[b + r], rs.at[b + r], (px, py, c)).wait_recv()
        for r, (px, py) in enumerate(chips):
            got = cout[0].at[r]
            _rcopy(got, got, ss.at[b + r], rs.at[b + r], (px, py, c)).wait_send()

    return _Task([pbf], [SDS((3,) + pbf.shape[1:], BF)], [], 3, start, finish, peers=OTHER_CHIPS)


def _t_swap(fin):
    def start(cin, cout, ss, rs, b):
        x, y, c, _ = _where()
        mine = cout[0].at[c]
        _rcopy(mine, mine, ss.at[b], rs.at[b], (x, y, 1 - c)).start()

    def finish(cin, cout, ss, rs, b):
        x, y, c, _ = _where()
        got = cout[0].at[1 - c]
        _rcopy(got, got, ss.at[b], rs.at[b], (x, y, 1 - c)).wait_recv()
        _rcopy(got, got, ss.at[b], rs.at[b], (x, y, 1 - c)).wait_send()

    return _Task([fin], [SDS(fin.shape, fin.dtype)], [(0, 0)], 1, start, finish, peers=(SIBLING,))


def _t_allgather(buf):
    def peers():
        x, y, c, _ = _where()
        out = []
        for rel in range(1, N_DEV):
            px, py, pc = x ^ ((rel >> 2) & 1), y ^ ((rel >> 1) & 1), c ^ (rel & 1)
            out.append((rel - 1, 4 * px + 2 * py + pc, (px, py, pc)))
        return 4 * x + 2 * y + c, out

    def start(cin, cout, ss, rs, b):
        me, ps = peers()
        mine = cout[0].at[me]
        for k, _, dev in ps:
            _rcopy(mine, mine, ss.at[b + k], rs.at[b + k], dev).start()

    def finish(cin, cout, ss, rs, b):
        me, ps = peers()
        for k, pidx, dev in ps:
            got = cout[0].at[pidx]
            _rcopy(got, got, ss.at[b + k], rs.at[b + k], dev).wait_recv()
        for k, _, dev in ps:
            mine = cout[0].at[me]
            _rcopy(mine, mine, ss.at[b + k], rs.at[b + k], dev).wait_send()

    return _Task([buf], [SDS(buf.shape, buf.dtype)], [(0, 0)], N_DEV - 1, start, finish, peers=EVERYONE)


def _run_tasks(comm, which, cin, cout, ss, rs):
    i0 = o0 = s0 = 0
    for t in comm:
        getattr(t, which)(cin[i0:i0 + len(t.ins)], cout[o0:o0 + len(t.outs)], ss, rs, s0)
        i0, o0, s0 = i0 + len(t.ins), o0 + len(t.outs), s0 + t.n_sem


def _from_hbm(*arrays):
    return [pltpu.with_memory_space_constraint(a, pltpu.HBM) for a in arrays]


def _in_hbm(shapes):
    return [pltpu.HBM(s.shape, s.dtype) for s in shapes]


def _comm_layout(comm, n_in, n_out):
    c_in = [a for t in comm for a in t.ins]
    c_out = [s for t in comm for s in t.outs]
    aliases, i0, o0 = {}, 0, 0
    for t in comm:
        for i, o in t.alias:
            aliases[n_in + i0 + i] = n_out + o0 + o
        i0, o0 = i0 + len(t.ins), o0 + len(t.outs)
    return c_in, c_out, aliases, sum(t.n_sem for t in comm)


def _call(body, operands, *, name, grid, in_specs, out_specs, out_shape, scratch_shapes=(), sem=None, vmem_mib=None, comm=(),
          free=(), prefetch=()):
    operands = [o if s.memory_space == pltpu.SMEM or k in free else pltpu.with_memory_space_constraint(o, pltpu.HBM)
                for k, (o, s) in enumerate(zip(operands, in_specs))]
    n_pre, n_in, n_out, n_scr = len(prefetch), len(in_specs), len(out_specs), len(scratch_shapes)
    c_in, c_out, aliases, n_sem = _comm_layout(comm, n_pre + n_in, n_out)
    sems = [pltpu.SemaphoreType.DMA((n_sem,)), pltpu.SemaphoreType.DMA((n_sem,))] if comm else []

    def wrapped(*refs):
        pre, refs = refs[:n_pre], refs[n_pre:]
        ins, cin = refs[:n_in], refs[n_in:n_in + len(c_in)]
        rest = refs[n_in + len(c_in):]
        outs, cout = rest[:n_out], rest[n_out:n_out + len(c_out)]
        rest = rest[n_out + len(c_out):]
        scr, csem = rest[:n_scr], rest[n_scr:]
        if not comm:
            return body(*pre, *ins, *outs, *scr)
        step = functools.reduce(lambda acc, k: acc * grid[k] + pl.program_id(k), range(len(grid)), 0)
        n_steps = math.prod(grid)

        @pl.when(step == 0)
        def _():
            _enter(comm)
            _run_tasks(comm, "start", cin, cout, *csem)

        pl.when(step == n_steps // 2)(lambda: _run_tasks(comm, "middle", cin, cout, *csem))
        body(*pre, *ins, *outs, *scr)
        pl.when(step == n_steps - 1)(lambda: _run_tasks(comm, "finish", cin, cout, *csem))

    grid_spec = pltpu.PrefetchScalarGridSpec(
        num_scalar_prefetch=n_pre, grid=grid, in_specs=list(in_specs) + [ANY] * len(c_in),
        out_specs=list(out_specs) + [ANY] * len(c_out), scratch_shapes=list(scratch_shapes) + sems)
    return _pcall(
        wrapped, name=name, grid_spec=grid_spec, out_shape=_in_hbm(list(out_shape) + c_out), input_output_aliases=aliases,
        compiler_params=_params(("arbitrary",) * len(grid) if comm else sem, vmem_mib,
                                BARRIER_OF[_peers_of(comm)] if comm else None),
    )(*prefetch, *operands, *_from_hbm(*c_in))


def _comm_call(name, comm):
    c_in, c_out, aliases, n_sem = _comm_layout(comm, 0, 0)

    def body(*refs):
        cin, cout, (ss, rs) = refs[:len(c_in)], refs[len(c_in):len(c_in) + len(c_out)], refs[len(c_in) + len(c_out):]
        _enter(comm)
        for phase in ("start", "middle", "finish"):
            _run_tasks(comm, phase, cin, cout, ss, rs)

    return _pcall(
        body, name=name, in_specs=[ANY] * len(c_in), out_specs=[ANY] * len(c_out), out_shape=_in_hbm(c_out),
        scratch_shapes=[pltpu.SemaphoreType.DMA((n_sem,)), pltpu.SemaphoreType.DMA((n_sem,))],
        input_output_aliases=aliases, compiler_params=_params(collective_id=BARRIER_OF[_peers_of(comm)]),
    )(*_from_hbm(*c_in))


def _inproj(x, g1, w_int, comm=()):
    tm = TM

    def body(x_ref, g_ref, w_ref, proj_ref, u_ref):
        xf = x_ref[...]
        r = lax.rsqrt(jnp.mean(xf * xf, axis=-1, keepdims=True) + EPS)
        u = (xf * r * g_ref[...]).astype(BF)
        u_ref[...] = u
        proj_ref[...] = _dot(u, w_ref[...], 1, 1)

    return _call(
        body, (x, g1, w_int), name="inproj", grid=(T // tm,),
        in_specs=[pl.BlockSpec((tm, D), lambda i: (i, 0)), pl.BlockSpec((1, D), lambda i: (0, 0)),
                  _resident((INW, D))],
        out_specs=[pl.BlockSpec((tm, INW), lambda i: (i, 0)), pl.BlockSpec((tm, D), lambda i: (i, 0))],
        out_shape=[SDS((T, INW), F32), SDS((T, D), BF)], sem=("parallel",), vmem_mib=40, comm=comm, free=(0, 1))


def _outproj(y, w_out, x, g2):
    tm = TM

    def body(y_ref, w_ref, x_ref, g_ref, h1_ref, u2_ref):
        h1 = x_ref[...] + _dot(y_ref[...], w_ref[...], 1, 0)
        h1_ref[...] = h1
        r = lax.rsqrt(jnp.mean(h1 * h1, axis=-1, keepdims=True) + EPS)
        u2_ref[...] = (h1 * r * g_ref[...]).astype(BF)

    return _call(
        body, (y, w_out, x, g2), name="outproj", grid=(T // tm,),
        in_specs=[pl.BlockSpec((tm, D), lambda i: (i, 0)), _resident((D, D)),
                  pl.BlockSpec((tm, D), lambda i: (i, 0)), pl.BlockSpec((1, D), lambda i: (0, 0))],
        out_specs=[pl.BlockSpec((tm, D), lambda i: (i, 0)), pl.BlockSpec((tm, D), lambda i: (i, 0))],
        out_shape=[SDS((T, D), F32), SDS((T, D), BF)], sem=("parallel",), vmem_mib=32, free=(1, 2, 3))


def _ffn_up(u2, w_upt, comm=()):
    tm, tn = 1024, 512

    def body(u_ref, w_ref, o_ref):
        o_ref[...] = _dot(u_ref[...], w_ref[...], 1, 1).astype(BF)

    return _call(
        body, (u2, w_upt), name="ffn_up", grid=(T // tm, 2 * DFF // tn),
        in_specs=[pl.BlockSpec((tm, D), lambda i, j: (i, 0)), pl.BlockSpec((tn, D), lambda i, j: (j, 0))],
        out_specs=[pl.BlockSpec((tm, tn), lambda i, j: (i, j))], out_shape=[SDS((T, 2 * DFF), BF)],
        sem=("parallel", "parallel"), vmem_mib=32, comm=comm, free=(1,))


def _ffn_down(a, w_down, h1, tgt):
    tm = TM

    def body(a_ref, w_ref, h1_ref, t_ref, dh_ref, dhb_ref, l_ref):
        @pl.when(pl.program_id(0) == 0)
        def _():
            l_ref[...] = jnp.zeros_like(l_ref)

        h2 = h1_ref[...] + _dot(a_ref[...], w_ref[...], 1, 0)
        e = h2 - t_ref[...]
        dh = e * (1.0 / D)
        dh_ref[...] = dh
        dhb_ref[...] = dh.astype(BF)
        e2 = jnp.sum((e * e).reshape(tm // 8, 8, D), axis=0)
        acc = e2[:, 0:128]
        for k in range(1, D // 128):
            acc = acc + e2[:, k * 128:(k + 1) * 128]
        l_ref[...] += acc

    return _call(
        body, (a, w_down, h1, tgt), name="ffn_down", grid=(T // tm,),
        in_specs=[pl.BlockSpec((tm, DFF), lambda i: (i, 0)), _resident((DFF, D)),
                  pl.BlockSpec((tm, D), lambda i: (i, 0)), pl.BlockSpec((tm, D), lambda i: (i, 0))],
        out_specs=[pl.BlockSpec((tm, D), lambda i: (i, 0)), pl.BlockSpec((tm, D), lambda i: (i, 0)),
                   pl.BlockSpec((8, 128), lambda i: (0, 0))],
        out_shape=[SDS((T, D), F32), SDS((T, D), BF), SDS((8, 128), F32)], sem=("arbitrary",), vmem_mib=40, free=(2, 3))


def _bucket_table():
    q = np.arange(BLK, dtype=np.int32)[:, None]
    j = np.arange(2 * BLK, dtype=np.int32)[None, :]
    n = np.maximum(q + BLK - j, 0)
    nf = np.maximum(n, 1).astype(np.float32)
    max_exact = NBUCKET // 2
    large = max_exact + (np.log(nf / np.float32(max_exact)) / np.float32(math.log(BLK / max_exact))
                         * np.float32(NBUCKET - max_exact)).astype(np.int32)
    large = np.minimum(large, NBUCKET - 1)
    return np.where(n < max_exact, n, large).astype(np.int32)


def _band_bias_bwd(dbias, bucket, me):
    def body(me_ref, db_ref, bk_ref, o_ref):
        bk = bk_ref[...]
        for b in range(NBUCKET):
            m = bk == b
            for h in range(NH):
                v = jnp.where(m, db_ref[h * BLK:(h + 1) * BLK, :], 0.0)
                s = jnp.sum(jnp.sum(v, axis=1, keepdims=True), axis=0, keepdims=True)
                o_ref[0, h:h + 1, b:b + 1] = s

    grid_spec = pltpu.PrefetchScalarGridSpec(
        num_scalar_prefetch=1, grid=(1,),
        in_specs=[pl.BlockSpec((NH * BLK, 2 * BLK), lambda i, me_ref: (0, 0)),
                  pl.BlockSpec((BLK, 2 * BLK), lambda i, me_ref: (0, 0))],
        out_specs=pl.BlockSpec((1, NH, NBUCKET), lambda i, me_ref: (me_ref[0], 0, 0)),
    )
    return _pcall(body, name="band_bias_bwd", grid_spec=grid_spec, out_shape=SDS((N_DEV, NH, NBUCKET), F32),
                  compiler_params=_params(("arbitrary",)))(me, dbias, bucket)


def _two_bf16(x):
    hi = x.astype(BF)
    return hi, (x - hi.astype(F32)).astype(BF)


def _head_sums(x, seg):
    hi, lo = _two_bf16(x)
    s = seg[0:x.shape[1], :]
    return _dot(hi, s, 1, 0) + _dot(lo, s, 1, 0)


def _head_spread(v, seg, width):
    hi, lo = _two_bf16(v)
    s = seg[0:width, :]
    return _dot(hi, s, 1, 1) + _dot(lo, s, 1, 1)


def _head_norm(x, g_t, seg, by_head=False):
    if by_head:
        heads = [x[:, h * HD:(h + 1) * HD] for h in range(x.shape[1] // HD)]
        r = jnp.concatenate([jnp.broadcast_to(lax.rsqrt(jnp.mean(v * v, axis=-1, keepdims=True) + EPS), v.shape)
                             for v in heads], axis=1)
    else:
        r = lax.rsqrt(_head_sums(x * x, seg) * (1.0 / HD) + EPS)
        r = _head_spread(r, seg, x.shape[1])
    return x * r * g_t, r


def _head_norm_bwd(dy, x, r, g_t, seg):
    dg_t = jnp.sum(dy * (x * r), axis=0, keepdims=True)
    dgx = dy * g_t
    mean = _head_spread(_head_sums(x * dgx, seg) * (1.0 / HD), seg, x.shape[1])
    return r * dgx - x * (r * r * r) * mean, dg_t


def _fold_heads(v):
    out = v[:, 0:HD]
    for h in range(1, v.shape[1] // HD):
        out = out + v[:, h * HD:(h + 1) * HD]
    return out


def _mix_forward(P, zc8, zh8, pkv, first, cw, qg_t, kg_t, gco, gao, seg, sink_ref, bias_ref, by_head=False):
    gate_b = P[:, 0:CW]
    gate_c = P[:, CW:2 * CW]
    hc = P[:, 2 * CW:3 * CW]
    z = gate_c * hc
    keep = jnp.where(first, 0.0, 1.0)
    zp = zc8 * zh8 * keep
    p1 = zp[7:8, :]
    p2 = zp[6:7, :]
    row = lax.broadcasted_iota(jnp.int32, (BLK, 1), 0)
    z1 = jnp.where(row == 0, p1, pltpu.roll(z, 1, 0))
    z2 = jnp.where(row == 0, p2, jnp.where(row == 1, p1, pltpu.roll(z, 2, 0)))
    cz = cw[0:1, :] * z2 + cw[1:2, :] * z1 + cw[2:3, :] * z
    y_conv = gate_b * cz

    scale = HD ** -0.5
    qi = lax.broadcasted_iota(jnp.int32, (GQ * BLK, 2 * BLK), 0) & (BLK - 1)
    kj = lax.broadcasted_iota(jnp.int32, (GQ * BLK, 2 * BLK), 1)
    dd = qi + BLK - kj
    first_key = jnp.where(first, BLK, 0)
    valid = (dd >= 0) & (dd < BLK) & (kj >= first_key)

    q0 = 3 * CW
    k0 = q0 + AW
    v0 = k0 + NKV * HD
    q_raw = P[:, q0:k0]
    qn, rq = _head_norm(q_raw, qg_t, seg, by_head)
    qs = (qn * scale).astype(BF)
    k_raw = jnp.concatenate([pkv[:, 0:NKV * HD], P[:, k0:v0]], axis=0)
    kn, rk = _head_norm(k_raw, kg_t, seg, by_head)
    knb = kn.astype(BF)
    heads = []
    outs = []
    for kv in range(NKV):
        kb = knb[:, kv * HD:(kv + 1) * HD]
        vb = jnp.concatenate([pkv[:, NKV * HD + kv * HD:NKV * HD + (kv + 1) * HD],
                              P[:, v0 + kv * HD:v0 + (kv + 1) * HD]], axis=0).astype(BF)
        Q = jnp.concatenate([qs[:, (kv * GQ + g) * HD:(kv * GQ + g + 1) * HD] for g in range(GQ)], axis=0)
        S = _dot(Q, kb, 1, 1) + bias_ref[kv * GQ * BLK:(kv + 1) * GQ * BLK, :]
        S = jnp.where(valid, S, NEG_INF)
        sink = jnp.concatenate([jnp.full((BLK, 1), sink_ref[0, kv * GQ + g], F32) for g in range(GQ)], axis=0)
        m = jnp.maximum(jnp.max(S, axis=-1, keepdims=True), sink)
        p = jnp.exp(S - m)
        es = jnp.exp(sink - m)
        denom = jnp.sum(p, axis=-1, keepdims=True) + es
        probs = p / denom
        O = _dot(probs.astype(BF), vb, 1, 0)
        heads.append(dict(kb=kb, vb=vb, Q=Q, probs=probs, psink=es / denom, O=O))
        outs += [O[g * BLK:(g + 1) * BLK, :] for g in range(GQ)]
    y_attn = jnp.concatenate(outs, axis=1)

    rc = lax.rsqrt(jnp.mean(y_conv * y_conv, axis=-1, keepdims=True) + EPS)
    ra = lax.rsqrt(jnp.mean(y_attn * y_attn, axis=-1, keepdims=True) + EPS)
    y = jnp.concatenate([y_conv * rc * gco, y_attn * ra * gao], axis=1)
    return dict(gate_b=gate_b, gate_c=gate_c, hc=hc, z=z, z1=z1, z2=z2, cz=cz, y_conv=y_conv, y_attn=y_attn,
                rc=rc, ra=ra, heads=heads, y=y, row=row, scale=scale, q_raw=q_raw, rq=rq, k_raw=k_raw, rk=rk)


BPS = 2
TILE = BPS * BLK
KV0 = 3 * CW + AW


def _mix_in_specs(tile_of):
    return [
        pl.BlockSpec(memory_space=pltpu.SMEM),
        pl.BlockSpec((TILE, INW), lambda s: (tile_of(s), 0)),
        pl.BlockSpec((8, CW), lambda s: (jnp.maximum(tile_of(s) * (TILE // 8) - 1, 0), 1)),
        pl.BlockSpec((8, CW), lambda s: (jnp.maximum(tile_of(s) * (TILE // 8) - 1, 0), 2)),
        pl.BlockSpec((BLK, 2 * NKV * HD), lambda s: (jnp.maximum(tile_of(s) * BPS - 1, 0), KV0 // (2 * NKV * HD))),
    ]


def _block_inputs(tile, b, zc_ref, zh_ref, pkv_ref, first_tile):
    P = tile[b * BLK:(b + 1) * BLK, :]
    if b == 0:
        return P, zc_ref[...], zh_ref[...], pkv_ref[...], first_tile
    lo = b * BLK
    return P, tile[lo - 8:lo, CW:2 * CW], tile[lo - 8:lo, 2 * CW:3 * CW], tile[lo - BLK:lo, KV0:KV0 + 2 * NKV * HD], False


def _mix_param_specs():
    return [
        pl.BlockSpec((8, CW), lambda s: (0, 0)),
        pl.BlockSpec((1, AW), lambda s: (0, 0)),
        pl.BlockSpec((1, NKV * HD), lambda s: (0, 0)),
        pl.BlockSpec((1, CW), lambda s: (0, 0)),
        pl.BlockSpec((1, AW), lambda s: (0, 0)),
        pl.BlockSpec((AW, 128), lambda s: (0, 0)),
        pl.BlockSpec((NH * BLK, 2 * BLK), lambda s: (0, 0)),
    ]


def _mix_params(cw8, qg, kg, gco, gao, bias):
    seg = np.zeros((AW, 128), np.float32)
    seg[np.arange(AW), np.arange(AW) // HD] = 1.0
    return (cw8, jnp.tile(qg, (1, NH)), jnp.tile(kg, (1, NKV)), gco, gao, jnp.asarray(seg, BF), bias)


def _mix_fwd(proj, sinks, cw8, qg, kg, gco, gao, bias, comm=()):
    def body(sink_ref, p_ref, zc_ref, zh_ref, pkv_ref, cw_ref, qg_ref, kg_ref, gco_ref, gao_ref, seg_ref, bias_ref, y_ref):
        tile = p_ref[...]
        for b in range(BPS):
            f = _mix_forward(*_block_inputs(tile, b, zc_ref, zh_ref, pkv_ref, pl.program_id(0) == 0), cw_ref[...],
                             qg_ref[...], kg_ref[...], gco_ref[...], gao_ref[...], seg_ref[...], sink_ref, bias_ref, by_head=True)
            y_ref[b * BLK:(b + 1) * BLK, :] = f["y"].astype(BF)

    return _call(
        body, (sinks, proj, proj, proj, proj, *_mix_params(cw8, qg, kg, gco, gao, bias)), name="mix_fwd", grid=(T // TILE,),
        in_specs=_mix_in_specs(lambda s: s) + _mix_param_specs(),
        out_specs=[pl.BlockSpec((TILE, D), lambda s: (s, 0))], out_shape=[SDS((T, D), BF)],
        sem=("parallel",), vmem_mib=40, comm=comm, free=tuple(range(5, 12)))


def _mix_bwd(proj, dy, sinks, cw8, qg, kg, gco, gao, bias, comm=()):
    n_steps = T // TILE

    def tile_of(s):
        return n_steps - 1 - s

    def body(sink_ref, p_ref, zc_ref, zh_ref, pkv_ref, dy_ref, cw_ref, qg_ref, kg_ref, gco_ref, gao_ref, seg_ref, bias_ref,
             dproj_ref, dcw_ref, dqg_ref, dkg_ref, dgco_ref, dgao_ref, dsink_ref, dbias_ref,
             ndcz_ref, dkc_ref, dvc_ref):
        s = pl.program_id(0)

        @pl.when(s == 0)
        def _():
            for r in (dcw_ref, dqg_ref, dkg_ref, dgco_ref, dgao_ref, dsink_ref, dbias_ref, ndcz_ref, dkc_ref, dvc_ref):
                r[...] = jnp.zeros_like(r)

        params = (cw_ref[...], qg_ref[...], kg_ref[...], gco_ref[...], gao_ref[...], seg_ref[...])
        tile = p_ref[...]
        carry = (ndcz_ref[...], dkc_ref[...], dvc_ref[...])
        total = None
        for b in reversed(range(BPS)):
            f = _mix_forward(*_block_inputs(tile, b, zc_ref, zh_ref, pkv_ref, s == n_steps - 1), *params, sink_ref, bias_ref)
            pieces, sums, carry = one_block(f, dy_ref[b * BLK:(b + 1) * BLK, :], params, carry)
            for lo, piece in pieces:
                dproj_ref[b * BLK:(b + 1) * BLK, lo:lo + piece.shape[1]] = piece
            total = sums if total is None else [t + v for t, v in zip(total, sums)]
        ndcz_ref[...], dkc_ref[...], dvc_ref[...] = carry
        dcw, dqg_t, dkg_t, dgco, dgao, dsink, *ds = total
        dcw_ref[0:3, :] += dcw
        dqg_ref[...] += _fold_heads(dqg_t)
        dkg_ref[...] += _fold_heads(dkg_t)
        dgco_ref[...] += dgco
        dgao_ref[...] += dgao
        dsink_ref[...] += dsink
        for kv in range(NKV):
            dbias_ref[kv * GQ * BLK:(kv + 1) * GQ * BLK, :] += ds[kv]

    def one_block(f, dy, params, carry):
        cw, qg_v, kg_v, gco_v, gao_v, seg = params
        nxt, dk_carry, dv_carry = carry
        dyc, dgco = _rms_bwd(dy[:, 0:CW], f["y_conv"], f["rc"], gco_v)
        dya, dgao = _rms_bwd(dy[:, CW:CW + AW], f["y_attn"], f["ra"], gao_v)

        row = f["row"]
        dgate_b = dyc * f["cz"]
        dcz = dyc * f["gate_b"]
        dcw = jnp.concatenate([jnp.sum(dcz * f[k], axis=0, keepdims=True) for k in ("z2", "z1", "z")], axis=0)
        n0 = nxt[0:1, :]
        n1 = nxt[1:2, :]
        d1 = jnp.where(row == BLK - 1, n0, pltpu.roll(dcz, BLK - 1, 0))
        d2 = jnp.where(row == BLK - 1, n1, jnp.where(row == BLK - 2, n0, pltpu.roll(dcz, BLK - 2, 0)))
        dz = cw[2:3, :] * dcz + cw[1:2, :] * d1 + cw[0:1, :] * d2
        pieces = [(0, dgate_b.astype(BF)), (CW, (dz * f["hc"]).astype(BF)), (2 * CW, (dz * f["gate_c"]).astype(BF))]

        scale = f["scale"]
        lane = lax.broadcasted_iota(jnp.int32, (1, 128), 1)
        dsink = jnp.zeros((1, 128), F32)
        dq_cols, dk_cols, dv_cols, dk_prev, dv_prev, ds = [], [], [], [], [], []
        for kv in range(NKV):
            hd = f["heads"][kv]
            dO = jnp.concatenate([dya[:, (kv * GQ + g) * HD:(kv * GQ + g + 1) * HD] for g in range(GQ)], axis=0)
            delta = jnp.sum(dO * hd["O"], axis=-1, keepdims=True)
            dOb = dO.astype(BF)
            dP = _dot(dOb, hd["vb"], 1, 1)
            dS = hd["probs"] * (dP - delta)
            dsk = hd["psink"] * delta
            for g in range(GQ):
                tot = jnp.sum(dsk[g * BLK:(g + 1) * BLK, :], axis=0, keepdims=True)
                dsink = dsink - jnp.where(lane == kv * GQ + g, tot, 0.0)
            ds.append(dS)
            dSb = dS.astype(BF)
            dQ = _dot(dSb, hd["kb"], 1, 0)
            dKb = _dot(dSb, hd["Q"], 0, 0)
            dVb = _dot(hd["probs"].astype(BF), dOb, 0, 0)
            dk_cols.append(dKb[BLK:, :] + dk_carry[:, kv * HD:(kv + 1) * HD])
            dv_cols.append(dVb[BLK:, :] + dv_carry[:, kv * HD:(kv + 1) * HD])
            dk_prev.append(dKb[:BLK, :])
            dv_prev.append(dVb[:BLK, :])
            dq_cols += [dQ[g * BLK:(g + 1) * BLK, :] for g in range(GQ)]
        dq_raw, dqg_t = _head_norm_bwd(jnp.concatenate(dq_cols, axis=1) * scale, f["q_raw"], f["rq"], qg_v, seg)
        dk_raw, dkg_t = _head_norm_bwd(jnp.concatenate(dk_cols, axis=1), f["k_raw"][BLK:, :], f["rk"][BLK:, :], kg_v, seg)
        pieces.append((3 * CW, jnp.concatenate([dq_raw, dk_raw] + dv_cols, axis=1).astype(BF)))
        owed = (dcz[0:8, :], jnp.concatenate(dk_prev, axis=1), jnp.concatenate(dv_prev, axis=1))
        return pieces, [dcw, dqg_t, dkg_t, dgco, dgao, dsink, *ds], owed

    small = lambda r, c: pl.BlockSpec((r, c), lambda s: (0, 0))
    return _call(
        body, (sinks, proj, proj, proj, proj, dy, *_mix_params(cw8, qg, kg, gco, gao, bias)), name="mix_bwd", grid=(n_steps,),
        in_specs=_mix_in_specs(tile_of) + [pl.BlockSpec((TILE, D), lambda s: (tile_of(s), 0))] + _mix_param_specs(),
        out_specs=[pl.BlockSpec((TILE, INW), lambda s: (tile_of(s), 0)), small(8, CW), small(1, HD), small(1, HD),
                   small(1, CW), small(1, AW), small(1, 128), small(NH * BLK, 2 * BLK)],
        out_shape=[SDS((T, INW), BF), SDS((8, CW), F32), SDS((1, HD), F32), SDS((1, HD), F32), SDS((1, CW), F32),
                   SDS((1, AW), F32), SDS((1, 128), F32), SDS((NH * BLK, 2 * BLK), F32)],
        scratch_shapes=[pltpu.VMEM((8, CW), F32), pltpu.VMEM((BLK, NKV * HD), F32), pltpu.VMEM((BLK, NKV * HD), F32)],
        sem=("arbitrary",), vmem_mib=56, comm=comm, free=(1, 2, 3, 4) + tuple(range(6, 13)))


FT = 256
NFT = DFF // FT
RC = 128
NCH = T // RC
LEAD = 16


def _rows8(x):
    return jnp.sum(x.reshape(x.shape[0] // 8, 8, x.shape[1]), axis=0)


def _ffn_act_specs():
    return [
        pl.BlockSpec((T, FT), lambda j: (0, j)), pl.BlockSpec((T, FT), lambda j: (0, NFT + j)),
        pl.BlockSpec((8, FT), lambda j: (0, j)), pl.BlockSpec((8, FT), lambda j: (0, NFT + j)),
        pl.BlockSpec((1, FT), lambda j: (0, j)), pl.BlockSpec((1, FT), lambda j: (0, NFT + j)),
    ]


def _conv_rows(win, w, b, n):
    win = win.astype(F32)
    u = win[LEAD:LEAD + n]
    u1 = pltpu.roll(win, 1, 0)[LEAD:LEAD + n]
    u2 = pltpu.roll(win, 2, 0)[LEAD:LEAD + n]
    return u2, u1, u, w[0:1, :] * u2 + w[1:2, :] * u1 + w[2:3, :] * u + b


def _ffn_act(up, fw8, fb, comm=()):
    def body(ug_ref, uv_ref, wg_ref, wv_ref, bg_ref, bv_ref, a_ref):
        wg, wv, bg, bv = wg_ref[...], wv_ref[...], bg_ref[...], bv_ref[...]

        def chunk(win_g, win_v):
            gp = _conv_rows(win_g, wg, bg, RC)[3]
            vp = _conv_rows(win_v, wv, bv, RC)[3]
            return (gp * jax.nn.sigmoid(gp) * vp).astype(BF)

        zero = jnp.zeros((LEAD, FT), BF)
        a_ref[0:RC, :] = chunk(jnp.concatenate([zero, ug_ref[0:RC, :]], axis=0),
                               jnp.concatenate([zero, uv_ref[0:RC, :]], axis=0))

        def step(i, carry):
            r0 = pl.multiple_of(i * RC, RC)
            win = pl.ds(r0 - LEAD, RC + LEAD)
            a_ref[pl.ds(r0, RC), :] = chunk(ug_ref[win, :], uv_ref[win, :])
            return carry

        lax.fori_loop(1, NCH, step, 0)

    return _call(
        body, (up, up, fw8, fw8, fb, fb), name="ffn_act", grid=(NFT,), in_specs=_ffn_act_specs(),
        out_specs=[pl.BlockSpec((T, FT), lambda j: (0, j))], out_shape=[SDS((T, DFF), BF)],
        sem=("parallel",), vmem_mib=40, comm=comm, free=(2, 3, 4, 5))


def _ffn_act_bwd(up, da, fw8, fb, comm=()):
    ext = RC + LEAD

    def body(ug_ref, uv_ref, wg_ref, wv_ref, bg_ref, bv_ref, da_ref,
             dug_ref, duv_ref, dwg_ref, dwv_ref, dbg_ref, dbv_ref):
        wg, wv, bg, bv = wg_ref[...], wv_ref[...], bg_ref[...], bv_ref[...]

        def chunk(win_g, win_v, da_e):
            g2, g1, g0, gp = _conv_rows(win_g, wg, bg, ext)
            v2, v1, v0, vp = _conv_rows(win_v, wv, bv, ext)
            da_e = da_e.astype(F32)
            sig = jax.nn.sigmoid(gp)
            dvp = da_e * (gp * sig)
            dgp = da_e * vp * (sig * (1.0 + gp * (1.0 - sig)))

            def back(dp, w):
                return (w[2:3, :] * dp[0:RC] + w[1:2, :] * pltpu.roll(dp, ext - 1, 0)[0:RC]
                        + w[0:1, :] * pltpu.roll(dp, ext - 2, 0)[0:RC]).astype(BF)

            def sums(dp, u2, u1, u0):
                d = dp[0:RC]
                return [_rows8(d), _rows8(d * u2[0:RC]), _rows8(d * u1[0:RC]), _rows8(d * u0[0:RC])]

            return back(dgp, wg), back(dvp, wv), sums(dgp, g2, g1, g0) + sums(dvp, v2, v1, v0)

        zero = jnp.zeros((LEAD, FT), BF)
        dug, duv, acc = chunk(jnp.concatenate([zero, ug_ref[0:ext, :]], axis=0),
                              jnp.concatenate([zero, uv_ref[0:ext, :]], axis=0), da_ref[0:ext, :])
        dug_ref[0:RC, :] = dug
        duv_ref[0:RC, :] = duv

        def step(i, acc):
            r0 = pl.multiple_of(i * RC, RC)
            win = pl.ds(r0 - LEAD, ext + LEAD)
            dug, duv, part = chunk(ug_ref[win, :], uv_ref[win, :], da_ref[pl.ds(r0, ext), :])
            dug_ref[pl.ds(r0, RC), :] = dug
            duv_ref[pl.ds(r0, RC), :] = duv
            return [a + p for a, p in zip(acc, part)]

        acc = lax.fori_loop(1, NCH - 1, step, acc)
        r0 = T - RC
        tail = lambda ref, lo: jnp.concatenate([ref[lo:T, :], zero], axis=0)
        dug, duv, part = chunk(tail(ug_ref, r0 - LEAD), tail(uv_ref, r0 - LEAD), tail(da_ref, r0))
        dug_ref[r0:T, :] = dug
        duv_ref[r0:T, :] = duv
        tot = [jnp.sum(a + p, axis=0, keepdims=True) for a, p in zip(acc, part)]
        for k, (dw_ref, db_ref) in enumerate(((dwg_ref, dbg_ref), (dwv_ref, dbv_ref))):
            db_ref[...] = tot[4 * k]
            dw_ref[...] = jnp.zeros_like(dw_ref)
            for r in range(3):
                dw_ref[r:r + 1, :] = tot[4 * k + 1 + r]

    col = lambda r: pl.BlockSpec((r, FT), lambda j: (0, j))
    return _call(
        body, (up, up, fw8, fw8, fb, fb, da), name="ffn_act_bwd", grid=(NFT,),
        in_specs=_ffn_act_specs() + [pl.BlockSpec((T, FT), lambda j: (0, j))],
        out_specs=[col(T), col(T), col(8), col(8), col(1), col(1)],
        out_shape=[SDS((T, DFF), BF), SDS((T, DFF), BF), SDS((8, DFF), F32), SDS((8, DFF), F32),
                   SDS((1, DFF), F32), SDS((1, DFF), F32)],
        sem=("parallel",), vmem_mib=40, comm=comm, free=(0, 1, 2, 3, 4, 5))


def _ffn_down_bwd(dh2b, w_down, comm=()):
    tm = TM

    def body(d_ref, w_ref, o_ref):
        o_ref[...] = _dot(d_ref[...], w_ref[...], 1, 1).astype(BF)

    return _call(
        body, (dh2b, w_down), name="ffn_down_bwd", grid=(T // tm,),
        in_specs=[pl.BlockSpec((tm, D), lambda i: (i, 0)), _resident((DFF, D))],
        out_specs=[pl.BlockSpec((tm, DFF), lambda i: (i, 0))], out_shape=[SDS((T, DFF), BF)],
        sem=("parallel",), vmem_mib=40, comm=comm, free=(0, 1))


def _norm_matmul_bwd(name, a_list, w_t, k_offsets, xin, g, dres, want_bf16, comm=(), slot=None):
    tm = TM
    ks = [a.shape[1] for a in a_list]
    n_a = len(a_list)
    n_pre = 0 if slot is None else 1

    def body(*refs):
        refs = refs[n_pre:]
        a_refs = refs[:n_a]
        w_ref, x_ref, g_ref, r_ref = refs[n_a:n_a + 4]
        outs = refs[n_a + 4:]
        dx_ref, dg_ref = outs[0], (outs[-1] if slot is None else outs[-1].at[0])

        @pl.when(pl.program_id(0) == 0)
        def _():
            dg_ref[...] = jnp.zeros_like(dg_ref)

        du = _dot(a_refs[0][...], w_ref[k_offsets[0]:k_offsets[0] + ks[0], :], 1, 0)
        for k in range(1, n_a):
            du = du + _dot(a_refs[k][...], w_ref[k_offsets[k]:k_offsets[k] + ks[k], :], 1, 0)
        x = x_ref[...]
        r = lax.rsqrt(jnp.mean(x * x, axis=-1, keepdims=True) + EPS)
        dx, dg = _rms_bwd(du, x, r, g_ref[...])
        dx = r_ref[...] + dx
        dx_ref[...] = dx
        if want_bf16:
            outs[1][...] = dx.astype(BF)
        dg_ref[...] += dg

    tile = lambda c: pl.BlockSpec((tm, c), lambda i, *_: (i, 0))
    if slot is None:
        dg_spec, dg_shape = pl.BlockSpec((1, D), lambda i: (0, 0)), SDS((1, D), F32)
    else:
        dg_spec, dg_shape = pl.BlockSpec((1, 1, D), lambda i, slot_ref: (slot_ref[0], 0, 0)), SDS((N_DEV, 1, D), F32)
    out_specs = [tile(D)] + ([tile(D)] if want_bf16 else []) + [dg_spec]
    out_shape = [SDS((T, D), F32)] + ([SDS((T, D), BF)] if want_bf16 else []) + [dg_shape]
    return _call(
        body, (*a_list, w_t, xin, g, dres), name=name, grid=(T // tm,), prefetch=() if slot is None else (slot,),
        in_specs=[tile(k) for k in ks] + [_resident(w_t.shape), tile(D),
                                           pl.BlockSpec((1, D), lambda i, *_: (0, 0)), tile(D)],
        out_specs=out_specs, out_shape=out_shape, sem=("arbitrary",), vmem_mib=56, comm=comm, free=tuple(range(n_a + 4)))


def _out_bwd(dh1b, w_out, comm=()):
    tm = TM

    def body(d_ref, w_ref, o_ref):
        o_ref[...] = _dot(d_ref[...], w_ref[...], 1, 1)

    return _call(
        body, (dh1b, w_out), name="out_bwd", grid=(T // tm,),
        in_specs=[pl.BlockSpec((tm, D), lambda i: (i, 0)), _resident((D, D))],
        out_specs=[pl.BlockSpec((tm, D), lambda i: (i, 0))], out_shape=[SDS((T, D), F32)],
        sem=("parallel",), vmem_mib=32, comm=comm, free=(0, 1))


def _wgrad(name, a_list, b, old_a, comm=()):
    m_k = a_list[0].shape[1]
    tm = max(t for t in range(128, m_k // 2 + 1, 128) if m_k % t == 0)
    steps = [a.shape[1] // tm for a in a_list]
    starts = [sum(steps[:k]) for k in range(len(a_list))]
    n_a = len(a_list)

    def body(*refs):
        a_refs, b_ref, o_ref = refs[:n_a], refs[n_a], refs[n_a + 1]
        i = pl.program_id(0)
        for k in range(n_a):
            @pl.when((i >= starts[k]) & (i < starts[k] + steps[k]))
            def _(k=k):
                o_ref[...] = _dot(a_refs[k][...], b_ref[...], 0, 0).astype(BF)

    def a_spec(k):
        return pl.BlockSpec((T, tm), lambda i: (0, jnp.clip(i - starts[k], 0, steps[k] - 1)))

    m_total = tm * sum(steps)
    return _call(
        body, (*a_list, b), name=name, grid=(sum(steps),),
        in_specs=[a_spec(k) for k in range(n_a)] + [_resident((T, D))],
        out_specs=[pl.BlockSpec((tm, D), lambda i: (i, 0))], out_shape=[SDS((m_total, D), BF)],
        sem=("parallel",), vmem_mib=40, comm=comm, free=() if old_a is None else tuple(range(n_a)) if old_a else (n_a,))


def _chip_sum(name, gbf, from_sib, core, chip):
    h = gbf.shape[1]
    th = h // 2

    def body(core_ref, chip_ref, g_ref, s_ref, pbf_ref, own_ref):
        p = g_ref[0].astype(F32) + s_ref[0].astype(F32)
        pbf_ref[0] = p.astype(BF)

        @pl.when(pl.program_id(1) == chip_ref[0])
        def _():
            own_ref[...] = p

    grid_spec = pltpu.PrefetchScalarGridSpec(
        num_scalar_prefetch=2, grid=(h // th, N_CHIPS),
        in_specs=[pl.BlockSpec((1, th, D), lambda t, jj, core_ref, chip_ref: (2 * jj + core_ref[0], t, 0)),
                  pl.BlockSpec((1, th, D), lambda t, jj, core_ref, chip_ref: (jj, t, 0))],
        out_specs=[pl.BlockSpec((1, th, D), lambda t, jj, core_ref, chip_ref: (jj, t, 0)),
                   pl.BlockSpec((th, D), lambda t, jj, core_ref, chip_ref: (t, 0))],
    )
    return _pcall(
        body, name=name, grid_spec=grid_spec, out_shape=_in_hbm([SDS((N_CHIPS, h, D), BF), SDS((h, D), F32)]),
        compiler_params=_params(("arbitrary", "arbitrary"), 32),
    )(core, chip, *_from_hbm(gbf, from_sib))


def _final_sum(name, own, from_chips, core, comm=()):
    h = own.shape[0]

    def body(core_ref, o_ref, r_ref, f_ref):
        f_ref[0] = ((o_ref[...] + r_ref[0].astype(F32)) + r_ref[1].astype(F32)) + r_ref[2].astype(F32)

    return _call(
        body, (own, from_chips), name=name, grid=(1,), prefetch=(core,),
        in_specs=[pl.BlockSpec((h, D), lambda i, core_ref: (0, 0)), pl.BlockSpec((3, h, D), lambda i, core_ref: (0, 0, 0))],
        out_specs=[pl.BlockSpec((1, h, D), lambda i, core_ref: (core_ref[0], 0, 0))], out_shape=[SDS((2, h, D), F32)],
        sem=("arbitrary",), vmem_mib=40, comm=comm)


def _adam_math(w, g, m, v):
    nm = ADAM_B1 * m + (1.0 - ADAM_B1) * g
    nv = ADAM_B2 * v + (1.0 - ADAM_B2) * (g * g)
    m_hat = nm / (1.0 - ADAM_B1 ** ADAM_STEP)
    v_hat = nv / (1.0 - ADAM_B2 ** ADAM_STEP)
    return -ADAM_LR * (m_hat / (jnp.sqrt(v_hat) + ADAM_EPS) + ADAM_WD * w), nm, nv


def _adamw(name, w, g, m, v, tr, copy_g=False, stage=True, g_transposed=False):
    rows, cols = w.shape

    def body(w_ref, g_ref, m_ref, v_ref, *outs):
        g_val = g_ref[...].T if g_transposed else g_ref[...]
        if copy_g:
            outs[0][...] = g_val
        d_ref, nm_ref, nv_ref = outs[-3:]
        d_ref[...], nm_ref[...], nv_ref[...] = _adam_math(w_ref[...], g_val, m_ref[...], v_ref[...])

    spec = pl.BlockSpec((tr, cols), lambda i: (i, 0))
    n_out = 4 if copy_g else 3
    g_spec = pl.BlockSpec((cols, tr), lambda i: (0, i)) if g_transposed else spec
    return _call(body, (w, g, m, v), name=name, grid=(rows // tr,), in_specs=[spec, g_spec, spec, spec], out_specs=[spec] * n_out,
                 out_shape=[SDS((rows, cols), F32)] * n_out, sem=("parallel",), vmem_mib=32,
                 free=(0, 2, 3) if stage else ())


C_G1, C_G2, C_GCO, C_GAO, C_DCW, C_DQG, C_DKG, C_SINK, C_SQ = 0, 1024, 2048, 2560, 3072, 4608, 4736, 4864, 5632
P_W = C_SQ + 128


def _pack_small(me, dfwg, dfwv, dfbg, dfbv, dg2, dgco, dgao, dcw8, dqg, dkg, dsink, sq):
    def body(me_ref, dfwg_r, dfwv_r, dfbg_r, dfbv_r, dg2_r, dgco_r, dgao_r, dcw_r, dqg_r, dkg_r, dsink_r, sq_r, o):
        o[...] = jnp.zeros_like(o)
        o[0, :, 0:DFF] = dfwg_r[...]
        o[0, :, DFF:2 * DFF] = dfwv_r[...]
        o[0, 3:4, 0:DFF] = dfbg_r[...]
        o[0, 3:4, DFF:2 * DFF] = dfbv_r[...]
        o[0, 4:5, C_G2:C_G2 + D] = dg2_r[...]
        o[0, 4:5, C_GCO:C_GCO + CW] = dgco_r[...]
        o[0, 4:5, C_GAO:C_GAO + AW] = dgao_r[...]
        for r in range(3):
            o[0, 4:5, C_DCW + r * CW:C_DCW + (r + 1) * CW] = dcw_r[r:r + 1, :]
        o[0, 4:5, C_DQG:C_DQG + HD] = dqg_r[...]
        o[0, 4:5, C_DKG:C_DKG + HD] = dkg_r[...]
        o[0, 4:5, C_SINK:C_SINK + 128] = dsink_r[...]
        o[0, :, C_SQ:C_SQ + 128] = sq_r[...]

    ins = (dfwg, dfwv, dfbg, dfbv, dg2, dgco, dgao, dcw8, dqg, dkg, dsink, sq)
    return _call(body, ins, name="pack_small", grid=(1,), prefetch=(me,),
                 in_specs=[pl.BlockSpec(a.shape, lambda i, me_ref: (0, 0)) for a in ins],
                 out_specs=[pl.BlockSpec((1, 8, P_W), lambda i, me_ref: (me_ref[0], 0, 0))],
                 out_shape=[SDS((N_DEV, 8, P_W), F32)], sem=("arbitrary",))[0]


N_SMALL = 11


def _small_adam(chip, p_all, g1_all, tbl_all, ws, ms, vs):
    fw_cols = 2 * DFF // N_CHIPS
    cw_cols = CW // N_CHIPS

    def body(chip_ref, p_ref, fw_ref, cw0_ref, cw1_ref, cw2_ref, g1_ref, tbl_ref, *refs):
        w_r, m_r, v_r = refs[0:N_SMALL], refs[N_SMALL:2 * N_SMALL], refs[2 * N_SMALL:3 * N_SMALL]
        outs = refs[3 * N_SMALL:]
        g_o, d_o, nm_o, nv_o = (outs[k * N_SMALL:(k + 1) * N_SMALL] for k in range(4))
        loss_o = outs[4 * N_SMALL]

        def total(ref):
            s = ref[0]
            for k in range(1, N_DEV):
                s = s + ref[k]
            return s

        S = total(p_ref)
        fw = total(fw_ref)
        cws = [total(r) for r in (cw0_ref, cw1_ref, cw2_ref)]

        def step(i, g, at):
            d, nm, nv = _adam_math(w_r[i][at], g, m_r[i][at], v_r[i][at])
            g_o[i][at], d_o[i][at], nm_o[i][at], nv_o[i][at] = g, d, nm, nv

        everything = (slice(None), slice(None))
        step(0, total(g1_ref), everything)
        for r in range(3):
            step(1, cws[r][4:5, :], (r, slice(None), slice(None)))
        step(2, S[4:5, C_DQG:C_DQG + HD], everything)
        step(3, S[4:5, C_DKG:C_DKG + HD], everything)
        step(4, total(tbl_ref), everything)
        step(5, S[4:5, C_SINK:C_SINK + NH], everything)
        step(6, S[4:5, C_GCO:C_GCO + CW], everything)
        step(7, S[4:5, C_GAO:C_GAO + AW], everything)
        step(8, S[4:5, C_G2:C_G2 + D], everything)
        for r in range(3):
            step(9, fw[r:r + 1, :], (r, slice(None), slice(None)))
        step(10, S[3:4, 0:2 * DFF], everything)
        sq = S[:, C_SQ:C_SQ + 128]
        loss_o[...] = jnp.sum(jnp.sum(sq, axis=1, keepdims=True), axis=0, keepdims=True) * (0.5 / D)

    def full(a):
        n = len(a.shape)
        return pl.BlockSpec(a.shape, lambda i, chip_ref: (0,) * n)

    params = [*ws, *ms, *vs]
    out = _call(
        body, (p_all, p_all, p_all, p_all, p_all, g1_all, tbl_all, *params), name="small_adam", grid=(1,), prefetch=(chip,),
        in_specs=[full(p_all),
                  pl.BlockSpec((N_DEV, 8, fw_cols), lambda i, chip_ref: (0, 0, chip_ref[0])),
                  *[pl.BlockSpec((N_DEV, 8, cw_cols), lambda i, chip_ref, r=r: (0, 0, (C_DCW + r * CW) // cw_cols + chip_ref[0]))
                    for r in range(3)],
                  full(g1_all), full(tbl_all), *[full(a) for a in params]],
        out_specs=[full(a) for a in ws] * 4 + [pl.BlockSpec((1, 1), lambda i, chip_ref: (0, 0))],
        out_shape=[SDS(a.shape, F32) for a in ws] * 4 + [SDS((1, 1), F32)], sem=("arbitrary",), vmem_mib=32)
    return out[0:N_SMALL], out[N_SMALL:2 * N_SMALL], out[2 * N_SMALL:3 * N_SMALL], out[3 * N_SMALL:4 * N_SMALL], out[4 * N_SMALL]


PLACE_STEPS = 4


def _place_specs(shards):
    rows = [s.shape[0] // PLACE_STEPS for s in shards]
    return ([pl.BlockSpec((r, D), lambda i, chip_ref: (i, 0)) for r in rows],
            [pl.BlockSpec((r, D), lambda i, chip_ref: (chip_ref[0] * PLACE_STEPS + i, 0)) for r in rows],
            [SDS((N_CHIPS * s.shape[0], D), BF) for s in shards])


def _place_first(chip, shard, conv_w, ffn_conv_w):
    def body(chip_ref, a, s0, s1, o, t0, t1):
        o[...] = a[...].astype(BF)

        @pl.when(pl.program_id(0) == 0)
        def _():
            for s, t in ((s0, t0), (s1, t1)):
                t[...] = jnp.zeros_like(t)
                t[0, 0:3, :] = s[...]

    ins, outs, shapes = _place_specs([shard])
    taps = (conv_w, ffn_conv_w)
    return _call(
        body, (shard, conv_w, ffn_conv_w), name="place_first", grid=(PLACE_STEPS,), prefetch=(chip,),
        in_specs=ins + [pl.BlockSpec(s.shape, lambda i, chip_ref: (0, 0)) for s in taps],
        out_specs=outs + [pl.BlockSpec((1, 8, s.shape[1]), lambda i, chip_ref: (chip_ref[0], 0, 0)) for s in taps],
        out_shape=shapes + [SDS((N_CHIPS, 8, s.shape[1]), F32) for s in taps],
        sem=("arbitrary",), vmem_mib=32, free=(1, 2))


def _place_rest(chip, shards, w_up, table, bucket, comm):
    n = len(shards)
    c_up = w_up.shape[1]
    edges = [round(k * (c_up // 128) / PLACE_STEPS) * 128 for k in range(PLACE_STEPS + 1)]

    def body(chip_ref, *refs):
        a, (up_ref, tab_ref, bk_ref), o = refs[:n], refs[n:n + 3], refs[n + 3:2 * n + 3]
        up_o, bias_ref = refs[2 * n + 3:]
        for src, dst in zip(a, o):
            dst[...] = src[...].astype(BF)
        for k in range(PLACE_STEPS):
            @pl.when(pl.program_id(0) == k)
            def _(k=k):
                up_o[edges[k]:edges[k + 1], :] = up_ref[:, edges[k]:edges[k + 1]].T.astype(BF)

        @pl.when(pl.program_id(0) == 0)
        def _():
            bk = bk_ref[...]
            eq = [bk == b for b in range(NBUCKET)]
            for h in range(NH):
                acc = jnp.zeros((BLK, 2 * BLK), F32)
                for b in range(NBUCKET):
                    acc = jnp.where(eq[b], tab_ref[h, b], acc)
                bias_ref[h * BLK:(h + 1) * BLK, :] = acc

    ins, outs, shapes = _place_specs(shards)
    return _call(
        body, (*shards, w_up, table, bucket), name="place_rest", grid=(PLACE_STEPS,), prefetch=(chip,),
        in_specs=ins + [_resident(w_up.shape), pl.BlockSpec(memory_space=pltpu.SMEM),
                        pl.BlockSpec(bucket.shape, lambda i, chip_ref: (0, 0))],
        out_specs=outs + [pl.BlockSpec((c_up, D), lambda i, chip_ref: (chip_ref[0], 0)),
                          pl.BlockSpec((NH * BLK, 2 * BLK), lambda i, chip_ref: (0, 0))],
        out_shape=shapes + [SDS((N_CHIPS * c_up, D), BF), SDS((NH * BLK, 2 * BLK), F32)],
        sem=("arbitrary",), vmem_mib=32, comm=comm, free=(n + 1, n + 2))


def kernel(x, norm_mix_g, w_in, conv_w, q_norm_g, k_norm_g, rel_bias_table, sinks, out_norm_conv_g, out_norm_attn_g, w_out, norm_ffn_g, w_up, ffn_conv_w, ffn_conv_b, w_down, loss_target, m_norm_mix_g, m_w_in, m_conv_w, m_q_norm_g, m_k_norm_g, m_rel_bias_table, m_sinks, m_out_norm_conv_g, m_out_norm_attn_g, m_w_out, m_norm_ffn_g, m_w_up, m_ffn_conv_w, m_ffn_conv_b, m_w_down, v_norm_mix_g, v_w_in, v_conv_w, v_q_norm_g, v_k_norm_g, v_rel_bias_table, v_sinks, v_out_norm_conv_g, v_out_norm_attn_g, v_w_out, v_norm_ffn_g, v_w_up, v_ffn_conv_w, v_ffn_conv_b, v_w_down):
    as_arg = lambda i: jnp.reshape(i, (1,)).astype(jnp.int32)
    chip = as_arg(2 * lax.axis_index("x") + lax.axis_index("y"))
    core = as_arg(lax.axis_index("c"))
    me = 2 * chip + core
    xs, tgt = x[0], loss_target[0]
    qg, kg, gco, gao, g1, g2, fb = q_norm_g, k_norm_g, out_norm_conv_g, out_norm_attn_g, norm_mix_g, norm_ffn_g, ffn_conv_b
    pieces = lambda g: g.reshape(N_DEV, g.shape[0] // N_DEV, D)
    whole = lambda f: f.reshape(2 * f.shape[1], D)

    bucket = jnp.asarray(_bucket_table())
    p_in, p_cw, p_fw = _place_first(chip, w_in[0].T, conv_w[0], ffn_conv_w[0])
    p_out, p_down, p_up, bias, w_int, cw_all, fw_all = _place_rest(
        chip, [w_out[0], w_down[0]], w_up[0], rel_bias_table.T, bucket,
        comm=[_t_gather(p_in), _t_small_weights(p_cw), _t_small_weights(p_fw)])
    cw8 = jnp.transpose(cw_all, (1, 0, 2)).reshape(8, CW)
    fw8 = jnp.transpose(fw_all, (1, 0, 2)).reshape(8, 2 * DFF)

    early = 3 / 11
    proj, u1, w_out_f, p_up = _inproj(xs, g1, w_int, comm=[_t_gather(p_out), _t_gather(p_up, (0, early))])
    y, w_upt = _mix_fwd(proj, sinks, cw8, qg, kg, gco, gao, bias, comm=[_t_gather(p_up, (early, 1))])
    h1, u2 = _outproj(y, w_out_f, xs, g2)
    up, = _ffn_up(u2, w_upt)
    a, w_down_f = _ffn_act(up, fw8, fb, comm=[_t_gather(p_down)])
    dh2, dh2b, sq = _ffn_down(a, w_down_f, h1, tgt)

    gdbf, = _wgrad("wgrad_down", [a], dh2b, None)
    da, sib_down = _ffn_down_bwd(dh2b, w_down_f, comm=[_t_sibling(pieces(gdbf))])
    pbf_down, own_down = _chip_sum("chip_sum_w_down", pieces(gdbf), sib_down, core, chip)
    dug, duv, dfwg, dfwv, dfbg, dfbv, chips_down = _ffn_act_bwd(up, da, fw8, fb, comm=[_t_chips(pbf_down)])
    fin_down, = _final_sum("final_sum_w_down", own_down, chips_down, core)
    gubf, = _wgrad("wgrad_up", [dug, duv], u2, False)
    dh1, dh1b, dg2, sib_up, fin_down = _norm_matmul_bwd(
        "ffn_up_bwd", [dug, duv], w_upt, [0, DFF], h1, g2, dh2, True, comm=[_t_sibling(pieces(gubf)), _t_swap(fin_down)])
    pbf_up, own_up = _chip_sum("chip_sum_w_up", pieces(gubf), sib_up, core, chip)
    gobf, = _wgrad("wgrad_out", [y], dh1b, True)
    dy, sib_out = _out_bwd(dh1b, w_out_f, comm=[_t_sibling(pieces(gobf))])
    pbf_out, own_out = _chip_sum("chip_sum_w_out", pieces(gobf), sib_out, core, chip)
    dproj, dcw8, dqg, dkg, dgco, dgao, dsink, dbias, chips_up, chips_out = _mix_bwd(
        proj, dy, sinks, cw8, qg, kg, gco, gao, bias, comm=[_t_chips(pbf_up), _t_chips(pbf_out)])
    fin_up, = _final_sum("final_sum_w_up", own_up, chips_up, core)
    tbl_all = _band_bias_bwd(dbias, bucket, me)
    p_all = _pack_small(me, dfwg, dfwv, dfbg, dfbv, dg2, dgco, dgao, dcw8, dqg, dkg, dsink, sq)
    gibf, fin_up, p_all, tbl_all = _wgrad(
        "wgrad_in", [dproj], u1, False, comm=[_t_swap(fin_up), _t_allgather(p_all), _t_allgather(tbl_all)])
    fin_out, sib_in = _final_sum("final_sum_w_out", own_out, chips_out, core, comm=[_t_sibling(pieces(gibf))])
    pbf_in, own_in = _chip_sum("chip_sum_w_in", pieces(gibf), sib_in, core, chip)
    dx, g1_all, chips_in, fin_out = _norm_matmul_bwd(
        "in_bwd", [dproj], w_int, [0], xs, g1, dh1, False, comm=[_t_chips(pbf_in), _t_swap(fin_out)], slot=me)
    fin_in, = _final_sum("final_sum_w_in", own_in, chips_in, core)
    g1_all, fin_in = _comm_call("gather_last", [_t_allgather(g1_all), _t_swap(fin_in)])

    g_w_out, g_w_down = whole(fin_out), whole(fin_down)
    g_w_down, d_down, nm_down, nv_down = _adamw("adamw_w_down", w_down[0], g_w_down, m_w_down[0], v_w_down[0], 352, True)
    g_w_up, d_up, nm_up, nv_up = _adamw(
        "adamw_w_up", w_up[0], whole(fin_up), m_w_up[0], v_w_up[0], 256, True, stage=False, g_transposed=True)
    g_w_out, d_out, nm_out, nv_out = _adamw("adamw_w_out", w_out[0], g_w_out, m_w_out[0], v_w_out[0], 256, True)
    g_w_in, d_in, nm_in, nv_in = [a.T for a in _adamw(
        "adamw_w_in", w_in[0].T, whole(fin_in), m_w_in[0].T, v_w_in[0].T, INW // N_CHIPS // 3, True)]
    taps = lambda a: jnp.transpose(a, (1, 0, 2))
    sw = [norm_mix_g, taps(conv_w), q_norm_g, k_norm_g, rel_bias_table.T, sinks, out_norm_conv_g, out_norm_attn_g,
          norm_ffn_g, taps(ffn_conv_w), ffn_conv_b]
    smm = [m_norm_mix_g, taps(m_conv_w), m_q_norm_g, m_k_norm_g, m_rel_bias_table.T, m_sinks, m_out_norm_conv_g,
           m_out_norm_attn_g, m_norm_ffn_g, taps(m_ffn_conv_w), m_ffn_conv_b]
    smv = [v_norm_mix_g, taps(v_conv_w), v_q_norm_g, v_k_norm_g, v_rel_bias_table.T, v_sinks, v_out_norm_conv_g,
           v_out_norm_attn_g, v_norm_ffn_g, taps(v_ffn_conv_w), v_ffn_conv_b]
    *small_out, loss = _small_adam(chip, p_all, g1_all, tbl_all, sw, smm, smv)
    sg, sd, snm, snv = [list(r) for r in small_out]
    for r in (sg, sd, snm, snv):
        r[1], r[4], r[9] = taps(r[1]), r[4].T, taps(r[9])

    def order(s, b_in, b_out, b_up, b_down):
        return (s[0], b_in[None], s[1], s[2], s[3], s[4], s[5], s[6], s[7], b_out[None], s[8], b_up[None],
                s[9], s[10], b_down[None])

    return (loss.reshape(()), dx[None],
            *order(sg, g_w_in, g_w_out, g_w_up, g_w_down),
            *order(sd, d_in, d_out, d_up, d_down),
            *order(snm, nm_in, nm_out, nm_up, nm_down),
            *order(snv, nv_in, nv_out, nv_up, nv_down))
```

```python
import functools
import math

import numpy as np

import jax
import jax.numpy as jnp
from jax import lax
from jax.experimental import pallas as pl
from jax.experimental.pallas import tpu as pltpu

F32 = jnp.float32
BF = jnp.bfloat16
SDS = jax.ShapeDtypeStruct

T = 2048
D = 1024
CW = 512
AW = 512
HD = 64
NH = 8
NKV = 2
GQ = 4
INW = 2304
DFF = 2816
BLK = 128
NB = T // BLK
NBUCKET = 32
EPS = 1e-6
NEG_INF = -1e30
N_CHIPS = 4
N_DEV = 8

ADAM_LR = 0.001
ADAM_B1 = 0.9
ADAM_B2 = 0.999
ADAM_EPS = 1e-08
ADAM_WD = 0.01
ADAM_STEP = 10

TM = 512
MIB = 1024 * 1024
MESH = pl.DeviceIdType.MESH
ANY = pl.BlockSpec(memory_space=pl.ANY)

_pcall = pl.pallas_call


def _params(sem=None, vmem_mib=None, collective_id=None):
    kw = {} if collective_id is None else {"collective_id": collective_id}
    if sem is not None:
        kw["dimension_semantics"] = sem
    if vmem_mib is not None:
        kw["vmem_limit_bytes"] = vmem_mib * MIB
    return pltpu.CompilerParams(**kw)


def _resident(shape):
    return pl.BlockSpec(shape, lambda *_: (0,) * len(shape), pipeline_mode=pl.Buffered(1))


def _dot(a, b, ca, cb):
    return lax.dot_general(a, b, (((ca,), (cb,)), ((), ())), preferred_element_type=F32)


def _rms_bwd(dy, x, r, g):
    dg = jnp.sum(dy * (x * r), axis=0, keepdims=True)
    dgx = dy * g
    dx = r * dgx - x * (r * r * r) * jnp.mean(x * dgx, axis=-1, keepdims=True)
    return dx, dg


def _where():
    x, y, c = lax.axis_index("x"), lax.axis_index("y"), lax.axis_index("c")
    return x, y, c, [(1 - x, y), (x, 1 - y), (1 - x, 1 - y)]


def _rcopy(src, dst, ssem, rsem, dev):
    return pltpu.make_async_remote_copy(src_ref=src, dst_ref=dst, send_sem=ssem, recv_sem=rsem, device_id=dev,
                                        device_id_type=MESH)


SIBLING, Y_CHIP, X_CHIP, DIAGONAL_CHIP = 1, 2, 4, 6
OTHER_CHIPS = (Y_CHIP, X_CHIP, DIAGONAL_CHIP)
EVERYONE = tuple(range(1, N_DEV))
BARRIER_OF = {(SIBLING,): 0, (SIBLING, Y_CHIP, X_CHIP): 1, OTHER_CHIPS: 2, (SIBLING,) + OTHER_CHIPS: 3, EVERYONE: 4}


def _peer(rel):
    x, y, c, _ = _where()
    return x ^ ((rel >> 2) & 1), y ^ ((rel >> 1) & 1), c ^ (rel & 1)


class _Task:
    def __init__(self, ins, outs, alias, n_sem, start, finish, middle=None, peers=()):
        self.ins, self.outs, self.alias, self.n_sem, self.start, self.finish = ins, outs, alias, n_sem, start, finish
        self.middle = middle if middle is not None else (lambda *args: None)
        self.peers = peers


def _peers_of(comm):
    return tuple(sorted({p for t in comm for p in t.peers}))


def _enter(comm):
    peers = _peers_of(comm)
    barrier = pltpu.get_barrier_semaphore()
    for rel in peers:
        pl.semaphore_signal(barrier, inc=1, device_id=_peer(rel), device_id_type=MESH)
    pl.semaphore_wait(barrier, len(peers))


ROWS16 = 16


def _t_gather(placed, part=(0, 1)):
    R = placed.shape[0] // N_CHIPS
    q = R // 4
    lo, hi = (round(f * (q // ROWS16)) * ROWS16 for f in part)

    def quarter(chip_index, core, k):
        return pl.ds(pl.multiple_of(chip_index * R + core * 2 * q + k * q + lo, ROWS16), hi - lo)

    def places():
        x, y, c, _ = _where()
        return c, 2 * x + y, 2 * (1 - x) + y, 2 * x + (1 - y), 2 * (1 - x) + (1 - y), (1 - x, y, c), (x, 1 - y, c), (x, y, 1 - c)

    def copy(buf, k, chip_index, core, quart, ss, rs, b, dev):
        window = buf.at[quarter(chip_index, core, quart)]
        return _rcopy(window, window, ss.at[b + k], rs.at[b + k], dev)

    def start(cin, cout, ss, rs, b):
        c, me, _, _, _, x_nbr, y_nbr, _ = places()
        for k, (quart, dev) in enumerate(((0, x_nbr), (1, y_nbr), (1, x_nbr), (0, y_nbr))):
            copy(cout[0], k, me, c, quart, ss, rs, b, dev).start()

    def middle(cin, cout, ss, rs, b):
        c, _, xc, yc, _, x_nbr, y_nbr, sib = places()
        for k, chip_index, quart, dev in ((0, xc, 0, y_nbr), (1, yc, 1, x_nbr)):
            copy(cout[0], k, chip_index, c, quart, ss, rs, b, dev).wait_recv()
            copy(cout[0], 4 + k, chip_index, c, quart, ss, rs, b, dev).start()
            copy(cout[0], 6 + k, chip_index, c, quart, ss, rs, b, sib).start()

    later = ((2, 1, 1), (3, 2, 0), (4, 3, 0), (5, 3, 1))

    def finish(cin, cout, ss, rs, b):
        c, me, xc, yc, dc, _, _, sib = places()
        chip_of = {1: xc, 2: yc, 3: dc}
        for k, whose, quart in later:
            copy(cout[0], k, chip_of[whose], c, quart, ss, rs, b, sib).wait_recv()
            copy(cout[0], 6 + k, chip_of[whose], c, quart, ss, rs, b, sib).start()
        for k, whose, quart in ((0, 1, 0), (1, 2, 1)) + later:
            copy(cout[0], 6 + k, chip_of[whose], 1 - c, quart, ss, rs, b, sib).wait_recv()
        for k in range(12):
            copy(cout[0], k, me, c, 0, ss, rs, b, sib).wait_send()

    return _Task([placed], [SDS(placed.shape, placed.dtype)], [(0, 0)], 12, start, finish, middle, peers=(SIBLING, Y_CHIP, X_CHIP))


def _t_small_weights(buf):
    def start(cin, cout, ss, rs, b):
        x, y, c, chips = _where()
        mine = cout[0].at[2 * x + y]
        for r, (px, py) in enumerate(chips):
            _rcopy(mine, mine, ss.at[b + r], rs.at[b + r], (px, py, c)).start()

    def finish(cin, cout, ss, rs, b):
        x, y, c, chips = _where()
        for r, (px, py) in enumerate(chips):
            got = cout[0].at[2 * px + py]
            _rcopy(got, got, ss.at[b + r], rs.at[b + r], (px, py, c)).wait_recv()
        for r, (px, py) in enumerate(chips):
            mine = cout[0].at[2 * x + y]
            _rcopy(mine, mine, ss.at[b + r], rs.at[b + r], (px, py, c)).wait_send()

    return _Task([buf], [SDS(buf.shape, buf.dtype)], [(0, 0)], 3, start, finish, peers=OTHER_CHIPS)


def _t_sibling(gbf):
    def start(cin, cout, ss, rs, b):
        x, y, c, _ = _where()
        for jj in range(N_CHIPS):
            _rcopy(cin[0].at[2 * jj + (1 - c)], cout[0].at[jj], ss.at[b + jj], rs.at[b + jj], (x, y, 1 - c)).start()

    def finish(cin, cout, ss, rs, b):
        x, y, c, _ = _where()
        for jj in range(N_CHIPS):
            got = cout[0].at[jj]
            _rcopy(got, got, ss.at[b + jj], rs.at[b + jj], (x, y, 1 - c)).wait_recv()
        for jj in range(N_CHIPS):
            got = cout[0].at[jj]
            _rcopy(got, got, ss.at[b + jj], rs.at[b + jj], (x, y, 1 - c)).wait_send()

    return _Task([gbf], [SDS((N_CHIPS,) + gbf.shape[1:], BF)], [], N_CHIPS, start, finish, peers=(SIBLING,))


def _t_chips(pbf):
    def start(cin, cout, ss, rs, b):
        x, y, c, chips = _where()
        for r, (px, py) in enumerate(chips):
            _rcopy(cin[0].at[2 * px + py], cout[0].at[r], ss.at[b + r], rs.at[b + r], (px, py, c)).start()

    def finish(cin, cout, ss, rs, b):
        x, y, c, chips = _where()
        for r, (px, py) in enumerate(chips):
            got = cout[0].at[r]
            _rcopy(got, got, ss.at[b + r], rs.at[b + r], (px, py, c)).wait_recv()
        for r, (px, py) in enumerate(chips):
            got = cout[0].at[r]
            _rcopy(got, got, ss.at[b + r], rs.at[b + r], (px, py, c)).wait_send()

    return _Task([pbf], [SDS((3,) + pbf.shape[1:], BF)], [], 3, start, finish, peers=OTHER_CHIPS)


def _t_swap(fin):
    def start(cin, cout, ss, rs, b):
        x, y, c, _ = _where()
        mine = cout[0].at[c]
        _rcopy(mine, mine, ss.at[b], rs.at[b], (x, y, 1 - c)).start()

    def finish(cin, cout, ss, rs, b):
        x, y, c, _ = _where()
        got = cout[0].at[1 - c]
        _rcopy(got, got, ss.at[b], rs.at[b], (x, y, 1 - c)).wait_recv()
        _rcopy(got, got, ss.at[b], rs.at[b], (x, y, 1 - c)).wait_send()

    return _Task([fin], [SDS(fin.shape, fin.dtype)], [(0, 0)], 1, start, finish, peers=(SIBLING,))


def _t_allgather(buf):
    def peers():
        x, y, c, _ = _where()
        out = []
        for rel in range(1, N_DEV):
            px, py, pc = x ^ ((rel >> 2) & 1), y ^ ((rel >> 1) & 1), c ^ (rel & 1)
            out.append((rel - 1, 4 * px + 2 * py + pc, (px, py, pc)))
        return 4 * x + 2 * y + c, out

    def start(cin, cout, ss, rs, b):
        me, ps = peers()
        mine = cout[0].at[me]
        for k, _, dev in ps:
            _rcopy(mine, mine, ss.at[b + k], rs.at[b + k], dev).start()

    def finish(cin, cout, ss, rs, b):
        me, ps = peers()
        for k, pidx, dev in ps:
            got = cout[0].at[pidx]
            _rcopy(got, got, ss.at[b + k], rs.at[b + k], dev).wait_recv()
        for k, _, dev in ps:
            mine = cout[0].at[me]
            _rcopy(mine, mine, ss.at[b + k], rs.at[b + k], dev).wait_send()

    return _Task([buf], [SDS(buf.shape, buf.dtype)], [(0, 0)], N_DEV - 1, start, finish, peers=EVERYONE)


def _run_tasks(comm, which, cin, cout, ss, rs):
    i0 = o0 = s0 = 0
    for t in comm:
        getattr(t, which)(cin[i0:i0 + len(t.ins)], cout[o0:o0 + len(t.outs)], ss, rs, s0)
        i0, o0, s0 = i0 + len(t.ins), o0 + len(t.outs), s0 + t.n_sem


def _from_hbm(*arrays):
    return [pltpu.with_memory_space_constraint(a, pltpu.HBM) for a in arrays]


def _in_hbm(shapes):
    return [pltpu.HBM(s.shape, s.dtype) for s in shapes]


def _comm_layout(comm, n_in, n_out):
    c_in = [a for t in comm for a in t.ins]
    c_out = [s for t in comm for s in t.outs]
    aliases, i0, o0 = {}, 0, 0
    for t in comm:
        for i, o in t.alias:
            aliases[n_in + i0 + i] = n_out + o0 + o
        i0, o0 = i0 + len(t.ins), o0 + len(t.outs)
    return c_in, c_out, aliases, sum(t.n_sem for t in comm)


def _call(body, operands, *, name, grid, in_specs, out_specs, out_shape, scratch_shapes=(), sem=None, vmem_mib=None, comm=(),
          free=(), prefetch=()):
    operands = [o if s.memory_space == pltpu.SMEM or k in free else pltpu.with_memory_space_constraint(o, pltpu.HBM)
                for k, (o, s) in enumerate(zip(operands, in_specs))]
    n_pre, n_in, n_out, n_scr = len(prefetch), len(in_specs), len(out_specs), len(scratch_shapes)
    c_in, c_out, aliases, n_sem = _comm_layout(comm, n_pre + n_in, n_out)
    sems = [pltpu.SemaphoreType.DMA((n_sem,)), pltpu.SemaphoreType.DMA((n_sem,))] if comm else []

    def wrapped(*refs):
        pre, refs = refs[:n_pre], refs[n_pre:]
        ins, cin = refs[:n_in], refs[n_in:n_in + len(c_in)]
        rest = refs[n_in + len(c_in):]
        outs, cout = rest[:n_out], rest[n_out:n_out + len(c_out)]
        rest = rest[n_out + len(c_out):]
        scr, csem = rest[:n_scr], rest[n_scr:]
        if not comm:
            return body(*pre, *ins, *outs, *scr)
        step = functools.reduce(lambda acc, k: acc * grid[k] + pl.program_id(k), range(len(grid)), 0)
        n_steps = math.prod(grid)

        @pl.when(step == 0)
        def _():
            _enter(comm)
            _run_tasks(comm, "start", cin, cout, *csem)

        pl.when(step == n_steps // 2)(lambda: _run_tasks(comm, "middle", cin, cout, *csem))
        body(*pre, *ins, *outs, *scr)
        pl.when(step == n_steps - 1)(lambda: _run_tasks(comm, "finish", cin, cout, *csem))

    grid_spec = pltpu.PrefetchScalarGridSpec(
        num_scalar_prefetch=n_pre, grid=grid, in_specs=list(in_specs) + [ANY] * len(c_in),
        out_specs=list(out_specs) + [ANY] * len(c_out), scratch_shapes=list(scratch_shapes) + sems)
    return _pcall(
        wrapped, name=name, grid_spec=grid_spec, out_shape=_in_hbm(list(out_shape) + c_out), input_output_aliases=aliases,
        compiler_params=_params(("arbitrary",) * len(grid) if comm else sem, vmem_mib,
                                BARRIER_OF[_peers_of(comm)] if comm else None),
    )(*prefetch, *operands, *_from_hbm(*c_in))


def _comm_call(name, comm):
    c_in, c_out, aliases, n_sem = _comm_layout(comm, 0, 0)

    def body(*refs):
        cin, cout, (ss, rs) = refs[:len(c_in)], refs[len(c_in):len(c_in) + len(c_out)], refs[len(c_in) + len(c_out):]
        _enter(comm)
        for phase in ("start", "middle", "finish"):
            _run_tasks(comm, phase, cin, cout, ss, rs)

    return _pcall(
        body, name=name, in_specs=[ANY] * len(c_in), out_specs=[ANY] * len(c_out), out_shape=_in_hbm(c_out),
        scratch_shapes=[pltpu.SemaphoreType.DMA((n_sem,)), pltpu.SemaphoreType.DMA((n_sem,))],
        input_output_aliases=aliases, compiler_params=_params(collective_id=BARRIER_OF[_peers_of(comm)]),
    )(*_from_hbm(*c_in))


def _inproj(x, g1, w_int, comm=()):
    tm = TM

    def body(x_ref, g_ref, w_ref, proj_ref, u_ref):
        xf = x_ref[...]
        r = lax.rsqrt(jnp.mean(xf * xf, axis=-1, keepdims=True) + EPS)
        u = (xf * r * g_ref[...]).astype(BF)
        u_ref[...] = u
        proj_ref[...] = _dot(u, w_ref[...], 1, 1)

    return _call(
        body, (x, g1, w_int), name="inproj", grid=(T // tm,),
        in_specs=[pl.BlockSpec((tm, D), lambda i: (i, 0)), pl.BlockSpec((1, D), lambda i: (0, 0)),
                  _resident((INW, D))],
        out_specs=[pl.BlockSpec((tm, INW), lambda i: (i, 0)), pl.BlockSpec((tm, D), lambda i: (i, 0))],
        out_shape=[SDS((T, INW), F32), SDS((T, D), BF)], sem=("parallel",), vmem_mib=40, comm=comm, free=(0, 1))


def _outproj(y, w_out, x, g2):
    tm = TM

    def body(y_ref, w_ref, x_ref, g_ref, h1_ref, u2_ref):
        h1 = x_ref[...] + _dot(y_ref[...], w_ref[...], 1, 0)
        h1_ref[...] = h1
        r = lax.rsqrt(jnp.mean(h1 * h1, axis=-1, keepdims=True) + EPS)
        u2_ref[...] = (h1 * r * g_ref[...]).astype(BF)

    return _call(
        body, (y, w_out, x, g2), name="outproj", grid=(T // tm,),
        in_specs=[pl.BlockSpec((tm, D), lambda i: (i, 0)), _resident((D, D)),
                  pl.BlockSpec((tm, D), lambda i: (i, 0)), pl.BlockSpec((1, D), lambda i: (0, 0))],
        out_specs=[pl.BlockSpec((tm, D), lambda i: (i, 0)), pl.BlockSpec((tm, D), lambda i: (i, 0))],
        out_shape=[SDS((T, D), F32), SDS((T, D), BF)], sem=("parallel",), vmem_mib=32, free=(1, 2, 3))


def _ffn_up(u2, w_upt, comm=()):
    tm, tn = 1024, 512

    def body(u_ref, w_ref, o_ref):
        o_ref[...] = _dot(u_ref[...], w_ref[...], 1, 1).astype(BF)

    return _call(
        body, (u2, w_upt), name="ffn_up", grid=(T // tm, 2 * DFF // tn),
        in_specs=[pl.BlockSpec((tm, D), lambda i, j: (i, 0)), pl.BlockSpec((tn, D), lambda i, j: (j, 0))],
        out_specs=[pl.BlockSpec((tm, tn), lambda i, j: (i, j))], out_shape=[SDS((T, 2 * DFF), BF)],
        sem=("parallel", "parallel"), vmem_mib=32, comm=comm, free=(1,))


def _ffn_down(a, w_down, h1, tgt):
    tm = TM

    def body(a_ref, w_ref, h1_ref, t_ref, dh_ref, dhb_ref, l_ref):
        @pl.when(pl.program_id(0) == 0)
        def _():
            l_ref[...] = jnp.zeros_like(l_ref)

        h2 = h1_ref[...] + _dot(a_ref[...], w_ref[...], 1, 0)
        e = h2 - t_ref[...]
        dh = e * (1.0 / D)
        dh_ref[...] = dh
        dhb_ref[...] = dh.astype(BF)
        e2 = jnp.sum((e * e).reshape(tm // 8, 8, D), axis=0)
        acc = e2[:, 0:128]
        for k in range(1, D // 128):
            acc = acc + e2[:, k * 128:(k + 1) * 128]
        l_ref[...] += acc

    return _call(
        body, (a, w_down, h1, tgt), name="ffn_down", grid=(T // tm,),
        in_specs=[pl.BlockSpec((tm, DFF), lambda i: (i, 0)), _resident((DFF, D)),
                  pl.BlockSpec((tm, D), lambda i: (i, 0)), pl.BlockSpec((tm, D), lambda i: (i, 0))],
        out_specs=[pl.BlockSpec((tm, D), lambda i: (i, 0)), pl.BlockSpec((tm, D), lambda i: (i, 0)),
                   pl.BlockSpec((8, 128), lambda i: (0, 0))],
        out_shape=[SDS((T, D), F32), SDS((T, D), BF), SDS((8, 128), F32)], sem=("arbitrary",), vmem_mib=40, free=(2, 3))


def _bucket_table():
    q = np.arange(BLK, dtype=np.int32)[:, None]
    j = np.arange(2 * BLK, dtype=np.int32)[None, :]
    n = np.maximum(q + BLK - j, 0)
    nf = np.maximum(n, 1).astype(np.float32)
    max_exact = NBUCKET // 2
    large = max_exact + (np.log(nf / np.float32(max_exact)) / np.float32(math.log(BLK / max_exact))
                         * np.float32(NBUCKET - max_exact)).astype(np.int32)
    large = np.minimum(large, NBUCKET - 1)
    return np.where(n < max_exact, n, large).astype(np.int32)


def _band_bias_bwd(dbias, bucket, me):
    def body(me_ref, db_ref, bk_ref, o_ref):
        bk = bk_ref[...]
        for b in range(NBUCKET):
            m = bk == b
            for h in range(NH):
                v = jnp.where(m, db_ref[h * BLK:(h + 1) * BLK, :], 0.0)
                s = jnp.sum(jnp.sum(v, axis=1, keepdims=True), axis=0, keepdims=True)
                o_ref[0, h:h + 1, b:b + 1] = s

    grid_spec = pltpu.PrefetchScalarGridSpec(
        num_scalar_prefetch=1, grid=(1,),
        in_specs=[pl.BlockSpec((NH * BLK, 2 * BLK), lambda i, me_ref: (0, 0)),
                  pl.BlockSpec((BLK, 2 * BLK), lambda i, me_ref: (0, 0))],
        out_specs=pl.BlockSpec((1, NH, NBUCKET), lambda i, me_ref: (me_ref[0], 0, 0)),
    )
    return _pcall(body, name="band_bias_bwd", grid_spec=grid_spec, out_shape=SDS((N_DEV, NH, NBUCKET), F32),
                  compiler_params=_params(("arbitrary",)))(me, dbias, bucket)


def _two_bf16(x):
    hi = x.astype(BF)
    return hi, (x - hi.astype(F32)).astype(BF)


def _head_sums(x, seg):
    hi, lo = _two_bf16(x)
    s = seg[0:x.shape[1], :]
    return _dot(hi, s, 1, 0) + _dot(lo, s, 1, 0)


def _head_spread(v, seg, width):
    hi, lo = _two_bf16(v)
    s = seg[0:width, :]
    return _dot(hi, s, 1, 1) + _dot(lo, s, 1, 1)


def _head_norm(x, g_t, seg, by_head=False):
    if by_head:
        heads = [x[:, h * HD:(h + 1) * HD] for h in range(x.shape[1] // HD)]
        r = jnp.concatenate([jnp.broadcast_to(lax.rsqrt(jnp.mean(v * v, axis=-1, keepdims=True) + EPS), v.shape)
                             for v in heads], axis=1)
    else:
        r = lax.rsqrt(_head_sums(x * x, seg) * (1.0 / HD) + EPS)
        r = _head_spread(r, seg, x.shape[1])
    return x * r * g_t, r


def _head_norm_bwd(dy, x, r, g_t, seg):
    dg_t = jnp.sum(dy * (x * r), axis=0, keepdims=True)
    dgx = dy * g_t
    mean = _head_spread(_head_sums(x * dgx, seg) * (1.0 / HD), seg, x.shape[1])
    return r * dgx - x * (r * r * r) * mean, dg_t


def _fold_heads(v):
    out = v[:, 0:HD]
    for h in range(1, v.shape[1] // HD):
        out = out + v[:, h * HD:(h + 1) * HD]
    return out


def _mix_forward(P, zc8, zh8, pkv, first, cw, qg_t, kg_t, gco, gao, seg, sink_ref, bias_ref, by_head=False):
    gate_b = P[:, 0:CW]
    gate_c = P[:, CW:2 * CW]
    hc = P[:, 2 * CW:3 * CW]
    z = gate_c * hc
    keep = jnp.where(first, 0.0, 1.0)
    zp = zc8 * zh8 * keep
    p1 = zp[7:8, :]
    p2 = zp[6:7, :]
    row = lax.broadcasted_iota(jnp.int32, (BLK, 1), 0)
    z1 = jnp.where(row == 0, p1, pltpu.roll(z, 1, 0))
    z2 = jnp.where(row == 0, p2, jnp.where(row == 1, p1, pltpu.roll(z, 2, 0)))
    cz = cw[0:1, :] * z2 + cw[1:2, :] * z1 + cw[2:3, :] * z
    y_conv = gate_b * cz

    scale = HD ** -0.5
    qi = lax.broadcasted_iota(jnp.int32, (BLK, 2 * BLK), 0)
    kj = lax.broadcasted_iota(jnp.int32, (BLK, 2 * BLK), 1)
    dd = qi + BLK - kj
    first_key = jnp.where(first, BLK, 0)
    valid = (dd >= 0) & (dd < BLK) & (kj >= first_key)

    q0 = 3 * CW
    k0 = q0 + AW
    v0 = k0 + NKV * HD
    q_raw = P[:, q0:k0]
    qn, rq = _head_norm(q_raw, qg_t, seg, by_head)
    qs = (qn * scale).astype(BF)
    k_raw = jnp.concatenate([pkv[:, 0:NKV * HD], P[:, k0:v0]], axis=0)
    kn, rk = _head_norm(k_raw, kg_t, seg, by_head)
    knb = kn.astype(BF)
    heads = []
    for h in range(NH):
        kv = h // GQ
        kb = knb[:, kv * HD:(kv + 1) * HD]
        vb = jnp.concatenate([pkv[:, NKV * HD + kv * HD:NKV * HD + (kv + 1) * HD],
                              P[:, v0 + kv * HD:v0 + (kv + 1) * HD]], axis=0).astype(BF)
        Q = qs[:, h * HD:(h + 1) * HD]
        S = _dot(Q, kb, 1, 1) + bias_ref[h * BLK:(h + 1) * BLK, :]
        S = jnp.where(valid, S, NEG_INF)
        sink = sink_ref[0, h]
        m = jnp.maximum(jnp.max(S, axis=-1, keepdims=True), sink)
        p = jnp.exp(S - m)
        es = jnp.exp(sink - m)
        denom = jnp.sum(p, axis=-1, keepdims=True) + es
        probs = p / denom
        O = _dot(probs.astype(BF), vb, 1, 0)
        heads.append(dict(kb=kb, vb=vb, Q=Q, probs=probs, psink=es / denom, O=O))
    y_attn = jnp.concatenate([hd["O"] for hd in heads], axis=1)

    rc = lax.rsqrt(jnp.mean(y_conv * y_conv, axis=-1, keepdims=True) + EPS)
    ra = lax.rsqrt(jnp.mean(y_attn * y_attn, axis=-1, keepdims=True) + EPS)
    y = jnp.concatenate([y_conv * rc * gco, y_attn * ra * gao], axis=1)
    return dict(gate_b=gate_b, gate_c=gate_c, hc=hc, z=z, z1=z1, z2=z2, cz=cz, y_conv=y_conv, y_attn=y_attn,
                rc=rc, ra=ra, heads=heads, y=y, row=row, scale=scale, q_raw=q_raw, rq=rq, k_raw=k_raw, rk=rk)


BPS = 2
TILE = BPS * BLK
KV0 = 3 * CW + AW


def _mix_in_specs(tile_of):
    return [
        pl.BlockSpec(memory_space=pltpu.SMEM),
        pl.BlockSpec((TILE, INW), lambda s: (tile_of(s), 0)),
        pl.BlockSpec((8, CW), lambda s: (jnp.maximum(tile_of(s) * (TILE // 8) - 1, 0), 1)),
        pl.BlockSpec((8, CW), lambda s: (jnp.maximum(tile_of(s) * (TILE // 8) - 1, 0), 2)),
        pl.BlockSpec((BLK, 2 * NKV * HD), lambda s: (jnp.maximum(tile_of(s) * BPS - 1, 0), KV0 // (2 * NKV * HD))),
    ]


def _block_inputs(tile, b, zc_ref, zh_ref, pkv_ref, first_tile):
    P = tile[b * BLK:(b + 1) * BLK, :]
    if b == 0:
        return P, zc_ref[...], zh_ref[...], pkv_ref[...], first_tile
    lo = b * BLK
    return P, tile[lo - 8:lo, CW:2 * CW], tile[lo - 8:lo, 2 * CW:3 * CW], tile[lo - BLK:lo, KV0:KV0 + 2 * NKV * HD], False


def _mix_param_specs():
    return [
        pl.BlockSpec((8, CW), lambda s: (0, 0)),
        pl.BlockSpec((1, AW), lambda s: (0, 0)),
        pl.BlockSpec((1, NKV * HD), lambda s: (0, 0)),
        pl.BlockSpec((1, CW), lambda s: (0, 0)),
        pl.BlockSpec((1, AW), lambda s: (0, 0)),
        pl.BlockSpec((AW, 128), lambda s: (0, 0)),
        pl.BlockSpec((NH * BLK, 2 * BLK), lambda s: (0, 0)),
    ]


def _mix_params(cw8, qg, kg, gco, gao, bias):
    seg = np.zeros((AW, 128), np.float32)
    seg[np.arange(AW), np.arange(AW) // HD] = 1.0
    return (cw8, jnp.tile(qg, (1, NH)), jnp.tile(kg, (1, NKV)), gco, gao, jnp.asarray(seg, BF), bias)


def _mix_fwd(proj, sinks, cw8, qg, kg, gco, gao, bias, comm=()):
    def body(sink_ref, p_ref, zc_ref, zh_ref, pkv_ref, cw_ref, qg_ref, kg_ref, gco_ref, gao_ref, seg_ref, bias_ref, y_ref):
        tile = p_ref[...]
        for b in range(BPS):
            f = _mix_forward(*_block_inputs(tile, b, zc_ref, zh_ref, pkv_ref, pl.program_id(0) == 0), cw_ref[...],
                             qg_ref[...], kg_ref[...], gco_ref[...], gao_ref[...], seg_ref[...], sink_ref, bias_ref, by_head=True)
            y_ref[b * BLK:(b + 1) * BLK, :] = f["y"].astype(BF)

    return _call(
        body, (sinks, proj, proj, proj, proj, *_mix_params(cw8, qg, kg, gco, gao, bias)), name="mix_fwd", grid=(T // TILE,),
        in_specs=_mix_in_specs(lambda s: s) + _mix_param_specs(),
        out_specs=[pl.BlockSpec((TILE, D), lambda s: (s, 0))], out_shape=[SDS((T, D), BF)],
        sem=("parallel",), vmem_mib=40, comm=comm, free=tuple(range(5, 12)))


def _mix_bwd(proj, dy, sinks, cw8, qg, kg, gco, gao, bias, comm=()):
    n_steps = T // TILE

    def tile_of(s):
        return n_steps - 1 - s

    def body(sink_ref, p_ref, zc_ref, zh_ref, pkv_ref, dy_ref, cw_ref, qg_ref, kg_ref, gco_ref, gao_ref, seg_ref, bias_ref,
             dproj_ref, dcw_ref, dqg_ref, dkg_ref, dgco_ref, dgao_ref, dsink_ref, dbias_ref,
             ndcz_ref, dkc_ref, dvc_ref):
        s = pl.program_id(0)

        @pl.when(s == 0)
        def _():
            for r in (dcw_ref, dqg_ref, dkg_ref, dgco_ref, dgao_ref, dsink_ref, dbias_ref, ndcz_ref, dkc_ref, dvc_ref):
                r[...] = jnp.zeros_like(r)

        params = (cw_ref[...], qg_ref[...], kg_ref[...], gco_ref[...], gao_ref[...], seg_ref[...])
        tile = p_ref[...]
        carry = (ndcz_ref[...], dkc_ref[...], dvc_ref[...])
        total = None
        for b in reversed(range(BPS)):
            f = _mix_forward(*_block_inputs(tile, b, zc_ref, zh_ref, pkv_ref, s == n_steps - 1), *params, sink_ref, bias_ref)
            pieces, sums, carry = one_block(f, dy_ref[b * BLK:(b + 1) * BLK, :], params, carry)
            for lo, piece in pieces:
                dproj_ref[b * BLK:(b + 1) * BLK, lo:lo + piece.shape[1]] = piece
            total = sums if total is None else [t + v for t, v in zip(total, sums)]
        ndcz_ref[...], dkc_ref[...], dvc_ref[...] = carry
        dcw, dqg_t, dkg_t, dgco, dgao, dsink, *ds = total
        dcw_ref[0:3, :] += dcw
        dqg_ref[...] += _fold_heads(dqg_t)
        dkg_ref[...] += _fold_heads(dkg_t)
        dgco_ref[...] += dgco
        dgao_ref[...] += dgao
        dsink_ref[...] += dsink
        for h in range(NH):
            dbias_ref[h * BLK:(h + 1) * BLK, :] += ds[h]

    def one_block(f, dy, params, carry):
        cw, qg_v, kg_v, gco_v, gao_v, seg = params
        nxt, dk_carry, dv_carry = carry
        dyc, dgco = _rms_bwd(dy[:, 0:CW], f["y_conv"], f["rc"], gco_v)
        dya, dgao = _rms_bwd(dy[:, CW:CW + AW], f["y_attn"], f["ra"], gao_v)

        row = f["row"]
        dgate_b = dyc * f["cz"]
        dcz = dyc * f["gate_b"]
        dcw = jnp.concatenate([jnp.sum(dcz * f[k], axis=0, keepdims=True) for k in ("z2", "z1", "z")], axis=0)
        n0 = nxt[0:1, :]
        n1 = nxt[1:2, :]
        d1 = jnp.where(row == BLK - 1, n0, pltpu.roll(dcz, BLK - 1, 0))
        d2 = jnp.where(row == BLK - 1, n1, jnp.where(row == BLK - 2, n0, pltpu.roll(dcz, BLK - 2, 0)))
        dz = cw[2:3, :] * dcz + cw[1:2, :] * d1 + cw[0:1, :] * d2
        pieces = [(0, dgate_b.astype(BF)), (CW, (dz * f["hc"]).astype(BF)), (2 * CW, (dz * f["gate_c"]).astype(BF))]

        scale = f["scale"]
        lane = lax.broadcasted_iota(jnp.int32, (1, 128), 1)
        dsink = jnp.zeros((1, 128), F32)
        dq_cols, dk_cols, dv_cols, dk_prev, dv_prev, ds = [], [], [], [], [], []
        for kv in range(NKV):
            dKb = dVb = 0.0
            for h in range(kv * GQ, (kv + 1) * GQ):
                hd = f["heads"][h]
                dO = dya[:, h * HD:(h + 1) * HD]
                delta = jnp.sum(dO * hd["O"], axis=-1, keepdims=True)
                dOb = dO.astype(BF)
                dP = _dot(dOb, hd["vb"], 1, 1)
                dS = hd["probs"] * (dP - delta)
                tot = jnp.sum(hd["psink"] * delta, axis=0, keepdims=True)
                dsink = dsink - jnp.where(lane == h, tot, 0.0)
                ds.append(dS)
                dSb = dS.astype(BF)
                dq_cols.append(_dot(dSb, hd["kb"], 1, 0))
                dKb = dKb + _dot(dSb, hd["Q"], 0, 0)
                dVb = dVb + _dot(hd["probs"].astype(BF), dOb, 0, 0)
            dk_cols.append(dKb[BLK:, :] + dk_carry[:, kv * HD:(kv + 1) * HD])
            dv_cols.append(dVb[BLK:, :] + dv_carry[:, kv * HD:(kv + 1) * HD])
            dk_prev.append(dKb[:BLK, :])
            dv_prev.append(dVb[:BLK, :])
        dq_raw, dqg_t = _head_norm_bwd(jnp.concatenate(dq_cols, axis=1) * scale, f["q_raw"], f["rq"], qg_v, seg)
        dk_raw, dkg_t = _head_norm_bwd(jnp.concatenate(dk_cols, axis=1), f["k_raw"][BLK:, :], f["rk"][BLK:, :], kg_v, seg)
        pieces.append((3 * CW, jnp.concatenate([dq_raw, dk_raw] + dv_cols, axis=1).astype(BF)))
        owed = (dcz[0:8, :], jnp.concatenate(dk_prev, axis=1), jnp.concatenate(dv_prev, axis=1))
        return pieces, [dcw, dqg_t, dkg_t, dgco, dgao, dsink, *ds], owed

    small = lambda r, c: pl.BlockSpec((r, c), lambda s: (0, 0))
    return _call(
        body, (sinks, proj, proj, proj, proj, dy, *_mix_params(cw8, qg, kg, gco, gao, bias)), name="mix_bwd", grid=(n_steps,),
        in_specs=_mix_in_specs(tile_of) + [pl.BlockSpec((TILE, D), lambda s: (tile_of(s), 0))] + _mix_param_specs(),
        out_specs=[pl.BlockSpec((TILE, INW), lambda s: (tile_of(s), 0)), small(8, CW), small(1, HD), small(1, HD),
                   small(1, CW), small(1, AW), small(1, 128), small(NH * BLK, 2 * BLK)],
        out_shape=[SDS((T, INW), BF), SDS((8, CW), F32), SDS((1, HD), F32), SDS((1, HD), F32), SDS((1, CW), F32),
                   SDS((1, AW), F32), SDS((1, 128), F32), SDS((NH * BLK, 2 * BLK), F32)],
        scratch_shapes=[pltpu.VMEM((8, CW), F32), pltpu.VMEM((BLK, NKV * HD), F32), pltpu.VMEM((BLK, NKV * HD), F32)],
        sem=("arbitrary",), vmem_mib=56, comm=comm, free=(1, 2, 3, 4) + tuple(range(6, 13)))


FT = 256
NFT = DFF // FT
RC = 128
NCH = T // RC
LEAD = 16


def _rows8(x):
    return jnp.sum(x.reshape(x.shape[0] // 8, 8, x.shape[1]), axis=0)


def _ffn_act_specs():
    return [
        pl.BlockSpec((T, FT), lambda j: (0, j)), pl.BlockSpec((T, FT), lambda j: (0, NFT + j)),
        pl.BlockSpec((8, FT), lambda j: (0, j)), pl.BlockSpec((8, FT), lambda j: (0, NFT + j)),
        pl.BlockSpec((1, FT), lambda j: (0, j)), pl.BlockSpec((1, FT), lambda j: (0, NFT + j)),
    ]


def _conv_rows(win, w, b, n):
    win = win.astype(F32)
    u = win[LEAD:LEAD + n]
    u1 = pltpu.roll(win, 1, 0)[LEAD:LEAD + n]
    u2 = pltpu.roll(win, 2, 0)[LEAD:LEAD + n]
    return u2, u1, u, w[0:1, :] * u2 + w[1:2, :] * u1 + w[2:3, :] * u + b


def _ffn_act(up, fw8, fb, comm=()):
    def body(ug_ref, uv_ref, wg_ref, wv_ref, bg_ref, bv_ref, a_ref):
        wg, wv, bg, bv = wg_ref[...], wv_ref[...], bg_ref[...], bv_ref[...]

        def chunk(win_g, win_v):
            gp = _conv_rows(win_g, wg, bg, RC)[3]
            vp = _conv_rows(win_v, wv, bv, RC)[3]
            return (gp * jax.nn.sigmoid(gp) * vp).astype(BF)

        zero = jnp.zeros((LEAD, FT), BF)
        a_ref[0:RC, :] = chunk(jnp.concatenate([zero, ug_ref[0:RC, :]], axis=0),
                               jnp.concatenate([zero, uv_ref[0:RC, :]], axis=0))

        def step(i, carry):
            r0 = pl.multiple_of(i * RC, RC)
            win = pl.ds(r0 - LEAD, RC + LEAD)
            a_ref[pl.ds(r0, RC), :] = chunk(ug_ref[win, :], uv_ref[win, :])
            return carry

        lax.fori_loop(1, NCH, step, 0)

    return _call(
        body, (up, up, fw8, fw8, fb, fb), name="ffn_act", grid=(NFT,), in_specs=_ffn_act_specs(),
        out_specs=[pl.BlockSpec((T, FT), lambda j: (0, j))], out_shape=[SDS((T, DFF), BF)],
        sem=("parallel",), vmem_mib=40, comm=comm, free=(2, 3, 4, 5))


def _ffn_act_bwd(up, da, fw8, fb, comm=()):
    ext = RC + LEAD

    def body(ug_ref, uv_ref, wg_ref, wv_ref, bg_ref, bv_ref, da_ref,
             dug_ref, duv_ref, dwg_ref, dwv_ref, dbg_ref, dbv_ref):
        wg, wv, bg, bv = wg_ref[...], wv_ref[...], bg_ref[...], bv_ref[...]

        def chunk(win_g, win_v, da_e):
            g2, g1, g0, gp = _conv_rows(win_g, wg, bg, ext)
            v2, v1, v0, vp = _conv_rows(win_v, wv, bv, ext)
            da_e = da_e.astype(F32)
            sig = jax.nn.sigmoid(gp)
            dvp = da_e * (gp * sig)
            dgp = da_e * vp * (sig * (1.0 + gp * (1.0 - sig)))

            def back(dp, w):
                return (w[2:3, :] * dp[0:RC] + w[1:2, :] * pltpu.roll(dp, ext - 1, 0)[0:RC]
                        + w[0:1, :] * pltpu.roll(dp, ext - 2, 0)[0:RC]).astype(BF)

            def sums(dp, u2, u1, u0):
                d = dp[0:RC]
                return [_rows8(d), _rows8(d * u2[0:RC]), _rows8(d * u1[0:RC]), _rows8(d * u0[0:RC])]

            return back(dgp, wg), back(dvp, wv), sums(dgp, g2, g1, g0) + sums(dvp, v2, v1, v0)

        zero = jnp.zeros((LEAD, FT), BF)
        dug, duv, acc = chunk(jnp.concatenate([zero, ug_ref[0:ext, :]], axis=0),
                              jnp.concatenate([zero, uv_ref[0:ext, :]], axis=0), da_ref[0:ext, :])
        dug_ref[0:RC, :] = dug
        duv_ref[0:RC, :] = duv

        def step(i, acc):
            r0 = pl.multiple_of(i * RC, RC)
            win = pl.ds(r0 - LEAD, ext + LEAD)
            dug, duv, part = chunk(ug_ref[win, :], uv_ref[win, :], da_ref[pl.ds(r0, ext), :])
            dug_ref[pl.ds(r0, RC), :] = dug
            duv_ref[pl.ds(r0, RC), :] = duv
            return [a + p for a, p in zip(acc, part)]

        acc = lax.fori_loop(1, NCH - 1, step, acc)
        r0 = T - RC
        tail = lambda ref, lo: jnp.concatenate([ref[lo:T, :], zero], axis=0)
        dug, duv, part = chunk(tail(ug_ref, r0 - LEAD), tail(uv_ref, r0 - LEAD), tail(da_ref, r0))
        dug_ref[r0:T, :] = dug
        duv_ref[r0:T, :] = duv
        tot = [jnp.sum(a + p, axis=0, keepdims=True) for a, p in zip(acc, part)]
        for k, (dw_ref, db_ref) in enumerate(((dwg_ref, dbg_ref), (dwv_ref, dbv_ref))):
            db_ref[...] = tot[4 * k]
            dw_ref[...] = jnp.zeros_like(dw_ref)
            for r in range(3):
                dw_ref[r:r + 1, :] = tot[4 * k + 1 + r]

    col = lambda r: pl.BlockSpec((r, FT), lambda j: (0, j))
    return _call(
        body, (up, up, fw8, fw8, fb, fb, da), name="ffn_act_bwd", grid=(NFT,),
        in_specs=_ffn_act_specs() + [pl.BlockSpec((T, FT), lambda j: (0, j))],
        out_specs=[col(T), col(T), col(8), col(8), col(1), col(1)],
        out_shape=[SDS((T, DFF), BF), SDS((T, DFF), BF), SDS((8, DFF), F32), SDS((8, DFF), F32),
                   SDS((1, DFF), F32), SDS((1, DFF), F32)],
        sem=("parallel",), vmem_mib=40, comm=comm, free=(0, 1, 2, 3, 4, 5))


def _ffn_down_bwd(dh2b, w_down, comm=()):
    tm = TM

    def body(d_ref, w_ref, o_ref):
        o_ref[...] = _dot(d_ref[...], w_ref[...], 1, 1).astype(BF)

    return _call(
        body, (dh2b, w_down), name="ffn_down_bwd", grid=(T // tm,),
        in_specs=[pl.BlockSpec((tm, D), lambda i: (i, 0)), _resident((DFF, D))],
        out_specs=[pl.BlockSpec((tm, DFF), lambda i: (i, 0))], out_shape=[SDS((T, DFF), BF)],
        sem=("parallel",), vmem_mib=40, comm=comm, free=(0, 1))


def _norm_matmul_bwd(name, a_list, w_t, k_offsets, xin, g, dres, want_bf16, comm=(), slot=None):
    tm = TM
    ks = [a.shape[1] for a in a_list]
    n_a = len(a_list)
    n_pre = 0 if slot is None else 1

    def body(*refs):
        refs = refs[n_pre:]
        a_refs = refs[:n_a]
        w_ref, x_ref, g_ref, r_ref = refs[n_a:n_a + 4]
        outs = refs[n_a + 4:]
        dx_ref, dg_ref = outs[0], (outs[-1] if slot is None else outs[-1].at[0])

        @pl.when(pl.program_id(0) == 0)
        def _():
            dg_ref[...] = jnp.zeros_like(dg_ref)

        du = _dot(a_refs[0][...], w_ref[k_offsets[0]:k_offsets[0] + ks[0], :], 1, 0)
        for k in range(1, n_a):
            du = du + _dot(a_refs[k][...], w_ref[k_offsets[k]:k_offsets[k] + ks[k], :], 1, 0)
        x = x_ref[...]
        r = lax.rsqrt(jnp.mean(x * x, axis=-1, keepdims=True) + EPS)
        dx, dg = _rms_bwd(du, x, r, g_ref[...])
        dx = r_ref[...] + dx
        dx_ref[...] = dx
        if want_bf16:
            outs[1][...] = dx.astype(BF)
        dg_ref[...] += dg

    tile = lambda c: pl.BlockSpec((tm, c), lambda i, *_: (i, 0))
    if slot is None:
        dg_spec, dg_shape = pl.BlockSpec((1, D), lambda i: (0, 0)), SDS((1, D), F32)
    else:
        dg_spec, dg_shape = pl.BlockSpec((1, 1, D), lambda i, slot_ref: (slot_ref[0], 0, 0)), SDS((N_DEV, 1, D), F32)
    out_specs = [tile(D)] + ([tile(D)] if want_bf16 else []) + [dg_spec]
    out_shape = [SDS((T, D), F32)] + ([SDS((T, D), BF)] if want_bf16 else []) + [dg_shape]
    return _call(
        body, (*a_list, w_t, xin, g, dres), name=name, grid=(T // tm,), prefetch=() if slot is None else (slot,),
        in_specs=[tile(k) for k in ks] + [_resident(w_t.shape), tile(D),
                                           pl.BlockSpec((1, D), lambda i, *_: (0, 0)), tile(D)],
        out_specs=out_specs, out_shape=out_shape, sem=("arbitrary",), vmem_mib=56, comm=comm, free=tuple(range(n_a + 4)))


def _out_bwd(dh1b, w_out, comm=()):
    tm = TM

    def body(d_ref, w_ref, o_ref):
        o_ref[...] = _dot(d_ref[...], w_ref[...], 1, 1)

    return _call(
        body, (dh1b, w_out), name="out_bwd", grid=(T // tm,),
        in_specs=[pl.BlockSpec((tm, D), lambda i: (i, 0)), _resident((D, D))],
        out_specs=[pl.BlockSpec((tm, D), lambda i: (i, 0))], out_shape=[SDS((T, D), F32)],
        sem=("parallel",), vmem_mib=32, comm=comm, free=(0, 1))


def _wgrad(name, a_list, b, old_a, comm=()):
    m_k = a_list[0].shape[1]
    tm = max(t for t in range(128, m_k // 2 + 1, 128) if m_k % t == 0)
    steps = [a.shape[1] // tm for a in a_list]
    starts = [sum(steps[:k]) for k in range(len(a_list))]
    n_a = len(a_list)

    def body(*refs):
        a_refs, b_ref, o_ref = refs[:n_a], refs[n_a], refs[n_a + 1]
        i = pl.program_id(0)
        for k in range(n_a):
            @pl.when((i >= starts[k]) & (i < starts[k] + steps[k]))
            def _(k=k):
                o_ref[...] = _dot(a_refs[k][...], b_ref[...], 0, 0).astype(BF)

    def a_spec(k):
        return pl.BlockSpec((T, tm), lambda i: (0, jnp.clip(i - starts[k], 0, steps[k] - 1)))

    m_total = tm * sum(steps)
    return _call(
        body, (*a_list, b), name=name, grid=(sum(steps),),
        in_specs=[a_spec(k) for k in range(n_a)] + [_resident((T, D))],
        out_specs=[pl.BlockSpec((tm, D), lambda i: (i, 0))], out_shape=[SDS((m_total, D), BF)],
        sem=("parallel",), vmem_mib=40, comm=comm, free=() if old_a is None else tuple(range(n_a)) if old_a else (n_a,))


def _chip_sum(name, gbf, from_sib, core, chip):
    h = gbf.shape[1]
    th = h // 2

    def body(core_ref, chip_ref, g_ref, s_ref, pbf_ref, own_ref):
        p = g_ref[0].astype(F32) + s_ref[0].astype(F32)
        pbf_ref[0] = p.astype(BF)

        @pl.when(pl.program_id(1) == chip_ref[0])
        def _():
            own_ref[...] = p

    grid_spec = pltpu.PrefetchScalarGridSpec(
        num_scalar_prefetch=2, grid=(h // th, N_CHIPS),
        in_specs=[pl.BlockSpec((1, th, D), lambda t, jj, core_ref, chip_ref: (2 * jj + core_ref[0], t, 0)),
                  pl.BlockSpec((1, th, D), lambda t, jj, core_ref, chip_ref: (jj, t, 0))],
        out_specs=[pl.BlockSpec((1, th, D), lambda t, jj, core_ref, chip_ref: (jj, t, 0)),
                   pl.BlockSpec((th, D), lambda t, jj, core_ref, chip_ref: (t, 0))],
    )
    return _pcall(
        body, name=name, grid_spec=grid_spec, out_shape=_in_hbm([SDS((N_CHIPS, h, D), BF), SDS((h, D), F32)]),
        compiler_params=_params(("arbitrary", "arbitrary"), 32),
    )(core, chip, *_from_hbm(gbf, from_sib))


def _final_sum(name, own, from_chips, core, comm=()):
    h = own.shape[0]

    def body(core_ref, o_ref, r_ref, f_ref):
        f_ref[0] = ((o_ref[...] + r_ref[0].astype(F32)) + r_ref[1].astype(F32)) + r_ref[2].astype(F32)

    return _call(
        body, (own, from_chips), name=name, grid=(1,), prefetch=(core,),
        in_specs=[pl.BlockSpec((h, D), lambda i, core_ref: (0, 0)), pl.BlockSpec((3, h, D), lambda i, core_ref: (0, 0, 0))],
        out_specs=[pl.BlockSpec((1, h, D), lambda i, core_ref: (core_ref[0], 0, 0))], out_shape=[SDS((2, h, D), F32)],
        sem=("arbitrary",), vmem_mib=40, comm=comm)


def _adam_math(w, g, m, v):
    nm = ADAM_B1 * m + (1.0 - ADAM_B1) * g
    nv = ADAM_B2 * v + (1.0 - ADAM_B2) * (g * g)
    m_hat = nm / (1.0 - ADAM_B1 ** ADAM_STEP)
    v_hat = nv / (1.0 - ADAM_B2 ** ADAM_STEP)
    return -ADAM_LR * (m_hat / (jnp.sqrt(v_hat) + ADAM_EPS) + ADAM_WD * w), nm, nv


def _adamw(name, w, g, m, v, tr, copy_g=False, stage=True, g_transposed=False):
    rows, cols = w.shape

    def body(w_ref, g_ref, m_ref, v_ref, *outs):
        g_val = g_ref[...].T if g_transposed else g_ref[...]
        if copy_g:
            outs[0][...] = g_val
        d_ref, nm_ref, nv_ref = outs[-3:]
        d_ref[...], nm_ref[...], nv_ref[...] = _adam_math(w_ref[...], g_val, m_ref[...], v_ref[...])

    spec = pl.BlockSpec((tr, cols), lambda i: (i, 0))
    n_out = 4 if copy_g else 3
    g_spec = pl.BlockSpec((cols, tr), lambda i: (0, i)) if g_transposed else spec
    return _call(body, (w, g, m, v), name=name, grid=(rows // tr,), in_specs=[spec, g_spec, spec, spec], out_specs=[spec] * n_out,
                 out_shape=[SDS((rows, cols), F32)] * n_out, sem=("parallel",), vmem_mib=32,
                 free=(0, 2, 3) if stage else ())


C_G1, C_G2, C_GCO, C_GAO, C_DCW, C_DQG, C_DKG, C_SINK, C_SQ = 0, 1024, 2048, 2560, 3072, 4608, 4736, 4864, 5632
P_W = C_SQ + 128


def _pack_small(me, dfwg, dfwv, dfbg, dfbv, dg2, dgco, dgao, dcw8, dqg, dkg, dsink, sq):
    def body(me_ref, dfwg_r, dfwv_r, dfbg_r, dfbv_r, dg2_r, dgco_r, dgao_r, dcw_r, dqg_r, dkg_r, dsink_r, sq_r, o):
        o[...] = jnp.zeros_like(o)
        o[0, :, 0:DFF] = dfwg_r[...]
        o[0, :, DFF:2 * DFF] = dfwv_r[...]
        o[0, 3:4, 0:DFF] = dfbg_r[...]
        o[0, 3:4, DFF:2 * DFF] = dfbv_r[...]
        o[0, 4:5, C_G2:C_G2 + D] = dg2_r[...]
        o[0, 4:5, C_GCO:C_GCO + CW] = dgco_r[...]
        o[0, 4:5, C_GAO:C_GAO + AW] = dgao_r[...]
        for r in range(3):
            o[0, 4:5, C_DCW + r * CW:C_DCW + (r + 1) * CW] = dcw_r[r:r + 1, :]
        o[0, 4:5, C_DQG:C_DQG + HD] = dqg_r[...]
        o[0, 4:5, C_DKG:C_DKG + HD] = dkg_r[...]
        o[0, 4:5, C_SINK:C_SINK + 128] = dsink_r[...]
        o[0, :, C_SQ:C_SQ + 128] = sq_r[...]

    ins = (dfwg, dfwv, dfbg, dfbv, dg2, dgco, dgao, dcw8, dqg, dkg, dsink, sq)
    return _call(body, ins, name="pack_small", grid=(1,), prefetch=(me,),
                 in_specs=[pl.BlockSpec(a.shape, lambda i, me_ref: (0, 0)) for a in ins],
                 out_specs=[pl.BlockSpec((1, 8, P_W), lambda i, me_ref: (me_ref[0], 0, 0))],
                 out_shape=[SDS((N_DEV, 8, P_W), F32)], sem=("arbitrary",))[0]


N_SMALL = 11


def _small_adam(chip, p_all, g1_all, tbl_all, ws, ms, vs):
    fw_cols = 2 * DFF // N_CHIPS
    cw_cols = CW // N_CHIPS

    def body(chip_ref, p_ref, fw_ref, cw0_ref, cw1_ref, cw2_ref, g1_ref, tbl_ref, *refs):
        w_r, m_r, v_r = refs[0:N_SMALL], refs[N_SMALL:2 * N_SMALL], refs[2 * N_SMALL:3 * N_SMALL]
        outs = refs[3 * N_SMALL:]
        g_o, d_o, nm_o, nv_o = (outs[k * N_SMALL:(k + 1) * N_SMALL] for k in range(4))
        loss_o = outs[4 * N_SMALL]

        def total(ref):
            s = ref[0]
            for k in range(1, N_DEV):
                s = s + ref[k]
            return s

        S = total(p_ref)
        fw = total(fw_ref)
        cws = [total(r) for r in (cw0_ref, cw1_ref, cw2_ref)]

        def step(i, g, at):
            d, nm, nv = _adam_math(w_r[i][at], g, m_r[i][at], v_r[i][at])
            g_o[i][at], d_o[i][at], nm_o[i][at], nv_o[i][at] = g, d, nm, nv

        everything = (slice(None), slice(None))
        step(0, total(g1_ref), everything)
        for r in range(3):
            step(1, cws[r][4:5, :], (r, slice(None), slice(None)))
        step(2, S[4:5, C_DQG:C_DQG + HD], everything)
        step(3, S[4:5, C_DKG:C_DKG + HD], everything)
        step(4, total(tbl_ref), everything)
        step(5, S[4:5, C_SINK:C_SINK + NH], everything)
        step(6, S[4:5, C_GCO:C_GCO + CW], everything)
        step(7, S[4:5, C_GAO:C_GAO + AW], everything)
        step(8, S[4:5, C_G2:C_G2 + D], everything)
        for r in range(3):
            step(9, fw[r:r + 1, :], (r, slice(None), slice(None)))
        step(10, S[3:4, 0:2 * DFF], everything)
        sq = S[:, C_SQ:C_SQ + 128]
        loss_o[...] = jnp.sum(jnp.sum(sq, axis=1, keepdims=True), axis=0, keepdims=True) * (0.5 / D)

    def full(a):
        n = len(a.shape)
        return pl.BlockSpec(a.shape, lambda i, chip_ref: (0,) * n)

    params = [*ws, *ms, *vs]
    out = _call(
        body, (p_all, p_all, p_all, p_all, p_all, g1_all, tbl_all, *params), name="small_adam", grid=(1,), prefetch=(chip,),
        in_specs=[full(p_all),
                  pl.BlockSpec((N_DEV, 8, fw_cols), lambda i, chip_ref: (0, 0, chip_ref[0])),
                  *[pl.BlockSpec((N_DEV, 8, cw_cols), lambda i, chip_ref, r=r: (0, 0, (C_DCW + r * CW) // cw_cols + chip_ref[0]))
                    for r in range(3)],
                  full(g1_all), full(tbl_all), *[full(a) for a in params]],
        out_specs=[full(a) for a in ws] * 4 + [pl.BlockSpec((1, 1), lambda i, chip_ref: (0, 0))],
        out_shape=[SDS(a.shape, F32) for a in ws] * 4 + [SDS((1, 1), F32)], sem=("arbitrary",), vmem_mib=32)
    return out[0:N_SMALL], out[N_SMALL:2 * N_SMALL], out[2 * N_SMALL:3 * N_SMALL], out[3 * N_SMALL:4 * N_SMALL], out[4 * N_SMALL]


PLACE_STEPS = 4


def _place_specs(shards):
    rows = [s.shape[0] // PLACE_STEPS for s in shards]
    return ([pl.BlockSpec((r, D), lambda i, chip_ref: (i, 0)) for r in rows],
            [pl.BlockSpec((r, D), lambda i, chip_ref: (chip_ref[0] * PLACE_STEPS + i, 0)) for r in rows],
            [SDS((N_CHIPS * s.shape[0], D), BF) for s in shards])


def _place_first(chip, shard, conv_w, ffn_conv_w):
    def body(chip_ref, a, s0, s1, o, t0, t1):
        o[...] = a[...].astype(BF)

        @pl.when(pl.program_id(0) == 0)
        def _():
            for s, t in ((s0, t0), (s1, t1)):
                t[...] = jnp.zeros_like(t)
                t[0, 0:3, :] = s[...]

    ins, outs, shapes = _place_specs([shard])
    taps = (conv_w, ffn_conv_w)
    return _call(
        body, (shard, conv_w, ffn_conv_w), name="place_first", grid=(PLACE_STEPS,), prefetch=(chip,),
        in_specs=ins + [pl.BlockSpec(s.shape, lambda i, chip_ref: (0, 0)) for s in taps],
        out_specs=outs + [pl.BlockSpec((1, 8, s.shape[1]), lambda i, chip_ref: (chip_ref[0], 0, 0)) for s in taps],
        out_shape=shapes + [SDS((N_CHIPS, 8, s.shape[1]), F32) for s in taps],
        sem=("arbitrary",), vmem_mib=32, free=(1, 2))


def _place_rest(chip, shards, w_up, table, bucket, comm):
    n = len(shards)
    c_up = w_up.shape[1]
    edges = [round(k * (c_up // 128) / PLACE_STEPS) * 128 for k in range(PLACE_STEPS + 1)]

    def body(chip_ref, *refs):
        a, (up_ref, tab_ref, bk_ref), o = refs[:n], refs[n:n + 3], refs[n + 3:2 * n + 3]
        up_o, bias_ref = refs[2 * n + 3:]
        for src, dst in zip(a, o):
            dst[...] = src[...].astype(BF)
        for k in range(PLACE_STEPS):
            @pl.when(pl.program_id(0) == k)
            def _(k=k):
                up_o[edges[k]:edges[k + 1], :] = up_ref[:, edges[k]:edges[k + 1]].T.astype(BF)

        @pl.when(pl.program_id(0) == 0)
        def _():
            bk = bk_ref[...]
            eq = [bk == b for b in range(NBUCKET)]
            for h in range(NH):
                acc = jnp.zeros((BLK, 2 * BLK), F32)
                for b in range(NBUCKET):
                    acc = jnp.where(eq[b], tab_ref[h, b], acc)
                bias_ref[h * BLK:(h + 1) * BLK, :] = acc

    ins, outs, shapes = _place_specs(shards)
    return _call(
        body, (*shards, w_up, table, bucket), name="place_rest", grid=(PLACE_STEPS,), prefetch=(chip,),
        in_specs=ins + [_resident(w_up.shape), pl.BlockSpec(memory_space=pltpu.SMEM),
                        pl.BlockSpec(bucket.shape, lambda i, chip_ref: (0, 0))],
        out_specs=outs + [pl.BlockSpec((c_up, D), lambda i, chip_ref: (chip_ref[0], 0)),
                          pl.BlockSpec((NH * BLK, 2 * BLK), lambda i, chip_ref: (0, 0))],
        out_shape=shapes + [SDS((N_CHIPS * c_up, D), BF), SDS((NH * BLK, 2 * BLK), F32)],
        sem=("arbitrary",), vmem_mib=32, comm=comm, free=(n + 1, n + 2))


def kernel(x, norm_mix_g, w_in, conv_w, q_norm_g, k_norm_g, rel_bias_table, sinks, out_norm_conv_g, out_norm_attn_g, w_out, norm_ffn_g, w_up, ffn_conv_w, ffn_conv_b, w_down, loss_target, m_norm_mix_g, m_w_in, m_conv_w, m_q_norm_g, m_k_norm_g, m_rel_bias_table, m_sinks, m_out_norm_conv_g, m_out_norm_attn_g, m_w_out, m_norm_ffn_g, m_w_up, m_ffn_conv_w, m_ffn_conv_b, m_w_down, v_norm_mix_g, v_w_in, v_conv_w, v_q_norm_g, v_k_norm_g, v_rel_bias_table, v_sinks, v_out_norm_conv_g, v_out_norm_attn_g, v_w_out, v_norm_ffn_g, v_w_up, v_ffn_conv_w, v_ffn_conv_b, v_w_down):
    as_arg = lambda i: jnp.reshape(i, (1,)).astype(jnp.int32)
    chip = as_arg(2 * lax.axis_index("x") + lax.axis_index("y"))
    core = as_arg(lax.axis_index("c"))
    me = 2 * chip + core
    xs, tgt = x[0], loss_target[0]
    qg, kg, gco, gao, g1, g2, fb = q_norm_g, k_norm_g, out_norm_conv_g, out_norm_attn_g, norm_mix_g, norm_ffn_g, ffn_conv_b
    pieces = lambda g: g.reshape(N_DEV, g.shape[0] // N_DEV, D)
    whole = lambda f: f.reshape(2 * f.shape[1], D)

    bucket = jnp.asarray(_bucket_table())
    p_in, p_cw, p_fw = _place_first(chip, w_in[0].T, conv_w[0], ffn_conv_w[0])
    p_out, p_down, p_up, bias, w_int, cw_all, fw_all = _place_rest(
        chip, [w_out[0], w_down[0]], w_up[0], rel_bias_table.T, bucket,
        comm=[_t_gather(p_in), _t_small_weights(p_cw), _t_small_weights(p_fw)])
    cw8 = jnp.transpose(cw_all, (1, 0, 2)).reshape(8, CW)
    fw8 = jnp.transpose(fw_all, (1, 0, 2)).reshape(8, 2 * DFF)

    early = 3 / 11
    proj, u1, w_out_f, p_up = _inproj(xs, g1, w_int, comm=[_t_gather(p_out), _t_gather(p_up, (0, early))])
    y, w_upt = _mix_fwd(proj, sinks, cw8, qg, kg, gco, gao, bias, comm=[_t_gather(p_up, (early, 1))])
    h1, u2 = _outproj(y, w_out_f, xs, g2)
    up, = _ffn_up(u2, w_upt)
    a, w_down_f = _ffn_act(up, fw8, fb, comm=[_t_gather(p_down)])
    dh2, dh2b, sq = _ffn_down(a, w_down_f, h1, tgt)

    gdbf, = _wgrad("wgrad_down", [a], dh2b, None)
    da, sib_down = _ffn_down_bwd(dh2b, w_down_f, comm=[_t_sibling(pieces(gdbf))])
    pbf_down, own_down = _chip_sum("chip_sum_w_down", pieces(gdbf), sib_down, core, chip)
    dug, duv, dfwg, dfwv, dfbg, dfbv, chips_down = _ffn_act_bwd(up, da, fw8, fb, comm=[_t_chips(pbf_down)])
    fin_down, = _final_sum("final_sum_w_down", own_down, chips_down, core)
    gubf, = _wgrad("wgrad_up", [dug, duv], u2, False)
    dh1, dh1b, dg2, sib_up, fin_down = _norm_matmul_bwd(
        "ffn_up_bwd", [dug, duv], w_upt, [0, DFF], h1, g2, dh2, True, comm=[_t_sibling(pieces(gubf)), _t_swap(fin_down)])
    pbf_up, own_up = _chip_sum("chip_sum_w_up", pieces(gubf), sib_up, core, chip)
    gobf, = _wgrad("wgrad_out", [y], dh1b, True)
    dy, sib_out = _out_bwd(dh1b, w_out_f, comm=[_t_sibling(pieces(gobf))])
    pbf_out, own_out = _chip_sum("chip_sum_w_out", pieces(gobf), sib_out, core, chip)
    dproj, dcw8, dqg, dkg, dgco, dgao, dsink, dbias, chips_up = _mix_bwd(
        proj, dy, sinks, cw8, qg, kg, gco, gao, bias, comm=[_t_chips(pbf_up)])
    fin_up, = _final_sum("final_sum_w_up", own_up, chips_up, core)
    tbl_all = _band_bias_bwd(dbias, bucket, me)
    p_all = _pack_small(me, dfwg, dfwv, dfbg, dfbv, dg2, dgco, dgao, dcw8, dqg, dkg, dsink, sq)
    gibf, fin_up, p_all, tbl_all, chips_out = _wgrad(
        "wgrad_in", [dproj], u1, False,
        comm=[_t_swap(fin_up), _t_allgather(p_all), _t_allgather(tbl_all), _t_chips(pbf_out)])
    fin_out, sib_in = _final_sum("final_sum_w_out", own_out, chips_out, core, comm=[_t_sibling(pieces(gibf))])
    pbf_in, own_in = _chip_sum("chip_sum_w_in", pieces(gibf), sib_in, core, chip)
    dx, g1_all, chips_in, fin_out = _norm_matmul_bwd(
        "in_bwd", [dproj], w_int, [0], xs, g1, dh1, False, comm=[_t_chips(pbf_in), _t_swap(fin_out)], slot=me)
    fin_in, = _final_sum("final_sum_w_in", own_in, chips_in, core)
    g1_all, fin_in = _comm_call("gather_last", [_t_allgather(g1_all), _t_swap(fin_in)])

    g_w_out, g_w_down = whole(fin_out), whole(fin_down)
    g_w_down, d_down, nm_down, nv_down = _adamw("adamw_w_down", w_down[0], g_w_down, m_w_down[0], v_w_down[0], 352, True)
    g_w_up, d_up, nm_up, nv_up = _adamw(
        "adamw_w_up", w_up[0], whole(fin_up), m_w_up[0], v_w_up[0], 256, True, stage=False, g_transposed=True)
    g_w_out, d_out, nm_out, nv_out = _adamw("adamw_w_out", w_out[0], g_w_out, m_w_out[0], v_w_out[0], 256, True)
    g_w_in, d_in, nm_in, nv_in = [a.T for a in _adamw(
        "adamw_w_in", w_in[0].T, whole(fin_in), m_w_in[0].T, v_w_in[0].T, INW // N_CHIPS // 3, True)]
    taps = lambda a: jnp.transpose(a, (1, 0, 2))
    sw = [norm_mix_g, taps(conv_w), q_norm_g, k_norm_g, rel_bias_table.T, sinks, out_norm_conv_g, out_norm_attn_g,
          norm_ffn_g, taps(ffn_conv_w), ffn_conv_b]
    smm = [m_norm_mix_g, taps(m_conv_w), m_q_norm_g, m_k_norm_g, m_rel_bias_table.T, m_sinks, m_out_norm_conv_g,
           m_out_norm_attn_g, m_norm_ffn_g, taps(m_ffn_conv_w), m_ffn_conv_b]
    smv = [v_norm_mix_g, taps(v_conv_w), v_q_norm_g, v_k_norm_g, v_rel_bias_table.T, v_sinks, v_out_norm_conv_g,
           v_out_norm_attn_g, v_norm_ffn_g, taps(v_ffn_conv_w), v_ffn_conv_b]
    *small_out, loss = _small_adam(chip, p_all, g1_all, tbl_all, sw, smm, smv)
    sg, sd, snm, snv = [list(r) for r in small_out]
    for r in (sg, sd, snm, snv):
        r[1], r[4], r[9] = taps(r[1]), r[4].T, taps(r[9])

    def order(s, b_in, b_out, b_up, b_down):
        return (s[0], b_in[None], s[1], s[2], s[3], s[4], s[5], s[6], s[7], b_out[None], s[8], b_up[None],
                s[9], s[10], b_down[None])

    return (loss.reshape(()), dx[None],
            *order(sg, g_w_in, g_w_out, g_w_up, g_w_down),
            *order(sd, d_in, d_out, d_up, d_down),
            *order(snm, nm_in, nm_out, nm_up, nm_down),
            *order(snv, nv_in, nv_out, nv_up, nv_down))
```

```python
import functools
import math

import numpy as np

import jax
import jax.numpy as jnp
from jax import lax
from jax.experimental import pallas as pl
from jax.experimental.pallas import tpu as pltpu

F32 = jnp.float32
BF = jnp.bfloat16
SDS = jax.ShapeDtypeStruct

T = 2048
D = 1024
CW = 512
AW = 512
HD = 64
NH = 8
NKV = 2
GQ = 4
INW = 2304
DFF = 2816
BLK = 128
NB = T // BLK
NBUCKET = 32
EPS = 1e-6
NEG_INF = -1e30
N_CHIPS = 4
N_DEV = 8

ADAM_LR = 0.001
ADAM_B1 = 0.9
ADAM_B2 = 0.999
ADAM_EPS = 1e-08
ADAM_WD = 0.01
ADAM_STEP = 10

TM = 512
MIB = 1024 * 1024
MESH = pl.DeviceIdType.MESH
ANY = pl.BlockSpec(memory_space=pl.ANY)

_pcall = pl.pallas_call


def _params(sem=None, vmem_mib=None, collective_id=None):
    kw = {} if collective_id is None else {"collective_id": collective_id}
    if sem is not None:
        kw["dimension_semantics"] = sem
    if vmem_mib is not None:
        kw["vmem_limit_bytes"] = vmem_mib * MIB
    return pltpu.CompilerParams(**kw)


def _resident(shape):
    return pl.BlockSpec(shape, lambda *_: (0,) * len(shape), pipeline_mode=pl.Buffered(1))


def _dot(a, b, ca, cb):
    return lax.dot_general(a, b, (((ca,), (cb,)), ((), ())), preferred_element_type=F32)


def _rms_bwd(dy, x, r, g):
    dg = jnp.sum(dy * (x * r), axis=0, keepdims=True)
    dgx = dy * g
    dx = r * dgx - x * (r * r * r) * jnp.mean(x * dgx, axis=-1, keepdims=True)
    return dx, dg


def _where():
    x, y, c = lax.axis_index("x"), lax.axis_index("y"), lax.axis_index("c")
    return x, y, c, [(1 - x, y), (x, 1 - y), (1 - x, 1 - y)]


def _rcopy(src, dst, ssem, rsem, dev):
    return pltpu.make_async_remote_copy(src_ref=src, dst_ref=dst, send_sem=ssem, recv_sem=rsem, device_id=dev,
                                        device_id_type=MESH)


SIBLING, Y_CHIP, X_CHIP, DIAGONAL_CHIP = 1, 2, 4, 6
OTHER_CHIPS = (Y_CHIP, X_CHIP, DIAGONAL_CHIP)
EVERYONE = tuple(range(1, N_DEV))
BARRIER_OF = {(SIBLING,): 0, (SIBLING, Y_CHIP, X_CHIP): 1, OTHER_CHIPS: 2, (SIBLING,) + OTHER_CHIPS: 3, EVERYONE: 4}


def _peer(rel):
    x, y, c, _ = _where()
    return x ^ ((rel >> 2) & 1), y ^ ((rel >> 1) & 1), c ^ (rel & 1)


class _Task:
    def __init__(self, ins, outs, alias, n_sem, start, finish, middle=None, peers=()):
        self.ins, self.outs, self.alias, self.n_sem, self.start, self.finish = ins, outs, alias, n_sem, start, finish
        self.middle = middle if middle is not None else (lambda *args: None)
        self.peers = peers


def _peers_of(comm):
    return tuple(sorted({p for t in comm for p in t.peers}))


def _enter(comm):
    peers = _peers_of(comm)
    barrier = pltpu.get_barrier_semaphore()
    for rel in peers:
        pl.semaphore_signal(barrier, inc=1, device_id=_peer(rel), device_id_type=MESH)
    pl.semaphore_wait(barrier, len(peers))


ROWS16 = 16


def _t_gather(placed, part=(0, 1)):
    R = placed.shape[0] // N_CHIPS
    q = R // 4
    lo, hi = (round(f * (q // ROWS16)) * ROWS16 for f in part)

    def quarter(chip_index, core, k):
        return pl.ds(pl.multiple_of(chip_index * R + core * 2 * q + k * q + lo, ROWS16), hi - lo)

    def places():
        x, y, c, _ = _where()
        return c, 2 * x + y, 2 * (1 - x) + y, 2 * x + (1 - y), 2 * (1 - x) + (1 - y), (1 - x, y, c), (x, 1 - y, c), (x, y, 1 - c)

    def copy(buf, k, chip_index, core, quart, ss, rs, b, dev):
        window = buf.at[quarter(chip_index, core, quart)]
        return _rcopy(window, window, ss.at[b + k], rs.at[b + k], dev)

    def start(cin, cout, ss, rs, b):
        c, me, _, _, _, x_nbr, y_nbr, _ = places()
        for k, (quart, dev) in enumerate(((0, x_nbr), (1, y_nbr), (1, x_nbr), (0, y_nbr))):
            copy(cout[0], k, me, c, quart, ss, rs, b, dev).start()

    def middle(cin, cout, ss, rs, b):
        c, _, xc, yc, _, x_nbr, y_nbr, sib = places()
        for k, chip_index, quart, dev in ((0, xc, 0, y_nbr), (1, yc, 1, x_nbr)):
            copy(cout[0], k, chip_index, c, quart, ss, rs, b, dev).wait_recv()
            copy(cout[0], 4 + k, chip_index, c, quart, ss, rs, b, dev).start()
            copy(cout[0], 6 + k, chip_index, c, quart, ss, rs, b, sib).start()

    later = ((2, 1, 1), (3, 2, 0), (4, 3, 0), (5, 3, 1))

    def finish(cin, cout, ss, rs, b):
        c, me, xc, yc, dc, _, _, sib = places()
        chip_of = {1: xc, 2: yc, 3: dc}
        for k, whose, quart in later:
            copy(cout[0], k, chip_of[whose], c, quart, ss, rs, b, sib).wait_recv()
            copy(cout[0], 6 + k, chip_of[whose], c, quart, ss, rs, b, sib).start()
        for k, whose, quart in ((0, 1, 0), (1, 2, 1)) + later:
            copy(cout[0], 6 + k, chip_of[whose], 1 - c, quart, ss, rs, b, sib).wait_recv()
        for k in range(12):
            copy(cout[0], k, me, c, 0, ss, rs, b, sib).wait_send()

    return _Task([placed], [SDS(placed.shape, placed.dtype)], [(0, 0)], 12, start, finish, middle, peers=(SIBLING, Y_CHIP, X_CHIP))


def _t_small_weights(buf):
    def start(cin, cout, ss, rs, b):
        x, y, c, chips = _where()
        mine = cout[0].at[2 * x + y]
        for r, (px, py) in enumerate(chips):
            _rcopy(mine, mine, ss.at[b + r], rs.at[b + r], (px, py, c)).start()

    def finish(cin, cout, ss, rs, b):
        x, y, c, chips = _where()
        for r, (px, py) in enumerate(chips):
            got = cout[0].at[2 * px + py]
            _rcopy(got, got, ss.at[b + r], rs.at[b + r], (px, py, c)).wait_recv()
        for r, (px, py) in enumerate(chips):
            mine = cout[0].at[2 * x + y]
            _rcopy(mine, mine, ss.at[b + r], rs.at[b + r], (px, py, c)).wait_send()

    return _Task([buf], [SDS(buf.shape, buf.dtype)], [(0, 0)], 3, start, finish, peers=OTHER_CHIPS)


def _t_sibling(gbf):
    def start(cin, cout, ss, rs, b):
        x, y, c, _ = _where()
        for jj in range(N_CHIPS):
            _rcopy(cin[0].at[2 * jj + (1 - c)], cout[0].at[jj], ss.at[b + jj], rs.at[b + jj], (x, y, 1 - c)).start()

    def finish(cin, cout, ss, rs, b):
        x, y, c, _ = _where()
        for jj in range(N_CHIPS):
            got = cout[0].at[jj]
            _rcopy(got, got, ss.at[b + jj], rs.at[b + jj], (x, y, 1 - c)).wait_recv()
        for jj in range(N_CHIPS):
            got = cout[0].at[jj]
            _rcopy(got, got, ss.at[b + jj], rs.at[b + jj], (x, y, 1 - c)).wait_send()

    return _Task([gbf], [SDS((N_CHIPS,) + gbf.shape[1:], BF)], [], N_CHIPS, start, finish, peers=(SIBLING,))


def _t_chips(pbf):
    def start(cin, cout, ss, rs, b):
        x, y, c, chips = _where()
        for r, (px, py) in enumerate(chips):
            _rcopy(cin[0].at[2 * px + py], cout[0].at[r], ss.at[b + r], rs.at[b + r], (px, py, c)).start()

    def finish(cin, cout, ss, rs, b):
        x, y, c, chips = _where()
        for r, (px, py) in enumerate(chips):
            got = cout[0].at[r]
            _rcopy(got, got, ss.at[b + r], rs.at[b + r], (px, py, c)).wait_recv()
        for r, (px, py) in enumerate(chips):
            got = cout[0].at[r]
            _rcopy(got, got, ss.at[b + r], rs.at[b + r], (px, py, c)).wait_send()

    return _Task([pbf], [SDS((3,) + pbf.shape[1:], BF)], [], 3, start, finish, peers=OTHER_CHIPS)


def _t_swap(fin):
    def start(cin, cout, ss, rs, b):
        x, y, c, _ = _where()
        mine = cout[0].at[c]
        _rcopy(mine, mine, ss.at[b], rs.at[b], (x, y, 1 - c)).start()

    def finish(cin, cout, ss, rs, b):
        x, y, c, _ = _where()
        got = cout[0].at[1 - c]
        _rcopy(got, got, ss.at[b], rs.at[b], (x, y, 1 - c)).wait_recv()
        _rcopy(got, got, ss.at[b], rs.at[b], (x, y, 1 - c)).wait_send()

    return _Task([fin], [SDS(fin.shape, fin.dtype)], [(0, 0)], 1, start, finish, peers=(SIBLING,))


def _t_allgather(buf):
    def peers():
        x, y, c, _ = _where()
        out = []
        for rel in range(1, N_DEV):
            px, py, pc = x ^ ((rel >> 2) & 1), y ^ ((rel >> 1) & 1), c ^ (rel & 1)
            out.append((rel - 1, 4 * px + 2 * py + pc, (px, py, pc)))
        return 4 * x + 2 * y + c, out

    def start(cin, cout, ss, rs, b):
        me, ps = peers()
        mine = cout[0].at[me]
        for k, _, dev in ps:
            _rcopy(mine, mine, ss.at[b + k], rs.at[b + k], dev).start()

    def finish(cin, cout, ss, rs, b):
        me, ps = peers()
        for k, pidx, dev in ps:
            got = cout[0].at[pidx]
            _rcopy(got, got, ss.at[b + k], rs.at[b + k], dev).wait_recv()
        for k, _, dev in ps:
            mine = cout[0].at[me]
            _rcopy(mine, mine, ss.at[b + k], rs.at[b + k], dev).wait_send()

    return _Task([buf], [SDS(buf.shape, buf.dtype)], [(0, 0)], N_DEV - 1, start, finish, peers=EVERYONE)


def _run_tasks(comm, which, cin, cout, ss, rs):
    i0 = o0 = s0 = 0
    for t in comm:
        getattr(t, which)(cin[i0:i0 + len(t.ins)], cout[o0:o0 + len(t.outs)], ss, rs, s0)
        i0, o0, s0 = i0 + len(t.ins), o0 + len(t.outs), s0 + t.n_sem


def _from_hbm(*arrays):
    return [pltpu.with_memory_space_constraint(a, pltpu.HBM) for a in arrays]


def _in_hbm(shapes):
    return [pltpu.HBM(s.shape, s.dtype) for s in shapes]


def _comm_layout(comm, n_in, n_out):
    c_in = [a for t in comm for a in t.ins]
    c_out = [s for t in comm for s in t.outs]
    aliases, i0, o0 = {}, 0, 0
    for t in comm:
        for i, o in t.alias:
            aliases[n_in + i0 + i] = n_out + o0 + o
        i0, o0 = i0 + len(t.ins), o0 + len(t.outs)
    return c_in, c_out, aliases, sum(t.n_sem for t in comm)


def _call(body, operands, *, name, grid, in_specs, out_specs, out_shape, scratch_shapes=(), sem=None, vmem_mib=None, comm=(),
          free=(), prefetch=()):
    operands = [o if s.memory_space == pltpu.SMEM or k in free else pltpu.with_memory_space_constraint(o, pltpu.HBM)
                for k, (o, s) in enumerate(zip(operands, in_specs))]
    n_pre, n_in, n_out, n_scr = len(prefetch), len(in_specs), len(out_specs), len(scratch_shapes)
    c_in, c_out, aliases, n_sem = _comm_layout(comm, n_pre + n_in, n_out)
    sems = [pltpu.SemaphoreType.DMA((n_sem,)), pltpu.SemaphoreType.DMA((n_sem,))] if comm else []

    def wrapped(*refs):
        pre, refs = refs[:n_pre], refs[n_pre:]
        ins, cin = refs[:n_in], refs[n_in:n_in + len(c_in)]
        rest = refs[n_in + len(c_in):]
        outs, cout = rest[:n_out], rest[n_out:n_out + len(c_out)]
        rest = rest[n_out + len(c_out):]
        scr, csem = rest[:n_scr], rest[n_scr:]
        if not comm:
            return body(*pre, *ins, *outs, *scr)
        step = functools.reduce(lambda acc, k: acc * grid[k] + pl.program_id(k), range(len(grid)), 0)
        n_steps = math.prod(grid)

        @pl.when(step == 0)
        def _():
            _enter(comm)
            _run_tasks(comm, "start", cin, cout, *csem)

        pl.when(step == n_steps // 2)(lambda: _run_tasks(comm, "middle", cin, cout, *csem))
        body(*pre, *ins, *outs, *scr)
        pl.when(step == n_steps - 1)(lambda: _run_tasks(comm, "finish", cin, cout, *csem))

    grid_spec = pltpu.PrefetchScalarGridSpec(
        num_scalar_prefetch=n_pre, grid=grid, in_specs=list(in_specs) + [ANY] * len(c_in),
        out_specs=list(out_specs) + [ANY] * len(c_out), scratch_shapes=list(scratch_shapes) + sems)
    return _pcall(
        wrapped, name=name, grid_spec=grid_spec, out_shape=_in_hbm(list(out_shape) + c_out), input_output_aliases=aliases,
        compiler_params=_params(("arbitrary",) * len(grid) if comm else sem, vmem_mib,
                                BARRIER_OF[_peers_of(comm)] if comm else None),
    )(*prefetch, *operands, *_from_hbm(*c_in))


def _comm_call(name, comm):
    c_in, c_out, aliases, n_sem = _comm_layout(comm, 0, 0)

    def body(*refs):
        cin, cout, (ss, rs) = refs[:len(c_in)], refs[len(c_in):len(c_in) + len(c_out)], refs[len(c_in) + len(c_out):]
        _enter(comm)
        for phase in ("start", "middle", "finish"):
            _run_tasks(comm, phase, cin, cout, ss, rs)

    return _pcall(
        body, name=name, in_specs=[ANY] * len(c_in), out_specs=[ANY] * len(c_out), out_shape=_in_hbm(c_out),
        scratch_shapes=[pltpu.SemaphoreType.DMA((n_sem,)), pltpu.SemaphoreType.DMA((n_sem,))],
        input_output_aliases=aliases, compiler_params=_params(collective_id=BARRIER_OF[_peers_of(comm)]),
    )(*_from_hbm(*c_in))


def _inproj(x, g1, w_int, comm=()):
    tm = TM

    def body(x_ref, g_ref, w_ref, proj_ref, u_ref):
        xf = x_ref[...]
        r = lax.rsqrt(jnp.mean(xf * xf, axis=-1, keepdims=True) + EPS)
        u = (xf * r * g_ref[...]).astype(BF)
        u_ref[...] = u
        proj_ref[...] = _dot(u, w_ref[...], 1, 1)

    return _call(
        body, (x, g1, w_int), name="inproj", grid=(T // tm,),
        in_specs=[pl.BlockSpec((tm, D), lambda i: (i, 0)), pl.BlockSpec((1, D), lambda i: (0, 0)),
                  _resident((INW, D))],
        out_specs=[pl.BlockSpec((tm, INW), lambda i: (i, 0)), pl.BlockSpec((tm, D), lambda i: (i, 0))],
        out_shape=[SDS((T, INW), F32), SDS((T, D), BF)], sem=("parallel",), vmem_mib=40, comm=comm, free=(0, 1))


def _outproj(y, w_out, x, g2):
    tm = TM

    def body(y_ref, w_ref, x_ref, g_ref, h1_ref, u2_ref):
        h1 = x_ref[...] + _dot(y_ref[...], w_ref[...], 1, 0)
        h1_ref[...] = h1
        r = lax.rsqrt(jnp.mean(h1 * h1, axis=-1, keepdims=True) + EPS)
        u2_ref[...] = (h1 * r * g_ref[...]).astype(BF)

    return _call(
        body, (y, w_out, x, g2), name="outproj", grid=(T // tm,),
        in_specs=[pl.BlockSpec((tm, D), lambda i: (i, 0)), _resident((D, D)),
                  pl.BlockSpec((tm, D), lambda i: (i, 0)), pl.BlockSpec((1, D), lambda i: (0, 0))],
        out_specs=[pl.BlockSpec((tm, D), lambda i: (i, 0)), pl.BlockSpec((tm, D), lambda i: (i, 0))],
        out_shape=[SDS((T, D), F32), SDS((T, D), BF)], sem=("parallel",), vmem_mib=32, free=(2, 3))


def _ffn_up(u2, w_upt, comm=()):
    tm, tn = 1024, 512

    def body(u_ref, w_ref, o_ref):
        o_ref[...] = _dot(u_ref[...], w_ref[...], 1, 1).astype(BF)

    return _call(
        body, (u2, w_upt), name="ffn_up", grid=(T // tm, 2 * DFF // tn),
        in_specs=[pl.BlockSpec((tm, D), lambda i, j: (i, 0)), pl.BlockSpec((tn, D), lambda i, j: (j, 0))],
        out_specs=[pl.BlockSpec((tm, tn), lambda i, j: (i, j))], out_shape=[SDS((T, 2 * DFF), BF)],
        sem=("parallel", "parallel"), vmem_mib=32, comm=comm, free=(1,))


def _ffn_down(a, w_down, h1, tgt):
    tm = TM

    def body(a_ref, w_ref, h1_ref, t_ref, dh_ref, dhb_ref, l_ref):
        @pl.when(pl.program_id(0) == 0)
        def _():
            l_ref[...] = jnp.zeros_like(l_ref)

        h2 = h1_ref[...] + _dot(a_ref[...], w_ref[...], 1, 0)
        e = h2 - t_ref[...]
        dh = e * (1.0 / D)
        dh_ref[...] = dh
        dhb_ref[...] = dh.astype(BF)
        e2 = jnp.sum((e * e).reshape(tm // 8, 8, D), axis=0)
        acc = e2[:, 0:128]
        for k in range(1, D // 128):
            acc = acc + e2[:, k * 128:(k + 1) * 128]
        l_ref[...] += acc

    return _call(
        body, (a, w_down, h1, tgt), name="ffn_down", grid=(T // tm,),
        in_specs=[pl.BlockSpec((tm, DFF), lambda i: (i, 0)), _resident((DFF, D)),
                  pl.BlockSpec((tm, D), lambda i: (i, 0)), pl.BlockSpec((tm, D), lambda i: (i, 0))],
        out_specs=[pl.BlockSpec((tm, D), lambda i: (i, 0)), pl.BlockSpec((tm, D), lambda i: (i, 0)),
                   pl.BlockSpec((8, 128), lambda i: (0, 0))],
        out_shape=[SDS((T, D), F32), SDS((T, D), BF), SDS((8, 128), F32)], sem=("arbitrary",), vmem_mib=40, free=(2, 3))


def _bucket_table():
    q = np.arange(BLK, dtype=np.int32)[:, None]
    j = np.arange(2 * BLK, dtype=np.int32)[None, :]
    n = np.maximum(q + BLK - j, 0)
    nf = np.maximum(n, 1).astype(np.float32)
    max_exact = NBUCKET // 2
    large = max_exact + (np.log(nf / np.float32(max_exact)) / np.float32(math.log(BLK / max_exact))
                         * np.float32(NBUCKET - max_exact)).astype(np.int32)
    large = np.minimum(large, NBUCKET - 1)
    return np.where(n < max_exact, n, large).astype(np.int32)


def _band_bias_bwd(dbias, bucket, me):
    def body(me_ref, db_ref, bk_ref, o_ref):
        bk = bk_ref[...]
        for b in range(NBUCKET):
            m = bk == b
            for h in range(NH):
                v = jnp.where(m, db_ref[h * BLK:(h + 1) * BLK, :], 0.0)
                s = jnp.sum(jnp.sum(v, axis=1, keepdims=True), axis=0, keepdims=True)
                o_ref[0, h:h + 1, b:b + 1] = s

    grid_spec = pltpu.PrefetchScalarGridSpec(
        num_scalar_prefetch=1, grid=(1,),
        in_specs=[pl.BlockSpec((NH * BLK, 2 * BLK), lambda i, me_ref: (0, 0)),
                  pl.BlockSpec((BLK, 2 * BLK), lambda i, me_ref: (0, 0))],
        out_specs=pl.BlockSpec((1, NH, NBUCKET), lambda i, me_ref: (me_ref[0], 0, 0)),
    )
    return _pcall(body, name="band_bias_bwd", grid_spec=grid_spec, out_shape=SDS((N_DEV, NH, NBUCKET), F32),
                  compiler_params=_params(("arbitrary",)))(me, dbias, bucket)


def _two_bf16(x):
    hi = x.astype(BF)
    return hi, (x - hi.astype(F32)).astype(BF)


def _head_sums(x, seg):
    hi, lo = _two_bf16(x)
    s = seg[0:x.shape[1], :]
    return _dot(hi, s, 1, 0) + _dot(lo, s, 1, 0)


def _head_spread(v, seg, width):
    hi, lo = _two_bf16(v)
    s = seg[0:width, :]
    return _dot(hi, s, 1, 1) + _dot(lo, s, 1, 1)


def _head_norm(x, g_t, seg, by_head=False):
    if by_head:
        heads = [x[:, h * HD:(h + 1) * HD] for h in range(x.shape[1] // HD)]
        r = jnp.concatenate([jnp.broadcast_to(lax.rsqrt(jnp.mean(v * v, axis=-1, keepdims=True) + EPS), v.shape)
                             for v in heads], axis=1)
    else:
        r = lax.rsqrt(_head_sums(x * x, seg) * (1.0 / HD) + EPS)
        r = _head_spread(r, seg, x.shape[1])
    return x * r * g_t, r


def _head_norm_bwd(dy, x, r, g_t, seg):
    dg_t = jnp.sum(dy * (x * r), axis=0, keepdims=True)
    dgx = dy * g_t
    mean = _head_spread(_head_sums(x * dgx, seg) * (1.0 / HD), seg, x.shape[1])
    return r * dgx - x * (r * r * r) * mean, dg_t


def _fold_heads(v):
    out = v[:, 0:HD]
    for h in range(1, v.shape[1] // HD):
        out = out + v[:, h * HD:(h + 1) * HD]
    return out


def _mix_forward(P, zc8, zh8, pkv, first, cw, qg_t, kg_t, gco, gao, seg, sink_ref, bias_ref, by_head=False):
    gate_b = P[:, 0:CW]
    gate_c = P[:, CW:2 * CW]
    hc = P[:, 2 * CW:3 * CW]
    z = gate_c * hc
    keep = jnp.where(first, 0.0, 1.0)
    zp = zc8 * zh8 * keep
    p1 = zp[7:8, :]
    p2 = zp[6:7, :]
    row = lax.broadcasted_iota(jnp.int32, (BLK, 1), 0)
    z1 = jnp.where(row == 0, p1, pltpu.roll(z, 1, 0))
    z2 = jnp.where(row == 0, p2, jnp.where(row == 1, p1, pltpu.roll(z, 2, 0)))
    cz = cw[0:1, :] * z2 + cw[1:2, :] * z1 + cw[2:3, :] * z
    y_conv = gate_b * cz

    scale = HD ** -0.5
    qi = lax.broadcasted_iota(jnp.int32, (BLK, 2 * BLK), 0)
    kj = lax.broadcasted_iota(jnp.int32, (BLK, 2 * BLK), 1)
    dd = qi + BLK - kj
    first_key = jnp.where(first, BLK, 0)
    valid = (dd >= 0) & (dd < BLK) & (kj >= first_key)

    q0 = 3 * CW
    k0 = q0 + AW
    v0 = k0 + NKV * HD
    q_raw = P[:, q0:k0]
    qn, rq = _head_norm(q_raw, qg_t, seg, by_head)
    qs = (qn * scale).astype(BF)
    k_raw = jnp.concatenate([pkv[:, 0:NKV * HD], P[:, k0:v0]], axis=0)
    kn, rk = _head_norm(k_raw, kg_t, seg, by_head)
    knb = kn.astype(BF)
    heads = []
    for h in range(NH):
        kv = h // GQ
        kb = knb[:, kv * HD:(kv + 1) * HD]
        vb = jnp.concatenate([pkv[:, NKV * HD + kv * HD:NKV * HD + (kv + 1) * HD],
                              P[:, v0 + kv * HD:v0 + (kv + 1) * HD]], axis=0).astype(BF)
        Q = qs[:, h * HD:(h + 1) * HD]
        S = _dot(Q, kb, 1, 1) + bias_ref[h * BLK:(h + 1) * BLK, :]
        S = jnp.where(valid, S, NEG_INF)
        sink = sink_ref[0, h]
        m = jnp.maximum(jnp.max(S, axis=-1, keepdims=True), sink)
        p = jnp.exp(S - m)
        es = jnp.exp(sink - m)
        denom = jnp.sum(p, axis=-1, keepdims=True) + es
        probs = p / denom
        O = _dot(probs.astype(BF), vb, 1, 0)
        heads.append(dict(kb=kb, vb=vb, Q=Q, probs=probs, psink=es / denom, O=O))
    y_attn = jnp.concatenate([hd["O"] for hd in heads], axis=1)

    rc = lax.rsqrt(jnp.mean(y_conv * y_conv, axis=-1, keepdims=True) + EPS)
    ra = lax.rsqrt(jnp.mean(y_attn * y_attn, axis=-1, keepdims=True) + EPS)
    y = jnp.concatenate([y_conv * rc * gco, y_attn * ra * gao], axis=1)
    return dict(gate_b=gate_b, gate_c=gate_c, hc=hc, z=z, z1=z1, z2=z2, cz=cz, y_conv=y_conv, y_attn=y_attn,
                rc=rc, ra=ra, heads=heads, y=y, row=row, scale=scale, q_raw=q_raw, rq=rq, k_raw=k_raw, rk=rk)


BPS = 2
TILE = BPS * BLK
KV0 = 3 * CW + AW


def _mix_in_specs(tile_of):
    return [
        pl.BlockSpec(memory_space=pltpu.SMEM),
        pl.BlockSpec((TILE, INW), lambda s: (tile_of(s), 0)),
        pl.BlockSpec((8, CW), lambda s: (jnp.maximum(tile_of(s) * (TILE // 8) - 1, 0), 1)),
        pl.BlockSpec((8, CW), lambda s: (jnp.maximum(tile_of(s) * (TILE // 8) - 1, 0), 2)),
        pl.BlockSpec((BLK, 2 * NKV * HD), lambda s: (jnp.maximum(tile_of(s) * BPS - 1, 0), KV0 // (2 * NKV * HD))),
    ]


def _block_inputs(tile, b, zc_ref, zh_ref, pkv_ref, first_tile):
    P = tile[b * BLK:(b + 1) * BLK, :]
    if b == 0:
        return P, zc_ref[...], zh_ref[...], pkv_ref[...], first_tile
    lo = b * BLK
    return P, tile[lo - 8:lo, CW:2 * CW], tile[lo - 8:lo, 2 * CW:3 * CW], tile[lo - BLK:lo, KV0:KV0 + 2 * NKV * HD], False


def _mix_param_specs():
    return [
        pl.BlockSpec((8, CW), lambda s: (0, 0)),
        pl.BlockSpec((1, AW), lambda s: (0, 0)),
        pl.BlockSpec((1, NKV * HD), lambda s: (0, 0)),
        pl.BlockSpec((1, CW), lambda s: (0, 0)),
        pl.BlockSpec((1, AW), lambda s: (0, 0)),
        pl.BlockSpec((AW, 128), lambda s: (0, 0)),
        pl.BlockSpec((NH * BLK, 2 * BLK), lambda s: (0, 0)),
    ]


def _mix_params(cw8, qg, kg, gco, gao, bias):
    seg = np.zeros((AW, 128), np.float32)
    seg[np.arange(AW), np.arange(AW) // HD] = 1.0
    return (cw8, jnp.tile(qg, (1, NH)), jnp.tile(kg, (1, NKV)), gco, gao, jnp.asarray(seg, BF), bias)


def _mix_fwd(proj, sinks, cw8, qg, kg, gco, gao, bias, comm=()):
    def body(sink_ref, p_ref, zc_ref, zh_ref, pkv_ref, cw_ref, qg_ref, kg_ref, gco_ref, gao_ref, seg_ref, bias_ref, y_ref):
        tile = p_ref[...]
        for b in range(BPS):
            f = _mix_forward(*_block_inputs(tile, b, zc_ref, zh_ref, pkv_ref, pl.program_id(0) == 0), cw_ref[...],
                             qg_ref[...], kg_ref[...], gco_ref[...], gao_ref[...], seg_ref[...], sink_ref, bias_ref, by_head=True)
            y_ref[b * BLK:(b + 1) * BLK, :] = f["y"].astype(BF)

    return _call(
        body, (sinks, proj, proj, proj, proj, *_mix_params(cw8, qg, kg, gco, gao, bias)), name="mix_fwd", grid=(T // TILE,),
        in_specs=_mix_in_specs(lambda s: s) + _mix_param_specs(),
        out_specs=[pl.BlockSpec((TILE, D), lambda s: (s, 0))], out_shape=[SDS((T, D), BF)],
        sem=("parallel",), vmem_mib=40, comm=comm, free=tuple(range(5, 12)))


def _mix_bwd(proj, dy, sinks, cw8, qg, kg, gco, gao, bias, comm=()):
    n_steps = T // TILE

    def tile_of(s):
        return n_steps - 1 - s

    def body(sink_ref, p_ref, zc_ref, zh_ref, pkv_ref, dy_ref, cw_ref, qg_ref, kg_ref, gco_ref, gao_ref, seg_ref, bias_ref,
             dproj_ref, dcw_ref, dqg_ref, dkg_ref, dgco_ref, dgao_ref, dsink_ref, dbias_ref,
             ndcz_ref, dkc_ref, dvc_ref):
        s = pl.program_id(0)

        @pl.when(s == 0)
        def _():
            for r in (dcw_ref, dqg_ref, dkg_ref, dgco_ref, dgao_ref, dsink_ref, dbias_ref, ndcz_ref, dkc_ref, dvc_ref):
                r[...] = jnp.zeros_like(r)

        params = (cw_ref[...], qg_ref[...], kg_ref[...], gco_ref[...], gao_ref[...], seg_ref[...])
        tile = p_ref[...]
        carry = (ndcz_ref[...], dkc_ref[...], dvc_ref[...])
        total = None
        for b in reversed(range(BPS)):
            f = _mix_forward(*_block_inputs(tile, b, zc_ref, zh_ref, pkv_ref, s == n_steps - 1), *params, sink_ref, bias_ref)
            pieces, sums, carry = one_block(f, dy_ref[b * BLK:(b + 1) * BLK, :], params, carry)
            for lo, piece in pieces:
                dproj_ref[b * BLK:(b + 1) * BLK, lo:lo + piece.shape[1]] = piece
            total = sums if total is None else [t + v for t, v in zip(total, sums)]
        ndcz_ref[...], dkc_ref[...], dvc_ref[...] = carry
        dcw, dqg_t, dkg_t, dgco, dgao, dsink, *ds = total
        dcw_ref[0:3, :] += dcw
        dqg_ref[...] += _fold_heads(dqg_t)
        dkg_ref[...] += _fold_heads(dkg_t)
        dgco_ref[...] += dgco
        dgao_ref[...] += dgao
        dsink_ref[...] += dsink
        for h in range(NH):
            dbias_ref[h * BLK:(h + 1) * BLK, :] += ds[h]

    def one_block(f, dy, params, carry):
        cw, qg_v, kg_v, gco_v, gao_v, seg = params
        nxt, dk_carry, dv_carry = carry
        dyc, dgco = _rms_bwd(dy[:, 0:CW], f["y_conv"], f["rc"], gco_v)
        dya, dgao = _rms_bwd(dy[:, CW:CW + AW], f["y_attn"], f["ra"], gao_v)

        row = f["row"]
        dgate_b = dyc * f["cz"]
        dcz = dyc * f["gate_b"]
        dcw = jnp.concatenate([jnp.sum(dcz * f[k], axis=0, keepdims=True) for k in ("z2", "z1", "z")], axis=0)
        n0 = nxt[0:1, :]
        n1 = nxt[1:2, :]
        d1 = jnp.where(row == BLK - 1, n0, pltpu.roll(dcz, BLK - 1, 0))
        d2 = jnp.where(row == BLK - 1, n1, jnp.where(row == BLK - 2, n0, pltpu.roll(dcz, BLK - 2, 0)))
        dz = cw[2:3, :] * dcz + cw[1:2, :] * d1 + cw[0:1, :] * d2
        pieces = [(0, dgate_b.astype(BF)), (CW, (dz * f["hc"]).astype(BF)), (2 * CW, (dz * f["gate_c"]).astype(BF))]

        scale = f["scale"]
        lane = lax.broadcasted_iota(jnp.int32, (1, 128), 1)
        dsink = jnp.zeros((1, 128), F32)
        dq_cols, dk_cols, dv_cols, dk_prev, dv_prev, ds = [], [], [], [], [], []
        for kv in range(NKV):
            dKb = dVb = 0.0
            for h in range(kv * GQ, (kv + 1) * GQ):
                hd = f["heads"][h]
                dO = dya[:, h * HD:(h + 1) * HD]
                delta = jnp.sum(dO * hd["O"], axis=-1, keepdims=True)
                dOb = dO.astype(BF)
                dP = _dot(dOb, hd["vb"], 1, 1)
                dS = hd["probs"] * (dP - delta)
                tot = jnp.sum(hd["psink"] * delta, axis=0, keepdims=True)
                dsink = dsink - jnp.where(lane == h, tot, 0.0)
                ds.append(dS)
                dSb = dS.astype(BF)
                dq_cols.append(_dot(dSb, hd["kb"], 1, 0))
                dKb = dKb + _dot(dSb, hd["Q"], 0, 0)
                dVb = dVb + _dot(hd["probs"].astype(BF), dOb, 0, 0)
            dk_cols.append(dKb[BLK:, :] + dk_carry[:, kv * HD:(kv + 1) * HD])
            dv_cols.append(dVb[BLK:, :] + dv_carry[:, kv * HD:(kv + 1) * HD])
            dk_prev.append(dKb[:BLK, :])
            dv_prev.append(dVb[:BLK, :])
        dq_raw, dqg_t = _head_norm_bwd(jnp.concatenate(dq_cols, axis=1) * scale, f["q_raw"], f["rq"], qg_v, seg)
        dk_raw, dkg_t = _head_norm_bwd(jnp.concatenate(dk_cols, axis=1), f["k_raw"][BLK:, :], f["rk"][BLK:, :], kg_v, seg)
        pieces.append((3 * CW, jnp.concatenate([dq_raw, dk_raw] + dv_cols, axis=1).astype(BF)))
        owed = (dcz[0:8, :], jnp.concatenate(dk_prev, axis=1), jnp.concatenate(dv_prev, axis=1))
        return pieces, [dcw, dqg_t, dkg_t, dgco, dgao, dsink, *ds], owed

    small = lambda r, c: pl.BlockSpec((r, c), lambda s: (0, 0))
    return _call(
        body, (sinks, proj, proj, proj, proj, dy, *_mix_params(cw8, qg, kg, gco, gao, bias)), name="mix_bwd", grid=(n_steps,),
        in_specs=_mix_in_specs(tile_of) + [pl.BlockSpec((TILE, D), lambda s: (tile_of(s), 0))] + _mix_param_specs(),
        out_specs=[pl.BlockSpec((TILE, INW), lambda s: (tile_of(s), 0)), small(8, CW), small(1, HD), small(1, HD),
                   small(1, CW), small(1, AW), small(1, 128), small(NH * BLK, 2 * BLK)],
        out_shape=[SDS((T, INW), BF), SDS((8, CW), F32), SDS((1, HD), F32), SDS((1, HD), F32), SDS((1, CW), F32),
                   SDS((1, AW), F32), SDS((1, 128), F32), SDS((NH * BLK, 2 * BLK), F32)],
        scratch_shapes=[pltpu.VMEM((8, CW), F32), pltpu.VMEM((BLK, NKV * HD), F32), pltpu.VMEM((BLK, NKV * HD), F32)],
        sem=("arbitrary",), vmem_mib=56, comm=comm, free=(1, 2, 3, 4) + tuple(range(6, 13)))


FT = 256
NFT = DFF // FT
RC = 128
NCH = T // RC
LEAD = 16


def _rows8(x):
    return jnp.sum(x.reshape(x.shape[0] // 8, 8, x.shape[1]), axis=0)


def _ffn_act_specs():
    return [
        pl.BlockSpec((T, FT), lambda j: (0, j)), pl.BlockSpec((T, FT), lambda j: (0, NFT + j)),
        pl.BlockSpec((8, FT), lambda j: (0, j)), pl.BlockSpec((8, FT), lambda j: (0, NFT + j)),
        pl.BlockSpec((1, FT), lambda j: (0, j)), pl.BlockSpec((1, FT), lambda j: (0, NFT + j)),
    ]


def _conv_rows(win, w, b, n):
    win = win.astype(F32)
    u = win[LEAD:LEAD + n]
    u1 = pltpu.roll(win, 1, 0)[LEAD:LEAD + n]
    u2 = pltpu.roll(win, 2, 0)[LEAD:LEAD + n]
    return u2, u1, u, w[0:1, :] * u2 + w[1:2, :] * u1 + w[2:3, :] * u + b


def _ffn_act(up, fw8, fb, comm=()):
    def body(ug_ref, uv_ref, wg_ref, wv_ref, bg_ref, bv_ref, a_ref):
        wg, wv, bg, bv = wg_ref[...], wv_ref[...], bg_ref[...], bv_ref[...]

        def chunk(win_g, win_v):
            gp = _conv_rows(win_g, wg, bg, RC)[3]
            vp = _conv_rows(win_v, wv, bv, RC)[3]
            return (gp * jax.nn.sigmoid(gp) * vp).astype(BF)

        zero = jnp.zeros((LEAD, FT), BF)
        a_ref[0:RC, :] = chunk(jnp.concatenate([zero, ug_ref[0:RC, :]], axis=0),
                               jnp.concatenate([zero, uv_ref[0:RC, :]], axis=0))

        def step(i, carry):
            r0 = pl.multiple_of(i * RC, RC)
            win = pl.ds(r0 - LEAD, RC + LEAD)
            a_ref[pl.ds(r0, RC), :] = chunk(ug_ref[win, :], uv_ref[win, :])
            return carry

        lax.fori_loop(1, NCH, step, 0)

    return _call(
        body, (up, up, fw8, fw8, fb, fb), name="ffn_act", grid=(NFT,), in_specs=_ffn_act_specs(),
        out_specs=[pl.BlockSpec((T, FT), lambda j: (0, j))], out_shape=[SDS((T, DFF), BF)],
        sem=("parallel",), vmem_mib=40, comm=comm, free=(2, 3, 4, 5))


def _ffn_act_bwd(up, da, fw8, fb, comm=()):
    ext = RC + LEAD

    def body(ug_ref, uv_ref, wg_ref, wv_ref, bg_ref, bv_ref, da_ref,
             dug_ref, duv_ref, dwg_ref, dwv_ref, dbg_ref, dbv_ref):
        wg, wv, bg, bv = wg_ref[...], wv_ref[...], bg_ref[...], bv_ref[...]

        def chunk(win_g, win_v, da_e):
            g2, g1, g0, gp = _conv_rows(win_g, wg, bg, ext)
            v2, v1, v0, vp = _conv_rows(win_v, wv, bv, ext)
            da_e = da_e.astype(F32)
            sig = jax.nn.sigmoid(gp)
            dvp = da_e * (gp * sig)
            dgp = da_e * vp * (sig * (1.0 + gp * (1.0 - sig)))

            def back(dp, w):
                return (w[2:3, :] * dp[0:RC] + w[1:2, :] * pltpu.roll(dp, ext - 1, 0)[0:RC]
                        + w[0:1, :] * pltpu.roll(dp, ext - 2, 0)[0:RC]).astype(BF)

            def sums(dp, u2, u1, u0):
                d = dp[0:RC]
                return [_rows8(d), _rows8(d * u2[0:RC]), _rows8(d * u1[0:RC]), _rows8(d * u0[0:RC])]

            return back(dgp, wg), back(dvp, wv), sums(dgp, g2, g1, g0) + sums(dvp, v2, v1, v0)

        zero = jnp.zeros((LEAD, FT), BF)
        dug, duv, acc = chunk(jnp.concatenate([zero, ug_ref[0:ext, :]], axis=0),
                              jnp.concatenate([zero, uv_ref[0:ext, :]], axis=0), da_ref[0:ext, :])
        dug_ref[0:RC, :] = dug
        duv_ref[0:RC, :] = duv

        def step(i, acc):
            r0 = pl.multiple_of(i * RC, RC)
            win = pl.ds(r0 - LEAD, ext + LEAD)
            dug, duv, part = chunk(ug_ref[win, :], uv_ref[win, :], da_ref[pl.ds(r0, ext), :])
            dug_ref[pl.ds(r0, RC), :] = dug
            duv_ref[pl.ds(r0, RC), :] = duv
            return [a + p for a, p in zip(acc, part)]

        acc = lax.fori_loop(1, NCH - 1, step, acc)
        r0 = T - RC
        tail = lambda ref, lo: jnp.concatenate([ref[lo:T, :], zero], axis=0)
        dug, duv, part = chunk(tail(ug_ref, r0 - LEAD), tail(uv_ref, r0 - LEAD), tail(da_ref, r0))
        dug_ref[r0:T, :] = dug
        duv_ref[r0:T, :] = duv
        tot = [jnp.sum(a + p, axis=0, keepdims=True) for a, p in zip(acc, part)]
        for k, (dw_ref, db_ref) in enumerate(((dwg_ref, dbg_ref), (dwv_ref, dbv_ref))):
            db_ref[...] = tot[4 * k]
            dw_ref[...] = jnp.zeros_like(dw_ref)
            for r in range(3):
                dw_ref[r:r + 1, :] = tot[4 * k + 1 + r]

    col = lambda r: pl.BlockSpec((r, FT), lambda j: (0, j))
    return _call(
        body, (up, up, fw8, fw8, fb, fb, da), name="ffn_act_bwd", grid=(NFT,),
        in_specs=_ffn_act_specs() + [pl.BlockSpec((T, FT), lambda j: (0, j))],
        out_specs=[col(T), col(T), col(8), col(8), col(1), col(1)],
        out_shape=[SDS((T, DFF), BF), SDS((T, DFF), BF), SDS((8, DFF), F32), SDS((8, DFF), F32),
                   SDS((1, DFF), F32), SDS((1, DFF), F32)],
        sem=("parallel",), vmem_mib=40, comm=comm, free=(0, 1, 2, 3, 4, 5))


def _ffn_down_bwd(dh2b, w_down, comm=()):
    tm = TM

    def body(d_ref, w_ref, o_ref):
        o_ref[...] = _dot(d_ref[...], w_ref[...], 1, 1).astype(BF)

    return _call(
        body, (dh2b, w_down), name="ffn_down_bwd", grid=(T // tm,),
        in_specs=[pl.BlockSpec((tm, D), lambda i: (i, 0)), _resident((DFF, D))],
        out_specs=[pl.BlockSpec((tm, DFF), lambda i: (i, 0))], out_shape=[SDS((T, DFF), BF)],
        sem=("parallel",), vmem_mib=40, comm=comm, free=(0, 1))


def _norm_matmul_bwd(name, a_list, w_t, k_offsets, xin, g, dres, want_bf16, comm=(), slot=None):
    tm = TM
    ks = [a.shape[1] for a in a_list]
    n_a = len(a_list)
    n_pre = 0 if slot is None else 1

    def body(*refs):
        refs = refs[n_pre:]
        a_refs = refs[:n_a]
        w_ref, x_ref, g_ref, r_ref = refs[n_a:n_a + 4]
        outs = refs[n_a + 4:]
        dx_ref, dg_ref = outs[0], (outs[-1] if slot is None else outs[-1].at[0])

        @pl.when(pl.program_id(0) == 0)
        def _():
            dg_ref[...] = jnp.zeros_like(dg_ref)

        du = _dot(a_refs[0][...], w_ref[k_offsets[0]:k_offsets[0] + ks[0], :], 1, 0)
        for k in range(1, n_a):
            du = du + _dot(a_refs[k][...], w_ref[k_offsets[k]:k_offsets[k] + ks[k], :], 1, 0)
        x = x_ref[...]
        r = lax.rsqrt(jnp.mean(x * x, axis=-1, keepdims=True) + EPS)
        dx, dg = _rms_bwd(du, x, r, g_ref[...])
        dx = r_ref[...] + dx
        dx_ref[...] = dx
        if want_bf16:
            outs[1][...] = dx.astype(BF)
        dg_ref[...] += dg

    tile = lambda c: pl.BlockSpec((tm, c), lambda i, *_: (i, 0))
    if slot is None:
        dg_spec, dg_shape = pl.BlockSpec((1, D), lambda i: (0, 0)), SDS((1, D), F32)
    else:
        dg_spec, dg_shape = pl.BlockSpec((1, 1, D), lambda i, slot_ref: (slot_ref[0], 0, 0)), SDS((N_DEV, 1, D), F32)
    out_specs = [tile(D)] + ([tile(D)] if want_bf16 else []) + [dg_spec]
    out_shape = [SDS((T, D), F32)] + ([SDS((T, D), BF)] if want_bf16 else []) + [dg_shape]
    return _call(
        body, (*a_list, w_t, xin, g, dres), name=name, grid=(T // tm,), prefetch=() if slot is None else (slot,),
        in_specs=[tile(k) for k in ks] + [_resident(w_t.shape), tile(D),
                                           pl.BlockSpec((1, D), lambda i, *_: (0, 0)), tile(D)],
        out_specs=out_specs, out_shape=out_shape, sem=("arbitrary",), vmem_mib=56, comm=comm, free=tuple(range(n_a + 4)))


def _out_bwd(dh1b, w_out, comm=()):
    tm = TM

    def body(d_ref, w_ref, o_ref):
        o_ref[...] = _dot(d_ref[...], w_ref[...], 1, 1)

    return _call(
        body, (dh1b, w_out), name="out_bwd", grid=(T // tm,),
        in_specs=[pl.BlockSpec((tm, D), lambda i: (i, 0)), _resident((D, D))],
        out_specs=[pl.BlockSpec((tm, D), lambda i: (i, 0))], out_shape=[SDS((T, D), F32)],
        sem=("parallel",), vmem_mib=32, comm=comm, free=(0, 1))


def _wgrad(name, a_list, b, old_a, comm=()):
    m_k = a_list[0].shape[1]
    tm = max(t for t in range(128, m_k // 2 + 1, 128) if m_k % t == 0)
    steps = [a.shape[1] // tm for a in a_list]
    starts = [sum(steps[:k]) for k in range(len(a_list))]
    n_a = len(a_list)

    def body(*refs):
        a_refs, b_ref, o_ref = refs[:n_a], refs[n_a], refs[n_a + 1]
        i = pl.program_id(0)
        for k in range(n_a):
            @pl.when((i >= starts[k]) & (i < starts[k] + steps[k]))
            def _(k=k):
                o_ref[...] = _dot(a_refs[k][...], b_ref[...], 0, 0).astype(BF)

    def a_spec(k):
        return pl.BlockSpec((T, tm), lambda i: (0, jnp.clip(i - starts[k], 0, steps[k] - 1)))

    m_total = tm * sum(steps)
    return _call(
        body, (*a_list, b), name=name, grid=(sum(steps),),
        in_specs=[a_spec(k) for k in range(n_a)] + [_resident((T, D))],
        out_specs=[pl.BlockSpec((tm, D), lambda i: (i, 0))], out_shape=[SDS((m_total, D), BF)],
        sem=("parallel",), vmem_mib=40, comm=comm, free=() if old_a is None else tuple(range(n_a)) if old_a else (n_a,))


def _chip_sum(name, gbf, from_sib, core, chip):
    h = gbf.shape[1]
    th = h // 2

    def body(core_ref, chip_ref, g_ref, s_ref, pbf_ref, own_ref):
        p = g_ref[0].astype(F32) + s_ref[0].astype(F32)
        pbf_ref[0] = p.astype(BF)

        @pl.when(pl.program_id(1) == chip_ref[0])
        def _():
            own_ref[...] = p

    grid_spec = pltpu.PrefetchScalarGridSpec(
        num_scalar_prefetch=2, grid=(h // th, N_CHIPS),
        in_specs=[pl.BlockSpec((1, th, D), lambda t, jj, core_ref, chip_ref: (2 * jj + core_ref[0], t, 0)),
                  pl.BlockSpec((1, th, D), lambda t, jj, core_ref, chip_ref: (jj, t, 0))],
        out_specs=[pl.BlockSpec((1, th, D), lambda t, jj, core_ref, chip_ref: (jj, t, 0)),
                   pl.BlockSpec((th, D), lambda t, jj, core_ref, chip_ref: (t, 0))],
    )
    return _pcall(
        body, name=name, grid_spec=grid_spec, out_shape=_in_hbm([SDS((N_CHIPS, h, D), BF), SDS((h, D), F32)]),
        compiler_params=_params(("arbitrary", "arbitrary"), 32),
    )(core, chip, *_from_hbm(gbf, from_sib))


def _final_sum(name, own, from_chips, core, comm=()):
    h = own.shape[0]

    def body(core_ref, o_ref, r_ref, f_ref):
        f_ref[0] = ((o_ref[...] + r_ref[0].astype(F32)) + r_ref[1].astype(F32)) + r_ref[2].astype(F32)

    return _call(
        body, (own, from_chips), name=name, grid=(1,), prefetch=(core,),
        in_specs=[pl.BlockSpec((h, D), lambda i, core_ref: (0, 0)), pl.BlockSpec((3, h, D), lambda i, core_ref: (0, 0, 0))],
        out_specs=[pl.BlockSpec((1, h, D), lambda i, core_ref: (core_ref[0], 0, 0))], out_shape=[SDS((2, h, D), F32)],
        sem=("arbitrary",), vmem_mib=40, comm=comm)


def _adam_math(w, g, m, v):
    nm = ADAM_B1 * m + (1.0 - ADAM_B1) * g
    nv = ADAM_B2 * v + (1.0 - ADAM_B2) * (g * g)
    m_hat = nm / (1.0 - ADAM_B1 ** ADAM_STEP)
    v_hat = nv / (1.0 - ADAM_B2 ** ADAM_STEP)
    return -ADAM_LR * (m_hat / (jnp.sqrt(v_hat) + ADAM_EPS) + ADAM_WD * w), nm, nv


def _adamw(name, w, g, m, v, tr, copy_g=False, stage=True, g_transposed=False):
    rows, cols = w.shape

    def body(w_ref, g_ref, m_ref, v_ref, *outs):
        d_ref, nm_ref, nv_ref = outs[-3:]
        for c in [pl.ds(c0, 128) for c0 in range(0, cols, 128)] if g_transposed else [slice(None)]:
            g_val = g_ref[c, :].T if g_transposed else g_ref[...]
            if copy_g:
                outs[0][:, c] = g_val
            d_ref[:, c], nm_ref[:, c], nv_ref[:, c] = _adam_math(w_ref[:, c], g_val, m_ref[:, c], v_ref[:, c])

    spec = pl.BlockSpec((tr, cols), lambda i: (i, 0))
    n_out = 4 if copy_g else 3
    g_spec = pl.BlockSpec((cols, tr), lambda i: (0, i)) if g_transposed else spec
    return _call(body, (w, g, m, v), name=name, grid=(rows // tr,), in_specs=[spec, g_spec, spec, spec], out_specs=[spec] * n_out,
                 out_shape=[SDS((rows, cols), F32)] * n_out, sem=("parallel",), vmem_mib=32,
                 free=(0, 2, 3) if stage else ())


C_G1, C_G2, C_GCO, C_GAO, C_DCW, C_DQG, C_DKG, C_SINK, C_SQ = 0, 1024, 2048, 2560, 3072, 4608, 4736, 4864, 5632
P_W = C_SQ + 128


def _pack_small(me, dfwg, dfwv, dfbg, dfbv, dg2, dgco, dgao, dcw8, dqg, dkg, dsink, sq):
    def body(me_ref, dfwg_r, dfwv_r, dfbg_r, dfbv_r, dg2_r, dgco_r, dgao_r, dcw_r, dqg_r, dkg_r, dsink_r, sq_r, o):
        o[...] = jnp.zeros_like(o)
        o[0, :, 0:DFF] = dfwg_r[...]
        o[0, :, DFF:2 * DFF] = dfwv_r[...]
        o[0, 3:4, 0:DFF] = dfbg_r[...]
        o[0, 3:4, DFF:2 * DFF] = dfbv_r[...]
        o[0, 4:5, C_G2:C_G2 + D] = dg2_r[...]
        o[0, 4:5, C_GCO:C_GCO + CW] = dgco_r[...]
        o[0, 4:5, C_GAO:C_GAO + AW] = dgao_r[...]
        for r in range(3):
            o[0, 4:5, C_DCW + r * CW:C_DCW + (r + 1) * CW] = dcw_r[r:r + 1, :]
        o[0, 4:5, C_DQG:C_DQG + HD] = dqg_r[...]
        o[0, 4:5, C_DKG:C_DKG + HD] = dkg_r[...]
        o[0, 4:5, C_SINK:C_SINK + 128] = dsink_r[...]
        o[0, :, C_SQ:C_SQ + 128] = sq_r[...]

    ins = (dfwg, dfwv, dfbg, dfbv, dg2, dgco, dgao, dcw8, dqg, dkg, dsink, sq)
    return _call(body, ins, name="pack_small", grid=(1,), prefetch=(me,),
                 in_specs=[pl.BlockSpec(a.shape, lambda i, me_ref: (0, 0)) for a in ins],
                 out_specs=[pl.BlockSpec((1, 8, P_W), lambda i, me_ref: (me_ref[0], 0, 0))],
                 out_shape=[SDS((N_DEV, 8, P_W), F32)], sem=("arbitrary",))[0]


N_SMALL = 11


def _small_adam(chip, p_all, g1_all, tbl_all, ws, ms, vs):
    fw_cols = 2 * DFF // N_CHIPS
    cw_cols = CW // N_CHIPS

    def body(chip_ref, p_ref, fw_ref, cw0_ref, cw1_ref, cw2_ref, g1_ref, tbl_ref, *refs):
        w_r, m_r, v_r = refs[0:N_SMALL], refs[N_SMALL:2 * N_SMALL], refs[2 * N_SMALL:3 * N_SMALL]
        outs = refs[3 * N_SMALL:]
        g_o, d_o, nm_o, nv_o = (outs[k * N_SMALL:(k + 1) * N_SMALL] for k in range(4))
        loss_o = outs[4 * N_SMALL]

        def total(ref):
            s = ref[0]
            for k in range(1, N_DEV):
                s = s + ref[k]
            return s

        S = total(p_ref)
        fw = total(fw_ref)
        cws = [total(r) for r in (cw0_ref, cw1_ref, cw2_ref)]

        def step(i, g, at):
            d, nm, nv = _adam_math(w_r[i][at], g, m_r[i][at], v_r[i][at])
            g_o[i][at], d_o[i][at], nm_o[i][at], nv_o[i][at] = g, d, nm, nv

        everything = (slice(None), slice(None))
        step(0, total(g1_ref), everything)
        for r in range(3):
            step(1, cws[r][4:5, :], (r, slice(None), slice(None)))
        step(2, S[4:5, C_DQG:C_DQG + HD], everything)
        step(3, S[4:5, C_DKG:C_DKG + HD], everything)
        step(4, total(tbl_ref), everything)
        step(5, S[4:5, C_SINK:C_SINK + NH], everything)
        step(6, S[4:5, C_GCO:C_GCO + CW], everything)
        step(7, S[4:5, C_GAO:C_GAO + AW], everything)
        step(8, S[4:5, C_G2:C_G2 + D], everything)
        for r in range(3):
            step(9, fw[r:r + 1, :], (r, slice(None), slice(None)))
        step(10, S[3:4, 0:2 * DFF], everything)
        sq = S[:, C_SQ:C_SQ + 128]
        loss_o[...] = jnp.sum(jnp.sum(sq, axis=1, keepdims=True), axis=0, keepdims=True) * (0.5 / D)

    def full(a):
        n = len(a.shape)
        return pl.BlockSpec(a.shape, lambda i, chip_ref: (0,) * n)

    params = [*ws, *ms, *vs]
    out = _call(
        body, (p_all, p_all, p_all, p_all, p_all, g1_all, tbl_all, *params), name="small_adam", grid=(1,), prefetch=(chip,),
        in_specs=[full(p_all),
                  pl.BlockSpec((N_DEV, 8, fw_cols), lambda i, chip_ref: (0, 0, chip_ref[0])),
                  *[pl.BlockSpec((N_DEV, 8, cw_cols), lambda i, chip_ref, r=r: (0, 0, (C_DCW + r * CW) // cw_cols + chip_ref[0]))
                    for r in range(3)],
                  full(g1_all), full(tbl_all), *[full(a) for a in params]],
        out_specs=[full(a) for a in ws] * 4 + [pl.BlockSpec((1, 1), lambda i, chip_ref: (0, 0))],
        out_shape=[SDS(a.shape, F32) for a in ws] * 4 + [SDS((1, 1), F32)], sem=("arbitrary",), vmem_mib=32)
    return out[0:N_SMALL], out[N_SMALL:2 * N_SMALL], out[2 * N_SMALL:3 * N_SMALL], out[3 * N_SMALL:4 * N_SMALL], out[4 * N_SMALL]


PLACE_STEPS = 4


def _place_specs(shards):
    rows = [s.shape[0] // PLACE_STEPS for s in shards]
    return ([pl.BlockSpec((r, D), lambda i, chip_ref: (i, 0)) for r in rows],
            [pl.BlockSpec((r, D), lambda i, chip_ref: (chip_ref[0] * PLACE_STEPS + i, 0)) for r in rows],
            [SDS((N_CHIPS * s.shape[0], D), BF) for s in shards])


def _place_first(chip, shard, conv_w, ffn_conv_w):
    def body(chip_ref, a, s0, s1, o, t0, t1):
        o[...] = a[...].astype(BF)

        @pl.when(pl.program_id(0) == 0)
        def _():
            for s, t in ((s0, t0), (s1, t1)):
                t[...] = jnp.zeros_like(t)
                t[0, 0:3, :] = s[...]

    ins, outs, shapes = _place_specs([shard])
    taps = (conv_w, ffn_conv_w)
    return _call(
        body, (shard, conv_w, ffn_conv_w), name="place_first", grid=(PLACE_STEPS,), prefetch=(chip,),
        in_specs=ins + [pl.BlockSpec(s.shape, lambda i, chip_ref: (0, 0)) for s in taps],
        out_specs=outs + [pl.BlockSpec((1, 8, s.shape[1]), lambda i, chip_ref: (chip_ref[0], 0, 0)) for s in taps],
        out_shape=shapes + [SDS((N_CHIPS, 8, s.shape[1]), F32) for s in taps],
        sem=("arbitrary",), vmem_mib=32, free=(1, 2))


def _place_rest(chip, shards, w_up, table, bucket, comm):
    n = len(shards)
    c_up = w_up.shape[1]
    edges = [round(k * (c_up // 128) / PLACE_STEPS) * 128 for k in range(PLACE_STEPS + 1)]

    def body(chip_ref, *refs):
        a, (up_ref, tab_ref, bk_ref), o = refs[:n], refs[n:n + 3], refs[n + 3:2 * n + 3]
        up_o, bias_ref = refs[2 * n + 3:]
        for src, dst in zip(a, o):
            dst[...] = src[...].astype(BF)
        for k in range(PLACE_STEPS):
            @pl.when(pl.program_id(0) == k)
            def _(k=k):
                up_o[edges[k]:edges[k + 1], :] = up_ref[:, edges[k]:edges[k + 1]].T.astype(BF)

        @pl.when(pl.program_id(0) == 0)
        def _():
            bk = bk_ref[...]
            eq = [bk == b for b in range(NBUCKET)]
            for h in range(NH):
                acc = jnp.zeros((BLK, 2 * BLK), F32)
                for b in range(NBUCKET):
                    acc = jnp.where(eq[b], tab_ref[h, b], acc)
                bias_ref[h * BLK:(h + 1) * BLK, :] = acc

    ins, outs, shapes = _place_specs(shards)
    return _call(
        body, (*shards, w_up, table, bucket), name="place_rest", grid=(PLACE_STEPS,), prefetch=(chip,),
        in_specs=ins + [_resident(w_up.shape), pl.BlockSpec(memory_space=pltpu.SMEM),
                        pl.BlockSpec(bucket.shape, lambda i, chip_ref: (0, 0))],
        out_specs=outs + [pl.BlockSpec((c_up, D), lambda i, chip_ref: (chip_ref[0], 0)),
                          pl.BlockSpec((NH * BLK, 2 * BLK), lambda i, chip_ref: (0, 0))],
        out_shape=shapes + [SDS((N_CHIPS * c_up, D), BF), SDS((NH * BLK, 2 * BLK), F32)],
        sem=("arbitrary",), vmem_mib=32, comm=comm, free=(n + 1, n + 2))


def kernel(x, norm_mix_g, w_in, conv_w, q_norm_g, k_norm_g, rel_bias_table, sinks, out_norm_conv_g, out_norm_attn_g, w_out, norm_ffn_g, w_up, ffn_conv_w, ffn_conv_b, w_down, loss_target, m_norm_mix_g, m_w_in, m_conv_w, m_q_norm_g, m_k_norm_g, m_rel_bias_table, m_sinks, m_out_norm_conv_g, m_out_norm_attn_g, m_w_out, m_norm_ffn_g, m_w_up, m_ffn_conv_w, m_ffn_conv_b, m_w_down, v_norm_mix_g, v_w_in, v_conv_w, v_q_norm_g, v_k_norm_g, v_rel_bias_table, v_sinks, v_out_norm_conv_g, v_out_norm_attn_g, v_w_out, v_norm_ffn_g, v_w_up, v_ffn_conv_w, v_ffn_conv_b, v_w_down):
    as_arg = lambda i: jnp.reshape(i, (1,)).astype(jnp.int32)
    chip = as_arg(2 * lax.axis_index("x") + lax.axis_index("y"))
    core = as_arg(lax.axis_index("c"))
    me = 2 * chip + core
    xs, tgt = x[0], loss_target[0]
    qg, kg, gco, gao, g1, g2, fb = q_norm_g, k_norm_g, out_norm_conv_g, out_norm_attn_g, norm_mix_g, norm_ffn_g, ffn_conv_b
    pieces = lambda g: g.reshape(N_DEV, g.shape[0] // N_DEV, D)
    whole = lambda f: f.reshape(2 * f.shape[1], D)

    bucket = jnp.asarray(_bucket_table())
    p_in, p_cw, p_fw = _place_first(chip, w_in[0].T, conv_w[0], ffn_conv_w[0])
    p_out, p_down, p_up, bias, w_int, cw_all, fw_all = _place_rest(
        chip, [w_out[0], w_down[0]], w_up[0], rel_bias_table.T, bucket,
        comm=[_t_gather(p_in), _t_small_weights(p_cw), _t_small_weights(p_fw)])
    cw8 = jnp.transpose(cw_all, (1, 0, 2)).reshape(8, CW)
    fw8 = jnp.transpose(fw_all, (1, 0, 2)).reshape(8, 2 * DFF)

    early = 3 / 11
    proj, u1, w_out_f, p_up = _inproj(xs, g1, w_int, comm=[_t_gather(p_out), _t_gather(p_up, (0, early))])
    y, w_upt = _mix_fwd(proj, sinks, cw8, qg, kg, gco, gao, bias, comm=[_t_gather(p_up, (early, 1))])
    h1, u2 = _outproj(y, w_out_f, xs, g2)
    up, = _ffn_up(u2, w_upt)
    a, w_down_f = _ffn_act(up, fw8, fb, comm=[_t_gather(p_down)])
    dh2, dh2b, sq = _ffn_down(a, w_down_f, h1, tgt)

    gdbf, = _wgrad("wgrad_down", [a], dh2b, None)
    da, sib_down = _ffn_down_bwd(dh2b, w_down_f, comm=[_t_sibling(pieces(gdbf))])
    pbf_down, own_down = _chip_sum("chip_sum_w_down", pieces(gdbf), sib_down, core, chip)
    dug, duv, dfwg, dfwv, dfbg, dfbv, chips_down = _ffn_act_bwd(up, da, fw8, fb, comm=[_t_chips(pbf_down)])
    fin_down, = _final_sum("final_sum_w_down", own_down, chips_down, core)
    gubf, = _wgrad("wgrad_up", [dug, duv], u2, False)
    dh1, dh1b, dg2, sib_up, fin_down = _norm_matmul_bwd(
        "ffn_up_bwd", [dug, duv], w_upt, [0, DFF], h1, g2, dh2, True, comm=[_t_sibling(pieces(gubf)), _t_swap(fin_down)])
    pbf_up, own_up = _chip_sum("chip_sum_w_up", pieces(gubf), sib_up, core, chip)
    gobf, = _wgrad("wgrad_out", [y], dh1b, True)
    dy, sib_out = _out_bwd(dh1b, w_out_f, comm=[_t_sibling(pieces(gobf))])
    pbf_out, own_out = _chip_sum("chip_sum_w_out", pieces(gobf), sib_out, core, chip)
    dproj, dcw8, dqg, dkg, dgco, dgao, dsink, dbias, chips_up, chips_out = _mix_bwd(
        proj, dy, sinks, cw8, qg, kg, gco, gao, bias, comm=[_t_chips(pbf_up), _t_chips(pbf_out)])
    fin_up, = _final_sum("final_sum_w_up", own_up, chips_up, core)
    tbl_all = _band_bias_bwd(dbias, bucket, me)
    p_all = _pack_small(me, dfwg, dfwv, dfbg, dfbv, dg2, dgco, dgao, dcw8, dqg, dkg, dsink, sq)
    gibf, fin_up, p_all, tbl_all = _wgrad(
        "wgrad_in", [dproj], u1, False, comm=[_t_swap(fin_up), _t_allgather(p_all), _t_allgather(tbl_all)])
    fin_out, sib_in = _final_sum("final_sum_w_out", own_out, chips_out, core, comm=[_t_sibling(pieces(gibf))])
    pbf_in, own_in = _chip_sum("chip_sum_w_in", pieces(gibf), sib_in, core, chip)
    dx, g1_all, chips_in, fin_out = _norm_matmul_bwd(
        "in_bwd", [dproj], w_int, [0], xs, g1, dh1, False, comm=[_t_chips(pbf_in), _t_swap(fin_out)], slot=me)
    fin_in, = _final_sum("final_sum_w_in", own_in, chips_in, core)
    g1_all, fin_in = _comm_call("gather_last", [_t_allgather(g1_all), _t_swap(fin_in)])

    g_w_out, g_w_down = whole(fin_out), whole(fin_down)
    g_w_down, d_down, nm_down, nv_down = _adamw("adamw_w_down", w_down[0], g_w_down, m_w_down[0], v_w_down[0], 352, True)
    g_w_up, d_up, nm_up, nv_up = _adamw(
        "adamw_w_up", w_up[0], whole(fin_up), m_w_up[0], v_w_up[0], 256, True, stage=False, g_transposed=True)
    g_w_out, d_out, nm_out, nv_out = _adamw("adamw_w_out", w_out[0], g_w_out, m_w_out[0], v_w_out[0], 256, True, stage=False)
    g_w_in, d_in, nm_in, nv_in = [a.T for a in _adamw(
        "adamw_w_in", w_in[0].T, whole(fin_in), m_w_in[0].T, v_w_in[0].T, INW // N_CHIPS // 3, True)]
    taps = lambda a: jnp.transpose(a, (1, 0, 2))
    sw = [norm_mix_g, taps(conv_w), q_norm_g, k_norm_g, rel_bias_table.T, sinks, out_norm_conv_g, out_norm_attn_g,
          norm_ffn_g, taps(ffn_conv_w), ffn_conv_b]
    smm = [m_norm_mix_g, taps(m_conv_w), m_q_norm_g, m_k_norm_g, m_rel_bias_table.T, m_sinks, m_out_norm_conv_g,
           m_out_norm_attn_g, m_norm_ffn_g, taps(m_ffn_conv_w), m_ffn_conv_b]
    smv = [v_norm_mix_g, taps(v_conv_w), v_q_norm_g, v_k_norm_g, v_rel_bias_table.T, v_sinks, v_out_norm_conv_g,
           v_out_norm_attn_g, v_norm_ffn_g, taps(v_ffn_conv_w), v_ffn_conv_b]
    *small_out, loss = _small_adam(chip, p_all, g1_all, tbl_all, sw, smm, smv)
    sg, sd, snm, snv = [list(r) for r in small_out]
    for r in (sg, sd, snm, snv):
        r[1], r[4], r[9] = taps(r[1]), r[4].T, taps(r[9])

    def order(s, b_in, b_out, b_up, b_down):
        return (s[0], b_in[None], s[1], s[2], s[3], s[4], s[5], s[6], s[7], b_out[None], s[8], b_up[None],
                s[9], s[10], b_down[None])

    return (loss.reshape(()), dx[None],
            *order(sg, g_w_in, g_w_out, g_w_up, g_w_down),
            *order(sd, d_in, d_out, d_up, d_down),
            *order(snm, nm_in, nm_out, nm_up, nm_down),
            *order(snv, nv_in, nv_out, nv_up, nv_down))
```

```python
import functools
import math

import numpy as np

import jax
import jax.numpy as jnp
from jax import lax
from jax.experimental import pallas as pl
from jax.experimental.pallas import tpu as pltpu

F32 = jnp.float32
BF = jnp.bfloat16
SDS = jax.ShapeDtypeStruct

T = 2048
D = 1024
CW = 512
AW = 512
HD = 64
NH = 8
NKV = 2
GQ = 4
INW = 2304
DFF = 2816
BLK = 128
NB = T // BLK
NBUCKET = 32
EPS = 1e-6
NEG_INF = -1e30
N_CHIPS = 4
N_DEV = 8

ADAM_LR = 0.001
ADAM_B1 = 0.9
ADAM_B2 = 0.999
ADAM_EPS = 1e-08
ADAM_WD = 0.01
ADAM_STEP = 10

TM = 512
MIB = 1024 * 1024
MESH = pl.DeviceIdType.MESH
ANY = pl.BlockSpec(memory_space=pl.ANY)

_pcall = pl.pallas_call


def _params(sem=None, vmem_mib=None, collective_id=None):
    kw = {} if collective_id is None else {"collective_id": collective_id}
    if sem is not None:
        kw["dimension_semantics"] = sem
    if vmem_mib is not None:
        kw["vmem_limit_bytes"] = vmem_mib * MIB
    return pltpu.CompilerParams(**kw)


def _resident(shape):
    return pl.BlockSpec(shape, lambda *_: (0,) * len(shape), pipeline_mode=pl.Buffered(1))


def _dot(a, b, ca, cb):
    return lax.dot_general(a, b, (((ca,), (cb,)), ((), ())), preferred_element_type=F32)


def _rms_bwd(dy, x, r, g):
    dg = jnp.sum(dy * (x * r), axis=0, keepdims=True)
    dgx = dy * g
    dx = r * dgx - x * (r * r * r) * jnp.mean(x * dgx, axis=-1, keepdims=True)
    return dx, dg


def _where():
    x, y, c = lax.axis_index("x"), lax.axis_index("y"), lax.axis_index("c")
    return x, y, c, [(1 - x, y), (x, 1 - y), (1 - x, 1 - y)]


def _rcopy(src, dst, ssem, rsem, dev):
    return pltpu.make_async_remote_copy(src_ref=src, dst_ref=dst, send_sem=ssem, recv_sem=rsem, device_id=dev,
                                        device_id_type=MESH)


SIBLING, Y_CHIP, X_CHIP, DIAGONAL_CHIP = 1, 2, 4, 6
OTHER_CHIPS = (Y_CHIP, X_CHIP, DIAGONAL_CHIP)
EVERYONE = tuple(range(1, N_DEV))
BARRIER_OF = {(SIBLING,): 0, (SIBLING, Y_CHIP, X_CHIP): 1, OTHER_CHIPS: 2, (SIBLING,) + OTHER_CHIPS: 3, EVERYONE: 4}


def _peer(rel):
    x, y, c, _ = _where()
    return x ^ ((rel >> 2) & 1), y ^ ((rel >> 1) & 1), c ^ (rel & 1)


class _Task:
    def __init__(self, ins, outs, alias, n_sem, start, finish, middle=None, peers=()):
        self.ins, self.outs, self.alias, self.n_sem, self.start, self.finish = ins, outs, alias, n_sem, start, finish
        self.middle = middle if middle is not None else (lambda *args: None)
        self.peers = peers


def _peers_of(comm):
    return tuple(sorted({p for t in comm for p in t.peers}))


def _enter(comm):
    peers = _peers_of(comm)
    barrier = pltpu.get_barrier_semaphore()
    for rel in peers:
        pl.semaphore_signal(barrier, inc=1, device_id=_peer(rel), device_id_type=MESH)
    pl.semaphore_wait(barrier, len(peers))


ROWS16 = 16


def _t_gather(placed, part=(0, 1)):
    R = placed.shape[0] // N_CHIPS
    q = R // 4
    lo, hi = (round(f * (q // ROWS16)) * ROWS16 for f in part)

    def quarter(chip_index, core, k):
        return pl.ds(pl.multiple_of(chip_index * R + core * 2 * q + k * q + lo, ROWS16), hi - lo)

    def places():
        x, y, c, _ = _where()
        return c, 2 * x + y, 2 * (1 - x) + y, 2 * x + (1 - y), 2 * (1 - x) + (1 - y), (1 - x, y, c), (x, 1 - y, c), (x, y, 1 - c)

    def copy(buf, k, chip_index, core, quart, ss, rs, b, dev):
        window = buf.at[quarter(chip_index, core, quart)]
        return _rcopy(window, window, ss.at[b + k], rs.at[b + k], dev)

    def start(cin, cout, ss, rs, b):
        c, me, _, _, _, x_nbr, y_nbr, _ = places()
        for k, (quart, dev) in enumerate(((0, x_nbr), (1, y_nbr), (1, x_nbr), (0, y_nbr))):
            copy(cout[0], k, me, c, quart, ss, rs, b, dev).start()

    def middle(cin, cout, ss, rs, b):
        c, _, xc, yc, _, x_nbr, y_nbr, sib = places()
        for k, chip_index, quart, dev in ((0, xc, 0, y_nbr), (1, yc, 1, x_nbr)):
            copy(cout[0], k, chip_index, c, quart, ss, rs, b, dev).wait_recv()
            copy(cout[0], 4 + k, chip_index, c, quart, ss, rs, b, dev).start()
            copy(cout[0], 6 + k, chip_index, c, quart, ss, rs, b, sib).start()

    later = ((2, 1, 1), (3, 2, 0), (4, 3, 0), (5, 3, 1))

    def finish(cin, cout, ss, rs, b):
        c, me, xc, yc, dc, _, _, sib = places()
        chip_of = {1: xc, 2: yc, 3: dc}
        for k, whose, quart in later:
            copy(cout[0], k, chip_of[whose], c, quart, ss, rs, b, sib).wait_recv()
            copy(cout[0], 6 + k, chip_of[whose], c, quart, ss, rs, b, sib).start()
        for k, whose, quart in ((0, 1, 0), (1, 2, 1)) + later:
            copy(cout[0], 6 + k, chip_of[whose], 1 - c, quart, ss, rs, b, sib).wait_recv()
        for k in range(12):
            copy(cout[0], k, me, c, 0, ss, rs, b, sib).wait_send()

    return _Task([placed], [SDS(placed.shape, placed.dtype)], [(0, 0)], 12, start, finish, middle, peers=(SIBLING, Y_CHIP, X_CHIP))


def _t_small_weights(buf):
    def start(cin, cout, ss, rs, b):
        x, y, c, chips = _where()
        mine = cout[0].at[2 * x + y]
        for r, (px, py) in enumerate(chips):
            _rcopy(mine, mine, ss.at[b + r], rs.at[b + r], (px, py, c)).start()

    def finish(cin, cout, ss, rs, b):
        x, y, c, chips = _where()
        for r, (px, py) in enumerate(chips):
            got = cout[0].at[2 * px + py]
            _rcopy(got, got, ss.at[b + r], rs.at[b + r], (px, py, c)).wait_recv()
        for r, (px, py) in enumerate(chips):
            mine = cout[0].at[2 * x + y]
            _rcopy(mine, mine, ss.at[b + r], rs.at[b + r], (px, py, c)).wait_send()

    return _Task([buf], [SDS(buf.shape, buf.dtype)], [(0, 0)], 3, start, finish, peers=OTHER_CHIPS)


def _t_sibling(gbf):
    def start(cin, cout, ss, rs, b):
        x, y, c, _ = _where()
        for jj in range(N_CHIPS):
            _rcopy(cin[0].at[2 * jj + (1 - c)], cout[0].at[jj], ss.at[b + jj], rs.at[b + jj], (x, y, 1 - c)).start()

    def finish(cin, cout, ss, rs, b):
        x, y, c, _ = _where()
        for jj in range(N_CHIPS):
            got = cout[0].at[jj]
            _rcopy(got, got, ss.at[b + jj], rs.at[b + jj], (x, y, 1 - c)).wait_recv()
        for jj in range(N_CHIPS):
            got = cout[0].at[jj]
            _rcopy(got, got, ss.at[b + jj], rs.at[b + jj], (x, y, 1 - c)).wait_send()

    return _Task([gbf], [SDS((N_CHIPS,) + gbf.shape[1:], BF)], [], N_CHIPS, start, finish, peers=(SIBLING,))


def _t_chips(pbf):
    def start(cin, cout, ss, rs, b):
        x, y, c, chips = _where()
        for r, (px, py) in enumerate(chips):
            _rcopy(cin[0].at[2 * px + py], cout[0].at[r], ss.at[b + r], rs.at[b + r], (px, py, c)).start()

    def finish(cin, cout, ss, rs, b):
        x, y, c, chips = _where()
        for r, (px, py) in enumerate(chips):
            got = cout[0].at[r]
            _rcopy(got, got, ss.at[b + r], rs.at[b + r], (px, py, c)).wait_recv()
        for r, (px, py) in enumerate(chips):
            got = cout[0].at[r]
            _rcopy(got, got, ss.at[b + r], rs.at[b + r], (px, py, c)).wait_send()

    return _Task([pbf], [SDS((3,) + pbf.shape[1:], BF)], [], 3, start, finish, peers=OTHER_CHIPS)


def _t_swap(fin):
    def start(cin, cout, ss, rs, b):
        x, y, c, _ = _where()
        mine = cout[0].at[c]
        _rcopy(mine, mine, ss.at[b], rs.at[b], (x, y, 1 - c)).start()

    def finish(cin, cout, ss, rs, b):
        x, y, c, _ = _where()
        got = cout[0].at[1 - c]
        _rcopy(got, got, ss.at[b], rs.at[b], (x, y, 1 - c)).wait_recv()
        _rcopy(got, got, ss.at[b], rs.at[b], (x, y, 1 - c)).wait_send()

    return _Task([fin], [SDS(fin.shape, fin.dtype)], [(0, 0)], 1, start, finish, peers=(SIBLING,))


def _t_allgather(buf):
    def peers():
        x, y, c, _ = _where()
        out = []
        for rel in range(1, N_DEV):
            px, py, pc = x ^ ((rel >> 2) & 1), y ^ ((rel >> 1) & 1), c ^ (rel & 1)
            out.append((rel - 1, 4 * px + 2 * py + pc, (px, py, pc)))
        return 4 * x + 2 * y + c, out

    def start(cin, cout, ss, rs, b):
        me, ps = peers()
        mine = cout[0].at[me]
        for k, _, dev in ps:
            _rcopy(mine, mine, ss.at[b + k], rs.at[b + k], dev).start()

    def finish(cin, cout, ss, rs, b):
        me, ps = peers()
        for k, pidx, dev in ps:
            got = cout[0].at[pidx]
            _rcopy(got, got, ss.at[b + k], rs.at[b + k], dev).wait_recv()
        for k, _, dev in ps:
            mine = cout[0].at[me]
            _rcopy(mine, mine, ss.at[b + k], rs.at[b + k], dev).wait_send()

    return _Task([buf], [SDS(buf.shape, buf.dtype)], [(0, 0)], N_DEV - 1, start, finish, peers=EVERYONE)


def _run_tasks(comm, which, cin, cout, ss, rs):
    i0 = o0 = s0 = 0
    for t in comm:
        getattr(t, which)(cin[i0:i0 + len(t.ins)], cout[o0:o0 + len(t.outs)], ss, rs, s0)
        i0, o0, s0 = i0 + len(t.ins), o0 + len(t.outs), s0 + t.n_sem


def _from_hbm(*arrays):
    return [pltpu.with_memory_space_constraint(a, pltpu.HBM) for a in arrays]


def _in_hbm(shapes):
    return [pltpu.HBM(s.shape, s.dtype) for s in shapes]


def _comm_layout(comm, n_in, n_out):
    c_in = [a for t in comm for a in t.ins]
    c_out = [s for t in comm for s in t.outs]
    aliases, i0, o0 = {}, 0, 0
    for t in comm:
        for i, o in t.alias:
            aliases[n_in + i0 + i] = n_out + o0 + o
        i0, o0 = i0 + len(t.ins), o0 + len(t.outs)
    return c_in, c_out, aliases, sum(t.n_sem for t in comm)


def _call(body, operands, *, name, grid, in_specs, out_specs, out_shape, scratch_shapes=(), sem=None, vmem_mib=None, comm=(),
          free=(), prefetch=()):
    operands = [o if s.memory_space == pltpu.SMEM or k in free else pltpu.with_memory_space_constraint(o, pltpu.HBM)
                for k, (o, s) in enumerate(zip(operands, in_specs))]
    n_pre, n_in, n_out, n_scr = len(prefetch), len(in_specs), len(out_specs), len(scratch_shapes)
    c_in, c_out, aliases, n_sem = _comm_layout(comm, n_pre + n_in, n_out)
    sems = [pltpu.SemaphoreType.DMA((n_sem,)), pltpu.SemaphoreType.DMA((n_sem,))] if comm else []

    def wrapped(*refs):
        pre, refs = refs[:n_pre], refs[n_pre:]
        ins, cin = refs[:n_in], refs[n_in:n_in + len(c_in)]
        rest = refs[n_in + len(c_in):]
        outs, cout = rest[:n_out], rest[n_out:n_out + len(c_out)]
        rest = rest[n_out + len(c_out):]
        scr, csem = rest[:n_scr], rest[n_scr:]
        if not comm:
            return body(*pre, *ins, *outs, *scr)
        step = functools.reduce(lambda acc, k: acc * grid[k] + pl.program_id(k), range(len(grid)), 0)
        n_steps = math.prod(grid)

        @pl.when(step == 0)
        def _():
            _enter(comm)
            _run_tasks(comm, "start", cin, cout, *csem)

        pl.when(step == n_steps // 2)(lambda: _run_tasks(comm, "middle", cin, cout, *csem))
        body(*pre, *ins, *outs, *scr)
        pl.when(step == n_steps - 1)(lambda: _run_tasks(comm, "finish", cin, cout, *csem))

    grid_spec = pltpu.PrefetchScalarGridSpec(
        num_scalar_prefetch=n_pre, grid=grid, in_specs=list(in_specs) + [ANY] * len(c_in),
        out_specs=list(out_specs) + [ANY] * len(c_out), scratch_shapes=list(scratch_shapes) + sems)
    return _pcall(
        wrapped, name=name, grid_spec=grid_spec, out_shape=_in_hbm(list(out_shape) + c_out), input_output_aliases=aliases,
        compiler_params=_params(("arbitrary",) * len(grid) if comm else sem, vmem_mib,
                                BARRIER_OF[_peers_of(comm)] if comm else None),
    )(*prefetch, *operands, *_from_hbm(*c_in))


def _comm_call(name, comm):
    c_in, c_out, aliases, n_sem = _comm_layout(comm, 0, 0)

    def body(*refs):
        cin, cout, (ss, rs) = refs[:len(c_in)], refs[len(c_in):len(c_in) + len(c_out)], refs[len(c_in) + len(c_out):]
        _enter(comm)
        for phase in ("start", "middle", "finish"):
            _run_tasks(comm, phase, cin, cout, ss, rs)

    return _pcall(
        body, name=name, in_specs=[ANY] * len(c_in), out_specs=[ANY] * len(c_out), out_shape=_in_hbm(c_out),
        scratch_shapes=[pltpu.SemaphoreType.DMA((n_sem,)), pltpu.SemaphoreType.DMA((n_sem,))],
        input_output_aliases=aliases, compiler_params=_params(collective_id=BARRIER_OF[_peers_of(comm)]),
    )(*_from_hbm(*c_in))


def _inproj(x, g1, w_int, comm=()):
    tm = TM

    def body(x_ref, g_ref, w_ref, proj_ref, u_ref):
        xf = x_ref[...]
        r = lax.rsqrt(jnp.mean(xf * xf, axis=-1, keepdims=True) + EPS)
        u = (xf * r * g_ref[...]).astype(BF)
        u_ref[...] = u
        proj_ref[...] = _dot(u, w_ref[...], 1, 1)

    return _call(
        body, (x, g1, w_int), name="inproj", grid=(T // tm,),
        in_specs=[pl.BlockSpec((tm, D), lambda i: (i, 0)), pl.BlockSpec((1, D), lambda i: (0, 0)),
                  _resident((INW, D))],
        out_specs=[pl.BlockSpec((tm, INW), lambda i: (i, 0)), pl.BlockSpec((tm, D), lambda i: (i, 0))],
        out_shape=[SDS((T, INW), F32), SDS((T, D), BF)], sem=("parallel",), vmem_mib=40, comm=comm, free=(0, 1))


def _outproj(y, w_out, x, g2):
    tm = TM

    def body(y_ref, w_ref, x_ref, g_ref, h1_ref, u2_ref):
        h1 = x_ref[...] + _dot(y_ref[...], w_ref[...], 1, 0)
        h1_ref[...] = h1
        r = lax.rsqrt(jnp.mean(h1 * h1, axis=-1, keepdims=True) + EPS)
        u2_ref[...] = (h1 * r * g_ref[...]).astype(BF)

    return _call(
        body, (y, w_out, x, g2), name="outproj", grid=(T // tm,),
        in_specs=[pl.BlockSpec((tm, D), lambda i: (i, 0)), _resident((D, D)),
                  pl.BlockSpec((tm, D), lambda i: (i, 0)), pl.BlockSpec((1, D), lambda i: (0, 0))],
        out_specs=[pl.BlockSpec((tm, D), lambda i: (i, 0)), pl.BlockSpec((tm, D), lambda i: (i, 0))],
        out_shape=[SDS((T, D), F32), SDS((T, D), BF)], sem=("parallel",), vmem_mib=32, free=(2, 3))


def _ffn_up(u2, w_upt, comm=()):
    tm, tn = 1024, 512

    def body(u_ref, w_ref, o_ref):
        o_ref[...] = _dot(u_ref[...], w_ref[...], 1, 1).astype(BF)

    return _call(
        body, (u2, w_upt), name="ffn_up", grid=(T // tm, 2 * DFF // tn),
        in_specs=[pl.BlockSpec((tm, D), lambda i, j: (i, 0)), pl.BlockSpec((tn, D), lambda i, j: (j, 0))],
        out_specs=[pl.BlockSpec((tm, tn), lambda i, j: (i, j))], out_shape=[SDS((T, 2 * DFF), BF)],
        sem=("parallel", "parallel"), vmem_mib=32, comm=comm, free=(1,))


def _ffn_down(a, w_down, h1, tgt):
    tm = TM

    def body(a_ref, w_ref, h1_ref, t_ref, dh_ref, dhb_ref, l_ref):
        @pl.when(pl.program_id(0) == 0)
        def _():
            l_ref[...] = jnp.zeros_like(l_ref)

        h2 = h1_ref[...] + _dot(a_ref[...], w_ref[...], 1, 0)
        e = h2 - t_ref[...]
        dh = e * (1.0 / D)
        dh_ref[...] = dh
        dhb_ref[...] = dh.astype(BF)
        e2 = jnp.sum((e * e).reshape(tm // 8, 8, D), axis=0)
        acc = e2[:, 0:128]
        for k in range(1, D // 128):
            acc = acc + e2[:, k * 128:(k + 1) * 128]
        l_ref[...] += acc

    return _call(
        body, (a, w_down, h1, tgt), name="ffn_down", grid=(T // tm,),
        in_specs=[pl.BlockSpec((tm, DFF), lambda i: (i, 0)), _resident((DFF, D)),
                  pl.BlockSpec((tm, D), lambda i: (i, 0)), pl.BlockSpec((tm, D), lambda i: (i, 0))],
        out_specs=[pl.BlockSpec((tm, D), lambda i: (i, 0)), pl.BlockSpec((tm, D), lambda i: (i, 0)),
                   pl.BlockSpec((8, 128), lambda i: (0, 0))],
        out_shape=[SDS((T, D), F32), SDS((T, D), BF), SDS((8, 128), F32)], sem=("arbitrary",), vmem_mib=40, free=(2, 3))


def _bucket_table():
    q = np.arange(BLK, dtype=np.int32)[:, None]
    j = np.arange(2 * BLK, dtype=np.int32)[None, :]
    n = np.maximum(q + BLK - j, 0)
    nf = np.maximum(n, 1).astype(np.float32)
    max_exact = NBUCKET // 2
    large = max_exact + (np.log(nf / np.float32(max_exact)) / np.float32(math.log(BLK / max_exact))
                         * np.float32(NBUCKET - max_exact)).astype(np.int32)
    large = np.minimum(large, NBUCKET - 1)
    return np.where(n < max_exact, n, large).astype(np.int32)


def _band_bias_bwd(dbias, bucket, me):
    def body(me_ref, db_ref, bk_ref, o_ref):
        bk = bk_ref[...]
        for b in range(NBUCKET):
            m = bk == b
            for h in range(NH):
                v = jnp.where(m, db_ref[h * BLK:(h + 1) * BLK, :], 0.0)
                s = jnp.sum(jnp.sum(v, axis=1, keepdims=True), axis=0, keepdims=True)
                o_ref[0, h:h + 1, b:b + 1] = s

    grid_spec = pltpu.PrefetchScalarGridSpec(
        num_scalar_prefetch=1, grid=(1,),
        in_specs=[pl.BlockSpec((NH * BLK, 2 * BLK), lambda i, me_ref: (0, 0)),
                  pl.BlockSpec((BLK, 2 * BLK), lambda i, me_ref: (0, 0))],
        out_specs=pl.BlockSpec((1, NH, NBUCKET), lambda i, me_ref: (me_ref[0], 0, 0)),
    )
    return _pcall(body, name="band_bias_bwd", grid_spec=grid_spec, out_shape=SDS((N_DEV, NH, NBUCKET), F32),
                  compiler_params=_params(("arbitrary",)))(me, dbias, bucket)


def _two_bf16(x):
    hi = x.astype(BF)
    return hi, (x - hi.astype(F32)).astype(BF)


def _head_sums(x, seg):
    hi, lo = _two_bf16(x)
    s = seg[0:x.shape[1], :]
    return _dot(hi, s, 1, 0) + _dot(lo, s, 1, 0)


def _head_spread(v, seg, width):
    hi, lo = _two_bf16(v)
    s = seg[0:width, :]
    return _dot(hi, s, 1, 1) + _dot(lo, s, 1, 1)


def _head_norm(x, g_t, seg, by_head=False):
    if by_head:
        heads = [x[:, h * HD:(h + 1) * HD] for h in range(x.shape[1] // HD)]
        r = jnp.concatenate([jnp.broadcast_to(lax.rsqrt(jnp.mean(v * v, axis=-1, keepdims=True) + EPS), v.shape)
                             for v in heads], axis=1)
    else:
        r = lax.rsqrt(_head_sums(x * x, seg) * (1.0 / HD) + EPS)
        r = _head_spread(r, seg, x.shape[1])
    return x * r * g_t, r


def _head_norm_bwd(dy, x, r, g_t, seg):
    dg_t = jnp.sum(dy * (x * r), axis=0, keepdims=True)
    dgx = dy * g_t
    mean = _head_spread(_head_sums(x * dgx, seg) * (1.0 / HD), seg, x.shape[1])
    return r * dgx - x * (r * r * r) * mean, dg_t


def _fold_heads(v):
    out = v[:, 0:HD]
    for h in range(1, v.shape[1] // HD):
        out = out + v[:, h * HD:(h + 1) * HD]
    return out


def _mix_forward(P, zc8, zh8, pkv, first, cw, qg_t, kg_t, gco, gao, seg, sink_ref, bias_ref, by_head=False):
    gate_b = P[:, 0:CW]
    gate_c = P[:, CW:2 * CW]
    hc = P[:, 2 * CW:3 * CW]
    z = gate_c * hc
    keep = jnp.where(first, 0.0, 1.0)
    zp = zc8 * zh8 * keep
    p1 = zp[7:8, :]
    p2 = zp[6:7, :]
    row = lax.broadcasted_iota(jnp.int32, (BLK, 1), 0)
    z1 = jnp.where(row == 0, p1, pltpu.roll(z, 1, 0))
    z2 = jnp.where(row == 0, p2, jnp.where(row == 1, p1, pltpu.roll(z, 2, 0)))
    cz = cw[0:1, :] * z2 + cw[1:2, :] * z1 + cw[2:3, :] * z
    y_conv = gate_b * cz

    scale = HD ** -0.5
    qi = lax.broadcasted_iota(jnp.int32, (BLK, 2 * BLK), 0)
    kj = lax.broadcasted_iota(jnp.int32, (BLK, 2 * BLK), 1)
    dd = qi + BLK - kj
    first_key = jnp.where(first, BLK, 0)
    valid = (dd >= 0) & (dd < BLK) & (kj >= first_key)

    q0 = 3 * CW
    k0 = q0 + AW
    v0 = k0 + NKV * HD
    q_raw = P[:, q0:k0]
    qn, rq = _head_norm(q_raw, qg_t, seg, by_head)
    qs = (qn * scale).astype(BF)
    k_raw = jnp.concatenate([pkv[:, 0:NKV * HD], P[:, k0:v0]], axis=0)
    kn, rk = _head_norm(k_raw, kg_t, seg, by_head)
    knb = kn.astype(BF)
    heads = []
    for h in range(NH):
        kv = h // GQ
        kb = knb[:, kv * HD:(kv + 1) * HD]
        vb = jnp.concatenate([pkv[:, NKV * HD + kv * HD:NKV * HD + (kv + 1) * HD],
                              P[:, v0 + kv * HD:v0 + (kv + 1) * HD]], axis=0).astype(BF)
        Q = qs[:, h * HD:(h + 1) * HD]
        S = _dot(Q, kb, 1, 1) + bias_ref[h * BLK:(h + 1) * BLK, :]
        S = jnp.where(valid, S, NEG_INF)
        sink = sink_ref[0, h]
        m = jnp.maximum(jnp.max(S, axis=-1, keepdims=True), sink)
        p = jnp.exp(S - m)
        es = jnp.exp(sink - m)
        denom = jnp.sum(p, axis=-1, keepdims=True) + es
        probs = p / denom
        O = _dot(probs.astype(BF), vb, 1, 0)
        heads.append(dict(kb=kb, vb=vb, Q=Q, probs=probs, psink=es / denom, O=O))
    y_attn = jnp.concatenate([hd["O"] for hd in heads], axis=1)

    rc = lax.rsqrt(jnp.mean(y_conv * y_conv, axis=-1, keepdims=True) + EPS)
    ra = lax.rsqrt(jnp.mean(y_attn * y_attn, axis=-1, keepdims=True) + EPS)
    y = jnp.concatenate([y_conv * rc * gco, y_attn * ra * gao], axis=1)
    return dict(gate_b=gate_b, gate_c=gate_c, hc=hc, z=z, z1=z1, z2=z2, cz=cz, y_conv=y_conv, y_attn=y_attn,
                rc=rc, ra=ra, heads=heads, y=y, row=row, scale=scale, q_raw=q_raw, rq=rq, k_raw=k_raw, rk=rk)


BPS = 2
TILE = BPS * BLK
KV0 = 3 * CW + AW


def _mix_in_specs(tile_of):
    return [
        pl.BlockSpec(memory_space=pltpu.SMEM),
        pl.BlockSpec((TILE, INW), lambda s: (tile_of(s), 0)),
        pl.BlockSpec((8, CW), lambda s: (jnp.maximum(tile_of(s) * (TILE // 8) - 1, 0), 1)),
        pl.BlockSpec((8, CW), lambda s: (jnp.maximum(tile_of(s) * (TILE // 8) - 1, 0), 2)),
        pl.BlockSpec((BLK, 2 * NKV * HD), lambda s: (jnp.maximum(tile_of(s) * BPS - 1, 0), KV0 // (2 * NKV * HD))),
    ]


def _block_inputs(tile, b, zc_ref, zh_ref, pkv_ref, first_tile):
    P = tile[b * BLK:(b + 1) * BLK, :]
    if b == 0:
        return P, zc_ref[...], zh_ref[...], pkv_ref[...], first_tile
    lo = b * BLK
    return P, tile[lo - 8:lo, CW:2 * CW], tile[lo - 8:lo, 2 * CW:3 * CW], tile[lo - BLK:lo, KV0:KV0 + 2 * NKV * HD], False


def _mix_param_specs():
    return [
        pl.BlockSpec((8, CW), lambda s: (0, 0)),
        pl.BlockSpec((1, AW), lambda s: (0, 0)),
        pl.BlockSpec((1, NKV * HD), lambda s: (0, 0)),
        pl.BlockSpec((1, CW), lambda s: (0, 0)),
        pl.BlockSpec((1, AW), lambda s: (0, 0)),
        pl.BlockSpec((AW, 128), lambda s: (0, 0)),
        pl.BlockSpec((NH * BLK, 2 * BLK), lambda s: (0, 0)),
    ]


def _mix_params(cw8, qg, kg, gco, gao, bias):
    seg = np.zeros((AW, 128), np.float32)
    seg[np.arange(AW), np.arange(AW) // HD] = 1.0
    return (cw8, jnp.tile(qg, (1, NH)), jnp.tile(kg, (1, NKV)), gco, gao, jnp.asarray(seg, BF), bias)


def _mix_fwd(proj, sinks, cw8, qg, kg, gco, gao, bias, comm=()):
    def body(sink_ref, p_ref, zc_ref, zh_ref, pkv_ref, cw_ref, qg_ref, kg_ref, gco_ref, gao_ref, seg_ref, bias_ref, y_ref):
        tile = p_ref[...]
        for b in range(BPS):
            f = _mix_forward(*_block_inputs(tile, b, zc_ref, zh_ref, pkv_ref, pl.program_id(0) == 0), cw_ref[...],
                             qg_ref[...], kg_ref[...], gco_ref[...], gao_ref[...], seg_ref[...], sink_ref, bias_ref, by_head=True)
            y_ref[b * BLK:(b + 1) * BLK, :] = f["y"].astype(BF)

    return _call(
        body, (sinks, proj, proj, proj, proj, *_mix_params(cw8, qg, kg, gco, gao, bias)), name="mix_fwd", grid=(T // TILE,),
        in_specs=_mix_in_specs(lambda s: s) + _mix_param_specs(),
        out_specs=[pl.BlockSpec((TILE, D), lambda s: (s, 0))], out_shape=[SDS((T, D), BF)],
        sem=("parallel",), vmem_mib=40, comm=comm, free=tuple(range(5, 12)))


def _mix_bwd(proj, dy, sinks, cw8, qg, kg, gco, gao, bias, comm=()):
    n_steps = T // TILE

    def tile_of(s):
        return n_steps - 1 - s

    def body(sink_ref, p_ref, zc_ref, zh_ref, pkv_ref, dy_ref, cw_ref, qg_ref, kg_ref, gco_ref, gao_ref, seg_ref, bias_ref,
             dproj_ref, dcw_ref, dqg_ref, dkg_ref, dgco_ref, dgao_ref, dsink_ref, dbias_ref,
             ndcz_ref, dkc_ref, dvc_ref):
        s = pl.program_id(0)

        @pl.when(s == 0)
        def _():
            for r in (dcw_ref, dqg_ref, dkg_ref, dgco_ref, dgao_ref, dsink_ref, dbias_ref, ndcz_ref, dkc_ref, dvc_ref):
                r[...] = jnp.zeros_like(r)

        params = (cw_ref[...], qg_ref[...], kg_ref[...], gco_ref[...], gao_ref[...], seg_ref[...])
        tile = p_ref[...]
        carry = (ndcz_ref[...], dkc_ref[...], dvc_ref[...])
        total = None
        for b in reversed(range(BPS)):
            f = _mix_forward(*_block_inputs(tile, b, zc_ref, zh_ref, pkv_ref, s == n_steps - 1), *params, sink_ref, bias_ref)
            pieces, sums, carry = one_block(f, dy_ref[b * BLK:(b + 1) * BLK, :], params, carry)
            for lo, piece in pieces:
                dproj_ref[b * BLK:(b + 1) * BLK, lo:lo + piece.shape[1]] = piece
            total = sums if total is None else [t + v for t, v in zip(total, sums)]
        ndcz_ref[...], dkc_ref[...], dvc_ref[...] = carry
        dcw, dqg_t, dkg_t, dgco, dgao, dsink, *ds = total
        dcw_ref[0:3, :] += dcw
        dqg_ref[...] += _fold_heads(dqg_t)
        dkg_ref[...] += _fold_heads(dkg_t)
        dgco_ref[...] += dgco
        dgao_ref[...] += dgao
        dsink_ref[...] += dsink
        for h in range(NH):
            dbias_ref[h * BLK:(h + 1) * BLK, :] += ds[h]

    def one_block(f, dy, params, carry):
        cw, qg_v, kg_v, gco_v, gao_v, seg = params
        nxt, dk_carry, dv_carry = carry
        dyc, dgco = _rms_bwd(dy[:, 0:CW], f["y_conv"], f["rc"], gco_v)
        dya, dgao = _rms_bwd(dy[:, CW:CW + AW], f["y_attn"], f["ra"], gao_v)

        row = f["row"]
        dgate_b = dyc * f["cz"]
        dcz = dyc * f["gate_b"]
        dcw = jnp.concatenate([jnp.sum(dcz * f[k], axis=0, keepdims=True) for k in ("z2", "z1", "z")], axis=0)
        n0 = nxt[0:1, :]
        n1 = nxt[1:2, :]
        d1 = jnp.where(row == BLK - 1, n0, pltpu.roll(dcz, BLK - 1, 0))
        d2 = jnp.where(row == BLK - 1, n1, jnp.where(row == BLK - 2, n0, pltpu.roll(dcz, BLK - 2, 0)))
        dz = cw[2:3, :] * dcz + cw[1:2, :] * d1 + cw[0:1, :] * d2
        pieces = [(0, dgate_b.astype(BF)), (CW, (dz * f["hc"]).astype(BF)), (2 * CW, (dz * f["gate_c"]).astype(BF))]

        scale = f["scale"]
        lane = lax.broadcasted_iota(jnp.int32, (1, 128), 1)
        dsink = jnp.zeros((1, 128), F32)
        dq_cols, dk_cols, dv_cols, dk_prev, dv_prev, ds = [], [], [], [], [], []
        for kv in range(NKV):
            dKb = dVb = 0.0
            for h in range(kv * GQ, (kv + 1) * GQ):
                hd = f["heads"][h]
                dO = dya[:, h * HD:(h + 1) * HD]
                delta = jnp.sum(dO * hd["O"], axis=-1, keepdims=True)
                dOb = dO.astype(BF)
                dP = _dot(dOb, hd["vb"], 1, 1)
                dS = hd["probs"] * (dP - delta)
                tot = jnp.sum(hd["psink"] * delta, axis=0, keepdims=True)
                dsink = dsink - jnp.where(lane == h, tot, 0.0)
                ds.append(dS)
                dSb = dS.astype(BF)
                dq_cols.append(_dot(dSb, hd["kb"], 1, 0))
                dKb = dKb + _dot(dSb, hd["Q"], 0, 0)
                dVb = dVb + _dot(hd["probs"].astype(BF), dOb, 0, 0)
            dk_cols.append(dKb[BLK:, :] + dk_carry[:, kv * HD:(kv + 1) * HD])
            dv_cols.append(dVb[BLK:, :] + dv_carry[:, kv * HD:(kv + 1) * HD])
            dk_prev.append(dKb[:BLK, :])
            dv_prev.append(dVb[:BLK, :])
        dq_raw, dqg_t = _head_norm_bwd(jnp.concatenate(dq_cols, axis=1) * scale, f["q_raw"], f["rq"], qg_v, seg)
        dk_raw, dkg_t = _head_norm_bwd(jnp.concatenate(dk_cols, axis=1), f["k_raw"][BLK:, :], f["rk"][BLK:, :], kg_v, seg)
        pieces.append((3 * CW, jnp.concatenate([dq_raw, dk_raw] + dv_cols, axis=1).astype(BF)))
        owed = (dcz[0:8, :], jnp.concatenate(dk_prev, axis=1), jnp.concatenate(dv_prev, axis=1))
        return pieces, [dcw, dqg_t, dkg_t, dgco, dgao, dsink, *ds], owed

    small = lambda r, c: pl.BlockSpec((r, c), lambda s: (0, 0))
    return _call(
        body, (sinks, proj, proj, proj, proj, dy, *_mix_params(cw8, qg, kg, gco, gao, bias)), name="mix_bwd", grid=(n_steps,),
        in_specs=_mix_in_specs(tile_of) + [pl.BlockSpec((TILE, D), lambda s: (tile_of(s), 0))] + _mix_param_specs(),
        out_specs=[pl.BlockSpec((TILE, INW), lambda s: (tile_of(s), 0)), small(8, CW), small(1, HD), small(1, HD),
                   small(1, CW), small(1, AW), small(1, 128), small(NH * BLK, 2 * BLK)],
        out_shape=[SDS((T, INW), BF), SDS((8, CW), F32), SDS((1, HD), F32), SDS((1, HD), F32), SDS((1, CW), F32),
                   SDS((1, AW), F32), SDS((1, 128), F32), SDS((NH * BLK, 2 * BLK), F32)],
        scratch_shapes=[pltpu.VMEM((8, CW), F32), pltpu.VMEM((BLK, NKV * HD), F32), pltpu.VMEM((BLK, NKV * HD), F32)],
        sem=("arbitrary",), vmem_mib=56, comm=comm, free=(1, 2, 3, 4) + tuple(range(6, 13)))


FT = 256
NFT = DFF // FT
RC = 128
NCH = T // RC
LEAD = 16


def _rows8(x):
    return jnp.sum(x.reshape(x.shape[0] // 8, 8, x.shape[1]), axis=0)


def _ffn_act_specs():
    return [
        pl.BlockSpec((T, FT), lambda j: (0, j)), pl.BlockSpec((T, FT), lambda j: (0, NFT + j)),
        pl.BlockSpec((8, FT), lambda j: (0, j)), pl.BlockSpec((8, FT), lambda j: (0, NFT + j)),
        pl.BlockSpec((1, FT), lambda j: (0, j)), pl.BlockSpec((1, FT), lambda j: (0, NFT + j)),
    ]


def _conv_rows(win, w, b, n):
    win = win.astype(F32)
    u = win[LEAD:LEAD + n]
    u1 = pltpu.roll(win, 1, 0)[LEAD:LEAD + n]
    u2 = pltpu.roll(win, 2, 0)[LEAD:LEAD + n]
    return u2, u1, u, w[0:1, :] * u2 + w[1:2, :] * u1 + w[2:3, :] * u + b


def _ffn_act(up, fw8, fb, comm=()):
    def body(ug_ref, uv_ref, wg_ref, wv_ref, bg_ref, bv_ref, a_ref):
        wg, wv, bg, bv = wg_ref[...], wv_ref[...], bg_ref[...], bv_ref[...]

        def chunk(win_g, win_v):
            gp = _conv_rows(win_g, wg, bg, RC)[3]
            vp = _conv_rows(win_v, wv, bv, RC)[3]
            return (gp * jax.nn.sigmoid(gp) * vp).astype(BF)

        zero = jnp.zeros((LEAD, FT), BF)
        a_ref[0:RC, :] = chunk(jnp.concatenate([zero, ug_ref[0:RC, :]], axis=0),
                               jnp.concatenate([zero, uv_ref[0:RC, :]], axis=0))

        def step(i, carry):
            r0 = pl.multiple_of(i * RC, RC)
            win = pl.ds(r0 - LEAD, RC + LEAD)
            a_ref[pl.ds(r0, RC), :] = chunk(ug_ref[win, :], uv_ref[win, :])
            return carry

        lax.fori_loop(1, NCH, step, 0)

    return _call(
        body, (up, up, fw8, fw8, fb, fb), name="ffn_act", grid=(NFT,), in_specs=_ffn_act_specs(),
        out_specs=[pl.BlockSpec((T, FT), lambda j: (0, j))], out_shape=[SDS((T, DFF), BF)],
        sem=("parallel",), vmem_mib=40, comm=comm, free=(2, 3, 4, 5))


def _ffn_act_bwd(up, da, fw8, fb, comm=()):
    ext = RC + LEAD

    def body(ug_ref, uv_ref, wg_ref, wv_ref, bg_ref, bv_ref, da_ref,
             dug_ref, duv_ref, dwg_ref, dwv_ref, dbg_ref, dbv_ref):
        wg, wv, bg, bv = wg_ref[...], wv_ref[...], bg_ref[...], bv_ref[...]

        def chunk(win_g, win_v, da_e):
            g2, g1, g0, gp = _conv_rows(win_g, wg, bg, ext)
            v2, v1, v0, vp = _conv_rows(win_v, wv, bv, ext)
            da_e = da_e.astype(F32)
            sig = jax.nn.sigmoid(gp)
            dvp = da_e * (gp * sig)
            dgp = da_e * vp * (sig * (1.0 + gp * (1.0 - sig)))

            def back(dp, w):
                return (w[2:3, :] * dp[0:RC] + w[1:2, :] * pltpu.roll(dp, ext - 1, 0)[0:RC]
                        + w[0:1, :] * pltpu.roll(dp, ext - 2, 0)[0:RC]).astype(BF)

            def sums(dp, u2, u1, u0):
                d = dp[0:RC]
                return [_rows8(d), _rows8(d * u2[0:RC]), _rows8(d * u1[0:RC]), _rows8(d * u0[0:RC])]

            return back(dgp, wg), back(dvp, wv), sums(dgp, g2, g1, g0) + sums(dvp, v2, v1, v0)

        zero = jnp.zeros((LEAD, FT), BF)
        dug, duv, acc = chunk(jnp.concatenate([zero, ug_ref[0:ext, :]], axis=0),
                              jnp.concatenate([zero, uv_ref[0:ext, :]], axis=0), da_ref[0:ext, :])
        dug_ref[0:RC, :] = dug
        duv_ref[0:RC, :] = duv

        def step(i, acc):
            r0 = pl.multiple_of(i * RC, RC)
            win = pl.ds(r0 - LEAD, ext + LEAD)
            dug, duv, part = chunk(ug_ref[win, :], uv_ref[win, :], da_ref[pl.ds(r0, ext), :])
            dug_ref[pl.ds(r0, RC), :] = dug
            duv_ref[pl.ds(r0, RC), :] = duv
            return [a + p for a, p in zip(acc, part)]

        acc = lax.fori_loop(1, NCH - 1, step, acc)
        r0 = T - RC
        tail = lambda ref, lo: jnp.concatenate([ref[lo:T, :], zero], axis=0)
        dug, duv, part = chunk(tail(ug_ref, r0 - LEAD), tail(uv_ref, r0 - LEAD), tail(da_ref, r0))
        dug_ref[r0:T, :] = dug
        duv_ref[r0:T, :] = duv
        tot = [jnp.sum(a + p, axis=0, keepdims=True) for a, p in zip(acc, part)]
        for k, (dw_ref, db_ref) in enumerate(((dwg_ref, dbg_ref), (dwv_ref, dbv_ref))):
            db_ref[...] = tot[4 * k]
            dw_ref[...] = jnp.zeros_like(dw_ref)
            for r in range(3):
                dw_ref[r:r + 1, :] = tot[4 * k + 1 + r]

    col = lambda r: pl.BlockSpec((r, FT), lambda j: (0, j))
    return _call(
        body, (up, up, fw8, fw8, fb, fb, da), name="ffn_act_bwd", grid=(NFT,),
        in_specs=_ffn_act_specs() + [pl.BlockSpec((T, FT), lambda j: (0, j))],
        out_specs=[col(T), col(T), col(8), col(8), col(1), col(1)],
        out_shape=[SDS((T, DFF), BF), SDS((T, DFF), BF), SDS((8, DFF), F32), SDS((8, DFF), F32),
                   SDS((1, DFF), F32), SDS((1, DFF), F32)],
        sem=("parallel",), vmem_mib=40, comm=comm, free=(0, 1, 2, 3, 4, 5))


def _ffn_down_bwd(dh2b, w_down, comm=()):
    tm = TM

    def body(d_ref, w_ref, o_ref):
        o_ref[...] = _dot(d_ref[...], w_ref[...], 1, 1).astype(BF)

    return _call(
        body, (dh2b, w_down), name="ffn_down_bwd", grid=(T // tm,),
        in_specs=[pl.BlockSpec((tm, D), lambda i: (i, 0)), _resident((DFF, D))],
        out_specs=[pl.BlockSpec((tm, DFF), lambda i: (i, 0))], out_shape=[SDS((T, DFF), BF)],
        sem=("parallel",), vmem_mib=40, comm=comm, free=(0, 1))


def _norm_matmul_bwd(name, a_list, w_t, k_offsets, xin, g, dres, want_bf16, comm=(), slot=None):
    tm = TM
    ks = [a.shape[1] for a in a_list]
    n_a = len(a_list)
    n_pre = 0 if slot is None else 1

    def body(*refs):
        refs = refs[n_pre:]
        a_refs = refs[:n_a]
        w_ref, x_ref, g_ref, r_ref = refs[n_a:n_a + 4]
        outs = refs[n_a + 4:]
        dx_ref, dg_ref = outs[0], (outs[-1] if slot is None else outs[-1].at[0])

        @pl.when(pl.program_id(0) == 0)
        def _():
            dg_ref[...] = jnp.zeros_like(dg_ref)

        du = _dot(a_refs[0][...], w_ref[k_offsets[0]:k_offsets[0] + ks[0], :], 1, 0)
        for k in range(1, n_a):
            du = du + _dot(a_refs[k][...], w_ref[k_offsets[k]:k_offsets[k] + ks[k], :], 1, 0)
        x = x_ref[...]
        r = lax.rsqrt(jnp.mean(x * x, axis=-1, keepdims=True) + EPS)
        dx, dg = _rms_bwd(du, x, r, g_ref[...])
        dx = r_ref[...] + dx
        dx_ref[...] = dx
        if want_bf16:
            outs[1][...] = dx.astype(BF)
        dg_ref[...] += dg

    tile = lambda c: pl.BlockSpec((tm, c), lambda i, *_: (i, 0))
    if slot is None:
        dg_spec, dg_shape = pl.BlockSpec((1, D), lambda i: (0, 0)), SDS((1, D), F32)
    else:
        dg_spec, dg_shape = pl.BlockSpec((1, 1, D), lambda i, slot_ref: (slot_ref[0], 0, 0)), SDS((N_DEV, 1, D), F32)
    out_specs = [tile(D)] + ([tile(D)] if want_bf16 else []) + [dg_spec]
    out_shape = [SDS((T, D), F32)] + ([SDS((T, D), BF)] if want_bf16 else []) + [dg_shape]
    return _call(
        body, (*a_list, w_t, xin, g, dres), name=name, grid=(T // tm,), prefetch=() if slot is None else (slot,),
        in_specs=[tile(k) for k in ks] + [_resident(w_t.shape), tile(D),
                                           pl.BlockSpec((1, D), lambda i, *_: (0, 0)), tile(D)],
        out_specs=out_specs, out_shape=out_shape, sem=("arbitrary",), vmem_mib=56, comm=comm, free=tuple(range(n_a + 4)))


def _out_bwd(dh1b, w_out, comm=()):
    tm = TM

    def body(d_ref, w_ref, o_ref):
        o_ref[...] = _dot(d_ref[...], w_ref[...], 1, 1)

    return _call(
        body, (dh1b, w_out), name="out_bwd", grid=(T // tm,),
        in_specs=[pl.BlockSpec((tm, D), lambda i: (i, 0)), _resident((D, D))],
        out_specs=[pl.BlockSpec((tm, D), lambda i: (i, 0))], out_shape=[SDS((T, D), F32)],
        sem=("parallel",), vmem_mib=32, comm=comm, free=(0, 1))


def _wgrad(name, a_list, b, old_a, comm=()):
    m_k = a_list[0].shape[1]
    tm = max(t for t in range(128, m_k // 2 + 1, 128) if m_k % t == 0)
    steps = [a.shape[1] // tm for a in a_list]
    starts = [sum(steps[:k]) for k in range(len(a_list))]
    n_a = len(a_list)

    def body(*refs):
        a_refs, b_ref, o_ref = refs[:n_a], refs[n_a], refs[n_a + 1]
        i = pl.program_id(0)
        for k in range(n_a):
            @pl.when((i >= starts[k]) & (i < starts[k] + steps[k]))
            def _(k=k):
                o_ref[...] = _dot(a_refs[k][...], b_ref[...], 0, 0).astype(BF)

    def a_spec(k):
        return pl.BlockSpec((T, tm), lambda i: (0, jnp.clip(i - starts[k], 0, steps[k] - 1)))

    m_total = tm * sum(steps)
    return _call(
        body, (*a_list, b), name=name, grid=(sum(steps),),
        in_specs=[a_spec(k) for k in range(n_a)] + [_resident((T, D))],
        out_specs=[pl.BlockSpec((tm, D), lambda i: (i, 0))], out_shape=[SDS((m_total, D), BF)],
        sem=("parallel",), vmem_mib=40, comm=comm, free=() if old_a is None else tuple(range(n_a)) if old_a else (n_a,))


def _chip_sum(name, gbf, from_sib, core, chip):
    h = gbf.shape[1]
    th = h

    def body(core_ref, chip_ref, g_ref, s_ref, pbf_ref, own_ref):
        p = g_ref[0].astype(F32) + s_ref[0].astype(F32)
        pbf_ref[0] = p.astype(BF)

        @pl.when(pl.program_id(1) == chip_ref[0])
        def _():
            own_ref[...] = p

    grid_spec = pltpu.PrefetchScalarGridSpec(
        num_scalar_prefetch=2, grid=(h // th, N_CHIPS),
        in_specs=[pl.BlockSpec((1, th, D), lambda t, jj, core_ref, chip_ref: (2 * jj + core_ref[0], t, 0)),
                  pl.BlockSpec((1, th, D), lambda t, jj, core_ref, chip_ref: (jj, t, 0))],
        out_specs=[pl.BlockSpec((1, th, D), lambda t, jj, core_ref, chip_ref: (jj, t, 0)),
                   pl.BlockSpec((th, D), lambda t, jj, core_ref, chip_ref: (t, 0))],
    )
    return _pcall(
        body, name=name, grid_spec=grid_spec, out_shape=_in_hbm([SDS((N_CHIPS, h, D), BF), SDS((h, D), F32)]),
        compiler_params=_params(("arbitrary", "arbitrary"), 32),
    )(core, chip, *_from_hbm(gbf, from_sib))


def _final_sum(name, own, from_chips, core, comm=()):
    h = own.shape[0]

    def body(core_ref, o_ref, r_ref, f_ref):
        f_ref[0] = ((o_ref[...] + r_ref[0].astype(F32)) + r_ref[1].astype(F32)) + r_ref[2].astype(F32)

    return _call(
        body, (own, from_chips), name=name, grid=(1,), prefetch=(core,),
        in_specs=[pl.BlockSpec((h, D), lambda i, core_ref: (0, 0)), pl.BlockSpec((3, h, D), lambda i, core_ref: (0, 0, 0))],
        out_specs=[pl.BlockSpec((1, h, D), lambda i, core_ref: (core_ref[0], 0, 0))], out_shape=[SDS((2, h, D), F32)],
        sem=("arbitrary",), vmem_mib=40, comm=comm)


def _adam_math(w, g, m, v):
    nm = ADAM_B1 * m + (1.0 - ADAM_B1) * g
    nv = ADAM_B2 * v + (1.0 - ADAM_B2) * (g * g)
    m_hat = nm / (1.0 - ADAM_B1 ** ADAM_STEP)
    v_hat = nv / (1.0 - ADAM_B2 ** ADAM_STEP)
    return -ADAM_LR * (m_hat / (jnp.sqrt(v_hat) + ADAM_EPS) + ADAM_WD * w), nm, nv


def _adamw(name, w, g, m, v, tr, copy_g=False, stage=True, g_transposed=False):
    rows, cols = w.shape

    def body(w_ref, g_ref, m_ref, v_ref, *outs):
        d_ref, nm_ref, nv_ref = outs[-3:]
        for c in [pl.ds(c0, 128) for c0 in range(0, cols, 128)] if g_transposed else [slice(None)]:
            g_val = g_ref[c, :].T if g_transposed else g_ref[...]
            if copy_g:
                outs[0][:, c] = g_val
            d_ref[:, c], nm_ref[:, c], nv_ref[:, c] = _adam_math(w_ref[:, c], g_val, m_ref[:, c], v_ref[:, c])

    spec = pl.BlockSpec((tr, cols), lambda i: (i, 0))
    n_out = 4 if copy_g else 3
    g_spec = pl.BlockSpec((cols, tr), lambda i: (0, i)) if g_transposed else spec
    return _call(body, (w, g, m, v), name=name, grid=(rows // tr,), in_specs=[spec, g_spec, spec, spec], out_specs=[spec] * n_out,
                 out_shape=[SDS((rows, cols), F32)] * n_out, sem=("parallel",), vmem_mib=32,
                 free=(0, 2, 3) if stage else ())


C_G1, C_G2, C_GCO, C_GAO, C_DCW, C_DQG, C_DKG, C_SINK, C_SQ = 0, 1024, 2048, 2560, 3072, 4608, 4736, 4864, 5632
P_W = C_SQ + 128


def _pack_small(me, dfwg, dfwv, dfbg, dfbv, dg2, dgco, dgao, dcw8, dqg, dkg, dsink, sq):
    def body(me_ref, dfwg_r, dfwv_r, dfbg_r, dfbv_r, dg2_r, dgco_r, dgao_r, dcw_r, dqg_r, dkg_r, dsink_r, sq_r, o):
        o[...] = jnp.zeros_like(o)
        o[0, :, 0:DFF] = dfwg_r[...]
        o[0, :, DFF:2 * DFF] = dfwv_r[...]
        o[0, 3:4, 0:DFF] = dfbg_r[...]
        o[0, 3:4, DFF:2 * DFF] = dfbv_r[...]
        o[0, 4:5, C_G2:C_G2 + D] = dg2_r[...]
        o[0, 4:5, C_GCO:C_GCO + CW] = dgco_r[...]
        o[0, 4:5, C_GAO:C_GAO + AW] = dgao_r[...]
        for r in range(3):
            o[0, 4:5, C_DCW + r * CW:C_DCW + (r + 1) * CW] = dcw_r[r:r + 1, :]
        o[0, 4:5, C_DQG:C_DQG + HD] = dqg_r[...]
        o[0, 4:5, C_DKG:C_DKG + HD] = dkg_r[...]
        o[0, 4:5, C_SINK:C_SINK + 128] = dsink_r[...]
        o[0, :, C_SQ:C_SQ + 128] = sq_r[...]

    ins = (dfwg, dfwv, dfbg, dfbv, dg2, dgco, dgao, dcw8, dqg, dkg, dsink, sq)
    return _call(body, ins, name="pack_small", grid=(1,), prefetch=(me,),
                 in_specs=[pl.BlockSpec(a.shape, lambda i, me_ref: (0, 0)) for a in ins],
                 out_specs=[pl.BlockSpec((1, 8, P_W), lambda i, me_ref: (me_ref[0], 0, 0))],
                 out_shape=[SDS((N_DEV, 8, P_W), F32)], sem=("arbitrary",))[0]


N_SMALL = 11


def _small_adam(chip, p_all, g1_all, tbl_all, ws, ms, vs):
    fw_cols = 2 * DFF // N_CHIPS
    cw_cols = CW // N_CHIPS

    def body(chip_ref, p_ref, fw_ref, cw0_ref, cw1_ref, cw2_ref, g1_ref, tbl_ref, *refs):
        w_r, m_r, v_r = refs[0:N_SMALL], refs[N_SMALL:2 * N_SMALL], refs[2 * N_SMALL:3 * N_SMALL]
        outs = refs[3 * N_SMALL:]
        g_o, d_o, nm_o, nv_o = (outs[k * N_SMALL:(k + 1) * N_SMALL] for k in range(4))
        loss_o = outs[4 * N_SMALL]

        def total(ref):
            s = ref[0]
            for k in range(1, N_DEV):
                s = s + ref[k]
            return s

        S = total(p_ref)
        fw = total(fw_ref)
        cws = [total(r) for r in (cw0_ref, cw1_ref, cw2_ref)]

        def step(i, g, at):
            d, nm, nv = _adam_math(w_r[i][at], g, m_r[i][at], v_r[i][at])
            g_o[i][at], d_o[i][at], nm_o[i][at], nv_o[i][at] = g, d, nm, nv

        everything = (slice(None), slice(None))
        step(0, total(g1_ref), everything)
        for r in range(3):
            step(1, cws[r][4:5, :], (r, slice(None), slice(None)))
        step(2, S[4:5, C_DQG:C_DQG + HD], everything)
        step(3, S[4:5, C_DKG:C_DKG + HD], everything)
        step(4, total(tbl_ref), everything)
        step(5, S[4:5, C_SINK:C_SINK + NH], everything)
        step(6, S[4:5, C_GCO:C_GCO + CW], everything)
        step(7, S[4:5, C_GAO:C_GAO + AW], everything)
        step(8, S[4:5, C_G2:C_G2 + D], everything)
        for r in range(3):
            step(9, fw[r:r + 1, :], (r, slice(None), slice(None)))
        step(10, S[3:4, 0:2 * DFF], everything)
        sq = S[:, C_SQ:C_SQ + 128]
        loss_o[...] = jnp.sum(jnp.sum(sq, axis=1, keepdims=True), axis=0, keepdims=True) * (0.5 / D)

    def full(a):
        n = len(a.shape)
        return pl.BlockSpec(a.shape, lambda i, chip_ref: (0,) * n)

    params = [*ws, *ms, *vs]
    out = _call(
        body, (p_all, p_all, p_all, p_all, p_all, g1_all, tbl_all, *params), name="small_adam", grid=(1,), prefetch=(chip,),
        in_specs=[full(p_all),
                  pl.BlockSpec((N_DEV, 8, fw_cols), lambda i, chip_ref: (0, 0, chip_ref[0])),
                  *[pl.BlockSpec((N_DEV, 8, cw_cols), lambda i, chip_ref, r=r: (0, 0, (C_DCW + r * CW) // cw_cols + chip_ref[0]))
                    for r in range(3)],
                  full(g1_all), full(tbl_all), *[full(a) for a in params]],
        out_specs=[full(a) for a in ws] * 4 + [pl.BlockSpec((1, 1), lambda i, chip_ref: (0, 0))],
        out_shape=[SDS(a.shape, F32) for a in ws] * 4 + [SDS((1, 1), F32)], sem=("arbitrary",), vmem_mib=32)
    return out[0:N_SMALL], out[N_SMALL:2 * N_SMALL], out[2 * N_SMALL:3 * N_SMALL], out[3 * N_SMALL:4 * N_SMALL], out[4 * N_SMALL]


PLACE_STEPS = 4


def _place_specs(shards):
    rows = [s.shape[0] // PLACE_STEPS for s in shards]
    return ([pl.BlockSpec((r, D), lambda i, chip_ref: (i, 0)) for r in rows],
            [pl.BlockSpec((r, D), lambda i, chip_ref: (chip_ref[0] * PLACE_STEPS + i, 0)) for r in rows],
            [SDS((N_CHIPS * s.shape[0], D), BF) for s in shards])


def _place_first(chip, shard, conv_w, ffn_conv_w):
    def body(chip_ref, a, s0, s1, o, t0, t1):
        o[...] = a[...].astype(BF)

        @pl.when(pl.program_id(0) == 0)
        def _():
            for s, t in ((s0, t0), (s1, t1)):
                t[...] = jnp.zeros_like(t)
                t[0, 0:3, :] = s[...]

    ins, outs, shapes = _place_specs([shard])
    taps = (conv_w, ffn_conv_w)
    return _call(
        body, (shard, conv_w, ffn_conv_w), name="place_first", grid=(PLACE_STEPS,), prefetch=(chip,),
        in_specs=ins + [pl.BlockSpec(s.shape, lambda i, chip_ref: (0, 0)) for s in taps],
        out_specs=outs + [pl.BlockSpec((1, 8, s.shape[1]), lambda i, chip_ref: (chip_ref[0], 0, 0)) for s in taps],
        out_shape=shapes + [SDS((N_CHIPS, 8, s.shape[1]), F32) for s in taps],
        sem=("arbitrary",), vmem_mib=32, free=(1, 2))


def _place_rest(chip, shards, w_up, table, bucket, comm):
    n = len(shards)
    c_up = w_up.shape[1]
    edges = [round(k * (c_up // 128) / PLACE_STEPS) * 128 for k in range(PLACE_STEPS + 1)]

    def body(chip_ref, *refs):
        a, (up_ref, tab_ref, bk_ref), o = refs[:n], refs[n:n + 3], refs[n + 3:2 * n + 3]
        up_o, bias_ref = refs[2 * n + 3:]
        for src, dst in zip(a, o):
            dst[...] = src[...].astype(BF)
        for k in range(PLACE_STEPS):
            @pl.when(pl.program_id(0) == k)
            def _(k=k):
                up_o[edges[k]:edges[k + 1], :] = up_ref[:, edges[k]:edges[k + 1]].T.astype(BF)

        @pl.when(pl.program_id(0) == 0)
        def _():
            bk = bk_ref[...]
            eq = [bk == b for b in range(NBUCKET)]
            for h in range(NH):
                acc = jnp.zeros((BLK, 2 * BLK), F32)
                for b in range(NBUCKET):
                    acc = jnp.where(eq[b], tab_ref[h, b], acc)
                bias_ref[h * BLK:(h + 1) * BLK, :] = acc

    ins, outs, shapes = _place_specs(shards)
    return _call(
        body, (*shards, w_up, table, bucket), name="place_rest", grid=(PLACE_STEPS,), prefetch=(chip,),
        in_specs=ins + [_resident(w_up.shape), pl.BlockSpec(memory_space=pltpu.SMEM),
                        pl.BlockSpec(bucket.shape, lambda i, chip_ref: (0, 0))],
        out_specs=outs + [pl.BlockSpec((c_up, D), lambda i, chip_ref: (chip_ref[0], 0)),
                          pl.BlockSpec((NH * BLK, 2 * BLK), lambda i, chip_ref: (0, 0))],
        out_shape=shapes + [SDS((N_CHIPS * c_up, D), BF), SDS((NH * BLK, 2 * BLK), F32)],
        sem=("arbitrary",), vmem_mib=32, comm=comm, free=(n + 1, n + 2))


def kernel(x, norm_mix_g, w_in, conv_w, q_norm_g, k_norm_g, rel_bias_table, sinks, out_norm_conv_g, out_norm_attn_g, w_out, norm_ffn_g, w_up, ffn_conv_w, ffn_conv_b, w_down, loss_target, m_norm_mix_g, m_w_in, m_conv_w, m_q_norm_g, m_k_norm_g, m_rel_bias_table, m_sinks, m_out_norm_conv_g, m_out_norm_attn_g, m_w_out, m_norm_ffn_g, m_w_up, m_ffn_conv_w, m_ffn_conv_b, m_w_down, v_norm_mix_g, v_w_in, v_conv_w, v_q_norm_g, v_k_norm_g, v_rel_bias_table, v_sinks, v_out_norm_conv_g, v_out_norm_attn_g, v_w_out, v_norm_ffn_g, v_w_up, v_ffn_conv_w, v_ffn_conv_b, v_w_down):
    as_arg = lambda i: jnp.reshape(i, (1,)).astype(jnp.int32)
    chip = as_arg(2 * lax.axis_index("x") + lax.axis_index("y"))
    core = as_arg(lax.axis_index("c"))
    me = 2 * chip + core
    xs, tgt = x[0], loss_target[0]
    qg, kg, gco, gao, g1, g2, fb = q_norm_g, k_norm_g, out_norm_conv_g, out_norm_attn_g, norm_mix_g, norm_ffn_g, ffn_conv_b
    pieces = lambda g: g.reshape(N_DEV, g.shape[0] // N_DEV, D)
    whole = lambda f: f.reshape(2 * f.shape[1], D)

    bucket = jnp.asarray(_bucket_table())
    p_in, p_cw, p_fw = _place_first(chip, w_in[0].T, conv_w[0], ffn_conv_w[0])
    p_out, p_down, p_up, bias, w_int, cw_all, fw_all = _place_rest(
        chip, [w_out[0], w_down[0]], w_up[0], rel_bias_table.T, bucket,
        comm=[_t_gather(p_in), _t_small_weights(p_cw), _t_small_weights(p_fw)])
    cw8 = jnp.transpose(cw_all, (1, 0, 2)).reshape(8, CW)
    fw8 = jnp.transpose(fw_all, (1, 0, 2)).reshape(8, 2 * DFF)

    early = 3 / 11
    proj, u1, w_out_f, p_up = _inproj(xs, g1, w_int, comm=[_t_gather(p_out), _t_gather(p_up, (0, early))])
    y, w_upt = _mix_fwd(proj, sinks, cw8, qg, kg, gco, gao, bias, comm=[_t_gather(p_up, (early, 1))])
    h1, u2 = _outproj(y, w_out_f, xs, g2)
    up, = _ffn_up(u2, w_upt)
    a, w_down_f = _ffn_act(up, fw8, fb, comm=[_t_gather(p_down)])
    dh2, dh2b, sq = _ffn_down(a, w_down_f, h1, tgt)

    gdbf, = _wgrad("wgrad_down", [a], dh2b, None)
    da, sib_down = _ffn_down_bwd(dh2b, w_down_f, comm=[_t_sibling(pieces(gdbf))])
    pbf_down, own_down = _chip_sum("chip_sum_w_down", pieces(gdbf), sib_down, core, chip)
    dug, duv, dfwg, dfwv, dfbg, dfbv, chips_down = _ffn_act_bwd(up, da, fw8, fb, comm=[_t_chips(pbf_down)])
    fin_down, = _final_sum("final_sum_w_down", own_down, chips_down, core)
    gubf, = _wgrad("wgrad_up", [dug, duv], u2, False)
    dh1, dh1b, dg2, sib_up, fin_down = _norm_matmul_bwd(
        "ffn_up_bwd", [dug, duv], w_upt, [0, DFF], h1, g2, dh2, True, comm=[_t_sibling(pieces(gubf)), _t_swap(fin_down)])
    pbf_up, own_up = _chip_sum("chip_sum_w_up", pieces(gubf), sib_up, core, chip)
    gobf, = _wgrad("wgrad_out", [y], dh1b, True)
    dy, sib_out = _out_bwd(dh1b, w_out_f, comm=[_t_sibling(pieces(gobf))])
    pbf_out, own_out = _chip_sum("chip_sum_w_out", pieces(gobf), sib_out, core, chip)
    dproj, dcw8, dqg, dkg, dgco, dgao, dsink, dbias, chips_up, chips_out = _mix_bwd(
        proj, dy, sinks, cw8, qg, kg, gco, gao, bias, comm=[_t_chips(pbf_up), _t_chips(pbf_out)])
    fin_up, = _final_sum("final_sum_w_up", own_up, chips_up, core)
    tbl_all = _band_bias_bwd(dbias, bucket, me)
    p_all = _pack_small(me, dfwg, dfwv, dfbg, dfbv, dg2, dgco, dgao, dcw8, dqg, dkg, dsink, sq)
    gibf, = _wgrad("wgrad_in", [dproj], u1, False)
    fin_out, sib_in = _final_sum("final_sum_w_out", own_out, chips_out, core, comm=[_t_sibling(pieces(gibf))])
    pbf_in, own_in = _chip_sum("chip_sum_w_in", pieces(gibf), sib_in, core, chip)
    dx, g1_all, chips_in, fin_out, fin_up, p_all, tbl_all = _norm_matmul_bwd(
        "in_bwd", [dproj], w_int, [0], xs, g1, dh1, False, slot=me,
        comm=[_t_chips(pbf_in), _t_swap(fin_out), _t_swap(fin_up), _t_allgather(p_all), _t_allgather(tbl_all)])
    fin_in, = _final_sum("final_sum_w_in", own_in, chips_in, core)
    g1_all, fin_in = _comm_call("gather_last", [_t_allgather(g1_all), _t_swap(fin_in)])

    g_w_out, g_w_down = whole(fin_out), whole(fin_down)
    g_w_down, d_down, nm_down, nv_down = _adamw("adamw_w_down", w_down[0], g_w_down, m_w_down[0], v_w_down[0], 352, True)
    g_w_up, d_up, nm_up, nv_up = _adamw(
        "adamw_w_up", w_up[0], whole(fin_up), m_w_up[0], v_w_up[0], 256, True, stage=False, g_transposed=True)
    g_w_out, d_out, nm_out, nv_out = _adamw("adamw_w_out", w_out[0], g_w_out, m_w_out[0], v_w_out[0], 256, True, stage=False)
    g_w_in, d_in, nm_in, nv_in = [a.T for a in _adamw(
        "adamw_w_in", w_in[0].T, whole(fin_in), m_w_in[0].T, v_w_in[0].T, INW // N_CHIPS // 3, True)]
    taps = lambda a: jnp.transpose(a, (1, 0, 2))
    sw = [norm_mix_g, taps(conv_w), q_norm_g, k_norm_g, rel_bias_table.T, sinks, out_norm_conv_g, out_norm_attn_g,
          norm_ffn_g, taps(ffn_conv_w), ffn_conv_b]
    smm = [m_norm_mix_g, taps(m_conv_w), m_q_norm_g, m_k_norm_g, m_rel_bias_table.T, m_sinks, m_out_norm_conv_g,
           m_out_norm_attn_g, m_norm_ffn_g, taps(m_ffn_conv_w), m_ffn_conv_b]
    smv = [v_norm_mix_g, taps(v_conv_w), v_q_norm_g, v_k_norm_g, v_rel_bias_table.T, v_sinks, v_out_norm_conv_g,
           v_out_norm_attn_g, v_norm_ffn_g, taps(v_ffn_conv_w), v_ffn_conv_b]
    *small_out, loss = _small_adam(chip, p_all, g1_all, tbl_all, sw, smm, smv)
    sg, sd, snm, snv = [list(r) for r in small_out]
    for r in (sg, sd, snm, snv):
        r[1], r[4], r[9] = taps(r[1]), r[4].T, taps(r[9])

    def order(s, b_in, b_out, b_up, b_down):
        return (s[0], b_in[None], s[1], s[2], s[3], s[4], s[5], s[6], s[7], b_out[None], s[8], b_up[None],
                s[9], s[10], b_down[None])

    return (loss.reshape(()), dx[None],
            *order(sg, g_w_in, g_w_out, g_w_up, g_w_down),
            *order(sd, d_in, d_out, d_up, d_down),
            *order(snm, nm_in, nm_out, nm_up, nm_down),
            *order(snv, nv_in, nv_out, nv_up, nv_down))
```

```python
import functools
import math

import numpy as np

import jax
import jax.numpy as jnp
from jax import lax
from jax.experimental import pallas as pl
from jax.experimental.pallas import tpu as pltpu

F32 = jnp.float32
BF = jnp.bfloat16
SDS = jax.ShapeDtypeStruct

T = 2048
D = 1024
CW = 512
AW = 512
HD = 64
NH = 8
NKV = 2
GQ = 4
INW = 2304
DFF = 2816
BLK = 128
NB = T // BLK
NBUCKET = 32
EPS = 1e-6
NEG_INF = -1e30
N_CHIPS = 4
N_DEV = 8

ADAM_LR = 0.001
ADAM_B1 = 0.9
ADAM_B2 = 0.999
ADAM_EPS = 1e-08
ADAM_WD = 0.01
ADAM_STEP = 10

TM = 512
MIB = 1024 * 1024
MESH = pl.DeviceIdType.MESH
ANY = pl.BlockSpec(memory_space=pl.ANY)

_pcall = pl.pallas_call


def _params(sem=None, vmem_mib=None, collective_id=None):
    kw = {} if collective_id is None else {"collective_id": collective_id}
    if sem is not None:
        kw["dimension_semantics"] = sem
    if vmem_mib is not None:
        kw["vmem_limit_bytes"] = vmem_mib * MIB
    return pltpu.CompilerParams(**kw)


def _resident(shape):
    return pl.BlockSpec(shape, lambda *_: (0,) * len(shape), pipeline_mode=pl.Buffered(1))


def _dot(a, b, ca, cb):
    return lax.dot_general(a, b, (((ca,), (cb,)), ((), ())), preferred_element_type=F32)


def _rms_bwd(dy, x, r, g):
    dg = jnp.sum(dy * (x * r), axis=0, keepdims=True)
    dgx = dy * g
    dx = r * dgx - x * (r * r * r) * jnp.mean(x * dgx, axis=-1, keepdims=True)
    return dx, dg


def _where():
    x, y, c = lax.axis_index("x"), lax.axis_index("y"), lax.axis_index("c")
    return x, y, c, [(1 - x, y), (x, 1 - y), (1 - x, 1 - y)]


def _rcopy(src, dst, ssem, rsem, dev):
    return pltpu.make_async_remote_copy(src_ref=src, dst_ref=dst, send_sem=ssem, recv_sem=rsem, device_id=dev,
                                        device_id_type=MESH)


SIBLING, Y_CHIP, X_CHIP, DIAGONAL_CHIP = 1, 2, 4, 6
OTHER_CHIPS = (Y_CHIP, X_CHIP, DIAGONAL_CHIP)
EVERYONE = tuple(range(1, N_DEV))
BARRIER_OF = {(SIBLING,): 0, (SIBLING, Y_CHIP, X_CHIP): 1, OTHER_CHIPS: 2, (SIBLING,) + OTHER_CHIPS: 3, EVERYONE: 4}


def _peer(rel):
    x, y, c, _ = _where()
    return x ^ ((rel >> 2) & 1), y ^ ((rel >> 1) & 1), c ^ (rel & 1)


class _Task:
    def __init__(self, ins, outs, alias, n_sem, start, finish, middle=None, peers=()):
        self.ins, self.outs, self.alias, self.n_sem, self.start, self.finish = ins, outs, alias, n_sem, start, finish
        self.middle = middle if middle is not None else (lambda *args: None)
        self.peers = peers


def _peers_of(comm):
    return tuple(sorted({p for t in comm for p in t.peers}))


def _enter(comm):
    peers = _peers_of(comm)
    barrier = pltpu.get_barrier_semaphore()
    for rel in peers:
        pl.semaphore_signal(barrier, inc=1, device_id=_peer(rel), device_id_type=MESH)
    pl.semaphore_wait(barrier, len(peers))


ROWS16 = 16


def _t_gather(placed, part=(0, 1), relayed_first=False):
    R = placed.shape[0] // N_CHIPS
    q = R // 4
    lo, hi = (round(f * (q // ROWS16)) * ROWS16 for f in part)

    def quarter(chip_index, core, k):
        return pl.ds(pl.multiple_of(chip_index * R + core * 2 * q + k * q + lo, ROWS16), hi - lo)

    def places():
        x, y, c, _ = _where()
        return c, 2 * x + y, 2 * (1 - x) + y, 2 * x + (1 - y), 2 * (1 - x) + (1 - y), (1 - x, y, c), (x, 1 - y, c), (x, y, 1 - c)

    def copy(buf, k, chip_index, core, quart, ss, rs, b, dev):
        window = buf.at[quarter(chip_index, core, quart)]
        return _rcopy(window, window, ss.at[b + k], rs.at[b + k], dev)

    def first_hop(cout, ss, rs, b, which):
        c, me, _, _, _, x_nbr, y_nbr, _ = places()
        for k, (quart, dev) in enumerate(((0, x_nbr), (1, y_nbr), (1, x_nbr), (0, y_nbr))):
            if k in which:
                copy(cout[0], k, me, c, quart, ss, rs, b, dev).start()

    def start(cin, cout, ss, rs, b):
        first_hop(cout, ss, rs, b, (0, 1) if relayed_first else (0, 1, 2, 3))

    def middle(cin, cout, ss, rs, b):
        c, _, xc, yc, _, x_nbr, y_nbr, sib = places()
        for k, chip_index, quart, dev in ((0, xc, 0, y_nbr), (1, yc, 1, x_nbr)):
            copy(cout[0], k, chip_index, c, quart, ss, rs, b, dev).wait_recv()
            copy(cout[0], 4 + k, chip_index, c, quart, ss, rs, b, dev).start()
            copy(cout[0], 6 + k, chip_index, c, quart, ss, rs, b, sib).start()
        if relayed_first:
            first_hop(cout, ss, rs, b, (2, 3))

    later = ((2, 1, 1), (3, 2, 0), (4, 3, 0), (5, 3, 1))

    def finish(cin, cout, ss, rs, b):
        c, me, xc, yc, dc, _, _, sib = places()
        chip_of = {1: xc, 2: yc, 3: dc}
        for k, whose, quart in later:
            copy(cout[0], k, chip_of[whose], c, quart, ss, rs, b, sib).wait_recv()
            copy(cout[0], 6 + k, chip_of[whose], c, quart, ss, rs, b, sib).start()
        for k, whose, quart in ((0, 1, 0), (1, 2, 1)) + later:
            copy(cout[0], 6 + k, chip_of[whose], 1 - c, quart, ss, rs, b, sib).wait_recv()
        for k in range(12):
            copy(cout[0], k, me, c, 0, ss, rs, b, sib).wait_send()

    return _Task([placed], [SDS(placed.shape, placed.dtype)], [(0, 0)], 12, start, finish, middle, peers=(SIBLING, Y_CHIP, X_CHIP))


def _t_small_weights(buf):
    def start(cin, cout, ss, rs, b):
        x, y, c, chips = _where()
        mine = cout[0].at[2 * x + y]
        for r, (px, py) in enumerate(chips):
            _rcopy(mine, mine, ss.at[b + r], rs.at[b + r], (px, py, c)).start()

    def finish(cin, cout, ss, rs, b):
        x, y, c, chips = _where()
        for r, (px, py) in enumerate(chips):
            got = cout[0].at[2 * px + py]
            _rcopy(got, got, ss.at[b + r], rs.at[b + r], (px, py, c)).wait_recv()
        for r, (px, py) in enumerate(chips):
            mine = cout[0].at[2 * x + y]
            _rcopy(mine, mine, ss.at[b + r], rs.at[b + r], (px, py, c)).wait_send()

    return _Task([buf], [SDS(buf.shape, buf.dtype)], [(0, 0)], 3, start, finish, peers=OTHER_CHIPS)


def _t_sibling(gbf):
    def start(cin, cout, ss, rs, b):
        x, y, c, _ = _where()
        for jj in range(N_CHIPS):
            _rcopy(cin[0].at[2 * jj + (1 - c)], cout[0].at[jj], ss.at[b + jj], rs.at[b + jj], (x, y, 1 - c)).start()

    def finish(cin, cout, ss, rs, b):
        x, y, c, _ = _where()
        for jj in range(N_CHIPS):
            got = cout[0].at[jj]
            _rcopy(got, got, ss.at[b + jj], rs.at[b + jj], (x, y, 1 - c)).wait_recv()
        for jj in range(N_CHIPS):
            got = cout[0].at[jj]
            _rcopy(got, got, ss.at[b + jj], rs.at[b + jj], (x, y, 1 - c)).wait_send()

    return _Task([gbf], [SDS((N_CHIPS,) + gbf.shape[1:], BF)], [], N_CHIPS, start, finish, peers=(SIBLING,))


def _t_chips(pbf):
    def start(cin, cout, ss, rs, b):
        x, y, c, chips = _where()
        for r, (px, py) in enumerate(chips):
            _rcopy(cin[0].at[2 * px + py], cout[0].at[r], ss.at[b + r], rs.at[b + r], (px, py, c)).start()

    def finish(cin, cout, ss, rs, b):
        x, y, c, chips = _where()
        for r, (px, py) in enumerate(chips):
            got = cout[0].at[r]
            _rcopy(got, got, ss.at[b + r], rs.at[b + r], (px, py, c)).wait_recv()
        for r, (px, py) in enumerate(chips):
            got = cout[0].at[r]
            _rcopy(got, got, ss.at[b + r], rs.at[b + r], (px, py, c)).wait_send()

    return _Task([pbf], [SDS((3,) + pbf.shape[1:], BF)], [], 3, start, finish, peers=OTHER_CHIPS)


def _t_swap(fin):
    def start(cin, cout, ss, rs, b):
        x, y, c, _ = _where()
        mine = cout[0].at[c]
        _rcopy(mine, mine, ss.at[b], rs.at[b], (x, y, 1 - c)).start()

    def finish(cin, cout, ss, rs, b):
        x, y, c, _ = _where()
        got = cout[0].at[1 - c]
        _rcopy(got, got, ss.at[b], rs.at[b], (x, y, 1 - c)).wait_recv()
        _rcopy(got, got, ss.at[b], rs.at[b], (x, y, 1 - c)).wait_send()

    return _Task([fin], [SDS(fin.shape, fin.dtype)], [(0, 0)], 1, start, finish, peers=(SIBLING,))


def _t_allgather(buf):
    def peers():
        x, y, c, _ = _where()
        out = []
        for rel in range(1, N_DEV):
            px, py, pc = x ^ ((rel >> 2) & 1), y ^ ((rel >> 1) & 1), c ^ (rel & 1)
            out.append((rel - 1, 4 * px + 2 * py + pc, (px, py, pc)))
        return 4 * x + 2 * y + c, out

    def start(cin, cout, ss, rs, b):
        me, ps = peers()
        mine = cout[0].at[me]
        for k, _, dev in ps:
            _rcopy(mine, mine, ss.at[b + k], rs.at[b + k], dev).start()

    def finish(cin, cout, ss, rs, b):
        me, ps = peers()
        for k, pidx, dev in ps:
            got = cout[0].at[pidx]
            _rcopy(got, got, ss.at[b + k], rs.at[b + k], dev).wait_recv()
        for k, _, dev in ps:
            mine = cout[0].at[me]
            _rcopy(mine, mine, ss.at[b + k], rs.at[b + k], dev).wait_send()

    return _Task([buf], [SDS(buf.shape, buf.dtype)], [(0, 0)], N_DEV - 1, start, finish, peers=EVERYONE)


def _run_tasks(comm, which, cin, cout, ss, rs):
    i0 = o0 = s0 = 0
    for t in comm:
        getattr(t, which)(cin[i0:i0 + len(t.ins)], cout[o0:o0 + len(t.outs)], ss, rs, s0)
        i0, o0, s0 = i0 + len(t.ins), o0 + len(t.outs), s0 + t.n_sem


def _from_hbm(*arrays):
    return [pltpu.with_memory_space_constraint(a, pltpu.HBM) for a in arrays]


def _in_hbm(shapes):
    return [pltpu.HBM(s.shape, s.dtype) for s in shapes]


def _comm_layout(comm, n_in, n_out):
    c_in = [a for t in comm for a in t.ins]
    c_out = [s for t in comm for s in t.outs]
    aliases, i0, o0 = {}, 0, 0
    for t in comm:
        for i, o in t.alias:
            aliases[n_in + i0 + i] = n_out + o0 + o
        i0, o0 = i0 + len(t.ins), o0 + len(t.outs)
    return c_in, c_out, aliases, sum(t.n_sem for t in comm)


def _call(body, operands, *, name, grid, in_specs, out_specs, out_shape, scratch_shapes=(), sem=None, vmem_mib=None, comm=(),
          free=(), prefetch=()):
    operands = [o if s.memory_space == pltpu.SMEM or k in free else pltpu.with_memory_space_constraint(o, pltpu.HBM)
                for k, (o, s) in enumerate(zip(operands, in_specs))]
    n_pre, n_in, n_out, n_scr = len(prefetch), len(in_specs), len(out_specs), len(scratch_shapes)
    c_in, c_out, aliases, n_sem = _comm_layout(comm, n_pre + n_in, n_out)
    sems = [pltpu.SemaphoreType.DMA((n_sem,)), pltpu.SemaphoreType.DMA((n_sem,))] if comm else []

    def wrapped(*refs):
        pre, refs = refs[:n_pre], refs[n_pre:]
        ins, cin = refs[:n_in], refs[n_in:n_in + len(c_in)]
        rest = refs[n_in + len(c_in):]
        outs, cout = rest[:n_out], rest[n_out:n_out + len(c_out)]
        rest = rest[n_out + len(c_out):]
        scr, csem = rest[:n_scr], rest[n_scr:]
        if not comm:
            return body(*pre, *ins, *outs, *scr)
        step = functools.reduce(lambda acc, k: acc * grid[k] + pl.program_id(k), range(len(grid)), 0)
        n_steps = math.prod(grid)

        @pl.when(step == 0)
        def _():
            _enter(comm)
            _run_tasks(comm, "start", cin, cout, *csem)

        pl.when(step == n_steps // 2)(lambda: _run_tasks(comm, "middle", cin, cout, *csem))
        body(*pre, *ins, *outs, *scr)
        pl.when(step == n_steps - 1)(lambda: _run_tasks(comm, "finish", cin, cout, *csem))

    grid_spec = pltpu.PrefetchScalarGridSpec(
        num_scalar_prefetch=n_pre, grid=grid, in_specs=list(in_specs) + [ANY] * len(c_in),
        out_specs=list(out_specs) + [ANY] * len(c_out), scratch_shapes=list(scratch_shapes) + sems)
    return _pcall(
        wrapped, name=name, grid_spec=grid_spec, out_shape=_in_hbm(list(out_shape) + c_out), input_output_aliases=aliases,
        compiler_params=_params(("arbitrary",) * len(grid) if comm else sem, vmem_mib,
                                BARRIER_OF[_peers_of(comm)] if comm else None),
    )(*prefetch, *operands, *_from_hbm(*c_in))


def _comm_call(name, comm):
    c_in, c_out, aliases, n_sem = _comm_layout(comm, 0, 0)

    def body(*refs):
        cin, cout, (ss, rs) = refs[:len(c_in)], refs[len(c_in):len(c_in) + len(c_out)], refs[len(c_in) + len(c_out):]
        _enter(comm)
        for phase in ("start", "middle", "finish"):
            _run_tasks(comm, phase, cin, cout, ss, rs)

    return _pcall(
        body, name=name, in_specs=[ANY] * len(c_in), out_specs=[ANY] * len(c_out), out_shape=_in_hbm(c_out),
        scratch_shapes=[pltpu.SemaphoreType.DMA((n_sem,)), pltpu.SemaphoreType.DMA((n_sem,))],
        input_output_aliases=aliases, compiler_params=_params(collective_id=BARRIER_OF[_peers_of(comm)]),
    )(*_from_hbm(*c_in))


def _inproj(x, g1, w_int, comm=()):
    tm = TM

    def body(x_ref, g_ref, w_ref, proj_ref, u_ref):
        xf = x_ref[...]
        r = lax.rsqrt(jnp.mean(xf * xf, axis=-1, keepdims=True) + EPS)
        u = (xf * r * g_ref[...]).astype(BF)
        u_ref[...] = u
        proj_ref[...] = _dot(u, w_ref[...], 1, 1)

    return _call(
        body, (x, g1, w_int), name="inproj", grid=(T // tm,),
        in_specs=[pl.BlockSpec((tm, D), lambda i: (i, 0)), pl.BlockSpec((1, D), lambda i: (0, 0)),
                  _resident((INW, D))],
        out_specs=[pl.BlockSpec((tm, INW), lambda i: (i, 0)), pl.BlockSpec((tm, D), lambda i: (i, 0))],
        out_shape=[SDS((T, INW), F32), SDS((T, D), BF)], sem=("parallel",), vmem_mib=40, comm=comm, free=(0, 1))


def _outproj(y, w_out, x, g2):
    tm = TM

    def body(y_ref, w_ref, x_ref, g_ref, h1_ref, u2_ref):
        h1 = x_ref[...] + _dot(y_ref[...], w_ref[...], 1, 0)
        h1_ref[...] = h1
        r = lax.rsqrt(jnp.mean(h1 * h1, axis=-1, keepdims=True) + EPS)
        u2_ref[...] = (h1 * r * g_ref[...]).astype(BF)

    return _call(
        body, (y, w_out, x, g2), name="outproj", grid=(T // tm,),
        in_specs=[pl.BlockSpec((tm, D), lambda i: (i, 0)), _resident((D, D)),
                  pl.BlockSpec((tm, D), lambda i: (i, 0)), pl.BlockSpec((1, D), lambda i: (0, 0))],
        out_specs=[pl.BlockSpec((tm, D), lambda i: (i, 0)), pl.BlockSpec((tm, D), lambda i: (i, 0))],
        out_shape=[SDS((T, D), F32), SDS((T, D), BF)], sem=("parallel",), vmem_mib=32, free=(2, 3))


def _ffn_up(u2, w_upt, comm=()):
    tm, tn = 1024, 512

    def body(u_ref, w_ref, o_ref):
        o_ref[...] = _dot(u_ref[...], w_ref[...], 1, 1).astype(BF)

    return _call(
        body, (u2, w_upt), name="ffn_up", grid=(T // tm, 2 * DFF // tn),
        in_specs=[pl.BlockSpec((tm, D), lambda i, j: (i, 0)), pl.BlockSpec((tn, D), lambda i, j: (j, 0))],
        out_specs=[pl.BlockSpec((tm, tn), lambda i, j: (i, j))], out_shape=[SDS((T, 2 * DFF), BF)],
        sem=("parallel", "parallel"), vmem_mib=32, comm=comm, free=(1,))


def _ffn_down(a, w_down, h1, tgt):
    tm = TM

    def body(a_ref, w_ref, h1_ref, t_ref, dh_ref, dhb_ref, l_ref):
        @pl.when(pl.program_id(0) == 0)
        def _():
            l_ref[...] = jnp.zeros_like(l_ref)

        h2 = h1_ref[...] + _dot(a_ref[...], w_ref[...], 1, 0)
        e = h2 - t_ref[...]
        dh = e * (1.0 / D)
        dh_ref[...] = dh
        dhb_ref[...] = dh.astype(BF)
        e2 = jnp.sum((e * e).reshape(tm // 8, 8, D), axis=0)
        acc = e2[:, 0:128]
        for k in range(1, D // 128):
            acc = acc + e2[:, k * 128:(k + 1) * 128]
        l_ref[...] += acc

    return _call(
        body, (a, w_down, h1, tgt), name="ffn_down", grid=(T // tm,),
        in_specs=[pl.BlockSpec((tm, DFF), lambda i: (i, 0)), _resident((DFF, D)),
                  pl.BlockSpec((tm, D), lambda i: (i, 0)), pl.BlockSpec((tm, D), lambda i: (i, 0))],
        out_specs=[pl.BlockSpec((tm, D), lambda i: (i, 0)), pl.BlockSpec((tm, D), lambda i: (i, 0)),
                   pl.BlockSpec((8, 128), lambda i: (0, 0))],
        out_shape=[SDS((T, D), F32), SDS((T, D), BF), SDS((8, 128), F32)], sem=("arbitrary",), vmem_mib=40, free=(2, 3))


def _bucket_table():
    q = np.arange(BLK, dtype=np.int32)[:, None]
    j = np.arange(2 * BLK, dtype=np.int32)[None, :]
    n = np.maximum(q + BLK - j, 0)
    nf = np.maximum(n, 1).astype(np.float32)
    max_exact = NBUCKET // 2
    large = max_exact + (np.log(nf / np.float32(max_exact)) / np.float32(math.log(BLK / max_exact))
                         * np.float32(NBUCKET - max_exact)).astype(np.int32)
    large = np.minimum(large, NBUCKET - 1)
    return np.where(n < max_exact, n, large).astype(np.int32)


def _band_bias_bwd(dbias, bucket, me):
    def body(me_ref, db_ref, bk_ref, o_ref):
        bk = bk_ref[...]
        for b in range(NBUCKET):
            m = bk == b
            for h in range(NH):
                v = jnp.where(m, db_ref[h * BLK:(h + 1) * BLK, :], 0.0)
                s = jnp.sum(jnp.sum(v, axis=1, keepdims=True), axis=0, keepdims=True)
                o_ref[0, h:h + 1, b:b + 1] = s

    grid_spec = pltpu.PrefetchScalarGridSpec(
        num_scalar_prefetch=1, grid=(1,),
        in_specs=[pl.BlockSpec((NH * BLK, 2 * BLK), lambda i, me_ref: (0, 0)),
                  pl.BlockSpec((BLK, 2 * BLK), lambda i, me_ref: (0, 0))],
        out_specs=pl.BlockSpec((1, NH, NBUCKET), lambda i, me_ref: (me_ref[0], 0, 0)),
    )
    return _pcall(body, name="band_bias_bwd", grid_spec=grid_spec, out_shape=SDS((N_DEV, NH, NBUCKET), F32),
                  compiler_params=_params(("arbitrary",)))(me, dbias, bucket)


def _two_bf16(x):
    hi = x.astype(BF)
    return hi, (x - hi.astype(F32)).astype(BF)


def _head_sums(x, seg):
    hi, lo = _two_bf16(x)
    s = seg[0:x.shape[1], :]
    return _dot(hi, s, 1, 0) + _dot(lo, s, 1, 0)


def _head_spread(v, seg, width):
    hi, lo = _two_bf16(v)
    s = seg[0:width, :]
    return _dot(hi, s, 1, 1) + _dot(lo, s, 1, 1)


def _head_norm(x, g_t, seg, by_head=False):
    if by_head:
        heads = [x[:, h * HD:(h + 1) * HD] for h in range(x.shape[1] // HD)]
        r = jnp.concatenate([jnp.broadcast_to(lax.rsqrt(jnp.mean(v * v, axis=-1, keepdims=True) + EPS), v.shape)
                             for v in heads], axis=1)
    else:
        r = lax.rsqrt(_head_sums(x * x, seg) * (1.0 / HD) + EPS)
        r = _head_spread(r, seg, x.shape[1])
    return x * r * g_t, r


def _head_norm_bwd(dy, x, r, g_t, seg):
    dg_t = jnp.sum(dy * (x * r), axis=0, keepdims=True)
    dgx = dy * g_t
    mean = _head_spread(_head_sums(x * dgx, seg) * (1.0 / HD), seg, x.shape[1])
    return r * dgx - x * (r * r * r) * mean, dg_t


def _fold_heads(v):
    out = v[:, 0:HD]
    for h in range(1, v.shape[1] // HD):
        out = out + v[:, h * HD:(h + 1) * HD]
    return out


def _mix_forward(P, zc8, zh8, pkv, first, cw, qg_t, kg_t, gco, gao, seg, sink_ref, bias_ref, by_head=False):
    gate_b = P[:, 0:CW]
    gate_c = P[:, CW:2 * CW]
    hc = P[:, 2 * CW:3 * CW]
    z = gate_c * hc
    keep = jnp.where(first, 0.0, 1.0)
    zp = zc8 * zh8 * keep
    p1 = zp[7:8, :]
    p2 = zp[6:7, :]
    row = lax.broadcasted_iota(jnp.int32, (BLK, 1), 0)
    z1 = jnp.where(row == 0, p1, pltpu.roll(z, 1, 0))
    z2 = jnp.where(row == 0, p2, jnp.where(row == 1, p1, pltpu.roll(z, 2, 0)))
    cz = cw[0:1, :] * z2 + cw[1:2, :] * z1 + cw[2:3, :] * z
    y_conv = gate_b * cz

    scale = HD ** -0.5
    qi = lax.broadcasted_iota(jnp.int32, (BLK, 2 * BLK), 0)
    kj = lax.broadcasted_iota(jnp.int32, (BLK, 2 * BLK), 1)
    dd = qi + BLK - kj
    first_key = jnp.where(first, BLK, 0)
    valid = (dd >= 0) & (dd < BLK) & (kj >= first_key)

    q0 = 3 * CW
    k0 = q0 + AW
    v0 = k0 + NKV * HD
    q_raw = P[:, q0:k0]
    qn, rq = _head_norm(q_raw, qg_t, seg, by_head)
    qs = (qn * scale).astype(BF)
    k_raw = jnp.concatenate([pkv[:, 0:NKV * HD], P[:, k0:v0]], axis=0)
    kn, rk = _head_norm(k_raw, kg_t, seg, by_head)
    knb = kn.astype(BF)
    heads = []
    for h in range(NH):
        kv = h // GQ
        kb = knb[:, kv * HD:(kv + 1) * HD]
        vb = jnp.concatenate([pkv[:, NKV * HD + kv * HD:NKV * HD + (kv + 1) * HD],
                              P[:, v0 + kv * HD:v0 + (kv + 1) * HD]], axis=0).astype(BF)
        Q = qs[:, h * HD:(h + 1) * HD]
        S = _dot(Q, kb, 1, 1) + bias_ref[h * BLK:(h + 1) * BLK, :]
        S = jnp.where(valid, S, NEG_INF)
        sink = sink_ref[0, h]
        m = jnp.maximum(jnp.max(S, axis=-1, keepdims=True), sink)
        p = jnp.exp(S - m)
        es = jnp.exp(sink - m)
        denom = jnp.sum(p, axis=-1, keepdims=True) + es
        probs = p / denom
        O = _dot(probs.astype(BF), vb, 1, 0)
        heads.append(dict(kb=kb, vb=vb, Q=Q, probs=probs, psink=es / denom, O=O))
    y_attn = jnp.concatenate([hd["O"] for hd in heads], axis=1)

    rc = lax.rsqrt(jnp.mean(y_conv * y_conv, axis=-1, keepdims=True) + EPS)
    ra = lax.rsqrt(jnp.mean(y_attn * y_attn, axis=-1, keepdims=True) + EPS)
    y = jnp.concatenate([y_conv * rc * gco, y_attn * ra * gao], axis=1)
    return dict(gate_b=gate_b, gate_c=gate_c, hc=hc, z=z, z1=z1, z2=z2, cz=cz, y_conv=y_conv, y_attn=y_attn,
                rc=rc, ra=ra, heads=heads, y=y, row=row, scale=scale, q_raw=q_raw, rq=rq, k_raw=k_raw, rk=rk)


BPS = 2
TILE = BPS * BLK
KV0 = 3 * CW + AW


def _mix_in_specs(tile_of):
    return [
        pl.BlockSpec(memory_space=pltpu.SMEM),
        pl.BlockSpec((TILE, INW), lambda s: (tile_of(s), 0)),
        pl.BlockSpec((8, CW), lambda s: (jnp.maximum(tile_of(s) * (TILE // 8) - 1, 0), 1)),
        pl.BlockSpec((8, CW), lambda s: (jnp.maximum(tile_of(s) * (TILE // 8) - 1, 0), 2)),
        pl.BlockSpec((BLK, 2 * NKV * HD), lambda s: (jnp.maximum(tile_of(s) * BPS - 1, 0), KV0 // (2 * NKV * HD))),
    ]


def _block_inputs(tile, b, zc_ref, zh_ref, pkv_ref, first_tile):
    P = tile[b * BLK:(b + 1) * BLK, :]
    if b == 0:
        return P, zc_ref[...], zh_ref[...], pkv_ref[...], first_tile
    lo = b * BLK
    return P, tile[lo - 8:lo, CW:2 * CW], tile[lo - 8:lo, 2 * CW:3 * CW], tile[lo - BLK:lo, KV0:KV0 + 2 * NKV * HD], False


def _mix_param_specs():
    return [
        pl.BlockSpec((8, CW), lambda s: (0, 0)),
        pl.BlockSpec((1, AW), lambda s: (0, 0)),
        pl.BlockSpec((1, NKV * HD), lambda s: (0, 0)),
        pl.BlockSpec((1, CW), lambda s: (0, 0)),
        pl.BlockSpec((1, AW), lambda s: (0, 0)),
        pl.BlockSpec((AW, 128), lambda s: (0, 0)),
        pl.BlockSpec((NH * BLK, 2 * BLK), lambda s: (0, 0)),
    ]


def _mix_params(cw8, qg, kg, gco, gao, bias):
    seg = np.zeros((AW, 128), np.float32)
    seg[np.arange(AW), np.arange(AW) // HD] = 1.0
    return (cw8, jnp.tile(qg, (1, NH)), jnp.tile(kg, (1, NKV)), gco, gao, jnp.asarray(seg, BF), bias)


def _mix_fwd(proj, sinks, cw8, qg, kg, gco, gao, bias, comm=()):
    def body(sink_ref, p_ref, zc_ref, zh_ref, pkv_ref, cw_ref, qg_ref, kg_ref, gco_ref, gao_ref, seg_ref, bias_ref, y_ref):
        tile = p_ref[...]
        for b in range(BPS):
            f = _mix_forward(*_block_inputs(tile, b, zc_ref, zh_ref, pkv_ref, pl.program_id(0) == 0), cw_ref[...],
                             qg_ref[...], kg_ref[...], gco_ref[...], gao_ref[...], seg_ref[...], sink_ref, bias_ref, by_head=True)
            y_ref[b * BLK:(b + 1) * BLK, :] = f["y"].astype(BF)

    return _call(
        body, (sinks, proj, proj, proj, proj, *_mix_params(cw8, qg, kg, gco, gao, bias)), name="mix_fwd", grid=(T // TILE,),
        in_specs=_mix_in_specs(lambda s: s) + _mix_param_specs(),
        out_specs=[pl.BlockSpec((TILE, D), lambda s: (s, 0))], out_shape=[SDS((T, D), BF)],
        sem=("parallel",), vmem_mib=40, comm=comm, free=tuple(range(5, 12)))


def _mix_bwd(proj, dy, sinks, cw8, qg, kg, gco, gao, bias, comm=()):
    n_steps = T // TILE

    def tile_of(s):
        return n_steps - 1 - s

    def body(sink_ref, p_ref, zc_ref, zh_ref, pkv_ref, dy_ref, cw_ref, qg_ref, kg_ref, gco_ref, gao_ref, seg_ref, bias_ref,
             dproj_ref, dcw_ref, dqg_ref, dkg_ref, dgco_ref, dgao_ref, dsink_ref, dbias_ref,
             ndcz_ref, dkc_ref, dvc_ref):
        s = pl.program_id(0)

        @pl.when(s == 0)
        def _():
            for r in (dcw_ref, dqg_ref, dkg_ref, dgco_ref, dgao_ref, dsink_ref, dbias_ref, ndcz_ref, dkc_ref, dvc_ref):
                r[...] = jnp.zeros_like(r)

        params = (cw_ref[...], qg_ref[...], kg_ref[...], gco_ref[...], gao_ref[...], seg_ref[...])
        tile = p_ref[...]
        carry = (ndcz_ref[...], dkc_ref[...], dvc_ref[...])
        total = None
        for b in reversed(range(BPS)):
            f = _mix_forward(*_block_inputs(tile, b, zc_ref, zh_ref, pkv_ref, s == n_steps - 1), *params, sink_ref, bias_ref)
            pieces, sums, carry = one_block(f, dy_ref[b * BLK:(b + 1) * BLK, :], params, carry)
            for lo, piece in pieces:
                dproj_ref[b * BLK:(b + 1) * BLK, lo:lo + piece.shape[1]] = piece
            total = sums if total is None else [t + v for t, v in zip(total, sums)]
        ndcz_ref[...], dkc_ref[...], dvc_ref[...] = carry
        dcw, dqg_t, dkg_t, dgco, dgao, dsink, *ds = total
        dcw_ref[0:3, :] += dcw
        dqg_ref[...] += _fold_heads(dqg_t)
        dkg_ref[...] += _fold_heads(dkg_t)
        dgco_ref[...] += dgco
        dgao_ref[...] += dgao
        dsink_ref[...] += dsink
        for h in range(NH):
            dbias_ref[h * BLK:(h + 1) * BLK, :] += ds[h]

    def one_block(f, dy, params, carry):
        cw, qg_v, kg_v, gco_v, gao_v, seg = params
        nxt, dk_carry, dv_carry = carry
        dyc, dgco = _rms_bwd(dy[:, 0:CW], f["y_conv"], f["rc"], gco_v)
        dya, dgao = _rms_bwd(dy[:, CW:CW + AW], f["y_attn"], f["ra"], gao_v)

        row = f["row"]
        dgate_b = dyc * f["cz"]
        dcz = dyc * f["gate_b"]
        dcw = jnp.concatenate([jnp.sum(dcz * f[k], axis=0, keepdims=True) for k in ("z2", "z1", "z")], axis=0)
        n0 = nxt[0:1, :]
        n1 = nxt[1:2, :]
        d1 = jnp.where(row == BLK - 1, n0, pltpu.roll(dcz, BLK - 1, 0))
        d2 = jnp.where(row == BLK - 1, n1, jnp.where(row == BLK - 2, n0, pltpu.roll(dcz, BLK - 2, 0)))
        dz = cw[2:3, :] * dcz + cw[1:2, :] * d1 + cw[0:1, :] * d2
        pieces = [(0, dgate_b.astype(BF)), (CW, (dz * f["hc"]).astype(BF)), (2 * CW, (dz * f["gate_c"]).astype(BF))]

        scale = f["scale"]
        lane = lax.broadcasted_iota(jnp.int32, (1, 128), 1)
        dsink = jnp.zeros((1, 128), F32)
        dq_cols, dk_cols, dv_cols, dk_prev, dv_prev, ds = [], [], [], [], [], []
        for kv in range(NKV):
            dKb = dVb = 0.0
            for h in range(kv * GQ, (kv + 1) * GQ):
                hd = f["heads"][h]
                dO = dya[:, h * HD:(h + 1) * HD]
                delta = jnp.sum(dO * hd["O"], axis=-1, keepdims=True)
                dOb = dO.astype(BF)
                dP = _dot(dOb, hd["vb"], 1, 1)
                dS = hd["probs"] * (dP - delta)
                tot = jnp.sum(hd["psink"] * delta, axis=0, keepdims=True)
                dsink = dsink - jnp.where(lane == h, tot, 0.0)
                ds.append(dS)
                dSb = dS.astype(BF)
                dq_cols.append(_dot(dSb, hd["kb"], 1, 0))
                dKb = dKb + _dot(dSb, hd["Q"], 0, 0)
                dVb = dVb + _dot(hd["probs"].astype(BF), dOb, 0, 0)
            dk_cols.append(dKb[BLK:, :] + dk_carry[:, kv * HD:(kv + 1) * HD])
            dv_cols.append(dVb[BLK:, :] + dv_carry[:, kv * HD:(kv + 1) * HD])
            dk_prev.append(dKb[:BLK, :])
            dv_prev.append(dVb[:BLK, :])
        dq_raw, dqg_t = _head_norm_bwd(jnp.concatenate(dq_cols, axis=1) * scale, f["q_raw"], f["rq"], qg_v, seg)
        dk_raw, dkg_t = _head_norm_bwd(jnp.concatenate(dk_cols, axis=1), f["k_raw"][BLK:, :], f["rk"][BLK:, :], kg_v, seg)
        pieces.append((3 * CW, jnp.concatenate([dq_raw, dk_raw] + dv_cols, axis=1).astype(BF)))
        owed = (dcz[0:8, :], jnp.concatenate(dk_prev, axis=1), jnp.concatenate(dv_prev, axis=1))
        return pieces, [dcw, dqg_t, dkg_t, dgco, dgao, dsink, *ds], owed

    small = lambda r, c: pl.BlockSpec((r, c), lambda s: (0, 0))
    return _call(
        body, (sinks, proj, proj, proj, proj, dy, *_mix_params(cw8, qg, kg, gco, gao, bias)), name="mix_bwd", grid=(n_steps,),
        in_specs=_mix_in_specs(tile_of) + [pl.BlockSpec((TILE, D), lambda s: (tile_of(s), 0))] + _mix_param_specs(),
        out_specs=[pl.BlockSpec((TILE, INW), lambda s: (tile_of(s), 0)), small(8, CW), small(1, HD), small(1, HD),
                   small(1, CW), small(1, AW), small(1, 128), small(NH * BLK, 2 * BLK)],
        out_shape=[SDS((T, INW), BF), SDS((8, CW), F32), SDS((1, HD), F32), SDS((1, HD), F32), SDS((1, CW), F32),
                   SDS((1, AW), F32), SDS((1, 128), F32), SDS((NH * BLK, 2 * BLK), F32)],
        scratch_shapes=[pltpu.VMEM((8, CW), F32), pltpu.VMEM((BLK, NKV * HD), F32), pltpu.VMEM((BLK, NKV * HD), F32)],
        sem=("arbitrary",), vmem_mib=56, comm=comm, free=(1, 2, 3, 4) + tuple(range(6, 13)))


FT = 256
NFT = DFF // FT
RC = 128
NCH = T // RC
LEAD = 16


def _rows8(x):
    return jnp.sum(x.reshape(x.shape[0] // 8, 8, x.shape[1]), axis=0)


def _ffn_act_specs():
    return [
        pl.BlockSpec((T, FT), lambda j: (0, j)), pl.BlockSpec((T, FT), lambda j: (0, NFT + j)),
        pl.BlockSpec((8, FT), lambda j: (0, j)), pl.BlockSpec((8, FT), lambda j: (0, NFT + j)),
        pl.BlockSpec((1, FT), lambda j: (0, j)), pl.BlockSpec((1, FT), lambda j: (0, NFT + j)),
    ]


def _conv_rows(win, w, b, n):
    win = win.astype(F32)
    u = win[LEAD:LEAD + n]
    u1 = pltpu.roll(win, 1, 0)[LEAD:LEAD + n]
    u2 = pltpu.roll(win, 2, 0)[LEAD:LEAD + n]
    return u2, u1, u, w[0:1, :] * u2 + w[1:2, :] * u1 + w[2:3, :] * u + b


def _ffn_act(up, fw8, fb, comm=()):
    def body(ug_ref, uv_ref, wg_ref, wv_ref, bg_ref, bv_ref, a_ref):
        wg, wv, bg, bv = wg_ref[...], wv_ref[...], bg_ref[...], bv_ref[...]

        def chunk(win_g, win_v):
            gp = _conv_rows(win_g, wg, bg, RC)[3]
            vp = _conv_rows(win_v, wv, bv, RC)[3]
            return (gp * jax.nn.sigmoid(gp) * vp).astype(BF)

        zero = jnp.zeros((LEAD, FT), BF)
        a_ref[0:RC, :] = chunk(jnp.concatenate([zero, ug_ref[0:RC, :]], axis=0),
                               jnp.concatenate([zero, uv_ref[0:RC, :]], axis=0))

        def step(i, carry):
            r0 = pl.multiple_of(i * RC, RC)
            win = pl.ds(r0 - LEAD, RC + LEAD)
            a_ref[pl.ds(r0, RC), :] = chunk(ug_ref[win, :], uv_ref[win, :])
            return carry

        lax.fori_loop(1, NCH, step, 0)

    return _call(
        body, (up, up, fw8, fw8, fb, fb), name="ffn_act", grid=(NFT,), in_specs=_ffn_act_specs(),
        out_specs=[pl.BlockSpec((T, FT), lambda j: (0, j))], out_shape=[SDS((T, DFF), BF)],
        sem=("parallel",), vmem_mib=40, comm=comm, free=(2, 3, 4, 5))


def _ffn_act_bwd(up, da, fw8, fb, comm=()):
    ext = RC + LEAD

    def body(ug_ref, uv_ref, wg_ref, wv_ref, bg_ref, bv_ref, da_ref,
             dug_ref, duv_ref, dwg_ref, dwv_ref, dbg_ref, dbv_ref):
        wg, wv, bg, bv = wg_ref[...], wv_ref[...], bg_ref[...], bv_ref[...]

        def chunk(win_g, win_v, da_e):
            g2, g1, g0, gp = _conv_rows(win_g, wg, bg, ext)
            v2, v1, v0, vp = _conv_rows(win_v, wv, bv, ext)
            da_e = da_e.astype(F32)
            sig = jax.nn.sigmoid(gp)
            dvp = da_e * (gp * sig)
            dgp = da_e * vp * (sig * (1.0 + gp * (1.0 - sig)))

            def back(dp, w):
                return (w[2:3, :] * dp[0:RC] + w[1:2, :] * pltpu.roll(dp, ext - 1, 0)[0:RC]
                        + w[0:1, :] * pltpu.roll(dp, ext - 2, 0)[0:RC]).astype(BF)

            def sums(dp, u2, u1, u0):
                d = dp[0:RC]
                return [_rows8(d), _rows8(d * u2[0:RC]), _rows8(d * u1[0:RC]), _rows8(d * u0[0:RC])]

            return back(dgp, wg), back(dvp, wv), sums(dgp, g2, g1, g0) + sums(dvp, v2, v1, v0)

        zero = jnp.zeros((LEAD, FT), BF)
        dug, duv, acc = chunk(jnp.concatenate([zero, ug_ref[0:ext, :]], axis=0),
                              jnp.concatenate([zero, uv_ref[0:ext, :]], axis=0), da_ref[0:ext, :])
        dug_ref[0:RC, :] = dug
        duv_ref[0:RC, :] = duv

        def step(i, acc):
            r0 = pl.multiple_of(i * RC, RC)
            win = pl.ds(r0 - LEAD, ext + LEAD)
            dug, duv, part = chunk(ug_ref[win, :], uv_ref[win, :], da_ref[pl.ds(r0, ext), :])
            dug_ref[pl.ds(r0, RC), :] = dug
            duv_ref[pl.ds(r0, RC), :] = duv
            return [a + p for a, p in zip(acc, part)]

        acc = lax.fori_loop(1, NCH - 1, step, acc)
        r0 = T - RC
        tail = lambda ref, lo: jnp.concatenate([ref[lo:T, :], zero], axis=0)
        dug, duv, part = chunk(tail(ug_ref, r0 - LEAD), tail(uv_ref, r0 - LEAD), tail(da_ref, r0))
        dug_ref[r0:T, :] = dug
        duv_ref[r0:T, :] = duv
        tot = [jnp.sum(a + p, axis=0, keepdims=True) for a, p in zip(acc, part)]
        for k, (dw_ref, db_ref) in enumerate(((dwg_ref, dbg_ref), (dwv_ref, dbv_ref))):
            db_ref[...] = tot[4 * k]
            dw_ref[...] = jnp.zeros_like(dw_ref)
            for r in range(3):
                dw_ref[r:r + 1, :] = tot[4 * k + 1 + r]

    col = lambda r: pl.BlockSpec((r, FT), lambda j: (0, j))
    return _call(
        body, (up, up, fw8, fw8, fb, fb, da), name="ffn_act_bwd", grid=(NFT,),
        in_specs=_ffn_act_specs() + [pl.BlockSpec((T, FT), lambda j: (0, j))],
        out_specs=[col(T), col(T), col(8), col(8), col(1), col(1)],
        out_shape=[SDS((T, DFF), BF), SDS((T, DFF), BF), SDS((8, DFF), F32), SDS((8, DFF), F32),
                   SDS((1, DFF), F32), SDS((1, DFF), F32)],
        sem=("parallel",), vmem_mib=40, comm=comm, free=(0, 1, 2, 3, 4, 5))


def _ffn_down_bwd(dh2b, w_down, comm=()):
    tm = TM

    def body(d_ref, w_ref, o_ref):
        o_ref[...] = _dot(d_ref[...], w_ref[...], 1, 1).astype(BF)

    return _call(
        body, (dh2b, w_down), name="ffn_down_bwd", grid=(T // tm,),
        in_specs=[pl.BlockSpec((tm, D), lambda i: (i, 0)), _resident((DFF, D))],
        out_specs=[pl.BlockSpec((tm, DFF), lambda i: (i, 0))], out_shape=[SDS((T, DFF), BF)],
        sem=("parallel",), vmem_mib=40, comm=comm, free=(0, 1))


def _norm_matmul_bwd(name, a_list, w_t, k_offsets, xin, g, dres, want_bf16, comm=(), slot=None):
    tm = TM
    ks = [a.shape[1] for a in a_list]
    n_a = len(a_list)
    n_pre = 0 if slot is None else 1

    def body(*refs):
        refs = refs[n_pre:]
        a_refs = refs[:n_a]
        w_ref, x_ref, g_ref, r_ref = refs[n_a:n_a + 4]
        outs = refs[n_a + 4:]
        dx_ref, dg_ref = outs[0], (outs[-1] if slot is None else outs[-1].at[0])

        @pl.when(pl.program_id(0) == 0)
        def _():
            dg_ref[...] = jnp.zeros_like(dg_ref)

        du = _dot(a_refs[0][...], w_ref[k_offsets[0]:k_offsets[0] + ks[0], :], 1, 0)
        for k in range(1, n_a):
            du = du + _dot(a_refs[k][...], w_ref[k_offsets[k]:k_offsets[k] + ks[k], :], 1, 0)
        x = x_ref[...]
        r = lax.rsqrt(jnp.mean(x * x, axis=-1, keepdims=True) + EPS)
        dx, dg = _rms_bwd(du, x, r, g_ref[...])
        dx = r_ref[...] + dx
        dx_ref[...] = dx
        if want_bf16:
            outs[1][...] = dx.astype(BF)
        dg_ref[...] += dg

    tile = lambda c: pl.BlockSpec((tm, c), lambda i, *_: (i, 0))
    if slot is None:
        dg_spec, dg_shape = pl.BlockSpec((1, D), lambda i: (0, 0)), SDS((1, D), F32)
    else:
        dg_spec, dg_shape = pl.BlockSpec((1, 1, D), lambda i, slot_ref: (slot_ref[0], 0, 0)), SDS((N_DEV, 1, D), F32)
    out_specs = [tile(D)] + ([tile(D)] if want_bf16 else []) + [dg_spec]
    out_shape = [SDS((T, D), F32)] + ([SDS((T, D), BF)] if want_bf16 else []) + [dg_shape]
    return _call(
        body, (*a_list, w_t, xin, g, dres), name=name, grid=(T // tm,), prefetch=() if slot is None else (slot,),
        in_specs=[tile(k) for k in ks] + [_resident(w_t.shape), tile(D),
                                           pl.BlockSpec((1, D), lambda i, *_: (0, 0)), tile(D)],
        out_specs=out_specs, out_shape=out_shape, sem=("arbitrary",), vmem_mib=56, comm=comm, free=tuple(range(n_a + 4)))


def _out_bwd(dh1b, w_out, comm=()):
    tm = TM

    def body(d_ref, w_ref, o_ref):
        o_ref[...] = _dot(d_ref[...], w_ref[...], 1, 1)

    return _call(
        body, (dh1b, w_out), name="out_bwd", grid=(T // tm,),
        in_specs=[pl.BlockSpec((tm, D), lambda i: (i, 0)), _resident((D, D))],
        out_specs=[pl.BlockSpec((tm, D), lambda i: (i, 0))], out_shape=[SDS((T, D), F32)],
        sem=("parallel",), vmem_mib=32, comm=comm, free=(0, 1))


def _wgrad(name, a_list, b, old_a, comm=()):
    m_k = a_list[0].shape[1]
    tm = max(t for t in range(128, m_k // 2 + 1, 128) if m_k % t == 0)
    steps = [a.shape[1] // tm for a in a_list]
    starts = [sum(steps[:k]) for k in range(len(a_list))]
    n_a = len(a_list)

    def body(*refs):
        a_refs, b_ref, o_ref = refs[:n_a], refs[n_a], refs[n_a + 1]
        i = pl.program_id(0)
        for k in range(n_a):
            @pl.when((i >= starts[k]) & (i < starts[k] + steps[k]))
            def _(k=k):
                o_ref[...] = _dot(a_refs[k][...], b_ref[...], 0, 0).astype(BF)

    def a_spec(k):
        return pl.BlockSpec((T, tm), lambda i: (0, jnp.clip(i - starts[k], 0, steps[k] - 1)))

    m_total = tm * sum(steps)
    return _call(
        body, (*a_list, b), name=name, grid=(sum(steps),),
        in_specs=[a_spec(k) for k in range(n_a)] + [_resident((T, D))],
        out_specs=[pl.BlockSpec((tm, D), lambda i: (i, 0))], out_shape=[SDS((m_total, D), BF)],
        sem=("parallel",), vmem_mib=40, comm=comm, free=() if old_a is None else tuple(range(n_a)) if old_a else (n_a,))


def _chip_sum(name, gbf, from_sib, core, chip):
    h = gbf.shape[1]
    th = h

    def body(core_ref, chip_ref, g_ref, s_ref, pbf_ref, own_ref):
        p = g_ref[0].astype(F32) + s_ref[0].astype(F32)
        pbf_ref[0] = p.astype(BF)

        @pl.when(pl.program_id(1) == chip_ref[0])
        def _():
            own_ref[...] = p

    grid_spec = pltpu.PrefetchScalarGridSpec(
        num_scalar_prefetch=2, grid=(h // th, N_CHIPS),
        in_specs=[pl.BlockSpec((1, th, D), lambda t, jj, core_ref, chip_ref: (2 * jj + core_ref[0], t, 0)),
                  pl.BlockSpec((1, th, D), lambda t, jj, core_ref, chip_ref: (jj, t, 0))],
        out_specs=[pl.BlockSpec((1, th, D), lambda t, jj, core_ref, chip_ref: (jj, t, 0)),
                   pl.BlockSpec((th, D), lambda t, jj, core_ref, chip_ref: (t, 0))],
    )
    return _pcall(
        body, name=name, grid_spec=grid_spec, out_shape=_in_hbm([SDS((N_CHIPS, h, D), BF), SDS((h, D), F32)]),
        compiler_params=_params(("arbitrary", "arbitrary"), 32),
    )(core, chip, *_from_hbm(gbf, from_sib))


def _final_sum(name, own, from_chips, core, comm=()):
    h = own.shape[0]

    def body(core_ref, o_ref, r_ref, f_ref):
        f_ref[0] = ((o_ref[...] + r_ref[0].astype(F32)) + r_ref[1].astype(F32)) + r_ref[2].astype(F32)

    return _call(
        body, (own, from_chips), name=name, grid=(1,), prefetch=(core,),
        in_specs=[pl.BlockSpec((h, D), lambda i, core_ref: (0, 0)), pl.BlockSpec((3, h, D), lambda i, core_ref: (0, 0, 0))],
        out_specs=[pl.BlockSpec((1, h, D), lambda i, core_ref: (core_ref[0], 0, 0))], out_shape=[SDS((2, h, D), F32)],
        sem=("arbitrary",), vmem_mib=40, comm=comm)


def _adam_math(w, g, m, v):
    nm = ADAM_B1 * m + (1.0 - ADAM_B1) * g
    nv = ADAM_B2 * v + (1.0 - ADAM_B2) * (g * g)
    m_hat = nm / (1.0 - ADAM_B1 ** ADAM_STEP)
    v_hat = nv / (1.0 - ADAM_B2 ** ADAM_STEP)
    return -ADAM_LR * (m_hat / (jnp.sqrt(v_hat) + ADAM_EPS) + ADAM_WD * w), nm, nv


def _adamw(name, w, g, m, v, tr, copy_g=False, stage=True, g_transposed=False):
    rows, cols = w.shape

    def body(w_ref, g_ref, m_ref, v_ref, *outs):
        d_ref, nm_ref, nv_ref = outs[-3:]
        for c in [pl.ds(c0, 128) for c0 in range(0, cols, 128)] if g_transposed else [slice(None)]:
            g_val = g_ref[c, :].T if g_transposed else g_ref[...]
            if copy_g:
                outs[0][:, c] = g_val
            d_ref[:, c], nm_ref[:, c], nv_ref[:, c] = _adam_math(w_ref[:, c], g_val, m_ref[:, c], v_ref[:, c])

    spec = pl.BlockSpec((tr, cols), lambda i: (i, 0))
    n_out = 4 if copy_g else 3
    g_spec = pl.BlockSpec((cols, tr), lambda i: (0, i)) if g_transposed else spec
    return _call(body, (w, g, m, v), name=name, grid=(rows // tr,), in_specs=[spec, g_spec, spec, spec], out_specs=[spec] * n_out,
                 out_shape=[SDS((rows, cols), F32)] * n_out, sem=("parallel",), vmem_mib=32,
                 free=(0, 2, 3) if stage else ())


C_G1, C_G2, C_GCO, C_GAO, C_DCW, C_DQG, C_DKG, C_SINK, C_SQ = 0, 1024, 2048, 2560, 3072, 4608, 4736, 4864, 5632
P_W = C_SQ + 128


def _pack_small(me, dfwg, dfwv, dfbg, dfbv, dg2, dgco, dgao, dcw8, dqg, dkg, dsink, sq):
    def body(me_ref, dfwg_r, dfwv_r, dfbg_r, dfbv_r, dg2_r, dgco_r, dgao_r, dcw_r, dqg_r, dkg_r, dsink_r, sq_r, o):
        o[...] = jnp.zeros_like(o)
        o[0, :, 0:DFF] = dfwg_r[...]
        o[0, :, DFF:2 * DFF] = dfwv_r[...]
        o[0, 3:4, 0:DFF] = dfbg_r[...]
        o[0, 3:4, DFF:2 * DFF] = dfbv_r[...]
        o[0, 4:5, C_G2:C_G2 + D] = dg2_r[...]
        o[0, 4:5, C_GCO:C_GCO + CW] = dgco_r[...]
        o[0, 4:5, C_GAO:C_GAO + AW] = dgao_r[...]
        for r in range(3):
            o[0, 4:5, C_DCW + r * CW:C_DCW + (r + 1) * CW] = dcw_r[r:r + 1, :]
        o[0, 4:5, C_DQG:C_DQG + HD] = dqg_r[...]
        o[0, 4:5, C_DKG:C_DKG + HD] = dkg_r[...]
        o[0, 4:5, C_SINK:C_SINK + 128] = dsink_r[...]
        o[0, :, C_SQ:C_SQ + 128] = sq_r[...]

    ins = (dfwg, dfwv, dfbg, dfbv, dg2, dgco, dgao, dcw8, dqg, dkg, dsink, sq)
    return _call(body, ins, name="pack_small", grid=(1,), prefetch=(me,),
                 in_specs=[pl.BlockSpec(a.shape, lambda i, me_ref: (0, 0)) for a in ins],
                 out_specs=[pl.BlockSpec((1, 8, P_W), lambda i, me_ref: (me_ref[0], 0, 0))],
                 out_shape=[SDS((N_DEV, 8, P_W), F32)], sem=("arbitrary",))[0]


N_SMALL = 11


def _small_adam(chip, p_all, g1_all, tbl_all, ws, ms, vs):
    fw_cols = 2 * DFF // N_CHIPS
    cw_cols = CW // N_CHIPS

    def body(chip_ref, p_ref, fw_ref, cw0_ref, cw1_ref, cw2_ref, g1_ref, tbl_ref, *refs):
        w_r, m_r, v_r = refs[0:N_SMALL], refs[N_SMALL:2 * N_SMALL], refs[2 * N_SMALL:3 * N_SMALL]
        outs = refs[3 * N_SMALL:]
        g_o, d_o, nm_o, nv_o = (outs[k * N_SMALL:(k + 1) * N_SMALL] for k in range(4))
        loss_o = outs[4 * N_SMALL]

        def total(ref):
            s = ref[0]
            for k in range(1, N_DEV):
                s = s + ref[k]
            return s

        S = total(p_ref)
        fw = total(fw_ref)
        cws = [total(r) for r in (cw0_ref, cw1_ref, cw2_ref)]

        def step(i, g, at):
            d, nm, nv = _adam_math(w_r[i][at], g, m_r[i][at], v_r[i][at])
            g_o[i][at], d_o[i][at], nm_o[i][at], nv_o[i][at] = g, d, nm, nv

        everything = (slice(None), slice(None))
        step(0, total(g1_ref), everything)
        for r in range(3):
            step(1, cws[r][4:5, :], (r, slice(None), slice(None)))
        step(2, S[4:5, C_DQG:C_DQG + HD], everything)
        step(3, S[4:5, C_DKG:C_DKG + HD], everything)
        step(4, total(tbl_ref), everything)
        step(5, S[4:5, C_SINK:C_SINK + NH], everything)
        step(6, S[4:5, C_GCO:C_GCO + CW], everything)
        step(7, S[4:5, C_GAO:C_GAO + AW], everything)
        step(8, S[4:5, C_G2:C_G2 + D], everything)
        for r in range(3):
            step(9, fw[r:r + 1, :], (r, slice(None), slice(None)))
        step(10, S[3:4, 0:2 * DFF], everything)
        sq = S[:, C_SQ:C_SQ + 128]
        loss_o[...] = jnp.sum(jnp.sum(sq, axis=1, keepdims=True), axis=0, keepdims=True) * (0.5 / D)

    def full(a):
        n = len(a.shape)
        return pl.BlockSpec(a.shape, lambda i, chip_ref: (0,) * n)

    params = [*ws, *ms, *vs]
    out = _call(
        body, (p_all, p_all, p_all, p_all, p_all, g1_all, tbl_all, *params), name="small_adam", grid=(1,), prefetch=(chip,),
        in_specs=[full(p_all),
                  pl.BlockSpec((N_DEV, 8, fw_cols), lambda i, chip_ref: (0, 0, chip_ref[0])),
                  *[pl.BlockSpec((N_DEV, 8, cw_cols), lambda i, chip_ref, r=r: (0, 0, (C_DCW + r * CW) // cw_cols + chip_ref[0]))
                    for r in range(3)],
                  full(g1_all), full(tbl_all), *[full(a) for a in params]],
        out_specs=[full(a) for a in ws] * 4 + [pl.BlockSpec((1, 1), lambda i, chip_ref: (0, 0))],
        out_shape=[SDS(a.shape, F32) for a in ws] * 4 + [SDS((1, 1), F32)], sem=("arbitrary",), vmem_mib=32)
    return out[0:N_SMALL], out[N_SMALL:2 * N_SMALL], out[2 * N_SMALL:3 * N_SMALL], out[3 * N_SMALL:4 * N_SMALL], out[4 * N_SMALL]


PLACE_STEPS = 4


def _place_specs(shards):
    rows = [s.shape[0] // PLACE_STEPS for s in shards]
    return ([pl.BlockSpec((r, D), lambda i, chip_ref: (i, 0)) for r in rows],
            [pl.BlockSpec((r, D), lambda i, chip_ref: (chip_ref[0] * PLACE_STEPS + i, 0)) for r in rows],
            [SDS((N_CHIPS * s.shape[0], D), BF) for s in shards])


def _place_first(chip, shard, conv_w, ffn_conv_w):
    def body(chip_ref, a, s0, s1, o, t0, t1):
        o[...] = a[...].astype(BF)

        @pl.when(pl.program_id(0) == 0)
        def _():
            for s, t in ((s0, t0), (s1, t1)):
                t[...] = jnp.zeros_like(t)
                t[0, 0:3, :] = s[...]

    ins, outs, shapes = _place_specs([shard])
    taps = (conv_w, ffn_conv_w)
    return _call(
        body, (shard, conv_w, ffn_conv_w), name="place_first", grid=(PLACE_STEPS,), prefetch=(chip,),
        in_specs=ins + [pl.BlockSpec(s.shape, lambda i, chip_ref: (0, 0)) for s in taps],
        out_specs=outs + [pl.BlockSpec((1, 8, s.shape[1]), lambda i, chip_ref: (chip_ref[0], 0, 0)) for s in taps],
        out_shape=shapes + [SDS((N_CHIPS, 8, s.shape[1]), F32) for s in taps],
        sem=("arbitrary",), vmem_mib=32, free=(1, 2))


def _place_rest(chip, shards, w_up, table, bucket, comm):
    n = len(shards)
    c_up = w_up.shape[1]
    edges = [round(k * (c_up // 128) / PLACE_STEPS) * 128 for k in range(PLACE_STEPS + 1)]

    def body(chip_ref, *refs):
        a, (up_ref, tab_ref, bk_ref), o = refs[:n], refs[n:n + 3], refs[n + 3:2 * n + 3]
        up_o, bias_ref = refs[2 * n + 3:]
        for src, dst in zip(a, o):
            dst[...] = src[...].astype(BF)
        for k in range(PLACE_STEPS):
            @pl.when(pl.program_id(0) == k)
            def _(k=k):
                up_o[edges[k]:edges[k + 1], :] = up_ref[:, edges[k]:edges[k + 1]].T.astype(BF)

        @pl.when(pl.program_id(0) == 0)
        def _():
            bk = bk_ref[...]
            eq = [bk == b for b in range(NBUCKET)]
            for h in range(NH):
                acc = jnp.zeros((BLK, 2 * BLK), F32)
                for b in range(NBUCKET):
                    acc = jnp.where(eq[b], tab_ref[h, b], acc)
                bias_ref[h * BLK:(h + 1) * BLK, :] = acc

    ins, outs, shapes = _place_specs(shards)
    return _call(
        body, (*shards, w_up, table, bucket), name="place_rest", grid=(PLACE_STEPS,), prefetch=(chip,),
        in_specs=ins + [_resident(w_up.shape), pl.BlockSpec(memory_space=pltpu.SMEM),
                        pl.BlockSpec(bucket.shape, lambda i, chip_ref: (0, 0))],
        out_specs=outs + [pl.BlockSpec((c_up, D), lambda i, chip_ref: (chip_ref[0], 0)),
                          pl.BlockSpec((NH * BLK, 2 * BLK), lambda i, chip_ref: (0, 0))],
        out_shape=shapes + [SDS((N_CHIPS * c_up, D), BF), SDS((NH * BLK, 2 * BLK), F32)],
        sem=("arbitrary",), vmem_mib=32, comm=comm, free=(n + 1, n + 2))


def kernel(x, norm_mix_g, w_in, conv_w, q_norm_g, k_norm_g, rel_bias_table, sinks, out_norm_conv_g, out_norm_attn_g, w_out, norm_ffn_g, w_up, ffn_conv_w, ffn_conv_b, w_down, loss_target, m_norm_mix_g, m_w_in, m_conv_w, m_q_norm_g, m_k_norm_g, m_rel_bias_table, m_sinks, m_out_norm_conv_g, m_out_norm_attn_g, m_w_out, m_norm_ffn_g, m_w_up, m_ffn_conv_w, m_ffn_conv_b, m_w_down, v_norm_mix_g, v_w_in, v_conv_w, v_q_norm_g, v_k_norm_g, v_rel_bias_table, v_sinks, v_out_norm_conv_g, v_out_norm_attn_g, v_w_out, v_norm_ffn_g, v_w_up, v_ffn_conv_w, v_ffn_conv_b, v_w_down):
    as_arg = lambda i: jnp.reshape(i, (1,)).astype(jnp.int32)
    chip = as_arg(2 * lax.axis_index("x") + lax.axis_index("y"))
    core = as_arg(lax.axis_index("c"))
    me = 2 * chip + core
    xs, tgt = x[0], loss_target[0]
    qg, kg, gco, gao, g1, g2, fb = q_norm_g, k_norm_g, out_norm_conv_g, out_norm_attn_g, norm_mix_g, norm_ffn_g, ffn_conv_b
    pieces = lambda g: g.reshape(N_DEV, g.shape[0] // N_DEV, D)
    whole = lambda f: f.reshape(2 * f.shape[1], D)

    bucket = jnp.asarray(_bucket_table())
    p_in, p_cw, p_fw = _place_first(chip, w_in[0].T, conv_w[0], ffn_conv_w[0])
    p_out, p_down, p_up, bias, w_int, cw_all, fw_all = _place_rest(
        chip, [w_out[0], w_down[0]], w_up[0], rel_bias_table.T, bucket,
        comm=[_t_gather(p_in, relayed_first=True), _t_small_weights(p_cw), _t_small_weights(p_fw)])
    cw8 = jnp.transpose(cw_all, (1, 0, 2)).reshape(8, CW)
    fw8 = jnp.transpose(fw_all, (1, 0, 2)).reshape(8, 2 * DFF)

    early = 3 / 11
    proj, u1, w_out_f, p_up = _inproj(xs, g1, w_int, comm=[_t_gather(p_out), _t_gather(p_up, (0, early))])
    y, w_upt = _mix_fwd(proj, sinks, cw8, qg, kg, gco, gao, bias, comm=[_t_gather(p_up, (early, 1))])
    h1, u2 = _outproj(y, w_out_f, xs, g2)
    up, = _ffn_up(u2, w_upt)
    a, w_down_f = _ffn_act(up, fw8, fb, comm=[_t_gather(p_down)])
    dh2, dh2b, sq = _ffn_down(a, w_down_f, h1, tgt)

    gdbf, = _wgrad("wgrad_down", [a], dh2b, None)
    da, sib_down = _ffn_down_bwd(dh2b, w_down_f, comm=[_t_sibling(pieces(gdbf))])
    pbf_down, own_down = _chip_sum("chip_sum_w_down", pieces(gdbf), sib_down, core, chip)
    dug, duv, dfwg, dfwv, dfbg, dfbv, chips_down = _ffn_act_bwd(up, da, fw8, fb, comm=[_t_chips(pbf_down)])
    fin_down, = _final_sum("final_sum_w_down", own_down, chips_down, core)
    gubf, = _wgrad("wgrad_up", [dug, duv], u2, False)
    dh1, dh1b, dg2, sib_up, fin_down = _norm_matmul_bwd(
        "ffn_up_bwd", [dug, duv], w_upt, [0, DFF], h1, g2, dh2, True, comm=[_t_sibling(pieces(gubf)), _t_swap(fin_down)])
    pbf_up, own_up = _chip_sum("chip_sum_w_up", pieces(gubf), sib_up, core, chip)
    gobf, = _wgrad("wgrad_out", [y], dh1b, True)
    dy, sib_out = _out_bwd(dh1b, w_out_f, comm=[_t_sibling(pieces(gobf))])
    pbf_out, own_out = _chip_sum("chip_sum_w_out", pieces(gobf), sib_out, core, chip)
    dproj, dcw8, dqg, dkg, dgco, dgao, dsink, dbias, chips_up, chips_out = _mix_bwd(
        proj, dy, sinks, cw8, qg, kg, gco, gao, bias, comm=[_t_chips(pbf_up), _t_chips(pbf_out)])
    fin_up, = _final_sum("final_sum_w_up", own_up, chips_up, core)
    tbl_all = _band_bias_bwd(dbias, bucket, me)
    p_all = _pack_small(me, dfwg, dfwv, dfbg, dfbv, dg2, dgco, dgao, dcw8, dqg, dkg, dsink, sq)
    gibf, fin_up, p_all, tbl_all = _wgrad(
        "wgrad_in", [dproj], u1, False, comm=[_t_swap(fin_up), _t_allgather(p_all), _t_allgather(tbl_all)])
    fin_out, sib_in = _final_sum("final_sum_w_out", own_out, chips_out, core, comm=[_t_sibling(pieces(gibf))])
    pbf_in, own_in = _chip_sum("chip_sum_w_in", pieces(gibf), sib_in, core, chip)
    dx, g1_all, chips_in, fin_out = _norm_matmul_bwd(
        "in_bwd", [dproj], w_int, [0], xs, g1, dh1, False, comm=[_t_chips(pbf_in), _t_swap(fin_out)], slot=me)
    fin_in, = _final_sum("final_sum_w_in", own_in, chips_in, core)
    g1_all, fin_in = _comm_call("gather_last", [_t_allgather(g1_all), _t_swap(fin_in)])

    g_w_out, g_w_down = whole(fin_out), whole(fin_down)
    g_w_down, d_down, nm_down, nv_down = _adamw("adamw_w_down", w_down[0], g_w_down, m_w_down[0], v_w_down[0], 352, True)
    g_w_up, d_up, nm_up, nv_up = _adamw(
        "adamw_w_up", w_up[0], whole(fin_up), m_w_up[0], v_w_up[0], 256, True, stage=False, g_transposed=True)
    g_w_out, d_out, nm_out, nv_out = _adamw("adamw_w_out", w_out[0], g_w_out, m_w_out[0], v_w_out[0], 256, True, stage=False)
    g_w_in, d_in, nm_in, nv_in = [a.T for a in _adamw(
        "adamw_w_in", w_in[0].T, whole(fin_in), m_w_in[0].T, v_w_in[0].T, INW // N_CHIPS // 3, True)]
    taps = lambda a: jnp.transpose(a, (1, 0, 2))
    sw = [norm_mix_g, taps(conv_w), q_norm_g, k_norm_g, rel_bias_table.T, sinks, out_norm_conv_g, out_norm_attn_g,
          norm_ffn_g, taps(ffn_conv_w), ffn_conv_b]
    smm = [m_norm_mix_g, taps(m_conv_w), m_q_norm_g, m_k_norm_g, m_rel_bias_table.T, m_sinks, m_out_norm_conv_g,
           m_out_norm_attn_g, m_norm_ffn_g, taps(m_ffn_conv_w), m_ffn_conv_b]
    smv = [v_norm_mix_g, taps(v_conv_w), v_q_norm_g, v_k_norm_g, v_rel_bias_table.T, v_sinks, v_out_norm_conv_g,
           v_out_norm_attn_g, v_norm_ffn_g, taps(v_ffn_conv_w), v_ffn_conv_b]
    *small_out, loss = _small_adam(chip, p_all, g1_all, tbl_all, sw, smm, smv)
    sg, sd, snm, snv = [list(r) for r in small_out]
    for r in (sg, sd, snm, snv):
        r[1], r[4], r[9] = taps(r[1]), r[4].T, taps(r[9])

    def order(s, b_in, b_out, b_up, b_down):
        return (s[0], b_in[None], s[1], s[2], s[3], s[4], s[5], s[6], s[7], b_out[None], s[8], b_up[None],
                s[9], s[10], b_down[None])

    return (loss.reshape(()), dx[None],
            *order(sg, g_w_in, g_w_out, g_w_up, g_w_down),
            *order(sd, d_in, d_out, d_up, d_down),
            *order(snm, nm_in, nm_out, nm_up, nm_down),
            *order(snv, nv_in, nv_out, nv_up, nv_down))
```

```python
import functools
import math

import numpy as np

import jax
import jax.numpy as jnp
from jax import lax
from jax.experimental import pallas as pl
from jax.experimental.pallas import tpu as pltpu

F32 = jnp.float32
BF = jnp.bfloat16
SDS = jax.ShapeDtypeStruct

T = 2048
D = 1024
CW = 512
AW = 512
HD = 64
NH = 8
NKV = 2
GQ = 4
INW = 2304
DFF = 2816
BLK = 128
NB = T // BLK
NBUCKET = 32
EPS = 1e-6
NEG_INF = -1e30
N_CHIPS = 4
N_DEV = 8

ADAM_LR = 0.001
ADAM_B1 = 0.9
ADAM_B2 = 0.999
ADAM_EPS = 1e-08
ADAM_WD = 0.01
ADAM_STEP = 10

TM = 512
MIB = 1024 * 1024
MESH = pl.DeviceIdType.MESH
ANY = pl.BlockSpec(memory_space=pl.ANY)

_pcall = pl.pallas_call


def _params(sem=None, vmem_mib=None, collective_id=None):
    kw = {} if collective_id is None else {"collective_id": collective_id}
    if sem is not None:
        kw["dimension_semantics"] = sem
    if vmem_mib is not None:
        kw["vmem_limit_bytes"] = vmem_mib * MIB
    return pltpu.CompilerParams(**kw)


def _resident(shape):
    return pl.BlockSpec(shape, lambda *_: (0,) * len(shape), pipeline_mode=pl.Buffered(1))


def _dot(a, b, ca, cb):
    return lax.dot_general(a, b, (((ca,), (cb,)), ((), ())), preferred_element_type=F32)


def _rms_bwd(dy, x, r, g):
    dg = jnp.sum(dy * (x * r), axis=0, keepdims=True)
    dgx = dy * g
    dx = r * dgx - x * (r * r * r) * jnp.mean(x * dgx, axis=-1, keepdims=True)
    return dx, dg


def _where():
    x, y, c = lax.axis_index("x"), lax.axis_index("y"), lax.axis_index("c")
    return x, y, c, [(1 - x, y), (x, 1 - y), (1 - x, 1 - y)]


def _rcopy(src, dst, ssem, rsem, dev):
    return pltpu.make_async_remote_copy(src_ref=src, dst_ref=dst, send_sem=ssem, recv_sem=rsem, device_id=dev,
                                        device_id_type=MESH)


SIBLING, Y_CHIP, X_CHIP, DIAGONAL_CHIP = 1, 2, 4, 6
OTHER_CHIPS = (Y_CHIP, X_CHIP, DIAGONAL_CHIP)
EVERYONE = tuple(range(1, N_DEV))
BARRIER_OF = {(SIBLING,): 0, (SIBLING, Y_CHIP, X_CHIP): 1, OTHER_CHIPS: 2, (SIBLING,) + OTHER_CHIPS: 3, EVERYONE: 4}


def _peer(rel):
    x, y, c, _ = _where()
    return x ^ ((rel >> 2) & 1), y ^ ((rel >> 1) & 1), c ^ (rel & 1)


class _Task:
    def __init__(self, ins, outs, alias, n_sem, start, finish, middle=None, peers=()):
        self.ins, self.outs, self.alias, self.n_sem, self.start, self.finish = ins, outs, alias, n_sem, start, finish
        self.middle = middle if middle is not None else (lambda *args: None)
        self.peers = peers


def _peers_of(comm):
    return tuple(sorted({p for t in comm for p in t.peers}))


def _enter(comm):
    peers = _peers_of(comm)
    barrier = pltpu.get_barrier_semaphore()
    for rel in peers:
        pl.semaphore_signal(barrier, inc=1, device_id=_peer(rel), device_id_type=MESH)
    pl.semaphore_wait(barrier, len(peers))


ROWS16 = 16


def _t_gather(placed, part=(0, 1), relayed_first=False):
    R = placed.shape[0] // N_CHIPS
    q = R // 4
    lo, hi = (round(f * (q // ROWS16)) * ROWS16 for f in part)

    def quarter(chip_index, core, k):
        return pl.ds(pl.multiple_of(chip_index * R + core * 2 * q + k * q + lo, ROWS16), hi - lo)

    def places():
        x, y, c, _ = _where()
        return c, 2 * x + y, 2 * (1 - x) + y, 2 * x + (1 - y), 2 * (1 - x) + (1 - y), (1 - x, y, c), (x, 1 - y, c), (x, y, 1 - c)

    def copy(buf, k, chip_index, core, quart, ss, rs, b, dev):
        window = buf.at[quarter(chip_index, core, quart)]
        return _rcopy(window, window, ss.at[b + k], rs.at[b + k], dev)

    def first_hop(cout, ss, rs, b, which):
        c, me, _, _, _, x_nbr, y_nbr, _ = places()
        for k, (quart, dev) in enumerate(((0, x_nbr), (1, y_nbr), (1, x_nbr), (0, y_nbr))):
            if k in which:
                copy(cout[0], k, me, c, quart, ss, rs, b, dev).start()

    def start(cin, cout, ss, rs, b):
        first_hop(cout, ss, rs, b, (0, 1) if relayed_first else (0, 1, 2, 3))

    def middle(cin, cout, ss, rs, b):
        c, _, xc, yc, _, x_nbr, y_nbr, sib = places()
        for k, chip_index, quart, dev in ((0, xc, 0, y_nbr), (1, yc, 1, x_nbr)):
            copy(cout[0], k, chip_index, c, quart, ss, rs, b, dev).wait_recv()
            copy(cout[0], 4 + k, chip_index, c, quart, ss, rs, b, dev).start()
            copy(cout[0], 6 + k, chip_index, c, quart, ss, rs, b, sib).start()
        if relayed_first:
            first_hop(cout, ss, rs, b, (2, 3))

    later = ((2, 1, 1), (3, 2, 0), (4, 3, 0), (5, 3, 1))

    def finish(cin, cout, ss, rs, b):
        c, me, xc, yc, dc, _, _, sib = places()
        chip_of = {1: xc, 2: yc, 3: dc}
        for k, whose, quart in later:
            copy(cout[0], k, chip_of[whose], c, quart, ss, rs, b, sib).wait_recv()
            copy(cout[0], 6 + k, chip_of[whose], c, quart, ss, rs, b, sib).start()
        for k, whose, quart in ((0, 1, 0), (1, 2, 1)) + later:
            copy(cout[0], 6 + k, chip_of[whose], 1 - c, quart, ss, rs, b, sib).wait_recv()
        for k in range(12):
            copy(cout[0], k, me, c, 0, ss, rs, b, sib).wait_send()

    return _Task([placed], [SDS(placed.shape, placed.dtype)], [(0, 0)], 12, start, finish, middle, peers=(SIBLING, Y_CHIP, X_CHIP))


def _t_small_weights(buf):
    def start(cin, cout, ss, rs, b):
        x, y, c, chips = _where()
        mine = cout[0].at[2 * x + y]
        for r, (px, py) in enumerate(chips):
            _rcopy(mine, mine, ss.at[b + r], rs.at[b + r], (px, py, c)).start()

    def finish(cin, cout, ss, rs, b):
        x, y, c, chips = _where()
        for r, (px, py) in enumerate(chips):
            got = cout[0].at[2 * px + py]
            _rcopy(got, got, ss.at[b + r], rs.at[b + r], (px, py, c)).wait_recv()
        for r, (px, py) in enumerate(chips):
            mine = cout[0].at[2 * x + y]
            _rcopy(mine, mine, ss.at[b + r], rs.at[b + r], (px, py, c)).wait_send()

    return _Task([buf], [SDS(buf.shape, buf.dtype)], [(0, 0)], 3, start, finish, peers=OTHER_CHIPS)


def _t_sibling(gbf):
    def start(cin, cout, ss, rs, b):
        x, y, c, _ = _where()
        for jj in range(N_CHIPS):
            _rcopy(cin[0].at[2 * jj + (1 - c)], cout[0].at[jj], ss.at[b + jj], rs.at[b + jj], (x, y, 1 - c)).start()

    def finish(cin, cout, ss, rs, b):
        x, y, c, _ = _where()
        for jj in range(N_CHIPS):
            got = cout[0].at[jj]
            _rcopy(got, got, ss.at[b + jj], rs.at[b + jj], (x, y, 1 - c)).wait_recv()
        for jj in range(N_CHIPS):
            got = cout[0].at[jj]
            _rcopy(got, got, ss.at[b + jj], rs.at[b + jj], (x, y, 1 - c)).wait_send()

    return _Task([gbf], [SDS((N_CHIPS,) + gbf.shape[1:], BF)], [], N_CHIPS, start, finish, peers=(SIBLING,))


def _t_chips(pbf):
    def start(cin, cout, ss, rs, b):
        x, y, c, chips = _where()
        for r, (px, py) in enumerate(chips):
            _rcopy(cin[0].at[2 * px + py], cout[0].at[r], ss.at[b + r], rs.at[b + r], (px, py, c)).start()

    def finish(cin, cout, ss, rs, b):
        x, y, c, chips = _where()
        for r, (px, py) in enumerate(chips):
            got = cout[0].at[r]
            _rcopy(got, got, ss.at[b + r], rs.at[b + r], (px, py, c)).wait_recv()
        for r, (px, py) in enumerate(chips):
            got = cout[0].at[r]
            _rcopy(got, got, ss.at[b + r], rs.at[b + r], (px, py, c)).wait_send()

    return _Task([pbf], [SDS((3,) + pbf.shape[1:], BF)], [], 3, start, finish, peers=OTHER_CHIPS)


def _t_swap(fin):
    def start(cin, cout, ss, rs, b):
        x, y, c, _ = _where()
        mine = cout[0].at[c]
        _rcopy(mine, mine, ss.at[b], rs.at[b], (x, y, 1 - c)).start()

    def finish(cin, cout, ss, rs, b):
        x, y, c, _ = _where()
        got = cout[0].at[1 - c]
        _rcopy(got, got, ss.at[b], rs.at[b], (x, y, 1 - c)).wait_recv()
        _rcopy(got, got, ss.at[b], rs.at[b], (x, y, 1 - c)).wait_send()

    return _Task([fin], [SDS(fin.shape, fin.dtype)], [(0, 0)], 1, start, finish, peers=(SIBLING,))


def _t_allgather(buf):
    def peers():
        x, y, c, _ = _where()
        out = []
        for rel in range(1, N_DEV):
            px, py, pc = x ^ ((rel >> 2) & 1), y ^ ((rel >> 1) & 1), c ^ (rel & 1)
            out.append((rel - 1, 4 * px + 2 * py + pc, (px, py, pc)))
        return 4 * x + 2 * y + c, out

    def start(cin, cout, ss, rs, b):
        me, ps = peers()
        mine = cout[0].at[me]
        for k, _, dev in ps:
            _rcopy(mine, mine, ss.at[b + k], rs.at[b + k], dev).start()

    def finish(cin, cout, ss, rs, b):
        me, ps = peers()
        for k, pidx, dev in ps:
            got = cout[0].at[pidx]
            _rcopy(got, got, ss.at[b + k], rs.at[b + k], dev).wait_recv()
        for k, _, dev in ps:
            mine = cout[0].at[me]
            _rcopy(mine, mine, ss.at[b + k], rs.at[b + k], dev).wait_send()

    return _Task([buf], [SDS(buf.shape, buf.dtype)], [(0, 0)], N_DEV - 1, start, finish, peers=EVERYONE)


def _run_tasks(comm, which, cin, cout, ss, rs):
    i0 = o0 = s0 = 0
    for t in comm:
        getattr(t, which)(cin[i0:i0 + len(t.ins)], cout[o0:o0 + len(t.outs)], ss, rs, s0)
        i0, o0, s0 = i0 + len(t.ins), o0 + len(t.outs), s0 + t.n_sem


def _from_hbm(*arrays):
    return [pltpu.with_memory_space_constraint(a, pltpu.HBM) for a in arrays]


def _in_hbm(shapes):
    return [pltpu.HBM(s.shape, s.dtype) for s in shapes]


def _comm_layout(comm, n_in, n_out):
    c_in = [a for t in comm for a in t.ins]
    c_out = [s for t in comm for s in t.outs]
    aliases, i0, o0 = {}, 0, 0
    for t in comm:
        for i, o in t.alias:
            aliases[n_in + i0 + i] = n_out + o0 + o
        i0, o0 = i0 + len(t.ins), o0 + len(t.outs)
    return c_in, c_out, aliases, sum(t.n_sem for t in comm)


def _call(body, operands, *, name, grid, in_specs, out_specs, out_shape, scratch_shapes=(), sem=None, vmem_mib=None, comm=(),
          free=(), prefetch=()):
    operands = [o if s.memory_space == pltpu.SMEM or k in free else pltpu.with_memory_space_constraint(o, pltpu.HBM)
                for k, (o, s) in enumerate(zip(operands, in_specs))]
    n_pre, n_in, n_out, n_scr = len(prefetch), len(in_specs), len(out_specs), len(scratch_shapes)
    c_in, c_out, aliases, n_sem = _comm_layout(comm, n_pre + n_in, n_out)
    sems = [pltpu.SemaphoreType.DMA((n_sem,)), pltpu.SemaphoreType.DMA((n_sem,))] if comm else []

    def wrapped(*refs):
        pre, refs = refs[:n_pre], refs[n_pre:]
        ins, cin = refs[:n_in], refs[n_in:n_in + len(c_in)]
        rest = refs[n_in + len(c_in):]
        outs, cout = rest[:n_out], rest[n_out:n_out + len(c_out)]
        rest = rest[n_out + len(c_out):]
        scr, csem = rest[:n_scr], rest[n_scr:]
        if not comm:
            return body(*pre, *ins, *outs, *scr)
        step = functools.reduce(lambda acc, k: acc * grid[k] + pl.program_id(k), range(len(grid)), 0)
        n_steps = math.prod(grid)

        @pl.when(step == 0)
        def _():
            _enter(comm)
            _run_tasks(comm, "start", cin, cout, *csem)

        pl.when(step == n_steps // 2)(lambda: _run_tasks(comm, "middle", cin, cout, *csem))
        body(*pre, *ins, *outs, *scr)
        pl.when(step == n_steps - 1)(lambda: _run_tasks(comm, "finish", cin, cout, *csem))

    grid_spec = pltpu.PrefetchScalarGridSpec(
        num_scalar_prefetch=n_pre, grid=grid, in_specs=list(in_specs) + [ANY] * len(c_in),
        out_specs=list(out_specs) + [ANY] * len(c_out), scratch_shapes=list(scratch_shapes) + sems)
    return _pcall(
        wrapped, name=name, grid_spec=grid_spec, out_shape=_in_hbm(list(out_shape) + c_out), input_output_aliases=aliases,
        compiler_params=_params(("arbitrary",) * len(grid) if comm else sem, vmem_mib,
                                BARRIER_OF[_peers_of(comm)] if comm else None),
    )(*prefetch, *operands, *_from_hbm(*c_in))


def _comm_call(name, comm):
    c_in, c_out, aliases, n_sem = _comm_layout(comm, 0, 0)

    def body(*refs):
        cin, cout, (ss, rs) = refs[:len(c_in)], refs[len(c_in):len(c_in) + len(c_out)], refs[len(c_in) + len(c_out):]
        _enter(comm)
        for phase in ("start", "middle", "finish"):
            _run_tasks(comm, phase, cin, cout, ss, rs)

    return _pcall(
        body, name=name, in_specs=[ANY] * len(c_in), out_specs=[ANY] * len(c_out), out_shape=_in_hbm(c_out),
        scratch_shapes=[pltpu.SemaphoreType.DMA((n_sem,)), pltpu.SemaphoreType.DMA((n_sem,))],
        input_output_aliases=aliases, compiler_params=_params(collective_id=BARRIER_OF[_peers_of(comm)]),
    )(*_from_hbm(*c_in))


def _inproj(x, g1, w_int, comm=()):
    tm = TM

    def body(x_ref, g_ref, w_ref, proj_ref, u_ref):
        xf = x_ref[...]
        r = lax.rsqrt(jnp.mean(xf * xf, axis=-1, keepdims=True) + EPS)
        u = (xf * r * g_ref[...]).astype(BF)
        u_ref[...] = u
        proj_ref[...] = _dot(u, w_ref[...], 1, 1)

    return _call(
        body, (x, g1, w_int), name="inproj", grid=(T // tm,),
        in_specs=[pl.BlockSpec((tm, D), lambda i: (i, 0)), pl.BlockSpec((1, D), lambda i: (0, 0)),
                  _resident((INW, D))],
        out_specs=[pl.BlockSpec((tm, INW), lambda i: (i, 0)), pl.BlockSpec((tm, D), lambda i: (i, 0))],
        out_shape=[SDS((T, INW), F32), SDS((T, D), BF)], sem=("parallel",), vmem_mib=40, comm=comm, free=(0, 1))


def _outproj(y, w_out, x, g2):
    tm = TM

    def body(y_ref, w_ref, x_ref, g_ref, h1_ref, u2_ref):
        h1 = x_ref[...] + _dot(y_ref[...], w_ref[...], 1, 0)
        h1_ref[...] = h1
        r = lax.rsqrt(jnp.mean(h1 * h1, axis=-1, keepdims=True) + EPS)
        u2_ref[...] = (h1 * r * g_ref[...]).astype(BF)

    return _call(
        body, (y, w_out, x, g2), name="outproj", grid=(T // tm,),
        in_specs=[pl.BlockSpec((tm, D), lambda i: (i, 0)), _resident((D, D)),
                  pl.BlockSpec((tm, D), lambda i: (i, 0)), pl.BlockSpec((1, D), lambda i: (0, 0))],
        out_specs=[pl.BlockSpec((tm, D), lambda i: (i, 0)), pl.BlockSpec((tm, D), lambda i: (i, 0))],
        out_shape=[SDS((T, D), F32), SDS((T, D), BF)], sem=("parallel",), vmem_mib=32, free=(2, 3))


def _ffn_up(u2, w_upt, comm=()):
    tm, tn = 1024, 512

    def body(u_ref, w_ref, o_ref):
        o_ref[...] = _dot(u_ref[...], w_ref[...], 1, 1).astype(BF)

    return _call(
        body, (u2, w_upt), name="ffn_up", grid=(T // tm, 2 * DFF // tn),
        in_specs=[pl.BlockSpec((tm, D), lambda i, j: (i, 0)), pl.BlockSpec((tn, D), lambda i, j: (j, 0))],
        out_specs=[pl.BlockSpec((tm, tn), lambda i, j: (i, j))], out_shape=[SDS((T, 2 * DFF), BF)],
        sem=("parallel", "parallel"), vmem_mib=32, comm=comm, free=(1,))


def _ffn_down(a, w_down, h1, tgt):
    tm = TM

    def body(a_ref, w_ref, h1_ref, t_ref, dh_ref, dhb_ref, l_ref):
        @pl.when(pl.program_id(0) == 0)
        def _():
            l_ref[...] = jnp.zeros_like(l_ref)

        h2 = h1_ref[...] + _dot(a_ref[...], w_ref[...], 1, 0)
        e = h2 - t_ref[...]
        dh = e * (1.0 / D)
        dh_ref[...] = dh
        dhb_ref[...] = dh.astype(BF)
        e2 = jnp.sum((e * e).reshape(tm // 8, 8, D), axis=0)
        acc = e2[:, 0:128]
        for k in range(1, D // 128):
            acc = acc + e2[:, k * 128:(k + 1) * 128]
        l_ref[...] += acc

    return _call(
        body, (a, w_down, h1, tgt), name="ffn_down", grid=(T // tm,),
        in_specs=[pl.BlockSpec((tm, DFF), lambda i: (i, 0)), _resident((DFF, D)),
                  pl.BlockSpec((tm, D), lambda i: (i, 0)), pl.BlockSpec((tm, D), lambda i: (i, 0))],
        out_specs=[pl.BlockSpec((tm, D), lambda i: (i, 0)), pl.BlockSpec((tm, D), lambda i: (i, 0)),
                   pl.BlockSpec((8, 128), lambda i: (0, 0))],
        out_shape=[SDS((T, D), F32), SDS((T, D), BF), SDS((8, 128), F32)], sem=("arbitrary",), vmem_mib=40, free=(2, 3))


def _bucket_table():
    q = np.arange(BLK, dtype=np.int32)[:, None]
    j = np.arange(2 * BLK, dtype=np.int32)[None, :]
    n = np.maximum(q + BLK - j, 0)
    nf = np.maximum(n, 1).astype(np.float32)
    max_exact = NBUCKET // 2
    large = max_exact + (np.log(nf / np.float32(max_exact)) / np.float32(math.log(BLK / max_exact))
                         * np.float32(NBUCKET - max_exact)).astype(np.int32)
    large = np.minimum(large, NBUCKET - 1)
    return np.where(n < max_exact, n, large).astype(np.int32)


def _band_bias_bwd(dbias, bucket, me):
    def body(me_ref, db_ref, bk_ref, o_ref):
        bk = bk_ref[...]
        for b in range(NBUCKET):
            m = bk == b
            for h in range(NH):
                v = jnp.where(m, db_ref[h * BLK:(h + 1) * BLK, :], 0.0)
                s = jnp.sum(jnp.sum(v, axis=1, keepdims=True), axis=0, keepdims=True)
                o_ref[0, h:h + 1, b:b + 1] = s

    grid_spec = pltpu.PrefetchScalarGridSpec(
        num_scalar_prefetch=1, grid=(1,),
        in_specs=[pl.BlockSpec((NH * BLK, 2 * BLK), lambda i, me_ref: (0, 0)),
                  pl.BlockSpec((BLK, 2 * BLK), lambda i, me_ref: (0, 0))],
        out_specs=pl.BlockSpec((1, NH, NBUCKET), lambda i, me_ref: (me_ref[0], 0, 0)),
    )
    return _pcall(body, name="band_bias_bwd", grid_spec=grid_spec, out_shape=SDS((N_DEV, NH, NBUCKET), F32),
                  compiler_params=_params(("arbitrary",)))(me, dbias, bucket)


def _two_bf16(x):
    hi = x.astype(BF)
    return hi, (x - hi.astype(F32)).astype(BF)


def _head_sums(x, seg):
    hi, lo = _two_bf16(x)
    s = seg[0:x.shape[1], :]
    return _dot(hi, s, 1, 0) + _dot(lo, s, 1, 0)


def _head_spread(v, seg, width):
    hi, lo = _two_bf16(v)
    s = seg[0:width, :]
    return _dot(hi, s, 1, 1) + _dot(lo, s, 1, 1)


def _head_norm(x, g_t, seg, by_head=False):
    if by_head:
        heads = [x[:, h * HD:(h + 1) * HD] for h in range(x.shape[1] // HD)]
        r = jnp.concatenate([jnp.broadcast_to(lax.rsqrt(jnp.mean(v * v, axis=-1, keepdims=True) + EPS), v.shape)
                             for v in heads], axis=1)
    else:
        r = lax.rsqrt(_head_sums(x * x, seg) * (1.0 / HD) + EPS)
        r = _head_spread(r, seg, x.shape[1])
    return x * r * g_t, r


def _head_norm_bwd(dy, x, r, g_t, seg):
    dg_t = jnp.sum(dy * (x * r), axis=0, keepdims=True)
    dgx = dy * g_t
    mean = _head_spread(_head_sums(x * dgx, seg) * (1.0 / HD), seg, x.shape[1])
    return r * dgx - x * (r * r * r) * mean, dg_t


def _fold_heads(v):
    out = v[:, 0:HD]
    for h in range(1, v.shape[1] // HD):
        out = out + v[:, h * HD:(h + 1) * HD]
    return out


def _mix_forward(P, zc8, zh8, pkv, first, cw, qg_t, kg_t, gco, gao, seg, sink_ref, bias_ref, by_head=False):
    gate_b = P[:, 0:CW]
    gate_c = P[:, CW:2 * CW]
    hc = P[:, 2 * CW:3 * CW]
    z = gate_c * hc
    keep = jnp.where(first, 0.0, 1.0)
    zp = zc8 * zh8 * keep
    p1 = zp[7:8, :]
    p2 = zp[6:7, :]
    row = lax.broadcasted_iota(jnp.int32, (BLK, 1), 0)
    z1 = jnp.where(row == 0, p1, pltpu.roll(z, 1, 0))
    z2 = jnp.where(row == 0, p2, jnp.where(row == 1, p1, pltpu.roll(z, 2, 0)))
    cz = cw[0:1, :] * z2 + cw[1:2, :] * z1 + cw[2:3, :] * z
    y_conv = gate_b * cz

    scale = HD ** -0.5
    qi = lax.broadcasted_iota(jnp.int32, (BLK, 2 * BLK), 0)
    kj = lax.broadcasted_iota(jnp.int32, (BLK, 2 * BLK), 1)
    dd = qi + BLK - kj
    first_key = jnp.where(first, BLK, 0)
    valid = (dd >= 0) & (dd < BLK) & (kj >= first_key)

    q0 = 3 * CW
    k0 = q0 + AW
    v0 = k0 + NKV * HD
    q_raw = P[:, q0:k0]
    qn, rq = _head_norm(q_raw, qg_t, seg, by_head)
    qs = (qn * scale).astype(BF)
    k_raw = jnp.concatenate([pkv[:, 0:NKV * HD], P[:, k0:v0]], axis=0)
    kn, rk = _head_norm(k_raw, kg_t, seg, by_head)
    knb = kn.astype(BF)
    heads = []
    for h in range(NH):
        kv = h // GQ
        kb = knb[:, kv * HD:(kv + 1) * HD]
        vb = jnp.concatenate([pkv[:, NKV * HD + kv * HD:NKV * HD + (kv + 1) * HD],
                              P[:, v0 + kv * HD:v0 + (kv + 1) * HD]], axis=0).astype(BF)
        Q = qs[:, h * HD:(h + 1) * HD]
        S = _dot(Q, kb, 1, 1) + bias_ref[h * BLK:(h + 1) * BLK, :]
        S = jnp.where(valid, S, NEG_INF)
        sink = sink_ref[0, h]
        m = jnp.maximum(jnp.max(S, axis=-1, keepdims=True), sink)
        p = jnp.exp(S - m)
        es = jnp.exp(sink - m)
        denom = jnp.sum(p, axis=-1, keepdims=True) + es
        probs = p / denom
        O = _dot(probs.astype(BF), vb, 1, 0)
        heads.append(dict(kb=kb, vb=vb, Q=Q, probs=probs, psink=es / denom, O=O))
    y_attn = jnp.concatenate([hd["O"] for hd in heads], axis=1)

    rc = lax.rsqrt(jnp.mean(y_conv * y_conv, axis=-1, keepdims=True) + EPS)
    ra = lax.rsqrt(jnp.mean(y_attn * y_attn, axis=-1, keepdims=True) + EPS)
    y = jnp.concatenate([y_conv * rc * gco, y_attn * ra * gao], axis=1)
    return dict(gate_b=gate_b, gate_c=gate_c, hc=hc, z=z, z1=z1, z2=z2, cz=cz, y_conv=y_conv, y_attn=y_attn,
                rc=rc, ra=ra, heads=heads, y=y, row=row, scale=scale, q_raw=q_raw, rq=rq, k_raw=k_raw, rk=rk)


BPS = 2
TILE = BPS * BLK
KV0 = 3 * CW + AW


def _mix_in_specs(tile_of):
    return [
        pl.BlockSpec(memory_space=pltpu.SMEM),
        pl.BlockSpec((TILE, INW), lambda s: (tile_of(s), 0)),
        pl.BlockSpec((8, CW), lambda s: (jnp.maximum(tile_of(s) * (TILE // 8) - 1, 0), 1)),
        pl.BlockSpec((8, CW), lambda s: (jnp.maximum(tile_of(s) * (TILE // 8) - 1, 0), 2)),
        pl.BlockSpec((BLK, 2 * NKV * HD), lambda s: (jnp.maximum(tile_of(s) * BPS - 1, 0), KV0 // (2 * NKV * HD))),
    ]


def _block_inputs(tile, b, zc_ref, zh_ref, pkv_ref, first_tile):
    P = tile[b * BLK:(b + 1) * BLK, :]
    if b == 0:
        return P, zc_ref[...], zh_ref[...], pkv_ref[...], first_tile
    lo = b * BLK
    return P, tile[lo - 8:lo, CW:2 * CW], tile[lo - 8:lo, 2 * CW:3 * CW], tile[lo - BLK:lo, KV0:KV0 + 2 * NKV * HD], False


def _mix_param_specs():
    return [
        pl.BlockSpec((8, CW), lambda s: (0, 0)),
        pl.BlockSpec((1, AW), lambda s: (0, 0)),
        pl.BlockSpec((1, NKV * HD), lambda s: (0, 0)),
        pl.BlockSpec((1, CW), lambda s: (0, 0)),
        pl.BlockSpec((1, AW), lambda s: (0, 0)),
        pl.BlockSpec((AW, 128), lambda s: (0, 0)),
        pl.BlockSpec((NH * BLK, 2 * BLK), lambda s: (0, 0)),
    ]


def _mix_params(cw8, qg, kg, gco, gao, bias):
    seg = np.zeros((AW, 128), np.float32)
    seg[np.arange(AW), np.arange(AW) // HD] = 1.0
    return (cw8, jnp.tile(qg, (1, NH)), jnp.tile(kg, (1, NKV)), gco, gao, jnp.asarray(seg, BF), bias)


def _mix_fwd(proj, sinks, cw8, qg, kg, gco, gao, bias, comm=()):
    def body(sink_ref, p_ref, zc_ref, zh_ref, pkv_ref, cw_ref, qg_ref, kg_ref, gco_ref, gao_ref, seg_ref, bias_ref, y_ref):
        tile = p_ref[...]
        for b in range(BPS):
            f = _mix_forward(*_block_inputs(tile, b, zc_ref, zh_ref, pkv_ref, pl.program_id(0) == 0), cw_ref[...],
                             qg_ref[...], kg_ref[...], gco_ref[...], gao_ref[...], seg_ref[...], sink_ref, bias_ref, by_head=True)
            y_ref[b * BLK:(b + 1) * BLK, :] = f["y"].astype(BF)

    return _call(
        body, (sinks, proj, proj, proj, proj, *_mix_params(cw8, qg, kg, gco, gao, bias)), name="mix_fwd", grid=(T // TILE,),
        in_specs=_mix_in_specs(lambda s: s) + _mix_param_specs(),
        out_specs=[pl.BlockSpec((TILE, D), lambda s: (s, 0))], out_shape=[SDS((T, D), BF)],
        sem=("parallel",), vmem_mib=40, comm=comm, free=tuple(range(5, 12)))


def _mix_bwd(proj, dy, sinks, cw8, qg, kg, gco, gao, bias, comm=()):
    n_steps = T // TILE

    def tile_of(s):
        return n_steps - 1 - s

    def body(sink_ref, p_ref, zc_ref, zh_ref, pkv_ref, dy_ref, cw_ref, qg_ref, kg_ref, gco_ref, gao_ref, seg_ref, bias_ref,
             dproj_ref, dcw_ref, dqg_ref, dkg_ref, dgco_ref, dgao_ref, dsink_ref, dbias_ref,
             ndcz_ref, dkc_ref, dvc_ref):
        s = pl.program_id(0)

        @pl.when(s == 0)
        def _():
            for r in (dcw_ref, dqg_ref, dkg_ref, dgco_ref, dgao_ref, dsink_ref, dbias_ref, ndcz_ref, dkc_ref, dvc_ref):
                r[...] = jnp.zeros_like(r)

        params = (cw_ref[...], qg_ref[...], kg_ref[...], gco_ref[...], gao_ref[...], seg_ref[...])
        tile = p_ref[...]
        carry = (ndcz_ref[...], dkc_ref[...], dvc_ref[...])
        total = None
        for b in reversed(range(BPS)):
            f = _mix_forward(*_block_inputs(tile, b, zc_ref, zh_ref, pkv_ref, s == n_steps - 1), *params, sink_ref, bias_ref)
            pieces, sums, carry = one_block(f, dy_ref[b * BLK:(b + 1) * BLK, :], params, carry)
            for lo, piece in pieces:
                dproj_ref[b * BLK:(b + 1) * BLK, lo:lo + piece.shape[1]] = piece
            total = sums if total is None else [t + v for t, v in zip(total, sums)]
        ndcz_ref[...], dkc_ref[...], dvc_ref[...] = carry
        dcw, dqg_t, dkg_t, dgco, dgao, dsink, *ds = total
        dcw_ref[0:3, :] += dcw
        dqg_ref[...] += _fold_heads(dqg_t)
        dkg_ref[...] += _fold_heads(dkg_t)
        dgco_ref[...] += dgco
        dgao_ref[...] += dgao
        dsink_ref[...] += dsink
        for h in range(NH):
            dbias_ref[h * BLK:(h + 1) * BLK, :] += ds[h]

    def one_block(f, dy, params, carry):
        cw, qg_v, kg_v, gco_v, gao_v, seg = params
        nxt, dk_carry, dv_carry = carry
        dyc, dgco = _rms_bwd(dy[:, 0:CW], f["y_conv"], f["rc"], gco_v)
        dya, dgao = _rms_bwd(dy[:, CW:CW + AW], f["y_attn"], f["ra"], gao_v)

        row = f["row"]
        dgate_b = dyc * f["cz"]
        dcz = dyc * f["gate_b"]
        dcw = jnp.concatenate([jnp.sum(dcz * f[k], axis=0, keepdims=True) for k in ("z2", "z1", "z")], axis=0)
        n0 = nxt[0:1, :]
        n1 = nxt[1:2, :]
        d1 = jnp.where(row == BLK - 1, n0, pltpu.roll(dcz, BLK - 1, 0))
        d2 = jnp.where(row == BLK - 1, n1, jnp.where(row == BLK - 2, n0, pltpu.roll(dcz, BLK - 2, 0)))
        dz = cw[2:3, :] * dcz + cw[1:2, :] * d1 + cw[0:1, :] * d2
        pieces = [(0, dgate_b.astype(BF)), (CW, (dz * f["hc"]).astype(BF)), (2 * CW, (dz * f["gate_c"]).astype(BF))]

        scale = f["scale"]
        lane = lax.broadcasted_iota(jnp.int32, (1, 128), 1)
        dsink = jnp.zeros((1, 128), F32)
        dq_cols, dk_cols, dv_cols, dk_prev, dv_prev, ds = [], [], [], [], [], []
        for kv in range(NKV):
            dKb = dVb = 0.0
            for h in range(kv * GQ, (kv + 1) * GQ):
                hd = f["heads"][h]
                dO = dya[:, h * HD:(h + 1) * HD]
                delta = jnp.sum(dO * hd["O"], axis=-1, keepdims=True)
                dOb = dO.astype(BF)
                dP = _dot(dOb, hd["vb"], 1, 1)
                dS = hd["probs"] * (dP - delta)
                tot = jnp.sum(hd["psink"] * delta, axis=0, keepdims=True)
                dsink = dsink - jnp.where(lane == h, tot, 0.0)
                ds.append(dS)
                dSb = dS.astype(BF)
                dq_cols.append(_dot(dSb, hd["kb"], 1, 0))
                dKb = dKb + _dot(dSb, hd["Q"], 0, 0)
                dVb = dVb + _dot(hd["probs"].astype(BF), dOb, 0, 0)
            dk_cols.append(dKb[BLK:, :] + dk_carry[:, kv * HD:(kv + 1) * HD])
            dv_cols.append(dVb[BLK:, :] + dv_carry[:, kv * HD:(kv + 1) * HD])
            dk_prev.append(dKb[:BLK, :])
            dv_prev.append(dVb[:BLK, :])
        dq_raw, dqg_t = _head_norm_bwd(jnp.concatenate(dq_cols, axis=1) * scale, f["q_raw"], f["rq"], qg_v, seg)
        dk_raw, dkg_t = _head_norm_bwd(jnp.concatenate(dk_cols, axis=1), f["k_raw"][BLK:, :], f["rk"][BLK:, :], kg_v, seg)
        pieces.append((3 * CW, jnp.concatenate([dq_raw, dk_raw] + dv_cols, axis=1).astype(BF)))
        owed = (dcz[0:8, :], jnp.concatenate(dk_prev, axis=1), jnp.concatenate(dv_prev, axis=1))
        return pieces, [dcw, dqg_t, dkg_t, dgco, dgao, dsink, *ds], owed

    small = lambda r, c: pl.BlockSpec((r, c), lambda s: (0, 0))
    return _call(
        body, (sinks, proj, proj, proj, proj, dy, *_mix_params(cw8, qg, kg, gco, gao, bias)), name="mix_bwd", grid=(n_steps,),
        in_specs=_mix_in_specs(tile_of) + [pl.BlockSpec((TILE, D), lambda s: (tile_of(s), 0))] + _mix_param_specs(),
        out_specs=[pl.BlockSpec((TILE, INW), lambda s: (tile_of(s), 0)), small(8, CW), small(1, HD), small(1, HD),
                   small(1, CW), small(1, AW), small(1, 128), small(NH * BLK, 2 * BLK)],
        out_shape=[SDS((T, INW), BF), SDS((8, CW), F32), SDS((1, HD), F32), SDS((1, HD), F32), SDS((1, CW), F32),
                   SDS((1, AW), F32), SDS((1, 128), F32), SDS((NH * BLK, 2 * BLK), F32)],
        scratch_shapes=[pltpu.VMEM((8, CW), F32), pltpu.VMEM((BLK, NKV * HD), F32), pltpu.VMEM((BLK, NKV * HD), F32)],
        sem=("arbitrary",), vmem_mib=56, comm=comm, free=(1, 2, 3, 4) + tuple(range(6, 13)))


FT = 256
NFT = DFF // FT
RC = 256
NCH = T // RC
LEAD = 16


def _rows8(x):
    return jnp.sum(x.reshape(x.shape[0] // 8, 8, x.shape[1]), axis=0)


def _ffn_act_specs():
    return [
        pl.BlockSpec((T, FT), lambda j: (0, j)), pl.BlockSpec((T, FT), lambda j: (0, NFT + j)),
        pl.BlockSpec((8, FT), lambda j: (0, j)), pl.BlockSpec((8, FT), lambda j: (0, NFT + j)),
        pl.BlockSpec((1, FT), lambda j: (0, j)), pl.BlockSpec((1, FT), lambda j: (0, NFT + j)),
    ]


def _conv_rows(win, w, b, n):
    win = win.astype(F32)
    u = win[LEAD:LEAD + n]
    u1 = pltpu.roll(win, 1, 0)[LEAD:LEAD + n]
    u2 = pltpu.roll(win, 2, 0)[LEAD:LEAD + n]
    return u2, u1, u, w[0:1, :] * u2 + w[1:2, :] * u1 + w[2:3, :] * u + b


def _ffn_act(up, fw8, fb, comm=()):
    def body(ug_ref, uv_ref, wg_ref, wv_ref, bg_ref, bv_ref, a_ref):
        wg, wv, bg, bv = wg_ref[...], wv_ref[...], bg_ref[...], bv_ref[...]

        def chunk(win_g, win_v):
            gp = _conv_rows(win_g, wg, bg, RC)[3]
            vp = _conv_rows(win_v, wv, bv, RC)[3]
            return (gp * jax.nn.sigmoid(gp) * vp).astype(BF)

        zero = jnp.zeros((LEAD, FT), BF)
        a_ref[0:RC, :] = chunk(jnp.concatenate([zero, ug_ref[0:RC, :]], axis=0),
                               jnp.concatenate([zero, uv_ref[0:RC, :]], axis=0))

        def step(i, carry):
            r0 = pl.multiple_of(i * RC, RC)
            win = pl.ds(r0 - LEAD, RC + LEAD)
            a_ref[pl.ds(r0, RC), :] = chunk(ug_ref[win, :], uv_ref[win, :])
            return carry

        lax.fori_loop(1, NCH, step, 0)

    return _call(
        body, (up, up, fw8, fw8, fb, fb), name="ffn_act", grid=(NFT,), in_specs=_ffn_act_specs(),
        out_specs=[pl.BlockSpec((T, FT), lambda j: (0, j))], out_shape=[SDS((T, DFF), BF)],
        sem=("parallel",), vmem_mib=40, comm=comm, free=(2, 3, 4, 5))


def _ffn_act_bwd(up, da, fw8, fb, comm=()):
    ext = RC + LEAD

    def body(ug_ref, uv_ref, wg_ref, wv_ref, bg_ref, bv_ref, da_ref,
             dug_ref, duv_ref, dwg_ref, dwv_ref, dbg_ref, dbv_ref):
        wg, wv, bg, bv = wg_ref[...], wv_ref[...], bg_ref[...], bv_ref[...]

        def chunk(win_g, win_v, da_e):
            g2, g1, g0, gp = _conv_rows(win_g, wg, bg, ext)
            v2, v1, v0, vp = _conv_rows(win_v, wv, bv, ext)
            da_e = da_e.astype(F32)
            sig = jax.nn.sigmoid(gp)
            dvp = da_e * (gp * sig)
            dgp = da_e * vp * (sig * (1.0 + gp * (1.0 - sig)))

            def back(dp, w):
                return (w[2:3, :] * dp[0:RC] + w[1:2, :] * pltpu.roll(dp, ext - 1, 0)[0:RC]
                        + w[0:1, :] * pltpu.roll(dp, ext - 2, 0)[0:RC]).astype(BF)

            def sums(dp, u2, u1, u0):
                d = dp[0:RC]
                return [_rows8(d), _rows8(d * u2[0:RC]), _rows8(d * u1[0:RC]), _rows8(d * u0[0:RC])]

            return back(dgp, wg), back(dvp, wv), sums(dgp, g2, g1, g0) + sums(dvp, v2, v1, v0)

        zero = jnp.zeros((LEAD, FT), BF)
        dug, duv, acc = chunk(jnp.concatenate([zero, ug_ref[0:ext, :]], axis=0),
                              jnp.concatenate([zero, uv_ref[0:ext, :]], axis=0), da_ref[0:ext, :])
        dug_ref[0:RC, :] = dug
        duv_ref[0:RC, :] = duv

        def step(i, acc):
            r0 = pl.multiple_of(i * RC, RC)
            win = pl.ds(r0 - LEAD, ext + LEAD)
            dug, duv, part = chunk(ug_ref[win, :], uv_ref[win, :], da_ref[pl.ds(r0, ext), :])
            dug_ref[pl.ds(r0, RC), :] = dug
            duv_ref[pl.ds(r0, RC), :] = duv
            return [a + p for a, p in zip(acc, part)]

        acc = lax.fori_loop(1, NCH - 1, step, acc)
        r0 = T - RC
        tail = lambda ref, lo: jnp.concatenate([ref[lo:T, :], zero], axis=0)
        dug, duv, part = chunk(tail(ug_ref, r0 - LEAD), tail(uv_ref, r0 - LEAD), tail(da_ref, r0))
        dug_ref[r0:T, :] = dug
        duv_ref[r0:T, :] = duv
        tot = [jnp.sum(a + p, axis=0, keepdims=True) for a, p in zip(acc, part)]
        for k, (dw_ref, db_ref) in enumerate(((dwg_ref, dbg_ref), (dwv_ref, dbv_ref))):
            db_ref[...] = tot[4 * k]
            dw_ref[...] = jnp.zeros_like(dw_ref)
            for r in range(3):
                dw_ref[r:r + 1, :] = tot[4 * k + 1 + r]

    col = lambda r: pl.BlockSpec((r, FT), lambda j: (0, j))
    return _call(
        body, (up, up, fw8, fw8, fb, fb, da), name="ffn_act_bwd", grid=(NFT,),
        in_specs=_ffn_act_specs() + [pl.BlockSpec((T, FT), lambda j: (0, j))],
        out_specs=[col(T), col(T), col(8), col(8), col(1), col(1)],
        out_shape=[SDS((T, DFF), BF), SDS((T, DFF), BF), SDS((8, DFF), F32), SDS((8, DFF), F32),
                   SDS((1, DFF), F32), SDS((1, DFF), F32)],
        sem=("parallel",), vmem_mib=40, comm=comm, free=(0, 1, 2, 3, 4, 5))


def _ffn_down_bwd(dh2b, w_down, comm=()):
    tm = TM

    def body(d_ref, w_ref, o_ref):
        o_ref[...] = _dot(d_ref[...], w_ref[...], 1, 1).astype(BF)

    return _call(
        body, (dh2b, w_down), name="ffn_down_bwd", grid=(T // tm,),
        in_specs=[pl.BlockSpec((tm, D), lambda i: (i, 0)), _resident((DFF, D))],
        out_specs=[pl.BlockSpec((tm, DFF), lambda i: (i, 0))], out_shape=[SDS((T, DFF), BF)],
        sem=("parallel",), vmem_mib=40, comm=comm, free=(0, 1))


def _norm_matmul_bwd(name, a_list, w_t, k_offsets, xin, g, dres, want_bf16, comm=(), slot=None):
    tm = TM
    ks = [a.shape[1] for a in a_list]
    n_a = len(a_list)
    n_pre = 0 if slot is None else 1

    def body(*refs):
        refs = refs[n_pre:]
        a_refs = refs[:n_a]
        w_ref, x_ref, g_ref, r_ref = refs[n_a:n_a + 4]
        outs = refs[n_a + 4:]
        dx_ref, dg_ref = outs[0], (outs[-1] if slot is None else outs[-1].at[0])

        @pl.when(pl.program_id(0) == 0)
        def _():
            dg_ref[...] = jnp.zeros_like(dg_ref)

        du = _dot(a_refs[0][...], w_ref[k_offsets[0]:k_offsets[0] + ks[0], :], 1, 0)
        for k in range(1, n_a):
            du = du + _dot(a_refs[k][...], w_ref[k_offsets[k]:k_offsets[k] + ks[k], :], 1, 0)
        x = x_ref[...]
        r = lax.rsqrt(jnp.mean(x * x, axis=-1, keepdims=True) + EPS)
        dx, dg = _rms_bwd(du, x, r, g_ref[...])
        dx = r_ref[...] + dx
        dx_ref[...] = dx
        if want_bf16:
            outs[1][...] = dx.astype(BF)
        dg_ref[...] += dg

    tile = lambda c: pl.BlockSpec((tm, c), lambda i, *_: (i, 0))
    if slot is None:
        dg_spec, dg_shape = pl.BlockSpec((1, D), lambda i: (0, 0)), SDS((1, D), F32)
    else:
        dg_spec, dg_shape = pl.BlockSpec((1, 1, D), lambda i, slot_ref: (slot_ref[0], 0, 0)), SDS((N_DEV, 1, D), F32)
    out_specs = [tile(D)] + ([tile(D)] if want_bf16 else []) + [dg_spec]
    out_shape = [SDS((T, D), F32)] + ([SDS((T, D), BF)] if want_bf16 else []) + [dg_shape]
    return _call(
        body, (*a_list, w_t, xin, g, dres), name=name, grid=(T // tm,), prefetch=() if slot is None else (slot,),
        in_specs=[tile(k) for k in ks] + [_resident(w_t.shape), tile(D),
                                           pl.BlockSpec((1, D), lambda i, *_: (0, 0)), tile(D)],
        out_specs=out_specs, out_shape=out_shape, sem=("arbitrary",), vmem_mib=56, comm=comm, free=tuple(range(n_a + 4)))


def _out_bwd(dh1b, w_out, comm=()):
    tm = TM

    def body(d_ref, w_ref, o_ref):
        o_ref[...] = _dot(d_ref[...], w_ref[...], 1, 1)

    return _call(
        body, (dh1b, w_out), name="out_bwd", grid=(T // tm,),
        in_specs=[pl.BlockSpec((tm, D), lambda i: (i, 0)), _resident((D, D))],
        out_specs=[pl.BlockSpec((tm, D), lambda i: (i, 0))], out_shape=[SDS((T, D), F32)],
        sem=("parallel",), vmem_mib=32, comm=comm, free=(0, 1))


def _wgrad(name, a_list, b, old_a, comm=()):
    m_k = a_list[0].shape[1]
    tm = max(t for t in range(128, m_k // 2 + 1, 128) if m_k % t == 0)
    steps = [a.shape[1] // tm for a in a_list]
    starts = [sum(steps[:k]) for k in range(len(a_list))]
    n_a = len(a_list)

    def body(*refs):
        a_refs, b_ref, o_ref = refs[:n_a], refs[n_a], refs[n_a + 1]
        i = pl.program_id(0)
        for k in range(n_a):
            @pl.when((i >= starts[k]) & (i < starts[k] + steps[k]))
            def _(k=k):
                o_ref[...] = _dot(a_refs[k][...], b_ref[...], 0, 0).astype(BF)

    def a_spec(k):
        return pl.BlockSpec((T, tm), lambda i: (0, jnp.clip(i - starts[k], 0, steps[k] - 1)))

    m_total = tm * sum(steps)
    return _call(
        body, (*a_list, b), name=name, grid=(sum(steps),),
        in_specs=[a_spec(k) for k in range(n_a)] + [_resident((T, D))],
        out_specs=[pl.BlockSpec((tm, D), lambda i: (i, 0))], out_shape=[SDS((m_total, D), BF)],
        sem=("parallel",), vmem_mib=40, comm=comm, free=() if old_a is None else tuple(range(n_a)) if old_a else (n_a,))


def _chip_sum(name, gbf, from_sib, core, chip):
    h = gbf.shape[1]
    th = h

    def body(core_ref, chip_ref, g_ref, s_ref, pbf_ref, own_ref):
        p = g_ref[0].astype(F32) + s_ref[0].astype(F32)
        pbf_ref[0] = p.astype(BF)

        @pl.when(pl.program_id(1) == chip_ref[0])
        def _():
            own_ref[...] = p

    grid_spec = pltpu.PrefetchScalarGridSpec(
        num_scalar_prefetch=2, grid=(h // th, N_CHIPS),
        in_specs=[pl.BlockSpec((1, th, D), lambda t, jj, core_ref, chip_ref: (2 * jj + core_ref[0], t, 0)),
                  pl.BlockSpec((1, th, D), lambda t, jj, core_ref, chip_ref: (jj, t, 0))],
        out_specs=[pl.BlockSpec((1, th, D), lambda t, jj, core_ref, chip_ref: (jj, t, 0)),
                   pl.BlockSpec((th, D), lambda t, jj, core_ref, chip_ref: (t, 0))],
    )
    return _pcall(
        body, name=name, grid_spec=grid_spec, out_shape=_in_hbm([SDS((N_CHIPS, h, D), BF), SDS((h, D), F32)]),
        compiler_params=_params(("arbitrary", "arbitrary"), 32),
    )(core, chip, *_from_hbm(gbf, from_sib))


def _final_sum(name, own, from_chips, core, comm=()):
    h = own.shape[0]

    def body(core_ref, o_ref, r_ref, f_ref):
        f_ref[0] = ((o_ref[...] + r_ref[0].astype(F32)) + r_ref[1].astype(F32)) + r_ref[2].astype(F32)

    return _call(
        body, (own, from_chips), name=name, grid=(1,), prefetch=(core,),
        in_specs=[pl.BlockSpec((h, D), lambda i, core_ref: (0, 0)), pl.BlockSpec((3, h, D), lambda i, core_ref: (0, 0, 0))],
        out_specs=[pl.BlockSpec((1, h, D), lambda i, core_ref: (core_ref[0], 0, 0))], out_shape=[SDS((2, h, D), F32)],
        sem=("arbitrary",), vmem_mib=40, comm=comm)


def _adam_math(w, g, m, v):
    nm = ADAM_B1 * m + (1.0 - ADAM_B1) * g
    nv = ADAM_B2 * v + (1.0 - ADAM_B2) * (g * g)
    m_hat = nm / (1.0 - ADAM_B1 ** ADAM_STEP)
    v_hat = nv / (1.0 - ADAM_B2 ** ADAM_STEP)
    return -ADAM_LR * (m_hat / (jnp.sqrt(v_hat) + ADAM_EPS) + ADAM_WD * w), nm, nv


def _adamw(name, w, g, m, v, tr, copy_g=False, stage=True, g_transposed=False):
    rows, cols = w.shape

    def body(w_ref, g_ref, m_ref, v_ref, *outs):
        d_ref, nm_ref, nv_ref = outs[-3:]
        for c in [pl.ds(c0, 128) for c0 in range(0, cols, 128)] if g_transposed else [slice(None)]:
            g_val = g_ref[c, :].T if g_transposed else g_ref[...]
            if copy_g:
                outs[0][:, c] = g_val
            d_ref[:, c], nm_ref[:, c], nv_ref[:, c] = _adam_math(w_ref[:, c], g_val, m_ref[:, c], v_ref[:, c])

    spec = pl.BlockSpec((tr, cols), lambda i: (i, 0))
    n_out = 4 if copy_g else 3
    g_spec = pl.BlockSpec((cols, tr), lambda i: (0, i)) if g_transposed else spec
    return _call(body, (w, g, m, v), name=name, grid=(rows // tr,), in_specs=[spec, g_spec, spec, spec], out_specs=[spec] * n_out,
                 out_shape=[SDS((rows, cols), F32)] * n_out, sem=("parallel",), vmem_mib=32,
                 free=(0, 2, 3) if stage else ())


C_SQ = 2 * DFF
P_W = C_SQ + 128
R_G2, R_GO, R_DCW, R_QK = 0, 1, 2, 5
C_GCO, C_GAO, C_DQG, C_DKG, C_SINK = 0, CW, 0, 128, 256


def _pack(name, me, ins, width, fill):
    def body(me_ref, *refs):
        o = refs[-1]
        o[...] = jnp.zeros_like(o)
        fill(o, *refs[:-1])

    return _call(body, ins, name=name, grid=(1,), prefetch=(me,),
                 in_specs=[pl.BlockSpec(a.shape, lambda i, me_ref: (0, 0)) for a in ins],
                 out_specs=[pl.BlockSpec((1, 8, width), lambda i, me_ref: (me_ref[0], 0, 0))],
                 out_shape=[SDS((N_DEV, 8, width), F32)], sem=("arbitrary",))[0]


def _pack_ffn(me, dfwg, dfwv, dfbg, dfbv, sq):
    def fill(o, dfwg_r, dfwv_r, dfbg_r, dfbv_r, sq_r):
        o[0, :, 0:DFF] = dfwg_r[...]
        o[0, :, DFF:2 * DFF] = dfwv_r[...]
        o[0, 3:4, 0:DFF] = dfbg_r[...]
        o[0, 3:4, DFF:2 * DFF] = dfbv_r[...]
        o[0, :, C_SQ:C_SQ + 128] = sq_r[...]

    return _pack("pack_ffn", me, (dfwg, dfwv, dfbg, dfbv, sq), P_W, fill)


def _pack_mix(me, dg2, dgco, dgao, dcw8, dqg, dkg, dsink):
    def fill(o, dg2_r, dgco_r, dgao_r, dcw_r, dqg_r, dkg_r, dsink_r):
        o[0, R_G2:R_G2 + 1, :] = dg2_r[...]
        o[0, R_GO:R_GO + 1, C_GCO:C_GCO + CW] = dgco_r[...]
        o[0, R_GO:R_GO + 1, C_GAO:C_GAO + AW] = dgao_r[...]
        o[0, R_DCW:R_DCW + 3, 0:CW] = dcw_r[0:3, :]
        o[0, R_QK:R_QK + 1, C_DQG:C_DQG + HD] = dqg_r[...]
        o[0, R_QK:R_QK + 1, C_DKG:C_DKG + HD] = dkg_r[...]
        o[0, R_QK:R_QK + 1, C_SINK:C_SINK + 128] = dsink_r[...]

    return _pack("pack_mix", me, (dg2, dgco, dgao, dcw8, dqg, dkg, dsink), D, fill)


N_SMALL = 11


def _small_adam(chip, p_all, pm_all, g1_all, tbl_all, ws, ms, vs):
    fw_cols = 2 * DFF // N_CHIPS
    cw_cols = CW // N_CHIPS

    def body(chip_ref, p_ref, fw_ref, pm_ref, cw_ref, g1_ref, tbl_ref, *refs):
        w_r, m_r, v_r = refs[0:N_SMALL], refs[N_SMALL:2 * N_SMALL], refs[2 * N_SMALL:3 * N_SMALL]
        outs = refs[3 * N_SMALL:]
        g_o, d_o, nm_o, nv_o = (outs[k * N_SMALL:(k + 1) * N_SMALL] for k in range(4))
        loss_o = outs[4 * N_SMALL]

        def total(ref):
            s = ref[0]
            for k in range(1, N_DEV):
                s = s + ref[k]
            return s

        S = total(p_ref)
        fw = total(fw_ref)
        M = total(pm_ref)
        cw = total(cw_ref)

        def step(i, g, at):
            d, nm, nv = _adam_math(w_r[i][at], g, m_r[i][at], v_r[i][at])
            g_o[i][at], d_o[i][at], nm_o[i][at], nv_o[i][at] = g, d, nm, nv

        everything = (slice(None), slice(None))
        step(0, total(g1_ref), everything)
        for r in range(3):
            step(1, cw[R_DCW + r:R_DCW + r + 1, :], (r, slice(None), slice(None)))
        step(2, M[R_QK:R_QK + 1, C_DQG:C_DQG + HD], everything)
        step(3, M[R_QK:R_QK + 1, C_DKG:C_DKG + HD], everything)
        step(4, total(tbl_ref), everything)
        step(5, M[R_QK:R_QK + 1, C_SINK:C_SINK + NH], everything)
        step(6, M[R_GO:R_GO + 1, C_GCO:C_GCO + CW], everything)
        step(7, M[R_GO:R_GO + 1, C_GAO:C_GAO + AW], everything)
        step(8, M[R_G2:R_G2 + 1, :], everything)
        for r in range(3):
            step(9, fw[r:r + 1, :], (r, slice(None), slice(None)))
        step(10, S[3:4, 0:2 * DFF], everything)
        sq = S[:, C_SQ:C_SQ + 128]
        loss_o[...] = jnp.sum(jnp.sum(sq, axis=1, keepdims=True), axis=0, keepdims=True) * (0.5 / D)

    def full(a):
        n = len(a.shape)
        return pl.BlockSpec(a.shape, lambda i, chip_ref: (0,) * n)

    params = [*ws, *ms, *vs]
    out = _call(
        body, (p_all, p_all, pm_all, pm_all, g1_all, tbl_all, *params), name="small_adam", grid=(1,), prefetch=(chip,),
        in_specs=[full(p_all),
                  pl.BlockSpec((N_DEV, 8, fw_cols), lambda i, chip_ref: (0, 0, chip_ref[0])),
                  full(pm_all),
                  pl.BlockSpec((N_DEV, 8, cw_cols), lambda i, chip_ref: (0, 0, chip_ref[0])),
                  full(g1_all), full(tbl_all), *[full(a) for a in params]],
        out_specs=[full(a) for a in ws] * 4 + [pl.BlockSpec((1, 1), lambda i, chip_ref: (0, 0))],
        out_shape=[SDS(a.shape, F32) for a in ws] * 4 + [SDS((1, 1), F32)], sem=("arbitrary",), vmem_mib=32)
    return out[0:N_SMALL], out[N_SMALL:2 * N_SMALL], out[2 * N_SMALL:3 * N_SMALL], out[3 * N_SMALL:4 * N_SMALL], out[4 * N_SMALL]


PLACE_STEPS = 4


def _place_specs(shards):
    rows = [s.shape[0] // PLACE_STEPS for s in shards]
    return ([pl.BlockSpec((r, D), lambda i, chip_ref: (i, 0)) for r in rows],
            [pl.BlockSpec((r, D), lambda i, chip_ref: (chip_ref[0] * PLACE_STEPS + i, 0)) for r in rows],
            [SDS((N_CHIPS * s.shape[0], D), BF) for s in shards])


def _place_first(chip, shard, conv_w, ffn_conv_w):
    def body(chip_ref, a, s0, s1, o, t0, t1):
        o[...] = a[...].astype(BF)

        @pl.when(pl.program_id(0) == 0)
        def _():
            for s, t in ((s0, t0), (s1, t1)):
                t[...] = jnp.zeros_like(t)
                t[0, 0:3, :] = s[...]

    ins, outs, shapes = _place_specs([shard])
    taps = (conv_w, ffn_conv_w)
    return _call(
        body, (shard, conv_w, ffn_conv_w), name="place_first", grid=(PLACE_STEPS,), prefetch=(chip,),
        in_specs=ins + [pl.BlockSpec(s.shape, lambda i, chip_ref: (0, 0)) for s in taps],
        out_specs=outs + [pl.BlockSpec((1, 8, s.shape[1]), lambda i, chip_ref: (chip_ref[0], 0, 0)) for s in taps],
        out_shape=shapes + [SDS((N_CHIPS, 8, s.shape[1]), F32) for s in taps],
        sem=("arbitrary",), vmem_mib=32, free=(1, 2))


def _place_rest(chip, shards, w_up, table, bucket, comm):
    n = len(shards)
    c_up = w_up.shape[1]
    edges = [round(k * (c_up // 128) / PLACE_STEPS) * 128 for k in range(PLACE_STEPS + 1)]

    def body(chip_ref, *refs):
        a, (up_ref, tab_ref, bk_ref), o = refs[:n], refs[n:n + 3], refs[n + 3:2 * n + 3]
        up_o, bias_ref = refs[2 * n + 3:]
        for src, dst in zip(a, o):
            dst[...] = src[...].astype(BF)
        for k in range(PLACE_STEPS):
            @pl.when(pl.program_id(0) == k)
            def _(k=k):
                up_o[edges[k]:edges[k + 1], :] = up_ref[:, edges[k]:edges[k + 1]].T.astype(BF)

        @pl.when(pl.program_id(0) == 0)
        def _():
            bk = bk_ref[...]
            eq = [bk == b for b in range(NBUCKET)]
            for h in range(NH):
                acc = jnp.zeros((BLK, 2 * BLK), F32)
                for b in range(NBUCKET):
                    acc = jnp.where(eq[b], tab_ref[h, b], acc)
                bias_ref[h * BLK:(h + 1) * BLK, :] = acc

    ins, outs, shapes = _place_specs(shards)
    return _call(
        body, (*shards, w_up, table, bucket), name="place_rest", grid=(PLACE_STEPS,), prefetch=(chip,),
        in_specs=ins + [_resident(w_up.shape), pl.BlockSpec(memory_space=pltpu.SMEM),
                        pl.BlockSpec(bucket.shape, lambda i, chip_ref: (0, 0))],
        out_specs=outs + [pl.BlockSpec((c_up, D), lambda i, chip_ref: (chip_ref[0], 0)),
                          pl.BlockSpec((NH * BLK, 2 * BLK), lambda i, chip_ref: (0, 0))],
        out_shape=shapes + [SDS((N_CHIPS * c_up, D), BF), SDS((NH * BLK, 2 * BLK), F32)],
        sem=("arbitrary",), vmem_mib=32, comm=comm, free=(n + 1, n + 2))


def kernel(x, norm_mix_g, w_in, conv_w, q_norm_g, k_norm_g, rel_bias_table, sinks, out_norm_conv_g, out_norm_attn_g, w_out, norm_ffn_g, w_up, ffn_conv_w, ffn_conv_b, w_down, loss_target, m_norm_mix_g, m_w_in, m_conv_w, m_q_norm_g, m_k_norm_g, m_rel_bias_table, m_sinks, m_out_norm_conv_g, m_out_norm_attn_g, m_w_out, m_norm_ffn_g, m_w_up, m_ffn_conv_w, m_ffn_conv_b, m_w_down, v_norm_mix_g, v_w_in, v_conv_w, v_q_norm_g, v_k_norm_g, v_rel_bias_table, v_sinks, v_out_norm_conv_g, v_out_norm_attn_g, v_w_out, v_norm_ffn_g, v_w_up, v_ffn_conv_w, v_ffn_conv_b, v_w_down):
    as_arg = lambda i: jnp.reshape(i, (1,)).astype(jnp.int32)
    chip = as_arg(2 * lax.axis_index("x") + lax.axis_index("y"))
    core = as_arg(lax.axis_index("c"))
    me = 2 * chip + core
    xs, tgt = x[0], loss_target[0]
    qg, kg, gco, gao, g1, g2, fb = q_norm_g, k_norm_g, out_norm_conv_g, out_norm_attn_g, norm_mix_g, norm_ffn_g, ffn_conv_b
    pieces = lambda g: g.reshape(N_DEV, g.shape[0] // N_DEV, D)
    whole = lambda f: f.reshape(2 * f.shape[1], D)

    bucket = jnp.asarray(_bucket_table())
    p_in, p_cw, p_fw = _place_first(chip, w_in[0].T, conv_w[0], ffn_conv_w[0])
    p_out, p_down, p_up, bias, w_int, cw_all, fw_all = _place_rest(
        chip, [w_out[0], w_down[0]], w_up[0], rel_bias_table.T, bucket,
        comm=[_t_gather(p_in, relayed_first=True), _t_small_weights(p_cw), _t_small_weights(p_fw)])
    cw8 = jnp.transpose(cw_all, (1, 0, 2)).reshape(8, CW)
    fw8 = jnp.transpose(fw_all, (1, 0, 2)).reshape(8, 2 * DFF)

    early = 3 / 11
    proj, u1, w_out_f, p_up = _inproj(xs, g1, w_int, comm=[_t_gather(p_out), _t_gather(p_up, (0, early))])
    y, w_upt = _mix_fwd(proj, sinks, cw8, qg, kg, gco, gao, bias, comm=[_t_gather(p_up, (early, 1))])
    h1, u2 = _outproj(y, w_out_f, xs, g2)
    up, = _ffn_up(u2, w_upt)
    a, w_down_f = _ffn_act(up, fw8, fb, comm=[_t_gather(p_down)])
    dh2, dh2b, sq = _ffn_down(a, w_down_f, h1, tgt)

    gdbf, = _wgrad("wgrad_down", [a], dh2b, None)
    da, sib_down = _ffn_down_bwd(dh2b, w_down_f, comm=[_t_sibling(pieces(gdbf))])
    pbf_down, own_down = _chip_sum("chip_sum_w_down", pieces(gdbf), sib_down, core, chip)
    dug, duv, dfwg, dfwv, dfbg, dfbv, chips_down = _ffn_act_bwd(up, da, fw8, fb, comm=[_t_chips(pbf_down)])
    fin_down, = _final_sum("final_sum_w_down", own_down, chips_down, core)
    gubf, = _wgrad("wgrad_up", [dug, duv], u2, False)
    p_all = _pack_ffn(me, dfwg, dfwv, dfbg, dfbv, sq)
    dh1, dh1b, dg2, sib_up, fin_down, p_all = _norm_matmul_bwd(
        "ffn_up_bwd", [dug, duv], w_upt, [0, DFF], h1, g2, dh2, True,
        comm=[_t_sibling(pieces(gubf)), _t_swap(fin_down), _t_allgather(p_all)])
    pbf_up, own_up = _chip_sum("chip_sum_w_up", pieces(gubf), sib_up, core, chip)
    gobf, = _wgrad("wgrad_out", [y], dh1b, True)
    dy, sib_out = _out_bwd(dh1b, w_out_f, comm=[_t_sibling(pieces(gobf))])
    pbf_out, own_out = _chip_sum("chip_sum_w_out", pieces(gobf), sib_out, core, chip)
    dproj, dcw8, dqg, dkg, dgco, dgao, dsink, dbias, chips_up, chips_out = _mix_bwd(
        proj, dy, sinks, cw8, qg, kg, gco, gao, bias, comm=[_t_chips(pbf_up), _t_chips(pbf_out)])
    fin_up, = _final_sum("final_sum_w_up", own_up, chips_up, core)
    tbl_all = _band_bias_bwd(dbias, bucket, me)
    pm_all = _pack_mix(me, dg2, dgco, dgao, dcw8, dqg, dkg, dsink)
    gibf, fin_up, pm_all, tbl_all = _wgrad(
        "wgrad_in", [dproj], u1, False, comm=[_t_swap(fin_up), _t_allgather(pm_all), _t_allgather(tbl_all)])
    fin_out, sib_in = _final_sum("final_sum_w_out", own_out, chips_out, core, comm=[_t_sibling(pieces(gibf))])
    pbf_in, own_in = _chip_sum("chip_sum_w_in", pieces(gibf), sib_in, core, chip)
    dx, g1_all, chips_in, fin_out = _norm_matmul_bwd(
        "in_bwd", [dproj], w_int, [0], xs, g1, dh1, False, comm=[_t_chips(pbf_in), _t_swap(fin_out)], slot=me)
    fin_in, = _final_sum("final_sum_w_in", own_in, chips_in, core)
    g1_all, fin_in = _comm_call("gather_last", [_t_allgather(g1_all), _t_swap(fin_in)])

    g_w_out, g_w_down = whole(fin_out), whole(fin_down)
    g_w_down, d_down, nm_down, nv_down = _adamw("adamw_w_down", w_down[0], g_w_down, m_w_down[0], v_w_down[0], 352, True)
    g_w_up, d_up, nm_up, nv_up = _adamw(
        "adamw_w_up", w_up[0], whole(fin_up), m_w_up[0], v_w_up[0], 256, True, stage=False, g_transposed=True)
    g_w_out, d_out, nm_out, nv_out = _adamw("adamw_w_out", w_out[0], g_w_out, m_w_out[0], v_w_out[0], 256, True, stage=False)
    g_w_in, d_in, nm_in, nv_in = [a.T for a in _adamw(
        "adamw_w_in", w_in[0].T, whole(fin_in), m_w_in[0].T, v_w_in[0].T, INW // N_CHIPS // 3, True)]
    taps = lambda a: jnp.transpose(a, (1, 0, 2))
    sw = [norm_mix_g, taps(conv_w), q_norm_g, k_norm_g, rel_bias_table.T, sinks, out_norm_conv_g, out_norm_attn_g,
          norm_ffn_g, taps(ffn_conv_w), ffn_conv_b]
    smm = [m_norm_mix_g, taps(m_conv_w), m_q_norm_g, m_k_norm_g, m_rel_bias_table.T, m_sinks, m_out_norm_conv_g,
           m_out_norm_attn_g, m_norm_ffn_g, taps(m_ffn_conv_w), m_ffn_conv_b]
    smv = [v_norm_mix_g, taps(v_conv_w), v_q_norm_g, v_k_norm_g, v_rel_bias_table.T, v_sinks, v_out_norm_conv_g,
           v_out_norm_attn_g, v_norm_ffn_g, taps(v_ffn_conv_w), v_ffn_conv_b]
    *small_out, loss = _small_adam(chip, p_all, pm_all, g1_all, tbl_all, sw, smm, smv)
    sg, sd, snm, snv = [list(r) for r in small_out]
    for r in (sg, sd, snm, snv):
        r[1], r[4], r[9] = taps(r[1]), r[4].T, taps(r[9])

    def order(s, b_in, b_out, b_up, b_down):
        return (s[0], b_in[None], s[1], s[2], s[3], s[4], s[5], s[6], s[7], b_out[None], s[8], b_up[None],
                s[9], s[10], b_down[None])

    return (loss.reshape(()), dx[None],
            *order(sg, g_w_in, g_w_out, g_w_up, g_w_down),
            *order(sd, d_in, d_out, d_up, d_down),
            *order(snm, nm_in, nm_out, nm_up, nm_down),
            *order(snv, nv_in, nv_out, nv_up, nv_down))
```

```python
import functools
import math

import numpy as np

import jax
import jax.numpy as jnp
from jax import lax
from jax.experimental import pallas as pl
from jax.experimental.pallas import tpu as pltpu

F32 = jnp.float32
BF = jnp.bfloat16
SDS = jax.ShapeDtypeStruct

T = 2048
D = 1024
CW = 512
AW = 512
HD = 64
NH = 8
NKV = 2
GQ = 4
INW = 2304
DFF = 2816
BLK = 128
NB = T // BLK
NBUCKET = 32
EPS = 1e-6
NEG_INF = -1e30
N_CHIPS = 4
N_DEV = 8

ADAM_LR = 0.001
ADAM_B1 = 0.9
ADAM_B2 = 0.999
ADAM_EPS = 1e-08
ADAM_WD = 0.01
ADAM_STEP = 10

TM = 512
MIB = 1024 * 1024
MESH = pl.DeviceIdType.MESH
ANY = pl.BlockSpec(memory_space=pl.ANY)

_pcall = pl.pallas_call


def _params(sem=None, vmem_mib=None, collective_id=None):
    kw = {} if collective_id is None else {"collective_id": collective_id}
    if sem is not None:
        kw["dimension_semantics"] = sem
    if vmem_mib is not None:
        kw["vmem_limit_bytes"] = vmem_mib * MIB
    return pltpu.CompilerParams(**kw)


def _resident(shape):
    return pl.BlockSpec(shape, lambda *_: (0,) * len(shape), pipeline_mode=pl.Buffered(1))


def _dot(a, b, ca, cb):
    return lax.dot_general(a, b, (((ca,), (cb,)), ((), ())), preferred_element_type=F32)


def _rms_bwd(dy, x, r, g):
    dg = jnp.sum(dy * (x * r), axis=0, keepdims=True)
    dgx = dy * g
    dx = r * dgx - x * (r * r * r) * jnp.mean(x * dgx, axis=-1, keepdims=True)
    return dx, dg


def _where():
    x, y, c = lax.axis_index("x"), lax.axis_index("y"), lax.axis_index("c")
    return x, y, c, [(1 - x, y), (x, 1 - y), (1 - x, 1 - y)]


def _rcopy(src, dst, ssem, rsem, dev):
    return pltpu.make_async_remote_copy(src_ref=src, dst_ref=dst, send_sem=ssem, recv_sem=rsem, device_id=dev,
                                        device_id_type=MESH)


SIBLING, Y_CHIP, X_CHIP, DIAGONAL_CHIP = 1, 2, 4, 6
OTHER_CHIPS = (Y_CHIP, X_CHIP, DIAGONAL_CHIP)
EVERYONE = tuple(range(1, N_DEV))
BARRIER_OF = {(SIBLING,): 0, (SIBLING, Y_CHIP, X_CHIP): 1, OTHER_CHIPS: 2, (SIBLING,) + OTHER_CHIPS: 3, EVERYONE: 4}


def _peer(rel):
    x, y, c, _ = _where()
    return x ^ ((rel >> 2) & 1), y ^ ((rel >> 1) & 1), c ^ (rel & 1)


class _Task:
    def __init__(self, ins, outs, alias, n_sem, start, finish, middle=None, peers=()):
        self.ins, self.outs, self.alias, self.n_sem, self.start, self.finish = ins, outs, alias, n_sem, start, finish
        self.middle = middle if middle is not None else (lambda *args: None)
        self.peers = peers


def _peers_of(comm):
    return tuple(sorted({p for t in comm for p in t.peers}))


def _enter(comm):
    peers = _peers_of(comm)
    barrier = pltpu.get_barrier_semaphore()
    for rel in peers:
        pl.semaphore_signal(barrier, inc=1, device_id=_peer(rel), device_id_type=MESH)
    pl.semaphore_wait(barrier, len(peers))


ROWS16 = 16


def _t_gather(placed, part=(0, 1), relayed_first=False):
    R = placed.shape[0] // N_CHIPS
    q = R // 4
    lo, hi = (round(f * (q // ROWS16)) * ROWS16 for f in part)

    def quarter(chip_index, core, k):
        return pl.ds(pl.multiple_of(chip_index * R + core * 2 * q + k * q + lo, ROWS16), hi - lo)

    def places():
        x, y, c, _ = _where()
        return c, 2 * x + y, 2 * (1 - x) + y, 2 * x + (1 - y), 2 * (1 - x) + (1 - y), (1 - x, y, c), (x, 1 - y, c), (x, y, 1 - c)

    def copy(buf, k, chip_index, core, quart, ss, rs, b, dev):
        window = buf.at[quarter(chip_index, core, quart)]
        return _rcopy(window, window, ss.at[b + k], rs.at[b + k], dev)

    def first_hop(cout, ss, rs, b, which):
        c, me, _, _, _, x_nbr, y_nbr, _ = places()
        for k, (quart, dev) in enumerate(((0, x_nbr), (1, y_nbr), (1, x_nbr), (0, y_nbr))):
            if k in which:
                copy(cout[0], k, me, c, quart, ss, rs, b, dev).start()

    def start(cin, cout, ss, rs, b):
        first_hop(cout, ss, rs, b, (0, 1) if relayed_first else (0, 1, 2, 3))

    def middle(cin, cout, ss, rs, b):
        c, _, xc, yc, _, x_nbr, y_nbr, sib = places()
        for k, chip_index, quart, dev in ((0, xc, 0, y_nbr), (1, yc, 1, x_nbr)):
            copy(cout[0], k, chip_index, c, quart, ss, rs, b, dev).wait_recv()
            copy(cout[0], 4 + k, chip_index, c, quart, ss, rs, b, dev).start()
            copy(cout[0], 6 + k, chip_index, c, quart, ss, rs, b, sib).start()
        if relayed_first:
            first_hop(cout, ss, rs, b, (2, 3))

    later = ((2, 1, 1), (3, 2, 0), (4, 3, 0), (5, 3, 1))

    def finish(cin, cout, ss, rs, b):
        c, me, xc, yc, dc, _, _, sib = places()
        chip_of = {1: xc, 2: yc, 3: dc}
        for k, whose, quart in later:
            copy(cout[0], k, chip_of[whose], c, quart, ss, rs, b, sib).wait_recv()
            copy(cout[0], 6 + k, chip_of[whose], c, quart, ss, rs, b, sib).start()
        for k, whose, quart in ((0, 1, 0), (1, 2, 1)) + later:
            copy(cout[0], 6 + k, chip_of[whose], 1 - c, quart, ss, rs, b, sib).wait_recv()
        for k in range(12):
            copy(cout[0], k, me, c, 0, ss, rs, b, sib).wait_send()

    return _Task([placed], [SDS(placed.shape, placed.dtype)], [(0, 0)], 12, start, finish, middle, peers=(SIBLING, Y_CHIP, X_CHIP))


def _t_small_weights(buf):
    def start(cin, cout, ss, rs, b):
        x, y, c, chips = _where()
        mine = cout[0].at[2 * x + y]
        for r, (px, py) in enumerate(chips):
            _rcopy(mine, mine, ss.at[b + r], rs.at[b + r], (px, py, c)).start()

    def finish(cin, cout, ss, rs, b):
        x, y, c, chips = _where()
        for r, (px, py) in enumerate(chips):
            got = cout[0].at[2 * px + py]
            _rcopy(got, got, ss.at[b + r], rs.at[b + r], (px, py, c)).wait_recv()
        for r, (px, py) in enumerate(chips):
            mine = cout[0].at[2 * x + y]
            _rcopy(mine, mine, ss.at[b + r], rs.at[b + r], (px, py, c)).wait_send()

    return _Task([buf], [SDS(buf.shape, buf.dtype)], [(0, 0)], 3, start, finish, peers=OTHER_CHIPS)


def _t_sibling(gbf):
    def start(cin, cout, ss, rs, b):
        x, y, c, _ = _where()
        for jj in range(N_CHIPS):
            _rcopy(cin[0].at[2 * jj + (1 - c)], cout[0].at[jj], ss.at[b + jj], rs.at[b + jj], (x, y, 1 - c)).start()

    def finish(cin, cout, ss, rs, b):
        x, y, c, _ = _where()
        for jj in range(N_CHIPS):
            got = cout[0].at[jj]
            _rcopy(got, got, ss.at[b + jj], rs.at[b + jj], (x, y, 1 - c)).wait_recv()
        for jj in range(N_CHIPS):
            got = cout[0].at[jj]
            _rcopy(got, got, ss.at[b + jj], rs.at[b + jj], (x, y, 1 - c)).wait_send()

    return _Task([gbf], [SDS((N_CHIPS,) + gbf.shape[1:], BF)], [], N_CHIPS, start, finish, peers=(SIBLING,))


def _t_chips(pbf):
    def start(cin, cout, ss, rs, b):
        x, y, c, chips = _where()
        for r, (px, py) in enumerate(chips):
            _rcopy(cin[0].at[2 * px + py], cout[0].at[r], ss.at[b + r], rs.at[b + r], (px, py, c)).start()

    def finish(cin, cout, ss, rs, b):
        x, y, c, chips = _where()
        for r, (px, py) in enumerate(chips):
            got = cout[0].at[r]
            _rcopy(got, got, ss.at[b + r], rs.at[b + r], (px, py, c)).wait_recv()
        for r, (px, py) in enumerate(chips):
            got = cout[0].at[r]
            _rcopy(got, got, ss.at[b + r], rs.at[b + r], (px, py, c)).wait_send()

    return _Task([pbf], [SDS((3,) + pbf.shape[1:], BF)], [], 3, start, finish, peers=OTHER_CHIPS)


def _t_swap(fin):
    def start(cin, cout, ss, rs, b):
        x, y, c, _ = _where()
        mine = cout[0].at[c]
        _rcopy(mine, mine, ss.at[b], rs.at[b], (x, y, 1 - c)).start()

    def finish(cin, cout, ss, rs, b):
        x, y, c, _ = _where()
        got = cout[0].at[1 - c]
        _rcopy(got, got, ss.at[b], rs.at[b], (x, y, 1 - c)).wait_recv()
        _rcopy(got, got, ss.at[b], rs.at[b], (x, y, 1 - c)).wait_send()

    return _Task([fin], [SDS(fin.shape, fin.dtype)], [(0, 0)], 1, start, finish, peers=(SIBLING,))


def _t_allgather(buf):
    def peers():
        x, y, c, _ = _where()
        out = []
        for rel in range(1, N_DEV):
            px, py, pc = x ^ ((rel >> 2) & 1), y ^ ((rel >> 1) & 1), c ^ (rel & 1)
            out.append((rel - 1, 4 * px + 2 * py + pc, (px, py, pc)))
        return 4 * x + 2 * y + c, out

    def start(cin, cout, ss, rs, b):
        me, ps = peers()
        mine = cout[0].at[me]
        for k, _, dev in ps:
            _rcopy(mine, mine, ss.at[b + k], rs.at[b + k], dev).start()

    def finish(cin, cout, ss, rs, b):
        me, ps = peers()
        for k, pidx, dev in ps:
            got = cout[0].at[pidx]
            _rcopy(got, got, ss.at[b + k], rs.at[b + k], dev).wait_recv()
        for k, _, dev in ps:
            mine = cout[0].at[me]
            _rcopy(mine, mine, ss.at[b + k], rs.at[b + k], dev).wait_send()

    return _Task([buf], [SDS(buf.shape, buf.dtype)], [(0, 0)], N_DEV - 1, start, finish, peers=EVERYONE)


def _run_tasks(comm, which, cin, cout, ss, rs):
    i0 = o0 = s0 = 0
    for t in comm:
        getattr(t, which)(cin[i0:i0 + len(t.ins)], cout[o0:o0 + len(t.outs)], ss, rs, s0)
        i0, o0, s0 = i0 + len(t.ins), o0 + len(t.outs), s0 + t.n_sem


def _from_hbm(*arrays):
    return [pltpu.with_memory_space_constraint(a, pltpu.HBM) for a in arrays]


def _in_hbm(shapes):
    return [pltpu.HBM(s.shape, s.dtype) for s in shapes]


def _comm_layout(comm, n_in, n_out):
    c_in = [a for t in comm for a in t.ins]
    c_out = [s for t in comm for s in t.outs]
    aliases, i0, o0 = {}, 0, 0
    for t in comm:
        for i, o in t.alias:
            aliases[n_in + i0 + i] = n_out + o0 + o
        i0, o0 = i0 + len(t.ins), o0 + len(t.outs)
    return c_in, c_out, aliases, sum(t.n_sem for t in comm)


def _call(body, operands, *, name, grid, in_specs, out_specs, out_shape, scratch_shapes=(), sem=None, vmem_mib=None, comm=(),
          free=(), prefetch=()):
    operands = [o if s.memory_space == pltpu.SMEM or k in free else pltpu.with_memory_space_constraint(o, pltpu.HBM)
                for k, (o, s) in enumerate(zip(operands, in_specs))]
    n_pre, n_in, n_out, n_scr = len(prefetch), len(in_specs), len(out_specs), len(scratch_shapes)
    c_in, c_out, aliases, n_sem = _comm_layout(comm, n_pre + n_in, n_out)
    sems = [pltpu.SemaphoreType.DMA((n_sem,)), pltpu.SemaphoreType.DMA((n_sem,))] if comm else []

    def wrapped(*refs):
        pre, refs = refs[:n_pre], refs[n_pre:]
        ins, cin = refs[:n_in], refs[n_in:n_in + len(c_in)]
        rest = refs[n_in + len(c_in):]
        outs, cout = rest[:n_out], rest[n_out:n_out + len(c_out)]
        rest = rest[n_out + len(c_out):]
        scr, csem = rest[:n_scr], rest[n_scr:]
        if not comm:
            return body(*pre, *ins, *outs, *scr)
        step = functools.reduce(lambda acc, k: acc * grid[k] + pl.program_id(k), range(len(grid)), 0)
        n_steps = math.prod(grid)

        @pl.when(step == 0)
        def _():
            _enter(comm)
            _run_tasks(comm, "start", cin, cout, *csem)

        pl.when(step == n_steps // 2)(lambda: _run_tasks(comm, "middle", cin, cout, *csem))
        body(*pre, *ins, *outs, *scr)
        pl.when(step == n_steps - 1)(lambda: _run_tasks(comm, "finish", cin, cout, *csem))

    grid_spec = pltpu.PrefetchScalarGridSpec(
        num_scalar_prefetch=n_pre, grid=grid, in_specs=list(in_specs) + [ANY] * len(c_in),
        out_specs=list(out_specs) + [ANY] * len(c_out), scratch_shapes=list(scratch_shapes) + sems)
    return _pcall(
        wrapped, name=name, grid_spec=grid_spec, out_shape=_in_hbm(list(out_shape) + c_out), input_output_aliases=aliases,
        compiler_params=_params(("arbitrary",) * len(grid) if comm else sem, vmem_mib,
                                BARRIER_OF[_peers_of(comm)] if comm else None),
    )(*prefetch, *operands, *_from_hbm(*c_in))


def _comm_call(name, comm):
    c_in, c_out, aliases, n_sem = _comm_layout(comm, 0, 0)

    def body(*refs):
        cin, cout, (ss, rs) = refs[:len(c_in)], refs[len(c_in):len(c_in) + len(c_out)], refs[len(c_in) + len(c_out):]
        _enter(comm)
        for phase in ("start", "middle", "finish"):
            _run_tasks(comm, phase, cin, cout, ss, rs)

    return _pcall(
        body, name=name, in_specs=[ANY] * len(c_in), out_specs=[ANY] * len(c_out), out_shape=_in_hbm(c_out),
        scratch_shapes=[pltpu.SemaphoreType.DMA((n_sem,)), pltpu.SemaphoreType.DMA((n_sem,))],
        input_output_aliases=aliases, compiler_params=_params(collective_id=BARRIER_OF[_peers_of(comm)]),
    )(*_from_hbm(*c_in))


def _inproj(x, g1, w_int, comm=()):
    tm = TM

    def body(x_ref, g_ref, w_ref, proj_ref, u_ref):
        xf = x_ref[...]
        r = lax.rsqrt(jnp.mean(xf * xf, axis=-1, keepdims=True) + EPS)
        u = (xf * r * g_ref[...]).astype(BF)
        u_ref[...] = u
        proj_ref[...] = _dot(u, w_ref[...], 1, 1)

    return _call(
        body, (x, g1, w_int), name="inproj", grid=(T // tm,),
        in_specs=[pl.BlockSpec((tm, D), lambda i: (i, 0)), pl.BlockSpec((1, D), lambda i: (0, 0)),
                  _resident((INW, D))],
        out_specs=[pl.BlockSpec((tm, INW), lambda i: (i, 0)), pl.BlockSpec((tm, D), lambda i: (i, 0))],
        out_shape=[SDS((T, INW), F32), SDS((T, D), BF)], sem=("parallel",), vmem_mib=40, comm=comm, free=(0, 1))


def _outproj(y, w_out, x, g2):
    tm = TM

    def body(y_ref, w_ref, x_ref, g_ref, h1_ref, u2_ref):
        h1 = x_ref[...] + _dot(y_ref[...], w_ref[...], 1, 0)
        h1_ref[...] = h1
        r = lax.rsqrt(jnp.mean(h1 * h1, axis=-1, keepdims=True) + EPS)
        u2_ref[...] = (h1 * r * g_ref[...]).astype(BF)

    return _call(
        body, (y, w_out, x, g2), name="outproj", grid=(T // tm,),
        in_specs=[pl.BlockSpec((tm, D), lambda i: (i, 0)), _resident((D, D)),
                  pl.BlockSpec((tm, D), lambda i: (i, 0)), pl.BlockSpec((1, D), lambda i: (0, 0))],
        out_specs=[pl.BlockSpec((tm, D), lambda i: (i, 0)), pl.BlockSpec((tm, D), lambda i: (i, 0))],
        out_shape=[SDS((T, D), F32), SDS((T, D), BF)], sem=("parallel",), vmem_mib=32, free=(2, 3))


def _ffn_up(u2, w_upt, comm=()):
    tm, tn = 1024, 512

    def body(u_ref, w_ref, o_ref):
        o_ref[...] = _dot(u_ref[...], w_ref[...], 1, 1).astype(BF)

    return _call(
        body, (u2, w_upt), name="ffn_up", grid=(T // tm, 2 * DFF // tn),
        in_specs=[pl.BlockSpec((tm, D), lambda i, j: (i, 0)), pl.BlockSpec((tn, D), lambda i, j: (j, 0))],
        out_specs=[pl.BlockSpec((tm, tn), lambda i, j: (i, j))], out_shape=[SDS((T, 2 * DFF), BF)],
        sem=("parallel", "parallel"), vmem_mib=32, comm=comm, free=(1,))


def _ffn_down(a, w_down, h1, tgt):
    tm = TM

    def body(a_ref, w_ref, h1_ref, t_ref, dh_ref, dhb_ref, l_ref):
        @pl.when(pl.program_id(0) == 0)
        def _():
            l_ref[...] = jnp.zeros_like(l_ref)

        h2 = h1_ref[...] + _dot(a_ref[...], w_ref[...], 1, 0)
        e = h2 - t_ref[...]
        dh = e * (1.0 / D)
        dh_ref[...] = dh
        dhb_ref[...] = dh.astype(BF)
        e2 = jnp.sum((e * e).reshape(tm // 8, 8, D), axis=0)
        acc = e2[:, 0:128]
        for k in range(1, D // 128):
            acc = acc + e2[:, k * 128:(k + 1) * 128]
        l_ref[...] += acc

    return _call(
        body, (a, w_down, h1, tgt), name="ffn_down", grid=(T // tm,),
        in_specs=[pl.BlockSpec((tm, DFF), lambda i: (i, 0)), _resident((DFF, D)),
                  pl.BlockSpec((tm, D), lambda i: (i, 0)), pl.BlockSpec((tm, D), lambda i: (i, 0))],
        out_specs=[pl.BlockSpec((tm, D), lambda i: (i, 0)), pl.BlockSpec((tm, D), lambda i: (i, 0)),
                   pl.BlockSpec((8, 128), lambda i: (0, 0))],
        out_shape=[SDS((T, D), F32), SDS((T, D), BF), SDS((8, 128), F32)], sem=("arbitrary",), vmem_mib=40, free=(2, 3))


def _bucket_table():
    q = np.arange(BLK, dtype=np.int32)[:, None]
    j = np.arange(2 * BLK, dtype=np.int32)[None, :]
    n = np.maximum(q + BLK - j, 0)
    nf = np.maximum(n, 1).astype(np.float32)
    max_exact = NBUCKET // 2
    large = max_exact + (np.log(nf / np.float32(max_exact)) / np.float32(math.log(BLK / max_exact))
                         * np.float32(NBUCKET - max_exact)).astype(np.int32)
    large = np.minimum(large, NBUCKET - 1)
    return np.where(n < max_exact, n, large).astype(np.int32)


def _band_bias_bwd(dbias, bucket, me):
    def body(me_ref, db_ref, bk_ref, o_ref):
        bk = bk_ref[...]
        for b in range(NBUCKET):
            m = bk == b
            for h in range(NH):
                v = jnp.where(m, db_ref[h * BLK:(h + 1) * BLK, :], 0.0)
                s = jnp.sum(jnp.sum(v, axis=1, keepdims=True), axis=0, keepdims=True)
                o_ref[0, h:h + 1, b:b + 1] = s

    grid_spec = pltpu.PrefetchScalarGridSpec(
        num_scalar_prefetch=1, grid=(1,),
        in_specs=[pl.BlockSpec((NH * BLK, 2 * BLK), lambda i, me_ref: (0, 0)),
                  pl.BlockSpec((BLK, 2 * BLK), lambda i, me_ref: (0, 0))],
        out_specs=pl.BlockSpec((1, NH, NBUCKET), lambda i, me_ref: (me_ref[0], 0, 0)),
    )
    return _pcall(body, name="band_bias_bwd", grid_spec=grid_spec, out_shape=SDS((N_DEV, NH, NBUCKET), F32),
                  compiler_params=_params(("arbitrary",)))(me, dbias, bucket)


def _two_bf16(x):
    hi = x.astype(BF)
    return hi, (x - hi.astype(F32)).astype(BF)


def _head_sums(x, seg):
    hi, lo = _two_bf16(x)
    s = seg[0:x.shape[1], :]
    return _dot(hi, s, 1, 0) + _dot(lo, s, 1, 0)


def _head_spread(v, seg, width):
    hi, lo = _two_bf16(v)
    s = seg[0:width, :]
    return _dot(hi, s, 1, 1) + _dot(lo, s, 1, 1)


def _head_norm(x, g_t, seg, by_head=False):
    if by_head:
        heads = [x[:, h * HD:(h + 1) * HD] for h in range(x.shape[1] // HD)]
        r = jnp.concatenate([jnp.broadcast_to(lax.rsqrt(jnp.mean(v * v, axis=-1, keepdims=True) + EPS), v.shape)
                             for v in heads], axis=1)
    else:
        r = lax.rsqrt(_head_sums(x * x, seg) * (1.0 / HD) + EPS)
        r = _head_spread(r, seg, x.shape[1])
    return x * r * g_t, r


def _head_norm_bwd(dy, x, r, g_t, seg):
    dg_t = jnp.sum(dy * (x * r), axis=0, keepdims=True)
    dgx = dy * g_t
    mean = _head_spread(_head_sums(x * dgx, seg) * (1.0 / HD), seg, x.shape[1])
    return r * dgx - x * (r * r * r) * mean, dg_t


def _fold_heads(v):
    out = v[:, 0:HD]
    for h in range(1, v.shape[1] // HD):
        out = out + v[:, h * HD:(h + 1) * HD]
    return out


def _mix_forward(P, zc8, zh8, pkv, first, cw, qg_t, kg_t, gco, gao, seg, sink_ref, bias_ref, by_head=False):
    gate_b = P[:, 0:CW]
    gate_c = P[:, CW:2 * CW]
    hc = P[:, 2 * CW:3 * CW]
    z = gate_c * hc
    keep = jnp.where(first, 0.0, 1.0)
    zp = zc8 * zh8 * keep
    p1 = zp[7:8, :]
    p2 = zp[6:7, :]
    row = lax.broadcasted_iota(jnp.int32, (BLK, 1), 0)
    z1 = jnp.where(row == 0, p1, pltpu.roll(z, 1, 0))
    z2 = jnp.where(row == 0, p2, jnp.where(row == 1, p1, pltpu.roll(z, 2, 0)))
    cz = cw[0:1, :] * z2 + cw[1:2, :] * z1 + cw[2:3, :] * z
    y_conv = gate_b * cz

    scale = HD ** -0.5
    qi = lax.broadcasted_iota(jnp.int32, (BLK, 2 * BLK), 0)
    kj = lax.broadcasted_iota(jnp.int32, (BLK, 2 * BLK), 1)
    dd = qi + BLK - kj
    first_key = jnp.where(first, BLK, 0)
    valid = (dd >= 0) & (dd < BLK) & (kj >= first_key)

    q0 = 3 * CW
    k0 = q0 + AW
    v0 = k0 + NKV * HD
    q_raw = P[:, q0:k0]
    qn, rq = _head_norm(q_raw, qg_t, seg, by_head)
    qs = (qn * scale).astype(BF)
    k_raw = jnp.concatenate([pkv[:, 0:NKV * HD], P[:, k0:v0]], axis=0)
    kn, rk = _head_norm(k_raw, kg_t, seg, by_head)
    knb = kn.astype(BF)
    heads = []
    for h in range(NH):
        kv = h // GQ
        kb = knb[:, kv * HD:(kv + 1) * HD]
        vb = jnp.concatenate([pkv[:, NKV * HD + kv * HD:NKV * HD + (kv + 1) * HD],
                              P[:, v0 + kv * HD:v0 + (kv + 1) * HD]], axis=0).astype(BF)
        Q = qs[:, h * HD:(h + 1) * HD]
        S = _dot(Q, kb, 1, 1) + bias_ref[h * BLK:(h + 1) * BLK, :]
        S = jnp.where(valid, S, NEG_INF)
        sink = sink_ref[0, h]
        m = jnp.maximum(jnp.max(S, axis=-1, keepdims=True), sink)
        p = jnp.exp(S - m)
        es = jnp.exp(sink - m)
        denom = jnp.sum(p, axis=-1, keepdims=True) + es
        probs = p / denom
        O = _dot(probs.astype(BF), vb, 1, 0)
        heads.append(dict(kb=kb, vb=vb, Q=Q, probs=probs, psink=es / denom, O=O))
    y_attn = jnp.concatenate([hd["O"] for hd in heads], axis=1)

    rc = lax.rsqrt(jnp.mean(y_conv * y_conv, axis=-1, keepdims=True) + EPS)
    ra = lax.rsqrt(jnp.mean(y_attn * y_attn, axis=-1, keepdims=True) + EPS)
    y = jnp.concatenate([y_conv * rc * gco, y_attn * ra * gao], axis=1)
    return dict(gate_b=gate_b, gate_c=gate_c, hc=hc, z=z, z1=z1, z2=z2, cz=cz, y_conv=y_conv, y_attn=y_attn,
                rc=rc, ra=ra, heads=heads, y=y, row=row, scale=scale, q_raw=q_raw, rq=rq, k_raw=k_raw, rk=rk)


BPS = 2
TILE = BPS * BLK
KV0 = 3 * CW + AW


def _mix_in_specs(tile_of):
    return [
        pl.BlockSpec(memory_space=pltpu.SMEM),
        pl.BlockSpec((TILE, INW), lambda s: (tile_of(s), 0)),
        pl.BlockSpec((8, CW), lambda s: (jnp.maximum(tile_of(s) * (TILE // 8) - 1, 0), 1)),
        pl.BlockSpec((8, CW), lambda s: (jnp.maximum(tile_of(s) * (TILE // 8) - 1, 0), 2)),
        pl.BlockSpec((BLK, 2 * NKV * HD), lambda s: (jnp.maximum(tile_of(s) * BPS - 1, 0), KV0 // (2 * NKV * HD))),
    ]


def _block_inputs(tile, b, zc_ref, zh_ref, pkv_ref, first_tile):
    P = tile[b * BLK:(b + 1) * BLK, :]
    if b == 0:
        return P, zc_ref[...], zh_ref[...], pkv_ref[...], first_tile
    lo = b * BLK
    return P, tile[lo - 8:lo, CW:2 * CW], tile[lo - 8:lo, 2 * CW:3 * CW], tile[lo - BLK:lo, KV0:KV0 + 2 * NKV * HD], False


def _mix_param_specs():
    return [
        pl.BlockSpec((8, CW), lambda s: (0, 0)),
        pl.BlockSpec((1, AW), lambda s: (0, 0)),
        pl.BlockSpec((1, NKV * HD), lambda s: (0, 0)),
        pl.BlockSpec((1, CW), lambda s: (0, 0)),
        pl.BlockSpec((1, AW), lambda s: (0, 0)),
        pl.BlockSpec((AW, 128), lambda s: (0, 0)),
        pl.BlockSpec((NH * BLK, 2 * BLK), lambda s: (0, 0)),
    ]


def _mix_params(cw8, qg, kg, gco, gao, bias):
    seg = np.zeros((AW, 128), np.float32)
    seg[np.arange(AW), np.arange(AW) // HD] = 1.0
    return (cw8, jnp.tile(qg, (1, NH)), jnp.tile(kg, (1, NKV)), gco, gao, jnp.asarray(seg, BF), bias)


def _mix_fwd(proj, sinks, cw8, qg, kg, gco, gao, bias, comm=()):
    def body(sink_ref, p_ref, zc_ref, zh_ref, pkv_ref, cw_ref, qg_ref, kg_ref, gco_ref, gao_ref, seg_ref, bias_ref, y_ref):
        tile = p_ref[...]
        for b in range(BPS):
            f = _mix_forward(*_block_inputs(tile, b, zc_ref, zh_ref, pkv_ref, pl.program_id(0) == 0), cw_ref[...],
                             qg_ref[...], kg_ref[...], gco_ref[...], gao_ref[...], seg_ref[...], sink_ref, bias_ref, by_head=True)
            y_ref[b * BLK:(b + 1) * BLK, :] = f["y"].astype(BF)

    return _call(
        body, (sinks, proj, proj, proj, proj, *_mix_params(cw8, qg, kg, gco, gao, bias)), name="mix_fwd", grid=(T // TILE,),
        in_specs=_mix_in_specs(lambda s: s) + _mix_param_specs(),
        out_specs=[pl.BlockSpec((TILE, D), lambda s: (s, 0))], out_shape=[SDS((T, D), BF)],
        sem=("parallel",), vmem_mib=40, comm=comm, free=tuple(range(5, 12)))


def _mix_bwd(proj, dy, sinks, cw8, qg, kg, gco, gao, bias, comm=()):
    n_steps = T // TILE

    def tile_of(s):
        return n_steps - 1 - s

    def body(sink_ref, p_ref, zc_ref, zh_ref, pkv_ref, dy_ref, cw_ref, qg_ref, kg_ref, gco_ref, gao_ref, seg_ref, bias_ref,
             dproj_ref, dcw_ref, dqg_ref, dkg_ref, dgco_ref, dgao_ref, dsink_ref, dbias_ref,
             ndcz_ref, dkc_ref, dvc_ref):
        s = pl.program_id(0)

        @pl.when(s == 0)
        def _():
            for r in (dcw_ref, dqg_ref, dkg_ref, dgco_ref, dgao_ref, dsink_ref, dbias_ref, ndcz_ref, dkc_ref, dvc_ref):
                r[...] = jnp.zeros_like(r)

        params = (cw_ref[...], qg_ref[...], kg_ref[...], gco_ref[...], gao_ref[...], seg_ref[...])
        tile = p_ref[...]
        carry = (ndcz_ref[...], dkc_ref[...], dvc_ref[...])
        total = None
        for b in reversed(range(BPS)):
            f = _mix_forward(*_block_inputs(tile, b, zc_ref, zh_ref, pkv_ref, s == n_steps - 1), *params, sink_ref, bias_ref)
            pieces, sums, carry = one_block(f, dy_ref[b * BLK:(b + 1) * BLK, :], params, carry)
            for lo, piece in pieces:
                dproj_ref[b * BLK:(b + 1) * BLK, lo:lo + piece.shape[1]] = piece
            total = sums if total is None else [t + v for t, v in zip(total, sums)]
        ndcz_ref[...], dkc_ref[...], dvc_ref[...] = carry
        dcw, dqg_t, dkg_t, dgco, dgao, dsink, *ds = total
        dcw_ref[0:3, :] += dcw
        dqg_ref[...] += _fold_heads(dqg_t)
        dkg_ref[...] += _fold_heads(dkg_t)
        dgco_ref[...] += dgco
        dgao_ref[...] += dgao
        dsink_ref[...] += dsink
        for h in range(NH):
            dbias_ref[h * BLK:(h + 1) * BLK, :] += ds[h]

    def one_block(f, dy, params, carry):
        cw, qg_v, kg_v, gco_v, gao_v, seg = params
        nxt, dk_carry, dv_carry = carry
        dyc, dgco = _rms_bwd(dy[:, 0:CW], f["y_conv"], f["rc"], gco_v)
        dya, dgao = _rms_bwd(dy[:, CW:CW + AW], f["y_attn"], f["ra"], gao_v)

        row = f["row"]
        dgate_b = dyc * f["cz"]
        dcz = dyc * f["gate_b"]
        dcw = jnp.concatenate([jnp.sum(dcz * f[k], axis=0, keepdims=True) for k in ("z2", "z1", "z")], axis=0)
        n0 = nxt[0:1, :]
        n1 = nxt[1:2, :]
        d1 = jnp.where(row == BLK - 1, n0, pltpu.roll(dcz, BLK - 1, 0))
        d2 = jnp.where(row == BLK - 1, n1, jnp.where(row == BLK - 2, n0, pltpu.roll(dcz, BLK - 2, 0)))
        dz = cw[2:3, :] * dcz + cw[1:2, :] * d1 + cw[0:1, :] * d2
        pieces = [(0, dgate_b.astype(BF)), (CW, (dz * f["hc"]).astype(BF)), (2 * CW, (dz * f["gate_c"]).astype(BF))]

        scale = f["scale"]
        lane = lax.broadcasted_iota(jnp.int32, (1, 128), 1)
        dsink = jnp.zeros((1, 128), F32)
        dq_cols, dk_cols, dv_cols, dk_prev, dv_prev, ds = [], [], [], [], [], []
        for kv in range(NKV):
            dKb = dVb = 0.0
            for h in range(kv * GQ, (kv + 1) * GQ):
                hd = f["heads"][h]
                dO = dya[:, h * HD:(h + 1) * HD]
                delta = jnp.sum(dO * hd["O"], axis=-1, keepdims=True)
                dOb = dO.astype(BF)
                dP = _dot(dOb, hd["vb"], 1, 1)
                dS = hd["probs"] * (dP - delta)
                tot = jnp.sum(hd["psink"] * delta, axis=0, keepdims=True)
                dsink = dsink - jnp.where(lane == h, tot, 0.0)
                ds.append(dS)
                dSb = dS.astype(BF)
                dq_cols.append(_dot(dSb, hd["kb"], 1, 0))
                dKb = dKb + _dot(dSb, hd["Q"], 0, 0)
                dVb = dVb + _dot(hd["probs"].astype(BF), dOb, 0, 0)
            dk_cols.append(dKb[BLK:, :] + dk_carry[:, kv * HD:(kv + 1) * HD])
            dv_cols.append(dVb[BLK:, :] + dv_carry[:, kv * HD:(kv + 1) * HD])
            dk_prev.append(dKb[:BLK, :])
            dv_prev.append(dVb[:BLK, :])
        dq_raw, dqg_t = _head_norm_bwd(jnp.concatenate(dq_cols, axis=1) * scale, f["q_raw"], f["rq"], qg_v, seg)
        dk_raw, dkg_t = _head_norm_bwd(jnp.concatenate(dk_cols, axis=1), f["k_raw"][BLK:, :], f["rk"][BLK:, :], kg_v, seg)
        pieces.append((3 * CW, jnp.concatenate([dq_raw, dk_raw] + dv_cols, axis=1).astype(BF)))
        owed = (dcz[0:8, :], jnp.concatenate(dk_prev, axis=1), jnp.concatenate(dv_prev, axis=1))
        return pieces, [dcw, dqg_t, dkg_t, dgco, dgao, dsink, *ds], owed

    small = lambda r, c: pl.BlockSpec((r, c), lambda s: (0, 0))
    return _call(
        body, (sinks, proj, proj, proj, proj, dy, *_mix_params(cw8, qg, kg, gco, gao, bias)), name="mix_bwd", grid=(n_steps,),
        in_specs=_mix_in_specs(tile_of) + [pl.BlockSpec((TILE, D), lambda s: (tile_of(s), 0))] + _mix_param_specs(),
        out_specs=[pl.BlockSpec((TILE, INW), lambda s: (tile_of(s), 0)), small(8, CW), small(1, HD), small(1, HD),
                   small(1, CW), small(1, AW), small(1, 128), small(NH * BLK, 2 * BLK)],
        out_shape=[SDS((T, INW), BF), SDS((8, CW), F32), SDS((1, HD), F32), SDS((1, HD), F32), SDS((1, CW), F32),
                   SDS((1, AW), F32), SDS((1, 128), F32), SDS((NH * BLK, 2 * BLK), F32)],
        scratch_shapes=[pltpu.VMEM((8, CW), F32), pltpu.VMEM((BLK, NKV * HD), F32), pltpu.VMEM((BLK, NKV * HD), F32)],
        sem=("arbitrary",), vmem_mib=56, comm=comm, free=(1, 2, 3, 4) + tuple(range(6, 13)))


FT = 256
NFT = DFF // FT
RC = 256
NCH = T // RC
LEAD = 16


def _rows8(x):
    return jnp.sum(x.reshape(x.shape[0] // 8, 8, x.shape[1]), axis=0)


def _ffn_act_specs():
    return [
        pl.BlockSpec((T, FT), lambda j: (0, j)), pl.BlockSpec((T, FT), lambda j: (0, NFT + j)),
        pl.BlockSpec((8, FT), lambda j: (0, j)), pl.BlockSpec((8, FT), lambda j: (0, NFT + j)),
        pl.BlockSpec((1, FT), lambda j: (0, j)), pl.BlockSpec((1, FT), lambda j: (0, NFT + j)),
    ]


def _conv_rows(win, w, b, n):
    win = win.astype(F32)
    u = win[LEAD:LEAD + n]
    u1 = pltpu.roll(win, 1, 0)[LEAD:LEAD + n]
    u2 = pltpu.roll(win, 2, 0)[LEAD:LEAD + n]
    return u2, u1, u, w[0:1, :] * u2 + w[1:2, :] * u1 + w[2:3, :] * u + b


def _ffn_act(up, fw8, fb, comm=()):
    def body(ug_ref, uv_ref, wg_ref, wv_ref, bg_ref, bv_ref, a_ref):
        wg, wv, bg, bv = wg_ref[...], wv_ref[...], bg_ref[...], bv_ref[...]

        def chunk(win_g, win_v):
            gp = _conv_rows(win_g, wg, bg, RC)[3]
            vp = _conv_rows(win_v, wv, bv, RC)[3]
            return (gp * jax.nn.sigmoid(gp) * vp).astype(BF)

        zero = jnp.zeros((LEAD, FT), BF)
        a_ref[0:RC, :] = chunk(jnp.concatenate([zero, ug_ref[0:RC, :]], axis=0),
                               jnp.concatenate([zero, uv_ref[0:RC, :]], axis=0))

        def step(i, carry):
            r0 = pl.multiple_of(i * RC, RC)
            win = pl.ds(r0 - LEAD, RC + LEAD)
            a_ref[pl.ds(r0, RC), :] = chunk(ug_ref[win, :], uv_ref[win, :])
            return carry

        lax.fori_loop(1, NCH, step, 0)

    return _call(
        body, (up, up, fw8, fw8, fb, fb), name="ffn_act", grid=(NFT,), in_specs=_ffn_act_specs(),
        out_specs=[pl.BlockSpec((T, FT), lambda j: (0, j))], out_shape=[SDS((T, DFF), BF)],
        sem=("parallel",), vmem_mib=40, comm=comm, free=(2, 3, 4, 5))


def _ffn_act_bwd(up, da, fw8, fb, comm=()):
    ext = RC + LEAD

    def body(ug_ref, uv_ref, wg_ref, wv_ref, bg_ref, bv_ref, da_ref,
             dug_ref, duv_ref, dwg_ref, dwv_ref, dbg_ref, dbv_ref):
        wg, wv, bg, bv = wg_ref[...], wv_ref[...], bg_ref[...], bv_ref[...]

        def chunk(win_g, win_v, da_e):
            g2, g1, g0, gp = _conv_rows(win_g, wg, bg, ext)
            v2, v1, v0, vp = _conv_rows(win_v, wv, bv, ext)
            da_e = da_e.astype(F32)
            sig = jax.nn.sigmoid(gp)
            dvp = da_e * (gp * sig)
            dgp = da_e * vp * (sig * (1.0 + gp * (1.0 - sig)))

            def back(dp, w):
                return (w[2:3, :] * dp[0:RC] + w[1:2, :] * pltpu.roll(dp, ext - 1, 0)[0:RC]
                        + w[0:1, :] * pltpu.roll(dp, ext - 2, 0)[0:RC]).astype(BF)

            def sums(dp, u2, u1, u0):
                d = dp[0:RC]
                return [_rows8(d), _rows8(d * u2[0:RC]), _rows8(d * u1[0:RC]), _rows8(d * u0[0:RC])]

            return back(dgp, wg), back(dvp, wv), sums(dgp, g2, g1, g0) + sums(dvp, v2, v1, v0)

        zero = jnp.zeros((LEAD, FT), BF)
        dug, duv, acc = chunk(jnp.concatenate([zero, ug_ref[0:ext, :]], axis=0),
                              jnp.concatenate([zero, uv_ref[0:ext, :]], axis=0), da_ref[0:ext, :])
        dug_ref[0:RC, :] = dug
        duv_ref[0:RC, :] = duv

        def step(i, acc):
            r0 = pl.multiple_of(i * RC, RC)
            win = pl.ds(r0 - LEAD, ext + LEAD)
            dug, duv, part = chunk(ug_ref[win, :], uv_ref[win, :], da_ref[pl.ds(r0, ext), :])
            dug_ref[pl.ds(r0, RC), :] = dug
            duv_ref[pl.ds(r0, RC), :] = duv
            return [a + p for a, p in zip(acc, part)]

        acc = lax.fori_loop(1, NCH - 1, step, acc)
        r0 = T - RC
        tail = lambda ref, lo: jnp.concatenate([ref[lo:T, :], zero], axis=0)
        dug, duv, part = chunk(tail(ug_ref, r0 - LEAD), tail(uv_ref, r0 - LEAD), tail(da_ref, r0))
        dug_ref[r0:T, :] = dug
        duv_ref[r0:T, :] = duv
        tot = [jnp.sum(a + p, axis=0, keepdims=True) for a, p in zip(acc, part)]
        for k, (dw_ref, db_ref) in enumerate(((dwg_ref, dbg_ref), (dwv_ref, dbv_ref))):
            db_ref[...] = tot[4 * k]
            dw_ref[...] = jnp.zeros_like(dw_ref)
            for r in range(3):
                dw_ref[r:r + 1, :] = tot[4 * k + 1 + r]

    col = lambda r: pl.BlockSpec((r, FT), lambda j: (0, j))
    return _call(
        body, (up, up, fw8, fw8, fb, fb, da), name="ffn_act_bwd", grid=(NFT,),
        in_specs=_ffn_act_specs() + [pl.BlockSpec((T, FT), lambda j: (0, j))],
        out_specs=[col(T), col(T), col(8), col(8), col(1), col(1)],
        out_shape=[SDS((T, DFF), BF), SDS((T, DFF), BF), SDS((8, DFF), F32), SDS((8, DFF), F32),
                   SDS((1, DFF), F32), SDS((1, DFF), F32)],
        sem=("parallel",), vmem_mib=40, comm=comm, free=(0, 1, 2, 3, 4, 5))


def _ffn_down_bwd(dh2b, w_down, comm=()):
    tm = TM

    def body(d_ref, w_ref, o_ref):
        o_ref[...] = _dot(d_ref[...], w_ref[...], 1, 1).astype(BF)

    return _call(
        body, (dh2b, w_down), name="ffn_down_bwd", grid=(T // tm,),
        in_specs=[pl.BlockSpec((tm, D), lambda i: (i, 0)), _resident((DFF, D))],
        out_specs=[pl.BlockSpec((tm, DFF), lambda i: (i, 0))], out_shape=[SDS((T, DFF), BF)],
        sem=("parallel",), vmem_mib=40, comm=comm, free=(0, 1))


def _norm_matmul_bwd(name, a_list, w_t, k_offsets, xin, g, dres, want_bf16, comm=(), slot=None):
    tm = TM
    ks = [a.shape[1] for a in a_list]
    n_a = len(a_list)
    n_pre = 0 if slot is None else 1

    def body(*refs):
        refs = refs[n_pre:]
        a_refs = refs[:n_a]
        w_ref, x_ref, g_ref, r_ref = refs[n_a:n_a + 4]
        outs = refs[n_a + 4:]
        dx_ref, dg_ref = outs[0], (outs[-1] if slot is None else outs[-1].at[0])

        @pl.when(pl.program_id(0) == 0)
        def _():
            dg_ref[...] = jnp.zeros_like(dg_ref)

        du = _dot(a_refs[0][...], w_ref[k_offsets[0]:k_offsets[0] + ks[0], :], 1, 0)
        for k in range(1, n_a):
            du = du + _dot(a_refs[k][...], w_ref[k_offsets[k]:k_offsets[k] + ks[k], :], 1, 0)
        x = x_ref[...]
        r = lax.rsqrt(jnp.mean(x * x, axis=-1, keepdims=True) + EPS)
        dx, dg = _rms_bwd(du, x, r, g_ref[...])
        dx = r_ref[...] + dx
        dx_ref[...] = dx
        if want_bf16:
            outs[1][...] = dx.astype(BF)
        dg_ref[...] += dg

    tile = lambda c: pl.BlockSpec((tm, c), lambda i, *_: (i, 0))
    if slot is None:
        dg_spec, dg_shape = pl.BlockSpec((1, D), lambda i: (0, 0)), SDS((1, D), F32)
    else:
        dg_spec, dg_shape = pl.BlockSpec((1, 1, D), lambda i, slot_ref: (slot_ref[0], 0, 0)), SDS((N_DEV, 1, D), F32)
    out_specs = [tile(D)] + ([tile(D)] if want_bf16 else []) + [dg_spec]
    out_shape = [SDS((T, D), F32)] + ([SDS((T, D), BF)] if want_bf16 else []) + [dg_shape]
    return _call(
        body, (*a_list, w_t, xin, g, dres), name=name, grid=(T // tm,), prefetch=() if slot is None else (slot,),
        in_specs=[tile(k) for k in ks] + [_resident(w_t.shape), tile(D),
                                           pl.BlockSpec((1, D), lambda i, *_: (0, 0)), tile(D)],
        out_specs=out_specs, out_shape=out_shape, sem=("arbitrary",), vmem_mib=56, comm=comm, free=tuple(range(n_a + 4)))


def _out_bwd(dh1b, w_out, comm=()):
    tm = TM

    def body(d_ref, w_ref, o_ref):
        o_ref[...] = _dot(d_ref[...], w_ref[...], 1, 1)

    return _call(
        body, (dh1b, w_out), name="out_bwd", grid=(T // tm,),
        in_specs=[pl.BlockSpec((tm, D), lambda i: (i, 0)), _resident((D, D))],
        out_specs=[pl.BlockSpec((tm, D), lambda i: (i, 0))], out_shape=[SDS((T, D), F32)],
        sem=("parallel",), vmem_mib=32, comm=comm, free=(0, 1))


def _wgrad(name, a_list, b, old_a, comm=()):
    m_k = a_list[0].shape[1]
    tm = max(t for t in range(128, m_k // 2 + 1, 128) if m_k % t == 0)
    steps = [a.shape[1] // tm for a in a_list]
    starts = [sum(steps[:k]) for k in range(len(a_list))]
    n_a = len(a_list)

    def body(*refs):
        a_refs, b_ref, o_ref = refs[:n_a], refs[n_a], refs[n_a + 1]
        i = pl.program_id(0)
        for k in range(n_a):
            @pl.when((i >= starts[k]) & (i < starts[k] + steps[k]))
            def _(k=k):
                o_ref[...] = _dot(a_refs[k][...], b_ref[...], 0, 0).astype(BF)

    def a_spec(k):
        return pl.BlockSpec((T, tm), lambda i: (0, jnp.clip(i - starts[k], 0, steps[k] - 1)))

    m_total = tm * sum(steps)
    return _call(
        body, (*a_list, b), name=name, grid=(sum(steps),),
        in_specs=[a_spec(k) for k in range(n_a)] + [_resident((T, D))],
        out_specs=[pl.BlockSpec((tm, D), lambda i: (i, 0))], out_shape=[SDS((m_total, D), BF)],
        sem=("parallel",), vmem_mib=40, comm=comm, free=() if old_a is None else tuple(range(n_a)) if old_a else (n_a,))


def _chip_sum(name, gbf, from_sib, core, chip):
    h = gbf.shape[1]
    th = h

    def body(core_ref, chip_ref, g_ref, s_ref, pbf_ref, own_ref):
        p = g_ref[0].astype(F32) + s_ref[0].astype(F32)
        pbf_ref[0] = p.astype(BF)

        @pl.when(pl.program_id(1) == chip_ref[0])
        def _():
            own_ref[...] = p

    grid_spec = pltpu.PrefetchScalarGridSpec(
        num_scalar_prefetch=2, grid=(h // th, N_CHIPS),
        in_specs=[pl.BlockSpec((1, th, D), lambda t, jj, core_ref, chip_ref: (2 * jj + core_ref[0], t, 0)),
                  pl.BlockSpec((1, th, D), lambda t, jj, core_ref, chip_ref: (jj, t, 0))],
        out_specs=[pl.BlockSpec((1, th, D), lambda t, jj, core_ref, chip_ref: (jj, t, 0)),
                   pl.BlockSpec((th, D), lambda t, jj, core_ref, chip_ref: (t, 0))],
    )
    return _pcall(
        body, name=name, grid_spec=grid_spec, out_shape=_in_hbm([SDS((N_CHIPS, h, D), BF), SDS((h, D), F32)]),
        compiler_params=_params(("arbitrary", "arbitrary"), 32),
    )(core, chip, *_from_hbm(gbf, from_sib))


def _final_sum(name, own, from_chips, core, comm=()):
    h = own.shape[0]

    def body(core_ref, o_ref, r_ref, f_ref):
        f_ref[0] = ((o_ref[...] + r_ref[0].astype(F32)) + r_ref[1].astype(F32)) + r_ref[2].astype(F32)

    return _call(
        body, (own, from_chips), name=name, grid=(1,), prefetch=(core,),
        in_specs=[pl.BlockSpec((h, D), lambda i, core_ref: (0, 0)), pl.BlockSpec((3, h, D), lambda i, core_ref: (0, 0, 0))],
        out_specs=[pl.BlockSpec((1, h, D), lambda i, core_ref: (core_ref[0], 0, 0))], out_shape=[SDS((2, h, D), F32)],
        sem=("arbitrary",), vmem_mib=40, comm=comm)


def _adam_math(w, g, m, v):
    nm = ADAM_B1 * m + (1.0 - ADAM_B1) * g
    nv = ADAM_B2 * v + (1.0 - ADAM_B2) * (g * g)
    m_hat = nm / (1.0 - ADAM_B1 ** ADAM_STEP)
    v_hat = nv / (1.0 - ADAM_B2 ** ADAM_STEP)
    return -ADAM_LR * (m_hat / (jnp.sqrt(v_hat) + ADAM_EPS) + ADAM_WD * w), nm, nv


def _adamw(name, w, g, m, v, tr, copy_g=False, stage=True, g_transposed=False):
    rows, cols = w.shape

    def body(w_ref, g_ref, m_ref, v_ref, *outs):
        d_ref, nm_ref, nv_ref = outs[-3:]
        for c in [pl.ds(c0, 128) for c0 in range(0, cols, 128)] if g_transposed else [slice(None)]:
            g_val = g_ref[c, :].T if g_transposed else g_ref[...]
            if copy_g:
                outs[0][:, c] = g_val
            d_ref[:, c], nm_ref[:, c], nv_ref[:, c] = _adam_math(w_ref[:, c], g_val, m_ref[:, c], v_ref[:, c])

    spec = pl.BlockSpec((tr, cols), lambda i: (i, 0))
    n_out = 4 if copy_g else 3
    g_spec = pl.BlockSpec((cols, tr), lambda i: (0, i)) if g_transposed else spec
    return _call(body, (w, g, m, v), name=name, grid=(rows // tr,), in_specs=[spec, g_spec, spec, spec], out_specs=[spec] * n_out,
                 out_shape=[SDS((rows, cols), F32)] * n_out, sem=("parallel",), vmem_mib=32,
                 free=(0, 2, 3) if stage else ())


C_SQ = 2 * DFF
P_W = C_SQ + 128
R_G2, R_GO, R_DCW, R_QK = 0, 1, 2, 5
C_GCO, C_GAO, C_DQG, C_DKG, C_SINK = 0, CW, 0, 128, 256


def _pack(name, me, ins, width, fill):
    def body(me_ref, *refs):
        o = refs[-1]
        o[...] = jnp.zeros_like(o)
        fill(o, *refs[:-1])

    return _call(body, ins, name=name, grid=(1,), prefetch=(me,),
                 in_specs=[pl.BlockSpec(a.shape, lambda i, me_ref: (0, 0)) for a in ins],
                 out_specs=[pl.BlockSpec((1, 8, width), lambda i, me_ref: (me_ref[0], 0, 0))],
                 out_shape=[SDS((N_DEV, 8, width), F32)], sem=("arbitrary",))[0]


def _pack_ffn(me, dfwg, dfwv, dfbg, dfbv, sq):
    def fill(o, dfwg_r, dfwv_r, dfbg_r, dfbv_r, sq_r):
        o[0, :, 0:DFF] = dfwg_r[...]
        o[0, :, DFF:2 * DFF] = dfwv_r[...]
        o[0, 3:4, 0:DFF] = dfbg_r[...]
        o[0, 3:4, DFF:2 * DFF] = dfbv_r[...]
        o[0, :, C_SQ:C_SQ + 128] = sq_r[...]

    return _pack("pack_ffn", me, (dfwg, dfwv, dfbg, dfbv, sq), P_W, fill)


def _pack_mix(me, dg2, dgco, dgao, dcw8, dqg, dkg, dsink):
    def fill(o, dg2_r, dgco_r, dgao_r, dcw_r, dqg_r, dkg_r, dsink_r):
        o[0, R_G2:R_G2 + 1, :] = dg2_r[...]
        o[0, R_GO:R_GO + 1, C_GCO:C_GCO + CW] = dgco_r[...]
        o[0, R_GO:R_GO + 1, C_GAO:C_GAO + AW] = dgao_r[...]
        o[0, R_DCW:R_DCW + 3, 0:CW] = dcw_r[0:3, :]
        o[0, R_QK:R_QK + 1, C_DQG:C_DQG + HD] = dqg_r[...]
        o[0, R_QK:R_QK + 1, C_DKG:C_DKG + HD] = dkg_r[...]
        o[0, R_QK:R_QK + 1, C_SINK:C_SINK + 128] = dsink_r[...]

    return _pack("pack_mix", me, (dg2, dgco, dgao, dcw8, dqg, dkg, dsink), D, fill)


N_SMALL = 11


def _small_adam(chip, p_all, pm_all, g1_all, tbl_all, ws, ms, vs):
    fw_cols = 2 * DFF // N_CHIPS
    cw_cols = CW // N_CHIPS

    def body(chip_ref, p_ref, fw_ref, pm_ref, cw_ref, g1_ref, tbl_ref, *refs):
        w_r, m_r, v_r = refs[0:N_SMALL], refs[N_SMALL:2 * N_SMALL], refs[2 * N_SMALL:3 * N_SMALL]
        outs = refs[3 * N_SMALL:]
        g_o, d_o, nm_o, nv_o = (outs[k * N_SMALL:(k + 1) * N_SMALL] for k in range(4))
        loss_o = outs[4 * N_SMALL]

        def total(ref):
            s = ref[0]
            for k in range(1, N_DEV):
                s = s + ref[k]
            return s

        S = total(p_ref)
        fw = total(fw_ref)
        M = total(pm_ref)
        cw = total(cw_ref)

        def step(i, g, at):
            d, nm, nv = _adam_math(w_r[i][at], g, m_r[i][at], v_r[i][at])
            g_o[i][at], d_o[i][at], nm_o[i][at], nv_o[i][at] = g, d, nm, nv

        everything = (slice(None), slice(None))
        step(0, total(g1_ref), everything)
        for r in range(3):
            step(1, cw[R_DCW + r:R_DCW + r + 1, :], (r, slice(None), slice(None)))
        step(2, M[R_QK:R_QK + 1, C_DQG:C_DQG + HD], everything)
        step(3, M[R_QK:R_QK + 1, C_DKG:C_DKG + HD], everything)
        step(4, total(tbl_ref), everything)
        step(5, M[R_QK:R_QK + 1, C_SINK:C_SINK + NH], everything)
        step(6, M[R_GO:R_GO + 1, C_GCO:C_GCO + CW], everything)
        step(7, M[R_GO:R_GO + 1, C_GAO:C_GAO + AW], everything)
        step(8, M[R_G2:R_G2 + 1, :], everything)
        for r in range(3):
            step(9, fw[r:r + 1, :], (r, slice(None), slice(None)))
        step(10, S[3:4, 0:2 * DFF], everything)
        sq = S[:, C_SQ:C_SQ + 128]
        loss_o[...] = jnp.sum(jnp.sum(sq, axis=1, keepdims=True), axis=0, keepdims=True) * (0.5 / D)

    def full(a):
        n = len(a.shape)
        return pl.BlockSpec(a.shape, lambda i, chip_ref: (0,) * n)

    params = [*ws, *ms, *vs]
    out = _call(
        body, (p_all, p_all, pm_all, pm_all, g1_all, tbl_all, *params), name="small_adam", grid=(1,), prefetch=(chip,),
        in_specs=[full(p_all),
                  pl.BlockSpec((N_DEV, 8, fw_cols), lambda i, chip_ref: (0, 0, chip_ref[0])),
                  full(pm_all),
                  pl.BlockSpec((N_DEV, 8, cw_cols), lambda i, chip_ref: (0, 0, chip_ref[0])),
                  full(g1_all), full(tbl_all), *[full(a) for a in params]],
        out_specs=[full(a) for a in ws] * 4 + [pl.BlockSpec((1, 1), lambda i, chip_ref: (0, 0))],
        out_shape=[SDS(a.shape, F32) for a in ws] * 4 + [SDS((1, 1), F32)], sem=("arbitrary",), vmem_mib=32)
    return out[0:N_SMALL], out[N_SMALL:2 * N_SMALL], out[2 * N_SMALL:3 * N_SMALL], out[3 * N_SMALL:4 * N_SMALL], out[4 * N_SMALL]


PLACE_STEPS = 4


def _place_specs(shards):
    rows = [s.shape[0] // PLACE_STEPS for s in shards]
    return ([pl.BlockSpec((r, D), lambda i, chip_ref: (i, 0)) for r in rows],
            [pl.BlockSpec((r, D), lambda i, chip_ref: (chip_ref[0] * PLACE_STEPS + i, 0)) for r in rows],
            [SDS((N_CHIPS * s.shape[0], D), BF) for s in shards])


def _place_first(chip, shard, conv_w, ffn_conv_w):
    def body(chip_ref, a, s0, s1, o, t0, t1):
        o[...] = a[...].astype(BF)

        @pl.when(pl.program_id(0) == 0)
        def _():
            for s, t in ((s0, t0), (s1, t1)):
                t[...] = jnp.zeros_like(t)
                t[0, 0:3, :] = s[...]

    ins, outs, shapes = _place_specs([shard])
    taps = (conv_w, ffn_conv_w)
    return _call(
        body, (shard, conv_w, ffn_conv_w), name="place_first", grid=(PLACE_STEPS,), prefetch=(chip,),
        in_specs=ins + [pl.BlockSpec(s.shape, lambda i, chip_ref: (0, 0)) for s in taps],
        out_specs=outs + [pl.BlockSpec((1, 8, s.shape[1]), lambda i, chip_ref: (chip_ref[0], 0, 0)) for s in taps],
        out_shape=shapes + [SDS((N_CHIPS, 8, s.shape[1]), F32) for s in taps],
        sem=("arbitrary",), vmem_mib=32, free=(1, 2))


def _place_rest(chip, shards, w_up, table, bucket, comm):
    n = len(shards)
    c_up = w_up.shape[1]
    edges = [round(k * (c_up // 128) / PLACE_STEPS) * 128 for k in range(PLACE_STEPS + 1)]

    def body(chip_ref, *refs):
        a, (up_ref, tab_ref, bk_ref), o = refs[:n], refs[n:n + 3], refs[n + 3:2 * n + 3]
        up_o, bias_ref = refs[2 * n + 3:]
        for src, dst in zip(a, o):
            dst[...] = src[...].astype(BF)
        for k in range(PLACE_STEPS):
            @pl.when(pl.program_id(0) == k)
            def _(k=k):
                up_o[edges[k]:edges[k + 1], :] = up_ref[:, edges[k]:edges[k + 1]].T.astype(BF)

        @pl.when(pl.program_id(0) == 0)
        def _():
            bk = bk_ref[...]
            eq = [bk == b for b in range(NBUCKET)]
            for h in range(NH):
                acc = jnp.zeros((BLK, 2 * BLK), F32)
                for b in range(NBUCKET):
                    acc = jnp.where(eq[b], tab_ref[h, b], acc)
                bias_ref[h * BLK:(h + 1) * BLK, :] = acc

    ins, outs, shapes = _place_specs(shards)
    return _call(
        body, (*shards, w_up, table, bucket), name="place_rest", grid=(PLACE_STEPS,), prefetch=(chip,),
        in_specs=ins + [_resident(w_up.shape), pl.BlockSpec(memory_space=pltpu.SMEM),
                        pl.BlockSpec(bucket.shape, lambda i, chip_ref: (0, 0))],
        out_specs=outs + [pl.BlockSpec((c_up, D), lambda i, chip_ref: (chip_ref[0], 0)),
                          pl.BlockSpec((NH * BLK, 2 * BLK), lambda i, chip_ref: (0, 0))],
        out_shape=shapes + [SDS((N_CHIPS * c_up, D), BF), SDS((NH * BLK, 2 * BLK), F32)],
        sem=("arbitrary",), vmem_mib=32, comm=comm, free=(n + 1, n + 2))


def kernel(x, norm_mix_g, w_in, conv_w, q_norm_g, k_norm_g, rel_bias_table, sinks, out_norm_conv_g, out_norm_attn_g, w_out, norm_ffn_g, w_up, ffn_conv_w, ffn_conv_b, w_down, loss_target, m_norm_mix_g, m_w_in, m_conv_w, m_q_norm_g, m_k_norm_g, m_rel_bias_table, m_sinks, m_out_norm_conv_g, m_out_norm_attn_g, m_w_out, m_norm_ffn_g, m_w_up, m_ffn_conv_w, m_ffn_conv_b, m_w_down, v_norm_mix_g, v_w_in, v_conv_w, v_q_norm_g, v_k_norm_g, v_rel_bias_table, v_sinks, v_out_norm_conv_g, v_out_norm_attn_g, v_w_out, v_norm_ffn_g, v_w_up, v_ffn_conv_w, v_ffn_conv_b, v_w_down):
    as_arg = lambda i: jnp.reshape(i, (1,)).astype(jnp.int32)
    chip = as_arg(2 * lax.axis_index("x") + lax.axis_index("y"))
    core = as_arg(lax.axis_index("c"))
    me = 2 * chip + core
    xs, tgt = x[0], loss_target[0]
    qg, kg, gco, gao, g1, g2, fb = q_norm_g, k_norm_g, out_norm_conv_g, out_norm_attn_g, norm_mix_g, norm_ffn_g, ffn_conv_b
    pieces = lambda g: g.reshape(N_DEV, g.shape[0] // N_DEV, D)
    whole = lambda f: f.reshape(2 * f.shape[1], D)

    bucket = jnp.asarray(_bucket_table())
    p_in, p_cw, p_fw = _place_first(chip, w_in[0].T, conv_w[0], ffn_conv_w[0])
    p_out, p_down, p_up, bias, w_int, cw_all, fw_all = _place_rest(
        chip, [w_out[0], w_down[0]], w_up[0], rel_bias_table.T, bucket,
        comm=[_t_gather(p_in, relayed_first=True), _t_small_weights(p_cw), _t_small_weights(p_fw)])
    cw8 = jnp.transpose(cw_all, (1, 0, 2)).reshape(8, CW)
    fw8 = jnp.transpose(fw_all, (1, 0, 2)).reshape(8, 2 * DFF)

    early = 3 / 11
    proj, u1, w_out_f, p_up = _inproj(xs, g1, w_int, comm=[_t_gather(p_out), _t_gather(p_up, (0, early))])
    y, w_upt = _mix_fwd(proj, sinks, cw8, qg, kg, gco, gao, bias, comm=[_t_gather(p_up, (early, 1))])
    h1, u2 = _outproj(y, w_out_f, xs, g2)
    up, w_down_f = _ffn_up(u2, w_upt, comm=[_t_gather(p_down)])
    a, = _ffn_act(up, fw8, fb)
    dh2, dh2b, sq = _ffn_down(a, w_down_f, h1, tgt)

    gdbf, = _wgrad("wgrad_down", [a], dh2b, None)
    da, sib_down = _ffn_down_bwd(dh2b, w_down_f, comm=[_t_sibling(pieces(gdbf))])
    pbf_down, own_down = _chip_sum("chip_sum_w_down", pieces(gdbf), sib_down, core, chip)
    dug, duv, dfwg, dfwv, dfbg, dfbv, chips_down = _ffn_act_bwd(up, da, fw8, fb, comm=[_t_chips(pbf_down)])
    fin_down, = _final_sum("final_sum_w_down", own_down, chips_down, core)
    gubf, = _wgrad("wgrad_up", [dug, duv], u2, False)
    p_all = _pack_ffn(me, dfwg, dfwv, dfbg, dfbv, sq)
    dh1, dh1b, dg2, sib_up, fin_down, p_all = _norm_matmul_bwd(
        "ffn_up_bwd", [dug, duv], w_upt, [0, DFF], h1, g2, dh2, True,
        comm=[_t_sibling(pieces(gubf)), _t_swap(fin_down), _t_allgather(p_all)])
    pbf_up, own_up = _chip_sum("chip_sum_w_up", pieces(gubf), sib_up, core, chip)
    gobf, = _wgrad("wgrad_out", [y], dh1b, True)
    dy, sib_out = _out_bwd(dh1b, w_out_f, comm=[_t_sibling(pieces(gobf))])
    pbf_out, own_out = _chip_sum("chip_sum_w_out", pieces(gobf), sib_out, core, chip)
    dproj, dcw8, dqg, dkg, dgco, dgao, dsink, dbias, chips_up = _mix_bwd(
        proj, dy, sinks, cw8, qg, kg, gco, gao, bias, comm=[_t_chips(pbf_up)])
    fin_up, = _final_sum("final_sum_w_up", own_up, chips_up, core)
    tbl_all = _band_bias_bwd(dbias, bucket, me)
    pm_all = _pack_mix(me, dg2, dgco, dgao, dcw8, dqg, dkg, dsink)
    gibf, chips_out, fin_up, pm_all, tbl_all = _wgrad(
        "wgrad_in", [dproj], u1, False,
        comm=[_t_chips(pbf_out), _t_swap(fin_up), _t_allgather(pm_all), _t_allgather(tbl_all)])
    fin_out, sib_in = _final_sum("final_sum_w_out", own_out, chips_out, core, comm=[_t_sibling(pieces(gibf))])
    pbf_in, own_in = _chip_sum("chip_sum_w_in", pieces(gibf), sib_in, core, chip)
    dx, g1_all, chips_in, fin_out = _norm_matmul_bwd(
        "in_bwd", [dproj], w_int, [0], xs, g1, dh1, False, comm=[_t_chips(pbf_in), _t_swap(fin_out)], slot=me)
    fin_in, = _final_sum("final_sum_w_in", own_in, chips_in, core)
    g1_all, fin_in = _comm_call("gather_last", [_t_allgather(g1_all), _t_swap(fin_in)])

    g_w_out, g_w_down = whole(fin_out), whole(fin_down)
    g_w_down, d_down, nm_down, nv_down = _adamw("adamw_w_down", w_down[0], g_w_down, m_w_down[0], v_w_down[0], 352, True)
    g_w_up, d_up, nm_up, nv_up = _adamw(
        "adamw_w_up", w_up[0], whole(fin_up), m_w_up[0], v_w_up[0], 256, True, stage=False, g_transposed=True)
    g_w_out, d_out, nm_out, nv_out = _adamw("adamw_w_out", w_out[0], g_w_out, m_w_out[0], v_w_out[0], 256, True, stage=False)
    g_w_in, d_in, nm_in, nv_in = [a.T for a in _adamw(
        "adamw_w_in", w_in[0].T, whole(fin_in), m_w_in[0].T, v_w_in[0].T, INW // N_CHIPS // 3, True)]
    taps = lambda a: jnp.transpose(a, (1, 0, 2))
    sw = [norm_mix_g, taps(conv_w), q_norm_g, k_norm_g, rel_bias_table.T, sinks, out_norm_conv_g, out_norm_attn_g,
          norm_ffn_g, taps(ffn_conv_w), ffn_conv_b]
    smm = [m_norm_mix_g, taps(m_conv_w), m_q_norm_g, m_k_norm_g, m_rel_bias_table.T, m_sinks, m_out_norm_conv_g,
           m_out_norm_attn_g, m_norm_ffn_g, taps(m_ffn_conv_w), m_ffn_conv_b]
    smv = [v_norm_mix_g, taps(v_conv_w), v_q_norm_g, v_k_norm_g, v_rel_bias_table.T, v_sinks, v_out_norm_conv_g,
           v_out_norm_attn_g, v_norm_ffn_g, taps(v_ffn_conv_w), v_ffn_conv_b]
    *small_out, loss = _small_adam(chip, p_all, pm_all, g1_all, tbl_all, sw, smm, smv)
    sg, sd, snm, snv = [list(r) for r in small_out]
    for r in (sg, sd, snm, snv):
        r[1], r[4], r[9] = taps(r[1]), r[4].T, taps(r[9])

    def order(s, b_in, b_out, b_up, b_down):
        return (s[0], b_in[None], s[1], s[2], s[3], s[4], s[5], s[6], s[7], b_out[None], s[8], b_up[None],
                s[9], s[10], b_down[None])

    return (loss.reshape(()), dx[None],
            *order(sg, g_w_in, g_w_out, g_w_up, g_w_down),
            *order(sd, d_in, d_out, d_up, d_down),
            *order(snm, nm_in, nm_out, nm_up, nm_down),
            *order(snv, nv_in, nv_out, nv_up, nv_down))
```

```python
import functools
import math

import numpy as np

import jax
import jax.numpy as jnp
from jax import lax
from jax.experimental import pallas as pl
from jax.experimental.pallas import tpu as pltpu

F32 = jnp.float32
BF = jnp.bfloat16
SDS = jax.ShapeDtypeStruct

T = 2048
D = 1024
CW = 512
AW = 512
HD = 64
NH = 8
NKV = 2
GQ = 4
INW = 2304
DFF = 2816
BLK = 128
NB = T // BLK
NBUCKET = 32
EPS = 1e-6
NEG_INF = -1e30
N_CHIPS = 4
N_DEV = 8

ADAM_LR = 0.001
ADAM_B1 = 0.9
ADAM_B2 = 0.999
ADAM_EPS = 1e-08
ADAM_WD = 0.01
ADAM_STEP = 10

TM = 512
MIB = 1024 * 1024
MESH = pl.DeviceIdType.MESH
ANY = pl.BlockSpec(memory_space=pl.ANY)

_pcall = pl.pallas_call


def _params(sem=None, vmem_mib=None, collective_id=None):
    kw = {} if collective_id is None else {"collective_id": collective_id}
    if sem is not None:
        kw["dimension_semantics"] = sem
    if vmem_mib is not None:
        kw["vmem_limit_bytes"] = vmem_mib * MIB
    return pltpu.CompilerParams(**kw)


def _resident(shape):
    return pl.BlockSpec(shape, lambda *_: (0,) * len(shape), pipeline_mode=pl.Buffered(1))


def _dot(a, b, ca, cb):
    return lax.dot_general(a, b, (((ca,), (cb,)), ((), ())), preferred_element_type=F32)


def _rms_bwd(dy, x, r, g):
    dg = jnp.sum(dy * (x * r), axis=0, keepdims=True)
    dgx = dy * g
    dx = r * dgx - x * (r * r * r) * jnp.mean(x * dgx, axis=-1, keepdims=True)
    return dx, dg


def _where():
    x, y, c = lax.axis_index("x"), lax.axis_index("y"), lax.axis_index("c")
    return x, y, c, [(1 - x, y), (x, 1 - y), (1 - x, 1 - y)]


def _rcopy(src, dst, ssem, rsem, dev):
    return pltpu.make_async_remote_copy(src_ref=src, dst_ref=dst, send_sem=ssem, recv_sem=rsem, device_id=dev,
                                        device_id_type=MESH)


SIBLING, Y_CHIP, X_CHIP, DIAGONAL_CHIP = 1, 2, 4, 6
OTHER_CHIPS = (Y_CHIP, X_CHIP, DIAGONAL_CHIP)
EVERYONE = tuple(range(1, N_DEV))
BARRIER_OF = {(SIBLING,): 0, (SIBLING, Y_CHIP, X_CHIP): 1, OTHER_CHIPS: 2, (SIBLING,) + OTHER_CHIPS: 3, EVERYONE: 4}


def _peer(rel):
    x, y, c, _ = _where()
    return x ^ ((rel >> 2) & 1), y ^ ((rel >> 1) & 1), c ^ (rel & 1)


class _Task:
    def __init__(self, ins, outs, alias, n_sem, start, finish, middle=None, peers=()):
        self.ins, self.outs, self.alias, self.n_sem, self.start, self.finish = ins, outs, alias, n_sem, start, finish
        self.middle = middle if middle is not None else (lambda *args: None)
        self.peers = peers


def _peers_of(comm):
    return tuple(sorted({p for t in comm for p in t.peers}))


def _enter(comm):
    peers = _peers_of(comm)
    barrier = pltpu.get_barrier_semaphore()
    for rel in peers:
        pl.semaphore_signal(barrier, inc=1, device_id=_peer(rel), device_id_type=MESH)
    pl.semaphore_wait(barrier, len(peers))


ROWS16 = 16


def _t_gather(placed, part=(0, 1), relayed_first=False):
    R = placed.shape[0] // N_CHIPS
    q = R // 4
    lo, hi = (round(f * (q // ROWS16)) * ROWS16 for f in part)

    def quarter(chip_index, core, k):
        return pl.ds(pl.multiple_of(chip_index * R + core * 2 * q + k * q + lo, ROWS16), hi - lo)

    def places():
        x, y, c, _ = _where()
        return c, 2 * x + y, 2 * (1 - x) + y, 2 * x + (1 - y), 2 * (1 - x) + (1 - y), (1 - x, y, c), (x, 1 - y, c), (x, y, 1 - c)

    def copy(buf, k, chip_index, core, quart, ss, rs, b, dev):
        window = buf.at[quarter(chip_index, core, quart)]
        return _rcopy(window, window, ss.at[b + k], rs.at[b + k], dev)

    def first_hop(cout, ss, rs, b, which):
        c, me, _, _, _, x_nbr, y_nbr, _ = places()
        for k, (quart, dev) in enumerate(((0, x_nbr), (1, y_nbr), (1, x_nbr), (0, y_nbr))):
            if k in which:
                copy(cout[0], k, me, c, quart, ss, rs, b, dev).start()

    def start(cin, cout, ss, rs, b):
        first_hop(cout, ss, rs, b, (0, 1) if relayed_first else (0, 1, 2, 3))

    def middle(cin, cout, ss, rs, b):
        c, _, xc, yc, _, x_nbr, y_nbr, sib = places()
        for k, chip_index, quart, dev in ((0, xc, 0, y_nbr), (1, yc, 1, x_nbr)):
            copy(cout[0], k, chip_index, c, quart, ss, rs, b, dev).wait_recv()
            copy(cout[0], 4 + k, chip_index, c, quart, ss, rs, b, dev).start()
            copy(cout[0], 6 + k, chip_index, c, quart, ss, rs, b, sib).start()
        if relayed_first:
            first_hop(cout, ss, rs, b, (2, 3))

    later = ((2, 1, 1), (3, 2, 0), (4, 3, 0), (5, 3, 1))

    def finish(cin, cout, ss, rs, b):
        c, me, xc, yc, dc, _, _, sib = places()
        chip_of = {1: xc, 2: yc, 3: dc}
        for k, whose, quart in later:
            copy(cout[0], k, chip_of[whose], c, quart, ss, rs, b, sib).wait_recv()
            copy(cout[0], 6 + k, chip_of[whose], c, quart, ss, rs, b, sib).start()
        for k, whose, quart in ((0, 1, 0), (1, 2, 1)) + later:
            copy(cout[0], 6 + k, chip_of[whose], 1 - c, quart, ss, rs, b, sib).wait_recv()
        for k in range(12):
            copy(cout[0], k, me, c, 0, ss, rs, b, sib).wait_send()

    return _Task([placed], [SDS(placed.shape, placed.dtype)], [(0, 0)], 12, start, finish, middle, peers=(SIBLING, Y_CHIP, X_CHIP))


def _t_small_weights(buf):
    def start(cin, cout, ss, rs, b):
        x, y, c, chips = _where()
        mine = cout[0].at[2 * x + y]
        for r, (px, py) in enumerate(chips):
            _rcopy(mine, mine, ss.at[b + r], rs.at[b + r], (px, py, c)).start()

    def finish(cin, cout, ss, rs, b):
        x, y, c, chips = _where()
        for r, (px, py) in enumerate(chips):
            got = cout[0].at[2 * px + py]
            _rcopy(got, got, ss.at[b + r], rs.at[b + r], (px, py, c)).wait_recv()
        for r, (px, py) in enumerate(chips):
            mine = cout[0].at[2 * x + y]
            _rcopy(mine, mine, ss.at[b + r], rs.at[b + r], (px, py, c)).wait_send()

    return _Task([buf], [SDS(buf.shape, buf.dtype)], [(0, 0)], 3, start, finish, peers=OTHER_CHIPS)


def _t_sibling(gbf):
    def start(cin, cout, ss, rs, b):
        x, y, c, _ = _where()
        for jj in range(N_CHIPS):
            _rcopy(cin[0].at[2 * jj + (1 - c)], cout[0].at[jj], ss.at[b + jj], rs.at[b + jj], (x, y, 1 - c)).start()

    def finish(cin, cout, ss, rs, b):
        x, y, c, _ = _where()
        for jj in range(N_CHIPS):
            got = cout[0].at[jj]
            _rcopy(got, got, ss.at[b + jj], rs.at[b + jj], (x, y, 1 - c)).wait_recv()
        for jj in range(N_CHIPS):
            got = cout[0].at[jj]
            _rcopy(got, got, ss.at[b + jj], rs.at[b + jj], (x, y, 1 - c)).wait_send()

    return _Task([gbf], [SDS((N_CHIPS,) + gbf.shape[1:], BF)], [], N_CHIPS, start, finish, peers=(SIBLING,))


def _t_chips(pbf):
    def start(cin, cout, ss, rs, b):
        x, y, c, chips = _where()
        for r, (px, py) in enumerate(chips):
            _rcopy(cin[0].at[2 * px + py], cout[0].at[r], ss.at[b + r], rs.at[b + r], (px, py, c)).start()

    def finish(cin, cout, ss, rs, b):
        x, y, c, chips = _where()
        for r, (px, py) in enumerate(chips):
            got = cout[0].at[r]
            _rcopy(got, got, ss.at[b + r], rs.at[b + r], (px, py, c)).wait_recv()
        for r, (px, py) in enumerate(chips):
            got = cout[0].at[r]
            _rcopy(got, got, ss.at[b + r], rs.at[b + r], (px, py, c)).wait_send()

    return _Task([pbf], [SDS((3,) + pbf.shape[1:], BF)], [], 3, start, finish, peers=OTHER_CHIPS)


def _t_swap(fin):
    def start(cin, cout, ss, rs, b):
        x, y, c, _ = _where()
        mine = cout[0].at[c]
        _rcopy(mine, mine, ss.at[b], rs.at[b], (x, y, 1 - c)).start()

    def finish(cin, cout, ss, rs, b):
        x, y, c, _ = _where()
        got = cout[0].at[1 - c]
        _rcopy(got, got, ss.at[b], rs.at[b], (x, y, 1 - c)).wait_recv()
        _rcopy(got, got, ss.at[b], rs.at[b], (x, y, 1 - c)).wait_send()

    return _Task([fin], [SDS(fin.shape, fin.dtype)], [(0, 0)], 1, start, finish, peers=(SIBLING,))


def _t_allgather(buf):
    def peers():
        x, y, c, _ = _where()
        out = []
        for rel in range(1, N_DEV):
            px, py, pc = x ^ ((rel >> 2) & 1), y ^ ((rel >> 1) & 1), c ^ (rel & 1)
            out.append((rel - 1, 4 * px + 2 * py + pc, (px, py, pc)))
        return 4 * x + 2 * y + c, out

    def start(cin, cout, ss, rs, b):
        me, ps = peers()
        mine = cout[0].at[me]
        for k, _, dev in ps:
            _rcopy(mine, mine, ss.at[b + k], rs.at[b + k], dev).start()

    def finish(cin, cout, ss, rs, b):
        me, ps = peers()
        for k, pidx, dev in ps:
            got = cout[0].at[pidx]
            _rcopy(got, got, ss.at[b + k], rs.at[b + k], dev).wait_recv()
        for k, _, dev in ps:
            mine = cout[0].at[me]
            _rcopy(mine, mine, ss.at[b + k], rs.at[b + k], dev).wait_send()

    return _Task([buf], [SDS(buf.shape, buf.dtype)], [(0, 0)], N_DEV - 1, start, finish, peers=EVERYONE)


def _run_tasks(comm, which, cin, cout, ss, rs):
    i0 = o0 = s0 = 0
    for t in comm:
        getattr(t, which)(cin[i0:i0 + len(t.ins)], cout[o0:o0 + len(t.outs)], ss, rs, s0)
        i0, o0, s0 = i0 + len(t.ins), o0 + len(t.outs), s0 + t.n_sem


def _from_hbm(*arrays):
    return [pltpu.with_memory_space_constraint(a, pltpu.HBM) for a in arrays]


def _in_hbm(shapes):
    return [pltpu.HBM(s.shape, s.dtype) for s in shapes]


def _comm_layout(comm, n_in, n_out):
    c_in = [a for t in comm for a in t.ins]
    c_out = [s for t in comm for s in t.outs]
    aliases, i0, o0 = {}, 0, 0
    for t in comm:
        for i, o in t.alias:
            aliases[n_in + i0 + i] = n_out + o0 + o
        i0, o0 = i0 + len(t.ins), o0 + len(t.outs)
    return c_in, c_out, aliases, sum(t.n_sem for t in comm)


def _call(body, operands, *, name, grid, in_specs, out_specs, out_shape, scratch_shapes=(), sem=None, vmem_mib=None, comm=(),
          free=(), prefetch=()):
    operands = [o if s.memory_space == pltpu.SMEM or k in free else pltpu.with_memory_space_constraint(o, pltpu.HBM)
                for k, (o, s) in enumerate(zip(operands, in_specs))]
    n_pre, n_in, n_out, n_scr = len(prefetch), len(in_specs), len(out_specs), len(scratch_shapes)
    c_in, c_out, aliases, n_sem = _comm_layout(comm, n_pre + n_in, n_out)
    sems = [pltpu.SemaphoreType.DMA((n_sem,)), pltpu.SemaphoreType.DMA((n_sem,))] if comm else []

    def wrapped(*refs):
        pre, refs = refs[:n_pre], refs[n_pre:]
        ins, cin = refs[:n_in], refs[n_in:n_in + len(c_in)]
        rest = refs[n_in + len(c_in):]
        outs, cout = rest[:n_out], rest[n_out:n_out + len(c_out)]
        rest = rest[n_out + len(c_out):]
        scr, csem = rest[:n_scr], rest[n_scr:]
        if not comm:
            return body(*pre, *ins, *outs, *scr)
        step = functools.reduce(lambda acc, k: acc * grid[k] + pl.program_id(k), range(len(grid)), 0)
        n_steps = math.prod(grid)

        @pl.when(step == 0)
        def _():
            _enter(comm)
            _run_tasks(comm, "start", cin, cout, *csem)

        pl.when(step == n_steps // 2)(lambda: _run_tasks(comm, "middle", cin, cout, *csem))
        body(*pre, *ins, *outs, *scr)
        pl.when(step == n_steps - 1)(lambda: _run_tasks(comm, "finish", cin, cout, *csem))

    grid_spec = pltpu.PrefetchScalarGridSpec(
        num_scalar_prefetch=n_pre, grid=grid, in_specs=list(in_specs) + [ANY] * len(c_in),
        out_specs=list(out_specs) + [ANY] * len(c_out), scratch_shapes=list(scratch_shapes) + sems)
    return _pcall(
        wrapped, name=name, grid_spec=grid_spec, out_shape=_in_hbm(list(out_shape) + c_out), input_output_aliases=aliases,
        compiler_params=_params(("arbitrary",) * len(grid) if comm else sem, vmem_mib,
                                BARRIER_OF[_peers_of(comm)] if comm else None),
    )(*prefetch, *operands, *_from_hbm(*c_in))


def _comm_call(name, comm):
    c_in, c_out, aliases, n_sem = _comm_layout(comm, 0, 0)

    def body(*refs):
        cin, cout, (ss, rs) = refs[:len(c_in)], refs[len(c_in):len(c_in) + len(c_out)], refs[len(c_in) + len(c_out):]
        _enter(comm)
        for phase in ("start", "middle", "finish"):
            _run_tasks(comm, phase, cin, cout, ss, rs)

    return _pcall(
        body, name=name, in_specs=[ANY] * len(c_in), out_specs=[ANY] * len(c_out), out_shape=_in_hbm(c_out),
        scratch_shapes=[pltpu.SemaphoreType.DMA((n_sem,)), pltpu.SemaphoreType.DMA((n_sem,))],
        input_output_aliases=aliases, compiler_params=_params(collective_id=BARRIER_OF[_peers_of(comm)]),
    )(*_from_hbm(*c_in))


def _inproj(x, g1, w_int, comm=()):
    tm = TM

    def body(x_ref, g_ref, w_ref, proj_ref, u_ref):
        xf = x_ref[...]
        r = lax.rsqrt(jnp.mean(xf * xf, axis=-1, keepdims=True) + EPS)
        u = (xf * r * g_ref[...]).astype(BF)
        u_ref[...] = u
        proj_ref[...] = _dot(u, w_ref[...], 1, 1)

    return _call(
        body, (x, g1, w_int), name="inproj", grid=(T // tm,),
        in_specs=[pl.BlockSpec((tm, D), lambda i: (i, 0)), pl.BlockSpec((1, D), lambda i: (0, 0)),
                  _resident((INW, D))],
        out_specs=[pl.BlockSpec((tm, INW), lambda i: (i, 0)), pl.BlockSpec((tm, D), lambda i: (i, 0))],
        out_shape=[SDS((T, INW), F32), SDS((T, D), BF)], sem=("parallel",), vmem_mib=40, comm=comm, free=(0, 1))


def _outproj(y, w_out, x, g2):
    tm = TM

    def body(y_ref, w_ref, x_ref, g_ref, h1_ref, u2_ref):
        h1 = x_ref[...] + _dot(y_ref[...], w_ref[...], 1, 0)
        h1_ref[...] = h1
        r = lax.rsqrt(jnp.mean(h1 * h1, axis=-1, keepdims=True) + EPS)
        u2_ref[...] = (h1 * r * g_ref[...]).astype(BF)

    return _call(
        body, (y, w_out, x, g2), name="outproj", grid=(T // tm,),
        in_specs=[pl.BlockSpec((tm, D), lambda i: (i, 0)), _resident((D, D)),
                  pl.BlockSpec((tm, D), lambda i: (i, 0)), pl.BlockSpec((1, D), lambda i: (0, 0))],
        out_specs=[pl.BlockSpec((tm, D), lambda i: (i, 0)), pl.BlockSpec((tm, D), lambda i: (i, 0))],
        out_shape=[SDS((T, D), F32), SDS((T, D), BF)], sem=("parallel",), vmem_mib=32, free=(2, 3))


def _ffn_up(u2, w_upt, comm=()):
    tm, tn = 1024, 512

    def body(u_ref, w_ref, o_ref):
        o_ref[...] = _dot(u_ref[...], w_ref[...], 1, 1).astype(BF)

    return _call(
        body, (u2, w_upt), name="ffn_up", grid=(T // tm, 2 * DFF // tn),
        in_specs=[pl.BlockSpec((tm, D), lambda i, j: (i, 0)), pl.BlockSpec((tn, D), lambda i, j: (j, 0))],
        out_specs=[pl.BlockSpec((tm, tn), lambda i, j: (i, j))], out_shape=[SDS((T, 2 * DFF), BF)],
        sem=("parallel", "parallel"), vmem_mib=32, comm=comm, free=(1,))


def _ffn_down(a, w_down, h1, tgt):
    tm = TM

    def body(a_ref, w_ref, h1_ref, t_ref, dh_ref, dhb_ref, l_ref):
        @pl.when(pl.program_id(0) == 0)
        def _():
            l_ref[...] = jnp.zeros_like(l_ref)

        h2 = h1_ref[...] + _dot(a_ref[...], w_ref[...], 1, 0)
        e = h2 - t_ref[...]
        dh = e * (1.0 / D)
        dh_ref[...] = dh
        dhb_ref[...] = dh.astype(BF)
        e2 = jnp.sum((e * e).reshape(tm // 8, 8, D), axis=0)
        acc = e2[:, 0:128]
        for k in range(1, D // 128):
            acc = acc + e2[:, k * 128:(k + 1) * 128]
        l_ref[...] += acc

    return _call(
        body, (a, w_down, h1, tgt), name="ffn_down", grid=(T // tm,),
        in_specs=[pl.BlockSpec((tm, DFF), lambda i: (i, 0)), _resident((DFF, D)),
                  pl.BlockSpec((tm, D), lambda i: (i, 0)), pl.BlockSpec((tm, D), lambda i: (i, 0))],
        out_specs=[pl.BlockSpec((tm, D), lambda i: (i, 0)), pl.BlockSpec((tm, D), lambda i: (i, 0)),
                   pl.BlockSpec((8, 128), lambda i: (0, 0))],
        out_shape=[SDS((T, D), F32), SDS((T, D), BF), SDS((8, 128), F32)], sem=("arbitrary",), vmem_mib=40, free=(2, 3))


def _bucket_table():
    q = np.arange(BLK, dtype=np.int32)[:, None]
    j = np.arange(2 * BLK, dtype=np.int32)[None, :]
    n = np.maximum(q + BLK - j, 0)
    nf = np.maximum(n, 1).astype(np.float32)
    max_exact = NBUCKET // 2
    large = max_exact + (np.log(nf / np.float32(max_exact)) / np.float32(math.log(BLK / max_exact))
                         * np.float32(NBUCKET - max_exact)).astype(np.int32)
    large = np.minimum(large, NBUCKET - 1)
    return np.where(n < max_exact, n, large).astype(np.int32)


def _band_bias_bwd(dbias, bucket, me):
    def body(me_ref, db_ref, bk_ref, o_ref):
        bk = bk_ref[...]
        for b in range(NBUCKET):
            m = bk == b
            for h in range(NH):
                v = jnp.where(m, db_ref[h * BLK:(h + 1) * BLK, :], 0.0)
                s = jnp.sum(jnp.sum(v, axis=1, keepdims=True), axis=0, keepdims=True)
                o_ref[0, h:h + 1, b:b + 1] = s

    grid_spec = pltpu.PrefetchScalarGridSpec(
        num_scalar_prefetch=1, grid=(1,),
        in_specs=[pl.BlockSpec((NH * BLK, 2 * BLK), lambda i, me_ref: (0, 0)),
                  pl.BlockSpec((BLK, 2 * BLK), lambda i, me_ref: (0, 0))],
        out_specs=pl.BlockSpec((1, NH, NBUCKET), lambda i, me_ref: (me_ref[0], 0, 0)),
    )
    return _pcall(body, name="band_bias_bwd", grid_spec=grid_spec, out_shape=SDS((N_DEV, NH, NBUCKET), F32),
                  compiler_params=_params(("arbitrary",)))(me, dbias, bucket)


def _two_bf16(x):
    hi = x.astype(BF)
    return hi, (x - hi.astype(F32)).astype(BF)


def _head_sums(x, seg):
    hi, lo = _two_bf16(x)
    s = seg[0:x.shape[1], :]
    return _dot(hi, s, 1, 0) + _dot(lo, s, 1, 0)


def _head_spread(v, seg, width):
    hi, lo = _two_bf16(v)
    s = seg[0:width, :]
    return _dot(hi, s, 1, 1) + _dot(lo, s, 1, 1)


def _head_norm(x, g_t, seg, by_head=False):
    if by_head:
        heads = [x[:, h * HD:(h + 1) * HD] for h in range(x.shape[1] // HD)]
        r = jnp.concatenate([jnp.broadcast_to(lax.rsqrt(jnp.mean(v * v, axis=-1, keepdims=True) + EPS), v.shape)
                             for v in heads], axis=1)
    else:
        r = lax.rsqrt(_head_sums(x * x, seg) * (1.0 / HD) + EPS)
        r = _head_spread(r, seg, x.shape[1])
    return x * r * g_t, r


def _head_norm_bwd(dy, x, r, g_t, seg):
    dg_t = jnp.sum(dy * (x * r), axis=0, keepdims=True)
    dgx = dy * g_t
    mean = _head_spread(_head_sums(x * dgx, seg) * (1.0 / HD), seg, x.shape[1])
    return r * dgx - x * (r * r * r) * mean, dg_t


def _fold_heads(v):
    out = v[:, 0:HD]
    for h in range(1, v.shape[1] // HD):
        out = out + v[:, h * HD:(h + 1) * HD]
    return out


def _mix_forward(P, zc8, zh8, pkv, first, cw, qg_t, kg_t, gco, gao, seg, sink_ref, bias_ref, by_head=False):
    gate_b = P[:, 0:CW]
    gate_c = P[:, CW:2 * CW]
    hc = P[:, 2 * CW:3 * CW]
    z = gate_c * hc
    keep = jnp.where(first, 0.0, 1.0)
    zp = zc8 * zh8 * keep
    p1 = zp[7:8, :]
    p2 = zp[6:7, :]
    row = lax.broadcasted_iota(jnp.int32, (BLK, 1), 0)
    z1 = jnp.where(row == 0, p1, pltpu.roll(z, 1, 0))
    z2 = jnp.where(row == 0, p2, jnp.where(row == 1, p1, pltpu.roll(z, 2, 0)))
    cz = cw[0:1, :] * z2 + cw[1:2, :] * z1 + cw[2:3, :] * z
    y_conv = gate_b * cz

    scale = HD ** -0.5
    qi = lax.broadcasted_iota(jnp.int32, (BLK, 2 * BLK), 0)
    kj = lax.broadcasted_iota(jnp.int32, (BLK, 2 * BLK), 1)
    dd = qi + BLK - kj
    first_key = jnp.where(first, BLK, 0)
    valid = (dd >= 0) & (dd < BLK) & (kj >= first_key)

    q0 = 3 * CW
    k0 = q0 + AW
    v0 = k0 + NKV * HD
    q_raw = P[:, q0:k0]
    qn, rq = _head_norm(q_raw, qg_t, seg, by_head)
    qs = (qn * scale).astype(BF)
    k_raw = jnp.concatenate([pkv[:, 0:NKV * HD], P[:, k0:v0]], axis=0)
    kn, rk = _head_norm(k_raw, kg_t, seg, by_head)
    knb = kn.astype(BF)
    heads = []
    for h in range(NH):
        kv = h // GQ
        kb = knb[:, kv * HD:(kv + 1) * HD]
        vb = jnp.concatenate([pkv[:, NKV * HD + kv * HD:NKV * HD + (kv + 1) * HD],
                              P[:, v0 + kv * HD:v0 + (kv + 1) * HD]], axis=0).astype(BF)
        Q = qs[:, h * HD:(h + 1) * HD]
        S = _dot(Q, kb, 1, 1) + bias_ref[h * BLK:(h + 1) * BLK, :]
        S = jnp.where(valid, S, NEG_INF)
        sink = sink_ref[0, h]
        m = jnp.maximum(jnp.max(S, axis=-1, keepdims=True), sink)
        p = jnp.exp(S - m)
        es = jnp.exp(sink - m)
        denom = jnp.sum(p, axis=-1, keepdims=True) + es
        probs = p / denom
        O = _dot(probs.astype(BF), vb, 1, 0)
        heads.append(dict(kb=kb, vb=vb, Q=Q, probs=probs, psink=es / denom, O=O))
    y_attn = jnp.concatenate([hd["O"] for hd in heads], axis=1)

    rc = lax.rsqrt(jnp.mean(y_conv * y_conv, axis=-1, keepdims=True) + EPS)
    ra = lax.rsqrt(jnp.mean(y_attn * y_attn, axis=-1, keepdims=True) + EPS)
    y = jnp.concatenate([y_conv * rc * gco, y_attn * ra * gao], axis=1)
    return dict(gate_b=gate_b, gate_c=gate_c, hc=hc, z=z, z1=z1, z2=z2, cz=cz, y_conv=y_conv, y_attn=y_attn,
                rc=rc, ra=ra, heads=heads, y=y, row=row, scale=scale, q_raw=q_raw, rq=rq, k_raw=k_raw, rk=rk)


BPS = 2
TILE = BPS * BLK
KV0 = 3 * CW + AW


def _mix_in_specs(tile_of):
    return [
        pl.BlockSpec(memory_space=pltpu.SMEM),
        pl.BlockSpec((TILE, INW), lambda s: (tile_of(s), 0)),
        pl.BlockSpec((8, CW), lambda s: (jnp.maximum(tile_of(s) * (TILE // 8) - 1, 0), 1)),
        pl.BlockSpec((8, CW), lambda s: (jnp.maximum(tile_of(s) * (TILE // 8) - 1, 0), 2)),
        pl.BlockSpec((BLK, 2 * NKV * HD), lambda s: (jnp.maximum(tile_of(s) * BPS - 1, 0), KV0 // (2 * NKV * HD))),
    ]


def _block_inputs(tile, b, zc_ref, zh_ref, pkv_ref, first_tile):
    P = tile[b * BLK:(b + 1) * BLK, :]
    if b == 0:
        return P, zc_ref[...], zh_ref[...], pkv_ref[...], first_tile
    lo = b * BLK
    return P, tile[lo - 8:lo, CW:2 * CW], tile[lo - 8:lo, 2 * CW:3 * CW], tile[lo - BLK:lo, KV0:KV0 + 2 * NKV * HD], False


def _mix_param_specs():
    return [
        pl.BlockSpec((8, CW), lambda s: (0, 0)),
        pl.BlockSpec((1, AW), lambda s: (0, 0)),
        pl.BlockSpec((1, NKV * HD), lambda s: (0, 0)),
        pl.BlockSpec((1, CW), lambda s: (0, 0)),
        pl.BlockSpec((1, AW), lambda s: (0, 0)),
        pl.BlockSpec((AW, 128), lambda s: (0, 0)),
        pl.BlockSpec((NH * BLK, 2 * BLK), lambda s: (0, 0)),
    ]


def _mix_params(cw8, qg, kg, gco, gao, bias):
    seg = np.zeros((AW, 128), np.float32)
    seg[np.arange(AW), np.arange(AW) // HD] = 1.0
    return (cw8, jnp.tile(qg, (1, NH)), jnp.tile(kg, (1, NKV)), gco, gao, jnp.asarray(seg, BF), bias)


def _mix_fwd(proj, sinks, cw8, qg, kg, gco, gao, bias, comm=()):
    def body(sink_ref, p_ref, zc_ref, zh_ref, pkv_ref, cw_ref, qg_ref, kg_ref, gco_ref, gao_ref, seg_ref, bias_ref, y_ref):
        tile = p_ref[...]
        for b in range(BPS):
            f = _mix_forward(*_block_inputs(tile, b, zc_ref, zh_ref, pkv_ref, pl.program_id(0) == 0), cw_ref[...],
                             qg_ref[...], kg_ref[...], gco_ref[...], gao_ref[...], seg_ref[...], sink_ref, bias_ref, by_head=True)
            y_ref[b * BLK:(b + 1) * BLK, :] = f["y"].astype(BF)

    return _call(
        body, (sinks, proj, proj, proj, proj, *_mix_params(cw8, qg, kg, gco, gao, bias)), name="mix_fwd", grid=(T // TILE,),
        in_specs=_mix_in_specs(lambda s: s) + _mix_param_specs(),
        out_specs=[pl.BlockSpec((TILE, D), lambda s: (s, 0))], out_shape=[SDS((T, D), BF)],
        sem=("parallel",), vmem_mib=40, comm=comm, free=tuple(range(5, 12)))


def _mix_bwd(proj, dy, sinks, cw8, qg, kg, gco, gao, bias, comm=()):
    n_steps = T // TILE

    def tile_of(s):
        return n_steps - 1 - s

    def body(sink_ref, p_ref, zc_ref, zh_ref, pkv_ref, dy_ref, cw_ref, qg_ref, kg_ref, gco_ref, gao_ref, seg_ref, bias_ref,
             dproj_ref, dcw_ref, dqg_ref, dkg_ref, dgco_ref, dgao_ref, dsink_ref, dbias_ref,
             ndcz_ref, dkc_ref, dvc_ref):
        s = pl.program_id(0)

        @pl.when(s == 0)
        def _():
            for r in (dcw_ref, dqg_ref, dkg_ref, dgco_ref, dgao_ref, dsink_ref, dbias_ref, ndcz_ref, dkc_ref, dvc_ref):
                r[...] = jnp.zeros_like(r)

        params = (cw_ref[...], qg_ref[...], kg_ref[...], gco_ref[...], gao_ref[...], seg_ref[...])
        tile = p_ref[...]
        carry = (ndcz_ref[...], dkc_ref[...], dvc_ref[...])
        total = None
        for b in reversed(range(BPS)):
            f = _mix_forward(*_block_inputs(tile, b, zc_ref, zh_ref, pkv_ref, s == n_steps - 1), *params, sink_ref, bias_ref)
            pieces, sums, carry = one_block(f, dy_ref[b * BLK:(b + 1) * BLK, :], params, carry)
            for lo, piece in pieces:
                dproj_ref[b * BLK:(b + 1) * BLK, lo:lo + piece.shape[1]] = piece
            total = sums if total is None else [t + v for t, v in zip(total, sums)]
        ndcz_ref[...], dkc_ref[...], dvc_ref[...] = carry
        dcw, dqg_t, dkg_t, dgco, dgao, dsink, *ds = total
        dcw_ref[0:3, :] += dcw
        dqg_ref[...] += _fold_heads(dqg_t)
        dkg_ref[...] += _fold_heads(dkg_t)
        dgco_ref[...] += dgco
        dgao_ref[...] += dgao
        dsink_ref[...] += dsink
        for h in range(NH):
            dbias_ref[h * BLK:(h + 1) * BLK, :] += ds[h]

    def one_block(f, dy, params, carry):
        cw, qg_v, kg_v, gco_v, gao_v, seg = params
        nxt, dk_carry, dv_carry = carry
        dyc, dgco = _rms_bwd(dy[:, 0:CW], f["y_conv"], f["rc"], gco_v)
        dya, dgao = _rms_bwd(dy[:, CW:CW + AW], f["y_attn"], f["ra"], gao_v)

        row = f["row"]
        dgate_b = dyc * f["cz"]
        dcz = dyc * f["gate_b"]
        dcw = jnp.concatenate([jnp.sum(dcz * f[k], axis=0, keepdims=True) for k in ("z2", "z1", "z")], axis=0)
        n0 = nxt[0:1, :]
        n1 = nxt[1:2, :]
        d1 = jnp.where(row == BLK - 1, n0, pltpu.roll(dcz, BLK - 1, 0))
        d2 = jnp.where(row == BLK - 1, n1, jnp.where(row == BLK - 2, n0, pltpu.roll(dcz, BLK - 2, 0)))
        dz = cw[2:3, :] * dcz + cw[1:2, :] * d1 + cw[0:1, :] * d2
        pieces = [(0, dgate_b.astype(BF)), (CW, (dz * f["hc"]).astype(BF)), (2 * CW, (dz * f["gate_c"]).astype(BF))]

        scale = f["scale"]
        lane = lax.broadcasted_iota(jnp.int32, (1, 128), 1)
        dsink = jnp.zeros((1, 128), F32)
        dq_cols, dk_cols, dv_cols, dk_prev, dv_prev, ds = [], [], [], [], [], []
        for kv in range(NKV):
            dKb = dVb = 0.0
            for h in range(kv * GQ, (kv + 1) * GQ):
                hd = f["heads"][h]
                dO = dya[:, h * HD:(h + 1) * HD]
                delta = jnp.sum(dO * hd["O"], axis=-1, keepdims=True)
                dOb = dO.astype(BF)
                dP = _dot(dOb, hd["vb"], 1, 1)
                dS = hd["probs"] * (dP - delta)
                tot = jnp.sum(hd["psink"] * delta, axis=0, keepdims=True)
                dsink = dsink - jnp.where(lane == h, tot, 0.0)
                ds.append(dS)
                dSb = dS.astype(BF)
                dq_cols.append(_dot(dSb, hd["kb"], 1, 0))
                dKb = dKb + _dot(dSb, hd["Q"], 0, 0)
                dVb = dVb + _dot(hd["probs"].astype(BF), dOb, 0, 0)
            dk_cols.append(dKb[BLK:, :] + dk_carry[:, kv * HD:(kv + 1) * HD])
            dv_cols.append(dVb[BLK:, :] + dv_carry[:, kv * HD:(kv + 1) * HD])
            dk_prev.append(dKb[:BLK, :])
            dv_prev.append(dVb[:BLK, :])
        dq_raw, dqg_t = _head_norm_bwd(jnp.concatenate(dq_cols, axis=1) * scale, f["q_raw"], f["rq"], qg_v, seg)
        dk_raw, dkg_t = _head_norm_bwd(jnp.concatenate(dk_cols, axis=1), f["k_raw"][BLK:, :], f["rk"][BLK:, :], kg_v, seg)
        pieces.append((3 * CW, jnp.concatenate([dq_raw, dk_raw] + dv_cols, axis=1).astype(BF)))
        owed = (dcz[0:8, :], jnp.concatenate(dk_prev, axis=1), jnp.concatenate(dv_prev, axis=1))
        return pieces, [dcw, dqg_t, dkg_t, dgco, dgao, dsink, *ds], owed

    small = lambda r, c: pl.BlockSpec((r, c), lambda s: (0, 0))
    return _call(
        body, (sinks, proj, proj, proj, proj, dy, *_mix_params(cw8, qg, kg, gco, gao, bias)), name="mix_bwd", grid=(n_steps,),
        in_specs=_mix_in_specs(tile_of) + [pl.BlockSpec((TILE, D), lambda s: (tile_of(s), 0))] + _mix_param_specs(),
        out_specs=[pl.BlockSpec((TILE, INW), lambda s: (tile_of(s), 0)), small(8, CW), small(1, HD), small(1, HD),
                   small(1, CW), small(1, AW), small(1, 128), small(NH * BLK, 2 * BLK)],
        out_shape=[SDS((T, INW), BF), SDS((8, CW), F32), SDS((1, HD), F32), SDS((1, HD), F32), SDS((1, CW), F32),
                   SDS((1, AW), F32), SDS((1, 128), F32), SDS((NH * BLK, 2 * BLK), F32)],
        scratch_shapes=[pltpu.VMEM((8, CW), F32), pltpu.VMEM((BLK, NKV * HD), F32), pltpu.VMEM((BLK, NKV * HD), F32)],
        sem=("arbitrary",), vmem_mib=56, comm=comm, free=(1, 2, 3, 4) + tuple(range(6, 13)))


FT = 256
NFT = DFF // FT
RC = 1024
NCH = T // RC
LEAD = 16


def _rows8(x):
    return jnp.sum(x.reshape(x.shape[0] // 8, 8, x.shape[1]), axis=0)


def _ffn_act_specs():
    return [
        pl.BlockSpec((T, FT), lambda j: (0, j)), pl.BlockSpec((T, FT), lambda j: (0, NFT + j)),
        pl.BlockSpec((8, FT), lambda j: (0, j)), pl.BlockSpec((8, FT), lambda j: (0, NFT + j)),
        pl.BlockSpec((1, FT), lambda j: (0, j)), pl.BlockSpec((1, FT), lambda j: (0, NFT + j)),
    ]


def _conv_rows(win, w, b, n):
    win = win.astype(F32)
    u = win[LEAD:LEAD + n]
    u1 = pltpu.roll(win, 1, 0)[LEAD:LEAD + n]
    u2 = pltpu.roll(win, 2, 0)[LEAD:LEAD + n]
    return u2, u1, u, w[0:1, :] * u2 + w[1:2, :] * u1 + w[2:3, :] * u + b


def _ffn_act(up, fw8, fb, comm=()):
    def body(ug_ref, uv_ref, wg_ref, wv_ref, bg_ref, bv_ref, a_ref):
        wg, wv, bg, bv = wg_ref[...], wv_ref[...], bg_ref[...], bv_ref[...]

        def chunk(win_g, win_v):
            gp = _conv_rows(win_g, wg, bg, RC)[3]
            vp = _conv_rows(win_v, wv, bv, RC)[3]
            return (gp * jax.nn.sigmoid(gp) * vp).astype(BF)

        zero = jnp.zeros((LEAD, FT), BF)
        a_ref[0:RC, :] = chunk(jnp.concatenate([zero, ug_ref[0:RC, :]], axis=0),
                               jnp.concatenate([zero, uv_ref[0:RC, :]], axis=0))

        def step(i, carry):
            r0 = pl.multiple_of(i * RC, RC)
            win = pl.ds(r0 - LEAD, RC + LEAD)
            a_ref[pl.ds(r0, RC), :] = chunk(ug_ref[win, :], uv_ref[win, :])
            return carry

        lax.fori_loop(1, NCH, step, 0)

    return _call(
        body, (up, up, fw8, fw8, fb, fb), name="ffn_act", grid=(NFT,), in_specs=_ffn_act_specs(),
        out_specs=[pl.BlockSpec((T, FT), lambda j: (0, j))], out_shape=[SDS((T, DFF), BF)],
        sem=("parallel",), vmem_mib=40, comm=comm, free=(2, 3, 4, 5))


def _ffn_act_bwd(up, da, fw8, fb, comm=()):
    ext = RC + LEAD

    def body(ug_ref, uv_ref, wg_ref, wv_ref, bg_ref, bv_ref, da_ref,
             dug_ref, duv_ref, dwg_ref, dwv_ref, dbg_ref, dbv_ref):
        wg, wv, bg, bv = wg_ref[...], wv_ref[...], bg_ref[...], bv_ref[...]

        def chunk(win_g, win_v, da_e):
            g2, g1, g0, gp = _conv_rows(win_g, wg, bg, ext)
            v2, v1, v0, vp = _conv_rows(win_v, wv, bv, ext)
            da_e = da_e.astype(F32)
            sig = jax.nn.sigmoid(gp)
            dvp = da_e * (gp * sig)
            dgp = da_e * vp * (sig * (1.0 + gp * (1.0 - sig)))

            def back(dp, w):
                return (w[2:3, :] * dp[0:RC] + w[1:2, :] * pltpu.roll(dp, ext - 1, 0)[0:RC]
                        + w[0:1, :] * pltpu.roll(dp, ext - 2, 0)[0:RC]).astype(BF)

            def sums(dp, u2, u1, u0):
                d = dp[0:RC]
                return [_rows8(d), _rows8(d * u2[0:RC]), _rows8(d * u1[0:RC]), _rows8(d * u0[0:RC])]

            return back(dgp, wg), back(dvp, wv), sums(dgp, g2, g1, g0) + sums(dvp, v2, v1, v0)

        zero = jnp.zeros((LEAD, FT), BF)
        dug, duv, acc = chunk(jnp.concatenate([zero, ug_ref[0:ext, :]], axis=0),
                              jnp.concatenate([zero, uv_ref[0:ext, :]], axis=0), da_ref[0:ext, :])
        dug_ref[0:RC, :] = dug
        duv_ref[0:RC, :] = duv

        def step(i, acc):
            r0 = pl.multiple_of(i * RC, RC)
            win = pl.ds(r0 - LEAD, ext + LEAD)
            dug, duv, part = chunk(ug_ref[win, :], uv_ref[win, :], da_ref[pl.ds(r0, ext), :])
            dug_ref[pl.ds(r0, RC), :] = dug
            duv_ref[pl.ds(r0, RC), :] = duv
            return [a + p for a, p in zip(acc, part)]

        acc = lax.fori_loop(1, NCH - 1, step, acc)
        r0 = T - RC
        tail = lambda ref, lo: jnp.concatenate([ref[lo:T, :], zero], axis=0)
        dug, duv, part = chunk(tail(ug_ref, r0 - LEAD), tail(uv_ref, r0 - LEAD), tail(da_ref, r0))
        dug_ref[r0:T, :] = dug
        duv_ref[r0:T, :] = duv
        tot = [jnp.sum(a + p, axis=0, keepdims=True) for a, p in zip(acc, part)]
        for k, (dw_ref, db_ref) in enumerate(((dwg_ref, dbg_ref), (dwv_ref, dbv_ref))):
            db_ref[...] = tot[4 * k]
            dw_ref[...] = jnp.zeros_like(dw_ref)
            for r in range(3):
                dw_ref[r:r + 1, :] = tot[4 * k + 1 + r]

    col = lambda r: pl.BlockSpec((r, FT), lambda j: (0, j))
    return _call(
        body, (up, up, fw8, fw8, fb, fb, da), name="ffn_act_bwd", grid=(NFT,),
        in_specs=_ffn_act_specs() + [pl.BlockSpec((T, FT), lambda j: (0, j))],
        out_specs=[col(T), col(T), col(8), col(8), col(1), col(1)],
        out_shape=[SDS((T, DFF), BF), SDS((T, DFF), BF), SDS((8, DFF), F32), SDS((8, DFF), F32),
                   SDS((1, DFF), F32), SDS((1, DFF), F32)],
        sem=("parallel",), vmem_mib=40, comm=comm, free=(0, 1, 2, 3, 4, 5))


def _ffn_down_bwd(dh2b, w_down, comm=()):
    tm = TM

    def body(d_ref, w_ref, o_ref):
        o_ref[...] = _dot(d_ref[...], w_ref[...], 1, 1).astype(BF)

    return _call(
        body, (dh2b, w_down), name="ffn_down_bwd", grid=(T // tm,),
        in_specs=[pl.BlockSpec((tm, D), lambda i: (i, 0)), _resident((DFF, D))],
        out_specs=[pl.BlockSpec((tm, DFF), lambda i: (i, 0))], out_shape=[SDS((T, DFF), BF)],
        sem=("parallel",), vmem_mib=40, comm=comm, free=(0, 1))


def _norm_matmul_bwd(name, a_list, w_t, k_offsets, xin, g, dres, want_bf16, comm=(), slot=None):
    tm = TM
    ks = [a.shape[1] for a in a_list]
    n_a = len(a_list)
    n_pre = 0 if slot is None else 1

    def body(*refs):
        refs = refs[n_pre:]
        a_refs = refs[:n_a]
        w_ref, x_ref, g_ref, r_ref = refs[n_a:n_a + 4]
        outs = refs[n_a + 4:]
        dx_ref, dg_ref = outs[0], (outs[-1] if slot is None else outs[-1].at[0])

        @pl.when(pl.program_id(0) == 0)
        def _():
            dg_ref[...] = jnp.zeros_like(dg_ref)

        du = _dot(a_refs[0][...], w_ref[k_offsets[0]:k_offsets[0] + ks[0], :], 1, 0)
        for k in range(1, n_a):
            du = du + _dot(a_refs[k][...], w_ref[k_offsets[k]:k_offsets[k] + ks[k], :], 1, 0)
        x = x_ref[...]
        r = lax.rsqrt(jnp.mean(x * x, axis=-1, keepdims=True) + EPS)
        dx, dg = _rms_bwd(du, x, r, g_ref[...])
        dx = r_ref[...] + dx
        dx_ref[...] = dx
        if want_bf16:
            outs[1][...] = dx.astype(BF)
        dg_ref[...] += dg

    tile = lambda c: pl.BlockSpec((tm, c), lambda i, *_: (i, 0))
    if slot is None:
        dg_spec, dg_shape = pl.BlockSpec((1, D), lambda i: (0, 0)), SDS((1, D), F32)
    else:
        dg_spec, dg_shape = pl.BlockSpec((1, 1, D), lambda i, slot_ref: (slot_ref[0], 0, 0)), SDS((N_DEV, 1, D), F32)
    out_specs = [tile(D)] + ([tile(D)] if want_bf16 else []) + [dg_spec]
    out_shape = [SDS((T, D), F32)] + ([SDS((T, D), BF)] if want_bf16 else []) + [dg_shape]
    return _call(
        body, (*a_list, w_t, xin, g, dres), name=name, grid=(T // tm,), prefetch=() if slot is None else (slot,),
        in_specs=[tile(k) for k in ks] + [_resident(w_t.shape), tile(D),
                                           pl.BlockSpec((1, D), lambda i, *_: (0, 0)), tile(D)],
        out_specs=out_specs, out_shape=out_shape, sem=("arbitrary",), vmem_mib=56, comm=comm, free=tuple(range(n_a + 4)))


def _out_bwd(dh1b, w_out, comm=()):
    tm = TM

    def body(d_ref, w_ref, o_ref):
        o_ref[...] = _dot(d_ref[...], w_ref[...], 1, 1)

    return _call(
        body, (dh1b, w_out), name="out_bwd", grid=(T // tm,),
        in_specs=[pl.BlockSpec((tm, D), lambda i: (i, 0)), _resident((D, D))],
        out_specs=[pl.BlockSpec((tm, D), lambda i: (i, 0))], out_shape=[SDS((T, D), F32)],
        sem=("parallel",), vmem_mib=32, comm=comm, free=(0, 1))


def _wgrad(name, a_list, b, old_a, comm=()):
    m_k = a_list[0].shape[1]
    tm = max(t for t in range(128, m_k // 2 + 1, 128) if m_k % t == 0)
    steps = [a.shape[1] // tm for a in a_list]
    starts = [sum(steps[:k]) for k in range(len(a_list))]
    n_a = len(a_list)

    def body(*refs):
        a_refs, b_ref, o_ref = refs[:n_a], refs[n_a], refs[n_a + 1]
        i = pl.program_id(0)
        for k in range(n_a):
            @pl.when((i >= starts[k]) & (i < starts[k] + steps[k]))
            def _(k=k):
                o_ref[...] = _dot(a_refs[k][...], b_ref[...], 0, 0).astype(BF)

    def a_spec(k):
        return pl.BlockSpec((T, tm), lambda i: (0, jnp.clip(i - starts[k], 0, steps[k] - 1)))

    m_total = tm * sum(steps)
    return _call(
        body, (*a_list, b), name=name, grid=(sum(steps),),
        in_specs=[a_spec(k) for k in range(n_a)] + [_resident((T, D))],
        out_specs=[pl.BlockSpec((tm, D), lambda i: (i, 0))], out_shape=[SDS((m_total, D), BF)],
        sem=("parallel",), vmem_mib=40, comm=comm, free=() if old_a is None else tuple(range(n_a)) if old_a else (n_a,))


def _chip_sum(name, gbf, from_sib, core, chip):
    h = gbf.shape[1]
    th = h

    def body(core_ref, chip_ref, g_ref, s_ref, pbf_ref, own_ref):
        p = g_ref[0].astype(F32) + s_ref[0].astype(F32)
        pbf_ref[0] = p.astype(BF)

        @pl.when(pl.program_id(1) == chip_ref[0])
        def _():
            own_ref[...] = p

    grid_spec = pltpu.PrefetchScalarGridSpec(
        num_scalar_prefetch=2, grid=(h // th, N_CHIPS),
        in_specs=[pl.BlockSpec((1, th, D), lambda t, jj, core_ref, chip_ref: (2 * jj + core_ref[0], t, 0)),
                  pl.BlockSpec((1, th, D), lambda t, jj, core_ref, chip_ref: (jj, t, 0))],
        out_specs=[pl.BlockSpec((1, th, D), lambda t, jj, core_ref, chip_ref: (jj, t, 0)),
                   pl.BlockSpec((th, D), lambda t, jj, core_ref, chip_ref: (t, 0))],
    )
    return _pcall(
        body, name=name, grid_spec=grid_spec, out_shape=_in_hbm([SDS((N_CHIPS, h, D), BF), SDS((h, D), F32)]),
        compiler_params=_params(("arbitrary", "arbitrary"), 32),
    )(core, chip, *_from_hbm(gbf, from_sib))


def _final_sum(name, own, from_chips, core, comm=()):
    h = own.shape[0]

    def body(core_ref, o_ref, r_ref, f_ref):
        f_ref[0] = ((o_ref[...] + r_ref[0].astype(F32)) + r_ref[1].astype(F32)) + r_ref[2].astype(F32)

    return _call(
        body, (own, from_chips), name=name, grid=(1,), prefetch=(core,),
        in_specs=[pl.BlockSpec((h, D), lambda i, core_ref: (0, 0)), pl.BlockSpec((3, h, D), lambda i, core_ref: (0, 0, 0))],
        out_specs=[pl.BlockSpec((1, h, D), lambda i, core_ref: (core_ref[0], 0, 0))], out_shape=[SDS((2, h, D), F32)],
        sem=("arbitrary",), vmem_mib=40, comm=comm)


def _adam_math(w, g, m, v):
    nm = ADAM_B1 * m + (1.0 - ADAM_B1) * g
    nv = ADAM_B2 * v + (1.0 - ADAM_B2) * (g * g)
    m_hat = nm / (1.0 - ADAM_B1 ** ADAM_STEP)
    v_hat = nv / (1.0 - ADAM_B2 ** ADAM_STEP)
    return -ADAM_LR * (m_hat / (jnp.sqrt(v_hat) + ADAM_EPS) + ADAM_WD * w), nm, nv


def _adamw(name, w, g, m, v, tr, copy_g=False, stage=True, g_transposed=False):
    rows, cols = w.shape

    def body(w_ref, g_ref, m_ref, v_ref, *outs):
        d_ref, nm_ref, nv_ref = outs[-3:]
        for c in [pl.ds(c0, 128) for c0 in range(0, cols, 128)] if g_transposed else [slice(None)]:
            g_val = g_ref[c, :].T if g_transposed else g_ref[...]
            if copy_g:
                outs[0][:, c] = g_val
            d_ref[:, c], nm_ref[:, c], nv_ref[:, c] = _adam_math(w_ref[:, c], g_val, m_ref[:, c], v_ref[:, c])

    spec = pl.BlockSpec((tr, cols), lambda i: (i, 0))
    n_out = 4 if copy_g else 3
    g_spec = pl.BlockSpec((cols, tr), lambda i: (0, i)) if g_transposed else spec
    return _call(body, (w, g, m, v), name=name, grid=(rows // tr,), in_specs=[spec, g_spec, spec, spec], out_specs=[spec] * n_out,
                 out_shape=[SDS((rows, cols), F32)] * n_out, sem=("parallel",), vmem_mib=32,
                 free=(0, 2, 3) if stage else ())


C_SQ = 2 * DFF
P_W = C_SQ + 128
R_G2, R_GO, R_DCW, R_QK = 0, 1, 2, 5
C_GCO, C_GAO, C_DQG, C_DKG, C_SINK = 0, CW, 0, 128, 256


def _pack(name, me, ins, width, fill):
    def body(me_ref, *refs):
        o = refs[-1]
        o[...] = jnp.zeros_like(o)
        fill(o, *refs[:-1])

    return _call(body, ins, name=name, grid=(1,), prefetch=(me,),
                 in_specs=[pl.BlockSpec(a.shape, lambda i, me_ref: (0, 0)) for a in ins],
                 out_specs=[pl.BlockSpec((1, 8, width), lambda i, me_ref: (me_ref[0], 0, 0))],
                 out_shape=[SDS((N_DEV, 8, width), F32)], sem=("arbitrary",))[0]


def _pack_ffn(me, dfwg, dfwv, dfbg, dfbv, sq):
    def fill(o, dfwg_r, dfwv_r, dfbg_r, dfbv_r, sq_r):
        o[0, :, 0:DFF] = dfwg_r[...]
        o[0, :, DFF:2 * DFF] = dfwv_r[...]
        o[0, 3:4, 0:DFF] = dfbg_r[...]
        o[0, 3:4, DFF:2 * DFF] = dfbv_r[...]
        o[0, :, C_SQ:C_SQ + 128] = sq_r[...]

    return _pack("pack_ffn", me, (dfwg, dfwv, dfbg, dfbv, sq), P_W, fill)


def _pack_mix(me, dg2, dgco, dgao, dcw8, dqg, dkg, dsink):
    def fill(o, dg2_r, dgco_r, dgao_r, dcw_r, dqg_r, dkg_r, dsink_r):
        o[0, R_G2:R_G2 + 1, :] = dg2_r[...]
        o[0, R_GO:R_GO + 1, C_GCO:C_GCO + CW] = dgco_r[...]
        o[0, R_GO:R_GO + 1, C_GAO:C_GAO + AW] = dgao_r[...]
        o[0, R_DCW:R_DCW + 3, 0:CW] = dcw_r[0:3, :]
        o[0, R_QK:R_QK + 1, C_DQG:C_DQG + HD] = dqg_r[...]
        o[0, R_QK:R_QK + 1, C_DKG:C_DKG + HD] = dkg_r[...]
        o[0, R_QK:R_QK + 1, C_SINK:C_SINK + 128] = dsink_r[...]

    return _pack("pack_mix", me, (dg2, dgco, dgao, dcw8, dqg, dkg, dsink), D, fill)


N_SMALL = 11


def _small_adam(chip, p_all, pm_all, g1_all, tbl_all, ws, ms, vs):
    fw_cols = 2 * DFF // N_CHIPS
    cw_cols = CW // N_CHIPS

    def body(chip_ref, p_ref, fw_ref, pm_ref, cw_ref, g1_ref, tbl_ref, *refs):
        w_r, m_r, v_r = refs[0:N_SMALL], refs[N_SMALL:2 * N_SMALL], refs[2 * N_SMALL:3 * N_SMALL]
        outs = refs[3 * N_SMALL:]
        g_o, d_o, nm_o, nv_o = (outs[k * N_SMALL:(k + 1) * N_SMALL] for k in range(4))
        loss_o = outs[4 * N_SMALL]

        def total(ref):
            s = ref[0]
            for k in range(1, N_DEV):
                s = s + ref[k]
            return s

        S = total(p_ref)
        fw = total(fw_ref)
        M = total(pm_ref)
        cw = total(cw_ref)

        def step(i, g, at):
            d, nm, nv = _adam_math(w_r[i][at], g, m_r[i][at], v_r[i][at])
            g_o[i][at], d_o[i][at], nm_o[i][at], nv_o[i][at] = g, d, nm, nv

        everything = (slice(None), slice(None))
        step(0, total(g1_ref), everything)
        for r in range(3):
            step(1, cw[R_DCW + r:R_DCW + r + 1, :], (r, slice(None), slice(None)))
        step(2, M[R_QK:R_QK + 1, C_DQG:C_DQG + HD], everything)
        step(3, M[R_QK:R_QK + 1, C_DKG:C_DKG + HD], everything)
        step(4, total(tbl_ref), everything)
        step(5, M[R_QK:R_QK + 1, C_SINK:C_SINK + NH], everything)
        step(6, M[R_GO:R_GO + 1, C_GCO:C_GCO + CW], everything)
        step(7, M[R_GO:R_GO + 1, C_GAO:C_GAO + AW], everything)
        step(8, M[R_G2:R_G2 + 1, :], everything)
        for r in range(3):
            step(9, fw[r:r + 1, :], (r, slice(None), slice(None)))
        step(10, S[3:4, 0:2 * DFF], everything)
        sq = S[:, C_SQ:C_SQ + 128]
        loss_o[...] = jnp.sum(jnp.sum(sq, axis=1, keepdims=True), axis=0, keepdims=True) * (0.5 / D)

    def full(a):
        n = len(a.shape)
        return pl.BlockSpec(a.shape, lambda i, chip_ref: (0,) * n)

    params = [*ws, *ms, *vs]
    out = _call(
        body, (p_all, p_all, pm_all, pm_all, g1_all, tbl_all, *params), name="small_adam", grid=(1,), prefetch=(chip,),
        in_specs=[full(p_all),
                  pl.BlockSpec((N_DEV, 8, fw_cols), lambda i, chip_ref: (0, 0, chip_ref[0])),
                  full(pm_all),
                  pl.BlockSpec((N_DEV, 8, cw_cols), lambda i, chip_ref: (0, 0, chip_ref[0])),
                  full(g1_all), full(tbl_all), *[full(a) for a in params]],
        out_specs=[full(a) for a in ws] * 4 + [pl.BlockSpec((1, 1), lambda i, chip_ref: (0, 0))],
        out_shape=[SDS(a.shape, F32) for a in ws] * 4 + [SDS((1, 1), F32)], sem=("arbitrary",), vmem_mib=32)
    return out[0:N_SMALL], out[N_SMALL:2 * N_SMALL], out[2 * N_SMALL:3 * N_SMALL], out[3 * N_SMALL:4 * N_SMALL], out[4 * N_SMALL]


PLACE_STEPS = 4


def _place_specs(shards):
    rows = [s.shape[0] // PLACE_STEPS for s in shards]
    return ([pl.BlockSpec((r, D), lambda i, chip_ref: (i, 0)) for r in rows],
            [pl.BlockSpec((r, D), lambda i, chip_ref: (chip_ref[0] * PLACE_STEPS + i, 0)) for r in rows],
            [SDS((N_CHIPS * s.shape[0], D), BF) for s in shards])


def _place_first(chip, shard, conv_w, ffn_conv_w):
    def body(chip_ref, a, s0, s1, o, t0, t1):
        o[...] = a[...].astype(BF)

        @pl.when(pl.program_id(0) == 0)
        def _():
            for s, t in ((s0, t0), (s1, t1)):
                t[...] = jnp.zeros_like(t)
                t[0, 0:3, :] = s[...]

    ins, outs, shapes = _place_specs([shard])
    taps = (conv_w, ffn_conv_w)
    return _call(
        body, (shard, conv_w, ffn_conv_w), name="place_first", grid=(PLACE_STEPS,), prefetch=(chip,),
        in_specs=ins + [pl.BlockSpec(s.shape, lambda i, chip_ref: (0, 0)) for s in taps],
        out_specs=outs + [pl.BlockSpec((1, 8, s.shape[1]), lambda i, chip_ref: (chip_ref[0], 0, 0)) for s in taps],
        out_shape=shapes + [SDS((N_CHIPS, 8, s.shape[1]), F32) for s in taps],
        sem=("arbitrary",), vmem_mib=32, free=(1, 2))


def _place_rest(chip, shards, w_up, table, bucket, comm):
    n = len(shards)
    c_up = w_up.shape[1]
    edges = [round(k * (c_up // 128) / PLACE_STEPS) * 128 for k in range(PLACE_STEPS + 1)]

    def body(chip_ref, *refs):
        a, (up_ref, tab_ref, bk_ref), o = refs[:n], refs[n:n + 3], refs[n + 3:2 * n + 3]
        up_o, bias_ref = refs[2 * n + 3:]
        for src, dst in zip(a, o):
            dst[...] = src[...].astype(BF)
        for k in range(PLACE_STEPS):
            @pl.when(pl.program_id(0) == k)
            def _(k=k):
                up_o[edges[k]:edges[k + 1], :] = up_ref[:, edges[k]:edges[k + 1]].T.astype(BF)

        @pl.when(pl.program_id(0) == 0)
        def _():
            bk = bk_ref[...]
            eq = [bk == b for b in range(NBUCKET)]
            for h in range(NH):
                acc = jnp.zeros((BLK, 2 * BLK), F32)
                for b in range(NBUCKET):
                    acc = jnp.where(eq[b], tab_ref[h, b], acc)
                bias_ref[h * BLK:(h + 1) * BLK, :] = acc

    ins, outs, shapes = _place_specs(shards)
    return _call(
        body, (*shards, w_up, table, bucket), name="place_rest", grid=(PLACE_STEPS,), prefetch=(chip,),
        in_specs=ins + [_resident(w_up.shape), pl.BlockSpec(memory_space=pltpu.SMEM),
                        pl.BlockSpec(bucket.shape, lambda i, chip_ref: (0, 0))],
        out_specs=outs + [pl.BlockSpec((c_up, D), lambda i, chip_ref: (chip_ref[0], 0)),
                          pl.BlockSpec((NH * BLK, 2 * BLK), lambda i, chip_ref: (0, 0))],
        out_shape=shapes + [SDS((N_CHIPS * c_up, D), BF), SDS((NH * BLK, 2 * BLK), F32)],
        sem=("arbitrary",), vmem_mib=32, comm=comm, free=(n + 1, n + 2))


def kernel(x, norm_mix_g, w_in, conv_w, q_norm_g, k_norm_g, rel_bias_table, sinks, out_norm_conv_g, out_norm_attn_g, w_out, norm_ffn_g, w_up, ffn_conv_w, ffn_conv_b, w_down, loss_target, m_norm_mix_g, m_w_in, m_conv_w, m_q_norm_g, m_k_norm_g, m_rel_bias_table, m_sinks, m_out_norm_conv_g, m_out_norm_attn_g, m_w_out, m_norm_ffn_g, m_w_up, m_ffn_conv_w, m_ffn_conv_b, m_w_down, v_norm_mix_g, v_w_in, v_conv_w, v_q_norm_g, v_k_norm_g, v_rel_bias_table, v_sinks, v_out_norm_conv_g, v_out_norm_attn_g, v_w_out, v_norm_ffn_g, v_w_up, v_ffn_conv_w, v_ffn_conv_b, v_w_down):
    as_arg = lambda i: jnp.reshape(i, (1,)).astype(jnp.int32)
    chip = as_arg(2 * lax.axis_index("x") + lax.axis_index("y"))
    core = as_arg(lax.axis_index("c"))
    me = 2 * chip + core
    xs, tgt = x[0], loss_target[0]
    qg, kg, gco, gao, g1, g2, fb = q_norm_g, k_norm_g, out_norm_conv_g, out_norm_attn_g, norm_mix_g, norm_ffn_g, ffn_conv_b
    pieces = lambda g: g.reshape(N_DEV, g.shape[0] // N_DEV, D)
    whole = lambda f: f.reshape(2 * f.shape[1], D)

    bucket = jnp.asarray(_bucket_table())
    p_in, p_cw, p_fw = _place_first(chip, w_in[0].T, conv_w[0], ffn_conv_w[0])
    p_out, p_down, p_up, bias, w_int, cw_all, fw_all = _place_rest(
        chip, [w_out[0], w_down[0]], w_up[0], rel_bias_table.T, bucket,
        comm=[_t_gather(p_in, relayed_first=True), _t_small_weights(p_cw), _t_small_weights(p_fw)])
    cw8 = jnp.transpose(cw_all, (1, 0, 2)).reshape(8, CW)
    fw8 = jnp.transpose(fw_all, (1, 0, 2)).reshape(8, 2 * DFF)

    early = 3 / 11
    proj, u1, w_out_f, p_up = _inproj(xs, g1, w_int, comm=[_t_gather(p_out), _t_gather(p_up, (0, early))])
    y, w_upt = _mix_fwd(proj, sinks, cw8, qg, kg, gco, gao, bias, comm=[_t_gather(p_up, (early, 1))])
    h1, u2 = _outproj(y, w_out_f, xs, g2)
    up, w_down_f = _ffn_up(u2, w_upt, comm=[_t_gather(p_down)])
    a, = _ffn_act(up, fw8, fb)
    dh2, dh2b, sq = _ffn_down(a, w_down_f, h1, tgt)

    gdbf, = _wgrad("wgrad_down", [a], dh2b, None)
    da, sib_down = _ffn_down_bwd(dh2b, w_down_f, comm=[_t_sibling(pieces(gdbf))])
    pbf_down, own_down = _chip_sum("chip_sum_w_down", pieces(gdbf), sib_down, core, chip)
    dug, duv, dfwg, dfwv, dfbg, dfbv, chips_down = _ffn_act_bwd(up, da, fw8, fb, comm=[_t_chips(pbf_down)])
    fin_down, = _final_sum("final_sum_w_down", own_down, chips_down, core)
    gubf, = _wgrad("wgrad_up", [dug, duv], u2, False)
    p_all = _pack_ffn(me, dfwg, dfwv, dfbg, dfbv, sq)
    dh1, dh1b, dg2, sib_up, fin_down, p_all = _norm_matmul_bwd(
        "ffn_up_bwd", [dug, duv], w_upt, [0, DFF], h1, g2, dh2, True,
        comm=[_t_sibling(pieces(gubf)), _t_swap(fin_down), _t_allgather(p_all)])
    pbf_up, own_up = _chip_sum("chip_sum_w_up", pieces(gubf), sib_up, core, chip)
    gobf, = _wgrad("wgrad_out", [y], dh1b, True)
    dy, sib_out = _out_bwd(dh1b, w_out_f, comm=[_t_sibling(pieces(gobf))])
    pbf_out, own_out = _chip_sum("chip_sum_w_out", pieces(gobf), sib_out, core, chip)
    dproj, dcw8, dqg, dkg, dgco, dgao, dsink, dbias, chips_up = _mix_bwd(
        proj, dy, sinks, cw8, qg, kg, gco, gao, bias, comm=[_t_chips(pbf_up)])
    fin_up, = _final_sum("final_sum_w_up", own_up, chips_up, core)
    tbl_all = _band_bias_bwd(dbias, bucket, me)
    pm_all = _pack_mix(me, dg2, dgco, dgao, dcw8, dqg, dkg, dsink)
    gibf, chips_out, fin_up, pm_all, tbl_all = _wgrad(
        "wgrad_in", [dproj], u1, False,
        comm=[_t_chips(pbf_out), _t_swap(fin_up), _t_allgather(pm_all), _t_allgather(tbl_all)])
    fin_out, sib_in = _final_sum("final_sum_w_out", own_out, chips_out, core, comm=[_t_sibling(pieces(gibf))])
    pbf_in, own_in = _chip_sum("chip_sum_w_in", pieces(gibf), sib_in, core, chip)
    dx, g1_all, chips_in, fin_out = _norm_matmul_bwd(
        "in_bwd", [dproj], w_int, [0], xs, g1, dh1, False, comm=[_t_chips(pbf_in), _t_swap(fin_out)], slot=me)
    fin_in, = _final_sum("final_sum_w_in", own_in, chips_in, core)
    g1_all, fin_in = _comm_call("gather_last", [_t_allgather(g1_all), _t_swap(fin_in)])

    g_w_out, g_w_down = whole(fin_out), whole(fin_down)
    g_w_down, d_down, nm_down, nv_down = _adamw("adamw_w_down", w_down[0], g_w_down, m_w_down[0], v_w_down[0], 352, True)
    g_w_up, d_up, nm_up, nv_up = _adamw(
        "adamw_w_up", w_up[0], whole(fin_up), m_w_up[0], v_w_up[0], 256, True, stage=False, g_transposed=True)
    g_w_out, d_out, nm_out, nv_out = _adamw("adamw_w_out", w_out[0], g_w_out, m_w_out[0], v_w_out[0], 256, True, stage=False)
    g_w_in, d_in, nm_in, nv_in = [a.T for a in _adamw(
        "adamw_w_in", w_in[0].T, whole(fin_in), m_w_in[0].T, v_w_in[0].T, INW // N_CHIPS // 3, True, stage=False)]
    taps = lambda a: jnp.transpose(a, (1, 0, 2))
    sw = [norm_mix_g, taps(conv_w), q_norm_g, k_norm_g, rel_bias_table.T, sinks, out_norm_conv_g, out_norm_attn_g,
          norm_ffn_g, taps(ffn_conv_w), ffn_conv_b]
    smm = [m_norm_mix_g, taps(m_conv_w), m_q_norm_g, m_k_norm_g, m_rel_bias_table.T, m_sinks, m_out_norm_conv_g,
           m_out_norm_attn_g, m_norm_ffn_g, taps(m_ffn_conv_w), m_ffn_conv_b]
    smv = [v_norm_mix_g, taps(v_conv_w), v_q_norm_g, v_k_norm_g, v_rel_bias_table.T, v_sinks, v_out_norm_conv_g,
           v_out_norm_attn_g, v_norm_ffn_g, taps(v_ffn_conv_w), v_ffn_conv_b]
    *small_out, loss = _small_adam(chip, p_all, pm_all, g1_all, tbl_all, sw, smm, smv)
    sg, sd, snm, snv = [list(r) for r in small_out]
    for r in (sg, sd, snm, snv):
        r[1], r[4], r[9] = taps(r[1]), r[4].T, taps(r[9])

    def order(s, b_in, b_out, b_up, b_down):
        return (s[0], b_in[None], s[1], s[2], s[3], s[4], s[5], s[6], s[7], b_out[None], s[8], b_up[None],
                s[9], s[10], b_down[None])

    return (loss.reshape(()), dx[None],
            *order(sg, g_w_in, g_w_out, g_w_up, g_w_down),
            *order(sd, d_in, d_out, d_up, d_down),
            *order(snm, nm_in, nm_out, nm_up, nm_down),
            *order(snv, nv_in, nv_out, nv_up, nv_down))
```

```python
import functools
import math

import numpy as np

import jax
import jax.numpy as jnp
from jax import lax
from jax.experimental import pallas as pl
from jax.experimental.pallas import tpu as pltpu

F32 = jnp.float32
BF = jnp.bfloat16
SDS = jax.ShapeDtypeStruct

T = 2048
D = 1024
CW = 512
AW = 512
HD = 64
NH = 8
NKV = 2
GQ = 4
INW = 2304
DFF = 2816
BLK = 128
NB = T // BLK
NBUCKET = 32
EPS = 1e-6
NEG_INF = -1e30
N_CHIPS = 4
N_DEV = 8

ADAM_LR = 0.001
ADAM_B1 = 0.9
ADAM_B2 = 0.999
ADAM_EPS = 1e-08
ADAM_WD = 0.01
ADAM_STEP = 10

TM = 512
MIB = 1024 * 1024
MESH = pl.DeviceIdType.MESH
ANY = pl.BlockSpec(memory_space=pl.ANY)

_pcall = pl.pallas_call


def _params(sem=None, vmem_mib=None, collective_id=None):
    kw = {} if collective_id is None else {"collective_id": collective_id}
    if sem is not None:
        kw["dimension_semantics"] = sem
    if vmem_mib is not None:
        kw["vmem_limit_bytes"] = vmem_mib * MIB
    return pltpu.CompilerParams(**kw)


def _resident(shape):
    return pl.BlockSpec(shape, lambda *_: (0,) * len(shape), pipeline_mode=pl.Buffered(1))


def _dot(a, b, ca, cb):
    return lax.dot_general(a, b, (((ca,), (cb,)), ((), ())), preferred_element_type=F32)


def _rms_bwd(dy, x, r, g):
    dg = jnp.sum(dy * (x * r), axis=0, keepdims=True)
    dgx = dy * g
    dx = r * dgx - x * (r * r * r) * jnp.mean(x * dgx, axis=-1, keepdims=True)
    return dx, dg


def _where():
    x, y, c = lax.axis_index("x"), lax.axis_index("y"), lax.axis_index("c")
    return x, y, c, [(1 - x, y), (x, 1 - y), (1 - x, 1 - y)]


def _rcopy(src, dst, ssem, rsem, dev):
    return pltpu.make_async_remote_copy(src_ref=src, dst_ref=dst, send_sem=ssem, recv_sem=rsem, device_id=dev,
                                        device_id_type=MESH)


SIBLING, Y_CHIP, X_CHIP, DIAGONAL_CHIP = 1, 2, 4, 6
OTHER_CHIPS = (Y_CHIP, X_CHIP, DIAGONAL_CHIP)
EVERYONE = tuple(range(1, N_DEV))
BARRIER_OF = {(SIBLING,): 0, (SIBLING, Y_CHIP, X_CHIP): 1, OTHER_CHIPS: 2, (SIBLING,) + OTHER_CHIPS: 3, EVERYONE: 4}


def _peer(rel):
    x, y, c, _ = _where()
    return x ^ ((rel >> 2) & 1), y ^ ((rel >> 1) & 1), c ^ (rel & 1)


class _Task:
    def __init__(self, ins, outs, alias, n_sem, start, finish, middle=None, peers=()):
        self.ins, self.outs, self.alias, self.n_sem, self.start, self.finish = ins, outs, alias, n_sem, start, finish
        self.middle = middle if middle is not None else (lambda *args: None)
        self.peers = peers


def _peers_of(comm):
    return tuple(sorted({p for t in comm for p in t.peers}))


def _enter(comm):
    peers = _peers_of(comm)
    barrier = pltpu.get_barrier_semaphore()
    for rel in peers:
        pl.semaphore_signal(barrier, inc=1, device_id=_peer(rel), device_id_type=MESH)
    pl.semaphore_wait(barrier, len(peers))


ROWS16 = 16


def _t_gather(placed, part=(0, 1), relayed_first=False):
    R = placed.shape[0] // N_CHIPS
    q = R // 4
    lo, hi = (round(f * (q // ROWS16)) * ROWS16 for f in part)

    def quarter(chip_index, core, k):
        return pl.ds(pl.multiple_of(chip_index * R + core * 2 * q + k * q + lo, ROWS16), hi - lo)

    def places():
        x, y, c, _ = _where()
        return c, 2 * x + y, 2 * (1 - x) + y, 2 * x + (1 - y), 2 * (1 - x) + (1 - y), (1 - x, y, c), (x, 1 - y, c), (x, y, 1 - c)

    def copy(buf, k, chip_index, core, quart, ss, rs, b, dev):
        window = buf.at[quarter(chip_index, core, quart)]
        return _rcopy(window, window, ss.at[b + k], rs.at[b + k], dev)

    def first_hop(cout, ss, rs, b, which):
        c, me, _, _, _, x_nbr, y_nbr, _ = places()
        for k, (quart, dev) in enumerate(((0, x_nbr), (1, y_nbr), (1, x_nbr), (0, y_nbr))):
            if k in which:
                copy(cout[0], k, me, c, quart, ss, rs, b, dev).start()

    def start(cin, cout, ss, rs, b):
        first_hop(cout, ss, rs, b, (0, 1) if relayed_first else (0, 1, 2, 3))

    def middle(cin, cout, ss, rs, b):
        c, _, xc, yc, _, x_nbr, y_nbr, sib = places()
        for k, chip_index, quart, dev in ((0, xc, 0, y_nbr), (1, yc, 1, x_nbr)):
            copy(cout[0], k, chip_index, c, quart, ss, rs, b, dev).wait_recv()
            copy(cout[0], 4 + k, chip_index, c, quart, ss, rs, b, dev).start()
            copy(cout[0], 6 + k, chip_index, c, quart, ss, rs, b, sib).start()
        if relayed_first:
            first_hop(cout, ss, rs, b, (2, 3))

    later = ((2, 1, 1), (3, 2, 0), (4, 3, 0), (5, 3, 1))

    def finish(cin, cout, ss, rs, b):
        c, me, xc, yc, dc, _, _, sib = places()
        chip_of = {1: xc, 2: yc, 3: dc}
        for k, whose, quart in later:
            copy(cout[0], k, chip_of[whose], c, quart, ss, rs, b, sib).wait_recv()
            copy(cout[0], 6 + k, chip_of[whose], c, quart, ss, rs, b, sib).start()
        for k, whose, quart in ((0, 1, 0), (1, 2, 1)) + later:
            copy(cout[0], 6 + k, chip_of[whose], 1 - c, quart, ss, rs, b, sib).wait_recv()
        for k in range(12):
            copy(cout[0], k, me, c, 0, ss, rs, b, sib).wait_send()

    return _Task([placed], [SDS(placed.shape, placed.dtype)], [(0, 0)], 12, start, finish, middle, peers=(SIBLING, Y_CHIP, X_CHIP))


def _t_small_weights(buf):
    def start(cin, cout, ss, rs, b):
        x, y, c, chips = _where()
        mine = cout[0].at[2 * x + y]
        for r, (px, py) in enumerate(chips):
            _rcopy(mine, mine, ss.at[b + r], rs.at[b + r], (px, py, c)).start()

    def finish(cin, cout, ss, rs, b):
        x, y, c, chips = _where()
        for r, (px, py) in enumerate(chips):
            got = cout[0].at[2 * px + py]
            _rcopy(got, got, ss.at[b + r], rs.at[b + r], (px, py, c)).wait_recv()
        for r, (px, py) in enumerate(chips):
            mine = cout[0].at[2 * x + y]
            _rcopy(mine, mine, ss.at[b + r], rs.at[b + r], (px, py, c)).wait_send()

    return _Task([buf], [SDS(buf.shape, buf.dtype)], [(0, 0)], 3, start, finish, peers=OTHER_CHIPS)


def _t_sibling(gbf):
    def start(cin, cout, ss, rs, b):
        x, y, c, _ = _where()
        for jj in range(N_CHIPS):
            _rcopy(cin[0].at[2 * jj + (1 - c)], cout[0].at[jj], ss.at[b + jj], rs.at[b + jj], (x, y, 1 - c)).start()

    def finish(cin, cout, ss, rs, b):
        x, y, c, _ = _where()
        for jj in range(N_CHIPS):
            got = cout[0].at[jj]
            _rcopy(got, got, ss.at[b + jj], rs.at[b + jj], (x, y, 1 - c)).wait_recv()
        for jj in range(N_CHIPS):
            got = cout[0].at[jj]
            _rcopy(got, got, ss.at[b + jj], rs.at[b + jj], (x, y, 1 - c)).wait_send()

    return _Task([gbf], [SDS((N_CHIPS,) + gbf.shape[1:], BF)], [], N_CHIPS, start, finish, peers=(SIBLING,))


def _t_chips(pbf):
    def start(cin, cout, ss, rs, b):
        x, y, c, chips = _where()
        for r, (px, py) in enumerate(chips):
            _rcopy(cin[0].at[2 * px + py], cout[0].at[r], ss.at[b + r], rs.at[b + r], (px, py, c)).start()

    def finish(cin, cout, ss, rs, b):
        x, y, c, chips = _where()
        for r, (px, py) in enumerate(chips):
            got = cout[0].at[r]
            _rcopy(got, got, ss.at[b + r], rs.at[b + r], (px, py, c)).wait_recv()
        for r, (px, py) in enumerate(chips):
            got = cout[0].at[r]
            _rcopy(got, got, ss.at[b + r], rs.at[b + r], (px, py, c)).wait_send()

    return _Task([pbf], [SDS((3,) + pbf.shape[1:], BF)], [], 3, start, finish, peers=OTHER_CHIPS)


def _t_swap(fin):
    def start(cin, cout, ss, rs, b):
        x, y, c, _ = _where()
        mine = cout[0].at[c]
        _rcopy(mine, mine, ss.at[b], rs.at[b], (x, y, 1 - c)).start()

    def finish(cin, cout, ss, rs, b):
        x, y, c, _ = _where()
        got = cout[0].at[1 - c]
        _rcopy(got, got, ss.at[b], rs.at[b], (x, y, 1 - c)).wait_recv()
        _rcopy(got, got, ss.at[b], rs.at[b], (x, y, 1 - c)).wait_send()

    return _Task([fin], [SDS(fin.shape, fin.dtype)], [(0, 0)], 1, start, finish, peers=(SIBLING,))


def _t_allgather(buf):
    def peers():
        x, y, c, _ = _where()
        out = []
        for rel in range(1, N_DEV):
            px, py, pc = x ^ ((rel >> 2) & 1), y ^ ((rel >> 1) & 1), c ^ (rel & 1)
            out.append((rel - 1, 4 * px + 2 * py + pc, (px, py, pc)))
        return 4 * x + 2 * y + c, out

    def start(cin, cout, ss, rs, b):
        me, ps = peers()
        mine = cout[0].at[me]
        for k, _, dev in ps:
            _rcopy(mine, mine, ss.at[b + k], rs.at[b + k], dev).start()

    def finish(cin, cout, ss, rs, b):
        me, ps = peers()
        for k, pidx, dev in ps:
            got = cout[0].at[pidx]
            _rcopy(got, got, ss.at[b + k], rs.at[b + k], dev).wait_recv()
        for k, _, dev in ps:
            mine = cout[0].at[me]
            _rcopy(mine, mine, ss.at[b + k], rs.at[b + k], dev).wait_send()

    return _Task([buf], [SDS(buf.shape, buf.dtype)], [(0, 0)], N_DEV - 1, start, finish, peers=EVERYONE)


def _run_tasks(comm, which, cin, cout, ss, rs):
    i0 = o0 = s0 = 0
    for t in comm:
        getattr(t, which)(cin[i0:i0 + len(t.ins)], cout[o0:o0 + len(t.outs)], ss, rs, s0)
        i0, o0, s0 = i0 + len(t.ins), o0 + len(t.outs), s0 + t.n_sem


def _from_hbm(*arrays):
    return [pltpu.with_memory_space_constraint(a, pltpu.HBM) for a in arrays]


def _in_hbm(shapes):
    return [pltpu.HBM(s.shape, s.dtype) for s in shapes]


def _comm_layout(comm, n_in, n_out):
    c_in = [a for t in comm for a in t.ins]
    c_out = [s for t in comm for s in t.outs]
    aliases, i0, o0 = {}, 0, 0
    for t in comm:
        for i, o in t.alias:
            aliases[n_in + i0 + i] = n_out + o0 + o
        i0, o0 = i0 + len(t.ins), o0 + len(t.outs)
    return c_in, c_out, aliases, sum(t.n_sem for t in comm)


def _call(body, operands, *, name, grid, in_specs, out_specs, out_shape, scratch_shapes=(), sem=None, vmem_mib=None, comm=(),
          free=(), prefetch=()):
    operands = [o if s.memory_space == pltpu.SMEM or k in free else pltpu.with_memory_space_constraint(o, pltpu.HBM)
                for k, (o, s) in enumerate(zip(operands, in_specs))]
    n_pre, n_in, n_out, n_scr = len(prefetch), len(in_specs), len(out_specs), len(scratch_shapes)
    c_in, c_out, aliases, n_sem = _comm_layout(comm, n_pre + n_in, n_out)
    sems = [pltpu.SemaphoreType.DMA((n_sem,)), pltpu.SemaphoreType.DMA((n_sem,))] if comm else []

    def wrapped(*refs):
        pre, refs = refs[:n_pre], refs[n_pre:]
        ins, cin = refs[:n_in], refs[n_in:n_in + len(c_in)]
        rest = refs[n_in + len(c_in):]
        outs, cout = rest[:n_out], rest[n_out:n_out + len(c_out)]
        rest = rest[n_out + len(c_out):]
        scr, csem = rest[:n_scr], rest[n_scr:]
        if not comm:
            return body(*pre, *ins, *outs, *scr)
        step = functools.reduce(lambda acc, k: acc * grid[k] + pl.program_id(k), range(len(grid)), 0)
        n_steps = math.prod(grid)

        @pl.when(step == 0)
        def _():
            _enter(comm)
            _run_tasks(comm, "start", cin, cout, *csem)

        pl.when(step == n_steps // 2)(lambda: _run_tasks(comm, "middle", cin, cout, *csem))
        body(*pre, *ins, *outs, *scr)
        pl.when(step == n_steps - 1)(lambda: _run_tasks(comm, "finish", cin, cout, *csem))

    grid_spec = pltpu.PrefetchScalarGridSpec(
        num_scalar_prefetch=n_pre, grid=grid, in_specs=list(in_specs) + [ANY] * len(c_in),
        out_specs=list(out_specs) + [ANY] * len(c_out), scratch_shapes=list(scratch_shapes) + sems)
    return _pcall(
        wrapped, name=name, grid_spec=grid_spec, out_shape=_in_hbm(list(out_shape) + c_out), input_output_aliases=aliases,
        compiler_params=_params(("arbitrary",) * len(grid) if comm else sem, vmem_mib,
                                BARRIER_OF[_peers_of(comm)] if comm else None),
    )(*prefetch, *operands, *_from_hbm(*c_in))


def _comm_call(name, comm):
    c_in, c_out, aliases, n_sem = _comm_layout(comm, 0, 0)

    def body(*refs):
        cin, cout, (ss, rs) = refs[:len(c_in)], refs[len(c_in):len(c_in) + len(c_out)], refs[len(c_in) + len(c_out):]
        _enter(comm)
        for phase in ("start", "middle", "finish"):
            _run_tasks(comm, phase, cin, cout, ss, rs)

    return _pcall(
        body, name=name, in_specs=[ANY] * len(c_in), out_specs=[ANY] * len(c_out), out_shape=_in_hbm(c_out),
        scratch_shapes=[pltpu.SemaphoreType.DMA((n_sem,)), pltpu.SemaphoreType.DMA((n_sem,))],
        input_output_aliases=aliases, compiler_params=_params(collective_id=BARRIER_OF[_peers_of(comm)]),
    )(*_from_hbm(*c_in))


def _inproj(x, g1, w_int, comm=()):
    tm = TM

    def body(x_ref, g_ref, w_ref, proj_ref, u_ref):
        xf = x_ref[...]
        r = lax.rsqrt(jnp.mean(xf * xf, axis=-1, keepdims=True) + EPS)
        u = (xf * r * g_ref[...]).astype(BF)
        u_ref[...] = u
        proj_ref[...] = _dot(u, w_ref[...], 1, 1)

    return _call(
        body, (x, g1, w_int), name="inproj", grid=(T // tm,),
        in_specs=[pl.BlockSpec((tm, D), lambda i: (i, 0)), pl.BlockSpec((1, D), lambda i: (0, 0)),
                  _resident((INW, D))],
        out_specs=[pl.BlockSpec((tm, INW), lambda i: (i, 0)), pl.BlockSpec((tm, D), lambda i: (i, 0))],
        out_shape=[SDS((T, INW), F32), SDS((T, D), BF)], sem=("parallel",), vmem_mib=40, comm=comm, free=(0, 1))


def _outproj(y, w_out, x, g2):
    tm = TM

    def body(y_ref, w_ref, x_ref, g_ref, h1_ref, u2_ref):
        h1 = x_ref[...] + _dot(y_ref[...], w_ref[...], 1, 0)
        h1_ref[...] = h1
        r = lax.rsqrt(jnp.mean(h1 * h1, axis=-1, keepdims=True) + EPS)
        u2_ref[...] = (h1 * r * g_ref[...]).astype(BF)

    return _call(
        body, (y, w_out, x, g2), name="outproj", grid=(T // tm,),
        in_specs=[pl.BlockSpec((tm, D), lambda i: (i, 0)), _resident((D, D)),
                  pl.BlockSpec((tm, D), lambda i: (i, 0)), pl.BlockSpec((1, D), lambda i: (0, 0))],
        out_specs=[pl.BlockSpec((tm, D), lambda i: (i, 0)), pl.BlockSpec((tm, D), lambda i: (i, 0))],
        out_shape=[SDS((T, D), F32), SDS((T, D), BF)], sem=("parallel",), vmem_mib=32, free=(2, 3))


def _ffn_up(u2, w_upt, comm=()):
    tm, tn = 1024, 512

    def body(u_ref, w_ref, o_ref):
        o_ref[...] = _dot(u_ref[...], w_ref[...], 1, 1).astype(BF)

    return _call(
        body, (u2, w_upt), name="ffn_up", grid=(T // tm, 2 * DFF // tn),
        in_specs=[pl.BlockSpec((tm, D), lambda i, j: (i, 0)), pl.BlockSpec((tn, D), lambda i, j: (j, 0))],
        out_specs=[pl.BlockSpec((tm, tn), lambda i, j: (i, j))], out_shape=[SDS((T, 2 * DFF), BF)],
        sem=("parallel", "parallel"), vmem_mib=32, comm=comm, free=(1,))


def _ffn_down(a, w_down, h1, tgt):
    tm = TM

    def body(a_ref, w_ref, h1_ref, t_ref, dh_ref, dhb_ref, l_ref):
        @pl.when(pl.program_id(0) == 0)
        def _():
            l_ref[...] = jnp.zeros_like(l_ref)

        h2 = h1_ref[...] + _dot(a_ref[...], w_ref[...], 1, 0)
        e = h2 - t_ref[...]
        dh = e * (1.0 / D)
        dh_ref[...] = dh
        dhb_ref[...] = dh.astype(BF)
        e2 = jnp.sum((e * e).reshape(tm // 8, 8, D), axis=0)
        acc = e2[:, 0:128]
        for k in range(1, D // 128):
            acc = acc + e2[:, k * 128:(k + 1) * 128]
        l_ref[...] += acc

    return _call(
        body, (a, w_down, h1, tgt), name="ffn_down", grid=(T // tm,),
        in_specs=[pl.BlockSpec((tm, DFF), lambda i: (i, 0)), _resident((DFF, D)),
                  pl.BlockSpec((tm, D), lambda i: (i, 0)), pl.BlockSpec((tm, D), lambda i: (i, 0))],
        out_specs=[pl.BlockSpec((tm, D), lambda i: (i, 0)), pl.BlockSpec((tm, D), lambda i: (i, 0)),
                   pl.BlockSpec((8, 128), lambda i: (0, 0))],
        out_shape=[SDS((T, D), F32), SDS((T, D), BF), SDS((8, 128), F32)], sem=("arbitrary",), vmem_mib=40, free=(2, 3))


def _bucket_table():
    q = np.arange(BLK, dtype=np.int32)[:, None]
    j = np.arange(2 * BLK, dtype=np.int32)[None, :]
    n = np.maximum(q + BLK - j, 0)
    nf = np.maximum(n, 1).astype(np.float32)
    max_exact = NBUCKET // 2
    large = max_exact + (np.log(nf / np.float32(max_exact)) / np.float32(math.log(BLK / max_exact))
                         * np.float32(NBUCKET - max_exact)).astype(np.int32)
    large = np.minimum(large, NBUCKET - 1)
    return np.where(n < max_exact, n, large).astype(np.int32)


def _band_bias_bwd(dbias, bucket, me):
    def body(me_ref, db_ref, bk_ref, o_ref):
        bk = bk_ref[...]
        for b in range(NBUCKET):
            m = bk == b
            for h in range(NH):
                v = jnp.where(m, db_ref[h * BLK:(h + 1) * BLK, :], 0.0)
                s = jnp.sum(jnp.sum(v, axis=1, keepdims=True), axis=0, keepdims=True)
                o_ref[0, h:h + 1, b:b + 1] = s

    grid_spec = pltpu.PrefetchScalarGridSpec(
        num_scalar_prefetch=1, grid=(1,),
        in_specs=[pl.BlockSpec((NH * BLK, 2 * BLK), lambda i, me_ref: (0, 0)),
                  pl.BlockSpec((BLK, 2 * BLK), lambda i, me_ref: (0, 0))],
        out_specs=pl.BlockSpec((1, NH, NBUCKET), lambda i, me_ref: (me_ref[0], 0, 0)),
    )
    return _pcall(body, name="band_bias_bwd", grid_spec=grid_spec, out_shape=SDS((N_DEV, NH, NBUCKET), F32),
                  compiler_params=_params(("arbitrary",)))(me, dbias, bucket)


def _two_bf16(x):
    hi = x.astype(BF)
    return hi, (x - hi.astype(F32)).astype(BF)


def _head_sums(x, seg):
    hi, lo = _two_bf16(x)
    s = seg[0:x.shape[1], :]
    return _dot(hi, s, 1, 0) + _dot(lo, s, 1, 0)


def _head_spread(v, seg, width):
    hi, lo = _two_bf16(v)
    s = seg[0:width, :]
    return _dot(hi, s, 1, 1) + _dot(lo, s, 1, 1)


def _head_norm(x, g_t, seg, by_head=False):
    if by_head:
        heads = [x[:, h * HD:(h + 1) * HD] for h in range(x.shape[1] // HD)]
        r = jnp.concatenate([jnp.broadcast_to(lax.rsqrt(jnp.mean(v * v, axis=-1, keepdims=True) + EPS), v.shape)
                             for v in heads], axis=1)
    else:
        r = lax.rsqrt(_head_sums(x * x, seg) * (1.0 / HD) + EPS)
        r = _head_spread(r, seg, x.shape[1])
    return x * r * g_t, r


def _head_norm_bwd(dy, x, r, g_t, seg):
    dg_t = jnp.sum(dy * (x * r), axis=0, keepdims=True)
    dgx = dy * g_t
    mean = _head_spread(_head_sums(x * dgx, seg) * (1.0 / HD), seg, x.shape[1])
    return r * dgx - x * (r * r * r) * mean, dg_t


def _fold_heads(v):
    out = v[:, 0:HD]
    for h in range(1, v.shape[1] // HD):
        out = out + v[:, h * HD:(h + 1) * HD]
    return out


def _mix_forward(P, zc8, zh8, pkv, first, cw, qg_t, kg_t, gco, gao, seg, sink_ref, bias_ref, by_head=False):
    gate_b = P[:, 0:CW]
    gate_c = P[:, CW:2 * CW]
    hc = P[:, 2 * CW:3 * CW]
    z = gate_c * hc
    keep = jnp.where(first, 0.0, 1.0)
    zp = zc8 * zh8 * keep
    p1 = zp[7:8, :]
    p2 = zp[6:7, :]
    row = lax.broadcasted_iota(jnp.int32, (BLK, 1), 0)
    z1 = jnp.where(row == 0, p1, pltpu.roll(z, 1, 0))
    z2 = jnp.where(row == 0, p2, jnp.where(row == 1, p1, pltpu.roll(z, 2, 0)))
    cz = cw[0:1, :] * z2 + cw[1:2, :] * z1 + cw[2:3, :] * z
    y_conv = gate_b * cz

    scale = HD ** -0.5
    qi = lax.broadcasted_iota(jnp.int32, (BLK, 2 * BLK), 0)
    kj = lax.broadcasted_iota(jnp.int32, (BLK, 2 * BLK), 1)
    dd = qi + BLK - kj
    first_key = jnp.where(first, BLK, 0)
    valid = (dd >= 0) & (dd < BLK) & (kj >= first_key)

    q0 = 3 * CW
    k0 = q0 + AW
    v0 = k0 + NKV * HD
    q_raw = P[:, q0:k0]
    qn, rq = _head_norm(q_raw, qg_t, seg, by_head)
    qs = (qn * scale).astype(BF)
    k_raw = jnp.concatenate([pkv[:, 0:NKV * HD], P[:, k0:v0]], axis=0)
    kn, rk = _head_norm(k_raw, kg_t, seg, by_head)
    knb = kn.astype(BF)
    heads = []
    for h in range(NH):
        kv = h // GQ
        kb = knb[:, kv * HD:(kv + 1) * HD]
        vb = jnp.concatenate([pkv[:, NKV * HD + kv * HD:NKV * HD + (kv + 1) * HD],
                              P[:, v0 + kv * HD:v0 + (kv + 1) * HD]], axis=0).astype(BF)
        Q = qs[:, h * HD:(h + 1) * HD]
        S = _dot(Q, kb, 1, 1) + bias_ref[h * BLK:(h + 1) * BLK, :]
        S = jnp.where(valid, S, NEG_INF)
        sink = sink_ref[0, h]
        m = jnp.maximum(jnp.max(S, axis=-1, keepdims=True), sink)
        p = jnp.exp(S - m)
        es = jnp.exp(sink - m)
        denom = jnp.sum(p, axis=-1, keepdims=True) + es
        probs = p / denom
        O = _dot(probs.astype(BF), vb, 1, 0)
        heads.append(dict(kb=kb, vb=vb, Q=Q, probs=probs, psink=es / denom, O=O))
    y_attn = jnp.concatenate([hd["O"] for hd in heads], axis=1)

    rc = lax.rsqrt(jnp.mean(y_conv * y_conv, axis=-1, keepdims=True) + EPS)
    ra = lax.rsqrt(jnp.mean(y_attn * y_attn, axis=-1, keepdims=True) + EPS)
    y = jnp.concatenate([y_conv * rc * gco, y_attn * ra * gao], axis=1)
    return dict(gate_b=gate_b, gate_c=gate_c, hc=hc, z=z, z1=z1, z2=z2, cz=cz, y_conv=y_conv, y_attn=y_attn,
                rc=rc, ra=ra, heads=heads, y=y, row=row, scale=scale, q_raw=q_raw, rq=rq, k_raw=k_raw, rk=rk)


BPS = 2
TILE = BPS * BLK
KV0 = 3 * CW + AW


def _mix_in_specs(tile_of):
    return [
        pl.BlockSpec(memory_space=pltpu.SMEM),
        pl.BlockSpec((TILE, INW), lambda s: (tile_of(s), 0)),
        pl.BlockSpec((8, CW), lambda s: (jnp.maximum(tile_of(s) * (TILE // 8) - 1, 0), 1)),
        pl.BlockSpec((8, CW), lambda s: (jnp.maximum(tile_of(s) * (TILE // 8) - 1, 0), 2)),
        pl.BlockSpec((BLK, 2 * NKV * HD), lambda s: (jnp.maximum(tile_of(s) * BPS - 1, 0), KV0 // (2 * NKV * HD))),
    ]


def _block_inputs(tile, b, zc_ref, zh_ref, pkv_ref, first_tile):
    P = tile[b * BLK:(b + 1) * BLK, :]
    if b == 0:
        return P, zc_ref[...], zh_ref[...], pkv_ref[...], first_tile
    lo = b * BLK
    return P, tile[lo - 8:lo, CW:2 * CW], tile[lo - 8:lo, 2 * CW:3 * CW], tile[lo - BLK:lo, KV0:KV0 + 2 * NKV * HD], False


def _mix_param_specs():
    return [
        pl.BlockSpec((8, CW), lambda s: (0, 0)),
        pl.BlockSpec((1, AW), lambda s: (0, 0)),
        pl.BlockSpec((1, NKV * HD), lambda s: (0, 0)),
        pl.BlockSpec((1, CW), lambda s: (0, 0)),
        pl.BlockSpec((1, AW), lambda s: (0, 0)),
        pl.BlockSpec((AW, 128), lambda s: (0, 0)),
        pl.BlockSpec((NH * BLK, 2 * BLK), lambda s: (0, 0)),
    ]


def _mix_params(cw8, qg, kg, gco, gao, bias):
    seg = np.zeros((AW, 128), np.float32)
    seg[np.arange(AW), np.arange(AW) // HD] = 1.0
    return (cw8, jnp.tile(qg, (1, NH)), jnp.tile(kg, (1, NKV)), gco, gao, jnp.asarray(seg, BF), bias)


def _mix_fwd(proj, sinks, cw8, qg, kg, gco, gao, bias, comm=()):
    def body(sink_ref, p_ref, zc_ref, zh_ref, pkv_ref, cw_ref, qg_ref, kg_ref, gco_ref, gao_ref, seg_ref, bias_ref, y_ref):
        tile = p_ref[...]
        for b in range(BPS):
            f = _mix_forward(*_block_inputs(tile, b, zc_ref, zh_ref, pkv_ref, pl.program_id(0) == 0), cw_ref[...],
                             qg_ref[...], kg_ref[...], gco_ref[...], gao_ref[...], seg_ref[...], sink_ref, bias_ref, by_head=True)
            y_ref[b * BLK:(b + 1) * BLK, :] = f["y"].astype(BF)

    return _call(
        body, (sinks, proj, proj, proj, proj, *_mix_params(cw8, qg, kg, gco, gao, bias)), name="mix_fwd", grid=(T // TILE,),
        in_specs=_mix_in_specs(lambda s: s) + _mix_param_specs(),
        out_specs=[pl.BlockSpec((TILE, D), lambda s: (s, 0))], out_shape=[SDS((T, D), BF)],
        sem=("parallel",), vmem_mib=40, comm=comm, free=tuple(range(5, 12)))


def _mix_bwd(proj, dy, sinks, cw8, qg, kg, gco, gao, bias, comm=()):
    n_steps = T // TILE

    def tile_of(s):
        return n_steps - 1 - s

    def body(sink_ref, p_ref, zc_ref, zh_ref, pkv_ref, dy_ref, cw_ref, qg_ref, kg_ref, gco_ref, gao_ref, seg_ref, bias_ref,
             dproj_ref, dcw_ref, dqg_ref, dkg_ref, dgco_ref, dgao_ref, dsink_ref, dbias_ref,
             ndcz_ref, dkc_ref, dvc_ref):
        s = pl.program_id(0)

        @pl.when(s == 0)
        def _():
            for r in (dcw_ref, dqg_ref, dkg_ref, dgco_ref, dgao_ref, dsink_ref, dbias_ref, ndcz_ref, dkc_ref, dvc_ref):
                r[...] = jnp.zeros_like(r)

        params = (cw_ref[...], qg_ref[...], kg_ref[...], gco_ref[...], gao_ref[...], seg_ref[...])
        tile = p_ref[...]
        carry = (ndcz_ref[...], dkc_ref[...], dvc_ref[...])
        total = None
        for b in reversed(range(BPS)):
            f = _mix_forward(*_block_inputs(tile, b, zc_ref, zh_ref, pkv_ref, s == n_steps - 1), *params, sink_ref, bias_ref)
            pieces, sums, carry = one_block(f, dy_ref[b * BLK:(b + 1) * BLK, :], params, carry)
            for lo, piece in pieces:
                dproj_ref[b * BLK:(b + 1) * BLK, lo:lo + piece.shape[1]] = piece
            total = sums if total is None else [t + v for t, v in zip(total, sums)]
        ndcz_ref[...], dkc_ref[...], dvc_ref[...] = carry
        dcw, dqg_t, dkg_t, dgco, dgao, dsink, *ds = total
        dcw_ref[0:3, :] += dcw
        dqg_ref[...] += _fold_heads(dqg_t)
        dkg_ref[...] += _fold_heads(dkg_t)
        dgco_ref[...] += dgco
        dgao_ref[...] += dgao
        dsink_ref[...] += dsink
        for h in range(NH):
            dbias_ref[h * BLK:(h + 1) * BLK, :] += ds[h]

    def one_block(f, dy, params, carry):
        cw, qg_v, kg_v, gco_v, gao_v, seg = params
        nxt, dk_carry, dv_carry = carry
        dyc, dgco = _rms_bwd(dy[:, 0:CW], f["y_conv"], f["rc"], gco_v)
        dya, dgao = _rms_bwd(dy[:, CW:CW + AW], f["y_attn"], f["ra"], gao_v)

        row = f["row"]
        dgate_b = dyc * f["cz"]
        dcz = dyc * f["gate_b"]
        dcw = jnp.concatenate([jnp.sum(dcz * f[k], axis=0, keepdims=True) for k in ("z2", "z1", "z")], axis=0)
        n0 = nxt[0:1, :]
        n1 = nxt[1:2, :]
        d1 = jnp.where(row == BLK - 1, n0, pltpu.roll(dcz, BLK - 1, 0))
        d2 = jnp.where(row == BLK - 1, n1, jnp.where(row == BLK - 2, n0, pltpu.roll(dcz, BLK - 2, 0)))
        dz = cw[2:3, :] * dcz + cw[1:2, :] * d1 + cw[0:1, :] * d2
        pieces = [(0, dgate_b.astype(BF)), (CW, (dz * f["hc"]).astype(BF)), (2 * CW, (dz * f["gate_c"]).astype(BF))]

        scale = f["scale"]
        lane = lax.broadcasted_iota(jnp.int32, (1, 128), 1)
        dsink = jnp.zeros((1, 128), F32)
        dq_cols, dk_cols, dv_cols, dk_prev, dv_prev, ds = [], [], [], [], [], []
        for kv in range(NKV):
            dKb = dVb = 0.0
            for h in range(kv * GQ, (kv + 1) * GQ):
                hd = f["heads"][h]
                dO = dya[:, h * HD:(h + 1) * HD]
                delta = jnp.sum(dO * hd["O"], axis=-1, keepdims=True)
                dOb = dO.astype(BF)
                dP = _dot(dOb, hd["vb"], 1, 1)
                dS = hd["probs"] * (dP - delta)
                tot = jnp.sum(hd["psink"] * delta, axis=0, keepdims=True)
                dsink = dsink - jnp.where(lane == h, tot, 0.0)
                ds.append(dS)
                dSb = dS.astype(BF)
                dq_cols.append(_dot(dSb, hd["kb"], 1, 0))
                dKb = dKb + _dot(dSb, hd["Q"], 0, 0)
                dVb = dVb + _dot(hd["probs"].astype(BF), dOb, 0, 0)
            dk_cols.append(dKb[BLK:, :] + dk_carry[:, kv * HD:(kv + 1) * HD])
            dv_cols.append(dVb[BLK:, :] + dv_carry[:, kv * HD:(kv + 1) * HD])
            dk_prev.append(dKb[:BLK, :])
            dv_prev.append(dVb[:BLK, :])
        dq_raw, dqg_t = _head_norm_bwd(jnp.concatenate(dq_cols, axis=1) * scale, f["q_raw"], f["rq"], qg_v, seg)
        dk_raw, dkg_t = _head_norm_bwd(jnp.concatenate(dk_cols, axis=1), f["k_raw"][BLK:, :], f["rk"][BLK:, :], kg_v, seg)
        pieces.append((3 * CW, jnp.concatenate([dq_raw, dk_raw] + dv_cols, axis=1).astype(BF)))
        owed = (dcz[0:8, :], jnp.concatenate(dk_prev, axis=1), jnp.concatenate(dv_prev, axis=1))
        return pieces, [dcw, dqg_t, dkg_t, dgco, dgao, dsink, *ds], owed

    small = lambda r, c: pl.BlockSpec((r, c), lambda s: (0, 0))
    return _call(
        body, (sinks, proj, proj, proj, proj, dy, *_mix_params(cw8, qg, kg, gco, gao, bias)), name="mix_bwd", grid=(n_steps,),
        in_specs=_mix_in_specs(tile_of) + [pl.BlockSpec((TILE, D), lambda s: (tile_of(s), 0))] + _mix_param_specs(),
        out_specs=[pl.BlockSpec((TILE, INW), lambda s: (tile_of(s), 0)), small(8, CW), small(1, HD), small(1, HD),
                   small(1, CW), small(1, AW), small(1, 128), small(NH * BLK, 2 * BLK)],
        out_shape=[SDS((T, INW), BF), SDS((8, CW), F32), SDS((1, HD), F32), SDS((1, HD), F32), SDS((1, CW), F32),
                   SDS((1, AW), F32), SDS((1, 128), F32), SDS((NH * BLK, 2 * BLK), F32)],
        scratch_shapes=[pltpu.VMEM((8, CW), F32), pltpu.VMEM((BLK, NKV * HD), F32), pltpu.VMEM((BLK, NKV * HD), F32)],
        sem=("arbitrary",), vmem_mib=56, comm=comm, free=(1, 2, 3, 4) + tuple(range(6, 13)))


FT = 256
NFT = DFF // FT
RC = 1024
NCH = T // RC
LEAD = 16


def _rows8(x):
    return jnp.sum(x.reshape(x.shape[0] // 8, 8, x.shape[1]), axis=0)


def _ffn_act_specs():
    return [
        pl.BlockSpec((T, FT), lambda j: (0, j)), pl.BlockSpec((T, FT), lambda j: (0, NFT + j)),
        pl.BlockSpec((8, FT), lambda j: (0, j)), pl.BlockSpec((8, FT), lambda j: (0, NFT + j)),
        pl.BlockSpec((1, FT), lambda j: (0, j)), pl.BlockSpec((1, FT), lambda j: (0, NFT + j)),
    ]


def _conv_rows(win, w, b, n):
    win = win.astype(F32)
    u = win[LEAD:LEAD + n]
    u1 = pltpu.roll(win, 1, 0)[LEAD:LEAD + n]
    u2 = pltpu.roll(win, 2, 0)[LEAD:LEAD + n]
    return u2, u1, u, w[0:1, :] * u2 + w[1:2, :] * u1 + w[2:3, :] * u + b


def _ffn_act(up, fw8, fb, comm=()):
    def body(ug_ref, uv_ref, wg_ref, wv_ref, bg_ref, bv_ref, a_ref):
        wg, wv, bg, bv = wg_ref[...], wv_ref[...], bg_ref[...], bv_ref[...]

        def chunk(win_g, win_v):
            gp = _conv_rows(win_g, wg, bg, RC)[3]
            vp = _conv_rows(win_v, wv, bv, RC)[3]
            return (gp * jax.nn.sigmoid(gp) * vp).astype(BF)

        zero = jnp.zeros((LEAD, FT), BF)
        a_ref[0:RC, :] = chunk(jnp.concatenate([zero, ug_ref[0:RC, :]], axis=0),
                               jnp.concatenate([zero, uv_ref[0:RC, :]], axis=0))

        def step(i, carry):
            r0 = pl.multiple_of(i * RC, RC)
            win = pl.ds(r0 - LEAD, RC + LEAD)
            a_ref[pl.ds(r0, RC), :] = chunk(ug_ref[win, :], uv_ref[win, :])
            return carry

        lax.fori_loop(1, NCH, step, 0)

    return _call(
        body, (up, up, fw8, fw8, fb, fb), name="ffn_act", grid=(NFT,), in_specs=_ffn_act_specs(),
        out_specs=[pl.BlockSpec((T, FT), lambda j: (0, j))], out_shape=[SDS((T, DFF), BF)],
        sem=("parallel",), vmem_mib=40, comm=comm, free=(2, 3, 4, 5))


def _ffn_act_bwd(up, da, fw8, fb, comm=()):
    ext = RC + LEAD

    def body(ug_ref, uv_ref, wg_ref, wv_ref, bg_ref, bv_ref, da_ref,
             dug_ref, duv_ref, dwg_ref, dwv_ref, dbg_ref, dbv_ref):
        wg, wv, bg, bv = wg_ref[...], wv_ref[...], bg_ref[...], bv_ref[...]

        def chunk(win_g, win_v, da_e):
            g2, g1, g0, gp = _conv_rows(win_g, wg, bg, ext)
            v2, v1, v0, vp = _conv_rows(win_v, wv, bv, ext)
            da_e = da_e.astype(F32)
            sig = jax.nn.sigmoid(gp)
            dvp = da_e * (gp * sig)
            dgp = da_e * vp * (sig * (1.0 + gp * (1.0 - sig)))

            def back(dp, w):
                return (w[2:3, :] * dp[0:RC] + w[1:2, :] * pltpu.roll(dp, ext - 1, 0)[0:RC]
                        + w[0:1, :] * pltpu.roll(dp, ext - 2, 0)[0:RC]).astype(BF)

            def sums(dp, u2, u1, u0):
                d = dp[0:RC]
                return [_rows8(d), _rows8(d * u2[0:RC]), _rows8(d * u1[0:RC]), _rows8(d * u0[0:RC])]

            return back(dgp, wg), back(dvp, wv), sums(dgp, g2, g1, g0) + sums(dvp, v2, v1, v0)

        zero = jnp.zeros((LEAD, FT), BF)
        dug, duv, acc = chunk(jnp.concatenate([zero, ug_ref[0:ext, :]], axis=0),
                              jnp.concatenate([zero, uv_ref[0:ext, :]], axis=0), da_ref[0:ext, :])
        dug_ref[0:RC, :] = dug
        duv_ref[0:RC, :] = duv

        def step(i, acc):
            r0 = pl.multiple_of(i * RC, RC)
            win = pl.ds(r0 - LEAD, ext + LEAD)
            dug, duv, part = chunk(ug_ref[win, :], uv_ref[win, :], da_ref[pl.ds(r0, ext), :])
            dug_ref[pl.ds(r0, RC), :] = dug
            duv_ref[pl.ds(r0, RC), :] = duv
            return [a + p for a, p in zip(acc, part)]

        acc = lax.fori_loop(1, NCH - 1, step, acc)
        r0 = T - RC
        tail = lambda ref, lo: jnp.concatenate([ref[lo:T, :], zero], axis=0)
        dug, duv, part = chunk(tail(ug_ref, r0 - LEAD), tail(uv_ref, r0 - LEAD), tail(da_ref, r0))
        dug_ref[r0:T, :] = dug
        duv_ref[r0:T, :] = duv
        tot = [jnp.sum(a + p, axis=0, keepdims=True) for a, p in zip(acc, part)]
        for k, (dw_ref, db_ref) in enumerate(((dwg_ref, dbg_ref), (dwv_ref, dbv_ref))):
            db_ref[...] = tot[4 * k]
            dw_ref[...] = jnp.zeros_like(dw_ref)
            for r in range(3):
                dw_ref[r:r + 1, :] = tot[4 * k + 1 + r]

    col = lambda r: pl.BlockSpec((r, FT), lambda j: (0, j))
    return _call(
        body, (up, up, fw8, fw8, fb, fb, da), name="ffn_act_bwd", grid=(NFT,),
        in_specs=_ffn_act_specs() + [pl.BlockSpec((T, FT), lambda j: (0, j))],
        out_specs=[col(T), col(T), col(8), col(8), col(1), col(1)],
        out_shape=[SDS((T, DFF), BF), SDS((T, DFF), BF), SDS((8, DFF), F32), SDS((8, DFF), F32),
                   SDS((1, DFF), F32), SDS((1, DFF), F32)],
        sem=("parallel",), vmem_mib=40, comm=comm, free=(0, 1, 2, 3, 4, 5))


def _ffn_down_bwd(dh2b, w_down, comm=()):
    tm = TM

    def body(d_ref, w_ref, o_ref):
        o_ref[...] = _dot(d_ref[...], w_ref[...], 1, 1).astype(BF)

    return _call(
        body, (dh2b, w_down), name="ffn_down_bwd", grid=(T // tm,),
        in_specs=[pl.BlockSpec((tm, D), lambda i: (i, 0)), _resident((DFF, D))],
        out_specs=[pl.BlockSpec((tm, DFF), lambda i: (i, 0))], out_shape=[SDS((T, DFF), BF)],
        sem=("parallel",), vmem_mib=40, comm=comm, free=(0, 1))


def _norm_matmul_bwd(name, a_list, w_t, k_offsets, xin, g, dres, want_bf16, comm=(), slot=None):
    tm = TM
    ks = [a.shape[1] for a in a_list]
    n_a = len(a_list)
    n_pre = 0 if slot is None else 1

    def body(*refs):
        refs = refs[n_pre:]
        a_refs = refs[:n_a]
        w_ref, x_ref, g_ref, r_ref = refs[n_a:n_a + 4]
        outs = refs[n_a + 4:]
        dx_ref, dg_ref = outs[0], (outs[-1] if slot is None else outs[-1].at[0])

        @pl.when(pl.program_id(0) == 0)
        def _():
            dg_ref[...] = jnp.zeros_like(dg_ref)

        du = _dot(a_refs[0][...], w_ref[k_offsets[0]:k_offsets[0] + ks[0], :], 1, 0)
        for k in range(1, n_a):
            du = du + _dot(a_refs[k][...], w_ref[k_offsets[k]:k_offsets[k] + ks[k], :], 1, 0)
        x = x_ref[...]
        r = lax.rsqrt(jnp.mean(x * x, axis=-1, keepdims=True) + EPS)
        dx, dg = _rms_bwd(du, x, r, g_ref[...])
        dx = r_ref[...] + dx
        dx_ref[...] = dx
        if want_bf16:
            outs[1][...] = dx.astype(BF)
        dg_ref[...] += dg

    tile = lambda c: pl.BlockSpec((tm, c), lambda i, *_: (i, 0))
    if slot is None:
        dg_spec, dg_shape = pl.BlockSpec((1, D), lambda i: (0, 0)), SDS((1, D), F32)
    else:
        dg_spec, dg_shape = pl.BlockSpec((1, 1, D), lambda i, slot_ref: (slot_ref[0], 0, 0)), SDS((N_DEV, 1, D), F32)
    out_specs = [tile(D)] + ([tile(D)] if want_bf16 else []) + [dg_spec]
    out_shape = [SDS((T, D), F32)] + ([SDS((T, D), BF)] if want_bf16 else []) + [dg_shape]
    return _call(
        body, (*a_list, w_t, xin, g, dres), name=name, grid=(T // tm,), prefetch=() if slot is None else (slot,),
        in_specs=[tile(k) for k in ks] + [_resident(w_t.shape), tile(D),
                                           pl.BlockSpec((1, D), lambda i, *_: (0, 0)), tile(D)],
        out_specs=out_specs, out_shape=out_shape, sem=("arbitrary",), vmem_mib=56, comm=comm, free=tuple(range(n_a + 4)))


def _out_bwd(dh1b, w_out, comm=()):
    tm = TM

    def body(d_ref, w_ref, o_ref):
        o_ref[...] = _dot(d_ref[...], w_ref[...], 1, 1)

    return _call(
        body, (dh1b, w_out), name="out_bwd", grid=(T // tm,),
        in_specs=[pl.BlockSpec((tm, D), lambda i: (i, 0)), _resident((D, D))],
        out_specs=[pl.BlockSpec((tm, D), lambda i: (i, 0))], out_shape=[SDS((T, D), F32)],
        sem=("parallel",), vmem_mib=32, comm=comm, free=(0, 1))


def _wgrad(name, a_list, b, old_a, comm=()):
    m_k = a_list[0].shape[1]
    tm = max(t for t in range(128, m_k // 2 + 1, 128) if m_k % t == 0)
    steps = [a.shape[1] // tm for a in a_list]
    starts = [sum(steps[:k]) for k in range(len(a_list))]
    n_a = len(a_list)

    def body(*refs):
        a_refs, b_ref, o_ref = refs[:n_a], refs[n_a], refs[n_a + 1]
        i = pl.program_id(0)
        for k in range(n_a):
            @pl.when((i >= starts[k]) & (i < starts[k] + steps[k]))
            def _(k=k):
                o_ref[...] = _dot(a_refs[k][...], b_ref[...], 0, 0).astype(BF)

    def a_spec(k):
        return pl.BlockSpec((T, tm), lambda i: (0, jnp.clip(i - starts[k], 0, steps[k] - 1)))

    m_total = tm * sum(steps)
    return _call(
        body, (*a_list, b), name=name, grid=(sum(steps),),
        in_specs=[a_spec(k) for k in range(n_a)] + [_resident((T, D))],
        out_specs=[pl.BlockSpec((tm, D), lambda i: (i, 0))], out_shape=[SDS((m_total, D), BF)],
        sem=("parallel",), vmem_mib=40, comm=comm, free=() if old_a is None else tuple(range(n_a)) if old_a else (n_a,))


def _chip_sum(name, gbf, from_sib, core, chip):
    h = gbf.shape[1]
    th = h

    def body(core_ref, chip_ref, g_ref, s_ref, pbf_ref, own_ref):
        p = g_ref[0].astype(F32) + s_ref[0].astype(F32)
        pbf_ref[0] = p.astype(BF)

        @pl.when(pl.program_id(1) == chip_ref[0])
        def _():
            own_ref[...] = p

    grid_spec = pltpu.PrefetchScalarGridSpec(
        num_scalar_prefetch=2, grid=(h // th, N_CHIPS),
        in_specs=[pl.BlockSpec((1, th, D), lambda t, jj, core_ref, chip_ref: (2 * jj + core_ref[0], t, 0)),
                  pl.BlockSpec((1, th, D), lambda t, jj, core_ref, chip_ref: (jj, t, 0))],
        out_specs=[pl.BlockSpec((1, th, D), lambda t, jj, core_ref, chip_ref: (jj, t, 0)),
                   pl.BlockSpec((th, D), lambda t, jj, core_ref, chip_ref: (t, 0))],
    )
    return _pcall(
        body, name=name, grid_spec=grid_spec, out_shape=_in_hbm([SDS((N_CHIPS, h, D), BF), SDS((h, D), F32)]),
        compiler_params=_params(("arbitrary", "arbitrary"), 32),
    )(core, chip, *_from_hbm(gbf, from_sib))


def _final_sum(name, own, from_chips, core, comm=()):
    h = own.shape[0]

    def body(core_ref, o_ref, r_ref, f_ref):
        f_ref[0] = ((o_ref[...] + r_ref[0].astype(F32)) + r_ref[1].astype(F32)) + r_ref[2].astype(F32)

    return _call(
        body, (own, from_chips), name=name, grid=(1,), prefetch=(core,),
        in_specs=[pl.BlockSpec((h, D), lambda i, core_ref: (0, 0)), pl.BlockSpec((3, h, D), lambda i, core_ref: (0, 0, 0))],
        out_specs=[pl.BlockSpec((1, h, D), lambda i, core_ref: (core_ref[0], 0, 0))], out_shape=[SDS((2, h, D), F32)],
        sem=("arbitrary",), vmem_mib=40, comm=comm)


def _adam_math(w, g, m, v):
    nm = ADAM_B1 * m + (1.0 - ADAM_B1) * g
    nv = ADAM_B2 * v + (1.0 - ADAM_B2) * (g * g)
    m_hat = nm / (1.0 - ADAM_B1 ** ADAM_STEP)
    v_hat = nv / (1.0 - ADAM_B2 ** ADAM_STEP)
    return -ADAM_LR * (m_hat / (jnp.sqrt(v_hat) + ADAM_EPS) + ADAM_WD * w), nm, nv


def _adamw(name, w, g, m, v, tr, copy_g=False, stage=True, g_transposed=False):
    rows, cols = w.shape

    def body(w_ref, g_ref, m_ref, v_ref, *outs):
        d_ref, nm_ref, nv_ref = outs[-3:]
        for c in [pl.ds(c0, 128) for c0 in range(0, cols, 128)] if g_transposed else [slice(None)]:
            g_val = g_ref[c, :].T if g_transposed else g_ref[...]
            if copy_g:
                outs[0][:, c] = g_val
            d_ref[:, c], nm_ref[:, c], nv_ref[:, c] = _adam_math(w_ref[:, c], g_val, m_ref[:, c], v_ref[:, c])

    spec = pl.BlockSpec((tr, cols), lambda i: (i, 0))
    n_out = 4 if copy_g else 3
    g_spec = pl.BlockSpec((cols, tr), lambda i: (0, i)) if g_transposed else spec
    return _call(body, (w, g, m, v), name=name, grid=(rows // tr,), in_specs=[spec, g_spec, spec, spec], out_specs=[spec] * n_out,
                 out_shape=[SDS((rows, cols), F32)] * n_out, sem=("parallel",), vmem_mib=32,
                 free=(0, 2, 3) if stage else ())


C_SQ = 2 * DFF
P_W = C_SQ + 128
R_G2, R_GO, R_DCW, R_QK = 0, 1, 2, 5
C_GCO, C_GAO, C_DQG, C_DKG, C_SINK = 0, CW, 0, 128, 256


def _pack(name, me, ins, width, fill):
    def body(me_ref, *refs):
        o = refs[-1]
        o[...] = jnp.zeros_like(o)
        fill(o, *refs[:-1])

    return _call(body, ins, name=name, grid=(1,), prefetch=(me,),
                 in_specs=[pl.BlockSpec(a.shape, lambda i, me_ref: (0, 0)) for a in ins],
                 out_specs=[pl.BlockSpec((1, 8, width), lambda i, me_ref: (me_ref[0], 0, 0))],
                 out_shape=[SDS((N_DEV, 8, width), F32)], sem=("arbitrary",))[0]


def _pack_ffn(me, dfwg, dfwv, dfbg, dfbv, sq):
    def fill(o, dfwg_r, dfwv_r, dfbg_r, dfbv_r, sq_r):
        o[0, :, 0:DFF] = dfwg_r[...]
        o[0, :, DFF:2 * DFF] = dfwv_r[...]
        o[0, 3:4, 0:DFF] = dfbg_r[...]
        o[0, 3:4, DFF:2 * DFF] = dfbv_r[...]
        o[0, :, C_SQ:C_SQ + 128] = sq_r[...]

    return _pack("pack_ffn", me, (dfwg, dfwv, dfbg, dfbv, sq), P_W, fill)


def _pack_mix(me, dg2, dgco, dgao, dcw8, dqg, dkg, dsink):
    def fill(o, dg2_r, dgco_r, dgao_r, dcw_r, dqg_r, dkg_r, dsink_r):
        o[0, R_G2:R_G2 + 1, :] = dg2_r[...]
        o[0, R_GO:R_GO + 1, C_GCO:C_GCO + CW] = dgco_r[...]
        o[0, R_GO:R_GO + 1, C_GAO:C_GAO + AW] = dgao_r[...]
        o[0, R_DCW:R_DCW + 3, 0:CW] = dcw_r[0:3, :]
        o[0, R_QK:R_QK + 1, C_DQG:C_DQG + HD] = dqg_r[...]
        o[0, R_QK:R_QK + 1, C_DKG:C_DKG + HD] = dkg_r[...]
        o[0, R_QK:R_QK + 1, C_SINK:C_SINK + 128] = dsink_r[...]

    return _pack("pack_mix", me, (dg2, dgco, dgao, dcw8, dqg, dkg, dsink), D, fill)


N_SMALL = 11


def _small_adam(chip, p_all, pm_all, g1_all, tbl_all, ws, ms, vs):
    fw_cols = 2 * DFF // N_CHIPS
    cw_cols = CW // N_CHIPS

    def body(chip_ref, p_ref, fw_ref, pm_ref, cw_ref, g1_ref, tbl_ref, *refs):
        w_r, m_r, v_r = refs[0:N_SMALL], refs[N_SMALL:2 * N_SMALL], refs[2 * N_SMALL:3 * N_SMALL]
        outs = refs[3 * N_SMALL:]
        g_o, d_o, nm_o, nv_o = (outs[k * N_SMALL:(k + 1) * N_SMALL] for k in range(4))
        loss_o = outs[4 * N_SMALL]

        def total(ref):
            s = ref[0]
            for k in range(1, N_DEV):
                s = s + ref[k]
            return s

        S = total(p_ref)
        fw = total(fw_ref)
        M = total(pm_ref)
        cw = total(cw_ref)

        def step(i, g, at):
            d, nm, nv = _adam_math(w_r[i][at], g, m_r[i][at], v_r[i][at])
            g_o[i][at], d_o[i][at], nm_o[i][at], nv_o[i][at] = g, d, nm, nv

        everything = (slice(None), slice(None))
        step(0, total(g1_ref), everything)
        for r in range(3):
            step(1, cw[R_DCW + r:R_DCW + r + 1, :], (r, slice(None), slice(None)))
        step(2, M[R_QK:R_QK + 1, C_DQG:C_DQG + HD], everything)
        step(3, M[R_QK:R_QK + 1, C_DKG:C_DKG + HD], everything)
        step(4, total(tbl_ref), everything)
        step(5, M[R_QK:R_QK + 1, C_SINK:C_SINK + NH], everything)
        step(6, M[R_GO:R_GO + 1, C_GCO:C_GCO + CW], everything)
        step(7, M[R_GO:R_GO + 1, C_GAO:C_GAO + AW], everything)
        step(8, M[R_G2:R_G2 + 1, :], everything)
        for r in range(3):
            step(9, fw[r:r + 1, :], (r, slice(None), slice(None)))
        step(10, S[3:4, 0:2 * DFF], everything)
        sq = S[:, C_SQ:C_SQ + 128]
        loss_o[...] = jnp.sum(jnp.sum(sq, axis=1, keepdims=True), axis=0, keepdims=True) * (0.5 / D)

    def full(a):
        n = len(a.shape)
        return pl.BlockSpec(a.shape, lambda i, chip_ref: (0,) * n)

    params = [*ws, *ms, *vs]
    out = _call(
        body, (p_all, p_all, pm_all, pm_all, g1_all, tbl_all, *params), name="small_adam", grid=(1,), prefetch=(chip,),
        in_specs=[full(p_all),
                  pl.BlockSpec((N_DEV, 8, fw_cols), lambda i, chip_ref: (0, 0, chip_ref[0])),
                  full(pm_all),
                  pl.BlockSpec((N_DEV, 8, cw_cols), lambda i, chip_ref: (0, 0, chip_ref[0])),
                  full(g1_all), full(tbl_all), *[full(a) for a in params]],
        out_specs=[full(a) for a in ws] * 4 + [pl.BlockSpec((1, 1), lambda i, chip_ref: (0, 0))],
        out_shape=[SDS(a.shape, F32) for a in ws] * 4 + [SDS((1, 1), F32)], sem=("arbitrary",), vmem_mib=32)
    return out[0:N_SMALL], out[N_SMALL:2 * N_SMALL], out[2 * N_SMALL:3 * N_SMALL], out[3 * N_SMALL:4 * N_SMALL], out[4 * N_SMALL]


PLACE_STEPS = 4


def _place_specs(shards):
    rows = [s.shape[0] // PLACE_STEPS for s in shards]
    return ([pl.BlockSpec((r, D), lambda i, chip_ref: (i, 0)) for r in rows],
            [pl.BlockSpec((r, D), lambda i, chip_ref: (chip_ref[0] * PLACE_STEPS + i, 0)) for r in rows],
            [SDS((N_CHIPS * s.shape[0], D), BF) for s in shards])


def _place_first(chip, shards, conv_w, ffn_conv_w):
    n = len(shards)

    def body(chip_ref, *refs):
        a, (s0, s1), o, (t0, t1) = refs[:n], refs[n:n + 2], refs[n + 2:2 * n + 2], refs[2 * n + 2:]
        for src, dst in zip(a, o):
            dst[...] = src[...].astype(BF)

        @pl.when(pl.program_id(0) == 0)
        def _():
            for s, t in ((s0, t0), (s1, t1)):
                t[...] = jnp.zeros_like(t)
                t[0, 0:3, :] = s[...]

    ins, outs, shapes = _place_specs(shards)
    taps = (conv_w, ffn_conv_w)
    return _call(
        body, (*shards, conv_w, ffn_conv_w), name="place_first", grid=(PLACE_STEPS,), prefetch=(chip,),
        in_specs=ins + [pl.BlockSpec(s.shape, lambda i, chip_ref: (0, 0)) for s in taps],
        out_specs=outs + [pl.BlockSpec((1, 8, s.shape[1]), lambda i, chip_ref: (chip_ref[0], 0, 0)) for s in taps],
        out_shape=shapes + [SDS((N_CHIPS, 8, s.shape[1]), F32) for s in taps],
        sem=("arbitrary",), vmem_mib=32, free=(n, n + 1))


def _place_rest(chip, shards, w_up, table, bucket, comm):
    n = len(shards)
    c_up = w_up.shape[1]
    edges = [round(k * (c_up // 128) / PLACE_STEPS) * 128 for k in range(PLACE_STEPS + 1)]

    def body(chip_ref, *refs):
        a, (up_ref, tab_ref, bk_ref), o = refs[:n], refs[n:n + 3], refs[n + 3:2 * n + 3]
        up_o, bias_ref = refs[2 * n + 3:]
        for src, dst in zip(a, o):
            dst[...] = src[...].astype(BF)
        for k in range(PLACE_STEPS):
            @pl.when(pl.program_id(0) == k)
            def _(k=k):
                up_o[edges[k]:edges[k + 1], :] = up_ref[:, edges[k]:edges[k + 1]].T.astype(BF)

        @pl.when(pl.program_id(0) == 0)
        def _():
            bk = bk_ref[...]
            eq = [bk == b for b in range(NBUCKET)]
            for h in range(NH):
                acc = jnp.zeros((BLK, 2 * BLK), F32)
                for b in range(NBUCKET):
                    acc = jnp.where(eq[b], tab_ref[h, b], acc)
                bias_ref[h * BLK:(h + 1) * BLK, :] = acc

    ins, outs, shapes = _place_specs(shards)
    return _call(
        body, (*shards, w_up, table, bucket), name="place_rest", grid=(PLACE_STEPS,), prefetch=(chip,),
        in_specs=ins + [_resident(w_up.shape), pl.BlockSpec(memory_space=pltpu.SMEM),
                        pl.BlockSpec(bucket.shape, lambda i, chip_ref: (0, 0))],
        out_specs=outs + [pl.BlockSpec((c_up, D), lambda i, chip_ref: (chip_ref[0], 0)),
                          pl.BlockSpec((NH * BLK, 2 * BLK), lambda i, chip_ref: (0, 0))],
        out_shape=shapes + [SDS((N_CHIPS * c_up, D), BF), SDS((NH * BLK, 2 * BLK), F32)],
        sem=("arbitrary",), vmem_mib=32, comm=comm, free=(n + 1, n + 2))


def kernel(x, norm_mix_g, w_in, conv_w, q_norm_g, k_norm_g, rel_bias_table, sinks, out_norm_conv_g, out_norm_attn_g, w_out, norm_ffn_g, w_up, ffn_conv_w, ffn_conv_b, w_down, loss_target, m_norm_mix_g, m_w_in, m_conv_w, m_q_norm_g, m_k_norm_g, m_rel_bias_table, m_sinks, m_out_norm_conv_g, m_out_norm_attn_g, m_w_out, m_norm_ffn_g, m_w_up, m_ffn_conv_w, m_ffn_conv_b, m_w_down, v_norm_mix_g, v_w_in, v_conv_w, v_q_norm_g, v_k_norm_g, v_rel_bias_table, v_sinks, v_out_norm_conv_g, v_out_norm_attn_g, v_w_out, v_norm_ffn_g, v_w_up, v_ffn_conv_w, v_ffn_conv_b, v_w_down):
    as_arg = lambda i: jnp.reshape(i, (1,)).astype(jnp.int32)
    chip = as_arg(2 * lax.axis_index("x") + lax.axis_index("y"))
    core = as_arg(lax.axis_index("c"))
    me = 2 * chip + core
    xs, tgt = x[0], loss_target[0]
    qg, kg, gco, gao, g1, g2, fb = q_norm_g, k_norm_g, out_norm_conv_g, out_norm_attn_g, norm_mix_g, norm_ffn_g, ffn_conv_b
    pieces = lambda g: g.reshape(N_DEV, g.shape[0] // N_DEV, D)
    whole = lambda f: f.reshape(2 * f.shape[1], D)

    bucket = jnp.asarray(_bucket_table())
    p_in, p_out, p_cw, p_fw = _place_first(chip, [w_in[0].T, w_out[0]], conv_w[0], ffn_conv_w[0])
    p_down, p_up, bias, w_int, w_out_f, cw_all, fw_all = _place_rest(
        chip, [w_down[0]], w_up[0], rel_bias_table.T, bucket,
        comm=[_t_gather(p_in, relayed_first=True), _t_gather(p_out), _t_small_weights(p_cw), _t_small_weights(p_fw)])
    cw8 = jnp.transpose(cw_all, (1, 0, 2)).reshape(8, CW)
    fw8 = jnp.transpose(fw_all, (1, 0, 2)).reshape(8, 2 * DFF)

    early = 3 / 11
    proj, u1, p_up = _inproj(xs, g1, w_int, comm=[_t_gather(p_up, (0, early))])
    y, w_upt = _mix_fwd(proj, sinks, cw8, qg, kg, gco, gao, bias, comm=[_t_gather(p_up, (early, 1))])
    h1, u2 = _outproj(y, w_out_f, xs, g2)
    up, w_down_f = _ffn_up(u2, w_upt, comm=[_t_gather(p_down)])
    a, = _ffn_act(up, fw8, fb)
    dh2, dh2b, sq = _ffn_down(a, w_down_f, h1, tgt)

    gdbf, = _wgrad("wgrad_down", [a], dh2b, None)
    da, sib_down = _ffn_down_bwd(dh2b, w_down_f, comm=[_t_sibling(pieces(gdbf))])
    pbf_down, own_down = _chip_sum("chip_sum_w_down", pieces(gdbf), sib_down, core, chip)
    dug, duv, dfwg, dfwv, dfbg, dfbv, chips_down = _ffn_act_bwd(up, da, fw8, fb, comm=[_t_chips(pbf_down)])
    fin_down, = _final_sum("final_sum_w_down", own_down, chips_down, core)
    gubf, = _wgrad("wgrad_up", [dug, duv], u2, False)
    p_all = _pack_ffn(me, dfwg, dfwv, dfbg, dfbv, sq)
    dh1, dh1b, dg2, sib_up, fin_down, p_all = _norm_matmul_bwd(
        "ffn_up_bwd", [dug, duv], w_upt, [0, DFF], h1, g2, dh2, True,
        comm=[_t_sibling(pieces(gubf)), _t_swap(fin_down), _t_allgather(p_all)])
    pbf_up, own_up = _chip_sum("chip_sum_w_up", pieces(gubf), sib_up, core, chip)
    gobf, = _wgrad("wgrad_out", [y], dh1b, True)
    dy, sib_out = _out_bwd(dh1b, w_out_f, comm=[_t_sibling(pieces(gobf))])
    pbf_out, own_out = _chip_sum("chip_sum_w_out", pieces(gobf), sib_out, core, chip)
    dproj, dcw8, dqg, dkg, dgco, dgao, dsink, dbias, chips_up = _mix_bwd(
        proj, dy, sinks, cw8, qg, kg, gco, gao, bias, comm=[_t_chips(pbf_up)])
    fin_up, = _final_sum("final_sum_w_up", own_up, chips_up, core)
    tbl_all = _band_bias_bwd(dbias, bucket, me)
    pm_all = _pack_mix(me, dg2, dgco, dgao, dcw8, dqg, dkg, dsink)
    gibf, chips_out, fin_up, pm_all, tbl_all = _wgrad(
        "wgrad_in", [dproj], u1, False,
        comm=[_t_chips(pbf_out), _t_swap(fin_up), _t_allgather(pm_all), _t_allgather(tbl_all)])
    fin_out, sib_in = _final_sum("final_sum_w_out", own_out, chips_out, core, comm=[_t_sibling(pieces(gibf))])
    pbf_in, own_in = _chip_sum("chip_sum_w_in", pieces(gibf), sib_in, core, chip)
    dx, g1_all, chips_in, fin_out = _norm_matmul_bwd(
        "in_bwd", [dproj], w_int, [0], xs, g1, dh1, False, comm=[_t_chips(pbf_in), _t_swap(fin_out)], slot=me)
    fin_in, = _final_sum("final_sum_w_in", own_in, chips_in, core)
    g1_all, fin_in = _comm_call("gather_last", [_t_allgather(g1_all), _t_swap(fin_in)])

    g_w_out, g_w_down = whole(fin_out), whole(fin_down)
    g_w_down, d_down, nm_down, nv_down = _adamw("adamw_w_down", w_down[0], g_w_down, m_w_down[0], v_w_down[0], 352, True)
    g_w_up, d_up, nm_up, nv_up = _adamw(
        "adamw_w_up", w_up[0], whole(fin_up), m_w_up[0], v_w_up[0], 256, True, stage=False, g_transposed=True)
    g_w_out, d_out, nm_out, nv_out = _adamw("adamw_w_out", w_out[0], g_w_out, m_w_out[0], v_w_out[0], 256, True, stage=False)
    g_w_in, d_in, nm_in, nv_in = [a.T for a in _adamw(
        "adamw_w_in", w_in[0].T, whole(fin_in), m_w_in[0].T, v_w_in[0].T, INW // N_CHIPS // 3, True, stage=False)]
    taps = lambda a: jnp.transpose(a, (1, 0, 2))
    sw = [norm_mix_g, taps(conv_w), q_norm_g, k_norm_g, rel_bias_table.T, sinks, out_norm_conv_g, out_norm_attn_g,
          norm_ffn_g, taps(ffn_conv_w), ffn_conv_b]
    smm = [m_norm_mix_g, taps(m_conv_w), m_q_norm_g, m_k_norm_g, m_rel_bias_table.T, m_sinks, m_out_norm_conv_g,
           m_out_norm_attn_g, m_norm_ffn_g, taps(m_ffn_conv_w), m_ffn_conv_b]
    smv = [v_norm_mix_g, taps(v_conv_w), v_q_norm_g, v_k_norm_g, v_rel_bias_table.T, v_sinks, v_out_norm_conv_g,
           v_out_norm_attn_g, v_norm_ffn_g, taps(v_ffn_conv_w), v_ffn_conv_b]
    *small_out, loss = _small_adam(chip, p_all, pm_all, g1_all, tbl_all, sw, smm, smv)
    sg, sd, snm, snv = [list(r) for r in small_out]
    for r in (sg, sd, snm, snv):
        r[1], r[4], r[9] = taps(r[1]), r[4].T, taps(r[9])

    def order(s, b_in, b_out, b_up, b_down):
        return (s[0], b_in[None], s[1], s[2], s[3], s[4], s[5], s[6], s[7], b_out[None], s[8], b_up[None],
                s[9], s[10], b_down[None])

    return (loss.reshape(()), dx[None],
            *order(sg, g_w_in, g_w_out, g_w_up, g_w_down),
            *order(sd, d_in, d_out, d_up, d_down),
            *order(snm, nm_in, nm_out, nm_up, nm_down),
            *order(snv, nv_in, nv_out, nv_up, nv_down))
```

```python
import functools
import math

import numpy as np

import jax
import jax.numpy as jnp
from jax import lax
from jax.experimental import pallas as pl
from jax.experimental.pallas import tpu as pltpu

F32 = jnp.float32
BF = jnp.bfloat16
SDS = jax.ShapeDtypeStruct

T = 2048
D = 1024
CW = 512
AW = 512
HD = 64
NH = 8
NKV = 2
GQ = 4
INW = 2304
DFF = 2816
BLK = 128
NB = T // BLK
NBUCKET = 32
EPS = 1e-6
NEG_INF = -1e30
N_CHIPS = 4
N_DEV = 8

ADAM_LR = 0.001
ADAM_B1 = 0.9
ADAM_B2 = 0.999
ADAM_EPS = 1e-08
ADAM_WD = 0.01
ADAM_STEP = 10

TM = 512
MIB = 1024 * 1024
MESH = pl.DeviceIdType.MESH
ANY = pl.BlockSpec(memory_space=pl.ANY)

_pcall = pl.pallas_call


def _params(sem=None, vmem_mib=None, collective_id=None):
    kw = {} if collective_id is None else {"collective_id": collective_id}
    if sem is not None:
        kw["dimension_semantics"] = sem
    if vmem_mib is not None:
        kw["vmem_limit_bytes"] = vmem_mib * MIB
    return pltpu.CompilerParams(**kw)


def _resident(shape):
    return pl.BlockSpec(shape, lambda *_: (0,) * len(shape), pipeline_mode=pl.Buffered(1))


def _dot(a, b, ca, cb):
    return lax.dot_general(a, b, (((ca,), (cb,)), ((), ())), preferred_element_type=F32)


def _rms_bwd(dy, x, r, g):
    dg = jnp.sum(dy * (x * r), axis=0, keepdims=True)
    dgx = dy * g
    dx = r * dgx - x * (r * r * r) * jnp.mean(x * dgx, axis=-1, keepdims=True)
    return dx, dg


def _where():
    x, y, c = lax.axis_index("x"), lax.axis_index("y"), lax.axis_index("c")
    return x, y, c, [(1 - x, y), (x, 1 - y), (1 - x, 1 - y)]


def _rcopy(src, dst, ssem, rsem, dev):
    return pltpu.make_async_remote_copy(src_ref=src, dst_ref=dst, send_sem=ssem, recv_sem=rsem, device_id=dev,
                                        device_id_type=MESH)


SIBLING, Y_CHIP, X_CHIP, DIAGONAL_CHIP = 1, 2, 4, 6
OTHER_CHIPS = (Y_CHIP, X_CHIP, DIAGONAL_CHIP)
EVERYONE = tuple(range(1, N_DEV))
BARRIER_OF = {(SIBLING,): 0, (SIBLING, Y_CHIP, X_CHIP): 1, OTHER_CHIPS: 2, (SIBLING,) + OTHER_CHIPS: 3, EVERYONE: 4}


def _peer(rel):
    x, y, c, _ = _where()
    return x ^ ((rel >> 2) & 1), y ^ ((rel >> 1) & 1), c ^ (rel & 1)


class _Task:
    def __init__(self, ins, outs, alias, n_sem, start, finish, middle=None, peers=()):
        self.ins, self.outs, self.alias, self.n_sem, self.start, self.finish = ins, outs, alias, n_sem, start, finish
        self.middle = middle if middle is not None else (lambda *args: None)
        self.peers = peers


def _peers_of(comm):
    return tuple(sorted({p for t in comm for p in t.peers}))


def _enter(comm):
    peers = _peers_of(comm)
    barrier = pltpu.get_barrier_semaphore()
    for rel in peers:
        pl.semaphore_signal(barrier, inc=1, device_id=_peer(rel), device_id_type=MESH)
    pl.semaphore_wait(barrier, len(peers))


ROWS16 = 16


def _t_gather(placed, part=(0, 1), relayed_first=False):
    R = placed.shape[0] // N_CHIPS
    q = R // 4
    lo, hi = (round(f * (q // ROWS16)) * ROWS16 for f in part)

    def quarter(chip_index, core, k):
        return pl.ds(pl.multiple_of(chip_index * R + core * 2 * q + k * q + lo, ROWS16), hi - lo)

    def places():
        x, y, c, _ = _where()
        return c, 2 * x + y, 2 * (1 - x) + y, 2 * x + (1 - y), 2 * (1 - x) + (1 - y), (1 - x, y, c), (x, 1 - y, c), (x, y, 1 - c)

    def copy(buf, k, chip_index, core, quart, ss, rs, b, dev):
        window = buf.at[quarter(chip_index, core, quart)]
        return _rcopy(window, window, ss.at[b + k], rs.at[b + k], dev)

    def first_hop(cout, ss, rs, b, which):
        c, me, _, _, _, x_nbr, y_nbr, _ = places()
        for k, (quart, dev) in enumerate(((0, x_nbr), (1, y_nbr), (1, x_nbr), (0, y_nbr))):
            if k in which:
                copy(cout[0], k, me, c, quart, ss, rs, b, dev).start()

    def start(cin, cout, ss, rs, b):
        first_hop(cout, ss, rs, b, (0, 1) if relayed_first else (0, 1, 2, 3))

    def middle(cin, cout, ss, rs, b):
        c, _, xc, yc, _, x_nbr, y_nbr, sib = places()
        for k, chip_index, quart, dev in ((0, xc, 0, y_nbr), (1, yc, 1, x_nbr)):
            copy(cout[0], k, chip_index, c, quart, ss, rs, b, dev).wait_recv()
            copy(cout[0], 4 + k, chip_index, c, quart, ss, rs, b, dev).start()
            copy(cout[0], 6 + k, chip_index, c, quart, ss, rs, b, sib).start()
        if relayed_first:
            first_hop(cout, ss, rs, b, (2, 3))

    later = ((2, 1, 1), (3, 2, 0), (4, 3, 0), (5, 3, 1))

    def finish(cin, cout, ss, rs, b):
        c, me, xc, yc, dc, _, _, sib = places()
        chip_of = {1: xc, 2: yc, 3: dc}
        for k, whose, quart in later:
            copy(cout[0], k, chip_of[whose], c, quart, ss, rs, b, sib).wait_recv()
            copy(cout[0], 6 + k, chip_of[whose], c, quart, ss, rs, b, sib).start()
        for k, whose, quart in ((0, 1, 0), (1, 2, 1)) + later:
            copy(cout[0], 6 + k, chip_of[whose], 1 - c, quart, ss, rs, b, sib).wait_recv()
        for k in range(12):
            copy(cout[0], k, me, c, 0, ss, rs, b, sib).wait_send()

    return _Task([placed], [SDS(placed.shape, placed.dtype)], [(0, 0)], 12, start, finish, middle, peers=(SIBLING, Y_CHIP, X_CHIP))


def _t_small_weights(buf):
    def start(cin, cout, ss, rs, b):
        x, y, c, chips = _where()
        mine = cout[0].at[2 * x + y]
        for r, (px, py) in enumerate(chips):
            _rcopy(mine, mine, ss.at[b + r], rs.at[b + r], (px, py, c)).start()

    def finish(cin, cout, ss, rs, b):
        x, y, c, chips = _where()
        for r, (px, py) in enumerate(chips):
            got = cout[0].at[2 * px + py]
            _rcopy(got, got, ss.at[b + r], rs.at[b + r], (px, py, c)).wait_recv()
        for r, (px, py) in enumerate(chips):
            mine = cout[0].at[2 * x + y]
            _rcopy(mine, mine, ss.at[b + r], rs.at[b + r], (px, py, c)).wait_send()

    return _Task([buf], [SDS(buf.shape, buf.dtype)], [(0, 0)], 3, start, finish, peers=OTHER_CHIPS)


def _t_sibling(gbf):
    def start(cin, cout, ss, rs, b):
        x, y, c, _ = _where()
        for jj in range(N_CHIPS):
            _rcopy(cin[0].at[2 * jj + (1 - c)], cout[0].at[jj], ss.at[b + jj], rs.at[b + jj], (x, y, 1 - c)).start()

    def finish(cin, cout, ss, rs, b):
        x, y, c, _ = _where()
        for jj in range(N_CHIPS):
            got = cout[0].at[jj]
            _rcopy(got, got, ss.at[b + jj], rs.at[b + jj], (x, y, 1 - c)).wait_recv()
        for jj in range(N_CHIPS):
            got = cout[0].at[jj]
            _rcopy(got, got, ss.at[b + jj], rs.at[b + jj], (x, y, 1 - c)).wait_send()

    return _Task([gbf], [SDS((N_CHIPS,) + gbf.shape[1:], BF)], [], N_CHIPS, start, finish, peers=(SIBLING,))


def _t_chips(pbf):
    def start(cin, cout, ss, rs, b):
        x, y, c, chips = _where()
        for r, (px, py) in enumerate(chips):
            _rcopy(cin[0].at[2 * px + py], cout[0].at[r], ss.at[b + r], rs.at[b + r], (px, py, c)).start()

    def finish(cin, cout, ss, rs, b):
        x, y, c, chips = _where()
        for r, (px, py) in enumerate(chips):
            got = cout[0].at[r]
            _rcopy(got, got, ss.at[b + r], rs.at[b + r], (px, py, c)).wait_recv()
        for r, (px, py) in enumerate(chips):
            got = cout[0].at[r]
            _rcopy(got, got, ss.at[b + r], rs.at[b + r], (px, py, c)).wait_send()

    return _Task([pbf], [SDS((3,) + pbf.shape[1:], BF)], [], 3, start, finish, peers=OTHER_CHIPS)


def _t_swap(fin):
    def start(cin, cout, ss, rs, b):
        x, y, c, _ = _where()
        mine = cout[0].at[c]
        _rcopy(mine, mine, ss.at[b], rs.at[b], (x, y, 1 - c)).start()

    def finish(cin, cout, ss, rs, b):
        x, y, c, _ = _where()
        got = cout[0].at[1 - c]
        _rcopy(got, got, ss.at[b], rs.at[b], (x, y, 1 - c)).wait_recv()
        _rcopy(got, got, ss.at[b], rs.at[b], (x, y, 1 - c)).wait_send()

    return _Task([fin], [SDS(fin.shape, fin.dtype)], [(0, 0)], 1, start, finish, peers=(SIBLING,))


def _t_allgather(buf):
    def peers():
        x, y, c, _ = _where()
        out = []
        for rel in range(1, N_DEV):
            px, py, pc = x ^ ((rel >> 2) & 1), y ^ ((rel >> 1) & 1), c ^ (rel & 1)
            out.append((rel - 1, 4 * px + 2 * py + pc, (px, py, pc)))
        return 4 * x + 2 * y + c, out

    def start(cin, cout, ss, rs, b):
        me, ps = peers()
        mine = cout[0].at[me]
        for k, _, dev in ps:
            _rcopy(mine, mine, ss.at[b + k], rs.at[b + k], dev).start()

    def finish(cin, cout, ss, rs, b):
        me, ps = peers()
        for k, pidx, dev in ps:
            got = cout[0].at[pidx]
            _rcopy(got, got, ss.at[b + k], rs.at[b + k], dev).wait_recv()
        for k, _, dev in ps:
            mine = cout[0].at[me]
            _rcopy(mine, mine, ss.at[b + k], rs.at[b + k], dev).wait_send()

    return _Task([buf], [SDS(buf.shape, buf.dtype)], [(0, 0)], N_DEV - 1, start, finish, peers=EVERYONE)


def _run_tasks(comm, which, cin, cout, ss, rs):
    i0 = o0 = s0 = 0
    for t in comm:
        getattr(t, which)(cin[i0:i0 + len(t.ins)], cout[o0:o0 + len(t.outs)], ss, rs, s0)
        i0, o0, s0 = i0 + len(t.ins), o0 + len(t.outs), s0 + t.n_sem


def _from_hbm(*arrays):
    return [pltpu.with_memory_space_constraint(a, pltpu.HBM) for a in arrays]


def _in_hbm(shapes):
    return [pltpu.HBM(s.shape, s.dtype) for s in shapes]


def _comm_layout(comm, n_in, n_out):
    c_in = [a for t in comm for a in t.ins]
    c_out = [s for t in comm for s in t.outs]
    aliases, i0, o0 = {}, 0, 0
    for t in comm:
        for i, o in t.alias:
            aliases[n_in + i0 + i] = n_out + o0 + o
        i0, o0 = i0 + len(t.ins), o0 + len(t.outs)
    return c_in, c_out, aliases, sum(t.n_sem for t in comm)


def _call(body, operands, *, name, grid, in_specs, out_specs, out_shape, scratch_shapes=(), sem=None, vmem_mib=None, comm=(),
          free=(), prefetch=()):
    operands = [o if s.memory_space == pltpu.SMEM or k in free else pltpu.with_memory_space_constraint(o, pltpu.HBM)
                for k, (o, s) in enumerate(zip(operands, in_specs))]
    n_pre, n_in, n_out, n_scr = len(prefetch), len(in_specs), len(out_specs), len(scratch_shapes)
    c_in, c_out, aliases, n_sem = _comm_layout(comm, n_pre + n_in, n_out)
    sems = [pltpu.SemaphoreType.DMA((n_sem,)), pltpu.SemaphoreType.DMA((n_sem,))] if comm else []

    def wrapped(*refs):
        pre, refs = refs[:n_pre], refs[n_pre:]
        ins, cin = refs[:n_in], refs[n_in:n_in + len(c_in)]
        rest = refs[n_in + len(c_in):]
        outs, cout = rest[:n_out], rest[n_out:n_out + len(c_out)]
        rest = rest[n_out + len(c_out):]
        scr, csem = rest[:n_scr], rest[n_scr:]
        if not comm:
            return body(*pre, *ins, *outs, *scr)
        step = functools.reduce(lambda acc, k: acc * grid[k] + pl.program_id(k), range(len(grid)), 0)
        n_steps = math.prod(grid)

        @pl.when(step == 0)
        def _():
            _enter(comm)
            _run_tasks(comm, "start", cin, cout, *csem)

        pl.when(step == n_steps // 2)(lambda: _run_tasks(comm, "middle", cin, cout, *csem))
        body(*pre, *ins, *outs, *scr)
        pl.when(step == n_steps - 1)(lambda: _run_tasks(comm, "finish", cin, cout, *csem))

    grid_spec = pltpu.PrefetchScalarGridSpec(
        num_scalar_prefetch=n_pre, grid=grid, in_specs=list(in_specs) + [ANY] * len(c_in),
        out_specs=list(out_specs) + [ANY] * len(c_out), scratch_shapes=list(scratch_shapes) + sems)
    return _pcall(
        wrapped, name=name, grid_spec=grid_spec, out_shape=_in_hbm(list(out_shape) + c_out), input_output_aliases=aliases,
        compiler_params=_params(("arbitrary",) * len(grid) if comm else sem, vmem_mib,
                                BARRIER_OF[_peers_of(comm)] if comm else None),
    )(*prefetch, *operands, *_from_hbm(*c_in))


def _comm_call(name, comm):
    c_in, c_out, aliases, n_sem = _comm_layout(comm, 0, 0)

    def body(*refs):
        cin, cout, (ss, rs) = refs[:len(c_in)], refs[len(c_in):len(c_in) + len(c_out)], refs[len(c_in) + len(c_out):]
        _enter(comm)
        for phase in ("start", "middle", "finish"):
            _run_tasks(comm, phase, cin, cout, ss, rs)

    return _pcall(
        body, name=name, in_specs=[ANY] * len(c_in), out_specs=[ANY] * len(c_out), out_shape=_in_hbm(c_out),
        scratch_shapes=[pltpu.SemaphoreType.DMA((n_sem,)), pltpu.SemaphoreType.DMA((n_sem,))],
        input_output_aliases=aliases, compiler_params=_params(collective_id=BARRIER_OF[_peers_of(comm)]),
    )(*_from_hbm(*c_in))


def _inproj(x, g1, w_int, comm=()):
    tm = TM

    def body(x_ref, g_ref, w_ref, proj_ref, u_ref):
        xf = x_ref[...]
        r = lax.rsqrt(jnp.mean(xf * xf, axis=-1, keepdims=True) + EPS)
        u = (xf * r * g_ref[...]).astype(BF)
        u_ref[...] = u
        proj_ref[...] = _dot(u, w_ref[...], 1, 1)

    return _call(
        body, (x, g1, w_int), name="inproj", grid=(T // tm,),
        in_specs=[pl.BlockSpec((tm, D), lambda i: (i, 0)), pl.BlockSpec((1, D), lambda i: (0, 0)),
                  _resident((INW, D))],
        out_specs=[pl.BlockSpec((tm, INW), lambda i: (i, 0)), pl.BlockSpec((tm, D), lambda i: (i, 0))],
        out_shape=[SDS((T, INW), F32), SDS((T, D), BF)], sem=("parallel",), vmem_mib=40, comm=comm, free=(0, 1))


def _outproj(y, w_out, x, g2):
    tm = TM

    def body(y_ref, w_ref, x_ref, g_ref, h1_ref, u2_ref):
        h1 = x_ref[...] + _dot(y_ref[...], w_ref[...], 1, 0)
        h1_ref[...] = h1
        r = lax.rsqrt(jnp.mean(h1 * h1, axis=-1, keepdims=True) + EPS)
        u2_ref[...] = (h1 * r * g_ref[...]).astype(BF)

    return _call(
        body, (y, w_out, x, g2), name="outproj", grid=(T // tm,),
        in_specs=[pl.BlockSpec((tm, D), lambda i: (i, 0)), _resident((D, D)),
                  pl.BlockSpec((tm, D), lambda i: (i, 0)), pl.BlockSpec((1, D), lambda i: (0, 0))],
        out_specs=[pl.BlockSpec((tm, D), lambda i: (i, 0)), pl.BlockSpec((tm, D), lambda i: (i, 0))],
        out_shape=[SDS((T, D), F32), SDS((T, D), BF)], sem=("parallel",), vmem_mib=32, free=(2, 3))


def _ffn_up(u2, w_upt, comm=()):
    tm, tn = T, 512

    def body(u_ref, w_ref, o_ref):
        o_ref[...] = _dot(u_ref[...], w_ref[...], 1, 1).astype(BF)

    return _call(
        body, (u2, w_upt), name="ffn_up", grid=(T // tm, 2 * DFF // tn),
        in_specs=[pl.BlockSpec((tm, D), lambda i, j: (i, 0)), pl.BlockSpec((tn, D), lambda i, j: (j, 0))],
        out_specs=[pl.BlockSpec((tm, tn), lambda i, j: (i, j))], out_shape=[SDS((T, 2 * DFF), BF)],
        sem=("parallel", "parallel"), vmem_mib=32, comm=comm, free=(1,))


def _ffn_down(a, w_down, h1, tgt):
    tm = TM

    def body(a_ref, w_ref, h1_ref, t_ref, dh_ref, dhb_ref, l_ref):
        @pl.when(pl.program_id(0) == 0)
        def _():
            l_ref[...] = jnp.zeros_like(l_ref)

        h2 = h1_ref[...] + _dot(a_ref[...], w_ref[...], 1, 0)
        e = h2 - t_ref[...]
        dh = e * (1.0 / D)
        dh_ref[...] = dh
        dhb_ref[...] = dh.astype(BF)
        e2 = jnp.sum((e * e).reshape(tm // 8, 8, D), axis=0)
        acc = e2[:, 0:128]
        for k in range(1, D // 128):
            acc = acc + e2[:, k * 128:(k + 1) * 128]
        l_ref[...] += acc

    return _call(
        body, (a, w_down, h1, tgt), name="ffn_down", grid=(T // tm,),
        in_specs=[pl.BlockSpec((tm, DFF), lambda i: (i, 0)), _resident((DFF, D)),
                  pl.BlockSpec((tm, D), lambda i: (i, 0)), pl.BlockSpec((tm, D), lambda i: (i, 0))],
        out_specs=[pl.BlockSpec((tm, D), lambda i: (i, 0)), pl.BlockSpec((tm, D), lambda i: (i, 0)),
                   pl.BlockSpec((8, 128), lambda i: (0, 0))],
        out_shape=[SDS((T, D), F32), SDS((T, D), BF), SDS((8, 128), F32)], sem=("arbitrary",), vmem_mib=40, free=(2, 3))


def _bucket_table():
    q = np.arange(BLK, dtype=np.int32)[:, None]
    j = np.arange(2 * BLK, dtype=np.int32)[None, :]
    n = np.maximum(q + BLK - j, 0)
    nf = np.maximum(n, 1).astype(np.float32)
    max_exact = NBUCKET // 2
    large = max_exact + (np.log(nf / np.float32(max_exact)) / np.float32(math.log(BLK / max_exact))
                         * np.float32(NBUCKET - max_exact)).astype(np.int32)
    large = np.minimum(large, NBUCKET - 1)
    return np.where(n < max_exact, n, large).astype(np.int32)


def _band_bias_bwd(dbias, bucket, me):
    def body(me_ref, db_ref, bk_ref, o_ref):
        bk = bk_ref[...]
        for b in range(NBUCKET):
            m = bk == b
            for h in range(NH):
                v = jnp.where(m, db_ref[h * BLK:(h + 1) * BLK, :], 0.0)
                s = jnp.sum(jnp.sum(v, axis=1, keepdims=True), axis=0, keepdims=True)
                o_ref[0, h:h + 1, b:b + 1] = s

    grid_spec = pltpu.PrefetchScalarGridSpec(
        num_scalar_prefetch=1, grid=(1,),
        in_specs=[pl.BlockSpec((NH * BLK, 2 * BLK), lambda i, me_ref: (0, 0)),
                  pl.BlockSpec((BLK, 2 * BLK), lambda i, me_ref: (0, 0))],
        out_specs=pl.BlockSpec((1, NH, NBUCKET), lambda i, me_ref: (me_ref[0], 0, 0)),
    )
    return _pcall(body, name="band_bias_bwd", grid_spec=grid_spec, out_shape=SDS((N_DEV, NH, NBUCKET), F32),
                  compiler_params=_params(("arbitrary",)))(me, dbias, bucket)


def _two_bf16(x):
    hi = x.astype(BF)
    return hi, (x - hi.astype(F32)).astype(BF)


def _head_sums(x, seg):
    hi, lo = _two_bf16(x)
    s = seg[0:x.shape[1], :]
    return _dot(hi, s, 1, 0) + _dot(lo, s, 1, 0)


def _head_spread(v, seg, width):
    hi, lo = _two_bf16(v)
    s = seg[0:width, :]
    return _dot(hi, s, 1, 1) + _dot(lo, s, 1, 1)


def _head_norm(x, g_t, seg, by_head=False):
    if by_head:
        heads = [x[:, h * HD:(h + 1) * HD] for h in range(x.shape[1] // HD)]
        r = jnp.concatenate([jnp.broadcast_to(lax.rsqrt(jnp.mean(v * v, axis=-1, keepdims=True) + EPS), v.shape)
                             for v in heads], axis=1)
    else:
        r = lax.rsqrt(_head_sums(x * x, seg) * (1.0 / HD) + EPS)
        r = _head_spread(r, seg, x.shape[1])
    return x * r * g_t, r


def _head_norm_bwd(dy, x, r, g_t, seg):
    dg_t = jnp.sum(dy * (x * r), axis=0, keepdims=True)
    dgx = dy * g_t
    mean = _head_spread(_head_sums(x * dgx, seg) * (1.0 / HD), seg, x.shape[1])
    return r * dgx - x * (r * r * r) * mean, dg_t


def _fold_heads(v):
    out = v[:, 0:HD]
    for h in range(1, v.shape[1] // HD):
        out = out + v[:, h * HD:(h + 1) * HD]
    return out


def _mix_forward(P, zc8, zh8, pkv, first, cw, qg_t, kg_t, gco, gao, seg, sink_ref, bias_ref, by_head=False):
    gate_b = P[:, 0:CW]
    gate_c = P[:, CW:2 * CW]
    hc = P[:, 2 * CW:3 * CW]
    z = gate_c * hc
    keep = jnp.where(first, 0.0, 1.0)
    zp = zc8 * zh8 * keep
    p1 = zp[7:8, :]
    p2 = zp[6:7, :]
    row = lax.broadcasted_iota(jnp.int32, (BLK, 1), 0)
    z1 = jnp.where(row == 0, p1, pltpu.roll(z, 1, 0))
    z2 = jnp.where(row == 0, p2, jnp.where(row == 1, p1, pltpu.roll(z, 2, 0)))
    cz = cw[0:1, :] * z2 + cw[1:2, :] * z1 + cw[2:3, :] * z
    y_conv = gate_b * cz

    scale = HD ** -0.5
    qi = lax.broadcasted_iota(jnp.int32, (BLK, 2 * BLK), 0)
    kj = lax.broadcasted_iota(jnp.int32, (BLK, 2 * BLK), 1)
    dd = qi + BLK - kj
    first_key = jnp.where(first, BLK, 0)
    valid = (dd >= 0) & (dd < BLK) & (kj >= first_key)

    q0 = 3 * CW
    k0 = q0 + AW
    v0 = k0 + NKV * HD
    q_raw = P[:, q0:k0]
    qn, rq = _head_norm(q_raw, qg_t, seg, by_head)
    qs = (qn * scale).astype(BF)
    k_raw = jnp.concatenate([pkv[:, 0:NKV * HD], P[:, k0:v0]], axis=0)
    kn, rk = _head_norm(k_raw, kg_t, seg, by_head)
    knb = kn.astype(BF)
    heads = []
    for h in range(NH):
        kv = h // GQ
        kb = knb[:, kv * HD:(kv + 1) * HD]
        vb = jnp.concatenate([pkv[:, NKV * HD + kv * HD:NKV * HD + (kv + 1) * HD],
                              P[:, v0 + kv * HD:v0 + (kv + 1) * HD]], axis=0).astype(BF)
        Q = qs[:, h * HD:(h + 1) * HD]
        S = _dot(Q, kb, 1, 1) + bias_ref[h * BLK:(h + 1) * BLK, :]
        S = jnp.where(valid, S, NEG_INF)
        sink = sink_ref[0, h]
        m = jnp.maximum(jnp.max(S, axis=-1, keepdims=True), sink)
        p = jnp.exp(S - m)
        es = jnp.exp(sink - m)
        denom = jnp.sum(p, axis=-1, keepdims=True) + es
        probs = p / denom
        O = _dot(probs.astype(BF), vb, 1, 0)
        heads.append(dict(kb=kb, vb=vb, Q=Q, probs=probs, psink=es / denom, O=O))
    y_attn = jnp.concatenate([hd["O"] for hd in heads], axis=1)

    rc = lax.rsqrt(jnp.mean(y_conv * y_conv, axis=-1, keepdims=True) + EPS)
    ra = lax.rsqrt(jnp.mean(y_attn * y_attn, axis=-1, keepdims=True) + EPS)
    y = jnp.concatenate([y_conv * rc * gco, y_attn * ra * gao], axis=1)
    return dict(gate_b=gate_b, gate_c=gate_c, hc=hc, z=z, z1=z1, z2=z2, cz=cz, y_conv=y_conv, y_attn=y_attn,
                rc=rc, ra=ra, heads=heads, y=y, row=row, scale=scale, q_raw=q_raw, rq=rq, k_raw=k_raw, rk=rk)


BPS = 2
TILE = BPS * BLK
KV0 = 3 * CW + AW


def _mix_in_specs(tile_of):
    return [
        pl.BlockSpec(memory_space=pltpu.SMEM),
        pl.BlockSpec((TILE, INW), lambda s: (tile_of(s), 0)),
        pl.BlockSpec((8, CW), lambda s: (jnp.maximum(tile_of(s) * (TILE // 8) - 1, 0), 1)),
        pl.BlockSpec((8, CW), lambda s: (jnp.maximum(tile_of(s) * (TILE // 8) - 1, 0), 2)),
        pl.BlockSpec((BLK, 2 * NKV * HD), lambda s: (jnp.maximum(tile_of(s) * BPS - 1, 0), KV0 // (2 * NKV * HD))),
    ]


def _block_inputs(tile, b, zc_ref, zh_ref, pkv_ref, first_tile):
    P = tile[b * BLK:(b + 1) * BLK, :]
    if b == 0:
        return P, zc_ref[...], zh_ref[...], pkv_ref[...], first_tile
    lo = b * BLK
    return P, tile[lo - 8:lo, CW:2 * CW], tile[lo - 8:lo, 2 * CW:3 * CW], tile[lo - BLK:lo, KV0:KV0 + 2 * NKV * HD], False


def _mix_param_specs():
    return [
        pl.BlockSpec((8, CW), lambda s: (0, 0)),
        pl.BlockSpec((1, AW), lambda s: (0, 0)),
        pl.BlockSpec((1, NKV * HD), lambda s: (0, 0)),
        pl.BlockSpec((1, CW), lambda s: (0, 0)),
        pl.BlockSpec((1, AW), lambda s: (0, 0)),
        pl.BlockSpec((AW, 128), lambda s: (0, 0)),
        pl.BlockSpec((NH * BLK, 2 * BLK), lambda s: (0, 0)),
    ]


def _mix_params(cw8, qg, kg, gco, gao, bias):
    seg = np.zeros((AW, 128), np.float32)
    seg[np.arange(AW), np.arange(AW) // HD] = 1.0
    return (cw8, jnp.tile(qg, (1, NH)), jnp.tile(kg, (1, NKV)), gco, gao, jnp.asarray(seg, BF), bias)


def _mix_fwd(proj, sinks, cw8, qg, kg, gco, gao, bias, comm=()):
    def body(sink_ref, p_ref, zc_ref, zh_ref, pkv_ref, cw_ref, qg_ref, kg_ref, gco_ref, gao_ref, seg_ref, bias_ref, y_ref):
        tile = p_ref[...]
        for b in range(BPS):
            f = _mix_forward(*_block_inputs(tile, b, zc_ref, zh_ref, pkv_ref, pl.program_id(0) == 0), cw_ref[...],
                             qg_ref[...], kg_ref[...], gco_ref[...], gao_ref[...], seg_ref[...], sink_ref, bias_ref, by_head=True)
            y_ref[b * BLK:(b + 1) * BLK, :] = f["y"].astype(BF)

    return _call(
        body, (sinks, proj, proj, proj, proj, *_mix_params(cw8, qg, kg, gco, gao, bias)), name="mix_fwd", grid=(T // TILE,),
        in_specs=_mix_in_specs(lambda s: s) + _mix_param_specs(),
        out_specs=[pl.BlockSpec((TILE, D), lambda s: (s, 0))], out_shape=[SDS((T, D), BF)],
        sem=("parallel",), vmem_mib=40, comm=comm, free=tuple(range(5, 12)))


def _mix_bwd(proj, dy, sinks, cw8, qg, kg, gco, gao, bias, comm=()):
    n_steps = T // TILE

    def tile_of(s):
        return n_steps - 1 - s

    def body(sink_ref, p_ref, zc_ref, zh_ref, pkv_ref, dy_ref, cw_ref, qg_ref, kg_ref, gco_ref, gao_ref, seg_ref, bias_ref,
             dproj_ref, dcw_ref, dqg_ref, dkg_ref, dgco_ref, dgao_ref, dsink_ref, dbias_ref,
             ndcz_ref, dkc_ref, dvc_ref):
        s = pl.program_id(0)

        @pl.when(s == 0)
        def _():
            for r in (dcw_ref, dqg_ref, dkg_ref, dgco_ref, dgao_ref, dsink_ref, dbias_ref, ndcz_ref, dkc_ref, dvc_ref):
                r[...] = jnp.zeros_like(r)

        params = (cw_ref[...], qg_ref[...], kg_ref[...], gco_ref[...], gao_ref[...], seg_ref[...])
        tile = p_ref[...]
        carry = (ndcz_ref[...], dkc_ref[...], dvc_ref[...])
        total = None
        for b in reversed(range(BPS)):
            f = _mix_forward(*_block_inputs(tile, b, zc_ref, zh_ref, pkv_ref, s == n_steps - 1), *params, sink_ref, bias_ref)
            pieces, sums, carry = one_block(f, dy_ref[b * BLK:(b + 1) * BLK, :], params, carry)
            for lo, piece in pieces:
                dproj_ref[b * BLK:(b + 1) * BLK, lo:lo + piece.shape[1]] = piece
            total = sums if total is None else [t + v for t, v in zip(total, sums)]
        ndcz_ref[...], dkc_ref[...], dvc_ref[...] = carry
        dcw, dqg_t, dkg_t, dgco, dgao, dsink, *ds = total
        dcw_ref[0:3, :] += dcw
        dqg_ref[...] += _fold_heads(dqg_t)
        dkg_ref[...] += _fold_heads(dkg_t)
        dgco_ref[...] += dgco
        dgao_ref[...] += dgao
        dsink_ref[...] += dsink
        for h in range(NH):
            dbias_ref[h * BLK:(h + 1) * BLK, :] += ds[h]

    def one_block(f, dy, params, carry):
        cw, qg_v, kg_v, gco_v, gao_v, seg = params
        nxt, dk_carry, dv_carry = carry
        dyc, dgco = _rms_bwd(dy[:, 0:CW], f["y_conv"], f["rc"], gco_v)
        dya, dgao = _rms_bwd(dy[:, CW:CW + AW], f["y_attn"], f["ra"], gao_v)

        row = f["row"]
        dgate_b = dyc * f["cz"]
        dcz = dyc * f["gate_b"]
        dcw = jnp.concatenate([jnp.sum(dcz * f[k], axis=0, keepdims=True) for k in ("z2", "z1", "z")], axis=0)
        n0 = nxt[0:1, :]
        n1 = nxt[1:2, :]
        d1 = jnp.where(row == BLK - 1, n0, pltpu.roll(dcz, BLK - 1, 0))
        d2 = jnp.where(row == BLK - 1, n1, jnp.where(row == BLK - 2, n0, pltpu.roll(dcz, BLK - 2, 0)))
        dz = cw[2:3, :] * dcz + cw[1:2, :] * d1 + cw[0:1, :] * d2
        pieces = [(0, dgate_b.astype(BF)), (CW, (dz * f["hc"]).astype(BF)), (2 * CW, (dz * f["gate_c"]).astype(BF))]

        scale = f["scale"]
        lane = lax.broadcasted_iota(jnp.int32, (1, 128), 1)
        dsink = jnp.zeros((1, 128), F32)
        dq_cols, dk_cols, dv_cols, dk_prev, dv_prev, ds = [], [], [], [], [], []
        for kv in range(NKV):
            dKb = dVb = 0.0
            for h in range(kv * GQ, (kv + 1) * GQ):
                hd = f["heads"][h]
                dO = dya[:, h * HD:(h + 1) * HD]
                delta = jnp.sum(dO * hd["O"], axis=-1, keepdims=True)
                dOb = dO.astype(BF)
                dP = _dot(dOb, hd["vb"], 1, 1)
                dS = hd["probs"] * (dP - delta)
                tot = jnp.sum(hd["psink"] * delta, axis=0, keepdims=True)
                dsink = dsink - jnp.where(lane == h, tot, 0.0)
                ds.append(dS)
                dSb = dS.astype(BF)
                dq_cols.append(_dot(dSb, hd["kb"], 1, 0))
                dKb = dKb + _dot(dSb, hd["Q"], 0, 0)
                dVb = dVb + _dot(hd["probs"].astype(BF), dOb, 0, 0)
            dk_cols.append(dKb[BLK:, :] + dk_carry[:, kv * HD:(kv + 1) * HD])
            dv_cols.append(dVb[BLK:, :] + dv_carry[:, kv * HD:(kv + 1) * HD])
            dk_prev.append(dKb[:BLK, :])
            dv_prev.append(dVb[:BLK, :])
        dq_raw, dqg_t = _head_norm_bwd(jnp.concatenate(dq_cols, axis=1) * scale, f["q_raw"], f["rq"], qg_v, seg)
        dk_raw, dkg_t = _head_norm_bwd(jnp.concatenate(dk_cols, axis=1), f["k_raw"][BLK:, :], f["rk"][BLK:, :], kg_v, seg)
        pieces.append((3 * CW, jnp.concatenate([dq_raw, dk_raw] + dv_cols, axis=1).astype(BF)))
        owed = (dcz[0:8, :], jnp.concatenate(dk_prev, axis=1), jnp.concatenate(dv_prev, axis=1))
        return pieces, [dcw, dqg_t, dkg_t, dgco, dgao, dsink, *ds], owed

    small = lambda r, c: pl.BlockSpec((r, c), lambda s: (0, 0))
    return _call(
        body, (sinks, proj, proj, proj, proj, dy, *_mix_params(cw8, qg, kg, gco, gao, bias)), name="mix_bwd", grid=(n_steps,),
        in_specs=_mix_in_specs(tile_of) + [pl.BlockSpec((TILE, D), lambda s: (tile_of(s), 0))] + _mix_param_specs(),
        out_specs=[pl.BlockSpec((TILE, INW), lambda s: (tile_of(s), 0)), small(8, CW), small(1, HD), small(1, HD),
                   small(1, CW), small(1, AW), small(1, 128), small(NH * BLK, 2 * BLK)],
        out_shape=[SDS((T, INW), BF), SDS((8, CW), F32), SDS((1, HD), F32), SDS((1, HD), F32), SDS((1, CW), F32),
                   SDS((1, AW), F32), SDS((1, 128), F32), SDS((NH * BLK, 2 * BLK), F32)],
        scratch_shapes=[pltpu.VMEM((8, CW), F32), pltpu.VMEM((BLK, NKV * HD), F32), pltpu.VMEM((BLK, NKV * HD), F32)],
        sem=("arbitrary",), vmem_mib=56, comm=comm, free=(1, 2, 3, 4) + tuple(range(6, 13)))


FT = 256
NFT = DFF // FT
RC = 1024
NCH = T // RC
LEAD = 16


def _rows8(x):
    return jnp.sum(x.reshape(x.shape[0] // 8, 8, x.shape[1]), axis=0)


def _ffn_act_specs():
    return [
        pl.BlockSpec((T, FT), lambda j: (0, j)), pl.BlockSpec((T, FT), lambda j: (0, NFT + j)),
        pl.BlockSpec((8, FT), lambda j: (0, j)), pl.BlockSpec((8, FT), lambda j: (0, NFT + j)),
        pl.BlockSpec((1, FT), lambda j: (0, j)), pl.BlockSpec((1, FT), lambda j: (0, NFT + j)),
    ]


def _conv_rows(win, w, b, n):
    win = win.astype(F32)
    u = win[LEAD:LEAD + n]
    u1 = pltpu.roll(win, 1, 0)[LEAD:LEAD + n]
    u2 = pltpu.roll(win, 2, 0)[LEAD:LEAD + n]
    return u2, u1, u, w[0:1, :] * u2 + w[1:2, :] * u1 + w[2:3, :] * u + b


def _ffn_act(up, fw8, fb, comm=()):
    def body(ug_ref, uv_ref, wg_ref, wv_ref, bg_ref, bv_ref, a_ref):
        wg, wv, bg, bv = wg_ref[...], wv_ref[...], bg_ref[...], bv_ref[...]

        def chunk(win_g, win_v):
            gp = _conv_rows(win_g, wg, bg, RC)[3]
            vp = _conv_rows(win_v, wv, bv, RC)[3]
            return (gp * jax.nn.sigmoid(gp) * vp).astype(BF)

        zero = jnp.zeros((LEAD, FT), BF)
        a_ref[0:RC, :] = chunk(jnp.concatenate([zero, ug_ref[0:RC, :]], axis=0),
                               jnp.concatenate([zero, uv_ref[0:RC, :]], axis=0))

        def step(i, carry):
            r0 = pl.multiple_of(i * RC, RC)
            win = pl.ds(r0 - LEAD, RC + LEAD)
            a_ref[pl.ds(r0, RC), :] = chunk(ug_ref[win, :], uv_ref[win, :])
            return carry

        lax.fori_loop(1, NCH, step, 0)

    return _call(
        body, (up, up, fw8, fw8, fb, fb), name="ffn_act", grid=(NFT,), in_specs=_ffn_act_specs(),
        out_specs=[pl.BlockSpec((T, FT), lambda j: (0, j))], out_shape=[SDS((T, DFF), BF)],
        sem=("parallel",), vmem_mib=40, comm=comm, free=(2, 3, 4, 5))


def _ffn_act_bwd(up, da, fw8, fb, comm=()):
    ext = RC + LEAD

    def body(ug_ref, uv_ref, wg_ref, wv_ref, bg_ref, bv_ref, da_ref,
             dug_ref, duv_ref, dwg_ref, dwv_ref, dbg_ref, dbv_ref):
        wg, wv, bg, bv = wg_ref[...], wv_ref[...], bg_ref[...], bv_ref[...]

        def chunk(win_g, win_v, da_e):
            g2, g1, g0, gp = _conv_rows(win_g, wg, bg, ext)
            v2, v1, v0, vp = _conv_rows(win_v, wv, bv, ext)
            da_e = da_e.astype(F32)
            sig = jax.nn.sigmoid(gp)
            dvp = da_e * (gp * sig)
            dgp = da_e * vp * (sig * (1.0 + gp * (1.0 - sig)))

            def back(dp, w):
                return (w[2:3, :] * dp[0:RC] + w[1:2, :] * pltpu.roll(dp, ext - 1, 0)[0:RC]
                        + w[0:1, :] * pltpu.roll(dp, ext - 2, 0)[0:RC]).astype(BF)

            def sums(dp, u2, u1, u0):
                d = dp[0:RC]
                return [_rows8(d), _rows8(d * u2[0:RC]), _rows8(d * u1[0:RC]), _rows8(d * u0[0:RC])]

            return back(dgp, wg), back(dvp, wv), sums(dgp, g2, g1, g0) + sums(dvp, v2, v1, v0)

        zero = jnp.zeros((LEAD, FT), BF)
        dug, duv, acc = chunk(jnp.concatenate([zero, ug_ref[0:ext, :]], axis=0),
                              jnp.concatenate([zero, uv_ref[0:ext, :]], axis=0), da_ref[0:ext, :])
        dug_ref[0:RC, :] = dug
        duv_ref[0:RC, :] = duv

        def step(i, acc):
            r0 = pl.multiple_of(i * RC, RC)
            win = pl.ds(r0 - LEAD, ext + LEAD)
            dug, duv, part = chunk(ug_ref[win, :], uv_ref[win, :], da_ref[pl.ds(r0, ext), :])
            dug_ref[pl.ds(r0, RC), :] = dug
            duv_ref[pl.ds(r0, RC), :] = duv
            return [a + p for a, p in zip(acc, part)]

        acc = lax.fori_loop(1, NCH - 1, step, acc)
        r0 = T - RC
        tail = lambda ref, lo: jnp.concatenate([ref[lo:T, :], zero], axis=0)
        dug, duv, part = chunk(tail(ug_ref, r0 - LEAD), tail(uv_ref, r0 - LEAD), tail(da_ref, r0))
        dug_ref[r0:T, :] = dug
        duv_ref[r0:T, :] = duv
        tot = [jnp.sum(a + p, axis=0, keepdims=True) for a, p in zip(acc, part)]
        for k, (dw_ref, db_ref) in enumerate(((dwg_ref, dbg_ref), (dwv_ref, dbv_ref))):
            db_ref[...] = tot[4 * k]
            dw_ref[...] = jnp.zeros_like(dw_ref)
            for r in range(3):
                dw_ref[r:r + 1, :] = tot[4 * k + 1 + r]

    col = lambda r: pl.BlockSpec((r, FT), lambda j: (0, j))
    return _call(
        body, (up, up, fw8, fw8, fb, fb, da), name="ffn_act_bwd", grid=(NFT,),
        in_specs=_ffn_act_specs() + [pl.BlockSpec((T, FT), lambda j: (0, j))],
        out_specs=[col(T), col(T), col(8), col(8), col(1), col(1)],
        out_shape=[SDS((T, DFF), BF), SDS((T, DFF), BF), SDS((8, DFF), F32), SDS((8, DFF), F32),
                   SDS((1, DFF), F32), SDS((1, DFF), F32)],
        sem=("parallel",), vmem_mib=40, comm=comm, free=(0, 1, 2, 3, 4, 5))


def _ffn_down_bwd(dh2b, w_down, comm=()):
    tm = TM

    def body(d_ref, w_ref, o_ref):
        o_ref[...] = _dot(d_ref[...], w_ref[...], 1, 1).astype(BF)

    return _call(
        body, (dh2b, w_down), name="ffn_down_bwd", grid=(T // tm,),
        in_specs=[pl.BlockSpec((tm, D), lambda i: (i, 0)), _resident((DFF, D))],
        out_specs=[pl.BlockSpec((tm, DFF), lambda i: (i, 0))], out_shape=[SDS((T, DFF), BF)],
        sem=("parallel",), vmem_mib=40, comm=comm, free=(0, 1))


def _norm_matmul_bwd(name, a_list, w_t, k_offsets, xin, g, dres, want_bf16, comm=(), slot=None):
    tm = TM
    ks = [a.shape[1] for a in a_list]
    n_a = len(a_list)
    n_pre = 0 if slot is None else 1

    def body(*refs):
        refs = refs[n_pre:]
        a_refs = refs[:n_a]
        w_ref, x_ref, g_ref, r_ref = refs[n_a:n_a + 4]
        outs = refs[n_a + 4:]
        dx_ref, dg_ref = outs[0], (outs[-1] if slot is None else outs[-1].at[0])

        @pl.when(pl.program_id(0) == 0)
        def _():
            dg_ref[...] = jnp.zeros_like(dg_ref)

        du = _dot(a_refs[0][...], w_ref[k_offsets[0]:k_offsets[0] + ks[0], :], 1, 0)
        for k in range(1, n_a):
            du = du + _dot(a_refs[k][...], w_ref[k_offsets[k]:k_offsets[k] + ks[k], :], 1, 0)
        x = x_ref[...]
        r = lax.rsqrt(jnp.mean(x * x, axis=-1, keepdims=True) + EPS)
        dx, dg = _rms_bwd(du, x, r, g_ref[...])
        dx = r_ref[...] + dx
        dx_ref[...] = dx
        if want_bf16:
            outs[1][...] = dx.astype(BF)
        dg_ref[...] += dg

    tile = lambda c: pl.BlockSpec((tm, c), lambda i, *_: (i, 0))
    if slot is None:
        dg_spec, dg_shape = pl.BlockSpec((1, D), lambda i: (0, 0)), SDS((1, D), F32)
    else:
        dg_spec, dg_shape = pl.BlockSpec((1, 1, D), lambda i, slot_ref: (slot_ref[0], 0, 0)), SDS((N_DEV, 1, D), F32)
    out_specs = [tile(D)] + ([tile(D)] if want_bf16 else []) + [dg_spec]
    out_shape = [SDS((T, D), F32)] + ([SDS((T, D), BF)] if want_bf16 else []) + [dg_shape]
    return _call(
        body, (*a_list, w_t, xin, g, dres), name=name, grid=(T // tm,), prefetch=() if slot is None else (slot,),
        in_specs=[tile(k) for k in ks] + [_resident(w_t.shape), tile(D),
                                           pl.BlockSpec((1, D), lambda i, *_: (0, 0)), tile(D)],
        out_specs=out_specs, out_shape=out_shape, sem=("arbitrary",), vmem_mib=56, comm=comm, free=tuple(range(n_a + 4)))


def _out_bwd(dh1b, w_out, comm=()):
    tm = TM

    def body(d_ref, w_ref, o_ref):
        o_ref[...] = _dot(d_ref[...], w_ref[...], 1, 1)

    return _call(
        body, (dh1b, w_out), name="out_bwd", grid=(T // tm,),
        in_specs=[pl.BlockSpec((tm, D), lambda i: (i, 0)), _resident((D, D))],
        out_specs=[pl.BlockSpec((tm, D), lambda i: (i, 0))], out_shape=[SDS((T, D), F32)],
        sem=("parallel",), vmem_mib=32, comm=comm, free=(0, 1))


def _wgrad(name, a_list, b, old_a, comm=()):
    m_k = a_list[0].shape[1]
    tm = max(t for t in range(128, m_k // 2 + 1, 128) if m_k % t == 0)
    steps = [a.shape[1] // tm for a in a_list]
    starts = [sum(steps[:k]) for k in range(len(a_list))]
    n_a = len(a_list)

    def body(*refs):
        a_refs, b_ref, o_ref = refs[:n_a], refs[n_a], refs[n_a + 1]
        i = pl.program_id(0)
        for k in range(n_a):
            @pl.when((i >= starts[k]) & (i < starts[k] + steps[k]))
            def _(k=k):
                o_ref[...] = _dot(a_refs[k][...], b_ref[...], 0, 0).astype(BF)

    def a_spec(k):
        return pl.BlockSpec((T, tm), lambda i: (0, jnp.clip(i - starts[k], 0, steps[k] - 1)))

    m_total = tm * sum(steps)
    return _call(
        body, (*a_list, b), name=name, grid=(sum(steps),),
        in_specs=[a_spec(k) for k in range(n_a)] + [_resident((T, D))],
        out_specs=[pl.BlockSpec((tm, D), lambda i: (i, 0))], out_shape=[SDS((m_total, D), BF)],
        sem=("parallel",), vmem_mib=40, comm=comm, free=() if old_a is None else tuple(range(n_a)) if old_a else (n_a,))


def _chip_sum(name, gbf, from_sib, core, chip):
    h = gbf.shape[1]
    th = h

    def body(core_ref, chip_ref, g_ref, s_ref, pbf_ref, own_ref):
        p = g_ref[0].astype(F32) + s_ref[0].astype(F32)
        pbf_ref[0] = p.astype(BF)

        @pl.when(pl.program_id(1) == chip_ref[0])
        def _():
            own_ref[...] = p

    grid_spec = pltpu.PrefetchScalarGridSpec(
        num_scalar_prefetch=2, grid=(h // th, N_CHIPS),
        in_specs=[pl.BlockSpec((1, th, D), lambda t, jj, core_ref, chip_ref: (2 * jj + core_ref[0], t, 0)),
                  pl.BlockSpec((1, th, D), lambda t, jj, core_ref, chip_ref: (jj, t, 0))],
        out_specs=[pl.BlockSpec((1, th, D), lambda t, jj, core_ref, chip_ref: (jj, t, 0)),
                   pl.BlockSpec((th, D), lambda t, jj, core_ref, chip_ref: (t, 0))],
    )
    return _pcall(
        body, name=name, grid_spec=grid_spec, out_shape=_in_hbm([SDS((N_CHIPS, h, D), BF), SDS((h, D), F32)]),
        compiler_params=_params(("arbitrary", "arbitrary"), 32),
    )(core, chip, *_from_hbm(gbf, from_sib))


def _final_sum(name, own, from_chips, core, comm=()):
    h = own.shape[0]

    def body(core_ref, o_ref, r_ref, f_ref):
        f_ref[0] = ((o_ref[...] + r_ref[0].astype(F32)) + r_ref[1].astype(F32)) + r_ref[2].astype(F32)

    return _call(
        body, (own, from_chips), name=name, grid=(1,), prefetch=(core,),
        in_specs=[pl.BlockSpec((h, D), lambda i, core_ref: (0, 0)), pl.BlockSpec((3, h, D), lambda i, core_ref: (0, 0, 0))],
        out_specs=[pl.BlockSpec((1, h, D), lambda i, core_ref: (core_ref[0], 0, 0))], out_shape=[SDS((2, h, D), F32)],
        sem=("arbitrary",), vmem_mib=40, comm=comm)


def _adam_math(w, g, m, v):
    nm = ADAM_B1 * m + (1.0 - ADAM_B1) * g
    nv = ADAM_B2 * v + (1.0 - ADAM_B2) * (g * g)
    m_hat = nm / (1.0 - ADAM_B1 ** ADAM_STEP)
    v_hat = nv / (1.0 - ADAM_B2 ** ADAM_STEP)
    return -ADAM_LR * (m_hat / (jnp.sqrt(v_hat) + ADAM_EPS) + ADAM_WD * w), nm, nv


def _adamw(name, w, g, m, v, tr, copy_g=False, stage=True, g_transposed=False):
    rows, cols = w.shape

    def body(w_ref, g_ref, m_ref, v_ref, *outs):
        d_ref, nm_ref, nv_ref = outs[-3:]
        for c in [pl.ds(c0, 128) for c0 in range(0, cols, 128)] if g_transposed else [slice(None)]:
            g_val = g_ref[c, :].T if g_transposed else g_ref[...]
            if copy_g:
                outs[0][:, c] = g_val
            d_ref[:, c], nm_ref[:, c], nv_ref[:, c] = _adam_math(w_ref[:, c], g_val, m_ref[:, c], v_ref[:, c])

    spec = pl.BlockSpec((tr, cols), lambda i: (i, 0))
    n_out = 4 if copy_g else 3
    g_spec = pl.BlockSpec((cols, tr), lambda i: (0, i)) if g_transposed else spec
    return _call(body, (w, g, m, v), name=name, grid=(rows // tr,), in_specs=[spec, g_spec, spec, spec], out_specs=[spec] * n_out,
                 out_shape=[SDS((rows, cols), F32)] * n_out, sem=("parallel",), vmem_mib=32,
                 free=(0, 2, 3) if stage else ())


C_SQ = 2 * DFF
P_W = C_SQ + 128
R_G2, R_GO, R_DCW, R_QK = 0, 1, 2, 5
C_GCO, C_GAO, C_DQG, C_DKG, C_SINK = 0, CW, 0, 128, 256


def _pack(name, me, ins, width, fill):
    def body(me_ref, *refs):
        o = refs[-1]
        o[...] = jnp.zeros_like(o)
        fill(o, *refs[:-1])

    return _call(body, ins, name=name, grid=(1,), prefetch=(me,),
                 in_specs=[pl.BlockSpec(a.shape, lambda i, me_ref: (0, 0)) for a in ins],
                 out_specs=[pl.BlockSpec((1, 8, width), lambda i, me_ref: (me_ref[0], 0, 0))],
                 out_shape=[SDS((N_DEV, 8, width), F32)], sem=("arbitrary",))[0]


def _pack_ffn(me, dfwg, dfwv, dfbg, dfbv, sq):
    def fill(o, dfwg_r, dfwv_r, dfbg_r, dfbv_r, sq_r):
        o[0, :, 0:DFF] = dfwg_r[...]
        o[0, :, DFF:2 * DFF] = dfwv_r[...]
        o[0, 3:4, 0:DFF] = dfbg_r[...]
        o[0, 3:4, DFF:2 * DFF] = dfbv_r[...]
        o[0, :, C_SQ:C_SQ + 128] = sq_r[...]

    return _pack("pack_ffn", me, (dfwg, dfwv, dfbg, dfbv, sq), P_W, fill)


def _pack_mix(me, dg2, dgco, dgao, dcw8, dqg, dkg, dsink):
    def fill(o, dg2_r, dgco_r, dgao_r, dcw_r, dqg_r, dkg_r, dsink_r):
        o[0, R_G2:R_G2 + 1, :] = dg2_r[...]
        o[0, R_GO:R_GO + 1, C_GCO:C_GCO + CW] = dgco_r[...]
        o[0, R_GO:R_GO + 1, C_GAO:C_GAO + AW] = dgao_r[...]
        o[0, R_DCW:R_DCW + 3, 0:CW] = dcw_r[0:3, :]
        o[0, R_QK:R_QK + 1, C_DQG:C_DQG + HD] = dqg_r[...]
        o[0, R_QK:R_QK + 1, C_DKG:C_DKG + HD] = dkg_r[...]
        o[0, R_QK:R_QK + 1, C_SINK:C_SINK + 128] = dsink_r[...]

    return _pack("pack_mix", me, (dg2, dgco, dgao, dcw8, dqg, dkg, dsink), D, fill)


N_SMALL = 11


def _small_adam(chip, p_all, pm_all, g1_all, tbl_all, ws, ms, vs):
    fw_cols = 2 * DFF // N_CHIPS
    cw_cols = CW // N_CHIPS

    def body(chip_ref, p_ref, fw_ref, pm_ref, cw_ref, g1_ref, tbl_ref, *refs):
        w_r, m_r, v_r = refs[0:N_SMALL], refs[N_SMALL:2 * N_SMALL], refs[2 * N_SMALL:3 * N_SMALL]
        outs = refs[3 * N_SMALL:]
        g_o, d_o, nm_o, nv_o = (outs[k * N_SMALL:(k + 1) * N_SMALL] for k in range(4))
        loss_o = outs[4 * N_SMALL]

        def total(ref):
            s = ref[0]
            for k in range(1, N_DEV):
                s = s + ref[k]
            return s

        S = total(p_ref)
        fw = total(fw_ref)
        M = total(pm_ref)
        cw = total(cw_ref)

        def step(i, g, at):
            d, nm, nv = _adam_math(w_r[i][at], g, m_r[i][at], v_r[i][at])
            g_o[i][at], d_o[i][at], nm_o[i][at], nv_o[i][at] = g, d, nm, nv

        everything = (slice(None), slice(None))
        step(0, total(g1_ref), everything)
        for r in range(3):
            step(1, cw[R_DCW + r:R_DCW + r + 1, :], (r, slice(None), slice(None)))
        step(2, M[R_QK:R_QK + 1, C_DQG:C_DQG + HD], everything)
        step(3, M[R_QK:R_QK + 1, C_DKG:C_DKG + HD], everything)
        step(4, total(tbl_ref), everything)
        step(5, M[R_QK:R_QK + 1, C_SINK:C_SINK + NH], everything)
        step(6, M[R_GO:R_GO + 1, C_GCO:C_GCO + CW], everything)
        step(7, M[R_GO:R_GO + 1, C_GAO:C_GAO + AW], everything)
        step(8, M[R_G2:R_G2 + 1, :], everything)
        for r in range(3):
            step(9, fw[r:r + 1, :], (r, slice(None), slice(None)))
        step(10, S[3:4, 0:2 * DFF], everything)
        sq = S[:, C_SQ:C_SQ + 128]
        loss_o[...] = jnp.sum(jnp.sum(sq, axis=1, keepdims=True), axis=0, keepdims=True) * (0.5 / D)

    def full(a):
        n = len(a.shape)
        return pl.BlockSpec(a.shape, lambda i, chip_ref: (0,) * n)

    params = [*ws, *ms, *vs]
    out = _call(
        body, (p_all, p_all, pm_all, pm_all, g1_all, tbl_all, *params), name="small_adam", grid=(1,), prefetch=(chip,),
        in_specs=[full(p_all),
                  pl.BlockSpec((N_DEV, 8, fw_cols), lambda i, chip_ref: (0, 0, chip_ref[0])),
                  full(pm_all),
                  pl.BlockSpec((N_DEV, 8, cw_cols), lambda i, chip_ref: (0, 0, chip_ref[0])),
                  full(g1_all), full(tbl_all), *[full(a) for a in params]],
        out_specs=[full(a) for a in ws] * 4 + [pl.BlockSpec((1, 1), lambda i, chip_ref: (0, 0))],
        out_shape=[SDS(a.shape, F32) for a in ws] * 4 + [SDS((1, 1), F32)], sem=("arbitrary",), vmem_mib=32)
    return out[0:N_SMALL], out[N_SMALL:2 * N_SMALL], out[2 * N_SMALL:3 * N_SMALL], out[3 * N_SMALL:4 * N_SMALL], out[4 * N_SMALL]


PLACE_STEPS = 4


def _place_specs(shards):
    rows = [s.shape[0] // PLACE_STEPS for s in shards]
    return ([pl.BlockSpec((r, D), lambda i, chip_ref: (i, 0)) for r in rows],
            [pl.BlockSpec((r, D), lambda i, chip_ref: (chip_ref[0] * PLACE_STEPS + i, 0)) for r in rows],
            [SDS((N_CHIPS * s.shape[0], D), BF) for s in shards])


def _place_first(chip, shard, conv_w, ffn_conv_w):
    def body(chip_ref, a, s0, s1, o, t0, t1):
        o[...] = a[...].astype(BF)

        @pl.when(pl.program_id(0) == 0)
        def _():
            for s, t in ((s0, t0), (s1, t1)):
                t[...] = jnp.zeros_like(t)
                t[0, 0:3, :] = s[...]

    ins, outs, shapes = _place_specs([shard])
    taps = (conv_w, ffn_conv_w)
    return _call(
        body, (shard, conv_w, ffn_conv_w), name="place_first", grid=(PLACE_STEPS,), prefetch=(chip,),
        in_specs=ins + [pl.BlockSpec(s.shape, lambda i, chip_ref: (0, 0)) for s in taps],
        out_specs=outs + [pl.BlockSpec((1, 8, s.shape[1]), lambda i, chip_ref: (chip_ref[0], 0, 0)) for s in taps],
        out_shape=shapes + [SDS((N_CHIPS, 8, s.shape[1]), F32) for s in taps],
        sem=("arbitrary",), vmem_mib=32, free=(1, 2))


def _place_rest(chip, shards, w_up, table, bucket, comm):
    n = len(shards)
    c_up = w_up.shape[1]
    edges = [round(k * (c_up // 128) / PLACE_STEPS) * 128 for k in range(PLACE_STEPS + 1)]

    def body(chip_ref, *refs):
        a, (up_ref, tab_ref, bk_ref), o = refs[:n], refs[n:n + 3], refs[n + 3:2 * n + 3]
        up_o, bias_ref = refs[2 * n + 3:]
        for src, dst in zip(a, o):
            dst[...] = src[...].astype(BF)
        for k in range(PLACE_STEPS):
            @pl.when(pl.program_id(0) == k)
            def _(k=k):
                up_o[edges[k]:edges[k + 1], :] = up_ref[:, edges[k]:edges[k + 1]].T.astype(BF)

        @pl.when(pl.program_id(0) == 0)
        def _():
            bk = bk_ref[...]
            eq = [bk == b for b in range(NBUCKET)]
            for h in range(NH):
                acc = jnp.zeros((BLK, 2 * BLK), F32)
                for b in range(NBUCKET):
                    acc = jnp.where(eq[b], tab_ref[h, b], acc)
                bias_ref[h * BLK:(h + 1) * BLK, :] = acc

    ins, outs, shapes = _place_specs(shards)
    return _call(
        body, (*shards, w_up, table, bucket), name="place_rest", grid=(PLACE_STEPS,), prefetch=(chip,),
        in_specs=ins + [_resident(w_up.shape), pl.BlockSpec(memory_space=pltpu.SMEM),
                        pl.BlockSpec(bucket.shape, lambda i, chip_ref: (0, 0))],
        out_specs=outs + [pl.BlockSpec((c_up, D), lambda i, chip_ref: (chip_ref[0], 0)),
                          pl.BlockSpec((NH * BLK, 2 * BLK), lambda i, chip_ref: (0, 0))],
        out_shape=shapes + [SDS((N_CHIPS * c_up, D), BF), SDS((NH * BLK, 2 * BLK), F32)],
        sem=("arbitrary",), vmem_mib=32, comm=comm, free=(n + 1, n + 2))


def kernel(x, norm_mix_g, w_in, conv_w, q_norm_g, k_norm_g, rel_bias_table, sinks, out_norm_conv_g, out_norm_attn_g, w_out, norm_ffn_g, w_up, ffn_conv_w, ffn_conv_b, w_down, loss_target, m_norm_mix_g, m_w_in, m_conv_w, m_q_norm_g, m_k_norm_g, m_rel_bias_table, m_sinks, m_out_norm_conv_g, m_out_norm_attn_g, m_w_out, m_norm_ffn_g, m_w_up, m_ffn_conv_w, m_ffn_conv_b, m_w_down, v_norm_mix_g, v_w_in, v_conv_w, v_q_norm_g, v_k_norm_g, v_rel_bias_table, v_sinks, v_out_norm_conv_g, v_out_norm_attn_g, v_w_out, v_norm_ffn_g, v_w_up, v_ffn_conv_w, v_ffn_conv_b, v_w_down):
    as_arg = lambda i: jnp.reshape(i, (1,)).astype(jnp.int32)
    chip = as_arg(2 * lax.axis_index("x") + lax.axis_index("y"))
    core = as_arg(lax.axis_index("c"))
    me = 2 * chip + core
    xs, tgt = x[0], loss_target[0]
    qg, kg, gco, gao, g1, g2, fb = q_norm_g, k_norm_g, out_norm_conv_g, out_norm_attn_g, norm_mix_g, norm_ffn_g, ffn_conv_b
    pieces = lambda g: g.reshape(N_DEV, g.shape[0] // N_DEV, D)
    whole = lambda f: f.reshape(2 * f.shape[1], D)

    bucket = jnp.asarray(_bucket_table())
    p_in, p_cw, p_fw = _place_first(chip, w_in[0].T, conv_w[0], ffn_conv_w[0])
    p_out, p_down, p_up, bias, w_int, cw_all, fw_all = _place_rest(
        chip, [w_out[0], w_down[0]], w_up[0], rel_bias_table.T, bucket,
        comm=[_t_gather(p_in, relayed_first=True), _t_small_weights(p_cw), _t_small_weights(p_fw)])
    cw8 = jnp.transpose(cw_all, (1, 0, 2)).reshape(8, CW)
    fw8 = jnp.transpose(fw_all, (1, 0, 2)).reshape(8, 2 * DFF)

    early = 3 / 11
    proj, u1, w_out_f, p_up = _inproj(xs, g1, w_int, comm=[_t_gather(p_out), _t_gather(p_up, (0, early))])
    y, w_upt = _mix_fwd(proj, sinks, cw8, qg, kg, gco, gao, bias, comm=[_t_gather(p_up, (early, 1))])
    h1, u2 = _outproj(y, w_out_f, xs, g2)
    up, w_down_f = _ffn_up(u2, w_upt, comm=[_t_gather(p_down)])
    a, = _ffn_act(up, fw8, fb)
    dh2, dh2b, sq = _ffn_down(a, w_down_f, h1, tgt)

    gdbf, = _wgrad("wgrad_down", [a], dh2b, None)
    da, sib_down = _ffn_down_bwd(dh2b, w_down_f, comm=[_t_sibling(pieces(gdbf))])
    pbf_down, own_down = _chip_sum("chip_sum_w_down", pieces(gdbf), sib_down, core, chip)
    dug, duv, dfwg, dfwv, dfbg, dfbv, chips_down = _ffn_act_bwd(up, da, fw8, fb, comm=[_t_chips(pbf_down)])
    fin_down, = _final_sum("final_sum_w_down", own_down, chips_down, core)
    gubf, = _wgrad("wgrad_up", [dug, duv], u2, False)
    p_all = _pack_ffn(me, dfwg, dfwv, dfbg, dfbv, sq)
    dh1, dh1b, dg2, sib_up, fin_down, p_all = _norm_matmul_bwd(
        "ffn_up_bwd", [dug, duv], w_upt, [0, DFF], h1, g2, dh2, True,
        comm=[_t_sibling(pieces(gubf)), _t_swap(fin_down), _t_allgather(p_all)])
    pbf_up, own_up = _chip_sum("chip_sum_w_up", pieces(gubf), sib_up, core, chip)
    gobf, = _wgrad("wgrad_out", [y], dh1b, True)
    dy, sib_out = _out_bwd(dh1b, w_out_f, comm=[_t_sibling(pieces(gobf))])
    pbf_out, own_out = _chip_sum("chip_sum_w_out", pieces(gobf), sib_out, core, chip)
    dproj, dcw8, dqg, dkg, dgco, dgao, dsink, dbias, chips_up = _mix_bwd(
        proj, dy, sinks, cw8, qg, kg, gco, gao, bias, comm=[_t_chips(pbf_up)])
    fin_up, = _final_sum("final_sum_w_up", own_up, chips_up, core)
    tbl_all = _band_bias_bwd(dbias, bucket, me)
    pm_all = _pack_mix(me, dg2, dgco, dgao, dcw8, dqg, dkg, dsink)
    gibf, chips_out, fin_up, pm_all, tbl_all = _wgrad(
        "wgrad_in", [dproj], u1, False,
        comm=[_t_chips(pbf_out), _t_swap(fin_up), _t_allgather(pm_all), _t_allgather(tbl_all)])
    fin_out, sib_in = _final_sum("final_sum_w_out", own_out, chips_out, core, comm=[_t_sibling(pieces(gibf))])
    pbf_in, own_in = _chip_sum("chip_sum_w_in", pieces(gibf), sib_in, core, chip)
    dx, g1_all, chips_in, fin_out = _norm_matmul_bwd(
        "in_bwd", [dproj], w_int, [0], xs, g1, dh1, False, comm=[_t_chips(pbf_in), _t_swap(fin_out)], slot=me)
    fin_in, = _final_sum("final_sum_w_in", own_in, chips_in, core)
    g1_all, fin_in = _comm_call("gather_last", [_t_allgather(g1_all), _t_swap(fin_in)])

    g_w_out, g_w_down = whole(fin_out), whole(fin_down)
    g_w_down, d_down, nm_down, nv_down = _adamw("adamw_w_down", w_down[0], g_w_down, m_w_down[0], v_w_down[0], 352, True)
    g_w_up, d_up, nm_up, nv_up = _adamw(
        "adamw_w_up", w_up[0], whole(fin_up), m_w_up[0], v_w_up[0], 256, True, stage=False, g_transposed=True)
    g_w_out, d_out, nm_out, nv_out = _adamw("adamw_w_out", w_out[0], g_w_out, m_w_out[0], v_w_out[0], 256, True, stage=False)
    g_w_in, d_in, nm_in, nv_in = [a.T for a in _adamw(
        "adamw_w_in", w_in[0].T, whole(fin_in), m_w_in[0].T, v_w_in[0].T, INW // N_CHIPS // 3, True, stage=False)]
    taps = lambda a: jnp.transpose(a, (1, 0, 2))
    sw = [norm_mix_g, taps(conv_w), q_norm_g, k_norm_g, rel_bias_table.T, sinks, out_norm_conv_g, out_norm_attn_g,
          norm_ffn_g, taps(ffn_conv_w), ffn_conv_b]
    smm = [m_norm_mix_g, taps(m_conv_w), m_q_norm_g, m_k_norm_g, m_rel_bias_table.T, m_sinks, m_out_norm_conv_g,
           m_out_norm_attn_g, m_norm_ffn_g, taps(m_ffn_conv_w), m_ffn_conv_b]
    smv = [v_norm_mix_g, taps(v_conv_w), v_q_norm_g, v_k_norm_g, v_rel_bias_table.T, v_sinks, v_out_norm_conv_g,
           v_out_norm_attn_g, v_norm_ffn_g, taps(v_ffn_conv_w), v_ffn_conv_b]
    *small_out, loss = _small_adam(chip, p_all, pm_all, g1_all, tbl_all, sw, smm, smv)
    sg, sd, snm, snv = [list(r) for r in small_out]
    for r in (sg, sd, snm, snv):
        r[1], r[4], r[9] = taps(r[1]), r[4].T, taps(r[9])

    def order(s, b_in, b_out, b_up, b_down):
        return (s[0], b_in[None], s[1], s[2], s[3], s[4], s[5], s[6], s[7], b_out[None], s[8], b_up[None],
                s[9], s[10], b_down[None])

    return (loss.reshape(()), dx[None],
            *order(sg, g_w_in, g_w_out, g_w_up, g_w_down),
            *order(sd, d_in, d_out, d_up, d_down),
            *order(snm, nm_in, nm_out, nm_up, nm_down),
            *order(snv, nv_in, nv_out, nv_up, nv_down))
```

```python
import functools
import math

import numpy as np

import jax
import jax.numpy as jnp
from jax import lax
from jax.experimental import pallas as pl
from jax.experimental.pallas import tpu as pltpu

F32 = jnp.float32
BF = jnp.bfloat16
SDS = jax.ShapeDtypeStruct

T = 2048
D = 1024
CW = 512
AW = 512
HD = 64
NH = 8
NKV = 2
GQ = 4
INW = 2304
DFF = 2816
BLK = 128
NB = T // BLK
NBUCKET = 32
EPS = 1e-6
NEG_INF = -1e30
N_CHIPS = 4
N_DEV = 8

ADAM_LR = 0.001
ADAM_B1 = 0.9
ADAM_B2 = 0.999
ADAM_EPS = 1e-08
ADAM_WD = 0.01
ADAM_STEP = 10

TM = 512
MIB = 1024 * 1024
MESH = pl.DeviceIdType.MESH
ANY = pl.BlockSpec(memory_space=pl.ANY)

_pcall = pl.pallas_call


def _params(sem=None, vmem_mib=None, collective_id=None):
    kw = {} if collective_id is None else {"collective_id": collective_id}
    if sem is not None:
        kw["dimension_semantics"] = sem
    if vmem_mib is not None:
        kw["vmem_limit_bytes"] = vmem_mib * MIB
    return pltpu.CompilerParams(**kw)


def _resident(shape):
    return pl.BlockSpec(shape, lambda *_: (0,) * len(shape), pipeline_mode=pl.Buffered(1))


def _dot(a, b, ca, cb):
    return lax.dot_general(a, b, (((ca,), (cb,)), ((), ())), preferred_element_type=F32)


def _rms_bwd(dy, x, r, g):
    dg = jnp.sum(dy * (x * r), axis=0, keepdims=True)
    dgx = dy * g
    dx = r * dgx - x * (r * r * r) * jnp.mean(x * dgx, axis=-1, keepdims=True)
    return dx, dg


def _where():
    x, y, c = lax.axis_index("x"), lax.axis_index("y"), lax.axis_index("c")
    return x, y, c, [(1 - x, y), (x, 1 - y), (1 - x, 1 - y)]


def _rcopy(src, dst, ssem, rsem, dev):
    return pltpu.make_async_remote_copy(src_ref=src, dst_ref=dst, send_sem=ssem, recv_sem=rsem, device_id=dev,
                                        device_id_type=MESH)


SIBLING, Y_CHIP, X_CHIP, DIAGONAL_CHIP = 1, 2, 4, 6
OTHER_CHIPS = (Y_CHIP, X_CHIP, DIAGONAL_CHIP)
EVERYONE = tuple(range(1, N_DEV))
BARRIER_OF = {(SIBLING,): 0, (SIBLING, Y_CHIP, X_CHIP): 1, OTHER_CHIPS: 2, (SIBLING,) + OTHER_CHIPS: 3, EVERYONE: 4}


def _peer(rel):
    x, y, c, _ = _where()
    return x ^ ((rel >> 2) & 1), y ^ ((rel >> 1) & 1), c ^ (rel & 1)


class _Task:
    def __init__(self, ins, outs, alias, n_sem, start, finish, middle=None, peers=()):
        self.ins, self.outs, self.alias, self.n_sem, self.start, self.finish = ins, outs, alias, n_sem, start, finish
        self.middle = middle if middle is not None else (lambda *args: None)
        self.peers = peers


def _peers_of(comm):
    return tuple(sorted({p for t in comm for p in t.peers}))


def _enter(comm):
    peers = _peers_of(comm)
    barrier = pltpu.get_barrier_semaphore()
    for rel in peers:
        pl.semaphore_signal(barrier, inc=1, device_id=_peer(rel), device_id_type=MESH)
    pl.semaphore_wait(barrier, len(peers))


ROWS16 = 16


def _t_gather(placed, part=(0, 1), relayed_first=False):
    R = placed.shape[0] // N_CHIPS
    q = R // 4
    lo, hi = (round(f * (q // ROWS16)) * ROWS16 for f in part)

    def quarter(chip_index, core, k):
        return pl.ds(pl.multiple_of(chip_index * R + core * 2 * q + k * q + lo, ROWS16), hi - lo)

    def places():
        x, y, c, _ = _where()
        return c, 2 * x + y, 2 * (1 - x) + y, 2 * x + (1 - y), 2 * (1 - x) + (1 - y), (1 - x, y, c), (x, 1 - y, c), (x, y, 1 - c)

    def copy(buf, k, chip_index, core, quart, ss, rs, b, dev):
        window = buf.at[quarter(chip_index, core, quart)]
        return _rcopy(window, window, ss.at[b + k], rs.at[b + k], dev)

    def first_hop(cout, ss, rs, b, which):
        c, me, _, _, _, x_nbr, y_nbr, _ = places()
        for k, (quart, dev) in enumerate(((0, x_nbr), (1, y_nbr), (1, x_nbr), (0, y_nbr))):
            if k in which:
                copy(cout[0], k, me, c, quart, ss, rs, b, dev).start()

    def start(cin, cout, ss, rs, b):
        first_hop(cout, ss, rs, b, (0, 1) if relayed_first else (0, 1, 2, 3))

    def middle(cin, cout, ss, rs, b):
        c, _, xc, yc, _, x_nbr, y_nbr, sib = places()
        for k, chip_index, quart, dev in ((0, xc, 0, y_nbr), (1, yc, 1, x_nbr)):
            copy(cout[0], k, chip_index, c, quart, ss, rs, b, dev).wait_recv()
            copy(cout[0], 4 + k, chip_index, c, quart, ss, rs, b, dev).start()
            copy(cout[0], 6 + k, chip_index, c, quart, ss, rs, b, sib).start()
        if relayed_first:
            first_hop(cout, ss, rs, b, (2, 3))

    later = ((2, 1, 1), (3, 2, 0), (4, 3, 0), (5, 3, 1))

    def finish(cin, cout, ss, rs, b):
        c, me, xc, yc, dc, _, _, sib = places()
        chip_of = {1: xc, 2: yc, 3: dc}
        for k, whose, quart in later:
            copy(cout[0], k, chip_of[whose], c, quart, ss, rs, b, sib).wait_recv()
            copy(cout[0], 6 + k, chip_of[whose], c, quart, ss, rs, b, sib).start()
        for k, whose, quart in ((0, 1, 0), (1, 2, 1)) + later:
            copy(cout[0], 6 + k, chip_of[whose], 1 - c, quart, ss, rs, b, sib).wait_recv()
        for k in range(12):
            copy(cout[0], k, me, c, 0, ss, rs, b, sib).wait_send()

    return _Task([placed], [SDS(placed.shape, placed.dtype)], [(0, 0)], 12, start, finish, middle, peers=(SIBLING, Y_CHIP, X_CHIP))


def _t_small_weights(buf):
    def start(cin, cout, ss, rs, b):
        x, y, c, chips = _where()
        mine = cout[0].at[2 * x + y]
        for r, (px, py) in enumerate(chips):
            _rcopy(mine, mine, ss.at[b + r], rs.at[b + r], (px, py, c)).start()

    def finish(cin, cout, ss, rs, b):
        x, y, c, chips = _where()
        for r, (px, py) in enumerate(chips):
            got = cout[0].at[2 * px + py]
            _rcopy(got, got, ss.at[b + r], rs.at[b + r], (px, py, c)).wait_recv()
        for r, (px, py) in enumerate(chips):
            mine = cout[0].at[2 * x + y]
            _rcopy(mine, mine, ss.at[b + r], rs.at[b + r], (px, py, c)).wait_send()

    return _Task([buf], [SDS(buf.shape, buf.dtype)], [(0, 0)], 3, start, finish, peers=OTHER_CHIPS)


def _t_sibling(gbf):
    def start(cin, cout, ss, rs, b):
        x, y, c, _ = _where()
        for jj in range(N_CHIPS):
            _rcopy(cin[0].at[2 * jj + (1 - c)], cout[0].at[jj], ss.at[b + jj], rs.at[b + jj], (x, y, 1 - c)).start()

    def finish(cin, cout, ss, rs, b):
        x, y, c, _ = _where()
        for jj in range(N_CHIPS):
            got = cout[0].at[jj]
            _rcopy(got, got, ss.at[b + jj], rs.at[b + jj], (x, y, 1 - c)).wait_recv()
        for jj in range(N_CHIPS):
            got = cout[0].at[jj]
            _rcopy(got, got, ss.at[b + jj], rs.at[b + jj], (x, y, 1 - c)).wait_send()

    return _Task([gbf], [SDS((N_CHIPS,) + gbf.shape[1:], BF)], [], N_CHIPS, start, finish, peers=(SIBLING,))


def _t_chips(pbf):
    def start(cin, cout, ss, rs, b):
        x, y, c, chips = _where()
        for r, (px, py) in enumerate(chips):
            _rcopy(cin[0].at[2 * px + py], cout[0].at[r], ss.at[b + r], rs.at[b + r], (px, py, c)).start()

    def finish(cin, cout, ss, rs, b):
        x, y, c, chips = _where()
        for r, (px, py) in enumerate(chips):
            got = cout[0].at[r]
            _rcopy(got, got, ss.at[b + r], rs.at[b + r], (px, py, c)).wait_recv()
        for r, (px, py) in enumerate(chips):
            got = cout[0].at[r]
            _rcopy(got, got, ss.at[b + r], rs.at[b + r], (px, py, c)).wait_send()

    return _Task([pbf], [SDS((3,) + pbf.shape[1:], BF)], [], 3, start, finish, peers=OTHER_CHIPS)


def _t_swap(fin):
    def start(cin, cout, ss, rs, b):
        x, y, c, _ = _where()
        mine = cout[0].at[c]
        _rcopy(mine, mine, ss.at[b], rs.at[b], (x, y, 1 - c)).start()

    def finish(cin, cout, ss, rs, b):
        x, y, c, _ = _where()
        got = cout[0].at[1 - c]
        _rcopy(got, got, ss.at[b], rs.at[b], (x, y, 1 - c)).wait_recv()
        _rcopy(got, got, ss.at[b], rs.at[b], (x, y, 1 - c)).wait_send()

    return _Task([fin], [SDS(fin.shape, fin.dtype)], [(0, 0)], 1, start, finish, peers=(SIBLING,))


def _t_allgather(buf):
    def peers():
        x, y, c, _ = _where()
        out = []
        for rel in range(1, N_DEV):
            px, py, pc = x ^ ((rel >> 2) & 1), y ^ ((rel >> 1) & 1), c ^ (rel & 1)
            out.append((rel - 1, 4 * px + 2 * py + pc, (px, py, pc)))
        return 4 * x + 2 * y + c, out

    def start(cin, cout, ss, rs, b):
        me, ps = peers()
        mine = cout[0].at[me]
        for k, _, dev in ps:
            _rcopy(mine, mine, ss.at[b + k], rs.at[b + k], dev).start()

    def finish(cin, cout, ss, rs, b):
        me, ps = peers()
        for k, pidx, dev in ps:
            got = cout[0].at[pidx]
            _rcopy(got, got, ss.at[b + k], rs.at[b + k], dev).wait_recv()
        for k, _, dev in ps:
            mine = cout[0].at[me]
            _rcopy(mine, mine, ss.at[b + k], rs.at[b + k], dev).wait_send()

    return _Task([buf], [SDS(buf.shape, buf.dtype)], [(0, 0)], N_DEV - 1, start, finish, peers=EVERYONE)


def _run_tasks(comm, which, cin, cout, ss, rs):
    i0 = o0 = s0 = 0
    for t in comm:
        getattr(t, which)(cin[i0:i0 + len(t.ins)], cout[o0:o0 + len(t.outs)], ss, rs, s0)
        i0, o0, s0 = i0 + len(t.ins), o0 + len(t.outs), s0 + t.n_sem


def _from_hbm(*arrays):
    return [pltpu.with_memory_space_constraint(a, pltpu.HBM) for a in arrays]


def _in_hbm(shapes):
    return [pltpu.HBM(s.shape, s.dtype) for s in shapes]


def _comm_layout(comm, n_in, n_out):
    c_in = [a for t in comm for a in t.ins]
    c_out = [s for t in comm for s in t.outs]
    aliases, i0, o0 = {}, 0, 0
    for t in comm:
        for i, o in t.alias:
            aliases[n_in + i0 + i] = n_out + o0 + o
        i0, o0 = i0 + len(t.ins), o0 + len(t.outs)
    return c_in, c_out, aliases, sum(t.n_sem for t in comm)


def _call(body, operands, *, name, grid, in_specs, out_specs, out_shape, scratch_shapes=(), sem=None, vmem_mib=None, comm=(),
          free=(), prefetch=()):
    operands = [o if s.memory_space == pltpu.SMEM or k in free else pltpu.with_memory_space_constraint(o, pltpu.HBM)
                for k, (o, s) in enumerate(zip(operands, in_specs))]
    n_pre, n_in, n_out, n_scr = len(prefetch), len(in_specs), len(out_specs), len(scratch_shapes)
    c_in, c_out, aliases, n_sem = _comm_layout(comm, n_pre + n_in, n_out)
    sems = [pltpu.SemaphoreType.DMA((n_sem,)), pltpu.SemaphoreType.DMA((n_sem,))] if comm else []

    def wrapped(*refs):
        pre, refs = refs[:n_pre], refs[n_pre:]
        ins, cin = refs[:n_in], refs[n_in:n_in + len(c_in)]
        rest = refs[n_in + len(c_in):]
        outs, cout = rest[:n_out], rest[n_out:n_out + len(c_out)]
        rest = rest[n_out + len(c_out):]
        scr, csem = rest[:n_scr], rest[n_scr:]
        if not comm:
            return body(*pre, *ins, *outs, *scr)
        step = functools.reduce(lambda acc, k: acc * grid[k] + pl.program_id(k), range(len(grid)), 0)
        n_steps = math.prod(grid)

        @pl.when(step == 0)
        def _():
            _enter(comm)
            _run_tasks(comm, "start", cin, cout, *csem)

        pl.when(step == n_steps // 2)(lambda: _run_tasks(comm, "middle", cin, cout, *csem))
        body(*pre, *ins, *outs, *scr)
        pl.when(step == n_steps - 1)(lambda: _run_tasks(comm, "finish", cin, cout, *csem))

    grid_spec = pltpu.PrefetchScalarGridSpec(
        num_scalar_prefetch=n_pre, grid=grid, in_specs=list(in_specs) + [ANY] * len(c_in),
        out_specs=list(out_specs) + [ANY] * len(c_out), scratch_shapes=list(scratch_shapes) + sems)
    return _pcall(
        wrapped, name=name, grid_spec=grid_spec, out_shape=_in_hbm(list(out_shape) + c_out), input_output_aliases=aliases,
        compiler_params=_params(("arbitrary",) * len(grid) if comm else sem, vmem_mib,
                                BARRIER_OF[_peers_of(comm)] if comm else None),
    )(*prefetch, *operands, *_from_hbm(*c_in))


def _comm_call(name, comm):
    c_in, c_out, aliases, n_sem = _comm_layout(comm, 0, 0)

    def body(*refs):
        cin, cout, (ss, rs) = refs[:len(c_in)], refs[len(c_in):len(c_in) + len(c_out)], refs[len(c_in) + len(c_out):]
        _enter(comm)
        for phase in ("start", "middle", "finish"):
            _run_tasks(comm, phase, cin, cout, ss, rs)

    return _pcall(
        body, name=name, in_specs=[ANY] * len(c_in), out_specs=[ANY] * len(c_out), out_shape=_in_hbm(c_out),
        scratch_shapes=[pltpu.SemaphoreType.DMA((n_sem,)), pltpu.SemaphoreType.DMA((n_sem,))],
        input_output_aliases=aliases, compiler_params=_params(collective_id=BARRIER_OF[_peers_of(comm)]),
    )(*_from_hbm(*c_in))


def _inproj(x, g1, w_int, comm=()):
    tm = TM

    def body(x_ref, g_ref, w_ref, proj_ref, u_ref):
        xf = x_ref[...]
        r = lax.rsqrt(jnp.mean(xf * xf, axis=-1, keepdims=True) + EPS)
        u = (xf * r * g_ref[...]).astype(BF)
        u_ref[...] = u
        proj_ref[...] = _dot(u, w_ref[...], 1, 1)

    return _call(
        body, (x, g1, w_int), name="inproj", grid=(T // tm,),
        in_specs=[pl.BlockSpec((tm, D), lambda i: (i, 0)), pl.BlockSpec((1, D), lambda i: (0, 0)),
                  _resident((INW, D))],
        out_specs=[pl.BlockSpec((tm, INW), lambda i: (i, 0)), pl.BlockSpec((tm, D), lambda i: (i, 0))],
        out_shape=[SDS((T, INW), F32), SDS((T, D), BF)], sem=("parallel",), vmem_mib=40, comm=comm, free=(0, 1))


def _outproj(y, w_out, x, g2):
    tm = TM

    def body(y_ref, w_ref, x_ref, g_ref, h1_ref, u2_ref):
        h1 = x_ref[...] + _dot(y_ref[...], w_ref[...], 1, 0)
        h1_ref[...] = h1
        r = lax.rsqrt(jnp.mean(h1 * h1, axis=-1, keepdims=True) + EPS)
        u2_ref[...] = (h1 * r * g_ref[...]).astype(BF)

    return _call(
        body, (y, w_out, x, g2), name="outproj", grid=(T // tm,),
        in_specs=[pl.BlockSpec((tm, D), lambda i: (i, 0)), _resident((D, D)),
                  pl.BlockSpec((tm, D), lambda i: (i, 0)), pl.BlockSpec((1, D), lambda i: (0, 0))],
        out_specs=[pl.BlockSpec((tm, D), lambda i: (i, 0)), pl.BlockSpec((tm, D), lambda i: (i, 0))],
        out_shape=[SDS((T, D), F32), SDS((T, D), BF)], sem=("parallel",), vmem_mib=32, free=(2, 3))


def _ffn_up(u2, w_upt, comm=()):
    tm, tn = T, 512

    def body(u_ref, w_ref, o_ref):
        o_ref[...] = _dot(u_ref[...], w_ref[...], 1, 1).astype(BF)

    return _call(
        body, (u2, w_upt), name="ffn_up", grid=(T // tm, 2 * DFF // tn),
        in_specs=[pl.BlockSpec((tm, D), lambda i, j: (i, 0)), pl.BlockSpec((tn, D), lambda i, j: (j, 0))],
        out_specs=[pl.BlockSpec((tm, tn), lambda i, j: (i, j))], out_shape=[SDS((T, 2 * DFF), BF)],
        sem=("parallel", "parallel"), vmem_mib=32, comm=comm, free=(1,))


def _ffn_down(a, w_down, h1, tgt):
    tm = TM

    def body(a_ref, w_ref, h1_ref, t_ref, dh_ref, dhb_ref, l_ref):
        @pl.when(pl.program_id(0) == 0)
        def _():
            l_ref[...] = jnp.zeros_like(l_ref)

        h2 = h1_ref[...] + _dot(a_ref[...], w_ref[...], 1, 0)
        e = h2 - t_ref[...]
        dh = e * (1.0 / D)
        dh_ref[...] = dh
        dhb_ref[...] = dh.astype(BF)
        e2 = jnp.sum((e * e).reshape(tm // 8, 8, D), axis=0)
        acc = e2[:, 0:128]
        for k in range(1, D // 128):
            acc = acc + e2[:, k * 128:(k + 1) * 128]
        l_ref[...] += acc

    return _call(
        body, (a, w_down, h1, tgt), name="ffn_down", grid=(T // tm,),
        in_specs=[pl.BlockSpec((tm, DFF), lambda i: (i, 0)), _resident((DFF, D)),
                  pl.BlockSpec((tm, D), lambda i: (i, 0)), pl.BlockSpec((tm, D), lambda i: (i, 0))],
        out_specs=[pl.BlockSpec((tm, D), lambda i: (i, 0)), pl.BlockSpec((tm, D), lambda i: (i, 0)),
                   pl.BlockSpec((8, 128), lambda i: (0, 0))],
        out_shape=[SDS((T, D), F32), SDS((T, D), BF), SDS((8, 128), F32)], sem=("arbitrary",), vmem_mib=40, free=(2, 3))


def _bucket_table():
    q = np.arange(BLK, dtype=np.int32)[:, None]
    j = np.arange(2 * BLK, dtype=np.int32)[None, :]
    n = np.maximum(q + BLK - j, 0)
    nf = np.maximum(n, 1).astype(np.float32)
    max_exact = NBUCKET // 2
    large = max_exact + (np.log(nf / np.float32(max_exact)) / np.float32(math.log(BLK / max_exact))
                         * np.float32(NBUCKET - max_exact)).astype(np.int32)
    large = np.minimum(large, NBUCKET - 1)
    return np.where(n < max_exact, n, large).astype(np.int32)


def _band_bias_bwd(dbias, bucket, me):
    def body(me_ref, db_ref, bk_ref, o_ref):
        bk = bk_ref[...]
        for b in range(NBUCKET):
            m = bk == b
            for h in range(NH):
                v = jnp.where(m, db_ref[h * BLK:(h + 1) * BLK, :], 0.0)
                s = jnp.sum(jnp.sum(v, axis=1, keepdims=True), axis=0, keepdims=True)
                o_ref[0, h:h + 1, b:b + 1] = s

    grid_spec = pltpu.PrefetchScalarGridSpec(
        num_scalar_prefetch=1, grid=(1,),
        in_specs=[pl.BlockSpec((NH * BLK, 2 * BLK), lambda i, me_ref: (0, 0)),
                  pl.BlockSpec((BLK, 2 * BLK), lambda i, me_ref: (0, 0))],
        out_specs=pl.BlockSpec((1, NH, NBUCKET), lambda i, me_ref: (me_ref[0], 0, 0)),
    )
    return _pcall(body, name="band_bias_bwd", grid_spec=grid_spec, out_shape=SDS((N_DEV, NH, NBUCKET), F32),
                  compiler_params=_params(("arbitrary",)))(me, dbias, bucket)


def _two_bf16(x):
    hi = x.astype(BF)
    return hi, (x - hi.astype(F32)).astype(BF)


def _head_sums(x, seg):
    hi, lo = _two_bf16(x)
    s = seg[0:x.shape[1], :]
    return _dot(hi, s, 1, 0) + _dot(lo, s, 1, 0)


def _head_spread(v, seg, width):
    hi, lo = _two_bf16(v)
    s = seg[0:width, :]
    return _dot(hi, s, 1, 1) + _dot(lo, s, 1, 1)


def _head_norm(x, g_t, seg, by_head=False):
    if by_head:
        heads = [x[:, h * HD:(h + 1) * HD] for h in range(x.shape[1] // HD)]
        r = jnp.concatenate([jnp.broadcast_to(lax.rsqrt(jnp.mean(v * v, axis=-1, keepdims=True) + EPS), v.shape)
                             for v in heads], axis=1)
    else:
        r = lax.rsqrt(_head_sums(x * x, seg) * (1.0 / HD) + EPS)
        r = _head_spread(r, seg, x.shape[1])
    return x * r * g_t, r


def _head_norm_bwd(dy, x, r, g_t, seg):
    dg_t = jnp.sum(dy * (x * r), axis=0, keepdims=True)
    dgx = dy * g_t
    mean = _head_spread(_head_sums(x * dgx, seg) * (1.0 / HD), seg, x.shape[1])
    return r * dgx - x * (r * r * r) * mean, dg_t


def _fold_heads(v):
    out = v[:, 0:HD]
    for h in range(1, v.shape[1] // HD):
        out = out + v[:, h * HD:(h + 1) * HD]
    return out


def _mix_forward(P, zc8, zh8, pkv, first, cw, qg_t, kg_t, gco, gao, seg, sink_ref, bias_ref, by_head=False):
    gate_b = P[:, 0:CW]
    gate_c = P[:, CW:2 * CW]
    hc = P[:, 2 * CW:3 * CW]
    z = gate_c * hc
    keep = jnp.where(first, 0.0, 1.0)
    zp = zc8 * zh8 * keep
    p1 = zp[7:8, :]
    p2 = zp[6:7, :]
    row = lax.broadcasted_iota(jnp.int32, (BLK, 1), 0)
    z1 = jnp.where(row == 0, p1, pltpu.roll(z, 1, 0))
    z2 = jnp.where(row == 0, p2, jnp.where(row == 1, p1, pltpu.roll(z, 2, 0)))
    cz = cw[0:1, :] * z2 + cw[1:2, :] * z1 + cw[2:3, :] * z
    y_conv = gate_b * cz

    scale = HD ** -0.5
    qi = lax.broadcasted_iota(jnp.int32, (BLK, 2 * BLK), 0)
    kj = lax.broadcasted_iota(jnp.int32, (BLK, 2 * BLK), 1)
    dd = qi + BLK - kj
    first_key = jnp.where(first, BLK, 0)
    valid = (dd >= 0) & (dd < BLK) & (kj >= first_key)

    q0 = 3 * CW
    k0 = q0 + AW
    v0 = k0 + NKV * HD
    q_raw = P[:, q0:k0]
    qn, rq = _head_norm(q_raw, qg_t, seg, by_head)
    qs = (qn * scale).astype(BF)
    k_raw = jnp.concatenate([pkv[:, 0:NKV * HD], P[:, k0:v0]], axis=0)
    kn, rk = _head_norm(k_raw, kg_t, seg, by_head)
    knb = kn.astype(BF)
    heads = []
    for h in range(NH):
        kv = h // GQ
        kb = knb[:, kv * HD:(kv + 1) * HD]
        vb = jnp.concatenate([pkv[:, NKV * HD + kv * HD:NKV * HD + (kv + 1) * HD],
                              P[:, v0 + kv * HD:v0 + (kv + 1) * HD]], axis=0).astype(BF)
        Q = qs[:, h * HD:(h + 1) * HD]
        S = _dot(Q, kb, 1, 1) + bias_ref[h * BLK:(h + 1) * BLK, :]
        S = jnp.where(valid, S, NEG_INF)
        sink = sink_ref[0, h]
        m = jnp.maximum(jnp.max(S, axis=-1, keepdims=True), sink)
        p = jnp.exp(S - m)
        es = jnp.exp(sink - m)
        denom = jnp.sum(p, axis=-1, keepdims=True) + es
        probs = p / denom
        O = _dot(probs.astype(BF), vb, 1, 0)
        heads.append(dict(kb=kb, vb=vb, Q=Q, probs=probs, psink=es / denom, O=O))
    y_attn = jnp.concatenate([hd["O"] for hd in heads], axis=1)

    rc = lax.rsqrt(jnp.mean(y_conv * y_conv, axis=-1, keepdims=True) + EPS)
    ra = lax.rsqrt(jnp.mean(y_attn * y_attn, axis=-1, keepdims=True) + EPS)
    y = jnp.concatenate([y_conv * rc * gco, y_attn * ra * gao], axis=1)
    return dict(gate_b=gate_b, gate_c=gate_c, hc=hc, z=z, z1=z1, z2=z2, cz=cz, y_conv=y_conv, y_attn=y_attn,
                rc=rc, ra=ra, heads=heads, y=y, row=row, scale=scale, q_raw=q_raw, rq=rq, k_raw=k_raw, rk=rk)


BPS = 2
TILE = BPS * BLK
KV0 = 3 * CW + AW


def _mix_in_specs(tile_of):
    return [
        pl.BlockSpec(memory_space=pltpu.SMEM),
        pl.BlockSpec((TILE, INW), lambda s: (tile_of(s), 0)),
        pl.BlockSpec((8, CW), lambda s: (jnp.maximum(tile_of(s) * (TILE // 8) - 1, 0), 1)),
        pl.BlockSpec((8, CW), lambda s: (jnp.maximum(tile_of(s) * (TILE // 8) - 1, 0), 2)),
        pl.BlockSpec((BLK, 2 * NKV * HD), lambda s: (jnp.maximum(tile_of(s) * BPS - 1, 0), KV0 // (2 * NKV * HD))),
    ]


def _block_inputs(tile, b, zc_ref, zh_ref, pkv_ref, first_tile):
    P = tile[b * BLK:(b + 1) * BLK, :]
    if b == 0:
        return P, zc_ref[...], zh_ref[...], pkv_ref[...], first_tile
    lo = b * BLK
    return P, tile[lo - 8:lo, CW:2 * CW], tile[lo - 8:lo, 2 * CW:3 * CW], tile[lo - BLK:lo, KV0:KV0 + 2 * NKV * HD], False


def _mix_param_specs():
    return [
        pl.BlockSpec((8, CW), lambda s: (0, 0)),
        pl.BlockSpec((1, AW), lambda s: (0, 0)),
        pl.BlockSpec((1, NKV * HD), lambda s: (0, 0)),
        pl.BlockSpec((1, CW), lambda s: (0, 0)),
        pl.BlockSpec((1, AW), lambda s: (0, 0)),
        pl.BlockSpec((AW, 128), lambda s: (0, 0)),
        pl.BlockSpec((NH * BLK, 2 * BLK), lambda s: (0, 0)),
    ]


def _mix_params(cw8, qg, kg, gco, gao, bias):
    seg = np.zeros((AW, 128), np.float32)
    seg[np.arange(AW), np.arange(AW) // HD] = 1.0
    return (cw8, jnp.tile(qg, (1, NH)), jnp.tile(kg, (1, NKV)), gco, gao, jnp.asarray(seg, BF), bias)


def _mix_fwd(proj, sinks, cw8, qg, kg, gco, gao, bias, comm=()):
    def body(sink_ref, p_ref, zc_ref, zh_ref, pkv_ref, cw_ref, qg_ref, kg_ref, gco_ref, gao_ref, seg_ref, bias_ref, y_ref):
        tile = p_ref[...]
        for b in range(BPS):
            f = _mix_forward(*_block_inputs(tile, b, zc_ref, zh_ref, pkv_ref, pl.program_id(0) == 0), cw_ref[...],
                             qg_ref[...], kg_ref[...], gco_ref[...], gao_ref[...], seg_ref[...], sink_ref, bias_ref, by_head=True)
            y_ref[b * BLK:(b + 1) * BLK, :] = f["y"].astype(BF)

    return _call(
        body, (sinks, proj, proj, proj, proj, *_mix_params(cw8, qg, kg, gco, gao, bias)), name="mix_fwd", grid=(T // TILE,),
        in_specs=_mix_in_specs(lambda s: s) + _mix_param_specs(),
        out_specs=[pl.BlockSpec((TILE, D), lambda s: (s, 0))], out_shape=[SDS((T, D), BF)],
        sem=("parallel",), vmem_mib=40, comm=comm, free=tuple(range(5, 12)))


def _mix_bwd(proj, dy, sinks, cw8, qg, kg, gco, gao, bias, comm=()):
    n_steps = T // TILE

    def tile_of(s):
        return n_steps - 1 - s

    def body(sink_ref, p_ref, zc_ref, zh_ref, pkv_ref, dy_ref, cw_ref, qg_ref, kg_ref, gco_ref, gao_ref, seg_ref, bias_ref,
             dproj_ref, dcw_ref, dqg_ref, dkg_ref, dgco_ref, dgao_ref, dsink_ref, dbias_ref,
             ndcz_ref, dkc_ref, dvc_ref):
        s = pl.program_id(0)

        @pl.when(s == 0)
        def _():
            for r in (dcw_ref, dqg_ref, dkg_ref, dgco_ref, dgao_ref, dsink_ref, dbias_ref, ndcz_ref, dkc_ref, dvc_ref):
                r[...] = jnp.zeros_like(r)

        params = (cw_ref[...], qg_ref[...], kg_ref[...], gco_ref[...], gao_ref[...], seg_ref[...])
        tile = p_ref[...]
        carry = (ndcz_ref[...], dkc_ref[...], dvc_ref[...])
        total = None
        for b in reversed(range(BPS)):
            f = _mix_forward(*_block_inputs(tile, b, zc_ref, zh_ref, pkv_ref, s == n_steps - 1), *params, sink_ref, bias_ref)
            pieces, sums, carry = one_block(f, dy_ref[b * BLK:(b + 1) * BLK, :], params, carry)
            for lo, piece in pieces:
                dproj_ref[b * BLK:(b + 1) * BLK, lo:lo + piece.shape[1]] = piece
            total = sums if total is None else [t + v for t, v in zip(total, sums)]
        ndcz_ref[...], dkc_ref[...], dvc_ref[...] = carry
        dcw, dqg_t, dkg_t, dgco, dgao, dsink, *ds = total
        dcw_ref[0:3, :] += dcw
        dqg_ref[...] += _fold_heads(dqg_t)
        dkg_ref[...] += _fold_heads(dkg_t)
        dgco_ref[...] += dgco
        dgao_ref[...] += dgao
        dsink_ref[...] += dsink
        for h in range(NH):
            dbias_ref[h * BLK:(h + 1) * BLK, :] += ds[h]

    def one_block(f, dy, params, carry):
        cw, qg_v, kg_v, gco_v, gao_v, seg = params
        nxt, dk_carry, dv_carry = carry
        dyc, dgco = _rms_bwd(dy[:, 0:CW], f["y_conv"], f["rc"], gco_v)
        dya, dgao = _rms_bwd(dy[:, CW:CW + AW], f["y_attn"], f["ra"], gao_v)

        row = f["row"]
        dgate_b = dyc * f["cz"]
        dcz = dyc * f["gate_b"]
        dcw = jnp.concatenate([jnp.sum(dcz * f[k], axis=0, keepdims=True) for k in ("z2", "z1", "z")], axis=0)
        n0 = nxt[0:1, :]
        n1 = nxt[1:2, :]
        d1 = jnp.where(row == BLK - 1, n0, pltpu.roll(dcz, BLK - 1, 0))
        d2 = jnp.where(row == BLK - 1, n1, jnp.where(row == BLK - 2, n0, pltpu.roll(dcz, BLK - 2, 0)))
        dz = cw[2:3, :] * dcz + cw[1:2, :] * d1 + cw[0:1, :] * d2
        pieces = [(0, dgate_b.astype(BF)), (CW, (dz * f["hc"]).astype(BF)), (2 * CW, (dz * f["gate_c"]).astype(BF))]

        scale = f["scale"]
        lane = lax.broadcasted_iota(jnp.int32, (1, 128), 1)
        dsink = jnp.zeros((1, 128), F32)
        dq_cols, dk_cols, dv_cols, dk_prev, dv_prev, ds = [], [], [], [], [], []
        for kv in range(NKV):
            dKb = dVb = 0.0
            for h in range(kv * GQ, (kv + 1) * GQ):
                hd = f["heads"][h]
                dO = dya[:, h * HD:(h + 1) * HD]
                delta = jnp.sum(dO * hd["O"], axis=-1, keepdims=True)
                dOb = dO.astype(BF)
                dP = _dot(dOb, hd["vb"], 1, 1)
                dS = hd["probs"] * (dP - delta)
                tot = jnp.sum(hd["psink"] * delta, axis=0, keepdims=True)
                dsink = dsink - jnp.where(lane == h, tot, 0.0)
                ds.append(dS)
                dSb = dS.astype(BF)
                dq_cols.append(_dot(dSb, hd["kb"], 1, 0))
                dKb = dKb + _dot(dSb, hd["Q"], 0, 0)
                dVb = dVb + _dot(hd["probs"].astype(BF), dOb, 0, 0)
            dk_cols.append(dKb[BLK:, :] + dk_carry[:, kv * HD:(kv + 1) * HD])
            dv_cols.append(dVb[BLK:, :] + dv_carry[:, kv * HD:(kv + 1) * HD])
            dk_prev.append(dKb[:BLK, :])
            dv_prev.append(dVb[:BLK, :])
        dq_raw, dqg_t = _head_norm_bwd(jnp.concatenate(dq_cols, axis=1) * scale, f["q_raw"], f["rq"], qg_v, seg)
        dk_raw, dkg_t = _head_norm_bwd(jnp.concatenate(dk_cols, axis=1), f["k_raw"][BLK:, :], f["rk"][BLK:, :], kg_v, seg)
        pieces.append((3 * CW, jnp.concatenate([dq_raw, dk_raw] + dv_cols, axis=1).astype(BF)))
        owed = (dcz[0:8, :], jnp.concatenate(dk_prev, axis=1), jnp.concatenate(dv_prev, axis=1))
        return pieces, [dcw, dqg_t, dkg_t, dgco, dgao, dsink, *ds], owed

    small = lambda r, c: pl.BlockSpec((r, c), lambda s: (0, 0))
    return _call(
        body, (sinks, proj, proj, proj, proj, dy, *_mix_params(cw8, qg, kg, gco, gao, bias)), name="mix_bwd", grid=(n_steps,),
        in_specs=_mix_in_specs(tile_of) + [pl.BlockSpec((TILE, D), lambda s: (tile_of(s), 0))] + _mix_param_specs(),
        out_specs=[pl.BlockSpec((TILE, INW), lambda s: (tile_of(s), 0)), small(8, CW), small(1, HD), small(1, HD),
                   small(1, CW), small(1, AW), small(1, 128), small(NH * BLK, 2 * BLK)],
        out_shape=[SDS((T, INW), BF), SDS((8, CW), F32), SDS((1, HD), F32), SDS((1, HD), F32), SDS((1, CW), F32),
                   SDS((1, AW), F32), SDS((1, 128), F32), SDS((NH * BLK, 2 * BLK), F32)],
        scratch_shapes=[pltpu.VMEM((8, CW), F32), pltpu.VMEM((BLK, NKV * HD), F32), pltpu.VMEM((BLK, NKV * HD), F32)],
        sem=("arbitrary",), vmem_mib=56, comm=comm, free=(1, 2, 3, 4) + tuple(range(6, 13)))


FT = 256
NFT = DFF // FT
RC = 1024
NCH = T // RC
LEAD = 16


def _rows8(x):
    return jnp.sum(x.reshape(x.shape[0] // 8, 8, x.shape[1]), axis=0)


def _ffn_act_specs():
    return [
        pl.BlockSpec((T, FT), lambda j: (0, j)), pl.BlockSpec((T, FT), lambda j: (0, NFT + j)),
        pl.BlockSpec((8, FT), lambda j: (0, j)), pl.BlockSpec((8, FT), lambda j: (0, NFT + j)),
        pl.BlockSpec((1, FT), lambda j: (0, j)), pl.BlockSpec((1, FT), lambda j: (0, NFT + j)),
    ]


def _conv_rows(win, w, b, n):
    win = win.astype(F32)
    u = win[LEAD:LEAD + n]
    u1 = pltpu.roll(win, 1, 0)[LEAD:LEAD + n]
    u2 = pltpu.roll(win, 2, 0)[LEAD:LEAD + n]
    return u2, u1, u, w[0:1, :] * u2 + w[1:2, :] * u1 + w[2:3, :] * u + b


def _ffn_act(up, fw8, fb, comm=()):
    def body(ug_ref, uv_ref, wg_ref, wv_ref, bg_ref, bv_ref, a_ref):
        wg, wv, bg, bv = wg_ref[...], wv_ref[...], bg_ref[...], bv_ref[...]

        def chunk(win_g, win_v):
            gp = _conv_rows(win_g, wg, bg, RC)[3]
            vp = _conv_rows(win_v, wv, bv, RC)[3]
            return (gp * jax.nn.sigmoid(gp) * vp).astype(BF)

        zero = jnp.zeros((LEAD, FT), BF)
        a_ref[0:RC, :] = chunk(jnp.concatenate([zero, ug_ref[0:RC, :]], axis=0),
                               jnp.concatenate([zero, uv_ref[0:RC, :]], axis=0))

        def step(i, carry):
            r0 = pl.multiple_of(i * RC, RC)
            win = pl.ds(r0 - LEAD, RC + LEAD)
            a_ref[pl.ds(r0, RC), :] = chunk(ug_ref[win, :], uv_ref[win, :])
            return carry

        lax.fori_loop(1, NCH, step, 0)

    return _call(
        body, (up, up, fw8, fw8, fb, fb), name="ffn_act", grid=(NFT,), in_specs=_ffn_act_specs(),
        out_specs=[pl.BlockSpec((T, FT), lambda j: (0, j))], out_shape=[SDS((T, DFF), BF)],
        sem=("parallel",), vmem_mib=40, comm=comm, free=(2, 3, 4, 5))


def _ffn_act_bwd(up, da, fw8, fb, comm=()):
    ext = RC + LEAD

    def body(ug_ref, uv_ref, wg_ref, wv_ref, bg_ref, bv_ref, da_ref,
             dug_ref, duv_ref, dwg_ref, dwv_ref, dbg_ref, dbv_ref):
        wg, wv, bg, bv = wg_ref[...], wv_ref[...], bg_ref[...], bv_ref[...]

        def chunk(win_g, win_v, da_e):
            g2, g1, g0, gp = _conv_rows(win_g, wg, bg, ext)
            v2, v1, v0, vp = _conv_rows(win_v, wv, bv, ext)
            da_e = da_e.astype(F32)
            sig = jax.nn.sigmoid(gp)
            dvp = da_e * (gp * sig)
            dgp = da_e * vp * (sig * (1.0 + gp * (1.0 - sig)))

            def back(dp, w):
                return (w[2:3, :] * dp[0:RC] + w[1:2, :] * pltpu.roll(dp, ext - 1, 0)[0:RC]
                        + w[0:1, :] * pltpu.roll(dp, ext - 2, 0)[0:RC]).astype(BF)

            def sums(dp, u2, u1, u0):
                d = dp[0:RC]
                return [_rows8(d), _rows8(d * u2[0:RC]), _rows8(d * u1[0:RC]), _rows8(d * u0[0:RC])]

            return back(dgp, wg), back(dvp, wv), sums(dgp, g2, g1, g0) + sums(dvp, v2, v1, v0)

        zero = jnp.zeros((LEAD, FT), BF)
        dug, duv, acc = chunk(jnp.concatenate([zero, ug_ref[0:ext, :]], axis=0),
                              jnp.concatenate([zero, uv_ref[0:ext, :]], axis=0), da_ref[0:ext, :])
        dug_ref[0:RC, :] = dug
        duv_ref[0:RC, :] = duv

        def step(i, acc):
            r0 = pl.multiple_of(i * RC, RC)
            win = pl.ds(r0 - LEAD, ext + LEAD)
            dug, duv, part = chunk(ug_ref[win, :], uv_ref[win, :], da_ref[pl.ds(r0, ext), :])
            dug_ref[pl.ds(r0, RC), :] = dug
            duv_ref[pl.ds(r0, RC), :] = duv
            return [a + p for a, p in zip(acc, part)]

        acc = lax.fori_loop(1, NCH - 1, step, acc)
        r0 = T - RC
        tail = lambda ref, lo: jnp.concatenate([ref[lo:T, :], zero], axis=0)
        dug, duv, part = chunk(tail(ug_ref, r0 - LEAD), tail(uv_ref, r0 - LEAD), tail(da_ref, r0))
        dug_ref[r0:T, :] = dug
        duv_ref[r0:T, :] = duv
        tot = [jnp.sum(a + p, axis=0, keepdims=True) for a, p in zip(acc, part)]
        for k, (dw_ref, db_ref) in enumerate(((dwg_ref, dbg_ref), (dwv_ref, dbv_ref))):
            db_ref[...] = tot[4 * k]
            dw_ref[...] = jnp.zeros_like(dw_ref)
            for r in range(3):
                dw_ref[r:r + 1, :] = tot[4 * k + 1 + r]

    col = lambda r: pl.BlockSpec((r, FT), lambda j: (0, j))
    return _call(
        body, (up, up, fw8, fw8, fb, fb, da), name="ffn_act_bwd", grid=(NFT,),
        in_specs=_ffn_act_specs() + [pl.BlockSpec((T, FT), lambda j: (0, j))],
        out_specs=[col(T), col(T), col(8), col(8), col(1), col(1)],
        out_shape=[SDS((T, DFF), BF), SDS((T, DFF), BF), SDS((8, DFF), F32), SDS((8, DFF), F32),
                   SDS((1, DFF), F32), SDS((1, DFF), F32)],
        sem=("parallel",), vmem_mib=40, comm=comm, free=(0, 1, 2, 3, 4, 5))


def _ffn_down_bwd(dh2b, w_down, comm=()):
    tm = TM

    def body(d_ref, w_ref, o_ref):
        o_ref[...] = _dot(d_ref[...], w_ref[...], 1, 1).astype(BF)

    return _call(
        body, (dh2b, w_down), name="ffn_down_bwd", grid=(T // tm,),
        in_specs=[pl.BlockSpec((tm, D), lambda i: (i, 0)), _resident((DFF, D))],
        out_specs=[pl.BlockSpec((tm, DFF), lambda i: (i, 0))], out_shape=[SDS((T, DFF), BF)],
        sem=("parallel",), vmem_mib=40, comm=comm, free=(0, 1))


def _norm_matmul_bwd(name, a_list, w_t, k_offsets, xin, g, dres, want_bf16, comm=(), slot=None):
    tm = TM
    ks = [a.shape[1] for a in a_list]
    n_a = len(a_list)
    n_pre = 0 if slot is None else 1

    def body(*refs):
        refs = refs[n_pre:]
        a_refs = refs[:n_a]
        w_ref, x_ref, g_ref, r_ref = refs[n_a:n_a + 4]
        outs = refs[n_a + 4:]
        dx_ref, dg_ref = outs[0], (outs[-1] if slot is None else outs[-1].at[0])

        @pl.when(pl.program_id(0) == 0)
        def _():
            dg_ref[...] = jnp.zeros_like(dg_ref)

        du = _dot(a_refs[0][...], w_ref[k_offsets[0]:k_offsets[0] + ks[0], :], 1, 0)
        for k in range(1, n_a):
            du = du + _dot(a_refs[k][...], w_ref[k_offsets[k]:k_offsets[k] + ks[k], :], 1, 0)
        x = x_ref[...]
        r = lax.rsqrt(jnp.mean(x * x, axis=-1, keepdims=True) + EPS)
        dx, dg = _rms_bwd(du, x, r, g_ref[...])
        dx = r_ref[...] + dx
        dx_ref[...] = dx
        if want_bf16:
            outs[1][...] = dx.astype(BF)
        dg_ref[...] += dg

    tile = lambda c: pl.BlockSpec((tm, c), lambda i, *_: (i, 0))
    if slot is None:
        dg_spec, dg_shape = pl.BlockSpec((1, D), lambda i: (0, 0)), SDS((1, D), F32)
    else:
        dg_spec, dg_shape = pl.BlockSpec((1, 1, D), lambda i, slot_ref: (slot_ref[0], 0, 0)), SDS((N_DEV, 1, D), F32)
    out_specs = [tile(D)] + ([tile(D)] if want_bf16 else []) + [dg_spec]
    out_shape = [SDS((T, D), F32)] + ([SDS((T, D), BF)] if want_bf16 else []) + [dg_shape]
    return _call(
        body, (*a_list, w_t, xin, g, dres), name=name, grid=(T // tm,), prefetch=() if slot is None else (slot,),
        in_specs=[tile(k) for k in ks] + [_resident(w_t.shape), tile(D),
                                           pl.BlockSpec((1, D), lambda i, *_: (0, 0)), tile(D)],
        out_specs=out_specs, out_shape=out_shape, sem=("arbitrary",), vmem_mib=56, comm=comm, free=tuple(range(n_a + 4)))


def _out_bwd(dh1b, w_out, comm=()):
    tm = TM

    def body(d_ref, w_ref, o_ref):
        o_ref[...] = _dot(d_ref[...], w_ref[...], 1, 1)

    return _call(
        body, (dh1b, w_out), name="out_bwd", grid=(T // tm,),
        in_specs=[pl.BlockSpec((tm, D), lambda i: (i, 0)), _resident((D, D))],
        out_specs=[pl.BlockSpec((tm, D), lambda i: (i, 0))], out_shape=[SDS((T, D), F32)],
        sem=("parallel",), vmem_mib=32, comm=comm, free=(0, 1))


def _wgrad(name, a_list, b, old_a, comm=()):
    m_k = a_list[0].shape[1]
    tm = max(t for t in range(128, m_k // 2 + 1, 128) if m_k % t == 0)
    steps = [a.shape[1] // tm for a in a_list]
    starts = [sum(steps[:k]) for k in range(len(a_list))]
    n_a = len(a_list)

    def body(*refs):
        a_refs, b_ref, o_ref = refs[:n_a], refs[n_a], refs[n_a + 1]
        i = pl.program_id(0)
        for k in range(n_a):
            @pl.when((i >= starts[k]) & (i < starts[k] + steps[k]))
            def _(k=k):
                o_ref[...] = _dot(a_refs[k][...], b_ref[...], 0, 0).astype(BF)

    def a_spec(k):
        return pl.BlockSpec((T, tm), lambda i: (0, jnp.clip(i - starts[k], 0, steps[k] - 1)))

    m_total = tm * sum(steps)
    return _call(
        body, (*a_list, b), name=name, grid=(sum(steps),),
        in_specs=[a_spec(k) for k in range(n_a)] + [_resident((T, D))],
        out_specs=[pl.BlockSpec((tm, D), lambda i: (i, 0))], out_shape=[SDS((m_total, D), BF)],
        sem=("parallel",), vmem_mib=40, comm=comm, free=() if old_a is None else tuple(range(n_a)) if old_a else (n_a,))


def _chip_sum(name, gbf, from_sib, core, chip):
    h = gbf.shape[1]
    th = h

    def body(core_ref, chip_ref, g_ref, s_ref, pbf_ref, own_ref):
        p = g_ref[0].astype(F32) + s_ref[0].astype(F32)
        pbf_ref[0] = p.astype(BF)

        @pl.when(pl.program_id(1) == chip_ref[0])
        def _():
            own_ref[...] = p

    grid_spec = pltpu.PrefetchScalarGridSpec(
        num_scalar_prefetch=2, grid=(h // th, N_CHIPS),
        in_specs=[pl.BlockSpec((1, th, D), lambda t, jj, core_ref, chip_ref: (2 * jj + core_ref[0], t, 0)),
                  pl.BlockSpec((1, th, D), lambda t, jj, core_ref, chip_ref: (jj, t, 0))],
        out_specs=[pl.BlockSpec((1, th, D), lambda t, jj, core_ref, chip_ref: (jj, t, 0)),
                   pl.BlockSpec((th, D), lambda t, jj, core_ref, chip_ref: (t, 0))],
    )
    return _pcall(
        body, name=name, grid_spec=grid_spec, out_shape=_in_hbm([SDS((N_CHIPS, h, D), BF), SDS((h, D), F32)]),
        compiler_params=_params(("arbitrary", "arbitrary"), 32),
    )(core, chip, *_from_hbm(gbf, from_sib))


def _final_sum(name, own, from_chips, core, comm=()):
    h = own.shape[0]
    n = 4 if h % (4 * ROWS16) == 0 else 2
    th = h // n

    def body(core_ref, o_ref, r_ref, f_ref):
        f_ref[0] = ((o_ref[...] + r_ref[0].astype(F32)) + r_ref[1].astype(F32)) + r_ref[2].astype(F32)

    return _call(
        body, (own, from_chips), name=name, grid=(n,), prefetch=(core,),
        in_specs=[pl.BlockSpec((th, D), lambda i, core_ref: (i, 0)), pl.BlockSpec((3, th, D), lambda i, core_ref: (0, i, 0))],
        out_specs=[pl.BlockSpec((1, th, D), lambda i, core_ref: (core_ref[0], i, 0))], out_shape=[SDS((2, h, D), F32)],
        sem=("arbitrary",), vmem_mib=40, comm=comm)


def _adam_math(w, g, m, v):
    nm = ADAM_B1 * m + (1.0 - ADAM_B1) * g
    nv = ADAM_B2 * v + (1.0 - ADAM_B2) * (g * g)
    m_hat = nm / (1.0 - ADAM_B1 ** ADAM_STEP)
    v_hat = nv / (1.0 - ADAM_B2 ** ADAM_STEP)
    return -ADAM_LR * (m_hat / (jnp.sqrt(v_hat) + ADAM_EPS) + ADAM_WD * w), nm, nv


def _adamw(name, w, g, m, v, tr, copy_g=False, stage=True, g_transposed=False):
    rows, cols = w.shape

    def body(w_ref, g_ref, m_ref, v_ref, *outs):
        d_ref, nm_ref, nv_ref = outs[-3:]
        for c in [pl.ds(c0, 128) for c0 in range(0, cols, 128)] if g_transposed else [slice(None)]:
            g_val = g_ref[c, :].T if g_transposed else g_ref[...]
            if copy_g:
                outs[0][:, c] = g_val
            d_ref[:, c], nm_ref[:, c], nv_ref[:, c] = _adam_math(w_ref[:, c], g_val, m_ref[:, c], v_ref[:, c])

    spec = pl.BlockSpec((tr, cols), lambda i: (i, 0))
    n_out = 4 if copy_g else 3
    g_spec = pl.BlockSpec((cols, tr), lambda i: (0, i)) if g_transposed else spec
    return _call(body, (w, g, m, v), name=name, grid=(rows // tr,), in_specs=[spec, g_spec, spec, spec], out_specs=[spec] * n_out,
                 out_shape=[SDS((rows, cols), F32)] * n_out, sem=("parallel",), vmem_mib=32,
                 free=(0, 2, 3) if stage else ())


C_SQ = 2 * DFF
P_W = C_SQ + 128
R_G2, R_GO, R_DCW, R_QK = 0, 1, 2, 5
C_GCO, C_GAO, C_DQG, C_DKG, C_SINK = 0, CW, 0, 128, 256


def _pack(name, me, ins, width, fill):
    def body(me_ref, *refs):
        o = refs[-1]
        o[...] = jnp.zeros_like(o)
        fill(o, *refs[:-1])

    return _call(body, ins, name=name, grid=(1,), prefetch=(me,),
                 in_specs=[pl.BlockSpec(a.shape, lambda i, me_ref: (0, 0)) for a in ins],
                 out_specs=[pl.BlockSpec((1, 8, width), lambda i, me_ref: (me_ref[0], 0, 0))],
                 out_shape=[SDS((N_DEV, 8, width), F32)], sem=("arbitrary",))[0]


def _pack_ffn(me, dfwg, dfwv, dfbg, dfbv, sq):
    def fill(o, dfwg_r, dfwv_r, dfbg_r, dfbv_r, sq_r):
        o[0, :, 0:DFF] = dfwg_r[...]
        o[0, :, DFF:2 * DFF] = dfwv_r[...]
        o[0, 3:4, 0:DFF] = dfbg_r[...]
        o[0, 3:4, DFF:2 * DFF] = dfbv_r[...]
        o[0, :, C_SQ:C_SQ + 128] = sq_r[...]

    return _pack("pack_ffn", me, (dfwg, dfwv, dfbg, dfbv, sq), P_W, fill)


def _pack_mix(me, dg2, dgco, dgao, dcw8, dqg, dkg, dsink):
    def fill(o, dg2_r, dgco_r, dgao_r, dcw_r, dqg_r, dkg_r, dsink_r):
        o[0, R_G2:R_G2 + 1, :] = dg2_r[...]
        o[0, R_GO:R_GO + 1, C_GCO:C_GCO + CW] = dgco_r[...]
        o[0, R_GO:R_GO + 1, C_GAO:C_GAO + AW] = dgao_r[...]
        o[0, R_DCW:R_DCW + 3, 0:CW] = dcw_r[0:3, :]
        o[0, R_QK:R_QK + 1, C_DQG:C_DQG + HD] = dqg_r[...]
        o[0, R_QK:R_QK + 1, C_DKG:C_DKG + HD] = dkg_r[...]
        o[0, R_QK:R_QK + 1, C_SINK:C_SINK + 128] = dsink_r[...]

    return _pack("pack_mix", me, (dg2, dgco, dgao, dcw8, dqg, dkg, dsink), D, fill)


N_SMALL = 11


def _small_adam(chip, p_all, pm_all, g1_all, tbl_all, ws, ms, vs):
    fw_cols = 2 * DFF // N_CHIPS
    cw_cols = CW // N_CHIPS

    def body(chip_ref, p_ref, fw_ref, pm_ref, cw_ref, g1_ref, tbl_ref, *refs):
        w_r, m_r, v_r = refs[0:N_SMALL], refs[N_SMALL:2 * N_SMALL], refs[2 * N_SMALL:3 * N_SMALL]
        outs = refs[3 * N_SMALL:]
        g_o, d_o, nm_o, nv_o = (outs[k * N_SMALL:(k + 1) * N_SMALL] for k in range(4))
        loss_o = outs[4 * N_SMALL]

        def total(ref):
            s = ref[0]
            for k in range(1, N_DEV):
                s = s + ref[k]
            return s

        S = total(p_ref)
        fw = total(fw_ref)
        M = total(pm_ref)
        cw = total(cw_ref)

        def step(i, g, at):
            d, nm, nv = _adam_math(w_r[i][at], g, m_r[i][at], v_r[i][at])
            g_o[i][at], d_o[i][at], nm_o[i][at], nv_o[i][at] = g, d, nm, nv

        everything = (slice(None), slice(None))
        step(0, total(g1_ref), everything)
        for r in range(3):
            step(1, cw[R_DCW + r:R_DCW + r + 1, :], (r, slice(None), slice(None)))
        step(2, M[R_QK:R_QK + 1, C_DQG:C_DQG + HD], everything)
        step(3, M[R_QK:R_QK + 1, C_DKG:C_DKG + HD], everything)
        step(4, total(tbl_ref), everything)
        step(5, M[R_QK:R_QK + 1, C_SINK:C_SINK + NH], everything)
        step(6, M[R_GO:R_GO + 1, C_GCO:C_GCO + CW], everything)
        step(7, M[R_GO:R_GO + 1, C_GAO:C_GAO + AW], everything)
        step(8, M[R_G2:R_G2 + 1, :], everything)
        for r in range(3):
            step(9, fw[r:r + 1, :], (r, slice(None), slice(None)))
        step(10, S[3:4, 0:2 * DFF], everything)
        sq = S[:, C_SQ:C_SQ + 128]
        loss_o[...] = jnp.sum(jnp.sum(sq, axis=1, keepdims=True), axis=0, keepdims=True) * (0.5 / D)

    def full(a):
        n = len(a.shape)
        return pl.BlockSpec(a.shape, lambda i, chip_ref: (0,) * n)

    params = [*ws, *ms, *vs]
    out = _call(
        body, (p_all, p_all, pm_all, pm_all, g1_all, tbl_all, *params), name="small_adam", grid=(1,), prefetch=(chip,),
        in_specs=[full(p_all),
                  pl.BlockSpec((N_DEV, 8, fw_cols), lambda i, chip_ref: (0, 0, chip_ref[0])),
                  full(pm_all),
                  pl.BlockSpec((N_DEV, 8, cw_cols), lambda i, chip_ref: (0, 0, chip_ref[0])),
                  full(g1_all), full(tbl_all), *[full(a) for a in params]],
        out_specs=[full(a) for a in ws] * 4 + [pl.BlockSpec((1, 1), lambda i, chip_ref: (0, 0))],
        out_shape=[SDS(a.shape, F32) for a in ws] * 4 + [SDS((1, 1), F32)], sem=("arbitrary",), vmem_mib=32)
    return out[0:N_SMALL], out[N_SMALL:2 * N_SMALL], out[2 * N_SMALL:3 * N_SMALL], out[3 * N_SMALL:4 * N_SMALL], out[4 * N_SMALL]


PLACE_STEPS = 4


def _place_specs(shards):
    rows = [s.shape[0] // PLACE_STEPS for s in shards]
    return ([pl.BlockSpec((r, D), lambda i, chip_ref: (i, 0)) for r in rows],
            [pl.BlockSpec((r, D), lambda i, chip_ref: (chip_ref[0] * PLACE_STEPS + i, 0)) for r in rows],
            [SDS((N_CHIPS * s.shape[0], D), BF) for s in shards])


def _place_first(chip, shard, conv_w, ffn_conv_w):
    def body(chip_ref, a, s0, s1, o, t0, t1):
        o[...] = a[...].astype(BF)

        @pl.when(pl.program_id(0) == 0)
        def _():
            for s, t in ((s0, t0), (s1, t1)):
                t[...] = jnp.zeros_like(t)
                t[0, 0:3, :] = s[...]

    ins, outs, shapes = _place_specs([shard])
    taps = (conv_w, ffn_conv_w)
    return _call(
        body, (shard, conv_w, ffn_conv_w), name="place_first", grid=(PLACE_STEPS,), prefetch=(chip,),
        in_specs=ins + [pl.BlockSpec(s.shape, lambda i, chip_ref: (0, 0)) for s in taps],
        out_specs=outs + [pl.BlockSpec((1, 8, s.shape[1]), lambda i, chip_ref: (chip_ref[0], 0, 0)) for s in taps],
        out_shape=shapes + [SDS((N_CHIPS, 8, s.shape[1]), F32) for s in taps],
        sem=("arbitrary",), vmem_mib=32, free=(1, 2))


def _place_rest(chip, shards, w_up, table, bucket, comm):
    n = len(shards)
    c_up = w_up.shape[1]
    edges = [round(k * (c_up // 128) / PLACE_STEPS) * 128 for k in range(PLACE_STEPS + 1)]

    def body(chip_ref, *refs):
        a, (up_ref, tab_ref, bk_ref), o = refs[:n], refs[n:n + 3], refs[n + 3:2 * n + 3]
        up_o, bias_ref = refs[2 * n + 3:]
        for src, dst in zip(a, o):
            dst[...] = src[...].astype(BF)
        for k in range(PLACE_STEPS):
            @pl.when(pl.program_id(0) == k)
            def _(k=k):
                up_o[edges[k]:edges[k + 1], :] = up_ref[:, edges[k]:edges[k + 1]].T.astype(BF)

        @pl.when(pl.program_id(0) == 0)
        def _():
            bk = bk_ref[...]
            eq = [bk == b for b in range(NBUCKET)]
            for h in range(NH):
                acc = jnp.zeros((BLK, 2 * BLK), F32)
                for b in range(NBUCKET):
                    acc = jnp.where(eq[b], tab_ref[h, b], acc)
                bias_ref[h * BLK:(h + 1) * BLK, :] = acc

    ins, outs, shapes = _place_specs(shards)
    return _call(
        body, (*shards, w_up, table, bucket), name="place_rest", grid=(PLACE_STEPS,), prefetch=(chip,),
        in_specs=ins + [_resident(w_up.shape), pl.BlockSpec(memory_space=pltpu.SMEM),
                        pl.BlockSpec(bucket.shape, lambda i, chip_ref: (0, 0))],
        out_specs=outs + [pl.BlockSpec((c_up, D), lambda i, chip_ref: (chip_ref[0], 0)),
                          pl.BlockSpec((NH * BLK, 2 * BLK), lambda i, chip_ref: (0, 0))],
        out_shape=shapes + [SDS((N_CHIPS * c_up, D), BF), SDS((NH * BLK, 2 * BLK), F32)],
        sem=("arbitrary",), vmem_mib=32, comm=comm, free=(n + 1, n + 2))


def kernel(x, norm_mix_g, w_in, conv_w, q_norm_g, k_norm_g, rel_bias_table, sinks, out_norm_conv_g, out_norm_attn_g, w_out, norm_ffn_g, w_up, ffn_conv_w, ffn_conv_b, w_down, loss_target, m_norm_mix_g, m_w_in, m_conv_w, m_q_norm_g, m_k_norm_g, m_rel_bias_table, m_sinks, m_out_norm_conv_g, m_out_norm_attn_g, m_w_out, m_norm_ffn_g, m_w_up, m_ffn_conv_w, m_ffn_conv_b, m_w_down, v_norm_mix_g, v_w_in, v_conv_w, v_q_norm_g, v_k_norm_g, v_rel_bias_table, v_sinks, v_out_norm_conv_g, v_out_norm_attn_g, v_w_out, v_norm_ffn_g, v_w_up, v_ffn_conv_w, v_ffn_conv_b, v_w_down):
    as_arg = lambda i: jnp.reshape(i, (1,)).astype(jnp.int32)
    chip = as_arg(2 * lax.axis_index("x") + lax.axis_index("y"))
    core = as_arg(lax.axis_index("c"))
    me = 2 * chip + core
    xs, tgt = x[0], loss_target[0]
    qg, kg, gco, gao, g1, g2, fb = q_norm_g, k_norm_g, out_norm_conv_g, out_norm_attn_g, norm_mix_g, norm_ffn_g, ffn_conv_b
    pieces = lambda g: g.reshape(N_DEV, g.shape[0] // N_DEV, D)
    whole = lambda f: f.reshape(2 * f.shape[1], D)

    bucket = jnp.asarray(_bucket_table())
    p_in, p_cw, p_fw = _place_first(chip, w_in[0].T, conv_w[0], ffn_conv_w[0])
    p_out, p_down, p_up, bias, w_int, cw_all, fw_all = _place_rest(
        chip, [w_out[0], w_down[0]], w_up[0], rel_bias_table.T, bucket,
        comm=[_t_gather(p_in, relayed_first=True), _t_small_weights(p_cw), _t_small_weights(p_fw)])
    cw8 = jnp.transpose(cw_all, (1, 0, 2)).reshape(8, CW)
    fw8 = jnp.transpose(fw_all, (1, 0, 2)).reshape(8, 2 * DFF)

    early = 3 / 11
    proj, u1, w_out_f, p_up = _inproj(xs, g1, w_int, comm=[_t_gather(p_out), _t_gather(p_up, (0, early))])
    y, w_upt = _mix_fwd(proj, sinks, cw8, qg, kg, gco, gao, bias, comm=[_t_gather(p_up, (early, 1))])
    h1, u2 = _outproj(y, w_out_f, xs, g2)
    up, w_down_f = _ffn_up(u2, w_upt, comm=[_t_gather(p_down)])
    a, = _ffn_act(up, fw8, fb)
    dh2, dh2b, sq = _ffn_down(a, w_down_f, h1, tgt)

    gdbf, = _wgrad("wgrad_down", [a], dh2b, None)
    da, sib_down = _ffn_down_bwd(dh2b, w_down_f, comm=[_t_sibling(pieces(gdbf))])
    pbf_down, own_down = _chip_sum("chip_sum_w_down", pieces(gdbf), sib_down, core, chip)
    dug, duv, dfwg, dfwv, dfbg, dfbv, chips_down = _ffn_act_bwd(up, da, fw8, fb, comm=[_t_chips(pbf_down)])
    fin_down, = _final_sum("final_sum_w_down", own_down, chips_down, core)
    gubf, = _wgrad("wgrad_up", [dug, duv], u2, False)
    p_all = _pack_ffn(me, dfwg, dfwv, dfbg, dfbv, sq)
    dh1, dh1b, dg2, sib_up, fin_down, p_all = _norm_matmul_bwd(
        "ffn_up_bwd", [dug, duv], w_upt, [0, DFF], h1, g2, dh2, True,
        comm=[_t_sibling(pieces(gubf)), _t_swap(fin_down), _t_allgather(p_all)])
    pbf_up, own_up = _chip_sum("chip_sum_w_up", pieces(gubf), sib_up, core, chip)
    gobf, = _wgrad("wgrad_out", [y], dh1b, True)
    dy, sib_out = _out_bwd(dh1b, w_out_f, comm=[_t_sibling(pieces(gobf))])
    pbf_out, own_out = _chip_sum("chip_sum_w_out", pieces(gobf), sib_out, core, chip)
    dproj, dcw8, dqg, dkg, dgco, dgao, dsink, dbias, chips_up = _mix_bwd(
        proj, dy, sinks, cw8, qg, kg, gco, gao, bias, comm=[_t_chips(pbf_up)])
    fin_up, = _final_sum("final_sum_w_up", own_up, chips_up, core)
    tbl_all = _band_bias_bwd(dbias, bucket, me)
    pm_all = _pack_mix(me, dg2, dgco, dgao, dcw8, dqg, dkg, dsink)
    gibf, chips_out, fin_up, pm_all, tbl_all = _wgrad(
        "wgrad_in", [dproj], u1, False,
        comm=[_t_chips(pbf_out), _t_swap(fin_up), _t_allgather(pm_all), _t_allgather(tbl_all)])
    fin_out, sib_in = _final_sum("final_sum_w_out", own_out, chips_out, core, comm=[_t_sibling(pieces(gibf))])
    pbf_in, own_in = _chip_sum("chip_sum_w_in", pieces(gibf), sib_in, core, chip)
    dx, g1_all, chips_in, fin_out = _norm_matmul_bwd(
        "in_bwd", [dproj], w_int, [0], xs, g1, dh1, False, comm=[_t_chips(pbf_in), _t_swap(fin_out)], slot=me)
    fin_in, = _final_sum("final_sum_w_in", own_in, chips_in, core)
    g1_all, fin_in = _comm_call("gather_last", [_t_allgather(g1_all), _t_swap(fin_in)])

    g_w_out, g_w_down = whole(fin_out), whole(fin_down)
    g_w_down, d_down, nm_down, nv_down = _adamw("adamw_w_down", w_down[0], g_w_down, m_w_down[0], v_w_down[0], 352, True)
    g_w_up, d_up, nm_up, nv_up = _adamw(
        "adamw_w_up", w_up[0], whole(fin_up), m_w_up[0], v_w_up[0], 256, True, stage=False, g_transposed=True)
    g_w_out, d_out, nm_out, nv_out = _adamw("adamw_w_out", w_out[0], g_w_out, m_w_out[0], v_w_out[0], 256, True, stage=False)
    g_w_in, d_in, nm_in, nv_in = [a.T for a in _adamw(
        "adamw_w_in", w_in[0].T, whole(fin_in), m_w_in[0].T, v_w_in[0].T, INW // N_CHIPS // 3, True, stage=False)]
    taps = lambda a: jnp.transpose(a, (1, 0, 2))
    sw = [norm_mix_g, taps(conv_w), q_norm_g, k_norm_g, rel_bias_table.T, sinks, out_norm_conv_g, out_norm_attn_g,
          norm_ffn_g, taps(ffn_conv_w), ffn_conv_b]
    smm = [m_norm_mix_g, taps(m_conv_w), m_q_norm_g, m_k_norm_g, m_rel_bias_table.T, m_sinks, m_out_norm_conv_g,
           m_out_norm_attn_g, m_norm_ffn_g, taps(m_ffn_conv_w), m_ffn_conv_b]
    smv = [v_norm_mix_g, taps(v_conv_w), v_q_norm_g, v_k_norm_g, v_rel_bias_table.T, v_sinks, v_out_norm_conv_g,
           v_out_norm_attn_g, v_norm_ffn_g, taps(v_ffn_conv_w), v_ffn_conv_b]
    *small_out, loss = _small_adam(chip, p_all, pm_all, g1_all, tbl_all, sw, smm, smv)
    sg, sd, snm, snv = [list(r) for r in small_out]
    for r in (sg, sd, snm, snv):
        r[1], r[4], r[9] = taps(r[1]), r[4].T, taps(r[9])

    def order(s, b_in, b_out, b_up, b_down):
        return (s[0], b_in[None], s[1], s[2], s[3], s[4], s[5], s[6], s[7], b_out[None], s[8], b_up[None],
                s[9], s[10], b_down[None])

    return (loss.reshape(()), dx[None],
            *order(sg, g_w_in, g_w_out, g_w_up, g_w_down),
            *order(sd, d_in, d_out, d_up, d_down),
            *order(snm, nm_in, nm_out, nm_up, nm_down),
            *order(snv, nv_in, nv_out, nv_up, nv_down))
```

```python
import functools
import math

import numpy as np

import jax
import jax.numpy as jnp
from jax import lax
from jax.experimental import pallas as pl
from jax.experimental.pallas import tpu as pltpu

F32 = jnp.float32
BF = jnp.bfloat16
SDS = jax.ShapeDtypeStruct

T = 2048
D = 1024
CW = 512
AW = 512
HD = 64
NH = 8
NKV = 2
GQ = 4
INW = 2304
DFF = 2816
BLK = 128
NB = T // BLK
NBUCKET = 32
EPS = 1e-6
NEG_INF = -1e30
N_CHIPS = 4
N_DEV = 8

ADAM_LR = 0.001
ADAM_B1 = 0.9
ADAM_B2 = 0.999
ADAM_EPS = 1e-08
ADAM_WD = 0.01
ADAM_STEP = 10

TM = 512
MIB = 1024 * 1024
MESH = pl.DeviceIdType.MESH
ANY = pl.BlockSpec(memory_space=pl.ANY)

_pcall = pl.pallas_call


def _params(sem=None, vmem_mib=None, collective_id=None):
    kw = {} if collective_id is None else {"collective_id": collective_id}
    if sem is not None:
        kw["dimension_semantics"] = sem
    if vmem_mib is not None:
        kw["vmem_limit_bytes"] = vmem_mib * MIB
    return pltpu.CompilerParams(**kw)


def _resident(shape):
    return pl.BlockSpec(shape, lambda *_: (0,) * len(shape), pipeline_mode=pl.Buffered(1))


def _dot(a, b, ca, cb):
    return lax.dot_general(a, b, (((ca,), (cb,)), ((), ())), preferred_element_type=F32)


def _rms_bwd(dy, x, r, g):
    dg = jnp.sum(dy * (x * r), axis=0, keepdims=True)
    dgx = dy * g
    dx = r * dgx - x * (r * r * r) * jnp.mean(x * dgx, axis=-1, keepdims=True)
    return dx, dg


def _where():
    x, y, c = lax.axis_index("x"), lax.axis_index("y"), lax.axis_index("c")
    return x, y, c, [(1 - x, y), (x, 1 - y), (1 - x, 1 - y)]


def _rcopy(src, dst, ssem, rsem, dev):
    return pltpu.make_async_remote_copy(src_ref=src, dst_ref=dst, send_sem=ssem, recv_sem=rsem, device_id=dev,
                                        device_id_type=MESH)


SIBLING, Y_CHIP, X_CHIP, DIAGONAL_CHIP = 1, 2, 4, 6
OTHER_CHIPS = (Y_CHIP, X_CHIP, DIAGONAL_CHIP)
EVERYONE = tuple(range(1, N_DEV))
BARRIER_OF = {(SIBLING,): 0, (SIBLING, Y_CHIP, X_CHIP): 1, OTHER_CHIPS: 2, (SIBLING,) + OTHER_CHIPS: 3, EVERYONE: 4}


def _peer(rel):
    x, y, c, _ = _where()
    return x ^ ((rel >> 2) & 1), y ^ ((rel >> 1) & 1), c ^ (rel & 1)


class _Task:
    def __init__(self, ins, outs, alias, n_sem, start, finish, middle=None, peers=()):
        self.ins, self.outs, self.alias, self.n_sem, self.start, self.finish = ins, outs, alias, n_sem, start, finish
        self.middle = middle if middle is not None else (lambda *args: None)
        self.peers = peers


def _peers_of(comm):
    return tuple(sorted({p for t in comm for p in t.peers}))


def _enter(comm):
    peers = _peers_of(comm)
    barrier = pltpu.get_barrier_semaphore()
    for rel in peers:
        pl.semaphore_signal(barrier, inc=1, device_id=_peer(rel), device_id_type=MESH)
    pl.semaphore_wait(barrier, len(peers))


ROWS16 = 16


def _t_gather(placed, part=(0, 1), relayed_first=False):
    R = placed.shape[0] // N_CHIPS
    q = R // 4
    lo, hi = (round(f * (q // ROWS16)) * ROWS16 for f in part)

    def quarter(chip_index, core, k):
        return pl.ds(pl.multiple_of(chip_index * R + core * 2 * q + k * q + lo, ROWS16), hi - lo)

    def places():
        x, y, c, _ = _where()
        return c, 2 * x + y, 2 * (1 - x) + y, 2 * x + (1 - y), 2 * (1 - x) + (1 - y), (1 - x, y, c), (x, 1 - y, c), (x, y, 1 - c)

    def copy(buf, k, chip_index, core, quart, ss, rs, b, dev):
        window = buf.at[quarter(chip_index, core, quart)]
        return _rcopy(window, window, ss.at[b + k], rs.at[b + k], dev)

    def first_hop(cout, ss, rs, b, which):
        c, me, _, _, _, x_nbr, y_nbr, _ = places()
        for k, (quart, dev) in enumerate(((0, x_nbr), (1, y_nbr), (1, x_nbr), (0, y_nbr))):
            if k in which:
                copy(cout[0], k, me, c, quart, ss, rs, b, dev).start()

    def start(cin, cout, ss, rs, b):
        first_hop(cout, ss, rs, b, (0, 1) if relayed_first else (0, 1, 2, 3))

    def middle(cin, cout, ss, rs, b):
        c, _, xc, yc, _, x_nbr, y_nbr, sib = places()
        for k, chip_index, quart, dev in ((0, xc, 0, y_nbr), (1, yc, 1, x_nbr)):
            copy(cout[0], k, chip_index, c, quart, ss, rs, b, dev).wait_recv()
            copy(cout[0], 4 + k, chip_index, c, quart, ss, rs, b, dev).start()
            copy(cout[0], 6 + k, chip_index, c, quart, ss, rs, b, sib).start()
        if relayed_first:
            first_hop(cout, ss, rs, b, (2, 3))

    later = ((2, 1, 1), (3, 2, 0), (4, 3, 0), (5, 3, 1))

    def finish(cin, cout, ss, rs, b):
        c, me, xc, yc, dc, _, _, sib = places()
        chip_of = {1: xc, 2: yc, 3: dc}
        for k, whose, quart in later:
            copy(cout[0], k, chip_of[whose], c, quart, ss, rs, b, sib).wait_recv()
            copy(cout[0], 6 + k, chip_of[whose], c, quart, ss, rs, b, sib).start()
        for k, whose, quart in ((0, 1, 0), (1, 2, 1)) + later:
            copy(cout[0], 6 + k, chip_of[whose], 1 - c, quart, ss, rs, b, sib).wait_recv()
        for k in range(12):
            copy(cout[0], k, me, c, 0, ss, rs, b, sib).wait_send()

    return _Task([placed], [SDS(placed.shape, placed.dtype)], [(0, 0)], 12, start, finish, middle, peers=(SIBLING, Y_CHIP, X_CHIP))


def _t_small_weights(buf):
    def start(cin, cout, ss, rs, b):
        x, y, c, chips = _where()
        mine = cout[0].at[2 * x + y]
        for r, (px, py) in enumerate(chips):
            _rcopy(mine, mine, ss.at[b + r], rs.at[b + r], (px, py, c)).start()

    def finish(cin, cout, ss, rs, b):
        x, y, c, chips = _where()
        for r, (px, py) in enumerate(chips):
            got = cout[0].at[2 * px + py]
            _rcopy(got, got, ss.at[b + r], rs.at[b + r], (px, py, c)).wait_recv()
        for r, (px, py) in enumerate(chips):
            mine = cout[0].at[2 * x + y]
            _rcopy(mine, mine, ss.at[b + r], rs.at[b + r], (px, py, c)).wait_send()

    return _Task([buf], [SDS(buf.shape, buf.dtype)], [(0, 0)], 3, start, finish, peers=OTHER_CHIPS)


def _t_sibling(gbf):
    def start(cin, cout, ss, rs, b):
        x, y, c, _ = _where()
        for jj in range(N_CHIPS):
            _rcopy(cin[0].at[2 * jj + (1 - c)], cout[0].at[jj], ss.at[b + jj], rs.at[b + jj], (x, y, 1 - c)).start()

    def finish(cin, cout, ss, rs, b):
        x, y, c, _ = _where()
        for jj in range(N_CHIPS):
            got = cout[0].at[jj]
            _rcopy(got, got, ss.at[b + jj], rs.at[b + jj], (x, y, 1 - c)).wait_recv()
        for jj in range(N_CHIPS):
            got = cout[0].at[jj]
            _rcopy(got, got, ss.at[b + jj], rs.at[b + jj], (x, y, 1 - c)).wait_send()

    return _Task([gbf], [SDS((N_CHIPS,) + gbf.shape[1:], BF)], [], N_CHIPS, start, finish, peers=(SIBLING,))


def _t_chips(pbf):
    def start(cin, cout, ss, rs, b):
        x, y, c, chips = _where()
        for r, (px, py) in enumerate(chips):
            _rcopy(cin[0].at[2 * px + py], cout[0].at[r], ss.at[b + r], rs.at[b + r], (px, py, c)).start()

    def finish(cin, cout, ss, rs, b):
        x, y, c, chips = _where()
        for r, (px, py) in enumerate(chips):
            got = cout[0].at[r]
            _rcopy(got, got, ss.at[b + r], rs.at[b + r], (px, py, c)).wait_recv()
        for r, (px, py) in enumerate(chips):
            got = cout[0].at[r]
            _rcopy(got, got, ss.at[b + r], rs.at[b + r], (px, py, c)).wait_send()

    return _Task([pbf], [SDS((3,) + pbf.shape[1:], BF)], [], 3, start, finish, peers=OTHER_CHIPS)


def _t_swap(fin):
    def start(cin, cout, ss, rs, b):
        x, y, c, _ = _where()
        mine = cout[0].at[c]
        _rcopy(mine, mine, ss.at[b], rs.at[b], (x, y, 1 - c)).start()

    def finish(cin, cout, ss, rs, b):
        x, y, c, _ = _where()
        got = cout[0].at[1 - c]
        _rcopy(got, got, ss.at[b], rs.at[b], (x, y, 1 - c)).wait_recv()
        _rcopy(got, got, ss.at[b], rs.at[b], (x, y, 1 - c)).wait_send()

    return _Task([fin], [SDS(fin.shape, fin.dtype)], [(0, 0)], 1, start, finish, peers=(SIBLING,))


def _t_allgather(buf):
    def peers():
        x, y, c, _ = _where()
        out = []
        for rel in range(1, N_DEV):
            px, py, pc = x ^ ((rel >> 2) & 1), y ^ ((rel >> 1) & 1), c ^ (rel & 1)
            out.append((rel - 1, 4 * px + 2 * py + pc, (px, py, pc)))
        return 4 * x + 2 * y + c, out

    def start(cin, cout, ss, rs, b):
        me, ps = peers()
        mine = cout[0].at[me]
        for k, _, dev in ps:
            _rcopy(mine, mine, ss.at[b + k], rs.at[b + k], dev).start()

    def finish(cin, cout, ss, rs, b):
        me, ps = peers()
        for k, pidx, dev in ps:
            got = cout[0].at[pidx]
            _rcopy(got, got, ss.at[b + k], rs.at[b + k], dev).wait_recv()
        for k, _, dev in ps:
            mine = cout[0].at[me]
            _rcopy(mine, mine, ss.at[b + k], rs.at[b + k], dev).wait_send()

    return _Task([buf], [SDS(buf.shape, buf.dtype)], [(0, 0)], N_DEV - 1, start, finish, peers=EVERYONE)


def _run_tasks(comm, which, cin, cout, ss, rs):
    i0 = o0 = s0 = 0
    for t in comm:
        getattr(t, which)(cin[i0:i0 + len(t.ins)], cout[o0:o0 + len(t.outs)], ss, rs, s0)
        i0, o0, s0 = i0 + len(t.ins), o0 + len(t.outs), s0 + t.n_sem


def _from_hbm(*arrays):
    return [pltpu.with_memory_space_constraint(a, pltpu.HBM) for a in arrays]


def _in_hbm(shapes):
    return [pltpu.HBM(s.shape, s.dtype) for s in shapes]


def _comm_layout(comm, n_in, n_out):
    c_in = [a for t in comm for a in t.ins]
    c_out = [s for t in comm for s in t.outs]
    aliases, i0, o0 = {}, 0, 0
    for t in comm:
        for i, o in t.alias:
            aliases[n_in + i0 + i] = n_out + o0 + o
        i0, o0 = i0 + len(t.ins), o0 + len(t.outs)
    return c_in, c_out, aliases, sum(t.n_sem for t in comm)


def _call(body, operands, *, name, grid, in_specs, out_specs, out_shape, scratch_shapes=(), sem=None, vmem_mib=None, comm=(),
          free=(), prefetch=()):
    operands = [o if s.memory_space == pltpu.SMEM or k in free else pltpu.with_memory_space_constraint(o, pltpu.HBM)
                for k, (o, s) in enumerate(zip(operands, in_specs))]
    n_pre, n_in, n_out, n_scr = len(prefetch), len(in_specs), len(out_specs), len(scratch_shapes)
    c_in, c_out, aliases, n_sem = _comm_layout(comm, n_pre + n_in, n_out)
    sems = [pltpu.SemaphoreType.DMA((n_sem,)), pltpu.SemaphoreType.DMA((n_sem,))] if comm else []

    def wrapped(*refs):
        pre, refs = refs[:n_pre], refs[n_pre:]
        ins, cin = refs[:n_in], refs[n_in:n_in + len(c_in)]
        rest = refs[n_in + len(c_in):]
        outs, cout = rest[:n_out], rest[n_out:n_out + len(c_out)]
        rest = rest[n_out + len(c_out):]
        scr, csem = rest[:n_scr], rest[n_scr:]
        if not comm:
            return body(*pre, *ins, *outs, *scr)
        step = functools.reduce(lambda acc, k: acc * grid[k] + pl.program_id(k), range(len(grid)), 0)
        n_steps = math.prod(grid)

        @pl.when(step == 0)
        def _():
            _enter(comm)
            _run_tasks(comm, "start", cin, cout, *csem)

        pl.when(step == n_steps // 2)(lambda: _run_tasks(comm, "middle", cin, cout, *csem))
        body(*pre, *ins, *outs, *scr)
        pl.when(step == n_steps - 1)(lambda: _run_tasks(comm, "finish", cin, cout, *csem))

    grid_spec = pltpu.PrefetchScalarGridSpec(
        num_scalar_prefetch=n_pre, grid=grid, in_specs=list(in_specs) + [ANY] * len(c_in),
        out_specs=list(out_specs) + [ANY] * len(c_out), scratch_shapes=list(scratch_shapes) + sems)
    return _pcall(
        wrapped, name=name, grid_spec=grid_spec, out_shape=_in_hbm(list(out_shape) + c_out), input_output_aliases=aliases,
        compiler_params=_params(("arbitrary",) * len(grid) if comm else sem, vmem_mib,
                                BARRIER_OF[_peers_of(comm)] if comm else None),
    )(*prefetch, *operands, *_from_hbm(*c_in))


def _comm_call(name, comm):
    c_in, c_out, aliases, n_sem = _comm_layout(comm, 0, 0)

    def body(*refs):
        cin, cout, (ss, rs) = refs[:len(c_in)], refs[len(c_in):len(c_in) + len(c_out)], refs[len(c_in) + len(c_out):]
        _enter(comm)
        for phase in ("start", "middle", "finish"):
            _run_tasks(comm, phase, cin, cout, ss, rs)

    return _pcall(
        body, name=name, in_specs=[ANY] * len(c_in), out_specs=[ANY] * len(c_out), out_shape=_in_hbm(c_out),
        scratch_shapes=[pltpu.SemaphoreType.DMA((n_sem,)), pltpu.SemaphoreType.DMA((n_sem,))],
        input_output_aliases=aliases, compiler_params=_params(collective_id=BARRIER_OF[_peers_of(comm)]),
    )(*_from_hbm(*c_in))


def _inproj(x, g1, w_int, comm=()):
    tm = TM

    def body(x_ref, g_ref, w_ref, proj_ref, u_ref):
        xf = x_ref[...]
        r = lax.rsqrt(jnp.mean(xf * xf, axis=-1, keepdims=True) + EPS)
        u = (xf * r * g_ref[...]).astype(BF)
        u_ref[...] = u
        proj_ref[...] = _dot(u, w_ref[...], 1, 1)

    return _call(
        body, (x, g1, w_int), name="inproj", grid=(T // tm,),
        in_specs=[pl.BlockSpec((tm, D), lambda i: (i, 0)), pl.BlockSpec((1, D), lambda i: (0, 0)),
                  _resident((INW, D))],
        out_specs=[pl.BlockSpec((tm, INW), lambda i: (i, 0)), pl.BlockSpec((tm, D), lambda i: (i, 0))],
        out_shape=[SDS((T, INW), F32), SDS((T, D), BF)], sem=("parallel",), vmem_mib=40, comm=comm, free=(0, 1))


def _outproj(y, w_out, x, g2):
    tm = TM

    def body(y_ref, w_ref, x_ref, g_ref, h1_ref, u2_ref):
        h1 = x_ref[...] + _dot(y_ref[...], w_ref[...], 1, 0)
        h1_ref[...] = h1
        r = lax.rsqrt(jnp.mean(h1 * h1, axis=-1, keepdims=True) + EPS)
        u2_ref[...] = (h1 * r * g_ref[...]).astype(BF)

    return _call(
        body, (y, w_out, x, g2), name="outproj", grid=(T // tm,),
        in_specs=[pl.BlockSpec((tm, D), lambda i: (i, 0)), _resident((D, D)),
                  pl.BlockSpec((tm, D), lambda i: (i, 0)), pl.BlockSpec((1, D), lambda i: (0, 0))],
        out_specs=[pl.BlockSpec((tm, D), lambda i: (i, 0)), pl.BlockSpec((tm, D), lambda i: (i, 0))],
        out_shape=[SDS((T, D), F32), SDS((T, D), BF)], sem=("parallel",), vmem_mib=32, free=(2, 3))


def _ffn_up(u2, w_upt, comm=()):
    tm, tn = T, 512

    def body(u_ref, w_ref, o_ref):
        o_ref[...] = _dot(u_ref[...], w_ref[...], 1, 1).astype(BF)

    return _call(
        body, (u2, w_upt), name="ffn_up", grid=(T // tm, 2 * DFF // tn),
        in_specs=[pl.BlockSpec((tm, D), lambda i, j: (i, 0)), pl.BlockSpec((tn, D), lambda i, j: (j, 0))],
        out_specs=[pl.BlockSpec((tm, tn), lambda i, j: (i, j))], out_shape=[SDS((T, 2 * DFF), BF)],
        sem=("parallel", "parallel"), vmem_mib=32, comm=comm, free=(1,))


def _ffn_down(a, w_down, h1, tgt):
    tm = TM

    def body(a_ref, w_ref, h1_ref, t_ref, dh_ref, dhb_ref, l_ref):
        @pl.when(pl.program_id(0) == 0)
        def _():
            l_ref[...] = jnp.zeros_like(l_ref)

        h2 = h1_ref[...] + _dot(a_ref[...], w_ref[...], 1, 0)
        e = h2 - t_ref[...]
        dh = e * (1.0 / D)
        dh_ref[...] = dh
        dhb_ref[...] = dh.astype(BF)
        e2 = jnp.sum((e * e).reshape(tm // 8, 8, D), axis=0)
        acc = e2[:, 0:128]
        for k in range(1, D // 128):
            acc = acc + e2[:, k * 128:(k + 1) * 128]
        l_ref[...] += acc

    return _call(
        body, (a, w_down, h1, tgt), name="ffn_down", grid=(T // tm,),
        in_specs=[pl.BlockSpec((tm, DFF), lambda i: (i, 0)), _resident((DFF, D)),
                  pl.BlockSpec((tm, D), lambda i: (i, 0)), pl.BlockSpec((tm, D), lambda i: (i, 0))],
        out_specs=[pl.BlockSpec((tm, D), lambda i: (i, 0)), pl.BlockSpec((tm, D), lambda i: (i, 0)),
                   pl.BlockSpec((8, 128), lambda i: (0, 0))],
        out_shape=[SDS((T, D), F32), SDS((T, D), BF), SDS((8, 128), F32)], sem=("arbitrary",), vmem_mib=40, free=(2, 3))


def _bucket_table():
    q = np.arange(BLK, dtype=np.int32)[:, None]
    j = np.arange(2 * BLK, dtype=np.int32)[None, :]
    n = np.maximum(q + BLK - j, 0)
    nf = np.maximum(n, 1).astype(np.float32)
    max_exact = NBUCKET // 2
    large = max_exact + (np.log(nf / np.float32(max_exact)) / np.float32(math.log(BLK / max_exact))
                         * np.float32(NBUCKET - max_exact)).astype(np.int32)
    large = np.minimum(large, NBUCKET - 1)
    return np.where(n < max_exact, n, large).astype(np.int32)


def _band_bias_bwd(dbias, bucket, me):
    def body(me_ref, db_ref, bk_ref, o_ref):
        bk = bk_ref[...]
        for b in range(NBUCKET):
            m = bk == b
            for h in range(NH):
                v = jnp.where(m, db_ref[h * BLK:(h + 1) * BLK, :], 0.0)
                s = jnp.sum(jnp.sum(v, axis=1, keepdims=True), axis=0, keepdims=True)
                o_ref[0, h:h + 1, b:b + 1] = s

    grid_spec = pltpu.PrefetchScalarGridSpec(
        num_scalar_prefetch=1, grid=(1,),
        in_specs=[pl.BlockSpec((NH * BLK, 2 * BLK), lambda i, me_ref: (0, 0)),
                  pl.BlockSpec((BLK, 2 * BLK), lambda i, me_ref: (0, 0))],
        out_specs=pl.BlockSpec((1, NH, NBUCKET), lambda i, me_ref: (me_ref[0], 0, 0)),
    )
    return _pcall(body, name="band_bias_bwd", grid_spec=grid_spec, out_shape=SDS((N_DEV, NH, NBUCKET), F32),
                  compiler_params=_params(("arbitrary",)))(me, dbias, bucket)


def _two_bf16(x):
    hi = x.astype(BF)
    return hi, (x - hi.astype(F32)).astype(BF)


def _head_sums(x, seg):
    hi, lo = _two_bf16(x)
    s = seg[0:x.shape[1], :]
    return _dot(hi, s, 1, 0) + _dot(lo, s, 1, 0)


def _head_spread(v, seg, width):
    hi, lo = _two_bf16(v)
    s = seg[0:width, :]
    return _dot(hi, s, 1, 1) + _dot(lo, s, 1, 1)


def _head_norm(x, g_t, seg, by_head=False):
    if by_head:
        heads = [x[:, h * HD:(h + 1) * HD] for h in range(x.shape[1] // HD)]
        r = jnp.concatenate([jnp.broadcast_to(lax.rsqrt(jnp.mean(v * v, axis=-1, keepdims=True) + EPS), v.shape)
                             for v in heads], axis=1)
    else:
        r = lax.rsqrt(_head_sums(x * x, seg) * (1.0 / HD) + EPS)
        r = _head_spread(r, seg, x.shape[1])
    return x * r * g_t, r


def _head_norm_bwd(dy, x, r, g_t, seg):
    dg_t = jnp.sum(dy * (x * r), axis=0, keepdims=True)
    dgx = dy * g_t
    mean = _head_spread(_head_sums(x * dgx, seg) * (1.0 / HD), seg, x.shape[1])
    return r * dgx - x * (r * r * r) * mean, dg_t


def _fold_heads(v):
    out = v[:, 0:HD]
    for h in range(1, v.shape[1] // HD):
        out = out + v[:, h * HD:(h + 1) * HD]
    return out


def _mix_forward(P, zc8, zh8, pkv, first, cw, qg_t, kg_t, gco, gao, seg, sink_ref, bias_ref, by_head=False):
    gate_b = P[:, 0:CW]
    gate_c = P[:, CW:2 * CW]
    hc = P[:, 2 * CW:3 * CW]
    z = gate_c * hc
    keep = jnp.where(first, 0.0, 1.0)
    zp = zc8 * zh8 * keep
    p1 = zp[7:8, :]
    p2 = zp[6:7, :]
    row = lax.broadcasted_iota(jnp.int32, (BLK, 1), 0)
    z1 = jnp.where(row == 0, p1, pltpu.roll(z, 1, 0))
    z2 = jnp.where(row == 0, p2, jnp.where(row == 1, p1, pltpu.roll(z, 2, 0)))
    cz = cw[0:1, :] * z2 + cw[1:2, :] * z1 + cw[2:3, :] * z
    y_conv = gate_b * cz

    scale = HD ** -0.5
    qi = lax.broadcasted_iota(jnp.int32, (BLK, 2 * BLK), 0)
    kj = lax.broadcasted_iota(jnp.int32, (BLK, 2 * BLK), 1)
    dd = qi + BLK - kj
    first_key = jnp.where(first, BLK, 0)
    valid = (dd >= 0) & (dd < BLK) & (kj >= first_key)

    q0 = 3 * CW
    k0 = q0 + AW
    v0 = k0 + NKV * HD
    q_raw = P[:, q0:k0]
    qn, rq = _head_norm(q_raw, qg_t, seg, by_head)
    qs = (qn * scale).astype(BF)
    k_raw = jnp.concatenate([pkv[:, 0:NKV * HD], P[:, k0:v0]], axis=0)
    kn, rk = _head_norm(k_raw, kg_t, seg, by_head)
    knb = kn.astype(BF)
    heads = []
    for h in range(NH):
        kv = h // GQ
        kb = knb[:, kv * HD:(kv + 1) * HD]
        vb = jnp.concatenate([pkv[:, NKV * HD + kv * HD:NKV * HD + (kv + 1) * HD],
                              P[:, v0 + kv * HD:v0 + (kv + 1) * HD]], axis=0).astype(BF)
        Q = qs[:, h * HD:(h + 1) * HD]
        S = _dot(Q, kb, 1, 1) + bias_ref[h * BLK:(h + 1) * BLK, :]
        S = jnp.where(valid, S, NEG_INF)
        sink = sink_ref[0, h]
        m = jnp.maximum(jnp.max(S, axis=-1, keepdims=True), sink)
        p = jnp.exp(S - m)
        es = jnp.exp(sink - m)
        denom = jnp.sum(p, axis=-1, keepdims=True) + es
        probs = p / denom
        O = _dot(probs.astype(BF), vb, 1, 0)
        heads.append(dict(kb=kb, vb=vb, Q=Q, probs=probs, psink=es / denom, O=O))
    y_attn = jnp.concatenate([hd["O"] for hd in heads], axis=1)

    rc = lax.rsqrt(jnp.mean(y_conv * y_conv, axis=-1, keepdims=True) + EPS)
    ra = lax.rsqrt(jnp.mean(y_attn * y_attn, axis=-1, keepdims=True) + EPS)
    y = jnp.concatenate([y_conv * rc * gco, y_attn * ra * gao], axis=1)
    return dict(gate_b=gate_b, gate_c=gate_c, hc=hc, z=z, z1=z1, z2=z2, cz=cz, y_conv=y_conv, y_attn=y_attn,
                rc=rc, ra=ra, heads=heads, y=y, row=row, scale=scale, q_raw=q_raw, rq=rq, k_raw=k_raw, rk=rk)


BPS = 2
TILE = BPS * BLK
KV0 = 3 * CW + AW


def _mix_in_specs(tile_of):
    return [
        pl.BlockSpec(memory_space=pltpu.SMEM),
        pl.BlockSpec((TILE, INW), lambda s: (tile_of(s), 0)),
        pl.BlockSpec((8, CW), lambda s: (jnp.maximum(tile_of(s) * (TILE // 8) - 1, 0), 1)),
        pl.BlockSpec((8, CW), lambda s: (jnp.maximum(tile_of(s) * (TILE // 8) - 1, 0), 2)),
        pl.BlockSpec((BLK, 2 * NKV * HD), lambda s: (jnp.maximum(tile_of(s) * BPS - 1, 0), KV0 // (2 * NKV * HD))),
    ]


def _block_inputs(tile, b, zc_ref, zh_ref, pkv_ref, first_tile):
    P = tile[b * BLK:(b + 1) * BLK, :]
    if b == 0:
        return P, zc_ref[...], zh_ref[...], pkv_ref[...], first_tile
    lo = b * BLK
    return P, tile[lo - 8:lo, CW:2 * CW], tile[lo - 8:lo, 2 * CW:3 * CW], tile[lo - BLK:lo, KV0:KV0 + 2 * NKV * HD], False


def _mix_param_specs():
    return [
        pl.BlockSpec((8, CW), lambda s: (0, 0)),
        pl.BlockSpec((1, AW), lambda s: (0, 0)),
        pl.BlockSpec((1, NKV * HD), lambda s: (0, 0)),
        pl.BlockSpec((1, CW), lambda s: (0, 0)),
        pl.BlockSpec((1, AW), lambda s: (0, 0)),
        pl.BlockSpec((AW, 128), lambda s: (0, 0)),
        pl.BlockSpec((NH * BLK, 2 * BLK), lambda s: (0, 0)),
    ]


def _mix_params(cw8, qg, kg, gco, gao, bias):
    seg = np.zeros((AW, 128), np.float32)
    seg[np.arange(AW), np.arange(AW) // HD] = 1.0
    return (cw8, jnp.tile(qg, (1, NH)), jnp.tile(kg, (1, NKV)), gco, gao, jnp.asarray(seg, BF), bias)


def _mix_fwd(proj, sinks, cw8, qg, kg, gco, gao, bias, comm=()):
    def body(sink_ref, p_ref, zc_ref, zh_ref, pkv_ref, cw_ref, qg_ref, kg_ref, gco_ref, gao_ref, seg_ref, bias_ref, y_ref):
        tile = p_ref[...]
        for b in range(BPS):
            f = _mix_forward(*_block_inputs(tile, b, zc_ref, zh_ref, pkv_ref, pl.program_id(0) == 0), cw_ref[...],
                             qg_ref[...], kg_ref[...], gco_ref[...], gao_ref[...], seg_ref[...], sink_ref, bias_ref, by_head=True)
            y_ref[b * BLK:(b + 1) * BLK, :] = f["y"].astype(BF)

    return _call(
        body, (sinks, proj, proj, proj, proj, *_mix_params(cw8, qg, kg, gco, gao, bias)), name="mix_fwd", grid=(T // TILE,),
        in_specs=_mix_in_specs(lambda s: s) + _mix_param_specs(),
        out_specs=[pl.BlockSpec((TILE, D), lambda s: (s, 0))], out_shape=[SDS((T, D), BF)],
        sem=("parallel",), vmem_mib=40, comm=comm, free=tuple(range(5, 12)))


def _mix_bwd(proj, dy, sinks, cw8, qg, kg, gco, gao, bias, comm=()):
    n_steps = T // TILE

    def tile_of(s):
        return n_steps - 1 - s

    def body(sink_ref, p_ref, zc_ref, zh_ref, pkv_ref, dy_ref, cw_ref, qg_ref, kg_ref, gco_ref, gao_ref, seg_ref, bias_ref,
             dproj_ref, dcw_ref, dqg_ref, dkg_ref, dgco_ref, dgao_ref, dsink_ref, dbias_ref,
             ndcz_ref, dkc_ref, dvc_ref):
        s = pl.program_id(0)

        @pl.when(s == 0)
        def _():
            for r in (dcw_ref, dqg_ref, dkg_ref, dgco_ref, dgao_ref, dsink_ref, dbias_ref, ndcz_ref, dkc_ref, dvc_ref):
                r[...] = jnp.zeros_like(r)

        params = (cw_ref[...], qg_ref[...], kg_ref[...], gco_ref[...], gao_ref[...], seg_ref[...])
        tile = p_ref[...]
        carry = (ndcz_ref[...], dkc_ref[...], dvc_ref[...])
        total = None
        for b in reversed(range(BPS)):
            f = _mix_forward(*_block_inputs(tile, b, zc_ref, zh_ref, pkv_ref, s == n_steps - 1), *params, sink_ref, bias_ref)
            pieces, sums, carry = one_block(f, dy_ref[b * BLK:(b + 1) * BLK, :], params, carry)
            for lo, piece in pieces:
                dproj_ref[b * BLK:(b + 1) * BLK, lo:lo + piece.shape[1]] = piece
            total = sums if total is None else [t + v for t, v in zip(total, sums)]
        ndcz_ref[...], dkc_ref[...], dvc_ref[...] = carry
        dcw, dqg_t, dkg_t, dgco, dgao, dsink, *ds = total
        dcw_ref[0:3, :] += dcw
        dqg_ref[...] += _fold_heads(dqg_t)
        dkg_ref[...] += _fold_heads(dkg_t)
        dgco_ref[...] += dgco
        dgao_ref[...] += dgao
        dsink_ref[...] += dsink
        for h in range(NH):
            dbias_ref[h * BLK:(h + 1) * BLK, :] += ds[h]

    def one_block(f, dy, params, carry):
        cw, qg_v, kg_v, gco_v, gao_v, seg = params
        nxt, dk_carry, dv_carry = carry
        dyc, dgco = _rms_bwd(dy[:, 0:CW], f["y_conv"], f["rc"], gco_v)
        dya, dgao = _rms_bwd(dy[:, CW:CW + AW], f["y_attn"], f["ra"], gao_v)

        row = f["row"]
        dgate_b = dyc * f["cz"]
        dcz = dyc * f["gate_b"]
        dcw = jnp.concatenate([jnp.sum(dcz * f[k], axis=0, keepdims=True) for k in ("z2", "z1", "z")], axis=0)
        n0 = nxt[0:1, :]
        n1 = nxt[1:2, :]
        d1 = jnp.where(row == BLK - 1, n0, pltpu.roll(dcz, BLK - 1, 0))
        d2 = jnp.where(row == BLK - 1, n1, jnp.where(row == BLK - 2, n0, pltpu.roll(dcz, BLK - 2, 0)))
        dz = cw[2:3, :] * dcz + cw[1:2, :] * d1 + cw[0:1, :] * d2
        pieces = [(0, dgate_b.astype(BF)), (CW, (dz * f["hc"]).astype(BF)), (2 * CW, (dz * f["gate_c"]).astype(BF))]

        scale = f["scale"]
        lane = lax.broadcasted_iota(jnp.int32, (1, 128), 1)
        dsink = jnp.zeros((1, 128), F32)
        dq_cols, dk_cols, dv_cols, dk_prev, dv_prev, ds = [], [], [], [], [], []
        for kv in range(NKV):
            dKb = dVb = 0.0
            for h in range(kv * GQ, (kv + 1) * GQ):
                hd = f["heads"][h]
                dO = dya[:, h * HD:(h + 1) * HD]
                delta = jnp.sum(dO * hd["O"], axis=-1, keepdims=True)
                dOb = dO.astype(BF)
                dP = _dot(dOb, hd["vb"], 1, 1)
                dS = hd["probs"] * (dP - delta)
                tot = jnp.sum(hd["psink"] * delta, axis=0, keepdims=True)
                dsink = dsink - jnp.where(lane == h, tot, 0.0)
                ds.append(dS)
                dSb = dS.astype(BF)
                dq_cols.append(_dot(dSb, hd["kb"], 1, 0))
                dKb = dKb + _dot(dSb, hd["Q"], 0, 0)
                dVb = dVb + _dot(hd["probs"].astype(BF), dOb, 0, 0)
            dk_cols.append(dKb[BLK:, :] + dk_carry[:, kv * HD:(kv + 1) * HD])
            dv_cols.append(dVb[BLK:, :] + dv_carry[:, kv * HD:(kv + 1) * HD])
            dk_prev.append(dKb[:BLK, :])
            dv_prev.append(dVb[:BLK, :])
        dq_raw, dqg_t = _head_norm_bwd(jnp.concatenate(dq_cols, axis=1) * scale, f["q_raw"], f["rq"], qg_v, seg)
        dk_raw, dkg_t = _head_norm_bwd(jnp.concatenate(dk_cols, axis=1), f["k_raw"][BLK:, :], f["rk"][BLK:, :], kg_v, seg)
        pieces.append((3 * CW, jnp.concatenate([dq_raw, dk_raw] + dv_cols, axis=1).astype(BF)))
        owed = (dcz[0:8, :], jnp.concatenate(dk_prev, axis=1), jnp.concatenate(dv_prev, axis=1))
        return pieces, [dcw, dqg_t, dkg_t, dgco, dgao, dsink, *ds], owed

    small = lambda r, c: pl.BlockSpec((r, c), lambda s: (0, 0))
    return _call(
        body, (sinks, proj, proj, proj, proj, dy, *_mix_params(cw8, qg, kg, gco, gao, bias)), name="mix_bwd", grid=(n_steps,),
        in_specs=_mix_in_specs(tile_of) + [pl.BlockSpec((TILE, D), lambda s: (tile_of(s), 0))] + _mix_param_specs(),
        out_specs=[pl.BlockSpec((TILE, INW), lambda s: (tile_of(s), 0)), small(8, CW), small(1, HD), small(1, HD),
                   small(1, CW), small(1, AW), small(1, 128), small(NH * BLK, 2 * BLK)],
        out_shape=[SDS((T, INW), BF), SDS((8, CW), F32), SDS((1, HD), F32), SDS((1, HD), F32), SDS((1, CW), F32),
                   SDS((1, AW), F32), SDS((1, 128), F32), SDS((NH * BLK, 2 * BLK), F32)],
        scratch_shapes=[pltpu.VMEM((8, CW), F32), pltpu.VMEM((BLK, NKV * HD), F32), pltpu.VMEM((BLK, NKV * HD), F32)],
        sem=("arbitrary",), vmem_mib=56, comm=comm, free=(1, 2, 3, 4) + tuple(range(6, 13)))


FT = 256
NFT = DFF // FT
RC = 1024
NCH = T // RC
LEAD = 16


def _rows8(x):
    return jnp.sum(x.reshape(x.shape[0] // 8, 8, x.shape[1]), axis=0)


def _ffn_act_specs():
    return [
        pl.BlockSpec((T, FT), lambda j: (0, j)), pl.BlockSpec((T, FT), lambda j: (0, NFT + j)),
        pl.BlockSpec((8, FT), lambda j: (0, j)), pl.BlockSpec((8, FT), lambda j: (0, NFT + j)),
        pl.BlockSpec((1, FT), lambda j: (0, j)), pl.BlockSpec((1, FT), lambda j: (0, NFT + j)),
    ]


def _conv_rows(win, w, b, n):
    win = win.astype(F32)
    u = win[LEAD:LEAD + n]
    u1 = pltpu.roll(win, 1, 0)[LEAD:LEAD + n]
    u2 = pltpu.roll(win, 2, 0)[LEAD:LEAD + n]
    return u2, u1, u, w[0:1, :] * u2 + w[1:2, :] * u1 + w[2:3, :] * u + b


def _ffn_act(up, fw8, fb, comm=()):
    def body(ug_ref, uv_ref, wg_ref, wv_ref, bg_ref, bv_ref, a_ref, pg_ref, pv_ref):
        wg, wv, bg, bv = wg_ref[...], wv_ref[...], bg_ref[...], bv_ref[...]

        def chunk(rows, win_g, win_v):
            gp = _conv_rows(win_g, wg, bg, RC)[3]
            vp = _conv_rows(win_v, wv, bv, RC)[3]
            a_ref[rows, :] = (gp * jax.nn.sigmoid(gp) * vp).astype(BF)
            pg_ref[rows, :] = gp.astype(BF)
            pv_ref[rows, :] = vp.astype(BF)

        zero = jnp.zeros((LEAD, FT), BF)
        chunk(pl.ds(0, RC), jnp.concatenate([zero, ug_ref[0:RC, :]], axis=0), jnp.concatenate([zero, uv_ref[0:RC, :]], axis=0))

        def step(i, carry):
            r0 = pl.multiple_of(i * RC, RC)
            win = pl.ds(r0 - LEAD, RC + LEAD)
            chunk(pl.ds(r0, RC), ug_ref[win, :], uv_ref[win, :])
            return carry

        lax.fori_loop(1, NCH, step, 0)

    col = pl.BlockSpec((T, FT), lambda j: (0, j))
    return _call(
        body, (up, up, fw8, fw8, fb, fb), name="ffn_act", grid=(NFT,), in_specs=_ffn_act_specs(),
        out_specs=[col, col, col], out_shape=[SDS((T, DFF), BF)] * 3,
        sem=("parallel",), vmem_mib=40, comm=comm, free=(2, 3, 4, 5))


def _ffn_act_bwd(up, pre_g, pre_v, da, fw8, comm=()):
    ext = RC + LEAD

    def body(ug_ref, uv_ref, wg_ref, wv_ref, pg_ref, pv_ref, da_ref, dug_ref, duv_ref, dwg_ref, dwv_ref, dbg_ref, dbv_ref):
        wg, wv = wg_ref[...], wv_ref[...]

        def chunk(u_g, u_v, gp, vp, da_e):
            gp, vp, da_e = gp.astype(F32), vp.astype(F32), da_e.astype(F32)
            sig = jax.nn.sigmoid(gp)
            dvp = da_e * (gp * sig)
            dgp = da_e * vp * (sig * (1.0 + gp * (1.0 - sig)))

            def branch(dp, w, u):
                d0, d1, d2 = dp[0:RC], pltpu.roll(dp, ext - 1, 0)[0:RC], pltpu.roll(dp, ext - 2, 0)[0:RC]
                du = (w[2:3, :] * d0 + w[1:2, :] * d1 + w[0:1, :] * d2).astype(BF)
                u = u.astype(F32)
                return du, [_rows8(d0), _rows8(d2 * u), _rows8(d1 * u), _rows8(d0 * u)]

            dug, sums_g = branch(dgp, wg, u_g)
            duv, sums_v = branch(dvp, wv, u_v)
            return dug, duv, sums_g + sums_v

        def step(i, acc):
            r0 = pl.multiple_of(i * RC, RC)
            rows, more = pl.ds(r0, RC), pl.ds(r0, ext)
            dug, duv, part = chunk(ug_ref[rows, :], uv_ref[rows, :], pg_ref[more, :], pv_ref[more, :], da_ref[more, :])
            dug_ref[rows, :] = dug
            duv_ref[rows, :] = duv
            return [a + p for a, p in zip(acc, part)]

        acc = lax.fori_loop(0, NCH - 1, step, [jnp.zeros((8, FT), F32)] * 8)
        r0 = T - RC
        zero = jnp.zeros((LEAD, FT), BF)
        tail = lambda ref: jnp.concatenate([ref[r0:T, :], zero], axis=0)
        dug, duv, part = chunk(ug_ref[r0:T, :], uv_ref[r0:T, :], tail(pg_ref), tail(pv_ref), tail(da_ref))
        dug_ref[r0:T, :] = dug
        duv_ref[r0:T, :] = duv
        tot = [jnp.sum(a + p, axis=0, keepdims=True) for a, p in zip(acc, part)]
        for k, (dw_ref, db_ref) in enumerate(((dwg_ref, dbg_ref), (dwv_ref, dbv_ref))):
            db_ref[...] = tot[4 * k]
            dw_ref[...] = jnp.zeros_like(dw_ref)
            for r in range(3):
                dw_ref[r:r + 1, :] = tot[4 * k + 1 + r]

    col = lambda r: pl.BlockSpec((r, FT), lambda j: (0, j))
    return _call(
        body, (up, up, fw8, fw8, pre_g, pre_v, da), name="ffn_act_bwd", grid=(NFT,),
        in_specs=_ffn_act_specs()[0:4] + [col(T), col(T), col(T)],
        out_specs=[col(T), col(T), col(8), col(8), col(1), col(1)],
        out_shape=[SDS((T, DFF), BF), SDS((T, DFF), BF), SDS((8, DFF), F32), SDS((8, DFF), F32),
                   SDS((1, DFF), F32), SDS((1, DFF), F32)],
        sem=("parallel",), vmem_mib=40, comm=comm, free=(0, 1, 2, 3))


def _ffn_down_bwd(dh2b, w_down, comm=()):
    tm = TM

    def body(d_ref, w_ref, o_ref):
        o_ref[...] = _dot(d_ref[...], w_ref[...], 1, 1).astype(BF)

    return _call(
        body, (dh2b, w_down), name="ffn_down_bwd", grid=(T // tm,),
        in_specs=[pl.BlockSpec((tm, D), lambda i: (i, 0)), _resident((DFF, D))],
        out_specs=[pl.BlockSpec((tm, DFF), lambda i: (i, 0))], out_shape=[SDS((T, DFF), BF)],
        sem=("parallel",), vmem_mib=40, comm=comm, free=(0, 1))


def _norm_matmul_bwd(name, a_list, w_t, k_offsets, xin, g, dres, want_bf16, comm=(), slot=None):
    tm = TM
    ks = [a.shape[1] for a in a_list]
    n_a = len(a_list)
    n_pre = 0 if slot is None else 1

    def body(*refs):
        refs = refs[n_pre:]
        a_refs = refs[:n_a]
        w_ref, x_ref, g_ref, r_ref = refs[n_a:n_a + 4]
        outs = refs[n_a + 4:]
        dx_ref, dg_ref = outs[0], (outs[-1] if slot is None else outs[-1].at[0])

        @pl.when(pl.program_id(0) == 0)
        def _():
            dg_ref[...] = jnp.zeros_like(dg_ref)

        du = _dot(a_refs[0][...], w_ref[k_offsets[0]:k_offsets[0] + ks[0], :], 1, 0)
        for k in range(1, n_a):
            du = du + _dot(a_refs[k][...], w_ref[k_offsets[k]:k_offsets[k] + ks[k], :], 1, 0)
        x = x_ref[...]
        r = lax.rsqrt(jnp.mean(x * x, axis=-1, keepdims=True) + EPS)
        dx, dg = _rms_bwd(du, x, r, g_ref[...])
        dx = r_ref[...] + dx
        dx_ref[...] = dx
        if want_bf16:
            outs[1][...] = dx.astype(BF)
        dg_ref[...] += dg

    tile = lambda c: pl.BlockSpec((tm, c), lambda i, *_: (i, 0))
    if slot is None:
        dg_spec, dg_shape = pl.BlockSpec((1, D), lambda i: (0, 0)), SDS((1, D), F32)
    else:
        dg_spec, dg_shape = pl.BlockSpec((1, 1, D), lambda i, slot_ref: (slot_ref[0], 0, 0)), SDS((N_DEV, 1, D), F32)
    out_specs = [tile(D)] + ([tile(D)] if want_bf16 else []) + [dg_spec]
    out_shape = [SDS((T, D), F32)] + ([SDS((T, D), BF)] if want_bf16 else []) + [dg_shape]
    return _call(
        body, (*a_list, w_t, xin, g, dres), name=name, grid=(T // tm,), prefetch=() if slot is None else (slot,),
        in_specs=[tile(k) for k in ks] + [_resident(w_t.shape), tile(D),
                                           pl.BlockSpec((1, D), lambda i, *_: (0, 0)), tile(D)],
        out_specs=out_specs, out_shape=out_shape, sem=("arbitrary",), vmem_mib=56, comm=comm, free=tuple(range(n_a + 4)))


def _out_bwd(dh1b, w_out, comm=()):
    tm = TM

    def body(d_ref, w_ref, o_ref):
        o_ref[...] = _dot(d_ref[...], w_ref[...], 1, 1)

    return _call(
        body, (dh1b, w_out), name="out_bwd", grid=(T // tm,),
        in_specs=[pl.BlockSpec((tm, D), lambda i: (i, 0)), _resident((D, D))],
        out_specs=[pl.BlockSpec((tm, D), lambda i: (i, 0))], out_shape=[SDS((T, D), F32)],
        sem=("parallel",), vmem_mib=32, comm=comm, free=(0, 1))


def _wgrad(name, a_list, b, old_a, comm=()):
    m_k = a_list[0].shape[1]
    tm = max(t for t in range(128, m_k // 2 + 1, 128) if m_k % t == 0)
    steps = [a.shape[1] // tm for a in a_list]
    starts = [sum(steps[:k]) for k in range(len(a_list))]
    n_a = len(a_list)

    def body(*refs):
        a_refs, b_ref, o_ref = refs[:n_a], refs[n_a], refs[n_a + 1]
        i = pl.program_id(0)
        for k in range(n_a):
            @pl.when((i >= starts[k]) & (i < starts[k] + steps[k]))
            def _(k=k):
                o_ref[...] = _dot(a_refs[k][...], b_ref[...], 0, 0).astype(BF)

    def a_spec(k):
        return pl.BlockSpec((T, tm), lambda i: (0, jnp.clip(i - starts[k], 0, steps[k] - 1)))

    m_total = tm * sum(steps)
    return _call(
        body, (*a_list, b), name=name, grid=(sum(steps),),
        in_specs=[a_spec(k) for k in range(n_a)] + [_resident((T, D))],
        out_specs=[pl.BlockSpec((tm, D), lambda i: (i, 0))], out_shape=[SDS((m_total, D), BF)],
        sem=("parallel",), vmem_mib=40, comm=comm, free=() if old_a is None else tuple(range(n_a)) if old_a else (n_a,))


def _chip_sum(name, gbf, from_sib, core, chip):
    h = gbf.shape[1]
    th = h

    def body(core_ref, chip_ref, g_ref, s_ref, pbf_ref, own_ref):
        p = g_ref[0].astype(F32) + s_ref[0].astype(F32)
        pbf_ref[0] = p.astype(BF)

        @pl.when(pl.program_id(1) == chip_ref[0])
        def _():
            own_ref[...] = p

    grid_spec = pltpu.PrefetchScalarGridSpec(
        num_scalar_prefetch=2, grid=(h // th, N_CHIPS),
        in_specs=[pl.BlockSpec((1, th, D), lambda t, jj, core_ref, chip_ref: (2 * jj + core_ref[0], t, 0)),
                  pl.BlockSpec((1, th, D), lambda t, jj, core_ref, chip_ref: (jj, t, 0))],
        out_specs=[pl.BlockSpec((1, th, D), lambda t, jj, core_ref, chip_ref: (jj, t, 0)),
                   pl.BlockSpec((th, D), lambda t, jj, core_ref, chip_ref: (t, 0))],
    )
    return _pcall(
        body, name=name, grid_spec=grid_spec, out_shape=_in_hbm([SDS((N_CHIPS, h, D), BF), SDS((h, D), F32)]),
        compiler_params=_params(("arbitrary", "arbitrary"), 32),
    )(core, chip, *_from_hbm(gbf, from_sib))


def _final_sum(name, own, from_chips, core, comm=()):
    h = own.shape[0]
    n = 4 if h % (4 * ROWS16) == 0 else 2
    th = h // n

    def body(core_ref, o_ref, r_ref, f_ref):
        f_ref[0] = ((o_ref[...] + r_ref[0].astype(F32)) + r_ref[1].astype(F32)) + r_ref[2].astype(F32)

    return _call(
        body, (own, from_chips), name=name, grid=(n,), prefetch=(core,),
        in_specs=[pl.BlockSpec((th, D), lambda i, core_ref: (i, 0)), pl.BlockSpec((3, th, D), lambda i, core_ref: (0, i, 0))],
        out_specs=[pl.BlockSpec((1, th, D), lambda i, core_ref: (core_ref[0], i, 0))], out_shape=[SDS((2, h, D), F32)],
        sem=("arbitrary",), vmem_mib=40, comm=comm)


def _adam_math(w, g, m, v):
    nm = ADAM_B1 * m + (1.0 - ADAM_B1) * g
    nv = ADAM_B2 * v + (1.0 - ADAM_B2) * (g * g)
    m_hat = nm / (1.0 - ADAM_B1 ** ADAM_STEP)
    v_hat = nv / (1.0 - ADAM_B2 ** ADAM_STEP)
    return -ADAM_LR * (m_hat / (jnp.sqrt(v_hat) + ADAM_EPS) + ADAM_WD * w), nm, nv


def _adamw(name, w, g, m, v, tr, copy_g=False, stage=True, g_transposed=False):
    rows, cols = w.shape

    def body(w_ref, g_ref, m_ref, v_ref, *outs):
        d_ref, nm_ref, nv_ref = outs[-3:]
        for c in [pl.ds(c0, 128) for c0 in range(0, cols, 128)] if g_transposed else [slice(None)]:
            g_val = g_ref[c, :].T if g_transposed else g_ref[...]
            if copy_g:
                outs[0][:, c] = g_val
            d_ref[:, c], nm_ref[:, c], nv_ref[:, c] = _adam_math(w_ref[:, c], g_val, m_ref[:, c], v_ref[:, c])

    spec = pl.BlockSpec((tr, cols), lambda i: (i, 0))
    n_out = 4 if copy_g else 3
    g_spec = pl.BlockSpec((cols, tr), lambda i: (0, i)) if g_transposed else spec
    return _call(body, (w, g, m, v), name=name, grid=(rows // tr,), in_specs=[spec, g_spec, spec, spec], out_specs=[spec] * n_out,
                 out_shape=[SDS((rows, cols), F32)] * n_out, sem=("parallel",), vmem_mib=32,
                 free=(0, 2, 3) if stage else ())


C_SQ = 2 * DFF
P_W = C_SQ + 128
R_G2, R_GO, R_DCW, R_QK = 0, 1, 2, 5
C_GCO, C_GAO, C_DQG, C_DKG, C_SINK = 0, CW, 0, 128, 256


def _pack(name, me, ins, width, fill):
    def body(me_ref, *refs):
        o = refs[-1]
        o[...] = jnp.zeros_like(o)
        fill(o, *refs[:-1])

    return _call(body, ins, name=name, grid=(1,), prefetch=(me,),
                 in_specs=[pl.BlockSpec(a.shape, lambda i, me_ref: (0, 0)) for a in ins],
                 out_specs=[pl.BlockSpec((1, 8, width), lambda i, me_ref: (me_ref[0], 0, 0))],
                 out_shape=[SDS((N_DEV, 8, width), F32)], sem=("arbitrary",))[0]


def _pack_ffn(me, dfwg, dfwv, dfbg, dfbv, sq):
    def fill(o, dfwg_r, dfwv_r, dfbg_r, dfbv_r, sq_r):
        o[0, :, 0:DFF] = dfwg_r[...]
        o[0, :, DFF:2 * DFF] = dfwv_r[...]
        o[0, 3:4, 0:DFF] = dfbg_r[...]
        o[0, 3:4, DFF:2 * DFF] = dfbv_r[...]
        o[0, :, C_SQ:C_SQ + 128] = sq_r[...]

    return _pack("pack_ffn", me, (dfwg, dfwv, dfbg, dfbv, sq), P_W, fill)


def _pack_mix(me, dg2, dgco, dgao, dcw8, dqg, dkg, dsink):
    def fill(o, dg2_r, dgco_r, dgao_r, dcw_r, dqg_r, dkg_r, dsink_r):
        o[0, R_G2:R_G2 + 1, :] = dg2_r[...]
        o[0, R_GO:R_GO + 1, C_GCO:C_GCO + CW] = dgco_r[...]
        o[0, R_GO:R_GO + 1, C_GAO:C_GAO + AW] = dgao_r[...]
        o[0, R_DCW:R_DCW + 3, 0:CW] = dcw_r[0:3, :]
        o[0, R_QK:R_QK + 1, C_DQG:C_DQG + HD] = dqg_r[...]
        o[0, R_QK:R_QK + 1, C_DKG:C_DKG + HD] = dkg_r[...]
        o[0, R_QK:R_QK + 1, C_SINK:C_SINK + 128] = dsink_r[...]

    return _pack("pack_mix", me, (dg2, dgco, dgao, dcw8, dqg, dkg, dsink), D, fill)


N_SMALL = 11


def _small_adam(chip, p_all, pm_all, g1_all, tbl_all, ws, ms, vs):
    fw_cols = 2 * DFF // N_CHIPS
    cw_cols = CW // N_CHIPS

    def body(chip_ref, p_ref, fw_ref, pm_ref, cw_ref, g1_ref, tbl_ref, *refs):
        w_r, m_r, v_r = refs[0:N_SMALL], refs[N_SMALL:2 * N_SMALL], refs[2 * N_SMALL:3 * N_SMALL]
        outs = refs[3 * N_SMALL:]
        g_o, d_o, nm_o, nv_o = (outs[k * N_SMALL:(k + 1) * N_SMALL] for k in range(4))
        loss_o = outs[4 * N_SMALL]

        def total(ref):
            s = ref[0]
            for k in range(1, N_DEV):
                s = s + ref[k]
            return s

        S = total(p_ref)
        fw = total(fw_ref)
        M = total(pm_ref)
        cw = total(cw_ref)

        def step(i, g, at):
            d, nm, nv = _adam_math(w_r[i][at], g, m_r[i][at], v_r[i][at])
            g_o[i][at], d_o[i][at], nm_o[i][at], nv_o[i][at] = g, d, nm, nv

        everything = (slice(None), slice(None))
        step(0, total(g1_ref), everything)
        for r in range(3):
            step(1, cw[R_DCW + r:R_DCW + r + 1, :], (r, slice(None), slice(None)))
        step(2, M[R_QK:R_QK + 1, C_DQG:C_DQG + HD], everything)
        step(3, M[R_QK:R_QK + 1, C_DKG:C_DKG + HD], everything)
        step(4, total(tbl_ref), everything)
        step(5, M[R_QK:R_QK + 1, C_SINK:C_SINK + NH], everything)
        step(6, M[R_GO:R_GO + 1, C_GCO:C_GCO + CW], everything)
        step(7, M[R_GO:R_GO + 1, C_GAO:C_GAO + AW], everything)
        step(8, M[R_G2:R_G2 + 1, :], everything)
        for r in range(3):
            step(9, fw[r:r + 1, :], (r, slice(None), slice(None)))
        step(10, S[3:4, 0:2 * DFF], everything)
        sq = S[:, C_SQ:C_SQ + 128]
        loss_o[...] = jnp.sum(jnp.sum(sq, axis=1, keepdims=True), axis=0, keepdims=True) * (0.5 / D)

    def full(a):
        n = len(a.shape)
        return pl.BlockSpec(a.shape, lambda i, chip_ref: (0,) * n)

    params = [*ws, *ms, *vs]
    out = _call(
        body, (p_all, p_all, pm_all, pm_all, g1_all, tbl_all, *params), name="small_adam", grid=(1,), prefetch=(chip,),
        in_specs=[full(p_all),
                  pl.BlockSpec((N_DEV, 8, fw_cols), lambda i, chip_ref: (0, 0, chip_ref[0])),
                  full(pm_all),
                  pl.BlockSpec((N_DEV, 8, cw_cols), lambda i, chip_ref: (0, 0, chip_ref[0])),
                  full(g1_all), full(tbl_all), *[full(a) for a in params]],
        out_specs=[full(a) for a in ws] * 4 + [pl.BlockSpec((1, 1), lambda i, chip_ref: (0, 0))],
        out_shape=[SDS(a.shape, F32) for a in ws] * 4 + [SDS((1, 1), F32)], sem=("arbitrary",), vmem_mib=32)
    return out[0:N_SMALL], out[N_SMALL:2 * N_SMALL], out[2 * N_SMALL:3 * N_SMALL], out[3 * N_SMALL:4 * N_SMALL], out[4 * N_SMALL]


PLACE_STEPS = 4


def _place_specs(shards):
    rows = [s.shape[0] // PLACE_STEPS for s in shards]
    return ([pl.BlockSpec((r, D), lambda i, chip_ref: (i, 0)) for r in rows],
            [pl.BlockSpec((r, D), lambda i, chip_ref: (chip_ref[0] * PLACE_STEPS + i, 0)) for r in rows],
            [SDS((N_CHIPS * s.shape[0], D), BF) for s in shards])


def _place_first(chip, shard, conv_w, ffn_conv_w):
    def body(chip_ref, a, s0, s1, o, t0, t1):
        o[...] = a[...].astype(BF)

        @pl.when(pl.program_id(0) == 0)
        def _():
            for s, t in ((s0, t0), (s1, t1)):
                t[...] = jnp.zeros_like(t)
                t[0, 0:3, :] = s[...]

    ins, outs, shapes = _place_specs([shard])
    taps = (conv_w, ffn_conv_w)
    return _call(
        body, (shard, conv_w, ffn_conv_w), name="place_first", grid=(PLACE_STEPS,), prefetch=(chip,),
        in_specs=ins + [pl.BlockSpec(s.shape, lambda i, chip_ref: (0, 0)) for s in taps],
        out_specs=outs + [pl.BlockSpec((1, 8, s.shape[1]), lambda i, chip_ref: (chip_ref[0], 0, 0)) for s in taps],
        out_shape=shapes + [SDS((N_CHIPS, 8, s.shape[1]), F32) for s in taps],
        sem=("arbitrary",), vmem_mib=32, free=(1, 2))


def _place_rest(chip, shards, w_up, table, bucket, comm):
    n = len(shards)
    c_up = w_up.shape[1]
    edges = [round(k * (c_up // 128) / PLACE_STEPS) * 128 for k in range(PLACE_STEPS + 1)]

    def body(chip_ref, *refs):
        a, (up_ref, tab_ref, bk_ref), o = refs[:n], refs[n:n + 3], refs[n + 3:2 * n + 3]
        up_o, bias_ref = refs[2 * n + 3:]
        for src, dst in zip(a, o):
            dst[...] = src[...].astype(BF)
        for k in range(PLACE_STEPS):
            @pl.when(pl.program_id(0) == k)
            def _(k=k):
                up_o[edges[k]:edges[k + 1], :] = up_ref[:, edges[k]:edges[k + 1]].T.astype(BF)

        @pl.when(pl.program_id(0) == 0)
        def _():
            bk = bk_ref[...]
            eq = [bk == b for b in range(NBUCKET)]
            for h in range(NH):
                acc = jnp.zeros((BLK, 2 * BLK), F32)
                for b in range(NBUCKET):
                    acc = jnp.where(eq[b], tab_ref[h, b], acc)
                bias_ref[h * BLK:(h + 1) * BLK, :] = acc

    ins, outs, shapes = _place_specs(shards)
    return _call(
        body, (*shards, w_up, table, bucket), name="place_rest", grid=(PLACE_STEPS,), prefetch=(chip,),
        in_specs=ins + [_resident(w_up.shape), pl.BlockSpec(memory_space=pltpu.SMEM),
                        pl.BlockSpec(bucket.shape, lambda i, chip_ref: (0, 0))],
        out_specs=outs + [pl.BlockSpec((c_up, D), lambda i, chip_ref: (chip_ref[0], 0)),
                          pl.BlockSpec((NH * BLK, 2 * BLK), lambda i, chip_ref: (0, 0))],
        out_shape=shapes + [SDS((N_CHIPS * c_up, D), BF), SDS((NH * BLK, 2 * BLK), F32)],
        sem=("arbitrary",), vmem_mib=32, comm=comm, free=(n + 1, n + 2))


def kernel(x, norm_mix_g, w_in, conv_w, q_norm_g, k_norm_g, rel_bias_table, sinks, out_norm_conv_g, out_norm_attn_g, w_out, norm_ffn_g, w_up, ffn_conv_w, ffn_conv_b, w_down, loss_target, m_norm_mix_g, m_w_in, m_conv_w, m_q_norm_g, m_k_norm_g, m_rel_bias_table, m_sinks, m_out_norm_conv_g, m_out_norm_attn_g, m_w_out, m_norm_ffn_g, m_w_up, m_ffn_conv_w, m_ffn_conv_b, m_w_down, v_norm_mix_g, v_w_in, v_conv_w, v_q_norm_g, v_k_norm_g, v_rel_bias_table, v_sinks, v_out_norm_conv_g, v_out_norm_attn_g, v_w_out, v_norm_ffn_g, v_w_up, v_ffn_conv_w, v_ffn_conv_b, v_w_down):
    as_arg = lambda i: jnp.reshape(i, (1,)).astype(jnp.int32)
    chip = as_arg(2 * lax.axis_index("x") + lax.axis_index("y"))
    core = as_arg(lax.axis_index("c"))
    me = 2 * chip + core
    xs, tgt = x[0], loss_target[0]
    qg, kg, gco, gao, g1, g2, fb = q_norm_g, k_norm_g, out_norm_conv_g, out_norm_attn_g, norm_mix_g, norm_ffn_g, ffn_conv_b
    pieces = lambda g: g.reshape(N_DEV, g.shape[0] // N_DEV, D)
    whole = lambda f: f.reshape(2 * f.shape[1], D)

    bucket = jnp.asarray(_bucket_table())
    p_in, p_cw, p_fw = _place_first(chip, w_in[0].T, conv_w[0], ffn_conv_w[0])
    p_out, p_down, p_up, bias, w_int, cw_all, fw_all = _place_rest(
        chip, [w_out[0], w_down[0]], w_up[0], rel_bias_table.T, bucket,
        comm=[_t_gather(p_in, relayed_first=True), _t_small_weights(p_cw), _t_small_weights(p_fw)])
    cw8 = jnp.transpose(cw_all, (1, 0, 2)).reshape(8, CW)
    fw8 = jnp.transpose(fw_all, (1, 0, 2)).reshape(8, 2 * DFF)

    early = 3 / 11
    proj, u1, w_out_f, p_up = _inproj(xs, g1, w_int, comm=[_t_gather(p_out), _t_gather(p_up, (0, early))])
    y, w_upt = _mix_fwd(proj, sinks, cw8, qg, kg, gco, gao, bias, comm=[_t_gather(p_up, (early, 1))])
    h1, u2 = _outproj(y, w_out_f, xs, g2)
    up, w_down_f = _ffn_up(u2, w_upt, comm=[_t_gather(p_down)])
    a, pre_g, pre_v = _ffn_act(up, fw8, fb)
    dh2, dh2b, sq = _ffn_down(a, w_down_f, h1, tgt)

    gdbf, = _wgrad("wgrad_down", [a], dh2b, None)
    da, sib_down = _ffn_down_bwd(dh2b, w_down_f, comm=[_t_sibling(pieces(gdbf))])
    pbf_down, own_down = _chip_sum("chip_sum_w_down", pieces(gdbf), sib_down, core, chip)
    dug, duv, dfwg, dfwv, dfbg, dfbv, chips_down = _ffn_act_bwd(up, pre_g, pre_v, da, fw8, comm=[_t_chips(pbf_down)])
    fin_down, = _final_sum("final_sum_w_down", own_down, chips_down, core)
    gubf, = _wgrad("wgrad_up", [dug, duv], u2, False)
    p_all = _pack_ffn(me, dfwg, dfwv, dfbg, dfbv, sq)
    dh1, dh1b, dg2, sib_up, fin_down, p_all = _norm_matmul_bwd(
        "ffn_up_bwd", [dug, duv], w_upt, [0, DFF], h1, g2, dh2, True,
        comm=[_t_sibling(pieces(gubf)), _t_swap(fin_down), _t_allgather(p_all)])
    pbf_up, own_up = _chip_sum("chip_sum_w_up", pieces(gubf), sib_up, core, chip)
    gobf, = _wgrad("wgrad_out", [y], dh1b, True)
    dy, sib_out = _out_bwd(dh1b, w_out_f, comm=[_t_sibling(pieces(gobf))])
    pbf_out, own_out = _chip_sum("chip_sum_w_out", pieces(gobf), sib_out, core, chip)
    dproj, dcw8, dqg, dkg, dgco, dgao, dsink, dbias, chips_up = _mix_bwd(
        proj, dy, sinks, cw8, qg, kg, gco, gao, bias, comm=[_t_chips(pbf_up)])
    fin_up, = _final_sum("final_sum_w_up", own_up, chips_up, core)
    tbl_all = _band_bias_bwd(dbias, bucket, me)
    pm_all = _pack_mix(me, dg2, dgco, dgao, dcw8, dqg, dkg, dsink)
    gibf, chips_out, fin_up, pm_all, tbl_all = _wgrad(
        "wgrad_in", [dproj], u1, False,
        comm=[_t_chips(pbf_out), _t_swap(fin_up), _t_allgather(pm_all), _t_allgather(tbl_all)])
    fin_out, sib_in = _final_sum("final_sum_w_out", own_out, chips_out, core, comm=[_t_sibling(pieces(gibf))])
    pbf_in, own_in = _chip_sum("chip_sum_w_in", pieces(gibf), sib_in, core, chip)
    dx, g1_all, chips_in, fin_out = _norm_matmul_bwd(
        "in_bwd", [dproj], w_int, [0], xs, g1, dh1, False, comm=[_t_chips(pbf_in), _t_swap(fin_out)], slot=me)
    fin_in, = _final_sum("final_sum_w_in", own_in, chips_in, core)
    g1_all, fin_in = _comm_call("gather_last", [_t_allgather(g1_all), _t_swap(fin_in)])

    g_w_out, g_w_down = whole(fin_out), whole(fin_down)
    g_w_down, d_down, nm_down, nv_down = _adamw("adamw_w_down", w_down[0], g_w_down, m_w_down[0], v_w_down[0], 352, True)
    g_w_up, d_up, nm_up, nv_up = _adamw(
        "adamw_w_up", w_up[0], whole(fin_up), m_w_up[0], v_w_up[0], 256, True, stage=False, g_transposed=True)
    g_w_out, d_out, nm_out, nv_out = _adamw("adamw_w_out", w_out[0], g_w_out, m_w_out[0], v_w_out[0], 256, True, stage=False)
    g_w_in, d_in, nm_in, nv_in = [a.T for a in _adamw(
        "adamw_w_in", w_in[0].T, whole(fin_in), m_w_in[0].T, v_w_in[0].T, INW // N_CHIPS // 3, True, stage=False)]
    taps = lambda a: jnp.transpose(a, (1, 0, 2))
    sw = [norm_mix_g, taps(conv_w), q_norm_g, k_norm_g, rel_bias_table.T, sinks, out_norm_conv_g, out_norm_attn_g,
          norm_ffn_g, taps(ffn_conv_w), ffn_conv_b]
    smm = [m_norm_mix_g, taps(m_conv_w), m_q_norm_g, m_k_norm_g, m_rel_bias_table.T, m_sinks, m_out_norm_conv_g,
           m_out_norm_attn_g, m_norm_ffn_g, taps(m_ffn_conv_w), m_ffn_conv_b]
    smv = [v_norm_mix_g, taps(v_conv_w), v_q_norm_g, v_k_norm_g, v_rel_bias_table.T, v_sinks, v_out_norm_conv_g,
           v_out_norm_attn_g, v_norm_ffn_g, taps(v_ffn_conv_w), v_ffn_conv_b]
    *small_out, loss = _small_adam(chip, p_all, pm_all, g1_all, tbl_all, sw, smm, smv)
    sg, sd, snm, snv = [list(r) for r in small_out]
    for r in (sg, sd, snm, snv):
        r[1], r[4], r[9] = taps(r[1]), r[4].T, taps(r[9])

    def order(s, b_in, b_out, b_up, b_down):
        return (s[0], b_in[None], s[1], s[2], s[3], s[4], s[5], s[6], s[7], b_out[None], s[8], b_up[None],
                s[9], s[10], b_down[None])

    return (loss.reshape(()), dx[None],
            *order(sg, g_w_in, g_w_out, g_w_up, g_w_down),
            *order(sd, d_in, d_out, d_up, d_down),
            *order(snm, nm_in, nm_out, nm_up, nm_down),
            *order(snv, nv_in, nv_out, nv_up, nv_down))
```

```python
import functools
import math

import numpy as np

import jax
import jax.numpy as jnp
from jax import lax
from jax.experimental import pallas as pl
from jax.experimental.pallas import tpu as pltpu

F32 = jnp.float32
BF = jnp.bfloat16
SDS = jax.ShapeDtypeStruct

T = 2048
D = 1024
CW = 512
AW = 512
HD = 64
NH = 8
NKV = 2
GQ = 4
INW = 2304
DFF = 2816
BLK = 128
NB = T // BLK
NBUCKET = 32
EPS = 1e-6
NEG_INF = -1e30
N_CHIPS = 4
N_DEV = 8

ADAM_LR = 0.001
ADAM_B1 = 0.9
ADAM_B2 = 0.999
ADAM_EPS = 1e-08
ADAM_WD = 0.01
ADAM_STEP = 10

TM = 512
MIB = 1024 * 1024
MESH = pl.DeviceIdType.MESH
ANY = pl.BlockSpec(memory_space=pl.ANY)

_pcall = pl.pallas_call


def _params(sem=None, vmem_mib=None, collective_id=None):
    kw = {} if collective_id is None else {"collective_id": collective_id}
    if sem is not None:
        kw["dimension_semantics"] = sem
    if vmem_mib is not None:
        kw["vmem_limit_bytes"] = vmem_mib * MIB
    return pltpu.CompilerParams(**kw)


def _resident(shape):
    return pl.BlockSpec(shape, lambda *_: (0,) * len(shape), pipeline_mode=pl.Buffered(1))


def _dot(a, b, ca, cb):
    return lax.dot_general(a, b, (((ca,), (cb,)), ((), ())), preferred_element_type=F32)


def _rms_bwd(dy, x, r, g):
    dg = jnp.sum(dy * (x * r), axis=0, keepdims=True)
    dgx = dy * g
    dx = r * dgx - x * (r * r * r) * jnp.mean(x * dgx, axis=-1, keepdims=True)
    return dx, dg


def _where():
    x, y, c = lax.axis_index("x"), lax.axis_index("y"), lax.axis_index("c")
    return x, y, c, [(1 - x, y), (x, 1 - y), (1 - x, 1 - y)]


def _rcopy(src, dst, ssem, rsem, dev):
    return pltpu.make_async_remote_copy(src_ref=src, dst_ref=dst, send_sem=ssem, recv_sem=rsem, device_id=dev,
                                        device_id_type=MESH)


SIBLING, Y_CHIP, X_CHIP, DIAGONAL_CHIP = 1, 2, 4, 6
OTHER_CHIPS = (Y_CHIP, X_CHIP, DIAGONAL_CHIP)
EVERYONE = tuple(range(1, N_DEV))
BARRIER_OF = {(SIBLING,): 0, (SIBLING, Y_CHIP, X_CHIP): 1, OTHER_CHIPS: 2, (SIBLING,) + OTHER_CHIPS: 3, EVERYONE: 4}


def _peer(rel):
    x, y, c, _ = _where()
    return x ^ ((rel >> 2) & 1), y ^ ((rel >> 1) & 1), c ^ (rel & 1)


class _Task:
    def __init__(self, ins, outs, alias, n_sem, start, finish, middle=None, peers=()):
        self.ins, self.outs, self.alias, self.n_sem, self.start, self.finish = ins, outs, alias, n_sem, start, finish
        self.middle = middle if middle is not None else (lambda *args: None)
        self.peers = peers


def _peers_of(comm):
    return tuple(sorted({p for t in comm for p in t.peers}))


def _enter(comm):
    peers = _peers_of(comm)
    barrier = pltpu.get_barrier_semaphore()
    for rel in peers:
        pl.semaphore_signal(barrier, inc=1, device_id=_peer(rel), device_id_type=MESH)
    pl.semaphore_wait(barrier, len(peers))


ROWS16 = 16


def _t_gather(placed, part=(0, 1), relayed_first=False):
    R = placed.shape[0] // N_CHIPS
    q = R // 4
    lo, hi = (round(f * (q // ROWS16)) * ROWS16 for f in part)

    def quarter(chip_index, core, k):
        return pl.ds(pl.multiple_of(chip_index * R + core * 2 * q + k * q + lo, ROWS16), hi - lo)

    def places():
        x, y, c, _ = _where()
        return c, 2 * x + y, 2 * (1 - x) + y, 2 * x + (1 - y), 2 * (1 - x) + (1 - y), (1 - x, y, c), (x, 1 - y, c), (x, y, 1 - c)

    def copy(buf, k, chip_index, core, quart, ss, rs, b, dev):
        window = buf.at[quarter(chip_index, core, quart)]
        return _rcopy(window, window, ss.at[b + k], rs.at[b + k], dev)

    def first_hop(cout, ss, rs, b, which):
        c, me, _, _, _, x_nbr, y_nbr, _ = places()
        for k, (quart, dev) in enumerate(((0, x_nbr), (1, y_nbr), (1, x_nbr), (0, y_nbr))):
            if k in which:
                copy(cout[0], k, me, c, quart, ss, rs, b, dev).start()

    def start(cin, cout, ss, rs, b):
        first_hop(cout, ss, rs, b, (0, 1) if relayed_first else (0, 1, 2, 3))

    def middle(cin, cout, ss, rs, b):
        c, _, xc, yc, _, x_nbr, y_nbr, sib = places()
        for k, chip_index, quart, dev in ((0, xc, 0, y_nbr), (1, yc, 1, x_nbr)):
            copy(cout[0], k, chip_index, c, quart, ss, rs, b, dev).wait_recv()
            copy(cout[0], 4 + k, chip_index, c, quart, ss, rs, b, dev).start()
            copy(cout[0], 6 + k, chip_index, c, quart, ss, rs, b, sib).start()
        if relayed_first:
            first_hop(cout, ss, rs, b, (2, 3))

    later = ((2, 1, 1), (3, 2, 0), (4, 3, 0), (5, 3, 1))

    def finish(cin, cout, ss, rs, b):
        c, me, xc, yc, dc, _, _, sib = places()
        chip_of = {1: xc, 2: yc, 3: dc}
        for k, whose, quart in later:
            copy(cout[0], k, chip_of[whose], c, quart, ss, rs, b, sib).wait_recv()
            copy(cout[0], 6 + k, chip_of[whose], c, quart, ss, rs, b, sib).start()
        for k, whose, quart in ((0, 1, 0), (1, 2, 1)) + later:
            copy(cout[0], 6 + k, chip_of[whose], 1 - c, quart, ss, rs, b, sib).wait_recv()
        for k in range(12):
            copy(cout[0], k, me, c, 0, ss, rs, b, sib).wait_send()

    return _Task([placed], [SDS(placed.shape, placed.dtype)], [(0, 0)], 12, start, finish, middle, peers=(SIBLING, Y_CHIP, X_CHIP))


def _t_small_weights(buf):
    def start(cin, cout, ss, rs, b):
        x, y, c, chips = _where()
        mine = cout[0].at[2 * x + y]
        for r, (px, py) in enumerate(chips):
            _rcopy(mine, mine, ss.at[b + r], rs.at[b + r], (px, py, c)).start()

    def finish(cin, cout, ss, rs, b):
        x, y, c, chips = _where()
        for r, (px, py) in enumerate(chips):
            got = cout[0].at[2 * px + py]
            _rcopy(got, got, ss.at[b + r], rs.at[b + r], (px, py, c)).wait_recv()
        for r, (px, py) in enumerate(chips):
            mine = cout[0].at[2 * x + y]
            _rcopy(mine, mine, ss.at[b + r], rs.at[b + r], (px, py, c)).wait_send()

    return _Task([buf], [SDS(buf.shape, buf.dtype)], [(0, 0)], 3, start, finish, peers=OTHER_CHIPS)


def _t_sibling(gbf):
    def start(cin, cout, ss, rs, b):
        x, y, c, _ = _where()
        for jj in range(N_CHIPS):
            _rcopy(cin[0].at[2 * jj + (1 - c)], cout[0].at[jj], ss.at[b + jj], rs.at[b + jj], (x, y, 1 - c)).start()

    def finish(cin, cout, ss, rs, b):
        x, y, c, _ = _where()
        for jj in range(N_CHIPS):
            got = cout[0].at[jj]
            _rcopy(got, got, ss.at[b + jj], rs.at[b + jj], (x, y, 1 - c)).wait_recv()
        for jj in range(N_CHIPS):
            got = cout[0].at[jj]
            _rcopy(got, got, ss.at[b + jj], rs.at[b + jj], (x, y, 1 - c)).wait_send()

    return _Task([gbf], [SDS((N_CHIPS,) + gbf.shape[1:], BF)], [], N_CHIPS, start, finish, peers=(SIBLING,))


def _t_chips(pbf):
    def start(cin, cout, ss, rs, b):
        x, y, c, chips = _where()
        for r, (px, py) in enumerate(chips):
            _rcopy(cin[0].at[2 * px + py], cout[0].at[r], ss.at[b + r], rs.at[b + r], (px, py, c)).start()

    def finish(cin, cout, ss, rs, b):
        x, y, c, chips = _where()
        for r, (px, py) in enumerate(chips):
            got = cout[0].at[r]
            _rcopy(got, got, ss.at[b + r], rs.at[b + r], (px, py, c)).wait_recv()
        for r, (px, py) in enumerate(chips):
            got = cout[0].at[r]
            _rcopy(got, got, ss.at[b + r], rs.at[b + r], (px, py, c)).wait_send()

    return _Task([pbf], [SDS((3,) + pbf.shape[1:], BF)], [], 3, start, finish, peers=OTHER_CHIPS)


def _t_swap(fin):
    def start(cin, cout, ss, rs, b):
        x, y, c, _ = _where()
        mine = cout[0].at[c]
        _rcopy(mine, mine, ss.at[b], rs.at[b], (x, y, 1 - c)).start()

    def finish(cin, cout, ss, rs, b):
        x, y, c, _ = _where()
        got = cout[0].at[1 - c]
        _rcopy(got, got, ss.at[b], rs.at[b], (x, y, 1 - c)).wait_recv()
        _rcopy(got, got, ss.at[b], rs.at[b], (x, y, 1 - c)).wait_send()

    return _Task([fin], [SDS(fin.shape, fin.dtype)], [(0, 0)], 1, start, finish, peers=(SIBLING,))


def _t_allgather(buf):
    def peers():
        x, y, c, _ = _where()
        out = []
        for rel in range(1, N_DEV):
            px, py, pc = x ^ ((rel >> 2) & 1), y ^ ((rel >> 1) & 1), c ^ (rel & 1)
            out.append((rel - 1, 4 * px + 2 * py + pc, (px, py, pc)))
        return 4 * x + 2 * y + c, out

    def start(cin, cout, ss, rs, b):
        me, ps = peers()
        mine = cout[0].at[me]
        for k, _, dev in ps:
            _rcopy(mine, mine, ss.at[b + k], rs.at[b + k], dev).start()

    def finish(cin, cout, ss, rs, b):
        me, ps = peers()
        for k, pidx, dev in ps:
            got = cout[0].at[pidx]
            _rcopy(got, got, ss.at[b + k], rs.at[b + k], dev).wait_recv()
        for k, _, dev in ps:
            mine = cout[0].at[me]
            _rcopy(mine, mine, ss.at[b + k], rs.at[b + k], dev).wait_send()

    return _Task([buf], [SDS(buf.shape, buf.dtype)], [(0, 0)], N_DEV - 1, start, finish, peers=EVERYONE)


def _run_tasks(comm, which, cin, cout, ss, rs):
    i0 = o0 = s0 = 0
    for t in comm:
        getattr(t, which)(cin[i0:i0 + len(t.ins)], cout[o0:o0 + len(t.outs)], ss, rs, s0)
        i0, o0, s0 = i0 + len(t.ins), o0 + len(t.outs), s0 + t.n_sem


def _from_hbm(*arrays):
    return [pltpu.with_memory_space_constraint(a, pltpu.HBM) for a in arrays]


def _in_hbm(shapes):
    return [pltpu.HBM(s.shape, s.dtype) for s in shapes]


def _comm_layout(comm, n_in, n_out):
    c_in = [a for t in comm for a in t.ins]
    c_out = [s for t in comm for s in t.outs]
    aliases, i0, o0 = {}, 0, 0
    for t in comm:
        for i, o in t.alias:
            aliases[n_in + i0 + i] = n_out + o0 + o
        i0, o0 = i0 + len(t.ins), o0 + len(t.outs)
    return c_in, c_out, aliases, sum(t.n_sem for t in comm)


def _call(body, operands, *, name, grid, in_specs, out_specs, out_shape, scratch_shapes=(), sem=None, vmem_mib=None, comm=(),
          free=(), prefetch=()):
    operands = [o if s.memory_space == pltpu.SMEM or k in free else pltpu.with_memory_space_constraint(o, pltpu.HBM)
                for k, (o, s) in enumerate(zip(operands, in_specs))]
    n_pre, n_in, n_out, n_scr = len(prefetch), len(in_specs), len(out_specs), len(scratch_shapes)
    c_in, c_out, aliases, n_sem = _comm_layout(comm, n_pre + n_in, n_out)
    sems = [pltpu.SemaphoreType.DMA((n_sem,)), pltpu.SemaphoreType.DMA((n_sem,))] if comm else []

    def wrapped(*refs):
        pre, refs = refs[:n_pre], refs[n_pre:]
        ins, cin = refs[:n_in], refs[n_in:n_in + len(c_in)]
        rest = refs[n_in + len(c_in):]
        outs, cout = rest[:n_out], rest[n_out:n_out + len(c_out)]
        rest = rest[n_out + len(c_out):]
        scr, csem = rest[:n_scr], rest[n_scr:]
        if not comm:
            return body(*pre, *ins, *outs, *scr)
        step = functools.reduce(lambda acc, k: acc * grid[k] + pl.program_id(k), range(len(grid)), 0)
        n_steps = math.prod(grid)

        @pl.when(step == 0)
        def _():
            _enter(comm)
            _run_tasks(comm, "start", cin, cout, *csem)

        pl.when(step == n_steps // 2)(lambda: _run_tasks(comm, "middle", cin, cout, *csem))
        body(*pre, *ins, *outs, *scr)
        pl.when(step == n_steps - 1)(lambda: _run_tasks(comm, "finish", cin, cout, *csem))

    grid_spec = pltpu.PrefetchScalarGridSpec(
        num_scalar_prefetch=n_pre, grid=grid, in_specs=list(in_specs) + [ANY] * len(c_in),
        out_specs=list(out_specs) + [ANY] * len(c_out), scratch_shapes=list(scratch_shapes) + sems)
    return _pcall(
        wrapped, name=name, grid_spec=grid_spec, out_shape=_in_hbm(list(out_shape) + c_out), input_output_aliases=aliases,
        compiler_params=_params(("arbitrary",) * len(grid) if comm else sem, vmem_mib,
                                BARRIER_OF[_peers_of(comm)] if comm else None),
    )(*prefetch, *operands, *_from_hbm(*c_in))


def _comm_call(name, comm):
    c_in, c_out, aliases, n_sem = _comm_layout(comm, 0, 0)

    def body(*refs):
        cin, cout, (ss, rs) = refs[:len(c_in)], refs[len(c_in):len(c_in) + len(c_out)], refs[len(c_in) + len(c_out):]
        _enter(comm)
        for phase in ("start", "middle", "finish"):
            _run_tasks(comm, phase, cin, cout, ss, rs)

    return _pcall(
        body, name=name, in_specs=[ANY] * len(c_in), out_specs=[ANY] * len(c_out), out_shape=_in_hbm(c_out),
        scratch_shapes=[pltpu.SemaphoreType.DMA((n_sem,)), pltpu.SemaphoreType.DMA((n_sem,))],
        input_output_aliases=aliases, compiler_params=_params(collective_id=BARRIER_OF[_peers_of(comm)]),
    )(*_from_hbm(*c_in))


def _inproj(x, g1, w_int, comm=()):
    tm = TM

    def body(x_ref, g_ref, w_ref, proj_ref, u_ref):
        xf = x_ref[...]
        r = lax.rsqrt(jnp.mean(xf * xf, axis=-1, keepdims=True) + EPS)
        u = (xf * r * g_ref[...]).astype(BF)
        u_ref[...] = u
        proj_ref[...] = _dot(u, w_ref[...], 1, 1)

    return _call(
        body, (x, g1, w_int), name="inproj", grid=(T // tm,),
        in_specs=[pl.BlockSpec((tm, D), lambda i: (i, 0)), pl.BlockSpec((1, D), lambda i: (0, 0)),
                  _resident((INW, D))],
        out_specs=[pl.BlockSpec((tm, INW), lambda i: (i, 0)), pl.BlockSpec((tm, D), lambda i: (i, 0))],
        out_shape=[SDS((T, INW), F32), SDS((T, D), BF)], sem=("parallel",), vmem_mib=40, comm=comm, free=(0, 1))


def _outproj(y, w_out, x, g2):
    tm = TM

    def body(y_ref, w_ref, x_ref, g_ref, h1_ref, u2_ref):
        h1 = x_ref[...] + _dot(y_ref[...], w_ref[...], 1, 0)
        h1_ref[...] = h1
        r = lax.rsqrt(jnp.mean(h1 * h1, axis=-1, keepdims=True) + EPS)
        u2_ref[...] = (h1 * r * g_ref[...]).astype(BF)

    return _call(
        body, (y, w_out, x, g2), name="outproj", grid=(T // tm,),
        in_specs=[pl.BlockSpec((tm, D), lambda i: (i, 0)), _resident((D, D)),
                  pl.BlockSpec((tm, D), lambda i: (i, 0)), pl.BlockSpec((1, D), lambda i: (0, 0))],
        out_specs=[pl.BlockSpec((tm, D), lambda i: (i, 0)), pl.BlockSpec((tm, D), lambda i: (i, 0))],
        out_shape=[SDS((T, D), F32), SDS((T, D), BF)], sem=("parallel",), vmem_mib=32, free=(2, 3))


def _ffn_up(u2, w_upt, comm=()):
    tm, tn = T, 512

    def body(u_ref, w_ref, o_ref):
        o_ref[...] = _dot(u_ref[...], w_ref[...], 1, 1).astype(BF)

    return _call(
        body, (u2, w_upt), name="ffn_up", grid=(T // tm, 2 * DFF // tn),
        in_specs=[pl.BlockSpec((tm, D), lambda i, j: (i, 0)), pl.BlockSpec((tn, D), lambda i, j: (j, 0))],
        out_specs=[pl.BlockSpec((tm, tn), lambda i, j: (i, j))], out_shape=[SDS((T, 2 * DFF), BF)],
        sem=("parallel", "parallel"), vmem_mib=32, comm=comm, free=(1,))


def _ffn_down(a, w_down, h1, tgt):
    tm = TM

    def body(a_ref, w_ref, h1_ref, t_ref, dh_ref, dhb_ref, l_ref):
        @pl.when(pl.program_id(0) == 0)
        def _():
            l_ref[...] = jnp.zeros_like(l_ref)

        h2 = h1_ref[...] + _dot(a_ref[...], w_ref[...], 1, 0)
        e = h2 - t_ref[...]
        dh = e * (1.0 / D)
        dh_ref[...] = dh
        dhb_ref[...] = dh.astype(BF)
        e2 = jnp.sum((e * e).reshape(tm // 8, 8, D), axis=0)
        acc = e2[:, 0:128]
        for k in range(1, D // 128):
            acc = acc + e2[:, k * 128:(k + 1) * 128]
        l_ref[...] += acc

    return _call(
        body, (a, w_down, h1, tgt), name="ffn_down", grid=(T // tm,),
        in_specs=[pl.BlockSpec((tm, DFF), lambda i: (i, 0)), _resident((DFF, D)),
                  pl.BlockSpec((tm, D), lambda i: (i, 0)), pl.BlockSpec((tm, D), lambda i: (i, 0))],
        out_specs=[pl.BlockSpec((tm, D), lambda i: (i, 0)), pl.BlockSpec((tm, D), lambda i: (i, 0)),
                   pl.BlockSpec((8, 128), lambda i: (0, 0))],
        out_shape=[SDS((T, D), F32), SDS((T, D), BF), SDS((8, 128), F32)], sem=("arbitrary",), vmem_mib=40, free=(2, 3))


def _bucket_table():
    q = np.arange(BLK, dtype=np.int32)[:, None]
    j = np.arange(2 * BLK, dtype=np.int32)[None, :]
    n = np.maximum(q + BLK - j, 0)
    nf = np.maximum(n, 1).astype(np.float32)
    max_exact = NBUCKET // 2
    large = max_exact + (np.log(nf / np.float32(max_exact)) / np.float32(math.log(BLK / max_exact))
                         * np.float32(NBUCKET - max_exact)).astype(np.int32)
    large = np.minimum(large, NBUCKET - 1)
    return np.where(n < max_exact, n, large).astype(np.int32)


def _band_bias_bwd(dbias, bucket, me):
    def body(me_ref, db_ref, bk_ref, o_ref):
        bk = bk_ref[...]
        for b in range(NBUCKET):
            m = bk == b
            for h in range(NH):
                v = jnp.where(m, db_ref[h * BLK:(h + 1) * BLK, :], 0.0)
                s = jnp.sum(jnp.sum(v, axis=1, keepdims=True), axis=0, keepdims=True)
                o_ref[0, h:h + 1, b:b + 1] = s

    grid_spec = pltpu.PrefetchScalarGridSpec(
        num_scalar_prefetch=1, grid=(1,),
        in_specs=[pl.BlockSpec((NH * BLK, 2 * BLK), lambda i, me_ref: (0, 0)),
                  pl.BlockSpec((BLK, 2 * BLK), lambda i, me_ref: (0, 0))],
        out_specs=pl.BlockSpec((1, NH, NBUCKET), lambda i, me_ref: (me_ref[0], 0, 0)),
    )
    return _pcall(body, name="band_bias_bwd", grid_spec=grid_spec, out_shape=SDS((N_DEV, NH, NBUCKET), F32),
                  compiler_params=_params(("arbitrary",)))(me, dbias, bucket)


def _two_bf16(x):
    hi = x.astype(BF)
    return hi, (x - hi.astype(F32)).astype(BF)


def _head_sums(x, seg):
    hi, lo = _two_bf16(x)
    s = seg[0:x.shape[1], :]
    return _dot(hi, s, 1, 0) + _dot(lo, s, 1, 0)


def _head_spread(v, seg, width):
    hi, lo = _two_bf16(v)
    s = seg[0:width, :]
    return _dot(hi, s, 1, 1) + _dot(lo, s, 1, 1)


def _head_norm(x, g_t, seg, by_head=False):
    if by_head:
        heads = [x[:, h * HD:(h + 1) * HD] for h in range(x.shape[1] // HD)]
        r = jnp.concatenate([jnp.broadcast_to(lax.rsqrt(jnp.mean(v * v, axis=-1, keepdims=True) + EPS), v.shape)
                             for v in heads], axis=1)
    else:
        r = lax.rsqrt(_head_sums(x * x, seg) * (1.0 / HD) + EPS)
        r = _head_spread(r, seg, x.shape[1])
    return x * r * g_t, r


def _head_norm_bwd(dy, x, r, g_t, seg):
    dg_t = jnp.sum(dy * (x * r), axis=0, keepdims=True)
    dgx = dy * g_t
    mean = _head_spread(_head_sums(x * dgx, seg) * (1.0 / HD), seg, x.shape[1])
    return r * dgx - x * (r * r * r) * mean, dg_t


def _fold_heads(v):
    out = v[:, 0:HD]
    for h in range(1, v.shape[1] // HD):
        out = out + v[:, h * HD:(h + 1) * HD]
    return out


def _mix_forward(P, zc8, zh8, pkv, first, cw, qg_t, kg_t, gco, gao, seg, sink_ref, bias_ref, by_head=False):
    gate_b = P[:, 0:CW]
    gate_c = P[:, CW:2 * CW]
    hc = P[:, 2 * CW:3 * CW]
    z = gate_c * hc
    keep = jnp.where(first, 0.0, 1.0)
    zp = zc8 * zh8 * keep
    p1 = zp[7:8, :]
    p2 = zp[6:7, :]
    row = lax.broadcasted_iota(jnp.int32, (BLK, 1), 0)
    z1 = jnp.where(row == 0, p1, pltpu.roll(z, 1, 0))
    z2 = jnp.where(row == 0, p2, jnp.where(row == 1, p1, pltpu.roll(z, 2, 0)))
    cz = cw[0:1, :] * z2 + cw[1:2, :] * z1 + cw[2:3, :] * z
    y_conv = gate_b * cz

    scale = HD ** -0.5
    qi = lax.broadcasted_iota(jnp.int32, (BLK, 2 * BLK), 0)
    kj = lax.broadcasted_iota(jnp.int32, (BLK, 2 * BLK), 1)
    dd = qi + BLK - kj
    first_key = jnp.where(first, BLK, 0)
    valid = (dd >= 0) & (dd < BLK) & (kj >= first_key)

    q0 = 3 * CW
    k0 = q0 + AW
    v0 = k0 + NKV * HD
    q_raw = P[:, q0:k0]
    qn, rq = _head_norm(q_raw, qg_t, seg, by_head)
    qs = (qn * scale).astype(BF)
    k_raw = jnp.concatenate([pkv[:, 0:NKV * HD], P[:, k0:v0]], axis=0)
    kn, rk = _head_norm(k_raw, kg_t, seg, by_head)
    knb = kn.astype(BF)
    heads = []
    for h in range(NH):
        kv = h // GQ
        kb = knb[:, kv * HD:(kv + 1) * HD]
        vb = jnp.concatenate([pkv[:, NKV * HD + kv * HD:NKV * HD + (kv + 1) * HD],
                              P[:, v0 + kv * HD:v0 + (kv + 1) * HD]], axis=0).astype(BF)
        Q = qs[:, h * HD:(h + 1) * HD]
        S = _dot(Q, kb, 1, 1) + bias_ref[h * BLK:(h + 1) * BLK, :]
        S = jnp.where(valid, S, NEG_INF)
        sink = sink_ref[0, h]
        m = jnp.maximum(jnp.max(S, axis=-1, keepdims=True), sink)
        p = jnp.exp(S - m)
        es = jnp.exp(sink - m)
        denom = jnp.sum(p, axis=-1, keepdims=True) + es
        probs = p / denom
        O = _dot(probs.astype(BF), vb, 1, 0)
        heads.append(dict(kb=kb, vb=vb, Q=Q, probs=probs, psink=es / denom, O=O))
    y_attn = jnp.concatenate([hd["O"] for hd in heads], axis=1)

    rc = lax.rsqrt(jnp.mean(y_conv * y_conv, axis=-1, keepdims=True) + EPS)
    ra = lax.rsqrt(jnp.mean(y_attn * y_attn, axis=-1, keepdims=True) + EPS)
    y = jnp.concatenate([y_conv * rc * gco, y_attn * ra * gao], axis=1)
    return dict(gate_b=gate_b, gate_c=gate_c, hc=hc, z=z, z1=z1, z2=z2, cz=cz, y_conv=y_conv, y_attn=y_attn,
                rc=rc, ra=ra, heads=heads, y=y, row=row, scale=scale, q_raw=q_raw, rq=rq, k_raw=k_raw, rk=rk)


BPS = 2
TILE = BPS * BLK
KV0 = 3 * CW + AW


def _mix_in_specs(tile_of):
    return [
        pl.BlockSpec(memory_space=pltpu.SMEM),
        pl.BlockSpec((TILE, INW), lambda s: (tile_of(s), 0)),
        pl.BlockSpec((8, CW), lambda s: (jnp.maximum(tile_of(s) * (TILE // 8) - 1, 0), 1)),
        pl.BlockSpec((8, CW), lambda s: (jnp.maximum(tile_of(s) * (TILE // 8) - 1, 0), 2)),
        pl.BlockSpec((BLK, 2 * NKV * HD), lambda s: (jnp.maximum(tile_of(s) * BPS - 1, 0), KV0 // (2 * NKV * HD))),
    ]


def _block_inputs(tile, b, zc_ref, zh_ref, pkv_ref, first_tile):
    P = tile[b * BLK:(b + 1) * BLK, :]
    if b == 0:
        return P, zc_ref[...], zh_ref[...], pkv_ref[...], first_tile
    lo = b * BLK
    return P, tile[lo - 8:lo, CW:2 * CW], tile[lo - 8:lo, 2 * CW:3 * CW], tile[lo - BLK:lo, KV0:KV0 + 2 * NKV * HD], False


def _mix_param_specs():
    return [
        pl.BlockSpec((8, CW), lambda s: (0, 0)),
        pl.BlockSpec((1, AW), lambda s: (0, 0)),
        pl.BlockSpec((1, NKV * HD), lambda s: (0, 0)),
        pl.BlockSpec((1, CW), lambda s: (0, 0)),
        pl.BlockSpec((1, AW), lambda s: (0, 0)),
        pl.BlockSpec((AW, 128), lambda s: (0, 0)),
        pl.BlockSpec((NH * BLK, 2 * BLK), lambda s: (0, 0)),
    ]


def _mix_params(cw8, qg, kg, gco, gao, bias):
    seg = np.zeros((AW, 128), np.float32)
    seg[np.arange(AW), np.arange(AW) // HD] = 1.0
    return (cw8, jnp.tile(qg, (1, NH)), jnp.tile(kg, (1, NKV)), gco, gao, jnp.asarray(seg, BF), bias)


def _mix_fwd(proj, sinks, cw8, qg, kg, gco, gao, bias, comm=()):
    def body(sink_ref, p_ref, zc_ref, zh_ref, pkv_ref, cw_ref, qg_ref, kg_ref, gco_ref, gao_ref, seg_ref, bias_ref, y_ref):
        tile = p_ref[...]
        for b in range(BPS):
            f = _mix_forward(*_block_inputs(tile, b, zc_ref, zh_ref, pkv_ref, pl.program_id(0) == 0), cw_ref[...],
                             qg_ref[...], kg_ref[...], gco_ref[...], gao_ref[...], seg_ref[...], sink_ref, bias_ref, by_head=True)
            y_ref[b * BLK:(b + 1) * BLK, :] = f["y"].astype(BF)

    return _call(
        body, (sinks, proj, proj, proj, proj, *_mix_params(cw8, qg, kg, gco, gao, bias)), name="mix_fwd", grid=(T // TILE,),
        in_specs=_mix_in_specs(lambda s: s) + _mix_param_specs(),
        out_specs=[pl.BlockSpec((TILE, D), lambda s: (s, 0))], out_shape=[SDS((T, D), BF)],
        sem=("parallel",), vmem_mib=40, comm=comm, free=tuple(range(5, 12)))


def _mix_bwd(proj, dy, sinks, cw8, qg, kg, gco, gao, bias, comm=()):
    n_steps = T // TILE

    def tile_of(s):
        return n_steps - 1 - s

    def body(sink_ref, p_ref, zc_ref, zh_ref, pkv_ref, dy_ref, cw_ref, qg_ref, kg_ref, gco_ref, gao_ref, seg_ref, bias_ref,
             dproj_ref, dcw_ref, dqg_ref, dkg_ref, dgco_ref, dgao_ref, dsink_ref, dbias_ref,
             ndcz_ref, dkc_ref, dvc_ref):
        s = pl.program_id(0)

        @pl.when(s == 0)
        def _():
            for r in (dcw_ref, dqg_ref, dkg_ref, dgco_ref, dgao_ref, dsink_ref, dbias_ref, ndcz_ref, dkc_ref, dvc_ref):
                r[...] = jnp.zeros_like(r)

        params = (cw_ref[...], qg_ref[...], kg_ref[...], gco_ref[...], gao_ref[...], seg_ref[...])
        tile = p_ref[...]
        carry = (ndcz_ref[...], dkc_ref[...], dvc_ref[...])
        total = None
        for b in reversed(range(BPS)):
            f = _mix_forward(*_block_inputs(tile, b, zc_ref, zh_ref, pkv_ref, s == n_steps - 1), *params, sink_ref, bias_ref)
            pieces, sums, carry = one_block(f, dy_ref[b * BLK:(b + 1) * BLK, :], params, carry)
            for lo, piece in pieces:
                dproj_ref[b * BLK:(b + 1) * BLK, lo:lo + piece.shape[1]] = piece
            total = sums if total is None else [t + v for t, v in zip(total, sums)]
        ndcz_ref[...], dkc_ref[...], dvc_ref[...] = carry
        dcw, dqg_t, dkg_t, dgco, dgao, dsink, *ds = total
        dcw_ref[0:3, :] += dcw
        dqg_ref[...] += _fold_heads(dqg_t)
        dkg_ref[...] += _fold_heads(dkg_t)
        dgco_ref[...] += dgco
        dgao_ref[...] += dgao
        dsink_ref[...] += dsink
        for h in range(NH):
            dbias_ref[h * BLK:(h + 1) * BLK, :] += ds[h]

    def one_block(f, dy, params, carry):
        cw, qg_v, kg_v, gco_v, gao_v, seg = params
        nxt, dk_carry, dv_carry = carry
        dyc, dgco = _rms_bwd(dy[:, 0:CW], f["y_conv"], f["rc"], gco_v)
        dya, dgao = _rms_bwd(dy[:, CW:CW + AW], f["y_attn"], f["ra"], gao_v)

        row = f["row"]
        dgate_b = dyc * f["cz"]
        dcz = dyc * f["gate_b"]
        dcw = jnp.concatenate([jnp.sum(dcz * f[k], axis=0, keepdims=True) for k in ("z2", "z1", "z")], axis=0)
        n0 = nxt[0:1, :]
        n1 = nxt[1:2, :]
        d1 = jnp.where(row == BLK - 1, n0, pltpu.roll(dcz, BLK - 1, 0))
        d2 = jnp.where(row == BLK - 1, n1, jnp.where(row == BLK - 2, n0, pltpu.roll(dcz, BLK - 2, 0)))
        dz = cw[2:3, :] * dcz + cw[1:2, :] * d1 + cw[0:1, :] * d2
        pieces = [(0, dgate_b.astype(BF)), (CW, (dz * f["hc"]).astype(BF)), (2 * CW, (dz * f["gate_c"]).astype(BF))]

        scale = f["scale"]
        lane = lax.broadcasted_iota(jnp.int32, (1, 128), 1)
        dsink = jnp.zeros((1, 128), F32)
        dq_cols, dk_cols, dv_cols, dk_prev, dv_prev, ds = [], [], [], [], [], []
        for kv in range(NKV):
            dKb = dVb = 0.0
            for h in range(kv * GQ, (kv + 1) * GQ):
                hd = f["heads"][h]
                dO = dya[:, h * HD:(h + 1) * HD]
                delta = jnp.sum(dO * hd["O"], axis=-1, keepdims=True)
                dOb = dO.astype(BF)
                dP = _dot(dOb, hd["vb"], 1, 1)
                dS = hd["probs"] * (dP - delta)
                tot = jnp.sum(hd["psink"] * delta, axis=0, keepdims=True)
                dsink = dsink - jnp.where(lane == h, tot, 0.0)
                ds.append(dS)
                dSb = dS.astype(BF)
                dq_cols.append(_dot(dSb, hd["kb"], 1, 0))
                dKb = dKb + _dot(dSb, hd["Q"], 0, 0)
                dVb = dVb + _dot(hd["probs"].astype(BF), dOb, 0, 0)
            dk_cols.append(dKb[BLK:, :] + dk_carry[:, kv * HD:(kv + 1) * HD])
            dv_cols.append(dVb[BLK:, :] + dv_carry[:, kv * HD:(kv + 1) * HD])
            dk_prev.append(dKb[:BLK, :])
            dv_prev.append(dVb[:BLK, :])
        dq_raw, dqg_t = _head_norm_bwd(jnp.concatenate(dq_cols, axis=1) * scale, f["q_raw"], f["rq"], qg_v, seg)
        dk_raw, dkg_t = _head_norm_bwd(jnp.concatenate(dk_cols, axis=1), f["k_raw"][BLK:, :], f["rk"][BLK:, :], kg_v, seg)
        pieces.append((3 * CW, jnp.concatenate([dq_raw, dk_raw] + dv_cols, axis=1).astype(BF)))
        owed = (dcz[0:8, :], jnp.concatenate(dk_prev, axis=1), jnp.concatenate(dv_prev, axis=1))
        return pieces, [dcw, dqg_t, dkg_t, dgco, dgao, dsink, *ds], owed

    small = lambda r, c: pl.BlockSpec((r, c), lambda s: (0, 0))
    return _call(
        body, (sinks, proj, proj, proj, proj, dy, *_mix_params(cw8, qg, kg, gco, gao, bias)), name="mix_bwd", grid=(n_steps,),
        in_specs=_mix_in_specs(tile_of) + [pl.BlockSpec((TILE, D), lambda s: (tile_of(s), 0))] + _mix_param_specs(),
        out_specs=[pl.BlockSpec((TILE, INW), lambda s: (tile_of(s), 0)), small(8, CW), small(1, HD), small(1, HD),
                   small(1, CW), small(1, AW), small(1, 128), small(NH * BLK, 2 * BLK)],
        out_shape=[SDS((T, INW), BF), SDS((8, CW), F32), SDS((1, HD), F32), SDS((1, HD), F32), SDS((1, CW), F32),
                   SDS((1, AW), F32), SDS((1, 128), F32), SDS((NH * BLK, 2 * BLK), F32)],
        scratch_shapes=[pltpu.VMEM((8, CW), F32), pltpu.VMEM((BLK, NKV * HD), F32), pltpu.VMEM((BLK, NKV * HD), F32)],
        sem=("arbitrary",), vmem_mib=56, comm=comm, free=(1, 2, 3, 4) + tuple(range(6, 13)))


FT = 256
NFT = DFF // FT
RC = 1024
NCH = T // RC
LEAD = 16


def _rows8(x):
    return jnp.sum(x.reshape(x.shape[0] // 8, 8, x.shape[1]), axis=0)


def _ffn_act_specs():
    return [
        pl.BlockSpec((T, FT), lambda j: (0, j)), pl.BlockSpec((T, FT), lambda j: (0, NFT + j)),
        pl.BlockSpec((8, FT), lambda j: (0, j)), pl.BlockSpec((8, FT), lambda j: (0, NFT + j)),
        pl.BlockSpec((1, FT), lambda j: (0, j)), pl.BlockSpec((1, FT), lambda j: (0, NFT + j)),
    ]


def _conv_rows(win, w, b, n):
    win = win.astype(F32)
    u = win[LEAD:LEAD + n]
    u1 = pltpu.roll(win, 1, 0)[LEAD:LEAD + n]
    u2 = pltpu.roll(win, 2, 0)[LEAD:LEAD + n]
    return u2, u1, u, w[0:1, :] * u2 + w[1:2, :] * u1 + w[2:3, :] * u + b


def _ffn_act(up, fw8, fb, comm=()):
    def body(ug_ref, uv_ref, wg_ref, wv_ref, bg_ref, bv_ref, a_ref, pg_ref, pv_ref):
        wg, wv, bg, bv = wg_ref[...], wv_ref[...], bg_ref[...], bv_ref[...]

        def chunk(rows, win_g, win_v):
            gp = _conv_rows(win_g, wg, bg, RC)[3]
            vp = _conv_rows(win_v, wv, bv, RC)[3]
            a_ref[rows, :] = (gp * jax.nn.sigmoid(gp) * vp).astype(BF)
            pg_ref[0, rows, :] = gp.astype(BF)
            pv_ref[0, rows, :] = vp.astype(BF)

        zero = jnp.zeros((LEAD, FT), BF)
        chunk(pl.ds(0, RC), jnp.concatenate([zero, ug_ref[0:RC, :]], axis=0), jnp.concatenate([zero, uv_ref[0:RC, :]], axis=0))

        def step(i, carry):
            r0 = pl.multiple_of(i * RC, RC)
            win = pl.ds(r0 - LEAD, RC + LEAD)
            chunk(pl.ds(r0, RC), ug_ref[win, :], uv_ref[win, :])
            return carry

        lax.fori_loop(1, NCH, step, 0)

    tile = pl.BlockSpec((1, T, FT), lambda j: (j, 0, 0))
    return _call(
        body, (up, up, fw8, fw8, fb, fb), name="ffn_act", grid=(NFT,), in_specs=_ffn_act_specs(),
        out_specs=[pl.BlockSpec((T, FT), lambda j: (0, j)), tile, tile],
        out_shape=[SDS((T, DFF), BF), SDS((NFT, T, FT), BF), SDS((NFT, T, FT), BF)],
        sem=("parallel",), vmem_mib=40, comm=comm, free=(2, 3, 4, 5))


def _ffn_act_bwd(up, pre_g, pre_v, da, fw8, comm=()):
    ext = RC + LEAD

    def body(ug_ref, uv_ref, wg_ref, wv_ref, pg_ref, pv_ref, da_ref, dug_ref, duv_ref, dwg_ref, dwv_ref, dbg_ref, dbv_ref):
        wg, wv = wg_ref[...], wv_ref[...]

        def chunk(u_g, u_v, gp, vp, da_e):
            gp, vp, da_e = gp.astype(F32), vp.astype(F32), da_e.astype(F32)
            sig = jax.nn.sigmoid(gp)
            dvp = da_e * (gp * sig)
            dgp = da_e * vp * (sig * (1.0 + gp * (1.0 - sig)))

            def branch(dp, w, u):
                d0, d1, d2 = dp[0:RC], pltpu.roll(dp, ext - 1, 0)[0:RC], pltpu.roll(dp, ext - 2, 0)[0:RC]
                du = (w[2:3, :] * d0 + w[1:2, :] * d1 + w[0:1, :] * d2).astype(BF)
                u = u.astype(F32)
                return du, [_rows8(d0), _rows8(d2 * u), _rows8(d1 * u), _rows8(d0 * u)]

            dug, sums_g = branch(dgp, wg, u_g)
            duv, sums_v = branch(dvp, wv, u_v)
            return dug, duv, sums_g + sums_v

        def step(i, acc):
            r0 = pl.multiple_of(i * RC, RC)
            rows, more = pl.ds(r0, RC), pl.ds(r0, ext)
            dug, duv, part = chunk(ug_ref[rows, :], uv_ref[rows, :], pg_ref[0, more, :], pv_ref[0, more, :], da_ref[more, :])
            dug_ref[rows, :] = dug
            duv_ref[rows, :] = duv
            return [a + p for a, p in zip(acc, part)]

        acc = lax.fori_loop(0, NCH - 1, step, [jnp.zeros((8, FT), F32)] * 8)
        r0 = T - RC
        zero = jnp.zeros((LEAD, FT), BF)
        tail = lambda rows: jnp.concatenate([rows, zero], axis=0)
        dug, duv, part = chunk(ug_ref[r0:T, :], uv_ref[r0:T, :], tail(pg_ref[0, r0:T, :]), tail(pv_ref[0, r0:T, :]),
                               tail(da_ref[r0:T, :]))
        dug_ref[r0:T, :] = dug
        duv_ref[r0:T, :] = duv
        tot = [jnp.sum(a + p, axis=0, keepdims=True) for a, p in zip(acc, part)]
        for k, (dw_ref, db_ref) in enumerate(((dwg_ref, dbg_ref), (dwv_ref, dbv_ref))):
            db_ref[...] = tot[4 * k]
            dw_ref[...] = jnp.zeros_like(dw_ref)
            for r in range(3):
                dw_ref[r:r + 1, :] = tot[4 * k + 1 + r]

    col = lambda r: pl.BlockSpec((r, FT), lambda j: (0, j))
    return _call(
        body, (up, up, fw8, fw8, pre_g, pre_v, da), name="ffn_act_bwd", grid=(NFT,),
        in_specs=_ffn_act_specs()[0:4] + [pl.BlockSpec((1, T, FT), lambda j: (j, 0, 0))] * 2 + [col(T)],
        out_specs=[col(T), col(T), col(8), col(8), col(1), col(1)],
        out_shape=[SDS((T, DFF), BF), SDS((T, DFF), BF), SDS((8, DFF), F32), SDS((8, DFF), F32),
                   SDS((1, DFF), F32), SDS((1, DFF), F32)],
        sem=("parallel",), vmem_mib=40, comm=comm, free=(0, 1, 2, 3))


def _ffn_down_bwd(dh2b, w_down, comm=()):
    tm = TM

    def body(d_ref, w_ref, o_ref):
        o_ref[...] = _dot(d_ref[...], w_ref[...], 1, 1).astype(BF)

    return _call(
        body, (dh2b, w_down), name="ffn_down_bwd", grid=(T // tm,),
        in_specs=[pl.BlockSpec((tm, D), lambda i: (i, 0)), _resident((DFF, D))],
        out_specs=[pl.BlockSpec((tm, DFF), lambda i: (i, 0))], out_shape=[SDS((T, DFF), BF)],
        sem=("parallel",), vmem_mib=40, comm=comm, free=(0, 1))


def _norm_matmul_bwd(name, a_list, w_t, k_offsets, xin, g, dres, want_bf16, comm=(), slot=None):
    tm = TM
    ks = [a.shape[1] for a in a_list]
    n_a = len(a_list)
    n_pre = 0 if slot is None else 1

    def body(*refs):
        refs = refs[n_pre:]
        a_refs = refs[:n_a]
        w_ref, x_ref, g_ref, r_ref = refs[n_a:n_a + 4]
        outs = refs[n_a + 4:]
        dx_ref, dg_ref = outs[0], (outs[-1] if slot is None else outs[-1].at[0])

        @pl.when(pl.program_id(0) == 0)
        def _():
            dg_ref[...] = jnp.zeros_like(dg_ref)

        du = _dot(a_refs[0][...], w_ref[k_offsets[0]:k_offsets[0] + ks[0], :], 1, 0)
        for k in range(1, n_a):
            du = du + _dot(a_refs[k][...], w_ref[k_offsets[k]:k_offsets[k] + ks[k], :], 1, 0)
        x = x_ref[...]
        r = lax.rsqrt(jnp.mean(x * x, axis=-1, keepdims=True) + EPS)
        dx, dg = _rms_bwd(du, x, r, g_ref[...])
        dx = r_ref[...] + dx
        dx_ref[...] = dx
        if want_bf16:
            outs[1][...] = dx.astype(BF)
        dg_ref[...] += dg

    tile = lambda c: pl.BlockSpec((tm, c), lambda i, *_: (i, 0))
    if slot is None:
        dg_spec, dg_shape = pl.BlockSpec((1, D), lambda i: (0, 0)), SDS((1, D), F32)
    else:
        dg_spec, dg_shape = pl.BlockSpec((1, 1, D), lambda i, slot_ref: (slot_ref[0], 0, 0)), SDS((N_DEV, 1, D), F32)
    out_specs = [tile(D)] + ([tile(D)] if want_bf16 else []) + [dg_spec]
    out_shape = [SDS((T, D), F32)] + ([SDS((T, D), BF)] if want_bf16 else []) + [dg_shape]
    return _call(
        body, (*a_list, w_t, xin, g, dres), name=name, grid=(T // tm,), prefetch=() if slot is None else (slot,),
        in_specs=[tile(k) for k in ks] + [_resident(w_t.shape), tile(D),
                                           pl.BlockSpec((1, D), lambda i, *_: (0, 0)), tile(D)],
        out_specs=out_specs, out_shape=out_shape, sem=("arbitrary",), vmem_mib=56, comm=comm, free=tuple(range(n_a + 4)))


def _out_bwd(dh1b, w_out, comm=()):
    tm = TM

    def body(d_ref, w_ref, o_ref):
        o_ref[...] = _dot(d_ref[...], w_ref[...], 1, 1)

    return _call(
        body, (dh1b, w_out), name="out_bwd", grid=(T // tm,),
        in_specs=[pl.BlockSpec((tm, D), lambda i: (i, 0)), _resident((D, D))],
        out_specs=[pl.BlockSpec((tm, D), lambda i: (i, 0))], out_shape=[SDS((T, D), F32)],
        sem=("parallel",), vmem_mib=32, comm=comm, free=(0, 1))


def _wgrad(name, a_list, b, old_a, comm=()):
    m_k = a_list[0].shape[1]
    tm = max(t for t in range(128, m_k // 2 + 1, 128) if m_k % t == 0)
    steps = [a.shape[1] // tm for a in a_list]
    starts = [sum(steps[:k]) for k in range(len(a_list))]
    n_a = len(a_list)

    def body(*refs):
        a_refs, b_ref, o_ref = refs[:n_a], refs[n_a], refs[n_a + 1]
        i = pl.program_id(0)
        for k in range(n_a):
            @pl.when((i >= starts[k]) & (i < starts[k] + steps[k]))
            def _(k=k):
                o_ref[...] = _dot(a_refs[k][...], b_ref[...], 0, 0).astype(BF)

    def a_spec(k):
        return pl.BlockSpec((T, tm), lambda i: (0, jnp.clip(i - starts[k], 0, steps[k] - 1)))

    m_total = tm * sum(steps)
    return _call(
        body, (*a_list, b), name=name, grid=(sum(steps),),
        in_specs=[a_spec(k) for k in range(n_a)] + [_resident((T, D))],
        out_specs=[pl.BlockSpec((tm, D), lambda i: (i, 0))], out_shape=[SDS((m_total, D), BF)],
        sem=("parallel",), vmem_mib=40, comm=comm, free=() if old_a is None else tuple(range(n_a)) if old_a else (n_a,))


def _chip_sum(name, gbf, from_sib, core, chip):
    h = gbf.shape[1]
    th = h

    def body(core_ref, chip_ref, g_ref, s_ref, pbf_ref, own_ref):
        p = g_ref[0].astype(F32) + s_ref[0].astype(F32)
        pbf_ref[0] = p.astype(BF)

        @pl.when(pl.program_id(1) == chip_ref[0])
        def _():
            own_ref[...] = p

    grid_spec = pltpu.PrefetchScalarGridSpec(
        num_scalar_prefetch=2, grid=(h // th, N_CHIPS),
        in_specs=[pl.BlockSpec((1, th, D), lambda t, jj, core_ref, chip_ref: (2 * jj + core_ref[0], t, 0)),
                  pl.BlockSpec((1, th, D), lambda t, jj, core_ref, chip_ref: (jj, t, 0))],
        out_specs=[pl.BlockSpec((1, th, D), lambda t, jj, core_ref, chip_ref: (jj, t, 0)),
                   pl.BlockSpec((th, D), lambda t, jj, core_ref, chip_ref: (t, 0))],
    )
    return _pcall(
        body, name=name, grid_spec=grid_spec, out_shape=_in_hbm([SDS((N_CHIPS, h, D), BF), SDS((h, D), F32)]),
        compiler_params=_params(("arbitrary", "arbitrary"), 32),
    )(core, chip, *_from_hbm(gbf, from_sib))


def _final_sum(name, own, from_chips, core, comm=()):
    h = own.shape[0]
    n = 4 if h % (4 * ROWS16) == 0 else 2
    th = h // n

    def body(core_ref, o_ref, r_ref, f_ref):
        f_ref[0] = ((o_ref[...] + r_ref[0].astype(F32)) + r_ref[1].astype(F32)) + r_ref[2].astype(F32)

    return _call(
        body, (own, from_chips), name=name, grid=(n,), prefetch=(core,),
        in_specs=[pl.BlockSpec((th, D), lambda i, core_ref: (i, 0)), pl.BlockSpec((3, th, D), lambda i, core_ref: (0, i, 0))],
        out_specs=[pl.BlockSpec((1, th, D), lambda i, core_ref: (core_ref[0], i, 0))], out_shape=[SDS((2, h, D), F32)],
        sem=("arbitrary",), vmem_mib=40, comm=comm)


def _adam_math(w, g, m, v):
    nm = ADAM_B1 * m + (1.0 - ADAM_B1) * g
    nv = ADAM_B2 * v + (1.0 - ADAM_B2) * (g * g)
    m_hat = nm / (1.0 - ADAM_B1 ** ADAM_STEP)
    v_hat = nv / (1.0 - ADAM_B2 ** ADAM_STEP)
    return -ADAM_LR * (m_hat / (jnp.sqrt(v_hat) + ADAM_EPS) + ADAM_WD * w), nm, nv


def _adamw(name, w, g, m, v, tr, copy_g=False, stage=True, g_transposed=False):
    rows, cols = w.shape

    def body(w_ref, g_ref, m_ref, v_ref, *outs):
        d_ref, nm_ref, nv_ref = outs[-3:]
        for c in [pl.ds(c0, 128) for c0 in range(0, cols, 128)] if g_transposed else [slice(None)]:
            g_val = g_ref[c, :].T if g_transposed else g_ref[...]
            if copy_g:
                outs[0][:, c] = g_val
            d_ref[:, c], nm_ref[:, c], nv_ref[:, c] = _adam_math(w_ref[:, c], g_val, m_ref[:, c], v_ref[:, c])

    spec = pl.BlockSpec((tr, cols), lambda i: (i, 0))
    n_out = 4 if copy_g else 3
    g_spec = pl.BlockSpec((cols, tr), lambda i: (0, i)) if g_transposed else spec
    return _call(body, (w, g, m, v), name=name, grid=(rows // tr,), in_specs=[spec, g_spec, spec, spec], out_specs=[spec] * n_out,
                 out_shape=[SDS((rows, cols), F32)] * n_out, sem=("parallel",), vmem_mib=32,
                 free=(0, 2, 3) if stage else ())


C_SQ = 2 * DFF
P_W = C_SQ + 128
R_G2, R_GO, R_DCW, R_QK = 0, 1, 2, 5
C_GCO, C_GAO, C_DQG, C_DKG, C_SINK = 0, CW, 0, 128, 256


def _pack(name, me, ins, width, fill):
    def body(me_ref, *refs):
        o = refs[-1]
        o[...] = jnp.zeros_like(o)
        fill(o, *refs[:-1])

    return _call(body, ins, name=name, grid=(1,), prefetch=(me,),
                 in_specs=[pl.BlockSpec(a.shape, lambda i, me_ref: (0, 0)) for a in ins],
                 out_specs=[pl.BlockSpec((1, 8, width), lambda i, me_ref: (me_ref[0], 0, 0))],
                 out_shape=[SDS((N_DEV, 8, width), F32)], sem=("arbitrary",))[0]


def _pack_ffn(me, dfwg, dfwv, dfbg, dfbv, sq):
    def fill(o, dfwg_r, dfwv_r, dfbg_r, dfbv_r, sq_r):
        o[0, :, 0:DFF] = dfwg_r[...]
        o[0, :, DFF:2 * DFF] = dfwv_r[...]
        o[0, 3:4, 0:DFF] = dfbg_r[...]
        o[0, 3:4, DFF:2 * DFF] = dfbv_r[...]
        o[0, :, C_SQ:C_SQ + 128] = sq_r[...]

    return _pack("pack_ffn", me, (dfwg, dfwv, dfbg, dfbv, sq), P_W, fill)


def _pack_mix(me, dg2, dgco, dgao, dcw8, dqg, dkg, dsink):
    def fill(o, dg2_r, dgco_r, dgao_r, dcw_r, dqg_r, dkg_r, dsink_r):
        o[0, R_G2:R_G2 + 1, :] = dg2_r[...]
        o[0, R_GO:R_GO + 1, C_GCO:C_GCO + CW] = dgco_r[...]
        o[0, R_GO:R_GO + 1, C_GAO:C_GAO + AW] = dgao_r[...]
        o[0, R_DCW:R_DCW + 3, 0:CW] = dcw_r[0:3, :]
        o[0, R_QK:R_QK + 1, C_DQG:C_DQG + HD] = dqg_r[...]
        o[0, R_QK:R_QK + 1, C_DKG:C_DKG + HD] = dkg_r[...]
        o[0, R_QK:R_QK + 1, C_SINK:C_SINK + 128] = dsink_r[...]

    return _pack("pack_mix", me, (dg2, dgco, dgao, dcw8, dqg, dkg, dsink), D, fill)


N_SMALL = 11


def _small_adam(chip, p_all, pm_all, g1_all, tbl_all, ws, ms, vs):
    fw_cols = 2 * DFF // N_CHIPS
    cw_cols = CW // N_CHIPS

    def body(chip_ref, p_ref, fw_ref, pm_ref, cw_ref, g1_ref, tbl_ref, *refs):
        w_r, m_r, v_r = refs[0:N_SMALL], refs[N_SMALL:2 * N_SMALL], refs[2 * N_SMALL:3 * N_SMALL]
        outs = refs[3 * N_SMALL:]
        g_o, d_o, nm_o, nv_o = (outs[k * N_SMALL:(k + 1) * N_SMALL] for k in range(4))
        loss_o = outs[4 * N_SMALL]

        def total(ref):
            s = ref[0]
            for k in range(1, N_DEV):
                s = s + ref[k]
            return s

        S = total(p_ref)
        fw = total(fw_ref)
        M = total(pm_ref)
        cw = total(cw_ref)

        def step(i, g, at):
            d, nm, nv = _adam_math(w_r[i][at], g, m_r[i][at], v_r[i][at])
            g_o[i][at], d_o[i][at], nm_o[i][at], nv_o[i][at] = g, d, nm, nv

        everything = (slice(None), slice(None))
        step(0, total(g1_ref), everything)
        for r in range(3):
            step(1, cw[R_DCW + r:R_DCW + r + 1, :], (r, slice(None), slice(None)))
        step(2, M[R_QK:R_QK + 1, C_DQG:C_DQG + HD], everything)
        step(3, M[R_QK:R_QK + 1, C_DKG:C_DKG + HD], everything)
        step(4, total(tbl_ref), everything)
        step(5, M[R_QK:R_QK + 1, C_SINK:C_SINK + NH], everything)
        step(6, M[R_GO:R_GO + 1, C_GCO:C_GCO + CW], everything)
        step(7, M[R_GO:R_GO + 1, C_GAO:C_GAO + AW], everything)
        step(8, M[R_G2:R_G2 + 1, :], everything)
        for r in range(3):
            step(9, fw[r:r + 1, :], (r, slice(None), slice(None)))
        step(10, S[3:4, 0:2 * DFF], everything)
        sq = S[:, C_SQ:C_SQ + 128]
        loss_o[...] = jnp.sum(jnp.sum(sq, axis=1, keepdims=True), axis=0, keepdims=True) * (0.5 / D)

    def full(a):
        n = len(a.shape)
        return pl.BlockSpec(a.shape, lambda i, chip_ref: (0,) * n)

    params = [*ws, *ms, *vs]
    out = _call(
        body, (p_all, p_all, pm_all, pm_all, g1_all, tbl_all, *params), name="small_adam", grid=(1,), prefetch=(chip,),
        in_specs=[full(p_all),
                  pl.BlockSpec((N_DEV, 8, fw_cols), lambda i, chip_ref: (0, 0, chip_ref[0])),
                  full(pm_all),
                  pl.BlockSpec((N_DEV, 8, cw_cols), lambda i, chip_ref: (0, 0, chip_ref[0])),
                  full(g1_all), full(tbl_all), *[full(a) for a in params]],
        out_specs=[full(a) for a in ws] * 4 + [pl.BlockSpec((1, 1), lambda i, chip_ref: (0, 0))],
        out_shape=[SDS(a.shape, F32) for a in ws] * 4 + [SDS((1, 1), F32)], sem=("arbitrary",), vmem_mib=32)
    return out[0:N_SMALL], out[N_SMALL:2 * N_SMALL], out[2 * N_SMALL:3 * N_SMALL], out[3 * N_SMALL:4 * N_SMALL], out[4 * N_SMALL]


PLACE_STEPS = 4


def _place_specs(shards):
    rows = [s.shape[0] // PLACE_STEPS for s in shards]
    return ([pl.BlockSpec((r, D), lambda i, chip_ref: (i, 0)) for r in rows],
            [pl.BlockSpec((r, D), lambda i, chip_ref: (chip_ref[0] * PLACE_STEPS + i, 0)) for r in rows],
            [SDS((N_CHIPS * s.shape[0], D), BF) for s in shards])


def _place_first(chip, shard, conv_w, ffn_conv_w):
    def body(chip_ref, a, s0, s1, o, t0, t1):
        o[...] = a[...].astype(BF)

        @pl.when(pl.program_id(0) == 0)
        def _():
            for s, t in ((s0, t0), (s1, t1)):
                t[...] = jnp.zeros_like(t)
                t[0, 0:3, :] = s[...]

    ins, outs, shapes = _place_specs([shard])
    taps = (conv_w, ffn_conv_w)
    return _call(
        body, (shard, conv_w, ffn_conv_w), name="place_first", grid=(PLACE_STEPS,), prefetch=(chip,),
        in_specs=ins + [pl.BlockSpec(s.shape, lambda i, chip_ref: (0, 0)) for s in taps],
        out_specs=outs + [pl.BlockSpec((1, 8, s.shape[1]), lambda i, chip_ref: (chip_ref[0], 0, 0)) for s in taps],
        out_shape=shapes + [SDS((N_CHIPS, 8, s.shape[1]), F32) for s in taps],
        sem=("arbitrary",), vmem_mib=32, free=(1, 2))


def _place_rest(chip, shards, w_up, table, bucket, comm):
    n = len(shards)
    c_up = w_up.shape[1]
    edges = [round(k * (c_up // 128) / PLACE_STEPS) * 128 for k in range(PLACE_STEPS + 1)]

    def body(chip_ref, *refs):
        a, (up_ref, tab_ref, bk_ref), o = refs[:n], refs[n:n + 3], refs[n + 3:2 * n + 3]
        up_o, bias_ref = refs[2 * n + 3:]
        for src, dst in zip(a, o):
            dst[...] = src[...].astype(BF)
        for k in range(PLACE_STEPS):
            @pl.when(pl.program_id(0) == k)
            def _(k=k):
                up_o[edges[k]:edges[k + 1], :] = up_ref[:, edges[k]:edges[k + 1]].T.astype(BF)

        @pl.when(pl.program_id(0) == 0)
        def _():
            bk = bk_ref[...]
            eq = [bk == b for b in range(NBUCKET)]
            for h in range(NH):
                acc = jnp.zeros((BLK, 2 * BLK), F32)
                for b in range(NBUCKET):
                    acc = jnp.where(eq[b], tab_ref[h, b], acc)
                bias_ref[h * BLK:(h + 1) * BLK, :] = acc

    ins, outs, shapes = _place_specs(shards)
    return _call(
        body, (*shards, w_up, table, bucket), name="place_rest", grid=(PLACE_STEPS,), prefetch=(chip,),
        in_specs=ins + [_resident(w_up.shape), pl.BlockSpec(memory_space=pltpu.SMEM),
                        pl.BlockSpec(bucket.shape, lambda i, chip_ref: (0, 0))],
        out_specs=outs + [pl.BlockSpec((c_up, D), lambda i, chip_ref: (chip_ref[0], 0)),
                          pl.BlockSpec((NH * BLK, 2 * BLK), lambda i, chip_ref: (0, 0))],
        out_shape=shapes + [SDS((N_CHIPS * c_up, D), BF), SDS((NH * BLK, 2 * BLK), F32)],
        sem=("arbitrary",), vmem_mib=32, comm=comm, free=(n + 1, n + 2))


def kernel(x, norm_mix_g, w_in, conv_w, q_norm_g, k_norm_g, rel_bias_table, sinks, out_norm_conv_g, out_norm_attn_g, w_out, norm_ffn_g, w_up, ffn_conv_w, ffn_conv_b, w_down, loss_target, m_norm_mix_g, m_w_in, m_conv_w, m_q_norm_g, m_k_norm_g, m_rel_bias_table, m_sinks, m_out_norm_conv_g, m_out_norm_attn_g, m_w_out, m_norm_ffn_g, m_w_up, m_ffn_conv_w, m_ffn_conv_b, m_w_down, v_norm_mix_g, v_w_in, v_conv_w, v_q_norm_g, v_k_norm_g, v_rel_bias_table, v_sinks, v_out_norm_conv_g, v_out_norm_attn_g, v_w_out, v_norm_ffn_g, v_w_up, v_ffn_conv_w, v_ffn_conv_b, v_w_down):
    as_arg = lambda i: jnp.reshape(i, (1,)).astype(jnp.int32)
    chip = as_arg(2 * lax.axis_index("x") + lax.axis_index("y"))
    core = as_arg(lax.axis_index("c"))
    me = 2 * chip + core
    xs, tgt = x[0], loss_target[0]
    qg, kg, gco, gao, g1, g2, fb = q_norm_g, k_norm_g, out_norm_conv_g, out_norm_attn_g, norm_mix_g, norm_ffn_g, ffn_conv_b
    pieces = lambda g: g.reshape(N_DEV, g.shape[0] // N_DEV, D)
    whole = lambda f: f.reshape(2 * f.shape[1], D)

    bucket = jnp.asarray(_bucket_table())
    p_in, p_cw, p_fw = _place_first(chip, w_in[0].T, conv_w[0], ffn_conv_w[0])
    p_out, p_down, p_up, bias, w_int, cw_all, fw_all = _place_rest(
        chip, [w_out[0], w_down[0]], w_up[0], rel_bias_table.T, bucket,
        comm=[_t_gather(p_in, relayed_first=True), _t_small_weights(p_cw), _t_small_weights(p_fw)])
    cw8 = jnp.transpose(cw_all, (1, 0, 2)).reshape(8, CW)
    fw8 = jnp.transpose(fw_all, (1, 0, 2)).reshape(8, 2 * DFF)

    early = 3 / 11
    proj, u1, w_out_f, p_up = _inproj(xs, g1, w_int, comm=[_t_gather(p_out), _t_gather(p_up, (0, early))])
    y, w_upt = _mix_fwd(proj, sinks, cw8, qg, kg, gco, gao, bias, comm=[_t_gather(p_up, (early, 1))])
    h1, u2 = _outproj(y, w_out_f, xs, g2)
    up, w_down_f = _ffn_up(u2, w_upt, comm=[_t_gather(p_down)])
    a, pre_g, pre_v = _ffn_act(up, fw8, fb)
    dh2, dh2b, sq = _ffn_down(a, w_down_f, h1, tgt)

    gdbf, = _wgrad("wgrad_down", [a], dh2b, None)
    da, sib_down = _ffn_down_bwd(dh2b, w_down_f, comm=[_t_sibling(pieces(gdbf))])
    pbf_down, own_down = _chip_sum("chip_sum_w_down", pieces(gdbf), sib_down, core, chip)
    dug, duv, dfwg, dfwv, dfbg, dfbv, chips_down = _ffn_act_bwd(up, pre_g, pre_v, da, fw8, comm=[_t_chips(pbf_down)])
    fin_down, = _final_sum("final_sum_w_down", own_down, chips_down, core)
    gubf, = _wgrad("wgrad_up", [dug, duv], u2, False)
    p_all = _pack_ffn(me, dfwg, dfwv, dfbg, dfbv, sq)
    dh1, dh1b, dg2, sib_up, fin_down, p_all = _norm_matmul_bwd(
        "ffn_up_bwd", [dug, duv], w_upt, [0, DFF], h1, g2, dh2, True,
        comm=[_t_sibling(pieces(gubf)), _t_swap(fin_down), _t_allgather(p_all)])
    pbf_up, own_up = _chip_sum("chip_sum_w_up", pieces(gubf), sib_up, core, chip)
    gobf, = _wgrad("wgrad_out", [y], dh1b, True)
    dy, sib_out = _out_bwd(dh1b, w_out_f, comm=[_t_sibling(pieces(gobf))])
    pbf_out, own_out = _chip_sum("chip_sum_w_out", pieces(gobf), sib_out, core, chip)
    dproj, dcw8, dqg, dkg, dgco, dgao, dsink, dbias, chips_up = _mix_bwd(
        proj, dy, sinks, cw8, qg, kg, gco, gao, bias, comm=[_t_chips(pbf_up)])
    fin_up, = _final_sum("final_sum_w_up", own_up, chips_up, core)
    tbl_all = _band_bias_bwd(dbias, bucket, me)
    pm_all = _pack_mix(me, dg2, dgco, dgao, dcw8, dqg, dkg, dsink)
    gibf, chips_out, fin_up, pm_all, tbl_all = _wgrad(
        "wgrad_in", [dproj], u1, False,
        comm=[_t_chips(pbf_out), _t_swap(fin_up), _t_allgather(pm_all), _t_allgather(tbl_all)])
    fin_out, sib_in = _final_sum("final_sum_w_out", own_out, chips_out, core, comm=[_t_sibling(pieces(gibf))])
    pbf_in, own_in = _chip_sum("chip_sum_w_in", pieces(gibf), sib_in, core, chip)
    dx, g1_all, chips_in, fin_out = _norm_matmul_bwd(
        "in_bwd", [dproj], w_int, [0], xs, g1, dh1, False, comm=[_t_chips(pbf_in), _t_swap(fin_out)], slot=me)
    fin_in, = _final_sum("final_sum_w_in", own_in, chips_in, core)
    g1_all, fin_in = _comm_call("gather_last", [_t_allgather(g1_all), _t_swap(fin_in)])

    g_w_out, g_w_down = whole(fin_out), whole(fin_down)
    g_w_down, d_down, nm_down, nv_down = _adamw("adamw_w_down", w_down[0], g_w_down, m_w_down[0], v_w_down[0], 352, True)
    g_w_up, d_up, nm_up, nv_up = _adamw(
        "adamw_w_up", w_up[0], whole(fin_up), m_w_up[0], v_w_up[0], 256, True, stage=False, g_transposed=True)
    g_w_out, d_out, nm_out, nv_out = _adamw("adamw_w_out", w_out[0], g_w_out, m_w_out[0], v_w_out[0], 256, True, stage=False)
    g_w_in, d_in, nm_in, nv_in = [a.T for a in _adamw(
        "adamw_w_in", w_in[0].T, whole(fin_in), m_w_in[0].T, v_w_in[0].T, INW // N_CHIPS // 3, True, stage=False)]
    taps = lambda a: jnp.transpose(a, (1, 0, 2))
    sw = [norm_mix_g, taps(conv_w), q_norm_g, k_norm_g, rel_bias_table.T, sinks, out_norm_conv_g, out_norm_attn_g,
          norm_ffn_g, taps(ffn_conv_w), ffn_conv_b]
    smm = [m_norm_mix_g, taps(m_conv_w), m_q_norm_g, m_k_norm_g, m_rel_bias_table.T, m_sinks, m_out_norm_conv_g,
           m_out_norm_attn_g, m_norm_ffn_g, taps(m_ffn_conv_w), m_ffn_conv_b]
    smv = [v_norm_mix_g, taps(v_conv_w), v_q_norm_g, v_k_norm_g, v_rel_bias_table.T, v_sinks, v_out_norm_conv_g,
           v_out_norm_attn_g, v_norm_ffn_g, taps(v_ffn_conv_w), v_ffn_conv_b]
    *small_out, loss = _small_adam(chip, p_all, pm_all, g1_all, tbl_all, sw, smm, smv)
    sg, sd, snm, snv = [list(r) for r in small_out]
    for r in (sg, sd, snm, snv):
        r[1], r[4], r[9] = taps(r[1]), r[4].T, taps(r[9])

    def order(s, b_in, b_out, b_up, b_down):
        return (s[0], b_in[None], s[1], s[2], s[3], s[4], s[5], s[6], s[7], b_out[None], s[8], b_up[None],
                s[9], s[10], b_down[None])

    return (loss.reshape(()), dx[None],
            *order(sg, g_w_in, g_w_out, g_w_up, g_w_down),
            *order(sd, d_in, d_out, d_up, d_down),
            *order(snm, nm_in, nm_out, nm_up, nm_down),
            *order(snv, nv_in, nv_out, nv_up, nv_down))
```

```python
import functools
import math

import numpy as np

import jax
import jax.numpy as jnp
from jax import lax
from jax.experimental import pallas as pl
from jax.experimental.pallas import tpu as pltpu

F32 = jnp.float32
BF = jnp.bfloat16
SDS = jax.ShapeDtypeStruct

T = 2048
D = 1024
CW = 512
AW = 512
HD = 64
NH = 8
NKV = 2
GQ = 4
INW = 2304
DFF = 2816
BLK = 128
NB = T // BLK
NBUCKET = 32
EPS = 1e-6
NEG_INF = -1e30
N_CHIPS = 4
N_DEV = 8

ADAM_LR = 0.001
ADAM_B1 = 0.9
ADAM_B2 = 0.999
ADAM_EPS = 1e-08
ADAM_WD = 0.01
ADAM_STEP = 10

TM = 512
MIB = 1024 * 1024
MESH = pl.DeviceIdType.MESH
ANY = pl.BlockSpec(memory_space=pl.ANY)

_pcall = pl.pallas_call


def _params(sem=None, vmem_mib=None, collective_id=None):
    kw = {} if collective_id is None else {"collective_id": collective_id}
    if sem is not None:
        kw["dimension_semantics"] = sem
    if vmem_mib is not None:
        kw["vmem_limit_bytes"] = vmem_mib * MIB
    return pltpu.CompilerParams(**kw)


def _resident(shape):
    return pl.BlockSpec(shape, lambda *_: (0,) * len(shape), pipeline_mode=pl.Buffered(1))


def _dot(a, b, ca, cb):
    return lax.dot_general(a, b, (((ca,), (cb,)), ((), ())), preferred_element_type=F32)


def _rms_bwd(dy, x, r, g):
    dg = jnp.sum(dy * (x * r), axis=0, keepdims=True)
    dgx = dy * g
    dx = r * dgx - x * (r * r * r) * jnp.mean(x * dgx, axis=-1, keepdims=True)
    return dx, dg


def _where():
    x, y, c = lax.axis_index("x"), lax.axis_index("y"), lax.axis_index("c")
    return x, y, c, [(1 - x, y), (x, 1 - y), (1 - x, 1 - y)]


def _rcopy(src, dst, ssem, rsem, dev):
    return pltpu.make_async_remote_copy(src_ref=src, dst_ref=dst, send_sem=ssem, recv_sem=rsem, device_id=dev,
                                        device_id_type=MESH)


SIBLING, Y_CHIP, X_CHIP, DIAGONAL_CHIP = 1, 2, 4, 6
OTHER_CHIPS = (Y_CHIP, X_CHIP, DIAGONAL_CHIP)
EVERYONE = tuple(range(1, N_DEV))
BARRIER_OF = {(SIBLING,): 0, (SIBLING, Y_CHIP, X_CHIP): 1, OTHER_CHIPS: 2, (SIBLING,) + OTHER_CHIPS: 3, EVERYONE: 4}


def _peer(rel):
    x, y, c, _ = _where()
    return x ^ ((rel >> 2) & 1), y ^ ((rel >> 1) & 1), c ^ (rel & 1)


class _Task:
    def __init__(self, ins, outs, alias, n_sem, start, finish, middle=None, peers=()):
        self.ins, self.outs, self.alias, self.n_sem, self.start, self.finish = ins, outs, alias, n_sem, start, finish
        self.middle = middle if middle is not None else (lambda *args: None)
        self.peers = peers


def _peers_of(comm):
    return tuple(sorted({p for t in comm for p in t.peers}))


def _enter(comm):
    peers = _peers_of(comm)
    barrier = pltpu.get_barrier_semaphore()
    for rel in peers:
        pl.semaphore_signal(barrier, inc=1, device_id=_peer(rel), device_id_type=MESH)
    pl.semaphore_wait(barrier, len(peers))


ROWS16 = 16


def _t_gather(placed, part=(0, 1), relayed_first=False):
    R = placed.shape[0] // N_CHIPS
    q = R // 4
    lo, hi = (round(f * (q // ROWS16)) * ROWS16 for f in part)

    def quarter(chip_index, core, k):
        return pl.ds(pl.multiple_of(chip_index * R + core * 2 * q + k * q + lo, ROWS16), hi - lo)

    def places():
        x, y, c, _ = _where()
        return c, 2 * x + y, 2 * (1 - x) + y, 2 * x + (1 - y), 2 * (1 - x) + (1 - y), (1 - x, y, c), (x, 1 - y, c), (x, y, 1 - c)

    def copy(buf, k, chip_index, core, quart, ss, rs, b, dev):
        window = buf.at[quarter(chip_index, core, quart)]
        return _rcopy(window, window, ss.at[b + k], rs.at[b + k], dev)

    def first_hop(cout, ss, rs, b, which):
        c, me, _, _, _, x_nbr, y_nbr, _ = places()
        for k, (quart, dev) in enumerate(((0, x_nbr), (1, y_nbr), (1, x_nbr), (0, y_nbr))):
            if k in which:
                copy(cout[0], k, me, c, quart, ss, rs, b, dev).start()

    def start(cin, cout, ss, rs, b):
        first_hop(cout, ss, rs, b, (0, 1) if relayed_first else (0, 1, 2, 3))

    def middle(cin, cout, ss, rs, b):
        c, _, xc, yc, _, x_nbr, y_nbr, sib = places()
        for k, chip_index, quart, dev in ((0, xc, 0, y_nbr), (1, yc, 1, x_nbr)):
            copy(cout[0], k, chip_index, c, quart, ss, rs, b, dev).wait_recv()
            copy(cout[0], 4 + k, chip_index, c, quart, ss, rs, b, dev).start()
            copy(cout[0], 6 + k, chip_index, c, quart, ss, rs, b, sib).start()
        if relayed_first:
            first_hop(cout, ss, rs, b, (2, 3))

    later = ((2, 1, 1), (3, 2, 0), (4, 3, 0), (5, 3, 1))

    def finish(cin, cout, ss, rs, b):
        c, me, xc, yc, dc, _, _, sib = places()
        chip_of = {1: xc, 2: yc, 3: dc}
        for k, whose, quart in later:
            copy(cout[0], k, chip_of[whose], c, quart, ss, rs, b, sib).wait_recv()
            copy(cout[0], 6 + k, chip_of[whose], c, quart, ss, rs, b, sib).start()
        for k, whose, quart in ((0, 1, 0), (1, 2, 1)) + later:
            copy(cout[0], 6 + k, chip_of[whose], 1 - c, quart, ss, rs, b, sib).wait_recv()
        for k in range(12):
            copy(cout[0], k, me, c, 0, ss, rs, b, sib).wait_send()

    return _Task([placed], [SDS(placed.shape, placed.dtype)], [(0, 0)], 12, start, finish, middle, peers=(SIBLING, Y_CHIP, X_CHIP))


def _t_small_weights(buf):
    def start(cin, cout, ss, rs, b):
        x, y, c, chips = _where()
        mine = cout[0].at[2 * x + y]
        for r, (px, py) in enumerate(chips):
            _rcopy(mine, mine, ss.at[b + r], rs.at[b + r], (px, py, c)).start()

    def finish(cin, cout, ss, rs, b):
        x, y, c, chips = _where()
        for r, (px, py) in enumerate(chips):
            got = cout[0].at[2 * px + py]
            _rcopy(got, got, ss.at[b + r], rs.at[b + r], (px, py, c)).wait_recv()
        for r, (px, py) in enumerate(chips):
            mine = cout[0].at[2 * x + y]
            _rcopy(mine, mine, ss.at[b + r], rs.at[b + r], (px, py, c)).wait_send()

    return _Task([buf], [SDS(buf.shape, buf.dtype)], [(0, 0)], 3, start, finish, peers=OTHER_CHIPS)


def _t_sibling(gbf):
    def start(cin, cout, ss, rs, b):
        x, y, c, _ = _where()
        for jj in range(N_CHIPS):
            _rcopy(cin[0].at[2 * jj + (1 - c)], cout[0].at[jj], ss.at[b + jj], rs.at[b + jj], (x, y, 1 - c)).start()

    def finish(cin, cout, ss, rs, b):
        x, y, c, _ = _where()
        for jj in range(N_CHIPS):
            got = cout[0].at[jj]
            _rcopy(got, got, ss.at[b + jj], rs.at[b + jj], (x, y, 1 - c)).wait_recv()
        for jj in range(N_CHIPS):
            got = cout[0].at[jj]
            _rcopy(got, got, ss.at[b + jj], rs.at[b + jj], (x, y, 1 - c)).wait_send()

    return _Task([gbf], [SDS((N_CHIPS,) + gbf.shape[1:], BF)], [], N_CHIPS, start, finish, peers=(SIBLING,))


def _t_chips(pbf):
    def start(cin, cout, ss, rs, b):
        x, y, c, chips = _where()
        for r, (px, py) in enumerate(chips):
            _rcopy(cin[0].at[2 * px + py], cout[0].at[r], ss.at[b + r], rs.at[b + r], (px, py, c)).start()

    def finish(cin, cout, ss, rs, b):
        x, y, c, chips = _where()
        for r, (px, py) in enumerate(chips):
            got = cout[0].at[r]
            _rcopy(got, got, ss.at[b + r], rs.at[b + r], (px, py, c)).wait_recv()
        for r, (px, py) in enumerate(chips):
            got = cout[0].at[r]
            _rcopy(got, got, ss.at[b + r], rs.at[b + r], (px, py, c)).wait_send()

    return _Task([pbf], [SDS((3,) + pbf.shape[1:], BF)], [], 3, start, finish, peers=OTHER_CHIPS)


def _t_swap(fin):
    def start(cin, cout, ss, rs, b):
        x, y, c, _ = _where()
        mine = cout[0].at[c]
        _rcopy(mine, mine, ss.at[b], rs.at[b], (x, y, 1 - c)).start()

    def finish(cin, cout, ss, rs, b):
        x, y, c, _ = _where()
        got = cout[0].at[1 - c]
        _rcopy(got, got, ss.at[b], rs.at[b], (x, y, 1 - c)).wait_recv()
        _rcopy(got, got, ss.at[b], rs.at[b], (x, y, 1 - c)).wait_send()

    return _Task([fin], [SDS(fin.shape, fin.dtype)], [(0, 0)], 1, start, finish, peers=(SIBLING,))


def _t_allgather(buf):
    def peers():
        x, y, c, _ = _where()
        out = []
        for rel in range(1, N_DEV):
            px, py, pc = x ^ ((rel >> 2) & 1), y ^ ((rel >> 1) & 1), c ^ (rel & 1)
            out.append((rel - 1, 4 * px + 2 * py + pc, (px, py, pc)))
        return 4 * x + 2 * y + c, out

    def start(cin, cout, ss, rs, b):
        me, ps = peers()
        mine = cout[0].at[me]
        for k, _, dev in ps:
            _rcopy(mine, mine, ss.at[b + k], rs.at[b + k], dev).start()

    def finish(cin, cout, ss, rs, b):
        me, ps = peers()
        for k, pidx, dev in ps:
            got = cout[0].at[pidx]
            _rcopy(got, got, ss.at[b + k], rs.at[b + k], dev).wait_recv()
        for k, _, dev in ps:
            mine = cout[0].at[me]
            _rcopy(mine, mine, ss.at[b + k], rs.at[b + k], dev).wait_send()

    return _Task([buf], [SDS(buf.shape, buf.dtype)], [(0, 0)], N_DEV - 1, start, finish, peers=EVERYONE)


def _run_tasks(comm, which, cin, cout, ss, rs):
    i0 = o0 = s0 = 0
    for t in comm:
        getattr(t, which)(cin[i0:i0 + len(t.ins)], cout[o0:o0 + len(t.outs)], ss, rs, s0)
        i0, o0, s0 = i0 + len(t.ins), o0 + len(t.outs), s0 + t.n_sem


def _from_hbm(*arrays):
    return [pltpu.with_memory_space_constraint(a, pltpu.HBM) for a in arrays]


def _in_hbm(shapes):
    return [pltpu.HBM(s.shape, s.dtype) for s in shapes]


def _comm_layout(comm, n_in, n_out):
    c_in = [a for t in comm for a in t.ins]
    c_out = [s for t in comm for s in t.outs]
    aliases, i0, o0 = {}, 0, 0
    for t in comm:
        for i, o in t.alias:
            aliases[n_in + i0 + i] = n_out + o0 + o
        i0, o0 = i0 + len(t.ins), o0 + len(t.outs)
    return c_in, c_out, aliases, sum(t.n_sem for t in comm)


def _call(body, operands, *, name, grid, in_specs, out_specs, out_shape, scratch_shapes=(), sem=None, vmem_mib=None, comm=(),
          free=(), prefetch=()):
    operands = [o if s.memory_space == pltpu.SMEM or k in free else pltpu.with_memory_space_constraint(o, pltpu.HBM)
                for k, (o, s) in enumerate(zip(operands, in_specs))]
    n_pre, n_in, n_out, n_scr = len(prefetch), len(in_specs), len(out_specs), len(scratch_shapes)
    c_in, c_out, aliases, n_sem = _comm_layout(comm, n_pre + n_in, n_out)
    sems = [pltpu.SemaphoreType.DMA((n_sem,)), pltpu.SemaphoreType.DMA((n_sem,))] if comm else []

    def wrapped(*refs):
        pre, refs = refs[:n_pre], refs[n_pre:]
        ins, cin = refs[:n_in], refs[n_in:n_in + len(c_in)]
        rest = refs[n_in + len(c_in):]
        outs, cout = rest[:n_out], rest[n_out:n_out + len(c_out)]
        rest = rest[n_out + len(c_out):]
        scr, csem = rest[:n_scr], rest[n_scr:]
        if not comm:
            return body(*pre, *ins, *outs, *scr)
        step = functools.reduce(lambda acc, k: acc * grid[k] + pl.program_id(k), range(len(grid)), 0)
        n_steps = math.prod(grid)

        @pl.when(step == 0)
        def _():
            _enter(comm)
            _run_tasks(comm, "start", cin, cout, *csem)

        pl.when(step == n_steps // 2)(lambda: _run_tasks(comm, "middle", cin, cout, *csem))
        body(*pre, *ins, *outs, *scr)
        pl.when(step == n_steps - 1)(lambda: _run_tasks(comm, "finish", cin, cout, *csem))

    grid_spec = pltpu.PrefetchScalarGridSpec(
        num_scalar_prefetch=n_pre, grid=grid, in_specs=list(in_specs) + [ANY] * len(c_in),
        out_specs=list(out_specs) + [ANY] * len(c_out), scratch_shapes=list(scratch_shapes) + sems)
    return _pcall(
        wrapped, name=name, grid_spec=grid_spec, out_shape=_in_hbm(list(out_shape) + c_out), input_output_aliases=aliases,
        compiler_params=_params(("arbitrary",) * len(grid) if comm else sem, vmem_mib,
                                BARRIER_OF[_peers_of(comm)] if comm else None),
    )(*prefetch, *operands, *_from_hbm(*c_in))


def _comm_call(name, comm):
    c_in, c_out, aliases, n_sem = _comm_layout(comm, 0, 0)

    def body(*refs):
        cin, cout, (ss, rs) = refs[:len(c_in)], refs[len(c_in):len(c_in) + len(c_out)], refs[len(c_in) + len(c_out):]
        _enter(comm)
        for phase in ("start", "middle", "finish"):
            _run_tasks(comm, phase, cin, cout, ss, rs)

    return _pcall(
        body, name=name, in_specs=[ANY] * len(c_in), out_specs=[ANY] * len(c_out), out_shape=_in_hbm(c_out),
        scratch_shapes=[pltpu.SemaphoreType.DMA((n_sem,)), pltpu.SemaphoreType.DMA((n_sem,))],
        input_output_aliases=aliases, compiler_params=_params(collective_id=BARRIER_OF[_peers_of(comm)]),
    )(*_from_hbm(*c_in))


def _inproj(x, g1, w_int, comm=()):
    tm = TM

    def body(x_ref, g_ref, w_ref, proj_ref, u_ref):
        xf = x_ref[...]
        r = lax.rsqrt(jnp.mean(xf * xf, axis=-1, keepdims=True) + EPS)
        u = (xf * r * g_ref[...]).astype(BF)
        u_ref[...] = u
        proj_ref[...] = _dot(u, w_ref[...], 1, 1)

    return _call(
        body, (x, g1, w_int), name="inproj", grid=(T // tm,),
        in_specs=[pl.BlockSpec((tm, D), lambda i: (i, 0)), pl.BlockSpec((1, D), lambda i: (0, 0)),
                  _resident((INW, D))],
        out_specs=[pl.BlockSpec((tm, INW), lambda i: (i, 0)), pl.BlockSpec((tm, D), lambda i: (i, 0))],
        out_shape=[SDS((T, INW), F32), SDS((T, D), BF)], sem=("parallel",), vmem_mib=40, comm=comm, free=(0, 1))


def _outproj(y, w_out, x, g2):
    tm = TM

    def body(y_ref, w_ref, x_ref, g_ref, h1_ref, u2_ref):
        h1 = x_ref[...] + _dot(y_ref[...], w_ref[...], 1, 0)
        h1_ref[...] = h1
        r = lax.rsqrt(jnp.mean(h1 * h1, axis=-1, keepdims=True) + EPS)
        u2_ref[...] = (h1 * r * g_ref[...]).astype(BF)

    return _call(
        body, (y, w_out, x, g2), name="outproj", grid=(T // tm,),
        in_specs=[pl.BlockSpec((tm, D), lambda i: (i, 0)), _resident((D, D)),
                  pl.BlockSpec((tm, D), lambda i: (i, 0)), pl.BlockSpec((1, D), lambda i: (0, 0))],
        out_specs=[pl.BlockSpec((tm, D), lambda i: (i, 0)), pl.BlockSpec((tm, D), lambda i: (i, 0))],
        out_shape=[SDS((T, D), F32), SDS((T, D), BF)], sem=("parallel",), vmem_mib=32, free=(2, 3))


def _ffn_up(u2, w_upt, comm=()):
    tm, tn = T, 512

    def body(u_ref, w_ref, o_ref):
        o_ref[...] = _dot(u_ref[...], w_ref[...], 1, 1).astype(BF)

    return _call(
        body, (u2, w_upt), name="ffn_up", grid=(T // tm, 2 * DFF // tn),
        in_specs=[pl.BlockSpec((tm, D), lambda i, j: (i, 0)), pl.BlockSpec((tn, D), lambda i, j: (j, 0))],
        out_specs=[pl.BlockSpec((tm, tn), lambda i, j: (i, j))], out_shape=[SDS((T, 2 * DFF), BF)],
        sem=("parallel", "parallel"), vmem_mib=32, comm=comm, free=(1,))


def _ffn_down(a, w_down, h1, tgt):
    tm = TM

    def body(a_ref, w_ref, h1_ref, t_ref, dh_ref, dhb_ref, l_ref):
        @pl.when(pl.program_id(0) == 0)
        def _():
            l_ref[...] = jnp.zeros_like(l_ref)

        h2 = h1_ref[...] + _dot(a_ref[...], w_ref[...], 1, 0)
        e = h2 - t_ref[...]
        dh = e * (1.0 / D)
        dh_ref[...] = dh
        dhb_ref[...] = dh.astype(BF)
        e2 = jnp.sum((e * e).reshape(tm // 8, 8, D), axis=0)
        acc = e2[:, 0:128]
        for k in range(1, D // 128):
            acc = acc + e2[:, k * 128:(k + 1) * 128]
        l_ref[...] += acc

    return _call(
        body, (a, w_down, h1, tgt), name="ffn_down", grid=(T // tm,),
        in_specs=[pl.BlockSpec((tm, DFF), lambda i: (i, 0)), _resident((DFF, D)),
                  pl.BlockSpec((tm, D), lambda i: (i, 0)), pl.BlockSpec((tm, D), lambda i: (i, 0))],
        out_specs=[pl.BlockSpec((tm, D), lambda i: (i, 0)), pl.BlockSpec((tm, D), lambda i: (i, 0)),
                   pl.BlockSpec((8, 128), lambda i: (0, 0))],
        out_shape=[SDS((T, D), F32), SDS((T, D), BF), SDS((8, 128), F32)], sem=("arbitrary",), vmem_mib=40, free=(2, 3))


def _bucket_table():
    q = np.arange(BLK, dtype=np.int32)[:, None]
    j = np.arange(2 * BLK, dtype=np.int32)[None, :]
    n = np.maximum(q + BLK - j, 0)
    nf = np.maximum(n, 1).astype(np.float32)
    max_exact = NBUCKET // 2
    large = max_exact + (np.log(nf / np.float32(max_exact)) / np.float32(math.log(BLK / max_exact))
                         * np.float32(NBUCKET - max_exact)).astype(np.int32)
    large = np.minimum(large, NBUCKET - 1)
    return np.where(n < max_exact, n, large).astype(np.int32)


def _two_bf16(x):
    hi = x.astype(BF)
    return hi, (x - hi.astype(F32)).astype(BF)


def _head_sums(x, seg):
    hi, lo = _two_bf16(x)
    s = seg[0:x.shape[1], :]
    return _dot(hi, s, 1, 0) + _dot(lo, s, 1, 0)


def _head_spread(v, seg, width):
    hi, lo = _two_bf16(v)
    s = seg[0:width, :]
    return _dot(hi, s, 1, 1) + _dot(lo, s, 1, 1)


def _head_norm(x, g_t, seg, by_head=False):
    if by_head:
        heads = [x[:, h * HD:(h + 1) * HD] for h in range(x.shape[1] // HD)]
        r = jnp.concatenate([jnp.broadcast_to(lax.rsqrt(jnp.mean(v * v, axis=-1, keepdims=True) + EPS), v.shape)
                             for v in heads], axis=1)
    else:
        r = lax.rsqrt(_head_sums(x * x, seg) * (1.0 / HD) + EPS)
        r = _head_spread(r, seg, x.shape[1])
    return x * r * g_t, r


def _head_norm_bwd(dy, x, r, g_t, seg):
    dg_t = jnp.sum(dy * (x * r), axis=0, keepdims=True)
    dgx = dy * g_t
    mean = _head_spread(_head_sums(x * dgx, seg) * (1.0 / HD), seg, x.shape[1])
    return r * dgx - x * (r * r * r) * mean, dg_t


def _fold_heads(v):
    out = v[:, 0:HD]
    for h in range(1, v.shape[1] // HD):
        out = out + v[:, h * HD:(h + 1) * HD]
    return out


def _mix_forward(P, zc8, zh8, pkv, first, cw, qg_t, kg_t, gco, gao, seg, sink_ref, bias_ref, by_head=False):
    gate_b = P[:, 0:CW]
    gate_c = P[:, CW:2 * CW]
    hc = P[:, 2 * CW:3 * CW]
    z = gate_c * hc
    keep = jnp.where(first, 0.0, 1.0)
    zp = zc8 * zh8 * keep
    p1 = zp[7:8, :]
    p2 = zp[6:7, :]
    row = lax.broadcasted_iota(jnp.int32, (BLK, 1), 0)
    z1 = jnp.where(row == 0, p1, pltpu.roll(z, 1, 0))
    z2 = jnp.where(row == 0, p2, jnp.where(row == 1, p1, pltpu.roll(z, 2, 0)))
    cz = cw[0:1, :] * z2 + cw[1:2, :] * z1 + cw[2:3, :] * z
    y_conv = gate_b * cz

    scale = HD ** -0.5
    qi = lax.broadcasted_iota(jnp.int32, (BLK, 2 * BLK), 0)
    kj = lax.broadcasted_iota(jnp.int32, (BLK, 2 * BLK), 1)
    dd = qi + BLK - kj
    first_key = jnp.where(first, BLK, 0)
    valid = (dd >= 0) & (dd < BLK) & (kj >= first_key)

    q0 = 3 * CW
    k0 = q0 + AW
    v0 = k0 + NKV * HD
    q_raw = P[:, q0:k0]
    qn, rq = _head_norm(q_raw, qg_t, seg, by_head)
    qs = (qn * scale).astype(BF)
    k_raw = jnp.concatenate([pkv[:, 0:NKV * HD], P[:, k0:v0]], axis=0)
    kn, rk = _head_norm(k_raw, kg_t, seg, by_head)
    knb = kn.astype(BF)
    heads = []
    for h in range(NH):
        kv = h // GQ
        kb = knb[:, kv * HD:(kv + 1) * HD]
        vb = jnp.concatenate([pkv[:, NKV * HD + kv * HD:NKV * HD + (kv + 1) * HD],
                              P[:, v0 + kv * HD:v0 + (kv + 1) * HD]], axis=0).astype(BF)
        Q = qs[:, h * HD:(h + 1) * HD]
        S = _dot(Q, kb, 1, 1) + bias_ref[h * BLK:(h + 1) * BLK, :]
        S = jnp.where(valid, S, NEG_INF)
        sink = sink_ref[0, h]
        m = jnp.maximum(jnp.max(S, axis=-1, keepdims=True), sink)
        p = jnp.exp(S - m)
        es = jnp.exp(sink - m)
        denom = jnp.sum(p, axis=-1, keepdims=True) + es
        probs = p / denom
        O = _dot(probs.astype(BF), vb, 1, 0)
        heads.append(dict(kb=kb, vb=vb, Q=Q, probs=probs, psink=es / denom, O=O))
    y_attn = jnp.concatenate([hd["O"] for hd in heads], axis=1)

    rc = lax.rsqrt(jnp.mean(y_conv * y_conv, axis=-1, keepdims=True) + EPS)
    ra = lax.rsqrt(jnp.mean(y_attn * y_attn, axis=-1, keepdims=True) + EPS)
    y = jnp.concatenate([y_conv * rc * gco, y_attn * ra * gao], axis=1)
    return dict(gate_b=gate_b, gate_c=gate_c, hc=hc, z=z, z1=z1, z2=z2, cz=cz, y_conv=y_conv, y_attn=y_attn,
                rc=rc, ra=ra, heads=heads, y=y, row=row, scale=scale, q_raw=q_raw, rq=rq, k_raw=k_raw, rk=rk)


BPS = 2
TILE = BPS * BLK
KV0 = 3 * CW + AW


def _mix_in_specs(tile_of):
    return [
        pl.BlockSpec(memory_space=pltpu.SMEM),
        pl.BlockSpec((TILE, INW), lambda s: (tile_of(s), 0)),
        pl.BlockSpec((8, CW), lambda s: (jnp.maximum(tile_of(s) * (TILE // 8) - 1, 0), 1)),
        pl.BlockSpec((8, CW), lambda s: (jnp.maximum(tile_of(s) * (TILE // 8) - 1, 0), 2)),
        pl.BlockSpec((BLK, 2 * NKV * HD), lambda s: (jnp.maximum(tile_of(s) * BPS - 1, 0), KV0 // (2 * NKV * HD))),
    ]


def _block_inputs(tile, b, zc_ref, zh_ref, pkv_ref, first_tile):
    P = tile[b * BLK:(b + 1) * BLK, :]
    if b == 0:
        return P, zc_ref[...], zh_ref[...], pkv_ref[...], first_tile
    lo = b * BLK
    return P, tile[lo - 8:lo, CW:2 * CW], tile[lo - 8:lo, 2 * CW:3 * CW], tile[lo - BLK:lo, KV0:KV0 + 2 * NKV * HD], False


def _mix_param_specs():
    return [
        pl.BlockSpec((8, CW), lambda s: (0, 0)),
        pl.BlockSpec((1, AW), lambda s: (0, 0)),
        pl.BlockSpec((1, NKV * HD), lambda s: (0, 0)),
        pl.BlockSpec((1, CW), lambda s: (0, 0)),
        pl.BlockSpec((1, AW), lambda s: (0, 0)),
        pl.BlockSpec((AW, 128), lambda s: (0, 0)),
        pl.BlockSpec((NH * BLK, 2 * BLK), lambda s: (0, 0)),
    ]


def _mix_params(cw8, qg, kg, gco, gao, bias):
    seg = np.zeros((AW, 128), np.float32)
    seg[np.arange(AW), np.arange(AW) // HD] = 1.0
    return (cw8, jnp.tile(qg, (1, NH)), jnp.tile(kg, (1, NKV)), gco, gao, jnp.asarray(seg, BF), bias)


def _mix_fwd(proj, sinks, cw8, qg, kg, gco, gao, bias, comm=()):
    def body(sink_ref, p_ref, zc_ref, zh_ref, pkv_ref, cw_ref, qg_ref, kg_ref, gco_ref, gao_ref, seg_ref, bias_ref, y_ref):
        tile = p_ref[...]
        for b in range(BPS):
            f = _mix_forward(*_block_inputs(tile, b, zc_ref, zh_ref, pkv_ref, pl.program_id(0) == 0), cw_ref[...],
                             qg_ref[...], kg_ref[...], gco_ref[...], gao_ref[...], seg_ref[...], sink_ref, bias_ref, by_head=True)
            y_ref[b * BLK:(b + 1) * BLK, :] = f["y"].astype(BF)

    return _call(
        body, (sinks, proj, proj, proj, proj, *_mix_params(cw8, qg, kg, gco, gao, bias)), name="mix_fwd", grid=(T // TILE,),
        in_specs=_mix_in_specs(lambda s: s) + _mix_param_specs(),
        out_specs=[pl.BlockSpec((TILE, D), lambda s: (s, 0))], out_shape=[SDS((T, D), BF)],
        sem=("parallel",), vmem_mib=40, comm=comm, free=tuple(range(5, 12)))


def _mix_bwd(proj, dy, sinks, cw8, qg, kg, gco, gao, bias, comm=()):
    n_steps = T // TILE

    def tile_of(s):
        return n_steps - 1 - s

    def body(sink_ref, p_ref, zc_ref, zh_ref, pkv_ref, dy_ref, cw_ref, qg_ref, kg_ref, gco_ref, gao_ref, seg_ref, bias_ref,
             dproj_ref, dcw_ref, dqg_ref, dkg_ref, dgco_ref, dgao_ref, dsink_ref, dbias_ref,
             ndcz_ref, dkc_ref, dvc_ref):
        s = pl.program_id(0)

        @pl.when(s == 0)
        def _():
            for r in (dcw_ref, dqg_ref, dkg_ref, dgco_ref, dgao_ref, dsink_ref, dbias_ref, ndcz_ref, dkc_ref, dvc_ref):
                r[...] = jnp.zeros_like(r)

        params = (cw_ref[...], qg_ref[...], kg_ref[...], gco_ref[...], gao_ref[...], seg_ref[...])
        tile = p_ref[...]
        carry = (ndcz_ref[...], dkc_ref[...], dvc_ref[...])
        total = None
        for b in reversed(range(BPS)):
            f = _mix_forward(*_block_inputs(tile, b, zc_ref, zh_ref, pkv_ref, s == n_steps - 1), *params, sink_ref, bias_ref)
            pieces, sums, carry = one_block(f, dy_ref[b * BLK:(b + 1) * BLK, :], params, carry)
            for lo, piece in pieces:
                dproj_ref[b * BLK:(b + 1) * BLK, lo:lo + piece.shape[1]] = piece
            total = sums if total is None else [t + v for t, v in zip(total, sums)]
        ndcz_ref[...], dkc_ref[...], dvc_ref[...] = carry
        dcw, dqg_t, dkg_t, dgco, dgao, dsink, *ds = total
        dcw_ref[0:3, :] += dcw
        dqg_ref[...] += _fold_heads(dqg_t)
        dkg_ref[...] += _fold_heads(dkg_t)
        dgco_ref[...] += dgco
        dgao_ref[...] += dgao
        dsink_ref[...] += dsink
        for h in range(NH):
            dbias_ref[h * BLK:(h + 1) * BLK, :] += ds[h]

    def one_block(f, dy, params, carry):
        cw, qg_v, kg_v, gco_v, gao_v, seg = params
        nxt, dk_carry, dv_carry = carry
        dyc, dgco = _rms_bwd(dy[:, 0:CW], f["y_conv"], f["rc"], gco_v)
        dya, dgao = _rms_bwd(dy[:, CW:CW + AW], f["y_attn"], f["ra"], gao_v)

        row = f["row"]
        dgate_b = dyc * f["cz"]
        dcz = dyc * f["gate_b"]
        dcw = jnp.concatenate([jnp.sum(dcz * f[k], axis=0, keepdims=True) for k in ("z2", "z1", "z")], axis=0)
        n0 = nxt[0:1, :]
        n1 = nxt[1:2, :]
        d1 = jnp.where(row == BLK - 1, n0, pltpu.roll(dcz, BLK - 1, 0))
        d2 = jnp.where(row == BLK - 1, n1, jnp.where(row == BLK - 2, n0, pltpu.roll(dcz, BLK - 2, 0)))
        dz = cw[2:3, :] * dcz + cw[1:2, :] * d1 + cw[0:1, :] * d2
        pieces = [(0, dgate_b.astype(BF)), (CW, (dz * f["hc"]).astype(BF)), (2 * CW, (dz * f["gate_c"]).astype(BF))]

        scale = f["scale"]
        lane = lax.broadcasted_iota(jnp.int32, (1, 128), 1)
        dsink = jnp.zeros((1, 128), F32)
        dq_cols, dk_cols, dv_cols, dk_prev, dv_prev, ds = [], [], [], [], [], []
        for kv in range(NKV):
            dKb = dVb = 0.0
            for h in range(kv * GQ, (kv + 1) * GQ):
                hd = f["heads"][h]
                dO = dya[:, h * HD:(h + 1) * HD]
                delta = jnp.sum(dO * hd["O"], axis=-1, keepdims=True)
                dOb = dO.astype(BF)
                dP = _dot(dOb, hd["vb"], 1, 1)
                dS = hd["probs"] * (dP - delta)
                tot = jnp.sum(hd["psink"] * delta, axis=0, keepdims=True)
                dsink = dsink - jnp.where(lane == h, tot, 0.0)
                ds.append(dS)
                dSb = dS.astype(BF)
                dq_cols.append(_dot(dSb, hd["kb"], 1, 0))
                dKb = dKb + _dot(dSb, hd["Q"], 0, 0)
                dVb = dVb + _dot(hd["probs"].astype(BF), dOb, 0, 0)
            dk_cols.append(dKb[BLK:, :] + dk_carry[:, kv * HD:(kv + 1) * HD])
            dv_cols.append(dVb[BLK:, :] + dv_carry[:, kv * HD:(kv + 1) * HD])
            dk_prev.append(dKb[:BLK, :])
            dv_prev.append(dVb[:BLK, :])
        dq_raw, dqg_t = _head_norm_bwd(jnp.concatenate(dq_cols, axis=1) * scale, f["q_raw"], f["rq"], qg_v, seg)
        dk_raw, dkg_t = _head_norm_bwd(jnp.concatenate(dk_cols, axis=1), f["k_raw"][BLK:, :], f["rk"][BLK:, :], kg_v, seg)
        pieces.append((3 * CW, jnp.concatenate([dq_raw, dk_raw] + dv_cols, axis=1).astype(BF)))
        owed = (dcz[0:8, :], jnp.concatenate(dk_prev, axis=1), jnp.concatenate(dv_prev, axis=1))
        return pieces, [dcw, dqg_t, dkg_t, dgco, dgao, dsink, *ds], owed

    small = lambda r, c: pl.BlockSpec((r, c), lambda s: (0, 0))
    return _call(
        body, (sinks, proj, proj, proj, proj, dy, *_mix_params(cw8, qg, kg, gco, gao, bias)), name="mix_bwd", grid=(n_steps,),
        in_specs=_mix_in_specs(tile_of) + [pl.BlockSpec((TILE, D), lambda s: (tile_of(s), 0))] + _mix_param_specs(),
        out_specs=[pl.BlockSpec((TILE, INW), lambda s: (tile_of(s), 0)), small(8, CW), small(1, HD), small(1, HD),
                   small(1, CW), small(1, AW), small(1, 128), small(NH * BLK, 2 * BLK)],
        out_shape=[SDS((T, INW), BF), SDS((8, CW), F32), SDS((1, HD), F32), SDS((1, HD), F32), SDS((1, CW), F32),
                   SDS((1, AW), F32), SDS((1, 128), F32), SDS((NH * BLK, 2 * BLK), F32)],
        scratch_shapes=[pltpu.VMEM((8, CW), F32), pltpu.VMEM((BLK, NKV * HD), F32), pltpu.VMEM((BLK, NKV * HD), F32)],
        sem=("arbitrary",), vmem_mib=56, comm=comm, free=(1, 2, 3, 4) + tuple(range(6, 13)))


FT = 256
NFT = DFF // FT
RC = 1024
NCH = T // RC
LEAD = 16


def _rows8(x):
    return jnp.sum(x.reshape(x.shape[0] // 8, 8, x.shape[1]), axis=0)


def _ffn_act_specs():
    return [
        pl.BlockSpec((T, FT), lambda j: (0, j)), pl.BlockSpec((T, FT), lambda j: (0, NFT + j)),
        pl.BlockSpec((8, FT), lambda j: (0, j)), pl.BlockSpec((8, FT), lambda j: (0, NFT + j)),
        pl.BlockSpec((1, FT), lambda j: (0, j)), pl.BlockSpec((1, FT), lambda j: (0, NFT + j)),
    ]


def _conv_rows(win, w, b, n):
    win = win.astype(F32)
    u = win[LEAD:LEAD + n]
    u1 = pltpu.roll(win, 1, 0)[LEAD:LEAD + n]
    u2 = pltpu.roll(win, 2, 0)[LEAD:LEAD + n]
    return u2, u1, u, w[0:1, :] * u2 + w[1:2, :] * u1 + w[2:3, :] * u + b


def _ffn_act(up, fw8, fb, comm=()):
    def body(ug_ref, uv_ref, wg_ref, wv_ref, bg_ref, bv_ref, a_ref, pg_ref, pv_ref):
        wg, wv, bg, bv = wg_ref[...], wv_ref[...], bg_ref[...], bv_ref[...]

        def chunk(rows, win_g, win_v):
            gp = _conv_rows(win_g, wg, bg, RC)[3]
            vp = _conv_rows(win_v, wv, bv, RC)[3]
            a_ref[rows, :] = (gp * jax.nn.sigmoid(gp) * vp).astype(BF)
            pg_ref[0, rows, :] = gp.astype(BF)
            pv_ref[0, rows, :] = vp.astype(BF)

        zero = jnp.zeros((LEAD, FT), BF)
        chunk(pl.ds(0, RC), jnp.concatenate([zero, ug_ref[0:RC, :]], axis=0), jnp.concatenate([zero, uv_ref[0:RC, :]], axis=0))

        def step(i, carry):
            r0 = pl.multiple_of(i * RC, RC)
            win = pl.ds(r0 - LEAD, RC + LEAD)
            chunk(pl.ds(r0, RC), ug_ref[win, :], uv_ref[win, :])
            return carry

        lax.fori_loop(1, NCH, step, 0)

    tile = pl.BlockSpec((1, T, FT), lambda j: (j, 0, 0))
    return _call(
        body, (up, up, fw8, fw8, fb, fb), name="ffn_act", grid=(NFT,), in_specs=_ffn_act_specs(),
        out_specs=[pl.BlockSpec((T, FT), lambda j: (0, j)), tile, tile],
        out_shape=[SDS((T, DFF), BF), SDS((NFT, T, FT), BF), SDS((NFT, T, FT), BF)],
        sem=("parallel",), vmem_mib=40, comm=comm, free=(2, 3, 4, 5))


def _ffn_act_bwd(up, pre_g, pre_v, da, fw8, comm=()):
    ext = RC + LEAD

    def body(ug_ref, uv_ref, wg_ref, wv_ref, pg_ref, pv_ref, da_ref, dug_ref, duv_ref, dwg_ref, dwv_ref, dbg_ref, dbv_ref):
        wg, wv = wg_ref[...], wv_ref[...]

        def chunk(u_g, u_v, gp, vp, da_e):
            gp, vp, da_e = gp.astype(F32), vp.astype(F32), da_e.astype(F32)
            sig = jax.nn.sigmoid(gp)
            dvp = da_e * (gp * sig)
            dgp = da_e * vp * (sig * (1.0 + gp * (1.0 - sig)))

            def branch(dp, w, u):
                d0, d1, d2 = dp[0:RC], pltpu.roll(dp, ext - 1, 0)[0:RC], pltpu.roll(dp, ext - 2, 0)[0:RC]
                du = (w[2:3, :] * d0 + w[1:2, :] * d1 + w[0:1, :] * d2).astype(BF)
                u = u.astype(F32)
                return du, [_rows8(d0), _rows8(d2 * u), _rows8(d1 * u), _rows8(d0 * u)]

            dug, sums_g = branch(dgp, wg, u_g)
            duv, sums_v = branch(dvp, wv, u_v)
            return dug, duv, sums_g + sums_v

        def step(i, acc):
            r0 = pl.multiple_of(i * RC, RC)
            rows, more = pl.ds(r0, RC), pl.ds(r0, ext)
            dug, duv, part = chunk(ug_ref[rows, :], uv_ref[rows, :], pg_ref[0, more, :], pv_ref[0, more, :], da_ref[more, :])
            dug_ref[rows, :] = dug
            duv_ref[rows, :] = duv
            return [a + p for a, p in zip(acc, part)]

        acc = lax.fori_loop(0, NCH - 1, step, [jnp.zeros((8, FT), F32)] * 8)
        r0 = T - RC
        zero = jnp.zeros((LEAD, FT), BF)
        tail = lambda rows: jnp.concatenate([rows, zero], axis=0)
        dug, duv, part = chunk(ug_ref[r0:T, :], uv_ref[r0:T, :], tail(pg_ref[0, r0:T, :]), tail(pv_ref[0, r0:T, :]),
                               tail(da_ref[r0:T, :]))
        dug_ref[r0:T, :] = dug
        duv_ref[r0:T, :] = duv
        tot = [jnp.sum(a + p, axis=0, keepdims=True) for a, p in zip(acc, part)]
        for k, (dw_ref, db_ref) in enumerate(((dwg_ref, dbg_ref), (dwv_ref, dbv_ref))):
            db_ref[...] = tot[4 * k]
            dw_ref[...] = jnp.zeros_like(dw_ref)
            for r in range(3):
                dw_ref[r:r + 1, :] = tot[4 * k + 1 + r]

    col = lambda r: pl.BlockSpec((r, FT), lambda j: (0, j))
    return _call(
        body, (up, up, fw8, fw8, pre_g, pre_v, da), name="ffn_act_bwd", grid=(NFT,),
        in_specs=_ffn_act_specs()[0:4] + [pl.BlockSpec((1, T, FT), lambda j: (j, 0, 0))] * 2 + [col(T)],
        out_specs=[col(T), col(T), col(8), col(8), col(1), col(1)],
        out_shape=[SDS((T, DFF), BF), SDS((T, DFF), BF), SDS((8, DFF), F32), SDS((8, DFF), F32),
                   SDS((1, DFF), F32), SDS((1, DFF), F32)],
        sem=("parallel",), vmem_mib=40, comm=comm, free=(0, 1, 2, 3))


def _ffn_down_bwd(dh2b, w_down, comm=()):
    tm = TM

    def body(d_ref, w_ref, o_ref):
        o_ref[...] = _dot(d_ref[...], w_ref[...], 1, 1).astype(BF)

    return _call(
        body, (dh2b, w_down), name="ffn_down_bwd", grid=(T // tm,),
        in_specs=[pl.BlockSpec((tm, D), lambda i: (i, 0)), _resident((DFF, D))],
        out_specs=[pl.BlockSpec((tm, DFF), lambda i: (i, 0))], out_shape=[SDS((T, DFF), BF)],
        sem=("parallel",), vmem_mib=40, comm=comm, free=(0, 1))


def _norm_matmul_bwd(name, a_list, w_t, k_offsets, xin, g, dres, want_bf16, comm=(), slot=None, band=()):
    tm = TM
    ks = [a.shape[1] for a in a_list]
    n_a = len(a_list)
    n_pre = 0 if slot is None else 1
    n_in = n_a + 4 + len(band)

    def body(*refs):
        refs = refs[n_pre:]
        a_refs = refs[:n_a]
        w_ref, x_ref, g_ref, r_ref = refs[n_a:n_a + 4]
        outs = refs[n_in:]
        dg_out = outs[1 + want_bf16]
        dx_ref, dg_ref = outs[0], (dg_out if slot is None else dg_out.at[0])

        @pl.when(pl.program_id(0) == 0)
        def _():
            dg_ref[...] = jnp.zeros_like(dg_ref)
            if band:
                db_ref, bk_ref, tbl_ref = refs[n_a + 4], refs[n_a + 5], outs[2 + want_bf16]
                bk = bk_ref[...]
                for b in range(NBUCKET):
                    m = bk == b
                    for h in range(NH):
                        v = jnp.where(m, db_ref[h * BLK:(h + 1) * BLK, :], 0.0)
                        tbl_ref[0, h:h + 1, b:b + 1] = jnp.sum(jnp.sum(v, axis=1, keepdims=True), axis=0, keepdims=True)

        du = _dot(a_refs[0][...], w_ref[k_offsets[0]:k_offsets[0] + ks[0], :], 1, 0)
        for k in range(1, n_a):
            du = du + _dot(a_refs[k][...], w_ref[k_offsets[k]:k_offsets[k] + ks[k], :], 1, 0)
        x = x_ref[...]
        r = lax.rsqrt(jnp.mean(x * x, axis=-1, keepdims=True) + EPS)
        dx, dg = _rms_bwd(du, x, r, g_ref[...])
        dx = r_ref[...] + dx
        dx_ref[...] = dx
        if want_bf16:
            outs[1][...] = dx.astype(BF)
        dg_ref[...] += dg

    tile = lambda c: pl.BlockSpec((tm, c), lambda i, *_: (i, 0))
    if slot is None:
        dg_spec, dg_shape = pl.BlockSpec((1, D), lambda i: (0, 0)), SDS((1, D), F32)
    else:
        dg_spec, dg_shape = pl.BlockSpec((1, 1, D), lambda i, slot_ref: (slot_ref[0], 0, 0)), SDS((N_DEV, 1, D), F32)
    out_specs = [tile(D)] + ([tile(D)] if want_bf16 else []) + [dg_spec]
    out_shape = [SDS((T, D), F32)] + ([SDS((T, D), BF)] if want_bf16 else []) + [dg_shape]
    if band:
        out_specs.append(pl.BlockSpec((1, NH, NBUCKET), lambda i, slot_ref: (slot_ref[0], 0, 0)))
        out_shape.append(SDS((N_DEV, NH, NBUCKET), F32))
    return _call(
        body, (*a_list, w_t, xin, g, dres, *band), name=name, grid=(T // tm,), prefetch=() if slot is None else (slot,),
        in_specs=[tile(k) for k in ks] + [_resident(w_t.shape), tile(D), pl.BlockSpec((1, D), lambda i, *_: (0, 0)), tile(D)]
        + [pl.BlockSpec(b.shape, lambda i, *_: (0, 0)) for b in band],
        out_specs=out_specs, out_shape=out_shape, sem=("arbitrary",), vmem_mib=56, comm=comm, free=tuple(range(n_a + 4)))


def _out_bwd(dh1b, w_out, comm=()):
    tm = TM

    def body(d_ref, w_ref, o_ref):
        o_ref[...] = _dot(d_ref[...], w_ref[...], 1, 1)

    return _call(
        body, (dh1b, w_out), name="out_bwd", grid=(T // tm,),
        in_specs=[pl.BlockSpec((tm, D), lambda i: (i, 0)), _resident((D, D))],
        out_specs=[pl.BlockSpec((tm, D), lambda i: (i, 0))], out_shape=[SDS((T, D), F32)],
        sem=("parallel",), vmem_mib=32, comm=comm, free=(0, 1))


def _wgrad(name, a_list, b, old_a, comm=()):
    m_k = a_list[0].shape[1]
    tm = max(t for t in range(128, m_k // 2 + 1, 128) if m_k % t == 0)
    steps = [a.shape[1] // tm for a in a_list]
    starts = [sum(steps[:k]) for k in range(len(a_list))]
    n_a = len(a_list)

    def body(*refs):
        a_refs, b_ref, o_ref = refs[:n_a], refs[n_a], refs[n_a + 1]
        i = pl.program_id(0)
        for k in range(n_a):
            @pl.when((i >= starts[k]) & (i < starts[k] + steps[k]))
            def _(k=k):
                o_ref[...] = _dot(a_refs[k][...], b_ref[...], 0, 0).astype(BF)

    def a_spec(k):
        return pl.BlockSpec((T, tm), lambda i: (0, jnp.clip(i - starts[k], 0, steps[k] - 1)))

    m_total = tm * sum(steps)
    return _call(
        body, (*a_list, b), name=name, grid=(sum(steps),),
        in_specs=[a_spec(k) for k in range(n_a)] + [_resident((T, D))],
        out_specs=[pl.BlockSpec((tm, D), lambda i: (i, 0))], out_shape=[SDS((m_total, D), BF)],
        sem=("parallel",), vmem_mib=40, comm=comm, free=() if old_a is None else tuple(range(n_a)) if old_a else (n_a,))


def _chip_sum(name, gbf, from_sib, core, chip):
    h = gbf.shape[1]
    th = h

    def body(core_ref, chip_ref, g_ref, s_ref, pbf_ref, own_ref):
        p = g_ref[0].astype(F32) + s_ref[0].astype(F32)
        pbf_ref[0] = p.astype(BF)

        @pl.when(pl.program_id(1) == chip_ref[0])
        def _():
            own_ref[...] = p

    grid_spec = pltpu.PrefetchScalarGridSpec(
        num_scalar_prefetch=2, grid=(h // th, N_CHIPS),
        in_specs=[pl.BlockSpec((1, th, D), lambda t, jj, core_ref, chip_ref: (2 * jj + core_ref[0], t, 0)),
                  pl.BlockSpec((1, th, D), lambda t, jj, core_ref, chip_ref: (jj, t, 0))],
        out_specs=[pl.BlockSpec((1, th, D), lambda t, jj, core_ref, chip_ref: (jj, t, 0)),
                   pl.BlockSpec((th, D), lambda t, jj, core_ref, chip_ref: (t, 0))],
    )
    return _pcall(
        body, name=name, grid_spec=grid_spec, out_shape=_in_hbm([SDS((N_CHIPS, h, D), BF), SDS((h, D), F32)]),
        compiler_params=_params(("arbitrary", "arbitrary"), 32),
    )(core, chip, *_from_hbm(gbf, from_sib))


def _final_sum(name, own, from_chips, core, comm=()):
    h = own.shape[0]
    n = 4 if h % (4 * ROWS16) == 0 else 2
    th = h // n

    def body(core_ref, o_ref, r_ref, f_ref):
        f_ref[0] = ((o_ref[...] + r_ref[0].astype(F32)) + r_ref[1].astype(F32)) + r_ref[2].astype(F32)

    return _call(
        body, (own, from_chips), name=name, grid=(n,), prefetch=(core,),
        in_specs=[pl.BlockSpec((th, D), lambda i, core_ref: (i, 0)), pl.BlockSpec((3, th, D), lambda i, core_ref: (0, i, 0))],
        out_specs=[pl.BlockSpec((1, th, D), lambda i, core_ref: (core_ref[0], i, 0))], out_shape=[SDS((2, h, D), F32)],
        sem=("arbitrary",), vmem_mib=40, comm=comm)


def _adam_math(w, g, m, v):
    nm = ADAM_B1 * m + (1.0 - ADAM_B1) * g
    nv = ADAM_B2 * v + (1.0 - ADAM_B2) * (g * g)
    m_hat = nm / (1.0 - ADAM_B1 ** ADAM_STEP)
    v_hat = nv / (1.0 - ADAM_B2 ** ADAM_STEP)
    return -ADAM_LR * (m_hat / (jnp.sqrt(v_hat) + ADAM_EPS) + ADAM_WD * w), nm, nv


def _adamw(name, w, g, m, v, tr, copy_g=False, stage=True, g_transposed=False):
    rows, cols = w.shape

    def body(w_ref, g_ref, m_ref, v_ref, *outs):
        d_ref, nm_ref, nv_ref = outs[-3:]
        for c in [pl.ds(c0, 128) for c0 in range(0, cols, 128)] if g_transposed else [slice(None)]:
            g_val = g_ref[c, :].T if g_transposed else g_ref[...]
            if copy_g:
                outs[0][:, c] = g_val
            d_ref[:, c], nm_ref[:, c], nv_ref[:, c] = _adam_math(w_ref[:, c], g_val, m_ref[:, c], v_ref[:, c])

    spec = pl.BlockSpec((tr, cols), lambda i: (i, 0))
    n_out = 4 if copy_g else 3
    g_spec = pl.BlockSpec((cols, tr), lambda i: (0, i)) if g_transposed else spec
    return _call(body, (w, g, m, v), name=name, grid=(rows // tr,), in_specs=[spec, g_spec, spec, spec], out_specs=[spec] * n_out,
                 out_shape=[SDS((rows, cols), F32)] * n_out, sem=("parallel",), vmem_mib=32,
                 free=(0, 2, 3) if stage else ())


C_SQ = 2 * DFF
P_W = C_SQ + 128
R_G2, R_GO, R_DCW, R_QK = 0, 1, 2, 5
C_GCO, C_GAO, C_DQG, C_DKG, C_SINK = 0, CW, 0, 128, 256


def _pack(name, me, ins, width, fill):
    def body(me_ref, *refs):
        o = refs[-1]
        o[...] = jnp.zeros_like(o)
        fill(o, *refs[:-1])

    return _call(body, ins, name=name, grid=(1,), prefetch=(me,),
                 in_specs=[pl.BlockSpec(a.shape, lambda i, me_ref: (0, 0)) for a in ins],
                 out_specs=[pl.BlockSpec((1, 8, width), lambda i, me_ref: (me_ref[0], 0, 0))],
                 out_shape=[SDS((N_DEV, 8, width), F32)], sem=("arbitrary",))[0]


def _pack_ffn(me, dfwg, dfwv, dfbg, dfbv, sq):
    def fill(o, dfwg_r, dfwv_r, dfbg_r, dfbv_r, sq_r):
        o[0, :, 0:DFF] = dfwg_r[...]
        o[0, :, DFF:2 * DFF] = dfwv_r[...]
        o[0, 3:4, 0:DFF] = dfbg_r[...]
        o[0, 3:4, DFF:2 * DFF] = dfbv_r[...]
        o[0, :, C_SQ:C_SQ + 128] = sq_r[...]

    return _pack("pack_ffn", me, (dfwg, dfwv, dfbg, dfbv, sq), P_W, fill)


def _pack_mix(me, dg2, dgco, dgao, dcw8, dqg, dkg, dsink):
    def fill(o, dg2_r, dgco_r, dgao_r, dcw_r, dqg_r, dkg_r, dsink_r):
        o[0, R_G2:R_G2 + 1, :] = dg2_r[...]
        o[0, R_GO:R_GO + 1, C_GCO:C_GCO + CW] = dgco_r[...]
        o[0, R_GO:R_GO + 1, C_GAO:C_GAO + AW] = dgao_r[...]
        o[0, R_DCW:R_DCW + 3, 0:CW] = dcw_r[0:3, :]
        o[0, R_QK:R_QK + 1, C_DQG:C_DQG + HD] = dqg_r[...]
        o[0, R_QK:R_QK + 1, C_DKG:C_DKG + HD] = dkg_r[...]
        o[0, R_QK:R_QK + 1, C_SINK:C_SINK + 128] = dsink_r[...]

    return _pack("pack_mix", me, (dg2, dgco, dgao, dcw8, dqg, dkg, dsink), D, fill)


N_SMALL = 11


def _small_adam(chip, p_all, pm_all, g1_all, tbl_all, ws, ms, vs):
    fw_cols = 2 * DFF // N_CHIPS
    cw_cols = CW // N_CHIPS

    def body(chip_ref, p_ref, fw_ref, pm_ref, cw_ref, g1_ref, tbl_ref, *refs):
        w_r, m_r, v_r = refs[0:N_SMALL], refs[N_SMALL:2 * N_SMALL], refs[2 * N_SMALL:3 * N_SMALL]
        outs = refs[3 * N_SMALL:]
        g_o, d_o, nm_o, nv_o = (outs[k * N_SMALL:(k + 1) * N_SMALL] for k in range(4))
        loss_o = outs[4 * N_SMALL]

        def total(ref):
            s = ref[0]
            for k in range(1, N_DEV):
                s = s + ref[k]
            return s

        S = total(p_ref)
        fw = total(fw_ref)
        M = total(pm_ref)
        cw = total(cw_ref)

        def step(i, g, at):
            d, nm, nv = _adam_math(w_r[i][at], g, m_r[i][at], v_r[i][at])
            g_o[i][at], d_o[i][at], nm_o[i][at], nv_o[i][at] = g, d, nm, nv

        everything = (slice(None), slice(None))
        step(0, total(g1_ref), everything)
        for r in range(3):
            step(1, cw[R_DCW + r:R_DCW + r + 1, :], (r, slice(None), slice(None)))
        step(2, M[R_QK:R_QK + 1, C_DQG:C_DQG + HD], everything)
        step(3, M[R_QK:R_QK + 1, C_DKG:C_DKG + HD], everything)
        step(4, total(tbl_ref), everything)
        step(5, M[R_QK:R_QK + 1, C_SINK:C_SINK + NH], everything)
        step(6, M[R_GO:R_GO + 1, C_GCO:C_GCO + CW], everything)
        step(7, M[R_GO:R_GO + 1, C_GAO:C_GAO + AW], everything)
        step(8, M[R_G2:R_G2 + 1, :], everything)
        for r in range(3):
            step(9, fw[r:r + 1, :], (r, slice(None), slice(None)))
        step(10, S[3:4, 0:2 * DFF], everything)
        sq = S[:, C_SQ:C_SQ + 128]
        loss_o[...] = jnp.sum(jnp.sum(sq, axis=1, keepdims=True), axis=0, keepdims=True) * (0.5 / D)

    def full(a):
        n = len(a.shape)
        return pl.BlockSpec(a.shape, lambda i, chip_ref: (0,) * n)

    params = [*ws, *ms, *vs]
    out = _call(
        body, (p_all, p_all, pm_all, pm_all, g1_all, tbl_all, *params), name="small_adam", grid=(1,), prefetch=(chip,),
        in_specs=[full(p_all),
                  pl.BlockSpec((N_DEV, 8, fw_cols), lambda i, chip_ref: (0, 0, chip_ref[0])),
                  full(pm_all),
                  pl.BlockSpec((N_DEV, 8, cw_cols), lambda i, chip_ref: (0, 0, chip_ref[0])),
                  full(g1_all), full(tbl_all), *[full(a) for a in params]],
        out_specs=[full(a) for a in ws] * 4 + [pl.BlockSpec((1, 1), lambda i, chip_ref: (0, 0))],
        out_shape=[SDS(a.shape, F32) for a in ws] * 4 + [SDS((1, 1), F32)], sem=("arbitrary",), vmem_mib=32)
    return out[0:N_SMALL], out[N_SMALL:2 * N_SMALL], out[2 * N_SMALL:3 * N_SMALL], out[3 * N_SMALL:4 * N_SMALL], out[4 * N_SMALL]


PLACE_STEPS = 4


def _place_specs(shards):
    rows = [s.shape[0] // PLACE_STEPS for s in shards]
    return ([pl.BlockSpec((r, D), lambda i, chip_ref: (i, 0)) for r in rows],
            [pl.BlockSpec((r, D), lambda i, chip_ref: (chip_ref[0] * PLACE_STEPS + i, 0)) for r in rows],
            [SDS((N_CHIPS * s.shape[0], D), BF) for s in shards])


def _place_first(chip, shard, conv_w, ffn_conv_w):
    def body(chip_ref, a, s0, s1, o, t0, t1):
        o[...] = a[...].astype(BF)

        @pl.when(pl.program_id(0) == 0)
        def _():
            for s, t in ((s0, t0), (s1, t1)):
                t[...] = jnp.zeros_like(t)
                t[0, 0:3, :] = s[...]

    ins, outs, shapes = _place_specs([shard])
    taps = (conv_w, ffn_conv_w)
    return _call(
        body, (shard, conv_w, ffn_conv_w), name="place_first", grid=(PLACE_STEPS,), prefetch=(chip,),
        in_specs=ins + [pl.BlockSpec(s.shape, lambda i, chip_ref: (0, 0)) for s in taps],
        out_specs=outs + [pl.BlockSpec((1, 8, s.shape[1]), lambda i, chip_ref: (chip_ref[0], 0, 0)) for s in taps],
        out_shape=shapes + [SDS((N_CHIPS, 8, s.shape[1]), F32) for s in taps],
        sem=("arbitrary",), vmem_mib=32, free=(1, 2))


def _place_rest(chip, shards, w_up, table, bucket, comm):
    n = len(shards)
    c_up = w_up.shape[1]
    edges = [round(k * (c_up // 128) / PLACE_STEPS) * 128 for k in range(PLACE_STEPS + 1)]

    def body(chip_ref, *refs):
        a, (up_ref, tab_ref, bk_ref), o = refs[:n], refs[n:n + 3], refs[n + 3:2 * n + 3]
        up_o, bias_ref = refs[2 * n + 3:]
        for src, dst in zip(a, o):
            dst[...] = src[...].astype(BF)
        for k in range(PLACE_STEPS):
            @pl.when(pl.program_id(0) == k)
            def _(k=k):
                up_o[edges[k]:edges[k + 1], :] = up_ref[:, edges[k]:edges[k + 1]].T.astype(BF)

        @pl.when(pl.program_id(0) == 0)
        def _():
            bk = bk_ref[...]
            eq = [bk == b for b in range(NBUCKET)]
            for h in range(NH):
                acc = jnp.zeros((BLK, 2 * BLK), F32)
                for b in range(NBUCKET):
                    acc = jnp.where(eq[b], tab_ref[h, b], acc)
                bias_ref[h * BLK:(h + 1) * BLK, :] = acc

    ins, outs, shapes = _place_specs(shards)
    return _call(
        body, (*shards, w_up, table, bucket), name="place_rest", grid=(PLACE_STEPS,), prefetch=(chip,),
        in_specs=ins + [_resident(w_up.shape), pl.BlockSpec(memory_space=pltpu.SMEM),
                        pl.BlockSpec(bucket.shape, lambda i, chip_ref: (0, 0))],
        out_specs=outs + [pl.BlockSpec((c_up, D), lambda i, chip_ref: (chip_ref[0], 0)),
                          pl.BlockSpec((NH * BLK, 2 * BLK), lambda i, chip_ref: (0, 0))],
        out_shape=shapes + [SDS((N_CHIPS * c_up, D), BF), SDS((NH * BLK, 2 * BLK), F32)],
        sem=("arbitrary",), vmem_mib=32, comm=comm, free=(n + 1, n + 2))


def kernel(x, norm_mix_g, w_in, conv_w, q_norm_g, k_norm_g, rel_bias_table, sinks, out_norm_conv_g, out_norm_attn_g, w_out, norm_ffn_g, w_up, ffn_conv_w, ffn_conv_b, w_down, loss_target, m_norm_mix_g, m_w_in, m_conv_w, m_q_norm_g, m_k_norm_g, m_rel_bias_table, m_sinks, m_out_norm_conv_g, m_out_norm_attn_g, m_w_out, m_norm_ffn_g, m_w_up, m_ffn_conv_w, m_ffn_conv_b, m_w_down, v_norm_mix_g, v_w_in, v_conv_w, v_q_norm_g, v_k_norm_g, v_rel_bias_table, v_sinks, v_out_norm_conv_g, v_out_norm_attn_g, v_w_out, v_norm_ffn_g, v_w_up, v_ffn_conv_w, v_ffn_conv_b, v_w_down):
    as_arg = lambda i: jnp.reshape(i, (1,)).astype(jnp.int32)
    chip = as_arg(2 * lax.axis_index("x") + lax.axis_index("y"))
    core = as_arg(lax.axis_index("c"))
    me = 2 * chip + core
    xs, tgt = x[0], loss_target[0]
    qg, kg, gco, gao, g1, g2, fb = q_norm_g, k_norm_g, out_norm_conv_g, out_norm_attn_g, norm_mix_g, norm_ffn_g, ffn_conv_b
    pieces = lambda g: g.reshape(N_DEV, g.shape[0] // N_DEV, D)
    whole = lambda f: f.reshape(2 * f.shape[1], D)

    bucket = jnp.asarray(_bucket_table())
    p_in, p_cw, p_fw = _place_first(chip, w_in[0].T, conv_w[0], ffn_conv_w[0])
    p_out, p_down, p_up, bias, w_int, cw_all, fw_all = _place_rest(
        chip, [w_out[0], w_down[0]], w_up[0], rel_bias_table.T, bucket,
        comm=[_t_gather(p_in, relayed_first=True), _t_small_weights(p_cw), _t_small_weights(p_fw)])
    cw8 = jnp.transpose(cw_all, (1, 0, 2)).reshape(8, CW)
    fw8 = jnp.transpose(fw_all, (1, 0, 2)).reshape(8, 2 * DFF)

    early = 3 / 11
    proj, u1, w_out_f, p_up = _inproj(xs, g1, w_int, comm=[_t_gather(p_out), _t_gather(p_up, (0, early))])
    y, w_upt = _mix_fwd(proj, sinks, cw8, qg, kg, gco, gao, bias, comm=[_t_gather(p_up, (early, 1))])
    h1, u2 = _outproj(y, w_out_f, xs, g2)
    up, w_down_f = _ffn_up(u2, w_upt, comm=[_t_gather(p_down)])
    a, pre_g, pre_v = _ffn_act(up, fw8, fb)
    dh2, dh2b, sq = _ffn_down(a, w_down_f, h1, tgt)

    gdbf, = _wgrad("wgrad_down", [a], dh2b, None)
    da, sib_down = _ffn_down_bwd(dh2b, w_down_f, comm=[_t_sibling(pieces(gdbf))])
    pbf_down, own_down = _chip_sum("chip_sum_w_down", pieces(gdbf), sib_down, core, chip)
    dug, duv, dfwg, dfwv, dfbg, dfbv, chips_down = _ffn_act_bwd(up, pre_g, pre_v, da, fw8, comm=[_t_chips(pbf_down)])
    fin_down, = _final_sum("final_sum_w_down", own_down, chips_down, core)
    gubf, = _wgrad("wgrad_up", [dug, duv], u2, False)
    p_all = _pack_ffn(me, dfwg, dfwv, dfbg, dfbv, sq)
    dh1, dh1b, dg2, sib_up, fin_down, p_all = _norm_matmul_bwd(
        "ffn_up_bwd", [dug, duv], w_upt, [0, DFF], h1, g2, dh2, True,
        comm=[_t_sibling(pieces(gubf)), _t_swap(fin_down), _t_allgather(p_all)])
    pbf_up, own_up = _chip_sum("chip_sum_w_up", pieces(gubf), sib_up, core, chip)
    gobf, = _wgrad("wgrad_out", [y], dh1b, True)
    dy, sib_out = _out_bwd(dh1b, w_out_f, comm=[_t_sibling(pieces(gobf))])
    pbf_out, own_out = _chip_sum("chip_sum_w_out", pieces(gobf), sib_out, core, chip)
    dproj, dcw8, dqg, dkg, dgco, dgao, dsink, dbias, chips_up = _mix_bwd(
        proj, dy, sinks, cw8, qg, kg, gco, gao, bias, comm=[_t_chips(pbf_up)])
    fin_up, = _final_sum("final_sum_w_up", own_up, chips_up, core)
    pm_all = _pack_mix(me, dg2, dgco, dgao, dcw8, dqg, dkg, dsink)
    gibf, chips_out, fin_up, pm_all = _wgrad(
        "wgrad_in", [dproj], u1, False, comm=[_t_chips(pbf_out), _t_swap(fin_up), _t_allgather(pm_all)])
    fin_out, sib_in = _final_sum("final_sum_w_out", own_out, chips_out, core, comm=[_t_sibling(pieces(gibf))])
    pbf_in, own_in = _chip_sum("chip_sum_w_in", pieces(gibf), sib_in, core, chip)
    dx, g1_all, tbl_all, chips_in, fin_out = _norm_matmul_bwd(
        "in_bwd", [dproj], w_int, [0], xs, g1, dh1, False, comm=[_t_chips(pbf_in), _t_swap(fin_out)], slot=me,
        band=(dbias, bucket))
    fin_in, = _final_sum("final_sum_w_in", own_in, chips_in, core)
    g1_all, tbl_all, fin_in = _comm_call("gather_last", [_t_allgather(g1_all), _t_allgather(tbl_all), _t_swap(fin_in)])

    g_w_out, g_w_down = whole(fin_out), whole(fin_down)
    g_w_down, d_down, nm_down, nv_down = _adamw("adamw_w_down", w_down[0], g_w_down, m_w_down[0], v_w_down[0], 352, True)
    g_w_up, d_up, nm_up, nv_up = _adamw(
        "adamw_w_up", w_up[0], whole(fin_up), m_w_up[0], v_w_up[0], 256, True, stage=False, g_transposed=True)
    g_w_out, d_out, nm_out, nv_out = _adamw("adamw_w_out", w_out[0], g_w_out, m_w_out[0], v_w_out[0], 256, True, stage=False)
    g_w_in, d_in, nm_in, nv_in = [a.T for a in _adamw(
        "adamw_w_in", w_in[0].T, whole(fin_in), m_w_in[0].T, v_w_in[0].T, INW // N_CHIPS // 3, True, stage=False)]
    taps = lambda a: jnp.transpose(a, (1, 0, 2))
    sw = [norm_mix_g, taps(conv_w), q_norm_g, k_norm_g, rel_bias_table.T, sinks, out_norm_conv_g, out_norm_attn_g,
          norm_ffn_g, taps(ffn_conv_w), ffn_conv_b]
    smm = [m_norm_mix_g, taps(m_conv_w), m_q_norm_g, m_k_norm_g, m_rel_bias_table.T, m_sinks, m_out_norm_conv_g,
           m_out_norm_attn_g, m_norm_ffn_g, taps(m_ffn_conv_w), m_ffn_conv_b]
    smv = [v_norm_mix_g, taps(v_conv_w), v_q_norm_g, v_k_norm_g, v_rel_bias_table.T, v_sinks, v_out_norm_conv_g,
           v_out_norm_attn_g, v_norm_ffn_g, taps(v_ffn_conv_w), v_ffn_conv_b]
    *small_out, loss = _small_adam(chip, p_all, pm_all, g1_all, tbl_all, sw, smm, smv)
    sg, sd, snm, snv = [list(r) for r in small_out]
    for r in (sg, sd, snm, snv):
        r[1], r[4], r[9] = taps(r[1]), r[4].T, taps(r[9])

    def order(s, b_in, b_out, b_up, b_down):
        return (s[0], b_in[None], s[1], s[2], s[3], s[4], s[5], s[6], s[7], b_out[None], s[8], b_up[None],
                s[9], s[10], b_down[None])

    return (loss.reshape(()), dx[None],
            *order(sg, g_w_in, g_w_out, g_w_up, g_w_down),
            *order(sd, d_in, d_out, d_up, d_down),
            *order(snm, nm_in, nm_out, nm_up, nm_down),
            *order(snv, nv_in, nv_out, nv_up, nv_down))
```

```python
import functools
import math

import numpy as np

import jax
import jax.numpy as jnp
from jax import lax
from jax.experimental import pallas as pl
from jax.experimental.pallas import tpu as pltpu

F32 = jnp.float32
BF = jnp.bfloat16
SDS = jax.ShapeDtypeStruct

T = 2048
D = 1024
CW = 512
AW = 512
HD = 64
NH = 8
NKV = 2
GQ = 4
INW = 2304
DFF = 2816
BLK = 128
NB = T // BLK
NBUCKET = 32
EPS = 1e-6
NEG_INF = -1e30
N_CHIPS = 4
N_DEV = 8

ADAM_LR = 0.001
ADAM_B1 = 0.9
ADAM_B2 = 0.999
ADAM_EPS = 1e-08
ADAM_WD = 0.01
ADAM_STEP = 10

TM = 512
MIB = 1024 * 1024
MESH = pl.DeviceIdType.MESH
ANY = pl.BlockSpec(memory_space=pl.ANY)

_pcall = pl.pallas_call


def _params(sem=None, vmem_mib=None, collective_id=None):
    kw = {} if collective_id is None else {"collective_id": collective_id}
    if sem is not None:
        kw["dimension_semantics"] = sem
    if vmem_mib is not None:
        kw["vmem_limit_bytes"] = vmem_mib * MIB
    return pltpu.CompilerParams(**kw)


def _resident(shape):
    return pl.BlockSpec(shape, lambda *_: (0,) * len(shape), pipeline_mode=pl.Buffered(1))


def _dot(a, b, ca, cb):
    return lax.dot_general(a, b, (((ca,), (cb,)), ((), ())), preferred_element_type=F32)


def _rms_bwd(dy, x, r, g):
    dg = jnp.sum(dy * (x * r), axis=0, keepdims=True)
    dgx = dy * g
    dx = r * dgx - x * (r * r * r) * jnp.mean(x * dgx, axis=-1, keepdims=True)
    return dx, dg


def _where():
    x, y, c = lax.axis_index("x"), lax.axis_index("y"), lax.axis_index("c")
    return x, y, c, [(1 - x, y), (x, 1 - y), (1 - x, 1 - y)]


def _rcopy(src, dst, ssem, rsem, dev):
    return pltpu.make_async_remote_copy(src_ref=src, dst_ref=dst, send_sem=ssem, recv_sem=rsem, device_id=dev,
                                        device_id_type=MESH)


SIBLING, Y_CHIP, X_CHIP, DIAGONAL_CHIP = 1, 2, 4, 6
OTHER_CHIPS = (Y_CHIP, X_CHIP, DIAGONAL_CHIP)
EVERYONE = tuple(range(1, N_DEV))
BARRIER_OF = {(SIBLING,): 0, (SIBLING, Y_CHIP, X_CHIP): 1, OTHER_CHIPS: 2, (SIBLING,) + OTHER_CHIPS: 3, EVERYONE: 4}


def _peer(rel):
    x, y, c, _ = _where()
    return x ^ ((rel >> 2) & 1), y ^ ((rel >> 1) & 1), c ^ (rel & 1)


class _Task:
    def __init__(self, ins, outs, alias, n_sem, start, finish, middle=None, peers=()):
        self.ins, self.outs, self.alias, self.n_sem, self.start, self.finish = ins, outs, alias, n_sem, start, finish
        self.middle = middle if middle is not None else (lambda *args: None)
        self.peers = peers


def _peers_of(comm):
    return tuple(sorted({p for t in comm for p in t.peers}))


def _enter(comm):
    peers = _peers_of(comm)
    barrier = pltpu.get_barrier_semaphore()
    for rel in peers:
        pl.semaphore_signal(barrier, inc=1, device_id=_peer(rel), device_id_type=MESH)
    pl.semaphore_wait(barrier, len(peers))


ROWS16 = 16


def _t_gather(placed, part=(0, 1), relayed_first=False):
    R = placed.shape[0] // N_CHIPS
    q = R // 4
    lo, hi = (round(f * (q // ROWS16)) * ROWS16 for f in part)

    def quarter(chip_index, core, k):
        return pl.ds(pl.multiple_of(chip_index * R + core * 2 * q + k * q + lo, ROWS16), hi - lo)

    def places():
        x, y, c, _ = _where()
        return c, 2 * x + y, 2 * (1 - x) + y, 2 * x + (1 - y), 2 * (1 - x) + (1 - y), (1 - x, y, c), (x, 1 - y, c), (x, y, 1 - c)

    def copy(buf, k, chip_index, core, quart, ss, rs, b, dev):
        window = buf.at[quarter(chip_index, core, quart)]
        return _rcopy(window, window, ss.at[b + k], rs.at[b + k], dev)

    def first_hop(cout, ss, rs, b, which):
        c, me, _, _, _, x_nbr, y_nbr, _ = places()
        for k, (quart, dev) in enumerate(((0, x_nbr), (1, y_nbr), (1, x_nbr), (0, y_nbr))):
            if k in which:
                copy(cout[0], k, me, c, quart, ss, rs, b, dev).start()

    def start(cin, cout, ss, rs, b):
        first_hop(cout, ss, rs, b, (0, 1) if relayed_first else (0, 1, 2, 3))

    def middle(cin, cout, ss, rs, b):
        c, _, xc, yc, _, x_nbr, y_nbr, sib = places()
        for k, chip_index, quart, dev in ((0, xc, 0, y_nbr), (1, yc, 1, x_nbr)):
            copy(cout[0], k, chip_index, c, quart, ss, rs, b, dev).wait_recv()
            copy(cout[0], 4 + k, chip_index, c, quart, ss, rs, b, dev).start()
            copy(cout[0], 6 + k, chip_index, c, quart, ss, rs, b, sib).start()
        if relayed_first:
            first_hop(cout, ss, rs, b, (2, 3))

    later = ((2, 1, 1), (3, 2, 0), (4, 3, 0), (5, 3, 1))

    def finish(cin, cout, ss, rs, b):
        c, me, xc, yc, dc, _, _, sib = places()
        chip_of = {1: xc, 2: yc, 3: dc}
        for k, whose, quart in later:
            copy(cout[0], k, chip_of[whose], c, quart, ss, rs, b, sib).wait_recv()
            copy(cout[0], 6 + k, chip_of[whose], c, quart, ss, rs, b, sib).start()
        for k, whose, quart in ((0, 1, 0), (1, 2, 1)) + later:
            copy(cout[0], 6 + k, chip_of[whose], 1 - c, quart, ss, rs, b, sib).wait_recv()
        for k in range(12):
            copy(cout[0], k, me, c, 0, ss, rs, b, sib).wait_send()

    return _Task([placed], [SDS(placed.shape, placed.dtype)], [(0, 0)], 12, start, finish, middle, peers=(SIBLING, Y_CHIP, X_CHIP))


def _t_small_weights(buf):
    def start(cin, cout, ss, rs, b):
        x, y, c, chips = _where()
        mine = cout[0].at[2 * x + y]
        for r, (px, py) in enumerate(chips):
            _rcopy(mine, mine, ss.at[b + r], rs.at[b + r], (px, py, c)).start()

    def finish(cin, cout, ss, rs, b):
        x, y, c, chips = _where()
        for r, (px, py) in enumerate(chips):
            got = cout[0].at[2 * px + py]
            _rcopy(got, got, ss.at[b + r], rs.at[b + r], (px, py, c)).wait_recv()
        for r, (px, py) in enumerate(chips):
            mine = cout[0].at[2 * x + y]
            _rcopy(mine, mine, ss.at[b + r], rs.at[b + r], (px, py, c)).wait_send()

    return _Task([buf], [SDS(buf.shape, buf.dtype)], [(0, 0)], 3, start, finish, peers=OTHER_CHIPS)


def _t_sibling(gbf):
    def start(cin, cout, ss, rs, b):
        x, y, c, _ = _where()
        for jj in range(N_CHIPS):
            _rcopy(cin[0].at[2 * jj + (1 - c)], cout[0].at[jj], ss.at[b + jj], rs.at[b + jj], (x, y, 1 - c)).start()

    def finish(cin, cout, ss, rs, b):
        x, y, c, _ = _where()
        for jj in range(N_CHIPS):
            got = cout[0].at[jj]
            _rcopy(got, got, ss.at[b + jj], rs.at[b + jj], (x, y, 1 - c)).wait_recv()
        for jj in range(N_CHIPS):
            got = cout[0].at[jj]
            _rcopy(got, got, ss.at[b + jj], rs.at[b + jj], (x, y, 1 - c)).wait_send()

    return _Task([gbf], [SDS((N_CHIPS,) + gbf.shape[1:], BF)], [], N_CHIPS, start, finish, peers=(SIBLING,))


def _t_chips(pbf):
    def start(cin, cout, ss, rs, b):
        x, y, c, chips = _where()
        for r, (px, py) in enumerate(chips):
            _rcopy(cin[0].at[2 * px + py], cout[0].at[r], ss.at[b + r], rs.at[b + r], (px, py, c)).start()

    def finish(cin, cout, ss, rs, b):
        x, y, c, chips = _where()
        for r, (px, py) in enumerate(chips):
            got = cout[0].at[r]
            _rcopy(got, got, ss.at[b + r], rs.at[b + r], (px, py, c)).wait_recv()
        for r, (px, py) in enumerate(chips):
            got = cout[0].at[r]
            _rcopy(got, got, ss.at[b + r], rs.at[b + r], (px, py, c)).wait_send()

    return _Task([pbf], [SDS((3,) + pbf.shape[1:], BF)], [], 3, start, finish, peers=OTHER_CHIPS)


def _t_swap(fin):
    def start(cin, cout, ss, rs, b):
        x, y, c, _ = _where()
        mine = cout[0].at[c]
        _rcopy(mine, mine, ss.at[b], rs.at[b], (x, y, 1 - c)).start()

    def finish(cin, cout, ss, rs, b):
        x, y, c, _ = _where()
        got = cout[0].at[1 - c]
        _rcopy(got, got, ss.at[b], rs.at[b], (x, y, 1 - c)).wait_recv()
        _rcopy(got, got, ss.at[b], rs.at[b], (x, y, 1 - c)).wait_send()

    return _Task([fin], [SDS(fin.shape, fin.dtype)], [(0, 0)], 1, start, finish, peers=(SIBLING,))


def _t_allgather(buf):
    def peers():
        x, y, c, _ = _where()
        out = []
        for rel in range(1, N_DEV):
            px, py, pc = x ^ ((rel >> 2) & 1), y ^ ((rel >> 1) & 1), c ^ (rel & 1)
            out.append((rel - 1, 4 * px + 2 * py + pc, (px, py, pc)))
        return 4 * x + 2 * y + c, out

    def start(cin, cout, ss, rs, b):
        me, ps = peers()
        mine = cout[0].at[me]
        for k, _, dev in ps:
            _rcopy(mine, mine, ss.at[b + k], rs.at[b + k], dev).start()

    def finish(cin, cout, ss, rs, b):
        me, ps = peers()
        for k, pidx, dev in ps:
            got = cout[0].at[pidx]
            _rcopy(got, got, ss.at[b + k], rs.at[b + k], dev).wait_recv()
        for k, _, dev in ps:
            mine = cout[0].at[me]
            _rcopy(mine, mine, ss.at[b + k], rs.at[b + k], dev).wait_send()

    return _Task([buf], [SDS(buf.shape, buf.dtype)], [(0, 0)], N_DEV - 1, start, finish, peers=EVERYONE)


def _run_tasks(comm, which, cin, cout, ss, rs):
    i0 = o0 = s0 = 0
    for t in comm:
        getattr(t, which)(cin[i0:i0 + len(t.ins)], cout[o0:o0 + len(t.outs)], ss, rs, s0)
        i0, o0, s0 = i0 + len(t.ins), o0 + len(t.outs), s0 + t.n_sem


def _from_hbm(*arrays):
    return [pltpu.with_memory_space_constraint(a, pltpu.HBM) for a in arrays]


def _in_hbm(shapes):
    return [pltpu.HBM(s.shape, s.dtype) for s in shapes]


def _comm_layout(comm, n_in, n_out):
    c_in = [a for t in comm for a in t.ins]
    c_out = [s for t in comm for s in t.outs]
    aliases, i0, o0 = {}, 0, 0
    for t in comm:
        for i, o in t.alias:
            aliases[n_in + i0 + i] = n_out + o0 + o
        i0, o0 = i0 + len(t.ins), o0 + len(t.outs)
    return c_in, c_out, aliases, sum(t.n_sem for t in comm)


def _call(body, operands, *, name, grid, in_specs, out_specs, out_shape, scratch_shapes=(), sem=None, vmem_mib=None, comm=(),
          free=(), prefetch=()):
    operands = [o if s.memory_space == pltpu.SMEM or k in free else pltpu.with_memory_space_constraint(o, pltpu.HBM)
                for k, (o, s) in enumerate(zip(operands, in_specs))]
    n_pre, n_in, n_out, n_scr = len(prefetch), len(in_specs), len(out_specs), len(scratch_shapes)
    c_in, c_out, aliases, n_sem = _comm_layout(comm, n_pre + n_in, n_out)
    sems = [pltpu.SemaphoreType.DMA((n_sem,)), pltpu.SemaphoreType.DMA((n_sem,))] if comm else []

    def wrapped(*refs):
        pre, refs = refs[:n_pre], refs[n_pre:]
        ins, cin = refs[:n_in], refs[n_in:n_in + len(c_in)]
        rest = refs[n_in + len(c_in):]
        outs, cout = rest[:n_out], rest[n_out:n_out + len(c_out)]
        rest = rest[n_out + len(c_out):]
        scr, csem = rest[:n_scr], rest[n_scr:]
        if not comm:
            return body(*pre, *ins, *outs, *scr)
        step = functools.reduce(lambda acc, k: acc * grid[k] + pl.program_id(k), range(len(grid)), 0)
        n_steps = math.prod(grid)

        @pl.when(step == 0)
        def _():
            _enter(comm)
            _run_tasks(comm, "start", cin, cout, *csem)

        pl.when(step == n_steps // 2)(lambda: _run_tasks(comm, "middle", cin, cout, *csem))
        body(*pre, *ins, *outs, *scr)
        pl.when(step == n_steps - 1)(lambda: _run_tasks(comm, "finish", cin, cout, *csem))

    grid_spec = pltpu.PrefetchScalarGridSpec(
        num_scalar_prefetch=n_pre, grid=grid, in_specs=list(in_specs) + [ANY] * len(c_in),
        out_specs=list(out_specs) + [ANY] * len(c_out), scratch_shapes=list(scratch_shapes) + sems)
    return _pcall(
        wrapped, name=name, grid_spec=grid_spec, out_shape=_in_hbm(list(out_shape) + c_out), input_output_aliases=aliases,
        compiler_params=_params(("arbitrary",) * len(grid) if comm else sem, vmem_mib,
                                BARRIER_OF[_peers_of(comm)] if comm else None),
    )(*prefetch, *operands, *_from_hbm(*c_in))


def _comm_call(name, comm):
    c_in, c_out, aliases, n_sem = _comm_layout(comm, 0, 0)

    def body(*refs):
        cin, cout, (ss, rs) = refs[:len(c_in)], refs[len(c_in):len(c_in) + len(c_out)], refs[len(c_in) + len(c_out):]
        _enter(comm)
        for phase in ("start", "middle", "finish"):
            _run_tasks(comm, phase, cin, cout, ss, rs)

    return _pcall(
        body, name=name, in_specs=[ANY] * len(c_in), out_specs=[ANY] * len(c_out), out_shape=_in_hbm(c_out),
        scratch_shapes=[pltpu.SemaphoreType.DMA((n_sem,)), pltpu.SemaphoreType.DMA((n_sem,))],
        input_output_aliases=aliases, compiler_params=_params(collective_id=BARRIER_OF[_peers_of(comm)]),
    )(*_from_hbm(*c_in))


def _inproj(x, g1, w_int, comm=()):
    tm = TM

    def body(x_ref, g_ref, w_ref, proj_ref, u_ref):
        xf = x_ref[...]
        r = lax.rsqrt(jnp.mean(xf * xf, axis=-1, keepdims=True) + EPS)
        u = (xf * r * g_ref[...]).astype(BF)
        u_ref[...] = u
        proj_ref[...] = _dot(u, w_ref[...], 1, 1)

    return _call(
        body, (x, g1, w_int), name="inproj", grid=(T // tm,),
        in_specs=[pl.BlockSpec((tm, D), lambda i: (i, 0)), pl.BlockSpec((1, D), lambda i: (0, 0)),
                  _resident((INW, D))],
        out_specs=[pl.BlockSpec((tm, INW), lambda i: (i, 0)), pl.BlockSpec((tm, D), lambda i: (i, 0))],
        out_shape=[SDS((T, INW), F32), SDS((T, D), BF)], sem=("parallel",), vmem_mib=40, comm=comm, free=(0, 1))


def _outproj(y, w_out, x, g2):
    tm = TM

    def body(y_ref, w_ref, x_ref, g_ref, h1_ref, u2_ref):
        h1 = x_ref[...] + _dot(y_ref[...], w_ref[...], 1, 0)
        h1_ref[...] = h1
        r = lax.rsqrt(jnp.mean(h1 * h1, axis=-1, keepdims=True) + EPS)
        u2_ref[...] = (h1 * r * g_ref[...]).astype(BF)

    return _call(
        body, (y, w_out, x, g2), name="outproj", grid=(T // tm,),
        in_specs=[pl.BlockSpec((tm, D), lambda i: (i, 0)), _resident((D, D)),
                  pl.BlockSpec((tm, D), lambda i: (i, 0)), pl.BlockSpec((1, D), lambda i: (0, 0))],
        out_specs=[pl.BlockSpec((tm, D), lambda i: (i, 0)), pl.BlockSpec((tm, D), lambda i: (i, 0))],
        out_shape=[SDS((T, D), F32), SDS((T, D), BF)], sem=("parallel",), vmem_mib=32, free=(2, 3))


def _ffn_up(u2, w_upt, comm=()):
    tn = 2 * FT

    def body(u_ref, w_ref, o_ref):
        up = _dot(u_ref[...], w_ref[...], 1, 1).astype(BF)
        for k in range(tn // FT):
            o_ref[k] = up[:, k * FT:(k + 1) * FT]

    return _call(
        body, (u2, w_upt), name="ffn_up", grid=(2 * DFF // tn,),
        in_specs=[pl.BlockSpec((T, D), lambda j: (0, 0)), pl.BlockSpec((tn, D), lambda j: (j, 0))],
        out_specs=[pl.BlockSpec((tn // FT, T, FT), lambda j: (j, 0, 0))], out_shape=[SDS((2 * NFT, T, FT), BF)],
        sem=("parallel",), vmem_mib=32, comm=comm, free=(1,))


def _ffn_down(a, w_down, h1, tgt):
    tm = TM

    def body(a_ref, w_ref, h1_ref, t_ref, dh_ref, dhb_ref, l_ref):
        @pl.when(pl.program_id(0) == 0)
        def _():
            l_ref[...] = jnp.zeros_like(l_ref)

        h2 = h1_ref[...] + _dot(a_ref[...], w_ref[...], 1, 0)
        e = h2 - t_ref[...]
        dh = e * (1.0 / D)
        dh_ref[...] = dh
        dhb_ref[...] = dh.astype(BF)
        e2 = jnp.sum((e * e).reshape(tm // 8, 8, D), axis=0)
        acc = e2[:, 0:128]
        for k in range(1, D // 128):
            acc = acc + e2[:, k * 128:(k + 1) * 128]
        l_ref[...] += acc

    return _call(
        body, (a, w_down, h1, tgt), name="ffn_down", grid=(T // tm,),
        in_specs=[pl.BlockSpec((tm, DFF), lambda i: (i, 0)), _resident((DFF, D)),
                  pl.BlockSpec((tm, D), lambda i: (i, 0)), pl.BlockSpec((tm, D), lambda i: (i, 0))],
        out_specs=[pl.BlockSpec((tm, D), lambda i: (i, 0)), pl.BlockSpec((tm, D), lambda i: (i, 0)),
                   pl.BlockSpec((8, 128), lambda i: (0, 0))],
        out_shape=[SDS((T, D), F32), SDS((T, D), BF), SDS((8, 128), F32)], sem=("arbitrary",), vmem_mib=40, free=(2, 3))


def _bucket_table():
    q = np.arange(BLK, dtype=np.int32)[:, None]
    j = np.arange(2 * BLK, dtype=np.int32)[None, :]
    n = np.maximum(q + BLK - j, 0)
    nf = np.maximum(n, 1).astype(np.float32)
    max_exact = NBUCKET // 2
    large = max_exact + (np.log(nf / np.float32(max_exact)) / np.float32(math.log(BLK / max_exact))
                         * np.float32(NBUCKET - max_exact)).astype(np.int32)
    large = np.minimum(large, NBUCKET - 1)
    return np.where(n < max_exact, n, large).astype(np.int32)


def _two_bf16(x):
    hi = x.astype(BF)
    return hi, (x - hi.astype(F32)).astype(BF)


def _head_sums(x, seg):
    hi, lo = _two_bf16(x)
    s = seg[0:x.shape[1], :]
    return _dot(hi, s, 1, 0) + _dot(lo, s, 1, 0)


def _head_spread(v, seg, width):
    hi, lo = _two_bf16(v)
    s = seg[0:width, :]
    return _dot(hi, s, 1, 1) + _dot(lo, s, 1, 1)


def _head_norm(x, g_t, seg, by_head=False):
    if by_head:
        heads = [x[:, h * HD:(h + 1) * HD] for h in range(x.shape[1] // HD)]
        r = jnp.concatenate([jnp.broadcast_to(lax.rsqrt(jnp.mean(v * v, axis=-1, keepdims=True) + EPS), v.shape)
                             for v in heads], axis=1)
    else:
        r = lax.rsqrt(_head_sums(x * x, seg) * (1.0 / HD) + EPS)
        r = _head_spread(r, seg, x.shape[1])
    return x * r * g_t, r


def _head_norm_bwd(dy, x, r, g_t, seg):
    dg_t = jnp.sum(dy * (x * r), axis=0, keepdims=True)
    dgx = dy * g_t
    mean = _head_spread(_head_sums(x * dgx, seg) * (1.0 / HD), seg, x.shape[1])
    return r * dgx - x * (r * r * r) * mean, dg_t


def _fold_heads(v):
    out = v[:, 0:HD]
    for h in range(1, v.shape[1] // HD):
        out = out + v[:, h * HD:(h + 1) * HD]
    return out


def _mix_forward(P, zc8, zh8, pkv, first, cw, qg_t, kg_t, gco, gao, seg, sink_ref, bias_ref, by_head=False):
    gate_b = P[:, 0:CW]
    gate_c = P[:, CW:2 * CW]
    hc = P[:, 2 * CW:3 * CW]
    z = gate_c * hc
    keep = jnp.where(first, 0.0, 1.0)
    zp = zc8 * zh8 * keep
    p1 = zp[7:8, :]
    p2 = zp[6:7, :]
    row = lax.broadcasted_iota(jnp.int32, (BLK, 1), 0)
    z1 = jnp.where(row == 0, p1, pltpu.roll(z, 1, 0))
    z2 = jnp.where(row == 0, p2, jnp.where(row == 1, p1, pltpu.roll(z, 2, 0)))
    cz = cw[0:1, :] * z2 + cw[1:2, :] * z1 + cw[2:3, :] * z
    y_conv = gate_b * cz

    scale = HD ** -0.5
    qi = lax.broadcasted_iota(jnp.int32, (BLK, 2 * BLK), 0)
    kj = lax.broadcasted_iota(jnp.int32, (BLK, 2 * BLK), 1)
    dd = qi + BLK - kj
    first_key = jnp.where(first, BLK, 0)
    valid = (dd >= 0) & (dd < BLK) & (kj >= first_key)

    q0 = 3 * CW
    k0 = q0 + AW
    v0 = k0 + NKV * HD
    q_raw = P[:, q0:k0]
    qn, rq = _head_norm(q_raw, qg_t, seg, by_head)
    qs = (qn * scale).astype(BF)
    k_raw = jnp.concatenate([pkv[:, 0:NKV * HD], P[:, k0:v0]], axis=0)
    kn, rk = _head_norm(k_raw, kg_t, seg, by_head)
    knb = kn.astype(BF)
    heads = []
    for h in range(NH):
        kv = h // GQ
        kb = knb[:, kv * HD:(kv + 1) * HD]
        vb = jnp.concatenate([pkv[:, NKV * HD + kv * HD:NKV * HD + (kv + 1) * HD],
                              P[:, v0 + kv * HD:v0 + (kv + 1) * HD]], axis=0).astype(BF)
        Q = qs[:, h * HD:(h + 1) * HD]
        S = _dot(Q, kb, 1, 1) + bias_ref[h * BLK:(h + 1) * BLK, :]
        S = jnp.where(valid, S, NEG_INF)
        sink = sink_ref[0, h]
        m = jnp.maximum(jnp.max(S, axis=-1, keepdims=True), sink)
        p = jnp.exp(S - m)
        es = jnp.exp(sink - m)
        denom = jnp.sum(p, axis=-1, keepdims=True) + es
        probs = p / denom
        O = _dot(probs.astype(BF), vb, 1, 0)
        heads.append(dict(kb=kb, vb=vb, Q=Q, probs=probs, psink=es / denom, O=O))
    y_attn = jnp.concatenate([hd["O"] for hd in heads], axis=1)

    rc = lax.rsqrt(jnp.mean(y_conv * y_conv, axis=-1, keepdims=True) + EPS)
    ra = lax.rsqrt(jnp.mean(y_attn * y_attn, axis=-1, keepdims=True) + EPS)
    y = jnp.concatenate([y_conv * rc * gco, y_attn * ra * gao], axis=1)
    return dict(gate_b=gate_b, gate_c=gate_c, hc=hc, z=z, z1=z1, z2=z2, cz=cz, y_conv=y_conv, y_attn=y_attn,
                rc=rc, ra=ra, heads=heads, y=y, row=row, scale=scale, q_raw=q_raw, rq=rq, k_raw=k_raw, rk=rk)


BPS = 2
TILE = BPS * BLK
KV0 = 3 * CW + AW


def _mix_in_specs(tile_of):
    return [
        pl.BlockSpec(memory_space=pltpu.SMEM),
        pl.BlockSpec((TILE, INW), lambda s: (tile_of(s), 0)),
        pl.BlockSpec((8, CW), lambda s: (jnp.maximum(tile_of(s) * (TILE // 8) - 1, 0), 1)),
        pl.BlockSpec((8, CW), lambda s: (jnp.maximum(tile_of(s) * (TILE // 8) - 1, 0), 2)),
        pl.BlockSpec((BLK, 2 * NKV * HD), lambda s: (jnp.maximum(tile_of(s) * BPS - 1, 0), KV0 // (2 * NKV * HD))),
    ]


def _block_inputs(tile, b, zc_ref, zh_ref, pkv_ref, first_tile):
    P = tile[b * BLK:(b + 1) * BLK, :]
    if b == 0:
        return P, zc_ref[...], zh_ref[...], pkv_ref[...], first_tile
    lo = b * BLK
    return P, tile[lo - 8:lo, CW:2 * CW], tile[lo - 8:lo, 2 * CW:3 * CW], tile[lo - BLK:lo, KV0:KV0 + 2 * NKV * HD], False


def _mix_param_specs():
    return [
        pl.BlockSpec((8, CW), lambda s: (0, 0)),
        pl.BlockSpec((1, AW), lambda s: (0, 0)),
        pl.BlockSpec((1, NKV * HD), lambda s: (0, 0)),
        pl.BlockSpec((1, CW), lambda s: (0, 0)),
        pl.BlockSpec((1, AW), lambda s: (0, 0)),
        pl.BlockSpec((AW, 128), lambda s: (0, 0)),
        pl.BlockSpec((NH * BLK, 2 * BLK), lambda s: (0, 0)),
    ]


def _mix_params(cw8, qg, kg, gco, gao, bias):
    seg = np.zeros((AW, 128), np.float32)
    seg[np.arange(AW), np.arange(AW) // HD] = 1.0
    return (cw8, jnp.tile(qg, (1, NH)), jnp.tile(kg, (1, NKV)), gco, gao, jnp.asarray(seg, BF), bias)


def _mix_fwd(proj, sinks, cw8, qg, kg, gco, gao, bias, comm=()):
    def body(sink_ref, p_ref, zc_ref, zh_ref, pkv_ref, cw_ref, qg_ref, kg_ref, gco_ref, gao_ref, seg_ref, bias_ref, y_ref):
        tile = p_ref[...]
        for b in range(BPS):
            f = _mix_forward(*_block_inputs(tile, b, zc_ref, zh_ref, pkv_ref, pl.program_id(0) == 0), cw_ref[...],
                             qg_ref[...], kg_ref[...], gco_ref[...], gao_ref[...], seg_ref[...], sink_ref, bias_ref, by_head=True)
            y_ref[b * BLK:(b + 1) * BLK, :] = f["y"].astype(BF)

    return _call(
        body, (sinks, proj, proj, proj, proj, *_mix_params(cw8, qg, kg, gco, gao, bias)), name="mix_fwd", grid=(T // TILE,),
        in_specs=_mix_in_specs(lambda s: s) + _mix_param_specs(),
        out_specs=[pl.BlockSpec((TILE, D), lambda s: (s, 0))], out_shape=[SDS((T, D), BF)],
        sem=("parallel",), vmem_mib=40, comm=comm, free=tuple(range(5, 12)))


def _mix_bwd(proj, dy, sinks, cw8, qg, kg, gco, gao, bias, comm=()):
    n_steps = T // TILE

    def tile_of(s):
        return n_steps - 1 - s

    def body(sink_ref, p_ref, zc_ref, zh_ref, pkv_ref, dy_ref, cw_ref, qg_ref, kg_ref, gco_ref, gao_ref, seg_ref, bias_ref,
             dproj_ref, dcw_ref, dqg_ref, dkg_ref, dgco_ref, dgao_ref, dsink_ref, dbias_ref,
             ndcz_ref, dkc_ref, dvc_ref):
        s = pl.program_id(0)

        @pl.when(s == 0)
        def _():
            for r in (dcw_ref, dqg_ref, dkg_ref, dgco_ref, dgao_ref, dsink_ref, dbias_ref, ndcz_ref, dkc_ref, dvc_ref):
                r[...] = jnp.zeros_like(r)

        params = (cw_ref[...], qg_ref[...], kg_ref[...], gco_ref[...], gao_ref[...], seg_ref[...])
        tile = p_ref[...]
        carry = (ndcz_ref[...], dkc_ref[...], dvc_ref[...])
        total = None
        for b in reversed(range(BPS)):
            f = _mix_forward(*_block_inputs(tile, b, zc_ref, zh_ref, pkv_ref, s == n_steps - 1), *params, sink_ref, bias_ref)
            pieces, sums, carry = one_block(f, dy_ref[b * BLK:(b + 1) * BLK, :], params, carry)
            for lo, piece in pieces:
                dproj_ref[b * BLK:(b + 1) * BLK, lo:lo + piece.shape[1]] = piece
            total = sums if total is None else [t + v for t, v in zip(total, sums)]
        ndcz_ref[...], dkc_ref[...], dvc_ref[...] = carry
        dcw, dqg_t, dkg_t, dgco, dgao, dsink, *ds = total
        dcw_ref[0:3, :] += dcw
        dqg_ref[...] += _fold_heads(dqg_t)
        dkg_ref[...] += _fold_heads(dkg_t)
        dgco_ref[...] += dgco
        dgao_ref[...] += dgao
        dsink_ref[...] += dsink
        for h in range(NH):
            dbias_ref[h * BLK:(h + 1) * BLK, :] += ds[h]

    def one_block(f, dy, params, carry):
        cw, qg_v, kg_v, gco_v, gao_v, seg = params
        nxt, dk_carry, dv_carry = carry
        dyc, dgco = _rms_bwd(dy[:, 0:CW], f["y_conv"], f["rc"], gco_v)
        dya, dgao = _rms_bwd(dy[:, CW:CW + AW], f["y_attn"], f["ra"], gao_v)

        row = f["row"]
        dgate_b = dyc * f["cz"]
        dcz = dyc * f["gate_b"]
        dcw = jnp.concatenate([jnp.sum(dcz * f[k], axis=0, keepdims=True) for k in ("z2", "z1", "z")], axis=0)
        n0 = nxt[0:1, :]
        n1 = nxt[1:2, :]
        d1 = jnp.where(row == BLK - 1, n0, pltpu.roll(dcz, BLK - 1, 0))
        d2 = jnp.where(row == BLK - 1, n1, jnp.where(row == BLK - 2, n0, pltpu.roll(dcz, BLK - 2, 0)))
        dz = cw[2:3, :] * dcz + cw[1:2, :] * d1 + cw[0:1, :] * d2
        pieces = [(0, dgate_b.astype(BF)), (CW, (dz * f["hc"]).astype(BF)), (2 * CW, (dz * f["gate_c"]).astype(BF))]

        scale = f["scale"]
        lane = lax.broadcasted_iota(jnp.int32, (1, 128), 1)
        dsink = jnp.zeros((1, 128), F32)
        dq_cols, dk_cols, dv_cols, dk_prev, dv_prev, ds = [], [], [], [], [], []
        for kv in range(NKV):
            dKb = dVb = 0.0
            for h in range(kv * GQ, (kv + 1) * GQ):
                hd = f["heads"][h]
                dO = dya[:, h * HD:(h + 1) * HD]
                delta = jnp.sum(dO * hd["O"], axis=-1, keepdims=True)
                dOb = dO.astype(BF)
                dP = _dot(dOb, hd["vb"], 1, 1)
                dS = hd["probs"] * (dP - delta)
                tot = jnp.sum(hd["psink"] * delta, axis=0, keepdims=True)
                dsink = dsink - jnp.where(lane == h, tot, 0.0)
                ds.append(dS)
                dSb = dS.astype(BF)
                dq_cols.append(_dot(dSb, hd["kb"], 1, 0))
                dKb = dKb + _dot(dSb, hd["Q"], 0, 0)
                dVb = dVb + _dot(hd["probs"].astype(BF), dOb, 0, 0)
            dk_cols.append(dKb[BLK:, :] + dk_carry[:, kv * HD:(kv + 1) * HD])
            dv_cols.append(dVb[BLK:, :] + dv_carry[:, kv * HD:(kv + 1) * HD])
            dk_prev.append(dKb[:BLK, :])
            dv_prev.append(dVb[:BLK, :])
        dq_raw, dqg_t = _head_norm_bwd(jnp.concatenate(dq_cols, axis=1) * scale, f["q_raw"], f["rq"], qg_v, seg)
        dk_raw, dkg_t = _head_norm_bwd(jnp.concatenate(dk_cols, axis=1), f["k_raw"][BLK:, :], f["rk"][BLK:, :], kg_v, seg)
        pieces.append((3 * CW, jnp.concatenate([dq_raw, dk_raw] + dv_cols, axis=1).astype(BF)))
        owed = (dcz[0:8, :], jnp.concatenate(dk_prev, axis=1), jnp.concatenate(dv_prev, axis=1))
        return pieces, [dcw, dqg_t, dkg_t, dgco, dgao, dsink, *ds], owed

    small = lambda r, c: pl.BlockSpec((r, c), lambda s: (0, 0))
    return _call(
        body, (sinks, proj, proj, proj, proj, dy, *_mix_params(cw8, qg, kg, gco, gao, bias)), name="mix_bwd", grid=(n_steps,),
        in_specs=_mix_in_specs(tile_of) + [pl.BlockSpec((TILE, D), lambda s: (tile_of(s), 0))] + _mix_param_specs(),
        out_specs=[pl.BlockSpec((TILE, INW), lambda s: (tile_of(s), 0)), small(8, CW), small(1, HD), small(1, HD),
                   small(1, CW), small(1, AW), small(1, 128), small(NH * BLK, 2 * BLK)],
        out_shape=[SDS((T, INW), BF), SDS((8, CW), F32), SDS((1, HD), F32), SDS((1, HD), F32), SDS((1, CW), F32),
                   SDS((1, AW), F32), SDS((1, 128), F32), SDS((NH * BLK, 2 * BLK), F32)],
        scratch_shapes=[pltpu.VMEM((8, CW), F32), pltpu.VMEM((BLK, NKV * HD), F32), pltpu.VMEM((BLK, NKV * HD), F32)],
        sem=("arbitrary",), vmem_mib=56, comm=comm, free=(1, 2, 3, 4) + tuple(range(6, 13)))


FT = 256
NFT = DFF // FT
RC = 1024
NCH = T // RC
LEAD = 16


def _rows8(x):
    return jnp.sum(x.reshape(x.shape[0] // 8, 8, x.shape[1]), axis=0)


def _ffn_act_specs():
    return [
        pl.BlockSpec((1, T, FT), lambda j: (j, 0, 0)), pl.BlockSpec((1, T, FT), lambda j: (NFT + j, 0, 0)),
        pl.BlockSpec((8, FT), lambda j: (0, j)), pl.BlockSpec((8, FT), lambda j: (0, NFT + j)),
        pl.BlockSpec((1, FT), lambda j: (0, j)), pl.BlockSpec((1, FT), lambda j: (0, NFT + j)),
    ]


def _conv_rows(win, w, b, n):
    win = win.astype(F32)
    u = win[LEAD:LEAD + n]
    u1 = pltpu.roll(win, 1, 0)[LEAD:LEAD + n]
    u2 = pltpu.roll(win, 2, 0)[LEAD:LEAD + n]
    return u2, u1, u, w[0:1, :] * u2 + w[1:2, :] * u1 + w[2:3, :] * u + b


def _ffn_act(up, fw8, fb, comm=()):
    def body(ug_ref, uv_ref, wg_ref, wv_ref, bg_ref, bv_ref, a_ref, pg_ref, pv_ref):
        wg, wv, bg, bv = wg_ref[...], wv_ref[...], bg_ref[...], bv_ref[...]

        def chunk(rows, win_g, win_v):
            gp = _conv_rows(win_g, wg, bg, RC)[3]
            vp = _conv_rows(win_v, wv, bv, RC)[3]
            a_ref[rows, :] = (gp * jax.nn.sigmoid(gp) * vp).astype(BF)
            pg_ref[0, rows, :] = gp.astype(BF)
            pv_ref[0, rows, :] = vp.astype(BF)

        zero = jnp.zeros((LEAD, FT), BF)
        chunk(pl.ds(0, RC), jnp.concatenate([zero, ug_ref[0, 0:RC, :]], axis=0), jnp.concatenate([zero, uv_ref[0, 0:RC, :]], axis=0))

        def step(i, carry):
            r0 = pl.multiple_of(i * RC, RC)
            win = pl.ds(r0 - LEAD, RC + LEAD)
            chunk(pl.ds(r0, RC), ug_ref[0, win, :], uv_ref[0, win, :])
            return carry

        lax.fori_loop(1, NCH, step, 0)

    tile = pl.BlockSpec((1, T, FT), lambda j: (j, 0, 0))
    return _call(
        body, (up, up, fw8, fw8, fb, fb), name="ffn_act", grid=(NFT,), in_specs=_ffn_act_specs(),
        out_specs=[pl.BlockSpec((T, FT), lambda j: (0, j)), tile, tile],
        out_shape=[SDS((T, DFF), BF), SDS((NFT, T, FT), BF), SDS((NFT, T, FT), BF)],
        sem=("parallel",), vmem_mib=40, comm=comm, free=(2, 3, 4, 5))


def _ffn_act_bwd(up, pre_g, pre_v, da, fw8, comm=()):
    ext = RC + LEAD

    def body(ug_ref, uv_ref, wg_ref, wv_ref, pg_ref, pv_ref, da_ref, dug_ref, duv_ref, dwg_ref, dwv_ref, dbg_ref, dbv_ref):
        wg, wv = wg_ref[...], wv_ref[...]

        def chunk(u_g, u_v, gp, vp, da_e):
            gp, vp, da_e = gp.astype(F32), vp.astype(F32), da_e.astype(F32)
            sig = jax.nn.sigmoid(gp)
            dvp = da_e * (gp * sig)
            dgp = da_e * vp * (sig * (1.0 + gp * (1.0 - sig)))

            def branch(dp, w, u):
                d0, d1, d2 = dp[0:RC], pltpu.roll(dp, ext - 1, 0)[0:RC], pltpu.roll(dp, ext - 2, 0)[0:RC]
                du = (w[2:3, :] * d0 + w[1:2, :] * d1 + w[0:1, :] * d2).astype(BF)
                u = u.astype(F32)
                return du, [_rows8(d0), _rows8(d2 * u), _rows8(d1 * u), _rows8(d0 * u)]

            dug, sums_g = branch(dgp, wg, u_g)
            duv, sums_v = branch(dvp, wv, u_v)
            return dug, duv, sums_g + sums_v

        def step(i, acc):
            r0 = pl.multiple_of(i * RC, RC)
            rows, more = pl.ds(r0, RC), pl.ds(r0, ext)
            dug, duv, part = chunk(ug_ref[0, rows, :], uv_ref[0, rows, :], pg_ref[0, more, :], pv_ref[0, more, :], da_ref[more, :])
            dug_ref[rows, :] = dug
            duv_ref[rows, :] = duv
            return [a + p for a, p in zip(acc, part)]

        acc = lax.fori_loop(0, NCH - 1, step, [jnp.zeros((8, FT), F32)] * 8)
        r0 = T - RC
        zero = jnp.zeros((LEAD, FT), BF)
        tail = lambda rows: jnp.concatenate([rows, zero], axis=0)
        dug, duv, part = chunk(ug_ref[0, r0:T, :], uv_ref[0, r0:T, :], tail(pg_ref[0, r0:T, :]), tail(pv_ref[0, r0:T, :]),
                               tail(da_ref[r0:T, :]))
        dug_ref[r0:T, :] = dug
        duv_ref[r0:T, :] = duv
        tot = [jnp.sum(a + p, axis=0, keepdims=True) for a, p in zip(acc, part)]
        for k, (dw_ref, db_ref) in enumerate(((dwg_ref, dbg_ref), (dwv_ref, dbv_ref))):
            db_ref[...] = tot[4 * k]
            dw_ref[...] = jnp.zeros_like(dw_ref)
            for r in range(3):
                dw_ref[r:r + 1, :] = tot[4 * k + 1 + r]

    col = lambda r: pl.BlockSpec((r, FT), lambda j: (0, j))
    return _call(
        body, (up, up, fw8, fw8, pre_g, pre_v, da), name="ffn_act_bwd", grid=(NFT,),
        in_specs=_ffn_act_specs()[0:4] + [pl.BlockSpec((1, T, FT), lambda j: (j, 0, 0))] * 2 + [col(T)],
        out_specs=[col(T), col(T), col(8), col(8), col(1), col(1)],
        out_shape=[SDS((T, DFF), BF), SDS((T, DFF), BF), SDS((8, DFF), F32), SDS((8, DFF), F32),
                   SDS((1, DFF), F32), SDS((1, DFF), F32)],
        sem=("parallel",), vmem_mib=40, comm=comm, free=(0, 1, 2, 3))


def _ffn_down_bwd(dh2b, w_down, comm=()):
    tm = TM

    def body(d_ref, w_ref, o_ref):
        o_ref[...] = _dot(d_ref[...], w_ref[...], 1, 1).astype(BF)

    return _call(
        body, (dh2b, w_down), name="ffn_down_bwd", grid=(T // tm,),
        in_specs=[pl.BlockSpec((tm, D), lambda i: (i, 0)), _resident((DFF, D))],
        out_specs=[pl.BlockSpec((tm, DFF), lambda i: (i, 0))], out_shape=[SDS((T, DFF), BF)],
        sem=("parallel",), vmem_mib=40, comm=comm, free=(0, 1))


def _norm_matmul_bwd(name, a_list, w_t, k_offsets, xin, g, dres, want_bf16, comm=(), slot=None, band=()):
    tm = TM
    ks = [a.shape[1] for a in a_list]
    n_a = len(a_list)
    n_pre = 0 if slot is None else 1
    n_in = n_a + 4 + len(band)

    def body(*refs):
        refs = refs[n_pre:]
        a_refs = refs[:n_a]
        w_ref, x_ref, g_ref, r_ref = refs[n_a:n_a + 4]
        outs = refs[n_in:]
        dg_out = outs[1 + want_bf16]
        dx_ref, dg_ref = outs[0], (dg_out if slot is None else dg_out.at[0])

        @pl.when(pl.program_id(0) == 0)
        def _():
            dg_ref[...] = jnp.zeros_like(dg_ref)
            if band:
                db_ref, bk_ref, tbl_ref = refs[n_a + 4], refs[n_a + 5], outs[2 + want_bf16]
                bk = bk_ref[...]
                for b in range(NBUCKET):
                    m = bk == b
                    for h in range(NH):
                        v = jnp.where(m, db_ref[h * BLK:(h + 1) * BLK, :], 0.0)
                        tbl_ref[0, h:h + 1, b:b + 1] = jnp.sum(jnp.sum(v, axis=1, keepdims=True), axis=0, keepdims=True)

        du = _dot(a_refs[0][...], w_ref[k_offsets[0]:k_offsets[0] + ks[0], :], 1, 0)
        for k in range(1, n_a):
            du = du + _dot(a_refs[k][...], w_ref[k_offsets[k]:k_offsets[k] + ks[k], :], 1, 0)
        x = x_ref[...]
        r = lax.rsqrt(jnp.mean(x * x, axis=-1, keepdims=True) + EPS)
        dx, dg = _rms_bwd(du, x, r, g_ref[...])
        dx = r_ref[...] + dx
        dx_ref[...] = dx
        if want_bf16:
            outs[1][...] = dx.astype(BF)
        dg_ref[...] += dg

    tile = lambda c: pl.BlockSpec((tm, c), lambda i, *_: (i, 0))
    if slot is None:
        dg_spec, dg_shape = pl.BlockSpec((1, D), lambda i: (0, 0)), SDS((1, D), F32)
    else:
        dg_spec, dg_shape = pl.BlockSpec((1, 1, D), lambda i, slot_ref: (slot_ref[0], 0, 0)), SDS((N_DEV, 1, D), F32)
    out_specs = [tile(D)] + ([tile(D)] if want_bf16 else []) + [dg_spec]
    out_shape = [SDS((T, D), F32)] + ([SDS((T, D), BF)] if want_bf16 else []) + [dg_shape]
    if band:
        out_specs.append(pl.BlockSpec((1, NH, NBUCKET), lambda i, slot_ref: (slot_ref[0], 0, 0)))
        out_shape.append(SDS((N_DEV, NH, NBUCKET), F32))
    return _call(
        body, (*a_list, w_t, xin, g, dres, *band), name=name, grid=(T // tm,), prefetch=() if slot is None else (slot,),
        in_specs=[tile(k) for k in ks] + [_resident(w_t.shape), tile(D), pl.BlockSpec((1, D), lambda i, *_: (0, 0)), tile(D)]
        + [pl.BlockSpec(b.shape, lambda i, *_: (0, 0)) for b in band],
        out_specs=out_specs, out_shape=out_shape, sem=("arbitrary",), vmem_mib=56, comm=comm, free=tuple(range(n_a + 4)))


def _out_bwd(dh1b, w_out, comm=()):
    tm = TM

    def body(d_ref, w_ref, o_ref):
        o_ref[...] = _dot(d_ref[...], w_ref[...], 1, 1)

    return _call(
        body, (dh1b, w_out), name="out_bwd", grid=(T // tm,),
        in_specs=[pl.BlockSpec((tm, D), lambda i: (i, 0)), _resident((D, D))],
        out_specs=[pl.BlockSpec((tm, D), lambda i: (i, 0))], out_shape=[SDS((T, D), F32)],
        sem=("parallel",), vmem_mib=32, comm=comm, free=(0, 1))


def _wgrad(name, a_list, b, old_a, comm=()):
    m_k = a_list[0].shape[1]
    tm = max(t for t in range(128, m_k // 2 + 1, 128) if m_k % t == 0)
    steps = [a.shape[1] // tm for a in a_list]
    starts = [sum(steps[:k]) for k in range(len(a_list))]
    n_a = len(a_list)

    def body(*refs):
        a_refs, b_ref, o_ref = refs[:n_a], refs[n_a], refs[n_a + 1]
        i = pl.program_id(0)
        for k in range(n_a):
            @pl.when((i >= starts[k]) & (i < starts[k] + steps[k]))
            def _(k=k):
                o_ref[...] = _dot(a_refs[k][...], b_ref[...], 0, 0).astype(BF)

    def a_spec(k):
        return pl.BlockSpec((T, tm), lambda i: (0, jnp.clip(i - starts[k], 0, steps[k] - 1)))

    m_total = tm * sum(steps)
    return _call(
        body, (*a_list, b), name=name, grid=(sum(steps),),
        in_specs=[a_spec(k) for k in range(n_a)] + [_resident((T, D))],
        out_specs=[pl.BlockSpec((tm, D), lambda i: (i, 0))], out_shape=[SDS((m_total, D), BF)],
        sem=("parallel",), vmem_mib=40, comm=comm, free=() if old_a is None else tuple(range(n_a)) if old_a else (n_a,))


def _chip_sum(name, gbf, from_sib, core, chip):
    h = gbf.shape[1]
    th = h

    def body(core_ref, chip_ref, g_ref, s_ref, pbf_ref, own_ref):
        p = g_ref[0].astype(F32) + s_ref[0].astype(F32)
        pbf_ref[0] = p.astype(BF)

        @pl.when(pl.program_id(1) == chip_ref[0])
        def _():
            own_ref[...] = p

    grid_spec = pltpu.PrefetchScalarGridSpec(
        num_scalar_prefetch=2, grid=(h // th, N_CHIPS),
        in_specs=[pl.BlockSpec((1, th, D), lambda t, jj, core_ref, chip_ref: (2 * jj + core_ref[0], t, 0)),
                  pl.BlockSpec((1, th, D), lambda t, jj, core_ref, chip_ref: (jj, t, 0))],
        out_specs=[pl.BlockSpec((1, th, D), lambda t, jj, core_ref, chip_ref: (jj, t, 0)),
                   pl.BlockSpec((th, D), lambda t, jj, core_ref, chip_ref: (t, 0))],
    )
    return _pcall(
        body, name=name, grid_spec=grid_spec, out_shape=_in_hbm([SDS((N_CHIPS, h, D), BF), SDS((h, D), F32)]),
        compiler_params=_params(("arbitrary", "arbitrary"), 32),
    )(core, chip, *_from_hbm(gbf, from_sib))


def _final_sum(name, own, from_chips, core, comm=()):
    h = own.shape[0]
    n = 4 if h % (4 * ROWS16) == 0 else 2
    th = h // n

    def body(core_ref, o_ref, r_ref, f_ref):
        f_ref[0] = ((o_ref[...] + r_ref[0].astype(F32)) + r_ref[1].astype(F32)) + r_ref[2].astype(F32)

    return _call(
        body, (own, from_chips), name=name, grid=(n,), prefetch=(core,),
        in_specs=[pl.BlockSpec((th, D), lambda i, core_ref: (i, 0)), pl.BlockSpec((3, th, D), lambda i, core_ref: (0, i, 0))],
        out_specs=[pl.BlockSpec((1, th, D), lambda i, core_ref: (core_ref[0], i, 0))], out_shape=[SDS((2, h, D), F32)],
        sem=("arbitrary",), vmem_mib=40, comm=comm)


def _adam_math(w, g, m, v):
    nm = ADAM_B1 * m + (1.0 - ADAM_B1) * g
    nv = ADAM_B2 * v + (1.0 - ADAM_B2) * (g * g)
    m_hat = nm / (1.0 - ADAM_B1 ** ADAM_STEP)
    v_hat = nv / (1.0 - ADAM_B2 ** ADAM_STEP)
    return -ADAM_LR * (m_hat / (jnp.sqrt(v_hat) + ADAM_EPS) + ADAM_WD * w), nm, nv


def _adamw(name, w, g, m, v, tr, copy_g=False, stage=True, g_transposed=False):
    rows, cols = w.shape

    def body(w_ref, g_ref, m_ref, v_ref, *outs):
        d_ref, nm_ref, nv_ref = outs[-3:]
        for c in [pl.ds(c0, 128) for c0 in range(0, cols, 128)] if g_transposed else [slice(None)]:
            g_val = g_ref[c, :].T if g_transposed else g_ref[...]
            if copy_g:
                outs[0][:, c] = g_val
            d_ref[:, c], nm_ref[:, c], nv_ref[:, c] = _adam_math(w_ref[:, c], g_val, m_ref[:, c], v_ref[:, c])

    spec = pl.BlockSpec((tr, cols), lambda i: (i, 0))
    n_out = 4 if copy_g else 3
    g_spec = pl.BlockSpec((cols, tr), lambda i: (0, i)) if g_transposed else spec
    return _call(body, (w, g, m, v), name=name, grid=(rows // tr,), in_specs=[spec, g_spec, spec, spec], out_specs=[spec] * n_out,
                 out_shape=[SDS((rows, cols), F32)] * n_out, sem=("parallel",), vmem_mib=32,
                 free=(0, 2, 3) if stage else ())


C_SQ = 2 * DFF
P_W = C_SQ + 128
R_G2, R_GO, R_DCW, R_QK = 0, 1, 2, 5
C_GCO, C_GAO, C_DQG, C_DKG, C_SINK = 0, CW, 0, 128, 256


def _pack(name, me, ins, width, fill):
    def body(me_ref, *refs):
        o = refs[-1]
        o[...] = jnp.zeros_like(o)
        fill(o, *refs[:-1])

    return _call(body, ins, name=name, grid=(1,), prefetch=(me,),
                 in_specs=[pl.BlockSpec(a.shape, lambda i, me_ref: (0, 0)) for a in ins],
                 out_specs=[pl.BlockSpec((1, 8, width), lambda i, me_ref: (me_ref[0], 0, 0))],
                 out_shape=[SDS((N_DEV, 8, width), F32)], sem=("arbitrary",))[0]


def _pack_ffn(me, dfwg, dfwv, dfbg, dfbv, sq):
    def fill(o, dfwg_r, dfwv_r, dfbg_r, dfbv_r, sq_r):
        o[0, :, 0:DFF] = dfwg_r[...]
        o[0, :, DFF:2 * DFF] = dfwv_r[...]
        o[0, 3:4, 0:DFF] = dfbg_r[...]
        o[0, 3:4, DFF:2 * DFF] = dfbv_r[...]
        o[0, :, C_SQ:C_SQ + 128] = sq_r[...]

    return _pack("pack_ffn", me, (dfwg, dfwv, dfbg, dfbv, sq), P_W, fill)


def _pack_mix(me, dg2, dgco, dgao, dcw8, dqg, dkg, dsink):
    def fill(o, dg2_r, dgco_r, dgao_r, dcw_r, dqg_r, dkg_r, dsink_r):
        o[0, R_G2:R_G2 + 1, :] = dg2_r[...]
        o[0, R_GO:R_GO + 1, C_GCO:C_GCO + CW] = dgco_r[...]
        o[0, R_GO:R_GO + 1, C_GAO:C_GAO + AW] = dgao_r[...]
        o[0, R_DCW:R_DCW + 3, 0:CW] = dcw_r[0:3, :]
        o[0, R_QK:R_QK + 1, C_DQG:C_DQG + HD] = dqg_r[...]
        o[0, R_QK:R_QK + 1, C_DKG:C_DKG + HD] = dkg_r[...]
        o[0, R_QK:R_QK + 1, C_SINK:C_SINK + 128] = dsink_r[...]

    return _pack("pack_mix", me, (dg2, dgco, dgao, dcw8, dqg, dkg, dsink), D, fill)


N_SMALL = 11


def _small_adam(chip, p_all, pm_all, g1_all, tbl_all, ws, ms, vs):
    fw_cols = 2 * DFF // N_CHIPS
    cw_cols = CW // N_CHIPS

    def body(chip_ref, p_ref, fw_ref, pm_ref, cw_ref, g1_ref, tbl_ref, *refs):
        w_r, m_r, v_r = refs[0:N_SMALL], refs[N_SMALL:2 * N_SMALL], refs[2 * N_SMALL:3 * N_SMALL]
        outs = refs[3 * N_SMALL:]
        g_o, d_o, nm_o, nv_o = (outs[k * N_SMALL:(k + 1) * N_SMALL] for k in range(4))
        loss_o = outs[4 * N_SMALL]

        def total(ref):
            s = ref[0]
            for k in range(1, N_DEV):
                s = s + ref[k]
            return s

        S = total(p_ref)
        fw = total(fw_ref)
        M = total(pm_ref)
        cw = total(cw_ref)

        def step(i, g, at):
            d, nm, nv = _adam_math(w_r[i][at], g, m_r[i][at], v_r[i][at])
            g_o[i][at], d_o[i][at], nm_o[i][at], nv_o[i][at] = g, d, nm, nv

        everything = (slice(None), slice(None))
        step(0, total(g1_ref), everything)
        for r in range(3):
            step(1, cw[R_DCW + r:R_DCW + r + 1, :], (r, slice(None), slice(None)))
        step(2, M[R_QK:R_QK + 1, C_DQG:C_DQG + HD], everything)
        step(3, M[R_QK:R_QK + 1, C_DKG:C_DKG + HD], everything)
        step(4, total(tbl_ref), everything)
        step(5, M[R_QK:R_QK + 1, C_SINK:C_SINK + NH], everything)
        step(6, M[R_GO:R_GO + 1, C_GCO:C_GCO + CW], everything)
        step(7, M[R_GO:R_GO + 1, C_GAO:C_GAO + AW], everything)
        step(8, M[R_G2:R_G2 + 1, :], everything)
        for r in range(3):
            step(9, fw[r:r + 1, :], (r, slice(None), slice(None)))
        step(10, S[3:4, 0:2 * DFF], everything)
        sq = S[:, C_SQ:C_SQ + 128]
        loss_o[...] = jnp.sum(jnp.sum(sq, axis=1, keepdims=True), axis=0, keepdims=True) * (0.5 / D)

    def full(a):
        n = len(a.shape)
        return pl.BlockSpec(a.shape, lambda i, chip_ref: (0,) * n)

    params = [*ws, *ms, *vs]
    out = _call(
        body, (p_all, p_all, pm_all, pm_all, g1_all, tbl_all, *params), name="small_adam", grid=(1,), prefetch=(chip,),
        in_specs=[full(p_all),
                  pl.BlockSpec((N_DEV, 8, fw_cols), lambda i, chip_ref: (0, 0, chip_ref[0])),
                  full(pm_all),
                  pl.BlockSpec((N_DEV, 8, cw_cols), lambda i, chip_ref: (0, 0, chip_ref[0])),
                  full(g1_all), full(tbl_all), *[full(a) for a in params]],
        out_specs=[full(a) for a in ws] * 4 + [pl.BlockSpec((1, 1), lambda i, chip_ref: (0, 0))],
        out_shape=[SDS(a.shape, F32) for a in ws] * 4 + [SDS((1, 1), F32)], sem=("arbitrary",), vmem_mib=32)
    return out[0:N_SMALL], out[N_SMALL:2 * N_SMALL], out[2 * N_SMALL:3 * N_SMALL], out[3 * N_SMALL:4 * N_SMALL], out[4 * N_SMALL]


PLACE_STEPS = 4


def _place_specs(shards):
    rows = [s.shape[0] // PLACE_STEPS for s in shards]
    return ([pl.BlockSpec((r, D), lambda i, chip_ref: (i, 0)) for r in rows],
            [pl.BlockSpec((r, D), lambda i, chip_ref: (chip_ref[0] * PLACE_STEPS + i, 0)) for r in rows],
            [SDS((N_CHIPS * s.shape[0], D), BF) for s in shards])


def _place_first(chip, shard, conv_w, ffn_conv_w):
    def body(chip_ref, a, s0, s1, o, t0, t1):
        o[...] = a[...].astype(BF)

        @pl.when(pl.program_id(0) == 0)
        def _():
            for s, t in ((s0, t0), (s1, t1)):
                t[...] = jnp.zeros_like(t)
                t[0, 0:3, :] = s[...]

    ins, outs, shapes = _place_specs([shard])
    taps = (conv_w, ffn_conv_w)
    return _call(
        body, (shard, conv_w, ffn_conv_w), name="place_first", grid=(PLACE_STEPS,), prefetch=(chip,),
        in_specs=ins + [pl.BlockSpec(s.shape, lambda i, chip_ref: (0, 0)) for s in taps],
        out_specs=outs + [pl.BlockSpec((1, 8, s.shape[1]), lambda i, chip_ref: (chip_ref[0], 0, 0)) for s in taps],
        out_shape=shapes + [SDS((N_CHIPS, 8, s.shape[1]), F32) for s in taps],
        sem=("arbitrary",), vmem_mib=32, free=(1, 2))


def _place_rest(chip, shards, w_up, table, bucket, comm):
    n = len(shards)
    c_up = w_up.shape[1]
    edges = [round(k * (c_up // 128) / PLACE_STEPS) * 128 for k in range(PLACE_STEPS + 1)]

    def body(chip_ref, *refs):
        a, (up_ref, tab_ref, bk_ref), o = refs[:n], refs[n:n + 3], refs[n + 3:2 * n + 3]
        up_o, bias_ref = refs[2 * n + 3:]
        for src, dst in zip(a, o):
            dst[...] = src[...].astype(BF)
        for k in range(PLACE_STEPS):
            @pl.when(pl.program_id(0) == k)
            def _(k=k):
                up_o[edges[k]:edges[k + 1], :] = up_ref[:, edges[k]:edges[k + 1]].T.astype(BF)

        @pl.when(pl.program_id(0) == 0)
        def _():
            bk = bk_ref[...]
            eq = [bk == b for b in range(NBUCKET)]
            for h in range(NH):
                acc = jnp.zeros((BLK, 2 * BLK), F32)
                for b in range(NBUCKET):
                    acc = jnp.where(eq[b], tab_ref[h, b], acc)
                bias_ref[h * BLK:(h + 1) * BLK, :] = acc

    ins, outs, shapes = _place_specs(shards)
    return _call(
        body, (*shards, w_up, table, bucket), name="place_rest", grid=(PLACE_STEPS,), prefetch=(chip,),
        in_specs=ins + [_resident(w_up.shape), pl.BlockSpec(memory_space=pltpu.SMEM),
                        pl.BlockSpec(bucket.shape, lambda i, chip_ref: (0, 0))],
        out_specs=outs + [pl.BlockSpec((c_up, D), lambda i, chip_ref: (chip_ref[0], 0)),
                          pl.BlockSpec((NH * BLK, 2 * BLK), lambda i, chip_ref: (0, 0))],
        out_shape=shapes + [SDS((N_CHIPS * c_up, D), BF), SDS((NH * BLK, 2 * BLK), F32)],
        sem=("arbitrary",), vmem_mib=32, comm=comm, free=(n + 1, n + 2))


def kernel(x, norm_mix_g, w_in, conv_w, q_norm_g, k_norm_g, rel_bias_table, sinks, out_norm_conv_g, out_norm_attn_g, w_out, norm_ffn_g, w_up, ffn_conv_w, ffn_conv_b, w_down, loss_target, m_norm_mix_g, m_w_in, m_conv_w, m_q_norm_g, m_k_norm_g, m_rel_bias_table, m_sinks, m_out_norm_conv_g, m_out_norm_attn_g, m_w_out, m_norm_ffn_g, m_w_up, m_ffn_conv_w, m_ffn_conv_b, m_w_down, v_norm_mix_g, v_w_in, v_conv_w, v_q_norm_g, v_k_norm_g, v_rel_bias_table, v_sinks, v_out_norm_conv_g, v_out_norm_attn_g, v_w_out, v_norm_ffn_g, v_w_up, v_ffn_conv_w, v_ffn_conv_b, v_w_down):
    as_arg = lambda i: jnp.reshape(i, (1,)).astype(jnp.int32)
    chip = as_arg(2 * lax.axis_index("x") + lax.axis_index("y"))
    core = as_arg(lax.axis_index("c"))
    me = 2 * chip + core
    xs, tgt = x[0], loss_target[0]
    qg, kg, gco, gao, g1, g2, fb = q_norm_g, k_norm_g, out_norm_conv_g, out_norm_attn_g, norm_mix_g, norm_ffn_g, ffn_conv_b
    pieces = lambda g: g.reshape(N_DEV, g.shape[0] // N_DEV, D)
    whole = lambda f: f.reshape(2 * f.shape[1], D)

    bucket = jnp.asarray(_bucket_table())
    p_in, p_cw, p_fw = _place_first(chip, w_in[0].T, conv_w[0], ffn_conv_w[0])
    p_out, p_down, p_up, bias, w_int, cw_all, fw_all = _place_rest(
        chip, [w_out[0], w_down[0]], w_up[0], rel_bias_table.T, bucket,
        comm=[_t_gather(p_in, relayed_first=True), _t_small_weights(p_cw), _t_small_weights(p_fw)])
    cw8 = jnp.transpose(cw_all, (1, 0, 2)).reshape(8, CW)
    fw8 = jnp.transpose(fw_all, (1, 0, 2)).reshape(8, 2 * DFF)

    early = 3 / 11
    proj, u1, w_out_f, p_up = _inproj(xs, g1, w_int, comm=[_t_gather(p_out), _t_gather(p_up, (0, early))])
    y, w_upt = _mix_fwd(proj, sinks, cw8, qg, kg, gco, gao, bias, comm=[_t_gather(p_up, (early, 1))])
    h1, u2 = _outproj(y, w_out_f, xs, g2)
    up, w_down_f = _ffn_up(u2, w_upt, comm=[_t_gather(p_down)])
    a, pre_g, pre_v = _ffn_act(up, fw8, fb)
    dh2, dh2b, sq = _ffn_down(a, w_down_f, h1, tgt)

    gdbf, = _wgrad("wgrad_down", [a], dh2b, None)
    da, sib_down = _ffn_down_bwd(dh2b, w_down_f, comm=[_t_sibling(pieces(gdbf))])
    pbf_down, own_down = _chip_sum("chip_sum_w_down", pieces(gdbf), sib_down, core, chip)
    dug, duv, dfwg, dfwv, dfbg, dfbv, chips_down = _ffn_act_bwd(up, pre_g, pre_v, da, fw8, comm=[_t_chips(pbf_down)])
    fin_down, = _final_sum("final_sum_w_down", own_down, chips_down, core)
    gubf, = _wgrad("wgrad_up", [dug, duv], u2, False)
    p_all = _pack_ffn(me, dfwg, dfwv, dfbg, dfbv, sq)
    dh1, dh1b, dg2, sib_up, fin_down, p_all = _norm_matmul_bwd(
        "ffn_up_bwd", [dug, duv], w_upt, [0, DFF], h1, g2, dh2, True,
        comm=[_t_sibling(pieces(gubf)), _t_swap(fin_down), _t_allgather(p_all)])
    pbf_up, own_up = _chip_sum("chip_sum_w_up", pieces(gubf), sib_up, core, chip)
    gobf, = _wgrad("wgrad_out", [y], dh1b, True)
    dy, sib_out = _out_bwd(dh1b, w_out_f, comm=[_t_sibling(pieces(gobf))])
    pbf_out, own_out = _chip_sum("chip_sum_w_out", pieces(gobf), sib_out, core, chip)
    dproj, dcw8, dqg, dkg, dgco, dgao, dsink, dbias, chips_up = _mix_bwd(
        proj, dy, sinks, cw8, qg, kg, gco, gao, bias, comm=[_t_chips(pbf_up)])
    fin_up, = _final_sum("final_sum_w_up", own_up, chips_up, core)
    pm_all = _pack_mix(me, dg2, dgco, dgao, dcw8, dqg, dkg, dsink)
    gibf, chips_out, fin_up, pm_all = _wgrad(
        "wgrad_in", [dproj], u1, False, comm=[_t_chips(pbf_out), _t_swap(fin_up), _t_allgather(pm_all)])
    fin_out, sib_in = _final_sum("final_sum_w_out", own_out, chips_out, core, comm=[_t_sibling(pieces(gibf))])
    pbf_in, own_in = _chip_sum("chip_sum_w_in", pieces(gibf), sib_in, core, chip)
    dx, g1_all, tbl_all, chips_in, fin_out = _norm_matmul_bwd(
        "in_bwd", [dproj], w_int, [0], xs, g1, dh1, False, comm=[_t_chips(pbf_in), _t_swap(fin_out)], slot=me,
        band=(dbias, bucket))
    fin_in, = _final_sum("final_sum_w_in", own_in, chips_in, core)
    g1_all, tbl_all, fin_in = _comm_call("gather_last", [_t_allgather(g1_all), _t_allgather(tbl_all), _t_swap(fin_in)])

    g_w_out, g_w_down = whole(fin_out), whole(fin_down)
    g_w_down, d_down, nm_down, nv_down = _adamw("adamw_w_down", w_down[0], g_w_down, m_w_down[0], v_w_down[0], 352, True)
    g_w_up, d_up, nm_up, nv_up = _adamw(
        "adamw_w_up", w_up[0], whole(fin_up), m_w_up[0], v_w_up[0], 256, True, stage=False, g_transposed=True)
    g_w_out, d_out, nm_out, nv_out = _adamw("adamw_w_out", w_out[0], g_w_out, m_w_out[0], v_w_out[0], 256, True, stage=False)
    g_w_in, d_in, nm_in, nv_in = [a.T for a in _adamw(
        "adamw_w_in", w_in[0].T, whole(fin_in), m_w_in[0].T, v_w_in[0].T, INW // N_CHIPS // 3, True, stage=False)]
    taps = lambda a: jnp.transpose(a, (1, 0, 2))
    sw = [norm_mix_g, taps(conv_w), q_norm_g, k_norm_g, rel_bias_table.T, sinks, out_norm_conv_g, out_norm_attn_g,
          norm_ffn_g, taps(ffn_conv_w), ffn_conv_b]
    smm = [m_norm_mix_g, taps(m_conv_w), m_q_norm_g, m_k_norm_g, m_rel_bias_table.T, m_sinks, m_out_norm_conv_g,
           m_out_norm_attn_g, m_norm_ffn_g, taps(m_ffn_conv_w), m_ffn_conv_b]
    smv = [v_norm_mix_g, taps(v_conv_w), v_q_norm_g, v_k_norm_g, v_rel_bias_table.T, v_sinks, v_out_norm_conv_g,
           v_out_norm_attn_g, v_norm_ffn_g, taps(v_ffn_conv_w), v_ffn_conv_b]
    *small_out, loss = _small_adam(chip, p_all, pm_all, g1_all, tbl_all, sw, smm, smv)
    sg, sd, snm, snv = [list(r) for r in small_out]
    for r in (sg, sd, snm, snv):
        r[1], r[4], r[9] = taps(r[1]), r[4].T, taps(r[9])

    def order(s, b_in, b_out, b_up, b_down):
        return (s[0], b_in[None], s[1], s[2], s[3], s[4], s[5], s[6], s[7], b_out[None], s[8], b_up[None],
                s[9], s[10], b_down[None])

    return (loss.reshape(()), dx[None],
            *order(sg, g_w_in, g_w_out, g_w_up, g_w_down),
            *order(sd, d_in, d_out, d_up, d_down),
            *order(snm, nm_in, nm_out, nm_up, nm_down),
            *order(snv, nv_in, nv_out, nv_up, nv_down))
```

```python
import functools
import math

import numpy as np

import jax
import jax.numpy as jnp
from jax import lax
from jax.experimental import pallas as pl
from jax.experimental.pallas import tpu as pltpu

F32 = jnp.float32
BF = jnp.bfloat16
SDS = jax.ShapeDtypeStruct

T = 2048
D = 1024
CW = 512
AW = 512
HD = 64
NH = 8
NKV = 2
GQ = 4
INW = 2304
DFF = 2816
BLK = 128
NB = T // BLK
NBUCKET = 32
EPS = 1e-6
NEG_INF = -1e30
N_CHIPS = 4
N_DEV = 8

ADAM_LR = 0.001
ADAM_B1 = 0.9
ADAM_B2 = 0.999
ADAM_EPS = 1e-08
ADAM_WD = 0.01
ADAM_STEP = 10

TM = 512
MIB = 1024 * 1024
MESH = pl.DeviceIdType.MESH
ANY = pl.BlockSpec(memory_space=pl.ANY)

_pcall = pl.pallas_call


def _params(sem=None, vmem_mib=None, collective_id=None):
    kw = {} if collective_id is None else {"collective_id": collective_id}
    if sem is not None:
        kw["dimension_semantics"] = sem
    if vmem_mib is not None:
        kw["vmem_limit_bytes"] = vmem_mib * MIB
    return pltpu.CompilerParams(**kw)


def _resident(shape):
    return pl.BlockSpec(shape, lambda *_: (0,) * len(shape), pipeline_mode=pl.Buffered(1))


def _dot(a, b, ca, cb):
    return lax.dot_general(a, b, (((ca,), (cb,)), ((), ())), preferred_element_type=F32)


def _rms_bwd(dy, x, r, g):
    dg = jnp.sum(dy * (x * r), axis=0, keepdims=True)
    dgx = dy * g
    dx = r * dgx - x * (r * r * r) * jnp.mean(x * dgx, axis=-1, keepdims=True)
    return dx, dg


def _where():
    x, y, c = lax.axis_index("x"), lax.axis_index("y"), lax.axis_index("c")
    return x, y, c, [(1 - x, y), (x, 1 - y), (1 - x, 1 - y)]


def _rcopy(src, dst, ssem, rsem, dev):
    return pltpu.make_async_remote_copy(src_ref=src, dst_ref=dst, send_sem=ssem, recv_sem=rsem, device_id=dev,
                                        device_id_type=MESH)


SIBLING, Y_CHIP, X_CHIP, DIAGONAL_CHIP = 1, 2, 4, 6
OTHER_CHIPS = (Y_CHIP, X_CHIP, DIAGONAL_CHIP)
EVERYONE = tuple(range(1, N_DEV))
BARRIER_OF = {(SIBLING,): 0, (SIBLING, Y_CHIP, X_CHIP): 1, OTHER_CHIPS: 2, (SIBLING,) + OTHER_CHIPS: 3, EVERYONE: 4}


def _peer(rel):
    x, y, c, _ = _where()
    return x ^ ((rel >> 2) & 1), y ^ ((rel >> 1) & 1), c ^ (rel & 1)


class _Task:
    def __init__(self, ins, outs, alias, n_sem, start, finish, middle=None, peers=()):
        self.ins, self.outs, self.alias, self.n_sem, self.start, self.finish = ins, outs, alias, n_sem, start, finish
        self.middle = middle if middle is not None else (lambda *args: None)
        self.peers = peers


def _peers_of(comm):
    return tuple(sorted({p for t in comm for p in t.peers}))


def _enter(comm):
    peers = _peers_of(comm)
    barrier = pltpu.get_barrier_semaphore()
    for rel in peers:
        pl.semaphore_signal(barrier, inc=1, device_id=_peer(rel), device_id_type=MESH)
    pl.semaphore_wait(barrier, len(peers))


ROWS16 = 16


def _t_gather(placed, part=(0, 1), relayed_first=False):
    R = placed.shape[0] // N_CHIPS
    q = R // 4
    lo, hi = (round(f * (q // ROWS16)) * ROWS16 for f in part)

    def quarter(chip_index, core, k):
        return pl.ds(pl.multiple_of(chip_index * R + core * 2 * q + k * q + lo, ROWS16), hi - lo)

    def places():
        x, y, c, _ = _where()
        return c, 2 * x + y, 2 * (1 - x) + y, 2 * x + (1 - y), 2 * (1 - x) + (1 - y), (1 - x, y, c), (x, 1 - y, c), (x, y, 1 - c)

    def copy(buf, k, chip_index, core, quart, ss, rs, b, dev):
        window = buf.at[quarter(chip_index, core, quart)]
        return _rcopy(window, window, ss.at[b + k], rs.at[b + k], dev)

    def first_hop(cout, ss, rs, b, which):
        c, me, _, _, _, x_nbr, y_nbr, _ = places()
        for k, (quart, dev) in enumerate(((0, x_nbr), (1, y_nbr), (1, x_nbr), (0, y_nbr))):
            if k in which:
                copy(cout[0], k, me, c, quart, ss, rs, b, dev).start()

    def start(cin, cout, ss, rs, b):
        first_hop(cout, ss, rs, b, (0, 1) if relayed_first else (0, 1, 2, 3))

    def middle(cin, cout, ss, rs, b):
        c, _, xc, yc, _, x_nbr, y_nbr, sib = places()
        for k, chip_index, quart, dev in ((0, xc, 0, y_nbr), (1, yc, 1, x_nbr)):
            copy(cout[0], k, chip_index, c, quart, ss, rs, b, dev).wait_recv()
            copy(cout[0], 4 + k, chip_index, c, quart, ss, rs, b, dev).start()
            copy(cout[0], 6 + k, chip_index, c, quart, ss, rs, b, sib).start()
        if relayed_first:
            first_hop(cout, ss, rs, b, (2, 3))

    later = ((2, 1, 1), (3, 2, 0), (4, 3, 0), (5, 3, 1))

    def finish(cin, cout, ss, rs, b):
        c, me, xc, yc, dc, _, _, sib = places()
        chip_of = {1: xc, 2: yc, 3: dc}
        for k, whose, quart in later:
            copy(cout[0], k, chip_of[whose], c, quart, ss, rs, b, sib).wait_recv()
            copy(cout[0], 6 + k, chip_of[whose], c, quart, ss, rs, b, sib).start()
        for k, whose, quart in ((0, 1, 0), (1, 2, 1)) + later:
            copy(cout[0], 6 + k, chip_of[whose], 1 - c, quart, ss, rs, b, sib).wait_recv()
        for k in range(12):
            copy(cout[0], k, me, c, 0, ss, rs, b, sib).wait_send()

    return _Task([placed], [SDS(placed.shape, placed.dtype)], [(0, 0)], 12, start, finish, middle, peers=(SIBLING, Y_CHIP, X_CHIP))


def _t_small_weights(buf):
    def start(cin, cout, ss, rs, b):
        x, y, c, chips = _where()
        mine = cout[0].at[2 * x + y]
        for r, (px, py) in enumerate(chips):
            _rcopy(mine, mine, ss.at[b + r], rs.at[b + r], (px, py, c)).start()

    def finish(cin, cout, ss, rs, b):
        x, y, c, chips = _where()
        for r, (px, py) in enumerate(chips):
            got = cout[0].at[2 * px + py]
            _rcopy(got, got, ss.at[b + r], rs.at[b + r], (px, py, c)).wait_recv()
        for r, (px, py) in enumerate(chips):
            mine = cout[0].at[2 * x + y]
            _rcopy(mine, mine, ss.at[b + r], rs.at[b + r], (px, py, c)).wait_send()

    return _Task([buf], [SDS(buf.shape, buf.dtype)], [(0, 0)], 3, start, finish, peers=OTHER_CHIPS)


def _t_sibling(gbf):
    def start(cin, cout, ss, rs, b):
        x, y, c, _ = _where()
        for jj in range(N_CHIPS):
            _rcopy(cin[0].at[2 * jj + (1 - c)], cout[0].at[jj], ss.at[b + jj], rs.at[b + jj], (x, y, 1 - c)).start()

    def finish(cin, cout, ss, rs, b):
        x, y, c, _ = _where()
        for jj in range(N_CHIPS):
            got = cout[0].at[jj]
            _rcopy(got, got, ss.at[b + jj], rs.at[b + jj], (x, y, 1 - c)).wait_recv()
        for jj in range(N_CHIPS):
            got = cout[0].at[jj]
            _rcopy(got, got, ss.at[b + jj], rs.at[b + jj], (x, y, 1 - c)).wait_send()

    return _Task([gbf], [SDS((N_CHIPS,) + gbf.shape[1:], BF)], [], N_CHIPS, start, finish, peers=(SIBLING,))


def _t_chips(pbf):
    def start(cin, cout, ss, rs, b):
        x, y, c, chips = _where()
        for r, (px, py) in enumerate(chips):
            _rcopy(cin[0].at[2 * px + py], cout[0].at[r], ss.at[b + r], rs.at[b + r], (px, py, c)).start()

    def finish(cin, cout, ss, rs, b):
        x, y, c, chips = _where()
        for r, (px, py) in enumerate(chips):
            got = cout[0].at[r]
            _rcopy(got, got, ss.at[b + r], rs.at[b + r], (px, py, c)).wait_recv()
        for r, (px, py) in enumerate(chips):
            got = cout[0].at[r]
            _rcopy(got, got, ss.at[b + r], rs.at[b + r], (px, py, c)).wait_send()

    return _Task([pbf], [SDS((3,) + pbf.shape[1:], BF)], [], 3, start, finish, peers=OTHER_CHIPS)


def _t_swap(fin):
    def start(cin, cout, ss, rs, b):
        x, y, c, _ = _where()
        mine = cout[0].at[c]
        _rcopy(mine, mine, ss.at[b], rs.at[b], (x, y, 1 - c)).start()

    def finish(cin, cout, ss, rs, b):
        x, y, c, _ = _where()
        got = cout[0].at[1 - c]
        _rcopy(got, got, ss.at[b], rs.at[b], (x, y, 1 - c)).wait_recv()
        _rcopy(got, got, ss.at[b], rs.at[b], (x, y, 1 - c)).wait_send()

    return _Task([fin], [SDS(fin.shape, fin.dtype)], [(0, 0)], 1, start, finish, peers=(SIBLING,))


def _t_allgather(buf):
    def peers():
        x, y, c, _ = _where()
        out = []
        for rel in range(1, N_DEV):
            px, py, pc = x ^ ((rel >> 2) & 1), y ^ ((rel >> 1) & 1), c ^ (rel & 1)
            out.append((rel - 1, 4 * px + 2 * py + pc, (px, py, pc)))
        return 4 * x + 2 * y + c, out

    def start(cin, cout, ss, rs, b):
        me, ps = peers()
        mine = cout[0].at[me]
        for k, _, dev in ps:
            _rcopy(mine, mine, ss.at[b + k], rs.at[b + k], dev).start()

    def finish(cin, cout, ss, rs, b):
        me, ps = peers()
        for k, pidx, dev in ps:
            got = cout[0].at[pidx]
            _rcopy(got, got, ss.at[b + k], rs.at[b + k], dev).wait_recv()
        for k, _, dev in ps:
            mine = cout[0].at[me]
            _rcopy(mine, mine, ss.at[b + k], rs.at[b + k], dev).wait_send()

    return _Task([buf], [SDS(buf.shape, buf.dtype)], [(0, 0)], N_DEV - 1, start, finish, peers=EVERYONE)


def _run_tasks(comm, which, cin, cout, ss, rs):
    i0 = o0 = s0 = 0
    for t in comm:
        getattr(t, which)(cin[i0:i0 + len(t.ins)], cout[o0:o0 + len(t.outs)], ss, rs, s0)
        i0, o0, s0 = i0 + len(t.ins), o0 + len(t.outs), s0 + t.n_sem


def _from_hbm(*arrays):
    return [pltpu.with_memory_space_constraint(a, pltpu.HBM) for a in arrays]


def _in_hbm(shapes):
    return [pltpu.HBM(s.shape, s.dtype) for s in shapes]


def _comm_layout(comm, n_in, n_out):
    c_in = [a for t in comm for a in t.ins]
    c_out = [s for t in comm for s in t.outs]
    aliases, i0, o0 = {}, 0, 0
    for t in comm:
        for i, o in t.alias:
            aliases[n_in + i0 + i] = n_out + o0 + o
        i0, o0 = i0 + len(t.ins), o0 + len(t.outs)
    return c_in, c_out, aliases, sum(t.n_sem for t in comm)


def _call(body, operands, *, name, grid, in_specs, out_specs, out_shape, scratch_shapes=(), sem=None, vmem_mib=None, comm=(),
          free=(), prefetch=()):
    operands = [o if s.memory_space == pltpu.SMEM or k in free else pltpu.with_memory_space_constraint(o, pltpu.HBM)
                for k, (o, s) in enumerate(zip(operands, in_specs))]
    n_pre, n_in, n_out, n_scr = len(prefetch), len(in_specs), len(out_specs), len(scratch_shapes)
    c_in, c_out, aliases, n_sem = _comm_layout(comm, n_pre + n_in, n_out)
    sems = [pltpu.SemaphoreType.DMA((n_sem,)), pltpu.SemaphoreType.DMA((n_sem,))] if comm else []

    def wrapped(*refs):
        pre, refs = refs[:n_pre], refs[n_pre:]
        ins, cin = refs[:n_in], refs[n_in:n_in + len(c_in)]
        rest = refs[n_in + len(c_in):]
        outs, cout = rest[:n_out], rest[n_out:n_out + len(c_out)]
        rest = rest[n_out + len(c_out):]
        scr, csem = rest[:n_scr], rest[n_scr:]
        if not comm:
            return body(*pre, *ins, *outs, *scr)
        step = functools.reduce(lambda acc, k: acc * grid[k] + pl.program_id(k), range(len(grid)), 0)
        n_steps = math.prod(grid)

        @pl.when(step == 0)
        def _():
            _enter(comm)
            _run_tasks(comm, "start", cin, cout, *csem)

        pl.when(step == n_steps // 2)(lambda: _run_tasks(comm, "middle", cin, cout, *csem))
        body(*pre, *ins, *outs, *scr)
        pl.when(step == n_steps - 1)(lambda: _run_tasks(comm, "finish", cin, cout, *csem))

    grid_spec = pltpu.PrefetchScalarGridSpec(
        num_scalar_prefetch=n_pre, grid=grid, in_specs=list(in_specs) + [ANY] * len(c_in),
        out_specs=list(out_specs) + [ANY] * len(c_out), scratch_shapes=list(scratch_shapes) + sems)
    return _pcall(
        wrapped, name=name, grid_spec=grid_spec, out_shape=_in_hbm(list(out_shape) + c_out), input_output_aliases=aliases,
        compiler_params=_params(("arbitrary",) * len(grid) if comm else sem, vmem_mib,
                                BARRIER_OF[_peers_of(comm)] if comm else None),
    )(*prefetch, *operands, *_from_hbm(*c_in))


def _comm_call(name, comm):
    c_in, c_out, aliases, n_sem = _comm_layout(comm, 0, 0)

    def body(*refs):
        cin, cout, (ss, rs) = refs[:len(c_in)], refs[len(c_in):len(c_in) + len(c_out)], refs[len(c_in) + len(c_out):]
        _enter(comm)
        for phase in ("start", "middle", "finish"):
            _run_tasks(comm, phase, cin, cout, ss, rs)

    return _pcall(
        body, name=name, in_specs=[ANY] * len(c_in), out_specs=[ANY] * len(c_out), out_shape=_in_hbm(c_out),
        scratch_shapes=[pltpu.SemaphoreType.DMA((n_sem,)), pltpu.SemaphoreType.DMA((n_sem,))],
        input_output_aliases=aliases, compiler_params=_params(collective_id=BARRIER_OF[_peers_of(comm)]),
    )(*_from_hbm(*c_in))


def _inproj(x, g1, w_int, comm=()):
    tm = TM

    def body(x_ref, g_ref, w_ref, proj_ref, u_ref):
        xf = x_ref[...]
        r = lax.rsqrt(jnp.mean(xf * xf, axis=-1, keepdims=True) + EPS)
        u = (xf * r * g_ref[...]).astype(BF)
        u_ref[...] = u
        proj_ref[...] = _dot(u, w_ref[...], 1, 1)

    return _call(
        body, (x, g1, w_int), name="inproj", grid=(T // tm,),
        in_specs=[pl.BlockSpec((tm, D), lambda i: (i, 0)), pl.BlockSpec((1, D), lambda i: (0, 0)),
                  _resident((INW, D))],
        out_specs=[pl.BlockSpec((tm, INW), lambda i: (i, 0)), pl.BlockSpec((tm, D), lambda i: (i, 0))],
        out_shape=[SDS((T, INW), F32), SDS((T, D), BF)], sem=("parallel",), vmem_mib=40, comm=comm, free=(0, 1))


def _outproj(y, w_out, x, g2):
    tm = TM

    def body(y_ref, w_ref, x_ref, g_ref, h1_ref, u2_ref):
        h1 = x_ref[...] + _dot(y_ref[...], w_ref[...], 1, 0)
        h1_ref[...] = h1
        r = lax.rsqrt(jnp.mean(h1 * h1, axis=-1, keepdims=True) + EPS)
        u2_ref[...] = (h1 * r * g_ref[...]).astype(BF)

    return _call(
        body, (y, w_out, x, g2), name="outproj", grid=(T // tm,),
        in_specs=[pl.BlockSpec((tm, D), lambda i: (i, 0)), _resident((D, D)),
                  pl.BlockSpec((tm, D), lambda i: (i, 0)), pl.BlockSpec((1, D), lambda i: (0, 0))],
        out_specs=[pl.BlockSpec((tm, D), lambda i: (i, 0)), pl.BlockSpec((tm, D), lambda i: (i, 0))],
        out_shape=[SDS((T, D), F32), SDS((T, D), BF)], sem=("parallel",), vmem_mib=32, free=(2, 3))


def _ffn_up(u2, w_upt, comm=()):
    tm, tn = T, 512

    def body(u_ref, w_ref, o_ref):
        o_ref[...] = _dot(u_ref[...], w_ref[...], 1, 1).astype(BF)

    return _call(
        body, (u2, w_upt), name="ffn_up", grid=(T // tm, 2 * DFF // tn),
        in_specs=[pl.BlockSpec((tm, D), lambda i, j: (i, 0)), pl.BlockSpec((tn, D), lambda i, j: (j, 0))],
        out_specs=[pl.BlockSpec((tm, tn), lambda i, j: (i, j))], out_shape=[SDS((T, 2 * DFF), BF)],
        sem=("parallel", "parallel"), vmem_mib=32, comm=comm, free=(1,))


def _ffn_down(a, w_down, h1, tgt):
    tm = TM

    def body(a_ref, w_ref, h1_ref, t_ref, dh_ref, dhb_ref, l_ref):
        @pl.when(pl.program_id(0) == 0)
        def _():
            l_ref[...] = jnp.zeros_like(l_ref)

        h2 = h1_ref[...] + _dot(a_ref[...], w_ref[...], 1, 0)
        e = h2 - t_ref[...]
        dh = e * (1.0 / D)
        dh_ref[...] = dh
        dhb_ref[...] = dh.astype(BF)
        e2 = jnp.sum((e * e).reshape(tm // 8, 8, D), axis=0)
        acc = e2[:, 0:128]
        for k in range(1, D // 128):
            acc = acc + e2[:, k * 128:(k + 1) * 128]
        l_ref[...] += acc

    return _call(
        body, (a, w_down, h1, tgt), name="ffn_down", grid=(T // tm,),
        in_specs=[pl.BlockSpec((tm, DFF), lambda i: (i, 0)), _resident((DFF, D)),
                  pl.BlockSpec((tm, D), lambda i: (i, 0)), pl.BlockSpec((tm, D), lambda i: (i, 0))],
        out_specs=[pl.BlockSpec((tm, D), lambda i: (i, 0)), pl.BlockSpec((tm, D), lambda i: (i, 0)),
                   pl.BlockSpec((8, 128), lambda i: (0, 0))],
        out_shape=[SDS((T, D), F32), SDS((T, D), BF), SDS((8, 128), F32)], sem=("arbitrary",), vmem_mib=40, free=(2, 3))


def _bucket_table():
    q = np.arange(BLK, dtype=np.int32)[:, None]
    j = np.arange(2 * BLK, dtype=np.int32)[None, :]
    n = np.maximum(q + BLK - j, 0)
    nf = np.maximum(n, 1).astype(np.float32)
    max_exact = NBUCKET // 2
    large = max_exact + (np.log(nf / np.float32(max_exact)) / np.float32(math.log(BLK / max_exact))
                         * np.float32(NBUCKET - max_exact)).astype(np.int32)
    large = np.minimum(large, NBUCKET - 1)
    return np.where(n < max_exact, n, large).astype(np.int32)


def _two_bf16(x):
    hi = x.astype(BF)
    return hi, (x - hi.astype(F32)).astype(BF)


def _head_sums(x, seg):
    hi, lo = _two_bf16(x)
    s = seg[0:x.shape[1], :]
    return _dot(hi, s, 1, 0) + _dot(lo, s, 1, 0)


def _head_spread(v, seg, width):
    hi, lo = _two_bf16(v)
    s = seg[0:width, :]
    return _dot(hi, s, 1, 1) + _dot(lo, s, 1, 1)


def _head_norm(x, g_t, seg, by_head=False):
    if by_head:
        heads = [x[:, h * HD:(h + 1) * HD] for h in range(x.shape[1] // HD)]
        r = jnp.concatenate([jnp.broadcast_to(lax.rsqrt(jnp.mean(v * v, axis=-1, keepdims=True) + EPS), v.shape)
                             for v in heads], axis=1)
    else:
        r = lax.rsqrt(_head_sums(x * x, seg) * (1.0 / HD) + EPS)
        r = _head_spread(r, seg, x.shape[1])
    return x * r * g_t, r


def _head_norm_bwd(dy, x, r, g_t, seg):
    dg_t = jnp.sum(dy * (x * r), axis=0, keepdims=True)
    dgx = dy * g_t
    mean = _head_spread(_head_sums(x * dgx, seg) * (1.0 / HD), seg, x.shape[1])
    return r * dgx - x * (r * r * r) * mean, dg_t


def _fold_heads(v):
    out = v[:, 0:HD]
    for h in range(1, v.shape[1] // HD):
        out = out + v[:, h * HD:(h + 1) * HD]
    return out


def _mix_forward(P, zc8, zh8, pkv, first, cw, qg_t, kg_t, gco, gao, seg, sink_ref, bias_ref, by_head=False):
    gate_b = P[:, 0:CW]
    gate_c = P[:, CW:2 * CW]
    hc = P[:, 2 * CW:3 * CW]
    z = gate_c * hc
    keep = jnp.where(first, 0.0, 1.0)
    zp = zc8 * zh8 * keep
    p1 = zp[7:8, :]
    p2 = zp[6:7, :]
    row = lax.broadcasted_iota(jnp.int32, (BLK, 1), 0)
    z1 = jnp.where(row == 0, p1, pltpu.roll(z, 1, 0))
    z2 = jnp.where(row == 0, p2, jnp.where(row == 1, p1, pltpu.roll(z, 2, 0)))
    cz = cw[0:1, :] * z2 + cw[1:2, :] * z1 + cw[2:3, :] * z
    y_conv = gate_b * cz

    scale = HD ** -0.5
    qi = lax.broadcasted_iota(jnp.int32, (BLK, 2 * BLK), 0)
    kj = lax.broadcasted_iota(jnp.int32, (BLK, 2 * BLK), 1)
    dd = qi + BLK - kj
    first_key = jnp.where(first, BLK, 0)
    valid = (dd >= 0) & (dd < BLK) & (kj >= first_key)

    q0 = 3 * CW
    k0 = q0 + AW
    v0 = k0 + NKV * HD
    q_raw = P[:, q0:k0]
    qn, rq = _head_norm(q_raw, qg_t, seg, by_head)
    qs = (qn * scale).astype(BF)
    k_raw = jnp.concatenate([pkv[:, 0:NKV * HD], P[:, k0:v0]], axis=0)
    kn, rk = _head_norm(k_raw, kg_t, seg, by_head)
    knb = kn.astype(BF)
    heads = []
    for h in range(NH):
        kv = h // GQ
        kb = knb[:, kv * HD:(kv + 1) * HD]
        vb = jnp.concatenate([pkv[:, NKV * HD + kv * HD:NKV * HD + (kv + 1) * HD],
                              P[:, v0 + kv * HD:v0 + (kv + 1) * HD]], axis=0).astype(BF)
        Q = qs[:, h * HD:(h + 1) * HD]
        S = _dot(Q, kb, 1, 1) + bias_ref[h * BLK:(h + 1) * BLK, :]
        S = jnp.where(valid, S, NEG_INF)
        sink = sink_ref[0, h]
        m = jnp.maximum(jnp.max(S, axis=-1, keepdims=True), sink)
        p = jnp.exp(S - m)
        es = jnp.exp(sink - m)
        denom = jnp.sum(p, axis=-1, keepdims=True) + es
        probs = p / denom
        O = _dot(probs.astype(BF), vb, 1, 0)
        heads.append(dict(kb=kb, vb=vb, Q=Q, probs=probs, psink=es / denom, O=O))
    y_attn = jnp.concatenate([hd["O"] for hd in heads], axis=1)

    rc = lax.rsqrt(jnp.mean(y_conv * y_conv, axis=-1, keepdims=True) + EPS)
    ra = lax.rsqrt(jnp.mean(y_attn * y_attn, axis=-1, keepdims=True) + EPS)
    y = jnp.concatenate([y_conv * rc * gco, y_attn * ra * gao], axis=1)
    return dict(gate_b=gate_b, gate_c=gate_c, hc=hc, z=z, z1=z1, z2=z2, cz=cz, y_conv=y_conv, y_attn=y_attn,
                rc=rc, ra=ra, heads=heads, y=y, row=row, scale=scale, q_raw=q_raw, rq=rq, k_raw=k_raw, rk=rk)


BPS = 2
TILE = BPS * BLK
KV0 = 3 * CW + AW


def _mix_in_specs(tile_of):
    return [
        pl.BlockSpec(memory_space=pltpu.SMEM),
        pl.BlockSpec((TILE, INW), lambda s: (tile_of(s), 0)),
        pl.BlockSpec((8, CW), lambda s: (jnp.maximum(tile_of(s) * (TILE // 8) - 1, 0), 1)),
        pl.BlockSpec((8, CW), lambda s: (jnp.maximum(tile_of(s) * (TILE // 8) - 1, 0), 2)),
        pl.BlockSpec((BLK, 2 * NKV * HD), lambda s: (jnp.maximum(tile_of(s) * BPS - 1, 0), KV0 // (2 * NKV * HD))),
    ]


def _block_inputs(tile, b, zc_ref, zh_ref, pkv_ref, first_tile):
    P = tile[b * BLK:(b + 1) * BLK, :]
    if b == 0:
        return P, zc_ref[...], zh_ref[...], pkv_ref[...], first_tile
    lo = b * BLK
    return P, tile[lo - 8:lo, CW:2 * CW], tile[lo - 8:lo, 2 * CW:3 * CW], tile[lo - BLK:lo, KV0:KV0 + 2 * NKV * HD], False


def _mix_param_specs():
    return [
        pl.BlockSpec((8, CW), lambda s: (0, 0)),
        pl.BlockSpec((1, AW), lambda s: (0, 0)),
        pl.BlockSpec((1, NKV * HD), lambda s: (0, 0)),
        pl.BlockSpec((1, CW), lambda s: (0, 0)),
        pl.BlockSpec((1, AW), lambda s: (0, 0)),
        pl.BlockSpec((AW, 128), lambda s: (0, 0)),
        pl.BlockSpec((NH * BLK, 2 * BLK), lambda s: (0, 0)),
    ]


def _mix_params(cw8, qg, kg, gco, gao, bias):
    seg = np.zeros((AW, 128), np.float32)
    seg[np.arange(AW), np.arange(AW) // HD] = 1.0
    return (cw8, jnp.tile(qg, (1, NH)), jnp.tile(kg, (1, NKV)), gco, gao, jnp.asarray(seg, BF), bias)


def _mix_fwd(proj, sinks, cw8, qg, kg, gco, gao, bias, comm=()):
    def body(sink_ref, p_ref, zc_ref, zh_ref, pkv_ref, cw_ref, qg_ref, kg_ref, gco_ref, gao_ref, seg_ref, bias_ref, y_ref):
        tile = p_ref[...]
        for b in range(BPS):
            f = _mix_forward(*_block_inputs(tile, b, zc_ref, zh_ref, pkv_ref, pl.program_id(0) == 0), cw_ref[...],
                             qg_ref[...], kg_ref[...], gco_ref[...], gao_ref[...], seg_ref[...], sink_ref, bias_ref, by_head=True)
            y_ref[b * BLK:(b + 1) * BLK, :] = f["y"].astype(BF)

    return _call(
        body, (sinks, proj, proj, proj, proj, *_mix_params(cw8, qg, kg, gco, gao, bias)), name="mix_fwd", grid=(T // TILE,),
        in_specs=_mix_in_specs(lambda s: s) + _mix_param_specs(),
        out_specs=[pl.BlockSpec((TILE, D), lambda s: (s, 0))], out_shape=[SDS((T, D), BF)],
        sem=("parallel",), vmem_mib=40, comm=comm, free=tuple(range(5, 12)))


def _mix_bwd(proj, dy, sinks, cw8, qg, kg, gco, gao, bias, comm=()):
    n_steps = T // TILE

    def tile_of(s):
        return n_steps - 1 - s

    def body(sink_ref, p_ref, zc_ref, zh_ref, pkv_ref, dy_ref, cw_ref, qg_ref, kg_ref, gco_ref, gao_ref, seg_ref, bias_ref,
             dproj_ref, dcw_ref, dqg_ref, dkg_ref, dgco_ref, dgao_ref, dsink_ref, dbias_ref,
             ndcz_ref, dkc_ref, dvc_ref):
        s = pl.program_id(0)

        @pl.when(s == 0)
        def _():
            for r in (dcw_ref, dqg_ref, dkg_ref, dgco_ref, dgao_ref, dsink_ref, dbias_ref, ndcz_ref, dkc_ref, dvc_ref):
                r[...] = jnp.zeros_like(r)

        params = (cw_ref[...], qg_ref[...], kg_ref[...], gco_ref[...], gao_ref[...], seg_ref[...])
        tile = p_ref[...]
        carry = (ndcz_ref[...], dkc_ref[...], dvc_ref[...])
        total = None
        for b in reversed(range(BPS)):
            f = _mix_forward(*_block_inputs(tile, b, zc_ref, zh_ref, pkv_ref, s == n_steps - 1), *params, sink_ref, bias_ref)
            pieces, sums, carry = one_block(f, dy_ref[b * BLK:(b + 1) * BLK, :], params, carry)
            for lo, piece in pieces:
                dproj_ref[b * BLK:(b + 1) * BLK, lo:lo + piece.shape[1]] = piece
            total = sums if total is None else [t + v for t, v in zip(total, sums)]
        ndcz_ref[...], dkc_ref[...], dvc_ref[...] = carry
        dcw, dqg_t, dkg_t, dgco, dgao, dsink, *ds = total
        dcw_ref[0:3, :] += dcw
        dqg_ref[...] += _fold_heads(dqg_t)
        dkg_ref[...] += _fold_heads(dkg_t)
        dgco_ref[...] += dgco
        dgao_ref[...] += dgao
        dsink_ref[...] += dsink
        for h in range(NH):
            dbias_ref[h * BLK:(h + 1) * BLK, :] += ds[h]

    def one_block(f, dy, params, carry):
        cw, qg_v, kg_v, gco_v, gao_v, seg = params
        nxt, dk_carry, dv_carry = carry
        dyc, dgco = _rms_bwd(dy[:, 0:CW], f["y_conv"], f["rc"], gco_v)
        dya, dgao = _rms_bwd(dy[:, CW:CW + AW], f["y_attn"], f["ra"], gao_v)

        row = f["row"]
        dgate_b = dyc * f["cz"]
        dcz = dyc * f["gate_b"]
        dcw = jnp.concatenate([jnp.sum(dcz * f[k], axis=0, keepdims=True) for k in ("z2", "z1", "z")], axis=0)
        n0 = nxt[0:1, :]
        n1 = nxt[1:2, :]
        d1 = jnp.where(row == BLK - 1, n0, pltpu.roll(dcz, BLK - 1, 0))
        d2 = jnp.where(row == BLK - 1, n1, jnp.where(row == BLK - 2, n0, pltpu.roll(dcz, BLK - 2, 0)))
        dz = cw[2:3, :] * dcz + cw[1:2, :] * d1 + cw[0:1, :] * d2
        pieces = [(0, dgate_b.astype(BF)), (CW, (dz * f["hc"]).astype(BF)), (2 * CW, (dz * f["gate_c"]).astype(BF))]

        scale = f["scale"]
        lane = lax.broadcasted_iota(jnp.int32, (1, 128), 1)
        dsink = jnp.zeros((1, 128), F32)
        dq_cols, dk_cols, dv_cols, dk_prev, dv_prev, ds = [], [], [], [], [], []
        for kv in range(NKV):
            dKb = dVb = 0.0
            for h in range(kv * GQ, (kv + 1) * GQ):
                hd = f["heads"][h]
                dO = dya[:, h * HD:(h + 1) * HD]
                delta = jnp.sum(dO * hd["O"], axis=-1, keepdims=True)
                dOb = dO.astype(BF)
                dP = _dot(dOb, hd["vb"], 1, 1)
                dS = hd["probs"] * (dP - delta)
                tot = jnp.sum(hd["psink"] * delta, axis=0, keepdims=True)
                dsink = dsink - jnp.where(lane == h, tot, 0.0)
                ds.append(dS)
                dSb = dS.astype(BF)
                dq_cols.append(_dot(dSb, hd["kb"], 1, 0))
                dKb = dKb + _dot(dSb, hd["Q"], 0, 0)
                dVb = dVb + _dot(hd["probs"].astype(BF), dOb, 0, 0)
            dk_cols.append(dKb[BLK:, :] + dk_carry[:, kv * HD:(kv + 1) * HD])
            dv_cols.append(dVb[BLK:, :] + dv_carry[:, kv * HD:(kv + 1) * HD])
            dk_prev.append(dKb[:BLK, :])
            dv_prev.append(dVb[:BLK, :])
        dq_raw, dqg_t = _head_norm_bwd(jnp.concatenate(dq_cols, axis=1) * scale, f["q_raw"], f["rq"], qg_v, seg)
        dk_raw, dkg_t = _head_norm_bwd(jnp.concatenate(dk_cols, axis=1), f["k_raw"][BLK:, :], f["rk"][BLK:, :], kg_v, seg)
        pieces.append((3 * CW, jnp.concatenate([dq_raw, dk_raw] + dv_cols, axis=1).astype(BF)))
        owed = (dcz[0:8, :], jnp.concatenate(dk_prev, axis=1), jnp.concatenate(dv_prev, axis=1))
        return pieces, [dcw, dqg_t, dkg_t, dgco, dgao, dsink, *ds], owed

    small = lambda r, c: pl.BlockSpec((r, c), lambda s: (0, 0))
    return _call(
        body, (sinks, proj, proj, proj, proj, dy, *_mix_params(cw8, qg, kg, gco, gao, bias)), name="mix_bwd", grid=(n_steps,),
        in_specs=_mix_in_specs(tile_of) + [pl.BlockSpec((TILE, D), lambda s: (tile_of(s), 0))] + _mix_param_specs(),
        out_specs=[pl.BlockSpec((TILE, INW), lambda s: (tile_of(s), 0)), small(8, CW), small(1, HD), small(1, HD),
                   small(1, CW), small(1, AW), small(1, 128), small(NH * BLK, 2 * BLK)],
        out_shape=[SDS((T, INW), BF), SDS((8, CW), F32), SDS((1, HD), F32), SDS((1, HD), F32), SDS((1, CW), F32),
                   SDS((1, AW), F32), SDS((1, 128), F32), SDS((NH * BLK, 2 * BLK), F32)],
        scratch_shapes=[pltpu.VMEM((8, CW), F32), pltpu.VMEM((BLK, NKV * HD), F32), pltpu.VMEM((BLK, NKV * HD), F32)],
        sem=("arbitrary",), vmem_mib=56, comm=comm, free=(1, 2, 3, 4) + tuple(range(6, 13)))


FT = 256
NFT = DFF // FT
RC = 1024
NCH = T // RC
LEAD = 16


def _rows8(x):
    return jnp.sum(x.reshape(x.shape[0] // 8, 8, x.shape[1]), axis=0)


def _ffn_act_specs():
    return [
        pl.BlockSpec((T, FT), lambda j: (0, j)), pl.BlockSpec((T, FT), lambda j: (0, NFT + j)),
        pl.BlockSpec((8, FT), lambda j: (0, j)), pl.BlockSpec((8, FT), lambda j: (0, NFT + j)),
        pl.BlockSpec((1, FT), lambda j: (0, j)), pl.BlockSpec((1, FT), lambda j: (0, NFT + j)),
    ]


def _conv_rows(win, w, b, n):
    win = win.astype(F32)
    u = win[LEAD:LEAD + n]
    u1 = pltpu.roll(win, 1, 0)[LEAD:LEAD + n]
    u2 = pltpu.roll(win, 2, 0)[LEAD:LEAD + n]
    return u2, u1, u, w[0:1, :] * u2 + w[1:2, :] * u1 + w[2:3, :] * u + b


def _ffn_act(up, fw8, fb, comm=()):
    def body(ug_ref, uv_ref, wg_ref, wv_ref, bg_ref, bv_ref, a_ref, pg_ref, pv_ref):
        wg, wv, bg, bv = wg_ref[...], wv_ref[...], bg_ref[...], bv_ref[...]

        def chunk(rows, win_g, win_v):
            gp = _conv_rows(win_g, wg, bg, RC)[3]
            vp = _conv_rows(win_v, wv, bv, RC)[3]
            a_ref[rows, :] = (gp * jax.nn.sigmoid(gp) * vp).astype(BF)
            pg_ref[0, rows, :] = gp.astype(BF)
            pv_ref[0, rows, :] = vp.astype(BF)

        zero = jnp.zeros((LEAD, FT), BF)
        chunk(pl.ds(0, RC), jnp.concatenate([zero, ug_ref[0:RC, :]], axis=0), jnp.concatenate([zero, uv_ref[0:RC, :]], axis=0))

        def step(i, carry):
            r0 = pl.multiple_of(i * RC, RC)
            win = pl.ds(r0 - LEAD, RC + LEAD)
            chunk(pl.ds(r0, RC), ug_ref[win, :], uv_ref[win, :])
            return carry

        lax.fori_loop(1, NCH, step, 0)

    tile = pl.BlockSpec((1, T, FT), lambda j: (j, 0, 0))
    return _call(
        body, (up, up, fw8, fw8, fb, fb), name="ffn_act", grid=(NFT,), in_specs=_ffn_act_specs(),
        out_specs=[pl.BlockSpec((T, FT), lambda j: (0, j)), tile, tile],
        out_shape=[SDS((T, DFF), BF), SDS((NFT, T, FT), BF), SDS((NFT, T, FT), BF)],
        sem=("parallel",), vmem_mib=40, comm=comm, free=(2, 3, 4, 5))


def _ffn_act_bwd(up, pre_g, pre_v, da, fw8, comm=()):
    ext = RC + LEAD

    def body(ug_ref, uv_ref, wg_ref, wv_ref, pg_ref, pv_ref, da_ref, dug_ref, duv_ref, dwg_ref, dwv_ref, dbg_ref, dbv_ref):
        wg, wv = wg_ref[...], wv_ref[...]

        def chunk(u_g, u_v, gp, vp, da_e):
            gp, vp, da_e = gp.astype(F32), vp.astype(F32), da_e.astype(F32)
            sig = jax.nn.sigmoid(gp)
            dvp = da_e * (gp * sig)
            dgp = da_e * vp * (sig * (1.0 + gp * (1.0 - sig)))

            def branch(dp, w, u):
                d0, d1, d2 = dp[0:RC], pltpu.roll(dp, ext - 1, 0)[0:RC], pltpu.roll(dp, ext - 2, 0)[0:RC]
                du = (w[2:3, :] * d0 + w[1:2, :] * d1 + w[0:1, :] * d2).astype(BF)
                u = u.astype(F32)
                return du, [_rows8(d0), _rows8(d2 * u), _rows8(d1 * u), _rows8(d0 * u)]

            dug, sums_g = branch(dgp, wg, u_g)
            duv, sums_v = branch(dvp, wv, u_v)
            return dug, duv, sums_g + sums_v

        def step(i, acc):
            r0 = pl.multiple_of(i * RC, RC)
            rows, more = pl.ds(r0, RC), pl.ds(r0, ext)
            dug, duv, part = chunk(ug_ref[rows, :], uv_ref[rows, :], pg_ref[0, more, :], pv_ref[0, more, :], da_ref[more, :])
            dug_ref[rows, :] = dug
            duv_ref[rows, :] = duv
            return [a + p for a, p in zip(acc, part)]

        acc = lax.fori_loop(0, NCH - 1, step, [jnp.zeros((8, FT), F32)] * 8)
        r0 = T - RC
        zero = jnp.zeros((LEAD, FT), BF)
        tail = lambda rows: jnp.concatenate([rows, zero], axis=0)
        dug, duv, part = chunk(ug_ref[r0:T, :], uv_ref[r0:T, :], tail(pg_ref[0, r0:T, :]), tail(pv_ref[0, r0:T, :]),
                               tail(da_ref[r0:T, :]))
        dug_ref[r0:T, :] = dug
        duv_ref[r0:T, :] = duv
        tot = [jnp.sum(a + p, axis=0, keepdims=True) for a, p in zip(acc, part)]
        for k, (dw_ref, db_ref) in enumerate(((dwg_ref, dbg_ref), (dwv_ref, dbv_ref))):
            db_ref[...] = tot[4 * k]
            dw_ref[...] = jnp.zeros_like(dw_ref)
            for r in range(3):
                dw_ref[r:r + 1, :] = tot[4 * k + 1 + r]

    col = lambda r: pl.BlockSpec((r, FT), lambda j: (0, j))
    return _call(
        body, (up, up, fw8, fw8, pre_g, pre_v, da), name="ffn_act_bwd", grid=(NFT,),
        in_specs=_ffn_act_specs()[0:4] + [pl.BlockSpec((1, T, FT), lambda j: (j, 0, 0))] * 2 + [col(T)],
        out_specs=[col(T), col(T), col(8), col(8), col(1), col(1)],
        out_shape=[SDS((T, DFF), BF), SDS((T, DFF), BF), SDS((8, DFF), F32), SDS((8, DFF), F32),
                   SDS((1, DFF), F32), SDS((1, DFF), F32)],
        sem=("parallel",), vmem_mib=40, comm=comm, free=(0, 1, 2, 3))


def _ffn_down_bwd(dh2b, w_down, comm=()):
    tm = TM

    def body(d_ref, w_ref, o_ref):
        o_ref[...] = _dot(d_ref[...], w_ref[...], 1, 1).astype(BF)

    return _call(
        body, (dh2b, w_down), name="ffn_down_bwd", grid=(T // tm,),
        in_specs=[pl.BlockSpec((tm, D), lambda i: (i, 0)), _resident((DFF, D))],
        out_specs=[pl.BlockSpec((tm, DFF), lambda i: (i, 0))], out_shape=[SDS((T, DFF), BF)],
        sem=("parallel",), vmem_mib=40, comm=comm, free=(0, 1))


def _norm_matmul_bwd(name, a_list, w_t, k_offsets, xin, g, dres, want_bf16, comm=(), slot=None, band=(), adam=()):
    tm = TM
    ks = [a.shape[1] for a in a_list]
    n_a = len(a_list)
    n_pre = 0 if slot is None else 1
    n_in = n_a + 4 + len(band) + len(adam)

    def body(*refs):
        refs = refs[n_pre:]
        a_refs = refs[:n_a]
        w_ref, x_ref, g_ref, r_ref = refs[n_a:n_a + 4]
        outs = refs[n_in:]
        dg_out = outs[1 + want_bf16]
        dx_ref, dg_ref = outs[0], (dg_out if slot is None else dg_out.at[0])

        @pl.when(pl.program_id(0) == 0)
        def _():
            dg_ref[...] = jnp.zeros_like(dg_ref)
            if band:
                db_ref, bk_ref, tbl_ref = refs[n_a + 4], refs[n_a + 5], outs[2 + want_bf16]
                bk = bk_ref[...]
                for b in range(NBUCKET):
                    m = bk == b
                    for h in range(NH):
                        v = jnp.where(m, db_ref[h * BLK:(h + 1) * BLK, :], 0.0)
                        tbl_ref[0, h:h + 1, b:b + 1] = jnp.sum(jnp.sum(v, axis=1, keepdims=True), axis=0, keepdims=True)

        du = _dot(a_refs[0][...], w_ref[k_offsets[0]:k_offsets[0] + ks[0], :], 1, 0)
        for k in range(1, n_a):
            du = du + _dot(a_refs[k][...], w_ref[k_offsets[k]:k_offsets[k] + ks[k], :], 1, 0)
        x = x_ref[...]
        r = lax.rsqrt(jnp.mean(x * x, axis=-1, keepdims=True) + EPS)
        dx, dg = _rms_bwd(du, x, r, g_ref[...])
        dx = r_ref[...] + dx
        dx_ref[...] = dx
        if want_bf16:
            outs[1][...] = dx.astype(BF)
        dg_ref[...] += dg
        if adam:
            first = 2 + want_bf16 + bool(band)
            _adam_strips(*refs[n_in - 4:n_in], outs[first:first + 4], adam[0].shape[1], True, True)

    tile = lambda c: pl.BlockSpec((tm, c), lambda i, *_: (i, 0))
    if slot is None:
        dg_spec, dg_shape = pl.BlockSpec((1, D), lambda i: (0, 0)), SDS((1, D), F32)
    else:
        dg_spec, dg_shape = pl.BlockSpec((1, 1, D), lambda i, slot_ref: (slot_ref[0], 0, 0)), SDS((N_DEV, 1, D), F32)
    out_specs = [tile(D)] + ([tile(D)] if want_bf16 else []) + [dg_spec]
    out_shape = [SDS((T, D), F32)] + ([SDS((T, D), BF)] if want_bf16 else []) + [dg_shape]
    if band:
        out_specs.append(pl.BlockSpec((1, NH, NBUCKET), lambda i, slot_ref: (slot_ref[0], 0, 0)))
        out_shape.append(SDS((N_DEV, NH, NBUCKET), F32))
    adam_specs = []
    if adam:
        rows, cols = adam[0].shape
        tr = rows // (T // tm)
        rows_spec = pl.BlockSpec((tr, cols), lambda i, *_: (i, 0))
        adam_specs = [rows_spec, pl.BlockSpec((cols, tr), lambda i, *_: (0, i)), rows_spec, rows_spec]
        out_specs += [rows_spec] * 4
        out_shape += [SDS((rows, cols), F32)] * 4
    return _call(
        body, (*a_list, w_t, xin, g, dres, *band, *adam), name=name, grid=(T // tm,), prefetch=() if slot is None else (slot,),
        in_specs=[tile(k) for k in ks] + [_resident(w_t.shape), tile(D), pl.BlockSpec((1, D), lambda i, *_: (0, 0)), tile(D)]
        + [pl.BlockSpec(b.shape, lambda i, *_: (0, 0)) for b in band] + adam_specs,
        out_specs=out_specs, out_shape=out_shape, sem=("arbitrary",), vmem_mib=56, comm=comm, free=tuple(range(n_a + 4)))


def _out_bwd(dh1b, w_out, comm=()):
    tm = TM

    def body(d_ref, w_ref, o_ref):
        o_ref[...] = _dot(d_ref[...], w_ref[...], 1, 1)

    return _call(
        body, (dh1b, w_out), name="out_bwd", grid=(T // tm,),
        in_specs=[pl.BlockSpec((tm, D), lambda i: (i, 0)), _resident((D, D))],
        out_specs=[pl.BlockSpec((tm, D), lambda i: (i, 0))], out_shape=[SDS((T, D), F32)],
        sem=("parallel",), vmem_mib=32, comm=comm, free=(0, 1))


def _wgrad(name, a_list, b, old_a, comm=()):
    m_k = a_list[0].shape[1]
    tm = max(t for t in range(128, m_k // 2 + 1, 128) if m_k % t == 0)
    steps = [a.shape[1] // tm for a in a_list]
    starts = [sum(steps[:k]) for k in range(len(a_list))]
    n_a = len(a_list)

    def body(*refs):
        a_refs, b_ref, o_ref = refs[:n_a], refs[n_a], refs[n_a + 1]
        i = pl.program_id(0)
        for k in range(n_a):
            @pl.when((i >= starts[k]) & (i < starts[k] + steps[k]))
            def _(k=k):
                o_ref[...] = _dot(a_refs[k][...], b_ref[...], 0, 0).astype(BF)

    def a_spec(k):
        return pl.BlockSpec((T, tm), lambda i: (0, jnp.clip(i - starts[k], 0, steps[k] - 1)))

    m_total = tm * sum(steps)
    return _call(
        body, (*a_list, b), name=name, grid=(sum(steps),),
        in_specs=[a_spec(k) for k in range(n_a)] + [_resident((T, D))],
        out_specs=[pl.BlockSpec((tm, D), lambda i: (i, 0))], out_shape=[SDS((m_total, D), BF)],
        sem=("parallel",), vmem_mib=40, comm=comm, free=() if old_a is None else tuple(range(n_a)) if old_a else (n_a,))


def _chip_sum(name, gbf, from_sib, core, chip):
    h = gbf.shape[1]
    th = h

    def body(core_ref, chip_ref, g_ref, s_ref, pbf_ref, own_ref):
        p = g_ref[0].astype(F32) + s_ref[0].astype(F32)
        pbf_ref[0] = p.astype(BF)

        @pl.when(pl.program_id(1) == chip_ref[0])
        def _():
            own_ref[...] = p

    grid_spec = pltpu.PrefetchScalarGridSpec(
        num_scalar_prefetch=2, grid=(h // th, N_CHIPS),
        in_specs=[pl.BlockSpec((1, th, D), lambda t, jj, core_ref, chip_ref: (2 * jj + core_ref[0], t, 0)),
                  pl.BlockSpec((1, th, D), lambda t, jj, core_ref, chip_ref: (jj, t, 0))],
        out_specs=[pl.BlockSpec((1, th, D), lambda t, jj, core_ref, chip_ref: (jj, t, 0)),
                   pl.BlockSpec((th, D), lambda t, jj, core_ref, chip_ref: (t, 0))],
    )
    return _pcall(
        body, name=name, grid_spec=grid_spec, out_shape=_in_hbm([SDS((N_CHIPS, h, D), BF), SDS((h, D), F32)]),
        compiler_params=_params(("arbitrary", "arbitrary"), 32),
    )(core, chip, *_from_hbm(gbf, from_sib))


def _final_sum(name, own, from_chips, core, comm=()):
    h = own.shape[0]
    n = 4 if h % (4 * ROWS16) == 0 else 2
    th = h // n

    def body(core_ref, o_ref, r_ref, f_ref):
        f_ref[0] = ((o_ref[...] + r_ref[0].astype(F32)) + r_ref[1].astype(F32)) + r_ref[2].astype(F32)

    return _call(
        body, (own, from_chips), name=name, grid=(n,), prefetch=(core,),
        in_specs=[pl.BlockSpec((th, D), lambda i, core_ref: (i, 0)), pl.BlockSpec((3, th, D), lambda i, core_ref: (0, i, 0))],
        out_specs=[pl.BlockSpec((1, th, D), lambda i, core_ref: (core_ref[0], i, 0))], out_shape=[SDS((2, h, D), F32)],
        sem=("arbitrary",), vmem_mib=40, comm=comm)


def _adam_math(w, g, m, v):
    nm = ADAM_B1 * m + (1.0 - ADAM_B1) * g
    nv = ADAM_B2 * v + (1.0 - ADAM_B2) * (g * g)
    m_hat = nm / (1.0 - ADAM_B1 ** ADAM_STEP)
    v_hat = nv / (1.0 - ADAM_B2 ** ADAM_STEP)
    return -ADAM_LR * (m_hat / (jnp.sqrt(v_hat) + ADAM_EPS) + ADAM_WD * w), nm, nv


def _adam_strips(w_ref, g_ref, m_ref, v_ref, outs, cols, copy_g, g_transposed):
    d_ref, nm_ref, nv_ref = outs[-3:]
    for c in [pl.ds(c0, 128) for c0 in range(0, cols, 128)] if g_transposed else [slice(None)]:
        g_val = g_ref[c, :].T if g_transposed else g_ref[...]
        if copy_g:
            outs[0][:, c] = g_val
        d_ref[:, c], nm_ref[:, c], nv_ref[:, c] = _adam_math(w_ref[:, c], g_val, m_ref[:, c], v_ref[:, c])


def _adamw(name, w, g, m, v, tr, copy_g=False, stage=True, g_transposed=False):
    rows, cols = w.shape

    def body(w_ref, g_ref, m_ref, v_ref, *outs):
        _adam_strips(w_ref, g_ref, m_ref, v_ref, outs, cols, copy_g, g_transposed)

    spec = pl.BlockSpec((tr, cols), lambda i: (i, 0))
    n_out = 4 if copy_g else 3
    g_spec = pl.BlockSpec((cols, tr), lambda i: (0, i)) if g_transposed else spec
    return _call(body, (w, g, m, v), name=name, grid=(rows // tr,), in_specs=[spec, g_spec, spec, spec], out_specs=[spec] * n_out,
                 out_shape=[SDS((rows, cols), F32)] * n_out, sem=("parallel",), vmem_mib=32,
                 free=(0, 2, 3) if stage else ())


C_SQ = 2 * DFF
P_W = C_SQ + 128
R_G2, R_GO, R_DCW, R_QK = 0, 1, 2, 5
C_GCO, C_GAO, C_DQG, C_DKG, C_SINK = 0, CW, 0, 128, 256


def _pack(name, me, ins, width, fill):
    def body(me_ref, *refs):
        o = refs[-1]
        o[...] = jnp.zeros_like(o)
        fill(o, *refs[:-1])

    return _call(body, ins, name=name, grid=(1,), prefetch=(me,),
                 in_specs=[pl.BlockSpec(a.shape, lambda i, me_ref: (0, 0)) for a in ins],
                 out_specs=[pl.BlockSpec((1, 8, width), lambda i, me_ref: (me_ref[0], 0, 0))],
                 out_shape=[SDS((N_DEV, 8, width), F32)], sem=("arbitrary",))[0]


def _pack_ffn(me, dfwg, dfwv, dfbg, dfbv, sq):
    def fill(o, dfwg_r, dfwv_r, dfbg_r, dfbv_r, sq_r):
        o[0, :, 0:DFF] = dfwg_r[...]
        o[0, :, DFF:2 * DFF] = dfwv_r[...]
        o[0, 3:4, 0:DFF] = dfbg_r[...]
        o[0, 3:4, DFF:2 * DFF] = dfbv_r[...]
        o[0, :, C_SQ:C_SQ + 128] = sq_r[...]

    return _pack("pack_ffn", me, (dfwg, dfwv, dfbg, dfbv, sq), P_W, fill)


def _pack_mix(me, dg2, dgco, dgao, dcw8, dqg, dkg, dsink):
    def fill(o, dg2_r, dgco_r, dgao_r, dcw_r, dqg_r, dkg_r, dsink_r):
        o[0, R_G2:R_G2 + 1, :] = dg2_r[...]
        o[0, R_GO:R_GO + 1, C_GCO:C_GCO + CW] = dgco_r[...]
        o[0, R_GO:R_GO + 1, C_GAO:C_GAO + AW] = dgao_r[...]
        o[0, R_DCW:R_DCW + 3, 0:CW] = dcw_r[0:3, :]
        o[0, R_QK:R_QK + 1, C_DQG:C_DQG + HD] = dqg_r[...]
        o[0, R_QK:R_QK + 1, C_DKG:C_DKG + HD] = dkg_r[...]
        o[0, R_QK:R_QK + 1, C_SINK:C_SINK + 128] = dsink_r[...]

    return _pack("pack_mix", me, (dg2, dgco, dgao, dcw8, dqg, dkg, dsink), D, fill)


N_SMALL = 11


def _small_adam(chip, p_all, pm_all, g1_all, tbl_all, ws, ms, vs):
    fw_cols = 2 * DFF // N_CHIPS
    cw_cols = CW // N_CHIPS

    def body(chip_ref, p_ref, fw_ref, pm_ref, cw_ref, g1_ref, tbl_ref, *refs):
        w_r, m_r, v_r = refs[0:N_SMALL], refs[N_SMALL:2 * N_SMALL], refs[2 * N_SMALL:3 * N_SMALL]
        outs = refs[3 * N_SMALL:]
        g_o, d_o, nm_o, nv_o = (outs[k * N_SMALL:(k + 1) * N_SMALL] for k in range(4))
        loss_o = outs[4 * N_SMALL]

        def total(ref):
            s = ref[0]
            for k in range(1, N_DEV):
                s = s + ref[k]
            return s

        S = total(p_ref)
        fw = total(fw_ref)
        M = total(pm_ref)
        cw = total(cw_ref)

        def step(i, g, at):
            d, nm, nv = _adam_math(w_r[i][at], g, m_r[i][at], v_r[i][at])
            g_o[i][at], d_o[i][at], nm_o[i][at], nv_o[i][at] = g, d, nm, nv

        everything = (slice(None), slice(None))
        step(0, total(g1_ref), everything)
        for r in range(3):
            step(1, cw[R_DCW + r:R_DCW + r + 1, :], (r, slice(None), slice(None)))
        step(2, M[R_QK:R_QK + 1, C_DQG:C_DQG + HD], everything)
        step(3, M[R_QK:R_QK + 1, C_DKG:C_DKG + HD], everything)
        step(4, total(tbl_ref), everything)
        step(5, M[R_QK:R_QK + 1, C_SINK:C_SINK + NH], everything)
        step(6, M[R_GO:R_GO + 1, C_GCO:C_GCO + CW], everything)
        step(7, M[R_GO:R_GO + 1, C_GAO:C_GAO + AW], everything)
        step(8, M[R_G2:R_G2 + 1, :], everything)
        for r in range(3):
            step(9, fw[r:r + 1, :], (r, slice(None), slice(None)))
        step(10, S[3:4, 0:2 * DFF], everything)
        sq = S[:, C_SQ:C_SQ + 128]
        loss_o[...] = jnp.sum(jnp.sum(sq, axis=1, keepdims=True), axis=0, keepdims=True) * (0.5 / D)

    def full(a):
        n = len(a.shape)
        return pl.BlockSpec(a.shape, lambda i, chip_ref: (0,) * n)

    params = [*ws, *ms, *vs]
    out = _call(
        body, (p_all, p_all, pm_all, pm_all, g1_all, tbl_all, *params), name="small_adam", grid=(1,), prefetch=(chip,),
        in_specs=[full(p_all),
                  pl.BlockSpec((N_DEV, 8, fw_cols), lambda i, chip_ref: (0, 0, chip_ref[0])),
                  full(pm_all),
                  pl.BlockSpec((N_DEV, 8, cw_cols), lambda i, chip_ref: (0, 0, chip_ref[0])),
                  full(g1_all), full(tbl_all), *[full(a) for a in params]],
        out_specs=[full(a) for a in ws] * 4 + [pl.BlockSpec((1, 1), lambda i, chip_ref: (0, 0))],
        out_shape=[SDS(a.shape, F32) for a in ws] * 4 + [SDS((1, 1), F32)], sem=("arbitrary",), vmem_mib=32)
    return out[0:N_SMALL], out[N_SMALL:2 * N_SMALL], out[2 * N_SMALL:3 * N_SMALL], out[3 * N_SMALL:4 * N_SMALL], out[4 * N_SMALL]


PLACE_STEPS = 4


def _place_specs(shards):
    rows = [s.shape[0] // PLACE_STEPS for s in shards]
    return ([pl.BlockSpec((r, D), lambda i, chip_ref: (i, 0)) for r in rows],
            [pl.BlockSpec((r, D), lambda i, chip_ref: (chip_ref[0] * PLACE_STEPS + i, 0)) for r in rows],
            [SDS((N_CHIPS * s.shape[0], D), BF) for s in shards])


def _place_first(chip, shard, conv_w, ffn_conv_w):
    def body(chip_ref, a, s0, s1, o, t0, t1):
        o[...] = a[...].astype(BF)

        @pl.when(pl.program_id(0) == 0)
        def _():
            for s, t in ((s0, t0), (s1, t1)):
                t[...] = jnp.zeros_like(t)
                t[0, 0:3, :] = s[...]

    ins, outs, shapes = _place_specs([shard])
    taps = (conv_w, ffn_conv_w)
    return _call(
        body, (shard, conv_w, ffn_conv_w), name="place_first", grid=(PLACE_STEPS,), prefetch=(chip,),
        in_specs=ins + [pl.BlockSpec(s.shape, lambda i, chip_ref: (0, 0)) for s in taps],
        out_specs=outs + [pl.BlockSpec((1, 8, s.shape[1]), lambda i, chip_ref: (chip_ref[0], 0, 0)) for s in taps],
        out_shape=shapes + [SDS((N_CHIPS, 8, s.shape[1]), F32) for s in taps],
        sem=("arbitrary",), vmem_mib=32, free=(1, 2))


def _place_rest(chip, shards, w_up, table, bucket, comm):
    n = len(shards)
    c_up = w_up.shape[1]
    edges = [round(k * (c_up // 128) / PLACE_STEPS) * 128 for k in range(PLACE_STEPS + 1)]

    def body(chip_ref, *refs):
        a, (up_ref, tab_ref, bk_ref), o = refs[:n], refs[n:n + 3], refs[n + 3:2 * n + 3]
        up_o, bias_ref = refs[2 * n + 3:]
        for src, dst in zip(a, o):
            dst[...] = src[...].astype(BF)
        for k in range(PLACE_STEPS):
            @pl.when(pl.program_id(0) == k)
            def _(k=k):
                up_o[edges[k]:edges[k + 1], :] = up_ref[:, edges[k]:edges[k + 1]].T.astype(BF)

        @pl.when(pl.program_id(0) == 0)
        def _():
            bk = bk_ref[...]
            eq = [bk == b for b in range(NBUCKET)]
            for h in range(NH):
                acc = jnp.zeros((BLK, 2 * BLK), F32)
                for b in range(NBUCKET):
                    acc = jnp.where(eq[b], tab_ref[h, b], acc)
                bias_ref[h * BLK:(h + 1) * BLK, :] = acc

    ins, outs, shapes = _place_specs(shards)
    return _call(
        body, (*shards, w_up, table, bucket), name="place_rest", grid=(PLACE_STEPS,), prefetch=(chip,),
        in_specs=ins + [_resident(w_up.shape), pl.BlockSpec(memory_space=pltpu.SMEM),
                        pl.BlockSpec(bucket.shape, lambda i, chip_ref: (0, 0))],
        out_specs=outs + [pl.BlockSpec((c_up, D), lambda i, chip_ref: (chip_ref[0], 0)),
                          pl.BlockSpec((NH * BLK, 2 * BLK), lambda i, chip_ref: (0, 0))],
        out_shape=shapes + [SDS((N_CHIPS * c_up, D), BF), SDS((NH * BLK, 2 * BLK), F32)],
        sem=("arbitrary",), vmem_mib=32, comm=comm, free=(n + 1, n + 2))


def kernel(x, norm_mix_g, w_in, conv_w, q_norm_g, k_norm_g, rel_bias_table, sinks, out_norm_conv_g, out_norm_attn_g, w_out, norm_ffn_g, w_up, ffn_conv_w, ffn_conv_b, w_down, loss_target, m_norm_mix_g, m_w_in, m_conv_w, m_q_norm_g, m_k_norm_g, m_rel_bias_table, m_sinks, m_out_norm_conv_g, m_out_norm_attn_g, m_w_out, m_norm_ffn_g, m_w_up, m_ffn_conv_w, m_ffn_conv_b, m_w_down, v_norm_mix_g, v_w_in, v_conv_w, v_q_norm_g, v_k_norm_g, v_rel_bias_table, v_sinks, v_out_norm_conv_g, v_out_norm_attn_g, v_w_out, v_norm_ffn_g, v_w_up, v_ffn_conv_w, v_ffn_conv_b, v_w_down):
    as_arg = lambda i: jnp.reshape(i, (1,)).astype(jnp.int32)
    chip = as_arg(2 * lax.axis_index("x") + lax.axis_index("y"))
    core = as_arg(lax.axis_index("c"))
    me = 2 * chip + core
    xs, tgt = x[0], loss_target[0]
    qg, kg, gco, gao, g1, g2, fb = q_norm_g, k_norm_g, out_norm_conv_g, out_norm_attn_g, norm_mix_g, norm_ffn_g, ffn_conv_b
    pieces = lambda g: g.reshape(N_DEV, g.shape[0] // N_DEV, D)
    whole = lambda f: f.reshape(2 * f.shape[1], D)

    bucket = jnp.asarray(_bucket_table())
    p_in, p_cw, p_fw = _place_first(chip, w_in[0].T, conv_w[0], ffn_conv_w[0])
    p_out, p_down, p_up, bias, w_int, cw_all, fw_all = _place_rest(
        chip, [w_out[0], w_down[0]], w_up[0], rel_bias_table.T, bucket,
        comm=[_t_gather(p_in, relayed_first=True), _t_small_weights(p_cw), _t_small_weights(p_fw)])
    cw8 = jnp.transpose(cw_all, (1, 0, 2)).reshape(8, CW)
    fw8 = jnp.transpose(fw_all, (1, 0, 2)).reshape(8, 2 * DFF)

    early = 3 / 11
    proj, u1, w_out_f, p_up = _inproj(xs, g1, w_int, comm=[_t_gather(p_out), _t_gather(p_up, (0, early))])
    y, w_upt = _mix_fwd(proj, sinks, cw8, qg, kg, gco, gao, bias, comm=[_t_gather(p_up, (early, 1))])
    h1, u2 = _outproj(y, w_out_f, xs, g2)
    up, w_down_f = _ffn_up(u2, w_upt, comm=[_t_gather(p_down)])
    a, pre_g, pre_v = _ffn_act(up, fw8, fb)
    dh2, dh2b, sq = _ffn_down(a, w_down_f, h1, tgt)

    gdbf, = _wgrad("wgrad_down", [a], dh2b, None)
    da, sib_down = _ffn_down_bwd(dh2b, w_down_f, comm=[_t_sibling(pieces(gdbf))])
    pbf_down, own_down = _chip_sum("chip_sum_w_down", pieces(gdbf), sib_down, core, chip)
    dug, duv, dfwg, dfwv, dfbg, dfbv, chips_down = _ffn_act_bwd(up, pre_g, pre_v, da, fw8, comm=[_t_chips(pbf_down)])
    fin_down, = _final_sum("final_sum_w_down", own_down, chips_down, core)
    gubf, = _wgrad("wgrad_up", [dug, duv], u2, False)
    p_all = _pack_ffn(me, dfwg, dfwv, dfbg, dfbv, sq)
    dh1, dh1b, dg2, sib_up, fin_down, p_all = _norm_matmul_bwd(
        "ffn_up_bwd", [dug, duv], w_upt, [0, DFF], h1, g2, dh2, True,
        comm=[_t_sibling(pieces(gubf)), _t_swap(fin_down), _t_allgather(p_all)])
    pbf_up, own_up = _chip_sum("chip_sum_w_up", pieces(gubf), sib_up, core, chip)
    gobf, = _wgrad("wgrad_out", [y], dh1b, True)
    dy, sib_out = _out_bwd(dh1b, w_out_f, comm=[_t_sibling(pieces(gobf))])
    pbf_out, own_out = _chip_sum("chip_sum_w_out", pieces(gobf), sib_out, core, chip)
    dproj, dcw8, dqg, dkg, dgco, dgao, dsink, dbias, chips_up = _mix_bwd(
        proj, dy, sinks, cw8, qg, kg, gco, gao, bias, comm=[_t_chips(pbf_up)])
    fin_up, = _final_sum("final_sum_w_up", own_up, chips_up, core)
    pm_all = _pack_mix(me, dg2, dgco, dgao, dcw8, dqg, dkg, dsink)
    gibf, chips_out, fin_up, pm_all = _wgrad(
        "wgrad_in", [dproj], u1, False, comm=[_t_chips(pbf_out), _t_swap(fin_up), _t_allgather(pm_all)])
    fin_out, sib_in = _final_sum("final_sum_w_out", own_out, chips_out, core, comm=[_t_sibling(pieces(gibf))])
    pbf_in, own_in = _chip_sum("chip_sum_w_in", pieces(gibf), sib_in, core, chip)
    dx, g1_all, tbl_all, g_w_up, d_up, nm_up, nv_up, chips_in, fin_out = _norm_matmul_bwd(
        "in_bwd", [dproj], w_int, [0], xs, g1, dh1, False, comm=[_t_chips(pbf_in), _t_swap(fin_out)], slot=me,
        band=(dbias, bucket), adam=(w_up[0], whole(fin_up), m_w_up[0], v_w_up[0]))
    fin_in, = _final_sum("final_sum_w_in", own_in, chips_in, core)
    g1_all, tbl_all, fin_in = _comm_call("gather_last", [_t_allgather(g1_all), _t_allgather(tbl_all), _t_swap(fin_in)])

    g_w_out, g_w_down = whole(fin_out), whole(fin_down)
    g_w_down, d_down, nm_down, nv_down = _adamw("adamw_w_down", w_down[0], g_w_down, m_w_down[0], v_w_down[0], 352, True)
    g_w_out, d_out, nm_out, nv_out = _adamw("adamw_w_out", w_out[0], g_w_out, m_w_out[0], v_w_out[0], 256, True, stage=False)
    g_w_in, d_in, nm_in, nv_in = [a.T for a in _adamw(
        "adamw_w_in", w_in[0].T, whole(fin_in), m_w_in[0].T, v_w_in[0].T, INW // N_CHIPS // 3, True, stage=False)]
    taps = lambda a: jnp.transpose(a, (1, 0, 2))
    sw = [norm_mix_g, taps(conv_w), q_norm_g, k_norm_g, rel_bias_table.T, sinks, out_norm_conv_g, out_norm_attn_g,
          norm_ffn_g, taps(ffn_conv_w), ffn_conv_b]
    smm = [m_norm_mix_g, taps(m_conv_w), m_q_norm_g, m_k_norm_g, m_rel_bias_table.T, m_sinks, m_out_norm_conv_g,
           m_out_norm_attn_g, m_norm_ffn_g, taps(m_ffn_conv_w), m_ffn_conv_b]
    smv = [v_norm_mix_g, taps(v_conv_w), v_q_norm_g, v_k_norm_g, v_rel_bias_table.T, v_sinks, v_out_norm_conv_g,
           v_out_norm_attn_g, v_norm_ffn_g, taps(v_ffn_conv_w), v_ffn_conv_b]
    *small_out, loss = _small_adam(chip, p_all, pm_all, g1_all, tbl_all, sw, smm, smv)
    sg, sd, snm, snv = [list(r) for r in small_out]
    for r in (sg, sd, snm, snv):
        r[1], r[4], r[9] = taps(r[1]), r[4].T, taps(r[9])

    def order(s, b_in, b_out, b_up, b_down):
        return (s[0], b_in[None], s[1], s[2], s[3], s[4], s[5], s[6], s[7], b_out[None], s[8], b_up[None],
                s[9], s[10], b_down[None])

    return (loss.reshape(()), dx[None],
            *order(sg, g_w_in, g_w_out, g_w_up, g_w_down),
            *order(sd, d_in, d_out, d_up, d_down),
            *order(snm, nm_in, nm_out, nm_up, nm_down),
            *order(snv, nv_in, nv_out, nv_up, nv_down))
```

```python
import functools
import math

import numpy as np

import jax
import jax.numpy as jnp
from jax import lax
from jax.experimental import pallas as pl
from jax.experimental.pallas import tpu as pltpu

F32 = jnp.float32
BF = jnp.bfloat16
SDS = jax.ShapeDtypeStruct

T = 2048
D = 1024
CW = 512
AW = 512
HD = 64
NH = 8
NKV = 2
GQ = 4
INW = 2304
DFF = 2816
BLK = 128
NB = T // BLK
NBUCKET = 32
EPS = 1e-6
NEG_INF = -1e30
N_CHIPS = 4
N_DEV = 8

ADAM_LR = 0.001
ADAM_B1 = 0.9
ADAM_B2 = 0.999
ADAM_EPS = 1e-08
ADAM_WD = 0.01
ADAM_STEP = 10

TM = 512
MIB = 1024 * 1024
MESH = pl.DeviceIdType.MESH
ANY = pl.BlockSpec(memory_space=pl.ANY)

_pcall = pl.pallas_call


def _params(sem=None, vmem_mib=None, collective_id=None):
    kw = {} if collective_id is None else {"collective_id": collective_id}
    if sem is not None:
        kw["dimension_semantics"] = sem
    if vmem_mib is not None:
        kw["vmem_limit_bytes"] = vmem_mib * MIB
    return pltpu.CompilerParams(**kw)


def _resident(shape):
    return pl.BlockSpec(shape, lambda *_: (0,) * len(shape), pipeline_mode=pl.Buffered(1))


def _dot(a, b, ca, cb):
    return lax.dot_general(a, b, (((ca,), (cb,)), ((), ())), preferred_element_type=F32)


def _rms_bwd(dy, x, r, g):
    dg = jnp.sum(dy * (x * r), axis=0, keepdims=True)
    dgx = dy * g
    dx = r * dgx - x * (r * r * r) * jnp.mean(x * dgx, axis=-1, keepdims=True)
    return dx, dg


def _where():
    x, y, c = lax.axis_index("x"), lax.axis_index("y"), lax.axis_index("c")
    return x, y, c, [(1 - x, y), (x, 1 - y), (1 - x, 1 - y)]


def _rcopy(src, dst, ssem, rsem, dev):
    return pltpu.make_async_remote_copy(src_ref=src, dst_ref=dst, send_sem=ssem, recv_sem=rsem, device_id=dev,
                                        device_id_type=MESH)


SIBLING, Y_CHIP, X_CHIP, DIAGONAL_CHIP = 1, 2, 4, 6
OTHER_CHIPS = (Y_CHIP, X_CHIP, DIAGONAL_CHIP)
EVERYONE = tuple(range(1, N_DEV))
BARRIER_OF = {(SIBLING,): 0, (SIBLING, Y_CHIP, X_CHIP): 1, OTHER_CHIPS: 2, (SIBLING,) + OTHER_CHIPS: 3, EVERYONE: 4}


def _peer(rel):
    x, y, c, _ = _where()
    return x ^ ((rel >> 2) & 1), y ^ ((rel >> 1) & 1), c ^ (rel & 1)


class _Task:
    def __init__(self, ins, outs, alias, n_sem, start, finish, middle=None, peers=()):
        self.ins, self.outs, self.alias, self.n_sem, self.start, self.finish = ins, outs, alias, n_sem, start, finish
        self.middle = middle if middle is not None else (lambda *args: None)
        self.peers = peers


def _peers_of(comm):
    return tuple(sorted({p for t in comm for p in t.peers}))


def _enter(comm):
    peers = _peers_of(comm)
    barrier = pltpu.get_barrier_semaphore()
    for rel in peers:
        pl.semaphore_signal(barrier, inc=1, device_id=_peer(rel), device_id_type=MESH)
    pl.semaphore_wait(barrier, len(peers))


ROWS16 = 16


def _t_gather(placed, part=(0, 1), relayed_first=False):
    R = placed.shape[0] // N_CHIPS
    q = R // 4
    lo, hi = (round(f * (q // ROWS16)) * ROWS16 for f in part)

    def quarter(chip_index, core, k):
        return pl.ds(pl.multiple_of(chip_index * R + core * 2 * q + k * q + lo, ROWS16), hi - lo)

    def places():
        x, y, c, _ = _where()
        return c, 2 * x + y, 2 * (1 - x) + y, 2 * x + (1 - y), 2 * (1 - x) + (1 - y), (1 - x, y, c), (x, 1 - y, c), (x, y, 1 - c)

    def copy(buf, k, chip_index, core, quart, ss, rs, b, dev):
        window = buf.at[quarter(chip_index, core, quart)]
        return _rcopy(window, window, ss.at[b + k], rs.at[b + k], dev)

    def first_hop(cout, ss, rs, b, which):
        c, me, _, _, _, x_nbr, y_nbr, _ = places()
        for k, (quart, dev) in enumerate(((0, x_nbr), (1, y_nbr), (1, x_nbr), (0, y_nbr))):
            if k in which:
                copy(cout[0], k, me, c, quart, ss, rs, b, dev).start()

    def start(cin, cout, ss, rs, b):
        first_hop(cout, ss, rs, b, (0, 1) if relayed_first else (0, 1, 2, 3))

    def middle(cin, cout, ss, rs, b):
        c, _, xc, yc, _, x_nbr, y_nbr, sib = places()
        for k, chip_index, quart, dev in ((0, xc, 0, y_nbr), (1, yc, 1, x_nbr)):
            copy(cout[0], k, chip_index, c, quart, ss, rs, b, dev).wait_recv()
            copy(cout[0], 4 + k, chip_index, c, quart, ss, rs, b, dev).start()
            copy(cout[0], 6 + k, chip_index, c, quart, ss, rs, b, sib).start()
        if relayed_first:
            first_hop(cout, ss, rs, b, (2, 3))

    later = ((2, 1, 1), (3, 2, 0), (4, 3, 0), (5, 3, 1))

    def finish(cin, cout, ss, rs, b):
        c, me, xc, yc, dc, _, _, sib = places()
        chip_of = {1: xc, 2: yc, 3: dc}
        for k, whose, quart in later:
            copy(cout[0], k, chip_of[whose], c, quart, ss, rs, b, sib).wait_recv()
            copy(cout[0], 6 + k, chip_of[whose], c, quart, ss, rs, b, sib).start()
        for k, whose, quart in ((0, 1, 0), (1, 2, 1)) + later:
            copy(cout[0], 6 + k, chip_of[whose], 1 - c, quart, ss, rs, b, sib).wait_recv()
        for k in range(12):
            copy(cout[0], k, me, c, 0, ss, rs, b, sib).wait_send()

    return _Task([placed], [SDS(placed.shape, placed.dtype)], [(0, 0)], 12, start, finish, middle, peers=(SIBLING, Y_CHIP, X_CHIP))


def _t_small_weights(buf):
    def start(cin, cout, ss, rs, b):
        x, y, c, chips = _where()
        mine = cout[0].at[2 * x + y]
        for r, (px, py) in enumerate(chips):
            _rcopy(mine, mine, ss.at[b + r], rs.at[b + r], (px, py, c)).start()

    def finish(cin, cout, ss, rs, b):
        x, y, c, chips = _where()
        for r, (px, py) in enumerate(chips):
            got = cout[0].at[2 * px + py]
            _rcopy(got, got, ss.at[b + r], rs.at[b + r], (px, py, c)).wait_recv()
        for r, (px, py) in enumerate(chips):
            mine = cout[0].at[2 * x + y]
            _rcopy(mine, mine, ss.at[b + r], rs.at[b + r], (px, py, c)).wait_send()

    return _Task([buf], [SDS(buf.shape, buf.dtype)], [(0, 0)], 3, start, finish, peers=OTHER_CHIPS)


def _t_sibling(gbf):
    def start(cin, cout, ss, rs, b):
        x, y, c, _ = _where()
        for jj in range(N_CHIPS):
            _rcopy(cin[0].at[2 * jj + (1 - c)], cout[0].at[jj], ss.at[b + jj], rs.at[b + jj], (x, y, 1 - c)).start()

    def finish(cin, cout, ss, rs, b):
        x, y, c, _ = _where()
        for jj in range(N_CHIPS):
            got = cout[0].at[jj]
            _rcopy(got, got, ss.at[b + jj], rs.at[b + jj], (x, y, 1 - c)).wait_recv()
        for jj in range(N_CHIPS):
            got = cout[0].at[jj]
            _rcopy(got, got, ss.at[b + jj], rs.at[b + jj], (x, y, 1 - c)).wait_send()

    return _Task([gbf], [SDS((N_CHIPS,) + gbf.shape[1:], BF)], [], N_CHIPS, start, finish, peers=(SIBLING,))


def _t_chips(pbf):
    def start(cin, cout, ss, rs, b):
        x, y, c, chips = _where()
        for r, (px, py) in enumerate(chips):
            _rcopy(cin[0].at[2 * px + py], cout[0].at[r], ss.at[b + r], rs.at[b + r], (px, py, c)).start()

    def finish(cin, cout, ss, rs, b):
        x, y, c, chips = _where()
        for r, (px, py) in enumerate(chips):
            got = cout[0].at[r]
            _rcopy(got, got, ss.at[b + r], rs.at[b + r], (px, py, c)).wait_recv()
        for r, (px, py) in enumerate(chips):
            got = cout[0].at[r]
            _rcopy(got, got, ss.at[b + r], rs.at[b + r], (px, py, c)).wait_send()

    return _Task([pbf], [SDS((3,) + pbf.shape[1:], BF)], [], 3, start, finish, peers=OTHER_CHIPS)


def _t_swap(fin):
    def start(cin, cout, ss, rs, b):
        x, y, c, _ = _where()
        mine = cout[0].at[c]
        _rcopy(mine, mine, ss.at[b], rs.at[b], (x, y, 1 - c)).start()

    def finish(cin, cout, ss, rs, b):
        x, y, c, _ = _where()
        got = cout[0].at[1 - c]
        _rcopy(got, got, ss.at[b], rs.at[b], (x, y, 1 - c)).wait_recv()
        _rcopy(got, got, ss.at[b], rs.at[b], (x, y, 1 - c)).wait_send()

    return _Task([fin], [SDS(fin.shape, fin.dtype)], [(0, 0)], 1, start, finish, peers=(SIBLING,))


def _t_allgather(buf):
    def peers():
        x, y, c, _ = _where()
        out = []
        for rel in range(1, N_DEV):
            px, py, pc = x ^ ((rel >> 2) & 1), y ^ ((rel >> 1) & 1), c ^ (rel & 1)
            out.append((rel - 1, 4 * px + 2 * py + pc, (px, py, pc)))
        return 4 * x + 2 * y + c, out

    def start(cin, cout, ss, rs, b):
        me, ps = peers()
        mine = cout[0].at[me]
        for k, _, dev in ps:
            _rcopy(mine, mine, ss.at[b + k], rs.at[b + k], dev).start()

    def finish(cin, cout, ss, rs, b):
        me, ps = peers()
        for k, pidx, dev in ps:
            got = cout[0].at[pidx]
            _rcopy(got, got, ss.at[b + k], rs.at[b + k], dev).wait_recv()
        for k, _, dev in ps:
            mine = cout[0].at[me]
            _rcopy(mine, mine, ss.at[b + k], rs.at[b + k], dev).wait_send()

    return _Task([buf], [SDS(buf.shape, buf.dtype)], [(0, 0)], N_DEV - 1, start, finish, peers=EVERYONE)


def _run_tasks(comm, which, cin, cout, ss, rs):
    i0 = o0 = s0 = 0
    for t in comm:
        getattr(t, which)(cin[i0:i0 + len(t.ins)], cout[o0:o0 + len(t.outs)], ss, rs, s0)
        i0, o0, s0 = i0 + len(t.ins), o0 + len(t.outs), s0 + t.n_sem


def _from_hbm(*arrays):
    return [pltpu.with_memory_space_constraint(a, pltpu.HBM) for a in arrays]


def _in_hbm(shapes):
    return [pltpu.HBM(s.shape, s.dtype) for s in shapes]


def _comm_layout(comm, n_in, n_out):
    c_in = [a for t in comm for a in t.ins]
    c_out = [s for t in comm for s in t.outs]
    aliases, i0, o0 = {}, 0, 0
    for t in comm:
        for i, o in t.alias:
            aliases[n_in + i0 + i] = n_out + o0 + o
        i0, o0 = i0 + len(t.ins), o0 + len(t.outs)
    return c_in, c_out, aliases, sum(t.n_sem for t in comm)


def _call(body, operands, *, name, grid, in_specs, out_specs, out_shape, scratch_shapes=(), sem=None, vmem_mib=None, comm=(),
          free=(), prefetch=()):
    operands = [o if s.memory_space == pltpu.SMEM or k in free else pltpu.with_memory_space_constraint(o, pltpu.HBM)
                for k, (o, s) in enumerate(zip(operands, in_specs))]
    n_pre, n_in, n_out, n_scr = len(prefetch), len(in_specs), len(out_specs), len(scratch_shapes)
    c_in, c_out, aliases, n_sem = _comm_layout(comm, n_pre + n_in, n_out)
    sems = [pltpu.SemaphoreType.DMA((n_sem,)), pltpu.SemaphoreType.DMA((n_sem,))] if comm else []

    def wrapped(*refs):
        pre, refs = refs[:n_pre], refs[n_pre:]
        ins, cin = refs[:n_in], refs[n_in:n_in + len(c_in)]
        rest = refs[n_in + len(c_in):]
        outs, cout = rest[:n_out], rest[n_out:n_out + len(c_out)]
        rest = rest[n_out + len(c_out):]
        scr, csem = rest[:n_scr], rest[n_scr:]
        if not comm:
            return body(*pre, *ins, *outs, *scr)
        step = functools.reduce(lambda acc, k: acc * grid[k] + pl.program_id(k), range(len(grid)), 0)
        n_steps = math.prod(grid)

        @pl.when(step == 0)
        def _():
            _enter(comm)
            _run_tasks(comm, "start", cin, cout, *csem)

        pl.when(step == n_steps // 2)(lambda: _run_tasks(comm, "middle", cin, cout, *csem))
        body(*pre, *ins, *outs, *scr)
        pl.when(step == n_steps - 1)(lambda: _run_tasks(comm, "finish", cin, cout, *csem))

    grid_spec = pltpu.PrefetchScalarGridSpec(
        num_scalar_prefetch=n_pre, grid=grid, in_specs=list(in_specs) + [ANY] * len(c_in),
        out_specs=list(out_specs) + [ANY] * len(c_out), scratch_shapes=list(scratch_shapes) + sems)
    return _pcall(
        wrapped, name=name, grid_spec=grid_spec, out_shape=_in_hbm(list(out_shape) + c_out), input_output_aliases=aliases,
        compiler_params=_params(("arbitrary",) * len(grid) if comm else sem, vmem_mib,
                                BARRIER_OF[_peers_of(comm)] if comm else None),
    )(*prefetch, *operands, *_from_hbm(*c_in))


def _comm_call(name, comm):
    c_in, c_out, aliases, n_sem = _comm_layout(comm, 0, 0)

    def body(*refs):
        cin, cout, (ss, rs) = refs[:len(c_in)], refs[len(c_in):len(c_in) + len(c_out)], refs[len(c_in) + len(c_out):]
        _enter(comm)
        for phase in ("start", "middle", "finish"):
            _run_tasks(comm, phase, cin, cout, ss, rs)

    return _pcall(
        body, name=name, in_specs=[ANY] * len(c_in), out_specs=[ANY] * len(c_out), out_shape=_in_hbm(c_out),
        scratch_shapes=[pltpu.SemaphoreType.DMA((n_sem,)), pltpu.SemaphoreType.DMA((n_sem,))],
        input_output_aliases=aliases, compiler_params=_params(collective_id=BARRIER_OF[_peers_of(comm)]),
    )(*_from_hbm(*c_in))


def _inproj(x, g1, w_int, comm=()):
    tm = TM

    def body(x_ref, g_ref, w_ref, proj_ref, u_ref):
        xf = x_ref[...]
        r = lax.rsqrt(jnp.mean(xf * xf, axis=-1, keepdims=True) + EPS)
        u = (xf * r * g_ref[...]).astype(BF)
        u_ref[...] = u
        proj_ref[...] = _dot(u, w_ref[...], 1, 1)

    return _call(
        body, (x, g1, w_int), name="inproj", grid=(T // tm,),
        in_specs=[pl.BlockSpec((tm, D), lambda i: (i, 0)), pl.BlockSpec((1, D), lambda i: (0, 0)),
                  _resident((INW, D))],
        out_specs=[pl.BlockSpec((tm, INW), lambda i: (i, 0)), pl.BlockSpec((tm, D), lambda i: (i, 0))],
        out_shape=[SDS((T, INW), F32), SDS((T, D), BF)], sem=("parallel",), vmem_mib=40, comm=comm, free=(0, 1))


def _outproj(y, w_out, x, g2):
    tm = TM

    def body(y_ref, w_ref, x_ref, g_ref, h1_ref, u2_ref):
        h1 = x_ref[...] + _dot(y_ref[...], w_ref[...], 1, 0)
        h1_ref[...] = h1
        r = lax.rsqrt(jnp.mean(h1 * h1, axis=-1, keepdims=True) + EPS)
        u2_ref[...] = (h1 * r * g_ref[...]).astype(BF)

    return _call(
        body, (y, w_out, x, g2), name="outproj", grid=(T // tm,),
        in_specs=[pl.BlockSpec((tm, D), lambda i: (i, 0)), _resident((D, D)),
                  pl.BlockSpec((tm, D), lambda i: (i, 0)), pl.BlockSpec((1, D), lambda i: (0, 0))],
        out_specs=[pl.BlockSpec((tm, D), lambda i: (i, 0)), pl.BlockSpec((tm, D), lambda i: (i, 0))],
        out_shape=[SDS((T, D), F32), SDS((T, D), BF)], sem=("parallel",), vmem_mib=32, free=(2, 3))


def _ffn_up(u2, w_upt, comm=()):
    tm, tn = T, 512

    def body(u_ref, w_ref, o_ref):
        o_ref[...] = _dot(u_ref[...], w_ref[...], 1, 1).astype(BF)

    return _call(
        body, (u2, w_upt), name="ffn_up", grid=(T // tm, 2 * DFF // tn),
        in_specs=[pl.BlockSpec((tm, D), lambda i, j: (i, 0)), pl.BlockSpec((tn, D), lambda i, j: (j, 0))],
        out_specs=[pl.BlockSpec((tm, tn), lambda i, j: (i, j))], out_shape=[SDS((T, 2 * DFF), BF)],
        sem=("parallel", "parallel"), vmem_mib=32, comm=comm, free=(1,))


def _ffn_down(a, w_down, h1, tgt):
    tm = TM

    def body(a_ref, w_ref, h1_ref, t_ref, dh_ref, dhb_ref, l_ref):
        @pl.when(pl.program_id(0) == 0)
        def _():
            l_ref[...] = jnp.zeros_like(l_ref)

        h2 = h1_ref[...] + _dot(a_ref[...], w_ref[...], 1, 0)
        e = h2 - t_ref[...]
        dh = e * (1.0 / D)
        dh_ref[...] = dh
        dhb_ref[...] = dh.astype(BF)
        e2 = jnp.sum((e * e).reshape(tm // 8, 8, D), axis=0)
        acc = e2[:, 0:128]
        for k in range(1, D // 128):
            acc = acc + e2[:, k * 128:(k + 1) * 128]
        l_ref[...] += acc

    return _call(
        body, (a, w_down, h1, tgt), name="ffn_down", grid=(T // tm,),
        in_specs=[pl.BlockSpec((tm, DFF), lambda i: (i, 0)), _resident((DFF, D)),
                  pl.BlockSpec((tm, D), lambda i: (i, 0)), pl.BlockSpec((tm, D), lambda i: (i, 0))],
        out_specs=[pl.BlockSpec((tm, D), lambda i: (i, 0)), pl.BlockSpec((tm, D), lambda i: (i, 0)),
                   pl.BlockSpec((8, 128), lambda i: (0, 0))],
        out_shape=[SDS((T, D), F32), SDS((T, D), BF), SDS((8, 128), F32)], sem=("arbitrary",), vmem_mib=40, free=(2, 3))


def _bucket_table():
    q = np.arange(BLK, dtype=np.int32)[:, None]
    j = np.arange(2 * BLK, dtype=np.int32)[None, :]
    n = np.maximum(q + BLK - j, 0)
    nf = np.maximum(n, 1).astype(np.float32)
    max_exact = NBUCKET // 2
    large = max_exact + (np.log(nf / np.float32(max_exact)) / np.float32(math.log(BLK / max_exact))
                         * np.float32(NBUCKET - max_exact)).astype(np.int32)
    large = np.minimum(large, NBUCKET - 1)
    return np.where(n < max_exact, n, large).astype(np.int32)


def _two_bf16(x):
    hi = x.astype(BF)
    return hi, (x - hi.astype(F32)).astype(BF)


def _head_sums(x, seg):
    hi, lo = _two_bf16(x)
    s = seg[0:x.shape[1], :]
    return _dot(hi, s, 1, 0) + _dot(lo, s, 1, 0)


def _head_spread(v, seg, width):
    hi, lo = _two_bf16(v)
    s = seg[0:width, :]
    return _dot(hi, s, 1, 1) + _dot(lo, s, 1, 1)


def _head_norm(x, g_t, seg, by_head=False):
    if by_head:
        heads = [x[:, h * HD:(h + 1) * HD] for h in range(x.shape[1] // HD)]
        r = jnp.concatenate([jnp.broadcast_to(lax.rsqrt(jnp.mean(v * v, axis=-1, keepdims=True) + EPS), v.shape)
                             for v in heads], axis=1)
    else:
        r = lax.rsqrt(_head_sums(x * x, seg) * (1.0 / HD) + EPS)
        r = _head_spread(r, seg, x.shape[1])
    return x * r * g_t, r


def _head_norm_bwd(dy, x, r, g_t, seg):
    dg_t = jnp.sum(dy * (x * r), axis=0, keepdims=True)
    dgx = dy * g_t
    mean = _head_spread(_head_sums(x * dgx, seg) * (1.0 / HD), seg, x.shape[1])
    return r * dgx - x * (r * r * r) * mean, dg_t


def _fold_heads(v):
    out = v[:, 0:HD]
    for h in range(1, v.shape[1] // HD):
        out = out + v[:, h * HD:(h + 1) * HD]
    return out


def _mix_forward(P, zc8, zh8, pkv, first, cw, qg_t, kg_t, gco, gao, seg, sink_ref, bias_ref, by_head=False):
    gate_b = P[:, 0:CW]
    gate_c = P[:, CW:2 * CW]
    hc = P[:, 2 * CW:3 * CW]
    z = gate_c * hc
    keep = jnp.where(first, 0.0, 1.0)
    zp = zc8 * zh8 * keep
    p1 = zp[7:8, :]
    p2 = zp[6:7, :]
    row = lax.broadcasted_iota(jnp.int32, (BLK, 1), 0)
    z1 = jnp.where(row == 0, p1, pltpu.roll(z, 1, 0))
    z2 = jnp.where(row == 0, p2, jnp.where(row == 1, p1, pltpu.roll(z, 2, 0)))
    cz = cw[0:1, :] * z2 + cw[1:2, :] * z1 + cw[2:3, :] * z
    y_conv = gate_b * cz

    scale = HD ** -0.5
    qi = lax.broadcasted_iota(jnp.int32, (BLK, 2 * BLK), 0)
    kj = lax.broadcasted_iota(jnp.int32, (BLK, 2 * BLK), 1)
    dd = qi + BLK - kj
    first_key = jnp.where(first, BLK, 0)
    valid = (dd >= 0) & (dd < BLK) & (kj >= first_key)

    q0 = 3 * CW
    k0 = q0 + AW
    v0 = k0 + NKV * HD
    q_raw = P[:, q0:k0]
    qn, rq = _head_norm(q_raw, qg_t, seg, by_head)
    qs = (qn * scale).astype(BF)
    k_raw = jnp.concatenate([pkv[:, 0:NKV * HD], P[:, k0:v0]], axis=0)
    kn, rk = _head_norm(k_raw, kg_t, seg, by_head)
    knb = kn.astype(BF)
    heads = []
    for h in range(NH):
        kv = h // GQ
        kb = knb[:, kv * HD:(kv + 1) * HD]
        vb = jnp.concatenate([pkv[:, NKV * HD + kv * HD:NKV * HD + (kv + 1) * HD],
                              P[:, v0 + kv * HD:v0 + (kv + 1) * HD]], axis=0).astype(BF)
        Q = qs[:, h * HD:(h + 1) * HD]
        S = _dot(Q, kb, 1, 1) + bias_ref[h * BLK:(h + 1) * BLK, :]
        S = jnp.where(valid, S, NEG_INF)
        sink = sink_ref[0, h]
        m = jnp.maximum(jnp.max(S, axis=-1, keepdims=True), sink)
        p = jnp.exp(S - m)
        es = jnp.exp(sink - m)
        denom = jnp.sum(p, axis=-1, keepdims=True) + es
        probs = p / denom
        O = _dot(probs.astype(BF), vb, 1, 0)
        heads.append(dict(kb=kb, vb=vb, Q=Q, probs=probs, psink=es / denom, O=O))
    y_attn = jnp.concatenate([hd["O"] for hd in heads], axis=1)

    rc = lax.rsqrt(jnp.mean(y_conv * y_conv, axis=-1, keepdims=True) + EPS)
    ra = lax.rsqrt(jnp.mean(y_attn * y_attn, axis=-1, keepdims=True) + EPS)
    y = jnp.concatenate([y_conv * rc * gco, y_attn * ra * gao], axis=1)
    return dict(gate_b=gate_b, gate_c=gate_c, hc=hc, z=z, z1=z1, z2=z2, cz=cz, y_conv=y_conv, y_attn=y_attn,
                rc=rc, ra=ra, heads=heads, y=y, row=row, scale=scale, q_raw=q_raw, rq=rq, k_raw=k_raw, rk=rk)


BPS = 2
TILE = BPS * BLK
KV0 = 3 * CW + AW


def _mix_in_specs(tile_of):
    return [
        pl.BlockSpec(memory_space=pltpu.SMEM),
        pl.BlockSpec((TILE, INW), lambda s: (tile_of(s), 0)),
        pl.BlockSpec((8, CW), lambda s: (jnp.maximum(tile_of(s) * (TILE // 8) - 1, 0), 1)),
        pl.BlockSpec((8, CW), lambda s: (jnp.maximum(tile_of(s) * (TILE // 8) - 1, 0), 2)),
        pl.BlockSpec((BLK, 2 * NKV * HD), lambda s: (jnp.maximum(tile_of(s) * BPS - 1, 0), KV0 // (2 * NKV * HD))),
    ]


def _block_inputs(tile, b, zc_ref, zh_ref, pkv_ref, first_tile):
    P = tile[b * BLK:(b + 1) * BLK, :]
    if b == 0:
        return P, zc_ref[...], zh_ref[...], pkv_ref[...], first_tile
    lo = b * BLK
    return P, tile[lo - 8:lo, CW:2 * CW], tile[lo - 8:lo, 2 * CW:3 * CW], tile[lo - BLK:lo, KV0:KV0 + 2 * NKV * HD], False


def _mix_param_specs():
    return [
        pl.BlockSpec((8, CW), lambda s: (0, 0)),
        pl.BlockSpec((1, AW), lambda s: (0, 0)),
        pl.BlockSpec((1, NKV * HD), lambda s: (0, 0)),
        pl.BlockSpec((1, CW), lambda s: (0, 0)),
        pl.BlockSpec((1, AW), lambda s: (0, 0)),
        pl.BlockSpec((AW, 128), lambda s: (0, 0)),
        pl.BlockSpec((NH * BLK, 2 * BLK), lambda s: (0, 0)),
    ]


def _mix_params(cw8, qg, kg, gco, gao, bias):
    seg = np.zeros((AW, 128), np.float32)
    seg[np.arange(AW), np.arange(AW) // HD] = 1.0
    return (cw8, jnp.tile(qg, (1, NH)), jnp.tile(kg, (1, NKV)), gco, gao, jnp.asarray(seg, BF), bias)


def _mix_fwd(proj, sinks, cw8, qg, kg, gco, gao, bias, comm=()):
    def body(sink_ref, p_ref, zc_ref, zh_ref, pkv_ref, cw_ref, qg_ref, kg_ref, gco_ref, gao_ref, seg_ref, bias_ref, y_ref):
        tile = p_ref[...]
        for b in range(BPS):
            f = _mix_forward(*_block_inputs(tile, b, zc_ref, zh_ref, pkv_ref, pl.program_id(0) == 0), cw_ref[...],
                             qg_ref[...], kg_ref[...], gco_ref[...], gao_ref[...], seg_ref[...], sink_ref, bias_ref, by_head=True)
            y_ref[b * BLK:(b + 1) * BLK, :] = f["y"].astype(BF)

    return _call(
        body, (sinks, proj, proj, proj, proj, *_mix_params(cw8, qg, kg, gco, gao, bias)), name="mix_fwd", grid=(T // TILE,),
        in_specs=_mix_in_specs(lambda s: s) + _mix_param_specs(),
        out_specs=[pl.BlockSpec((TILE, D), lambda s: (s, 0))], out_shape=[SDS((T, D), BF)],
        sem=("parallel",), vmem_mib=40, comm=comm, free=tuple(range(5, 12)))


def _mix_bwd(proj, dy, sinks, cw8, qg, kg, gco, gao, bias, comm=()):
    n_steps = T // TILE

    def tile_of(s):
        return n_steps - 1 - s

    def body(sink_ref, p_ref, zc_ref, zh_ref, pkv_ref, dy_ref, cw_ref, qg_ref, kg_ref, gco_ref, gao_ref, seg_ref, bias_ref,
             dproj_ref, dcw_ref, dqg_ref, dkg_ref, dgco_ref, dgao_ref, dsink_ref, dbias_ref,
             ndcz_ref, dkc_ref, dvc_ref):
        s = pl.program_id(0)

        @pl.when(s == 0)
        def _():
            for r in (dcw_ref, dqg_ref, dkg_ref, dgco_ref, dgao_ref, dsink_ref, dbias_ref, ndcz_ref, dkc_ref, dvc_ref):
                r[...] = jnp.zeros_like(r)

        params = (cw_ref[...], qg_ref[...], kg_ref[...], gco_ref[...], gao_ref[...], seg_ref[...])
        tile = p_ref[...]
        carry = (ndcz_ref[...], dkc_ref[...], dvc_ref[...])
        total = None
        for b in reversed(range(BPS)):
            f = _mix_forward(*_block_inputs(tile, b, zc_ref, zh_ref, pkv_ref, s == n_steps - 1), *params, sink_ref, bias_ref)
            pieces, sums, carry = one_block(f, dy_ref[b * BLK:(b + 1) * BLK, :], params, carry)
            for lo, piece in pieces:
                dproj_ref[b * BLK:(b + 1) * BLK, lo:lo + piece.shape[1]] = piece
            total = sums if total is None else [t + v for t, v in zip(total, sums)]
        ndcz_ref[...], dkc_ref[...], dvc_ref[...] = carry
        dcw, dqg_t, dkg_t, dgco, dgao, dsink, *ds = total
        dcw_ref[0:3, :] += dcw
        dqg_ref[...] += _fold_heads(dqg_t)
        dkg_ref[...] += _fold_heads(dkg_t)
        dgco_ref[...] += dgco
        dgao_ref[...] += dgao
        dsink_ref[...] += dsink
        for h in range(NH):
            dbias_ref[h * BLK:(h + 1) * BLK, :] += ds[h]

    def one_block(f, dy, params, carry):
        cw, qg_v, kg_v, gco_v, gao_v, seg = params
        nxt, dk_carry, dv_carry = carry
        dyc, dgco = _rms_bwd(dy[:, 0:CW], f["y_conv"], f["rc"], gco_v)
        dya, dgao = _rms_bwd(dy[:, CW:CW + AW], f["y_attn"], f["ra"], gao_v)

        row = f["row"]
        dgate_b = dyc * f["cz"]
        dcz = dyc * f["gate_b"]
        dcw = jnp.concatenate([jnp.sum(dcz * f[k], axis=0, keepdims=True) for k in ("z2", "z1", "z")], axis=0)
        n0 = nxt[0:1, :]
        n1 = nxt[1:2, :]
        d1 = jnp.where(row == BLK - 1, n0, pltpu.roll(dcz, BLK - 1, 0))
        d2 = jnp.where(row == BLK - 1, n1, jnp.where(row == BLK - 2, n0, pltpu.roll(dcz, BLK - 2, 0)))
        dz = cw[2:3, :] * dcz + cw[1:2, :] * d1 + cw[0:1, :] * d2
        pieces = [(0, dgate_b.astype(BF)), (CW, (dz * f["hc"]).astype(BF)), (2 * CW, (dz * f["gate_c"]).astype(BF))]

        scale = f["scale"]
        lane = lax.broadcasted_iota(jnp.int32, (1, 128), 1)
        dsink = jnp.zeros((1, 128), F32)
        dq_cols, dk_cols, dv_cols, dk_prev, dv_prev, ds = [], [], [], [], [], []
        for kv in range(NKV):
            dKb = dVb = 0.0
            for h in range(kv * GQ, (kv + 1) * GQ):
                hd = f["heads"][h]
                dO = dya[:, h * HD:(h + 1) * HD]
                delta = jnp.sum(dO * hd["O"], axis=-1, keepdims=True)
                dOb = dO.astype(BF)
                dP = _dot(dOb, hd["vb"], 1, 1)
                dS = hd["probs"] * (dP - delta)
                tot = jnp.sum(hd["psink"] * delta, axis=0, keepdims=True)
                dsink = dsink - jnp.where(lane == h, tot, 0.0)
                ds.append(dS)
                dSb = dS.astype(BF)
                dq_cols.append(_dot(dSb, hd["kb"], 1, 0))
                dKb = dKb + _dot(dSb, hd["Q"], 0, 0)
                dVb = dVb + _dot(hd["probs"].astype(BF), dOb, 0, 0)
            dk_cols.append(dKb[BLK:, :] + dk_carry[:, kv * HD:(kv + 1) * HD])
            dv_cols.append(dVb[BLK:, :] + dv_carry[:, kv * HD:(kv + 1) * HD])
            dk_prev.append(dKb[:BLK, :])
            dv_prev.append(dVb[:BLK, :])
        dq_raw, dqg_t = _head_norm_bwd(jnp.concatenate(dq_cols, axis=1) * scale, f["q_raw"], f["rq"], qg_v, seg)
        dk_raw, dkg_t = _head_norm_bwd(jnp.concatenate(dk_cols, axis=1), f["k_raw"][BLK:, :], f["rk"][BLK:, :], kg_v, seg)
        pieces.append((3 * CW, jnp.concatenate([dq_raw, dk_raw] + dv_cols, axis=1).astype(BF)))
        owed = (dcz[0:8, :], jnp.concatenate(dk_prev, axis=1), jnp.concatenate(dv_prev, axis=1))
        return pieces, [dcw, dqg_t, dkg_t, dgco, dgao, dsink, *ds], owed

    small = lambda r, c: pl.BlockSpec((r, c), lambda s: (0, 0))
    return _call(
        body, (sinks, proj, proj, proj, proj, dy, *_mix_params(cw8, qg, kg, gco, gao, bias)), name="mix_bwd", grid=(n_steps,),
        in_specs=_mix_in_specs(tile_of) + [pl.BlockSpec((TILE, D), lambda s: (tile_of(s), 0))] + _mix_param_specs(),
        out_specs=[pl.BlockSpec((TILE, INW), lambda s: (tile_of(s), 0)), small(8, CW), small(1, HD), small(1, HD),
                   small(1, CW), small(1, AW), small(1, 128), small(NH * BLK, 2 * BLK)],
        out_shape=[SDS((T, INW), BF), SDS((8, CW), F32), SDS((1, HD), F32), SDS((1, HD), F32), SDS((1, CW), F32),
                   SDS((1, AW), F32), SDS((1, 128), F32), SDS((NH * BLK, 2 * BLK), F32)],
        scratch_shapes=[pltpu.VMEM((8, CW), F32), pltpu.VMEM((BLK, NKV * HD), F32), pltpu.VMEM((BLK, NKV * HD), F32)],
        sem=("arbitrary",), vmem_mib=56, comm=comm, free=(1, 2, 3, 4) + tuple(range(6, 13)))


FT = 256
NFT = DFF // FT
RC = 1024
NCH = T // RC
LEAD = 16


def _rows8(x):
    return jnp.sum(x.reshape(x.shape[0] // 8, 8, x.shape[1]), axis=0)


def _ffn_act_specs():
    return [
        pl.BlockSpec((T, FT), lambda j: (0, j)), pl.BlockSpec((T, FT), lambda j: (0, NFT + j)),
        pl.BlockSpec((8, FT), lambda j: (0, j)), pl.BlockSpec((8, FT), lambda j: (0, NFT + j)),
        pl.BlockSpec((1, FT), lambda j: (0, j)), pl.BlockSpec((1, FT), lambda j: (0, NFT + j)),
    ]


def _conv_rows(win, w, b, n):
    win = win.astype(F32)
    u = win[LEAD:LEAD + n]
    u1 = pltpu.roll(win, 1, 0)[LEAD:LEAD + n]
    u2 = pltpu.roll(win, 2, 0)[LEAD:LEAD + n]
    return u2, u1, u, w[0:1, :] * u2 + w[1:2, :] * u1 + w[2:3, :] * u + b


def _ffn_act(up, fw8, fb, comm=()):
    def body(ug_ref, uv_ref, wg_ref, wv_ref, bg_ref, bv_ref, a_ref, pg_ref, pv_ref):
        wg, wv, bg, bv = wg_ref[...], wv_ref[...], bg_ref[...], bv_ref[...]

        def chunk(rows, win_g, win_v):
            gp = _conv_rows(win_g, wg, bg, RC)[3]
            vp = _conv_rows(win_v, wv, bv, RC)[3]
            a_ref[rows, :] = (gp * jax.nn.sigmoid(gp) * vp).astype(BF)
            pg_ref[0, rows, :] = gp.astype(BF)
            pv_ref[0, rows, :] = vp.astype(BF)

        zero = jnp.zeros((LEAD, FT), BF)
        chunk(pl.ds(0, RC), jnp.concatenate([zero, ug_ref[0:RC, :]], axis=0), jnp.concatenate([zero, uv_ref[0:RC, :]], axis=0))

        def step(i, carry):
            r0 = pl.multiple_of(i * RC, RC)
            win = pl.ds(r0 - LEAD, RC + LEAD)
            chunk(pl.ds(r0, RC), ug_ref[win, :], uv_ref[win, :])
            return carry

        lax.fori_loop(1, NCH, step, 0)

    tile = pl.BlockSpec((1, T, FT), lambda j: (j, 0, 0))
    return _call(
        body, (up, up, fw8, fw8, fb, fb), name="ffn_act", grid=(NFT,), in_specs=_ffn_act_specs(),
        out_specs=[pl.BlockSpec((T, FT), lambda j: (0, j)), tile, tile],
        out_shape=[SDS((T, DFF), BF), SDS((NFT, T, FT), BF), SDS((NFT, T, FT), BF)],
        sem=("parallel",), vmem_mib=40, comm=comm, free=(2, 3, 4, 5))


def _ffn_act_bwd(up, pre_g, pre_v, da, fw8, comm=()):
    ext = RC + LEAD

    def body(ug_ref, uv_ref, wg_ref, wv_ref, pg_ref, pv_ref, da_ref, dug_ref, duv_ref, dwg_ref, dwv_ref, dbg_ref, dbv_ref):
        wg, wv = wg_ref[...], wv_ref[...]

        def chunk(u_g, u_v, gp, vp, da_e):
            gp, vp, da_e = gp.astype(F32), vp.astype(F32), da_e.astype(F32)
            sig = jax.nn.sigmoid(gp)
            dvp = da_e * (gp * sig)
            dgp = da_e * vp * (sig * (1.0 + gp * (1.0 - sig)))

            def branch(dp, w, u):
                d0, d1, d2 = dp[0:RC], pltpu.roll(dp, ext - 1, 0)[0:RC], pltpu.roll(dp, ext - 2, 0)[0:RC]
                du = (w[2:3, :] * d0 + w[1:2, :] * d1 + w[0:1, :] * d2).astype(BF)
                u = u.astype(F32)
                return du, [_rows8(d0), _rows8(d2 * u), _rows8(d1 * u), _rows8(d0 * u)]

            dug, sums_g = branch(dgp, wg, u_g)
            duv, sums_v = branch(dvp, wv, u_v)
            return dug, duv, sums_g + sums_v

        def step(i, acc):
            r0 = pl.multiple_of(i * RC, RC)
            rows, more = pl.ds(r0, RC), pl.ds(r0, ext)
            dug, duv, part = chunk(ug_ref[rows, :], uv_ref[rows, :], pg_ref[0, more, :], pv_ref[0, more, :], da_ref[more, :])
            dug_ref[rows, :] = dug
            duv_ref[rows, :] = duv
            return [a + p for a, p in zip(acc, part)]

        acc = lax.fori_loop(0, NCH - 1, step, [jnp.zeros((8, FT), F32)] * 8)
        r0 = T - RC
        zero = jnp.zeros((LEAD, FT), BF)
        tail = lambda rows: jnp.concatenate([rows, zero], axis=0)
        dug, duv, part = chunk(ug_ref[r0:T, :], uv_ref[r0:T, :], tail(pg_ref[0, r0:T, :]), tail(pv_ref[0, r0:T, :]),
                               tail(da_ref[r0:T, :]))
        dug_ref[r0:T, :] = dug
        duv_ref[r0:T, :] = duv
        tot = [jnp.sum(a + p, axis=0, keepdims=True) for a, p in zip(acc, part)]
        for k, (dw_ref, db_ref) in enumerate(((dwg_ref, dbg_ref), (dwv_ref, dbv_ref))):
            db_ref[...] = tot[4 * k]
            dw_ref[...] = jnp.zeros_like(dw_ref)
            for r in range(3):
                dw_ref[r:r + 1, :] = tot[4 * k + 1 + r]

    col = lambda r: pl.BlockSpec((r, FT), lambda j: (0, j))
    return _call(
        body, (up, up, fw8, fw8, pre_g, pre_v, da), name="ffn_act_bwd", grid=(NFT,),
        in_specs=_ffn_act_specs()[0:4] + [pl.BlockSpec((1, T, FT), lambda j: (j, 0, 0))] * 2 + [col(T)],
        out_specs=[col(T), col(T), col(8), col(8), col(1), col(1)],
        out_shape=[SDS((T, DFF), BF), SDS((T, DFF), BF), SDS((8, DFF), F32), SDS((8, DFF), F32),
                   SDS((1, DFF), F32), SDS((1, DFF), F32)],
        sem=("parallel",), vmem_mib=40, comm=comm, free=(0, 1, 2, 3))


def _ffn_down_bwd(dh2b, w_down, comm=()):
    tm = TM

    def body(d_ref, w_ref, o_ref):
        o_ref[...] = _dot(d_ref[...], w_ref[...], 1, 1).astype(BF)

    return _call(
        body, (dh2b, w_down), name="ffn_down_bwd", grid=(T // tm,),
        in_specs=[pl.BlockSpec((tm, D), lambda i: (i, 0)), _resident((DFF, D))],
        out_specs=[pl.BlockSpec((tm, DFF), lambda i: (i, 0))], out_shape=[SDS((T, DFF), BF)],
        sem=("parallel",), vmem_mib=40, comm=comm, free=(0, 1))


def _norm_matmul_bwd(name, a_list, w_t, k_offsets, xin, g, dres, want_bf16, comm=(), slot=None, band=(), pack=()):
    tm = TM
    ks = [a.shape[1] for a in a_list]
    n_a = len(a_list)
    n_pre = 0 if slot is None else 1
    n_in = n_a + 4 + len(band) + len(pack)

    def body(*refs):
        refs = refs[n_pre:]
        a_refs = refs[:n_a]
        w_ref, x_ref, g_ref, r_ref = refs[n_a:n_a + 4]
        outs = refs[n_in:]
        dg_out = outs[1 + want_bf16]
        dx_ref, dg_ref = outs[0], (dg_out if slot is None else dg_out.at[0])

        @pl.when(pl.program_id(0) == 0)
        def _():
            dg_ref[...] = jnp.zeros_like(dg_ref)
            if band:
                db_ref, bk_ref, tbl_ref = refs[n_a + 4], refs[n_a + 5], outs[2 + want_bf16]
                bk = bk_ref[...]
                for b in range(NBUCKET):
                    m = bk == b
                    for h in range(NH):
                        v = jnp.where(m, db_ref[h * BLK:(h + 1) * BLK, :], 0.0)
                        tbl_ref[0, h:h + 1, b:b + 1] = jnp.sum(jnp.sum(v, axis=1, keepdims=True), axis=0, keepdims=True)
            if pack:
                pm_ref = outs[2 + want_bf16 + bool(band)]
                pm_ref[...] = jnp.zeros_like(pm_ref)
                _fill_mix(pm_ref, *refs[n_in - len(pack):n_in])

        du = _dot(a_refs[0][...], w_ref[k_offsets[0]:k_offsets[0] + ks[0], :], 1, 0)
        for k in range(1, n_a):
            du = du + _dot(a_refs[k][...], w_ref[k_offsets[k]:k_offsets[k] + ks[k], :], 1, 0)
        x = x_ref[...]
        r = lax.rsqrt(jnp.mean(x * x, axis=-1, keepdims=True) + EPS)
        dx, dg = _rms_bwd(du, x, r, g_ref[...])
        dx = r_ref[...] + dx
        dx_ref[...] = dx
        if want_bf16:
            outs[1][...] = dx.astype(BF)
        dg_ref[...] += dg

    tile = lambda c: pl.BlockSpec((tm, c), lambda i, *_: (i, 0))
    if slot is None:
        dg_spec, dg_shape = pl.BlockSpec((1, D), lambda i: (0, 0)), SDS((1, D), F32)
    else:
        dg_spec, dg_shape = pl.BlockSpec((1, 1, D), lambda i, slot_ref: (slot_ref[0], 0, 0)), SDS((N_DEV, 1, D), F32)
    out_specs = [tile(D)] + ([tile(D)] if want_bf16 else []) + [dg_spec]
    out_shape = [SDS((T, D), F32)] + ([SDS((T, D), BF)] if want_bf16 else []) + [dg_shape]
    if band:
        out_specs.append(pl.BlockSpec((1, NH, NBUCKET), lambda i, slot_ref: (slot_ref[0], 0, 0)))
        out_shape.append(SDS((N_DEV, NH, NBUCKET), F32))
    if pack:
        out_specs.append(pl.BlockSpec((1, 8, D), lambda i, slot_ref: (slot_ref[0], 0, 0)))
        out_shape.append(SDS((N_DEV, 8, D), F32))
    return _call(
        body, (*a_list, w_t, xin, g, dres, *band, *pack), name=name, grid=(T // tm,), prefetch=() if slot is None else (slot,),
        in_specs=[tile(k) for k in ks] + [_resident(w_t.shape), tile(D), pl.BlockSpec((1, D), lambda i, *_: (0, 0)), tile(D)]
        + [pl.BlockSpec(b.shape, lambda i, *_: (0, 0)) for b in (*band, *pack)],
        out_specs=out_specs, out_shape=out_shape, sem=("arbitrary",), vmem_mib=56, comm=comm, free=tuple(range(n_a + 4)))


def _out_bwd(dh1b, w_out, comm=()):
    tm = TM

    def body(d_ref, w_ref, o_ref):
        o_ref[...] = _dot(d_ref[...], w_ref[...], 1, 1)

    return _call(
        body, (dh1b, w_out), name="out_bwd", grid=(T // tm,),
        in_specs=[pl.BlockSpec((tm, D), lambda i: (i, 0)), _resident((D, D))],
        out_specs=[pl.BlockSpec((tm, D), lambda i: (i, 0))], out_shape=[SDS((T, D), F32)],
        sem=("parallel",), vmem_mib=32, comm=comm, free=(0, 1))


def _wgrad(name, a_list, b, old_a, comm=()):
    m_k = a_list[0].shape[1]
    tm = max(t for t in range(128, m_k // 2 + 1, 128) if m_k % t == 0)
    steps = [a.shape[1] // tm for a in a_list]
    starts = [sum(steps[:k]) for k in range(len(a_list))]
    n_a = len(a_list)

    def body(*refs):
        a_refs, b_ref, o_ref = refs[:n_a], refs[n_a], refs[n_a + 1]
        i = pl.program_id(0)
        for k in range(n_a):
            @pl.when((i >= starts[k]) & (i < starts[k] + steps[k]))
            def _(k=k):
                o_ref[...] = _dot(a_refs[k][...], b_ref[...], 0, 0).astype(BF)

    def a_spec(k):
        return pl.BlockSpec((T, tm), lambda i: (0, jnp.clip(i - starts[k], 0, steps[k] - 1)))

    m_total = tm * sum(steps)
    return _call(
        body, (*a_list, b), name=name, grid=(sum(steps),),
        in_specs=[a_spec(k) for k in range(n_a)] + [_resident((T, D))],
        out_specs=[pl.BlockSpec((tm, D), lambda i: (i, 0))], out_shape=[SDS((m_total, D), BF)],
        sem=("parallel",), vmem_mib=40, comm=comm, free=() if old_a is None else tuple(range(n_a)) if old_a else (n_a,))


def _chip_sum(name, gbf, from_sib, core, chip):
    h = gbf.shape[1]
    th = h

    def body(core_ref, chip_ref, g_ref, s_ref, pbf_ref, own_ref):
        p = g_ref[0].astype(F32) + s_ref[0].astype(F32)
        pbf_ref[0] = p.astype(BF)

        @pl.when(pl.program_id(1) == chip_ref[0])
        def _():
            own_ref[...] = p

    grid_spec = pltpu.PrefetchScalarGridSpec(
        num_scalar_prefetch=2, grid=(h // th, N_CHIPS),
        in_specs=[pl.BlockSpec((1, th, D), lambda t, jj, core_ref, chip_ref: (2 * jj + core_ref[0], t, 0)),
                  pl.BlockSpec((1, th, D), lambda t, jj, core_ref, chip_ref: (jj, t, 0))],
        out_specs=[pl.BlockSpec((1, th, D), lambda t, jj, core_ref, chip_ref: (jj, t, 0)),
                   pl.BlockSpec((th, D), lambda t, jj, core_ref, chip_ref: (t, 0))],
    )
    return _pcall(
        body, name=name, grid_spec=grid_spec, out_shape=_in_hbm([SDS((N_CHIPS, h, D), BF), SDS((h, D), F32)]),
        compiler_params=_params(("arbitrary", "arbitrary"), 32),
    )(core, chip, *_from_hbm(gbf, from_sib))


def _final_sum(name, own, from_chips, core, comm=()):
    h = own.shape[0]
    n = 4 if h % (4 * ROWS16) == 0 else 2
    th = h // n

    def body(core_ref, o_ref, r_ref, f_ref):
        f_ref[0] = ((o_ref[...] + r_ref[0].astype(F32)) + r_ref[1].astype(F32)) + r_ref[2].astype(F32)

    return _call(
        body, (own, from_chips), name=name, grid=(n,), prefetch=(core,),
        in_specs=[pl.BlockSpec((th, D), lambda i, core_ref: (i, 0)), pl.BlockSpec((3, th, D), lambda i, core_ref: (0, i, 0))],
        out_specs=[pl.BlockSpec((1, th, D), lambda i, core_ref: (core_ref[0], i, 0))], out_shape=[SDS((2, h, D), F32)],
        sem=("arbitrary",), vmem_mib=40, comm=comm)


def _adam_math(w, g, m, v):
    nm = ADAM_B1 * m + (1.0 - ADAM_B1) * g
    nv = ADAM_B2 * v + (1.0 - ADAM_B2) * (g * g)
    m_hat = nm / (1.0 - ADAM_B1 ** ADAM_STEP)
    v_hat = nv / (1.0 - ADAM_B2 ** ADAM_STEP)
    return -ADAM_LR * (m_hat / (jnp.sqrt(v_hat) + ADAM_EPS) + ADAM_WD * w), nm, nv


def _adamw(name, w, g, m, v, tr, copy_g=False, stage=True, g_transposed=False):
    rows, cols = w.shape

    def body(w_ref, g_ref, m_ref, v_ref, *outs):
        d_ref, nm_ref, nv_ref = outs[-3:]
        for c in [pl.ds(c0, 128) for c0 in range(0, cols, 128)] if g_transposed else [slice(None)]:
            g_val = g_ref[c, :].T if g_transposed else g_ref[...]
            if copy_g:
                outs[0][:, c] = g_val
            d_ref[:, c], nm_ref[:, c], nv_ref[:, c] = _adam_math(w_ref[:, c], g_val, m_ref[:, c], v_ref[:, c])

    spec = pl.BlockSpec((tr, cols), lambda i: (i, 0))
    n_out = 4 if copy_g else 3
    g_spec = pl.BlockSpec((cols, tr), lambda i: (0, i)) if g_transposed else spec
    return _call(body, (w, g, m, v), name=name, grid=(rows // tr,), in_specs=[spec, g_spec, spec, spec], out_specs=[spec] * n_out,
                 out_shape=[SDS((rows, cols), F32)] * n_out, sem=("parallel",), vmem_mib=32,
                 free=(0, 2, 3) if stage else ())


C_SQ = 2 * DFF
P_W = C_SQ + 128
R_G2, R_GO, R_DCW, R_QK = 0, 1, 2, 5
C_GCO, C_GAO, C_DQG, C_DKG, C_SINK = 0, CW, 0, 128, 256


def _pack(name, me, ins, width, fill):
    def body(me_ref, *refs):
        o = refs[-1]
        o[...] = jnp.zeros_like(o)
        fill(o, *refs[:-1])

    return _call(body, ins, name=name, grid=(1,), prefetch=(me,),
                 in_specs=[pl.BlockSpec(a.shape, lambda i, me_ref: (0, 0)) for a in ins],
                 out_specs=[pl.BlockSpec((1, 8, width), lambda i, me_ref: (me_ref[0], 0, 0))],
                 out_shape=[SDS((N_DEV, 8, width), F32)], sem=("arbitrary",))[0]


def _pack_ffn(me, dfwg, dfwv, dfbg, dfbv, sq):
    def fill(o, dfwg_r, dfwv_r, dfbg_r, dfbv_r, sq_r):
        o[0, :, 0:DFF] = dfwg_r[...]
        o[0, :, DFF:2 * DFF] = dfwv_r[...]
        o[0, 3:4, 0:DFF] = dfbg_r[...]
        o[0, 3:4, DFF:2 * DFF] = dfbv_r[...]
        o[0, :, C_SQ:C_SQ + 128] = sq_r[...]

    return _pack("pack_ffn", me, (dfwg, dfwv, dfbg, dfbv, sq), P_W, fill)


def _fill_mix(o, dg2_r, dgco_r, dgao_r, dcw_r, dqg_r, dkg_r, dsink_r):
    o[0, R_G2:R_G2 + 1, :] = dg2_r[...]
    o[0, R_GO:R_GO + 1, C_GCO:C_GCO + CW] = dgco_r[...]
    o[0, R_GO:R_GO + 1, C_GAO:C_GAO + AW] = dgao_r[...]
    o[0, R_DCW:R_DCW + 3, 0:CW] = dcw_r[0:3, :]
    o[0, R_QK:R_QK + 1, C_DQG:C_DQG + HD] = dqg_r[...]
    o[0, R_QK:R_QK + 1, C_DKG:C_DKG + HD] = dkg_r[...]
    o[0, R_QK:R_QK + 1, C_SINK:C_SINK + 128] = dsink_r[...]


N_SMALL = 11


def _small_adam(chip, p_all, pm_all, g1_all, tbl_all, ws, ms, vs):
    fw_cols = 2 * DFF // N_CHIPS
    cw_cols = CW // N_CHIPS

    def body(chip_ref, p_ref, fw_ref, pm_ref, cw_ref, g1_ref, tbl_ref, *refs):
        w_r, m_r, v_r = refs[0:N_SMALL], refs[N_SMALL:2 * N_SMALL], refs[2 * N_SMALL:3 * N_SMALL]
        outs = refs[3 * N_SMALL:]
        g_o, d_o, nm_o, nv_o = (outs[k * N_SMALL:(k + 1) * N_SMALL] for k in range(4))
        loss_o = outs[4 * N_SMALL]

        def total(ref):
            s = ref[0]
            for k in range(1, N_DEV):
                s = s + ref[k]
            return s

        S = total(p_ref)
        fw = total(fw_ref)
        M = total(pm_ref)
        cw = total(cw_ref)

        def step(i, g, at):
            d, nm, nv = _adam_math(w_r[i][at], g, m_r[i][at], v_r[i][at])
            g_o[i][at], d_o[i][at], nm_o[i][at], nv_o[i][at] = g, d, nm, nv

        everything = (slice(None), slice(None))
        step(0, total(g1_ref), everything)
        for r in range(3):
            step(1, cw[R_DCW + r:R_DCW + r + 1, :], (r, slice(None), slice(None)))
        step(2, M[R_QK:R_QK + 1, C_DQG:C_DQG + HD], everything)
        step(3, M[R_QK:R_QK + 1, C_DKG:C_DKG + HD], everything)
        step(4, total(tbl_ref), everything)
        step(5, M[R_QK:R_QK + 1, C_SINK:C_SINK + NH], everything)
        step(6, M[R_GO:R_GO + 1, C_GCO:C_GCO + CW], everything)
        step(7, M[R_GO:R_GO + 1, C_GAO:C_GAO + AW], everything)
        step(8, M[R_G2:R_G2 + 1, :], everything)
        for r in range(3):
            step(9, fw[r:r + 1, :], (r, slice(None), slice(None)))
        step(10, S[3:4, 0:2 * DFF], everything)
        sq = S[:, C_SQ:C_SQ + 128]
        loss_o[...] = jnp.sum(jnp.sum(sq, axis=1, keepdims=True), axis=0, keepdims=True) * (0.5 / D)

    def full(a):
        n = len(a.shape)
        return pl.BlockSpec(a.shape, lambda i, chip_ref: (0,) * n)

    params = [*ws, *ms, *vs]
    out = _call(
        body, (p_all, p_all, pm_all, pm_all, g1_all, tbl_all, *params), name="small_adam", grid=(1,), prefetch=(chip,),
        in_specs=[full(p_all),
                  pl.BlockSpec((N_DEV, 8, fw_cols), lambda i, chip_ref: (0, 0, chip_ref[0])),
                  full(pm_all),
                  pl.BlockSpec((N_DEV, 8, cw_cols), lambda i, chip_ref: (0, 0, chip_ref[0])),
                  full(g1_all), full(tbl_all), *[full(a) for a in params]],
        out_specs=[full(a) for a in ws] * 4 + [pl.BlockSpec((1, 1), lambda i, chip_ref: (0, 0))],
        out_shape=[SDS(a.shape, F32) for a in ws] * 4 + [SDS((1, 1), F32)], sem=("arbitrary",), vmem_mib=32)
    return out[0:N_SMALL], out[N_SMALL:2 * N_SMALL], out[2 * N_SMALL:3 * N_SMALL], out[3 * N_SMALL:4 * N_SMALL], out[4 * N_SMALL]


PLACE_STEPS = 4


def _place_specs(shards):
    rows = [s.shape[0] // PLACE_STEPS for s in shards]
    return ([pl.BlockSpec((r, D), lambda i, chip_ref: (i, 0)) for r in rows],
            [pl.BlockSpec((r, D), lambda i, chip_ref: (chip_ref[0] * PLACE_STEPS + i, 0)) for r in rows],
            [SDS((N_CHIPS * s.shape[0], D), BF) for s in shards])


def _place_first(chip, shard, conv_w, ffn_conv_w):
    def body(chip_ref, a, s0, s1, o, t0, t1):
        o[...] = a[...].astype(BF)

        @pl.when(pl.program_id(0) == 0)
        def _():
            for s, t in ((s0, t0), (s1, t1)):
                t[...] = jnp.zeros_like(t)
                t[0, 0:3, :] = s[...]

    ins, outs, shapes = _place_specs([shard])
    taps = (conv_w, ffn_conv_w)
    return _call(
        body, (shard, conv_w, ffn_conv_w), name="place_first", grid=(PLACE_STEPS,), prefetch=(chip,),
        in_specs=ins + [pl.BlockSpec(s.shape, lambda i, chip_ref: (0, 0)) for s in taps],
        out_specs=outs + [pl.BlockSpec((1, 8, s.shape[1]), lambda i, chip_ref: (chip_ref[0], 0, 0)) for s in taps],
        out_shape=shapes + [SDS((N_CHIPS, 8, s.shape[1]), F32) for s in taps],
        sem=("arbitrary",), vmem_mib=32, free=(1, 2))


def _place_rest(chip, shards, w_up, table, bucket, comm):
    n = len(shards)
    c_up = w_up.shape[1]
    edges = [round(k * (c_up // 128) / PLACE_STEPS) * 128 for k in range(PLACE_STEPS + 1)]

    def body(chip_ref, *refs):
        a, (up_ref, tab_ref, bk_ref), o = refs[:n], refs[n:n + 3], refs[n + 3:2 * n + 3]
        up_o, bias_ref = refs[2 * n + 3:]
        for src, dst in zip(a, o):
            dst[...] = src[...].astype(BF)
        for k in range(PLACE_STEPS):
            @pl.when(pl.program_id(0) == k)
            def _(k=k):
                up_o[edges[k]:edges[k + 1], :] = up_ref[:, edges[k]:edges[k + 1]].T.astype(BF)

        @pl.when(pl.program_id(0) == 0)
        def _():
            bk = bk_ref[...]
            eq = [bk == b for b in range(NBUCKET)]
            for h in range(NH):
                acc = jnp.zeros((BLK, 2 * BLK), F32)
                for b in range(NBUCKET):
                    acc = jnp.where(eq[b], tab_ref[h, b], acc)
                bias_ref[h * BLK:(h + 1) * BLK, :] = acc

    ins, outs, shapes = _place_specs(shards)
    return _call(
        body, (*shards, w_up, table, bucket), name="place_rest", grid=(PLACE_STEPS,), prefetch=(chip,),
        in_specs=ins + [_resident(w_up.shape), pl.BlockSpec(memory_space=pltpu.SMEM),
                        pl.BlockSpec(bucket.shape, lambda i, chip_ref: (0, 0))],
        out_specs=outs + [pl.BlockSpec((c_up, D), lambda i, chip_ref: (chip_ref[0], 0)),
                          pl.BlockSpec((NH * BLK, 2 * BLK), lambda i, chip_ref: (0, 0))],
        out_shape=shapes + [SDS((N_CHIPS * c_up, D), BF), SDS((NH * BLK, 2 * BLK), F32)],
        sem=("arbitrary",), vmem_mib=32, comm=comm, free=(n + 1, n + 2))


def kernel(x, norm_mix_g, w_in, conv_w, q_norm_g, k_norm_g, rel_bias_table, sinks, out_norm_conv_g, out_norm_attn_g, w_out, norm_ffn_g, w_up, ffn_conv_w, ffn_conv_b, w_down, loss_target, m_norm_mix_g, m_w_in, m_conv_w, m_q_norm_g, m_k_norm_g, m_rel_bias_table, m_sinks, m_out_norm_conv_g, m_out_norm_attn_g, m_w_out, m_norm_ffn_g, m_w_up, m_ffn_conv_w, m_ffn_conv_b, m_w_down, v_norm_mix_g, v_w_in, v_conv_w, v_q_norm_g, v_k_norm_g, v_rel_bias_table, v_sinks, v_out_norm_conv_g, v_out_norm_attn_g, v_w_out, v_norm_ffn_g, v_w_up, v_ffn_conv_w, v_ffn_conv_b, v_w_down):
    as_arg = lambda i: jnp.reshape(i, (1,)).astype(jnp.int32)
    chip = as_arg(2 * lax.axis_index("x") + lax.axis_index("y"))
    core = as_arg(lax.axis_index("c"))
    me = 2 * chip + core
    xs, tgt = x[0], loss_target[0]
    qg, kg, gco, gao, g1, g2, fb = q_norm_g, k_norm_g, out_norm_conv_g, out_norm_attn_g, norm_mix_g, norm_ffn_g, ffn_conv_b
    pieces = lambda g: g.reshape(N_DEV, g.shape[0] // N_DEV, D)
    whole = lambda f: f.reshape(2 * f.shape[1], D)

    bucket = jnp.asarray(_bucket_table())
    p_in, p_cw, p_fw = _place_first(chip, w_in[0].T, conv_w[0], ffn_conv_w[0])
    p_out, p_down, p_up, bias, w_int, cw_all, fw_all = _place_rest(
        chip, [w_out[0], w_down[0]], w_up[0], rel_bias_table.T, bucket,
        comm=[_t_gather(p_in, relayed_first=True), _t_small_weights(p_cw), _t_small_weights(p_fw)])
    cw8 = jnp.transpose(cw_all, (1, 0, 2)).reshape(8, CW)
    fw8 = jnp.transpose(fw_all, (1, 0, 2)).reshape(8, 2 * DFF)

    early = 3 / 11
    proj, u1, w_out_f, p_up = _inproj(xs, g1, w_int, comm=[_t_gather(p_out), _t_gather(p_up, (0, early))])
    y, w_upt = _mix_fwd(proj, sinks, cw8, qg, kg, gco, gao, bias, comm=[_t_gather(p_up, (early, 1))])
    h1, u2 = _outproj(y, w_out_f, xs, g2)
    up, w_down_f = _ffn_up(u2, w_upt, comm=[_t_gather(p_down)])
    a, pre_g, pre_v = _ffn_act(up, fw8, fb)
    dh2, dh2b, sq = _ffn_down(a, w_down_f, h1, tgt)

    gdbf, = _wgrad("wgrad_down", [a], dh2b, None)
    da, sib_down = _ffn_down_bwd(dh2b, w_down_f, comm=[_t_sibling(pieces(gdbf))])
    pbf_down, own_down = _chip_sum("chip_sum_w_down", pieces(gdbf), sib_down, core, chip)
    dug, duv, dfwg, dfwv, dfbg, dfbv, chips_down = _ffn_act_bwd(up, pre_g, pre_v, da, fw8, comm=[_t_chips(pbf_down)])
    fin_down, = _final_sum("final_sum_w_down", own_down, chips_down, core)
    gubf, = _wgrad("wgrad_up", [dug, duv], u2, False)
    p_all = _pack_ffn(me, dfwg, dfwv, dfbg, dfbv, sq)
    dh1, dh1b, dg2, sib_up, fin_down, p_all = _norm_matmul_bwd(
        "ffn_up_bwd", [dug, duv], w_upt, [0, DFF], h1, g2, dh2, True,
        comm=[_t_sibling(pieces(gubf)), _t_swap(fin_down), _t_allgather(p_all)])
    pbf_up, own_up = _chip_sum("chip_sum_w_up", pieces(gubf), sib_up, core, chip)
    gobf, = _wgrad("wgrad_out", [y], dh1b, True)
    dy, sib_out = _out_bwd(dh1b, w_out_f, comm=[_t_sibling(pieces(gobf))])
    pbf_out, own_out = _chip_sum("chip_sum_w_out", pieces(gobf), sib_out, core, chip)
    dproj, dcw8, dqg, dkg, dgco, dgao, dsink, dbias, chips_up = _mix_bwd(
        proj, dy, sinks, cw8, qg, kg, gco, gao, bias, comm=[_t_chips(pbf_up)])
    fin_up, = _final_sum("final_sum_w_up", own_up, chips_up, core)
    gibf, chips_out, fin_up = _wgrad("wgrad_in", [dproj], u1, False, comm=[_t_chips(pbf_out), _t_swap(fin_up)])
    fin_out, sib_in = _final_sum("final_sum_w_out", own_out, chips_out, core, comm=[_t_sibling(pieces(gibf))])
    pbf_in, own_in = _chip_sum("chip_sum_w_in", pieces(gibf), sib_in, core, chip)
    dx, g1_all, tbl_all, pm_all, chips_in, fin_out = _norm_matmul_bwd(
        "in_bwd", [dproj], w_int, [0], xs, g1, dh1, False, comm=[_t_chips(pbf_in), _t_swap(fin_out)], slot=me,
        band=(dbias, bucket), pack=(dg2, dgco, dgao, dcw8, dqg, dkg, dsink))
    fin_in, = _final_sum("final_sum_w_in", own_in, chips_in, core)
    g1_all, tbl_all, pm_all, fin_in = _comm_call(
        "gather_last", [_t_allgather(g1_all), _t_allgather(tbl_all), _t_allgather(pm_all), _t_swap(fin_in)])

    g_w_out, g_w_down = whole(fin_out), whole(fin_down)
    g_w_down, d_down, nm_down, nv_down = _adamw("adamw_w_down", w_down[0], g_w_down, m_w_down[0], v_w_down[0], 352, True)
    g_w_up, d_up, nm_up, nv_up = _adamw(
        "adamw_w_up", w_up[0], whole(fin_up), m_w_up[0], v_w_up[0], 256, True, stage=False, g_transposed=True)
    g_w_out, d_out, nm_out, nv_out = _adamw("adamw_w_out", w_out[0], g_w_out, m_w_out[0], v_w_out[0], 256, True, stage=False)
    g_w_in, d_in, nm_in, nv_in = [a.T for a in _adamw(
        "adamw_w_in", w_in[0].T, whole(fin_in), m_w_in[0].T, v_w_in[0].T, INW // N_CHIPS // 3, True, stage=False)]
    taps = lambda a: jnp.transpose(a, (1, 0, 2))
    sw = [norm_mix_g, taps(conv_w), q_norm_g, k_norm_g, rel_bias_table.T, sinks, out_norm_conv_g, out_norm_attn_g,
          norm_ffn_g, taps(ffn_conv_w), ffn_conv_b]
    smm = [m_norm_mix_g, taps(m_conv_w), m_q_norm_g, m_k_norm_g, m_rel_bias_table.T, m_sinks, m_out_norm_conv_g,
           m_out_norm_attn_g, m_norm_ffn_g, taps(m_ffn_conv_w), m_ffn_conv_b]
    smv = [v_norm_mix_g, taps(v_conv_w), v_q_norm_g, v_k_norm_g, v_rel_bias_table.T, v_sinks, v_out_norm_conv_g,
           v_out_norm_attn_g, v_norm_ffn_g, taps(v_ffn_conv_w), v_ffn_conv_b]
    *small_out, loss = _small_adam(chip, p_all, pm_all, g1_all, tbl_all, sw, smm, smv)
    sg, sd, snm, snv = [list(r) for r in small_out]
    for r in (sg, sd, snm, snv):
        r[1], r[4], r[9] = taps(r[1]), r[4].T, taps(r[9])

    def order(s, b_in, b_out, b_up, b_down):
        return (s[0], b_in[None], s[1], s[2], s[3], s[4], s[5], s[6], s[7], b_out[None], s[8], b_up[None],
                s[9], s[10], b_down[None])

    return (loss.reshape(()), dx[None],
            *order(sg, g_w_in, g_w_out, g_w_up, g_w_down),
            *order(sd, d_in, d_out, d_up, d_down),
            *order(snm, nm_in, nm_out, nm_up, nm_down),
            *order(snv, nv_in, nv_out, nv_up, nv_down))
```

```python
import functools
import math

import numpy as np

import jax
import jax.numpy as jnp
from jax import lax
from jax.experimental import pallas as pl
from jax.experimental.pallas import tpu as pltpu

F32 = jnp.float32
BF = jnp.bfloat16
SDS = jax.ShapeDtypeStruct

T = 2048
D = 1024
CW = 512
AW = 512
HD = 64
NH = 8
NKV = 2
GQ = 4
INW = 2304
DFF = 2816
BLK = 128
NB = T // BLK
NBUCKET = 32
EPS = 1e-6
NEG_INF = -1e30
N_CHIPS = 4
N_DEV = 8

ADAM_LR = 0.001
ADAM_B1 = 0.9
ADAM_B2 = 0.999
ADAM_EPS = 1e-08
ADAM_WD = 0.01
ADAM_STEP = 10

TM = 512
MIB = 1024 * 1024
MESH = pl.DeviceIdType.MESH
ANY = pl.BlockSpec(memory_space=pl.ANY)

_pcall = pl.pallas_call


def _params(sem=None, vmem_mib=None, collective_id=None):
    kw = {} if collective_id is None else {"collective_id": collective_id}
    if sem is not None:
        kw["dimension_semantics"] = sem
    if vmem_mib is not None:
        kw["vmem_limit_bytes"] = vmem_mib * MIB
    return pltpu.CompilerParams(**kw)


def _resident(shape):
    return pl.BlockSpec(shape, lambda *_: (0,) * len(shape), pipeline_mode=pl.Buffered(1))


def _dot(a, b, ca, cb):
    return lax.dot_general(a, b, (((ca,), (cb,)), ((), ())), preferred_element_type=F32)


def _rms_bwd(dy, x, r, g):
    dg = jnp.sum(dy * (x * r), axis=0, keepdims=True)
    dgx = dy * g
    dx = r * dgx - x * (r * r * r) * jnp.mean(x * dgx, axis=-1, keepdims=True)
    return dx, dg


def _where():
    x, y, c = lax.axis_index("x"), lax.axis_index("y"), lax.axis_index("c")
    return x, y, c, [(1 - x, y), (x, 1 - y), (1 - x, 1 - y)]


def _rcopy(src, dst, ssem, rsem, dev):
    return pltpu.make_async_remote_copy(src_ref=src, dst_ref=dst, send_sem=ssem, recv_sem=rsem, device_id=dev,
                                        device_id_type=MESH)


SIBLING, Y_CHIP, X_CHIP, DIAGONAL_CHIP = 1, 2, 4, 6
OTHER_CHIPS = (Y_CHIP, X_CHIP, DIAGONAL_CHIP)
EVERYONE = tuple(range(1, N_DEV))
BARRIER_OF = {(SIBLING,): 0, (SIBLING, Y_CHIP, X_CHIP): 1, OTHER_CHIPS: 2, (SIBLING,) + OTHER_CHIPS: 3, EVERYONE: 4}


def _peer(rel):
    x, y, c, _ = _where()
    return x ^ ((rel >> 2) & 1), y ^ ((rel >> 1) & 1), c ^ (rel & 1)


class _Task:
    def __init__(self, ins, outs, alias, n_sem, start, finish, middle=None, peers=()):
        self.ins, self.outs, self.alias, self.n_sem, self.start, self.finish = ins, outs, alias, n_sem, start, finish
        self.middle = middle if middle is not None else (lambda *args: None)
        self.peers = peers


def _peers_of(comm):
    return tuple(sorted({p for t in comm for p in t.peers}))


def _enter(comm):
    peers = _peers_of(comm)
    barrier = pltpu.get_barrier_semaphore()
    for rel in peers:
        pl.semaphore_signal(barrier, inc=1, device_id=_peer(rel), device_id_type=MESH)
    pl.semaphore_wait(barrier, len(peers))


ROWS16 = 16


def _t_gather(placed, part=(0, 1), relayed_first=False):
    R = placed.shape[0] // N_CHIPS
    q = R // 4
    lo, hi = (round(f * (q // ROWS16)) * ROWS16 for f in part)

    def quarter(chip_index, core, k):
        return pl.ds(pl.multiple_of(chip_index * R + core * 2 * q + k * q + lo, ROWS16), hi - lo)

    def places():
        x, y, c, _ = _where()
        return c, 2 * x + y, 2 * (1 - x) + y, 2 * x + (1 - y), 2 * (1 - x) + (1 - y), (1 - x, y, c), (x, 1 - y, c), (x, y, 1 - c)

    def copy(buf, k, chip_index, core, quart, ss, rs, b, dev):
        window = buf.at[quarter(chip_index, core, quart)]
        return _rcopy(window, window, ss.at[b + k], rs.at[b + k], dev)

    def first_hop(cout, ss, rs, b, which):
        c, me, _, _, _, x_nbr, y_nbr, _ = places()
        for k, (quart, dev) in enumerate(((0, x_nbr), (1, y_nbr), (1, x_nbr), (0, y_nbr))):
            if k in which:
                copy(cout[0], k, me, c, quart, ss, rs, b, dev).start()

    def start(cin, cout, ss, rs, b):
        first_hop(cout, ss, rs, b, (0, 1) if relayed_first else (0, 1, 2, 3))

    def middle(cin, cout, ss, rs, b):
        c, _, xc, yc, _, x_nbr, y_nbr, sib = places()
        for k, chip_index, quart, dev in ((0, xc, 0, y_nbr), (1, yc, 1, x_nbr)):
            copy(cout[0], k, chip_index, c, quart, ss, rs, b, dev).wait_recv()
            copy(cout[0], 4 + k, chip_index, c, quart, ss, rs, b, dev).start()
            copy(cout[0], 6 + k, chip_index, c, quart, ss, rs, b, sib).start()
        if relayed_first:
            first_hop(cout, ss, rs, b, (2, 3))

    later = ((2, 1, 1), (3, 2, 0), (4, 3, 0), (5, 3, 1))

    def finish(cin, cout, ss, rs, b):
        c, me, xc, yc, dc, _, _, sib = places()
        chip_of = {1: xc, 2: yc, 3: dc}
        for k, whose, quart in later:
            copy(cout[0], k, chip_of[whose], c, quart, ss, rs, b, sib).wait_recv()
            copy(cout[0], 6 + k, chip_of[whose], c, quart, ss, rs, b, sib).start()
        for k, whose, quart in ((0, 1, 0), (1, 2, 1)) + later:
            copy(cout[0], 6 + k, chip_of[whose], 1 - c, quart, ss, rs, b, sib).wait_recv()
        for k in range(12):
            copy(cout[0], k, me, c, 0, ss, rs, b, sib).wait_send()

    return _Task([placed], [SDS(placed.shape, placed.dtype)], [(0, 0)], 12, start, finish, middle, peers=(SIBLING, Y_CHIP, X_CHIP))


def _t_small_weights(buf):
    def start(cin, cout, ss, rs, b):
        x, y, c, chips = _where()
        mine = cout[0].at[2 * x + y]
        for r, (px, py) in enumerate(chips):
            _rcopy(mine, mine, ss.at[b + r], rs.at[b + r], (px, py, c)).start()

    def finish(cin, cout, ss, rs, b):
        x, y, c, chips = _where()
        for r, (px, py) in enumerate(chips):
            got = cout[0].at[2 * px + py]
            _rcopy(got, got, ss.at[b + r], rs.at[b + r], (px, py, c)).wait_recv()
        for r, (px, py) in enumerate(chips):
            mine = cout[0].at[2 * x + y]
            _rcopy(mine, mine, ss.at[b + r], rs.at[b + r], (px, py, c)).wait_send()

    return _Task([buf], [SDS(buf.shape, buf.dtype)], [(0, 0)], 3, start, finish, peers=OTHER_CHIPS)


def _t_sibling(gbf):
    def start(cin, cout, ss, rs, b):
        x, y, c, _ = _where()
        for jj in range(N_CHIPS):
            _rcopy(cin[0].at[2 * jj + (1 - c)], cout[0].at[jj], ss.at[b + jj], rs.at[b + jj], (x, y, 1 - c)).start()

    def finish(cin, cout, ss, rs, b):
        x, y, c, _ = _where()
        for jj in range(N_CHIPS):
            got = cout[0].at[jj]
            _rcopy(got, got, ss.at[b + jj], rs.at[b + jj], (x, y, 1 - c)).wait_recv()
        for jj in range(N_CHIPS):
            got = cout[0].at[jj]
            _rcopy(got, got, ss.at[b + jj], rs.at[b + jj], (x, y, 1 - c)).wait_send()

    return _Task([gbf], [SDS((N_CHIPS,) + gbf.shape[1:], BF)], [], N_CHIPS, start, finish, peers=(SIBLING,))


def _t_chips(pbf):
    def start(cin, cout, ss, rs, b):
        x, y, c, chips = _where()
        for r, (px, py) in enumerate(chips):
            _rcopy(cin[0].at[2 * px + py], cout[0].at[r], ss.at[b + r], rs.at[b + r], (px, py, c)).start()

    def finish(cin, cout, ss, rs, b):
        x, y, c, chips = _where()
        for r, (px, py) in enumerate(chips):
            got = cout[0].at[r]
            _rcopy(got, got, ss.at[b + r], rs.at[b + r], (px, py, c)).wait_recv()
        for r, (px, py) in enumerate(chips):
            got = cout[0].at[r]
            _rcopy(got, got, ss.at[b + r], rs.at[b + r], (px, py, c)).wait_send()

    return _Task([pbf], [SDS((3,) + pbf.shape[1:], BF)], [], 3, start, finish, peers=OTHER_CHIPS)


def _t_swap(fin):
    def start(cin, cout, ss, rs, b):
        x, y, c, _ = _where()
        mine = cout[0].at[c]
        _rcopy(mine, mine, ss.at[b], rs.at[b], (x, y, 1 - c)).start()

    def finish(cin, cout, ss, rs, b):
        x, y, c, _ = _where()
        got = cout[0].at[1 - c]
        _rcopy(got, got, ss.at[b], rs.at[b], (x, y, 1 - c)).wait_recv()
        _rcopy(got, got, ss.at[b], rs.at[b], (x, y, 1 - c)).wait_send()

    return _Task([fin], [SDS(fin.shape, fin.dtype)], [(0, 0)], 1, start, finish, peers=(SIBLING,))


def _t_allgather(buf):
    def peers():
        x, y, c, _ = _where()
        out = []
        for rel in range(1, N_DEV):
            px, py, pc = x ^ ((rel >> 2) & 1), y ^ ((rel >> 1) & 1), c ^ (rel & 1)
            out.append((rel - 1, 4 * px + 2 * py + pc, (px, py, pc)))
        return 4 * x + 2 * y + c, out

    def start(cin, cout, ss, rs, b):
        me, ps = peers()
        mine = cout[0].at[me]
        for k, _, dev in ps:
            _rcopy(mine, mine, ss.at[b + k], rs.at[b + k], dev).start()

    def finish(cin, cout, ss, rs, b):
        me, ps = peers()
        for k, pidx, dev in ps:
            got = cout[0].at[pidx]
            _rcopy(got, got, ss.at[b + k], rs.at[b + k], dev).wait_recv()
        for k, _, dev in ps:
            mine = cout[0].at[me]
            _rcopy(mine, mine, ss.at[b + k], rs.at[b + k], dev).wait_send()

    return _Task([buf], [SDS(buf.shape, buf.dtype)], [(0, 0)], N_DEV - 1, start, finish, peers=EVERYONE)


def _run_tasks(comm, which, cin, cout, ss, rs):
    i0 = o0 = s0 = 0
    for t in comm:
        getattr(t, which)(cin[i0:i0 + len(t.ins)], cout[o0:o0 + len(t.outs)], ss, rs, s0)
        i0, o0, s0 = i0 + len(t.ins), o0 + len(t.outs), s0 + t.n_sem


def _from_hbm(*arrays):
    return [pltpu.with_memory_space_constraint(a, pltpu.HBM) for a in arrays]


def _in_hbm(shapes):
    return [pltpu.HBM(s.shape, s.dtype) for s in shapes]


def _comm_layout(comm, n_in, n_out):
    c_in = [a for t in comm for a in t.ins]
    c_out = [s for t in comm for s in t.outs]
    aliases, i0, o0 = {}, 0, 0
    for t in comm:
        for i, o in t.alias:
            aliases[n_in + i0 + i] = n_out + o0 + o
        i0, o0 = i0 + len(t.ins), o0 + len(t.outs)
    return c_in, c_out, aliases, sum(t.n_sem for t in comm)


def _call(body, operands, *, name, grid, in_specs, out_specs, out_shape, scratch_shapes=(), sem=None, vmem_mib=None, comm=(),
          free=(), prefetch=()):
    operands = [o if s.memory_space == pltpu.SMEM or k in free else pltpu.with_memory_space_constraint(o, pltpu.HBM)
                for k, (o, s) in enumerate(zip(operands, in_specs))]
    n_pre, n_in, n_out, n_scr = len(prefetch), len(in_specs), len(out_specs), len(scratch_shapes)
    c_in, c_out, aliases, n_sem = _comm_layout(comm, n_pre + n_in, n_out)
    sems = [pltpu.SemaphoreType.DMA((n_sem,)), pltpu.SemaphoreType.DMA((n_sem,))] if comm else []

    def wrapped(*refs):
        pre, refs = refs[:n_pre], refs[n_pre:]
        ins, cin = refs[:n_in], refs[n_in:n_in + len(c_in)]
        rest = refs[n_in + len(c_in):]
        outs, cout = rest[:n_out], rest[n_out:n_out + len(c_out)]
        rest = rest[n_out + len(c_out):]
        scr, csem = rest[:n_scr], rest[n_scr:]
        if not comm:
            return body(*pre, *ins, *outs, *scr)
        step = functools.reduce(lambda acc, k: acc * grid[k] + pl.program_id(k), range(len(grid)), 0)
        n_steps = math.prod(grid)

        @pl.when(step == 0)
        def _():
            _enter(comm)
            _run_tasks(comm, "start", cin, cout, *csem)

        pl.when(step == n_steps // 2)(lambda: _run_tasks(comm, "middle", cin, cout, *csem))
        body(*pre, *ins, *outs, *scr)
        pl.when(step == n_steps - 1)(lambda: _run_tasks(comm, "finish", cin, cout, *csem))

    grid_spec = pltpu.PrefetchScalarGridSpec(
        num_scalar_prefetch=n_pre, grid=grid, in_specs=list(in_specs) + [ANY] * len(c_in),
        out_specs=list(out_specs) + [ANY] * len(c_out), scratch_shapes=list(scratch_shapes) + sems)
    return _pcall(
        wrapped, name=name, grid_spec=grid_spec, out_shape=_in_hbm(list(out_shape) + c_out), input_output_aliases=aliases,
        compiler_params=_params(("arbitrary",) * len(grid) if comm else sem, vmem_mib,
                                BARRIER_OF[_peers_of(comm)] if comm else None),
    )(*prefetch, *operands, *_from_hbm(*c_in))


def _comm_call(name, comm):
    c_in, c_out, aliases, n_sem = _comm_layout(comm, 0, 0)

    def body(*refs):
        cin, cout, (ss, rs) = refs[:len(c_in)], refs[len(c_in):len(c_in) + len(c_out)], refs[len(c_in) + len(c_out):]
        _enter(comm)
        for phase in ("start", "middle", "finish"):
            _run_tasks(comm, phase, cin, cout, ss, rs)

    return _pcall(
        body, name=name, in_specs=[ANY] * len(c_in), out_specs=[ANY] * len(c_out), out_shape=_in_hbm(c_out),
        scratch_shapes=[pltpu.SemaphoreType.DMA((n_sem,)), pltpu.SemaphoreType.DMA((n_sem,))],
        input_output_aliases=aliases, compiler_params=_params(collective_id=BARRIER_OF[_peers_of(comm)]),
    )(*_from_hbm(*c_in))


def _inproj(x, g1, w_int, comm=()):
    tm = TM

    def body(x_ref, g_ref, w_ref, proj_ref, u_ref):
        xf = x_ref[...]
        r = lax.rsqrt(jnp.mean(xf * xf, axis=-1, keepdims=True) + EPS)
        u = (xf * r * g_ref[...]).astype(BF)
        u_ref[...] = u
        proj_ref[...] = _dot(u, w_ref[...], 1, 1)

    return _call(
        body, (x, g1, w_int), name="inproj", grid=(T // tm,),
        in_specs=[pl.BlockSpec((tm, D), lambda i: (i, 0)), pl.BlockSpec((1, D), lambda i: (0, 0)),
                  _resident((INW, D))],
        out_specs=[pl.BlockSpec((tm, INW), lambda i: (i, 0)), pl.BlockSpec((tm, D), lambda i: (i, 0))],
        out_shape=[SDS((T, INW), F32), SDS((T, D), BF)], sem=("parallel",), vmem_mib=40, comm=comm, free=(0, 1))


def _outproj(y, w_out, x, g2):
    tm = TM

    def body(y_ref, w_ref, x_ref, g_ref, h1_ref, u2_ref):
        h1 = x_ref[...] + _dot(y_ref[...], w_ref[...], 1, 0)
        h1_ref[...] = h1
        r = lax.rsqrt(jnp.mean(h1 * h1, axis=-1, keepdims=True) + EPS)
        u2_ref[...] = (h1 * r * g_ref[...]).astype(BF)

    return _call(
        body, (y, w_out, x, g2), name="outproj", grid=(T // tm,),
        in_specs=[pl.BlockSpec((tm, D), lambda i: (i, 0)), _resident((D, D)),
                  pl.BlockSpec((tm, D), lambda i: (i, 0)), pl.BlockSpec((1, D), lambda i: (0, 0))],
        out_specs=[pl.BlockSpec((tm, D), lambda i: (i, 0)), pl.BlockSpec((tm, D), lambda i: (i, 0))],
        out_shape=[SDS((T, D), F32), SDS((T, D), BF)], sem=("parallel",), vmem_mib=32, free=(2, 3))


def _ffn_up(u2, w_upt, comm=()):
    tm, tn = T, 512

    def body(u_ref, w_ref, o_ref):
        o_ref[...] = _dot(u_ref[...], w_ref[...], 1, 1).astype(BF)

    return _call(
        body, (u2, w_upt), name="ffn_up", grid=(T // tm, 2 * DFF // tn),
        in_specs=[pl.BlockSpec((tm, D), lambda i, j: (i, 0)), pl.BlockSpec((tn, D), lambda i, j: (j, 0))],
        out_specs=[pl.BlockSpec((tm, tn), lambda i, j: (i, j))], out_shape=[SDS((T, 2 * DFF), BF)],
        sem=("parallel", "parallel"), vmem_mib=32, comm=comm, free=(1,))


def _ffn_down(a, w_down, h1, tgt):
    tm = TM

    def body(a_ref, w_ref, h1_ref, t_ref, dh_ref, dhb_ref, l_ref):
        @pl.when(pl.program_id(0) == 0)
        def _():
            l_ref[...] = jnp.zeros_like(l_ref)

        h2 = h1_ref[...] + _dot(a_ref[...], w_ref[...], 1, 0)
        e = h2 - t_ref[...]
        dh = e * (1.0 / D)
        dh_ref[...] = dh
        dhb_ref[...] = dh.astype(BF)
        e2 = jnp.sum((e * e).reshape(tm // 8, 8, D), axis=0)
        acc = e2[:, 0:128]
        for k in range(1, D // 128):
            acc = acc + e2[:, k * 128:(k + 1) * 128]
        l_ref[...] += acc

    return _call(
        body, (a, w_down, h1, tgt), name="ffn_down", grid=(T // tm,),
        in_specs=[pl.BlockSpec((tm, DFF), lambda i: (i, 0)), _resident((DFF, D)),
                  pl.BlockSpec((tm, D), lambda i: (i, 0)), pl.BlockSpec((tm, D), lambda i: (i, 0))],
        out_specs=[pl.BlockSpec((tm, D), lambda i: (i, 0)), pl.BlockSpec((tm, D), lambda i: (i, 0)),
                   pl.BlockSpec((8, 128), lambda i: (0, 0))],
        out_shape=[SDS((T, D), F32), SDS((T, D), BF), SDS((8, 128), F32)], sem=("arbitrary",), vmem_mib=40, free=(2, 3))


def _bucket_table():
    q = np.arange(BLK, dtype=np.int32)[:, None]
    j = np.arange(2 * BLK, dtype=np.int32)[None, :]
    n = np.maximum(q + BLK - j, 0)
    nf = np.maximum(n, 1).astype(np.float32)
    max_exact = NBUCKET // 2
    large = max_exact + (np.log(nf / np.float32(max_exact)) / np.float32(math.log(BLK / max_exact))
                         * np.float32(NBUCKET - max_exact)).astype(np.int32)
    large = np.minimum(large, NBUCKET - 1)
    return np.where(n < max_exact, n, large).astype(np.int32)


def _two_bf16(x):
    hi = x.astype(BF)
    return hi, (x - hi.astype(F32)).astype(BF)


def _head_sums(x, seg):
    hi, lo = _two_bf16(x)
    s = seg[0:x.shape[1], :]
    return _dot(hi, s, 1, 0) + _dot(lo, s, 1, 0)


def _head_spread(v, seg, width):
    hi, lo = _two_bf16(v)
    s = seg[0:width, :]
    return _dot(hi, s, 1, 1) + _dot(lo, s, 1, 1)


def _head_norm(x, g_t, seg, by_head=False):
    if by_head:
        heads = [x[:, h * HD:(h + 1) * HD] for h in range(x.shape[1] // HD)]
        r = jnp.concatenate([jnp.broadcast_to(lax.rsqrt(jnp.mean(v * v, axis=-1, keepdims=True) + EPS), v.shape)
                             for v in heads], axis=1)
    else:
        r = lax.rsqrt(_head_sums(x * x, seg) * (1.0 / HD) + EPS)
        r = _head_spread(r, seg, x.shape[1])
    return x * r * g_t, r


def _head_norm_bwd(dy, x, r, g_t, seg):
    dg_t = jnp.sum(dy * (x * r), axis=0, keepdims=True)
    dgx = dy * g_t
    mean = _head_spread(_head_sums(x * dgx, seg) * (1.0 / HD), seg, x.shape[1])
    return r * dgx - x * (r * r * r) * mean, dg_t


def _fold_heads(v):
    out = v[:, 0:HD]
    for h in range(1, v.shape[1] // HD):
        out = out + v[:, h * HD:(h + 1) * HD]
    return out


def _mix_forward(P, zc8, zh8, pkv, first, cw, qg_t, kg_t, gco, gao, seg, sink_ref, bias_ref, by_head=False):
    gate_b = P[:, 0:CW]
    gate_c = P[:, CW:2 * CW]
    hc = P[:, 2 * CW:3 * CW]
    z = gate_c * hc
    keep = jnp.where(first, 0.0, 1.0)
    zp = zc8 * zh8 * keep
    p1 = zp[7:8, :]
    p2 = zp[6:7, :]
    row = lax.broadcasted_iota(jnp.int32, (BLK, 1), 0)
    z1 = jnp.where(row == 0, p1, pltpu.roll(z, 1, 0))
    z2 = jnp.where(row == 0, p2, jnp.where(row == 1, p1, pltpu.roll(z, 2, 0)))
    cz = cw[0:1, :] * z2 + cw[1:2, :] * z1 + cw[2:3, :] * z
    y_conv = gate_b * cz

    scale = HD ** -0.5
    qi = lax.broadcasted_iota(jnp.int32, (BLK, 2 * BLK), 0)
    kj = lax.broadcasted_iota(jnp.int32, (BLK, 2 * BLK), 1)
    dd = qi + BLK - kj
    first_key = jnp.where(first, BLK, 0)
    valid = (dd >= 0) & (dd < BLK) & (kj >= first_key)

    q0 = 3 * CW
    k0 = q0 + AW
    v0 = k0 + NKV * HD
    q_raw = P[:, q0:k0]
    qn, rq = _head_norm(q_raw, qg_t, seg, by_head)
    qs = (qn * scale).astype(BF)
    k_raw = jnp.concatenate([pkv[:, 0:NKV * HD], P[:, k0:v0]], axis=0)
    kn, rk = _head_norm(k_raw, kg_t, seg, by_head)
    knb = kn.astype(BF)
    heads = []
    for h in range(NH):
        kv = h // GQ
        kb = knb[:, kv * HD:(kv + 1) * HD]
        vb = jnp.concatenate([pkv[:, NKV * HD + kv * HD:NKV * HD + (kv + 1) * HD],
                              P[:, v0 + kv * HD:v0 + (kv + 1) * HD]], axis=0).astype(BF)
        Q = qs[:, h * HD:(h + 1) * HD]
        S = _dot(Q, kb, 1, 1) + bias_ref[h * BLK:(h + 1) * BLK, :]
        S = jnp.where(valid, S, NEG_INF)
        sink = sink_ref[0, h]
        m = jnp.maximum(jnp.max(S, axis=-1, keepdims=True), sink)
        p = jnp.exp(S - m)
        es = jnp.exp(sink - m)
        denom = jnp.sum(p, axis=-1, keepdims=True) + es
        probs = p / denom
        O = _dot(probs.astype(BF), vb, 1, 0)
        heads.append(dict(kb=kb, vb=vb, Q=Q, probs=probs, psink=es / denom, O=O))
    y_attn = jnp.concatenate([hd["O"] for hd in heads], axis=1)

    rc = lax.rsqrt(jnp.mean(y_conv * y_conv, axis=-1, keepdims=True) + EPS)
    ra = lax.rsqrt(jnp.mean(y_attn * y_attn, axis=-1, keepdims=True) + EPS)
    y = jnp.concatenate([y_conv * rc * gco, y_attn * ra * gao], axis=1)
    return dict(gate_b=gate_b, gate_c=gate_c, hc=hc, z=z, z1=z1, z2=z2, cz=cz, y_conv=y_conv, y_attn=y_attn,
                rc=rc, ra=ra, heads=heads, y=y, row=row, scale=scale, q_raw=q_raw, rq=rq, k_raw=k_raw, rk=rk)


BPS = 2
TILE = BPS * BLK
KV0 = 3 * CW + AW


def _mix_in_specs(tile_of):
    return [
        pl.BlockSpec(memory_space=pltpu.SMEM),
        pl.BlockSpec((TILE, INW), lambda s: (tile_of(s), 0)),
        pl.BlockSpec((8, CW), lambda s: (jnp.maximum(tile_of(s) * (TILE // 8) - 1, 0), 1)),
        pl.BlockSpec((8, CW), lambda s: (jnp.maximum(tile_of(s) * (TILE // 8) - 1, 0), 2)),
        pl.BlockSpec((BLK, 2 * NKV * HD), lambda s: (jnp.maximum(tile_of(s) * BPS - 1, 0), KV0 // (2 * NKV * HD))),
    ]


def _block_inputs(tile, b, zc_ref, zh_ref, pkv_ref, first_tile):
    P = tile[b * BLK:(b + 1) * BLK, :]
    if b == 0:
        return P, zc_ref[...], zh_ref[...], pkv_ref[...], first_tile
    lo = b * BLK
    return P, tile[lo - 8:lo, CW:2 * CW], tile[lo - 8:lo, 2 * CW:3 * CW], tile[lo - BLK:lo, KV0:KV0 + 2 * NKV * HD], False


def _mix_param_specs():
    return [
        pl.BlockSpec((8, CW), lambda s: (0, 0)),
        pl.BlockSpec((1, AW), lambda s: (0, 0)),
        pl.BlockSpec((1, NKV * HD), lambda s: (0, 0)),
        pl.BlockSpec((1, CW), lambda s: (0, 0)),
        pl.BlockSpec((1, AW), lambda s: (0, 0)),
        pl.BlockSpec((AW, 128), lambda s: (0, 0)),
        pl.BlockSpec((NH * BLK, 2 * BLK), lambda s: (0, 0)),
    ]


def _mix_params(cw8, qg, kg, gco, gao, bias):
    seg = np.zeros((AW, 128), np.float32)
    seg[np.arange(AW), np.arange(AW) // HD] = 1.0
    return (cw8, jnp.tile(qg, (1, NH)), jnp.tile(kg, (1, NKV)), gco, gao, jnp.asarray(seg, BF), bias)


def _mix_fwd(proj, sinks, cw8, qg, kg, gco, gao, bias, comm=()):
    def body(sink_ref, p_ref, zc_ref, zh_ref, pkv_ref, cw_ref, qg_ref, kg_ref, gco_ref, gao_ref, seg_ref, bias_ref, y_ref):
        tile = p_ref[...]
        for b in range(BPS):
            f = _mix_forward(*_block_inputs(tile, b, zc_ref, zh_ref, pkv_ref, pl.program_id(0) == 0), cw_ref[...],
                             qg_ref[...], kg_ref[...], gco_ref[...], gao_ref[...], seg_ref[...], sink_ref, bias_ref, by_head=True)
            y_ref[b * BLK:(b + 1) * BLK, :] = f["y"].astype(BF)

    return _call(
        body, (sinks, proj, proj, proj, proj, *_mix_params(cw8, qg, kg, gco, gao, bias)), name="mix_fwd", grid=(T // TILE,),
        in_specs=_mix_in_specs(lambda s: s) + _mix_param_specs(),
        out_specs=[pl.BlockSpec((TILE, D), lambda s: (s, 0))], out_shape=[SDS((T, D), BF)],
        sem=("parallel",), vmem_mib=40, comm=comm, free=tuple(range(5, 12)))


def _mix_bwd(proj, dy, sinks, cw8, qg, kg, gco, gao, bias, comm=()):
    n_steps = T // TILE

    def tile_of(s):
        return n_steps - 1 - s

    def body(sink_ref, p_ref, zc_ref, zh_ref, pkv_ref, dy_ref, cw_ref, qg_ref, kg_ref, gco_ref, gao_ref, seg_ref, bias_ref,
             dproj_ref, dcw_ref, dqg_ref, dkg_ref, dgco_ref, dgao_ref, dsink_ref, dbias_ref,
             ndcz_ref, dkc_ref, dvc_ref):
        s = pl.program_id(0)

        @pl.when(s == 0)
        def _():
            for r in (dcw_ref, dqg_ref, dkg_ref, dgco_ref, dgao_ref, dsink_ref, dbias_ref, ndcz_ref, dkc_ref, dvc_ref):
                r[...] = jnp.zeros_like(r)

        params = (cw_ref[...], qg_ref[...], kg_ref[...], gco_ref[...], gao_ref[...], seg_ref[...])
        tile = p_ref[...]
        carry = (ndcz_ref[...], dkc_ref[...], dvc_ref[...])
        total = None
        for b in reversed(range(BPS)):
            f = _mix_forward(*_block_inputs(tile, b, zc_ref, zh_ref, pkv_ref, s == n_steps - 1), *params, sink_ref, bias_ref)
            pieces, sums, carry = one_block(f, dy_ref[b * BLK:(b + 1) * BLK, :], params, carry)
            for lo, piece in pieces:
                dproj_ref[b * BLK:(b + 1) * BLK, lo:lo + piece.shape[1]] = piece
            total = sums if total is None else [t + v for t, v in zip(total, sums)]
        ndcz_ref[...], dkc_ref[...], dvc_ref[...] = carry
        dcw, dqg_t, dkg_t, dgco, dgao, dsink, *ds = total
        dcw_ref[0:3, :] += dcw
        dqg_ref[...] += _fold_heads(dqg_t)
        dkg_ref[...] += _fold_heads(dkg_t)
        dgco_ref[...] += dgco
        dgao_ref[...] += dgao
        dsink_ref[...] += dsink
        for h in range(NH):
            dbias_ref[h * BLK:(h + 1) * BLK, :] += ds[h]

    def one_block(f, dy, params, carry):
        cw, qg_v, kg_v, gco_v, gao_v, seg = params
        nxt, dk_carry, dv_carry = carry
        dyc, dgco = _rms_bwd(dy[:, 0:CW], f["y_conv"], f["rc"], gco_v)
        dya, dgao = _rms_bwd(dy[:, CW:CW + AW], f["y_attn"], f["ra"], gao_v)

        row = f["row"]
        dgate_b = dyc * f["cz"]
        dcz = dyc * f["gate_b"]
        dcw = jnp.concatenate([jnp.sum(dcz * f[k], axis=0, keepdims=True) for k in ("z2", "z1", "z")], axis=0)
        n0 = nxt[0:1, :]
        n1 = nxt[1:2, :]
        d1 = jnp.where(row == BLK - 1, n0, pltpu.roll(dcz, BLK - 1, 0))
        d2 = jnp.where(row == BLK - 1, n1, jnp.where(row == BLK - 2, n0, pltpu.roll(dcz, BLK - 2, 0)))
        dz = cw[2:3, :] * dcz + cw[1:2, :] * d1 + cw[0:1, :] * d2
        pieces = [(0, dgate_b.astype(BF)), (CW, (dz * f["hc"]).astype(BF)), (2 * CW, (dz * f["gate_c"]).astype(BF))]

        scale = f["scale"]
        lane = lax.broadcasted_iota(jnp.int32, (1, 128), 1)
        dsink = jnp.zeros((1, 128), F32)
        dq_cols, dk_cols, dv_cols, dk_prev, dv_prev, ds = [], [], [], [], [], []
        for kv in range(NKV):
            dKb = dVb = 0.0
            for h in range(kv * GQ, (kv + 1) * GQ):
                hd = f["heads"][h]
                dO = dya[:, h * HD:(h + 1) * HD]
                delta = jnp.sum(dO * hd["O"], axis=-1, keepdims=True)
                dOb = dO.astype(BF)
                dP = _dot(dOb, hd["vb"], 1, 1)
                dS = hd["probs"] * (dP - delta)
                tot = jnp.sum(hd["psink"] * delta, axis=0, keepdims=True)
                dsink = dsink - jnp.where(lane == h, tot, 0.0)
                ds.append(dS)
                dSb = dS.astype(BF)
                dq_cols.append(_dot(dSb, hd["kb"], 1, 0))
                dKb = dKb + _dot(dSb, hd["Q"], 0, 0)
                dVb = dVb + _dot(hd["probs"].astype(BF), dOb, 0, 0)
            dk_cols.append(dKb[BLK:, :] + dk_carry[:, kv * HD:(kv + 1) * HD])
            dv_cols.append(dVb[BLK:, :] + dv_carry[:, kv * HD:(kv + 1) * HD])
            dk_prev.append(dKb[:BLK, :])
            dv_prev.append(dVb[:BLK, :])
        dq_raw, dqg_t = _head_norm_bwd(jnp.concatenate(dq_cols, axis=1) * scale, f["q_raw"], f["rq"], qg_v, seg)
        dk_raw, dkg_t = _head_norm_bwd(jnp.concatenate(dk_cols, axis=1), f["k_raw"][BLK:, :], f["rk"][BLK:, :], kg_v, seg)
        pieces.append((3 * CW, jnp.concatenate([dq_raw, dk_raw] + dv_cols, axis=1).astype(BF)))
        owed = (dcz[0:8, :], jnp.concatenate(dk_prev, axis=1), jnp.concatenate(dv_prev, axis=1))
        return pieces, [dcw, dqg_t, dkg_t, dgco, dgao, dsink, *ds], owed

    small = lambda r, c: pl.BlockSpec((r, c), lambda s: (0, 0))
    return _call(
        body, (sinks, proj, proj, proj, proj, dy, *_mix_params(cw8, qg, kg, gco, gao, bias)), name="mix_bwd", grid=(n_steps,),
        in_specs=_mix_in_specs(tile_of) + [pl.BlockSpec((TILE, D), lambda s: (tile_of(s), 0))] + _mix_param_specs(),
        out_specs=[pl.BlockSpec((TILE, INW), lambda s: (tile_of(s), 0)), small(8, CW), small(1, HD), small(1, HD),
                   small(1, CW), small(1, AW), small(1, 128), small(NH * BLK, 2 * BLK)],
        out_shape=[SDS((T, INW), BF), SDS((8, CW), F32), SDS((1, HD), F32), SDS((1, HD), F32), SDS((1, CW), F32),
                   SDS((1, AW), F32), SDS((1, 128), F32), SDS((NH * BLK, 2 * BLK), F32)],
        scratch_shapes=[pltpu.VMEM((8, CW), F32), pltpu.VMEM((BLK, NKV * HD), F32), pltpu.VMEM((BLK, NKV * HD), F32)],
        sem=("arbitrary",), vmem_mib=56, comm=comm, free=(1, 2, 3, 4) + tuple(range(6, 13)))


FT = 256
NFT = DFF // FT
RC = 1024
NCH = T // RC
LEAD = 16


def _rows8(x):
    return jnp.sum(x.reshape(x.shape[0] // 8, 8, x.shape[1]), axis=0)


def _ffn_act_specs():
    return [
        pl.BlockSpec((T, FT), lambda j: (0, j)), pl.BlockSpec((T, FT), lambda j: (0, NFT + j)),
        pl.BlockSpec((8, FT), lambda j: (0, j)), pl.BlockSpec((8, FT), lambda j: (0, NFT + j)),
        pl.BlockSpec((1, FT), lambda j: (0, j)), pl.BlockSpec((1, FT), lambda j: (0, NFT + j)),
    ]


def _conv_rows(win, w, b, n):
    win = win.astype(F32)
    u = win[LEAD:LEAD + n]
    u1 = pltpu.roll(win, 1, 0)[LEAD:LEAD + n]
    u2 = pltpu.roll(win, 2, 0)[LEAD:LEAD + n]
    return u2, u1, u, w[0:1, :] * u2 + w[1:2, :] * u1 + w[2:3, :] * u + b


def _ffn_act(up, fw8, fb, comm=()):
    def body(ug_ref, uv_ref, wg_ref, wv_ref, bg_ref, bv_ref, a_ref, pg_ref, pv_ref):
        wg, wv, bg, bv = wg_ref[...], wv_ref[...], bg_ref[...], bv_ref[...]

        def chunk(rows, win_g, win_v):
            gp = _conv_rows(win_g, wg, bg, RC)[3]
            vp = _conv_rows(win_v, wv, bv, RC)[3]
            a_ref[rows, :] = (gp * jax.nn.sigmoid(gp) * vp).astype(BF)
            pg_ref[0, rows, :] = gp.astype(BF)
            pv_ref[0, rows, :] = vp.astype(BF)

        zero = jnp.zeros((LEAD, FT), BF)
        chunk(pl.ds(0, RC), jnp.concatenate([zero, ug_ref[0:RC, :]], axis=0), jnp.concatenate([zero, uv_ref[0:RC, :]], axis=0))

        def step(i, carry):
            r0 = pl.multiple_of(i * RC, RC)
            win = pl.ds(r0 - LEAD, RC + LEAD)
            chunk(pl.ds(r0, RC), ug_ref[win, :], uv_ref[win, :])
            return carry

        lax.fori_loop(1, NCH, step, 0)

    tile = pl.BlockSpec((1, T, FT), lambda j: (j, 0, 0))
    return _call(
        body, (up, up, fw8, fw8, fb, fb), name="ffn_act", grid=(NFT,), in_specs=_ffn_act_specs(),
        out_specs=[pl.BlockSpec((T, FT), lambda j: (0, j)), tile, tile],
        out_shape=[SDS((T, DFF), BF), SDS((NFT, T, FT), BF), SDS((NFT, T, FT), BF)],
        sem=("parallel",), vmem_mib=40, comm=comm, free=(2, 3, 4, 5))


def _ffn_act_bwd(up, pre_g, pre_v, da, fw8, comm=()):
    ext = RC + LEAD

    def body(ug_ref, uv_ref, wg_ref, wv_ref, pg_ref, pv_ref, da_ref, dug_ref, duv_ref, dwg_ref, dwv_ref, dbg_ref, dbv_ref):
        wg, wv = wg_ref[...], wv_ref[...]

        def chunk(u_g, u_v, gp, vp, da_e):
            gp, vp, da_e = gp.astype(F32), vp.astype(F32), da_e.astype(F32)
            sig = jax.nn.sigmoid(gp)
            dvp = da_e * (gp * sig)
            dgp = da_e * vp * (sig * (1.0 + gp * (1.0 - sig)))

            def branch(dp, w, u):
                d0, d1, d2 = dp[0:RC], pltpu.roll(dp, ext - 1, 0)[0:RC], pltpu.roll(dp, ext - 2, 0)[0:RC]
                du = (w[2:3, :] * d0 + w[1:2, :] * d1 + w[0:1, :] * d2).astype(BF)
                u = u.astype(F32)
                return du, [_rows8(d0), _rows8(d2 * u), _rows8(d1 * u), _rows8(d0 * u)]

            dug, sums_g = branch(dgp, wg, u_g)
            duv, sums_v = branch(dvp, wv, u_v)
            return dug, duv, sums_g + sums_v

        def step(i, acc):
            r0 = pl.multiple_of(i * RC, RC)
            rows, more = pl.ds(r0, RC), pl.ds(r0, ext)
            dug, duv, part = chunk(ug_ref[rows, :], uv_ref[rows, :], pg_ref[0, more, :], pv_ref[0, more, :], da_ref[more, :])
            dug_ref[rows, :] = dug
            duv_ref[rows, :] = duv
            return [a + p for a, p in zip(acc, part)]

        acc = lax.fori_loop(0, NCH - 1, step, [jnp.zeros((8, FT), F32)] * 8)
        r0 = T - RC
        zero = jnp.zeros((LEAD, FT), BF)
        tail = lambda rows: jnp.concatenate([rows, zero], axis=0)
        dug, duv, part = chunk(ug_ref[r0:T, :], uv_ref[r0:T, :], tail(pg_ref[0, r0:T, :]), tail(pv_ref[0, r0:T, :]),
                               tail(da_ref[r0:T, :]))
        dug_ref[r0:T, :] = dug
        duv_ref[r0:T, :] = duv
        tot = [jnp.sum(a + p, axis=0, keepdims=True) for a, p in zip(acc, part)]
        for k, (dw_ref, db_ref) in enumerate(((dwg_ref, dbg_ref), (dwv_ref, dbv_ref))):
            db_ref[...] = tot[4 * k]
            dw_ref[...] = jnp.zeros_like(dw_ref)
            for r in range(3):
                dw_ref[r:r + 1, :] = tot[4 * k + 1 + r]

    col = lambda r: pl.BlockSpec((r, FT), lambda j: (0, j))
    return _call(
        body, (up, up, fw8, fw8, pre_g, pre_v, da), name="ffn_act_bwd", grid=(NFT,),
        in_specs=_ffn_act_specs()[0:4] + [pl.BlockSpec((1, T, FT), lambda j: (j, 0, 0))] * 2 + [col(T)],
        out_specs=[col(T), col(T), col(8), col(8), col(1), col(1)],
        out_shape=[SDS((T, DFF), BF), SDS((T, DFF), BF), SDS((8, DFF), F32), SDS((8, DFF), F32),
                   SDS((1, DFF), F32), SDS((1, DFF), F32)],
        sem=("parallel",), vmem_mib=40, comm=comm, free=(0, 1, 2, 3))


def _ffn_down_bwd(dh2b, w_down, comm=()):
    tm = TM

    def body(d_ref, w_ref, o_ref):
        o_ref[...] = _dot(d_ref[...], w_ref[...], 1, 1).astype(BF)

    return _call(
        body, (dh2b, w_down), name="ffn_down_bwd", grid=(T // tm,),
        in_specs=[pl.BlockSpec((tm, D), lambda i: (i, 0)), _resident((DFF, D))],
        out_specs=[pl.BlockSpec((tm, DFF), lambda i: (i, 0))], out_shape=[SDS((T, DFF), BF)],
        sem=("parallel",), vmem_mib=40, comm=comm, free=(0, 1))


def _norm_matmul_bwd(name, a_list, w_t, k_offsets, xin, g, dres, want_bf16, comm=(), slot=None, band=(), pack=()):
    tm = TM
    ks = [a.shape[1] for a in a_list]
    n_a = len(a_list)
    n_pre = 0 if slot is None else 1
    n_in = n_a + 4 + len(band) + len(pack)

    def body(*refs):
        refs = refs[n_pre:]
        a_refs = refs[:n_a]
        w_ref, x_ref, g_ref, r_ref = refs[n_a:n_a + 4]
        outs = refs[n_in:]
        dg_out = outs[1 + want_bf16]
        dx_ref, dg_ref = outs[0], (dg_out if slot is None else dg_out.at[0])

        @pl.when(pl.program_id(0) == 0)
        def _():
            dg_ref[...] = jnp.zeros_like(dg_ref)
            if band:
                db_ref, bk_ref, tbl_ref = refs[n_a + 4], refs[n_a + 5], outs[2 + want_bf16]
                bk = bk_ref[...]
                for b in range(NBUCKET):
                    m = bk == b
                    for h in range(NH):
                        v = jnp.where(m, db_ref[h * BLK:(h + 1) * BLK, :], 0.0)
                        tbl_ref[0, h:h + 1, b:b + 1] = jnp.sum(jnp.sum(v, axis=1, keepdims=True), axis=0, keepdims=True)
            if pack:
                pm_ref = outs[2 + want_bf16 + bool(band)]
                pm_ref[...] = jnp.zeros_like(pm_ref)
                _fill_mix(pm_ref, *refs[n_in - len(pack):n_in])

        du = _dot(a_refs[0][...], w_ref[k_offsets[0]:k_offsets[0] + ks[0], :], 1, 0)
        for k in range(1, n_a):
            du = du + _dot(a_refs[k][...], w_ref[k_offsets[k]:k_offsets[k] + ks[k], :], 1, 0)
        x = x_ref[...]
        r = lax.rsqrt(jnp.mean(x * x, axis=-1, keepdims=True) + EPS)
        dx, dg = _rms_bwd(du, x, r, g_ref[...])
        dx = r_ref[...] + dx
        dx_ref[...] = dx
        if want_bf16:
            outs[1][...] = dx.astype(BF)
        dg_ref[...] += dg

    tile = lambda c: pl.BlockSpec((tm, c), lambda i, *_: (i, 0))
    if slot is None:
        dg_spec, dg_shape = pl.BlockSpec((1, D), lambda i: (0, 0)), SDS((1, D), F32)
    else:
        dg_spec, dg_shape = pl.BlockSpec((1, 1, D), lambda i, slot_ref: (slot_ref[0], 0, 0)), SDS((N_DEV, 1, D), F32)
    out_specs = [tile(D)] + ([tile(D)] if want_bf16 else []) + [dg_spec]
    out_shape = [SDS((T, D), F32)] + ([SDS((T, D), BF)] if want_bf16 else []) + [dg_shape]
    if band:
        out_specs.append(pl.BlockSpec((1, NH, NBUCKET), lambda i, slot_ref: (slot_ref[0], 0, 0)))
        out_shape.append(SDS((N_DEV, NH, NBUCKET), F32))
    if pack:
        out_specs.append(pl.BlockSpec((1, 8, D), lambda i, slot_ref: (slot_ref[0], 0, 0)))
        out_shape.append(SDS((N_DEV, 8, D), F32))
    return _call(
        body, (*a_list, w_t, xin, g, dres, *band, *pack), name=name, grid=(T // tm,), prefetch=() if slot is None else (slot,),
        in_specs=[tile(k) for k in ks] + [_resident(w_t.shape), tile(D), pl.BlockSpec((1, D), lambda i, *_: (0, 0)), tile(D)]
        + [pl.BlockSpec(b.shape, lambda i, *_: (0, 0)) for b in (*band, *pack)],
        out_specs=out_specs, out_shape=out_shape, sem=("arbitrary",), vmem_mib=56, comm=comm, free=tuple(range(n_a + 4)))


def _out_bwd(dh1b, w_out, comm=()):
    tm = TM

    def body(d_ref, w_ref, o_ref):
        o_ref[...] = _dot(d_ref[...], w_ref[...], 1, 1)

    return _call(
        body, (dh1b, w_out), name="out_bwd", grid=(T // tm,),
        in_specs=[pl.BlockSpec((tm, D), lambda i: (i, 0)), _resident((D, D))],
        out_specs=[pl.BlockSpec((tm, D), lambda i: (i, 0))], out_shape=[SDS((T, D), F32)],
        sem=("parallel",), vmem_mib=32, comm=comm, free=(0, 1))


def _wgrad(name, a_list, b, old_a, comm=()):
    m_k = a_list[0].shape[1]
    tm = max(t for t in range(128, m_k // 2 + 1, 128) if m_k % t == 0)
    steps = [a.shape[1] // tm for a in a_list]
    starts = [sum(steps[:k]) for k in range(len(a_list))]
    n_a = len(a_list)

    def body(*refs):
        a_refs, b_ref, o_ref = refs[:n_a], refs[n_a], refs[n_a + 1]
        i = pl.program_id(0)
        for k in range(n_a):
            @pl.when((i >= starts[k]) & (i < starts[k] + steps[k]))
            def _(k=k):
                o_ref[...] = _dot(a_refs[k][...], b_ref[...], 0, 0).astype(BF)

    def a_spec(k):
        return pl.BlockSpec((T, tm), lambda i: (0, jnp.clip(i - starts[k], 0, steps[k] - 1)))

    m_total = tm * sum(steps)
    return _call(
        body, (*a_list, b), name=name, grid=(sum(steps),),
        in_specs=[a_spec(k) for k in range(n_a)] + [_resident((T, D))],
        out_specs=[pl.BlockSpec((tm, D), lambda i: (i, 0))], out_shape=[SDS((m_total, D), BF)],
        sem=("parallel",), vmem_mib=40, comm=comm, free=() if old_a is None else tuple(range(n_a)) if old_a else (n_a,))


def _chip_sum(name, gbf, from_sib, core, chip):
    h = gbf.shape[1]
    th = h

    def body(core_ref, chip_ref, g_ref, s_ref, pbf_ref, own_ref):
        p = g_ref[0].astype(F32) + s_ref[0].astype(F32)
        pbf_ref[0] = p.astype(BF)

        @pl.when(pl.program_id(1) == chip_ref[0])
        def _():
            own_ref[...] = p

    grid_spec = pltpu.PrefetchScalarGridSpec(
        num_scalar_prefetch=2, grid=(h // th, N_CHIPS),
        in_specs=[pl.BlockSpec((1, th, D), lambda t, jj, core_ref, chip_ref: (2 * jj + core_ref[0], t, 0)),
                  pl.BlockSpec((1, th, D), lambda t, jj, core_ref, chip_ref: (jj, t, 0))],
        out_specs=[pl.BlockSpec((1, th, D), lambda t, jj, core_ref, chip_ref: (jj, t, 0)),
                   pl.BlockSpec((th, D), lambda t, jj, core_ref, chip_ref: (t, 0))],
    )
    return _pcall(
        body, name=name, grid_spec=grid_spec, out_shape=_in_hbm([SDS((N_CHIPS, h, D), BF), SDS((h, D), F32)]),
        compiler_params=_params(("arbitrary", "arbitrary"), 32),
    )(core, chip, *_from_hbm(gbf, from_sib))


def _final_sum(name, own, from_chips, core, comm=()):
    h = own.shape[0]
    n = 4 if h % (4 * ROWS16) == 0 else 2
    th = h // n

    def body(core_ref, o_ref, r_ref, f_ref):
        f_ref[0] = ((o_ref[...] + r_ref[0].astype(F32)) + r_ref[1].astype(F32)) + r_ref[2].astype(F32)

    return _call(
        body, (own, from_chips), name=name, grid=(n,), prefetch=(core,),
        in_specs=[pl.BlockSpec((th, D), lambda i, core_ref: (i, 0)), pl.BlockSpec((3, th, D), lambda i, core_ref: (0, i, 0))],
        out_specs=[pl.BlockSpec((1, th, D), lambda i, core_ref: (core_ref[0], i, 0))], out_shape=[SDS((2, h, D), F32)],
        sem=("arbitrary",), vmem_mib=40, comm=comm)


def _adam_math(w, g, m, v):
    nm = ADAM_B1 * m + (1.0 - ADAM_B1) * g
    nv = ADAM_B2 * v + (1.0 - ADAM_B2) * (g * g)
    m_hat = nm / (1.0 - ADAM_B1 ** ADAM_STEP)
    v_hat = nv / (1.0 - ADAM_B2 ** ADAM_STEP)
    return -ADAM_LR * (m_hat / (jnp.sqrt(v_hat) + ADAM_EPS) + ADAM_WD * w), nm, nv


def _adamw(name, w, g, m, v, tr, copy_g=False, stage=True, g_transposed=False):
    rows, cols = w.shape

    def body(w_ref, g_ref, m_ref, v_ref, *outs):
        d_ref, nm_ref, nv_ref = outs[-3:]
        for c in [pl.ds(c0, 128) for c0 in range(0, cols, 128)] if g_transposed else [slice(None)]:
            g_val = g_ref[c, :].T if g_transposed else g_ref[...]
            if copy_g:
                outs[0][:, c] = g_val
            d_ref[:, c], nm_ref[:, c], nv_ref[:, c] = _adam_math(w_ref[:, c], g_val, m_ref[:, c], v_ref[:, c])

    spec = pl.BlockSpec((tr, cols), lambda i: (i, 0))
    n_out = 4 if copy_g else 3
    g_spec = pl.BlockSpec((cols, tr), lambda i: (0, i)) if g_transposed else spec
    return _call(body, (w, g, m, v), name=name, grid=(rows // tr,), in_specs=[spec, g_spec, spec, spec], out_specs=[spec] * n_out,
                 out_shape=[SDS((rows, cols), F32)] * n_out, sem=("parallel",), vmem_mib=32,
                 free=(0, 2, 3) if stage else ())


C_SQ = 2 * DFF
P_W = C_SQ + 128
R_G2, R_GO, R_DCW, R_QK = 0, 1, 2, 5
C_GCO, C_GAO, C_DQG, C_DKG, C_SINK = 0, CW, 0, 128, 256


def _pack(name, me, ins, width, fill):
    def body(me_ref, *refs):
        o = refs[-1]
        o[...] = jnp.zeros_like(o)
        fill(o, *refs[:-1])

    return _call(body, ins, name=name, grid=(1,), prefetch=(me,),
                 in_specs=[pl.BlockSpec(a.shape, lambda i, me_ref: (0, 0)) for a in ins],
                 out_specs=[pl.BlockSpec((1, 8, width), lambda i, me_ref: (me_ref[0], 0, 0))],
                 out_shape=[SDS((N_DEV, 8, width), F32)], sem=("arbitrary",))[0]


def _pack_ffn(me, dfwg, dfwv, dfbg, dfbv, sq):
    def fill(o, dfwg_r, dfwv_r, dfbg_r, dfbv_r, sq_r):
        o[0, :, 0:DFF] = dfwg_r[...]
        o[0, :, DFF:2 * DFF] = dfwv_r[...]
        o[0, 3:4, 0:DFF] = dfbg_r[...]
        o[0, 3:4, DFF:2 * DFF] = dfbv_r[...]
        o[0, :, C_SQ:C_SQ + 128] = sq_r[...]

    return _pack("pack_ffn", me, (dfwg, dfwv, dfbg, dfbv, sq), P_W, fill)


def _fill_mix(o, dg2_r, dgco_r, dgao_r, dcw_r, dqg_r, dkg_r, dsink_r):
    o[0, R_G2:R_G2 + 1, :] = dg2_r[...]
    o[0, R_GO:R_GO + 1, C_GCO:C_GCO + CW] = dgco_r[...]
    o[0, R_GO:R_GO + 1, C_GAO:C_GAO + AW] = dgao_r[...]
    o[0, R_DCW:R_DCW + 3, 0:CW] = dcw_r[0:3, :]
    o[0, R_QK:R_QK + 1, C_DQG:C_DQG + HD] = dqg_r[...]
    o[0, R_QK:R_QK + 1, C_DKG:C_DKG + HD] = dkg_r[...]
    o[0, R_QK:R_QK + 1, C_SINK:C_SINK + 128] = dsink_r[...]


N_SMALL = 11


def _small_adam(chip, p_all, pm_all, g1_all, tbl_all, ws, ms, vs):
    fw_cols = 2 * DFF // N_CHIPS
    cw_cols = CW // N_CHIPS

    def body(chip_ref, p_ref, fw_ref, pm_ref, cw_ref, g1_ref, tbl_ref, *refs):
        w_r, m_r, v_r = refs[0:N_SMALL], refs[N_SMALL:2 * N_SMALL], refs[2 * N_SMALL:3 * N_SMALL]
        outs = refs[3 * N_SMALL:]
        g_o, d_o, nm_o, nv_o = (outs[k * N_SMALL:(k + 1) * N_SMALL] for k in range(4))
        loss_o = outs[4 * N_SMALL]

        def total(ref):
            s = ref[0]
            for k in range(1, N_DEV):
                s = s + ref[k]
            return s

        S = total(p_ref)
        fw = total(fw_ref)
        M = total(pm_ref)
        cw = total(cw_ref)

        def step(i, g, at):
            d, nm, nv = _adam_math(w_r[i][at], g, m_r[i][at], v_r[i][at])
            g_o[i][at], d_o[i][at], nm_o[i][at], nv_o[i][at] = g, d, nm, nv

        everything = (slice(None), slice(None))
        step(0, total(g1_ref), everything)
        for r in range(3):
            step(1, cw[R_DCW + r:R_DCW + r + 1, :], (r, slice(None), slice(None)))
        step(2, M[R_QK:R_QK + 1, C_DQG:C_DQG + HD], everything)
        step(3, M[R_QK:R_QK + 1, C_DKG:C_DKG + HD], everything)
        step(4, total(tbl_ref), everything)
        step(5, M[R_QK:R_QK + 1, C_SINK:C_SINK + NH], everything)
        step(6, M[R_GO:R_GO + 1, C_GCO:C_GCO + CW], everything)
        step(7, M[R_GO:R_GO + 1, C_GAO:C_GAO + AW], everything)
        step(8, M[R_G2:R_G2 + 1, :], everything)
        for r in range(3):
            step(9, fw[r:r + 1, :], (r, slice(None), slice(None)))
        step(10, S[3:4, 0:2 * DFF], everything)
        sq = S[:, C_SQ:C_SQ + 128]
        loss_o[...] = jnp.sum(jnp.sum(sq, axis=1, keepdims=True), axis=0, keepdims=True) * (0.5 / D)

    def full(a):
        n = len(a.shape)
        return pl.BlockSpec(a.shape, lambda i, chip_ref: (0,) * n)

    params = [*ws, *ms, *vs]
    out = _call(
        body, (p_all, p_all, pm_all, pm_all, g1_all, tbl_all, *params), name="small_adam", grid=(1,), prefetch=(chip,),
        in_specs=[full(p_all),
                  pl.BlockSpec((N_DEV, 8, fw_cols), lambda i, chip_ref: (0, 0, chip_ref[0])),
                  full(pm_all),
                  pl.BlockSpec((N_DEV, 8, cw_cols), lambda i, chip_ref: (0, 0, chip_ref[0])),
                  full(g1_all), full(tbl_all), *[full(a) for a in params]],
        out_specs=[full(a) for a in ws] * 4 + [pl.BlockSpec((1, 1), lambda i, chip_ref: (0, 0))],
        out_shape=[SDS(a.shape, F32) for a in ws] * 4 + [SDS((1, 1), F32)], sem=("arbitrary",), vmem_mib=32)
    return out[0:N_SMALL], out[N_SMALL:2 * N_SMALL], out[2 * N_SMALL:3 * N_SMALL], out[3 * N_SMALL:4 * N_SMALL], out[4 * N_SMALL]


PLACE_STEPS = 4


def _place_specs(shards):
    rows = [s.shape[0] // PLACE_STEPS for s in shards]
    return ([pl.BlockSpec((r, D), lambda i, chip_ref: (i, 0)) for r in rows],
            [pl.BlockSpec((r, D), lambda i, chip_ref: (chip_ref[0] * PLACE_STEPS + i, 0)) for r in rows],
            [SDS((N_CHIPS * s.shape[0], D), BF) for s in shards])


def _place_first(chip, shard, conv_w, ffn_conv_w):
    def body(chip_ref, a, s0, s1, o, t0, t1):
        o[...] = a[...].astype(BF)

        @pl.when(pl.program_id(0) == 0)
        def _():
            for s, t in ((s0, t0), (s1, t1)):
                t[...] = jnp.zeros_like(t)
                t[0, 0:3, :] = s[...]

    ins, outs, shapes = _place_specs([shard])
    taps = (conv_w, ffn_conv_w)
    return _call(
        body, (shard, conv_w, ffn_conv_w), name="place_first", grid=(PLACE_STEPS,), prefetch=(chip,),
        in_specs=ins + [pl.BlockSpec(s.shape, lambda i, chip_ref: (0, 0)) for s in taps],
        out_specs=outs + [pl.BlockSpec((1, 8, s.shape[1]), lambda i, chip_ref: (chip_ref[0], 0, 0)) for s in taps],
        out_shape=shapes + [SDS((N_CHIPS, 8, s.shape[1]), F32) for s in taps],
        sem=("arbitrary",), vmem_mib=32, free=(1, 2))


def _place_rest(chip, shards, w_up, table, bucket, comm):
    n = len(shards)
    c_up = w_up.shape[1]
    edges = [round(k * (c_up // 128) / PLACE_STEPS) * 128 for k in range(PLACE_STEPS + 1)]

    def body(chip_ref, *refs):
        a, (up_ref, tab_ref, bk_ref), o = refs[:n], refs[n:n + 3], refs[n + 3:2 * n + 3]
        up_o, bias_ref = refs[2 * n + 3:]
        for src, dst in zip(a, o):
            dst[...] = src[...].astype(BF)
        for k in range(PLACE_STEPS):
            @pl.when(pl.program_id(0) == k)
            def _(k=k):
                up_o[edges[k]:edges[k + 1], :] = up_ref[:, edges[k]:edges[k + 1]].T.astype(BF)

        @pl.when(pl.program_id(0) == 0)
        def _():
            bk = bk_ref[...]
            eq = [bk == b for b in range(NBUCKET)]
            for h in range(NH):
                acc = jnp.zeros((BLK, 2 * BLK), F32)
                for b in range(NBUCKET):
                    acc = jnp.where(eq[b], tab_ref[h, b], acc)
                bias_ref[h * BLK:(h + 1) * BLK, :] = acc

    ins, outs, shapes = _place_specs(shards)
    return _call(
        body, (*shards, w_up, table, bucket), name="place_rest", grid=(PLACE_STEPS,), prefetch=(chip,),
        in_specs=ins + [_resident(w_up.shape), pl.BlockSpec(memory_space=pltpu.SMEM),
                        pl.BlockSpec(bucket.shape, lambda i, chip_ref: (0, 0))],
        out_specs=outs + [pl.BlockSpec((c_up, D), lambda i, chip_ref: (chip_ref[0], 0)),
                          pl.BlockSpec((NH * BLK, 2 * BLK), lambda i, chip_ref: (0, 0))],
        out_shape=shapes + [SDS((N_CHIPS * c_up, D), BF), SDS((NH * BLK, 2 * BLK), F32)],
        sem=("arbitrary",), vmem_mib=32, comm=comm, free=(n + 1, n + 2))


def kernel(x, norm_mix_g, w_in, conv_w, q_norm_g, k_norm_g, rel_bias_table, sinks, out_norm_conv_g, out_norm_attn_g, w_out, norm_ffn_g, w_up, ffn_conv_w, ffn_conv_b, w_down, loss_target, m_norm_mix_g, m_w_in, m_conv_w, m_q_norm_g, m_k_norm_g, m_rel_bias_table, m_sinks, m_out_norm_conv_g, m_out_norm_attn_g, m_w_out, m_norm_ffn_g, m_w_up, m_ffn_conv_w, m_ffn_conv_b, m_w_down, v_norm_mix_g, v_w_in, v_conv_w, v_q_norm_g, v_k_norm_g, v_rel_bias_table, v_sinks, v_out_norm_conv_g, v_out_norm_attn_g, v_w_out, v_norm_ffn_g, v_w_up, v_ffn_conv_w, v_ffn_conv_b, v_w_down):
    as_arg = lambda i: jnp.reshape(i, (1,)).astype(jnp.int32)
    chip = as_arg(2 * lax.axis_index("x") + lax.axis_index("y"))
    core = as_arg(lax.axis_index("c"))
    me = 2 * chip + core
    xs, tgt = x[0], loss_target[0]
    qg, kg, gco, gao, g1, g2, fb = q_norm_g, k_norm_g, out_norm_conv_g, out_norm_attn_g, norm_mix_g, norm_ffn_g, ffn_conv_b
    pieces = lambda g: g.reshape(N_DEV, g.shape[0] // N_DEV, D)
    whole = lambda f: f.reshape(2 * f.shape[1], D)

    bucket = jnp.asarray(_bucket_table())
    p_in, p_cw, p_fw = _place_first(chip, w_in[0].T, conv_w[0], ffn_conv_w[0])
    p_out, p_down, p_up, bias, w_int, cw_all, fw_all = _place_rest(
        chip, [w_out[0], w_down[0]], w_up[0], rel_bias_table.T, bucket,
        comm=[_t_gather(p_in, relayed_first=True), _t_small_weights(p_cw), _t_small_weights(p_fw)])
    cw8 = jnp.transpose(cw_all, (1, 0, 2)).reshape(8, CW)
    fw8 = jnp.transpose(fw_all, (1, 0, 2)).reshape(8, 2 * DFF)

    early = 3 / 11
    proj, u1, w_out_f, p_up = _inproj(xs, g1, w_int, comm=[_t_gather(p_out), _t_gather(p_up, (0, early))])
    y, w_upt = _mix_fwd(proj, sinks, cw8, qg, kg, gco, gao, bias, comm=[_t_gather(p_up, (early, 1))])
    h1, u2 = _outproj(y, w_out_f, xs, g2)
    up, p_down = _ffn_up(u2, w_upt, comm=[_t_gather(p_down, (0, 0.5))])
    a, pre_g, pre_v, w_down_f = _ffn_act(up, fw8, fb, comm=[_t_gather(p_down, (0.5, 1))])
    dh2, dh2b, sq = _ffn_down(a, w_down_f, h1, tgt)

    gdbf, = _wgrad("wgrad_down", [a], dh2b, None)
    da, sib_down = _ffn_down_bwd(dh2b, w_down_f, comm=[_t_sibling(pieces(gdbf))])
    pbf_down, own_down = _chip_sum("chip_sum_w_down", pieces(gdbf), sib_down, core, chip)
    dug, duv, dfwg, dfwv, dfbg, dfbv, chips_down = _ffn_act_bwd(up, pre_g, pre_v, da, fw8, comm=[_t_chips(pbf_down)])
    fin_down, = _final_sum("final_sum_w_down", own_down, chips_down, core)
    gubf, = _wgrad("wgrad_up", [dug, duv], u2, False)
    p_all = _pack_ffn(me, dfwg, dfwv, dfbg, dfbv, sq)
    dh1, dh1b, dg2, sib_up, fin_down, p_all = _norm_matmul_bwd(
        "ffn_up_bwd", [dug, duv], w_upt, [0, DFF], h1, g2, dh2, True,
        comm=[_t_sibling(pieces(gubf)), _t_swap(fin_down), _t_allgather(p_all)])
    pbf_up, own_up = _chip_sum("chip_sum_w_up", pieces(gubf), sib_up, core, chip)
    gobf, = _wgrad("wgrad_out", [y], dh1b, True)
    dy, sib_out = _out_bwd(dh1b, w_out_f, comm=[_t_sibling(pieces(gobf))])
    pbf_out, own_out = _chip_sum("chip_sum_w_out", pieces(gobf), sib_out, core, chip)
    dproj, dcw8, dqg, dkg, dgco, dgao, dsink, dbias, chips_up = _mix_bwd(
        proj, dy, sinks, cw8, qg, kg, gco, gao, bias, comm=[_t_chips(pbf_up)])
    fin_up, = _final_sum("final_sum_w_up", own_up, chips_up, core)
    gibf, chips_out, fin_up = _wgrad("wgrad_in", [dproj], u1, False, comm=[_t_chips(pbf_out), _t_swap(fin_up)])
    fin_out, sib_in = _final_sum("final_sum_w_out", own_out, chips_out, core, comm=[_t_sibling(pieces(gibf))])
    pbf_in, own_in = _chip_sum("chip_sum_w_in", pieces(gibf), sib_in, core, chip)
    dx, g1_all, tbl_all, pm_all, chips_in, fin_out = _norm_matmul_bwd(
        "in_bwd", [dproj], w_int, [0], xs, g1, dh1, False, comm=[_t_chips(pbf_in), _t_swap(fin_out)], slot=me,
        band=(dbias, bucket), pack=(dg2, dgco, dgao, dcw8, dqg, dkg, dsink))
    fin_in, = _final_sum("final_sum_w_in", own_in, chips_in, core)
    g1_all, tbl_all, pm_all, fin_in = _comm_call(
        "gather_last", [_t_allgather(g1_all), _t_allgather(tbl_all), _t_allgather(pm_all), _t_swap(fin_in)])

    g_w_out, g_w_down = whole(fin_out), whole(fin_down)
    g_w_down, d_down, nm_down, nv_down = _adamw("adamw_w_down", w_down[0], g_w_down, m_w_down[0], v_w_down[0], 352, True)
    g_w_up, d_up, nm_up, nv_up = _adamw(
        "adamw_w_up", w_up[0], whole(fin_up), m_w_up[0], v_w_up[0], 256, True, stage=False, g_transposed=True)
    g_w_out, d_out, nm_out, nv_out = _adamw("adamw_w_out", w_out[0], g_w_out, m_w_out[0], v_w_out[0], 256, True, stage=False)
    g_w_in, d_in, nm_in, nv_in = [a.T for a in _adamw(
        "adamw_w_in", w_in[0].T, whole(fin_in), m_w_in[0].T, v_w_in[0].T, INW // N_CHIPS // 3, True, stage=False)]
    taps = lambda a: jnp.transpose(a, (1, 0, 2))
    sw = [norm_mix_g, taps(conv_w), q_norm_g, k_norm_g, rel_bias_table.T, sinks, out_norm_conv_g, out_norm_attn_g,
          norm_ffn_g, taps(ffn_conv_w), ffn_conv_b]
    smm = [m_norm_mix_g, taps(m_conv_w), m_q_norm_g, m_k_norm_g, m_rel_bias_table.T, m_sinks, m_out_norm_conv_g,
           m_out_norm_attn_g, m_norm_ffn_g, taps(m_ffn_conv_w), m_ffn_conv_b]
    smv = [v_norm_mix_g, taps(v_conv_w), v_q_norm_g, v_k_norm_g, v_rel_bias_table.T, v_sinks, v_out_norm_conv_g,
           v_out_norm_attn_g, v_norm_ffn_g, taps(v_ffn_conv_w), v_ffn_conv_b]
    *small_out, loss = _small_adam(chip, p_all, pm_all, g1_all, tbl_all, sw, smm, smv)
    sg, sd, snm, snv = [list(r) for r in small_out]
    for r in (sg, sd, snm, snv):
        r[1], r[4], r[9] = taps(r[1]), r[4].T, taps(r[9])

    def order(s, b_in, b_out, b_up, b_down):
        return (s[0], b_in[None], s[1], s[2], s[3], s[4], s[5], s[6], s[7], b_out[None], s[8], b_up[None],
                s[9], s[10], b_down[None])

    return (loss.reshape(()), dx[None],
            *order(sg, g_w_in, g_w_out, g_w_up, g_w_down),
            *order(sd, d_in, d_out, d_up, d_down),
            *order(snm, nm_in, nm_out, nm_up, nm_down),
            *order(snv, nv_in, nv_out, nv_up, nv_down))
```

```python
import functools
import math

import numpy as np

import jax
import jax.numpy as jnp
from jax import lax
from jax.experimental import pallas as pl
from jax.experimental.pallas import tpu as pltpu

F32 = jnp.float32
BF = jnp.bfloat16
SDS = jax.ShapeDtypeStruct

T = 2048
D = 1024
CW = 512
AW = 512
HD = 64
NH = 8
NKV = 2
GQ = 4
INW = 2304
DFF = 2816
BLK = 128
NB = T // BLK
NBUCKET = 32
EPS = 1e-6
NEG_INF = -1e30
N_CHIPS = 4
N_DEV = 8

ADAM_LR = 0.001
ADAM_B1 = 0.9
ADAM_B2 = 0.999
ADAM_EPS = 1e-08
ADAM_WD = 0.01
ADAM_STEP = 10

TM = 512
MIB = 1024 * 1024
MESH = pl.DeviceIdType.MESH
ANY = pl.BlockSpec(memory_space=pl.ANY)

_pcall = pl.pallas_call


def _params(sem=None, vmem_mib=None, collective_id=None):
    kw = {} if collective_id is None else {"collective_id": collective_id}
    if sem is not None:
        kw["dimension_semantics"] = sem
    if vmem_mib is not None:
        kw["vmem_limit_bytes"] = vmem_mib * MIB
    return pltpu.CompilerParams(**kw)


def _resident(shape):
    return pl.BlockSpec(shape, lambda *_: (0,) * len(shape), pipeline_mode=pl.Buffered(1))


def _dot(a, b, ca, cb):
    return lax.dot_general(a, b, (((ca,), (cb,)), ((), ())), preferred_element_type=F32)


def _rms_bwd(dy, x, r, g):
    dg = jnp.sum(dy * (x * r), axis=0, keepdims=True)
    dgx = dy * g
    dx = r * dgx - x * (r * r * r) * jnp.mean(x * dgx, axis=-1, keepdims=True)
    return dx, dg


def _where():
    x, y, c = lax.axis_index("x"), lax.axis_index("y"), lax.axis_index("c")
    return x, y, c, [(1 - x, y), (x, 1 - y), (1 - x, 1 - y)]


def _rcopy(src, dst, ssem, rsem, dev):
    return pltpu.make_async_remote_copy(src_ref=src, dst_ref=dst, send_sem=ssem, recv_sem=rsem, device_id=dev,
                                        device_id_type=MESH)


SIBLING, Y_CHIP, X_CHIP, DIAGONAL_CHIP = 1, 2, 4, 6
OTHER_CHIPS = (Y_CHIP, X_CHIP, DIAGONAL_CHIP)
EVERYONE = tuple(range(1, N_DEV))
BARRIER_OF = {(SIBLING,): 0, (SIBLING, Y_CHIP, X_CHIP): 1, OTHER_CHIPS: 2, (SIBLING,) + OTHER_CHIPS: 3, EVERYONE: 4}


def _peer(rel):
    x, y, c, _ = _where()
    return x ^ ((rel >> 2) & 1), y ^ ((rel >> 1) & 1), c ^ (rel & 1)


class _Task:
    def __init__(self, ins, outs, alias, n_sem, start, finish, middle=None, peers=()):
        self.ins, self.outs, self.alias, self.n_sem, self.start, self.finish = ins, outs, alias, n_sem, start, finish
        self.middle = middle if middle is not None else (lambda *args: None)
        self.peers = peers


def _peers_of(comm):
    return tuple(sorted({p for t in comm for p in t.peers}))


def _enter(comm):
    peers = _peers_of(comm)
    barrier = pltpu.get_barrier_semaphore()
    for rel in peers:
        pl.semaphore_signal(barrier, inc=1, device_id=_peer(rel), device_id_type=MESH)
    pl.semaphore_wait(barrier, len(peers))


ROWS16 = 16


def _t_gather(placed, part=(0, 1), relayed_first=False):
    R = placed.shape[0] // N_CHIPS
    q = R // 4
    lo, hi = (round(f * (q // ROWS16)) * ROWS16 for f in part)

    def quarter(chip_index, core, k):
        return pl.ds(pl.multiple_of(chip_index * R + core * 2 * q + k * q + lo, ROWS16), hi - lo)

    def places():
        x, y, c, _ = _where()
        return c, 2 * x + y, 2 * (1 - x) + y, 2 * x + (1 - y), 2 * (1 - x) + (1 - y), (1 - x, y, c), (x, 1 - y, c), (x, y, 1 - c)

    def copy(buf, k, chip_index, core, quart, ss, rs, b, dev):
        window = buf.at[quarter(chip_index, core, quart)]
        return _rcopy(window, window, ss.at[b + k], rs.at[b + k], dev)

    def first_hop(cout, ss, rs, b, which):
        c, me, _, _, _, x_nbr, y_nbr, _ = places()
        for k, (quart, dev) in enumerate(((0, x_nbr), (1, y_nbr), (1, x_nbr), (0, y_nbr))):
            if k in which:
                copy(cout[0], k, me, c, quart, ss, rs, b, dev).start()

    def start(cin, cout, ss, rs, b):
        first_hop(cout, ss, rs, b, (0, 1) if relayed_first else (0, 1, 2, 3))

    def middle(cin, cout, ss, rs, b):
        c, _, xc, yc, _, x_nbr, y_nbr, sib = places()
        for k, chip_index, quart, dev in ((0, xc, 0, y_nbr), (1, yc, 1, x_nbr)):
            copy(cout[0], k, chip_index, c, quart, ss, rs, b, dev).wait_recv()
            copy(cout[0], 4 + k, chip_index, c, quart, ss, rs, b, dev).start()
            copy(cout[0], 6 + k, chip_index, c, quart, ss, rs, b, sib).start()
        if relayed_first:
            first_hop(cout, ss, rs, b, (2, 3))

    later = ((2, 1, 1), (3, 2, 0), (4, 3, 0), (5, 3, 1))

    def finish(cin, cout, ss, rs, b):
        c, me, xc, yc, dc, _, _, sib = places()
        chip_of = {1: xc, 2: yc, 3: dc}
        for k, whose, quart in later:
            copy(cout[0], k, chip_of[whose], c, quart, ss, rs, b, sib).wait_recv()
            copy(cout[0], 6 + k, chip_of[whose], c, quart, ss, rs, b, sib).start()
        for k, whose, quart in ((0, 1, 0), (1, 2, 1)) + later:
            copy(cout[0], 6 + k, chip_of[whose], 1 - c, quart, ss, rs, b, sib).wait_recv()
        for k in range(12):
            copy(cout[0], k, me, c, 0, ss, rs, b, sib).wait_send()

    return _Task([placed], [SDS(placed.shape, placed.dtype)], [(0, 0)], 12, start, finish, middle, peers=(SIBLING, Y_CHIP, X_CHIP))


def _t_small_weights(buf):
    def start(cin, cout, ss, rs, b):
        x, y, c, chips = _where()
        mine = cout[0].at[2 * x + y]
        for r, (px, py) in enumerate(chips):
            _rcopy(mine, mine, ss.at[b + r], rs.at[b + r], (px, py, c)).start()

    def finish(cin, cout, ss, rs, b):
        x, y, c, chips = _where()
        for r, (px, py) in enumerate(chips):
            got = cout[0].at[2 * px + py]
            _rcopy(got, got, ss.at[b + r], rs.at[b + r], (px, py, c)).wait_recv()
        for r, (px, py) in enumerate(chips):
            mine = cout[0].at[2 * x + y]
            _rcopy(mine, mine, ss.at[b + r], rs.at[b + r], (px, py, c)).wait_send()

    return _Task([buf], [SDS(buf.shape, buf.dtype)], [(0, 0)], 3, start, finish, peers=OTHER_CHIPS)


def _t_sibling(gbf):
    def start(cin, cout, ss, rs, b):
        x, y, c, _ = _where()
        for jj in range(N_CHIPS):
            _rcopy(cin[0].at[2 * jj + (1 - c)], cout[0].at[jj], ss.at[b + jj], rs.at[b + jj], (x, y, 1 - c)).start()

    def finish(cin, cout, ss, rs, b):
        x, y, c, _ = _where()
        for jj in range(N_CHIPS):
            got = cout[0].at[jj]
            _rcopy(got, got, ss.at[b + jj], rs.at[b + jj], (x, y, 1 - c)).wait_recv()
        for jj in range(N_CHIPS):
            got = cout[0].at[jj]
            _rcopy(got, got, ss.at[b + jj], rs.at[b + jj], (x, y, 1 - c)).wait_send()

    return _Task([gbf], [SDS((N_CHIPS,) + gbf.shape[1:], BF)], [], N_CHIPS, start, finish, peers=(SIBLING,))


def _t_chips(pbf):
    def start(cin, cout, ss, rs, b):
        x, y, c, chips = _where()
        for r, (px, py) in enumerate(chips):
            _rcopy(cin[0].at[2 * px + py], cout[0].at[r], ss.at[b + r], rs.at[b + r], (px, py, c)).start()

    def finish(cin, cout, ss, rs, b):
        x, y, c, chips = _where()
        for r, (px, py) in enumerate(chips):
            got = cout[0].at[r]
            _rcopy(got, got, ss.at[b + r], rs.at[b + r], (px, py, c)).wait_recv()
        for r, (px, py) in enumerate(chips):
            got = cout[0].at[r]
            _rcopy(got, got, ss.at[b + r], rs.at[b + r], (px, py, c)).wait_send()

    return _Task([pbf], [SDS((3,) + pbf.shape[1:], BF)], [], 3, start, finish, peers=OTHER_CHIPS)


def _t_swap(fin):
    def start(cin, cout, ss, rs, b):
        x, y, c, _ = _where()
        mine = cout[0].at[c]
        _rcopy(mine, mine, ss.at[b], rs.at[b], (x, y, 1 - c)).start()

    def finish(cin, cout, ss, rs, b):
        x, y, c, _ = _where()
        got = cout[0].at[1 - c]
        _rcopy(got, got, ss.at[b], rs.at[b], (x, y, 1 - c)).wait_recv()
        _rcopy(got, got, ss.at[b], rs.at[b], (x, y, 1 - c)).wait_send()

    return _Task([fin], [SDS(fin.shape, fin.dtype)], [(0, 0)], 1, start, finish, peers=(SIBLING,))


def _t_allgather(buf):
    def peers():
        x, y, c, _ = _where()
        out = []
        for rel in range(1, N_DEV):
            px, py, pc = x ^ ((rel >> 2) & 1), y ^ ((rel >> 1) & 1), c ^ (rel & 1)
            out.append((rel - 1, 4 * px + 2 * py + pc, (px, py, pc)))
        return 4 * x + 2 * y + c, out

    def start(cin, cout, ss, rs, b):
        me, ps = peers()
        mine = cout[0].at[me]
        for k, _, dev in ps:
            _rcopy(mine, mine, ss.at[b + k], rs.at[b + k], dev).start()

    def finish(cin, cout, ss, rs, b):
        me, ps = peers()
        for k, pidx, dev in ps:
            got = cout[0].at[pidx]
            _rcopy(got, got, ss.at[b + k], rs.at[b + k], dev).wait_recv()
        for k, _, dev in ps:
            mine = cout[0].at[me]
            _rcopy(mine, mine, ss.at[b + k], rs.at[b + k], dev).wait_send()

    return _Task([buf], [SDS(buf.shape, buf.dtype)], [(0, 0)], N_DEV - 1, start, finish, peers=EVERYONE)


def _run_tasks(comm, which, cin, cout, ss, rs):
    i0 = o0 = s0 = 0
    for t in comm:
        getattr(t, which)(cin[i0:i0 + len(t.ins)], cout[o0:o0 + len(t.outs)], ss, rs, s0)
        i0, o0, s0 = i0 + len(t.ins), o0 + len(t.outs), s0 + t.n_sem


def _from_hbm(*arrays):
    return [pltpu.with_memory_space_constraint(a, pltpu.HBM) for a in arrays]


def _in_hbm(shapes):
    return [pltpu.HBM(s.shape, s.dtype) for s in shapes]


def _comm_layout(comm, n_in, n_out):
    c_in = [a for t in comm for a in t.ins]
    c_out = [s for t in comm for s in t.outs]
    aliases, i0, o0 = {}, 0, 0
    for t in comm:
        for i, o in t.alias:
            aliases[n_in + i0 + i] = n_out + o0 + o
        i0, o0 = i0 + len(t.ins), o0 + len(t.outs)
    return c_in, c_out, aliases, sum(t.n_sem for t in comm)


def _call(body, operands, *, name, grid, in_specs, out_specs, out_shape, scratch_shapes=(), sem=None, vmem_mib=None, comm=(),
          free=(), prefetch=()):
    operands = [o if s.memory_space == pltpu.SMEM or k in free else pltpu.with_memory_space_constraint(o, pltpu.HBM)
                for k, (o, s) in enumerate(zip(operands, in_specs))]
    n_pre, n_in, n_out, n_scr = len(prefetch), len(in_specs), len(out_specs), len(scratch_shapes)
    c_in, c_out, aliases, n_sem = _comm_layout(comm, n_pre + n_in, n_out)
    sems = [pltpu.SemaphoreType.DMA((n_sem,)), pltpu.SemaphoreType.DMA((n_sem,))] if comm else []

    def wrapped(*refs):
        pre, refs = refs[:n_pre], refs[n_pre:]
        ins, cin = refs[:n_in], refs[n_in:n_in + len(c_in)]
        rest = refs[n_in + len(c_in):]
        outs, cout = rest[:n_out], rest[n_out:n_out + len(c_out)]
        rest = rest[n_out + len(c_out):]
        scr, csem = rest[:n_scr], rest[n_scr:]
        if not comm:
            return body(*pre, *ins, *outs, *scr)
        step = functools.reduce(lambda acc, k: acc * grid[k] + pl.program_id(k), range(len(grid)), 0)
        n_steps = math.prod(grid)

        @pl.when(step == 0)
        def _():
            _enter(comm)
            _run_tasks(comm, "start", cin, cout, *csem)

        pl.when(step == n_steps // 2)(lambda: _run_tasks(comm, "middle", cin, cout, *csem))
        body(*pre, *ins, *outs, *scr)
        pl.when(step == n_steps - 1)(lambda: _run_tasks(comm, "finish", cin, cout, *csem))

    grid_spec = pltpu.PrefetchScalarGridSpec(
        num_scalar_prefetch=n_pre, grid=grid, in_specs=list(in_specs) + [ANY] * len(c_in),
        out_specs=list(out_specs) + [ANY] * len(c_out), scratch_shapes=list(scratch_shapes) + sems)
    return _pcall(
        wrapped, name=name, grid_spec=grid_spec, out_shape=_in_hbm(list(out_shape) + c_out), input_output_aliases=aliases,
        compiler_params=_params(("arbitrary",) * len(grid) if comm else sem, vmem_mib,
                                BARRIER_OF[_peers_of(comm)] if comm else None),
    )(*prefetch, *operands, *_from_hbm(*c_in))


def _comm_call(name, comm):
    c_in, c_out, aliases, n_sem = _comm_layout(comm, 0, 0)

    def body(*refs):
        cin, cout, (ss, rs) = refs[:len(c_in)], refs[len(c_in):len(c_in) + len(c_out)], refs[len(c_in) + len(c_out):]
        _enter(comm)
        for phase in ("start", "middle", "finish"):
            _run_tasks(comm, phase, cin, cout, ss, rs)

    return _pcall(
        body, name=name, in_specs=[ANY] * len(c_in), out_specs=[ANY] * len(c_out), out_shape=_in_hbm(c_out),
        scratch_shapes=[pltpu.SemaphoreType.DMA((n_sem,)), pltpu.SemaphoreType.DMA((n_sem,))],
        input_output_aliases=aliases, compiler_params=_params(collective_id=BARRIER_OF[_peers_of(comm)]),
    )(*_from_hbm(*c_in))


def _inproj(x, g1, w_int, comm=()):
    tm = TM

    def body(x_ref, g_ref, w_ref, proj_ref, u_ref):
        xf = x_ref[...]
        r = lax.rsqrt(jnp.mean(xf * xf, axis=-1, keepdims=True) + EPS)
        u = (xf * r * g_ref[...]).astype(BF)
        u_ref[...] = u
        proj_ref[...] = _dot(u, w_ref[...], 1, 1)

    return _call(
        body, (x, g1, w_int), name="inproj", grid=(T // tm,),
        in_specs=[pl.BlockSpec((tm, D), lambda i: (i, 0)), pl.BlockSpec((1, D), lambda i: (0, 0)),
                  _resident((INW, D))],
        out_specs=[pl.BlockSpec((tm, INW), lambda i: (i, 0)), pl.BlockSpec((tm, D), lambda i: (i, 0))],
        out_shape=[SDS((T, INW), F32), SDS((T, D), BF)], sem=("parallel",), vmem_mib=40, comm=comm, free=(0, 1))


def _outproj(y, w_out, x, g2):
    tm = TM

    def body(y_ref, w_ref, x_ref, g_ref, h1_ref, u2_ref):
        h1 = x_ref[...] + _dot(y_ref[...], w_ref[...], 1, 0)
        h1_ref[...] = h1
        r = lax.rsqrt(jnp.mean(h1 * h1, axis=-1, keepdims=True) + EPS)
        u2_ref[...] = (h1 * r * g_ref[...]).astype(BF)

    return _call(
        body, (y, w_out, x, g2), name="outproj", grid=(T // tm,),
        in_specs=[pl.BlockSpec((tm, D), lambda i: (i, 0)), _resident((D, D)),
                  pl.BlockSpec((tm, D), lambda i: (i, 0)), pl.BlockSpec((1, D), lambda i: (0, 0))],
        out_specs=[pl.BlockSpec((tm, D), lambda i: (i, 0)), pl.BlockSpec((tm, D), lambda i: (i, 0))],
        out_shape=[SDS((T, D), F32), SDS((T, D), BF)], sem=("parallel",), vmem_mib=32, free=(2, 3))


def _ffn_up(u2, w_upt, comm=()):
    tm, tn = T, 512

    def body(u_ref, w_ref, o_ref):
        o_ref[...] = _dot(u_ref[...], w_ref[...], 1, 1).astype(BF)

    return _call(
        body, (u2, w_upt), name="ffn_up", grid=(T // tm, 2 * DFF // tn),
        in_specs=[pl.BlockSpec((tm, D), lambda i, j: (i, 0)), pl.BlockSpec((tn, D), lambda i, j: (j, 0))],
        out_specs=[pl.BlockSpec((tm, tn), lambda i, j: (i, j))], out_shape=[SDS((T, 2 * DFF), BF)],
        sem=("parallel", "parallel"), vmem_mib=32, comm=comm, free=(1,))


def _ffn_down(a, w_down, h1, tgt):
    tm = TM

    def body(a_ref, w_ref, h1_ref, t_ref, dh_ref, dhb_ref, l_ref):
        @pl.when(pl.program_id(0) == 0)
        def _():
            l_ref[...] = jnp.zeros_like(l_ref)

        h2 = h1_ref[...] + _dot(a_ref[...], w_ref[...], 1, 0)
        e = h2 - t_ref[...]
        dh = e * (1.0 / D)
        dh_ref[...] = dh
        dhb_ref[...] = dh.astype(BF)
        e2 = jnp.sum((e * e).reshape(tm // 8, 8, D), axis=0)
        acc = e2[:, 0:128]
        for k in range(1, D // 128):
            acc = acc + e2[:, k * 128:(k + 1) * 128]
        l_ref[...] += acc

    return _call(
        body, (a, w_down, h1, tgt), name="ffn_down", grid=(T // tm,),
        in_specs=[pl.BlockSpec((tm, DFF), lambda i: (i, 0)), _resident((DFF, D)),
                  pl.BlockSpec((tm, D), lambda i: (i, 0)), pl.BlockSpec((tm, D), lambda i: (i, 0))],
        out_specs=[pl.BlockSpec((tm, D), lambda i: (i, 0)), pl.BlockSpec((tm, D), lambda i: (i, 0)),
                   pl.BlockSpec((8, 128), lambda i: (0, 0))],
        out_shape=[SDS((T, D), F32), SDS((T, D), BF), SDS((8, 128), F32)], sem=("arbitrary",), vmem_mib=40, free=(2, 3))


def _bucket_table():
    q = np.arange(BLK, dtype=np.int32)[:, None]
    j = np.arange(2 * BLK, dtype=np.int32)[None, :]
    n = np.maximum(q + BLK - j, 0)
    nf = np.maximum(n, 1).astype(np.float32)
    max_exact = NBUCKET // 2
    large = max_exact + (np.log(nf / np.float32(max_exact)) / np.float32(math.log(BLK / max_exact))
                         * np.float32(NBUCKET - max_exact)).astype(np.int32)
    large = np.minimum(large, NBUCKET - 1)
    return np.where(n < max_exact, n, large).astype(np.int32)


def _two_bf16(x):
    hi = x.astype(BF)
    return hi, (x - hi.astype(F32)).astype(BF)


def _head_sums(x, seg):
    hi, lo = _two_bf16(x)
    s = seg[0:x.shape[1], :]
    return _dot(hi, s, 1, 0) + _dot(lo, s, 1, 0)


def _head_spread(v, seg, width):
    hi, lo = _two_bf16(v)
    s = seg[0:width, :]
    return _dot(hi, s, 1, 1) + _dot(lo, s, 1, 1)


def _head_norm(x, g_t, seg, by_head=False):
    if by_head:
        heads = [x[:, h * HD:(h + 1) * HD] for h in range(x.shape[1] // HD)]
        r = jnp.concatenate([jnp.broadcast_to(lax.rsqrt(jnp.mean(v * v, axis=-1, keepdims=True) + EPS), v.shape)
                             for v in heads], axis=1)
    else:
        r = lax.rsqrt(_head_sums(x * x, seg) * (1.0 / HD) + EPS)
        r = _head_spread(r, seg, x.shape[1])
    return x * r * g_t, r


def _head_norm_bwd(dy, x, r, g_t, seg):
    dg_t = jnp.sum(dy * (x * r), axis=0, keepdims=True)
    dgx = dy * g_t
    mean = _head_spread(_head_sums(x * dgx, seg) * (1.0 / HD), seg, x.shape[1])
    return r * dgx - x * (r * r * r) * mean, dg_t


def _fold_heads(v):
    out = v[:, 0:HD]
    for h in range(1, v.shape[1] // HD):
        out = out + v[:, h * HD:(h + 1) * HD]
    return out


def _mix_forward(P, zc8, zh8, pkv, first, cw, qg_t, kg_t, gco, gao, seg, sink_ref, bias_ref, by_head=False):
    gate_b = P[:, 0:CW]
    gate_c = P[:, CW:2 * CW]
    hc = P[:, 2 * CW:3 * CW]
    z = gate_c * hc
    keep = jnp.where(first, 0.0, 1.0)
    zp = zc8 * zh8 * keep
    p1 = zp[7:8, :]
    p2 = zp[6:7, :]
    row = lax.broadcasted_iota(jnp.int32, (BLK, 1), 0)
    z1 = jnp.where(row == 0, p1, pltpu.roll(z, 1, 0))
    z2 = jnp.where(row == 0, p2, jnp.where(row == 1, p1, pltpu.roll(z, 2, 0)))
    cz = cw[0:1, :] * z2 + cw[1:2, :] * z1 + cw[2:3, :] * z
    y_conv = gate_b * cz

    scale = HD ** -0.5
    qi = lax.broadcasted_iota(jnp.int32, (BLK, 2 * BLK), 0)
    kj = lax.broadcasted_iota(jnp.int32, (BLK, 2 * BLK), 1)
    dd = qi + BLK - kj
    first_key = jnp.where(first, BLK, 0)
    valid = (dd >= 0) & (dd < BLK) & (kj >= first_key)

    q0 = 3 * CW
    k0 = q0 + AW
    v0 = k0 + NKV * HD
    q_raw = P[:, q0:k0]
    qn, rq = _head_norm(q_raw, qg_t, seg, by_head)
    qs = (qn * scale).astype(BF)
    k_raw = jnp.concatenate([pkv[:, 0:NKV * HD], P[:, k0:v0]], axis=0)
    kn, rk = _head_norm(k_raw, kg_t, seg, by_head)
    knb = kn.astype(BF)
    heads = []
    for h in range(NH):
        kv = h // GQ
        kb = knb[:, kv * HD:(kv + 1) * HD]
        vb = jnp.concatenate([pkv[:, NKV * HD + kv * HD:NKV * HD + (kv + 1) * HD],
                              P[:, v0 + kv * HD:v0 + (kv + 1) * HD]], axis=0).astype(BF)
        Q = qs[:, h * HD:(h + 1) * HD]
        S = _dot(Q, kb, 1, 1) + bias_ref[h * BLK:(h + 1) * BLK, :]
        S = jnp.where(valid, S, NEG_INF)
        sink = sink_ref[0, h]
        m = jnp.maximum(jnp.max(S, axis=-1, keepdims=True), sink)
        p = jnp.exp(S - m)
        es = jnp.exp(sink - m)
        denom = jnp.sum(p, axis=-1, keepdims=True) + es
        probs = p / denom
        O = _dot(probs.astype(BF), vb, 1, 0)
        heads.append(dict(kb=kb, vb=vb, Q=Q, probs=probs, psink=es / denom, O=O))
    y_attn = jnp.concatenate([hd["O"] for hd in heads], axis=1)

    rc = lax.rsqrt(jnp.mean(y_conv * y_conv, axis=-1, keepdims=True) + EPS)
    ra = lax.rsqrt(jnp.mean(y_attn * y_attn, axis=-1, keepdims=True) + EPS)
    y = jnp.concatenate([y_conv * rc * gco, y_attn * ra * gao], axis=1)
    return dict(gate_b=gate_b, gate_c=gate_c, hc=hc, z=z, z1=z1, z2=z2, cz=cz, y_conv=y_conv, y_attn=y_attn,
                rc=rc, ra=ra, heads=heads, y=y, row=row, scale=scale, q_raw=q_raw, rq=rq, k_raw=k_raw, rk=rk)


BPS = 2
TILE = BPS * BLK
KV0 = 3 * CW + AW


def _mix_in_specs(tile_of):
    return [
        pl.BlockSpec(memory_space=pltpu.SMEM),
        pl.BlockSpec((TILE, INW), lambda s: (tile_of(s), 0)),
        pl.BlockSpec((8, CW), lambda s: (jnp.maximum(tile_of(s) * (TILE // 8) - 1, 0), 1)),
        pl.BlockSpec((8, CW), lambda s: (jnp.maximum(tile_of(s) * (TILE // 8) - 1, 0), 2)),
        pl.BlockSpec((BLK, 2 * NKV * HD), lambda s: (jnp.maximum(tile_of(s) * BPS - 1, 0), KV0 // (2 * NKV * HD))),
    ]


def _block_inputs(tile, b, zc_ref, zh_ref, pkv_ref, first_tile):
    P = tile[b * BLK:(b + 1) * BLK, :]
    if b == 0:
        return P, zc_ref[...], zh_ref[...], pkv_ref[...], first_tile
    lo = b * BLK
    return P, tile[lo - 8:lo, CW:2 * CW], tile[lo - 8:lo, 2 * CW:3 * CW], tile[lo - BLK:lo, KV0:KV0 + 2 * NKV * HD], False


def _mix_param_specs():
    return [
        pl.BlockSpec((8, CW), lambda s: (0, 0)),
        pl.BlockSpec((1, AW), lambda s: (0, 0)),
        pl.BlockSpec((1, NKV * HD), lambda s: (0, 0)),
        pl.BlockSpec((1, CW), lambda s: (0, 0)),
        pl.BlockSpec((1, AW), lambda s: (0, 0)),
        pl.BlockSpec((AW, 128), lambda s: (0, 0)),
        pl.BlockSpec((NH * BLK, 2 * BLK), lambda s: (0, 0)),
    ]


def _mix_params(cw8, qg, kg, gco, gao, bias):
    seg = np.zeros((AW, 128), np.float32)
    seg[np.arange(AW), np.arange(AW) // HD] = 1.0
    return (cw8, jnp.tile(qg, (1, NH)), jnp.tile(kg, (1, NKV)), gco, gao, jnp.asarray(seg, BF), bias)


def _mix_fwd(proj, sinks, cw8, qg, kg, gco, gao, bias, comm=()):
    def body(sink_ref, p_ref, zc_ref, zh_ref, pkv_ref, cw_ref, qg_ref, kg_ref, gco_ref, gao_ref, seg_ref, bias_ref, y_ref):
        tile = p_ref[...]
        for b in range(BPS):
            f = _mix_forward(*_block_inputs(tile, b, zc_ref, zh_ref, pkv_ref, pl.program_id(0) == 0), cw_ref[...],
                             qg_ref[...], kg_ref[...], gco_ref[...], gao_ref[...], seg_ref[...], sink_ref, bias_ref, by_head=True)
            y_ref[b * BLK:(b + 1) * BLK, :] = f["y"].astype(BF)

    return _call(
        body, (sinks, proj, proj, proj, proj, *_mix_params(cw8, qg, kg, gco, gao, bias)), name="mix_fwd", grid=(T // TILE,),
        in_specs=_mix_in_specs(lambda s: s) + _mix_param_specs(),
        out_specs=[pl.BlockSpec((TILE, D), lambda s: (s, 0))], out_shape=[SDS((T, D), BF)],
        sem=("parallel",), vmem_mib=40, comm=comm, free=tuple(range(5, 12)))


def _mix_bwd(proj, dy, sinks, cw8, qg, kg, gco, gao, bias, comm=()):
    n_steps = T // TILE

    def tile_of(s):
        return n_steps - 1 - s

    def body(sink_ref, p_ref, zc_ref, zh_ref, pkv_ref, dy_ref, cw_ref, qg_ref, kg_ref, gco_ref, gao_ref, seg_ref, bias_ref,
             dproj_ref, dcw_ref, dqg_ref, dkg_ref, dgco_ref, dgao_ref, dsink_ref, dbias_ref,
             ndcz_ref, dkc_ref, dvc_ref):
        s = pl.program_id(0)

        @pl.when(s == 0)
        def _():
            for r in (dcw_ref, dqg_ref, dkg_ref, dgco_ref, dgao_ref, dsink_ref, dbias_ref, ndcz_ref, dkc_ref, dvc_ref):
                r[...] = jnp.zeros_like(r)

        params = (cw_ref[...], qg_ref[...], kg_ref[...], gco_ref[...], gao_ref[...], seg_ref[...])
        tile = p_ref[...]
        carry = (ndcz_ref[...], dkc_ref[...], dvc_ref[...])
        total = None
        for b in reversed(range(BPS)):
            f = _mix_forward(*_block_inputs(tile, b, zc_ref, zh_ref, pkv_ref, s == n_steps - 1), *params, sink_ref, bias_ref)
            pieces, sums, carry = one_block(f, dy_ref[b * BLK:(b + 1) * BLK, :], params, carry)
            for lo, piece in pieces:
                dproj_ref[b * BLK:(b + 1) * BLK, lo:lo + piece.shape[1]] = piece
            total = sums if total is None else [t + v for t, v in zip(total, sums)]
        ndcz_ref[...], dkc_ref[...], dvc_ref[...] = carry
        dcw, dqg_t, dkg_t, dgco, dgao, dsink, *ds = total
        dcw_ref[0:3, :] += dcw
        dqg_ref[...] += _fold_heads(dqg_t)
        dkg_ref[...] += _fold_heads(dkg_t)
        dgco_ref[...] += dgco
        dgao_ref[...] += dgao
        dsink_ref[...] += dsink
        for h in range(NH):
            dbias_ref[h * BLK:(h + 1) * BLK, :] += ds[h]

    def one_block(f, dy, params, carry):
        cw, qg_v, kg_v, gco_v, gao_v, seg = params
        nxt, dk_carry, dv_carry = carry
        dyc, dgco = _rms_bwd(dy[:, 0:CW], f["y_conv"], f["rc"], gco_v)
        dya, dgao = _rms_bwd(dy[:, CW:CW + AW], f["y_attn"], f["ra"], gao_v)

        row = f["row"]
        dgate_b = dyc * f["cz"]
        dcz = dyc * f["gate_b"]
        dcw = jnp.concatenate([jnp.sum(dcz * f[k], axis=0, keepdims=True) for k in ("z2", "z1", "z")], axis=0)
        n0 = nxt[0:1, :]
        n1 = nxt[1:2, :]
        d1 = jnp.where(row == BLK - 1, n0, pltpu.roll(dcz, BLK - 1, 0))
        d2 = jnp.where(row == BLK - 1, n1, jnp.where(row == BLK - 2, n0, pltpu.roll(dcz, BLK - 2, 0)))
        dz = cw[2:3, :] * dcz + cw[1:2, :] * d1 + cw[0:1, :] * d2
        pieces = [(0, dgate_b.astype(BF)), (CW, (dz * f["hc"]).astype(BF)), (2 * CW, (dz * f["gate_c"]).astype(BF))]

        scale = f["scale"]
        lane = lax.broadcasted_iota(jnp.int32, (1, 128), 1)
        dsink = jnp.zeros((1, 128), F32)
        dq_cols, dk_cols, dv_cols, dk_prev, dv_prev, ds = [], [], [], [], [], []
        for kv in range(NKV):
            dKb = dVb = 0.0
            for h in range(kv * GQ, (kv + 1) * GQ):
                hd = f["heads"][h]
                dO = dya[:, h * HD:(h + 1) * HD]
                delta = jnp.sum(dO * hd["O"], axis=-1, keepdims=True)
                dOb = dO.astype(BF)
                dP = _dot(dOb, hd["vb"], 1, 1)
                dS = hd["probs"] * (dP - delta)
                tot = jnp.sum(hd["psink"] * delta, axis=0, keepdims=True)
                dsink = dsink - jnp.where(lane == h, tot, 0.0)
                ds.append(dS)
                dSb = dS.astype(BF)
                dq_cols.append(_dot(dSb, hd["kb"], 1, 0))
                dKb = dKb + _dot(dSb, hd["Q"], 0, 0)
                dVb = dVb + _dot(hd["probs"].astype(BF), dOb, 0, 0)
            dk_cols.append(dKb[BLK:, :] + dk_carry[:, kv * HD:(kv + 1) * HD])
            dv_cols.append(dVb[BLK:, :] + dv_carry[:, kv * HD:(kv + 1) * HD])
            dk_prev.append(dKb[:BLK, :])
            dv_prev.append(dVb[:BLK, :])
        dq_raw, dqg_t = _head_norm_bwd(jnp.concatenate(dq_cols, axis=1) * scale, f["q_raw"], f["rq"], qg_v, seg)
        dk_raw, dkg_t = _head_norm_bwd(jnp.concatenate(dk_cols, axis=1), f["k_raw"][BLK:, :], f["rk"][BLK:, :], kg_v, seg)
        pieces.append((3 * CW, jnp.concatenate([dq_raw, dk_raw] + dv_cols, axis=1).astype(BF)))
        owed = (dcz[0:8, :], jnp.concatenate(dk_prev, axis=1), jnp.concatenate(dv_prev, axis=1))
        return pieces, [dcw, dqg_t, dkg_t, dgco, dgao, dsink, *ds], owed

    small = lambda r, c: pl.BlockSpec((r, c), lambda s: (0, 0))
    return _call(
        body, (sinks, proj, proj, proj, proj, dy, *_mix_params(cw8, qg, kg, gco, gao, bias)), name="mix_bwd", grid=(n_steps,),
        in_specs=_mix_in_specs(tile_of) + [pl.BlockSpec((TILE, D), lambda s: (tile_of(s), 0))] + _mix_param_specs(),
        out_specs=[pl.BlockSpec((TILE, INW), lambda s: (tile_of(s), 0)), small(8, CW), small(1, HD), small(1, HD),
                   small(1, CW), small(1, AW), small(1, 128), small(NH * BLK, 2 * BLK)],
        out_shape=[SDS((T, INW), BF), SDS((8, CW), F32), SDS((1, HD), F32), SDS((1, HD), F32), SDS((1, CW), F32),
                   SDS((1, AW), F32), SDS((1, 128), F32), SDS((NH * BLK, 2 * BLK), F32)],
        scratch_shapes=[pltpu.VMEM((8, CW), F32), pltpu.VMEM((BLK, NKV * HD), F32), pltpu.VMEM((BLK, NKV * HD), F32)],
        sem=("arbitrary",), vmem_mib=56, comm=comm, free=(1, 2, 3, 4) + tuple(range(6, 13)))


FT = 256
NFT = DFF // FT
RC = 1024
NCH = T // RC
LEAD = 16


def _rows8(x):
    return jnp.sum(x.reshape(x.shape[0] // 8, 8, x.shape[1]), axis=0)


def _ffn_act_specs():
    return [
        pl.BlockSpec((T, FT), lambda j: (0, j)), pl.BlockSpec((T, FT), lambda j: (0, NFT + j)),
        pl.BlockSpec((8, FT), lambda j: (0, j)), pl.BlockSpec((8, FT), lambda j: (0, NFT + j)),
        pl.BlockSpec((1, FT), lambda j: (0, j)), pl.BlockSpec((1, FT), lambda j: (0, NFT + j)),
    ]


def _conv_rows(win, w, b, n):
    win = win.astype(F32)
    u = win[LEAD:LEAD + n]
    u1 = pltpu.roll(win, 1, 0)[LEAD:LEAD + n]
    u2 = pltpu.roll(win, 2, 0)[LEAD:LEAD + n]
    return u2, u1, u, w[0:1, :] * u2 + w[1:2, :] * u1 + w[2:3, :] * u + b


def _ffn_act(up, fw8, fb, comm=()):
    def body(ug_ref, uv_ref, wg_ref, wv_ref, bg_ref, bv_ref, a_ref, pg_ref, pv_ref):
        wg, wv, bg, bv = wg_ref[...], wv_ref[...], bg_ref[...], bv_ref[...]

        def chunk(rows, win_g, win_v):
            gp = _conv_rows(win_g, wg, bg, RC)[3]
            vp = _conv_rows(win_v, wv, bv, RC)[3]
            a_ref[rows, :] = (gp * jax.nn.sigmoid(gp) * vp).astype(BF)
            pg_ref[0, rows, :] = gp.astype(BF)
            pv_ref[0, rows, :] = vp.astype(BF)

        zero = jnp.zeros((LEAD, FT), BF)
        chunk(pl.ds(0, RC), jnp.concatenate([zero, ug_ref[0:RC, :]], axis=0), jnp.concatenate([zero, uv_ref[0:RC, :]], axis=0))

        def step(i, carry):
            r0 = pl.multiple_of(i * RC, RC)
            win = pl.ds(r0 - LEAD, RC + LEAD)
            chunk(pl.ds(r0, RC), ug_ref[win, :], uv_ref[win, :])
            return carry

        lax.fori_loop(1, NCH, step, 0)

    tile = pl.BlockSpec((1, T, FT), lambda j: (j, 0, 0))
    return _call(
        body, (up, up, fw8, fw8, fb, fb), name="ffn_act", grid=(NFT,), in_specs=_ffn_act_specs(),
        out_specs=[pl.BlockSpec((T, FT), lambda j: (0, j)), tile, tile],
        out_shape=[SDS((T, DFF), BF), SDS((NFT, T, FT), BF), SDS((NFT, T, FT), BF)],
        sem=("parallel",), vmem_mib=40, comm=comm, free=(2, 3, 4, 5))


def _ffn_act_bwd(up, pre_g, pre_v, da, fw8, comm=()):
    ext = RC + LEAD

    def body(ug_ref, uv_ref, wg_ref, wv_ref, pg_ref, pv_ref, da_ref, dug_ref, duv_ref, dwg_ref, dwv_ref, dbg_ref, dbv_ref):
        wg, wv = wg_ref[...], wv_ref[...]

        def chunk(u_g, u_v, gp, vp, da_e):
            gp, vp, da_e = gp.astype(F32), vp.astype(F32), da_e.astype(F32)
            sig = jax.nn.sigmoid(gp)
            dvp = da_e * (gp * sig)
            dgp = da_e * vp * (sig * (1.0 + gp * (1.0 - sig)))

            def branch(dp, w, u):
                d0, d1, d2 = dp[0:RC], pltpu.roll(dp, ext - 1, 0)[0:RC], pltpu.roll(dp, ext - 2, 0)[0:RC]
                du = (w[2:3, :] * d0 + w[1:2, :] * d1 + w[0:1, :] * d2).astype(BF)
                u = u.astype(F32)
                return du, [_rows8(d0), _rows8(d2 * u), _rows8(d1 * u), _rows8(d0 * u)]

            dug, sums_g = branch(dgp, wg, u_g)
            duv, sums_v = branch(dvp, wv, u_v)
            return dug, duv, sums_g + sums_v

        def step(i, acc):
            r0 = pl.multiple_of(i * RC, RC)
            rows, more = pl.ds(r0, RC), pl.ds(r0, ext)
            dug, duv, part = chunk(ug_ref[rows, :], uv_ref[rows, :], pg_ref[0, more, :], pv_ref[0, more, :], da_ref[more, :])
            dug_ref[rows, :] = dug
            duv_ref[rows, :] = duv
            return [a + p for a, p in zip(acc, part)]

        acc = lax.fori_loop(0, NCH - 1, step, [jnp.zeros((8, FT), F32)] * 8)
        r0 = T - RC
        zero = jnp.zeros((LEAD, FT), BF)
        tail = lambda rows: jnp.concatenate([rows, zero], axis=0)
        dug, duv, part = chunk(ug_ref[r0:T, :], uv_ref[r0:T, :], tail(pg_ref[0, r0:T, :]), tail(pv_ref[0, r0:T, :]),
                               tail(da_ref[r0:T, :]))
        dug_ref[r0:T, :] = dug
        duv_ref[r0:T, :] = duv
        tot = [jnp.sum(a + p, axis=0, keepdims=True) for a, p in zip(acc, part)]
        for k, (dw_ref, db_ref) in enumerate(((dwg_ref, dbg_ref), (dwv_ref, dbv_ref))):
            db_ref[...] = tot[4 * k]
            dw_ref[...] = jnp.zeros_like(dw_ref)
            for r in range(3):
                dw_ref[r:r + 1, :] = tot[4 * k + 1 + r]

    col = lambda r: pl.BlockSpec((r, FT), lambda j: (0, j))
    return _call(
        body, (up, up, fw8, fw8, pre_g, pre_v, da), name="ffn_act_bwd", grid=(NFT,),
        in_specs=_ffn_act_specs()[0:4] + [pl.BlockSpec((1, T, FT), lambda j: (j, 0, 0))] * 2 + [col(T)],
        out_specs=[col(T), col(T), col(8), col(8), col(1), col(1)],
        out_shape=[SDS((T, DFF), BF), SDS((T, DFF), BF), SDS((8, DFF), F32), SDS((8, DFF), F32),
                   SDS((1, DFF), F32), SDS((1, DFF), F32)],
        sem=("parallel",), vmem_mib=40, comm=comm, free=(0, 1, 2, 3))


def _ffn_down_bwd(dh2b, w_down, comm=()):
    tm = TM

    def body(d_ref, w_ref, o_ref):
        o_ref[...] = _dot(d_ref[...], w_ref[...], 1, 1).astype(BF)

    return _call(
        body, (dh2b, w_down), name="ffn_down_bwd", grid=(T // tm,),
        in_specs=[pl.BlockSpec((tm, D), lambda i: (i, 0)), _resident((DFF, D))],
        out_specs=[pl.BlockSpec((tm, DFF), lambda i: (i, 0))], out_shape=[SDS((T, DFF), BF)],
        sem=("parallel",), vmem_mib=40, comm=comm, free=(0, 1))


def _norm_matmul_bwd(name, a_list, w_t, k_offsets, xin, g, dres, want_bf16, comm=(), slot=None, band=(), pack=()):
    tm = TM
    ks = [a.shape[1] for a in a_list]
    n_a = len(a_list)
    n_pre = 0 if slot is None else 1
    n_in = n_a + 4 + len(band) + len(pack)

    def body(*refs):
        refs = refs[n_pre:]
        a_refs = refs[:n_a]
        w_ref, x_ref, g_ref, r_ref = refs[n_a:n_a + 4]
        outs = refs[n_in:]
        dg_out = outs[1 + want_bf16]
        dx_ref, dg_ref = outs[0], (dg_out if slot is None else dg_out.at[0])

        @pl.when(pl.program_id(0) == 0)
        def _():
            dg_ref[...] = jnp.zeros_like(dg_ref)
            if band:
                db_ref, bk_ref, tbl_ref = refs[n_a + 4], refs[n_a + 5], outs[2 + want_bf16]
                bk = bk_ref[...]
                for b in range(NBUCKET):
                    m = bk == b
                    for h in range(NH):
                        v = jnp.where(m, db_ref[h * BLK:(h + 1) * BLK, :], 0.0)
                        tbl_ref[0, h:h + 1, b:b + 1] = jnp.sum(jnp.sum(v, axis=1, keepdims=True), axis=0, keepdims=True)
            if pack:
                pm_ref = outs[2 + want_bf16 + bool(band)]
                pm_ref[...] = jnp.zeros_like(pm_ref)
                _fill_mix(pm_ref, *refs[n_in - len(pack):n_in])

        du = _dot(a_refs[0][...], w_ref[k_offsets[0]:k_offsets[0] + ks[0], :], 1, 0)
        for k in range(1, n_a):
            du = du + _dot(a_refs[k][...], w_ref[k_offsets[k]:k_offsets[k] + ks[k], :], 1, 0)
        x = x_ref[...]
        r = lax.rsqrt(jnp.mean(x * x, axis=-1, keepdims=True) + EPS)
        dx, dg = _rms_bwd(du, x, r, g_ref[...])
        dx = r_ref[...] + dx
        dx_ref[...] = dx
        if want_bf16:
            outs[1][...] = dx.astype(BF)
        dg_ref[...] += dg

    tile = lambda c: pl.BlockSpec((tm, c), lambda i, *_: (i, 0))
    if slot is None:
        dg_spec, dg_shape = pl.BlockSpec((1, D), lambda i: (0, 0)), SDS((1, D), F32)
    else:
        dg_spec, dg_shape = pl.BlockSpec((1, 1, D), lambda i, slot_ref: (slot_ref[0], 0, 0)), SDS((N_DEV, 1, D), F32)
    out_specs = [tile(D)] + ([tile(D)] if want_bf16 else []) + [dg_spec]
    out_shape = [SDS((T, D), F32)] + ([SDS((T, D), BF)] if want_bf16 else []) + [dg_shape]
    if band:
        out_specs.append(pl.BlockSpec((1, NH, NBUCKET), lambda i, slot_ref: (slot_ref[0], 0, 0)))
        out_shape.append(SDS((N_DEV, NH, NBUCKET), F32))
    if pack:
        out_specs.append(pl.BlockSpec((1, 8, D), lambda i, slot_ref: (slot_ref[0], 0, 0)))
        out_shape.append(SDS((N_DEV, 8, D), F32))
    return _call(
        body, (*a_list, w_t, xin, g, dres, *band, *pack), name=name, grid=(T // tm,), prefetch=() if slot is None else (slot,),
        in_specs=[tile(k) for k in ks] + [_resident(w_t.shape), tile(D), pl.BlockSpec((1, D), lambda i, *_: (0, 0)), tile(D)]
        + [pl.BlockSpec(b.shape, lambda i, *_: (0, 0)) for b in (*band, *pack)],
        out_specs=out_specs, out_shape=out_shape, sem=("arbitrary",), vmem_mib=56, comm=comm, free=tuple(range(n_a + 4)))


def _out_bwd(dh1b, w_out, comm=()):
    tm = TM

    def body(d_ref, w_ref, o_ref):
        o_ref[...] = _dot(d_ref[...], w_ref[...], 1, 1)

    return _call(
        body, (dh1b, w_out), name="out_bwd", grid=(T // tm,),
        in_specs=[pl.BlockSpec((tm, D), lambda i: (i, 0)), _resident((D, D))],
        out_specs=[pl.BlockSpec((tm, D), lambda i: (i, 0))], out_shape=[SDS((T, D), F32)],
        sem=("parallel",), vmem_mib=32, comm=comm, free=(0, 1))


def _wgrad(name, a_list, b, old_a, comm=()):
    m_k = a_list[0].shape[1]
    tm = max(t for t in range(128, m_k // 2 + 1, 128) if m_k % t == 0)
    steps = [a.shape[1] // tm for a in a_list]
    starts = [sum(steps[:k]) for k in range(len(a_list))]
    n_a = len(a_list)

    def body(*refs):
        a_refs, b_ref, o_ref = refs[:n_a], refs[n_a], refs[n_a + 1]
        i = pl.program_id(0)
        for k in range(n_a):
            @pl.when((i >= starts[k]) & (i < starts[k] + steps[k]))
            def _(k=k):
                o_ref[...] = _dot(a_refs[k][...], b_ref[...], 0, 0).astype(BF)

    def a_spec(k):
        return pl.BlockSpec((T, tm), lambda i: (0, jnp.clip(i - starts[k], 0, steps[k] - 1)))

    m_total = tm * sum(steps)
    return _call(
        body, (*a_list, b), name=name, grid=(sum(steps),),
        in_specs=[a_spec(k) for k in range(n_a)] + [_resident((T, D))],
        out_specs=[pl.BlockSpec((tm, D), lambda i: (i, 0))], out_shape=[SDS((m_total, D), BF)],
        sem=("parallel",), vmem_mib=40, comm=comm, free=() if old_a is None else tuple(range(n_a)) if old_a else (n_a,))


def _chip_sum(name, gbf, from_sib, core, chip):
    h = gbf.shape[1]
    th = h

    def body(core_ref, chip_ref, g_ref, s_ref, pbf_ref, own_ref):
        p = g_ref[0].astype(F32) + s_ref[0].astype(F32)
        pbf_ref[0] = p.astype(BF)

        @pl.when(pl.program_id(1) == chip_ref[0])
        def _():
            own_ref[...] = p

    grid_spec = pltpu.PrefetchScalarGridSpec(
        num_scalar_prefetch=2, grid=(h // th, N_CHIPS),
        in_specs=[pl.BlockSpec((1, th, D), lambda t, jj, core_ref, chip_ref: (2 * jj + core_ref[0], t, 0)),
                  pl.BlockSpec((1, th, D), lambda t, jj, core_ref, chip_ref: (jj, t, 0))],
        out_specs=[pl.BlockSpec((1, th, D), lambda t, jj, core_ref, chip_ref: (jj, t, 0)),
                   pl.BlockSpec((th, D), lambda t, jj, core_ref, chip_ref: (t, 0))],
    )
    return _pcall(
        body, name=name, grid_spec=grid_spec, out_shape=_in_hbm([SDS((N_CHIPS, h, D), BF), SDS((h, D), F32)]),
        compiler_params=_params(("arbitrary", "arbitrary"), 32),
    )(core, chip, *_from_hbm(gbf, from_sib))


def _final_sum(name, own, from_chips, core, comm=()):
    h = own.shape[0]
    n = 4 if h % (4 * ROWS16) == 0 else 2
    th = h // n

    def body(core_ref, o_ref, r_ref, f_ref):
        f_ref[0] = ((o_ref[...] + r_ref[0].astype(F32)) + r_ref[1].astype(F32)) + r_ref[2].astype(F32)

    return _call(
        body, (own, from_chips), name=name, grid=(n,), prefetch=(core,),
        in_specs=[pl.BlockSpec((th, D), lambda i, core_ref: (i, 0)), pl.BlockSpec((3, th, D), lambda i, core_ref: (0, i, 0))],
        out_specs=[pl.BlockSpec((1, th, D), lambda i, core_ref: (core_ref[0], i, 0))], out_shape=[SDS((2, h, D), F32)],
        sem=("arbitrary",), vmem_mib=40, comm=comm)


def _adam_math(w, g, m, v):
    nm = ADAM_B1 * m + (1.0 - ADAM_B1) * g
    nv = ADAM_B2 * v + (1.0 - ADAM_B2) * (g * g)
    m_hat = nm / (1.0 - ADAM_B1 ** ADAM_STEP)
    v_hat = nv / (1.0 - ADAM_B2 ** ADAM_STEP)
    return -ADAM_LR * (m_hat / (jnp.sqrt(v_hat) + ADAM_EPS) + ADAM_WD * w), nm, nv


def _adamw(name, w, g, m, v, tr, copy_g=False, stage=True, g_transposed=False):
    rows, cols = w.shape

    def body(w_ref, g_ref, m_ref, v_ref, *outs):
        d_ref, nm_ref, nv_ref = outs[-3:]
        for c in [pl.ds(c0, 128) for c0 in range(0, cols, 128)] if g_transposed else [slice(None)]:
            g_val = g_ref[c, :].T if g_transposed else g_ref[...]
            if copy_g:
                outs[0][:, c] = g_val
            d_ref[:, c], nm_ref[:, c], nv_ref[:, c] = _adam_math(w_ref[:, c], g_val, m_ref[:, c], v_ref[:, c])

    spec = pl.BlockSpec((tr, cols), lambda i: (i, 0))
    n_out = 4 if copy_g else 3
    g_spec = pl.BlockSpec((cols, tr), lambda i: (0, i)) if g_transposed else spec
    return _call(body, (w, g, m, v), name=name, grid=(rows // tr,), in_specs=[spec, g_spec, spec, spec], out_specs=[spec] * n_out,
                 out_shape=[SDS((rows, cols), F32)] * n_out, sem=("parallel",), vmem_mib=32,
                 free=(0, 2, 3) if stage else ())


C_SQ = 2 * DFF
P_W = C_SQ + 128
R_G2, R_GO, R_DCW, R_QK = 0, 1, 2, 5
C_GCO, C_GAO, C_DQG, C_DKG, C_SINK = 0, CW, 0, 128, 256


def _pack(name, me, ins, width, fill):
    def body(me_ref, *refs):
        o = refs[-1]
        o[...] = jnp.zeros_like(o)
        fill(o, *refs[:-1])

    return _call(body, ins, name=name, grid=(1,), prefetch=(me,),
                 in_specs=[pl.BlockSpec(a.shape, lambda i, me_ref: (0, 0)) for a in ins],
                 out_specs=[pl.BlockSpec((1, 8, width), lambda i, me_ref: (me_ref[0], 0, 0))],
                 out_shape=[SDS((N_DEV, 8, width), F32)], sem=("arbitrary",))[0]


def _pack_ffn(me, dfwg, dfwv, dfbg, dfbv, sq):
    def fill(o, dfwg_r, dfwv_r, dfbg_r, dfbv_r, sq_r):
        o[0, :, 0:DFF] = dfwg_r[...]
        o[0, :, DFF:2 * DFF] = dfwv_r[...]
        o[0, 3:4, 0:DFF] = dfbg_r[...]
        o[0, 3:4, DFF:2 * DFF] = dfbv_r[...]
        o[0, :, C_SQ:C_SQ + 128] = sq_r[...]

    return _pack("pack_ffn", me, (dfwg, dfwv, dfbg, dfbv, sq), P_W, fill)


def _fill_mix(o, dg2_r, dgco_r, dgao_r, dcw_r, dqg_r, dkg_r, dsink_r):
    o[0, R_G2:R_G2 + 1, :] = dg2_r[...]
    o[0, R_GO:R_GO + 1, C_GCO:C_GCO + CW] = dgco_r[...]
    o[0, R_GO:R_GO + 1, C_GAO:C_GAO + AW] = dgao_r[...]
    o[0, R_DCW:R_DCW + 3, 0:CW] = dcw_r[0:3, :]
    o[0, R_QK:R_QK + 1, C_DQG:C_DQG + HD] = dqg_r[...]
    o[0, R_QK:R_QK + 1, C_DKG:C_DKG + HD] = dkg_r[...]
    o[0, R_QK:R_QK + 1, C_SINK:C_SINK + 128] = dsink_r[...]


N_SMALL = 11


def _small_adam(chip, p_all, pm_all, g1_all, tbl_all, ws, ms, vs):
    fw_cols = 2 * DFF // N_CHIPS
    cw_cols = CW // N_CHIPS

    def body(chip_ref, p_ref, fw_ref, pm_ref, cw_ref, g1_ref, tbl_ref, *refs):
        w_r, m_r, v_r = refs[0:N_SMALL], refs[N_SMALL:2 * N_SMALL], refs[2 * N_SMALL:3 * N_SMALL]
        outs = refs[3 * N_SMALL:]
        g_o, d_o, nm_o, nv_o = (outs[k * N_SMALL:(k + 1) * N_SMALL] for k in range(4))
        loss_o = outs[4 * N_SMALL]

        def total(ref):
            s = ref[0]
            for k in range(1, N_DEV):
                s = s + ref[k]
            return s

        S = total(p_ref)
        fw = total(fw_ref)
        M = total(pm_ref)
        cw = total(cw_ref)

        def step(i, g, at):
            d, nm, nv = _adam_math(w_r[i][at], g, m_r[i][at], v_r[i][at])
            g_o[i][at], d_o[i][at], nm_o[i][at], nv_o[i][at] = g, d, nm, nv

        everything = (slice(None), slice(None))
        step(0, total(g1_ref), everything)
        for r in range(3):
            step(1, cw[R_DCW + r:R_DCW + r + 1, :], (r, slice(None), slice(None)))
        step(2, M[R_QK:R_QK + 1, C_DQG:C_DQG + HD], everything)
        step(3, M[R_QK:R_QK + 1, C_DKG:C_DKG + HD], everything)
        step(4, total(tbl_ref), everything)
        step(5, M[R_QK:R_QK + 1, C_SINK:C_SINK + NH], everything)
        step(6, M[R_GO:R_GO + 1, C_GCO:C_GCO + CW], everything)
        step(7, M[R_GO:R_GO + 1, C_GAO:C_GAO + AW], everything)
        step(8, M[R_G2:R_G2 + 1, :], everything)
        for r in range(3):
            step(9, fw[r:r + 1, :], (r, slice(None), slice(None)))
        step(10, S[3:4, 0:2 * DFF], everything)
        sq = S[:, C_SQ:C_SQ + 128]
        loss_o[...] = jnp.sum(jnp.sum(sq, axis=1, keepdims=True), axis=0, keepdims=True) * (0.5 / D)

    def full(a):
        n = len(a.shape)
        return pl.BlockSpec(a.shape, lambda i, chip_ref: (0,) * n)

    params = [*ws, *ms, *vs]
    out = _call(
        body, (p_all, p_all, pm_all, pm_all, g1_all, tbl_all, *params), name="small_adam", grid=(1,), prefetch=(chip,),
        in_specs=[full(p_all),
                  pl.BlockSpec((N_DEV, 8, fw_cols), lambda i, chip_ref: (0, 0, chip_ref[0])),
                  full(pm_all),
                  pl.BlockSpec((N_DEV, 8, cw_cols), lambda i, chip_ref: (0, 0, chip_ref[0])),
                  full(g1_all), full(tbl_all), *[full(a) for a in params]],
        out_specs=[full(a) for a in ws] * 4 + [pl.BlockSpec((1, 1), lambda i, chip_ref: (0, 0))],
        out_shape=[SDS(a.shape, F32) for a in ws] * 4 + [SDS((1, 1), F32)], sem=("arbitrary",), vmem_mib=32)
    return out[0:N_SMALL], out[N_SMALL:2 * N_SMALL], out[2 * N_SMALL:3 * N_SMALL], out[3 * N_SMALL:4 * N_SMALL], out[4 * N_SMALL]


PLACE_STEPS = 4


def _place_specs(shards):
    rows = [s.shape[0] // PLACE_STEPS for s in shards]
    return ([pl.BlockSpec((r, D), lambda i, chip_ref: (i, 0)) for r in rows],
            [pl.BlockSpec((r, D), lambda i, chip_ref: (chip_ref[0] * PLACE_STEPS + i, 0)) for r in rows],
            [SDS((N_CHIPS * s.shape[0], D), BF) for s in shards])


def _place_first(chip, shard, conv_w, ffn_conv_w):
    def body(chip_ref, a, s0, s1, o, t0, t1):
        o[...] = a[...].astype(BF)

        @pl.when(pl.program_id(0) == 0)
        def _():
            for s, t in ((s0, t0), (s1, t1)):
                t[...] = jnp.zeros_like(t)
                t[0, 0:3, :] = s[...]

    ins, outs, shapes = _place_specs([shard])
    taps = (conv_w, ffn_conv_w)
    return _call(
        body, (shard, conv_w, ffn_conv_w), name="place_first", grid=(PLACE_STEPS,), prefetch=(chip,),
        in_specs=ins + [pl.BlockSpec(s.shape, lambda i, chip_ref: (0, 0)) for s in taps],
        out_specs=outs + [pl.BlockSpec((1, 8, s.shape[1]), lambda i, chip_ref: (chip_ref[0], 0, 0)) for s in taps],
        out_shape=shapes + [SDS((N_CHIPS, 8, s.shape[1]), F32) for s in taps],
        sem=("arbitrary",), vmem_mib=32, free=(1, 2))


def _place_rest(chip, shards, w_up, table, bucket, comm):
    n = len(shards)
    c_up = w_up.shape[1]
    edges = [round(k * (c_up // 128) / PLACE_STEPS) * 128 for k in range(PLACE_STEPS + 1)]

    def body(chip_ref, *refs):
        a, (up_ref, tab_ref, bk_ref), o = refs[:n], refs[n:n + 3], refs[n + 3:2 * n + 3]
        up_o, bias_ref = refs[2 * n + 3:]
        for src, dst in zip(a, o):
            dst[...] = src[...].astype(BF)
        for k in range(PLACE_STEPS):
            @pl.when(pl.program_id(0) == k)
            def _(k=k):
                up_o[edges[k]:edges[k + 1], :] = up_ref[:, edges[k]:edges[k + 1]].T.astype(BF)

        @pl.when(pl.program_id(0) == 0)
        def _():
            bk = bk_ref[...]
            eq = [bk == b for b in range(NBUCKET)]
            for h in range(NH):
                acc = jnp.zeros((BLK, 2 * BLK), F32)
                for b in range(NBUCKET):
                    acc = jnp.where(eq[b], tab_ref[h, b], acc)
                bias_ref[h * BLK:(h + 1) * BLK, :] = acc

    ins, outs, shapes = _place_specs(shards)
    return _call(
        body, (*shards, w_up, table, bucket), name="place_rest", grid=(PLACE_STEPS,), prefetch=(chip,),
        in_specs=ins + [_resident(w_up.shape), pl.BlockSpec(memory_space=pltpu.SMEM),
                        pl.BlockSpec(bucket.shape, lambda i, chip_ref: (0, 0))],
        out_specs=outs + [pl.BlockSpec((c_up, D), lambda i, chip_ref: (chip_ref[0], 0)),
                          pl.BlockSpec((NH * BLK, 2 * BLK), lambda i, chip_ref: (0, 0))],
        out_shape=shapes + [SDS((N_CHIPS * c_up, D), BF), SDS((NH * BLK, 2 * BLK), F32)],
        sem=("arbitrary",), vmem_mib=32, comm=comm, free=(n + 1, n + 2))


def kernel(x, norm_mix_g, w_in, conv_w, q_norm_g, k_norm_g, rel_bias_table, sinks, out_norm_conv_g, out_norm_attn_g, w_out, norm_ffn_g, w_up, ffn_conv_w, ffn_conv_b, w_down, loss_target, m_norm_mix_g, m_w_in, m_conv_w, m_q_norm_g, m_k_norm_g, m_rel_bias_table, m_sinks, m_out_norm_conv_g, m_out_norm_attn_g, m_w_out, m_norm_ffn_g, m_w_up, m_ffn_conv_w, m_ffn_conv_b, m_w_down, v_norm_mix_g, v_w_in, v_conv_w, v_q_norm_g, v_k_norm_g, v_rel_bias_table, v_sinks, v_out_norm_conv_g, v_out_norm_attn_g, v_w_out, v_norm_ffn_g, v_w_up, v_ffn_conv_w, v_ffn_conv_b, v_w_down):
    as_arg = lambda i: jnp.reshape(i, (1,)).astype(jnp.int32)
    chip = as_arg(2 * lax.axis_index("x") + lax.axis_index("y"))
    core = as_arg(lax.axis_index("c"))
    me = 2 * chip + core
    xs, tgt = x[0], loss_target[0]
    qg, kg, gco, gao, g1, g2, fb = q_norm_g, k_norm_g, out_norm_conv_g, out_norm_attn_g, norm_mix_g, norm_ffn_g, ffn_conv_b
    pieces = lambda g: g.reshape(N_DEV, g.shape[0] // N_DEV, D)
    whole = lambda f: f.reshape(2 * f.shape[1], D)

    bucket = jnp.asarray(_bucket_table())
    p_in, p_cw, p_fw = _place_first(chip, w_in[0].T, conv_w[0], ffn_conv_w[0])
    p_out, p_down, p_up, bias, w_int, cw_all, fw_all = _place_rest(
        chip, [w_out[0], w_down[0]], w_up[0], rel_bias_table.T, bucket,
        comm=[_t_gather(p_in, relayed_first=True), _t_small_weights(p_cw), _t_small_weights(p_fw)])
    cw8 = jnp.transpose(cw_all, (1, 0, 2)).reshape(8, CW)
    fw8 = jnp.transpose(fw_all, (1, 0, 2)).reshape(8, 2 * DFF)

    early = 3 / 11
    proj, u1, w_out_f, p_up = _inproj(xs, g1, w_int, comm=[_t_gather(p_out), _t_gather(p_up, (0, early))])
    y, w_upt = _mix_fwd(proj, sinks, cw8, qg, kg, gco, gao, bias, comm=[_t_gather(p_up, (early, 1))])
    h1, u2 = _outproj(y, w_out_f, xs, g2)
    up, w_down_f = _ffn_up(u2, w_upt, comm=[_t_gather(p_down)])
    a, pre_g, pre_v = _ffn_act(up, fw8, fb)
    dh2, dh2b, sq = _ffn_down(a, w_down_f, h1, tgt)

    gdbf, = _wgrad("wgrad_down", [a], dh2b, None)
    da, sib_down = _ffn_down_bwd(dh2b, w_down_f, comm=[_t_sibling(pieces(gdbf))])
    pbf_down, own_down = _chip_sum("chip_sum_w_down", pieces(gdbf), sib_down, core, chip)
    dug, duv, dfwg, dfwv, dfbg, dfbv, chips_down = _ffn_act_bwd(up, pre_g, pre_v, da, fw8, comm=[_t_chips(pbf_down)])
    fin_down, = _final_sum("final_sum_w_down", own_down, chips_down, core)
    gubf, = _wgrad("wgrad_up", [dug, duv], u2, False)
    p_all = _pack_ffn(me, dfwg, dfwv, dfbg, dfbv, sq)
    dh1, dh1b, dg2, sib_up, fin_down, p_all = _norm_matmul_bwd(
        "ffn_up_bwd", [dug, duv], w_upt, [0, DFF], h1, g2, dh2, True,
        comm=[_t_sibling(pieces(gubf)), _t_swap(fin_down), _t_allgather(p_all)])
    pbf_up, own_up = _chip_sum("chip_sum_w_up", pieces(gubf), sib_up, core, chip)
    gobf, = _wgrad("wgrad_out", [y], dh1b, True)
    dy, sib_out = _out_bwd(dh1b, w_out_f, comm=[_t_sibling(pieces(gobf))])
    pbf_out, own_out = _chip_sum("chip_sum_w_out", pieces(gobf), sib_out, core, chip)
    dproj, dcw8, dqg, dkg, dgco, dgao, dsink, dbias, chips_up = _mix_bwd(
        proj, dy, sinks, cw8, qg, kg, gco, gao, bias, comm=[_t_chips(pbf_up)])
    fin_up, = _final_sum("final_sum_w_up", own_up, chips_up, core)
    gibf, chips_out, fin_up = _wgrad("wgrad_in", [dproj], u1, False, comm=[_t_chips(pbf_out), _t_swap(fin_up)])
    fin_out, sib_in = _final_sum("final_sum_w_out", own_out, chips_out, core, comm=[_t_sibling(pieces(gibf))])
    pbf_in, own_in = _chip_sum("chip_sum_w_in", pieces(gibf), sib_in, core, chip)
    dx, g1_all, tbl_all, pm_all, chips_in, fin_out = _norm_matmul_bwd(
        "in_bwd", [dproj], w_int, [0], xs, g1, dh1, False, comm=[_t_chips(pbf_in), _t_swap(fin_out)], slot=me,
        band=(dbias, bucket), pack=(dg2, dgco, dgao, dcw8, dqg, dkg, dsink))
    fin_in, g1_all, tbl_all, pm_all = _final_sum(
        "final_sum_w_in", own_in, chips_in, core, comm=[_t_allgather(g1_all), _t_allgather(tbl_all), _t_allgather(pm_all)])
    fin_in, = _comm_call("swap_last", [_t_swap(fin_in)])

    g_w_out, g_w_down = whole(fin_out), whole(fin_down)
    g_w_down, d_down, nm_down, nv_down = _adamw("adamw_w_down", w_down[0], g_w_down, m_w_down[0], v_w_down[0], 352, True)
    g_w_up, d_up, nm_up, nv_up = _adamw(
        "adamw_w_up", w_up[0], whole(fin_up), m_w_up[0], v_w_up[0], 256, True, stage=False, g_transposed=True)
    g_w_out, d_out, nm_out, nv_out = _adamw("adamw_w_out", w_out[0], g_w_out, m_w_out[0], v_w_out[0], 256, True, stage=False)
    g_w_in, d_in, nm_in, nv_in = [a.T for a in _adamw(
        "adamw_w_in", w_in[0].T, whole(fin_in), m_w_in[0].T, v_w_in[0].T, INW // N_CHIPS // 3, True, stage=False)]
    taps = lambda a: jnp.transpose(a, (1, 0, 2))
    sw = [norm_mix_g, taps(conv_w), q_norm_g, k_norm_g, rel_bias_table.T, sinks, out_norm_conv_g, out_norm_attn_g,
          norm_ffn_g, taps(ffn_conv_w), ffn_conv_b]
    smm = [m_norm_mix_g, taps(m_conv_w), m_q_norm_g, m_k_norm_g, m_rel_bias_table.T, m_sinks, m_out_norm_conv_g,
           m_out_norm_attn_g, m_norm_ffn_g, taps(m_ffn_conv_w), m_ffn_conv_b]
    smv = [v_norm_mix_g, taps(v_conv_w), v_q_norm_g, v_k_norm_g, v_rel_bias_table.T, v_sinks, v_out_norm_conv_g,
           v_out_norm_attn_g, v_norm_ffn_g, taps(v_ffn_conv_w), v_ffn_conv_b]
    *small_out, loss = _small_adam(chip, p_all, pm_all, g1_all, tbl_all, sw, smm, smv)
    sg, sd, snm, snv = [list(r) for r in small_out]
    for r in (sg, sd, snm, snv):
        r[1], r[4], r[9] = taps(r[1]), r[4].T, taps(r[9])

    def order(s, b_in, b_out, b_up, b_down):
        return (s[0], b_in[None], s[1], s[2], s[3], s[4], s[5], s[6], s[7], b_out[None], s[8], b_up[None],
                s[9], s[10], b_down[None])

    return (loss.reshape(()), dx[None],
            *order(sg, g_w_in, g_w_out, g_w_up, g_w_down),
            *order(sd, d_in, d_out, d_up, d_down),
            *order(snm, nm_in, nm_out, nm_up, nm_down),
            *order(snv, nv_in, nv_out, nv_up, nv_down))
```

```python
import functools
import math

import numpy as np

import jax
import jax.numpy as jnp
from jax import lax
from jax.experimental import pallas as pl
from jax.experimental.pallas import tpu as pltpu

F32 = jnp.float32
BF = jnp.bfloat16
SDS = jax.ShapeDtypeStruct

T = 2048
D = 1024
CW = 512
AW = 512
HD = 64
NH = 8
NKV = 2
GQ = 4
INW = 2304
DFF = 2816
BLK = 128
NB = T // BLK
NBUCKET = 32
EPS = 1e-6
NEG_INF = -1e30
N_CHIPS = 4
N_DEV = 8

ADAM_LR = 0.001
ADAM_B1 = 0.9
ADAM_B2 = 0.999
ADAM_EPS = 1e-08
ADAM_WD = 0.01
ADAM_STEP = 10

TM = 512
MIB = 1024 * 1024
MESH = pl.DeviceIdType.MESH
ANY = pl.BlockSpec(memory_space=pl.ANY)

_pcall = pl.pallas_call


def _params(sem=None, vmem_mib=None, collective_id=None):
    kw = {} if collective_id is None else {"collective_id": collective_id}
    if sem is not None:
        kw["dimension_semantics"] = sem
    if vmem_mib is not None:
        kw["vmem_limit_bytes"] = vmem_mib * MIB
    return pltpu.CompilerParams(**kw)


def _resident(shape):
    return pl.BlockSpec(shape, lambda *_: (0,) * len(shape), pipeline_mode=pl.Buffered(1))


def _dot(a, b, ca, cb):
    return lax.dot_general(a, b, (((ca,), (cb,)), ((), ())), preferred_element_type=F32)


def _rms_bwd(dy, x, r, g):
    dg = jnp.sum(dy * (x * r), axis=0, keepdims=True)
    dgx = dy * g
    dx = r * dgx - x * (r * r * r) * jnp.mean(x * dgx, axis=-1, keepdims=True)
    return dx, dg


def _where():
    x, y, c = lax.axis_index("x"), lax.axis_index("y"), lax.axis_index("c")
    return x, y, c, [(1 - x, y), (x, 1 - y), (1 - x, 1 - y)]


def _rcopy(src, dst, ssem, rsem, dev):
    return pltpu.make_async_remote_copy(src_ref=src, dst_ref=dst, send_sem=ssem, recv_sem=rsem, device_id=dev,
                                        device_id_type=MESH)


SIBLING, Y_CHIP, X_CHIP, DIAGONAL_CHIP = 1, 2, 4, 6
OTHER_CHIPS = (Y_CHIP, X_CHIP, DIAGONAL_CHIP)
EVERYONE = tuple(range(1, N_DEV))
BARRIER_OF = {(SIBLING,): 0, (SIBLING, Y_CHIP, X_CHIP): 1, OTHER_CHIPS: 2, (SIBLING,) + OTHER_CHIPS: 3, EVERYONE: 4}


def _peer(rel):
    x, y, c, _ = _where()
    return x ^ ((rel >> 2) & 1), y ^ ((rel >> 1) & 1), c ^ (rel & 1)


class _Task:
    def __init__(self, ins, outs, alias, n_sem, start, finish, middle=None, peers=()):
        self.ins, self.outs, self.alias, self.n_sem, self.start, self.finish = ins, outs, alias, n_sem, start, finish
        self.middle = middle if middle is not None else (lambda *args: None)
        self.peers = peers


def _peers_of(comm):
    return tuple(sorted({p for t in comm for p in t.peers}))


def _enter(comm):
    peers = _peers_of(comm)
    barrier = pltpu.get_barrier_semaphore()
    for rel in peers:
        pl.semaphore_signal(barrier, inc=1, device_id=_peer(rel), device_id_type=MESH)
    pl.semaphore_wait(barrier, len(peers))


ROWS16 = 16


def _t_gather(placed, part=(0, 1), relayed_first=False):
    R = placed.shape[0] // N_CHIPS
    q = R // 4
    lo, hi = (round(f * (q // ROWS16)) * ROWS16 for f in part)

    def quarter(chip_index, core, k):
        return pl.ds(pl.multiple_of(chip_index * R + core * 2 * q + k * q + lo, ROWS16), hi - lo)

    def places():
        x, y, c, _ = _where()
        return c, 2 * x + y, 2 * (1 - x) + y, 2 * x + (1 - y), 2 * (1 - x) + (1 - y), (1 - x, y, c), (x, 1 - y, c), (x, y, 1 - c)

    def copy(buf, k, chip_index, core, quart, ss, rs, b, dev):
        window = buf.at[quarter(chip_index, core, quart)]
        return _rcopy(window, window, ss.at[b + k], rs.at[b + k], dev)

    def first_hop(cout, ss, rs, b, which):
        c, me, _, _, _, x_nbr, y_nbr, _ = places()
        for k, (quart, dev) in enumerate(((0, x_nbr), (1, y_nbr), (1, x_nbr), (0, y_nbr))):
            if k in which:
                copy(cout[0], k, me, c, quart, ss, rs, b, dev).start()

    def start(cin, cout, ss, rs, b):
        first_hop(cout, ss, rs, b, (0, 1) if relayed_first else (0, 1, 2, 3))

    def middle(cin, cout, ss, rs, b):
        c, _, xc, yc, _, x_nbr, y_nbr, sib = places()
        for k, chip_index, quart, dev in ((0, xc, 0, y_nbr), (1, yc, 1, x_nbr)):
            copy(cout[0], k, chip_index, c, quart, ss, rs, b, dev).wait_recv()
            copy(cout[0], 4 + k, chip_index, c, quart, ss, rs, b, dev).start()
            copy(cout[0], 6 + k, chip_index, c, quart, ss, rs, b, sib).start()
        if relayed_first:
            first_hop(cout, ss, rs, b, (2, 3))

    later = ((2, 1, 1), (3, 2, 0), (4, 3, 0), (5, 3, 1))

    def finish(cin, cout, ss, rs, b):
        c, me, xc, yc, dc, _, _, sib = places()
        chip_of = {1: xc, 2: yc, 3: dc}
        for k, whose, quart in later:
            copy(cout[0], k, chip_of[whose], c, quart, ss, rs, b, sib).wait_recv()
            copy(cout[0], 6 + k, chip_of[whose], c, quart, ss, rs, b, sib).start()
        for k, whose, quart in ((0, 1, 0), (1, 2, 1)) + later:
            copy(cout[0], 6 + k, chip_of[whose], 1 - c, quart, ss, rs, b, sib).wait_recv()
        for k in range(12):
            copy(cout[0], k, me, c, 0, ss, rs, b, sib).wait_send()

    return _Task([placed], [SDS(placed.shape, placed.dtype)], [(0, 0)], 12, start, finish, middle, peers=(SIBLING, Y_CHIP, X_CHIP))


def _t_small_weights(buf):
    def start(cin, cout, ss, rs, b):
        x, y, c, chips = _where()
        mine = cout[0].at[2 * x + y]
        for r, (px, py) in enumerate(chips):
            _rcopy(mine, mine, ss.at[b + r], rs.at[b + r], (px, py, c)).start()

    def finish(cin, cout, ss, rs, b):
        x, y, c, chips = _where()
        for r, (px, py) in enumerate(chips):
            got = cout[0].at[2 * px + py]
            _rcopy(got, got, ss.at[b + r], rs.at[b + r], (px, py, c)).wait_recv()
        for r, (px, py) in enumerate(chips):
            mine = cout[0].at[2 * x + y]
            _rcopy(mine, mine, ss.at[b + r], rs.at[b + r], (px, py, c)).wait_send()

    return _Task([buf], [SDS(buf.shape, buf.dtype)], [(0, 0)], 3, start, finish, peers=OTHER_CHIPS)


def _t_sibling(gbf):
    def start(cin, cout, ss, rs, b):
        x, y, c, _ = _where()
        for jj in range(N_CHIPS):
            _rcopy(cin[0].at[2 * jj + (1 - c)], cout[0].at[jj], ss.at[b + jj], rs.at[b + jj], (x, y, 1 - c)).start()

    def finish(cin, cout, ss, rs, b):
        x, y, c, _ = _where()
        for jj in range(N_CHIPS):
            got = cout[0].at[jj]
            _rcopy(got, got, ss.at[b + jj], rs.at[b + jj], (x, y, 1 - c)).wait_recv()
        for jj in range(N_CHIPS):
            got = cout[0].at[jj]
            _rcopy(got, got, ss.at[b + jj], rs.at[b + jj], (x, y, 1 - c)).wait_send()

    return _Task([gbf], [SDS((N_CHIPS,) + gbf.shape[1:], BF)], [], N_CHIPS, start, finish, peers=(SIBLING,))


def _t_chips(pbf):
    def start(cin, cout, ss, rs, b):
        x, y, c, chips = _where()
        for r, (px, py) in enumerate(chips):
            _rcopy(cin[0].at[2 * px + py], cout[0].at[r], ss.at[b + r], rs.at[b + r], (px, py, c)).start()

    def finish(cin, cout, ss, rs, b):
        x, y, c, chips = _where()
        for r, (px, py) in enumerate(chips):
            got = cout[0].at[r]
            _rcopy(got, got, ss.at[b + r], rs.at[b + r], (px, py, c)).wait_recv()
        for r, (px, py) in enumerate(chips):
            got = cout[0].at[r]
            _rcopy(got, got, ss.at[b + r], rs.at[b + r], (px, py, c)).wait_send()

    return _Task([pbf], [SDS((3,) + pbf.shape[1:], BF)], [], 3, start, finish, peers=OTHER_CHIPS)


def _t_swap(fin):
    def start(cin, cout, ss, rs, b):
        x, y, c, _ = _where()
        mine = cout[0].at[c]
        _rcopy(mine, mine, ss.at[b], rs.at[b], (x, y, 1 - c)).start()

    def finish(cin, cout, ss, rs, b):
        x, y, c, _ = _where()
        got = cout[0].at[1 - c]
        _rcopy(got, got, ss.at[b], rs.at[b], (x, y, 1 - c)).wait_recv()
        _rcopy(got, got, ss.at[b], rs.at[b], (x, y, 1 - c)).wait_send()

    return _Task([fin], [SDS(fin.shape, fin.dtype)], [(0, 0)], 1, start, finish, peers=(SIBLING,))


def _t_allgather(buf):
    def peers():
        x, y, c, _ = _where()
        out = []
        for rel in range(1, N_DEV):
            px, py, pc = x ^ ((rel >> 2) & 1), y ^ ((rel >> 1) & 1), c ^ (rel & 1)
            out.append((rel - 1, 4 * px + 2 * py + pc, (px, py, pc)))
        return 4 * x + 2 * y + c, out

    def start(cin, cout, ss, rs, b):
        me, ps = peers()
        mine = cout[0].at[me]
        for k, _, dev in ps:
            _rcopy(mine, mine, ss.at[b + k], rs.at[b + k], dev).start()

    def finish(cin, cout, ss, rs, b):
        me, ps = peers()
        for k, pidx, dev in ps:
            got = cout[0].at[pidx]
            _rcopy(got, got, ss.at[b + k], rs.at[b + k], dev).wait_recv()
        for k, _, dev in ps:
            mine = cout[0].at[me]
            _rcopy(mine, mine, ss.at[b + k], rs.at[b + k], dev).wait_send()

    return _Task([buf], [SDS(buf.shape, buf.dtype)], [(0, 0)], N_DEV - 1, start, finish, peers=EVERYONE)


def _run_tasks(comm, which, cin, cout, ss, rs):
    i0 = o0 = s0 = 0
    for t in comm:
        getattr(t, which)(cin[i0:i0 + len(t.ins)], cout[o0:o0 + len(t.outs)], ss, rs, s0)
        i0, o0, s0 = i0 + len(t.ins), o0 + len(t.outs), s0 + t.n_sem


def _from_hbm(*arrays):
    return [pltpu.with_memory_space_constraint(a, pltpu.HBM) for a in arrays]


def _in_hbm(shapes):
    return [pltpu.HBM(s.shape, s.dtype) for s in shapes]


def _comm_layout(comm, n_in, n_out):
    c_in = [a for t in comm for a in t.ins]
    c_out = [s for t in comm for s in t.outs]
    aliases, i0, o0 = {}, 0, 0
    for t in comm:
        for i, o in t.alias:
            aliases[n_in + i0 + i] = n_out + o0 + o
        i0, o0 = i0 + len(t.ins), o0 + len(t.outs)
    return c_in, c_out, aliases, sum(t.n_sem for t in comm)


def _call(body, operands, *, name, grid, in_specs, out_specs, out_shape, scratch_shapes=(), sem=None, vmem_mib=None, comm=(),
          free=(), prefetch=()):
    operands = [o if s.memory_space == pltpu.SMEM or k in free else pltpu.with_memory_space_constraint(o, pltpu.HBM)
                for k, (o, s) in enumerate(zip(operands, in_specs))]
    n_pre, n_in, n_out, n_scr = len(prefetch), len(in_specs), len(out_specs), len(scratch_shapes)
    c_in, c_out, aliases, n_sem = _comm_layout(comm, n_pre + n_in, n_out)
    sems = [pltpu.SemaphoreType.DMA((n_sem,)), pltpu.SemaphoreType.DMA((n_sem,))] if comm else []

    def wrapped(*refs):
        pre, refs = refs[:n_pre], refs[n_pre:]
        ins, cin = refs[:n_in], refs[n_in:n_in + len(c_in)]
        rest = refs[n_in + len(c_in):]
        outs, cout = rest[:n_out], rest[n_out:n_out + len(c_out)]
        rest = rest[n_out + len(c_out):]
        scr, csem = rest[:n_scr], rest[n_scr:]
        if not comm:
            return body(*pre, *ins, *outs, *scr)
        step = functools.reduce(lambda acc, k: acc * grid[k] + pl.program_id(k), range(len(grid)), 0)
        n_steps = math.prod(grid)

        @pl.when(step == 0)
        def _():
            _enter(comm)
            _run_tasks(comm, "start", cin, cout, *csem)

        pl.when(step == n_steps // 2)(lambda: _run_tasks(comm, "middle", cin, cout, *csem))
        body(*pre, *ins, *outs, *scr)
        pl.when(step == n_steps - 1)(lambda: _run_tasks(comm, "finish", cin, cout, *csem))

    grid_spec = pltpu.PrefetchScalarGridSpec(
        num_scalar_prefetch=n_pre, grid=grid, in_specs=list(in_specs) + [ANY] * len(c_in),
        out_specs=list(out_specs) + [ANY] * len(c_out), scratch_shapes=list(scratch_shapes) + sems)
    return _pcall(
        wrapped, name=name, grid_spec=grid_spec, out_shape=_in_hbm(list(out_shape) + c_out), input_output_aliases=aliases,
        compiler_params=_params(("arbitrary",) * len(grid) if comm else sem, vmem_mib,
                                BARRIER_OF[_peers_of(comm)] if comm else None),
    )(*prefetch, *operands, *_from_hbm(*c_in))


def _comm_call(name, comm):
    c_in, c_out, aliases, n_sem = _comm_layout(comm, 0, 0)

    def body(*refs):
        cin, cout, (ss, rs) = refs[:len(c_in)], refs[len(c_in):len(c_in) + len(c_out)], refs[len(c_in) + len(c_out):]
        _enter(comm)
        for phase in ("start", "middle", "finish"):
            _run_tasks(comm, phase, cin, cout, ss, rs)

    return _pcall(
        body, name=name, in_specs=[ANY] * len(c_in), out_specs=[ANY] * len(c_out), out_shape=_in_hbm(c_out),
        scratch_shapes=[pltpu.SemaphoreType.DMA((n_sem,)), pltpu.SemaphoreType.DMA((n_sem,))],
        input_output_aliases=aliases, compiler_params=_params(collective_id=BARRIER_OF[_peers_of(comm)]),
    )(*_from_hbm(*c_in))


def _inproj(x, g1, w_int, comm=()):
    tm = TM

    def body(x_ref, g_ref, w_ref, proj_ref, u_ref):
        xf = x_ref[...]
        r = lax.rsqrt(jnp.mean(xf * xf, axis=-1, keepdims=True) + EPS)
        u = (xf * r * g_ref[...]).astype(BF)
        u_ref[...] = u
        proj_ref[...] = _dot(u, w_ref[...], 1, 1)

    return _call(
        body, (x, g1, w_int), name="inproj", grid=(T // tm,),
        in_specs=[pl.BlockSpec((tm, D), lambda i: (i, 0)), pl.BlockSpec((1, D), lambda i: (0, 0)),
                  _resident((INW, D))],
        out_specs=[pl.BlockSpec((tm, INW), lambda i: (i, 0)), pl.BlockSpec((tm, D), lambda i: (i, 0))],
        out_shape=[SDS((T, INW), F32), SDS((T, D), BF)], sem=("parallel",), vmem_mib=40, comm=comm, free=(0, 1))


def _outproj(y, w_out, x, g2):
    tm = TM

    def body(y_ref, w_ref, x_ref, g_ref, h1_ref, u2_ref):
        h1 = x_ref[...] + _dot(y_ref[...], w_ref[...], 1, 0)
        h1_ref[...] = h1
        r = lax.rsqrt(jnp.mean(h1 * h1, axis=-1, keepdims=True) + EPS)
        u2_ref[...] = (h1 * r * g_ref[...]).astype(BF)

    return _call(
        body, (y, w_out, x, g2), name="outproj", grid=(T // tm,),
        in_specs=[pl.BlockSpec((tm, D), lambda i: (i, 0)), _resident((D, D)),
                  pl.BlockSpec((tm, D), lambda i: (i, 0)), pl.BlockSpec((1, D), lambda i: (0, 0))],
        out_specs=[pl.BlockSpec((tm, D), lambda i: (i, 0)), pl.BlockSpec((tm, D), lambda i: (i, 0))],
        out_shape=[SDS((T, D), F32), SDS((T, D), BF)], sem=("parallel",), vmem_mib=32, free=(2, 3))


def _ffn_up(u2, w_upt, comm=()):
    tm, tn = T, 512

    def body(u_ref, w_ref, o_ref):
        o_ref[...] = _dot(u_ref[...], w_ref[...], 1, 1).astype(BF)

    return _call(
        body, (u2, w_upt), name="ffn_up", grid=(T // tm, 2 * DFF // tn),
        in_specs=[pl.BlockSpec((tm, D), lambda i, j: (i, 0)), pl.BlockSpec((tn, D), lambda i, j: (j, 0))],
        out_specs=[pl.BlockSpec((tm, tn), lambda i, j: (i, j))], out_shape=[SDS((T, 2 * DFF), BF)],
        sem=("parallel", "parallel"), vmem_mib=32, comm=comm, free=(1,))


def _ffn_down(a, w_down, h1, tgt):
    tm = TM

    def body(a_ref, w_ref, h1_ref, t_ref, dh_ref, dhb_ref, l_ref):
        @pl.when(pl.program_id(0) == 0)
        def _():
            l_ref[...] = jnp.zeros_like(l_ref)

        h2 = h1_ref[...] + _dot(a_ref[...], w_ref[...], 1, 0)
        e = h2 - t_ref[...]
        dh = e * (1.0 / D)
        dh_ref[...] = dh
        dhb_ref[...] = dh.astype(BF)
        e2 = jnp.sum((e * e).reshape(tm // 8, 8, D), axis=0)
        acc = e2[:, 0:128]
        for k in range(1, D // 128):
            acc = acc + e2[:, k * 128:(k + 1) * 128]
        l_ref[...] += acc

    return _call(
        body, (a, w_down, h1, tgt), name="ffn_down", grid=(T // tm,),
        in_specs=[pl.BlockSpec((tm, DFF), lambda i: (i, 0)), _resident((DFF, D)),
                  pl.BlockSpec((tm, D), lambda i: (i, 0)), pl.BlockSpec((tm, D), lambda i: (i, 0))],
        out_specs=[pl.BlockSpec((tm, D), lambda i: (i, 0)), pl.BlockSpec((tm, D), lambda i: (i, 0)),
                   pl.BlockSpec((8, 128), lambda i: (0, 0))],
        out_shape=[SDS((T, D), F32), SDS((T, D), BF), SDS((8, 128), F32)], sem=("arbitrary",), vmem_mib=40, free=(2, 3))


def _bucket_table():
    q = np.arange(BLK, dtype=np.int32)[:, None]
    j = np.arange(2 * BLK, dtype=np.int32)[None, :]
    n = np.maximum(q + BLK - j, 0)
    nf = np.maximum(n, 1).astype(np.float32)
    max_exact = NBUCKET // 2
    large = max_exact + (np.log(nf / np.float32(max_exact)) / np.float32(math.log(BLK / max_exact))
                         * np.float32(NBUCKET - max_exact)).astype(np.int32)
    large = np.minimum(large, NBUCKET - 1)
    return np.where(n < max_exact, n, large).astype(np.int32)


def _two_bf16(x):
    hi = x.astype(BF)
    return hi, (x - hi.astype(F32)).astype(BF)


def _head_sums(x, seg):
    hi, lo = _two_bf16(x)
    s = seg[0:x.shape[1], :]
    return _dot(hi, s, 1, 0) + _dot(lo, s, 1, 0)


def _head_spread(v, seg, width):
    hi, lo = _two_bf16(v)
    s = seg[0:width, :]
    return _dot(hi, s, 1, 1) + _dot(lo, s, 1, 1)


def _head_norm(x, g_t, seg, by_head=False):
    if by_head:
        heads = [x[:, h * HD:(h + 1) * HD] for h in range(x.shape[1] // HD)]
        r = jnp.concatenate([jnp.broadcast_to(lax.rsqrt(jnp.mean(v * v, axis=-1, keepdims=True) + EPS), v.shape)
                             for v in heads], axis=1)
    else:
        r = lax.rsqrt(_head_sums(x * x, seg) * (1.0 / HD) + EPS)
        r = _head_spread(r, seg, x.shape[1])
    return x * r * g_t, r


def _head_norm_bwd(dy, x, r, g_t, seg):
    dg_t = jnp.sum(dy * (x * r), axis=0, keepdims=True)
    dgx = dy * g_t
    mean = _head_spread(_head_sums(x * dgx, seg) * (1.0 / HD), seg, x.shape[1])
    return r * dgx - x * (r * r * r) * mean, dg_t


def _fold_heads(v):
    out = v[:, 0:HD]
    for h in range(1, v.shape[1] // HD):
        out = out + v[:, h * HD:(h + 1) * HD]
    return out


def _mix_forward(P, zc8, zh8, pkv, first, cw, qg_t, kg_t, gco, gao, seg, sink_ref, bias_ref, by_head=False):
    gate_b = P[:, 0:CW]
    gate_c = P[:, CW:2 * CW]
    hc = P[:, 2 * CW:3 * CW]
    z = gate_c * hc
    keep = jnp.where(first, 0.0, 1.0)
    zp = zc8 * zh8 * keep
    p1 = zp[7:8, :]
    p2 = zp[6:7, :]
    row = lax.broadcasted_iota(jnp.int32, (BLK, 1), 0)
    z1 = jnp.where(row == 0, p1, pltpu.roll(z, 1, 0))
    z2 = jnp.where(row == 0, p2, jnp.where(row == 1, p1, pltpu.roll(z, 2, 0)))
    cz = cw[0:1, :] * z2 + cw[1:2, :] * z1 + cw[2:3, :] * z
    y_conv = gate_b * cz

    scale = HD ** -0.5
    qi = lax.broadcasted_iota(jnp.int32, (BLK, 2 * BLK), 0)
    kj = lax.broadcasted_iota(jnp.int32, (BLK, 2 * BLK), 1)
    dd = qi + BLK - kj
    first_key = jnp.where(first, BLK, 0)
    valid = (dd >= 0) & (dd < BLK) & (kj >= first_key)

    q0 = 3 * CW
    k0 = q0 + AW
    v0 = k0 + NKV * HD
    q_raw = P[:, q0:k0]
    qn, rq = _head_norm(q_raw, qg_t, seg, by_head)
    qs = (qn * scale).astype(BF)
    k_raw = jnp.concatenate([pkv[:, 0:NKV * HD], P[:, k0:v0]], axis=0)
    kn, rk = _head_norm(k_raw, kg_t, seg, by_head)
    knb = kn.astype(BF)
    heads = []
    for h in range(NH):
        kv = h // GQ
        kb = knb[:, kv * HD:(kv + 1) * HD]
        vb = jnp.concatenate([pkv[:, NKV * HD + kv * HD:NKV * HD + (kv + 1) * HD],
                              P[:, v0 + kv * HD:v0 + (kv + 1) * HD]], axis=0).astype(BF)
        Q = qs[:, h * HD:(h + 1) * HD]
        S = _dot(Q, kb, 1, 1) + bias_ref[h * BLK:(h + 1) * BLK, :]
        S = jnp.where(valid, S, NEG_INF)
        sink = sink_ref[0, h]
        m = jnp.maximum(jnp.max(S, axis=-1, keepdims=True), sink)
        p = jnp.exp(S - m)
        es = jnp.exp(sink - m)
        denom = jnp.sum(p, axis=-1, keepdims=True) + es
        probs = p / denom
        O = _dot(probs.astype(BF), vb, 1, 0)
        heads.append(dict(kb=kb, vb=vb, Q=Q, probs=probs, psink=es / denom, O=O))
    y_attn = jnp.concatenate([hd["O"] for hd in heads], axis=1)

    rc = lax.rsqrt(jnp.mean(y_conv * y_conv, axis=-1, keepdims=True) + EPS)
    ra = lax.rsqrt(jnp.mean(y_attn * y_attn, axis=-1, keepdims=True) + EPS)
    y = jnp.concatenate([y_conv * rc * gco, y_attn * ra * gao], axis=1)
    return dict(gate_b=gate_b, gate_c=gate_c, hc=hc, z=z, z1=z1, z2=z2, cz=cz, y_conv=y_conv, y_attn=y_attn,
                rc=rc, ra=ra, heads=heads, y=y, row=row, scale=scale, q_raw=q_raw, rq=rq, k_raw=k_raw, rk=rk)


BPS = 2
TILE = BPS * BLK
KV0 = 3 * CW + AW


def _mix_in_specs(tile_of):
    return [
        pl.BlockSpec(memory_space=pltpu.SMEM),
        pl.BlockSpec((TILE, INW), lambda s: (tile_of(s), 0)),
        pl.BlockSpec((8, CW), lambda s: (jnp.maximum(tile_of(s) * (TILE // 8) - 1, 0), 1)),
        pl.BlockSpec((8, CW), lambda s: (jnp.maximum(tile_of(s) * (TILE // 8) - 1, 0), 2)),
        pl.BlockSpec((BLK, 2 * NKV * HD), lambda s: (jnp.maximum(tile_of(s) * BPS - 1, 0), KV0 // (2 * NKV * HD))),
    ]


def _block_inputs(tile, b, zc_ref, zh_ref, pkv_ref, first_tile):
    P = tile[b * BLK:(b + 1) * BLK, :]
    if b == 0:
        return P, zc_ref[...], zh_ref[...], pkv_ref[...], first_tile
    lo = b * BLK
    return P, tile[lo - 8:lo, CW:2 * CW], tile[lo - 8:lo, 2 * CW:3 * CW], tile[lo - BLK:lo, KV0:KV0 + 2 * NKV * HD], False


def _mix_param_specs():
    return [
        pl.BlockSpec((8, CW), lambda s: (0, 0)),
        pl.BlockSpec((1, AW), lambda s: (0, 0)),
        pl.BlockSpec((1, NKV * HD), lambda s: (0, 0)),
        pl.BlockSpec((1, CW), lambda s: (0, 0)),
        pl.BlockSpec((1, AW), lambda s: (0, 0)),
        pl.BlockSpec((AW, 128), lambda s: (0, 0)),
        pl.BlockSpec((NH * BLK, 2 * BLK), lambda s: (0, 0)),
    ]


def _mix_params(cw8, qg, kg, gco, gao, bias):
    seg = np.zeros((AW, 128), np.float32)
    seg[np.arange(AW), np.arange(AW) // HD] = 1.0
    return (cw8, jnp.tile(qg, (1, NH)), jnp.tile(kg, (1, NKV)), gco, gao, jnp.asarray(seg, BF), bias)


def _mix_fwd(proj, sinks, cw8, qg, kg, gco, gao, bias, comm=()):
    def body(sink_ref, p_ref, zc_ref, zh_ref, pkv_ref, cw_ref, qg_ref, kg_ref, gco_ref, gao_ref, seg_ref, bias_ref, y_ref):
        tile = p_ref[...]
        for b in range(BPS):
            f = _mix_forward(*_block_inputs(tile, b, zc_ref, zh_ref, pkv_ref, pl.program_id(0) == 0), cw_ref[...],
                             qg_ref[...], kg_ref[...], gco_ref[...], gao_ref[...], seg_ref[...], sink_ref, bias_ref, by_head=True)
            y_ref[b * BLK:(b + 1) * BLK, :] = f["y"].astype(BF)

    return _call(
        body, (sinks, proj, proj, proj, proj, *_mix_params(cw8, qg, kg, gco, gao, bias)), name="mix_fwd", grid=(T // TILE,),
        in_specs=_mix_in_specs(lambda s: s) + _mix_param_specs(),
        out_specs=[pl.BlockSpec((TILE, D), lambda s: (s, 0))], out_shape=[SDS((T, D), BF)],
        sem=("parallel",), vmem_mib=40, comm=comm, free=tuple(range(5, 12)))


def _mix_bwd(proj, dy, sinks, cw8, qg, kg, gco, gao, bias, comm=()):
    n_steps = T // TILE

    def tile_of(s):
        return n_steps - 1 - s

    def body(sink_ref, p_ref, zc_ref, zh_ref, pkv_ref, dy_ref, cw_ref, qg_ref, kg_ref, gco_ref, gao_ref, seg_ref, bias_ref,
             dproj_ref, dcw_ref, dqg_ref, dkg_ref, dgco_ref, dgao_ref, dsink_ref, dbias_ref,
             ndcz_ref, dkc_ref, dvc_ref):
        s = pl.program_id(0)

        @pl.when(s == 0)
        def _():
            for r in (dcw_ref, dqg_ref, dkg_ref, dgco_ref, dgao_ref, dsink_ref, dbias_ref, ndcz_ref, dkc_ref, dvc_ref):
                r[...] = jnp.zeros_like(r)

        params = (cw_ref[...], qg_ref[...], kg_ref[...], gco_ref[...], gao_ref[...], seg_ref[...])
        tile = p_ref[...]
        carry = (ndcz_ref[...], dkc_ref[...], dvc_ref[...])
        total = None
        for b in reversed(range(BPS)):
            f = _mix_forward(*_block_inputs(tile, b, zc_ref, zh_ref, pkv_ref, s == n_steps - 1), *params, sink_ref, bias_ref)
            pieces, sums, carry = one_block(f, dy_ref[b * BLK:(b + 1) * BLK, :], params, carry)
            for lo, piece in pieces:
                dproj_ref[b * BLK:(b + 1) * BLK, lo:lo + piece.shape[1]] = piece
            total = sums if total is None else [t + v for t, v in zip(total, sums)]
        ndcz_ref[...], dkc_ref[...], dvc_ref[...] = carry
        dcw, dqg_t, dkg_t, dgco, dgao, dsink, *ds = total
        dcw_ref[0:3, :] += dcw
        dqg_ref[...] += _fold_heads(dqg_t)
        dkg_ref[...] += _fold_heads(dkg_t)
        dgco_ref[...] += dgco
        dgao_ref[...] += dgao
        dsink_ref[...] += dsink
        for h in range(NH):
            dbias_ref[h * BLK:(h + 1) * BLK, :] += ds[h]

    def one_block(f, dy, params, carry):
        cw, qg_v, kg_v, gco_v, gao_v, seg = params
        nxt, dk_carry, dv_carry = carry
        dyc, dgco = _rms_bwd(dy[:, 0:CW], f["y_conv"], f["rc"], gco_v)
        dya, dgao = _rms_bwd(dy[:, CW:CW + AW], f["y_attn"], f["ra"], gao_v)

        row = f["row"]
        dgate_b = dyc * f["cz"]
        dcz = dyc * f["gate_b"]
        dcw = jnp.concatenate([jnp.sum(dcz * f[k], axis=0, keepdims=True) for k in ("z2", "z1", "z")], axis=0)
        n0 = nxt[0:1, :]
        n1 = nxt[1:2, :]
        d1 = jnp.where(row == BLK - 1, n0, pltpu.roll(dcz, BLK - 1, 0))
        d2 = jnp.where(row == BLK - 1, n1, jnp.where(row == BLK - 2, n0, pltpu.roll(dcz, BLK - 2, 0)))
        dz = cw[2:3, :] * dcz + cw[1:2, :] * d1 + cw[0:1, :] * d2
        pieces = [(0, dgate_b.astype(BF)), (CW, (dz * f["hc"]).astype(BF)), (2 * CW, (dz * f["gate_c"]).astype(BF))]

        scale = f["scale"]
        lane = lax.broadcasted_iota(jnp.int32, (1, 128), 1)
        dsink = jnp.zeros((1, 128), F32)
        dq_cols, dk_cols, dv_cols, dk_prev, dv_prev, ds = [], [], [], [], [], []
        for kv in range(NKV):
            dKb = dVb = 0.0
            for h in range(kv * GQ, (kv + 1) * GQ):
                hd = f["heads"][h]
                dO = dya[:, h * HD:(h + 1) * HD]
                delta = jnp.sum(dO * hd["O"], axis=-1, keepdims=True)
                dOb = dO.astype(BF)
                dP = _dot(dOb, hd["vb"], 1, 1)
                dS = hd["probs"] * (dP - delta)
                tot = jnp.sum(hd["psink"] * delta, axis=0, keepdims=True)
                dsink = dsink - jnp.where(lane == h, tot, 0.0)
                ds.append(dS)
                dSb = dS.astype(BF)
                dq_cols.append(_dot(dSb, hd["kb"], 1, 0))
                dKb = dKb + _dot(dSb, hd["Q"], 0, 0)
                dVb = dVb + _dot(hd["probs"].astype(BF), dOb, 0, 0)
            dk_cols.append(dKb[BLK:, :] + dk_carry[:, kv * HD:(kv + 1) * HD])
            dv_cols.append(dVb[BLK:, :] + dv_carry[:, kv * HD:(kv + 1) * HD])
            dk_prev.append(dKb[:BLK, :])
            dv_prev.append(dVb[:BLK, :])
        dq_raw, dqg_t = _head_norm_bwd(jnp.concatenate(dq_cols, axis=1) * scale, f["q_raw"], f["rq"], qg_v, seg)
        dk_raw, dkg_t = _head_norm_bwd(jnp.concatenate(dk_cols, axis=1), f["k_raw"][BLK:, :], f["rk"][BLK:, :], kg_v, seg)
        pieces.append((3 * CW, jnp.concatenate([dq_raw, dk_raw] + dv_cols, axis=1).astype(BF)))
        owed = (dcz[0:8, :], jnp.concatenate(dk_prev, axis=1), jnp.concatenate(dv_prev, axis=1))
        return pieces, [dcw, dqg_t, dkg_t, dgco, dgao, dsink, *ds], owed

    small = lambda r, c: pl.BlockSpec((r, c), lambda s: (0, 0))
    return _call(
        body, (sinks, proj, proj, proj, proj, dy, *_mix_params(cw8, qg, kg, gco, gao, bias)), name="mix_bwd", grid=(n_steps,),
        in_specs=_mix_in_specs(tile_of) + [pl.BlockSpec((TILE, D), lambda s: (tile_of(s), 0))] + _mix_param_specs(),
        out_specs=[pl.BlockSpec((TILE, INW), lambda s: (tile_of(s), 0)), small(8, CW), small(1, HD), small(1, HD),
                   small(1, CW), small(1, AW), small(1, 128), small(NH * BLK, 2 * BLK)],
        out_shape=[SDS((T, INW), BF), SDS((8, CW), F32), SDS((1, HD), F32), SDS((1, HD), F32), SDS((1, CW), F32),
                   SDS((1, AW), F32), SDS((1, 128), F32), SDS((NH * BLK, 2 * BLK), F32)],
        scratch_shapes=[pltpu.VMEM((8, CW), F32), pltpu.VMEM((BLK, NKV * HD), F32), pltpu.VMEM((BLK, NKV * HD), F32)],
        sem=("arbitrary",), vmem_mib=56, comm=comm, free=(1, 2, 3, 4) + tuple(range(6, 13)))


FT = 256
NFT = DFF // FT
RC = 1024
NCH = T // RC
LEAD = 16


def _rows8(x):
    return jnp.sum(x.reshape(x.shape[0] // 8, 8, x.shape[1]), axis=0)


def _ffn_act_specs():
    return [
        pl.BlockSpec((T, FT), lambda j: (0, j)), pl.BlockSpec((T, FT), lambda j: (0, NFT + j)),
        pl.BlockSpec((8, FT), lambda j: (0, j)), pl.BlockSpec((8, FT), lambda j: (0, NFT + j)),
        pl.BlockSpec((1, FT), lambda j: (0, j)), pl.BlockSpec((1, FT), lambda j: (0, NFT + j)),
    ]


def _conv_rows(win, w, b, n):
    win = win.astype(F32)
    u = win[LEAD:LEAD + n]
    u1 = pltpu.roll(win, 1, 0)[LEAD:LEAD + n]
    u2 = pltpu.roll(win, 2, 0)[LEAD:LEAD + n]
    return u2, u1, u, w[0:1, :] * u2 + w[1:2, :] * u1 + w[2:3, :] * u + b


def _ffn_act(up, fw8, fb, comm=()):
    ring = 3

    def body(up_ref, wg_ref, wv_ref, bg_ref, bv_ref, a_ref, pg_ref, pv_ref, g_ring, v_ring, sem):
        wg, wv, bg, bv = wg_ref[...], wv_ref[...], bg_ref[...], bv_ref[...]
        j = pl.program_id(0)

        def fetch(t):
            slot = lax.rem(t, ring)
            return [pltpu.make_async_copy(up_ref.at[:, pl.ds(pl.multiple_of((half * NFT + t) * FT, FT), FT)],
                                          buf.at[slot], sem.at[half, slot]) for half, buf in ((0, g_ring), (1, v_ring))]

        @pl.when(j == 0)
        def _():
            for t in range(ring - 1):
                for c in fetch(t):
                    c.start()

        @pl.when(j + ring - 1 < NFT)
        def _():
            for c in fetch(j + ring - 1):
                c.start()

        for c in fetch(j):
            c.wait()
        ug_ref, uv_ref = g_ring.at[lax.rem(j, ring)], v_ring.at[lax.rem(j, ring)]

        def chunk(rows, win_g, win_v):
            gp = _conv_rows(win_g, wg, bg, RC)[3]
            vp = _conv_rows(win_v, wv, bv, RC)[3]
            a_ref[rows, :] = (gp * jax.nn.sigmoid(gp) * vp).astype(BF)
            pg_ref[0, rows, :] = gp.astype(BF)
            pv_ref[0, rows, :] = vp.astype(BF)

        zero = jnp.zeros((LEAD, FT), BF)
        chunk(pl.ds(0, RC), jnp.concatenate([zero, ug_ref[0:RC, :]], axis=0), jnp.concatenate([zero, uv_ref[0:RC, :]], axis=0))

        def step(i, carry):
            r0 = pl.multiple_of(i * RC, RC)
            win = pl.ds(r0 - LEAD, RC + LEAD)
            chunk(pl.ds(r0, RC), ug_ref[win, :], uv_ref[win, :])
            return carry

        lax.fori_loop(1, NCH, step, 0)

    tile = pl.BlockSpec((1, T, FT), lambda j: (j, 0, 0))
    return _call(
        body, (up, fw8, fw8, fb, fb), name="ffn_act", grid=(NFT,), in_specs=[ANY] + _ffn_act_specs()[2:],
        out_specs=[pl.BlockSpec((T, FT), lambda j: (0, j)), tile, tile],
        out_shape=[SDS((T, DFF), BF), SDS((NFT, T, FT), BF), SDS((NFT, T, FT), BF)],
        scratch_shapes=[pltpu.VMEM((ring, T, FT), BF), pltpu.VMEM((ring, T, FT), BF), pltpu.SemaphoreType.DMA((2, ring))],
        sem=("arbitrary",), vmem_mib=40, comm=comm, free=(1, 2, 3, 4))


def _ffn_act_bwd(up, pre_g, pre_v, da, fw8, comm=()):
    ext = RC + LEAD

    def body(ug_ref, uv_ref, wg_ref, wv_ref, pg_ref, pv_ref, da_ref, dug_ref, duv_ref, dwg_ref, dwv_ref, dbg_ref, dbv_ref):
        wg, wv = wg_ref[...], wv_ref[...]

        def chunk(u_g, u_v, gp, vp, da_e):
            gp, vp, da_e = gp.astype(F32), vp.astype(F32), da_e.astype(F32)
            sig = jax.nn.sigmoid(gp)
            dvp = da_e * (gp * sig)
            dgp = da_e * vp * (sig * (1.0 + gp * (1.0 - sig)))

            def branch(dp, w, u):
                d0, d1, d2 = dp[0:RC], pltpu.roll(dp, ext - 1, 0)[0:RC], pltpu.roll(dp, ext - 2, 0)[0:RC]
                du = (w[2:3, :] * d0 + w[1:2, :] * d1 + w[0:1, :] * d2).astype(BF)
                u = u.astype(F32)
                return du, [_rows8(d0), _rows8(d2 * u), _rows8(d1 * u), _rows8(d0 * u)]

            dug, sums_g = branch(dgp, wg, u_g)
            duv, sums_v = branch(dvp, wv, u_v)
            return dug, duv, sums_g + sums_v

        def step(i, acc):
            r0 = pl.multiple_of(i * RC, RC)
            rows, more = pl.ds(r0, RC), pl.ds(r0, ext)
            dug, duv, part = chunk(ug_ref[rows, :], uv_ref[rows, :], pg_ref[0, more, :], pv_ref[0, more, :], da_ref[more, :])
            dug_ref[rows, :] = dug
            duv_ref[rows, :] = duv
            return [a + p for a, p in zip(acc, part)]

        acc = lax.fori_loop(0, NCH - 1, step, [jnp.zeros((8, FT), F32)] * 8)
        r0 = T - RC
        zero = jnp.zeros((LEAD, FT), BF)
        tail = lambda rows: jnp.concatenate([rows, zero], axis=0)
        dug, duv, part = chunk(ug_ref[r0:T, :], uv_ref[r0:T, :], tail(pg_ref[0, r0:T, :]), tail(pv_ref[0, r0:T, :]),
                               tail(da_ref[r0:T, :]))
        dug_ref[r0:T, :] = dug
        duv_ref[r0:T, :] = duv
        tot = [jnp.sum(a + p, axis=0, keepdims=True) for a, p in zip(acc, part)]
        for k, (dw_ref, db_ref) in enumerate(((dwg_ref, dbg_ref), (dwv_ref, dbv_ref))):
            db_ref[...] = tot[4 * k]
            dw_ref[...] = jnp.zeros_like(dw_ref)
            for r in range(3):
                dw_ref[r:r + 1, :] = tot[4 * k + 1 + r]

    col = lambda r: pl.BlockSpec((r, FT), lambda j: (0, j))
    return _call(
        body, (up, up, fw8, fw8, pre_g, pre_v, da), name="ffn_act_bwd", grid=(NFT,),
        in_specs=_ffn_act_specs()[0:4] + [pl.BlockSpec((1, T, FT), lambda j: (j, 0, 0))] * 2 + [col(T)],
        out_specs=[col(T), col(T), col(8), col(8), col(1), col(1)],
        out_shape=[SDS((T, DFF), BF), SDS((T, DFF), BF), SDS((8, DFF), F32), SDS((8, DFF), F32),
                   SDS((1, DFF), F32), SDS((1, DFF), F32)],
        sem=("parallel",), vmem_mib=40, comm=comm, free=(0, 1, 2, 3))


def _ffn_down_bwd(dh2b, w_down, comm=()):
    tm = TM

    def body(d_ref, w_ref, o_ref):
        o_ref[...] = _dot(d_ref[...], w_ref[...], 1, 1).astype(BF)

    return _call(
        body, (dh2b, w_down), name="ffn_down_bwd", grid=(T // tm,),
        in_specs=[pl.BlockSpec((tm, D), lambda i: (i, 0)), _resident((DFF, D))],
        out_specs=[pl.BlockSpec((tm, DFF), lambda i: (i, 0))], out_shape=[SDS((T, DFF), BF)],
        sem=("parallel",), vmem_mib=40, comm=comm, free=(0, 1))


def _norm_matmul_bwd(name, a_list, w_t, k_offsets, xin, g, dres, want_bf16, comm=(), slot=None, band=(), pack=()):
    tm = TM
    ks = [a.shape[1] for a in a_list]
    n_a = len(a_list)
    n_pre = 0 if slot is None else 1
    n_in = n_a + 4 + len(band) + len(pack)

    def body(*refs):
        refs = refs[n_pre:]
        a_refs = refs[:n_a]
        w_ref, x_ref, g_ref, r_ref = refs[n_a:n_a + 4]
        outs = refs[n_in:]
        dg_out = outs[1 + want_bf16]
        dx_ref, dg_ref = outs[0], (dg_out if slot is None else dg_out.at[0])

        @pl.when(pl.program_id(0) == 0)
        def _():
            dg_ref[...] = jnp.zeros_like(dg_ref)
            if band:
                db_ref, bk_ref, tbl_ref = refs[n_a + 4], refs[n_a + 5], outs[2 + want_bf16]
                bk = bk_ref[...]
                for b in range(NBUCKET):
                    m = bk == b
                    for h in range(NH):
                        v = jnp.where(m, db_ref[h * BLK:(h + 1) * BLK, :], 0.0)
                        tbl_ref[0, h:h + 1, b:b + 1] = jnp.sum(jnp.sum(v, axis=1, keepdims=True), axis=0, keepdims=True)
            if pack:
                pm_ref = outs[2 + want_bf16 + bool(band)]
                pm_ref[...] = jnp.zeros_like(pm_ref)
                _fill_mix(pm_ref, *refs[n_in - len(pack):n_in])

        du = _dot(a_refs[0][...], w_ref[k_offsets[0]:k_offsets[0] + ks[0], :], 1, 0)
        for k in range(1, n_a):
            du = du + _dot(a_refs[k][...], w_ref[k_offsets[k]:k_offsets[k] + ks[k], :], 1, 0)
        x = x_ref[...]
        r = lax.rsqrt(jnp.mean(x * x, axis=-1, keepdims=True) + EPS)
        dx, dg = _rms_bwd(du, x, r, g_ref[...])
        dx = r_ref[...] + dx
        dx_ref[...] = dx
        if want_bf16:
            outs[1][...] = dx.astype(BF)
        dg_ref[...] += dg

    tile = lambda c: pl.BlockSpec((tm, c), lambda i, *_: (i, 0))
    if slot is None:
        dg_spec, dg_shape = pl.BlockSpec((1, D), lambda i: (0, 0)), SDS((1, D), F32)
    else:
        dg_spec, dg_shape = pl.BlockSpec((1, 1, D), lambda i, slot_ref: (slot_ref[0], 0, 0)), SDS((N_DEV, 1, D), F32)
    out_specs = [tile(D)] + ([tile(D)] if want_bf16 else []) + [dg_spec]
    out_shape = [SDS((T, D), F32)] + ([SDS((T, D), BF)] if want_bf16 else []) + [dg_shape]
    if band:
        out_specs.append(pl.BlockSpec((1, NH, NBUCKET), lambda i, slot_ref: (slot_ref[0], 0, 0)))
        out_shape.append(SDS((N_DEV, NH, NBUCKET), F32))
    if pack:
        out_specs.append(pl.BlockSpec((1, 8, D), lambda i, slot_ref: (slot_ref[0], 0, 0)))
        out_shape.append(SDS((N_DEV, 8, D), F32))
    return _call(
        body, (*a_list, w_t, xin, g, dres, *band, *pack), name=name, grid=(T // tm,), prefetch=() if slot is None else (slot,),
        in_specs=[tile(k) for k in ks] + [_resident(w_t.shape), tile(D), pl.BlockSpec((1, D), lambda i, *_: (0, 0)), tile(D)]
        + [pl.BlockSpec(b.shape, lambda i, *_: (0, 0)) for b in (*band, *pack)],
        out_specs=out_specs, out_shape=out_shape, sem=("arbitrary",), vmem_mib=56, comm=comm, free=tuple(range(n_a + 4)))


def _out_bwd(dh1b, w_out, comm=()):
    tm = TM

    def body(d_ref, w_ref, o_ref):
        o_ref[...] = _dot(d_ref[...], w_ref[...], 1, 1)

    return _call(
        body, (dh1b, w_out), name="out_bwd", grid=(T // tm,),
        in_specs=[pl.BlockSpec((tm, D), lambda i: (i, 0)), _resident((D, D))],
        out_specs=[pl.BlockSpec((tm, D), lambda i: (i, 0))], out_shape=[SDS((T, D), F32)],
        sem=("parallel",), vmem_mib=32, comm=comm, free=(0, 1))


def _wgrad(name, a_list, b, old_a, comm=()):
    m_k = a_list[0].shape[1]
    tm = max(t for t in range(128, m_k // 2 + 1, 128) if m_k % t == 0)
    steps = [a.shape[1] // tm for a in a_list]
    starts = [sum(steps[:k]) for k in range(len(a_list))]
    n_a = len(a_list)

    def body(*refs):
        a_refs, b_ref, o_ref = refs[:n_a], refs[n_a], refs[n_a + 1]
        i = pl.program_id(0)
        for k in range(n_a):
            @pl.when((i >= starts[k]) & (i < starts[k] + steps[k]))
            def _(k=k):
                o_ref[...] = _dot(a_refs[k][...], b_ref[...], 0, 0).astype(BF)

    def a_spec(k):
        return pl.BlockSpec((T, tm), lambda i: (0, jnp.clip(i - starts[k], 0, steps[k] - 1)))

    m_total = tm * sum(steps)
    return _call(
        body, (*a_list, b), name=name, grid=(sum(steps),),
        in_specs=[a_spec(k) for k in range(n_a)] + [_resident((T, D))],
        out_specs=[pl.BlockSpec((tm, D), lambda i: (i, 0))], out_shape=[SDS((m_total, D), BF)],
        sem=("parallel",), vmem_mib=40, comm=comm, free=() if old_a is None else tuple(range(n_a)) if old_a else (n_a,))


def _chip_sum(name, gbf, from_sib, core, chip):
    h = gbf.shape[1]
    th = h

    def body(core_ref, chip_ref, g_ref, s_ref, pbf_ref, own_ref):
        p = g_ref[0].astype(F32) + s_ref[0].astype(F32)
        pbf_ref[0] = p.astype(BF)

        @pl.when(pl.program_id(1) == chip_ref[0])
        def _():
            own_ref[...] = p

    grid_spec = pltpu.PrefetchScalarGridSpec(
        num_scalar_prefetch=2, grid=(h // th, N_CHIPS),
        in_specs=[pl.BlockSpec((1, th, D), lambda t, jj, core_ref, chip_ref: (2 * jj + core_ref[0], t, 0)),
                  pl.BlockSpec((1, th, D), lambda t, jj, core_ref, chip_ref: (jj, t, 0))],
        out_specs=[pl.BlockSpec((1, th, D), lambda t, jj, core_ref, chip_ref: (jj, t, 0)),
                   pl.BlockSpec((th, D), lambda t, jj, core_ref, chip_ref: (t, 0))],
    )
    return _pcall(
        body, name=name, grid_spec=grid_spec, out_shape=_in_hbm([SDS((N_CHIPS, h, D), BF), SDS((h, D), F32)]),
        compiler_params=_params(("arbitrary", "arbitrary"), 32),
    )(core, chip, *_from_hbm(gbf, from_sib))


def _final_sum(name, own, from_chips, core, comm=()):
    h = own.shape[0]
    n = 4 if h % (4 * ROWS16) == 0 else 2
    th = h // n

    def body(core_ref, o_ref, r_ref, f_ref):
        f_ref[0] = ((o_ref[...] + r_ref[0].astype(F32)) + r_ref[1].astype(F32)) + r_ref[2].astype(F32)

    return _call(
        body, (own, from_chips), name=name, grid=(n,), prefetch=(core,),
        in_specs=[pl.BlockSpec((th, D), lambda i, core_ref: (i, 0)), pl.BlockSpec((3, th, D), lambda i, core_ref: (0, i, 0))],
        out_specs=[pl.BlockSpec((1, th, D), lambda i, core_ref: (core_ref[0], i, 0))], out_shape=[SDS((2, h, D), F32)],
        sem=("arbitrary",), vmem_mib=40, comm=comm)


def _adam_math(w, g, m, v):
    nm = ADAM_B1 * m + (1.0 - ADAM_B1) * g
    nv = ADAM_B2 * v + (1.0 - ADAM_B2) * (g * g)
    m_hat = nm / (1.0 - ADAM_B1 ** ADAM_STEP)
    v_hat = nv / (1.0 - ADAM_B2 ** ADAM_STEP)
    return -ADAM_LR * (m_hat / (jnp.sqrt(v_hat) + ADAM_EPS) + ADAM_WD * w), nm, nv


def _adamw(name, w, g, m, v, tr, copy_g=False, stage=True, g_transposed=False):
    rows, cols = w.shape

    def body(w_ref, g_ref, m_ref, v_ref, *outs):
        d_ref, nm_ref, nv_ref = outs[-3:]
        for c in [pl.ds(c0, 128) for c0 in range(0, cols, 128)] if g_transposed else [slice(None)]:
            g_val = g_ref[c, :].T if g_transposed else g_ref[...]
            if copy_g:
                outs[0][:, c] = g_val
            d_ref[:, c], nm_ref[:, c], nv_ref[:, c] = _adam_math(w_ref[:, c], g_val, m_ref[:, c], v_ref[:, c])

    spec = pl.BlockSpec((tr, cols), lambda i: (i, 0))
    n_out = 4 if copy_g else 3
    g_spec = pl.BlockSpec((cols, tr), lambda i: (0, i)) if g_transposed else spec
    return _call(body, (w, g, m, v), name=name, grid=(rows // tr,), in_specs=[spec, g_spec, spec, spec], out_specs=[spec] * n_out,
                 out_shape=[SDS((rows, cols), F32)] * n_out, sem=("parallel",), vmem_mib=32,
                 free=(0, 2, 3) if stage else ())


C_SQ = 2 * DFF
P_W = C_SQ + 128
R_G2, R_GO, R_DCW, R_QK = 0, 1, 2, 5
C_GCO, C_GAO, C_DQG, C_DKG, C_SINK = 0, CW, 0, 128, 256


def _pack(name, me, ins, width, fill):
    def body(me_ref, *refs):
        o = refs[-1]
        o[...] = jnp.zeros_like(o)
        fill(o, *refs[:-1])

    return _call(body, ins, name=name, grid=(1,), prefetch=(me,),
                 in_specs=[pl.BlockSpec(a.shape, lambda i, me_ref: (0, 0)) for a in ins],
                 out_specs=[pl.BlockSpec((1, 8, width), lambda i, me_ref: (me_ref[0], 0, 0))],
                 out_shape=[SDS((N_DEV, 8, width), F32)], sem=("arbitrary",))[0]


def _pack_ffn(me, dfwg, dfwv, dfbg, dfbv, sq):
    def fill(o, dfwg_r, dfwv_r, dfbg_r, dfbv_r, sq_r):
        o[0, :, 0:DFF] = dfwg_r[...]
        o[0, :, DFF:2 * DFF] = dfwv_r[...]
        o[0, 3:4, 0:DFF] = dfbg_r[...]
        o[0, 3:4, DFF:2 * DFF] = dfbv_r[...]
        o[0, :, C_SQ:C_SQ + 128] = sq_r[...]

    return _pack("pack_ffn", me, (dfwg, dfwv, dfbg, dfbv, sq), P_W, fill)


def _fill_mix(o, dg2_r, dgco_r, dgao_r, dcw_r, dqg_r, dkg_r, dsink_r):
    o[0, R_G2:R_G2 + 1, :] = dg2_r[...]
    o[0, R_GO:R_GO + 1, C_GCO:C_GCO + CW] = dgco_r[...]
    o[0, R_GO:R_GO + 1, C_GAO:C_GAO + AW] = dgao_r[...]
    o[0, R_DCW:R_DCW + 3, 0:CW] = dcw_r[0:3, :]
    o[0, R_QK:R_QK + 1, C_DQG:C_DQG + HD] = dqg_r[...]
    o[0, R_QK:R_QK + 1, C_DKG:C_DKG + HD] = dkg_r[...]
    o[0, R_QK:R_QK + 1, C_SINK:C_SINK + 128] = dsink_r[...]


N_SMALL = 11


def _small_adam(chip, p_all, pm_all, g1_all, tbl_all, ws, ms, vs):
    fw_cols = 2 * DFF // N_CHIPS
    cw_cols = CW // N_CHIPS

    def body(chip_ref, p_ref, fw_ref, pm_ref, cw_ref, g1_ref, tbl_ref, *refs):
        w_r, m_r, v_r = refs[0:N_SMALL], refs[N_SMALL:2 * N_SMALL], refs[2 * N_SMALL:3 * N_SMALL]
        outs = refs[3 * N_SMALL:]
        g_o, d_o, nm_o, nv_o = (outs[k * N_SMALL:(k + 1) * N_SMALL] for k in range(4))
        loss_o = outs[4 * N_SMALL]

        def total(ref):
            s = ref[0]
            for k in range(1, N_DEV):
                s = s + ref[k]
            return s

        S = total(p_ref)
        fw = total(fw_ref)
        M = total(pm_ref)
        cw = total(cw_ref)

        def step(i, g, at):
            d, nm, nv = _adam_math(w_r[i][at], g, m_r[i][at], v_r[i][at])
            g_o[i][at], d_o[i][at], nm_o[i][at], nv_o[i][at] = g, d, nm, nv

        everything = (slice(None), slice(None))
        step(0, total(g1_ref), everything)
        for r in range(3):
            step(1, cw[R_DCW + r:R_DCW + r + 1, :], (r, slice(None), slice(None)))
        step(2, M[R_QK:R_QK + 1, C_DQG:C_DQG + HD], everything)
        step(3, M[R_QK:R_QK + 1, C_DKG:C_DKG + HD], everything)
        step(4, total(tbl_ref), everything)
        step(5, M[R_QK:R_QK + 1, C_SINK:C_SINK + NH], everything)
        step(6, M[R_GO:R_GO + 1, C_GCO:C_GCO + CW], everything)
        step(7, M[R_GO:R_GO + 1, C_GAO:C_GAO + AW], everything)
        step(8, M[R_G2:R_G2 + 1, :], everything)
        for r in range(3):
            step(9, fw[r:r + 1, :], (r, slice(None), slice(None)))
        step(10, S[3:4, 0:2 * DFF], everything)
        sq = S[:, C_SQ:C_SQ + 128]
        loss_o[...] = jnp.sum(jnp.sum(sq, axis=1, keepdims=True), axis=0, keepdims=True) * (0.5 / D)

    def full(a):
        n = len(a.shape)
        return pl.BlockSpec(a.shape, lambda i, chip_ref: (0,) * n)

    params = [*ws, *ms, *vs]
    out = _call(
        body, (p_all, p_all, pm_all, pm_all, g1_all, tbl_all, *params), name="small_adam", grid=(1,), prefetch=(chip,),
        in_specs=[full(p_all),
                  pl.BlockSpec((N_DEV, 8, fw_cols), lambda i, chip_ref: (0, 0, chip_ref[0])),
                  full(pm_all),
                  pl.BlockSpec((N_DEV, 8, cw_cols), lambda i, chip_ref: (0, 0, chip_ref[0])),
                  full(g1_all), full(tbl_all), *[full(a) for a in params]],
        out_specs=[full(a) for a in ws] * 4 + [pl.BlockSpec((1, 1), lambda i, chip_ref: (0, 0))],
        out_shape=[SDS(a.shape, F32) for a in ws] * 4 + [SDS((1, 1), F32)], sem=("arbitrary",), vmem_mib=32)
    return out[0:N_SMALL], out[N_SMALL:2 * N_SMALL], out[2 * N_SMALL:3 * N_SMALL], out[3 * N_SMALL:4 * N_SMALL], out[4 * N_SMALL]


PLACE_STEPS = 4


def _place_specs(shards):
    rows = [s.shape[0] // PLACE_STEPS for s in shards]
    return ([pl.BlockSpec((r, D), lambda i, chip_ref: (i, 0)) for r in rows],
            [pl.BlockSpec((r, D), lambda i, chip_ref: (chip_ref[0] * PLACE_STEPS + i, 0)) for r in rows],
            [SDS((N_CHIPS * s.shape[0], D), BF) for s in shards])


def _place_first(chip, shard, conv_w, ffn_conv_w):
    def body(chip_ref, a, s0, s1, o, t0, t1):
        o[...] = a[...].astype(BF)

        @pl.when(pl.program_id(0) == 0)
        def _():
            for s, t in ((s0, t0), (s1, t1)):
                t[...] = jnp.zeros_like(t)
                t[0, 0:3, :] = s[...]

    ins, outs, shapes = _place_specs([shard])
    taps = (conv_w, ffn_conv_w)
    return _call(
        body, (shard, conv_w, ffn_conv_w), name="place_first", grid=(PLACE_STEPS,), prefetch=(chip,),
        in_specs=ins + [pl.BlockSpec(s.shape, lambda i, chip_ref: (0, 0)) for s in taps],
        out_specs=outs + [pl.BlockSpec((1, 8, s.shape[1]), lambda i, chip_ref: (chip_ref[0], 0, 0)) for s in taps],
        out_shape=shapes + [SDS((N_CHIPS, 8, s.shape[1]), F32) for s in taps],
        sem=("arbitrary",), vmem_mib=32, free=(1, 2))


def _place_rest(chip, shards, w_up, table, bucket, comm):
    n = len(shards)
    c_up = w_up.shape[1]
    edges = [round(k * (c_up // 128) / PLACE_STEPS) * 128 for k in range(PLACE_STEPS + 1)]

    def body(chip_ref, *refs):
        a, (up_ref, tab_ref, bk_ref), o = refs[:n], refs[n:n + 3], refs[n + 3:2 * n + 3]
        up_o, bias_ref = refs[2 * n + 3:]
        for src, dst in zip(a, o):
            dst[...] = src[...].astype(BF)
        for k in range(PLACE_STEPS):
            @pl.when(pl.program_id(0) == k)
            def _(k=k):
                up_o[edges[k]:edges[k + 1], :] = up_ref[:, edges[k]:edges[k + 1]].T.astype(BF)

        @pl.when(pl.program_id(0) == 0)
        def _():
            bk = bk_ref[...]
            eq = [bk == b for b in range(NBUCKET)]
            for h in range(NH):
                acc = jnp.zeros((BLK, 2 * BLK), F32)
                for b in range(NBUCKET):
                    acc = jnp.where(eq[b], tab_ref[h, b], acc)
                bias_ref[h * BLK:(h + 1) * BLK, :] = acc

    ins, outs, shapes = _place_specs(shards)
    return _call(
        body, (*shards, w_up, table, bucket), name="place_rest", grid=(PLACE_STEPS,), prefetch=(chip,),
        in_specs=ins + [_resident(w_up.shape), pl.BlockSpec(memory_space=pltpu.SMEM),
                        pl.BlockSpec(bucket.shape, lambda i, chip_ref: (0, 0))],
        out_specs=outs + [pl.BlockSpec((c_up, D), lambda i, chip_ref: (chip_ref[0], 0)),
                          pl.BlockSpec((NH * BLK, 2 * BLK), lambda i, chip_ref: (0, 0))],
        out_shape=shapes + [SDS((N_CHIPS * c_up, D), BF), SDS((NH * BLK, 2 * BLK), F32)],
        sem=("arbitrary",), vmem_mib=32, comm=comm, free=(n + 1, n + 2))


def kernel(x, norm_mix_g, w_in, conv_w, q_norm_g, k_norm_g, rel_bias_table, sinks, out_norm_conv_g, out_norm_attn_g, w_out, norm_ffn_g, w_up, ffn_conv_w, ffn_conv_b, w_down, loss_target, m_norm_mix_g, m_w_in, m_conv_w, m_q_norm_g, m_k_norm_g, m_rel_bias_table, m_sinks, m_out_norm_conv_g, m_out_norm_attn_g, m_w_out, m_norm_ffn_g, m_w_up, m_ffn_conv_w, m_ffn_conv_b, m_w_down, v_norm_mix_g, v_w_in, v_conv_w, v_q_norm_g, v_k_norm_g, v_rel_bias_table, v_sinks, v_out_norm_conv_g, v_out_norm_attn_g, v_w_out, v_norm_ffn_g, v_w_up, v_ffn_conv_w, v_ffn_conv_b, v_w_down):
    as_arg = lambda i: jnp.reshape(i, (1,)).astype(jnp.int32)
    chip = as_arg(2 * lax.axis_index("x") + lax.axis_index("y"))
    core = as_arg(lax.axis_index("c"))
    me = 2 * chip + core
    xs, tgt = x[0], loss_target[0]
    qg, kg, gco, gao, g1, g2, fb = q_norm_g, k_norm_g, out_norm_conv_g, out_norm_attn_g, norm_mix_g, norm_ffn_g, ffn_conv_b
    pieces = lambda g: g.reshape(N_DEV, g.shape[0] // N_DEV, D)
    whole = lambda f: f.reshape(2 * f.shape[1], D)

    bucket = jnp.asarray(_bucket_table())
    p_in, p_cw, p_fw = _place_first(chip, w_in[0].T, conv_w[0], ffn_conv_w[0])
    p_out, p_down, p_up, bias, w_int, cw_all, fw_all = _place_rest(
        chip, [w_out[0], w_down[0]], w_up[0], rel_bias_table.T, bucket,
        comm=[_t_gather(p_in, relayed_first=True), _t_small_weights(p_cw), _t_small_weights(p_fw)])
    cw8 = jnp.transpose(cw_all, (1, 0, 2)).reshape(8, CW)
    fw8 = jnp.transpose(fw_all, (1, 0, 2)).reshape(8, 2 * DFF)

    early = 3 / 11
    proj, u1, w_out_f, p_up = _inproj(xs, g1, w_int, comm=[_t_gather(p_out), _t_gather(p_up, (0, early))])
    y, w_upt = _mix_fwd(proj, sinks, cw8, qg, kg, gco, gao, bias, comm=[_t_gather(p_up, (early, 1))])
    h1, u2 = _outproj(y, w_out_f, xs, g2)
    up, w_down_f = _ffn_up(u2, w_upt, comm=[_t_gather(p_down)])
    a, pre_g, pre_v = _ffn_act(up, fw8, fb)
    dh2, dh2b, sq = _ffn_down(a, w_down_f, h1, tgt)

    gdbf, = _wgrad("wgrad_down", [a], dh2b, None)
    da, sib_down = _ffn_down_bwd(dh2b, w_down_f, comm=[_t_sibling(pieces(gdbf))])
    pbf_down, own_down = _chip_sum("chip_sum_w_down", pieces(gdbf), sib_down, core, chip)
    dug, duv, dfwg, dfwv, dfbg, dfbv, chips_down = _ffn_act_bwd(up, pre_g, pre_v, da, fw8, comm=[_t_chips(pbf_down)])
    fin_down, = _final_sum("final_sum_w_down", own_down, chips_down, core)
    gubf, = _wgrad("wgrad_up", [dug, duv], u2, False)
    p_all = _pack_ffn(me, dfwg, dfwv, dfbg, dfbv, sq)
    dh1, dh1b, dg2, sib_up, fin_down, p_all = _norm_matmul_bwd(
        "ffn_up_bwd", [dug, duv], w_upt, [0, DFF], h1, g2, dh2, True,
        comm=[_t_sibling(pieces(gubf)), _t_swap(fin_down), _t_allgather(p_all)])
    pbf_up, own_up = _chip_sum("chip_sum_w_up", pieces(gubf), sib_up, core, chip)
    gobf, = _wgrad("wgrad_out", [y], dh1b, True)
    dy, sib_out = _out_bwd(dh1b, w_out_f, comm=[_t_sibling(pieces(gobf))])
    pbf_out, own_out = _chip_sum("chip_sum_w_out", pieces(gobf), sib_out, core, chip)
    dproj, dcw8, dqg, dkg, dgco, dgao, dsink, dbias, chips_up = _mix_bwd(
        proj, dy, sinks, cw8, qg, kg, gco, gao, bias, comm=[_t_chips(pbf_up)])
    fin_up, = _final_sum("final_sum_w_up", own_up, chips_up, core)
    gibf, chips_out, fin_up = _wgrad("wgrad_in", [dproj], u1, False, comm=[_t_chips(pbf_out), _t_swap(fin_up)])
    fin_out, sib_in = _final_sum("final_sum_w_out", own_out, chips_out, core, comm=[_t_sibling(pieces(gibf))])
    pbf_in, own_in = _chip_sum("chip_sum_w_in", pieces(gibf), sib_in, core, chip)
    dx, g1_all, tbl_all, pm_all, chips_in, fin_out = _norm_matmul_bwd(
        "in_bwd", [dproj], w_int, [0], xs, g1, dh1, False, comm=[_t_chips(pbf_in), _t_swap(fin_out)], slot=me,
        band=(dbias, bucket), pack=(dg2, dgco, dgao, dcw8, dqg, dkg, dsink))
    fin_in, = _final_sum("final_sum_w_in", own_in, chips_in, core)
    g1_all, tbl_all, pm_all, fin_in = _comm_call(
        "gather_last", [_t_allgather(g1_all), _t_allgather(tbl_all), _t_allgather(pm_all), _t_swap(fin_in)])

    g_w_out, g_w_down = whole(fin_out), whole(fin_down)
    g_w_down, d_down, nm_down, nv_down = _adamw("adamw_w_down", w_down[0], g_w_down, m_w_down[0], v_w_down[0], 352, True)
    g_w_up, d_up, nm_up, nv_up = _adamw(
        "adamw_w_up", w_up[0], whole(fin_up), m_w_up[0], v_w_up[0], 256, True, stage=False, g_transposed=True)
    g_w_out, d_out, nm_out, nv_out = _adamw("adamw_w_out", w_out[0], g_w_out, m_w_out[0], v_w_out[0], 256, True, stage=False)
    g_w_in, d_in, nm_in, nv_in = [a.T for a in _adamw(
        "adamw_w_in", w_in[0].T, whole(fin_in), m_w_in[0].T, v_w_in[0].T, INW // N_CHIPS // 3, True, stage=False)]
    taps = lambda a: jnp.transpose(a, (1, 0, 2))
    sw = [norm_mix_g, taps(conv_w), q_norm_g, k_norm_g, rel_bias_table.T, sinks, out_norm_conv_g, out_norm_attn_g,
          norm_ffn_g, taps(ffn_conv_w), ffn_conv_b]
    smm = [m_norm_mix_g, taps(m_conv_w), m_q_norm_g, m_k_norm_g, m_rel_bias_table.T, m_sinks, m_out_norm_conv_g,
           m_out_norm_attn_g, m_norm_ffn_g, taps(m_ffn_conv_w), m_ffn_conv_b]
    smv = [v_norm_mix_g, taps(v_conv_w), v_q_norm_g, v_k_norm_g, v_rel_bias_table.T, v_sinks, v_out_norm_conv_g,
           v_out_norm_attn_g, v_norm_ffn_g, taps(v_ffn_conv_w), v_ffn_conv_b]
    *small_out, loss = _small_adam(chip, p_all, pm_all, g1_all, tbl_all, sw, smm, smv)
    sg, sd, snm, snv = [list(r) for r in small_out]
    for r in (sg, sd, snm, snv):
        r[1], r[4], r[9] = taps(r[1]), r[4].T, taps(r[9])

    def order(s, b_in, b_out, b_up, b_down):
        return (s[0], b_in[None], s[1], s[2], s[3], s[4], s[5], s[6], s[7], b_out[None], s[8], b_up[None],
                s[9], s[10], b_down[None])

    return (loss.reshape(()), dx[None],
            *order(sg, g_w_in, g_w_out, g_w_up, g_w_down),
            *order(sd, d_in, d_out, d_up, d_down),
            *order(snm, nm_in, nm_out, nm_up, nm_down),
            *order(snv, nv_in, nv_out, nv_up, nv_down))
```

```python
import functools
import math

import numpy as np

import jax
import jax.numpy as jnp
from jax import lax
from jax.experimental import pallas as pl
from jax.experimental.pallas import tpu as pltpu

F32 = jnp.float32
BF = jnp.bfloat16
SDS = jax.ShapeDtypeStruct

T = 2048
D = 1024
CW = 512
AW = 512
HD = 64
NH = 8
NKV = 2
GQ = 4
INW = 2304
DFF = 2816
BLK = 128
NB = T // BLK
NBUCKET = 32
EPS = 1e-6
NEG_INF = -1e30
N_CHIPS = 4
N_DEV = 8

ADAM_LR = 0.001
ADAM_B1 = 0.9
ADAM_B2 = 0.999
ADAM_EPS = 1e-08
ADAM_WD = 0.01
ADAM_STEP = 10

TM = 512
MIB = 1024 * 1024
MESH = pl.DeviceIdType.MESH
ANY = pl.BlockSpec(memory_space=pl.ANY)

_pcall = pl.pallas_call


def _params(sem=None, vmem_mib=None, collective_id=None):
    kw = {} if collective_id is None else {"collective_id": collective_id}
    if sem is not None:
        kw["dimension_semantics"] = sem
    if vmem_mib is not None:
        kw["vmem_limit_bytes"] = vmem_mib * MIB
    return pltpu.CompilerParams(**kw)


def _resident(shape):
    return pl.BlockSpec(shape, lambda *_: (0,) * len(shape), pipeline_mode=pl.Buffered(1))


def _dot(a, b, ca, cb):
    return lax.dot_general(a, b, (((ca,), (cb,)), ((), ())), preferred_element_type=F32)


def _rms_bwd(dy, x, r, g):
    dg = jnp.sum(dy * (x * r), axis=0, keepdims=True)
    dgx = dy * g
    dx = r * dgx - x * (r * r * r) * jnp.mean(x * dgx, axis=-1, keepdims=True)
    return dx, dg


def _where():
    x, y, c = lax.axis_index("x"), lax.axis_index("y"), lax.axis_index("c")
    return x, y, c, [(1 - x, y), (x, 1 - y), (1 - x, 1 - y)]


def _rcopy(src, dst, ssem, rsem, dev):
    return pltpu.make_async_remote_copy(src_ref=src, dst_ref=dst, send_sem=ssem, recv_sem=rsem, device_id=dev,
                                        device_id_type=MESH)


SIBLING, Y_CHIP, X_CHIP, DIAGONAL_CHIP = 1, 2, 4, 6
OTHER_CHIPS = (Y_CHIP, X_CHIP, DIAGONAL_CHIP)
EVERYONE = tuple(range(1, N_DEV))
BARRIER_OF = {(SIBLING,): 0, (SIBLING, Y_CHIP, X_CHIP): 1, OTHER_CHIPS: 2, (SIBLING,) + OTHER_CHIPS: 3, EVERYONE: 4}


def _peer(rel):
    x, y, c, _ = _where()
    return x ^ ((rel >> 2) & 1), y ^ ((rel >> 1) & 1), c ^ (rel & 1)


class _Task:
    def __init__(self, ins, outs, alias, n_sem, start, finish, middle=None, peers=()):
        self.ins, self.outs, self.alias, self.n_sem, self.start, self.finish = ins, outs, alias, n_sem, start, finish
        self.middle = middle if middle is not None else (lambda *args: None)
        self.peers = peers


def _peers_of(comm):
    return tuple(sorted({p for t in comm for p in t.peers}))


def _enter(comm):
    peers = _peers_of(comm)
    barrier = pltpu.get_barrier_semaphore()
    for rel in peers:
        pl.semaphore_signal(barrier, inc=1, device_id=_peer(rel), device_id_type=MESH)
    pl.semaphore_wait(barrier, len(peers))


ROWS16 = 16


def _t_gather(placed, part=(0, 1), relayed_first=False):
    R = placed.shape[0] // N_CHIPS
    q = R // 4
    lo, hi = (round(f * (q // ROWS16)) * ROWS16 for f in part)

    def quarter(chip_index, core, k):
        return pl.ds(pl.multiple_of(chip_index * R + core * 2 * q + k * q + lo, ROWS16), hi - lo)

    def places():
        x, y, c, _ = _where()
        return c, 2 * x + y, 2 * (1 - x) + y, 2 * x + (1 - y), 2 * (1 - x) + (1 - y), (1 - x, y, c), (x, 1 - y, c), (x, y, 1 - c)

    def copy(buf, k, chip_index, core, quart, ss, rs, b, dev):
        window = buf.at[quarter(chip_index, core, quart)]
        return _rcopy(window, window, ss.at[b + k], rs.at[b + k], dev)

    def first_hop(cout, ss, rs, b, which):
        c, me, _, _, _, x_nbr, y_nbr, _ = places()
        for k, (quart, dev) in enumerate(((0, x_nbr), (1, y_nbr), (1, x_nbr), (0, y_nbr))):
            if k in which:
                copy(cout[0], k, me, c, quart, ss, rs, b, dev).start()

    def start(cin, cout, ss, rs, b):
        first_hop(cout, ss, rs, b, (0, 1) if relayed_first else (0, 1, 2, 3))

    def middle(cin, cout, ss, rs, b):
        c, _, xc, yc, _, x_nbr, y_nbr, sib = places()
        for k, chip_index, quart, dev in ((0, xc, 0, y_nbr), (1, yc, 1, x_nbr)):
            copy(cout[0], k, chip_index, c, quart, ss, rs, b, dev).wait_recv()
            copy(cout[0], 4 + k, chip_index, c, quart, ss, rs, b, dev).start()
            copy(cout[0], 6 + k, chip_index, c, quart, ss, rs, b, sib).start()
        if relayed_first:
            first_hop(cout, ss, rs, b, (2, 3))

    later = ((2, 1, 1), (3, 2, 0), (4, 3, 0), (5, 3, 1))

    def finish(cin, cout, ss, rs, b):
        c, me, xc, yc, dc, _, _, sib = places()
        chip_of = {1: xc, 2: yc, 3: dc}
        for k, whose, quart in later:
            copy(cout[0], k, chip_of[whose], c, quart, ss, rs, b, sib).wait_recv()
            copy(cout[0], 6 + k, chip_of[whose], c, quart, ss, rs, b, sib).start()
        for k, whose, quart in ((0, 1, 0), (1, 2, 1)) + later:
            copy(cout[0], 6 + k, chip_of[whose], 1 - c, quart, ss, rs, b, sib).wait_recv()
        for k in range(12):
            copy(cout[0], k, me, c, 0, ss, rs, b, sib).wait_send()

    return _Task([placed], [SDS(placed.shape, placed.dtype)], [(0, 0)], 12, start, finish, middle, peers=(SIBLING, Y_CHIP, X_CHIP))


def _t_small_weights(buf):
    def start(cin, cout, ss, rs, b):
        x, y, c, chips = _where()
        mine = cout[0].at[2 * x + y]
        for r, (px, py) in enumerate(chips):
            _rcopy(mine, mine, ss.at[b + r], rs.at[b + r], (px, py, c)).start()

    def finish(cin, cout, ss, rs, b):
        x, y, c, chips = _where()
        for r, (px, py) in enumerate(chips):
            got = cout[0].at[2 * px + py]
            _rcopy(got, got, ss.at[b + r], rs.at[b + r], (px, py, c)).wait_recv()
        for r, (px, py) in enumerate(chips):
            mine = cout[0].at[2 * x + y]
            _rcopy(mine, mine, ss.at[b + r], rs.at[b + r], (px, py, c)).wait_send()

    return _Task([buf], [SDS(buf.shape, buf.dtype)], [(0, 0)], 3, start, finish, peers=OTHER_CHIPS)


def _t_sibling(gbf):
    def start(cin, cout, ss, rs, b):
        x, y, c, _ = _where()
        for jj in range(N_CHIPS):
            _rcopy(cin[0].at[2 * jj + (1 - c)], cout[0].at[jj], ss.at[b + jj], rs.at[b + jj], (x, y, 1 - c)).start()

    def finish(cin, cout, ss, rs, b):
        x, y, c, _ = _where()
        for jj in range(N_CHIPS):
            got = cout[0].at[jj]
            _rcopy(got, got, ss.at[b + jj], rs.at[b + jj], (x, y, 1 - c)).wait_recv()
        for jj in range(N_CHIPS):
            got = cout[0].at[jj]
            _rcopy(got, got, ss.at[b + jj], rs.at[b + jj], (x, y, 1 - c)).wait_send()

    return _Task([gbf], [SDS((N_CHIPS,) + gbf.shape[1:], BF)], [], N_CHIPS, start, finish, peers=(SIBLING,))


def _t_chips(pbf):
    def start(cin, cout, ss, rs, b):
        x, y, c, chips = _where()
        for r, (px, py) in enumerate(chips):
            _rcopy(cin[0].at[2 * px + py], cout[0].at[r], ss.at[b + r], rs.at[b + r], (px, py, c)).start()

    def finish(cin, cout, ss, rs, b):
        x, y, c, chips = _where()
        for r, (px, py) in enumerate(chips):
            got = cout[0].at[r]
            _rcopy(got, got, ss.at[b + r], rs.at[b + r], (px, py, c)).wait_recv()
        for r, (px, py) in enumerate(chips):
            got = cout[0].at[r]
            _rcopy(got, got, ss.at[b + r], rs.at[b + r], (px, py, c)).wait_send()

    return _Task([pbf], [SDS((3,) + pbf.shape[1:], BF)], [], 3, start, finish, peers=OTHER_CHIPS)


def _t_swap(fin):
    def start(cin, cout, ss, rs, b):
        x, y, c, _ = _where()
        mine = cout[0].at[c]
        _rcopy(mine, mine, ss.at[b], rs.at[b], (x, y, 1 - c)).start()

    def finish(cin, cout, ss, rs, b):
        x, y, c, _ = _where()
        got = cout[0].at[1 - c]
        _rcopy(got, got, ss.at[b], rs.at[b], (x, y, 1 - c)).wait_recv()
        _rcopy(got, got, ss.at[b], rs.at[b], (x, y, 1 - c)).wait_send()

    return _Task([fin], [SDS(fin.shape, fin.dtype)], [(0, 0)], 1, start, finish, peers=(SIBLING,))


def _t_allgather(buf):
    def peers():
        x, y, c, _ = _where()
        out = []
        for rel in range(1, N_DEV):
            px, py, pc = x ^ ((rel >> 2) & 1), y ^ ((rel >> 1) & 1), c ^ (rel & 1)
            out.append((rel - 1, 4 * px + 2 * py + pc, (px, py, pc)))
        return 4 * x + 2 * y + c, out

    def start(cin, cout, ss, rs, b):
        me, ps = peers()
        mine = cout[0].at[me]
        for k, _, dev in ps:
            _rcopy(mine, mine, ss.at[b + k], rs.at[b + k], dev).start()

    def finish(cin, cout, ss, rs, b):
        me, ps = peers()
        for k, pidx, dev in ps:
            got = cout[0].at[pidx]
            _rcopy(got, got, ss.at[b + k], rs.at[b + k], dev).wait_recv()
        for k, _, dev in ps:
            mine = cout[0].at[me]
            _rcopy(mine, mine, ss.at[b + k], rs.at[b + k], dev).wait_send()

    return _Task([buf], [SDS(buf.shape, buf.dtype)], [(0, 0)], N_DEV - 1, start, finish, peers=EVERYONE)


def _run_tasks(comm, which, cin, cout, ss, rs):
    i0 = o0 = s0 = 0
    for t in comm:
        getattr(t, which)(cin[i0:i0 + len(t.ins)], cout[o0:o0 + len(t.outs)], ss, rs, s0)
        i0, o0, s0 = i0 + len(t.ins), o0 + len(t.outs), s0 + t.n_sem


def _from_hbm(*arrays):
    return [pltpu.with_memory_space_constraint(a, pltpu.HBM) for a in arrays]


def _in_hbm(shapes):
    return [pltpu.HBM(s.shape, s.dtype) for s in shapes]


def _comm_layout(comm, n_in, n_out):
    c_in = [a for t in comm for a in t.ins]
    c_out = [s for t in comm for s in t.outs]
    aliases, i0, o0 = {}, 0, 0
    for t in comm:
        for i, o in t.alias:
            aliases[n_in + i0 + i] = n_out + o0 + o
        i0, o0 = i0 + len(t.ins), o0 + len(t.outs)
    return c_in, c_out, aliases, sum(t.n_sem for t in comm)


def _call(body, operands, *, name, grid, in_specs, out_specs, out_shape, scratch_shapes=(), sem=None, vmem_mib=None, comm=(),
          free=(), prefetch=()):
    operands = [o if s.memory_space == pltpu.SMEM or k in free else pltpu.with_memory_space_constraint(o, pltpu.HBM)
                for k, (o, s) in enumerate(zip(operands, in_specs))]
    n_pre, n_in, n_out, n_scr = len(prefetch), len(in_specs), len(out_specs), len(scratch_shapes)
    c_in, c_out, aliases, n_sem = _comm_layout(comm, n_pre + n_in, n_out)
    sems = [pltpu.SemaphoreType.DMA((n_sem,)), pltpu.SemaphoreType.DMA((n_sem,))] if comm else []

    def wrapped(*refs):
        pre, refs = refs[:n_pre], refs[n_pre:]
        ins, cin = refs[:n_in], refs[n_in:n_in + len(c_in)]
        rest = refs[n_in + len(c_in):]
        outs, cout = rest[:n_out], rest[n_out:n_out + len(c_out)]
        rest = rest[n_out + len(c_out):]
        scr, csem = rest[:n_scr], rest[n_scr:]
        if not comm:
            return body(*pre, *ins, *outs, *scr)
        step = functools.reduce(lambda acc, k: acc * grid[k] + pl.program_id(k), range(len(grid)), 0)
        n_steps = math.prod(grid)

        @pl.when(step == 0)
        def _():
            _enter(comm)
            _run_tasks(comm, "start", cin, cout, *csem)

        pl.when(step == n_steps // 2)(lambda: _run_tasks(comm, "middle", cin, cout, *csem))
        body(*pre, *ins, *outs, *scr)
        pl.when(step == n_steps - 1)(lambda: _run_tasks(comm, "finish", cin, cout, *csem))

    grid_spec = pltpu.PrefetchScalarGridSpec(
        num_scalar_prefetch=n_pre, grid=grid, in_specs=list(in_specs) + [ANY] * len(c_in),
        out_specs=list(out_specs) + [ANY] * len(c_out), scratch_shapes=list(scratch_shapes) + sems)
    return _pcall(
        wrapped, name=name, grid_spec=grid_spec, out_shape=_in_hbm(list(out_shape) + c_out), input_output_aliases=aliases,
        compiler_params=_params(("arbitrary",) * len(grid) if comm else sem, vmem_mib,
                                BARRIER_OF[_peers_of(comm)] if comm else None),
    )(*prefetch, *operands, *_from_hbm(*c_in))


def _comm_call(name, comm):
    c_in, c_out, aliases, n_sem = _comm_layout(comm, 0, 0)

    def body(*refs):
        cin, cout, (ss, rs) = refs[:len(c_in)], refs[len(c_in):len(c_in) + len(c_out)], refs[len(c_in) + len(c_out):]
        _enter(comm)
        for phase in ("start", "middle", "finish"):
            _run_tasks(comm, phase, cin, cout, ss, rs)

    return _pcall(
        body, name=name, in_specs=[ANY] * len(c_in), out_specs=[ANY] * len(c_out), out_shape=_in_hbm(c_out),
        scratch_shapes=[pltpu.SemaphoreType.DMA((n_sem,)), pltpu.SemaphoreType.DMA((n_sem,))],
        input_output_aliases=aliases, compiler_params=_params(collective_id=BARRIER_OF[_peers_of(comm)]),
    )(*_from_hbm(*c_in))


def _inproj(x, g1, w_int, comm=()):
    tm = TM

    def body(x_ref, g_ref, w_ref, proj_ref, u_ref):
        xf = x_ref[...]
        r = lax.rsqrt(jnp.mean(xf * xf, axis=-1, keepdims=True) + EPS)
        u = (xf * r * g_ref[...]).astype(BF)
        u_ref[...] = u
        proj_ref[...] = _dot(u, w_ref[...], 1, 1)

    return _call(
        body, (x, g1, w_int), name="inproj", grid=(T // tm,),
        in_specs=[pl.BlockSpec((tm, D), lambda i: (i, 0)), pl.BlockSpec((1, D), lambda i: (0, 0)),
                  _resident((INW, D))],
        out_specs=[pl.BlockSpec((tm, INW), lambda i: (i, 0)), pl.BlockSpec((tm, D), lambda i: (i, 0))],
        out_shape=[SDS((T, INW), F32), SDS((T, D), BF)], sem=("parallel",), vmem_mib=40, comm=comm, free=(0, 1))


def _outproj(y, w_out, x, g2):
    tm = TM

    def body(y_ref, w_ref, x_ref, g_ref, h1_ref, u2_ref):
        h1 = x_ref[...] + _dot(y_ref[...], w_ref[...], 1, 0)
        h1_ref[...] = h1
        r = lax.rsqrt(jnp.mean(h1 * h1, axis=-1, keepdims=True) + EPS)
        u2_ref[...] = (h1 * r * g_ref[...]).astype(BF)

    return _call(
        body, (y, w_out, x, g2), name="outproj", grid=(T // tm,),
        in_specs=[pl.BlockSpec((tm, D), lambda i: (i, 0)), _resident((D, D)),
                  pl.BlockSpec((tm, D), lambda i: (i, 0)), pl.BlockSpec((1, D), lambda i: (0, 0))],
        out_specs=[pl.BlockSpec((tm, D), lambda i: (i, 0)), pl.BlockSpec((tm, D), lambda i: (i, 0))],
        out_shape=[SDS((T, D), F32), SDS((T, D), BF)], sem=("parallel",), vmem_mib=32, free=(2, 3))


def _ffn_up(u2, w_upt, comm=()):
    tm, tn = T, 512

    def body(u_ref, w_ref, o_ref):
        o_ref[...] = _dot(u_ref[...], w_ref[...], 1, 1).astype(BF)

    return _call(
        body, (u2, w_upt), name="ffn_up", grid=(T // tm, 2 * DFF // tn),
        in_specs=[pl.BlockSpec((tm, D), lambda i, j: (i, 0)), pl.BlockSpec((tn, D), lambda i, j: (j, 0))],
        out_specs=[pl.BlockSpec((tm, tn), lambda i, j: (i, j))], out_shape=[SDS((T, 2 * DFF), BF)],
        sem=("parallel", "parallel"), vmem_mib=32, comm=comm, free=(1,))


RING = 3


def _ring_fetch(srcs, rings, sem, step, n_steps, rows):
    def copies(t):
        slot = lax.rem(t, RING)
        return [pltpu.make_async_copy(src.at[pl.ds(pl.multiple_of(t * rows, rows), rows)], ring.at[slot], sem.at[k, slot])
                for k, (src, ring) in enumerate(zip(srcs, rings))]

    @pl.when(step == 0)
    def _():
        for t in range(RING - 1):
            for c in copies(t):
                c.start()

    @pl.when(step + RING - 1 < n_steps)
    def _():
        for c in copies(step + RING - 1):
            c.start()

    for c in copies(step):
        c.wait()
    return [ring.at[lax.rem(step, RING)] for ring in rings]


def _ffn_down(a, w_down, h1, tgt):
    tm = TM
    n_steps = T // tm

    def body(a_hbm, w_ref, h1_hbm, t_hbm, dh_ref, dhb_ref, l_ref, a_ring, h1_ring, t_ring, sem):
        a_ref, h1_ref, t_ref = _ring_fetch((a_hbm, h1_hbm, t_hbm), (a_ring, h1_ring, t_ring), sem, pl.program_id(0), n_steps, tm)

        @pl.when(pl.program_id(0) == 0)
        def _():
            l_ref[...] = jnp.zeros_like(l_ref)

        h2 = h1_ref[...] + _dot(a_ref[...], w_ref[...], 1, 0)
        e = h2 - t_ref[...]
        dh = e * (1.0 / D)
        dh_ref[...] = dh
        dhb_ref[...] = dh.astype(BF)
        e2 = jnp.sum((e * e).reshape(tm // 8, 8, D), axis=0)
        acc = e2[:, 0:128]
        for k in range(1, D // 128):
            acc = acc + e2[:, k * 128:(k + 1) * 128]
        l_ref[...] += acc

    return _call(
        body, (a, w_down, h1, tgt), name="ffn_down", grid=(n_steps,), in_specs=[ANY, _resident((DFF, D)), ANY, ANY],
        out_specs=[pl.BlockSpec((tm, D), lambda i: (i, 0)), pl.BlockSpec((tm, D), lambda i: (i, 0)),
                   pl.BlockSpec((8, 128), lambda i: (0, 0))],
        out_shape=[SDS((T, D), F32), SDS((T, D), BF), SDS((8, 128), F32)],
        scratch_shapes=[pltpu.VMEM((RING, tm, DFF), BF), pltpu.VMEM((RING, tm, D), F32), pltpu.VMEM((RING, tm, D), F32),
                        pltpu.SemaphoreType.DMA((3, RING))],
        sem=("arbitrary",), vmem_mib=48)


def _bucket_table():
    q = np.arange(BLK, dtype=np.int32)[:, None]
    j = np.arange(2 * BLK, dtype=np.int32)[None, :]
    n = np.maximum(q + BLK - j, 0)
    nf = np.maximum(n, 1).astype(np.float32)
    max_exact = NBUCKET // 2
    large = max_exact + (np.log(nf / np.float32(max_exact)) / np.float32(math.log(BLK / max_exact))
                         * np.float32(NBUCKET - max_exact)).astype(np.int32)
    large = np.minimum(large, NBUCKET - 1)
    return np.where(n < max_exact, n, large).astype(np.int32)


def _two_bf16(x):
    hi = x.astype(BF)
    return hi, (x - hi.astype(F32)).astype(BF)


def _head_sums(x, seg):
    hi, lo = _two_bf16(x)
    s = seg[0:x.shape[1], :]
    return _dot(hi, s, 1, 0) + _dot(lo, s, 1, 0)


def _head_spread(v, seg, width):
    hi, lo = _two_bf16(v)
    s = seg[0:width, :]
    return _dot(hi, s, 1, 1) + _dot(lo, s, 1, 1)


def _head_norm(x, g_t, seg, by_head=False):
    if by_head:
        heads = [x[:, h * HD:(h + 1) * HD] for h in range(x.shape[1] // HD)]
        r = jnp.concatenate([jnp.broadcast_to(lax.rsqrt(jnp.mean(v * v, axis=-1, keepdims=True) + EPS), v.shape)
                             for v in heads], axis=1)
    else:
        r = lax.rsqrt(_head_sums(x * x, seg) * (1.0 / HD) + EPS)
        r = _head_spread(r, seg, x.shape[1])
    return x * r * g_t, r


def _head_norm_bwd(dy, x, r, g_t, seg):
    dg_t = jnp.sum(dy * (x * r), axis=0, keepdims=True)
    dgx = dy * g_t
    mean = _head_spread(_head_sums(x * dgx, seg) * (1.0 / HD), seg, x.shape[1])
    return r * dgx - x * (r * r * r) * mean, dg_t


def _fold_heads(v):
    out = v[:, 0:HD]
    for h in range(1, v.shape[1] // HD):
        out = out + v[:, h * HD:(h + 1) * HD]
    return out


def _mix_forward(P, zc8, zh8, pkv, first, cw, qg_t, kg_t, gco, gao, seg, sink_ref, bias_ref, by_head=False):
    gate_b = P[:, 0:CW]
    gate_c = P[:, CW:2 * CW]
    hc = P[:, 2 * CW:3 * CW]
    z = gate_c * hc
    keep = jnp.where(first, 0.0, 1.0)
    zp = zc8 * zh8 * keep
    p1 = zp[7:8, :]
    p2 = zp[6:7, :]
    row = lax.broadcasted_iota(jnp.int32, (BLK, 1), 0)
    z1 = jnp.where(row == 0, p1, pltpu.roll(z, 1, 0))
    z2 = jnp.where(row == 0, p2, jnp.where(row == 1, p1, pltpu.roll(z, 2, 0)))
    cz = cw[0:1, :] * z2 + cw[1:2, :] * z1 + cw[2:3, :] * z
    y_conv = gate_b * cz

    scale = HD ** -0.5
    qi = lax.broadcasted_iota(jnp.int32, (BLK, 2 * BLK), 0)
    kj = lax.broadcasted_iota(jnp.int32, (BLK, 2 * BLK), 1)
    dd = qi + BLK - kj
    first_key = jnp.where(first, BLK, 0)
    valid = (dd >= 0) & (dd < BLK) & (kj >= first_key)

    q0 = 3 * CW
    k0 = q0 + AW
    v0 = k0 + NKV * HD
    q_raw = P[:, q0:k0]
    qn, rq = _head_norm(q_raw, qg_t, seg, by_head)
    qs = (qn * scale).astype(BF)
    k_raw = jnp.concatenate([pkv[:, 0:NKV * HD], P[:, k0:v0]], axis=0)
    kn, rk = _head_norm(k_raw, kg_t, seg, by_head)
    knb = kn.astype(BF)
    heads = []
    for h in range(NH):
        kv = h // GQ
        kb = knb[:, kv * HD:(kv + 1) * HD]
        vb = jnp.concatenate([pkv[:, NKV * HD + kv * HD:NKV * HD + (kv + 1) * HD],
                              P[:, v0 + kv * HD:v0 + (kv + 1) * HD]], axis=0).astype(BF)
        Q = qs[:, h * HD:(h + 1) * HD]
        S = _dot(Q, kb, 1, 1) + bias_ref[h * BLK:(h + 1) * BLK, :]
        S = jnp.where(valid, S, NEG_INF)
        sink = sink_ref[0, h]
        m = jnp.maximum(jnp.max(S, axis=-1, keepdims=True), sink)
        p = jnp.exp(S - m)
        es = jnp.exp(sink - m)
        denom = jnp.sum(p, axis=-1, keepdims=True) + es
        probs = p / denom
        O = _dot(probs.astype(BF), vb, 1, 0)
        heads.append(dict(kb=kb, vb=vb, Q=Q, probs=probs, psink=es / denom, O=O))
    y_attn = jnp.concatenate([hd["O"] for hd in heads], axis=1)

    rc = lax.rsqrt(jnp.mean(y_conv * y_conv, axis=-1, keepdims=True) + EPS)
    ra = lax.rsqrt(jnp.mean(y_attn * y_attn, axis=-1, keepdims=True) + EPS)
    y = jnp.concatenate([y_conv * rc * gco, y_attn * ra * gao], axis=1)
    return dict(gate_b=gate_b, gate_c=gate_c, hc=hc, z=z, z1=z1, z2=z2, cz=cz, y_conv=y_conv, y_attn=y_attn,
                rc=rc, ra=ra, heads=heads, y=y, row=row, scale=scale, q_raw=q_raw, rq=rq, k_raw=k_raw, rk=rk)


BPS = 2
TILE = BPS * BLK
KV0 = 3 * CW + AW


def _mix_in_specs(tile_of):
    return [
        pl.BlockSpec(memory_space=pltpu.SMEM),
        pl.BlockSpec((TILE, INW), lambda s: (tile_of(s), 0)),
        pl.BlockSpec((8, CW), lambda s: (jnp.maximum(tile_of(s) * (TILE // 8) - 1, 0), 1)),
        pl.BlockSpec((8, CW), lambda s: (jnp.maximum(tile_of(s) * (TILE // 8) - 1, 0), 2)),
        pl.BlockSpec((BLK, 2 * NKV * HD), lambda s: (jnp.maximum(tile_of(s) * BPS - 1, 0), KV0 // (2 * NKV * HD))),
    ]


def _block_inputs(tile, b, zc_ref, zh_ref, pkv_ref, first_tile):
    P = tile[b * BLK:(b + 1) * BLK, :]
    if b == 0:
        return P, zc_ref[...], zh_ref[...], pkv_ref[...], first_tile
    lo = b * BLK
    return P, tile[lo - 8:lo, CW:2 * CW], tile[lo - 8:lo, 2 * CW:3 * CW], tile[lo - BLK:lo, KV0:KV0 + 2 * NKV * HD], False


def _mix_param_specs():
    return [
        pl.BlockSpec((8, CW), lambda s: (0, 0)),
        pl.BlockSpec((1, AW), lambda s: (0, 0)),
        pl.BlockSpec((1, NKV * HD), lambda s: (0, 0)),
        pl.BlockSpec((1, CW), lambda s: (0, 0)),
        pl.BlockSpec((1, AW), lambda s: (0, 0)),
        pl.BlockSpec((AW, 128), lambda s: (0, 0)),
        pl.BlockSpec((NH * BLK, 2 * BLK), lambda s: (0, 0)),
    ]


def _mix_params(cw8, qg, kg, gco, gao, bias):
    seg = np.zeros((AW, 128), np.float32)
    seg[np.arange(AW), np.arange(AW) // HD] = 1.0
    return (cw8, jnp.tile(qg, (1, NH)), jnp.tile(kg, (1, NKV)), gco, gao, jnp.asarray(seg, BF), bias)


def _mix_fwd(proj, sinks, cw8, qg, kg, gco, gao, bias, comm=()):
    def body(sink_ref, p_ref, zc_ref, zh_ref, pkv_ref, cw_ref, qg_ref, kg_ref, gco_ref, gao_ref, seg_ref, bias_ref, y_ref):
        tile = p_ref[...]
        for b in range(BPS):
            f = _mix_forward(*_block_inputs(tile, b, zc_ref, zh_ref, pkv_ref, pl.program_id(0) == 0), cw_ref[...],
                             qg_ref[...], kg_ref[...], gco_ref[...], gao_ref[...], seg_ref[...], sink_ref, bias_ref, by_head=True)
            y_ref[b * BLK:(b + 1) * BLK, :] = f["y"].astype(BF)

    return _call(
        body, (sinks, proj, proj, proj, proj, *_mix_params(cw8, qg, kg, gco, gao, bias)), name="mix_fwd", grid=(T // TILE,),
        in_specs=_mix_in_specs(lambda s: s) + _mix_param_specs(),
        out_specs=[pl.BlockSpec((TILE, D), lambda s: (s, 0))], out_shape=[SDS((T, D), BF)],
        sem=("parallel",), vmem_mib=40, comm=comm, free=tuple(range(5, 12)))


def _mix_bwd(proj, dy, sinks, cw8, qg, kg, gco, gao, bias, comm=()):
    n_steps = T // TILE

    def tile_of(s):
        return n_steps - 1 - s

    def body(sink_ref, p_ref, zc_ref, zh_ref, pkv_ref, dy_ref, cw_ref, qg_ref, kg_ref, gco_ref, gao_ref, seg_ref, bias_ref,
             dproj_ref, dcw_ref, dqg_ref, dkg_ref, dgco_ref, dgao_ref, dsink_ref, dbias_ref,
             ndcz_ref, dkc_ref, dvc_ref):
        s = pl.program_id(0)

        @pl.when(s == 0)
        def _():
            for r in (dcw_ref, dqg_ref, dkg_ref, dgco_ref, dgao_ref, dsink_ref, dbias_ref, ndcz_ref, dkc_ref, dvc_ref):
                r[...] = jnp.zeros_like(r)

        params = (cw_ref[...], qg_ref[...], kg_ref[...], gco_ref[...], gao_ref[...], seg_ref[...])
        tile = p_ref[...]
        carry = (ndcz_ref[...], dkc_ref[...], dvc_ref[...])
        total = None
        for b in reversed(range(BPS)):
            f = _mix_forward(*_block_inputs(tile, b, zc_ref, zh_ref, pkv_ref, s == n_steps - 1), *params, sink_ref, bias_ref)
            pieces, sums, carry = one_block(f, dy_ref[b * BLK:(b + 1) * BLK, :], params, carry)
            for lo, piece in pieces:
                dproj_ref[b * BLK:(b + 1) * BLK, lo:lo + piece.shape[1]] = piece
            total = sums if total is None else [t + v for t, v in zip(total, sums)]
        ndcz_ref[...], dkc_ref[...], dvc_ref[...] = carry
        dcw, dqg_t, dkg_t, dgco, dgao, dsink, *ds = total
        dcw_ref[0:3, :] += dcw
        dqg_ref[...] += _fold_heads(dqg_t)
        dkg_ref[...] += _fold_heads(dkg_t)
        dgco_ref[...] += dgco
        dgao_ref[...] += dgao
        dsink_ref[...] += dsink
        for h in range(NH):
            dbias_ref[h * BLK:(h + 1) * BLK, :] += ds[h]

    def one_block(f, dy, params, carry):
        cw, qg_v, kg_v, gco_v, gao_v, seg = params
        nxt, dk_carry, dv_carry = carry
        dyc, dgco = _rms_bwd(dy[:, 0:CW], f["y_conv"], f["rc"], gco_v)
        dya, dgao = _rms_bwd(dy[:, CW:CW + AW], f["y_attn"], f["ra"], gao_v)

        row = f["row"]
        dgate_b = dyc * f["cz"]
        dcz = dyc * f["gate_b"]
        dcw = jnp.concatenate([jnp.sum(dcz * f[k], axis=0, keepdims=True) for k in ("z2", "z1", "z")], axis=0)
        n0 = nxt[0:1, :]
        n1 = nxt[1:2, :]
        d1 = jnp.where(row == BLK - 1, n0, pltpu.roll(dcz, BLK - 1, 0))
        d2 = jnp.where(row == BLK - 1, n1, jnp.where(row == BLK - 2, n0, pltpu.roll(dcz, BLK - 2, 0)))
        dz = cw[2:3, :] * dcz + cw[1:2, :] * d1 + cw[0:1, :] * d2
        pieces = [(0, dgate_b.astype(BF)), (CW, (dz * f["hc"]).astype(BF)), (2 * CW, (dz * f["gate_c"]).astype(BF))]

        scale = f["scale"]
        lane = lax.broadcasted_iota(jnp.int32, (1, 128), 1)
        dsink = jnp.zeros((1, 128), F32)
        dq_cols, dk_cols, dv_cols, dk_prev, dv_prev, ds = [], [], [], [], [], []
        for kv in range(NKV):
            dKb = dVb = 0.0
            for h in range(kv * GQ, (kv + 1) * GQ):
                hd = f["heads"][h]
                dO = dya[:, h * HD:(h + 1) * HD]
                delta = jnp.sum(dO * hd["O"], axis=-1, keepdims=True)
                dOb = dO.astype(BF)
                dP = _dot(dOb, hd["vb"], 1, 1)
                dS = hd["probs"] * (dP - delta)
                tot = jnp.sum(hd["psink"] * delta, axis=0, keepdims=True)
                dsink = dsink - jnp.where(lane == h, tot, 0.0)
                ds.append(dS)
                dSb = dS.astype(BF)
                dq_cols.append(_dot(dSb, hd["kb"], 1, 0))
                dKb = dKb + _dot(dSb, hd["Q"], 0, 0)
                dVb = dVb + _dot(hd["probs"].astype(BF), dOb, 0, 0)
            dk_cols.append(dKb[BLK:, :] + dk_carry[:, kv * HD:(kv + 1) * HD])
            dv_cols.append(dVb[BLK:, :] + dv_carry[:, kv * HD:(kv + 1) * HD])
            dk_prev.append(dKb[:BLK, :])
            dv_prev.append(dVb[:BLK, :])
        dq_raw, dqg_t = _head_norm_bwd(jnp.concatenate(dq_cols, axis=1) * scale, f["q_raw"], f["rq"], qg_v, seg)
        dk_raw, dkg_t = _head_norm_bwd(jnp.concatenate(dk_cols, axis=1), f["k_raw"][BLK:, :], f["rk"][BLK:, :], kg_v, seg)
        pieces.append((3 * CW, jnp.concatenate([dq_raw, dk_raw] + dv_cols, axis=1).astype(BF)))
        owed = (dcz[0:8, :], jnp.concatenate(dk_prev, axis=1), jnp.concatenate(dv_prev, axis=1))
        return pieces, [dcw, dqg_t, dkg_t, dgco, dgao, dsink, *ds], owed

    small = lambda r, c: pl.BlockSpec((r, c), lambda s: (0, 0))
    return _call(
        body, (sinks, proj, proj, proj, proj, dy, *_mix_params(cw8, qg, kg, gco, gao, bias)), name="mix_bwd", grid=(n_steps,),
        in_specs=_mix_in_specs(tile_of) + [pl.BlockSpec((TILE, D), lambda s: (tile_of(s), 0))] + _mix_param_specs(),
        out_specs=[pl.BlockSpec((TILE, INW), lambda s: (tile_of(s), 0)), small(8, CW), small(1, HD), small(1, HD),
                   small(1, CW), small(1, AW), small(1, 128), small(NH * BLK, 2 * BLK)],
        out_shape=[SDS((T, INW), BF), SDS((8, CW), F32), SDS((1, HD), F32), SDS((1, HD), F32), SDS((1, CW), F32),
                   SDS((1, AW), F32), SDS((1, 128), F32), SDS((NH * BLK, 2 * BLK), F32)],
        scratch_shapes=[pltpu.VMEM((8, CW), F32), pltpu.VMEM((BLK, NKV * HD), F32), pltpu.VMEM((BLK, NKV * HD), F32)],
        sem=("arbitrary",), vmem_mib=56, comm=comm, free=(1, 2, 3, 4) + tuple(range(6, 13)))


FT = 256
NFT = DFF // FT
RC = 1024
NCH = T // RC
LEAD = 16


def _rows8(x):
    return jnp.sum(x.reshape(x.shape[0] // 8, 8, x.shape[1]), axis=0)


def _ffn_act_specs():
    return [
        pl.BlockSpec((T, FT), lambda j: (0, j)), pl.BlockSpec((T, FT), lambda j: (0, NFT + j)),
        pl.BlockSpec((8, FT), lambda j: (0, j)), pl.BlockSpec((8, FT), lambda j: (0, NFT + j)),
        pl.BlockSpec((1, FT), lambda j: (0, j)), pl.BlockSpec((1, FT), lambda j: (0, NFT + j)),
    ]


def _conv_rows(win, w, b, n):
    win = win.astype(F32)
    u = win[LEAD:LEAD + n]
    u1 = pltpu.roll(win, 1, 0)[LEAD:LEAD + n]
    u2 = pltpu.roll(win, 2, 0)[LEAD:LEAD + n]
    return u2, u1, u, w[0:1, :] * u2 + w[1:2, :] * u1 + w[2:3, :] * u + b


def _ffn_act(up, fw8, fb, comm=()):
    ring = 3

    def body(up_ref, wg_ref, wv_ref, bg_ref, bv_ref, a_ref, pg_ref, pv_ref, g_ring, v_ring, sem):
        wg, wv, bg, bv = wg_ref[...], wv_ref[...], bg_ref[...], bv_ref[...]
        j = pl.program_id(0)

        def fetch(t):
            slot = lax.rem(t, ring)
            return [pltpu.make_async_copy(up_ref.at[:, pl.ds(pl.multiple_of((half * NFT + t) * FT, FT), FT)],
                                          buf.at[slot], sem.at[half, slot]) for half, buf in ((0, g_ring), (1, v_ring))]

        @pl.when(j == 0)
        def _():
            for t in range(ring - 1):
                for c in fetch(t):
                    c.start()

        @pl.when(j + ring - 1 < NFT)
        def _():
            for c in fetch(j + ring - 1):
                c.start()

        for c in fetch(j):
            c.wait()
        ug_ref, uv_ref = g_ring.at[lax.rem(j, ring)], v_ring.at[lax.rem(j, ring)]

        def chunk(rows, win_g, win_v):
            gp = _conv_rows(win_g, wg, bg, RC)[3]
            vp = _conv_rows(win_v, wv, bv, RC)[3]
            a_ref[rows, :] = (gp * jax.nn.sigmoid(gp) * vp).astype(BF)
            pg_ref[0, rows, :] = gp.astype(BF)
            pv_ref[0, rows, :] = vp.astype(BF)

        zero = jnp.zeros((LEAD, FT), BF)
        chunk(pl.ds(0, RC), jnp.concatenate([zero, ug_ref[0:RC, :]], axis=0), jnp.concatenate([zero, uv_ref[0:RC, :]], axis=0))

        def step(i, carry):
            r0 = pl.multiple_of(i * RC, RC)
            win = pl.ds(r0 - LEAD, RC + LEAD)
            chunk(pl.ds(r0, RC), ug_ref[win, :], uv_ref[win, :])
            return carry

        lax.fori_loop(1, NCH, step, 0)

    tile = pl.BlockSpec((1, T, FT), lambda j: (j, 0, 0))
    return _call(
        body, (up, fw8, fw8, fb, fb), name="ffn_act", grid=(NFT,), in_specs=[ANY] + _ffn_act_specs()[2:],
        out_specs=[pl.BlockSpec((T, FT), lambda j: (0, j)), tile, tile],
        out_shape=[SDS((T, DFF), BF), SDS((NFT, T, FT), BF), SDS((NFT, T, FT), BF)],
        scratch_shapes=[pltpu.VMEM((ring, T, FT), BF), pltpu.VMEM((ring, T, FT), BF), pltpu.SemaphoreType.DMA((2, ring))],
        sem=("arbitrary",), vmem_mib=40, comm=comm, free=(1, 2, 3, 4))


def _ffn_act_bwd(up, pre_g, pre_v, da, fw8, comm=()):
    ext = RC + LEAD

    def body(ug_ref, uv_ref, wg_ref, wv_ref, pg_ref, pv_ref, da_ref, dug_ref, duv_ref, dwg_ref, dwv_ref, dbg_ref, dbv_ref):
        wg, wv = wg_ref[...], wv_ref[...]

        def chunk(u_g, u_v, gp, vp, da_e):
            gp, vp, da_e = gp.astype(F32), vp.astype(F32), da_e.astype(F32)
            sig = jax.nn.sigmoid(gp)
            dvp = da_e * (gp * sig)
            dgp = da_e * vp * (sig * (1.0 + gp * (1.0 - sig)))

            def branch(dp, w, u):
                d0, d1, d2 = dp[0:RC], pltpu.roll(dp, ext - 1, 0)[0:RC], pltpu.roll(dp, ext - 2, 0)[0:RC]
                du = (w[2:3, :] * d0 + w[1:2, :] * d1 + w[0:1, :] * d2).astype(BF)
                u = u.astype(F32)
                return du, [_rows8(d0), _rows8(d2 * u), _rows8(d1 * u), _rows8(d0 * u)]

            dug, sums_g = branch(dgp, wg, u_g)
            duv, sums_v = branch(dvp, wv, u_v)
            return dug, duv, sums_g + sums_v

        def step(i, acc):
            r0 = pl.multiple_of(i * RC, RC)
            rows, more = pl.ds(r0, RC), pl.ds(r0, ext)
            dug, duv, part = chunk(ug_ref[rows, :], uv_ref[rows, :], pg_ref[0, more, :], pv_ref[0, more, :], da_ref[more, :])
            dug_ref[rows, :] = dug
            duv_ref[rows, :] = duv
            return [a + p for a, p in zip(acc, part)]

        acc = lax.fori_loop(0, NCH - 1, step, [jnp.zeros((8, FT), F32)] * 8)
        r0 = T - RC
        zero = jnp.zeros((LEAD, FT), BF)
        tail = lambda rows: jnp.concatenate([rows, zero], axis=0)
        dug, duv, part = chunk(ug_ref[r0:T, :], uv_ref[r0:T, :], tail(pg_ref[0, r0:T, :]), tail(pv_ref[0, r0:T, :]),
                               tail(da_ref[r0:T, :]))
        dug_ref[r0:T, :] = dug
        duv_ref[r0:T, :] = duv
        tot = [jnp.sum(a + p, axis=0, keepdims=True) for a, p in zip(acc, part)]
        for k, (dw_ref, db_ref) in enumerate(((dwg_ref, dbg_ref), (dwv_ref, dbv_ref))):
            db_ref[...] = tot[4 * k]
            dw_ref[...] = jnp.zeros_like(dw_ref)
            for r in range(3):
                dw_ref[r:r + 1, :] = tot[4 * k + 1 + r]

    col = lambda r: pl.BlockSpec((r, FT), lambda j: (0, j))
    return _call(
        body, (up, up, fw8, fw8, pre_g, pre_v, da), name="ffn_act_bwd", grid=(NFT,),
        in_specs=_ffn_act_specs()[0:4] + [pl.BlockSpec((1, T, FT), lambda j: (j, 0, 0))] * 2 + [col(T)],
        out_specs=[col(T), col(T), col(8), col(8), col(1), col(1)],
        out_shape=[SDS((T, DFF), BF), SDS((T, DFF), BF), SDS((8, DFF), F32), SDS((8, DFF), F32),
                   SDS((1, DFF), F32), SDS((1, DFF), F32)],
        sem=("parallel",), vmem_mib=40, comm=comm, free=(0, 1, 2, 3))


def _ffn_down_bwd(dh2b, w_down, comm=()):
    tm = TM

    def body(d_ref, w_ref, o_ref):
        o_ref[...] = _dot(d_ref[...], w_ref[...], 1, 1).astype(BF)

    return _call(
        body, (dh2b, w_down), name="ffn_down_bwd", grid=(T // tm,),
        in_specs=[pl.BlockSpec((tm, D), lambda i: (i, 0)), _resident((DFF, D))],
        out_specs=[pl.BlockSpec((tm, DFF), lambda i: (i, 0))], out_shape=[SDS((T, DFF), BF)],
        sem=("parallel",), vmem_mib=40, comm=comm, free=(0, 1))


def _norm_matmul_bwd(name, a_list, w_t, k_offsets, xin, g, dres, want_bf16, comm=(), slot=None, band=(), pack=()):
    tm = TM
    ks = [a.shape[1] for a in a_list]
    n_a = len(a_list)
    n_pre = 0 if slot is None else 1
    n_in = n_a + 4 + len(band) + len(pack)

    def body(*refs):
        refs = refs[n_pre:]
        a_refs = refs[:n_a]
        w_ref, x_ref, g_ref, r_ref = refs[n_a:n_a + 4]
        outs = refs[n_in:]
        dg_out = outs[1 + want_bf16]
        dx_ref, dg_ref = outs[0], (dg_out if slot is None else dg_out.at[0])

        @pl.when(pl.program_id(0) == 0)
        def _():
            dg_ref[...] = jnp.zeros_like(dg_ref)
            if band:
                db_ref, bk_ref, tbl_ref = refs[n_a + 4], refs[n_a + 5], outs[2 + want_bf16]
                bk = bk_ref[...]
                for b in range(NBUCKET):
                    m = bk == b
                    for h in range(NH):
                        v = jnp.where(m, db_ref[h * BLK:(h + 1) * BLK, :], 0.0)
                        tbl_ref[0, h:h + 1, b:b + 1] = jnp.sum(jnp.sum(v, axis=1, keepdims=True), axis=0, keepdims=True)
            if pack:
                pm_ref = outs[2 + want_bf16 + bool(band)]
                pm_ref[...] = jnp.zeros_like(pm_ref)
                _fill_mix(pm_ref, *refs[n_in - len(pack):n_in])

        du = _dot(a_refs[0][...], w_ref[k_offsets[0]:k_offsets[0] + ks[0], :], 1, 0)
        for k in range(1, n_a):
            du = du + _dot(a_refs[k][...], w_ref[k_offsets[k]:k_offsets[k] + ks[k], :], 1, 0)
        x = x_ref[...]
        r = lax.rsqrt(jnp.mean(x * x, axis=-1, keepdims=True) + EPS)
        dx, dg = _rms_bwd(du, x, r, g_ref[...])
        dx = r_ref[...] + dx
        dx_ref[...] = dx
        if want_bf16:
            outs[1][...] = dx.astype(BF)
        dg_ref[...] += dg

    tile = lambda c: pl.BlockSpec((tm, c), lambda i, *_: (i, 0))
    if slot is None:
        dg_spec, dg_shape = pl.BlockSpec((1, D), lambda i: (0, 0)), SDS((1, D), F32)
    else:
        dg_spec, dg_shape = pl.BlockSpec((1, 1, D), lambda i, slot_ref: (slot_ref[0], 0, 0)), SDS((N_DEV, 1, D), F32)
    out_specs = [tile(D)] + ([tile(D)] if want_bf16 else []) + [dg_spec]
    out_shape = [SDS((T, D), F32)] + ([SDS((T, D), BF)] if want_bf16 else []) + [dg_shape]
    if band:
        out_specs.append(pl.BlockSpec((1, NH, NBUCKET), lambda i, slot_ref: (slot_ref[0], 0, 0)))
        out_shape.append(SDS((N_DEV, NH, NBUCKET), F32))
    if pack:
        out_specs.append(pl.BlockSpec((1, 8, D), lambda i, slot_ref: (slot_ref[0], 0, 0)))
        out_shape.append(SDS((N_DEV, 8, D), F32))
    return _call(
        body, (*a_list, w_t, xin, g, dres, *band, *pack), name=name, grid=(T // tm,), prefetch=() if slot is None else (slot,),
        in_specs=[tile(k) for k in ks] + [_resident(w_t.shape), tile(D), pl.BlockSpec((1, D), lambda i, *_: (0, 0)), tile(D)]
        + [pl.BlockSpec(b.shape, lambda i, *_: (0, 0)) for b in (*band, *pack)],
        out_specs=out_specs, out_shape=out_shape, sem=("arbitrary",), vmem_mib=56, comm=comm, free=tuple(range(n_a + 4)))


def _out_bwd(dh1b, w_out, comm=()):
    tm = TM

    def body(d_ref, w_ref, o_ref):
        o_ref[...] = _dot(d_ref[...], w_ref[...], 1, 1)

    return _call(
        body, (dh1b, w_out), name="out_bwd", grid=(T // tm,),
        in_specs=[pl.BlockSpec((tm, D), lambda i: (i, 0)), _resident((D, D))],
        out_specs=[pl.BlockSpec((tm, D), lambda i: (i, 0))], out_shape=[SDS((T, D), F32)],
        sem=("parallel",), vmem_mib=32, comm=comm, free=(0, 1))


def _wgrad(name, a_list, b, old_a, comm=()):
    m_k = a_list[0].shape[1]
    tm = max(t for t in range(128, m_k // 2 + 1, 128) if m_k % t == 0)
    steps = [a.shape[1] // tm for a in a_list]
    starts = [sum(steps[:k]) for k in range(len(a_list))]
    n_a = len(a_list)

    def body(*refs):
        a_refs, b_ref, o_ref = refs[:n_a], refs[n_a], refs[n_a + 1]
        i = pl.program_id(0)
        for k in range(n_a):
            @pl.when((i >= starts[k]) & (i < starts[k] + steps[k]))
            def _(k=k):
                o_ref[...] = _dot(a_refs[k][...], b_ref[...], 0, 0).astype(BF)

    def a_spec(k):
        return pl.BlockSpec((T, tm), lambda i: (0, jnp.clip(i - starts[k], 0, steps[k] - 1)))

    m_total = tm * sum(steps)
    return _call(
        body, (*a_list, b), name=name, grid=(sum(steps),),
        in_specs=[a_spec(k) for k in range(n_a)] + [_resident((T, D))],
        out_specs=[pl.BlockSpec((tm, D), lambda i: (i, 0))], out_shape=[SDS((m_total, D), BF)],
        sem=("parallel",), vmem_mib=40, comm=comm, free=() if old_a is None else tuple(range(n_a)) if old_a else (n_a,))


def _chip_sum(name, gbf, from_sib, core, chip):
    h = gbf.shape[1]
    th = h

    def body(core_ref, chip_ref, g_ref, s_ref, pbf_ref, own_ref):
        p = g_ref[0].astype(F32) + s_ref[0].astype(F32)
        pbf_ref[0] = p.astype(BF)

        @pl.when(pl.program_id(1) == chip_ref[0])
        def _():
            own_ref[...] = p

    grid_spec = pltpu.PrefetchScalarGridSpec(
        num_scalar_prefetch=2, grid=(h // th, N_CHIPS),
        in_specs=[pl.BlockSpec((1, th, D), lambda t, jj, core_ref, chip_ref: (2 * jj + core_ref[0], t, 0)),
                  pl.BlockSpec((1, th, D), lambda t, jj, core_ref, chip_ref: (jj, t, 0))],
        out_specs=[pl.BlockSpec((1, th, D), lambda t, jj, core_ref, chip_ref: (jj, t, 0)),
                   pl.BlockSpec((th, D), lambda t, jj, core_ref, chip_ref: (t, 0))],
    )
    return _pcall(
        body, name=name, grid_spec=grid_spec, out_shape=_in_hbm([SDS((N_CHIPS, h, D), BF), SDS((h, D), F32)]),
        compiler_params=_params(("arbitrary", "arbitrary"), 32),
    )(core, chip, *_from_hbm(gbf, from_sib))


def _final_sum(name, own, from_chips, core, comm=()):
    h = own.shape[0]
    n = 4 if h % (4 * ROWS16) == 0 else 2
    th = h // n

    def body(core_ref, o_ref, r_ref, f_ref):
        f_ref[0] = ((o_ref[...] + r_ref[0].astype(F32)) + r_ref[1].astype(F32)) + r_ref[2].astype(F32)

    return _call(
        body, (own, from_chips), name=name, grid=(n,), prefetch=(core,),
        in_specs=[pl.BlockSpec((th, D), lambda i, core_ref: (i, 0)), pl.BlockSpec((3, th, D), lambda i, core_ref: (0, i, 0))],
        out_specs=[pl.BlockSpec((1, th, D), lambda i, core_ref: (core_ref[0], i, 0))], out_shape=[SDS((2, h, D), F32)],
        sem=("arbitrary",), vmem_mib=40, comm=comm)


def _adam_math(w, g, m, v):
    nm = ADAM_B1 * m + (1.0 - ADAM_B1) * g
    nv = ADAM_B2 * v + (1.0 - ADAM_B2) * (g * g)
    m_hat = nm / (1.0 - ADAM_B1 ** ADAM_STEP)
    v_hat = nv / (1.0 - ADAM_B2 ** ADAM_STEP)
    return -ADAM_LR * (m_hat / (jnp.sqrt(v_hat) + ADAM_EPS) + ADAM_WD * w), nm, nv


def _adamw(name, w, g, m, v, tr, copy_g=False, stage=True, g_transposed=False):
    rows, cols = w.shape

    def body(w_ref, g_ref, m_ref, v_ref, *outs):
        d_ref, nm_ref, nv_ref = outs[-3:]
        for c in [pl.ds(c0, 128) for c0 in range(0, cols, 128)] if g_transposed else [slice(None)]:
            g_val = g_ref[c, :].T if g_transposed else g_ref[...]
            if copy_g:
                outs[0][:, c] = g_val
            d_ref[:, c], nm_ref[:, c], nv_ref[:, c] = _adam_math(w_ref[:, c], g_val, m_ref[:, c], v_ref[:, c])

    spec = pl.BlockSpec((tr, cols), lambda i: (i, 0))
    n_out = 4 if copy_g else 3
    g_spec = pl.BlockSpec((cols, tr), lambda i: (0, i)) if g_transposed else spec
    return _call(body, (w, g, m, v), name=name, grid=(rows // tr,), in_specs=[spec, g_spec, spec, spec], out_specs=[spec] * n_out,
                 out_shape=[SDS((rows, cols), F32)] * n_out, sem=("parallel",), vmem_mib=32,
                 free=(0, 2, 3) if stage else ())


C_SQ = 2 * DFF
P_W = C_SQ + 128
R_G2, R_GO, R_DCW, R_QK = 0, 1, 2, 5
C_GCO, C_GAO, C_DQG, C_DKG, C_SINK = 0, CW, 0, 128, 256


def _pack(name, me, ins, width, fill):
    def body(me_ref, *refs):
        o = refs[-1]
        o[...] = jnp.zeros_like(o)
        fill(o, *refs[:-1])

    return _call(body, ins, name=name, grid=(1,), prefetch=(me,),
                 in_specs=[pl.BlockSpec(a.shape, lambda i, me_ref: (0, 0)) for a in ins],
                 out_specs=[pl.BlockSpec((1, 8, width), lambda i, me_ref: (me_ref[0], 0, 0))],
                 out_shape=[SDS((N_DEV, 8, width), F32)], sem=("arbitrary",))[0]


def _pack_ffn(me, dfwg, dfwv, dfbg, dfbv, sq):
    def fill(o, dfwg_r, dfwv_r, dfbg_r, dfbv_r, sq_r):
        o[0, :, 0:DFF] = dfwg_r[...]
        o[0, :, DFF:2 * DFF] = dfwv_r[...]
        o[0, 3:4, 0:DFF] = dfbg_r[...]
        o[0, 3:4, DFF:2 * DFF] = dfbv_r[...]
        o[0, :, C_SQ:C_SQ + 128] = sq_r[...]

    return _pack("pack_ffn", me, (dfwg, dfwv, dfbg, dfbv, sq), P_W, fill)


def _fill_mix(o, dg2_r, dgco_r, dgao_r, dcw_r, dqg_r, dkg_r, dsink_r):
    o[0, R_G2:R_G2 + 1, :] = dg2_r[...]
    o[0, R_GO:R_GO + 1, C_GCO:C_GCO + CW] = dgco_r[...]
    o[0, R_GO:R_GO + 1, C_GAO:C_GAO + AW] = dgao_r[...]
    o[0, R_DCW:R_DCW + 3, 0:CW] = dcw_r[0:3, :]
    o[0, R_QK:R_QK + 1, C_DQG:C_DQG + HD] = dqg_r[...]
    o[0, R_QK:R_QK + 1, C_DKG:C_DKG + HD] = dkg_r[...]
    o[0, R_QK:R_QK + 1, C_SINK:C_SINK + 128] = dsink_r[...]


N_SMALL = 11


def _small_adam(chip, p_all, pm_all, g1_all, tbl_all, ws, ms, vs):
    fw_cols = 2 * DFF // N_CHIPS
    cw_cols = CW // N_CHIPS

    def body(chip_ref, p_ref, fw_ref, pm_ref, cw_ref, g1_ref, tbl_ref, *refs):
        w_r, m_r, v_r = refs[0:N_SMALL], refs[N_SMALL:2 * N_SMALL], refs[2 * N_SMALL:3 * N_SMALL]
        outs = refs[3 * N_SMALL:]
        g_o, d_o, nm_o, nv_o = (outs[k * N_SMALL:(k + 1) * N_SMALL] for k in range(4))
        loss_o = outs[4 * N_SMALL]

        def total(ref):
            s = ref[0]
            for k in range(1, N_DEV):
                s = s + ref[k]
            return s

        S = total(p_ref)
        fw = total(fw_ref)
        M = total(pm_ref)
        cw = total(cw_ref)

        def step(i, g, at):
            d, nm, nv = _adam_math(w_r[i][at], g, m_r[i][at], v_r[i][at])
            g_o[i][at], d_o[i][at], nm_o[i][at], nv_o[i][at] = g, d, nm, nv

        everything = (slice(None), slice(None))
        step(0, total(g1_ref), everything)
        for r in range(3):
            step(1, cw[R_DCW + r:R_DCW + r + 1, :], (r, slice(None), slice(None)))
        step(2, M[R_QK:R_QK + 1, C_DQG:C_DQG + HD], everything)
        step(3, M[R_QK:R_QK + 1, C_DKG:C_DKG + HD], everything)
        step(4, total(tbl_ref), everything)
        step(5, M[R_QK:R_QK + 1, C_SINK:C_SINK + NH], everything)
        step(6, M[R_GO:R_GO + 1, C_GCO:C_GCO + CW], everything)
        step(7, M[R_GO:R_GO + 1, C_GAO:C_GAO + AW], everything)
        step(8, M[R_G2:R_G2 + 1, :], everything)
        for r in range(3):
            step(9, fw[r:r + 1, :], (r, slice(None), slice(None)))
        step(10, S[3:4, 0:2 * DFF], everything)
        sq = S[:, C_SQ:C_SQ + 128]
        loss_o[...] = jnp.sum(jnp.sum(sq, axis=1, keepdims=True), axis=0, keepdims=True) * (0.5 / D)

    def full(a):
        n = len(a.shape)
        return pl.BlockSpec(a.shape, lambda i, chip_ref: (0,) * n)

    params = [*ws, *ms, *vs]
    out = _call(
        body, (p_all, p_all, pm_all, pm_all, g1_all, tbl_all, *params), name="small_adam", grid=(1,), prefetch=(chip,),
        in_specs=[full(p_all),
                  pl.BlockSpec((N_DEV, 8, fw_cols), lambda i, chip_ref: (0, 0, chip_ref[0])),
                  full(pm_all),
                  pl.BlockSpec((N_DEV, 8, cw_cols), lambda i, chip_ref: (0, 0, chip_ref[0])),
                  full(g1_all), full(tbl_all), *[full(a) for a in params]],
        out_specs=[full(a) for a in ws] * 4 + [pl.BlockSpec((1, 1), lambda i, chip_ref: (0, 0))],
        out_shape=[SDS(a.shape, F32) for a in ws] * 4 + [SDS((1, 1), F32)], sem=("arbitrary",), vmem_mib=32)
    return out[0:N_SMALL], out[N_SMALL:2 * N_SMALL], out[2 * N_SMALL:3 * N_SMALL], out[3 * N_SMALL:4 * N_SMALL], out[4 * N_SMALL]


PLACE_STEPS = 4


def _place_specs(shards):
    rows = [s.shape[0] // PLACE_STEPS for s in shards]
    return ([pl.BlockSpec((r, D), lambda i, chip_ref: (i, 0)) for r in rows],
            [pl.BlockSpec((r, D), lambda i, chip_ref: (chip_ref[0] * PLACE_STEPS + i, 0)) for r in rows],
            [SDS((N_CHIPS * s.shape[0], D), BF) for s in shards])


def _place_first(chip, shard, conv_w, ffn_conv_w):
    def body(chip_ref, a, s0, s1, o, t0, t1):
        o[...] = a[...].astype(BF)

        @pl.when(pl.program_id(0) == 0)
        def _():
            for s, t in ((s0, t0), (s1, t1)):
                t[...] = jnp.zeros_like(t)
                t[0, 0:3, :] = s[...]

    ins, outs, shapes = _place_specs([shard])
    taps = (conv_w, ffn_conv_w)
    return _call(
        body, (shard, conv_w, ffn_conv_w), name="place_first", grid=(PLACE_STEPS,), prefetch=(chip,),
        in_specs=ins + [pl.BlockSpec(s.shape, lambda i, chip_ref: (0, 0)) for s in taps],
        out_specs=outs + [pl.BlockSpec((1, 8, s.shape[1]), lambda i, chip_ref: (chip_ref[0], 0, 0)) for s in taps],
        out_shape=shapes + [SDS((N_CHIPS, 8, s.shape[1]), F32) for s in taps],
        sem=("arbitrary",), vmem_mib=32, free=(1, 2))


def _place_rest(chip, shards, w_up, table, bucket, comm):
    n = len(shards)
    c_up = w_up.shape[1]
    edges = [round(k * (c_up // 128) / PLACE_STEPS) * 128 for k in range(PLACE_STEPS + 1)]

    def body(chip_ref, *refs):
        a, (up_ref, tab_ref, bk_ref), o = refs[:n], refs[n:n + 3], refs[n + 3:2 * n + 3]
        up_o, bias_ref = refs[2 * n + 3:]
        for src, dst in zip(a, o):
            dst[...] = src[...].astype(BF)
        for k in range(PLACE_STEPS):
            @pl.when(pl.program_id(0) == k)
            def _(k=k):
                up_o[edges[k]:edges[k + 1], :] = up_ref[:, edges[k]:edges[k + 1]].T.astype(BF)

        @pl.when(pl.program_id(0) == 0)
        def _():
            bk = bk_ref[...]
            eq = [bk == b for b in range(NBUCKET)]
            for h in range(NH):
                acc = jnp.zeros((BLK, 2 * BLK), F32)
                for b in range(NBUCKET):
                    acc = jnp.where(eq[b], tab_ref[h, b], acc)
                bias_ref[h * BLK:(h + 1) * BLK, :] = acc

    ins, outs, shapes = _place_specs(shards)
    return _call(
        body, (*shards, w_up, table, bucket), name="place_rest", grid=(PLACE_STEPS,), prefetch=(chip,),
        in_specs=ins + [_resident(w_up.shape), pl.BlockSpec(memory_space=pltpu.SMEM),
                        pl.BlockSpec(bucket.shape, lambda i, chip_ref: (0, 0))],
        out_specs=outs + [pl.BlockSpec((c_up, D), lambda i, chip_ref: (chip_ref[0], 0)),
                          pl.BlockSpec((NH * BLK, 2 * BLK), lambda i, chip_ref: (0, 0))],
        out_shape=shapes + [SDS((N_CHIPS * c_up, D), BF), SDS((NH * BLK, 2 * BLK), F32)],
        sem=("arbitrary",), vmem_mib=32, comm=comm, free=(n + 1, n + 2))


def kernel(x, norm_mix_g, w_in, conv_w, q_norm_g, k_norm_g, rel_bias_table, sinks, out_norm_conv_g, out_norm_attn_g, w_out, norm_ffn_g, w_up, ffn_conv_w, ffn_conv_b, w_down, loss_target, m_norm_mix_g, m_w_in, m_conv_w, m_q_norm_g, m_k_norm_g, m_rel_bias_table, m_sinks, m_out_norm_conv_g, m_out_norm_attn_g, m_w_out, m_norm_ffn_g, m_w_up, m_ffn_conv_w, m_ffn_conv_b, m_w_down, v_norm_mix_g, v_w_in, v_conv_w, v_q_norm_g, v_k_norm_g, v_rel_bias_table, v_sinks, v_out_norm_conv_g, v_out_norm_attn_g, v_w_out, v_norm_ffn_g, v_w_up, v_ffn_conv_w, v_ffn_conv_b, v_w_down):
    as_arg = lambda i: jnp.reshape(i, (1,)).astype(jnp.int32)
    chip = as_arg(2 * lax.axis_index("x") + lax.axis_index("y"))
    core = as_arg(lax.axis_index("c"))
    me = 2 * chip + core
    xs, tgt = x[0], loss_target[0]
    qg, kg, gco, gao, g1, g2, fb = q_norm_g, k_norm_g, out_norm_conv_g, out_norm_attn_g, norm_mix_g, norm_ffn_g, ffn_conv_b
    pieces = lambda g: g.reshape(N_DEV, g.shape[0] // N_DEV, D)
    whole = lambda f: f.reshape(2 * f.shape[1], D)

    bucket = jnp.asarray(_bucket_table())
    p_in, p_cw, p_fw = _place_first(chip, w_in[0].T, conv_w[0], ffn_conv_w[0])
    p_out, p_down, p_up, bias, w_int, cw_all, fw_all = _place_rest(
        chip, [w_out[0], w_down[0]], w_up[0], rel_bias_table.T, bucket,
        comm=[_t_gather(p_in, relayed_first=True), _t_small_weights(p_cw), _t_small_weights(p_fw)])
    cw8 = jnp.transpose(cw_all, (1, 0, 2)).reshape(8, CW)
    fw8 = jnp.transpose(fw_all, (1, 0, 2)).reshape(8, 2 * DFF)

    early = 3 / 11
    proj, u1, w_out_f, p_up = _inproj(xs, g1, w_int, comm=[_t_gather(p_out), _t_gather(p_up, (0, early))])
    y, w_upt = _mix_fwd(proj, sinks, cw8, qg, kg, gco, gao, bias, comm=[_t_gather(p_up, (early, 1))])
    h1, u2 = _outproj(y, w_out_f, xs, g2)
    up, w_down_f = _ffn_up(u2, w_upt, comm=[_t_gather(p_down)])
    a, pre_g, pre_v = _ffn_act(up, fw8, fb)
    dh2, dh2b, sq = _ffn_down(a, w_down_f, h1, tgt)

    gdbf, = _wgrad("wgrad_down", [a], dh2b, None)
    da, sib_down = _ffn_down_bwd(dh2b, w_down_f, comm=[_t_sibling(pieces(gdbf))])
    pbf_down, own_down = _chip_sum("chip_sum_w_down", pieces(gdbf), sib_down, core, chip)
    dug, duv, dfwg, dfwv, dfbg, dfbv, chips_down = _ffn_act_bwd(up, pre_g, pre_v, da, fw8, comm=[_t_chips(pbf_down)])
    fin_down, = _final_sum("final_sum_w_down", own_down, chips_down, core)
    gubf, = _wgrad("wgrad_up", [dug, duv], u2, False)
    p_all = _pack_ffn(me, dfwg, dfwv, dfbg, dfbv, sq)
    dh1, dh1b, dg2, sib_up, fin_down, p_all = _norm_matmul_bwd(
        "ffn_up_bwd", [dug, duv], w_upt, [0, DFF], h1, g2, dh2, True,
        comm=[_t_sibling(pieces(gubf)), _t_swap(fin_down), _t_allgather(p_all)])
    pbf_up, own_up = _chip_sum("chip_sum_w_up", pieces(gubf), sib_up, core, chip)
    gobf, = _wgrad("wgrad_out", [y], dh1b, True)
    dy, sib_out = _out_bwd(dh1b, w_out_f, comm=[_t_sibling(pieces(gobf))])
    pbf_out, own_out = _chip_sum("chip_sum_w_out", pieces(gobf), sib_out, core, chip)
    dproj, dcw8, dqg, dkg, dgco, dgao, dsink, dbias, chips_up = _mix_bwd(
        proj, dy, sinks, cw8, qg, kg, gco, gao, bias, comm=[_t_chips(pbf_up)])
    fin_up, = _final_sum("final_sum_w_up", own_up, chips_up, core)
    gibf, chips_out, fin_up = _wgrad("wgrad_in", [dproj], u1, False, comm=[_t_chips(pbf_out), _t_swap(fin_up)])
    fin_out, sib_in = _final_sum("final_sum_w_out", own_out, chips_out, core, comm=[_t_sibling(pieces(gibf))])
    pbf_in, own_in = _chip_sum("chip_sum_w_in", pieces(gibf), sib_in, core, chip)
    dx, g1_all, tbl_all, pm_all, chips_in, fin_out = _norm_matmul_bwd(
        "in_bwd", [dproj], w_int, [0], xs, g1, dh1, False, comm=[_t_chips(pbf_in), _t_swap(fin_out)], slot=me,
        band=(dbias, bucket), pack=(dg2, dgco, dgao, dcw8, dqg, dkg, dsink))
    fin_in, = _final_sum("final_sum_w_in", own_in, chips_in, core)
    g1_all, tbl_all, pm_all, fin_in = _comm_call(
        "gather_last", [_t_allgather(g1_all), _t_allgather(tbl_all), _t_allgather(pm_all), _t_swap(fin_in)])

    g_w_out, g_w_down = whole(fin_out), whole(fin_down)
    g_w_down, d_down, nm_down, nv_down = _adamw("adamw_w_down", w_down[0], g_w_down, m_w_down[0], v_w_down[0], 352, True)
    g_w_up, d_up, nm_up, nv_up = _adamw(
        "adamw_w_up", w_up[0], whole(fin_up), m_w_up[0], v_w_up[0], 256, True, stage=False, g_transposed=True)
    g_w_out, d_out, nm_out, nv_out = _adamw("adamw_w_out", w_out[0], g_w_out, m_w_out[0], v_w_out[0], 256, True, stage=False)
    g_w_in, d_in, nm_in, nv_in = [a.T for a in _adamw(
        "adamw_w_in", w_in[0].T, whole(fin_in), m_w_in[0].T, v_w_in[0].T, INW // N_CHIPS // 3, True, stage=False)]
    taps = lambda a: jnp.transpose(a, (1, 0, 2))
    sw = [norm_mix_g, taps(conv_w), q_norm_g, k_norm_g, rel_bias_table.T, sinks, out_norm_conv_g, out_norm_attn_g,
          norm_ffn_g, taps(ffn_conv_w), ffn_conv_b]
    smm = [m_norm_mix_g, taps(m_conv_w), m_q_norm_g, m_k_norm_g, m_rel_bias_table.T, m_sinks, m_out_norm_conv_g,
           m_out_norm_attn_g, m_norm_ffn_g, taps(m_ffn_conv_w), m_ffn_conv_b]
    smv = [v_norm_mix_g, taps(v_conv_w), v_q_norm_g, v_k_norm_g, v_rel_bias_table.T, v_sinks, v_out_norm_conv_g,
           v_out_norm_attn_g, v_norm_ffn_g, taps(v_ffn_conv_w), v_ffn_conv_b]
    *small_out, loss = _small_adam(chip, p_all, pm_all, g1_all, tbl_all, sw, smm, smv)
    sg, sd, snm, snv = [list(r) for r in small_out]
    for r in (sg, sd, snm, snv):
        r[1], r[4], r[9] = taps(r[1]), r[4].T, taps(r[9])

    def order(s, b_in, b_out, b_up, b_down):
        return (s[0], b_in[None], s[1], s[2], s[3], s[4], s[5], s[6], s[7], b_out[None], s[8], b_up[None],
                s[9], s[10], b_down[None])

    return (loss.reshape(()), dx[None],
            *order(sg, g_w_in, g_w_out, g_w_up, g_w_down),
            *order(sd, d_in, d_out, d_up, d_down),
            *order(snm, nm_in, nm_out, nm_up, nm_down),
            *order(snv, nv_in, nv_out, nv_up, nv_down))
```

```python
import functools
import math

import numpy as np

import jax
import jax.numpy as jnp
from jax import lax
from jax.experimental import pallas as pl
from jax.experimental.pallas import tpu as pltpu

F32 = jnp.float32
BF = jnp.bfloat16
SDS = jax.ShapeDtypeStruct

T = 2048
D = 1024
CW = 512
AW = 512
HD = 64
NH = 8
NKV = 2
GQ = 4
INW = 2304
DFF = 2816
BLK = 128
NB = T // BLK
NBUCKET = 32
EPS = 1e-6
NEG_INF = -1e30
N_CHIPS = 4
N_DEV = 8

ADAM_LR = 0.001
ADAM_B1 = 0.9
ADAM_B2 = 0.999
ADAM_EPS = 1e-08
ADAM_WD = 0.01
ADAM_STEP = 10

TM = 512
MIB = 1024 * 1024
MESH = pl.DeviceIdType.MESH
ANY = pl.BlockSpec(memory_space=pl.ANY)

_pcall = pl.pallas_call


def _params(sem=None, vmem_mib=None, collective_id=None):
    kw = {} if collective_id is None else {"collective_id": collective_id}
    if sem is not None:
        kw["dimension_semantics"] = sem
    if vmem_mib is not None:
        kw["vmem_limit_bytes"] = vmem_mib * MIB
    return pltpu.CompilerParams(**kw)


def _resident(shape):
    return pl.BlockSpec(shape, lambda *_: (0,) * len(shape), pipeline_mode=pl.Buffered(1))


def _dot(a, b, ca, cb):
    return lax.dot_general(a, b, (((ca,), (cb,)), ((), ())), preferred_element_type=F32)


def _rms_bwd(dy, x, r, g):
    dg = jnp.sum(dy * (x * r), axis=0, keepdims=True)
    dgx = dy * g
    dx = r * dgx - x * (r * r * r) * jnp.mean(x * dgx, axis=-1, keepdims=True)
    return dx, dg


def _where():
    x, y, c = lax.axis_index("x"), lax.axis_index("y"), lax.axis_index("c")
    return x, y, c, [(1 - x, y), (x, 1 - y), (1 - x, 1 - y)]


def _rcopy(src, dst, ssem, rsem, dev):
    return pltpu.make_async_remote_copy(src_ref=src, dst_ref=dst, send_sem=ssem, recv_sem=rsem, device_id=dev,
                                        device_id_type=MESH)


SIBLING, Y_CHIP, X_CHIP, DIAGONAL_CHIP = 1, 2, 4, 6
OTHER_CHIPS = (Y_CHIP, X_CHIP, DIAGONAL_CHIP)
EVERYONE = tuple(range(1, N_DEV))
BARRIER_OF = {(SIBLING,): 0, (SIBLING, Y_CHIP, X_CHIP): 1, OTHER_CHIPS: 2, (SIBLING,) + OTHER_CHIPS: 3, EVERYONE: 4}


def _peer(rel):
    x, y, c, _ = _where()
    return x ^ ((rel >> 2) & 1), y ^ ((rel >> 1) & 1), c ^ (rel & 1)


class _Task:
    def __init__(self, ins, outs, alias, n_sem, start, finish, middle=None, peers=()):
        self.ins, self.outs, self.alias, self.n_sem, self.start, self.finish = ins, outs, alias, n_sem, start, finish
        self.middle = middle if middle is not None else (lambda *args: None)
        self.peers = peers


def _peers_of(comm):
    return tuple(sorted({p for t in comm for p in t.peers}))


def _enter(comm):
    peers = _peers_of(comm)
    barrier = pltpu.get_barrier_semaphore()
    for rel in peers:
        pl.semaphore_signal(barrier, inc=1, device_id=_peer(rel), device_id_type=MESH)
    pl.semaphore_wait(barrier, len(peers))


ROWS16 = 16


def _t_gather(placed, part=(0, 1), relayed_first=False):
    R = placed.shape[0] // N_CHIPS
    q = R // 4
    lo, hi = (round(f * (q // ROWS16)) * ROWS16 for f in part)

    def quarter(chip_index, core, k):
        return pl.ds(pl.multiple_of(chip_index * R + core * 2 * q + k * q + lo, ROWS16), hi - lo)

    def places():
        x, y, c, _ = _where()
        return c, 2 * x + y, 2 * (1 - x) + y, 2 * x + (1 - y), 2 * (1 - x) + (1 - y), (1 - x, y, c), (x, 1 - y, c), (x, y, 1 - c)

    def copy(buf, k, chip_index, core, quart, ss, rs, b, dev):
        window = buf.at[quarter(chip_index, core, quart)]
        return _rcopy(window, window, ss.at[b + k], rs.at[b + k], dev)

    def first_hop(cout, ss, rs, b, which):
        c, me, _, _, _, x_nbr, y_nbr, _ = places()
        for k, (quart, dev) in enumerate(((0, x_nbr), (1, y_nbr), (1, x_nbr), (0, y_nbr))):
            if k in which:
                copy(cout[0], k, me, c, quart, ss, rs, b, dev).start()

    def start(cin, cout, ss, rs, b):
        first_hop(cout, ss, rs, b, (0, 1) if relayed_first else (0, 1, 2, 3))

    def middle(cin, cout, ss, rs, b):
        c, _, xc, yc, _, x_nbr, y_nbr, sib = places()
        for k, chip_index, quart, dev in ((0, xc, 0, y_nbr), (1, yc, 1, x_nbr)):
            copy(cout[0], k, chip_index, c, quart, ss, rs, b, dev).wait_recv()
            copy(cout[0], 4 + k, chip_index, c, quart, ss, rs, b, dev).start()
            copy(cout[0], 6 + k, chip_index, c, quart, ss, rs, b, sib).start()
        if relayed_first:
            first_hop(cout, ss, rs, b, (2, 3))

    later = ((2, 1, 1), (3, 2, 0), (4, 3, 0), (5, 3, 1))

    def finish(cin, cout, ss, rs, b):
        c, me, xc, yc, dc, _, _, sib = places()
        chip_of = {1: xc, 2: yc, 3: dc}
        for k, whose, quart in later:
            copy(cout[0], k, chip_of[whose], c, quart, ss, rs, b, sib).wait_recv()
            copy(cout[0], 6 + k, chip_of[whose], c, quart, ss, rs, b, sib).start()
        for k, whose, quart in ((0, 1, 0), (1, 2, 1)) + later:
            copy(cout[0], 6 + k, chip_of[whose], 1 - c, quart, ss, rs, b, sib).wait_recv()
        for k in range(12):
            copy(cout[0], k, me, c, 0, ss, rs, b, sib).wait_send()

    return _Task([placed], [SDS(placed.shape, placed.dtype)], [(0, 0)], 12, start, finish, middle, peers=(SIBLING, Y_CHIP, X_CHIP))


def _t_small_weights(buf):
    def start(cin, cout, ss, rs, b):
        x, y, c, chips = _where()
        mine = cout[0].at[2 * x + y]
        for r, (px, py) in enumerate(chips):
            _rcopy(mine, mine, ss.at[b + r], rs.at[b + r], (px, py, c)).start()

    def finish(cin, cout, ss, rs, b):
        x, y, c, chips = _where()
        for r, (px, py) in enumerate(chips):
            got = cout[0].at[2 * px + py]
            _rcopy(got, got, ss.at[b + r], rs.at[b + r], (px, py, c)).wait_recv()
        for r, (px, py) in enumerate(chips):
            mine = cout[0].at[2 * x + y]
            _rcopy(mine, mine, ss.at[b + r], rs.at[b + r], (px, py, c)).wait_send()

    return _Task([buf], [SDS(buf.shape, buf.dtype)], [(0, 0)], 3, start, finish, peers=OTHER_CHIPS)


def _t_sibling(gbf):
    def start(cin, cout, ss, rs, b):
        x, y, c, _ = _where()
        for jj in range(N_CHIPS):
            _rcopy(cin[0].at[2 * jj + (1 - c)], cout[0].at[jj], ss.at[b + jj], rs.at[b + jj], (x, y, 1 - c)).start()

    def finish(cin, cout, ss, rs, b):
        x, y, c, _ = _where()
        for jj in range(N_CHIPS):
            got = cout[0].at[jj]
            _rcopy(got, got, ss.at[b + jj], rs.at[b + jj], (x, y, 1 - c)).wait_recv()
        for jj in range(N_CHIPS):
            got = cout[0].at[jj]
            _rcopy(got, got, ss.at[b + jj], rs.at[b + jj], (x, y, 1 - c)).wait_send()

    return _Task([gbf], [SDS((N_CHIPS,) + gbf.shape[1:], BF)], [], N_CHIPS, start, finish, peers=(SIBLING,))


def _t_chips(pbf):
    def start(cin, cout, ss, rs, b):
        x, y, c, chips = _where()
        for r, (px, py) in enumerate(chips):
            _rcopy(cin[0].at[2 * px + py], cout[0].at[r], ss.at[b + r], rs.at[b + r], (px, py, c)).start()

    def finish(cin, cout, ss, rs, b):
        x, y, c, chips = _where()
        for r, (px, py) in enumerate(chips):
            got = cout[0].at[r]
            _rcopy(got, got, ss.at[b + r], rs.at[b + r], (px, py, c)).wait_recv()
        for r, (px, py) in enumerate(chips):
            got = cout[0].at[r]
            _rcopy(got, got, ss.at[b + r], rs.at[b + r], (px, py, c)).wait_send()

    return _Task([pbf], [SDS((3,) + pbf.shape[1:], BF)], [], 3, start, finish, peers=OTHER_CHIPS)


def _t_swap(fin):
    def start(cin, cout, ss, rs, b):
        x, y, c, _ = _where()
        mine = cout[0].at[c]
        _rcopy(mine, mine, ss.at[b], rs.at[b], (x, y, 1 - c)).start()

    def finish(cin, cout, ss, rs, b):
        x, y, c, _ = _where()
        got = cout[0].at[1 - c]
        _rcopy(got, got, ss.at[b], rs.at[b], (x, y, 1 - c)).wait_recv()
        _rcopy(got, got, ss.at[b], rs.at[b], (x, y, 1 - c)).wait_send()

    return _Task([fin], [SDS(fin.shape, fin.dtype)], [(0, 0)], 1, start, finish, peers=(SIBLING,))


def _t_allgather(buf):
    def peers():
        x, y, c, _ = _where()
        out = []
        for rel in range(1, N_DEV):
            px, py, pc = x ^ ((rel >> 2) & 1), y ^ ((rel >> 1) & 1), c ^ (rel & 1)
            out.append((rel - 1, 4 * px + 2 * py + pc, (px, py, pc)))
        return 4 * x + 2 * y + c, out

    def start(cin, cout, ss, rs, b):
        me, ps = peers()
        mine = cout[0].at[me]
        for k, _, dev in ps:
            _rcopy(mine, mine, ss.at[b + k], rs.at[b + k], dev).start()

    def finish(cin, cout, ss, rs, b):
        me, ps = peers()
        for k, pidx, dev in ps:
            got = cout[0].at[pidx]
            _rcopy(got, got, ss.at[b + k], rs.at[b + k], dev).wait_recv()
        for k, _, dev in ps:
            mine = cout[0].at[me]
            _rcopy(mine, mine, ss.at[b + k], rs.at[b + k], dev).wait_send()

    return _Task([buf], [SDS(buf.shape, buf.dtype)], [(0, 0)], N_DEV - 1, start, finish, peers=EVERYONE)


def _run_tasks(comm, which, cin, cout, ss, rs):
    i0 = o0 = s0 = 0
    for t in comm:
        getattr(t, which)(cin[i0:i0 + len(t.ins)], cout[o0:o0 + len(t.outs)], ss, rs, s0)
        i0, o0, s0 = i0 + len(t.ins), o0 + len(t.outs), s0 + t.n_sem


def _from_hbm(*arrays):
    return [pltpu.with_memory_space_constraint(a, pltpu.HBM) for a in arrays]


def _in_hbm(shapes):
    return [pltpu.HBM(s.shape, s.dtype) for s in shapes]


def _comm_layout(comm, n_in, n_out):
    c_in = [a for t in comm for a in t.ins]
    c_out = [s for t in comm for s in t.outs]
    aliases, i0, o0 = {}, 0, 0
    for t in comm:
        for i, o in t.alias:
            aliases[n_in + i0 + i] = n_out + o0 + o
        i0, o0 = i0 + len(t.ins), o0 + len(t.outs)
    return c_in, c_out, aliases, sum(t.n_sem for t in comm)


def _call(body, operands, *, name, grid, in_specs, out_specs, out_shape, scratch_shapes=(), sem=None, vmem_mib=None, comm=(),
          free=(), prefetch=()):
    operands = [o if s.memory_space == pltpu.SMEM or k in free else pltpu.with_memory_space_constraint(o, pltpu.HBM)
                for k, (o, s) in enumerate(zip(operands, in_specs))]
    n_pre, n_in, n_out, n_scr = len(prefetch), len(in_specs), len(out_specs), len(scratch_shapes)
    c_in, c_out, aliases, n_sem = _comm_layout(comm, n_pre + n_in, n_out)
    sems = [pltpu.SemaphoreType.DMA((n_sem,)), pltpu.SemaphoreType.DMA((n_sem,))] if comm else []

    def wrapped(*refs):
        pre, refs = refs[:n_pre], refs[n_pre:]
        ins, cin = refs[:n_in], refs[n_in:n_in + len(c_in)]
        rest = refs[n_in + len(c_in):]
        outs, cout = rest[:n_out], rest[n_out:n_out + len(c_out)]
        rest = rest[n_out + len(c_out):]
        scr, csem = rest[:n_scr], rest[n_scr:]
        if not comm:
            return body(*pre, *ins, *outs, *scr)
        step = functools.reduce(lambda acc, k: acc * grid[k] + pl.program_id(k), range(len(grid)), 0)
        n_steps = math.prod(grid)

        @pl.when(step == 0)
        def _():
            _enter(comm)
            _run_tasks(comm, "start", cin, cout, *csem)

        pl.when(step == n_steps // 2)(lambda: _run_tasks(comm, "middle", cin, cout, *csem))
        body(*pre, *ins, *outs, *scr)
        pl.when(step == n_steps - 1)(lambda: _run_tasks(comm, "finish", cin, cout, *csem))

    grid_spec = pltpu.PrefetchScalarGridSpec(
        num_scalar_prefetch=n_pre, grid=grid, in_specs=list(in_specs) + [ANY] * len(c_in),
        out_specs=list(out_specs) + [ANY] * len(c_out), scratch_shapes=list(scratch_shapes) + sems)
    return _pcall(
        wrapped, name=name, grid_spec=grid_spec, out_shape=_in_hbm(list(out_shape) + c_out), input_output_aliases=aliases,
        compiler_params=_params(("arbitrary",) * len(grid) if comm else sem, vmem_mib,
                                BARRIER_OF[_peers_of(comm)] if comm else None),
    )(*prefetch, *operands, *_from_hbm(*c_in))


def _comm_call(name, comm):
    c_in, c_out, aliases, n_sem = _comm_layout(comm, 0, 0)

    def body(*refs):
        cin, cout, (ss, rs) = refs[:len(c_in)], refs[len(c_in):len(c_in) + len(c_out)], refs[len(c_in) + len(c_out):]
        _enter(comm)
        for phase in ("start", "middle", "finish"):
            _run_tasks(comm, phase, cin, cout, ss, rs)

    return _pcall(
        body, name=name, in_specs=[ANY] * len(c_in), out_specs=[ANY] * len(c_out), out_shape=_in_hbm(c_out),
        scratch_shapes=[pltpu.SemaphoreType.DMA((n_sem,)), pltpu.SemaphoreType.DMA((n_sem,))],
        input_output_aliases=aliases, compiler_params=_params(collective_id=BARRIER_OF[_peers_of(comm)]),
    )(*_from_hbm(*c_in))


def _inproj(x, g1, w_int, comm=()):
    tm = TM

    def body(x_ref, g_ref, w_ref, proj_ref, u_ref):
        xf = x_ref[...]
        r = lax.rsqrt(jnp.mean(xf * xf, axis=-1, keepdims=True) + EPS)
        u = (xf * r * g_ref[...]).astype(BF)
        u_ref[...] = u
        proj_ref[...] = _dot(u, w_ref[...], 1, 1)

    return _call(
        body, (x, g1, w_int), name="inproj", grid=(T // tm,),
        in_specs=[pl.BlockSpec((tm, D), lambda i: (i, 0)), pl.BlockSpec((1, D), lambda i: (0, 0)),
                  _resident((INW, D))],
        out_specs=[pl.BlockSpec((tm, INW), lambda i: (i, 0)), pl.BlockSpec((tm, D), lambda i: (i, 0))],
        out_shape=[SDS((T, INW), F32), SDS((T, D), BF)], sem=("parallel",), vmem_mib=40, comm=comm, free=(0, 1))


def _outproj(y, w_out, x, g2):
    tm = TM

    def body(y_ref, w_ref, x_ref, g_ref, h1_ref, u2_ref):
        h1 = x_ref[...] + _dot(y_ref[...], w_ref[...], 1, 0)
        h1_ref[...] = h1
        r = lax.rsqrt(jnp.mean(h1 * h1, axis=-1, keepdims=True) + EPS)
        u2_ref[...] = (h1 * r * g_ref[...]).astype(BF)

    return _call(
        body, (y, w_out, x, g2), name="outproj", grid=(T // tm,),
        in_specs=[pl.BlockSpec((tm, D), lambda i: (i, 0)), _resident((D, D)),
                  pl.BlockSpec((tm, D), lambda i: (i, 0)), pl.BlockSpec((1, D), lambda i: (0, 0))],
        out_specs=[pl.BlockSpec((tm, D), lambda i: (i, 0)), pl.BlockSpec((tm, D), lambda i: (i, 0))],
        out_shape=[SDS((T, D), F32), SDS((T, D), BF)], sem=("parallel",), vmem_mib=32, free=(2, 3))


def _ffn_up(u2, w_upt, comm=()):
    tm, tn = T, 512

    def body(u_ref, w_ref, o_ref):
        o_ref[...] = _dot(u_ref[...], w_ref[...], 1, 1).astype(BF)

    return _call(
        body, (u2, w_upt), name="ffn_up", grid=(T // tm, 2 * DFF // tn),
        in_specs=[pl.BlockSpec((tm, D), lambda i, j: (i, 0)), pl.BlockSpec((tn, D), lambda i, j: (j, 0))],
        out_specs=[pl.BlockSpec((tm, tn), lambda i, j: (i, j))], out_shape=[SDS((T, 2 * DFF), BF)],
        sem=("parallel", "parallel"), vmem_mib=32, comm=comm, free=(1,))


def _ffn_down(a, w_down, h1, tgt):
    tm = TM

    def body(a_ref, w_ref, h1_ref, t_ref, dh_ref, dhb_ref, l_ref):
        @pl.when(pl.program_id(0) == 0)
        def _():
            l_ref[...] = jnp.zeros_like(l_ref)

        h2 = h1_ref[...] + _dot(a_ref[...], w_ref[...], 1, 0)
        e = h2 - t_ref[...]
        dh = e * (1.0 / D)
        dh_ref[...] = dh
        dhb_ref[...] = dh.astype(BF)
        e2 = jnp.sum((e * e).reshape(tm // 8, 8, D), axis=0)
        acc = e2[:, 0:128]
        for k in range(1, D // 128):
            acc = acc + e2[:, k * 128:(k + 1) * 128]
        l_ref[...] += acc

    return _call(
        body, (a, w_down, h1, tgt), name="ffn_down", grid=(T // tm,),
        in_specs=[pl.BlockSpec((tm, DFF), lambda i: (i, 0)), _resident((DFF, D)),
                  pl.BlockSpec((tm, D), lambda i: (i, 0)), pl.BlockSpec((tm, D), lambda i: (i, 0))],
        out_specs=[pl.BlockSpec((tm, D), lambda i: (i, 0)), pl.BlockSpec((tm, D), lambda i: (i, 0)),
                   pl.BlockSpec((8, 128), lambda i: (0, 0))],
        out_shape=[SDS((T, D), F32), SDS((T, D), BF), SDS((8, 128), F32)], sem=("arbitrary",), vmem_mib=40, free=(2, 3))


def _bucket_table():
    q = np.arange(BLK, dtype=np.int32)[:, None]
    j = np.arange(2 * BLK, dtype=np.int32)[None, :]
    n = np.maximum(q + BLK - j, 0)
    nf = np.maximum(n, 1).astype(np.float32)
    max_exact = NBUCKET // 2
    large = max_exact + (np.log(nf / np.float32(max_exact)) / np.float32(math.log(BLK / max_exact))
                         * np.float32(NBUCKET - max_exact)).astype(np.int32)
    large = np.minimum(large, NBUCKET - 1)
    return np.where(n < max_exact, n, large).astype(np.int32)


def _two_bf16(x):
    hi = x.astype(BF)
    return hi, (x - hi.astype(F32)).astype(BF)


def _head_sums(x, seg):
    hi, lo = _two_bf16(x)
    s = seg[0:x.shape[1], :]
    return _dot(hi, s, 1, 0) + _dot(lo, s, 1, 0)


def _head_spread(v, seg, width):
    hi, lo = _two_bf16(v)
    s = seg[0:width, :]
    return _dot(hi, s, 1, 1) + _dot(lo, s, 1, 1)


def _head_norm(x, g_t, seg, by_head=False):
    if by_head:
        heads = [x[:, h * HD:(h + 1) * HD] for h in range(x.shape[1] // HD)]
        r = jnp.concatenate([jnp.broadcast_to(lax.rsqrt(jnp.mean(v * v, axis=-1, keepdims=True) + EPS), v.shape)
                             for v in heads], axis=1)
    else:
        r = lax.rsqrt(_head_sums(x * x, seg) * (1.0 / HD) + EPS)
        r = _head_spread(r, seg, x.shape[1])
    return x * r * g_t, r


def _head_norm_bwd(dy, x, r, g_t, seg):
    dg_t = jnp.sum(dy * (x * r), axis=0, keepdims=True)
    dgx = dy * g_t
    mean = _head_spread(_head_sums(x * dgx, seg) * (1.0 / HD), seg, x.shape[1])
    return r * dgx - x * (r * r * r) * mean, dg_t


def _fold_heads(v):
    out = v[:, 0:HD]
    for h in range(1, v.shape[1] // HD):
        out = out + v[:, h * HD:(h + 1) * HD]
    return out


def _mix_forward(P, zc8, zh8, pkv, first, cw, qg_t, kg_t, gco, gao, seg, sink_ref, bias_ref, by_head=False):
    gate_b = P[:, 0:CW]
    gate_c = P[:, CW:2 * CW]
    hc = P[:, 2 * CW:3 * CW]
    z = gate_c * hc
    keep = jnp.where(first, 0.0, 1.0)
    zp = zc8 * zh8 * keep
    p1 = zp[7:8, :]
    p2 = zp[6:7, :]
    row = lax.broadcasted_iota(jnp.int32, (BLK, 1), 0)
    z1 = jnp.where(row == 0, p1, pltpu.roll(z, 1, 0))
    z2 = jnp.where(row == 0, p2, jnp.where(row == 1, p1, pltpu.roll(z, 2, 0)))
    cz = cw[0:1, :] * z2 + cw[1:2, :] * z1 + cw[2:3, :] * z
    y_conv = gate_b * cz

    scale = HD ** -0.5
    qi = lax.broadcasted_iota(jnp.int32, (BLK, 2 * BLK), 0)
    kj = lax.broadcasted_iota(jnp.int32, (BLK, 2 * BLK), 1)
    dd = qi + BLK - kj
    first_key = jnp.where(first, BLK, 0)
    valid = (dd >= 0) & (dd < BLK) & (kj >= first_key)

    q0 = 3 * CW
    k0 = q0 + AW
    v0 = k0 + NKV * HD
    q_raw = P[:, q0:k0]
    qn, rq = _head_norm(q_raw, qg_t, seg, by_head)
    qs = (qn * scale).astype(BF)
    k_raw = jnp.concatenate([pkv[:, 0:NKV * HD], P[:, k0:v0]], axis=0)
    kn, rk = _head_norm(k_raw, kg_t, seg, by_head)
    knb = kn.astype(BF)
    heads = []
    for h in range(NH):
        kv = h // GQ
        kb = knb[:, kv * HD:(kv + 1) * HD]
        vb = jnp.concatenate([pkv[:, NKV * HD + kv * HD:NKV * HD + (kv + 1) * HD],
                              P[:, v0 + kv * HD:v0 + (kv + 1) * HD]], axis=0).astype(BF)
        Q = qs[:, h * HD:(h + 1) * HD]
        S = _dot(Q, kb, 1, 1) + bias_ref[h * BLK:(h + 1) * BLK, :]
        S = jnp.where(valid, S, NEG_INF)
        sink = sink_ref[0, h]
        m = jnp.maximum(jnp.max(S, axis=-1, keepdims=True), sink)
        p = jnp.exp(S - m)
        es = jnp.exp(sink - m)
        denom = jnp.sum(p, axis=-1, keepdims=True) + es
        probs = p / denom
        O = _dot(probs.astype(BF), vb, 1, 0)
        heads.append(dict(kb=kb, vb=vb, Q=Q, probs=probs, psink=es / denom, O=O))
    y_attn = jnp.concatenate([hd["O"] for hd in heads], axis=1)

    rc = lax.rsqrt(jnp.mean(y_conv * y_conv, axis=-1, keepdims=True) + EPS)
    ra = lax.rsqrt(jnp.mean(y_attn * y_attn, axis=-1, keepdims=True) + EPS)
    y = jnp.concatenate([y_conv * rc * gco, y_attn * ra * gao], axis=1)
    return dict(gate_b=gate_b, gate_c=gate_c, hc=hc, z=z, z1=z1, z2=z2, cz=cz, y_conv=y_conv, y_attn=y_attn,
                rc=rc, ra=ra, heads=heads, y=y, row=row, scale=scale, q_raw=q_raw, rq=rq, k_raw=k_raw, rk=rk)


BPS = 2
TILE = BPS * BLK
KV0 = 3 * CW + AW


def _mix_in_specs(tile_of):
    return [
        pl.BlockSpec(memory_space=pltpu.SMEM),
        pl.BlockSpec((TILE, INW), lambda s: (tile_of(s), 0)),
        pl.BlockSpec((8, CW), lambda s: (jnp.maximum(tile_of(s) * (TILE // 8) - 1, 0), 1)),
        pl.BlockSpec((8, CW), lambda s: (jnp.maximum(tile_of(s) * (TILE // 8) - 1, 0), 2)),
        pl.BlockSpec((BLK, 2 * NKV * HD), lambda s: (jnp.maximum(tile_of(s) * BPS - 1, 0), KV0 // (2 * NKV * HD))),
    ]


def _block_inputs(tile, b, zc_ref, zh_ref, pkv_ref, first_tile):
    P = tile[b * BLK:(b + 1) * BLK, :]
    if b == 0:
        return P, zc_ref[...], zh_ref[...], pkv_ref[...], first_tile
    lo = b * BLK
    return P, tile[lo - 8:lo, CW:2 * CW], tile[lo - 8:lo, 2 * CW:3 * CW], tile[lo - BLK:lo, KV0:KV0 + 2 * NKV * HD], False


def _mix_param_specs():
    return [
        pl.BlockSpec((8, CW), lambda s: (0, 0)),
        pl.BlockSpec((1, AW), lambda s: (0, 0)),
        pl.BlockSpec((1, NKV * HD), lambda s: (0, 0)),
        pl.BlockSpec((1, CW), lambda s: (0, 0)),
        pl.BlockSpec((1, AW), lambda s: (0, 0)),
        pl.BlockSpec((AW, 128), lambda s: (0, 0)),
        pl.BlockSpec((NH * BLK, 2 * BLK), lambda s: (0, 0)),
    ]


def _mix_params(cw8, qg, kg, gco, gao, bias):
    seg = np.zeros((AW, 128), np.float32)
    seg[np.arange(AW), np.arange(AW) // HD] = 1.0
    return (cw8, jnp.tile(qg, (1, NH)), jnp.tile(kg, (1, NKV)), gco, gao, jnp.asarray(seg, BF), bias)


def _mix_fwd(proj, sinks, cw8, qg, kg, gco, gao, bias, comm=()):
    def body(sink_ref, p_ref, zc_ref, zh_ref, pkv_ref, cw_ref, qg_ref, kg_ref, gco_ref, gao_ref, seg_ref, bias_ref, y_ref):
        tile = p_ref[...]
        for b in range(BPS):
            f = _mix_forward(*_block_inputs(tile, b, zc_ref, zh_ref, pkv_ref, pl.program_id(0) == 0), cw_ref[...],
                             qg_ref[...], kg_ref[...], gco_ref[...], gao_ref[...], seg_ref[...], sink_ref, bias_ref, by_head=True)
            y_ref[b * BLK:(b + 1) * BLK, :] = f["y"].astype(BF)

    return _call(
        body, (sinks, proj, proj, proj, proj, *_mix_params(cw8, qg, kg, gco, gao, bias)), name="mix_fwd", grid=(T // TILE,),
        in_specs=_mix_in_specs(lambda s: s) + _mix_param_specs(),
        out_specs=[pl.BlockSpec((TILE, D), lambda s: (s, 0))], out_shape=[SDS((T, D), BF)],
        sem=("parallel",), vmem_mib=40, comm=comm, free=tuple(range(5, 12)))


def _mix_bwd(proj, dy, sinks, cw8, qg, kg, gco, gao, bias, comm=()):
    n_steps = T // TILE

    def tile_of(s):
        return n_steps - 1 - s

    def body(sink_ref, p_ref, zc_ref, zh_ref, pkv_ref, dy_ref, cw_ref, qg_ref, kg_ref, gco_ref, gao_ref, seg_ref, bias_ref,
             dproj_ref, dcw_ref, dqg_ref, dkg_ref, dgco_ref, dgao_ref, dsink_ref, dbias_ref,
             ndcz_ref, dkc_ref, dvc_ref):
        s = pl.program_id(0)

        @pl.when(s == 0)
        def _():
            for r in (dcw_ref, dqg_ref, dkg_ref, dgco_ref, dgao_ref, dsink_ref, dbias_ref, ndcz_ref, dkc_ref, dvc_ref):
                r[...] = jnp.zeros_like(r)

        params = (cw_ref[...], qg_ref[...], kg_ref[...], gco_ref[...], gao_ref[...], seg_ref[...])
        tile = p_ref[...]
        carry = (ndcz_ref[...], dkc_ref[...], dvc_ref[...])
        total = None
        for b in reversed(range(BPS)):
            f = _mix_forward(*_block_inputs(tile, b, zc_ref, zh_ref, pkv_ref, s == n_steps - 1), *params, sink_ref, bias_ref)
            pieces, sums, carry = one_block(f, dy_ref[b * BLK:(b + 1) * BLK, :], params, carry)
            for lo, piece in pieces:
                dproj_ref[b * BLK:(b + 1) * BLK, lo:lo + piece.shape[1]] = piece
            total = sums if total is None else [t + v for t, v in zip(total, sums)]
        ndcz_ref[...], dkc_ref[...], dvc_ref[...] = carry
        dcw, dqg_t, dkg_t, dgco, dgao, dsink, *ds = total
        dcw_ref[0:3, :] += dcw
        dqg_ref[...] += _fold_heads(dqg_t)
        dkg_ref[...] += _fold_heads(dkg_t)
        dgco_ref[...] += dgco
        dgao_ref[...] += dgao
        dsink_ref[...] += dsink
        for h in range(NH):
            dbias_ref[h * BLK:(h + 1) * BLK, :] += ds[h]

    def one_block(f, dy, params, carry):
        cw, qg_v, kg_v, gco_v, gao_v, seg = params
        nxt, dk_carry, dv_carry = carry
        dyc, dgco = _rms_bwd(dy[:, 0:CW], f["y_conv"], f["rc"], gco_v)
        dya, dgao = _rms_bwd(dy[:, CW:CW + AW], f["y_attn"], f["ra"], gao_v)

        row = f["row"]
        dgate_b = dyc * f["cz"]
        dcz = dyc * f["gate_b"]
        dcw = jnp.concatenate([jnp.sum(dcz * f[k], axis=0, keepdims=True) for k in ("z2", "z1", "z")], axis=0)
        n0 = nxt[0:1, :]
        n1 = nxt[1:2, :]
        d1 = jnp.where(row == BLK - 1, n0, pltpu.roll(dcz, BLK - 1, 0))
        d2 = jnp.where(row == BLK - 1, n1, jnp.where(row == BLK - 2, n0, pltpu.roll(dcz, BLK - 2, 0)))
        dz = cw[2:3, :] * dcz + cw[1:2, :] * d1 + cw[0:1, :] * d2
        pieces = [(0, dgate_b.astype(BF)), (CW, (dz * f["hc"]).astype(BF)), (2 * CW, (dz * f["gate_c"]).astype(BF))]

        scale = f["scale"]
        lane = lax.broadcasted_iota(jnp.int32, (1, 128), 1)
        dsink = jnp.zeros((1, 128), F32)
        dq_cols, dk_cols, dv_cols, dk_prev, dv_prev, ds = [], [], [], [], [], []
        for kv in range(NKV):
            dKb = dVb = 0.0
            for h in range(kv * GQ, (kv + 1) * GQ):
                hd = f["heads"][h]
                dO = dya[:, h * HD:(h + 1) * HD]
                delta = jnp.sum(dO * hd["O"], axis=-1, keepdims=True)
                dOb = dO.astype(BF)
                dP = _dot(dOb, hd["vb"], 1, 1)
                dS = hd["probs"] * (dP - delta)
                tot = jnp.sum(hd["psink"] * delta, axis=0, keepdims=True)
                dsink = dsink - jnp.where(lane == h, tot, 0.0)
                ds.append(dS)
                dSb = dS.astype(BF)
                dq_cols.append(_dot(dSb, hd["kb"], 1, 0))
                dKb = dKb + _dot(dSb, hd["Q"], 0, 0)
                dVb = dVb + _dot(hd["probs"].astype(BF), dOb, 0, 0)
            dk_cols.append(dKb[BLK:, :] + dk_carry[:, kv * HD:(kv + 1) * HD])
            dv_cols.append(dVb[BLK:, :] + dv_carry[:, kv * HD:(kv + 1) * HD])
            dk_prev.append(dKb[:BLK, :])
            dv_prev.append(dVb[:BLK, :])
        dq_raw, dqg_t = _head_norm_bwd(jnp.concatenate(dq_cols, axis=1) * scale, f["q_raw"], f["rq"], qg_v, seg)
        dk_raw, dkg_t = _head_norm_bwd(jnp.concatenate(dk_cols, axis=1), f["k_raw"][BLK:, :], f["rk"][BLK:, :], kg_v, seg)
        pieces.append((3 * CW, jnp.concatenate([dq_raw, dk_raw] + dv_cols, axis=1).astype(BF)))
        owed = (dcz[0:8, :], jnp.concatenate(dk_prev, axis=1), jnp.concatenate(dv_prev, axis=1))
        return pieces, [dcw, dqg_t, dkg_t, dgco, dgao, dsink, *ds], owed

    small = lambda r, c: pl.BlockSpec((r, c), lambda s: (0, 0))
    return _call(
        body, (sinks, proj, proj, proj, proj, dy, *_mix_params(cw8, qg, kg, gco, gao, bias)), name="mix_bwd", grid=(n_steps,),
        in_specs=_mix_in_specs(tile_of) + [pl.BlockSpec((TILE, D), lambda s: (tile_of(s), 0))] + _mix_param_specs(),
        out_specs=[pl.BlockSpec((TILE, INW), lambda s: (tile_of(s), 0)), small(8, CW), small(1, HD), small(1, HD),
                   small(1, CW), small(1, AW), small(1, 128), small(NH * BLK, 2 * BLK)],
        out_shape=[SDS((T, INW), BF), SDS((8, CW), F32), SDS((1, HD), F32), SDS((1, HD), F32), SDS((1, CW), F32),
                   SDS((1, AW), F32), SDS((1, 128), F32), SDS((NH * BLK, 2 * BLK), F32)],
        scratch_shapes=[pltpu.VMEM((8, CW), F32), pltpu.VMEM((BLK, NKV * HD), F32), pltpu.VMEM((BLK, NKV * HD), F32)],
        sem=("arbitrary",), vmem_mib=56, comm=comm, free=(1, 2, 3, 4) + tuple(range(6, 13)))


FT = 256
NFT = DFF // FT
RC = 1024
NCH = T // RC
LEAD = 16


def _rows8(x):
    return jnp.sum(x.reshape(x.shape[0] // 8, 8, x.shape[1]), axis=0)


def _ffn_act_specs():
    return [
        pl.BlockSpec((T, FT), lambda j: (0, j)), pl.BlockSpec((T, FT), lambda j: (0, NFT + j)),
        pl.BlockSpec((8, FT), lambda j: (0, j)), pl.BlockSpec((8, FT), lambda j: (0, NFT + j)),
        pl.BlockSpec((1, FT), lambda j: (0, j)), pl.BlockSpec((1, FT), lambda j: (0, NFT + j)),
    ]


def _conv_rows(win, w, b, n):
    win = win.astype(F32)
    u = win[LEAD:LEAD + n]
    u1 = pltpu.roll(win, 1, 0)[LEAD:LEAD + n]
    u2 = pltpu.roll(win, 2, 0)[LEAD:LEAD + n]
    return u2, u1, u, w[0:1, :] * u2 + w[1:2, :] * u1 + w[2:3, :] * u + b


def _ffn_act(up, fw8, fb, comm=()):
    ring = 3

    def body(up_ref, wg_ref, wv_ref, bg_ref, bv_ref, a_ref, pg_ref, pv_ref, g_ring, v_ring, sem):
        wg, wv, bg, bv = wg_ref[...], wv_ref[...], bg_ref[...], bv_ref[...]
        j = pl.program_id(0)

        def fetch(t):
            slot = lax.rem(t, ring)
            return [pltpu.make_async_copy(up_ref.at[:, pl.ds(pl.multiple_of((half * NFT + t) * FT, FT), FT)],
                                          buf.at[slot], sem.at[half, slot]) for half, buf in ((0, g_ring), (1, v_ring))]

        @pl.when(j == 0)
        def _():
            for t in range(ring - 1):
                for c in fetch(t):
                    c.start()

        @pl.when(j + ring - 1 < NFT)
        def _():
            for c in fetch(j + ring - 1):
                c.start()

        for c in fetch(j):
            c.wait()
        ug_ref, uv_ref = g_ring.at[lax.rem(j, ring)], v_ring.at[lax.rem(j, ring)]

        def chunk(rows, win_g, win_v):
            gp = _conv_rows(win_g, wg, bg, RC)[3]
            vp = _conv_rows(win_v, wv, bv, RC)[3]
            a_ref[rows, :] = (gp * jax.nn.sigmoid(gp) * vp).astype(BF)
            pg_ref[0, rows, :] = gp.astype(BF)
            pv_ref[0, rows, :] = vp.astype(BF)

        zero = jnp.zeros((LEAD, FT), BF)
        chunk(pl.ds(0, RC), jnp.concatenate([zero, ug_ref[0:RC, :]], axis=0), jnp.concatenate([zero, uv_ref[0:RC, :]], axis=0))

        def step(i, carry):
            r0 = pl.multiple_of(i * RC, RC)
            win = pl.ds(r0 - LEAD, RC + LEAD)
            chunk(pl.ds(r0, RC), ug_ref[win, :], uv_ref[win, :])
            return carry

        lax.fori_loop(1, NCH, step, 0)

    tile = pl.BlockSpec((1, T, FT), lambda j: (j, 0, 0))
    return _call(
        body, (up, fw8, fw8, fb, fb), name="ffn_act", grid=(NFT,), in_specs=[ANY] + _ffn_act_specs()[2:],
        out_specs=[pl.BlockSpec((T, FT), lambda j: (0, j)), tile, tile],
        out_shape=[SDS((T, DFF), BF), SDS((NFT, T, FT), BF), SDS((NFT, T, FT), BF)],
        scratch_shapes=[pltpu.VMEM((ring, T, FT), BF), pltpu.VMEM((ring, T, FT), BF), pltpu.SemaphoreType.DMA((2, ring))],
        sem=("arbitrary",), vmem_mib=40, comm=comm, free=(1, 2, 3, 4))


def _ffn_act_bwd(up, pre_g, pre_v, da, fw8, comm=()):
    ext = RC + LEAD

    def body(ug_ref, uv_ref, wg_ref, wv_ref, pg_ref, pv_ref, da_ref, dug_ref, duv_ref, dwg_ref, dwv_ref, dbg_ref, dbv_ref):
        wg, wv = wg_ref[...], wv_ref[...]

        def chunk(u_g, u_v, gp, vp, da_e):
            gp, vp, da_e = gp.astype(F32), vp.astype(F32), da_e.astype(F32)
            sig = jax.nn.sigmoid(gp)
            dvp = da_e * (gp * sig)
            dgp = da_e * vp * (sig * (1.0 + gp * (1.0 - sig)))

            def branch(dp, w, u):
                d0, d1, d2 = dp[0:RC], pltpu.roll(dp, ext - 1, 0)[0:RC], pltpu.roll(dp, ext - 2, 0)[0:RC]
                du = (w[2:3, :] * d0 + w[1:2, :] * d1 + w[0:1, :] * d2).astype(BF)
                u = u.astype(F32)
                return du, [_rows8(d0), _rows8(d2 * u), _rows8(d1 * u), _rows8(d0 * u)]

            dug, sums_g = branch(dgp, wg, u_g)
            duv, sums_v = branch(dvp, wv, u_v)
            return dug, duv, sums_g + sums_v

        def step(i, acc):
            r0 = pl.multiple_of(i * RC, RC)
            rows, more = pl.ds(r0, RC), pl.ds(r0, ext)
            dug, duv, part = chunk(ug_ref[rows, :], uv_ref[rows, :], pg_ref[0, more, :], pv_ref[0, more, :], da_ref[more, :])
            dug_ref[rows, :] = dug
            duv_ref[rows, :] = duv
            return [a + p for a, p in zip(acc, part)]

        acc = lax.fori_loop(0, NCH - 1, step, [jnp.zeros((8, FT), F32)] * 8)
        r0 = T - RC
        zero = jnp.zeros((LEAD, FT), BF)
        tail = lambda rows: jnp.concatenate([rows, zero], axis=0)
        dug, duv, part = chunk(ug_ref[r0:T, :], uv_ref[r0:T, :], tail(pg_ref[0, r0:T, :]), tail(pv_ref[0, r0:T, :]),
                               tail(da_ref[r0:T, :]))
        dug_ref[r0:T, :] = dug
        duv_ref[r0:T, :] = duv
        tot = [jnp.sum(a + p, axis=0, keepdims=True) for a, p in zip(acc, part)]
        for k, (dw_ref, db_ref) in enumerate(((dwg_ref, dbg_ref), (dwv_ref, dbv_ref))):
            db_ref[...] = tot[4 * k]
            dw_ref[...] = jnp.zeros_like(dw_ref)
            for r in range(3):
                dw_ref[r:r + 1, :] = tot[4 * k + 1 + r]

    col = lambda r: pl.BlockSpec((r, FT), lambda j: (0, j))
    return _call(
        body, (up, up, fw8, fw8, pre_g, pre_v, da), name="ffn_act_bwd", grid=(NFT,),
        in_specs=_ffn_act_specs()[0:4] + [pl.BlockSpec((1, T, FT), lambda j: (j, 0, 0))] * 2 + [col(T)],
        out_specs=[col(T), col(T), col(8), col(8), col(1), col(1)],
        out_shape=[SDS((T, DFF), BF), SDS((T, DFF), BF), SDS((8, DFF), F32), SDS((8, DFF), F32),
                   SDS((1, DFF), F32), SDS((1, DFF), F32)],
        sem=("parallel",), vmem_mib=40, comm=comm, free=(0, 1, 2, 3))


def _ffn_down_bwd(dh2b, w_down, comm=()):
    tm = TM

    def body(d_ref, w_ref, o_ref):
        o_ref[...] = _dot(d_ref[...], w_ref[...], 1, 1).astype(BF)

    return _call(
        body, (dh2b, w_down), name="ffn_down_bwd", grid=(T // tm,),
        in_specs=[pl.BlockSpec((tm, D), lambda i: (i, 0)), _resident((DFF, D))],
        out_specs=[pl.BlockSpec((tm, DFF), lambda i: (i, 0))], out_shape=[SDS((T, DFF), BF)],
        sem=("parallel",), vmem_mib=40, comm=comm, free=(0, 1))


def _norm_matmul_bwd(name, a_list, w_t, k_offsets, xin, g, dres, want_bf16, comm=(), slot=None, band=(), pack=()):
    tm = TM
    ks = [a.shape[1] for a in a_list]
    n_a = len(a_list)
    n_pre = 0 if slot is None else 1
    n_in = n_a + 4 + len(band) + len(pack)
    halves = 1
    pieces = [(k, k_offsets[k] + h * (ks[k] // halves), ks[k] // halves) for k in range(n_a) for h in range(halves)]

    def body(*refs):
        refs = refs[n_pre:]
        a_refs = refs[:n_a]
        w_ref, x_ref, g_ref, r_ref = refs[n_a:n_a + 4]
        outs = refs[n_in:len(refs) - 2]
        w_vmem, w_sem = refs[-2:]
        first = pl.program_id(0) == 0
        loads = [pltpu.make_async_copy(w_ref.at[pl.ds(lo, n)], w_vmem.at[pl.ds(lo, n)], w_sem.at[p])
                 for p, (_, lo, n) in enumerate(pieces)]
        dg_out = outs[1 + want_bf16]
        dx_ref, dg_ref = outs[0], (dg_out if slot is None else dg_out.at[0])

        @pl.when(pl.program_id(0) == 0)
        def _():
            dg_ref[...] = jnp.zeros_like(dg_ref)
            if band:
                db_ref, bk_ref, tbl_ref = refs[n_a + 4], refs[n_a + 5], outs[2 + want_bf16]
                bk = bk_ref[...]
                for b in range(NBUCKET):
                    m = bk == b
                    for h in range(NH):
                        v = jnp.where(m, db_ref[h * BLK:(h + 1) * BLK, :], 0.0)
                        tbl_ref[0, h:h + 1, b:b + 1] = jnp.sum(jnp.sum(v, axis=1, keepdims=True), axis=0, keepdims=True)
            if pack:
                pm_ref = outs[2 + want_bf16 + bool(band)]
                pm_ref[...] = jnp.zeros_like(pm_ref)
                _fill_mix(pm_ref, *refs[n_in - len(pack):n_in])

        pl.when(first)(lambda: [c.start() for c in loads] and None)
        du = None
        for p, (k, lo, n) in enumerate(pieces):
            pl.when(first)(loads[p].wait)
            col = lo - k_offsets[k]
            part = _dot(a_refs[k][:, col:col + n], w_vmem[lo:lo + n, :], 1, 0)
            du = part if du is None else du + part
        x = x_ref[...]
        r = lax.rsqrt(jnp.mean(x * x, axis=-1, keepdims=True) + EPS)
        dx, dg = _rms_bwd(du, x, r, g_ref[...])
        dx = r_ref[...] + dx
        dx_ref[...] = dx
        if want_bf16:
            outs[1][...] = dx.astype(BF)
        dg_ref[...] += dg

    tile = lambda c: pl.BlockSpec((tm, c), lambda i, *_: (i, 0))
    if slot is None:
        dg_spec, dg_shape = pl.BlockSpec((1, D), lambda i: (0, 0)), SDS((1, D), F32)
    else:
        dg_spec, dg_shape = pl.BlockSpec((1, 1, D), lambda i, slot_ref: (slot_ref[0], 0, 0)), SDS((N_DEV, 1, D), F32)
    out_specs = [tile(D)] + ([tile(D)] if want_bf16 else []) + [dg_spec]
    out_shape = [SDS((T, D), F32)] + ([SDS((T, D), BF)] if want_bf16 else []) + [dg_shape]
    if band:
        out_specs.append(pl.BlockSpec((1, NH, NBUCKET), lambda i, slot_ref: (slot_ref[0], 0, 0)))
        out_shape.append(SDS((N_DEV, NH, NBUCKET), F32))
    if pack:
        out_specs.append(pl.BlockSpec((1, 8, D), lambda i, slot_ref: (slot_ref[0], 0, 0)))
        out_shape.append(SDS((N_DEV, 8, D), F32))
    return _call(
        body, (*a_list, w_t, xin, g, dres, *band, *pack), name=name, grid=(T // tm,), prefetch=() if slot is None else (slot,),
        in_specs=[tile(k) for k in ks] + [ANY, tile(D), pl.BlockSpec((1, D), lambda i, *_: (0, 0)), tile(D)]
        + [pl.BlockSpec(b.shape, lambda i, *_: (0, 0)) for b in (*band, *pack)],
        out_specs=out_specs, out_shape=out_shape, sem=("arbitrary",), vmem_mib=56, comm=comm,
        scratch_shapes=[pltpu.VMEM(w_t.shape, BF), pltpu.SemaphoreType.DMA((len(pieces),))],
        free=tuple(k for k in range(n_a + 4) if k != n_a))


def _out_bwd(dh1b, w_out, comm=()):
    tm = TM

    def body(d_ref, w_ref, o_ref):
        o_ref[...] = _dot(d_ref[...], w_ref[...], 1, 1)

    return _call(
        body, (dh1b, w_out), name="out_bwd", grid=(T // tm,),
        in_specs=[pl.BlockSpec((tm, D), lambda i: (i, 0)), _resident((D, D))],
        out_specs=[pl.BlockSpec((tm, D), lambda i: (i, 0))], out_shape=[SDS((T, D), F32)],
        sem=("parallel",), vmem_mib=32, comm=comm, free=(0, 1))


def _wgrad(name, a_list, b, old_a, comm=()):
    m_k = a_list[0].shape[1]
    tm = max(t for t in range(128, m_k // 2 + 1, 128) if m_k % t == 0)
    steps = [a.shape[1] // tm for a in a_list]
    starts = [sum(steps[:k]) for k in range(len(a_list))]
    n_a = len(a_list)

    def body(*refs):
        a_refs, b_ref, o_ref = refs[:n_a], refs[n_a], refs[n_a + 1]
        i = pl.program_id(0)
        for k in range(n_a):
            @pl.when((i >= starts[k]) & (i < starts[k] + steps[k]))
            def _(k=k):
                o_ref[...] = _dot(a_refs[k][...], b_ref[...], 0, 0).astype(BF)

    def a_spec(k):
        return pl.BlockSpec((T, tm), lambda i: (0, jnp.clip(i - starts[k], 0, steps[k] - 1)))

    m_total = tm * sum(steps)
    return _call(
        body, (*a_list, b), name=name, grid=(sum(steps),),
        in_specs=[a_spec(k) for k in range(n_a)] + [_resident((T, D))],
        out_specs=[pl.BlockSpec((tm, D), lambda i: (i, 0))], out_shape=[SDS((m_total, D), BF)],
        sem=("parallel",), vmem_mib=40, comm=comm, free=() if old_a is None else tuple(range(n_a)) if old_a else (n_a,))


def _chip_sum(name, gbf, from_sib, core, chip):
    h = gbf.shape[1]
    th = h

    def body(core_ref, chip_ref, g_ref, s_ref, pbf_ref, own_ref):
        p = g_ref[0].astype(F32) + s_ref[0].astype(F32)
        pbf_ref[0] = p.astype(BF)

        @pl.when(pl.program_id(1) == chip_ref[0])
        def _():
            own_ref[...] = p

    grid_spec = pltpu.PrefetchScalarGridSpec(
        num_scalar_prefetch=2, grid=(h // th, N_CHIPS),
        in_specs=[pl.BlockSpec((1, th, D), lambda t, jj, core_ref, chip_ref: (2 * jj + core_ref[0], t, 0)),
                  pl.BlockSpec((1, th, D), lambda t, jj, core_ref, chip_ref: (jj, t, 0))],
        out_specs=[pl.BlockSpec((1, th, D), lambda t, jj, core_ref, chip_ref: (jj, t, 0)),
                   pl.BlockSpec((th, D), lambda t, jj, core_ref, chip_ref: (t, 0))],
    )
    return _pcall(
        body, name=name, grid_spec=grid_spec, out_shape=_in_hbm([SDS((N_CHIPS, h, D), BF), SDS((h, D), F32)]),
        compiler_params=_params(("arbitrary", "arbitrary"), 32),
    )(core, chip, *_from_hbm(gbf, from_sib))


def _final_sum(name, own, from_chips, core, comm=()):
    h = own.shape[0]
    n = 4 if h % (4 * ROWS16) == 0 else 2
    th = h // n

    def body(core_ref, o_ref, r_ref, f_ref):
        f_ref[0] = ((o_ref[...] + r_ref[0].astype(F32)) + r_ref[1].astype(F32)) + r_ref[2].astype(F32)

    return _call(
        body, (own, from_chips), name=name, grid=(n,), prefetch=(core,),
        in_specs=[pl.BlockSpec((th, D), lambda i, core_ref: (i, 0)), pl.BlockSpec((3, th, D), lambda i, core_ref: (0, i, 0))],
        out_specs=[pl.BlockSpec((1, th, D), lambda i, core_ref: (core_ref[0], i, 0))], out_shape=[SDS((2, h, D), F32)],
        sem=("arbitrary",), vmem_mib=40, comm=comm)


def _adam_math(w, g, m, v):
    nm = ADAM_B1 * m + (1.0 - ADAM_B1) * g
    nv = ADAM_B2 * v + (1.0 - ADAM_B2) * (g * g)
    m_hat = nm / (1.0 - ADAM_B1 ** ADAM_STEP)
    v_hat = nv / (1.0 - ADAM_B2 ** ADAM_STEP)
    return -ADAM_LR * (m_hat / (jnp.sqrt(v_hat) + ADAM_EPS) + ADAM_WD * w), nm, nv


def _adamw(name, w, g, m, v, tr, copy_g=False, stage=True, g_transposed=False):
    rows, cols = w.shape

    def body(w_ref, g_ref, m_ref, v_ref, *outs):
        d_ref, nm_ref, nv_ref = outs[-3:]
        for c in [pl.ds(c0, 128) for c0 in range(0, cols, 128)] if g_transposed else [slice(None)]:
            g_val = g_ref[c, :].T if g_transposed else g_ref[...]
            if copy_g:
                outs[0][:, c] = g_val
            d_ref[:, c], nm_ref[:, c], nv_ref[:, c] = _adam_math(w_ref[:, c], g_val, m_ref[:, c], v_ref[:, c])

    spec = pl.BlockSpec((tr, cols), lambda i: (i, 0))
    n_out = 4 if copy_g else 3
    g_spec = pl.BlockSpec((cols, tr), lambda i: (0, i)) if g_transposed else spec
    return _call(body, (w, g, m, v), name=name, grid=(rows // tr,), in_specs=[spec, g_spec, spec, spec], out_specs=[spec] * n_out,
                 out_shape=[SDS((rows, cols), F32)] * n_out, sem=("parallel",), vmem_mib=32,
                 free=(0, 2, 3) if stage else ())


C_SQ = 2 * DFF
P_W = C_SQ + 128
R_G2, R_GO, R_DCW, R_QK = 0, 1, 2, 5
C_GCO, C_GAO, C_DQG, C_DKG, C_SINK = 0, CW, 0, 128, 256


def _pack(name, me, ins, width, fill):
    def body(me_ref, *refs):
        o = refs[-1]
        o[...] = jnp.zeros_like(o)
        fill(o, *refs[:-1])

    return _call(body, ins, name=name, grid=(1,), prefetch=(me,),
                 in_specs=[pl.BlockSpec(a.shape, lambda i, me_ref: (0, 0)) for a in ins],
                 out_specs=[pl.BlockSpec((1, 8, width), lambda i, me_ref: (me_ref[0], 0, 0))],
                 out_shape=[SDS((N_DEV, 8, width), F32)], sem=("arbitrary",))[0]


def _pack_ffn(me, dfwg, dfwv, dfbg, dfbv, sq):
    def fill(o, dfwg_r, dfwv_r, dfbg_r, dfbv_r, sq_r):
        o[0, :, 0:DFF] = dfwg_r[...]
        o[0, :, DFF:2 * DFF] = dfwv_r[...]
        o[0, 3:4, 0:DFF] = dfbg_r[...]
        o[0, 3:4, DFF:2 * DFF] = dfbv_r[...]
        o[0, :, C_SQ:C_SQ + 128] = sq_r[...]

    return _pack("pack_ffn", me, (dfwg, dfwv, dfbg, dfbv, sq), P_W, fill)


def _fill_mix(o, dg2_r, dgco_r, dgao_r, dcw_r, dqg_r, dkg_r, dsink_r):
    o[0, R_G2:R_G2 + 1, :] = dg2_r[...]
    o[0, R_GO:R_GO + 1, C_GCO:C_GCO + CW] = dgco_r[...]
    o[0, R_GO:R_GO + 1, C_GAO:C_GAO + AW] = dgao_r[...]
    o[0, R_DCW:R_DCW + 3, 0:CW] = dcw_r[0:3, :]
    o[0, R_QK:R_QK + 1, C_DQG:C_DQG + HD] = dqg_r[...]
    o[0, R_QK:R_QK + 1, C_DKG:C_DKG + HD] = dkg_r[...]
    o[0, R_QK:R_QK + 1, C_SINK:C_SINK + 128] = dsink_r[...]


N_SMALL = 11


def _small_adam(chip, p_all, pm_all, g1_all, tbl_all, ws, ms, vs):
    fw_cols = 2 * DFF // N_CHIPS
    cw_cols = CW // N_CHIPS

    def body(chip_ref, p_ref, fw_ref, pm_ref, cw_ref, g1_ref, tbl_ref, *refs):
        w_r, m_r, v_r = refs[0:N_SMALL], refs[N_SMALL:2 * N_SMALL], refs[2 * N_SMALL:3 * N_SMALL]
        outs = refs[3 * N_SMALL:]
        g_o, d_o, nm_o, nv_o = (outs[k * N_SMALL:(k + 1) * N_SMALL] for k in range(4))
        loss_o = outs[4 * N_SMALL]

        def total(ref):
            s = ref[0]
            for k in range(1, N_DEV):
                s = s + ref[k]
            return s

        S = total(p_ref)
        fw = total(fw_ref)
        M = total(pm_ref)
        cw = total(cw_ref)

        def step(i, g, at):
            d, nm, nv = _adam_math(w_r[i][at], g, m_r[i][at], v_r[i][at])
            g_o[i][at], d_o[i][at], nm_o[i][at], nv_o[i][at] = g, d, nm, nv

        everything = (slice(None), slice(None))
        step(0, total(g1_ref), everything)
        for r in range(3):
            step(1, cw[R_DCW + r:R_DCW + r + 1, :], (r, slice(None), slice(None)))
        step(2, M[R_QK:R_QK + 1, C_DQG:C_DQG + HD], everything)
        step(3, M[R_QK:R_QK + 1, C_DKG:C_DKG + HD], everything)
        step(4, total(tbl_ref), everything)
        step(5, M[R_QK:R_QK + 1, C_SINK:C_SINK + NH], everything)
        step(6, M[R_GO:R_GO + 1, C_GCO:C_GCO + CW], everything)
        step(7, M[R_GO:R_GO + 1, C_GAO:C_GAO + AW], everything)
        step(8, M[R_G2:R_G2 + 1, :], everything)
        for r in range(3):
            step(9, fw[r:r + 1, :], (r, slice(None), slice(None)))
        step(10, S[3:4, 0:2 * DFF], everything)
        sq = S[:, C_SQ:C_SQ + 128]
        loss_o[...] = jnp.sum(jnp.sum(sq, axis=1, keepdims=True), axis=0, keepdims=True) * (0.5 / D)

    def full(a):
        n = len(a.shape)
        return pl.BlockSpec(a.shape, lambda i, chip_ref: (0,) * n)

    params = [*ws, *ms, *vs]
    out = _call(
        body, (p_all, p_all, pm_all, pm_all, g1_all, tbl_all, *params), name="small_adam", grid=(1,), prefetch=(chip,),
        in_specs=[full(p_all),
                  pl.BlockSpec((N_DEV, 8, fw_cols), lambda i, chip_ref: (0, 0, chip_ref[0])),
                  full(pm_all),
                  pl.BlockSpec((N_DEV, 8, cw_cols), lambda i, chip_ref: (0, 0, chip_ref[0])),
                  full(g1_all), full(tbl_all), *[full(a) for a in params]],
        out_specs=[full(a) for a in ws] * 4 + [pl.BlockSpec((1, 1), lambda i, chip_ref: (0, 0))],
        out_shape=[SDS(a.shape, F32) for a in ws] * 4 + [SDS((1, 1), F32)], sem=("arbitrary",), vmem_mib=32)
    return out[0:N_SMALL], out[N_SMALL:2 * N_SMALL], out[2 * N_SMALL:3 * N_SMALL], out[3 * N_SMALL:4 * N_SMALL], out[4 * N_SMALL]


PLACE_STEPS = 4


def _place_specs(shards):
    rows = [s.shape[0] // PLACE_STEPS for s in shards]
    return ([pl.BlockSpec((r, D), lambda i, chip_ref: (i, 0)) for r in rows],
            [pl.BlockSpec((r, D), lambda i, chip_ref: (chip_ref[0] * PLACE_STEPS + i, 0)) for r in rows],
            [SDS((N_CHIPS * s.shape[0], D), BF) for s in shards])


def _place_first(chip, shard, conv_w, ffn_conv_w):
    def body(chip_ref, a, s0, s1, o, t0, t1):
        o[...] = a[...].astype(BF)

        @pl.when(pl.program_id(0) == 0)
        def _():
            for s, t in ((s0, t0), (s1, t1)):
                t[...] = jnp.zeros_like(t)
                t[0, 0:3, :] = s[...]

    ins, outs, shapes = _place_specs([shard])
    taps = (conv_w, ffn_conv_w)
    return _call(
        body, (shard, conv_w, ffn_conv_w), name="place_first", grid=(PLACE_STEPS,), prefetch=(chip,),
        in_specs=ins + [pl.BlockSpec(s.shape, lambda i, chip_ref: (0, 0)) for s in taps],
        out_specs=outs + [pl.BlockSpec((1, 8, s.shape[1]), lambda i, chip_ref: (chip_ref[0], 0, 0)) for s in taps],
        out_shape=shapes + [SDS((N_CHIPS, 8, s.shape[1]), F32) for s in taps],
        sem=("arbitrary",), vmem_mib=32, free=(1, 2))


def _place_rest(chip, shards, w_up, table, bucket, comm):
    n = len(shards)
    c_up = w_up.shape[1]
    edges = [round(k * (c_up // 128) / PLACE_STEPS) * 128 for k in range(PLACE_STEPS + 1)]

    def body(chip_ref, *refs):
        a, (up_ref, tab_ref, bk_ref), o = refs[:n], refs[n:n + 3], refs[n + 3:2 * n + 3]
        up_o, bias_ref = refs[2 * n + 3:]
        for src, dst in zip(a, o):
            dst[...] = src[...].astype(BF)
        for k in range(PLACE_STEPS):
            @pl.when(pl.program_id(0) == k)
            def _(k=k):
                up_o[edges[k]:edges[k + 1], :] = up_ref[:, edges[k]:edges[k + 1]].T.astype(BF)

        @pl.when(pl.program_id(0) == 0)
        def _():
            bk = bk_ref[...]
            eq = [bk == b for b in range(NBUCKET)]
            for h in range(NH):
                acc = jnp.zeros((BLK, 2 * BLK), F32)
                for b in range(NBUCKET):
                    acc = jnp.where(eq[b], tab_ref[h, b], acc)
                bias_ref[h * BLK:(h + 1) * BLK, :] = acc

    ins, outs, shapes = _place_specs(shards)
    return _call(
        body, (*shards, w_up, table, bucket), name="place_rest", grid=(PLACE_STEPS,), prefetch=(chip,),
        in_specs=ins + [_resident(w_up.shape), pl.BlockSpec(memory_space=pltpu.SMEM),
                        pl.BlockSpec(bucket.shape, lambda i, chip_ref: (0, 0))],
        out_specs=outs + [pl.BlockSpec((c_up, D), lambda i, chip_ref: (chip_ref[0], 0)),
                          pl.BlockSpec((NH * BLK, 2 * BLK), lambda i, chip_ref: (0, 0))],
        out_shape=shapes + [SDS((N_CHIPS * c_up, D), BF), SDS((NH * BLK, 2 * BLK), F32)],
        sem=("arbitrary",), vmem_mib=32, comm=comm, free=(n + 1, n + 2))


def kernel(x, norm_mix_g, w_in, conv_w, q_norm_g, k_norm_g, rel_bias_table, sinks, out_norm_conv_g, out_norm_attn_g, w_out, norm_ffn_g, w_up, ffn_conv_w, ffn_conv_b, w_down, loss_target, m_norm_mix_g, m_w_in, m_conv_w, m_q_norm_g, m_k_norm_g, m_rel_bias_table, m_sinks, m_out_norm_conv_g, m_out_norm_attn_g, m_w_out, m_norm_ffn_g, m_w_up, m_ffn_conv_w, m_ffn_conv_b, m_w_down, v_norm_mix_g, v_w_in, v_conv_w, v_q_norm_g, v_k_norm_g, v_rel_bias_table, v_sinks, v_out_norm_conv_g, v_out_norm_attn_g, v_w_out, v_norm_ffn_g, v_w_up, v_ffn_conv_w, v_ffn_conv_b, v_w_down):
    as_arg = lambda i: jnp.reshape(i, (1,)).astype(jnp.int32)
    chip = as_arg(2 * lax.axis_index("x") + lax.axis_index("y"))
    core = as_arg(lax.axis_index("c"))
    me = 2 * chip + core
    xs, tgt = x[0], loss_target[0]
    qg, kg, gco, gao, g1, g2, fb = q_norm_g, k_norm_g, out_norm_conv_g, out_norm_attn_g, norm_mix_g, norm_ffn_g, ffn_conv_b
    pieces = lambda g: g.reshape(N_DEV, g.shape[0] // N_DEV, D)
    whole = lambda f: f.reshape(2 * f.shape[1], D)

    bucket = jnp.asarray(_bucket_table())
    p_in, p_cw, p_fw = _place_first(chip, w_in[0].T, conv_w[0], ffn_conv_w[0])
    p_out, p_down, p_up, bias, w_int, cw_all, fw_all = _place_rest(
        chip, [w_out[0], w_down[0]], w_up[0], rel_bias_table.T, bucket,
        comm=[_t_gather(p_in, relayed_first=True), _t_small_weights(p_cw), _t_small_weights(p_fw)])
    cw8 = jnp.transpose(cw_all, (1, 0, 2)).reshape(8, CW)
    fw8 = jnp.transpose(fw_all, (1, 0, 2)).reshape(8, 2 * DFF)

    early = 3 / 11
    proj, u1, w_out_f, p_up = _inproj(xs, g1, w_int, comm=[_t_gather(p_out), _t_gather(p_up, (0, early))])
    y, w_upt = _mix_fwd(proj, sinks, cw8, qg, kg, gco, gao, bias, comm=[_t_gather(p_up, (early, 1))])
    h1, u2 = _outproj(y, w_out_f, xs, g2)
    up, w_down_f = _ffn_up(u2, w_upt, comm=[_t_gather(p_down)])
    a, pre_g, pre_v = _ffn_act(up, fw8, fb)
    dh2, dh2b, sq = _ffn_down(a, w_down_f, h1, tgt)

    gdbf, = _wgrad("wgrad_down", [a], dh2b, None)
    da, sib_down = _ffn_down_bwd(dh2b, w_down_f, comm=[_t_sibling(pieces(gdbf))])
    pbf_down, own_down = _chip_sum("chip_sum_w_down", pieces(gdbf), sib_down, core, chip)
    dug, duv, dfwg, dfwv, dfbg, dfbv, chips_down = _ffn_act_bwd(up, pre_g, pre_v, da, fw8, comm=[_t_chips(pbf_down)])
    fin_down, = _final_sum("final_sum_w_down", own_down, chips_down, core)
    gubf, = _wgrad("wgrad_up", [dug, duv], u2, False)
    p_all = _pack_ffn(me, dfwg, dfwv, dfbg, dfbv, sq)
    dh1, dh1b, dg2, sib_up, fin_down, p_all = _norm_matmul_bwd(
        "ffn_up_bwd", [dug, duv], w_upt, [0, DFF], h1, g2, dh2, True,
        comm=[_t_sibling(pieces(gubf)), _t_swap(fin_down), _t_allgather(p_all)])
    pbf_up, own_up = _chip_sum("chip_sum_w_up", pieces(gubf), sib_up, core, chip)
    gobf, = _wgrad("wgrad_out", [y], dh1b, True)
    dy, sib_out = _out_bwd(dh1b, w_out_f, comm=[_t_sibling(pieces(gobf))])
    pbf_out, own_out = _chip_sum("chip_sum_w_out", pieces(gobf), sib_out, core, chip)
    dproj, dcw8, dqg, dkg, dgco, dgao, dsink, dbias, chips_up = _mix_bwd(
        proj, dy, sinks, cw8, qg, kg, gco, gao, bias, comm=[_t_chips(pbf_up)])
    fin_up, = _final_sum("final_sum_w_up", own_up, chips_up, core)
    gibf, chips_out, fin_up = _wgrad("wgrad_in", [dproj], u1, False, comm=[_t_chips(pbf_out), _t_swap(fin_up)])
    fin_out, sib_in = _final_sum("final_sum_w_out", own_out, chips_out, core, comm=[_t_sibling(pieces(gibf))])
    pbf_in, own_in = _chip_sum("chip_sum_w_in", pieces(gibf), sib_in, core, chip)
    dx, g1_all, tbl_all, pm_all, chips_in, fin_out = _norm_matmul_bwd(
        "in_bwd", [dproj], w_int, [0], xs, g1, dh1, False, comm=[_t_chips(pbf_in), _t_swap(fin_out)], slot=me,
        band=(dbias, bucket), pack=(dg2, dgco, dgao, dcw8, dqg, dkg, dsink))
    fin_in, = _final_sum("final_sum_w_in", own_in, chips_in, core)
    g1_all, tbl_all, pm_all, fin_in = _comm_call(
        "gather_last", [_t_allgather(g1_all), _t_allgather(tbl_all), _t_allgather(pm_all), _t_swap(fin_in)])

    g_w_out, g_w_down = whole(fin_out), whole(fin_down)
    g_w_down, d_down, nm_down, nv_down = _adamw("adamw_w_down", w_down[0], g_w_down, m_w_down[0], v_w_down[0], 352, True)
    g_w_up, d_up, nm_up, nv_up = _adamw(
        "adamw_w_up", w_up[0], whole(fin_up), m_w_up[0], v_w_up[0], 256, True, stage=False, g_transposed=True)
    g_w_out, d_out, nm_out, nv_out = _adamw("adamw_w_out", w_out[0], g_w_out, m_w_out[0], v_w_out[0], 256, True, stage=False)
    g_w_in, d_in, nm_in, nv_in = [a.T for a in _adamw(
        "adamw_w_in", w_in[0].T, whole(fin_in), m_w_in[0].T, v_w_in[0].T, INW // N_CHIPS // 3, True, stage=False)]
    taps = lambda a: jnp.transpose(a, (1, 0, 2))
    sw = [norm_mix_g, taps(conv_w), q_norm_g, k_norm_g, rel_bias_table.T, sinks, out_norm_conv_g, out_norm_attn_g,
          norm_ffn_g, taps(ffn_conv_w), ffn_conv_b]
    smm = [m_norm_mix_g, taps(m_conv_w), m_q_norm_g, m_k_norm_g, m_rel_bias_table.T, m_sinks, m_out_norm_conv_g,
           m_out_norm_attn_g, m_norm_ffn_g, taps(m_ffn_conv_w), m_ffn_conv_b]
    smv = [v_norm_mix_g, taps(v_conv_w), v_q_norm_g, v_k_norm_g, v_rel_bias_table.T, v_sinks, v_out_norm_conv_g,
           v_out_norm_attn_g, v_norm_ffn_g, taps(v_ffn_conv_w), v_ffn_conv_b]
    *small_out, loss = _small_adam(chip, p_all, pm_all, g1_all, tbl_all, sw, smm, smv)
    sg, sd, snm, snv = [list(r) for r in small_out]
    for r in (sg, sd, snm, snv):
        r[1], r[4], r[9] = taps(r[1]), r[4].T, taps(r[9])

    def order(s, b_in, b_out, b_up, b_down):
        return (s[0], b_in[None], s[1], s[2], s[3], s[4], s[5], s[6], s[7], b_out[None], s[8], b_up[None],
                s[9], s[10], b_down[None])

    return (loss.reshape(()), dx[None],
            *order(sg, g_w_in, g_w_out, g_w_up, g_w_down),
            *order(sd, d_in, d_out, d_up, d_down),
            *order(snm, nm_in, nm_out, nm_up, nm_down),
            *order(snv, nv_in, nv_out, nv_up, nv_down))
```

```python
import functools
import math

import numpy as np

import jax
import jax.numpy as jnp
from jax import lax
from jax.experimental import pallas as pl
from jax.experimental.pallas import tpu as pltpu

F32 = jnp.float32
BF = jnp.bfloat16
SDS = jax.ShapeDtypeStruct

T = 2048
D = 1024
CW = 512
AW = 512
HD = 64
NH = 8
NKV = 2
GQ = 4
INW = 2304
DFF = 2816
BLK = 128
NB = T // BLK
NBUCKET = 32
EPS = 1e-6
NEG_INF = -1e30
N_CHIPS = 4
N_DEV = 8

ADAM_LR = 0.001
ADAM_B1 = 0.9
ADAM_B2 = 0.999
ADAM_EPS = 1e-08
ADAM_WD = 0.01
ADAM_STEP = 10

TM = 512
MIB = 1024 * 1024
MESH = pl.DeviceIdType.MESH
ANY = pl.BlockSpec(memory_space=pl.ANY)

_pcall = pl.pallas_call


def _params(sem=None, vmem_mib=None, collective_id=None):
    kw = {} if collective_id is None else {"collective_id": collective_id}
    if sem is not None:
        kw["dimension_semantics"] = sem
    if vmem_mib is not None:
        kw["vmem_limit_bytes"] = vmem_mib * MIB
    return pltpu.CompilerParams(**kw)


def _resident(shape):
    return pl.BlockSpec(shape, lambda *_: (0,) * len(shape), pipeline_mode=pl.Buffered(1))


def _dot(a, b, ca, cb):
    return lax.dot_general(a, b, (((ca,), (cb,)), ((), ())), preferred_element_type=F32)


def _rms_bwd(dy, x, r, g):
    dg = jnp.sum(dy * (x * r), axis=0, keepdims=True)
    dgx = dy * g
    dx = r * dgx - x * (r * r * r) * jnp.mean(x * dgx, axis=-1, keepdims=True)
    return dx, dg


def _where():
    x, y, c = lax.axis_index("x"), lax.axis_index("y"), lax.axis_index("c")
    return x, y, c, [(1 - x, y), (x, 1 - y), (1 - x, 1 - y)]


def _rcopy(src, dst, ssem, rsem, dev):
    return pltpu.make_async_remote_copy(src_ref=src, dst_ref=dst, send_sem=ssem, recv_sem=rsem, device_id=dev,
                                        device_id_type=MESH)


SIBLING, Y_CHIP, X_CHIP, DIAGONAL_CHIP = 1, 2, 4, 6
OTHER_CHIPS = (Y_CHIP, X_CHIP, DIAGONAL_CHIP)
EVERYONE = tuple(range(1, N_DEV))
BARRIER_OF = {(SIBLING,): 0, (SIBLING, Y_CHIP, X_CHIP): 1, OTHER_CHIPS: 2, (SIBLING,) + OTHER_CHIPS: 3, EVERYONE: 4}


def _peer(rel):
    x, y, c, _ = _where()
    return x ^ ((rel >> 2) & 1), y ^ ((rel >> 1) & 1), c ^ (rel & 1)


class _Task:
    def __init__(self, ins, outs, alias, n_sem, start, finish, middle=None, peers=()):
        self.ins, self.outs, self.alias, self.n_sem, self.start, self.finish = ins, outs, alias, n_sem, start, finish
        self.middle = middle if middle is not None else (lambda *args: None)
        self.peers = peers


def _peers_of(comm):
    return tuple(sorted({p for t in comm for p in t.peers}))


def _enter(comm):
    peers = _peers_of(comm)
    barrier = pltpu.get_barrier_semaphore()
    for rel in peers:
        pl.semaphore_signal(barrier, inc=1, device_id=_peer(rel), device_id_type=MESH)
    pl.semaphore_wait(barrier, len(peers))


ROWS16 = 16


def _t_gather(placed, part=(0, 1), relayed_first=False):
    R = placed.shape[0] // N_CHIPS
    q = R // 4
    lo, hi = (round(f * (q // ROWS16)) * ROWS16 for f in part)

    def quarter(chip_index, core, k):
        return pl.ds(pl.multiple_of(chip_index * R + core * 2 * q + k * q + lo, ROWS16), hi - lo)

    def places():
        x, y, c, _ = _where()
        return c, 2 * x + y, 2 * (1 - x) + y, 2 * x + (1 - y), 2 * (1 - x) + (1 - y), (1 - x, y, c), (x, 1 - y, c), (x, y, 1 - c)

    def copy(buf, k, chip_index, core, quart, ss, rs, b, dev):
        window = buf.at[quarter(chip_index, core, quart)]
        return _rcopy(window, window, ss.at[b + k], rs.at[b + k], dev)

    def first_hop(cout, ss, rs, b, which):
        c, me, _, _, _, x_nbr, y_nbr, _ = places()
        for k, (quart, dev) in enumerate(((0, x_nbr), (1, y_nbr), (1, x_nbr), (0, y_nbr))):
            if k in which:
                copy(cout[0], k, me, c, quart, ss, rs, b, dev).start()

    def start(cin, cout, ss, rs, b):
        first_hop(cout, ss, rs, b, (0, 1) if relayed_first else (0, 1, 2, 3))

    def middle(cin, cout, ss, rs, b):
        c, _, xc, yc, _, x_nbr, y_nbr, sib = places()
        for k, chip_index, quart, dev in ((0, xc, 0, y_nbr), (1, yc, 1, x_nbr)):
            copy(cout[0], k, chip_index, c, quart, ss, rs, b, dev).wait_recv()
            copy(cout[0], 4 + k, chip_index, c, quart, ss, rs, b, dev).start()
            copy(cout[0], 6 + k, chip_index, c, quart, ss, rs, b, sib).start()
        if relayed_first:
            first_hop(cout, ss, rs, b, (2, 3))

    later = ((2, 1, 1), (3, 2, 0), (4, 3, 0), (5, 3, 1))

    def finish(cin, cout, ss, rs, b):
        c, me, xc, yc, dc, _, _, sib = places()
        chip_of = {1: xc, 2: yc, 3: dc}
        for k, whose, quart in later:
            copy(cout[0], k, chip_of[whose], c, quart, ss, rs, b, sib).wait_recv()
            copy(cout[0], 6 + k, chip_of[whose], c, quart, ss, rs, b, sib).start()
        for k, whose, quart in ((0, 1, 0), (1, 2, 1)) + later:
            copy(cout[0], 6 + k, chip_of[whose], 1 - c, quart, ss, rs, b, sib).wait_recv()
        for k in range(12):
            copy(cout[0], k, me, c, 0, ss, rs, b, sib).wait_send()

    return _Task([placed], [SDS(placed.shape, placed.dtype)], [(0, 0)], 12, start, finish, middle, peers=(SIBLING, Y_CHIP, X_CHIP))


def _t_small_weights(buf):
    def start(cin, cout, ss, rs, b):
        x, y, c, chips = _where()
        mine = cout[0].at[2 * x + y]
        for r, (px, py) in enumerate(chips):
            _rcopy(mine, mine, ss.at[b + r], rs.at[b + r], (px, py, c)).start()

    def finish(cin, cout, ss, rs, b):
        x, y, c, chips = _where()
        for r, (px, py) in enumerate(chips):
            got = cout[0].at[2 * px + py]
            _rcopy(got, got, ss.at[b + r], rs.at[b + r], (px, py, c)).wait_recv()
        for r, (px, py) in enumerate(chips):
            mine = cout[0].at[2 * x + y]
            _rcopy(mine, mine, ss.at[b + r], rs.at[b + r], (px, py, c)).wait_send()

    return _Task([buf], [SDS(buf.shape, buf.dtype)], [(0, 0)], 3, start, finish, peers=OTHER_CHIPS)


def _t_sibling(gbf):
    def start(cin, cout, ss, rs, b):
        x, y, c, _ = _where()
        for jj in range(N_CHIPS):
            _rcopy(cin[0].at[2 * jj + (1 - c)], cout[0].at[jj], ss.at[b + jj], rs.at[b + jj], (x, y, 1 - c)).start()

    def finish(cin, cout, ss, rs, b):
        x, y, c, _ = _where()
        for jj in range(N_CHIPS):
            got = cout[0].at[jj]
            _rcopy(got, got, ss.at[b + jj], rs.at[b + jj], (x, y, 1 - c)).wait_recv()
        for jj in range(N_CHIPS):
            got = cout[0].at[jj]
            _rcopy(got, got, ss.at[b + jj], rs.at[b + jj], (x, y, 1 - c)).wait_send()

    return _Task([gbf], [SDS((N_CHIPS,) + gbf.shape[1:], BF)], [], N_CHIPS, start, finish, peers=(SIBLING,))


def _t_chips(pbf):
    def start(cin, cout, ss, rs, b):
        x, y, c, chips = _where()
        for r, (px, py) in enumerate(chips):
            _rcopy(cin[0].at[2 * px + py], cout[0].at[r], ss.at[b + r], rs.at[b + r], (px, py, c)).start()

    def finish(cin, cout, ss, rs, b):
        x, y, c, chips = _where()
        for r, (px, py) in enumerate(chips):
            got = cout[0].at[r]
            _rcopy(got, got, ss.at[b + r], rs.at[b + r], (px, py, c)).wait_recv()
        for r, (px, py) in enumerate(chips):
            got = cout[0].at[r]
            _rcopy(got, got, ss.at[b + r], rs.at[b + r], (px, py, c)).wait_send()

    return _Task([pbf], [SDS((3,) + pbf.shape[1:], BF)], [], 3, start, finish, peers=OTHER_CHIPS)


def _t_swap(fin):
    def start(cin, cout, ss, rs, b):
        x, y, c, _ = _where()
        mine = cout[0].at[c]
        _rcopy(mine, mine, ss.at[b], rs.at[b], (x, y, 1 - c)).start()

    def finish(cin, cout, ss, rs, b):
        x, y, c, _ = _where()
        got = cout[0].at[1 - c]
        _rcopy(got, got, ss.at[b], rs.at[b], (x, y, 1 - c)).wait_recv()
        _rcopy(got, got, ss.at[b], rs.at[b], (x, y, 1 - c)).wait_send()

    return _Task([fin], [SDS(fin.shape, fin.dtype)], [(0, 0)], 1, start, finish, peers=(SIBLING,))


def _t_allgather(buf):
    def peers():
        x, y, c, _ = _where()
        out = []
        for rel in range(1, N_DEV):
            px, py, pc = x ^ ((rel >> 2) & 1), y ^ ((rel >> 1) & 1), c ^ (rel & 1)
            out.append((rel - 1, 4 * px + 2 * py + pc, (px, py, pc)))
        return 4 * x + 2 * y + c, out

    def start(cin, cout, ss, rs, b):
        me, ps = peers()
        mine = cout[0].at[me]
        for k, _, dev in ps:
            _rcopy(mine, mine, ss.at[b + k], rs.at[b + k], dev).start()

    def finish(cin, cout, ss, rs, b):
        me, ps = peers()
        for k, pidx, dev in ps:
            got = cout[0].at[pidx]
            _rcopy(got, got, ss.at[b + k], rs.at[b + k], dev).wait_recv()
        for k, _, dev in ps:
            mine = cout[0].at[me]
            _rcopy(mine, mine, ss.at[b + k], rs.at[b + k], dev).wait_send()

    return _Task([buf], [SDS(buf.shape, buf.dtype)], [(0, 0)], N_DEV - 1, start, finish, peers=EVERYONE)


def _run_tasks(comm, which, cin, cout, ss, rs):
    i0 = o0 = s0 = 0
    for t in comm:
        getattr(t, which)(cin[i0:i0 + len(t.ins)], cout[o0:o0 + len(t.outs)], ss, rs, s0)
        i0, o0, s0 = i0 + len(t.ins), o0 + len(t.outs), s0 + t.n_sem


def _from_hbm(*arrays):
    return [pltpu.with_memory_space_constraint(a, pltpu.HBM) for a in arrays]


def _in_hbm(shapes):
    return [pltpu.HBM(s.shape, s.dtype) for s in shapes]


def _comm_layout(comm, n_in, n_out):
    c_in = [a for t in comm for a in t.ins]
    c_out = [s for t in comm for s in t.outs]
    aliases, i0, o0 = {}, 0, 0
    for t in comm:
        for i, o in t.alias:
            aliases[n_in + i0 + i] = n_out + o0 + o
        i0, o0 = i0 + len(t.ins), o0 + len(t.outs)
    return c_in, c_out, aliases, sum(t.n_sem for t in comm)


def _call(body, operands, *, name, grid, in_specs, out_specs, out_shape, scratch_shapes=(), sem=None, vmem_mib=None, comm=(),
          free=(), prefetch=()):
    operands = [o if s.memory_space == pltpu.SMEM or k in free else pltpu.with_memory_space_constraint(o, pltpu.HBM)
                for k, (o, s) in enumerate(zip(operands, in_specs))]
    n_pre, n_in, n_out, n_scr = len(prefetch), len(in_specs), len(out_specs), len(scratch_shapes)
    c_in, c_out, aliases, n_sem = _comm_layout(comm, n_pre + n_in, n_out)
    sems = [pltpu.SemaphoreType.DMA((n_sem,)), pltpu.SemaphoreType.DMA((n_sem,))] if comm else []

    def wrapped(*refs):
        pre, refs = refs[:n_pre], refs[n_pre:]
        ins, cin = refs[:n_in], refs[n_in:n_in + len(c_in)]
        rest = refs[n_in + len(c_in):]
        outs, cout = rest[:n_out], rest[n_out:n_out + len(c_out)]
        rest = rest[n_out + len(c_out):]
        scr, csem = rest[:n_scr], rest[n_scr:]
        if not comm:
            return body(*pre, *ins, *outs, *scr)
        step = functools.reduce(lambda acc, k: acc * grid[k] + pl.program_id(k), range(len(grid)), 0)
        n_steps = math.prod(grid)

        @pl.when(step == 0)
        def _():
            _enter(comm)
            _run_tasks(comm, "start", cin, cout, *csem)

        pl.when(step == n_steps // 2)(lambda: _run_tasks(comm, "middle", cin, cout, *csem))
        body(*pre, *ins, *outs, *scr)
        pl.when(step == n_steps - 1)(lambda: _run_tasks(comm, "finish", cin, cout, *csem))

    grid_spec = pltpu.PrefetchScalarGridSpec(
        num_scalar_prefetch=n_pre, grid=grid, in_specs=list(in_specs) + [ANY] * len(c_in),
        out_specs=list(out_specs) + [ANY] * len(c_out), scratch_shapes=list(scratch_shapes) + sems)
    return _pcall(
        wrapped, name=name, grid_spec=grid_spec, out_shape=_in_hbm(list(out_shape) + c_out), input_output_aliases=aliases,
        compiler_params=_params(("arbitrary",) * len(grid) if comm else sem, vmem_mib,
                                BARRIER_OF[_peers_of(comm)] if comm else None),
    )(*prefetch, *operands, *_from_hbm(*c_in))


def _comm_call(name, comm):
    c_in, c_out, aliases, n_sem = _comm_layout(comm, 0, 0)

    def body(*refs):
        cin, cout, (ss, rs) = refs[:len(c_in)], refs[len(c_in):len(c_in) + len(c_out)], refs[len(c_in) + len(c_out):]
        _enter(comm)
        for phase in ("start", "middle", "finish"):
            _run_tasks(comm, phase, cin, cout, ss, rs)

    return _pcall(
        body, name=name, in_specs=[ANY] * len(c_in), out_specs=[ANY] * len(c_out), out_shape=_in_hbm(c_out),
        scratch_shapes=[pltpu.SemaphoreType.DMA((n_sem,)), pltpu.SemaphoreType.DMA((n_sem,))],
        input_output_aliases=aliases, compiler_params=_params(collective_id=BARRIER_OF[_peers_of(comm)]),
    )(*_from_hbm(*c_in))


def _inproj(x, g1, w_int, comm=()):
    tm = TM

    def body(x_ref, g_ref, w_ref, proj_ref, u_ref):
        xf = x_ref[...]
        r = lax.rsqrt(jnp.mean(xf * xf, axis=-1, keepdims=True) + EPS)
        u = (xf * r * g_ref[...]).astype(BF)
        u_ref[...] = u
        proj_ref[...] = _dot(u, w_ref[...], 1, 1)

    return _call(
        body, (x, g1, w_int), name="inproj", grid=(T // tm,),
        in_specs=[pl.BlockSpec((tm, D), lambda i: (i, 0)), pl.BlockSpec((1, D), lambda i: (0, 0)),
                  _resident((INW, D))],
        out_specs=[pl.BlockSpec((tm, INW), lambda i: (i, 0)), pl.BlockSpec((tm, D), lambda i: (i, 0))],
        out_shape=[SDS((T, INW), F32), SDS((T, D), BF)], sem=("parallel",), vmem_mib=40, comm=comm, free=(0, 1))


def _outproj(y, w_out, x, g2):
    tm = TM

    def body(y_ref, w_ref, x_ref, g_ref, h1_ref, u2_ref):
        h1 = x_ref[...] + _dot(y_ref[...], w_ref[...], 1, 0)
        h1_ref[...] = h1
        r = lax.rsqrt(jnp.mean(h1 * h1, axis=-1, keepdims=True) + EPS)
        u2_ref[...] = (h1 * r * g_ref[...]).astype(BF)

    return _call(
        body, (y, w_out, x, g2), name="outproj", grid=(T // tm,),
        in_specs=[pl.BlockSpec((tm, D), lambda i: (i, 0)), _resident((D, D)),
                  pl.BlockSpec((tm, D), lambda i: (i, 0)), pl.BlockSpec((1, D), lambda i: (0, 0))],
        out_specs=[pl.BlockSpec((tm, D), lambda i: (i, 0)), pl.BlockSpec((tm, D), lambda i: (i, 0))],
        out_shape=[SDS((T, D), F32), SDS((T, D), BF)], sem=("parallel",), vmem_mib=32, free=(2, 3))


def _ffn_up(u2, w_upt, comm=()):
    tm, tn = T, 512

    def body(u_ref, w_ref, o_ref):
        o_ref[...] = _dot(u_ref[...], w_ref[...], 1, 1).astype(BF)

    return _call(
        body, (u2, w_upt), name="ffn_up", grid=(T // tm, 2 * DFF // tn),
        in_specs=[pl.BlockSpec((tm, D), lambda i, j: (i, 0)), pl.BlockSpec((tn, D), lambda i, j: (j, 0))],
        out_specs=[pl.BlockSpec((tm, tn), lambda i, j: (i, j))], out_shape=[SDS((T, 2 * DFF), BF)],
        sem=("parallel", "parallel"), vmem_mib=32, comm=comm, free=(1,))


def _ffn_down(a, w_down, h1, tgt):
    tm = TM

    def body(a_ref, w_ref, h1_ref, t_ref, dh_ref, dhb_ref, l_ref):
        @pl.when(pl.program_id(0) == 0)
        def _():
            l_ref[...] = jnp.zeros_like(l_ref)

        h2 = h1_ref[...] + _dot(a_ref[...], w_ref[...], 1, 0)
        e = h2 - t_ref[...]
        dh = e * (1.0 / D)
        dh_ref[...] = dh
        dhb_ref[...] = dh.astype(BF)
        e2 = jnp.sum((e * e).reshape(tm // 8, 8, D), axis=0)
        acc = e2[:, 0:128]
        for k in range(1, D // 128):
            acc = acc + e2[:, k * 128:(k + 1) * 128]
        l_ref[...] += acc

    return _call(
        body, (a, w_down, h1, tgt), name="ffn_down", grid=(T // tm,),
        in_specs=[pl.BlockSpec((tm, DFF), lambda i: (i, 0)), _resident((DFF, D)),
                  pl.BlockSpec((tm, D), lambda i: (i, 0)), pl.BlockSpec((tm, D), lambda i: (i, 0))],
        out_specs=[pl.BlockSpec((tm, D), lambda i: (i, 0)), pl.BlockSpec((tm, D), lambda i: (i, 0)),
                   pl.BlockSpec((8, 128), lambda i: (0, 0))],
        out_shape=[SDS((T, D), F32), SDS((T, D), BF), SDS((8, 128), F32)], sem=("arbitrary",), vmem_mib=40, free=(2, 3))


def _bucket_table():
    q = np.arange(BLK, dtype=np.int32)[:, None]
    j = np.arange(2 * BLK, dtype=np.int32)[None, :]
    n = np.maximum(q + BLK - j, 0)
    nf = np.maximum(n, 1).astype(np.float32)
    max_exact = NBUCKET // 2
    large = max_exact + (np.log(nf / np.float32(max_exact)) / np.float32(math.log(BLK / max_exact))
                         * np.float32(NBUCKET - max_exact)).astype(np.int32)
    large = np.minimum(large, NBUCKET - 1)
    return np.where(n < max_exact, n, large).astype(np.int32)


def _two_bf16(x):
    hi = x.astype(BF)
    return hi, (x - hi.astype(F32)).astype(BF)


def _head_sums(x, seg):
    hi, lo = _two_bf16(x)
    s = seg[0:x.shape[1], :]
    return _dot(hi, s, 1, 0) + _dot(lo, s, 1, 0)


def _head_spread(v, seg, width):
    hi, lo = _two_bf16(v)
    s = seg[0:width, :]
    return _dot(hi, s, 1, 1) + _dot(lo, s, 1, 1)


def _head_norm(x, g_t, seg, by_head=False):
    if by_head:
        heads = [x[:, h * HD:(h + 1) * HD] for h in range(x.shape[1] // HD)]
        r = jnp.concatenate([jnp.broadcast_to(lax.rsqrt(jnp.mean(v * v, axis=-1, keepdims=True) + EPS), v.shape)
                             for v in heads], axis=1)
    else:
        r = lax.rsqrt(_head_sums(x * x, seg) * (1.0 / HD) + EPS)
        r = _head_spread(r, seg, x.shape[1])
    return x * r * g_t, r


def _head_norm_bwd(dy, x, r, g_t, seg):
    dg_t = jnp.sum(dy * (x * r), axis=0, keepdims=True)
    dgx = dy * g_t
    mean = _head_spread(_head_sums(x * dgx, seg) * (1.0 / HD), seg, x.shape[1])
    return r * dgx - x * (r * r * r) * mean, dg_t


def _fold_heads(v):
    out = v[:, 0:HD]
    for h in range(1, v.shape[1] // HD):
        out = out + v[:, h * HD:(h + 1) * HD]
    return out


def _mix_forward(P, zc8, zh8, pkv, first, cw, qg_t, kg_t, gco, gao, seg, sink_ref, bias_ref, by_head=False):
    gate_b = P[:, 0:CW]
    gate_c = P[:, CW:2 * CW]
    hc = P[:, 2 * CW:3 * CW]
    z = gate_c * hc
    keep = jnp.where(first, 0.0, 1.0)
    zp = zc8 * zh8 * keep
    p1 = zp[7:8, :]
    p2 = zp[6:7, :]
    row = lax.broadcasted_iota(jnp.int32, (BLK, 1), 0)
    z1 = jnp.where(row == 0, p1, pltpu.roll(z, 1, 0))
    z2 = jnp.where(row == 0, p2, jnp.where(row == 1, p1, pltpu.roll(z, 2, 0)))
    cz = cw[0:1, :] * z2 + cw[1:2, :] * z1 + cw[2:3, :] * z
    y_conv = gate_b * cz

    scale = HD ** -0.5
    qi = lax.broadcasted_iota(jnp.int32, (BLK, 2 * BLK), 0)
    kj = lax.broadcasted_iota(jnp.int32, (BLK, 2 * BLK), 1)
    dd = qi + BLK - kj
    first_key = jnp.where(first, BLK, 0)
    valid = (dd >= 0) & (dd < BLK) & (kj >= first_key)

    q0 = 3 * CW
    k0 = q0 + AW
    v0 = k0 + NKV * HD
    q_raw = P[:, q0:k0]
    qn, rq = _head_norm(q_raw, qg_t, seg, by_head)
    qs = (qn * scale).astype(BF)
    k_raw = jnp.concatenate([pkv[:, 0:NKV * HD], P[:, k0:v0]], axis=0)
    kn, rk = _head_norm(k_raw, kg_t, seg, by_head)
    knb = kn.astype(BF)
    heads = []
    for h in range(NH):
        kv = h // GQ
        kb = knb[:, kv * HD:(kv + 1) * HD]
        vb = jnp.concatenate([pkv[:, NKV * HD + kv * HD:NKV * HD + (kv + 1) * HD],
                              P[:, v0 + kv * HD:v0 + (kv + 1) * HD]], axis=0).astype(BF)
        Q = qs[:, h * HD:(h + 1) * HD]
        S = _dot(Q, kb, 1, 1) + bias_ref[h * BLK:(h + 1) * BLK, :]
        S = jnp.where(valid, S, NEG_INF)
        sink = sink_ref[0, h]
        m = jnp.maximum(jnp.max(S, axis=-1, keepdims=True), sink)
        p = jnp.exp(S - m)
        es = jnp.exp(sink - m)
        denom = jnp.sum(p, axis=-1, keepdims=True) + es
        probs = p / denom
        O = _dot(probs.astype(BF), vb, 1, 0)
        heads.append(dict(kb=kb, vb=vb, Q=Q, probs=probs, psink=es / denom, O=O))
    y_attn = jnp.concatenate([hd["O"] for hd in heads], axis=1)

    rc = lax.rsqrt(jnp.mean(y_conv * y_conv, axis=-1, keepdims=True) + EPS)
    ra = lax.rsqrt(jnp.mean(y_attn * y_attn, axis=-1, keepdims=True) + EPS)
    y = jnp.concatenate([y_conv * rc * gco, y_attn * ra * gao], axis=1)
    return dict(gate_b=gate_b, gate_c=gate_c, hc=hc, z=z, z1=z1, z2=z2, cz=cz, y_conv=y_conv, y_attn=y_attn,
                rc=rc, ra=ra, heads=heads, y=y, row=row, scale=scale, q_raw=q_raw, rq=rq, k_raw=k_raw, rk=rk)


BPS = 2
TILE = BPS * BLK
KV0 = 3 * CW + AW


def _mix_in_specs(tile_of):
    return [
        pl.BlockSpec(memory_space=pltpu.SMEM),
        pl.BlockSpec((TILE, INW), lambda s: (tile_of(s), 0)),
        pl.BlockSpec((8, CW), lambda s: (jnp.maximum(tile_of(s) * (TILE // 8) - 1, 0), 1)),
        pl.BlockSpec((8, CW), lambda s: (jnp.maximum(tile_of(s) * (TILE // 8) - 1, 0), 2)),
        pl.BlockSpec((BLK, 2 * NKV * HD), lambda s: (jnp.maximum(tile_of(s) * BPS - 1, 0), KV0 // (2 * NKV * HD))),
    ]


def _block_inputs(tile, b, zc_ref, zh_ref, pkv_ref, first_tile):
    P = tile[b * BLK:(b + 1) * BLK, :]
    if b == 0:
        return P, zc_ref[...], zh_ref[...], pkv_ref[...], first_tile
    lo = b * BLK
    return P, tile[lo - 8:lo, CW:2 * CW], tile[lo - 8:lo, 2 * CW:3 * CW], tile[lo - BLK:lo, KV0:KV0 + 2 * NKV * HD], False


def _mix_param_specs():
    return [
        pl.BlockSpec((8, CW), lambda s: (0, 0)),
        pl.BlockSpec((1, AW), lambda s: (0, 0)),
        pl.BlockSpec((1, NKV * HD), lambda s: (0, 0)),
        pl.BlockSpec((1, CW), lambda s: (0, 0)),
        pl.BlockSpec((1, AW), lambda s: (0, 0)),
        pl.BlockSpec((AW, 128), lambda s: (0, 0)),
        pl.BlockSpec((NH * BLK, 2 * BLK), lambda s: (0, 0)),
    ]


def _mix_params(cw8, qg, kg, gco, gao, bias):
    seg = np.zeros((AW, 128), np.float32)
    seg[np.arange(AW), np.arange(AW) // HD] = 1.0
    return (cw8, jnp.tile(qg, (1, NH)), jnp.tile(kg, (1, NKV)), gco, gao, jnp.asarray(seg, BF), bias)


def _mix_fwd(proj, sinks, cw8, qg, kg, gco, gao, bias, comm=()):
    def body(sink_ref, p_ref, zc_ref, zh_ref, pkv_ref, cw_ref, qg_ref, kg_ref, gco_ref, gao_ref, seg_ref, bias_ref, y_ref):
        tile = p_ref[...]
        for b in range(BPS):
            f = _mix_forward(*_block_inputs(tile, b, zc_ref, zh_ref, pkv_ref, pl.program_id(0) == 0), cw_ref[...],
                             qg_ref[...], kg_ref[...], gco_ref[...], gao_ref[...], seg_ref[...], sink_ref, bias_ref, by_head=True)
            y_ref[b * BLK:(b + 1) * BLK, :] = f["y"].astype(BF)

    return _call(
        body, (sinks, proj, proj, proj, proj, *_mix_params(cw8, qg, kg, gco, gao, bias)), name="mix_fwd", grid=(T // TILE,),
        in_specs=_mix_in_specs(lambda s: s) + _mix_param_specs(),
        out_specs=[pl.BlockSpec((TILE, D), lambda s: (s, 0))], out_shape=[SDS((T, D), BF)],
        sem=("parallel",), vmem_mib=40, comm=comm, free=tuple(range(5, 12)))


def _mix_bwd(proj, dy, sinks, cw8, qg, kg, gco, gao, bias, comm=()):
    n_steps = T // TILE

    def tile_of(s):
        return n_steps - 1 - s

    def body(sink_ref, p_ref, zc_ref, zh_ref, pkv_ref, dy_ref, cw_ref, qg_ref, kg_ref, gco_ref, gao_ref, seg_ref, bias_ref,
             dproj_ref, dcw_ref, dqg_ref, dkg_ref, dgco_ref, dgao_ref, dsink_ref, dbias_ref,
             ndcz_ref, dkc_ref, dvc_ref):
        s = pl.program_id(0)

        @pl.when(s == 0)
        def _():
            for r in (dcw_ref, dqg_ref, dkg_ref, dgco_ref, dgao_ref, dsink_ref, dbias_ref, ndcz_ref, dkc_ref, dvc_ref):
                r[...] = jnp.zeros_like(r)

        params = (cw_ref[...], qg_ref[...], kg_ref[...], gco_ref[...], gao_ref[...], seg_ref[...])
        tile = p_ref[...]
        carry = (ndcz_ref[...], dkc_ref[...], dvc_ref[...])
        total = None
        for b in reversed(range(BPS)):
            f = _mix_forward(*_block_inputs(tile, b, zc_ref, zh_ref, pkv_ref, s == n_steps - 1), *params, sink_ref, bias_ref)
            pieces, sums, carry = one_block(f, dy_ref[b * BLK:(b + 1) * BLK, :], params, carry)
            for lo, piece in pieces:
                dproj_ref[b * BLK:(b + 1) * BLK, lo:lo + piece.shape[1]] = piece
            total = sums if total is None else [t + v for t, v in zip(total, sums)]
        ndcz_ref[...], dkc_ref[...], dvc_ref[...] = carry
        dcw, dqg_t, dkg_t, dgco, dgao, dsink, *ds = total
        dcw_ref[0:3, :] += dcw
        dqg_ref[...] += _fold_heads(dqg_t)
        dkg_ref[...] += _fold_heads(dkg_t)
        dgco_ref[...] += dgco
        dgao_ref[...] += dgao
        dsink_ref[...] += dsink
        for h in range(NH):
            dbias_ref[h * BLK:(h + 1) * BLK, :] += ds[h]

    def one_block(f, dy, params, carry):
        cw, qg_v, kg_v, gco_v, gao_v, seg = params
        nxt, dk_carry, dv_carry = carry
        dyc, dgco = _rms_bwd(dy[:, 0:CW], f["y_conv"], f["rc"], gco_v)
        dya, dgao = _rms_bwd(dy[:, CW:CW + AW], f["y_attn"], f["ra"], gao_v)

        row = f["row"]
        dgate_b = dyc * f["cz"]
        dcz = dyc * f["gate_b"]
        dcw = jnp.concatenate([jnp.sum(dcz * f[k], axis=0, keepdims=True) for k in ("z2", "z1", "z")], axis=0)
        n0 = nxt[0:1, :]
        n1 = nxt[1:2, :]
        d1 = jnp.where(row == BLK - 1, n0, pltpu.roll(dcz, BLK - 1, 0))
        d2 = jnp.where(row == BLK - 1, n1, jnp.where(row == BLK - 2, n0, pltpu.roll(dcz, BLK - 2, 0)))
        dz = cw[2:3, :] * dcz + cw[1:2, :] * d1 + cw[0:1, :] * d2
        pieces = [(0, dgate_b.astype(BF)), (CW, (dz * f["hc"]).astype(BF)), (2 * CW, (dz * f["gate_c"]).astype(BF))]

        scale = f["scale"]
        lane = lax.broadcasted_iota(jnp.int32, (1, 128), 1)
        dsink = jnp.zeros((1, 128), F32)
        dq_cols, dk_cols, dv_cols, dk_prev, dv_prev, ds = [], [], [], [], [], []
        for kv in range(NKV):
            dKb = dVb = 0.0
            for h in range(kv * GQ, (kv + 1) * GQ):
                hd = f["heads"][h]
                dO = dya[:, h * HD:(h + 1) * HD]
                delta = jnp.sum(dO * hd["O"], axis=-1, keepdims=True)
                dOb = dO.astype(BF)
                dP = _dot(dOb, hd["vb"], 1, 1)
                dS = hd["probs"] * (dP - delta)
                tot = jnp.sum(hd["psink"] * delta, axis=0, keepdims=True)
                dsink = dsink - jnp.where(lane == h, tot, 0.0)
                ds.append(dS)
                dSb = dS.astype(BF)
                dq_cols.append(_dot(dSb, hd["kb"], 1, 0))
                dKb = dKb + _dot(dSb, hd["Q"], 0, 0)
                dVb = dVb + _dot(hd["probs"].astype(BF), dOb, 0, 0)
            dk_cols.append(dKb[BLK:, :] + dk_carry[:, kv * HD:(kv + 1) * HD])
            dv_cols.append(dVb[BLK:, :] + dv_carry[:, kv * HD:(kv + 1) * HD])
            dk_prev.append(dKb[:BLK, :])
            dv_prev.append(dVb[:BLK, :])
        dq_raw, dqg_t = _head_norm_bwd(jnp.concatenate(dq_cols, axis=1) * scale, f["q_raw"], f["rq"], qg_v, seg)
        dk_raw, dkg_t = _head_norm_bwd(jnp.concatenate(dk_cols, axis=1), f["k_raw"][BLK:, :], f["rk"][BLK:, :], kg_v, seg)
        pieces.append((3 * CW, jnp.concatenate([dq_raw, dk_raw] + dv_cols, axis=1).astype(BF)))
        owed = (dcz[0:8, :], jnp.concatenate(dk_prev, axis=1), jnp.concatenate(dv_prev, axis=1))
        return pieces, [dcw, dqg_t, dkg_t, dgco, dgao, dsink, *ds], owed

    small = lambda r, c: pl.BlockSpec((r, c), lambda s: (0, 0))
    return _call(
        body, (sinks, proj, proj, proj, proj, dy, *_mix_params(cw8, qg, kg, gco, gao, bias)), name="mix_bwd", grid=(n_steps,),
        in_specs=_mix_in_specs(tile_of) + [pl.BlockSpec((TILE, D), lambda s: (tile_of(s), 0))] + _mix_param_specs(),
        out_specs=[pl.BlockSpec((TILE, INW), lambda s: (tile_of(s), 0)), small(8, CW), small(1, HD), small(1, HD),
                   small(1, CW), small(1, AW), small(1, 128), small(NH * BLK, 2 * BLK)],
        out_shape=[SDS((T, INW), BF), SDS((8, CW), F32), SDS((1, HD), F32), SDS((1, HD), F32), SDS((1, CW), F32),
                   SDS((1, AW), F32), SDS((1, 128), F32), SDS((NH * BLK, 2 * BLK), F32)],
        scratch_shapes=[pltpu.VMEM((8, CW), F32), pltpu.VMEM((BLK, NKV * HD), F32), pltpu.VMEM((BLK, NKV * HD), F32)],
        sem=("arbitrary",), vmem_mib=56, comm=comm, free=(1, 2, 3, 4) + tuple(range(6, 13)))


FT = 256
NFT = DFF // FT
RC = 1024
NCH = T // RC
LEAD = 16


def _rows8(x):
    return jnp.sum(x.reshape(x.shape[0] // 8, 8, x.shape[1]), axis=0)


def _ffn_act_specs():
    return [
        pl.BlockSpec((T, FT), lambda j: (0, j)), pl.BlockSpec((T, FT), lambda j: (0, NFT + j)),
        pl.BlockSpec((8, FT), lambda j: (0, j)), pl.BlockSpec((8, FT), lambda j: (0, NFT + j)),
        pl.BlockSpec((1, FT), lambda j: (0, j)), pl.BlockSpec((1, FT), lambda j: (0, NFT + j)),
    ]


def _conv_rows(win, w, b, n):
    win = win.astype(F32)
    u = win[LEAD:LEAD + n]
    u1 = pltpu.roll(win, 1, 0)[LEAD:LEAD + n]
    u2 = pltpu.roll(win, 2, 0)[LEAD:LEAD + n]
    return u2, u1, u, w[0:1, :] * u2 + w[1:2, :] * u1 + w[2:3, :] * u + b


def _ffn_act(up, fw8, fb, comm=()):
    ring = 3

    def body(up_ref, wg_ref, wv_ref, bg_ref, bv_ref, a_ref, pg_ref, pv_ref, g_ring, v_ring, sem):
        wg, wv, bg, bv = wg_ref[...], wv_ref[...], bg_ref[...], bv_ref[...]
        j = pl.program_id(0)

        def fetch(t):
            slot = lax.rem(t, ring)
            return [pltpu.make_async_copy(up_ref.at[:, pl.ds(pl.multiple_of((half * NFT + t) * FT, FT), FT)],
                                          buf.at[slot], sem.at[half, slot]) for half, buf in ((0, g_ring), (1, v_ring))]

        @pl.when(j == 0)
        def _():
            for t in range(ring - 1):
                for c in fetch(t):
                    c.start()

        @pl.when(j + ring - 1 < NFT)
        def _():
            for c in fetch(j + ring - 1):
                c.start()

        for c in fetch(j):
            c.wait()
        ug_ref, uv_ref = g_ring.at[lax.rem(j, ring)], v_ring.at[lax.rem(j, ring)]

        def chunk(rows, win_g, win_v):
            gp = _conv_rows(win_g, wg, bg, RC)[3]
            vp = _conv_rows(win_v, wv, bv, RC)[3]
            a_ref[rows, :] = (gp * jax.nn.sigmoid(gp) * vp).astype(BF)
            pg_ref[0, rows, :] = gp.astype(BF)
            pv_ref[0, rows, :] = vp.astype(BF)

        zero = jnp.zeros((LEAD, FT), BF)
        chunk(pl.ds(0, RC), jnp.concatenate([zero, ug_ref[0:RC, :]], axis=0), jnp.concatenate([zero, uv_ref[0:RC, :]], axis=0))

        def step(i, carry):
            r0 = pl.multiple_of(i * RC, RC)
            win = pl.ds(r0 - LEAD, RC + LEAD)
            chunk(pl.ds(r0, RC), ug_ref[win, :], uv_ref[win, :])
            return carry

        lax.fori_loop(1, NCH, step, 0)

    tile = pl.BlockSpec((1, T, FT), lambda j: (j, 0, 0))
    return _call(
        body, (up, fw8, fw8, fb, fb), name="ffn_act", grid=(NFT,), in_specs=[ANY] + _ffn_act_specs()[2:],
        out_specs=[pl.BlockSpec((T, FT), lambda j: (0, j)), tile, tile],
        out_shape=[SDS((T, DFF), BF), SDS((NFT, T, FT), BF), SDS((NFT, T, FT), BF)],
        scratch_shapes=[pltpu.VMEM((ring, T, FT), BF), pltpu.VMEM((ring, T, FT), BF), pltpu.SemaphoreType.DMA((2, ring))],
        sem=("arbitrary",), vmem_mib=40, comm=comm, free=(1, 2, 3, 4))


def _ffn_act_bwd(up, pre_g, pre_v, da, fw8, comm=()):
    ext = RC + LEAD

    def body(ug_ref, uv_ref, wg_ref, wv_ref, pg_ref, pv_ref, da_ref, dug_ref, duv_ref, dwg_ref, dwv_ref, dbg_ref, dbv_ref):
        wg, wv = wg_ref[...], wv_ref[...]

        def chunk(u_g, u_v, gp, vp, da_e):
            gp, vp, da_e = gp.astype(F32), vp.astype(F32), da_e.astype(F32)
            sig = jax.nn.sigmoid(gp)
            dvp = da_e * (gp * sig)
            dgp = da_e * vp * (sig * (1.0 + gp * (1.0 - sig)))

            def branch(dp, w, u):
                d0, d1, d2 = dp[0:RC], pltpu.roll(dp, ext - 1, 0)[0:RC], pltpu.roll(dp, ext - 2, 0)[0:RC]
                du = (w[2:3, :] * d0 + w[1:2, :] * d1 + w[0:1, :] * d2).astype(BF)
                u = u.astype(F32)
                return du, [_rows8(d0), _rows8(d2 * u), _rows8(d1 * u), _rows8(d0 * u)]

            dug, sums_g = branch(dgp, wg, u_g)
            duv, sums_v = branch(dvp, wv, u_v)
            return dug, duv, sums_g + sums_v

        def step(i, acc):
            r0 = pl.multiple_of(i * RC, RC)
            rows, more = pl.ds(r0, RC), pl.ds(r0, ext)
            dug, duv, part = chunk(ug_ref[rows, :], uv_ref[rows, :], pg_ref[0, more, :], pv_ref[0, more, :], da_ref[more, :])
            dug_ref[rows, :] = dug
            duv_ref[rows, :] = duv
            return [a + p for a, p in zip(acc, part)]

        acc = lax.fori_loop(0, NCH - 1, step, [jnp.zeros((8, FT), F32)] * 8)
        r0 = T - RC
        zero = jnp.zeros((LEAD, FT), BF)
        tail = lambda rows: jnp.concatenate([rows, zero], axis=0)
        dug, duv, part = chunk(ug_ref[r0:T, :], uv_ref[r0:T, :], tail(pg_ref[0, r0:T, :]), tail(pv_ref[0, r0:T, :]),
                               tail(da_ref[r0:T, :]))
        dug_ref[r0:T, :] = dug
        duv_ref[r0:T, :] = duv
        tot = [jnp.sum(a + p, axis=0, keepdims=True) for a, p in zip(acc, part)]
        for k, (dw_ref, db_ref) in enumerate(((dwg_ref, dbg_ref), (dwv_ref, dbv_ref))):
            db_ref[...] = tot[4 * k]
            dw_ref[...] = jnp.zeros_like(dw_ref)
            for r in range(3):
                dw_ref[r:r + 1, :] = tot[4 * k + 1 + r]

    col = lambda r: pl.BlockSpec((r, FT), lambda j: (0, j))
    return _call(
        body, (up, up, fw8, fw8, pre_g, pre_v, da), name="ffn_act_bwd", grid=(NFT,),
        in_specs=_ffn_act_specs()[0:4] + [pl.BlockSpec((1, T, FT), lambda j: (j, 0, 0))] * 2 + [col(T)],
        out_specs=[col(T), col(T), col(8), col(8), col(1), col(1)],
        out_shape=[SDS((T, DFF), BF), SDS((T, DFF), BF), SDS((8, DFF), F32), SDS((8, DFF), F32),
                   SDS((1, DFF), F32), SDS((1, DFF), F32)],
        sem=("parallel",), vmem_mib=40, comm=comm, free=(0, 1, 2, 3))


def _ffn_down_bwd(dh2b, w_down, comm=()):
    tm = TM

    def body(d_ref, w_ref, o_ref):
        o_ref[...] = _dot(d_ref[...], w_ref[...], 1, 1).astype(BF)

    return _call(
        body, (dh2b, w_down), name="ffn_down_bwd", grid=(T // tm,),
        in_specs=[pl.BlockSpec((tm, D), lambda i: (i, 0)), _resident((DFF, D))],
        out_specs=[pl.BlockSpec((tm, DFF), lambda i: (i, 0))], out_shape=[SDS((T, DFF), BF)],
        sem=("parallel",), vmem_mib=40, comm=comm, free=(0, 1))


def _norm_matmul_bwd(name, a_list, w_t, k_offsets, xin, g, dres, want_bf16, comm=(), slot=None, band=(), pack=(),
                     hand_w=False):
    tm = TM
    ks = [a.shape[1] for a in a_list]
    n_a = len(a_list)
    n_pre = 0 if slot is None else 1
    n_in = n_a + 4 + len(band) + len(pack)

    def body(*refs):
        refs = refs[n_pre:]
        a_refs = refs[:n_a]
        w_ref, x_ref, g_ref, r_ref = refs[n_a:n_a + 4]
        outs = refs[n_in:len(refs) - 2 * hand_w]
        if hand_w:
            w_hbm, w_ref = w_ref, refs[-2]
            load = pltpu.make_async_copy(w_hbm, w_ref, refs[-1])

            @pl.when(pl.program_id(0) == 0)
            def _():
                load.start()
                load.wait()
        dg_out = outs[1 + want_bf16]
        dx_ref, dg_ref = outs[0], (dg_out if slot is None else dg_out.at[0])

        @pl.when(pl.program_id(0) == 0)
        def _():
            dg_ref[...] = jnp.zeros_like(dg_ref)
            if band:
                db_ref, bk_ref, tbl_ref = refs[n_a + 4], refs[n_a + 5], outs[2 + want_bf16]
                bk = bk_ref[...]
                for b in range(NBUCKET):
                    m = bk == b
                    for h in range(NH):
                        v = jnp.where(m, db_ref[h * BLK:(h + 1) * BLK, :], 0.0)
                        tbl_ref[0, h:h + 1, b:b + 1] = jnp.sum(jnp.sum(v, axis=1, keepdims=True), axis=0, keepdims=True)
            if pack:
                pm_ref = outs[2 + want_bf16 + bool(band)]
                pm_ref[...] = jnp.zeros_like(pm_ref)
                _fill_mix(pm_ref, *refs[n_in - len(pack):n_in])

        du = _dot(a_refs[0][...], w_ref[k_offsets[0]:k_offsets[0] + ks[0], :], 1, 0)
        for k in range(1, n_a):
            du = du + _dot(a_refs[k][...], w_ref[k_offsets[k]:k_offsets[k] + ks[k], :], 1, 0)
        x = x_ref[...]
        r = lax.rsqrt(jnp.mean(x * x, axis=-1, keepdims=True) + EPS)
        dx, dg = _rms_bwd(du, x, r, g_ref[...])
        dx = r_ref[...] + dx
        dx_ref[...] = dx
        if want_bf16:
            outs[1][...] = dx.astype(BF)
        dg_ref[...] += dg

    tile = lambda c: pl.BlockSpec((tm, c), lambda i, *_: (i, 0))
    if slot is None:
        dg_spec, dg_shape = pl.BlockSpec((1, D), lambda i: (0, 0)), SDS((1, D), F32)
    else:
        dg_spec, dg_shape = pl.BlockSpec((1, 1, D), lambda i, slot_ref: (slot_ref[0], 0, 0)), SDS((N_DEV, 1, D), F32)
    out_specs = [tile(D)] + ([tile(D)] if want_bf16 else []) + [dg_spec]
    out_shape = [SDS((T, D), F32)] + ([SDS((T, D), BF)] if want_bf16 else []) + [dg_shape]
    if band:
        out_specs.append(pl.BlockSpec((1, NH, NBUCKET), lambda i, slot_ref: (slot_ref[0], 0, 0)))
        out_shape.append(SDS((N_DEV, NH, NBUCKET), F32))
    if pack:
        out_specs.append(pl.BlockSpec((1, 8, D), lambda i, slot_ref: (slot_ref[0], 0, 0)))
        out_shape.append(SDS((N_DEV, 8, D), F32))
    return _call(
        body, (*a_list, w_t, xin, g, dres, *band, *pack), name=name, grid=(T // tm,), prefetch=() if slot is None else (slot,),
        in_specs=[tile(k) for k in ks] + [ANY if hand_w else _resident(w_t.shape), tile(D),
                                           pl.BlockSpec((1, D), lambda i, *_: (0, 0)), tile(D)]
        + [pl.BlockSpec(b.shape, lambda i, *_: (0, 0)) for b in (*band, *pack)],
        out_specs=out_specs, out_shape=out_shape, sem=("arbitrary",), vmem_mib=56, comm=comm,
        scratch_shapes=[pltpu.VMEM(w_t.shape, BF), pltpu.SemaphoreType.DMA(())] if hand_w else [],
        free=tuple(k for k in range(n_a + 4) if not (hand_w and k == n_a)))


def _out_bwd(dh1b, w_out, comm=()):
    tm = TM

    def body(d_ref, w_ref, o_ref):
        o_ref[...] = _dot(d_ref[...], w_ref[...], 1, 1)

    return _call(
        body, (dh1b, w_out), name="out_bwd", grid=(T // tm,),
        in_specs=[pl.BlockSpec((tm, D), lambda i: (i, 0)), _resident((D, D))],
        out_specs=[pl.BlockSpec((tm, D), lambda i: (i, 0))], out_shape=[SDS((T, D), F32)],
        sem=("parallel",), vmem_mib=32, comm=comm, free=(0, 1))


def _wgrad(name, a_list, b, old_a, comm=()):
    m_k = a_list[0].shape[1]
    tm = max(t for t in range(128, m_k // 2 + 1, 128) if m_k % t == 0)
    steps = [a.shape[1] // tm for a in a_list]
    starts = [sum(steps[:k]) for k in range(len(a_list))]
    n_a = len(a_list)

    def body(*refs):
        a_refs, b_ref, o_ref = refs[:n_a], refs[n_a], refs[n_a + 1]
        i = pl.program_id(0)
        for k in range(n_a):
            @pl.when((i >= starts[k]) & (i < starts[k] + steps[k]))
            def _(k=k):
                o_ref[...] = _dot(a_refs[k][...], b_ref[...], 0, 0).astype(BF)

    def a_spec(k):
        return pl.BlockSpec((T, tm), lambda i: (0, jnp.clip(i - starts[k], 0, steps[k] - 1)))

    m_total = tm * sum(steps)
    return _call(
        body, (*a_list, b), name=name, grid=(sum(steps),),
        in_specs=[a_spec(k) for k in range(n_a)] + [_resident((T, D))],
        out_specs=[pl.BlockSpec((tm, D), lambda i: (i, 0))], out_shape=[SDS((m_total, D), BF)],
        sem=("parallel",), vmem_mib=40, comm=comm, free=() if old_a is None else tuple(range(n_a)) if old_a else (n_a,))


def _chip_sum(name, gbf, from_sib, core, chip):
    h = gbf.shape[1]
    th = h

    def body(core_ref, chip_ref, g_ref, s_ref, pbf_ref, own_ref):
        p = g_ref[0].astype(F32) + s_ref[0].astype(F32)
        pbf_ref[0] = p.astype(BF)

        @pl.when(pl.program_id(1) == chip_ref[0])
        def _():
            own_ref[...] = p

    grid_spec = pltpu.PrefetchScalarGridSpec(
        num_scalar_prefetch=2, grid=(h // th, N_CHIPS),
        in_specs=[pl.BlockSpec((1, th, D), lambda t, jj, core_ref, chip_ref: (2 * jj + core_ref[0], t, 0)),
                  pl.BlockSpec((1, th, D), lambda t, jj, core_ref, chip_ref: (jj, t, 0))],
        out_specs=[pl.BlockSpec((1, th, D), lambda t, jj, core_ref, chip_ref: (jj, t, 0)),
                   pl.BlockSpec((th, D), lambda t, jj, core_ref, chip_ref: (t, 0))],
    )
    return _pcall(
        body, name=name, grid_spec=grid_spec, out_shape=_in_hbm([SDS((N_CHIPS, h, D), BF), SDS((h, D), F32)]),
        compiler_params=_params(("arbitrary", "arbitrary"), 32),
    )(core, chip, *_from_hbm(gbf, from_sib))


def _final_sum(name, own, from_chips, core, comm=()):
    h = own.shape[0]
    n = 4 if h % (4 * ROWS16) == 0 else 2
    th = h // n

    def body(core_ref, o_ref, r_ref, f_ref):
        f_ref[0] = ((o_ref[...] + r_ref[0].astype(F32)) + r_ref[1].astype(F32)) + r_ref[2].astype(F32)

    return _call(
        body, (own, from_chips), name=name, grid=(n,), prefetch=(core,),
        in_specs=[pl.BlockSpec((th, D), lambda i, core_ref: (i, 0)), pl.BlockSpec((3, th, D), lambda i, core_ref: (0, i, 0))],
        out_specs=[pl.BlockSpec((1, th, D), lambda i, core_ref: (core_ref[0], i, 0))], out_shape=[SDS((2, h, D), F32)],
        sem=("arbitrary",), vmem_mib=40, comm=comm)


def _adam_math(w, g, m, v):
    nm = ADAM_B1 * m + (1.0 - ADAM_B1) * g
    nv = ADAM_B2 * v + (1.0 - ADAM_B2) * (g * g)
    m_hat = nm / (1.0 - ADAM_B1 ** ADAM_STEP)
    v_hat = nv / (1.0 - ADAM_B2 ** ADAM_STEP)
    return -ADAM_LR * (m_hat / (jnp.sqrt(v_hat) + ADAM_EPS) + ADAM_WD * w), nm, nv


def _adamw(name, w, g, m, v, tr, copy_g=False, stage=True, g_transposed=False):
    rows, cols = w.shape

    def body(w_ref, g_ref, m_ref, v_ref, *outs):
        d_ref, nm_ref, nv_ref = outs[-3:]
        for c in [pl.ds(c0, 128) for c0 in range(0, cols, 128)] if g_transposed else [slice(None)]:
            g_val = g_ref[c, :].T if g_transposed else g_ref[...]
            if copy_g:
                outs[0][:, c] = g_val
            d_ref[:, c], nm_ref[:, c], nv_ref[:, c] = _adam_math(w_ref[:, c], g_val, m_ref[:, c], v_ref[:, c])

    spec = pl.BlockSpec((tr, cols), lambda i: (i, 0))
    n_out = 4 if copy_g else 3
    g_spec = pl.BlockSpec((cols, tr), lambda i: (0, i)) if g_transposed else spec
    return _call(body, (w, g, m, v), name=name, grid=(rows // tr,), in_specs=[spec, g_spec, spec, spec], out_specs=[spec] * n_out,
                 out_shape=[SDS((rows, cols), F32)] * n_out, sem=("parallel",), vmem_mib=32,
                 free=(0, 2, 3) if stage else ())


C_SQ = 2 * DFF
P_W = C_SQ + 128
R_G2, R_GO, R_DCW, R_QK = 0, 1, 2, 5
C_GCO, C_GAO, C_DQG, C_DKG, C_SINK = 0, CW, 0, 128, 256


def _pack(name, me, ins, width, fill):
    def body(me_ref, *refs):
        o = refs[-1]
        o[...] = jnp.zeros_like(o)
        fill(o, *refs[:-1])

    return _call(body, ins, name=name, grid=(1,), prefetch=(me,),
                 in_specs=[pl.BlockSpec(a.shape, lambda i, me_ref: (0, 0)) for a in ins],
                 out_specs=[pl.BlockSpec((1, 8, width), lambda i, me_ref: (me_ref[0], 0, 0))],
                 out_shape=[SDS((N_DEV, 8, width), F32)], sem=("arbitrary",))[0]


def _pack_ffn(me, dfwg, dfwv, dfbg, dfbv, sq):
    def fill(o, dfwg_r, dfwv_r, dfbg_r, dfbv_r, sq_r):
        o[0, :, 0:DFF] = dfwg_r[...]
        o[0, :, DFF:2 * DFF] = dfwv_r[...]
        o[0, 3:4, 0:DFF] = dfbg_r[...]
        o[0, 3:4, DFF:2 * DFF] = dfbv_r[...]
        o[0, :, C_SQ:C_SQ + 128] = sq_r[...]

    return _pack("pack_ffn", me, (dfwg, dfwv, dfbg, dfbv, sq), P_W, fill)


def _fill_mix(o, dg2_r, dgco_r, dgao_r, dcw_r, dqg_r, dkg_r, dsink_r):
    o[0, R_G2:R_G2 + 1, :] = dg2_r[...]
    o[0, R_GO:R_GO + 1, C_GCO:C_GCO + CW] = dgco_r[...]
    o[0, R_GO:R_GO + 1, C_GAO:C_GAO + AW] = dgao_r[...]
    o[0, R_DCW:R_DCW + 3, 0:CW] = dcw_r[0:3, :]
    o[0, R_QK:R_QK + 1, C_DQG:C_DQG + HD] = dqg_r[...]
    o[0, R_QK:R_QK + 1, C_DKG:C_DKG + HD] = dkg_r[...]
    o[0, R_QK:R_QK + 1, C_SINK:C_SINK + 128] = dsink_r[...]


N_SMALL = 11


def _small_adam(chip, p_all, pm_all, g1_all, tbl_all, ws, ms, vs):
    fw_cols = 2 * DFF // N_CHIPS
    cw_cols = CW // N_CHIPS

    def body(chip_ref, p_ref, fw_ref, pm_ref, cw_ref, g1_ref, tbl_ref, *refs):
        w_r, m_r, v_r = refs[0:N_SMALL], refs[N_SMALL:2 * N_SMALL], refs[2 * N_SMALL:3 * N_SMALL]
        outs = refs[3 * N_SMALL:]
        g_o, d_o, nm_o, nv_o = (outs[k * N_SMALL:(k + 1) * N_SMALL] for k in range(4))
        loss_o = outs[4 * N_SMALL]

        def total(ref):
            s = ref[0]
            for k in range(1, N_DEV):
                s = s + ref[k]
            return s

        S = total(p_ref)
        fw = total(fw_ref)
        M = total(pm_ref)
        cw = total(cw_ref)

        def step(i, g, at):
            d, nm, nv = _adam_math(w_r[i][at], g, m_r[i][at], v_r[i][at])
            g_o[i][at], d_o[i][at], nm_o[i][at], nv_o[i][at] = g, d, nm, nv

        everything = (slice(None), slice(None))
        step(0, total(g1_ref), everything)
        for r in range(3):
            step(1, cw[R_DCW + r:R_DCW + r + 1, :], (r, slice(None), slice(None)))
        step(2, M[R_QK:R_QK + 1, C_DQG:C_DQG + HD], everything)
        step(3, M[R_QK:R_QK + 1, C_DKG:C_DKG + HD], everything)
        step(4, total(tbl_ref), everything)
        step(5, M[R_QK:R_QK + 1, C_SINK:C_SINK + NH], everything)
        step(6, M[R_GO:R_GO + 1, C_GCO:C_GCO + CW], everything)
        step(7, M[R_GO:R_GO + 1, C_GAO:C_GAO + AW], everything)
        step(8, M[R_G2:R_G2 + 1, :], everything)
        for r in range(3):
            step(9, fw[r:r + 1, :], (r, slice(None), slice(None)))
        step(10, S[3:4, 0:2 * DFF], everything)
        sq = S[:, C_SQ:C_SQ + 128]
        loss_o[...] = jnp.sum(jnp.sum(sq, axis=1, keepdims=True), axis=0, keepdims=True) * (0.5 / D)

    def full(a):
        n = len(a.shape)
        return pl.BlockSpec(a.shape, lambda i, chip_ref: (0,) * n)

    params = [*ws, *ms, *vs]
    out = _call(
        body, (p_all, p_all, pm_all, pm_all, g1_all, tbl_all, *params), name="small_adam", grid=(1,), prefetch=(chip,),
        in_specs=[full(p_all),
                  pl.BlockSpec((N_DEV, 8, fw_cols), lambda i, chip_ref: (0, 0, chip_ref[0])),
                  full(pm_all),
                  pl.BlockSpec((N_DEV, 8, cw_cols), lambda i, chip_ref: (0, 0, chip_ref[0])),
                  full(g1_all), full(tbl_all), *[full(a) for a in params]],
        out_specs=[full(a) for a in ws] * 4 + [pl.BlockSpec((1, 1), lambda i, chip_ref: (0, 0))],
        out_shape=[SDS(a.shape, F32) for a in ws] * 4 + [SDS((1, 1), F32)], sem=("arbitrary",), vmem_mib=32)
    return out[0:N_SMALL], out[N_SMALL:2 * N_SMALL], out[2 * N_SMALL:3 * N_SMALL], out[3 * N_SMALL:4 * N_SMALL], out[4 * N_SMALL]


PLACE_STEPS = 4


def _place_specs(shards):
    rows = [s.shape[0] // PLACE_STEPS for s in shards]
    return ([pl.BlockSpec((r, D), lambda i, chip_ref: (i, 0)) for r in rows],
            [pl.BlockSpec((r, D), lambda i, chip_ref: (chip_ref[0] * PLACE_STEPS + i, 0)) for r in rows],
            [SDS((N_CHIPS * s.shape[0], D), BF) for s in shards])


def _place_first(chip, shard, conv_w, ffn_conv_w):
    def body(chip_ref, a, s0, s1, o, t0, t1):
        o[...] = a[...].astype(BF)

        @pl.when(pl.program_id(0) == 0)
        def _():
            for s, t in ((s0, t0), (s1, t1)):
                t[...] = jnp.zeros_like(t)
                t[0, 0:3, :] = s[...]

    ins, outs, shapes = _place_specs([shard])
    taps = (conv_w, ffn_conv_w)
    return _call(
        body, (shard, conv_w, ffn_conv_w), name="place_first", grid=(PLACE_STEPS,), prefetch=(chip,),
        in_specs=ins + [pl.BlockSpec(s.shape, lambda i, chip_ref: (0, 0)) for s in taps],
        out_specs=outs + [pl.BlockSpec((1, 8, s.shape[1]), lambda i, chip_ref: (chip_ref[0], 0, 0)) for s in taps],
        out_shape=shapes + [SDS((N_CHIPS, 8, s.shape[1]), F32) for s in taps],
        sem=("arbitrary",), vmem_mib=32, free=(1, 2))


def _place_rest(chip, shards, w_up, table, bucket, comm):
    n = len(shards)
    c_up = w_up.shape[1]
    edges = [round(k * (c_up // 128) / PLACE_STEPS) * 128 for k in range(PLACE_STEPS + 1)]

    def body(chip_ref, *refs):
        a, (up_ref, tab_ref, bk_ref), o = refs[:n], refs[n:n + 3], refs[n + 3:2 * n + 3]
        up_o, bias_ref = refs[2 * n + 3:]
        for src, dst in zip(a, o):
            dst[...] = src[...].astype(BF)
        for k in range(PLACE_STEPS):
            @pl.when(pl.program_id(0) == k)
            def _(k=k):
                up_o[edges[k]:edges[k + 1], :] = up_ref[:, edges[k]:edges[k + 1]].T.astype(BF)

        @pl.when(pl.program_id(0) == 0)
        def _():
            bk = bk_ref[...]
            eq = [bk == b for b in range(NBUCKET)]
            for h in range(NH):
                acc = jnp.zeros((BLK, 2 * BLK), F32)
                for b in range(NBUCKET):
                    acc = jnp.where(eq[b], tab_ref[h, b], acc)
                bias_ref[h * BLK:(h + 1) * BLK, :] = acc

    ins, outs, shapes = _place_specs(shards)
    return _call(
        body, (*shards, w_up, table, bucket), name="place_rest", grid=(PLACE_STEPS,), prefetch=(chip,),
        in_specs=ins + [_resident(w_up.shape), pl.BlockSpec(memory_space=pltpu.SMEM),
                        pl.BlockSpec(bucket.shape, lambda i, chip_ref: (0, 0))],
        out_specs=outs + [pl.BlockSpec((c_up, D), lambda i, chip_ref: (chip_ref[0], 0)),
                          pl.BlockSpec((NH * BLK, 2 * BLK), lambda i, chip_ref: (0, 0))],
        out_shape=shapes + [SDS((N_CHIPS * c_up, D), BF), SDS((NH * BLK, 2 * BLK), F32)],
        sem=("arbitrary",), vmem_mib=32, comm=comm, free=(n + 1, n + 2))


def kernel(x, norm_mix_g, w_in, conv_w, q_norm_g, k_norm_g, rel_bias_table, sinks, out_norm_conv_g, out_norm_attn_g, w_out, norm_ffn_g, w_up, ffn_conv_w, ffn_conv_b, w_down, loss_target, m_norm_mix_g, m_w_in, m_conv_w, m_q_norm_g, m_k_norm_g, m_rel_bias_table, m_sinks, m_out_norm_conv_g, m_out_norm_attn_g, m_w_out, m_norm_ffn_g, m_w_up, m_ffn_conv_w, m_ffn_conv_b, m_w_down, v_norm_mix_g, v_w_in, v_conv_w, v_q_norm_g, v_k_norm_g, v_rel_bias_table, v_sinks, v_out_norm_conv_g, v_out_norm_attn_g, v_w_out, v_norm_ffn_g, v_w_up, v_ffn_conv_w, v_ffn_conv_b, v_w_down):
    as_arg = lambda i: jnp.reshape(i, (1,)).astype(jnp.int32)
    chip = as_arg(2 * lax.axis_index("x") + lax.axis_index("y"))
    core = as_arg(lax.axis_index("c"))
    me = 2 * chip + core
    xs, tgt = x[0], loss_target[0]
    qg, kg, gco, gao, g1, g2, fb = q_norm_g, k_norm_g, out_norm_conv_g, out_norm_attn_g, norm_mix_g, norm_ffn_g, ffn_conv_b
    pieces = lambda g: g.reshape(N_DEV, g.shape[0] // N_DEV, D)
    whole = lambda f: f.reshape(2 * f.shape[1], D)

    bucket = jnp.asarray(_bucket_table())
    p_in, p_cw, p_fw = _place_first(chip, w_in[0].T, conv_w[0], ffn_conv_w[0])
    p_out, p_down, p_up, bias, w_int, cw_all, fw_all = _place_rest(
        chip, [w_out[0], w_down[0]], w_up[0], rel_bias_table.T, bucket,
        comm=[_t_gather(p_in, relayed_first=True), _t_small_weights(p_cw), _t_small_weights(p_fw)])
    cw8 = jnp.transpose(cw_all, (1, 0, 2)).reshape(8, CW)
    fw8 = jnp.transpose(fw_all, (1, 0, 2)).reshape(8, 2 * DFF)

    early = 3 / 11
    proj, u1, w_out_f, p_up = _inproj(xs, g1, w_int, comm=[_t_gather(p_out), _t_gather(p_up, (0, early))])
    y, w_upt = _mix_fwd(proj, sinks, cw8, qg, kg, gco, gao, bias, comm=[_t_gather(p_up, (early, 1))])
    h1, u2 = _outproj(y, w_out_f, xs, g2)
    up, w_down_f = _ffn_up(u2, w_upt, comm=[_t_gather(p_down)])
    a, pre_g, pre_v = _ffn_act(up, fw8, fb)
    dh2, dh2b, sq = _ffn_down(a, w_down_f, h1, tgt)

    gdbf, = _wgrad("wgrad_down", [a], dh2b, None)
    da, sib_down = _ffn_down_bwd(dh2b, w_down_f, comm=[_t_sibling(pieces(gdbf))])
    pbf_down, own_down = _chip_sum("chip_sum_w_down", pieces(gdbf), sib_down, core, chip)
    dug, duv, dfwg, dfwv, dfbg, dfbv, chips_down = _ffn_act_bwd(up, pre_g, pre_v, da, fw8, comm=[_t_chips(pbf_down)])
    fin_down, = _final_sum("final_sum_w_down", own_down, chips_down, core)
    gubf, = _wgrad("wgrad_up", [dug, duv], u2, False)
    p_all = _pack_ffn(me, dfwg, dfwv, dfbg, dfbv, sq)
    dh1, dh1b, dg2, sib_up, fin_down, p_all = _norm_matmul_bwd(
        "ffn_up_bwd", [dug, duv], w_upt, [0, DFF], h1, g2, dh2, True,
        comm=[_t_sibling(pieces(gubf)), _t_swap(fin_down), _t_allgather(p_all)])
    pbf_up, own_up = _chip_sum("chip_sum_w_up", pieces(gubf), sib_up, core, chip)
    gobf, = _wgrad("wgrad_out", [y], dh1b, True)
    dy, sib_out = _out_bwd(dh1b, w_out_f, comm=[_t_sibling(pieces(gobf))])
    pbf_out, own_out = _chip_sum("chip_sum_w_out", pieces(gobf), sib_out, core, chip)
    dproj, dcw8, dqg, dkg, dgco, dgao, dsink, dbias, chips_up = _mix_bwd(
        proj, dy, sinks, cw8, qg, kg, gco, gao, bias, comm=[_t_chips(pbf_up)])
    fin_up, = _final_sum("final_sum_w_up", own_up, chips_up, core)
    gibf, chips_out, fin_up = _wgrad("wgrad_in", [dproj], u1, False, comm=[_t_chips(pbf_out), _t_swap(fin_up)])
    fin_out, sib_in = _final_sum("final_sum_w_out", own_out, chips_out, core, comm=[_t_sibling(pieces(gibf))])
    pbf_in, own_in = _chip_sum("chip_sum_w_in", pieces(gibf), sib_in, core, chip)
    dx, g1_all, tbl_all, pm_all, chips_in, fin_out = _norm_matmul_bwd(
        "in_bwd", [dproj], w_int, [0], xs, g1, dh1, False, comm=[_t_chips(pbf_in), _t_swap(fin_out)], slot=me,
        band=(dbias, bucket), pack=(dg2, dgco, dgao, dcw8, dqg, dkg, dsink), hand_w=True)
    fin_in, = _final_sum("final_sum_w_in", own_in, chips_in, core)
    g1_all, tbl_all, pm_all, fin_in = _comm_call(
        "gather_last", [_t_allgather(g1_all), _t_allgather(tbl_all), _t_allgather(pm_all), _t_swap(fin_in)])

    g_w_out, g_w_down = whole(fin_out), whole(fin_down)
    g_w_down, d_down, nm_down, nv_down = _adamw("adamw_w_down", w_down[0], g_w_down, m_w_down[0], v_w_down[0], 352, True)
    g_w_up, d_up, nm_up, nv_up = _adamw(
        "adamw_w_up", w_up[0], whole(fin_up), m_w_up[0], v_w_up[0], 256, True, stage=False, g_transposed=True)
    g_w_out, d_out, nm_out, nv_out = _adamw("adamw_w_out", w_out[0], g_w_out, m_w_out[0], v_w_out[0], 256, True, stage=False)
    g_w_in, d_in, nm_in, nv_in = [a.T for a in _adamw(
        "adamw_w_in", w_in[0].T, whole(fin_in), m_w_in[0].T, v_w_in[0].T, INW // N_CHIPS // 3, True, stage=False)]
    taps = lambda a: jnp.transpose(a, (1, 0, 2))
    sw = [norm_mix_g, taps(conv_w), q_norm_g, k_norm_g, rel_bias_table.T, sinks, out_norm_conv_g, out_norm_attn_g,
          norm_ffn_g, taps(ffn_conv_w), ffn_conv_b]
    smm = [m_norm_mix_g, taps(m_conv_w), m_q_norm_g, m_k_norm_g, m_rel_bias_table.T, m_sinks, m_out_norm_conv_g,
           m_out_norm_attn_g, m_norm_ffn_g, taps(m_ffn_conv_w), m_ffn_conv_b]
    smv = [v_norm_mix_g, taps(v_conv_w), v_q_norm_g, v_k_norm_g, v_rel_bias_table.T, v_sinks, v_out_norm_conv_g,
           v_out_norm_attn_g, v_norm_ffn_g, taps(v_ffn_conv_w), v_ffn_conv_b]
    *small_out, loss = _small_adam(chip, p_all, pm_all, g1_all, tbl_all, sw, smm, smv)
    sg, sd, snm, snv = [list(r) for r in small_out]
    for r in (sg, sd, snm, snv):
        r[1], r[4], r[9] = taps(r[1]), r[4].T, taps(r[9])

    def order(s, b_in, b_out, b_up, b_down):
        return (s[0], b_in[None], s[1], s[2], s[3], s[4], s[5], s[6], s[7], b_out[None], s[8], b_up[None],
                s[9], s[10], b_down[None])

    return (loss.reshape(()), dx[None],
            *order(sg, g_w_in, g_w_out, g_w_up, g_w_down),
            *order(sd, d_in, d_out, d_up, d_down),
            *order(snm, nm_in, nm_out, nm_up, nm_down),
            *order(snv, nv_in, nv_out, nv_up, nv_down))
```
